```python
import jax, jax.numpy as jnp
from jax import lax
import numpy as np

D_MODEL = 1024
BATCH = 16
SEQ = 2048
DEPTH = 1

MEM_LEN = 256
POOL_WINDOWS = (2, 4, 8, 16)
POOL_GROUPS = 4
POOL_GROUP_DIM = D_MODEL // 8
POOL_WIDTH = POOL_GROUPS * POOL_GROUP_DIM
RET_HEADS = 4
RET_QK_DIM = D_MODEL // 8
RET_V_DIM = 2 * RET_QK_DIM
RET_QK_WIDTH = RET_HEADS * RET_QK_DIM
RET_V_WIDTH = RET_HEADS * RET_V_DIM
RET_CHUNK = 128
ROPE_BASE = 10000.0
XA_HEADS = 4
XA_HEAD_DIM = D_MODEL // 8
XA_WIDTH = XA_HEADS * XA_HEAD_DIM
N_BRANCH = 3
IN_WIDTH = POOL_WIDTH + 2 * RET_QK_WIDTH + 2 * RET_V_WIDTH + XA_WIDTH + N_BRANCH * D_MODEL
FFN_HIDDEN = 2816
CONV_WIDTH = 3
EPS = 1e-6

kernel_name = "hybrid_pool_retention_memxattn_convglu"


def rmsnorm(x, g):
    x32 = x.astype(jnp.float32)
    y = x32 * lax.rsqrt(jnp.mean(x32 * x32, axis=-1, keepdims=True) + EPS)
    return y.astype(x.dtype) * g


def pool_mixer(hp, w_pool, pool_scale):
    B, S, _ = hp.shape
    h32 = hp.astype(jnp.float32)
    cs = jnp.cumsum(h32, axis=1)
    t1 = jnp.arange(1, S + 1, dtype=jnp.float32)[None, :, None]
    outs = []
    for gi, w in enumerate(POOL_WINDOWS):
        c = cs[..., gi * POOL_GROUP_DIM:(gi + 1) * POOL_GROUP_DIM]
        c_shift = jnp.pad(c, ((0, 0), (w, 0), (0, 0)))[:, :S]
        outs.append((c - c_shift) / jnp.minimum(t1, float(w)))
    pooled = jnp.concatenate(outs, axis=-1) - h32
    pooled = pooled.astype(hp.dtype).reshape(B, S, POOL_GROUPS, POOL_GROUP_DIM)
    y = jnp.einsum('bsgc,gcd->bsgd', pooled, w_pool).reshape(B, S, POOL_WIDTH)
    return y * pool_scale


def rotary(x, pos):
    half = x.shape[-1] // 2
    inv = ROPE_BASE ** (-jnp.arange(half, dtype=jnp.float32) / half)
    ang = pos[:, None] * inv[None, :]
    cos = jnp.cos(ang)[None, :, None, :]
    sin = jnp.sin(ang)[None, :, None, :]
    x1, x2 = x[..., :half], x[..., half:]
    return jnp.concatenate([x1 * cos - x2 * sin, x1 * sin + x2 * cos], axis=-1)


def chunkwise_retention(q, k, v):
    B, S, H, dk = q.shape
    dv = v.shape[-1]
    C = RET_CHUNK
    N = S // C
    log_gamma = jnp.log1p(-jnp.exp2(-5.0 - jnp.arange(H, dtype=jnp.float32)))
    lg = log_gamma[:, None, None]
    idx = jnp.arange(C, dtype=jnp.float32)
    rel = idx[:, None] - idx[None, :]
    decay_intra = jnp.where(rel >= 0, jnp.exp(jnp.maximum(rel, 0.0) * lg), 0.0)
    q_decay = jnp.exp((idx + 1.0)[None, :, None] * lg)
    k_decay = jnp.exp((C - 1.0 - idx)[None, :, None] * lg)
    chunk_decay = jnp.exp(C * lg)

    def to_chunks(a):
        d = a.shape[-1]
        return a.reshape(B, N, C, H, d).transpose(1, 0, 3, 2, 4)

    qc, kc, vc = to_chunks(q), to_chunks(k * (dk ** -0.5)), to_chunks(v)

    def step(R, inp):
        qi, ki, vi = inp
        s = jnp.einsum('bhqd,bhkd->bhqk', qi, ki) * decay_intra
        o = (jnp.einsum('bhqk,bhkv->bhqv', s, vi)
             + jnp.einsum('bhqd,bhdv->bhqv', qi * q_decay, R))
        R = chunk_decay * R + jnp.einsum('bhkd,bhkv->bhdv', ki * k_decay, vi)
        return R, o

    R0 = jnp.zeros((B, H, dk, dv), jnp.float32)
    _, o = lax.scan(step, R0, (qc, kc, vc))
    return o.transpose(1, 0, 3, 2, 4).reshape(B, S, H, dv)


def retention_branch(q, k, v, gr, g_ret, b_ret):
    B, S, _ = q.shape
    pos = jnp.arange(S, dtype=jnp.float32)
    q4 = rotary(q.astype(jnp.float32).reshape(B, S, RET_HEADS, RET_QK_DIM), pos)
    k4 = rotary(k.astype(jnp.float32).reshape(B, S, RET_HEADS, RET_QK_DIM), pos)
    v4 = v.astype(jnp.float32).reshape(B, S, RET_HEADS, RET_V_DIM)
    o = chunkwise_retention(q4, k4, v4)
    mu = jnp.mean(o, axis=-1, keepdims=True)
    var = jnp.mean(jnp.square(o - mu), axis=-1, keepdims=True)
    o = ((o - mu) * lax.rsqrt(var + EPS)).reshape(B, S, RET_V_WIDTH).astype(q.dtype)
    o = o * g_ret + b_ret
    return jax.nn.silu(gr) * o


def memory_cross_attention(qx, mem_n, w_mem_kv):
    B, S, _ = qx.shape
    M = mem_n.shape[1]
    q = qx.reshape(B, S, XA_HEADS, XA_HEAD_DIM)
    kv = mem_n @ w_mem_kv
    k, v = jnp.split(kv, 2, axis=-1)
    k = k.reshape(B, M, XA_HEADS, XA_HEAD_DIM)
    v = v.reshape(B, M, XA_HEADS, XA_HEAD_DIM)
    s = jnp.einsum('bshd,bmhd->bhsm', q, k).astype(jnp.float32) * (XA_HEAD_DIM ** -0.5)
    p = jax.nn.softmax(s, axis=-1).astype(v.dtype)
    o = jnp.einsum('bhsm,bmhd->bshd', p, v)
    return o.reshape(B, S, XA_WIDTH)


def conv_glu_ffn(h, w_up, conv_w, conv_b, w_down):
    S = h.shape[1]
    up = h @ w_up
    a, b = jnp.split(up, 2, axis=-1)
    a_pad = jnp.pad(a, ((0, 0), (CONV_WIDTH - 1, 0), (0, 0)))
    a = sum(a_pad[:, j:j + S] * conv_w[j] for j in range(CONV_WIDTH)) + conv_b
    return (jax.nn.gelu(a) * b) @ w_down


def _fwd_setup_inputs(seed: int = 0) -> dict:
    key = jax.random.key(seed)
    ks = jax.random.split(key, 24)
    f32 = jnp.float32

    def nrm(k, shape, fan_in):
        return jax.random.normal(k, shape, f32) * (fan_in ** -0.5)

    def gain(k, shape):
        return 1.0 + 0.02 * jax.random.normal(k, shape, f32)

    L = DEPTH
    return {
        "x": jax.random.normal(ks[0], (BATCH, SEQ, D_MODEL), f32),
        "mem": jax.random.normal(ks[1], (BATCH, MEM_LEN, D_MODEL), f32),
        "g_mix": gain(ks[2], (L, D_MODEL)),
        "w_in": nrm(ks[3], (L, D_MODEL, IN_WIDTH), D_MODEL),
        "w_pool": nrm(ks[4], (L, POOL_GROUPS, POOL_GROUP_DIM, POOL_GROUP_DIM), POOL_GROUP_DIM),
        "pool_scale": 1.0 + 0.1 * jax.random.normal(ks[5], (L, POOL_WIDTH), f32),
        "w_a": nrm(ks[6], (L, POOL_WIDTH, D_MODEL), POOL_WIDTH),
        "g_ret": gain(ks[7], (L, RET_V_WIDTH)),
        "b_ret": 0.01 * jax.random.normal(ks[8], (L, RET_V_WIDTH), f32),
        "w_r": nrm(ks[9], (L, RET_V_WIDTH, D_MODEL), RET_V_WIDTH),
        "g_mem": gain(ks[10], (L, D_MODEL)),
        "w_mem_kv": nrm(ks[11], (L, D_MODEL, 2 * XA_WIDTH), D_MODEL),
        "w_c": nrm(ks[12], (L, XA_WIDTH, D_MODEL), XA_WIDTH),
        "w_out": nrm(ks[13], (L, D_MODEL, D_MODEL), D_MODEL),
        "g_ffn": gain(ks[14], (L, D_MODEL)),
        "w_up": nrm(ks[15], (L, D_MODEL, 2 * FFN_HIDDEN), D_MODEL),
        "conv_w": nrm(ks[16], (L, CONV_WIDTH, FFN_HIDDEN), CONV_WIDTH),
        "conv_b": 0.01 * jax.random.normal(ks[17], (L, FFN_HIDDEN), f32),
        "w_down": nrm(ks[18], (L, FFN_HIDDEN, D_MODEL), FFN_HIDDEN),
        "g_final": gain(ks[19], (D_MODEL,)),
    }


def _fwd_reference(x, mem, g_mix, w_in, w_pool, pool_scale, w_a, g_ret, b_ret, w_r,
              g_mem, w_mem_kv, w_c, w_out, g_ffn, w_up, conv_w, conv_b, w_down, g_final):
    splits = list(np.cumsum([POOL_WIDTH, RET_QK_WIDTH, RET_QK_WIDTH, RET_V_WIDTH,
                             RET_V_WIDTH, XA_WIDTH]))
    for l in range(DEPTH):
        h = rmsnorm(x, g_mix[l])
        proj = h @ w_in[l]
        hp, q, k, v, gr, qx, gl = jnp.split(proj, splits, axis=-1)
        y_pool = pool_mixer(hp, w_pool[l], pool_scale[l]) @ w_a[l]
        y_ret = retention_branch(q, k, v, gr, g_ret[l], b_ret[l]) @ w_r[l]
        mem_n = rmsnorm(mem, g_mem[l])
        y_mem = memory_cross_attention(qx, mem_n, w_mem_kv[l]) @ w_c[l]
        gate_pool, gate_ret, gate_mem = jnp.split(gl, N_BRANCH, axis=-1)
        merged = (jax.nn.sigmoid(gate_pool) * y_pool
                  + jax.nn.sigmoid(gate_ret) * y_ret
                  + jax.nn.sigmoid(gate_mem) * y_mem)
        x = x + merged @ w_out[l]
        x = x + conv_glu_ffn(rmsnorm(x, g_ffn[l]), w_up[l], conv_w[l], conv_b[l], w_down[l])
    return rmsnorm(x, g_final)


import jax as _jax
import jax.numpy as _jnp

TWIN_FORMAT = 'train_step'
FWD_PARAMS = ['x', 'mem', 'g_mix', 'w_in', 'w_pool', 'pool_scale', 'w_a', 'g_ret', 'b_ret', 'w_r', 'g_mem', 'w_mem_kv', 'w_c', 'w_out', 'g_ffn', 'w_up', 'conv_w', 'conv_b', 'w_down', 'g_final']
TWIN_WEIGHTS = ['g_mix', 'w_in', 'w_pool', 'pool_scale', 'w_a', 'g_ret', 'b_ret', 'w_r', 'g_mem', 'w_mem_kv', 'w_c', 'w_out', 'g_ffn', 'w_up', 'conv_w', 'conv_b', 'w_down', 'g_final']
TWIN_DIFF_INPUT = 'x'
TWIN_INPUTS = ['x', 'mem', 'g_mix', 'w_in', 'w_pool', 'pool_scale', 'w_a', 'g_ret', 'b_ret', 'w_r', 'g_mem', 'w_mem_kv', 'w_c', 'w_out', 'g_ffn', 'w_up', 'conv_w', 'conv_b', 'w_down', 'g_final', 'loss_target', 'm_g_mix', 'm_w_in', 'm_w_pool', 'm_pool_scale', 'm_w_a', 'm_g_ret', 'm_b_ret', 'm_w_r', 'm_g_mem', 'm_w_mem_kv', 'm_w_c', 'm_w_out', 'm_g_ffn', 'm_w_up', 'm_conv_w', 'm_conv_b', 'm_w_down', 'm_g_final', 'v_g_mix', 'v_w_in', 'v_w_pool', 'v_pool_scale', 'v_w_a', 'v_g_ret', 'v_b_ret', 'v_w_r', 'v_g_mem', 'v_w_mem_kv', 'v_w_c', 'v_w_out', 'v_g_ffn', 'v_w_up', 'v_conv_w', 'v_conv_b', 'v_w_down', 'v_g_final']
TWIN_OUTPUTS = ['loss', 'grad_x', 'grad_g_mix', 'grad_w_in', 'grad_w_pool', 'grad_pool_scale', 'grad_w_a', 'grad_g_ret', 'grad_b_ret', 'grad_w_r', 'grad_g_mem', 'grad_w_mem_kv', 'grad_w_c', 'grad_w_out', 'grad_g_ffn', 'grad_w_up', 'grad_conv_w', 'grad_conv_b', 'grad_w_down', 'grad_g_final', 'delta_g_mix', 'delta_w_in', 'delta_w_pool', 'delta_pool_scale', 'delta_w_a', 'delta_g_ret', 'delta_b_ret', 'delta_w_r', 'delta_g_mem', 'delta_w_mem_kv', 'delta_w_c', 'delta_w_out', 'delta_g_ffn', 'delta_w_up', 'delta_conv_w', 'delta_conv_b', 'delta_w_down', 'delta_g_final', 'new_m_g_mix', 'new_m_w_in', 'new_m_w_pool', 'new_m_pool_scale', 'new_m_w_a', 'new_m_g_ret', 'new_m_b_ret', 'new_m_w_r', 'new_m_g_mem', 'new_m_w_mem_kv', 'new_m_w_c', 'new_m_w_out', 'new_m_g_ffn', 'new_m_w_up', 'new_m_conv_w', 'new_m_conv_b', 'new_m_w_down', 'new_m_g_final', 'new_v_g_mix', 'new_v_w_in', 'new_v_w_pool', 'new_v_pool_scale', 'new_v_w_a', 'new_v_g_ret', 'new_v_b_ret', 'new_v_w_r', 'new_v_g_mem', 'new_v_w_mem_kv', 'new_v_w_c', 'new_v_w_out', 'new_v_g_ffn', 'new_v_w_up', 'new_v_conv_w', 'new_v_conv_b', 'new_v_w_down', 'new_v_g_final']
TWIN_LEAF_KINDS = {'loss': 'loss', 'grad_x': 'grad_x', 'grad_g_mix': 'grad_w', 'grad_w_in': 'grad_w', 'grad_w_pool': 'grad_w', 'grad_pool_scale': 'grad_w', 'grad_w_a': 'grad_w', 'grad_g_ret': 'grad_w', 'grad_b_ret': 'grad_w', 'grad_w_r': 'grad_w', 'grad_g_mem': 'grad_w', 'grad_w_mem_kv': 'grad_w', 'grad_w_c': 'grad_w', 'grad_w_out': 'grad_w', 'grad_g_ffn': 'grad_w', 'grad_w_up': 'grad_w', 'grad_conv_w': 'grad_w', 'grad_conv_b': 'grad_w', 'grad_w_down': 'grad_w', 'grad_g_final': 'grad_w', 'delta_g_mix': 'delta_w', 'delta_w_in': 'delta_w', 'delta_w_pool': 'delta_w', 'delta_pool_scale': 'delta_w', 'delta_w_a': 'delta_w', 'delta_g_ret': 'delta_w', 'delta_b_ret': 'delta_w', 'delta_w_r': 'delta_w', 'delta_g_mem': 'delta_w', 'delta_w_mem_kv': 'delta_w', 'delta_w_c': 'delta_w', 'delta_w_out': 'delta_w', 'delta_g_ffn': 'delta_w', 'delta_w_up': 'delta_w', 'delta_conv_w': 'delta_w', 'delta_conv_b': 'delta_w', 'delta_w_down': 'delta_w', 'delta_g_final': 'delta_w', 'new_m_g_mix': 'new_m', 'new_m_w_in': 'new_m', 'new_m_w_pool': 'new_m', 'new_m_pool_scale': 'new_m', 'new_m_w_a': 'new_m', 'new_m_g_ret': 'new_m', 'new_m_b_ret': 'new_m', 'new_m_w_r': 'new_m', 'new_m_g_mem': 'new_m', 'new_m_w_mem_kv': 'new_m', 'new_m_w_c': 'new_m', 'new_m_w_out': 'new_m', 'new_m_g_ffn': 'new_m', 'new_m_w_up': 'new_m', 'new_m_conv_w': 'new_m', 'new_m_conv_b': 'new_m', 'new_m_w_down': 'new_m', 'new_m_g_final': 'new_m', 'new_v_g_mix': 'new_v', 'new_v_w_in': 'new_v', 'new_v_w_pool': 'new_v', 'new_v_pool_scale': 'new_v', 'new_v_w_a': 'new_v', 'new_v_g_ret': 'new_v', 'new_v_b_ret': 'new_v', 'new_v_w_r': 'new_v', 'new_v_g_mem': 'new_v', 'new_v_w_mem_kv': 'new_v', 'new_v_w_c': 'new_v', 'new_v_w_out': 'new_v', 'new_v_g_ffn': 'new_v', 'new_v_w_up': 'new_v', 'new_v_conv_w': 'new_v', 'new_v_conv_b': 'new_v', 'new_v_w_down': 'new_v', 'new_v_g_final': 'new_v'}


def _forward(args):
    return _fwd_reference(*[args[k] for k in FWD_PARAMS])


def _output_shape():
    out = _jax.eval_shape(lambda: _forward(_fwd_setup_inputs(0)))
    return out.shape, out.dtype

N_MICROBATCH = 1
ADAM_LR = 0.001
ADAM_B1 = 0.9
ADAM_B2 = 0.999
ADAM_EPS = 1e-08
ADAM_WD = 0.01
ADAM_STEP = 10
PER_EXAMPLE_BATCH_AXIS = {'x': 0, 'mem': 0, 'loss_target': 0}
SHARED_INPUTS = []
_WEIGHT_DTYPES = {'g_mix': _jnp.float32, 'w_in': _jnp.float32, 'w_pool': _jnp.float32, 'pool_scale': _jnp.float32, 'w_a': _jnp.float32, 'g_ret': _jnp.float32, 'b_ret': _jnp.float32, 'w_r': _jnp.float32, 'g_mem': _jnp.float32, 'w_mem_kv': _jnp.float32, 'w_c': _jnp.float32, 'w_out': _jnp.float32, 'g_ffn': _jnp.float32, 'w_up': _jnp.float32, 'conv_w': _jnp.float32, 'conv_b': _jnp.float32, 'w_down': _jnp.float32, 'g_final': _jnp.float32}
MOMENT_SCALE = {'g_mix': 1.535623e-01, 'w_in': 5.493133e-02, 'w_pool': 1.180829e-01, 'pool_scale': 1.135093e-01, 'w_a': 8.325886e-02, 'g_ret': 5.971498e-02, 'b_ret': 6.095506e-02, 'w_r': 5.577509e-02, 'g_mem': 1.609439e-02, 'w_mem_kv': 1.425825e-02, 'w_c': 1.020642e-02, 'w_out': 1.004254e-01, 'g_ffn': 1.434634e-01, 'w_up': 5.368749e-02, 'conv_w': 5.474794e-02, 'conv_b': 5.230216e-02, 'w_down': 8.757026e-02, 'g_final': 3.196290e+01}


def _to_microbatches(a, axis):
    t = _jnp.moveaxis(a, axis, 0)
    t = t.reshape((N_MICROBATCH, t.shape[0] // N_MICROBATCH) + t.shape[1:])
    return _jnp.moveaxis(t, 1, axis + 1)


def setup_inputs(seed: int = 0) -> dict:
    inp = _fwd_setup_inputs(seed)
    key = _jax.random.fold_in(_jax.random.key(seed), 7919)
    shape, _ = _output_shape()
    out = dict(inp)
    out["loss_target"] = _jax.random.normal(_jax.random.fold_in(key, 0), shape, _jnp.float32)
    for i, name in enumerate(TWIN_WEIGHTS):
        w = inp[name].astype(_jnp.float32)
        if MOMENT_SCALE is None:
            s = _jnp.sqrt(_jnp.mean(_jnp.square(w)) + 1e-30)
        else:
            s = MOMENT_SCALE[name]
        km, kv = _jax.random.split(_jax.random.fold_in(key, i + 1))
        out[name] = w
        out["m_" + name] = s * _jax.random.normal(km, w.shape, _jnp.float32)
        out["v_" + name] = (s * s) * _jax.random.uniform(kv, w.shape, _jnp.float32, 0.5, 1.5)
    if N_MICROBATCH > 1:
        for name, axis in PER_EXAMPLE_BATCH_AXIS.items():
            out[name] = _to_microbatches(out[name], axis)
    return {'x': out['x'], 'mem': out['mem'], 'g_mix': out['g_mix'], 'w_in': out['w_in'], 'w_pool': out['w_pool'], 'pool_scale': out['pool_scale'], 'w_a': out['w_a'], 'g_ret': out['g_ret'], 'b_ret': out['b_ret'], 'w_r': out['w_r'], 'g_mem': out['g_mem'], 'w_mem_kv': out['w_mem_kv'], 'w_c': out['w_c'], 'w_out': out['w_out'], 'g_ffn': out['g_ffn'], 'w_up': out['w_up'], 'conv_w': out['conv_w'], 'conv_b': out['conv_b'], 'w_down': out['w_down'], 'g_final': out['g_final'], 'loss_target': out['loss_target'], 'm_g_mix': out['m_g_mix'], 'm_w_in': out['m_w_in'], 'm_w_pool': out['m_w_pool'], 'm_pool_scale': out['m_pool_scale'], 'm_w_a': out['m_w_a'], 'm_g_ret': out['m_g_ret'], 'm_b_ret': out['m_b_ret'], 'm_w_r': out['m_w_r'], 'm_g_mem': out['m_g_mem'], 'm_w_mem_kv': out['m_w_mem_kv'], 'm_w_c': out['m_w_c'], 'm_w_out': out['m_w_out'], 'm_g_ffn': out['m_g_ffn'], 'm_w_up': out['m_w_up'], 'm_conv_w': out['m_conv_w'], 'm_conv_b': out['m_conv_b'], 'm_w_down': out['m_w_down'], 'm_g_final': out['m_g_final'], 'v_g_mix': out['v_g_mix'], 'v_w_in': out['v_w_in'], 'v_w_pool': out['v_w_pool'], 'v_pool_scale': out['v_pool_scale'], 'v_w_a': out['v_w_a'], 'v_g_ret': out['v_g_ret'], 'v_b_ret': out['v_b_ret'], 'v_w_r': out['v_w_r'], 'v_g_mem': out['v_g_mem'], 'v_w_mem_kv': out['v_w_mem_kv'], 'v_w_c': out['v_w_c'], 'v_w_out': out['v_w_out'], 'v_g_ffn': out['v_g_ffn'], 'v_w_up': out['v_w_up'], 'v_conv_w': out['v_conv_w'], 'v_conv_b': out['v_conv_b'], 'v_w_down': out['v_w_down'], 'v_g_final': out['v_g_final']}


def _loss(weights, diff, rest, loss_target):
    with _jax.named_scope("forward"):
        args = {**rest, TWIN_DIFF_INPUT: diff, **{k: w.astype(_WEIGHT_DTYPES[k]) for k, w in weights.items()}}
        y = _forward(args)
    with _jax.named_scope("loss_head"):
        err = _jnp.square(y.astype(_jnp.float32) - loss_target)
        return 0.5 * _jnp.sum(_jnp.mean(err, axis=-1)) if err.ndim else 0.5 * err


def _adamw(w, g, m, v):
    m = ADAM_B1 * m + (1.0 - ADAM_B1) * g
    v = ADAM_B2 * v + (1.0 - ADAM_B2) * _jnp.square(g)
    m_hat = m / (1.0 - ADAM_B1 ** ADAM_STEP)
    v_hat = v / (1.0 - ADAM_B2 ** ADAM_STEP)
    delta = -ADAM_LR * (m_hat / (_jnp.sqrt(v_hat) + ADAM_EPS) + ADAM_WD * w)
    return delta, m, v


def reference(x, mem, g_mix, w_in, w_pool, pool_scale, w_a, g_ret, b_ret, w_r, g_mem, w_mem_kv, w_c, w_out, g_ffn, w_up, conv_w, conv_b, w_down, g_final, loss_target, m_g_mix, m_w_in, m_w_pool, m_pool_scale, m_w_a, m_g_ret, m_b_ret, m_w_r, m_g_mem, m_w_mem_kv, m_w_c, m_w_out, m_g_ffn, m_w_up, m_conv_w, m_conv_b, m_w_down, m_g_final, v_g_mix, v_w_in, v_w_pool, v_pool_scale, v_w_a, v_g_ret, v_b_ret, v_w_r, v_g_mem, v_w_mem_kv, v_w_c, v_w_out, v_g_ffn, v_w_up, v_conv_w, v_conv_b, v_w_down, v_g_final):
    given = dict(x=x, mem=mem, g_mix=g_mix, w_in=w_in, w_pool=w_pool, pool_scale=pool_scale, w_a=w_a, g_ret=g_ret, b_ret=b_ret, w_r=w_r, g_mem=g_mem, w_mem_kv=w_mem_kv, w_c=w_c, w_out=w_out, g_ffn=g_ffn, w_up=w_up, conv_w=conv_w, conv_b=conv_b, w_down=w_down, g_final=g_final, loss_target=loss_target, m_g_mix=m_g_mix, m_w_in=m_w_in, m_w_pool=m_w_pool, m_pool_scale=m_pool_scale, m_w_a=m_w_a, m_g_ret=m_g_ret, m_b_ret=m_b_ret, m_w_r=m_w_r, m_g_mem=m_g_mem, m_w_mem_kv=m_w_mem_kv, m_w_c=m_w_c, m_w_out=m_w_out, m_g_ffn=m_g_ffn, m_w_up=m_w_up, m_conv_w=m_conv_w, m_conv_b=m_conv_b, m_w_down=m_w_down, m_g_final=m_g_final, v_g_mix=v_g_mix, v_w_in=v_w_in, v_w_pool=v_w_pool, v_pool_scale=v_pool_scale, v_w_a=v_w_a, v_g_ret=v_g_ret, v_b_ret=v_b_ret, v_w_r=v_w_r, v_g_mem=v_g_mem, v_w_mem_kv=v_w_mem_kv, v_w_c=v_w_c, v_w_out=v_w_out, v_g_ffn=v_g_ffn, v_w_up=v_w_up, v_conv_w=v_conv_w, v_conv_b=v_conv_b, v_w_down=v_w_down, v_g_final=v_g_final)
    weights = {n: given[n] for n in TWIN_WEIGHTS}
    shared = {n: given[n] for n in SHARED_INPUTS}
    per_example = {n: given[n] for n in ['x', 'mem']}
    grad_fn = _jax.value_and_grad(_loss, argnums=(0, 1))

    def one_microbatch(ex, loss_target):
        ex = dict(ex)
        diff = ex.pop(TWIN_DIFF_INPUT)
        return grad_fn(weights, diff, {**shared, **ex}, loss_target)

    if N_MICROBATCH == 1:
        loss, (grad_w, grad_x) = one_microbatch(per_example, given["loss_target"])
    else:
        def body(carry, xs):
            loss_sum, grad_sum = carry
            l_k, (gw_k, gx_k) = one_microbatch(xs[0], xs[1])
            with _jax.named_scope("update"):
                return (loss_sum + l_k, _jax.tree.map(_jnp.add, grad_sum, gw_k)), gx_k

        init = (_jnp.zeros((), _jnp.float32), _jax.tree.map(_jnp.zeros_like, weights))
        (loss, grad_w), grad_x = _jax.lax.scan(body, init, (per_example, given["loss_target"]))
    with _jax.named_scope("update"):
        delta_w, new_m, new_v = {}, {}, {}
        for n in TWIN_WEIGHTS:
            delta_w[n], new_m[n], new_v[n] = _adamw(weights[n], grad_w[n], given["m_" + n], given["v_" + n])
    return (loss, grad_x, *[grad_w[n] for n in TWIN_WEIGHTS], *[delta_w[n] for n in TWIN_WEIGHTS],
            *[new_m[n] for n in TWIN_WEIGHTS], *[new_v[n] for n in TWIN_WEIGHTS])
```

```python
import functools
import math

import jax
import jax.numpy as jnp
from jax import lax
from jax.experimental import pallas as pl
from jax.experimental.pallas import tpu as pltpu

F32 = jnp.float32
BF16 = jnp.bfloat16

N_DEV = 8
D_MODEL = 1024
POOL_WINDOWS = (2, 4, 8, 16)
POOL_GROUP_DIM = 128
POOL_WIDTH = 512
POOL_HALO = 16
RET_HEADS = 4
RET_QK_DIM = 128
RET_V_DIM = 256
RET_CHUNK = 128
ROPE_BASE = 10000.0
XA_HEADS = 4
XA_HEAD_DIM = 128
XA_WIDTH = 512
IN_WIDTH = 7168
IN_SHARD = IN_WIDTH // N_DEV
FFN_HIDDEN = 2816
UP_SHARD = 2 * FFN_HIDDEN // N_DEV
FFN_SLABS = FFN_HIDDEN // UP_SHARD
EPS = 1e-6
ADAM_LR = 0.001
ADAM_B1 = 0.9
ADAM_B2 = 0.999
ADAM_EPS = 1e-08
ADAM_WD = 0.01
ADAM_STEP = 10
GELU_C = math.sqrt(2.0 / math.pi)
GELU_A = 0.044715
VMEM_LIMIT = 56 * 1024 * 1024
PACK_LANES = 1024
MESH = pl.DeviceIdType.MESH

COL_Q, COL_K, COL_V, COL_GR, COL_QX, COL_GL = 512, 1024, 1536, 2560, 3584, 4096

_DIMS = {
    "nn": (((1,), (0,)), ((), ())),
    "nt": (((1,), (1,)), ((), ())),
    "tn": (((0,), (0,)), ((), ())),
}


def _dot(a, b, kind="nn"):
    return lax.dot_general(a.astype(BF16), b.astype(BF16), _DIMS[kind], preferred_element_type=F32)


def _params(sem, vmem=VMEM_LIMIT):
    return pltpu.CompilerParams(dimension_semantics=sem, vmem_limit_bytes=vmem)


def _tile(n, pref):
    t = min(n, pref)
    while n % t:
        t //= 2
    return t


def _matmul(name, kind, a, b, out_shape, grid, a_spec, b_spec, o_spec, acc_shape, res=None, res_spec=None):
    nk = grid[-1]
    has_res = res is not None

    def body(*refs):
        a_ref, b_ref = refs[0], refs[1]
        o_ref = refs[2 + has_res]

        def prod():
            return _dot(a_ref[...], b_ref[...], kind)

        def finish(acc):
            if has_res:
                acc = acc + refs[2][...]
            o_ref[...] = acc.astype(o_ref.dtype)

        if nk == 1:
            finish(prod())
        else:
            acc_ref = refs[3 + has_res]
            k = pl.program_id(len(grid) - 1)

            @pl.when(k == 0)
            def _():
                acc_ref[...] = prod()

            @pl.when(k > 0)
            def _():
                acc_ref[...] += prod()

            @pl.when(k == nk - 1)
            def _():
                finish(acc_ref[...])

    in_specs = [a_spec, b_spec] + ([res_spec] if has_res else [])
    args = (a, b) + ((res,) if has_res else ())
    scratch = [pltpu.VMEM(acc_shape, F32)] if nk > 1 else []
    sem = ("parallel",) * (len(grid) - 1) + ("arbitrary",)
    return pl.pallas_call(
        body, out_shape=out_shape, grid=grid, in_specs=in_specs, out_specs=o_spec,
        scratch_shapes=scratch, name=name, compiler_params=_params(sem),
    )(*args)


def _mm_rows(name, a, w, out_dtype=F32, res=None, kind="nn", tm=512):
    M, K = a.shape
    N = w.shape[1] if kind == "nn" else w.shape[0]
    tm = _tile(M, tm)
    res_spec = pl.BlockSpec((tm, N), lambda i, k: (i, 0)) if res is not None else None
    return _matmul(
        name, kind, a, w, jax.ShapeDtypeStruct((M, N), out_dtype), (M // tm, 1),
        pl.BlockSpec((tm, K), lambda i, k: (i, 0)), pl.BlockSpec(w.shape, lambda i, k: (0, 0)),
        pl.BlockSpec((tm, N), lambda i, k: (i, 0)), (tm, N), res, res_spec)


def _mm_tn(name, a, b, out_dtype=F32, tk=512):
    T, M = a.shape
    N = b.shape[1]
    tk = _tile(T, tk)
    return _matmul(
        name, "tn", a, b, jax.ShapeDtypeStruct((M, N), out_dtype), (1, T // tk),
        pl.BlockSpec((tk, M), lambda i, k: (k, 0)), pl.BlockSpec((tk, N), lambda i, k: (k, 0)),
        pl.BlockSpec((M, N), lambda i, k: (0, 0)), (M, N))


def _mm_cols_slab(name, a, w_slabs, out_dtype=F32, tm=512):
    M, K = a.shape
    J, _, n = w_slabs.shape
    tm = _tile(M, tm)
    return _matmul(
        name, "nn", a, w_slabs, jax.ShapeDtypeStruct((M, J * n), out_dtype), (J, M // tm, 1),
        pl.BlockSpec((tm, K), lambda j, i, k: (i, 0)), pl.BlockSpec((None, K, n), lambda j, i, k: (j, 0, 0)),
        pl.BlockSpec((tm, n), lambda j, i, k: (i, j)), (tm, n))


def _mm_cols_slab_t(name, a, w_slabs, out_dtype=F32, tm=512):
    M = a.shape[0]
    J, K, n = w_slabs.shape
    tm = _tile(M, tm)
    return _matmul(
        name, "nt", a, w_slabs, jax.ShapeDtypeStruct((M, K), out_dtype), (M // tm, J),
        pl.BlockSpec((tm, n), lambda i, j: (i, j)), pl.BlockSpec((None, K, n), lambda i, j: (j, 0, 0)),
        pl.BlockSpec((tm, K), lambda i, j: (i, 0)), (tm, K))


def _mm_tn_slab(name, a, b, n, out_dtype=F32, tk=512):
    T, M = a.shape
    J = b.shape[1] // n
    tk = _tile(T, tk)
    return _matmul(
        name, "tn", a, b, jax.ShapeDtypeStruct((J, M, n), out_dtype), (J, T // tk),
        pl.BlockSpec((tk, M), lambda j, k: (k, 0)), pl.BlockSpec((tk, n), lambda j, k: (k, j)),
        pl.BlockSpec((None, M, n), lambda j, k: (j, 0, 0)), (M, n))


def _rms_fwd(name, x, g, tm=512):
    T, Dm = x.shape
    tm = _tile(T, tm)

    def body(x_ref, g_ref, h_ref):
        xv = x_ref[...]
        r = lax.rsqrt(jnp.mean(xv * xv, axis=-1, keepdims=True) + EPS)
        h_ref[...] = (xv * r * g_ref[...]).astype(h_ref.dtype)

    return pl.pallas_call(
        body, out_shape=jax.ShapeDtypeStruct((T, Dm), BF16), grid=(T // tm,),
        in_specs=[pl.BlockSpec((tm, Dm), lambda i: (i, 0)), pl.BlockSpec((1, Dm), lambda i: (0, 0))],
        out_specs=pl.BlockSpec((tm, Dm), lambda i: (i, 0)), name=name, compiler_params=_params(("parallel",)),
    )(x, g)


def _rms_bwd(name, x, g, dh, dres, tm=512):
    T, Dm = x.shape
    tm = _tile(T, tm)
    want_dx = dres is not None

    def body(*refs):
        if want_dx:
            x_ref, g_ref, dh_ref, dres_ref, dx_ref, dg_ref = refs
        else:
            x_ref, g_ref, dh_ref, dg_ref = refs
        xv = x_ref[...]
        r = lax.rsqrt(jnp.mean(xv * xv, axis=-1, keepdims=True) + EPS)
        xhat = xv * r
        dhv = dh_ref[...]

        @pl.when(pl.program_id(0) == 0)
        def _():
            dg_ref[...] = jnp.zeros_like(dg_ref)

        dg_ref[...] += jnp.sum(dhv * xhat, axis=0, keepdims=True)
        if want_dx:
            dxhat = dhv * g_ref[...]
            dx_ref[...] = dres_ref[...] + r * (dxhat - xhat * jnp.mean(dxhat * xhat, axis=-1, keepdims=True))

    row = pl.BlockSpec((tm, Dm), lambda i: (i, 0))
    vec = pl.BlockSpec((1, Dm), lambda i: (0, 0))
    if want_dx:
        return pl.pallas_call(
            body, out_shape=(jax.ShapeDtypeStruct((T, Dm), F32), jax.ShapeDtypeStruct((1, Dm), F32)),
            grid=(T // tm,), in_specs=[row, vec, row, row], out_specs=(row, vec), name=name,
            compiler_params=_params(("arbitrary",)),
        )(x, g, dh, dres)
    return pl.pallas_call(
        body, out_shape=jax.ShapeDtypeStruct((1, Dm), F32), grid=(T // tm,), in_specs=[row, vec, row],
        out_specs=vec, name=name, compiler_params=_params(("arbitrary",)),
    )(x, g, dh)


def _pool_rows(S):
    return _tile(S, 256)


def _pool_count(c0, rows, w):
    t = c0 + lax.broadcasted_iota(jnp.int32, (rows, 1), 0)
    return jnp.minimum(t + 1, w).astype(F32)


def _pool_fwd(proj, w_pool, scale, B, S):
    CH = _pool_rows(S)

    def body(hp_ref, wp_ref, sc_ref, o_ref, pad_ref):
        pad_ref[0:POOL_HALO, :] = jnp.zeros((POOL_HALO, POOL_WIDTH), F32)
        pad_ref[POOL_HALO:, :] = hp_ref[...]
        for gi, w in enumerate(POOL_WINDOWS):
            cols = slice(gi * POOL_GROUP_DIM, (gi + 1) * POOL_GROUP_DIM)
            for c in range(S // CH):
                base = POOL_HALO + c * CH
                acc = pad_ref[base:base + CH, cols]
                tok = acc
                for j in range(1, w):
                    acc = acc + pad_ref[base - j:base - j + CH, cols]
                pooled = acc / _pool_count(c * CH, CH, w) - tok
                z = _dot(pooled, wp_ref[gi])
                o_ref[c * CH:(c + 1) * CH, cols] = (z * sc_ref[:, cols]).astype(o_ref.dtype)

    return pl.pallas_call(
        body, out_shape=jax.ShapeDtypeStruct((B * S, POOL_WIDTH), BF16), grid=(B,),
        in_specs=[pl.BlockSpec((S, POOL_WIDTH), lambda b: (b, 0)),
                  pl.BlockSpec(w_pool.shape, lambda b: (0, 0, 0)),
                  pl.BlockSpec((1, POOL_WIDTH), lambda b: (0, 0))],
        out_specs=pl.BlockSpec((S, POOL_WIDTH), lambda b: (b, 0)),
        scratch_shapes=[pltpu.VMEM((S + POOL_HALO, POOL_WIDTH), F32)],
        name="pool_fwd", compiler_params=_params(("parallel",)),
    )(proj, w_pool, scale)


def _pool_bwd(proj, d_ypre, w_pool, scale, B, S):
    CH = _pool_rows(S)

    def body(hp_ref, dy_ref, wp_ref, sc_ref, dhp_ref, dwp_ref, dsc_ref, pad_ref, sc_pad_ref, dp_ref):
        @pl.when(pl.program_id(0) == 0)
        def _():
            dwp_ref[...] = jnp.zeros_like(dwp_ref)
            dsc_ref[...] = jnp.zeros_like(dsc_ref)

        pad_ref[0:POOL_HALO, :] = jnp.zeros((POOL_HALO, POOL_WIDTH), F32)
        pad_ref[POOL_HALO:, :] = hp_ref[...]
        sc_pad_ref[S:, :] = jnp.zeros((POOL_HALO, POOL_WIDTH), F32)
        for gi, w in enumerate(POOL_WINDOWS):
            cols = slice(gi * POOL_GROUP_DIM, (gi + 1) * POOL_GROUP_DIM)
            for c in range(S // CH):
                base = POOL_HALO + c * CH
                rows = slice(c * CH, (c + 1) * CH)
                acc = pad_ref[base:base + CH, cols]
                tok = acc
                for j in range(1, w):
                    acc = acc + pad_ref[base - j:base - j + CH, cols]
                cnt = _pool_count(c * CH, CH, w)
                pooled = acc / cnt - tok
                z = _dot(pooled, wp_ref[gi])
                dy = dy_ref[rows, cols]
                dsc_ref[:, cols] += jnp.sum(dy * z, axis=0, keepdims=True)
                dz = dy * sc_ref[:, cols]
                dwp_ref[gi] += _dot(pooled, dz, "tn")
                dpool = _dot(dz, wp_ref[gi], "nt")
                dp_ref[rows, cols] = dpool
                sc_pad_ref[rows, cols] = dpool / cnt
            for c in range(S // CH):
                rows = slice(c * CH, (c + 1) * CH)
                acc = sc_pad_ref[rows, cols]
                for j in range(1, w):
                    acc = acc + sc_pad_ref[c * CH + j:c * CH + j + CH, cols]
                dhp_ref[rows, cols] = (acc - dp_ref[rows, cols]).astype(dhp_ref.dtype)

    seq = pl.BlockSpec((S, POOL_WIDTH), lambda b: (b, 0))
    return pl.pallas_call(
        body,
        out_shape=(jax.ShapeDtypeStruct((B * S, POOL_WIDTH), BF16),
                   jax.ShapeDtypeStruct(w_pool.shape, F32), jax.ShapeDtypeStruct((1, POOL_WIDTH), F32)),
        grid=(B,),
        in_specs=[seq, seq, pl.BlockSpec(w_pool.shape, lambda b: (0, 0, 0)),
                  pl.BlockSpec((1, POOL_WIDTH), lambda b: (0, 0))],
        out_specs=(seq, pl.BlockSpec(w_pool.shape, lambda b: (0, 0, 0)),
                   pl.BlockSpec((1, POOL_WIDTH), lambda b: (0, 0))),
        scratch_shapes=[pltpu.VMEM((S + POOL_HALO, POOL_WIDTH), F32),
                        pltpu.VMEM((S + POOL_HALO, POOL_WIDTH), F32),
                        pltpu.VMEM((S, POOL_WIDTH), F32)],
        name="pool_bwd", compiler_params=_params(("arbitrary",)),
    )(proj, d_ypre, w_pool, scale)


def _ret_tables(S):
    half = RET_QK_DIM // 2
    inv = ROPE_BASE ** (-jnp.arange(half, dtype=F32) / half)
    ang = jnp.arange(S, dtype=F32)[:, None] * inv[None, :]
    cos, sin = jnp.cos(ang), jnp.sin(ang)
    cos_full = jnp.concatenate([cos, cos], axis=-1)
    sin_signed = jnp.concatenate([-sin, sin], axis=-1)
    C = RET_CHUNK
    lg = jnp.log1p(-jnp.exp2(-5.0 - jnp.arange(RET_HEADS, dtype=F32)))[:, None, None]
    idx = jnp.arange(C, dtype=F32)
    rel = idx[:, None] - idx[None, :]
    decay = jnp.where(rel >= 0, jnp.exp(jnp.maximum(rel, 0.0) * lg), 0.0)
    q_decay = jnp.broadcast_to(jnp.exp((idx + 1.0)[None, :, None] * lg), (RET_HEADS, C, RET_QK_DIM))
    k_decay = jnp.broadcast_to(jnp.exp((C - 1.0 - idx)[None, :, None] * lg), (RET_HEADS, C, RET_QK_DIM))
    c_decay = jnp.broadcast_to(jnp.exp(C * lg), (RET_HEADS, 1, RET_V_DIM))
    return cos_full, sin_signed, decay, q_decay, k_decay, c_decay


def _rope(x, cos_full, sin_signed):
    return x * cos_full + pltpu.roll(x, RET_QK_DIM // 2, axis=1) * sin_signed


def _rope_t(dy, cos_full, sin_signed):
    return dy * cos_full + pltpu.roll(dy * sin_signed, RET_QK_DIM // 2, axis=1)


def _ret_specs(S, bh):
    def at(col_of_head, width):
        def index(*ids):
            b, h = bh(*ids)
            return (b, col_of_head + h)
        return pl.BlockSpec((S, width), index)

    def per_head(shape):
        def index(*ids):
            _, h = bh(*ids)
            return (h,) + (0,) * len(shape)
        return pl.BlockSpec((None,) + shape, index)

    def head_vec(width):
        def index(*ids):
            _, h = bh(*ids)
            return (0, h)
        return pl.BlockSpec((1, width), index)

    table = pl.BlockSpec((S, RET_QK_DIM), lambda *ids: (0, 0))
    C = RET_CHUNK
    return dict(
        q=at(COL_Q // RET_QK_DIM, RET_QK_DIM), k=at(COL_K // RET_QK_DIM, RET_QK_DIM),
        v=at(COL_V // RET_V_DIM, RET_V_DIM), gr=at(COL_GR // RET_V_DIM, RET_V_DIM),
        table=table, decay=per_head((C, C)), qd=per_head((C, RET_QK_DIM)), kd=per_head((C, RET_QK_DIM)),
        cd=per_head((1, RET_V_DIM)), vec=head_vec(RET_V_DIM), out_qk=at(0, RET_QK_DIM), out_v=at(0, RET_V_DIM))


def _group_norm(o):
    mu = jnp.mean(o, axis=-1, keepdims=True)
    oc = o - mu
    rstd = lax.rsqrt(jnp.mean(oc * oc, axis=-1, keepdims=True) + EPS)
    return oc * rstd, rstd


def _ret_fwd(proj, g_ret, b_ret, tables, B, S):
    C = RET_CHUNK
    cos_t, sin_t, decay, q_decay, k_decay, c_decay = tables
    sp = _ret_specs(S, lambda b, h: (b, h))

    def body(q_ref, k_ref, v_ref, gr_ref, cos_ref, sin_ref, dec_ref, qd_ref, kd_ref, cd_ref, g_ref, b_ref,
             y_ref, r_ref):
        r_ref[...] = jnp.zeros_like(r_ref)

        def chunk(i, carry):
            rows = pl.ds(pl.multiple_of(i * C, C), C)
            cs, sn = cos_ref[rows, :], sin_ref[rows, :]
            q = _rope(q_ref[rows, :], cs, sn)
            k = _rope(k_ref[rows, :], cs, sn) * (RET_QK_DIM ** -0.5)
            v = v_ref[rows, :]
            R = r_ref[...]
            s = _dot(q, k, "nt") * dec_ref[...]
            o = _dot(s, v) + _dot(q * qd_ref[...], R)
            r_ref[...] = cd_ref[...] * R + _dot(k * kd_ref[...], v, "tn")
            on, _ = _group_norm(o)
            gr = gr_ref[rows, :]
            y_ref[rows, :] = (gr * jax.nn.sigmoid(gr) * (on * g_ref[...] + b_ref[...])).astype(y_ref.dtype)
            return carry

        lax.fori_loop(0, S // C, chunk, 0)

    return pl.pallas_call(
        body, out_shape=jax.ShapeDtypeStruct((B * S, RET_HEADS * RET_V_DIM), BF16), grid=(B, RET_HEADS),
        in_specs=[sp["q"], sp["k"], sp["v"], sp["gr"], sp["table"], sp["table"], sp["decay"], sp["qd"],
                  sp["kd"], sp["cd"], sp["vec"], sp["vec"]],
        out_specs=sp["out_v"], scratch_shapes=[pltpu.VMEM((RET_QK_DIM, RET_V_DIM), F32)],
        name="ret_fwd", compiler_params=_params(("parallel", "parallel")),
    )(proj, proj, proj, proj, cos_t, sin_t, decay, q_decay, k_decay, c_decay, g_ret, b_ret)


def _ret_bwd(proj, d_yr, g_ret, b_ret, tables, B, S):
    C = RET_CHUNK
    N = S // C
    cos_t, sin_t, decay, q_decay, k_decay, c_decay = tables
    sp = _ret_specs(S, lambda h, b: (b, h))
    qk_scale = RET_QK_DIM ** -0.5

    def body(q_ref, k_ref, v_ref, gr_ref, dy_ref, cos_ref, sin_ref, dec_ref, qd_ref, kd_ref, cd_ref, g_ref, b_ref,
             dq_ref, dk_ref, dv_ref, dgr_ref, dg_ref, db_ref, qr_ref, kr_ref, rs_ref, dr_ref):
        @pl.when(pl.program_id(1) == 0)
        def _():
            dg_ref[...] = jnp.zeros_like(dg_ref)
            db_ref[...] = jnp.zeros_like(db_ref)

        dr_ref[...] = jnp.zeros_like(dr_ref)

        def sweep(i, R):
            rows = pl.ds(pl.multiple_of(i * C, C), C)
            cs, sn = cos_ref[rows, :], sin_ref[rows, :]
            q = _rope(q_ref[rows, :], cs, sn)
            k = _rope(k_ref[rows, :], cs, sn) * qk_scale
            qr_ref[rows, :] = q
            kr_ref[rows, :] = k
            rs_ref[i] = R
            return cd_ref[...] * R + _dot(k * kd_ref[...], v_ref[rows, :], "tn")

        lax.fori_loop(0, N, sweep, jnp.zeros((RET_QK_DIM, RET_V_DIM), F32))

        def back(step, carry):
            i = N - 1 - step
            rows = pl.ds(pl.multiple_of(i * C, C), C)
            q, k, v = qr_ref[rows, :], kr_ref[rows, :], v_ref[rows, :]
            R, dR = rs_ref[i], dr_ref[...]
            dec, qd, kd = dec_ref[...], qd_ref[...], kd_ref[...]
            s = _dot(q, k, "nt") * dec
            o = _dot(s, v) + _dot(q * qd, R)
            on, rstd = _group_norm(o)
            oaff = on * g_ref[...] + b_ref[...]
            gr = gr_ref[rows, :]
            sg = jax.nn.sigmoid(gr)
            dy = dy_ref[rows, :]
            dgr_ref[rows, :] = (dy * oaff * (sg * (1.0 + gr * (1.0 - sg)))).astype(dgr_ref.dtype)
            doaff = dy * (gr * sg)
            dg_ref[...] += jnp.sum(doaff * on, axis=0, keepdims=True)
            db_ref[...] += jnp.sum(doaff, axis=0, keepdims=True)
            don = doaff * g_ref[...]
            do = rstd * (don - jnp.mean(don, axis=-1, keepdims=True)
                         - on * jnp.mean(don * on, axis=-1, keepdims=True))
            ds = _dot(do, v, "nt") * dec
            dq = _dot(ds, k) + qd * _dot(do, R, "nt")
            dk = _dot(ds, q, "tn") + kd * _dot(v, dR, "nt")
            dv_ref[rows, :] = (_dot(s, do, "tn") + _dot(k * kd, dR)).astype(dv_ref.dtype)
            dr_ref[...] = cd_ref[...] * dR + _dot(q * qd, do, "tn")
            cs, sn = cos_ref[rows, :], sin_ref[rows, :]
            dq_ref[rows, :] = _rope_t(dq, cs, sn).astype(dq_ref.dtype)
            dk_ref[rows, :] = _rope_t(dk * qk_scale, cs, sn).astype(dk_ref.dtype)
            return carry

        lax.fori_loop(0, N, back, 0)

    T = B * S
    qk_shape = jax.ShapeDtypeStruct((T, RET_HEADS * RET_QK_DIM), BF16)
    v_shape = jax.ShapeDtypeStruct((T, RET_HEADS * RET_V_DIM), BF16)
    vec_shape = jax.ShapeDtypeStruct((1, RET_HEADS * RET_V_DIM), F32)
    return pl.pallas_call(
        body, out_shape=(qk_shape, qk_shape, v_shape, v_shape, vec_shape, vec_shape), grid=(RET_HEADS, B),
        in_specs=[sp["q"], sp["k"], sp["v"], sp["gr"], sp["out_v"], sp["table"], sp["table"], sp["decay"],
                  sp["qd"], sp["kd"], sp["cd"], sp["vec"], sp["vec"]],
        out_specs=(sp["out_qk"], sp["out_qk"], sp["out_v"], sp["out_v"], sp["vec"], sp["vec"]),
        scratch_shapes=[pltpu.VMEM((S, RET_QK_DIM), F32), pltpu.VMEM((S, RET_QK_DIM), F32),
                        pltpu.VMEM((N, RET_QK_DIM, RET_V_DIM), F32), pltpu.VMEM((RET_QK_DIM, RET_V_DIM), F32)],
        name="ret_bwd", compiler_params=_params(("parallel", "arbitrary")),
    )(proj, proj, proj, proj, d_yr, cos_t, sin_t, decay, q_decay, k_decay, c_decay, g_ret, b_ret)


def _xa_rows(S):
    return _tile(S, 256)


def _xa_specs(S, M):
    q = pl.BlockSpec((S, XA_HEAD_DIM), lambda b, h: (b, COL_QX // XA_HEAD_DIM + h))
    k = pl.BlockSpec((M, XA_HEAD_DIM), lambda b, h: (b, h))
    v = pl.BlockSpec((M, XA_HEAD_DIM), lambda b, h: (b, XA_HEADS + h))
    o = pl.BlockSpec((S, XA_HEAD_DIM), lambda b, h: (b, h))
    return q, k, v, o


def _softmax_rows(s):
    e = jnp.exp(s - jnp.max(s, axis=-1, keepdims=True))
    return e / jnp.sum(e, axis=-1, keepdims=True)


def _xa_fwd(proj, kv, B, S, M):
    CH = _xa_rows(S)
    q_spec, k_spec, v_spec, o_spec = _xa_specs(S, M)

    def body(q_ref, k_ref, v_ref, o_ref):
        def chunk(i, carry):
            rows = pl.ds(pl.multiple_of(i * CH, CH), CH)
            p = _softmax_rows(_dot(q_ref[rows, :], k_ref[...], "nt") * (XA_HEAD_DIM ** -0.5))
            o_ref[rows, :] = _dot(p, v_ref[...]).astype(o_ref.dtype)
            return carry

        lax.fori_loop(0, S // CH, chunk, 0)

    return pl.pallas_call(
        body, out_shape=jax.ShapeDtypeStruct((B * S, XA_WIDTH), BF16), grid=(B, XA_HEADS),
        in_specs=[q_spec, k_spec, v_spec], out_specs=o_spec, name="xattn_fwd",
        compiler_params=_params(("parallel", "parallel")),
    )(proj, kv, kv)


def _xa_bwd(proj, kv, d_o, B, S, M):
    CH = _xa_rows(S)
    q_spec, k_spec, v_spec, o_spec = _xa_specs(S, M)
    scale = XA_HEAD_DIM ** -0.5

    def body(q_ref, k_ref, v_ref, do_ref, dq_ref, dk_ref, dv_ref):
        dk_ref[...] = jnp.zeros_like(dk_ref)
        dv_ref[...] = jnp.zeros_like(dv_ref)

        def chunk(i, carry):
            rows = pl.ds(pl.multiple_of(i * CH, CH), CH)
            q, do = q_ref[rows, :], do_ref[rows, :]
            p = _softmax_rows(_dot(q, k_ref[...], "nt") * scale)
            dp = _dot(do, v_ref[...], "nt")
            ds = p * (dp - jnp.sum(dp * p, axis=-1, keepdims=True)) * scale
            dq_ref[rows, :] = _dot(ds, k_ref[...]).astype(dq_ref.dtype)
            dk_ref[...] += _dot(ds, q, "tn")
            dv_ref[...] += _dot(p, do, "tn")
            return carry

        lax.fori_loop(0, S // CH, chunk, 0)

    kv_out = pl.BlockSpec((M, XA_HEAD_DIM), lambda b, h: (b, h))
    return pl.pallas_call(
        body,
        out_shape=(jax.ShapeDtypeStruct((B * S, XA_WIDTH), BF16), jax.ShapeDtypeStruct((B * M, XA_WIDTH), F32),
                   jax.ShapeDtypeStruct((B * M, XA_WIDTH), F32)),
        grid=(B, XA_HEADS), in_specs=[q_spec, k_spec, v_spec, o_spec], out_specs=(o_spec, kv_out, kv_out),
        name="xattn_bwd", compiler_params=_params(("parallel", "parallel")),
    )(proj, kv, kv, d_o)


def _gate_specs(tm):
    n = COL_GL // D_MODEL
    return [pl.BlockSpec((tm, D_MODEL), lambda i, j=j: (i, n + j)) for j in range(3)]


def _merge_fwd(proj, ys, tm=256):
    T = proj.shape[0]
    tm = _tile(T, tm)
    row = pl.BlockSpec((tm, D_MODEL), lambda i: (i, 0))

    def body(g0, g1, g2, y0, y1, y2, o_ref):
        acc = jax.nn.sigmoid(g0[...]) * y0[...]
        acc = acc + jax.nn.sigmoid(g1[...]) * y1[...]
        acc = acc + jax.nn.sigmoid(g2[...]) * y2[...]
        o_ref[...] = acc.astype(o_ref.dtype)

    return pl.pallas_call(
        body, out_shape=jax.ShapeDtypeStruct((T, D_MODEL), BF16), grid=(T // tm,),
        in_specs=_gate_specs(tm) + [row] * 3, out_specs=row, name="merge_fwd",
        compiler_params=_params(("parallel",)),
    )(proj, proj, proj, *ys)


def _merge_bwd(proj, ys, d_merged, tm=256):
    T = proj.shape[0]
    tm = _tile(T, tm)
    row = pl.BlockSpec((tm, D_MODEL), lambda i: (i, 0))

    def body(g0, g1, g2, y0, y1, y2, dm_ref, dgl_ref, d0, d1, d2):
        dm = dm_ref[...]
        for j, (g_ref, y_ref, d_ref) in enumerate(((g0, y0, d0), (g1, y1, d1), (g2, y2, d2))):
            sg = jax.nn.sigmoid(g_ref[...])
            d_ref[...] = (dm * sg).astype(d_ref.dtype)
            dgl_ref[:, j * D_MODEL:(j + 1) * D_MODEL] = (dm * y_ref[...] * sg * (1.0 - sg)).astype(dgl_ref.dtype)

    dy = jax.ShapeDtypeStruct((T, D_MODEL), BF16)
    return pl.pallas_call(
        body, out_shape=(jax.ShapeDtypeStruct((T, 3 * D_MODEL), BF16), dy, dy, dy), grid=(T // tm,),
        in_specs=_gate_specs(tm) + [row] * 4,
        out_specs=(pl.BlockSpec((tm, 3 * D_MODEL), lambda i: (i, 0)), row, row, row), name="merge_bwd",
        compiler_params=_params(("parallel",)),
    )(proj, proj, proj, *ys, d_merged)


def _gelu(x):
    return 0.5 * x * (1.0 + jnp.tanh(GELU_C * (x + GELU_A * x * x * x)))


def _gelu_grad(x):
    t = jnp.tanh(GELU_C * (x + GELU_A * x * x * x))
    return 0.5 * (1.0 + t) + 0.5 * x * (1.0 - t * t) * GELU_C * (1.0 + 3.0 * GELU_A * x * x)


def _shift_down(x, prev, n):
    rows = x.shape[0]
    r = lax.broadcasted_iota(jnp.int32, (rows, 1), 0)
    out = pltpu.roll(x, n, axis=0)
    for j in range(n):
        out = jnp.where(r == j, prev[8 - n + j:8 - n + j + 1, :], out)
    return out


def _shift_up(x, nxt, n):
    rows = x.shape[0]
    r = lax.broadcasted_iota(jnp.int32, (rows, 1), 0)
    out = pltpu.roll(x, rows - n, axis=0)
    for j in range(n):
        out = jnp.where(r == rows - n + j, nxt[j:j + 1, :], out)
    return out


def _conv(a, prev, cw, cb):
    return _shift_down(a, prev, 2) * cw[0:1, :] + _shift_down(a, prev, 1) * cw[1:2, :] + a * cw[2:3, :] + cb


def _glu_fwd(up, cw, cb, S, tm=256):
    T = up.shape[2]
    tm = _tile(S, tm)
    per_seq = S // tm

    def body(ab_ref, prev_ref, cw_ref, cb_ref, u_ref):
        i = pl.program_id(1)
        prev = jnp.where(i % per_seq == 0, 0.0, prev_ref[...])
        ac = _conv(ab_ref[0], prev, cw_ref[...], cb_ref[...])
        u_ref[...] = (_gelu(ac) * ab_ref[1]).astype(u_ref.dtype)

    return pl.pallas_call(
        body, out_shape=jax.ShapeDtypeStruct((FFN_SLABS, T, UP_SHARD), BF16), grid=(FFN_SLABS, T // tm),
        in_specs=[pl.BlockSpec((2, None, tm, UP_SHARD), lambda d, i: (0, d, i, 0)),
                  pl.BlockSpec((None, None, 8, UP_SHARD), lambda d, i: (0, d, jnp.maximum(i * (tm // 8) - 1, 0), 0)),
                  pl.BlockSpec((None, 3, UP_SHARD), lambda d, i: (d, 0, 0)),
                  pl.BlockSpec((None, 1, UP_SHARD), lambda d, i: (d, 0, 0))],
        out_specs=pl.BlockSpec((None, tm, UP_SHARD), lambda d, i: (d, i, 0)), name="glu_fwd",
        compiler_params=_params(("parallel", "parallel")),
    )(up, up, cw, cb)


def _glu_bwd(up, d_u, cw, cb, S, tm=256):
    T = up.shape[2]
    tm = _tile(S, tm)
    per_seq = S // tm
    n_tiles = T // tm
    last8 = tm // 8

    def body(ab_ref, prev_ref, abn_ref, du_ref, dun_ref, cw_ref, cb_ref, dup_ref, dcw_ref, dcb_ref):
        i = pl.program_id(1)

        @pl.when(i == 0)
        def _():
            dcw_ref[...] = jnp.zeros_like(dcw_ref)
            dcb_ref[...] = jnp.zeros_like(dcb_ref)

        cw, cb = cw_ref[...], cb_ref[...]
        a, b = ab_ref[0], ab_ref[1]
        prev = jnp.where(i % per_seq == 0, 0.0, prev_ref[...])
        a2, a1 = _shift_down(a, prev, 2), _shift_down(a, prev, 1)
        ac = a2 * cw[0:1, :] + a1 * cw[1:2, :] + a * cw[2:3, :] + cb
        du = du_ref[...]
        dup_ref[1] = (du * _gelu(ac)).astype(dup_ref.dtype)
        dac = du * b * _gelu_grad(ac)
        dcb_ref[...] += jnp.sum(dac, axis=0, keepdims=True)
        dcw_ref[0:1, :] += jnp.sum(dac * a2, axis=0, keepdims=True)
        dcw_ref[1:2, :] += jnp.sum(dac * a1, axis=0, keepdims=True)
        dcw_ref[2:3, :] += jnp.sum(dac * a, axis=0, keepdims=True)
        an = abn_ref[0]
        acn = _conv(an, a[tm - 8:, :], cw, cb)
        dacn = jnp.where(i % per_seq == per_seq - 1, 0.0, dun_ref[...] * abn_ref[1] * _gelu_grad(acn))
        da = dac * cw[2:3, :] + _shift_up(dac, dacn, 1) * cw[1:2, :] + _shift_up(dac, dacn, 2) * cw[0:1, :]
        dup_ref[0] = da.astype(dup_ref.dtype)

    def nxt(i):
        return jnp.minimum((i + 1) * last8, T // 8 - 1)

    return pl.pallas_call(
        body,
        out_shape=(jax.ShapeDtypeStruct((2, FFN_SLABS, T, UP_SHARD), BF16),
                   jax.ShapeDtypeStruct((FFN_SLABS, 3, UP_SHARD), F32),
                   jax.ShapeDtypeStruct((FFN_SLABS, 1, UP_SHARD), F32)),
        grid=(FFN_SLABS, n_tiles),
        in_specs=[pl.BlockSpec((2, None, tm, UP_SHARD), lambda d, i: (0, d, i, 0)),
                  pl.BlockSpec((None, None, 8, UP_SHARD), lambda d, i: (0, d, jnp.maximum(i * last8 - 1, 0), 0)),
                  pl.BlockSpec((2, None, 8, UP_SHARD), lambda d, i: (0, d, nxt(i), 0)),
                  pl.BlockSpec((None, tm, UP_SHARD), lambda d, i: (d, i, 0)),
                  pl.BlockSpec((None, 8, UP_SHARD), lambda d, i: (d, nxt(i), 0)),
                  pl.BlockSpec((None, 3, UP_SHARD), lambda d, i: (d, 0, 0)),
                  pl.BlockSpec((None, 1, UP_SHARD), lambda d, i: (d, 0, 0))],
        out_specs=(pl.BlockSpec((2, None, tm, UP_SHARD), lambda d, i: (0, d, i, 0)),
                   pl.BlockSpec((None, 3, UP_SHARD), lambda d, i: (d, 0, 0)),
                   pl.BlockSpec((None, 1, UP_SHARD), lambda d, i: (d, 0, 0))),
        name="glu_bwd", compiler_params=_params(("parallel", "arbitrary")),
    )(up, up, up, d_u, d_u, cw, cb)


def _mm_up(h2, w_up, tm=512):
    T, K = h2.shape
    tm = _tile(T, tm)
    return _matmul(
        "mm_up", "nn", h2, w_up, jax.ShapeDtypeStruct((N_DEV, T, UP_SHARD), F32), (N_DEV, T // tm, 1),
        pl.BlockSpec((tm, K), lambda j, i, k: (i, 0)), pl.BlockSpec((None, K, UP_SHARD), lambda j, i, k: (j, 0, 0)),
        pl.BlockSpec((None, tm, UP_SHARD), lambda j, i, k: (j, i, 0)), (tm, UP_SHARD))


def _mm_down(u, w_down, res, tm=512):
    J, T, n = u.shape
    tm = _tile(T, tm)
    row = pl.BlockSpec((tm, D_MODEL), lambda i, d: (i, 0))
    return _matmul(
        "mm_down", "nn", u, w_down, jax.ShapeDtypeStruct((T, D_MODEL), F32), (T // tm, J),
        pl.BlockSpec((None, tm, n), lambda i, d: (d, i, 0)), pl.BlockSpec((None, n, D_MODEL), lambda i, d: (d, 0, 0)),
        row, (tm, D_MODEL), res, row)


def _mm_down_t(dx, w_down, tm=512):
    T = dx.shape[0]
    J, n, _ = w_down.shape
    tm = _tile(T, tm)
    return _matmul(
        "mm_down_t", "nt", dx, w_down, jax.ShapeDtypeStruct((J, T, n), F32), (J, T // tm, 1),
        pl.BlockSpec((tm, D_MODEL), lambda d, i, k: (i, 0)), pl.BlockSpec((None, n, D_MODEL), lambda d, i, k: (d, 0, 0)),
        pl.BlockSpec((None, tm, n), lambda d, i, k: (d, i, 0)), (tm, n))


def _mm_dw_down(u, dx, tk=512):
    J, T, n = u.shape
    tk = _tile(T, tk)
    return _matmul(
        "mm_dw_down", "tn", u, dx, jax.ShapeDtypeStruct((J, n, D_MODEL), F32), (J, T // tk),
        pl.BlockSpec((None, tk, n), lambda d, k: (d, k, 0)), pl.BlockSpec((tk, D_MODEL), lambda d, k: (k, 0)),
        pl.BlockSpec((None, n, D_MODEL), lambda d, k: (d, 0, 0)), (n, D_MODEL))


def _mm_dw_up(h2, d_up, tk=512):
    T, K = h2.shape
    tk = _tile(T, tk)
    return _matmul(
        "mm_dw_up", "tn", h2, d_up, jax.ShapeDtypeStruct((N_DEV, K, UP_SHARD), F32), (N_DEV, T // tk),
        pl.BlockSpec((tk, K), lambda j, k: (k, 0)), pl.BlockSpec((None, tk, UP_SHARD), lambda j, k: (j, k, 0)),
        pl.BlockSpec((None, K, UP_SHARD), lambda j, k: (j, 0, 0)), (K, UP_SHARD))


def _mm_up_t(d_up, w_up, tm=512):
    J, T, n = d_up.shape
    K = w_up.shape[1]
    tm = _tile(T, tm)
    return _matmul(
        "mm_up_t", "nt", d_up, w_up, jax.ShapeDtypeStruct((T, K), F32), (T // tm, J),
        pl.BlockSpec((None, tm, n), lambda i, j: (j, i, 0)), pl.BlockSpec((None, K, n), lambda i, j: (j, 0, 0)),
        pl.BlockSpec((tm, K), lambda i, j: (i, 0)), (tm, K))


def _loss_head(x2, target, g_final, tm=512):
    T, Dm = x2.shape
    tm = _tile(T, tm)

    def body(x_ref, t_ref, g_ref, dx_ref, dg_ref, loss_ref):
        @pl.when(pl.program_id(0) == 0)
        def _():
            dg_ref[...] = jnp.zeros_like(dg_ref)
            loss_ref[...] = jnp.zeros_like(loss_ref)

        xv = x_ref[...]
        r = lax.rsqrt(jnp.mean(xv * xv, axis=-1, keepdims=True) + EPS)
        xhat = xv * r
        err = xhat * g_ref[...] - t_ref[...]
        loss_ref[...] += (0.5 / Dm) * jnp.sum(err * err)
        dy = err * (1.0 / Dm)
        dg_ref[...] += jnp.sum(dy * xhat, axis=0, keepdims=True)
        dxhat = dy * g_ref[...]
        dx_ref[...] = r * (dxhat - xhat * jnp.mean(dxhat * xhat, axis=-1, keepdims=True))

    row = pl.BlockSpec((tm, Dm), lambda i: (i, 0))
    vec = pl.BlockSpec((1, Dm), lambda i: (0, 0))
    return pl.pallas_call(
        body,
        out_shape=(jax.ShapeDtypeStruct((T, Dm), F32), jax.ShapeDtypeStruct((1, Dm), F32),
                   jax.ShapeDtypeStruct((1, Dm), F32)),
        grid=(T // tm,), in_specs=[row, row, vec], out_specs=(row, vec, vec), name="loss_head",
        compiler_params=_params(("arbitrary",)),
    )(x2, target, g_final)


def _peer(k):
    x, y, c = lax.axis_index("x"), lax.axis_index("y"), lax.axis_index("c")
    kx, ky, kc = (k >> 2) & 1, (k >> 1) & 1, k & 1
    px, py, pc = x ^ kx, y ^ ky, c ^ kc
    return (px, py, pc), 4 * px + 2 * py + pc


def _my_index():
    return 4 * lax.axis_index("x") + 2 * lax.axis_index("y") + lax.axis_index("c")


def _cast_shards(shards):
    def body(*refs):
        n = len(refs) // 2
        for src, dst in zip(refs[:n], refs[n:]):
            dst[...] = src[...].astype(dst.dtype)

    return pl.pallas_call(
        body, out_shape=[jax.ShapeDtypeStruct(s.shape, BF16) for s in shards], name="cast_shards",
        compiler_params=pltpu.CompilerParams(vmem_limit_bytes=VMEM_LIMIT),
    )(*shards)


def _all_gather(shards):
    n = len(shards)

    def body(*refs):
        src, dst = refs[:n], refs[n:2 * n]
        send_sems, recv_sems, local_sems = refs[2 * n:]
        me = _my_index()
        local = [pltpu.make_async_copy(src[w], dst[w].at[me], local_sems.at[w]) for w in range(n)]
        for cp in local:
            cp.start()
        copies = []
        for k in range(1, N_DEV):
            peer, peer_idx = _peer(k)
            for w in range(n):
                s = w * (N_DEV - 1) + k - 1
                send = pltpu.make_async_remote_copy(
                    src_ref=src[w], dst_ref=dst[w].at[me], send_sem=send_sems.at[s], recv_sem=recv_sems.at[s],
                    device_id=peer, device_id_type=MESH)
                send.start()
                arrive = pltpu.make_async_remote_copy(
                    src_ref=src[w], dst_ref=dst[w].at[peer_idx], send_sem=send_sems.at[s], recv_sem=recv_sems.at[s],
                    device_id=peer, device_id_type=MESH)
                copies.append((send, arrive))
        for send, arrive in copies:
            send.wait_send()
            arrive.wait_recv()
        for cp in local:
            cp.wait()

    hbm = pl.BlockSpec(memory_space=pltpu.HBM)
    return pl.pallas_call(
        body, out_shape=[jax.ShapeDtypeStruct((N_DEV,) + s.shape, s.dtype) for s in shards],
        in_specs=[hbm] * n, out_specs=[hbm] * n,
        scratch_shapes=[pltpu.SemaphoreType.DMA((n * (N_DEV - 1),)), pltpu.SemaphoreType.DMA((n * (N_DEV - 1),)),
                        pltpu.SemaphoreType.DMA((n,))],
        name="all_gather_weights",
        compiler_params=pltpu.CompilerParams(has_side_effects=True),
    )(*shards)


def _exchange_grads(partials, packed):
    n = len(partials)

    def body(*refs):
        src, packed_ref = refs[:n], refs[n]
        dst, gathered_ref = refs[n + 1:2 * n + 1], refs[2 * n + 1]
        send_sems, recv_sems, local_sems = refs[2 * n + 2:]
        me = _my_index()
        local = [pltpu.make_async_copy(src[w].at[me], dst[w].at[me], local_sems.at[w]) for w in range(n)]
        local.append(pltpu.make_async_copy(packed_ref, gathered_ref.at[me], local_sems.at[n]))
        for cp in local:
            cp.start()
        copies = []
        for k in range(1, N_DEV):
            peer, peer_idx = _peer(k)
            for w in range(n + 1):
                s = w * (N_DEV - 1) + k - 1
                if w < n:
                    src_ref, land, slot = src[w].at[peer_idx], dst[w].at[me], dst[w].at[peer_idx]
                else:
                    src_ref, land, slot = packed_ref, gathered_ref.at[me], gathered_ref.at[peer_idx]
                send = pltpu.make_async_remote_copy(
                    src_ref=src_ref, dst_ref=land, send_sem=send_sems.at[s], recv_sem=recv_sems.at[s],
                    device_id=peer, device_id_type=MESH)
                send.start()
                arrive = pltpu.make_async_remote_copy(
                    src_ref=src_ref, dst_ref=slot, send_sem=send_sems.at[s], recv_sem=recv_sems.at[s],
                    device_id=peer, device_id_type=MESH)
                copies.append((send, arrive))
        for send, arrive in copies:
            send.wait_send()
            arrive.wait_recv()
        for cp in local:
            cp.wait()

    hbm = pl.BlockSpec(memory_space=pltpu.HBM)
    n_sem = (n + 1) * (N_DEV - 1)
    outs = pl.pallas_call(
        body,
        out_shape=[jax.ShapeDtypeStruct(p.shape, p.dtype) for p in partials]
        + [jax.ShapeDtypeStruct((N_DEV,) + packed.shape, packed.dtype)],
        in_specs=[hbm] * (n + 1), out_specs=[hbm] * (n + 1),
        scratch_shapes=[pltpu.SemaphoreType.DMA((n_sem,)), pltpu.SemaphoreType.DMA((n_sem,)),
                        pltpu.SemaphoreType.DMA((n + 1,))],
        name="exchange_grads",
        compiler_params=pltpu.CompilerParams(has_side_effects=True),
    )(*partials, packed)
    return outs[:n], outs[n]


def _adamw(w, g, m, v):
    m = ADAM_B1 * m + (1.0 - ADAM_B1) * g
    v = ADAM_B2 * v + (1.0 - ADAM_B2) * (g * g)
    m_hat = m / (1.0 - ADAM_B1 ** ADAM_STEP)
    v_hat = v / (1.0 - ADAM_B2 ** ADAM_STEP)
    delta = -ADAM_LR * (m_hat / (jnp.sqrt(v_hat) + ADAM_EPS) + ADAM_WD * w)
    return delta, m, v


def _reduce_adam(name, parts, w, m, v, tr=128):
    R, Cn = w.shape
    tr = _tile(R, tr) if R % 8 == 0 else R
    if R // tr * tr != R or (tr % 8 and tr != R):
        tr = R

    def body(p_ref, w_ref, m_ref, v_ref, g_out, d_out, m_out, v_out):
        g = p_ref[0]
        for d in range(1, N_DEV):
            g = g + p_ref[d]
        delta, m_new, v_new = _adamw(w_ref[...], g, m_ref[...], v_ref[...])
        g_out[...] = g
        d_out[...] = delta
        m_out[...] = m_new
        v_out[...] = v_new

    row = pl.BlockSpec((tr, Cn), lambda i: (i, 0))
    shape = jax.ShapeDtypeStruct((R, Cn), F32)
    return pl.pallas_call(
        body, out_shape=(shape,) * 4, grid=(R // tr,),
        in_specs=[pl.BlockSpec((N_DEV, tr, Cn), lambda i: (0, i, 0)), row, row, row], out_specs=(row,) * 4,
        name=name, compiler_params=_params(("parallel",)),
    )(parts, w, m, v)


def _adam_only(name, g, w, m, v):
    def body(g_ref, w_ref, m_ref, v_ref, d_out, m_out, v_out):
        delta, m_new, v_new = _adamw(w_ref[...], g_ref[...], m_ref[...], v_ref[...])
        d_out[...] = delta
        m_out[...] = m_new
        v_out[...] = v_new

    shape = jax.ShapeDtypeStruct(w.shape, F32)
    return pl.pallas_call(body, out_shape=(shape,) * 3, name=name)(g, w, m, v)


def _pack(arrays):
    rows = []
    for a in arrays:
        flat = a.reshape(-1).astype(F32)
        pad = (-flat.shape[0]) % PACK_LANES
        rows.append(jnp.pad(flat, (0, pad)).reshape(-1, PACK_LANES))
    n = sum(r.shape[0] for r in rows)
    rows.append(jnp.zeros(((-n) % 8, PACK_LANES), F32))
    return jnp.concatenate(rows, axis=0)


def _unpack(packed, shapes):
    out, r = [], 0
    for shp in shapes:
        size = math.prod(shp)
        nr = -(-size // PACK_LANES)
        out.append(packed[r:r + nr].reshape(-1)[:size].reshape(shp))
        r += nr
    return out


def kernel(x, mem, g_mix, w_in, w_pool, pool_scale, w_a, g_ret, b_ret, w_r, g_mem, w_mem_kv, w_c, w_out, g_ffn, w_up, conv_w, conv_b, w_down, g_final, loss_target, m_g_mix, m_w_in, m_w_pool, m_pool_scale, m_w_a, m_g_ret, m_b_ret, m_w_r, m_g_mem, m_w_mem_kv, m_w_c, m_w_out, m_g_ffn, m_w_up, m_conv_w, m_conv_b, m_w_down, m_g_final, v_g_mix, v_w_in, v_w_pool, v_pool_scale, v_w_a, v_g_ret, v_b_ret, v_w_r, v_g_mem, v_w_mem_kv, v_w_c, v_w_out, v_g_ffn, v_w_up, v_conv_w, v_conv_b, v_w_down, v_g_final):
    B, S, _ = x.shape
    M = mem.shape[1]
    T = B * S
    me = _my_index()
    x2d = x.reshape(T, D_MODEL)
    mem2d = mem.reshape(B * M, D_MODEL)
    tgt2d = loss_target.reshape(T, D_MODEL)
    g_final2 = g_final.reshape(1, D_MODEL)

    big = dict(w_in=w_in[0], w_a=w_a[0], w_r=w_r[0], w_mem_kv=w_mem_kv[0], w_c=w_c[0], w_out=w_out[0],
               w_up=w_up[0], w_down=w_down[0])
    names = list(big)
    cast = _cast_shards([big[n] for n in names])
    gathered = _all_gather(list(cast) + [conv_w[0]])
    W = dict(zip(names, gathered[:-1]))
    Win, Wa, Wc, Wup = W["w_in"], W["w_a"], W["w_c"], W["w_up"]
    Wr = W["w_r"].reshape(D_MODEL, D_MODEL)
    Wkv = W["w_mem_kv"].reshape(D_MODEL, D_MODEL)
    Wout = W["w_out"].reshape(D_MODEL, D_MODEL)
    Wdown = W["w_down"].reshape(FFN_SLABS, UP_SHARD, D_MODEL)
    cw_full = gathered[-1].transpose(1, 0, 2).reshape(3, FFN_HIDDEN)
    cw = cw_full.reshape(3, FFN_SLABS, UP_SHARD).transpose(1, 0, 2)
    cb = conv_b[0].reshape(FFN_SLABS, 1, UP_SHARD)
    wp = w_pool[0]
    tables = _ret_tables(S)

    h = _rms_fwd("rms_mix", x2d, g_mix)
    proj = _mm_cols_slab("mm_in", h, Win)
    ypre = _pool_fwd(proj, wp, pool_scale, B, S)
    y_pool = _mm_cols_slab("mm_a", ypre, Wa)
    yr = _ret_fwd(proj, g_ret, b_ret, tables, B, S)
    y_ret = _mm_rows("mm_r", yr, Wr)
    mem_n = _rms_fwd("rms_mem", mem2d, g_mem)
    kv = _mm_rows("mm_kv", mem_n, Wkv)
    o_mem = _xa_fwd(proj, kv, B, S, M)
    y_mem = _mm_cols_slab("mm_c", o_mem, Wc)
    ys = (y_pool, y_ret, y_mem)
    merged = _merge_fwd(proj, ys)
    x1 = _mm_rows("mm_out", merged, Wout, res=x2d)
    h2 = _rms_fwd("rms_ffn", x1, g_ffn)
    up = _mm_up(h2, Wup).reshape(2, FFN_SLABS, T, UP_SHARD)
    u = _glu_fwd(up, cw, cb, S)
    x2 = _mm_down(u, Wdown, x1)

    dx2, dg_final, loss_part = _loss_head(x2, tgt2d, g_final2)
    d_u = _mm_down_t(dx2, Wdown)
    dW_down = _mm_dw_down(u, dx2)
    d_up, d_cw, d_cb = _glu_bwd(up, d_u, cw, cb, S)
    d_up = d_up.reshape(N_DEV, T, UP_SHARD)
    dW_up = _mm_dw_up(h2, d_up)
    d_h2 = _mm_up_t(d_up, Wup)
    dx1, dg_ffn = _rms_bwd("rms_ffn_bwd", x1, g_ffn, d_h2, dx2)
    d_merged = _mm_rows("mm_out_t", dx1, Wout, kind="nt")
    dW_out = _mm_tn("mm_dw_out", merged, dx1)
    d_gl, d_y_pool, d_y_ret, d_y_mem = _merge_bwd(proj, ys, d_merged)
    dW_a = _mm_tn_slab("mm_dw_a", ypre, d_y_pool, D_MODEL // N_DEV)
    d_ypre = _mm_cols_slab_t("mm_a_t", d_y_pool, Wa)
    d_hp, dw_pool, d_scale = _pool_bwd(proj, d_ypre, wp, pool_scale, B, S)
    dW_r = _mm_tn("mm_dw_r", yr, d_y_ret)
    d_yr = _mm_rows("mm_r_t", d_y_ret, Wr, kind="nt")
    d_q, d_k, d_v, d_gr, dg_ret, db_ret = _ret_bwd(proj, d_yr, g_ret, b_ret, tables, B, S)
    dW_c = _mm_tn_slab("mm_dw_c", o_mem, d_y_mem, D_MODEL // N_DEV)
    d_o_mem = _mm_cols_slab_t("mm_c_t", d_y_mem, Wc)
    d_qx, d_kmem, d_vmem = _xa_bwd(proj, kv, d_o_mem, B, S, M)
    d_kv = jnp.concatenate([d_kmem, d_vmem], axis=1)
    dW_kv = _mm_tn("mm_dw_kv", mem_n, d_kv)
    d_mem_n = _mm_rows("mm_kv_t", d_kv, Wkv, kind="nt")
    dg_mem = _rms_bwd("rms_mem_bwd", mem2d, g_mem, d_mem_n, None)
    d_proj = jnp.concatenate([d_hp, d_q, d_k, d_v, d_gr, d_qx, d_gl], axis=1)
    dW_in = _mm_tn_slab("mm_dw_in", h, d_proj, IN_SHARD)
    d_h = _mm_cols_slab_t("mm_in_t", d_proj, Win)
    grad_x, dg_mix = _rms_bwd("rms_mix_bwd", x2d, g_mix, d_h, dx1)

    partials = dict(
        w_in=dW_in, w_a=dW_a, w_r=dW_r.reshape(N_DEV, -1, D_MODEL), w_mem_kv=dW_kv.reshape(N_DEV, -1, D_MODEL),
        w_c=dW_c, w_out=dW_out.reshape(N_DEV, -1, D_MODEL), w_up=dW_up, w_down=dW_down.reshape(N_DEV, -1, D_MODEL))
    d_cw_full = d_cw.transpose(1, 0, 2).reshape(3, FFN_HIDDEN)
    small_names = ["g_mix", "w_pool", "pool_scale", "g_ret", "b_ret", "g_mem", "g_ffn", "conv_b", "g_final"]
    small_grads = [dg_mix, dw_pool, d_scale, dg_ret, db_ret, dg_mem, dg_ffn, d_cb.reshape(1, FFN_HIDDEN), dg_final]
    packed = _pack(small_grads + [d_cw_full, loss_part])
    received, packed_all = _exchange_grads([partials[n] for n in names], packed)
    received = dict(zip(names, received))

    args = dict(g_mix=g_mix, w_in=w_in, w_pool=w_pool, pool_scale=pool_scale, w_a=w_a, g_ret=g_ret, b_ret=b_ret,
                w_r=w_r, g_mem=g_mem, w_mem_kv=w_mem_kv, w_c=w_c, w_out=w_out, g_ffn=g_ffn, w_up=w_up,
                conv_w=conv_w, conv_b=conv_b, w_down=w_down, g_final=g_final)
    m_in = dict(g_mix=m_g_mix, w_in=m_w_in, w_pool=m_w_pool, pool_scale=m_pool_scale, w_a=m_w_a, g_ret=m_g_ret,
                b_ret=m_b_ret, w_r=m_w_r, g_mem=m_g_mem, w_mem_kv=m_w_mem_kv, w_c=m_w_c, w_out=m_w_out,
                g_ffn=m_g_ffn, w_up=m_w_up, conv_w=m_conv_w, conv_b=m_conv_b, w_down=m_w_down, g_final=m_g_final)
    v_in = dict(g_mix=v_g_mix, w_in=v_w_in, w_pool=v_w_pool, pool_scale=v_pool_scale, w_a=v_w_a, g_ret=v_g_ret,
                b_ret=v_b_ret, w_r=v_w_r, g_mem=v_g_mem, w_mem_kv=v_w_mem_kv, w_c=v_w_c, w_out=v_w_out,
                g_ffn=v_g_ffn, w_up=v_w_up, conv_w=v_conv_w, conv_b=v_conv_b, w_down=v_w_down, g_final=v_g_final)

    grads, deltas, new_m, new_v = {}, {}, {}, {}
    for n in names:
        shard = big[n].shape
        outs = _reduce_adam("adam_" + n, received[n], big[n], m_in[n][0], v_in[n][0])
        for store, val in zip((grads, deltas, new_m, new_v), outs):
            store[n] = val.reshape((1,) + shard)

    zeros_tail = [jnp.zeros((3, FFN_HIDDEN), F32), jnp.zeros((1, PACK_LANES), F32)]
    w_small = _pack([args[n] for n in small_names] + zeros_tail)
    m_small = _pack([m_in[n] for n in small_names] + zeros_tail)
    v_small = _pack([v_in[n] for n in small_names] + zeros_tail)
    small_out = _reduce_adam("adam_small", packed_all, w_small, m_small, v_small, tr=w_small.shape[0])
    small_shapes = [args[n].shape for n in small_names] + [(3, FFN_HIDDEN), (1, PACK_LANES)]
    for store, val in zip((grads, deltas, new_m, new_v), small_out):
        parts = _unpack(val, small_shapes)
        for n, p in zip(small_names, parts):
            store[n] = p
        if store is grads:
            g_cw_full, loss_row = parts[-2], parts[-1]
    loss = loss_row[0, 0]

    shard_cols = FFN_HIDDEN // N_DEV
    g_cw = lax.dynamic_slice_in_dim(g_cw_full, me * shard_cols, shard_cols, axis=1)
    d_, m_, v_ = _adam_only("adam_conv_w", g_cw, conv_w[0], m_conv_w[0], v_conv_w[0])
    grads["conv_w"], deltas["conv_w"], new_m["conv_w"], new_v["conv_w"] = g_cw[None], d_[None], m_[None], v_[None]

    order = ["g_mix", "w_in", "w_pool", "pool_scale", "w_a", "g_ret", "b_ret", "w_r", "g_mem", "w_mem_kv", "w_c",
             "w_out", "g_ffn", "w_up", "conv_w", "conv_b", "w_down", "g_final"]
    return (loss, grad_x.reshape(B, S, D_MODEL), *[grads[n] for n in order], *[deltas[n] for n in order],
            *[new_m[n] for n in order], *[new_v[n] for n in order])
```

```python
import functools
import math

import jax
import jax.numpy as jnp
from jax import lax
from jax.experimental import pallas as pl
from jax.experimental.pallas import tpu as pltpu

F32 = jnp.float32
BF16 = jnp.bfloat16

N_DEV = 8
D_MODEL = 1024
POOL_WINDOWS = (2, 4, 8, 16)
POOL_GROUP_DIM = 128
POOL_WIDTH = 512
POOL_HALO = 16
RET_HEADS = 4
RET_QK_DIM = 128
RET_V_DIM = 256
RET_CHUNK = 128
ROPE_BASE = 10000.0
XA_HEADS = 4
XA_HEAD_DIM = 128
XA_WIDTH = 512
IN_WIDTH = 7168
IN_SHARD = IN_WIDTH // N_DEV
FFN_HIDDEN = 2816
UP_SHARD = 2 * FFN_HIDDEN // N_DEV
FFN_SLABS = FFN_HIDDEN // UP_SHARD
EPS = 1e-6
ADAM_LR = 0.001
ADAM_B1 = 0.9
ADAM_B2 = 0.999
ADAM_EPS = 1e-08
ADAM_WD = 0.01
ADAM_STEP = 10
GELU_C = math.sqrt(2.0 / math.pi)
GELU_A = 0.044715
VMEM_LIMIT = 56 * 1024 * 1024
PACK_LANES = 1024
MESH = pl.DeviceIdType.MESH

COL_Q, COL_K, COL_V, COL_GR, COL_QX, COL_GL = 512, 1024, 1536, 2560, 3584, 4096

_DIMS = {
    "nn": (((1,), (0,)), ((), ())),
    "nt": (((1,), (1,)), ((), ())),
    "tn": (((0,), (0,)), ((), ())),
}


def _dot(a, b, kind="nn"):
    return lax.dot_general(a.astype(BF16), b.astype(BF16), _DIMS[kind], preferred_element_type=F32)


def _params(sem, vmem=VMEM_LIMIT):
    return pltpu.CompilerParams(dimension_semantics=sem, vmem_limit_bytes=vmem)


def _tile(n, pref):
    t = min(n, pref)
    while n % t:
        t //= 2
    return t


def _matmul(name, kind, a, b, out_shape, grid, a_spec, b_spec, o_spec, acc_shape, res=None, res_spec=None):
    nk = grid[-1]
    has_res = res is not None

    def body(*refs):
        a_ref, b_ref = refs[0], refs[1]
        o_ref = refs[2 + has_res]

        def prod():
            return _dot(a_ref[...], b_ref[...], kind)

        def finish(acc):
            if has_res:
                acc = acc + refs[2][...]
            o_ref[...] = acc.astype(o_ref.dtype)

        if nk == 1:
            finish(prod())
        else:
            acc_ref = refs[3 + has_res]
            k = pl.program_id(len(grid) - 1)

            @pl.when(k == 0)
            def _():
                acc_ref[...] = prod()

            @pl.when(k > 0)
            def _():
                acc_ref[...] += prod()

            @pl.when(k == nk - 1)
            def _():
                finish(acc_ref[...])

    in_specs = [a_spec, b_spec] + ([res_spec] if has_res else [])
    args = (a, b) + ((res,) if has_res else ())
    scratch = [pltpu.VMEM(acc_shape, F32)] if nk > 1 else []
    sem = ("parallel",) * (len(grid) - 1) + ("arbitrary",)
    return pl.pallas_call(
        body, out_shape=out_shape, grid=grid, in_specs=in_specs, out_specs=o_spec,
        scratch_shapes=scratch, name=name, compiler_params=_params(sem),
    )(*args)


def _mm_rows(name, a, w, out_dtype=F32, res=None, kind="nn", tm=512):
    M, K = a.shape
    N = w.shape[1] if kind == "nn" else w.shape[0]
    tm = _tile(M, tm)
    res_spec = pl.BlockSpec((tm, N), lambda i, k: (i, 0)) if res is not None else None
    return _matmul(
        name, kind, a, w, jax.ShapeDtypeStruct((M, N), out_dtype), (M // tm, 1),
        pl.BlockSpec((tm, K), lambda i, k: (i, 0)), pl.BlockSpec(w.shape, lambda i, k: (0, 0)),
        pl.BlockSpec((tm, N), lambda i, k: (i, 0)), (tm, N), res, res_spec)


def _mm_tn(name, a, b, out_dtype=F32, tk=512):
    T, M = a.shape
    N = b.shape[1]
    tk = _tile(T, tk)
    return _matmul(
        name, "tn", a, b, jax.ShapeDtypeStruct((M, N), out_dtype), (1, T // tk),
        pl.BlockSpec((tk, M), lambda i, k: (k, 0)), pl.BlockSpec((tk, N), lambda i, k: (k, 0)),
        pl.BlockSpec((M, N), lambda i, k: (0, 0)), (M, N))


def _mm_cols_slab(name, a, w_slabs, out_dtype=F32, tm=512):
    M, K = a.shape
    J, _, n = w_slabs.shape
    tm = _tile(M, tm)
    return _matmul(
        name, "nn", a, w_slabs, jax.ShapeDtypeStruct((M, J * n), out_dtype), (J, M // tm, 1),
        pl.BlockSpec((tm, K), lambda j, i, k: (i, 0)), pl.BlockSpec((None, K, n), lambda j, i, k: (j, 0, 0)),
        pl.BlockSpec((tm, n), lambda j, i, k: (i, j)), (tm, n))


def _mm_cols_slab_t(name, a, w_slabs, out_dtype=F32, tm=512):
    M = a.shape[0]
    J, K, n = w_slabs.shape
    tm = _tile(M, tm)
    return _matmul(
        name, "nt", a, w_slabs, jax.ShapeDtypeStruct((M, K), out_dtype), (M // tm, J),
        pl.BlockSpec((tm, n), lambda i, j: (i, j)), pl.BlockSpec((None, K, n), lambda i, j: (j, 0, 0)),
        pl.BlockSpec((tm, K), lambda i, j: (i, 0)), (tm, K))


def _mm_tn_slab(name, a, b, n, out_dtype=F32, tk=512):
    T, M = a.shape
    J = b.shape[1] // n
    tk = _tile(T, tk)
    return _matmul(
        name, "tn", a, b, jax.ShapeDtypeStruct((J, M, n), out_dtype), (J, T // tk),
        pl.BlockSpec((tk, M), lambda j, k: (k, 0)), pl.BlockSpec((tk, n), lambda j, k: (k, j)),
        pl.BlockSpec((None, M, n), lambda j, k: (j, 0, 0)), (M, n))


def _rms_fwd(name, x, g, tm=512):
    T, Dm = x.shape
    tm = _tile(T, tm)

    def body(x_ref, g_ref, h_ref):
        xv = x_ref[...]
        r = lax.rsqrt(jnp.mean(xv * xv, axis=-1, keepdims=True) + EPS)
        h_ref[...] = (xv * r * g_ref[...]).astype(h_ref.dtype)

    return pl.pallas_call(
        body, out_shape=jax.ShapeDtypeStruct((T, Dm), BF16), grid=(T // tm,),
        in_specs=[pl.BlockSpec((tm, Dm), lambda i: (i, 0)), pl.BlockSpec((1, Dm), lambda i: (0, 0))],
        out_specs=pl.BlockSpec((tm, Dm), lambda i: (i, 0)), name=name, compiler_params=_params(("parallel",)),
    )(x, g)


def _rms_bwd(name, x, g, dh, dres, tm=512):
    T, Dm = x.shape
    tm = _tile(T, tm)
    want_dx = dres is not None

    def body(*refs):
        if want_dx:
            x_ref, g_ref, dh_ref, dres_ref, dx_ref, dg_ref = refs
        else:
            x_ref, g_ref, dh_ref, dg_ref = refs
        xv = x_ref[...]
        r = lax.rsqrt(jnp.mean(xv * xv, axis=-1, keepdims=True) + EPS)
        xhat = xv * r
        dhv = dh_ref[...]

        @pl.when(pl.program_id(0) == 0)
        def _():
            dg_ref[...] = jnp.zeros_like(dg_ref)

        dg_ref[...] += jnp.sum(dhv * xhat, axis=0, keepdims=True)
        if want_dx:
            dxhat = dhv * g_ref[...]
            dx_ref[...] = dres_ref[...] + r * (dxhat - xhat * jnp.mean(dxhat * xhat, axis=-1, keepdims=True))

    row = pl.BlockSpec((tm, Dm), lambda i: (i, 0))
    vec = pl.BlockSpec((1, Dm), lambda i: (0, 0))
    if want_dx:
        return pl.pallas_call(
            body, out_shape=(jax.ShapeDtypeStruct((T, Dm), F32), jax.ShapeDtypeStruct((1, Dm), F32)),
            grid=(T // tm,), in_specs=[row, vec, row, row], out_specs=(row, vec), name=name,
            compiler_params=_params(("arbitrary",)),
        )(x, g, dh, dres)
    return pl.pallas_call(
        body, out_shape=jax.ShapeDtypeStruct((1, Dm), F32), grid=(T // tm,), in_specs=[row, vec, row],
        out_specs=vec, name=name, compiler_params=_params(("arbitrary",)),
    )(x, g, dh)


def _pool_rows(S):
    return _tile(S, 256)


def _pool_count(c0, rows, w):
    t = c0 + lax.broadcasted_iota(jnp.int32, (rows, 1), 0)
    return jnp.minimum(t + 1, w).astype(F32)


def _pool_fwd(proj, w_pool, scale, B, S):
    CH = _pool_rows(S)

    def body(hp_ref, wp_ref, sc_ref, o_ref, pad_ref):
        pad_ref[0:POOL_HALO, :] = jnp.zeros((POOL_HALO, POOL_WIDTH), F32)
        pad_ref[POOL_HALO:, :] = hp_ref[...]
        for gi, w in enumerate(POOL_WINDOWS):
            cols = slice(gi * POOL_GROUP_DIM, (gi + 1) * POOL_GROUP_DIM)
            for c in range(S // CH):
                base = POOL_HALO + c * CH
                acc = pad_ref[base:base + CH, cols]
                tok = acc
                for j in range(1, w):
                    acc = acc + pad_ref[base - j:base - j + CH, cols]
                pooled = acc / _pool_count(c * CH, CH, w) - tok
                z = _dot(pooled, wp_ref[gi])
                o_ref[c * CH:(c + 1) * CH, cols] = (z * sc_ref[:, cols]).astype(o_ref.dtype)

    return pl.pallas_call(
        body, out_shape=jax.ShapeDtypeStruct((B * S, POOL_WIDTH), BF16), grid=(B,),
        in_specs=[pl.BlockSpec((S, POOL_WIDTH), lambda b: (b, 0)),
                  pl.BlockSpec(w_pool.shape, lambda b: (0, 0, 0)),
                  pl.BlockSpec((1, POOL_WIDTH), lambda b: (0, 0))],
        out_specs=pl.BlockSpec((S, POOL_WIDTH), lambda b: (b, 0)),
        scratch_shapes=[pltpu.VMEM((S + POOL_HALO, POOL_WIDTH), F32)],
        name="pool_fwd", compiler_params=_params(("parallel",)),
    )(proj, w_pool, scale)


def _pool_bwd(proj, d_ypre, w_pool, scale, B, S):
    CH = _pool_rows(S)

    def body(hp_ref, dy_ref, wp_ref, sc_ref, dhp_ref, dwp_ref, dsc_ref, pad_ref, sc_pad_ref, dp_ref):
        @pl.when(pl.program_id(0) == 0)
        def _():
            dwp_ref[...] = jnp.zeros_like(dwp_ref)
            dsc_ref[...] = jnp.zeros_like(dsc_ref)

        pad_ref[0:POOL_HALO, :] = jnp.zeros((POOL_HALO, POOL_WIDTH), F32)
        pad_ref[POOL_HALO:, :] = hp_ref[...]
        sc_pad_ref[S:, :] = jnp.zeros((POOL_HALO, POOL_WIDTH), F32)
        for gi, w in enumerate(POOL_WINDOWS):
            cols = slice(gi * POOL_GROUP_DIM, (gi + 1) * POOL_GROUP_DIM)
            for c in range(S // CH):
                base = POOL_HALO + c * CH
                rows = slice(c * CH, (c + 1) * CH)
                acc = pad_ref[base:base + CH, cols]
                tok = acc
                for j in range(1, w):
                    acc = acc + pad_ref[base - j:base - j + CH, cols]
                cnt = _pool_count(c * CH, CH, w)
                pooled = acc / cnt - tok
                z = _dot(pooled, wp_ref[gi])
                dy = dy_ref[rows, cols]
                dsc_ref[:, cols] += jnp.sum(dy * z, axis=0, keepdims=True)
                dz = dy * sc_ref[:, cols]
                dwp_ref[gi] += _dot(pooled, dz, "tn")
                dpool = _dot(dz, wp_ref[gi], "nt")
                dp_ref[rows, cols] = dpool
                sc_pad_ref[rows, cols] = dpool / cnt
            for c in range(S // CH):
                rows = slice(c * CH, (c + 1) * CH)
                acc = sc_pad_ref[rows, cols]
                for j in range(1, w):
                    acc = acc + sc_pad_ref[c * CH + j:c * CH + j + CH, cols]
                dhp_ref[rows, cols] = (acc - dp_ref[rows, cols]).astype(dhp_ref.dtype)

    seq = pl.BlockSpec((S, POOL_WIDTH), lambda b: (b, 0))
    return pl.pallas_call(
        body,
        out_shape=(jax.ShapeDtypeStruct((B * S, POOL_WIDTH), BF16),
                   jax.ShapeDtypeStruct(w_pool.shape, F32), jax.ShapeDtypeStruct((1, POOL_WIDTH), F32)),
        grid=(B,),
        in_specs=[seq, seq, pl.BlockSpec(w_pool.shape, lambda b: (0, 0, 0)),
                  pl.BlockSpec((1, POOL_WIDTH), lambda b: (0, 0))],
        out_specs=(seq, pl.BlockSpec(w_pool.shape, lambda b: (0, 0, 0)),
                   pl.BlockSpec((1, POOL_WIDTH), lambda b: (0, 0))),
        scratch_shapes=[pltpu.VMEM((S + POOL_HALO, POOL_WIDTH), F32),
                        pltpu.VMEM((S + POOL_HALO, POOL_WIDTH), F32),
                        pltpu.VMEM((S, POOL_WIDTH), F32)],
        name="pool_bwd", compiler_params=_params(("arbitrary",)),
    )(proj, d_ypre, w_pool, scale)


def _ret_tables(S):
    half = RET_QK_DIM // 2
    inv = ROPE_BASE ** (-jnp.arange(half, dtype=F32) / half)
    ang = jnp.arange(S, dtype=F32)[:, None] * inv[None, :]
    cos, sin = jnp.cos(ang), jnp.sin(ang)
    cos_full = jnp.concatenate([cos, cos], axis=-1)
    sin_signed = jnp.concatenate([-sin, sin], axis=-1)
    C = RET_CHUNK
    lg = jnp.log1p(-jnp.exp2(-5.0 - jnp.arange(RET_HEADS, dtype=F32)))[:, None, None]
    idx = jnp.arange(C, dtype=F32)
    rel = idx[:, None] - idx[None, :]
    decay = jnp.where(rel >= 0, jnp.exp(jnp.maximum(rel, 0.0) * lg), 0.0)
    q_decay = jnp.broadcast_to(jnp.exp((idx + 1.0)[None, :, None] * lg), (RET_HEADS, C, RET_QK_DIM))
    k_decay = jnp.broadcast_to(jnp.exp((C - 1.0 - idx)[None, :, None] * lg), (RET_HEADS, C, RET_QK_DIM))
    c_decay = jnp.broadcast_to(jnp.exp(C * lg), (RET_HEADS, 1, RET_V_DIM))
    return cos_full, sin_signed, decay, q_decay, k_decay, c_decay


def _rope(x, cos_full, sin_signed):
    return x * cos_full + pltpu.roll(x, RET_QK_DIM // 2, axis=1) * sin_signed


def _rope_t(dy, cos_full, sin_signed):
    return dy * cos_full + pltpu.roll(dy * sin_signed, RET_QK_DIM // 2, axis=1)


def _ret_specs(S, bh):
    def at(col_of_head, width):
        def index(*ids):
            b, h = bh(*ids)
            return (b, col_of_head + h)
        return pl.BlockSpec((S, width), index)

    def per_head(shape):
        def index(*ids):
            _, h = bh(*ids)
            return (h,) + (0,) * len(shape)
        return pl.BlockSpec((None,) + shape, index)

    def head_vec(width):
        def index(*ids):
            _, h = bh(*ids)
            return (0, h)
        return pl.BlockSpec((1, width), index)

    table = pl.BlockSpec((S, RET_QK_DIM), lambda *ids: (0, 0))
    C = RET_CHUNK
    return dict(
        q=at(COL_Q // RET_QK_DIM, RET_QK_DIM), k=at(COL_K // RET_QK_DIM, RET_QK_DIM),
        v=at(COL_V // RET_V_DIM, RET_V_DIM), gr=at(COL_GR // RET_V_DIM, RET_V_DIM),
        table=table, decay=per_head((C, C)), qd=per_head((C, RET_QK_DIM)), kd=per_head((C, RET_QK_DIM)),
        cd=per_head((1, RET_V_DIM)), vec=head_vec(RET_V_DIM), out_qk=at(0, RET_QK_DIM), out_v=at(0, RET_V_DIM))


def _group_norm(o):
    mu = jnp.mean(o, axis=-1, keepdims=True)
    oc = o - mu
    rstd = lax.rsqrt(jnp.mean(oc * oc, axis=-1, keepdims=True) + EPS)
    return oc * rstd, rstd


def _ret_fwd(proj, g_ret, b_ret, tables, B, S):
    C = RET_CHUNK
    cos_t, sin_t, decay, q_decay, k_decay, c_decay = tables
    sp = _ret_specs(S, lambda b, h: (b, h))

    def body(q_ref, k_ref, v_ref, gr_ref, cos_ref, sin_ref, dec_ref, qd_ref, kd_ref, cd_ref, g_ref, b_ref,
             y_ref, r_ref):
        r_ref[...] = jnp.zeros_like(r_ref)

        def chunk(i, carry):
            rows = pl.ds(pl.multiple_of(i * C, C), C)
            cs, sn = cos_ref[rows, :], sin_ref[rows, :]
            q = _rope(q_ref[rows, :], cs, sn)
            k = _rope(k_ref[rows, :], cs, sn) * (RET_QK_DIM ** -0.5)
            v = v_ref[rows, :]
            R = r_ref[...]
            s = _dot(q, k, "nt") * dec_ref[...]
            o = _dot(s, v) + _dot(q * qd_ref[...], R)
            r_ref[...] = cd_ref[...] * R + _dot(k * kd_ref[...], v, "tn")
            on, _ = _group_norm(o)
            gr = gr_ref[rows, :]
            y_ref[rows, :] = (gr * jax.nn.sigmoid(gr) * (on * g_ref[...] + b_ref[...])).astype(y_ref.dtype)
            return carry

        lax.fori_loop(0, S // C, chunk, 0)

    return pl.pallas_call(
        body, out_shape=jax.ShapeDtypeStruct((B * S, RET_HEADS * RET_V_DIM), BF16), grid=(B, RET_HEADS),
        in_specs=[sp["q"], sp["k"], sp["v"], sp["gr"], sp["table"], sp["table"], sp["decay"], sp["qd"],
                  sp["kd"], sp["cd"], sp["vec"], sp["vec"]],
        out_specs=sp["out_v"], scratch_shapes=[pltpu.VMEM((RET_QK_DIM, RET_V_DIM), F32)],
        name="ret_fwd", compiler_params=_params(("parallel", "parallel")),
    )(proj, proj, proj, proj, cos_t, sin_t, decay, q_decay, k_decay, c_decay, g_ret, b_ret)


def _ret_bwd(proj, d_yr, g_ret, b_ret, tables, B, S):
    C = RET_CHUNK
    N = S // C
    cos_t, sin_t, decay, q_decay, k_decay, c_decay = tables
    sp = _ret_specs(S, lambda h, b: (b, h))
    qk_scale = RET_QK_DIM ** -0.5

    def body(q_ref, k_ref, v_ref, gr_ref, dy_ref, cos_ref, sin_ref, dec_ref, qd_ref, kd_ref, cd_ref, g_ref, b_ref,
             dq_ref, dk_ref, dv_ref, dgr_ref, dg_ref, db_ref, qr_ref, kr_ref, rs_ref, dr_ref):
        @pl.when(pl.program_id(1) == 0)
        def _():
            dg_ref[...] = jnp.zeros_like(dg_ref)
            db_ref[...] = jnp.zeros_like(db_ref)

        dr_ref[...] = jnp.zeros_like(dr_ref)

        def sweep(i, R):
            rows = pl.ds(pl.multiple_of(i * C, C), C)
            cs, sn = cos_ref[rows, :], sin_ref[rows, :]
            q = _rope(q_ref[rows, :], cs, sn)
            k = _rope(k_ref[rows, :], cs, sn) * qk_scale
            qr_ref[rows, :] = q
            kr_ref[rows, :] = k
            rs_ref[i] = R
            return cd_ref[...] * R + _dot(k * kd_ref[...], v_ref[rows, :], "tn")

        lax.fori_loop(0, N, sweep, jnp.zeros((RET_QK_DIM, RET_V_DIM), F32))

        def back(step, carry):
            i = N - 1 - step
            rows = pl.ds(pl.multiple_of(i * C, C), C)
            q, k, v = qr_ref[rows, :], kr_ref[rows, :], v_ref[rows, :]
            R, dR = rs_ref[i], dr_ref[...]
            dec, qd, kd = dec_ref[...], qd_ref[...], kd_ref[...]
            s = _dot(q, k, "nt") * dec
            o = _dot(s, v) + _dot(q * qd, R)
            on, rstd = _group_norm(o)
            oaff = on * g_ref[...] + b_ref[...]
            gr = gr_ref[rows, :]
            sg = jax.nn.sigmoid(gr)
            dy = dy_ref[rows, :]
            dgr_ref[rows, :] = (dy * oaff * (sg * (1.0 + gr * (1.0 - sg)))).astype(dgr_ref.dtype)
            doaff = dy * (gr * sg)
            dg_ref[...] += jnp.sum(doaff * on, axis=0, keepdims=True)
            db_ref[...] += jnp.sum(doaff, axis=0, keepdims=True)
            don = doaff * g_ref[...]
            do = rstd * (don - jnp.mean(don, axis=-1, keepdims=True)
                         - on * jnp.mean(don * on, axis=-1, keepdims=True))
            ds = _dot(do, v, "nt") * dec
            dq = _dot(ds, k) + qd * _dot(do, R, "nt")
            dk = _dot(ds, q, "tn") + kd * _dot(v, dR, "nt")
            dv_ref[rows, :] = (_dot(s, do, "tn") + _dot(k * kd, dR)).astype(dv_ref.dtype)
            dr_ref[...] = cd_ref[...] * dR + _dot(q * qd, do, "tn")
            cs, sn = cos_ref[rows, :], sin_ref[rows, :]
            dq_ref[rows, :] = _rope_t(dq, cs, sn).astype(dq_ref.dtype)
            dk_ref[rows, :] = _rope_t(dk * qk_scale, cs, sn).astype(dk_ref.dtype)
            return carry

        lax.fori_loop(0, N, back, 0)

    T = B * S
    qk_shape = jax.ShapeDtypeStruct((T, RET_HEADS * RET_QK_DIM), BF16)
    v_shape = jax.ShapeDtypeStruct((T, RET_HEADS * RET_V_DIM), BF16)
    vec_shape = jax.ShapeDtypeStruct((1, RET_HEADS * RET_V_DIM), F32)
    return pl.pallas_call(
        body, out_shape=(qk_shape, qk_shape, v_shape, v_shape, vec_shape, vec_shape), grid=(RET_HEADS, B),
        in_specs=[sp["q"], sp["k"], sp["v"], sp["gr"], sp["out_v"], sp["table"], sp["table"], sp["decay"],
                  sp["qd"], sp["kd"], sp["cd"], sp["vec"], sp["vec"]],
        out_specs=(sp["out_qk"], sp["out_qk"], sp["out_v"], sp["out_v"], sp["vec"], sp["vec"]),
        scratch_shapes=[pltpu.VMEM((S, RET_QK_DIM), F32), pltpu.VMEM((S, RET_QK_DIM), F32),
                        pltpu.VMEM((N, RET_QK_DIM, RET_V_DIM), F32), pltpu.VMEM((RET_QK_DIM, RET_V_DIM), F32)],
        name="ret_bwd", compiler_params=_params(("parallel", "arbitrary")),
    )(proj, proj, proj, proj, d_yr, cos_t, sin_t, decay, q_decay, k_decay, c_decay, g_ret, b_ret)


def _xa_rows(S):
    return _tile(S, 256)


def _xa_specs(S, M):
    q = pl.BlockSpec((S, XA_HEAD_DIM), lambda b, h: (b, COL_QX // XA_HEAD_DIM + h))
    k = pl.BlockSpec((M, XA_HEAD_DIM), lambda b, h: (b, h))
    v = pl.BlockSpec((M, XA_HEAD_DIM), lambda b, h: (b, XA_HEADS + h))
    o = pl.BlockSpec((S, XA_HEAD_DIM), lambda b, h: (b, h))
    return q, k, v, o


def _softmax_rows(s):
    e = jnp.exp(s - jnp.max(s, axis=-1, keepdims=True))
    return e / jnp.sum(e, axis=-1, keepdims=True)


def _xa_fwd(proj, kv, B, S, M):
    CH = _xa_rows(S)
    q_spec, k_spec, v_spec, o_spec = _xa_specs(S, M)

    def body(q_ref, k_ref, v_ref, o_ref):
        def chunk(i, carry):
            rows = pl.ds(pl.multiple_of(i * CH, CH), CH)
            p = _softmax_rows(_dot(q_ref[rows, :], k_ref[...], "nt") * (XA_HEAD_DIM ** -0.5))
            o_ref[rows, :] = _dot(p, v_ref[...]).astype(o_ref.dtype)
            return carry

        lax.fori_loop(0, S // CH, chunk, 0)

    return pl.pallas_call(
        body, out_shape=jax.ShapeDtypeStruct((B * S, XA_WIDTH), BF16), grid=(B, XA_HEADS),
        in_specs=[q_spec, k_spec, v_spec], out_specs=o_spec, name="xattn_fwd",
        compiler_params=_params(("parallel", "parallel")),
    )(proj, kv, kv)


def _xa_bwd(proj, kv, d_o, B, S, M):
    CH = _xa_rows(S)
    q_spec, k_spec, v_spec, o_spec = _xa_specs(S, M)
    scale = XA_HEAD_DIM ** -0.5

    def body(q_ref, k_ref, v_ref, do_ref, dq_ref, dk_ref, dv_ref):
        dk_ref[...] = jnp.zeros_like(dk_ref)
        dv_ref[...] = jnp.zeros_like(dv_ref)

        def chunk(i, carry):
            rows = pl.ds(pl.multiple_of(i * CH, CH), CH)
            q, do = q_ref[rows, :], do_ref[rows, :]
            p = _softmax_rows(_dot(q, k_ref[...], "nt") * scale)
            dp = _dot(do, v_ref[...], "nt")
            ds = p * (dp - jnp.sum(dp * p, axis=-1, keepdims=True)) * scale
            dq_ref[rows, :] = _dot(ds, k_ref[...]).astype(dq_ref.dtype)
            dk_ref[...] += _dot(ds, q, "tn")
            dv_ref[...] += _dot(p, do, "tn")
            return carry

        lax.fori_loop(0, S // CH, chunk, 0)

    kv_out = pl.BlockSpec((M, XA_HEAD_DIM), lambda b, h: (b, h))
    return pl.pallas_call(
        body,
        out_shape=(jax.ShapeDtypeStruct((B * S, XA_WIDTH), BF16), jax.ShapeDtypeStruct((B * M, XA_WIDTH), F32),
                   jax.ShapeDtypeStruct((B * M, XA_WIDTH), F32)),
        grid=(B, XA_HEADS), in_specs=[q_spec, k_spec, v_spec, o_spec], out_specs=(o_spec, kv_out, kv_out),
        name="xattn_bwd", compiler_params=_params(("parallel", "parallel")),
    )(proj, kv, kv, d_o)


def _gate_specs(tm):
    n = COL_GL // D_MODEL
    return [pl.BlockSpec((tm, D_MODEL), lambda i, j=j: (i, n + j)) for j in range(3)]


def _merge_fwd(proj, ys, tm=256):
    T = proj.shape[0]
    tm = _tile(T, tm)
    row = pl.BlockSpec((tm, D_MODEL), lambda i: (i, 0))

    def body(g0, g1, g2, y0, y1, y2, o_ref):
        acc = jax.nn.sigmoid(g0[...]) * y0[...]
        acc = acc + jax.nn.sigmoid(g1[...]) * y1[...]
        acc = acc + jax.nn.sigmoid(g2[...]) * y2[...]
        o_ref[...] = acc.astype(o_ref.dtype)

    return pl.pallas_call(
        body, out_shape=jax.ShapeDtypeStruct((T, D_MODEL), BF16), grid=(T // tm,),
        in_specs=_gate_specs(tm) + [row] * 3, out_specs=row, name="merge_fwd",
        compiler_params=_params(("parallel",)),
    )(proj, proj, proj, *ys)


def _merge_bwd(proj, ys, d_merged, tm=256):
    T = proj.shape[0]
    tm = _tile(T, tm)
    row = pl.BlockSpec((tm, D_MODEL), lambda i: (i, 0))

    def body(g0, g1, g2, y0, y1, y2, dm_ref, dgl_ref, d0, d1, d2):
        dm = dm_ref[...]
        for j, (g_ref, y_ref, d_ref) in enumerate(((g0, y0, d0), (g1, y1, d1), (g2, y2, d2))):
            sg = jax.nn.sigmoid(g_ref[...])
            d_ref[...] = (dm * sg).astype(d_ref.dtype)
            dgl_ref[:, j * D_MODEL:(j + 1) * D_MODEL] = (dm * y_ref[...] * sg * (1.0 - sg)).astype(dgl_ref.dtype)

    dy = jax.ShapeDtypeStruct((T, D_MODEL), BF16)
    return pl.pallas_call(
        body, out_shape=(jax.ShapeDtypeStruct((T, 3 * D_MODEL), BF16), dy, dy, dy), grid=(T // tm,),
        in_specs=_gate_specs(tm) + [row] * 4,
        out_specs=(pl.BlockSpec((tm, 3 * D_MODEL), lambda i: (i, 0)), row, row, row), name="merge_bwd",
        compiler_params=_params(("parallel",)),
    )(proj, proj, proj, *ys, d_merged)


def _gelu(x):
    return 0.5 * x * (1.0 + jnp.tanh(GELU_C * (x + GELU_A * x * x * x)))


def _gelu_grad(x):
    t = jnp.tanh(GELU_C * (x + GELU_A * x * x * x))
    return 0.5 * (1.0 + t) + 0.5 * x * (1.0 - t * t) * GELU_C * (1.0 + 3.0 * GELU_A * x * x)


def _shift_down(x, prev, n):
    rows = x.shape[0]
    r = lax.broadcasted_iota(jnp.int32, (rows, 1), 0)
    out = pltpu.roll(x, n, axis=0)
    for j in range(n):
        out = jnp.where(r == j, prev[8 - n + j:8 - n + j + 1, :], out)
    return out


def _shift_up(x, nxt, n):
    rows = x.shape[0]
    r = lax.broadcasted_iota(jnp.int32, (rows, 1), 0)
    out = pltpu.roll(x, rows - n, axis=0)
    for j in range(n):
        out = jnp.where(r == rows - n + j, nxt[j:j + 1, :], out)
    return out


def _conv(a, prev, cw, cb):
    return _shift_down(a, prev, 2) * cw[0:1, :] + _shift_down(a, prev, 1) * cw[1:2, :] + a * cw[2:3, :] + cb


def _glu_fwd(up, cw, cb, S, tm=256):
    T = up.shape[2]
    tm = _tile(S, tm)
    per_seq = S // tm

    def body(ab_ref, prev_ref, cw_ref, cb_ref, u_ref):
        i = pl.program_id(1)
        prev = jnp.where(i % per_seq == 0, 0.0, prev_ref[...])
        ac = _conv(ab_ref[0], prev, cw_ref[...], cb_ref[...])
        u_ref[...] = (_gelu(ac) * ab_ref[1]).astype(u_ref.dtype)

    return pl.pallas_call(
        body, out_shape=jax.ShapeDtypeStruct((FFN_SLABS, T, UP_SHARD), BF16), grid=(FFN_SLABS, T // tm),
        in_specs=[pl.BlockSpec((2, None, tm, UP_SHARD), lambda d, i: (0, d, i, 0)),
                  pl.BlockSpec((None, None, 8, UP_SHARD), lambda d, i: (0, d, jnp.maximum(i * (tm // 8) - 1, 0), 0)),
                  pl.BlockSpec((None, 3, UP_SHARD), lambda d, i: (d, 0, 0)),
                  pl.BlockSpec((None, 1, UP_SHARD), lambda d, i: (d, 0, 0))],
        out_specs=pl.BlockSpec((None, tm, UP_SHARD), lambda d, i: (d, i, 0)), name="glu_fwd",
        compiler_params=_params(("parallel", "parallel")),
    )(up, up, cw, cb)


def _glu_bwd(up, d_u, cw, cb, S, tm=256):
    T = up.shape[2]
    tm = _tile(S, tm)
    per_seq = S // tm
    n_tiles = T // tm
    last8 = tm // 8

    def body(ab_ref, prev_ref, abn_ref, du_ref, dun_ref, cw_ref, cb_ref, dup_ref, dcw_ref, dcb_ref):
        i = pl.program_id(1)

        @pl.when(i == 0)
        def _():
            dcw_ref[...] = jnp.zeros_like(dcw_ref)
            dcb_ref[...] = jnp.zeros_like(dcb_ref)

        cw, cb = cw_ref[...], cb_ref[...]
        a, b = ab_ref[0], ab_ref[1]
        prev = jnp.where(i % per_seq == 0, 0.0, prev_ref[...])
        a2, a1 = _shift_down(a, prev, 2), _shift_down(a, prev, 1)
        ac = a2 * cw[0:1, :] + a1 * cw[1:2, :] + a * cw[2:3, :] + cb
        du = du_ref[...]
        dup_ref[1] = (du * _gelu(ac)).astype(dup_ref.dtype)
        dac = du * b * _gelu_grad(ac)
        dcb_ref[...] += jnp.sum(dac, axis=0, keepdims=True)
        dcw_ref[0:1, :] += jnp.sum(dac * a2, axis=0, keepdims=True)
        dcw_ref[1:2, :] += jnp.sum(dac * a1, axis=0, keepdims=True)
        dcw_ref[2:3, :] += jnp.sum(dac * a, axis=0, keepdims=True)
        an = abn_ref[0]
        acn = _conv(an, a[tm - 8:, :], cw, cb)
        dacn = jnp.where(i % per_seq == per_seq - 1, 0.0, dun_ref[...] * abn_ref[1] * _gelu_grad(acn))
        da = dac * cw[2:3, :] + _shift_up(dac, dacn, 1) * cw[1:2, :] + _shift_up(dac, dacn, 2) * cw[0:1, :]
        dup_ref[0] = da.astype(dup_ref.dtype)

    def nxt(i):
        return jnp.minimum((i + 1) * last8, T // 8 - 1)

    return pl.pallas_call(
        body,
        out_shape=(jax.ShapeDtypeStruct((2, FFN_SLABS, T, UP_SHARD), BF16),
                   jax.ShapeDtypeStruct((FFN_SLABS, 3, UP_SHARD), F32),
                   jax.ShapeDtypeStruct((FFN_SLABS, 1, UP_SHARD), F32)),
        grid=(FFN_SLABS, n_tiles),
        in_specs=[pl.BlockSpec((2, None, tm, UP_SHARD), lambda d, i: (0, d, i, 0)),
                  pl.BlockSpec((None, None, 8, UP_SHARD), lambda d, i: (0, d, jnp.maximum(i * last8 - 1, 0), 0)),
                  pl.BlockSpec((2, None, 8, UP_SHARD), lambda d, i: (0, d, nxt(i), 0)),
                  pl.BlockSpec((None, tm, UP_SHARD), lambda d, i: (d, i, 0)),
                  pl.BlockSpec((None, 8, UP_SHARD), lambda d, i: (d, nxt(i), 0)),
                  pl.BlockSpec((None, 3, UP_SHARD), lambda d, i: (d, 0, 0)),
                  pl.BlockSpec((None, 1, UP_SHARD), lambda d, i: (d, 0, 0))],
        out_specs=(pl.BlockSpec((2, None, tm, UP_SHARD), lambda d, i: (0, d, i, 0)),
                   pl.BlockSpec((None, 3, UP_SHARD), lambda d, i: (d, 0, 0)),
                   pl.BlockSpec((None, 1, UP_SHARD), lambda d, i: (d, 0, 0))),
        name="glu_bwd", compiler_params=_params(("parallel", "arbitrary")),
    )(up, up, up, d_u, d_u, cw, cb)


def _mm_up(h2, w_up, tm=512):
    T, K = h2.shape
    tm = _tile(T, tm)
    return _matmul(
        "mm_up", "nn", h2, w_up, jax.ShapeDtypeStruct((N_DEV, T, UP_SHARD), F32), (N_DEV, T // tm, 1),
        pl.BlockSpec((tm, K), lambda j, i, k: (i, 0)), pl.BlockSpec((None, K, UP_SHARD), lambda j, i, k: (j, 0, 0)),
        pl.BlockSpec((None, tm, UP_SHARD), lambda j, i, k: (j, i, 0)), (tm, UP_SHARD))


def _mm_down(u, w_down, res, tm=512):
    J, T, n = u.shape
    tm = _tile(T, tm)
    row = pl.BlockSpec((tm, D_MODEL), lambda i, d: (i, 0))
    return _matmul(
        "mm_down", "nn", u, w_down, jax.ShapeDtypeStruct((T, D_MODEL), F32), (T // tm, J),
        pl.BlockSpec((None, tm, n), lambda i, d: (d, i, 0)), pl.BlockSpec((None, n, D_MODEL), lambda i, d: (d, 0, 0)),
        row, (tm, D_MODEL), res, row)


def _mm_down_t(dx, w_down, tm=512):
    T = dx.shape[0]
    J, n, _ = w_down.shape
    tm = _tile(T, tm)
    return _matmul(
        "mm_down_t", "nt", dx, w_down, jax.ShapeDtypeStruct((J, T, n), F32), (J, T // tm, 1),
        pl.BlockSpec((tm, D_MODEL), lambda d, i, k: (i, 0)), pl.BlockSpec((None, n, D_MODEL), lambda d, i, k: (d, 0, 0)),
        pl.BlockSpec((None, tm, n), lambda d, i, k: (d, i, 0)), (tm, n))


def _mm_dw_down(u, dx, tk=512):
    J, T, n = u.shape
    tk = _tile(T, tk)
    return _matmul(
        "mm_dw_down", "tn", u, dx, jax.ShapeDtypeStruct((J, n, D_MODEL), BF16), (J, T // tk),
        pl.BlockSpec((None, tk, n), lambda d, k: (d, k, 0)), pl.BlockSpec((tk, D_MODEL), lambda d, k: (k, 0)),
        pl.BlockSpec((None, n, D_MODEL), lambda d, k: (d, 0, 0)), (n, D_MODEL))


def _mm_dw_up(h2, d_up, tk=512):
    T, K = h2.shape
    tk = _tile(T, tk)
    return _matmul(
        "mm_dw_up", "tn", h2, d_up, jax.ShapeDtypeStruct((N_DEV, K, UP_SHARD), BF16), (N_DEV, T // tk),
        pl.BlockSpec((tk, K), lambda j, k: (k, 0)), pl.BlockSpec((None, tk, UP_SHARD), lambda j, k: (j, k, 0)),
        pl.BlockSpec((None, K, UP_SHARD), lambda j, k: (j, 0, 0)), (K, UP_SHARD))


def _mm_up_t(d_up, w_up, tm=512):
    J, T, n = d_up.shape
    K = w_up.shape[1]
    tm = _tile(T, tm)
    return _matmul(
        "mm_up_t", "nt", d_up, w_up, jax.ShapeDtypeStruct((T, K), F32), (T // tm, J),
        pl.BlockSpec((None, tm, n), lambda i, j: (j, i, 0)), pl.BlockSpec((None, K, n), lambda i, j: (j, 0, 0)),
        pl.BlockSpec((tm, K), lambda i, j: (i, 0)), (tm, K))


def _loss_head(x2, target, g_final, tm=512):
    T, Dm = x2.shape
    tm = _tile(T, tm)

    def body(x_ref, t_ref, g_ref, dx_ref, dg_ref, loss_ref):
        @pl.when(pl.program_id(0) == 0)
        def _():
            dg_ref[...] = jnp.zeros_like(dg_ref)
            loss_ref[...] = jnp.zeros_like(loss_ref)

        xv = x_ref[...]
        r = lax.rsqrt(jnp.mean(xv * xv, axis=-1, keepdims=True) + EPS)
        xhat = xv * r
        err = xhat * g_ref[...] - t_ref[...]
        loss_ref[...] += (0.5 / Dm) * jnp.sum(err * err)
        dy = err * (1.0 / Dm)
        dg_ref[...] += jnp.sum(dy * xhat, axis=0, keepdims=True)
        dxhat = dy * g_ref[...]
        dx_ref[...] = r * (dxhat - xhat * jnp.mean(dxhat * xhat, axis=-1, keepdims=True))

    row = pl.BlockSpec((tm, Dm), lambda i: (i, 0))
    vec = pl.BlockSpec((1, Dm), lambda i: (0, 0))
    return pl.pallas_call(
        body,
        out_shape=(jax.ShapeDtypeStruct((T, Dm), F32), jax.ShapeDtypeStruct((1, Dm), F32),
                   jax.ShapeDtypeStruct((1, Dm), F32)),
        grid=(T // tm,), in_specs=[row, row, vec], out_specs=(row, vec, vec), name="loss_head",
        compiler_params=_params(("arbitrary",)),
    )(x2, target, g_final)


def _peer(k):
    x, y, c = lax.axis_index("x"), lax.axis_index("y"), lax.axis_index("c")
    kx, ky, kc = (k >> 2) & 1, (k >> 1) & 1, k & 1
    px, py, pc = x ^ kx, y ^ ky, c ^ kc
    return (px, py, pc), 4 * px + 2 * py + pc


def _my_index():
    return 4 * lax.axis_index("x") + 2 * lax.axis_index("y") + lax.axis_index("c")


def _cast_shards(shards):
    def body(*refs):
        n = len(refs) // 2
        for src, dst in zip(refs[:n], refs[n:]):
            dst[...] = src[...].astype(dst.dtype)

    return pl.pallas_call(
        body, out_shape=[jax.ShapeDtypeStruct(s.shape, BF16) for s in shards], name="cast_shards",
        compiler_params=pltpu.CompilerParams(vmem_limit_bytes=VMEM_LIMIT),
    )(*shards)


def _all_gather(shards):
    n = len(shards)

    def body(*refs):
        src, dst = refs[:n], refs[n:2 * n]
        send_sems, recv_sems, local_sems = refs[2 * n:]
        me = _my_index()
        local = [pltpu.make_async_copy(src[w], dst[w].at[me], local_sems.at[w]) for w in range(n)]
        for cp in local:
            cp.start()
        copies = []
        for k in range(1, N_DEV):
            peer, peer_idx = _peer(k)
            for w in range(n):
                s = w * (N_DEV - 1) + k - 1
                send = pltpu.make_async_remote_copy(
                    src_ref=src[w], dst_ref=dst[w].at[me], send_sem=send_sems.at[s], recv_sem=recv_sems.at[s],
                    device_id=peer, device_id_type=MESH)
                send.start()
                arrive = pltpu.make_async_remote_copy(
                    src_ref=src[w], dst_ref=dst[w].at[peer_idx], send_sem=send_sems.at[s], recv_sem=recv_sems.at[s],
                    device_id=peer, device_id_type=MESH)
                copies.append((send, arrive))
        for send, arrive in copies:
            send.wait_send()
            arrive.wait_recv()
        for cp in local:
            cp.wait()

    hbm = pl.BlockSpec(memory_space=pltpu.HBM)
    return pl.pallas_call(
        body, out_shape=[jax.ShapeDtypeStruct((N_DEV,) + s.shape, s.dtype) for s in shards],
        in_specs=[hbm] * n, out_specs=[hbm] * n,
        scratch_shapes=[pltpu.SemaphoreType.DMA((n * (N_DEV - 1),)), pltpu.SemaphoreType.DMA((n * (N_DEV - 1),)),
                        pltpu.SemaphoreType.DMA((n,))],
        name="all_gather_weights",
        compiler_params=pltpu.CompilerParams(has_side_effects=True),
    )(*shards)


def _exchange_grads(partials, packed):
    n = len(partials)

    def body(*refs):
        src, packed_ref = refs[:n], refs[n]
        dst, gathered_ref = refs[n + 1:2 * n + 1], refs[2 * n + 1]
        send_sems, recv_sems, local_sems = refs[2 * n + 2:]
        me = _my_index()
        local = [pltpu.make_async_copy(src[w].at[me], dst[w].at[me], local_sems.at[w]) for w in range(n)]
        local.append(pltpu.make_async_copy(packed_ref, gathered_ref.at[me], local_sems.at[n]))
        for cp in local:
            cp.start()
        copies = []
        for k in range(1, N_DEV):
            peer, peer_idx = _peer(k)
            for w in range(n + 1):
                s = w * (N_DEV - 1) + k - 1
                if w < n:
                    src_ref, land, slot = src[w].at[peer_idx], dst[w].at[me], dst[w].at[peer_idx]
                else:
                    src_ref, land, slot = packed_ref, gathered_ref.at[me], gathered_ref.at[peer_idx]
                send = pltpu.make_async_remote_copy(
                    src_ref=src_ref, dst_ref=land, send_sem=send_sems.at[s], recv_sem=recv_sems.at[s],
                    device_id=peer, device_id_type=MESH)
                send.start()
                arrive = pltpu.make_async_remote_copy(
                    src_ref=src_ref, dst_ref=slot, send_sem=send_sems.at[s], recv_sem=recv_sems.at[s],
                    device_id=peer, device_id_type=MESH)
                copies.append((send, arrive))
        for send, arrive in copies:
            send.wait_send()
            arrive.wait_recv()
        for cp in local:
            cp.wait()

    hbm = pl.BlockSpec(memory_space=pltpu.HBM)
    n_sem = (n + 1) * (N_DEV - 1)
    outs = pl.pallas_call(
        body,
        out_shape=[jax.ShapeDtypeStruct(p.shape, p.dtype) for p in partials]
        + [jax.ShapeDtypeStruct((N_DEV,) + packed.shape, packed.dtype)],
        in_specs=[hbm] * (n + 1), out_specs=[hbm] * (n + 1),
        scratch_shapes=[pltpu.SemaphoreType.DMA((n_sem,)), pltpu.SemaphoreType.DMA((n_sem,)),
                        pltpu.SemaphoreType.DMA((n + 1,))],
        name="exchange_grads",
        compiler_params=pltpu.CompilerParams(has_side_effects=True),
    )(*partials, packed)
    return outs[:n], outs[n]


def _adamw(w, g, m, v):
    m = ADAM_B1 * m + (1.0 - ADAM_B1) * g
    v = ADAM_B2 * v + (1.0 - ADAM_B2) * (g * g)
    m_hat = m / (1.0 - ADAM_B1 ** ADAM_STEP)
    v_hat = v / (1.0 - ADAM_B2 ** ADAM_STEP)
    delta = -ADAM_LR * (m_hat / (jnp.sqrt(v_hat) + ADAM_EPS) + ADAM_WD * w)
    return delta, m, v


def _reduce_adam(name, parts, w, m, v, tr=128):
    R, Cn = w.shape
    tr = _tile(R, tr) if R % 8 == 0 else R
    if R // tr * tr != R or (tr % 8 and tr != R):
        tr = R

    def body(p_ref, w_ref, m_ref, v_ref, g_out, d_out, m_out, v_out):
        g = p_ref[0].astype(F32)
        for d in range(1, N_DEV):
            g = g + p_ref[d].astype(F32)
        delta, m_new, v_new = _adamw(w_ref[...], g, m_ref[...], v_ref[...])
        g_out[...] = g
        d_out[...] = delta
        m_out[...] = m_new
        v_out[...] = v_new

    row = pl.BlockSpec((tr, Cn), lambda i: (i, 0))
    shape = jax.ShapeDtypeStruct((R, Cn), F32)
    return pl.pallas_call(
        body, out_shape=(shape,) * 4, grid=(R // tr,),
        in_specs=[pl.BlockSpec((N_DEV, tr, Cn), lambda i: (0, i, 0)), row, row, row], out_specs=(row,) * 4,
        name=name, compiler_params=_params(("parallel",)),
    )(parts, w, m, v)


def _adam_only(name, g, w, m, v):
    def body(g_ref, w_ref, m_ref, v_ref, d_out, m_out, v_out):
        delta, m_new, v_new = _adamw(w_ref[...], g_ref[...], m_ref[...], v_ref[...])
        d_out[...] = delta
        m_out[...] = m_new
        v_out[...] = v_new

    shape = jax.ShapeDtypeStruct(w.shape, F32)
    return pl.pallas_call(body, out_shape=(shape,) * 3, name=name)(g, w, m, v)


def _pack(arrays):
    rows = []
    for a in arrays:
        flat = a.reshape(-1).astype(F32)
        pad = (-flat.shape[0]) % (8 * PACK_LANES)
        rows.append(jnp.pad(flat, (0, pad)).reshape(-1, PACK_LANES))
    return jnp.concatenate(rows, axis=0)


def _unpack(packed, shapes):
    out, r = [], 0
    for shp in shapes:
        size = math.prod(shp)
        nr = 8 * -(-size // (8 * PACK_LANES))
        out.append(packed[r:r + nr].reshape(-1)[:size].reshape(shp))
        r += nr
    return out


def kernel(x, mem, g_mix, w_in, w_pool, pool_scale, w_a, g_ret, b_ret, w_r, g_mem, w_mem_kv, w_c, w_out, g_ffn, w_up, conv_w, conv_b, w_down, g_final, loss_target, m_g_mix, m_w_in, m_w_pool, m_pool_scale, m_w_a, m_g_ret, m_b_ret, m_w_r, m_g_mem, m_w_mem_kv, m_w_c, m_w_out, m_g_ffn, m_w_up, m_conv_w, m_conv_b, m_w_down, m_g_final, v_g_mix, v_w_in, v_w_pool, v_pool_scale, v_w_a, v_g_ret, v_b_ret, v_w_r, v_g_mem, v_w_mem_kv, v_w_c, v_w_out, v_g_ffn, v_w_up, v_conv_w, v_conv_b, v_w_down, v_g_final):
    B, S, _ = x.shape
    M = mem.shape[1]
    T = B * S
    me = _my_index()
    x2d = x.reshape(T, D_MODEL)
    mem2d = mem.reshape(B * M, D_MODEL)
    tgt2d = loss_target.reshape(T, D_MODEL)
    g_final2 = g_final.reshape(1, D_MODEL)

    big = dict(w_in=w_in[0], w_a=w_a[0], w_r=w_r[0], w_mem_kv=w_mem_kv[0], w_c=w_c[0], w_out=w_out[0],
               w_up=w_up[0], w_down=w_down[0])
    names = list(big)
    cast = _cast_shards([big[n] for n in names])
    gathered = _all_gather(list(cast) + [conv_w[0]])
    W = dict(zip(names, gathered[:-1]))
    Win, Wup = W["w_in"], W["w_up"]
    Wa = W["w_a"].transpose(1, 0, 2).reshape(POOL_WIDTH, D_MODEL)
    Wc = W["w_c"].transpose(1, 0, 2).reshape(XA_WIDTH, D_MODEL)
    Wr = W["w_r"].reshape(D_MODEL, D_MODEL)
    Wkv = W["w_mem_kv"].reshape(D_MODEL, D_MODEL)
    Wout = W["w_out"].reshape(D_MODEL, D_MODEL)
    Wdown = W["w_down"].reshape(FFN_SLABS, UP_SHARD, D_MODEL)
    cw_full = gathered[-1].transpose(1, 0, 2).reshape(3, FFN_HIDDEN)
    cw = cw_full.reshape(3, FFN_SLABS, UP_SHARD).transpose(1, 0, 2)
    cb = conv_b[0].reshape(FFN_SLABS, 1, UP_SHARD)
    wp = w_pool[0]
    tables = _ret_tables(S)

    h = _rms_fwd("rms_mix", x2d, g_mix)
    proj = _mm_cols_slab("mm_in", h, Win)
    ypre = _pool_fwd(proj, wp, pool_scale, B, S)
    y_pool = _mm_rows("mm_a", ypre, Wa)
    yr = _ret_fwd(proj, g_ret, b_ret, tables, B, S)
    y_ret = _mm_rows("mm_r", yr, Wr)
    mem_n = _rms_fwd("rms_mem", mem2d, g_mem)
    kv = _mm_rows("mm_kv", mem_n, Wkv)
    o_mem = _xa_fwd(proj, kv, B, S, M)
    y_mem = _mm_rows("mm_c", o_mem, Wc)
    ys = (y_pool, y_ret, y_mem)
    merged = _merge_fwd(proj, ys)
    x1 = _mm_rows("mm_out", merged, Wout, res=x2d)
    h2 = _rms_fwd("rms_ffn", x1, g_ffn)
    up = _mm_up(h2, Wup).reshape(2, FFN_SLABS, T, UP_SHARD)
    u = _glu_fwd(up, cw, cb, S)
    x2 = _mm_down(u, Wdown, x1)

    dx2, dg_final, loss_part = _loss_head(x2, tgt2d, g_final2)
    d_u = _mm_down_t(dx2, Wdown)
    dW_down = _mm_dw_down(u, dx2)
    d_up, d_cw, d_cb = _glu_bwd(up, d_u, cw, cb, S)
    d_up = d_up.reshape(N_DEV, T, UP_SHARD)
    dW_up = _mm_dw_up(h2, d_up)
    d_h2 = _mm_up_t(d_up, Wup)
    dx1, dg_ffn = _rms_bwd("rms_ffn_bwd", x1, g_ffn, d_h2, dx2)
    d_merged = _mm_rows("mm_out_t", dx1, Wout, kind="nt")
    dW_out = _mm_tn("mm_dw_out", merged, dx1, BF16)
    d_gl, d_y_pool, d_y_ret, d_y_mem = _merge_bwd(proj, ys, d_merged)
    dW_a = _mm_tn("mm_dw_a", ypre, d_y_pool, BF16)
    d_ypre = _mm_rows("mm_a_t", d_y_pool, Wa, kind="nt")
    d_hp, dw_pool, d_scale = _pool_bwd(proj, d_ypre, wp, pool_scale, B, S)
    dW_r = _mm_tn("mm_dw_r", yr, d_y_ret, BF16)
    d_yr = _mm_rows("mm_r_t", d_y_ret, Wr, kind="nt")
    d_q, d_k, d_v, d_gr, dg_ret, db_ret = _ret_bwd(proj, d_yr, g_ret, b_ret, tables, B, S)
    dW_c = _mm_tn("mm_dw_c", o_mem, d_y_mem, BF16)
    d_o_mem = _mm_rows("mm_c_t", d_y_mem, Wc, kind="nt")
    d_qx, d_kmem, d_vmem = _xa_bwd(proj, kv, d_o_mem, B, S, M)
    d_kv = jnp.concatenate([d_kmem, d_vmem], axis=1)
    dW_kv = _mm_tn("mm_dw_kv", mem_n, d_kv, BF16)
    d_mem_n = _mm_rows("mm_kv_t", d_kv, Wkv, kind="nt")
    dg_mem = _rms_bwd("rms_mem_bwd", mem2d, g_mem, d_mem_n, None)
    d_proj = jnp.concatenate([d_hp, d_q, d_k, d_v, d_gr, d_qx, d_gl], axis=1)
    dW_in = _mm_tn_slab("mm_dw_in", h, d_proj, IN_SHARD, BF16)
    d_h = _mm_cols_slab_t("mm_in_t", d_proj, Win)
    grad_x, dg_mix = _rms_bwd("rms_mix_bwd", x2d, g_mix, d_h, dx1)

    partials = dict(
        w_in=dW_in, w_a=dW_a.reshape(POOL_WIDTH, N_DEV, -1).transpose(1, 0, 2),
        w_r=dW_r.reshape(N_DEV, -1, D_MODEL), w_mem_kv=dW_kv.reshape(N_DEV, -1, D_MODEL),
        w_c=dW_c.reshape(XA_WIDTH, N_DEV, -1).transpose(1, 0, 2), w_out=dW_out.reshape(N_DEV, -1, D_MODEL), w_up=dW_up, w_down=dW_down.reshape(N_DEV, -1, D_MODEL))
    d_cw_full = d_cw.transpose(1, 0, 2).reshape(3, FFN_HIDDEN)
    small_names = ["g_mix", "w_pool", "pool_scale", "g_ret", "b_ret", "g_mem", "g_ffn", "conv_b", "g_final"]
    small_grads = [dg_mix, dw_pool, d_scale, dg_ret, db_ret, dg_mem, dg_ffn, d_cb.reshape(1, FFN_HIDDEN), dg_final]
    packed = _pack(small_grads + [d_cw_full, loss_part])
    received, packed_all = _exchange_grads([partials[n] for n in names], packed)
    received = dict(zip(names, received))

    args = dict(g_mix=g_mix, w_in=w_in, w_pool=w_pool, pool_scale=pool_scale, w_a=w_a, g_ret=g_ret, b_ret=b_ret,
                w_r=w_r, g_mem=g_mem, w_mem_kv=w_mem_kv, w_c=w_c, w_out=w_out, g_ffn=g_ffn, w_up=w_up,
                conv_w=conv_w, conv_b=conv_b, w_down=w_down, g_final=g_final)
    m_in = dict(g_mix=m_g_mix, w_in=m_w_in, w_pool=m_w_pool, pool_scale=m_pool_scale, w_a=m_w_a, g_ret=m_g_ret,
                b_ret=m_b_ret, w_r=m_w_r, g_mem=m_g_mem, w_mem_kv=m_w_mem_kv, w_c=m_w_c, w_out=m_w_out,
                g_ffn=m_g_ffn, w_up=m_w_up, conv_w=m_conv_w, conv_b=m_conv_b, w_down=m_w_down, g_final=m_g_final)
    v_in = dict(g_mix=v_g_mix, w_in=v_w_in, w_pool=v_w_pool, pool_scale=v_pool_scale, w_a=v_w_a, g_ret=v_g_ret,
                b_ret=v_b_ret, w_r=v_w_r, g_mem=v_g_mem, w_mem_kv=v_w_mem_kv, w_c=v_w_c, w_out=v_w_out,
                g_ffn=v_g_ffn, w_up=v_w_up, conv_w=v_conv_w, conv_b=v_conv_b, w_down=v_w_down, g_final=v_g_final)

    grads, deltas, new_m, new_v = {}, {}, {}, {}
    for n in names:
        shard = big[n].shape
        outs = _reduce_adam("adam_" + n, received[n], big[n], m_in[n][0], v_in[n][0])
        for store, val in zip((grads, deltas, new_m, new_v), outs):
            store[n] = val.reshape((1,) + shard)

    zeros_tail = [jnp.zeros((3, FFN_HIDDEN), F32), jnp.zeros((1, PACK_LANES), F32)]
    w_small = _pack([args[n] for n in small_names] + zeros_tail)
    m_small = _pack([m_in[n] for n in small_names] + zeros_tail)
    v_small = _pack([v_in[n] for n in small_names] + zeros_tail)
    small_out = _reduce_adam("adam_small", packed_all, w_small, m_small, v_small, tr=w_small.shape[0])
    small_shapes = [args[n].shape for n in small_names] + [(3, FFN_HIDDEN), (1, PACK_LANES)]
    for store, val in zip((grads, deltas, new_m, new_v), small_out):
        parts = _unpack(val, small_shapes)
        for n, p in zip(small_names, parts):
            store[n] = p
        if store is grads:
            g_cw_full, loss_row = parts[-2], parts[-1]
    loss = loss_row[0, 0]

    shard_cols = FFN_HIDDEN // N_DEV
    g_cw = lax.dynamic_slice_in_dim(g_cw_full, me * shard_cols, shard_cols, axis=1)
    d_, m_, v_ = _adam_only("adam_conv_w", g_cw, conv_w[0], m_conv_w[0], v_conv_w[0])
    grads["conv_w"], deltas["conv_w"], new_m["conv_w"], new_v["conv_w"] = g_cw[None], d_[None], m_[None], v_[None]

    order = ["g_mix", "w_in", "w_pool", "pool_scale", "w_a", "g_ret", "b_ret", "w_r", "g_mem", "w_mem_kv", "w_c",
             "w_out", "g_ffn", "w_up", "conv_w", "conv_b", "w_down", "g_final"]
    return (loss, grad_x.reshape(B, S, D_MODEL), *[grads[n] for n in order], *[deltas[n] for n in order],
            *[new_m[n] for n in order], *[new_v[n] for n in order])
```

```python
import functools
import math

import jax
import jax.numpy as jnp
from jax import lax
from jax.experimental import pallas as pl
from jax.experimental.pallas import tpu as pltpu

F32 = jnp.float32
BF16 = jnp.bfloat16

N_DEV = 8
D_MODEL = 1024
POOL_WINDOWS = (2, 4, 8, 16)
POOL_GROUP_DIM = 128
POOL_WIDTH = 512
POOL_HALO = 16
RET_HEADS = 4
RET_QK_DIM = 128
RET_V_DIM = 256
RET_CHUNK = 128
ROPE_BASE = 10000.0
XA_HEADS = 4
XA_HEAD_DIM = 128
XA_WIDTH = 512
IN_WIDTH = 7168
IN_SHARD = IN_WIDTH // N_DEV
FFN_HIDDEN = 2816
UP_SHARD = 2 * FFN_HIDDEN // N_DEV
FFN_SLABS = FFN_HIDDEN // UP_SHARD
EPS = 1e-6
ADAM_LR = 0.001
ADAM_B1 = 0.9
ADAM_B2 = 0.999
ADAM_EPS = 1e-08
ADAM_WD = 0.01
ADAM_STEP = 10
GELU_C = math.sqrt(2.0 / math.pi)
GELU_A = 0.044715
VMEM_LIMIT = 56 * 1024 * 1024
PACK_LANES = 1024
MESH = pl.DeviceIdType.MESH

COL_Q, COL_K, COL_V, COL_GR, COL_QX, COL_GL = 512, 1024, 1536, 2560, 3584, 4096

_DIMS = {
    "nn": (((1,), (0,)), ((), ())),
    "nt": (((1,), (1,)), ((), ())),
    "tn": (((0,), (0,)), ((), ())),
}


def _dot(a, b, kind="nn"):
    return lax.dot_general(a.astype(BF16), b.astype(BF16), _DIMS[kind], preferred_element_type=F32)


def _params(sem, vmem=VMEM_LIMIT):
    return pltpu.CompilerParams(dimension_semantics=sem, vmem_limit_bytes=vmem)


def _tile(n, pref):
    t = min(n, pref)
    while n % t:
        t //= 2
    return t


def _mesh_pos():
    return lax.axis_index("x"), lax.axis_index("y"), lax.axis_index("c")


def _dev_index(x, y, c):
    return 4 * x + 2 * y + c


def _my_index():
    return _dev_index(*_mesh_pos())


def _remote(src, dst, send_sems, recv_sems, s, to):
    return pltpu.make_async_remote_copy(src_ref=src, dst_ref=dst, send_sem=send_sems.at[s], recv_sem=recv_sems.at[s],
                                        device_id=to, device_id_type=MESH)


class _Gather:
    def __init__(self, shards):
        self.inputs = list(shards)
        self.out_shapes = [jax.ShapeDtypeStruct((N_DEV,) + s.shape, s.dtype) for s in shards]
        n = len(shards)
        self.sem_shapes = [pltpu.SemaphoreType.DMA((7 * n,)), pltpu.SemaphoreType.DMA((7 * n,)),
                           pltpu.SemaphoreType.DMA((n,))]

    def _places(self):
        x, y, c = _mesh_pos()
        return (x, y, c), (x, y, 1 - c), [(1 - x, y), (x, 1 - y), (1 - x, 1 - y)]

    def _local(self, src, dst, sems):
        me = _my_index()
        return [pltpu.make_async_copy(src[w], dst[w].at[me], sems[2].at[w]) for w in range(len(src))]

    def start(self, src, dst, sems):
        me, sib, chips = self._places()
        for cp in self._local(src, dst, sems):
            cp.start()
        for w in range(len(src)):
            land = dst[w].at[_dev_index(*me)]
            _remote(src[w], land, sems[0], sems[1], 7 * w, sib).start()
            for j, chip in enumerate(chips):
                _remote(src[w], land, sems[0], sems[1], 7 * w + 1 + j, (*chip, me[2])).start()

    def finish(self, src, dst, sems):
        me, sib, chips = self._places()
        n = len(src)
        for j, chip in enumerate(chips):
            for w in range(n):
                block = dst[w].at[_dev_index(*chip, me[2])]
                _remote(src[w], block, sems[0], sems[1], 7 * w + 1 + j, me).wait_recv()
                _remote(block, block, sems[0], sems[1], 7 * w + 4 + j, sib).start()
        for w in range(n):
            _remote(src[w], dst[w].at[_dev_index(*sib)], sems[0], sems[1], 7 * w, me).wait_recv()
            for j, chip in enumerate(chips):
                block = dst[w].at[_dev_index(*chip, sib[2])]
                _remote(block, block, sems[0], sems[1], 7 * w + 4 + j, me).wait_recv()
            for k in range(7):
                _remote(src[w], dst[w].at[0], sems[0], sems[1], 7 * w + k, me).wait_send()
        for cp in self._local(src, dst, sems):
            cp.wait()


class _Exchange:
    def __init__(self, partials, whole=()):
        self.n_part = len(partials)
        self.inputs = list(partials) + list(whole)
        self.out_shapes = [jax.ShapeDtypeStruct(p.shape, p.dtype) for p in partials]
        self.out_shapes += [jax.ShapeDtypeStruct((N_DEV,) + a.shape, a.dtype) for a in whole]
        n = len(self.inputs)
        self.sem_shapes = [pltpu.SemaphoreType.DMA((7 * n,)), pltpu.SemaphoreType.DMA((7 * n,)),
                           pltpu.SemaphoreType.DMA((n,))]

    def _peer(self, k):
        x, y, c = _mesh_pos()
        p = (x ^ ((k >> 2) & 1), y ^ ((k >> 1) & 1), c ^ (k & 1))
        return p, _dev_index(*p)

    def _source(self, src, w, slot):
        return src[w].at[slot] if w < self.n_part else src[w]

    def _local(self, src, dst, sems):
        me = _my_index()
        return [pltpu.make_async_copy(self._source(src, w, me), dst[w].at[me], sems[2].at[w])
                for w in range(len(src))]

    def start(self, src, dst, sems):
        me = _my_index()
        for cp in self._local(src, dst, sems):
            cp.start()
        for k in range(1, N_DEV):
            peer, peer_idx = self._peer(k)
            for w in range(len(src)):
                _remote(self._source(src, w, peer_idx), dst[w].at[me], sems[0], sems[1], 7 * w + k - 1, peer).start()

    def finish(self, src, dst, sems):
        for k in range(1, N_DEV):
            peer, peer_idx = self._peer(k)
            for w in range(len(src)):
                cp = _remote(self._source(src, w, peer_idx), dst[w].at[peer_idx], sems[0], sems[1], 7 * w + k - 1, peer)
                cp.wait_send()
                cp.wait_recv()
        for cp in self._local(src, dst, sems):
            cp.wait()


def _pcall(body, args, *, name, out_shape, grid, in_specs, out_specs, scratch_shapes=(), sem=None, comm=None):
    single = not isinstance(out_shape, (tuple, list))
    outs = [out_shape] if single else list(out_shape)
    ospecs = [out_specs] if single else list(out_specs)
    n_in, n_out, n_scr = len(args), len(outs), len(scratch_shapes)

    def pick(res):
        return res[0] if single else tuple(res[:n_out])

    if comm is None:
        res = pl.pallas_call(
            body, out_shape=outs, grid=grid, in_specs=list(in_specs), out_specs=ospecs,
            scratch_shapes=list(scratch_shapes), name=name, compiler_params=_params(sem),
        )(*args)
        return pick(res), ()

    nci, nco = len(comm.inputs), len(comm.out_shapes)

    def carrier(*refs):
        at = 0
        parts = []
        for size in (n_in, nci, n_out, nco, n_scr, 3):
            parts.append(refs[at:at + size])
            at += size
        ins, cins, o, couts, scr, sems = parts
        ids = [pl.program_id(a) for a in range(len(grid))]
        first = functools.reduce(jnp.logical_and, [i == 0 for i in ids])
        last = functools.reduce(jnp.logical_and, [i == g - 1 for i, g in zip(ids, grid)])

        @pl.when(first)
        def _():
            comm.start(cins, couts, sems)

        body(*ins, *o, *scr)

        @pl.when(last)
        def _():
            comm.finish(cins, couts, sems)

    hbm = pl.BlockSpec(memory_space=pltpu.HBM)
    res = pl.pallas_call(
        carrier, out_shape=outs + comm.out_shapes, grid=grid, in_specs=list(in_specs) + [hbm] * nci,
        out_specs=ospecs + [hbm] * nco, scratch_shapes=list(scratch_shapes) + comm.sem_shapes, name=name,
        compiler_params=_params(("arbitrary",) * len(grid)),
    )(*args, *comm.inputs)
    return pick(res), tuple(res[n_out:])


def _comm_call(name, comm):
    def body(*refs):
        nci, nco = len(comm.inputs), len(comm.out_shapes)
        cins, couts, sems = refs[:nci], refs[nci:nci + nco], refs[nci + nco:]
        comm.start(cins, couts, sems)
        comm.finish(cins, couts, sems)

    hbm = pl.BlockSpec(memory_space=pltpu.HBM)
    return pl.pallas_call(
        body, out_shape=comm.out_shapes, in_specs=[hbm] * len(comm.inputs), out_specs=[hbm] * len(comm.out_shapes),
        scratch_shapes=comm.sem_shapes, name=name,
    )(*comm.inputs)


def _matmul(name, kind, a, b, out_shape, grid, a_spec, b_spec, o_spec, acc_shape, res=None, res_spec=None,
            comm=None):
    nk = grid[-1]
    has_res = res is not None

    def body(*refs):
        a_ref, b_ref = refs[0], refs[1]
        o_ref = refs[2 + has_res]

        def prod():
            return _dot(a_ref[...], b_ref[...], kind)

        def finish(acc):
            if has_res:
                acc = acc + refs[2][...]
            o_ref[...] = acc.astype(o_ref.dtype)

        if nk == 1:
            finish(prod())
        else:
            acc_ref = refs[3 + has_res]
            k = pl.program_id(len(grid) - 1)

            @pl.when(k == 0)
            def _():
                acc_ref[...] = prod()

            @pl.when(k > 0)
            def _():
                acc_ref[...] += prod()

            @pl.when(k == nk - 1)
            def _():
                finish(acc_ref[...])

    in_specs = [a_spec, b_spec] + ([res_spec] if has_res else [])
    args = (a, b) + ((res,) if has_res else ())
    scratch = [pltpu.VMEM(acc_shape, F32)] if nk > 1 else []
    sem = ("parallel",) * (len(grid) - 1) + ("arbitrary",)
    out, landed = _pcall(body, args, name=name, out_shape=out_shape, grid=grid, in_specs=in_specs,
                         out_specs=o_spec, scratch_shapes=scratch, sem=sem, comm=comm)
    return out if comm is None else (out, landed)


def _mm_rows(name, a, w, out_dtype=F32, res=None, kind="nn", tm=512, comm=None):
    M, K = a.shape
    N = w.shape[1] if kind == "nn" else w.shape[0]
    tm = _tile(M, tm)
    res_spec = pl.BlockSpec((tm, N), lambda i, k: (i, 0)) if res is not None else None
    return _matmul(
        name, kind, a, w, jax.ShapeDtypeStruct((M, N), out_dtype), (M // tm, 1),
        pl.BlockSpec((tm, K), lambda i, k: (i, 0)), pl.BlockSpec(w.shape, lambda i, k: (0, 0)),
        pl.BlockSpec((tm, N), lambda i, k: (i, 0)), (tm, N), res, res_spec, comm)


def _mm_tn(name, a, b, out_dtype=F32, tk=512, comm=None):
    T, M = a.shape
    N = b.shape[1]
    tk = _tile(T, tk)
    return _matmul(
        name, "tn", a, b, jax.ShapeDtypeStruct((M, N), out_dtype), (1, T // tk),
        pl.BlockSpec((tk, M), lambda i, k: (k, 0)), pl.BlockSpec((tk, N), lambda i, k: (k, 0)),
        pl.BlockSpec((M, N), lambda i, k: (0, 0)), (M, N), comm=comm)


def _mm_cols_slab(name, a, w_slabs, out_dtype=F32, tm=512, comm=None):
    M, K = a.shape
    J, _, n = w_slabs.shape
    tm = _tile(M, tm)
    return _matmul(
        name, "nn", a, w_slabs, jax.ShapeDtypeStruct((M, J * n), out_dtype), (J, M // tm, 1),
        pl.BlockSpec((tm, K), lambda j, i, k: (i, 0)), pl.BlockSpec((None, K, n), lambda j, i, k: (j, 0, 0)),
        pl.BlockSpec((tm, n), lambda j, i, k: (i, j)), (tm, n), comm=comm)


def _mm_cols_slab_t(name, a, w_slabs, out_dtype=F32, tm=512, comm=None):
    M = a.shape[0]
    J, K, n = w_slabs.shape
    tm = _tile(M, tm)
    return _matmul(
        name, "nt", a, w_slabs, jax.ShapeDtypeStruct((M, K), out_dtype), (M // tm, J),
        pl.BlockSpec((tm, n), lambda i, j: (i, j)), pl.BlockSpec((None, K, n), lambda i, j: (j, 0, 0)),
        pl.BlockSpec((tm, K), lambda i, j: (i, 0)), (tm, K), comm=comm)


def _mm_tn_slab(name, a, b, n, out_dtype=F32, tk=512, comm=None):
    T, M = a.shape
    J = b.shape[1] // n
    tk = _tile(T, tk)
    return _matmul(
        name, "tn", a, b, jax.ShapeDtypeStruct((J, M, n), out_dtype), (J, T // tk),
        pl.BlockSpec((tk, M), lambda j, k: (k, 0)), pl.BlockSpec((tk, n), lambda j, k: (k, j)),
        pl.BlockSpec((None, M, n), lambda j, k: (j, 0, 0)), (M, n), comm=comm)


def _rms_fwd(name, x, g, tm=512):
    T, Dm = x.shape
    tm = _tile(T, tm)

    def body(x_ref, g_ref, h_ref):
        xv = x_ref[...]
        r = lax.rsqrt(jnp.mean(xv * xv, axis=-1, keepdims=True) + EPS)
        h_ref[...] = (xv * r * g_ref[...]).astype(h_ref.dtype)

    return pl.pallas_call(
        body, out_shape=jax.ShapeDtypeStruct((T, Dm), BF16), grid=(T // tm,),
        in_specs=[pl.BlockSpec((tm, Dm), lambda i: (i, 0)), pl.BlockSpec((1, Dm), lambda i: (0, 0))],
        out_specs=pl.BlockSpec((tm, Dm), lambda i: (i, 0)), name=name, compiler_params=_params(("parallel",)),
    )(x, g)


def _rms_bwd(name, x, g, dh, dres, tm=512):
    T, Dm = x.shape
    tm = _tile(T, tm)
    want_dx = dres is not None

    def body(*refs):
        if want_dx:
            x_ref, g_ref, dh_ref, dres_ref, dx_ref, dg_ref = refs
        else:
            x_ref, g_ref, dh_ref, dg_ref = refs
        xv = x_ref[...]
        r = lax.rsqrt(jnp.mean(xv * xv, axis=-1, keepdims=True) + EPS)
        xhat = xv * r
        dhv = dh_ref[...]

        @pl.when(pl.program_id(0) == 0)
        def _():
            dg_ref[...] = jnp.zeros_like(dg_ref)

        dg_ref[...] += jnp.sum(dhv * xhat, axis=0, keepdims=True)
        if want_dx:
            dxhat = dhv * g_ref[...]
            dx_ref[...] = dres_ref[...] + r * (dxhat - xhat * jnp.mean(dxhat * xhat, axis=-1, keepdims=True))

    row = pl.BlockSpec((tm, Dm), lambda i: (i, 0))
    vec = pl.BlockSpec((1, Dm), lambda i: (0, 0))
    if want_dx:
        return pl.pallas_call(
            body, out_shape=(jax.ShapeDtypeStruct((T, Dm), F32), jax.ShapeDtypeStruct((1, Dm), F32)),
            grid=(T // tm,), in_specs=[row, vec, row, row], out_specs=(row, vec), name=name,
            compiler_params=_params(("arbitrary",)),
        )(x, g, dh, dres)
    return pl.pallas_call(
        body, out_shape=jax.ShapeDtypeStruct((1, Dm), F32), grid=(T // tm,), in_specs=[row, vec, row],
        out_specs=vec, name=name, compiler_params=_params(("arbitrary",)),
    )(x, g, dh)


def _pool_rows(S):
    return _tile(S, 256)


def _pool_count(c0, rows, w):
    t = c0 + lax.broadcasted_iota(jnp.int32, (rows, 1), 0)
    return jnp.minimum(t + 1, w).astype(F32)


def _pool_fwd(proj, w_pool, scale, B, S):
    CH = _pool_rows(S)

    def body(hp_ref, wp_ref, sc_ref, o_ref, pad_ref):
        pad_ref[0:POOL_HALO, :] = jnp.zeros((POOL_HALO, POOL_WIDTH), F32)
        pad_ref[POOL_HALO:, :] = hp_ref[...]
        for gi, w in enumerate(POOL_WINDOWS):
            cols = slice(gi * POOL_GROUP_DIM, (gi + 1) * POOL_GROUP_DIM)
            for c in range(S // CH):
                base = POOL_HALO + c * CH
                acc = pad_ref[base:base + CH, cols]
                tok = acc
                for j in range(1, w):
                    acc = acc + pad_ref[base - j:base - j + CH, cols]
                pooled = acc / _pool_count(c * CH, CH, w) - tok
                z = _dot(pooled, wp_ref[gi])
                o_ref[c * CH:(c + 1) * CH, cols] = (z * sc_ref[:, cols]).astype(o_ref.dtype)

    return pl.pallas_call(
        body, out_shape=jax.ShapeDtypeStruct((B * S, POOL_WIDTH), BF16), grid=(B,),
        in_specs=[pl.BlockSpec((S, POOL_WIDTH), lambda b: (b, 0)),
                  pl.BlockSpec(w_pool.shape, lambda b: (0, 0, 0)),
                  pl.BlockSpec((1, POOL_WIDTH), lambda b: (0, 0))],
        out_specs=pl.BlockSpec((S, POOL_WIDTH), lambda b: (b, 0)),
        scratch_shapes=[pltpu.VMEM((S + POOL_HALO, POOL_WIDTH), F32)],
        name="pool_fwd", compiler_params=_params(("parallel",)),
    )(proj, w_pool, scale)


def _pool_bwd(proj, d_ypre, w_pool, scale, B, S):
    CH = _pool_rows(S)

    def body(hp_ref, dy_ref, wp_ref, sc_ref, dhp_ref, dwp_ref, dsc_ref, pad_ref, sc_pad_ref, dp_ref):
        @pl.when(pl.program_id(0) == 0)
        def _():
            dwp_ref[...] = jnp.zeros_like(dwp_ref)
            dsc_ref[...] = jnp.zeros_like(dsc_ref)

        pad_ref[0:POOL_HALO, :] = jnp.zeros((POOL_HALO, POOL_WIDTH), F32)
        pad_ref[POOL_HALO:, :] = hp_ref[...]
        sc_pad_ref[S:, :] = jnp.zeros((POOL_HALO, POOL_WIDTH), F32)
        for gi, w in enumerate(POOL_WINDOWS):
            cols = slice(gi * POOL_GROUP_DIM, (gi + 1) * POOL_GROUP_DIM)
            for c in range(S // CH):
                base = POOL_HALO + c * CH
                rows = slice(c * CH, (c + 1) * CH)
                acc = pad_ref[base:base + CH, cols]
                tok = acc
                for j in range(1, w):
                    acc = acc + pad_ref[base - j:base - j + CH, cols]
                cnt = _pool_count(c * CH, CH, w)
                pooled = acc / cnt - tok
                z = _dot(pooled, wp_ref[gi])
                dy = dy_ref[rows, cols]
                dsc_ref[:, cols] += jnp.sum(dy * z, axis=0, keepdims=True)
                dz = dy * sc_ref[:, cols]
                dwp_ref[gi] += _dot(pooled, dz, "tn")
                dpool = _dot(dz, wp_ref[gi], "nt")
                dp_ref[rows, cols] = dpool
                sc_pad_ref[rows, cols] = dpool / cnt
            for c in range(S // CH):
                rows = slice(c * CH, (c + 1) * CH)
                acc = sc_pad_ref[rows, cols]
                for j in range(1, w):
                    acc = acc + sc_pad_ref[c * CH + j:c * CH + j + CH, cols]
                dhp_ref[rows, cols] = (acc - dp_ref[rows, cols]).astype(dhp_ref.dtype)

    seq = pl.BlockSpec((S, POOL_WIDTH), lambda b: (b, 0))
    return pl.pallas_call(
        body,
        out_shape=(jax.ShapeDtypeStruct((B * S, POOL_WIDTH), BF16),
                   jax.ShapeDtypeStruct(w_pool.shape, F32), jax.ShapeDtypeStruct((1, POOL_WIDTH), F32)),
        grid=(B,),
        in_specs=[seq, seq, pl.BlockSpec(w_pool.shape, lambda b: (0, 0, 0)),
                  pl.BlockSpec((1, POOL_WIDTH), lambda b: (0, 0))],
        out_specs=(seq, pl.BlockSpec(w_pool.shape, lambda b: (0, 0, 0)),
                   pl.BlockSpec((1, POOL_WIDTH), lambda b: (0, 0))),
        scratch_shapes=[pltpu.VMEM((S + POOL_HALO, POOL_WIDTH), F32),
                        pltpu.VMEM((S + POOL_HALO, POOL_WIDTH), F32),
                        pltpu.VMEM((S, POOL_WIDTH), F32)],
        name="pool_bwd", compiler_params=_params(("arbitrary",)),
    )(proj, d_ypre, w_pool, scale)


def _ret_tables(S):
    half = RET_QK_DIM // 2
    inv = ROPE_BASE ** (-jnp.arange(half, dtype=F32) / half)
    ang = jnp.arange(S, dtype=F32)[:, None] * inv[None, :]
    cos, sin = jnp.cos(ang), jnp.sin(ang)
    cos_full = jnp.concatenate([cos, cos], axis=-1)
    sin_signed = jnp.concatenate([-sin, sin], axis=-1)
    C = RET_CHUNK
    lg = jnp.log1p(-jnp.exp2(-5.0 - jnp.arange(RET_HEADS, dtype=F32)))[:, None, None]
    idx = jnp.arange(C, dtype=F32)
    rel = idx[:, None] - idx[None, :]
    decay = jnp.where(rel >= 0, jnp.exp(jnp.maximum(rel, 0.0) * lg), 0.0)
    q_decay = jnp.broadcast_to(jnp.exp((idx + 1.0)[None, :, None] * lg), (RET_HEADS, C, RET_QK_DIM))
    k_decay = jnp.broadcast_to(jnp.exp((C - 1.0 - idx)[None, :, None] * lg), (RET_HEADS, C, RET_QK_DIM))
    c_decay = jnp.broadcast_to(jnp.exp(C * lg), (RET_HEADS, 1, RET_V_DIM))
    return cos_full, sin_signed, decay, q_decay, k_decay, c_decay


def _rope(x, cos_full, sin_signed):
    return x * cos_full + pltpu.roll(x, RET_QK_DIM // 2, axis=1) * sin_signed


def _rope_t(dy, cos_full, sin_signed):
    return dy * cos_full + pltpu.roll(dy * sin_signed, RET_QK_DIM // 2, axis=1)


def _ret_specs(S, bh):
    def at(col_of_head, width):
        def index(*ids):
            b, h = bh(*ids)
            return (b, col_of_head + h)
        return pl.BlockSpec((S, width), index)

    def per_head(shape):
        def index(*ids):
            _, h = bh(*ids)
            return (h,) + (0,) * len(shape)
        return pl.BlockSpec((None,) + shape, index)

    def head_vec(width):
        def index(*ids):
            _, h = bh(*ids)
            return (0, h)
        return pl.BlockSpec((1, width), index)

    table = pl.BlockSpec((S, RET_QK_DIM), lambda *ids: (0, 0))
    C = RET_CHUNK
    return dict(
        q=at(COL_Q // RET_QK_DIM, RET_QK_DIM), k=at(COL_K // RET_QK_DIM, RET_QK_DIM),
        v=at(COL_V // RET_V_DIM, RET_V_DIM), gr=at(COL_GR // RET_V_DIM, RET_V_DIM),
        table=table, decay=per_head((C, C)), qd=per_head((C, RET_QK_DIM)), kd=per_head((C, RET_QK_DIM)),
        cd=per_head((1, RET_V_DIM)), vec=head_vec(RET_V_DIM), out_qk=at(0, RET_QK_DIM), out_v=at(0, RET_V_DIM))


def _group_norm(o):
    mu = jnp.mean(o, axis=-1, keepdims=True)
    oc = o - mu
    rstd = lax.rsqrt(jnp.mean(oc * oc, axis=-1, keepdims=True) + EPS)
    return oc * rstd, rstd


def _ret_fwd(proj, g_ret, b_ret, tables, B, S, comm=None):
    C = RET_CHUNK
    cos_t, sin_t, decay, q_decay, k_decay, c_decay = tables
    sp = _ret_specs(S, lambda b, h: (b, h))

    def body(q_ref, k_ref, v_ref, gr_ref, cos_ref, sin_ref, dec_ref, qd_ref, kd_ref, cd_ref, g_ref, b_ref,
             y_ref, r_ref):
        r_ref[...] = jnp.zeros_like(r_ref)

        def chunk(i, carry):
            rows = pl.ds(pl.multiple_of(i * C, C), C)
            cs, sn = cos_ref[rows, :], sin_ref[rows, :]
            q = _rope(q_ref[rows, :], cs, sn)
            k = _rope(k_ref[rows, :], cs, sn) * (RET_QK_DIM ** -0.5)
            v = v_ref[rows, :]
            R = r_ref[...]
            s = _dot(q, k, "nt") * dec_ref[...]
            o = _dot(s, v) + _dot(q * qd_ref[...], R)
            r_ref[...] = cd_ref[...] * R + _dot(k * kd_ref[...], v, "tn")
            on, _ = _group_norm(o)
            gr = gr_ref[rows, :]
            y_ref[rows, :] = (gr * jax.nn.sigmoid(gr) * (on * g_ref[...] + b_ref[...])).astype(y_ref.dtype)
            return carry

        lax.fori_loop(0, S // C, chunk, 0)

    return _pcall(
        body, (proj, proj, proj, proj, cos_t, sin_t, decay, q_decay, k_decay, c_decay, g_ret, b_ret),
        name="ret_fwd", out_shape=jax.ShapeDtypeStruct((B * S, RET_HEADS * RET_V_DIM), BF16), grid=(B, RET_HEADS),
        in_specs=[sp["q"], sp["k"], sp["v"], sp["gr"], sp["table"], sp["table"], sp["decay"], sp["qd"],
                  sp["kd"], sp["cd"], sp["vec"], sp["vec"]],
        out_specs=sp["out_v"], scratch_shapes=[pltpu.VMEM((RET_QK_DIM, RET_V_DIM), F32)],
        sem=("parallel", "parallel"), comm=comm)


def _ret_bwd(proj, d_yr, g_ret, b_ret, tables, B, S, comm=None):
    C = RET_CHUNK
    N = S // C
    cos_t, sin_t, decay, q_decay, k_decay, c_decay = tables
    sp = _ret_specs(S, lambda h, b: (b, h))
    qk_scale = RET_QK_DIM ** -0.5

    def body(q_ref, k_ref, v_ref, gr_ref, dy_ref, cos_ref, sin_ref, dec_ref, qd_ref, kd_ref, cd_ref, g_ref, b_ref,
             dq_ref, dk_ref, dv_ref, dgr_ref, dg_ref, db_ref, qr_ref, kr_ref, rs_ref, dr_ref):
        @pl.when(pl.program_id(1) == 0)
        def _():
            dg_ref[...] = jnp.zeros_like(dg_ref)
            db_ref[...] = jnp.zeros_like(db_ref)

        dr_ref[...] = jnp.zeros_like(dr_ref)

        def sweep(i, R):
            rows = pl.ds(pl.multiple_of(i * C, C), C)
            cs, sn = cos_ref[rows, :], sin_ref[rows, :]
            q = _rope(q_ref[rows, :], cs, sn)
            k = _rope(k_ref[rows, :], cs, sn) * qk_scale
            qr_ref[rows, :] = q
            kr_ref[rows, :] = k
            rs_ref[i] = R
            return cd_ref[...] * R + _dot(k * kd_ref[...], v_ref[rows, :], "tn")

        lax.fori_loop(0, N, sweep, jnp.zeros((RET_QK_DIM, RET_V_DIM), F32))

        def back(step, carry):
            i = N - 1 - step
            rows = pl.ds(pl.multiple_of(i * C, C), C)
            q, k, v = qr_ref[rows, :], kr_ref[rows, :], v_ref[rows, :]
            R, dR = rs_ref[i], dr_ref[...]
            dec, qd, kd = dec_ref[...], qd_ref[...], kd_ref[...]
            s = _dot(q, k, "nt") * dec
            o = _dot(s, v) + _dot(q * qd, R)
            on, rstd = _group_norm(o)
            oaff = on * g_ref[...] + b_ref[...]
            gr = gr_ref[rows, :]
            sg = jax.nn.sigmoid(gr)
            dy = dy_ref[rows, :]
            dgr_ref[rows, :] = (dy * oaff * (sg * (1.0 + gr * (1.0 - sg)))).astype(dgr_ref.dtype)
            doaff = dy * (gr * sg)
            dg_ref[...] += jnp.sum(doaff * on, axis=0, keepdims=True)
            db_ref[...] += jnp.sum(doaff, axis=0, keepdims=True)
            don = doaff * g_ref[...]
            do = rstd * (don - jnp.mean(don, axis=-1, keepdims=True)
                         - on * jnp.mean(don * on, axis=-1, keepdims=True))
            ds = _dot(do, v, "nt") * dec
            dq = _dot(ds, k) + qd * _dot(do, R, "nt")
            dk = _dot(ds, q, "tn") + kd * _dot(v, dR, "nt")
            dv_ref[rows, :] = (_dot(s, do, "tn") + _dot(k * kd, dR)).astype(dv_ref.dtype)
            dr_ref[...] = cd_ref[...] * dR + _dot(q * qd, do, "tn")
            cs, sn = cos_ref[rows, :], sin_ref[rows, :]
            dq_ref[rows, :] = _rope_t(dq, cs, sn).astype(dq_ref.dtype)
            dk_ref[rows, :] = _rope_t(dk * qk_scale, cs, sn).astype(dk_ref.dtype)
            return carry

        lax.fori_loop(0, N, back, 0)

    T = B * S
    qk_shape = jax.ShapeDtypeStruct((T, RET_HEADS * RET_QK_DIM), BF16)
    v_shape = jax.ShapeDtypeStruct((T, RET_HEADS * RET_V_DIM), BF16)
    vec_shape = jax.ShapeDtypeStruct((1, RET_HEADS * RET_V_DIM), F32)
    return _pcall(
        body, (proj, proj, proj, proj, d_yr, cos_t, sin_t, decay, q_decay, k_decay, c_decay, g_ret, b_ret),
        name="ret_bwd", out_shape=(qk_shape, qk_shape, v_shape, v_shape, vec_shape, vec_shape), grid=(RET_HEADS, B),
        in_specs=[sp["q"], sp["k"], sp["v"], sp["gr"], sp["out_v"], sp["table"], sp["table"], sp["decay"],
                  sp["qd"], sp["kd"], sp["cd"], sp["vec"], sp["vec"]],
        out_specs=(sp["out_qk"], sp["out_qk"], sp["out_v"], sp["out_v"], sp["vec"], sp["vec"]),
        scratch_shapes=[pltpu.VMEM((S, RET_QK_DIM), F32), pltpu.VMEM((S, RET_QK_DIM), F32),
                        pltpu.VMEM((N, RET_QK_DIM, RET_V_DIM), F32), pltpu.VMEM((RET_QK_DIM, RET_V_DIM), F32)],
        sem=("parallel", "arbitrary"), comm=comm)


def _xa_rows(S):
    return _tile(S, 256)


def _xa_specs(S, M):
    q = pl.BlockSpec((S, XA_HEAD_DIM), lambda b, h: (b, COL_QX // XA_HEAD_DIM + h))
    k = pl.BlockSpec((M, XA_HEAD_DIM), lambda b, h: (b, h))
    v = pl.BlockSpec((M, XA_HEAD_DIM), lambda b, h: (b, XA_HEADS + h))
    o = pl.BlockSpec((S, XA_HEAD_DIM), lambda b, h: (b, h))
    return q, k, v, o


def _softmax_rows(s):
    e = jnp.exp(s - jnp.max(s, axis=-1, keepdims=True))
    return e / jnp.sum(e, axis=-1, keepdims=True)


def _xa_fwd(proj, kv, B, S, M):
    CH = _xa_rows(S)
    q_spec, k_spec, v_spec, o_spec = _xa_specs(S, M)

    def body(q_ref, k_ref, v_ref, o_ref):
        def chunk(i, carry):
            rows = pl.ds(pl.multiple_of(i * CH, CH), CH)
            p = _softmax_rows(_dot(q_ref[rows, :], k_ref[...], "nt") * (XA_HEAD_DIM ** -0.5))
            o_ref[rows, :] = _dot(p, v_ref[...]).astype(o_ref.dtype)
            return carry

        lax.fori_loop(0, S // CH, chunk, 0)

    return pl.pallas_call(
        body, out_shape=jax.ShapeDtypeStruct((B * S, XA_WIDTH), BF16), grid=(B, XA_HEADS),
        in_specs=[q_spec, k_spec, v_spec], out_specs=o_spec, name="xattn_fwd",
        compiler_params=_params(("parallel", "parallel")),
    )(proj, kv, kv)


def _xa_bwd(proj, kv, d_o, B, S, M):
    CH = _xa_rows(S)
    q_spec, k_spec, v_spec, o_spec = _xa_specs(S, M)
    scale = XA_HEAD_DIM ** -0.5

    def body(q_ref, k_ref, v_ref, do_ref, dq_ref, dk_ref, dv_ref):
        dk_ref[...] = jnp.zeros_like(dk_ref)
        dv_ref[...] = jnp.zeros_like(dv_ref)

        def chunk(i, carry):
            rows = pl.ds(pl.multiple_of(i * CH, CH), CH)
            q, do = q_ref[rows, :], do_ref[rows, :]
            p = _softmax_rows(_dot(q, k_ref[...], "nt") * scale)
            dp = _dot(do, v_ref[...], "nt")
            ds = p * (dp - jnp.sum(dp * p, axis=-1, keepdims=True)) * scale
            dq_ref[rows, :] = _dot(ds, k_ref[...]).astype(dq_ref.dtype)
            dk_ref[...] += _dot(ds, q, "tn")
            dv_ref[...] += _dot(p, do, "tn")
            return carry

        lax.fori_loop(0, S // CH, chunk, 0)

    kv_out = pl.BlockSpec((M, XA_HEAD_DIM), lambda b, h: (b, h))
    return pl.pallas_call(
        body,
        out_shape=(jax.ShapeDtypeStruct((B * S, XA_WIDTH), BF16), jax.ShapeDtypeStruct((B * M, XA_WIDTH), F32),
                   jax.ShapeDtypeStruct((B * M, XA_WIDTH), F32)),
        grid=(B, XA_HEADS), in_specs=[q_spec, k_spec, v_spec, o_spec], out_specs=(o_spec, kv_out, kv_out),
        name="xattn_bwd", compiler_params=_params(("parallel", "parallel")),
    )(proj, kv, kv, d_o)


def _gate_specs(tm):
    n = COL_GL // D_MODEL
    return [pl.BlockSpec((tm, D_MODEL), lambda i, j=j: (i, n + j)) for j in range(3)]


def _merge_fwd(proj, ys, tm=256):
    T = proj.shape[0]
    tm = _tile(T, tm)
    row = pl.BlockSpec((tm, D_MODEL), lambda i: (i, 0))

    def body(g0, g1, g2, y0, y1, y2, o_ref):
        acc = jax.nn.sigmoid(g0[...]) * y0[...]
        acc = acc + jax.nn.sigmoid(g1[...]) * y1[...]
        acc = acc + jax.nn.sigmoid(g2[...]) * y2[...]
        o_ref[...] = acc.astype(o_ref.dtype)

    return pl.pallas_call(
        body, out_shape=jax.ShapeDtypeStruct((T, D_MODEL), BF16), grid=(T // tm,),
        in_specs=_gate_specs(tm) + [row] * 3, out_specs=row, name="merge_fwd",
        compiler_params=_params(("parallel",)),
    )(proj, proj, proj, *ys)


def _merge_bwd(proj, ys, d_merged, tm=256, comm=None):
    T = proj.shape[0]
    tm = _tile(T, tm)
    row = pl.BlockSpec((tm, D_MODEL), lambda i: (i, 0))

    def body(g0, g1, g2, y0, y1, y2, dm_ref, dgl_ref, d0, d1, d2):
        dm = dm_ref[...]
        for j, (g_ref, y_ref, d_ref) in enumerate(((g0, y0, d0), (g1, y1, d1), (g2, y2, d2))):
            sg = jax.nn.sigmoid(g_ref[...])
            d_ref[...] = (dm * sg).astype(d_ref.dtype)
            dgl_ref[:, j * D_MODEL:(j + 1) * D_MODEL] = (dm * y_ref[...] * sg * (1.0 - sg)).astype(dgl_ref.dtype)

    dy = jax.ShapeDtypeStruct((T, D_MODEL), BF16)
    return _pcall(
        body, (proj, proj, proj, *ys, d_merged), name="merge_bwd",
        out_shape=(jax.ShapeDtypeStruct((T, 3 * D_MODEL), BF16), dy, dy, dy), grid=(T // tm,),
        in_specs=_gate_specs(tm) + [row] * 4,
        out_specs=(pl.BlockSpec((tm, 3 * D_MODEL), lambda i: (i, 0)), row, row, row),
        sem=("parallel",), comm=comm)


def _gelu(x):
    return 0.5 * x * (1.0 + jnp.tanh(GELU_C * (x + GELU_A * x * x * x)))


def _gelu_grad(x):
    t = jnp.tanh(GELU_C * (x + GELU_A * x * x * x))
    return 0.5 * (1.0 + t) + 0.5 * x * (1.0 - t * t) * GELU_C * (1.0 + 3.0 * GELU_A * x * x)


def _shift_down(x, prev, n):
    rows = x.shape[0]
    r = lax.broadcasted_iota(jnp.int32, (rows, 1), 0)
    out = pltpu.roll(x, n, axis=0)
    for j in range(n):
        out = jnp.where(r == j, prev[8 - n + j:8 - n + j + 1, :], out)
    return out


def _shift_up(x, nxt, n):
    rows = x.shape[0]
    r = lax.broadcasted_iota(jnp.int32, (rows, 1), 0)
    out = pltpu.roll(x, rows - n, axis=0)
    for j in range(n):
        out = jnp.where(r == rows - n + j, nxt[j:j + 1, :], out)
    return out


def _conv(a, prev, cw, cb):
    return _shift_down(a, prev, 2) * cw[0:1, :] + _shift_down(a, prev, 1) * cw[1:2, :] + a * cw[2:3, :] + cb


def _glu_fwd(up, cw, cb, S, tm=256):
    T = up.shape[2]
    tm = _tile(S, tm)
    per_seq = S // tm

    def body(ab_ref, prev_ref, cw_ref, cb_ref, u_ref):
        i = pl.program_id(1)
        prev = jnp.where(i % per_seq == 0, 0.0, prev_ref[...])
        ac = _conv(ab_ref[0], prev, cw_ref[...], cb_ref[...])
        u_ref[...] = (_gelu(ac) * ab_ref[1]).astype(u_ref.dtype)

    return pl.pallas_call(
        body, out_shape=jax.ShapeDtypeStruct((FFN_SLABS, T, UP_SHARD), BF16), grid=(FFN_SLABS, T // tm),
        in_specs=[pl.BlockSpec((2, None, tm, UP_SHARD), lambda d, i: (0, d, i, 0)),
                  pl.BlockSpec((None, None, 8, UP_SHARD), lambda d, i: (0, d, jnp.maximum(i * (tm // 8) - 1, 0), 0)),
                  pl.BlockSpec((None, 3, UP_SHARD), lambda d, i: (d, 0, 0)),
                  pl.BlockSpec((None, 1, UP_SHARD), lambda d, i: (d, 0, 0))],
        out_specs=pl.BlockSpec((None, tm, UP_SHARD), lambda d, i: (d, i, 0)), name="glu_fwd",
        compiler_params=_params(("parallel", "parallel")),
    )(up, up, cw, cb)


def _glu_bwd(up, d_u, cw, cb, S, tm=256, comm=None):
    T = up.shape[2]
    tm = _tile(S, tm)
    per_seq = S // tm
    n_tiles = T // tm
    last8 = tm // 8

    def body(ab_ref, prev_ref, abn_ref, du_ref, dun_ref, cw_ref, cb_ref, dup_ref, dcw_ref, dcb_ref):
        i = pl.program_id(1)

        @pl.when(i == 0)
        def _():
            dcw_ref[...] = jnp.zeros_like(dcw_ref)
            dcb_ref[...] = jnp.zeros_like(dcb_ref)

        cw, cb = cw_ref[...], cb_ref[...]
        a, b = ab_ref[0], ab_ref[1]
        prev = jnp.where(i % per_seq == 0, 0.0, prev_ref[...])
        a2, a1 = _shift_down(a, prev, 2), _shift_down(a, prev, 1)
        ac = a2 * cw[0:1, :] + a1 * cw[1:2, :] + a * cw[2:3, :] + cb
        du = du_ref[...]
        dup_ref[1] = (du * _gelu(ac)).astype(dup_ref.dtype)
        dac = du * b * _gelu_grad(ac)
        dcb_ref[...] += jnp.sum(dac, axis=0, keepdims=True)
        dcw_ref[0:1, :] += jnp.sum(dac * a2, axis=0, keepdims=True)
        dcw_ref[1:2, :] += jnp.sum(dac * a1, axis=0, keepdims=True)
        dcw_ref[2:3, :] += jnp.sum(dac * a, axis=0, keepdims=True)
        an = abn_ref[0]
        acn = _conv(an, a[tm - 8:, :], cw, cb)
        dacn = jnp.where(i % per_seq == per_seq - 1, 0.0, dun_ref[...] * abn_ref[1] * _gelu_grad(acn))
        da = dac * cw[2:3, :] + _shift_up(dac, dacn, 1) * cw[1:2, :] + _shift_up(dac, dacn, 2) * cw[0:1, :]
        dup_ref[0] = da.astype(dup_ref.dtype)

    def nxt(i):
        return jnp.minimum((i + 1) * last8, T // 8 - 1)

    return _pcall(
        body, (up, up, up, d_u, d_u, cw, cb), name="glu_bwd",
        out_shape=(jax.ShapeDtypeStruct((2, FFN_SLABS, T, UP_SHARD), BF16),
                   jax.ShapeDtypeStruct((FFN_SLABS, 3, UP_SHARD), F32),
                   jax.ShapeDtypeStruct((FFN_SLABS, 1, UP_SHARD), F32)),
        grid=(FFN_SLABS, n_tiles),
        in_specs=[pl.BlockSpec((2, None, tm, UP_SHARD), lambda d, i: (0, d, i, 0)),
                  pl.BlockSpec((None, None, 8, UP_SHARD), lambda d, i: (0, d, jnp.maximum(i * last8 - 1, 0), 0)),
                  pl.BlockSpec((2, None, 8, UP_SHARD), lambda d, i: (0, d, nxt(i), 0)),
                  pl.BlockSpec((None, tm, UP_SHARD), lambda d, i: (d, i, 0)),
                  pl.BlockSpec((None, 8, UP_SHARD), lambda d, i: (d, nxt(i), 0)),
                  pl.BlockSpec((None, 3, UP_SHARD), lambda d, i: (d, 0, 0)),
                  pl.BlockSpec((None, 1, UP_SHARD), lambda d, i: (d, 0, 0))],
        out_specs=(pl.BlockSpec((2, None, tm, UP_SHARD), lambda d, i: (0, d, i, 0)),
                   pl.BlockSpec((None, 3, UP_SHARD), lambda d, i: (d, 0, 0)),
                   pl.BlockSpec((None, 1, UP_SHARD), lambda d, i: (d, 0, 0))),
        sem=("parallel", "arbitrary"), comm=comm)


def _mm_up(h2, w_up, tm=512):
    T, K = h2.shape
    tm = _tile(T, tm)
    return _matmul(
        "mm_up", "nn", h2, w_up, jax.ShapeDtypeStruct((N_DEV, T, UP_SHARD), F32), (N_DEV, T // tm, 1),
        pl.BlockSpec((tm, K), lambda j, i, k: (i, 0)), pl.BlockSpec((None, K, UP_SHARD), lambda j, i, k: (j, 0, 0)),
        pl.BlockSpec((None, tm, UP_SHARD), lambda j, i, k: (j, i, 0)), (tm, UP_SHARD))


def _mm_down(u, w_down, res, tm=512):
    J, T, n = u.shape
    tm = _tile(T, tm)
    row = pl.BlockSpec((tm, D_MODEL), lambda i, d: (i, 0))
    return _matmul(
        "mm_down", "nn", u, w_down, jax.ShapeDtypeStruct((T, D_MODEL), F32), (T // tm, J),
        pl.BlockSpec((None, tm, n), lambda i, d: (d, i, 0)), pl.BlockSpec((None, n, D_MODEL), lambda i, d: (d, 0, 0)),
        row, (tm, D_MODEL), res, row)


def _mm_down_t(dx, w_down, tm=512):
    T = dx.shape[0]
    J, n, _ = w_down.shape
    tm = _tile(T, tm)
    return _matmul(
        "mm_down_t", "nt", dx, w_down, jax.ShapeDtypeStruct((J, T, n), F32), (J, T // tm, 1),
        pl.BlockSpec((tm, D_MODEL), lambda d, i, k: (i, 0)), pl.BlockSpec((None, n, D_MODEL), lambda d, i, k: (d, 0, 0)),
        pl.BlockSpec((None, tm, n), lambda d, i, k: (d, i, 0)), (tm, n))


def _mm_dw_down(u, dx, tk=512):
    J, T, n = u.shape
    tk = _tile(T, tk)
    return _matmul(
        "mm_dw_down", "tn", u, dx, jax.ShapeDtypeStruct((J, n, D_MODEL), BF16), (J, T // tk),
        pl.BlockSpec((None, tk, n), lambda d, k: (d, k, 0)), pl.BlockSpec((tk, D_MODEL), lambda d, k: (k, 0)),
        pl.BlockSpec((None, n, D_MODEL), lambda d, k: (d, 0, 0)), (n, D_MODEL))


def _mm_dw_up(h2, d_up, tk=512):
    T, K = h2.shape
    tk = _tile(T, tk)
    return _matmul(
        "mm_dw_up", "tn", h2, d_up, jax.ShapeDtypeStruct((N_DEV, K, UP_SHARD), BF16), (N_DEV, T // tk),
        pl.BlockSpec((tk, K), lambda j, k: (k, 0)), pl.BlockSpec((None, tk, UP_SHARD), lambda j, k: (j, k, 0)),
        pl.BlockSpec((None, K, UP_SHARD), lambda j, k: (j, 0, 0)), (K, UP_SHARD))


def _mm_up_t(d_up, w_up, tm=512, comm=None):
    J, T, n = d_up.shape
    K = w_up.shape[1]
    tm = _tile(T, tm)
    return _matmul(
        "mm_up_t", "nt", d_up, w_up, jax.ShapeDtypeStruct((T, K), F32), (T // tm, J),
        pl.BlockSpec((None, tm, n), lambda i, j: (j, i, 0)), pl.BlockSpec((None, K, n), lambda i, j: (j, 0, 0)),
        pl.BlockSpec((tm, K), lambda i, j: (i, 0)), (tm, K), comm=comm)


def _loss_head(x2, target, g_final, tm=512):
    T, Dm = x2.shape
    tm = _tile(T, tm)

    def body(x_ref, t_ref, g_ref, dx_ref, dg_ref, loss_ref):
        @pl.when(pl.program_id(0) == 0)
        def _():
            dg_ref[...] = jnp.zeros_like(dg_ref)
            loss_ref[...] = jnp.zeros_like(loss_ref)

        xv = x_ref[...]
        r = lax.rsqrt(jnp.mean(xv * xv, axis=-1, keepdims=True) + EPS)
        xhat = xv * r
        err = xhat * g_ref[...] - t_ref[...]
        loss_ref[...] += (0.5 / Dm) * jnp.sum(err * err)
        dy = err * (1.0 / Dm)
        dg_ref[...] += jnp.sum(dy * xhat, axis=0, keepdims=True)
        dxhat = dy * g_ref[...]
        dx_ref[...] = r * (dxhat - xhat * jnp.mean(dxhat * xhat, axis=-1, keepdims=True))

    row = pl.BlockSpec((tm, Dm), lambda i: (i, 0))
    vec = pl.BlockSpec((1, Dm), lambda i: (0, 0))
    return pl.pallas_call(
        body,
        out_shape=(jax.ShapeDtypeStruct((T, Dm), F32), jax.ShapeDtypeStruct((1, Dm), F32),
                   jax.ShapeDtypeStruct((1, Dm), F32)),
        grid=(T // tm,), in_specs=[row, row, vec], out_specs=(row, vec, vec), name="loss_head",
        compiler_params=_params(("arbitrary",)),
    )(x2, target, g_final)


def _cast_shards(shards):
    def body(*refs):
        n = len(refs) // 2
        for src, dst in zip(refs[:n], refs[n:]):
            dst[...] = src[...].astype(dst.dtype)

    return pl.pallas_call(
        body, out_shape=[jax.ShapeDtypeStruct(s.shape, BF16) for s in shards], name="cast_shards",
        compiler_params=pltpu.CompilerParams(vmem_limit_bytes=VMEM_LIMIT),
    )(*shards)


def _adamw(w, g, m, v):
    m = ADAM_B1 * m + (1.0 - ADAM_B1) * g
    v = ADAM_B2 * v + (1.0 - ADAM_B2) * (g * g)
    m_hat = m / (1.0 - ADAM_B1 ** ADAM_STEP)
    v_hat = v / (1.0 - ADAM_B2 ** ADAM_STEP)
    delta = -ADAM_LR * (m_hat / (jnp.sqrt(v_hat) + ADAM_EPS) + ADAM_WD * w)
    return delta, m, v


def _reduce_adam(name, parts, w, m, v, tr=128):
    R, Cn = w.shape
    tr = _tile(R, tr) if R % 8 == 0 else R
    if R // tr * tr != R or (tr % 8 and tr != R):
        tr = R

    def body(p_ref, w_ref, m_ref, v_ref, g_out, d_out, m_out, v_out):
        g = p_ref[0].astype(F32)
        for d in range(1, N_DEV):
            g = g + p_ref[d].astype(F32)
        delta, m_new, v_new = _adamw(w_ref[...], g, m_ref[...], v_ref[...])
        g_out[...] = g
        d_out[...] = delta
        m_out[...] = m_new
        v_out[...] = v_new

    row = pl.BlockSpec((tr, Cn), lambda i: (i, 0))
    shape = jax.ShapeDtypeStruct((R, Cn), F32)
    return pl.pallas_call(
        body, out_shape=(shape,) * 4, grid=(R // tr,),
        in_specs=[pl.BlockSpec((N_DEV, tr, Cn), lambda i: (0, i, 0)), row, row, row], out_specs=(row,) * 4,
        name=name, compiler_params=_params(("parallel",)),
    )(parts, w, m, v)


def _adam_only(name, g, w, m, v):
    def body(g_ref, w_ref, m_ref, v_ref, d_out, m_out, v_out):
        delta, m_new, v_new = _adamw(w_ref[...], g_ref[...], m_ref[...], v_ref[...])
        d_out[...] = delta
        m_out[...] = m_new
        v_out[...] = v_new

    shape = jax.ShapeDtypeStruct(w.shape, F32)
    return pl.pallas_call(body, out_shape=(shape,) * 3, name=name)(g, w, m, v)


def _pack(arrays):
    rows = []
    for a in arrays:
        flat = a.reshape(-1).astype(F32)
        pad = (-flat.shape[0]) % (8 * PACK_LANES)
        rows.append(jnp.pad(flat, (0, pad)).reshape(-1, PACK_LANES))
    return jnp.concatenate(rows, axis=0)


def _unpack(packed, shapes):
    out, r = [], 0
    for shp in shapes:
        size = math.prod(shp)
        nr = 8 * -(-size // (8 * PACK_LANES))
        out.append(packed[r:r + nr].reshape(-1)[:size].reshape(shp))
        r += nr
    return out


def kernel(x, mem, g_mix, w_in, w_pool, pool_scale, w_a, g_ret, b_ret, w_r, g_mem, w_mem_kv, w_c, w_out, g_ffn, w_up, conv_w, conv_b, w_down, g_final, loss_target, m_g_mix, m_w_in, m_w_pool, m_pool_scale, m_w_a, m_g_ret, m_b_ret, m_w_r, m_g_mem, m_w_mem_kv, m_w_c, m_w_out, m_g_ffn, m_w_up, m_conv_w, m_conv_b, m_w_down, m_g_final, v_g_mix, v_w_in, v_w_pool, v_pool_scale, v_w_a, v_g_ret, v_b_ret, v_w_r, v_g_mem, v_w_mem_kv, v_w_c, v_w_out, v_g_ffn, v_w_up, v_conv_w, v_conv_b, v_w_down, v_g_final):
    B, S, _ = x.shape
    M = mem.shape[1]
    T = B * S
    me = _my_index()
    x2d = x.reshape(T, D_MODEL)
    mem2d = mem.reshape(B * M, D_MODEL)
    tgt2d = loss_target.reshape(T, D_MODEL)
    g_final2 = g_final.reshape(1, D_MODEL)

    big = dict(w_in=w_in[0], w_a=w_a[0], w_r=w_r[0], w_mem_kv=w_mem_kv[0], w_c=w_c[0], w_out=w_out[0],
               w_up=w_up[0], w_down=w_down[0])
    names = list(big)
    cast = dict(zip(names, _cast_shards([big[n] for n in names])))
    Win, cw_gathered = _comm_call("gather_w_in", _Gather([cast["w_in"], conv_w[0]]))
    cw_full = cw_gathered.transpose(1, 0, 2).reshape(3, FFN_HIDDEN)
    cw = cw_full.reshape(3, FFN_SLABS, UP_SHARD).transpose(1, 0, 2)
    cb = conv_b[0].reshape(FFN_SLABS, 1, UP_SHARD)
    wp = w_pool[0]
    tables = _ret_tables(S)

    h = _rms_fwd("rms_mix", x2d, g_mix)
    early = ("w_a", "w_r", "w_mem_kv", "w_c", "w_out", "w_down")
    proj, landed = _mm_cols_slab("mm_in", h, Win, comm=_Gather([cast[n] for n in early]))
    W = dict(zip(early, landed))
    Wa = W["w_a"].transpose(1, 0, 2).reshape(POOL_WIDTH, D_MODEL)
    Wc = W["w_c"].transpose(1, 0, 2).reshape(XA_WIDTH, D_MODEL)
    Wr = W["w_r"].reshape(D_MODEL, D_MODEL)
    Wkv = W["w_mem_kv"].reshape(D_MODEL, D_MODEL)
    Wout = W["w_out"].reshape(D_MODEL, D_MODEL)
    Wdown = W["w_down"].reshape(FFN_SLABS, UP_SHARD, D_MODEL)
    ypre = _pool_fwd(proj, wp, pool_scale, B, S)
    y_pool = _mm_rows("mm_a", ypre, Wa)
    yr, (Wup,) = _ret_fwd(proj, g_ret, b_ret, tables, B, S, comm=_Gather([cast["w_up"]]))
    y_ret = _mm_rows("mm_r", yr, Wr)
    mem_n = _rms_fwd("rms_mem", mem2d, g_mem)
    kv = _mm_rows("mm_kv", mem_n, Wkv)
    o_mem = _xa_fwd(proj, kv, B, S, M)
    y_mem = _mm_rows("mm_c", o_mem, Wc)
    ys = (y_pool, y_ret, y_mem)
    merged = _merge_fwd(proj, ys)
    x1 = _mm_rows("mm_out", merged, Wout, res=x2d)
    h2 = _rms_fwd("rms_ffn", x1, g_ffn)
    up = _mm_up(h2, Wup).reshape(2, FFN_SLABS, T, UP_SHARD)
    u = _glu_fwd(up, cw, cb, S)
    x2 = _mm_down(u, Wdown, x1)

    dx2, dg_final, loss_part = _loss_head(x2, tgt2d, g_final2)
    received = {}
    d_u = _mm_down_t(dx2, Wdown)
    dW_down = _mm_dw_down(u, dx2)
    (d_up, d_cw, d_cb), (received["w_down"],) = _glu_bwd(
        up, d_u, cw, cb, S, comm=_Exchange([dW_down.reshape(N_DEV, -1, D_MODEL)]))
    d_up = d_up.reshape(N_DEV, T, UP_SHARD)
    dW_up = _mm_dw_up(h2, d_up)
    d_h2, (received["w_up"],) = _mm_up_t(d_up, Wup, comm=_Exchange([dW_up]))
    dx1, dg_ffn = _rms_bwd("rms_ffn_bwd", x1, g_ffn, d_h2, dx2)
    d_merged = _mm_rows("mm_out_t", dx1, Wout, kind="nt")
    dW_out = _mm_tn("mm_dw_out", merged, dx1, BF16)
    (d_gl, d_y_pool, d_y_ret, d_y_mem), (received["w_out"],) = _merge_bwd(
        proj, ys, d_merged, comm=_Exchange([dW_out.reshape(N_DEV, -1, D_MODEL)]))
    dW_a = _mm_tn("mm_dw_a", ypre, d_y_pool, BF16)
    d_ypre = _mm_rows("mm_a_t", d_y_pool, Wa, kind="nt")
    d_hp, dw_pool, d_scale = _pool_bwd(proj, d_ypre, wp, pool_scale, B, S)
    dW_r = _mm_tn("mm_dw_r", yr, d_y_ret, BF16)
    d_yr = _mm_rows("mm_r_t", d_y_ret, Wr, kind="nt")
    (d_q, d_k, d_v, d_gr, dg_ret, db_ret), (received["w_a"], received["w_r"]) = _ret_bwd(
        proj, d_yr, g_ret, b_ret, tables, B, S,
        comm=_Exchange([dW_a.reshape(POOL_WIDTH, N_DEV, -1).transpose(1, 0, 2), dW_r.reshape(N_DEV, -1, D_MODEL)]))
    dW_c = _mm_tn("mm_dw_c", o_mem, d_y_mem, BF16)
    d_o_mem = _mm_rows("mm_c_t", d_y_mem, Wc, kind="nt")
    d_qx, d_kmem, d_vmem = _xa_bwd(proj, kv, d_o_mem, B, S, M)
    d_kv = jnp.concatenate([d_kmem, d_vmem], axis=1)
    dW_kv = _mm_tn("mm_dw_kv", mem_n, d_kv, BF16)
    d_mem_n = _mm_rows("mm_kv_t", d_kv, Wkv, kind="nt")
    dg_mem = _rms_bwd("rms_mem_bwd", mem2d, g_mem, d_mem_n, None)
    d_cw_full = d_cw.transpose(1, 0, 2).reshape(3, FFN_HIDDEN)
    small_names = ["w_pool", "pool_scale", "g_ret", "b_ret", "g_mem", "g_ffn", "conv_b", "g_final"]
    small_grads = [dw_pool, d_scale, dg_ret, db_ret, dg_mem, dg_ffn, d_cb.reshape(1, FFN_HIDDEN), dg_final]
    packed = _pack(small_grads + [d_cw_full, loss_part])
    d_proj = jnp.concatenate([d_hp, d_q, d_k, d_v, d_gr, d_qx, d_gl], axis=1)
    dW_in, (received["w_c"], received["w_mem_kv"], packed_all) = _mm_tn_slab(
        "mm_dw_in", h, d_proj, IN_SHARD, BF16,
        comm=_Exchange([dW_c.reshape(XA_WIDTH, N_DEV, -1).transpose(1, 0, 2), dW_kv.reshape(N_DEV, -1, D_MODEL)],
                       whole=[packed]))
    d_h, (received["w_in"],) = _mm_cols_slab_t("mm_in_t", d_proj, Win, comm=_Exchange([dW_in]))
    grad_x, dg_mix = _rms_bwd("rms_mix_bwd", x2d, g_mix, d_h, dx1)
    (g_mix_all,) = _comm_call("gather_g_mix", _Exchange([], whole=[_pack([dg_mix])]))

    args = dict(g_mix=g_mix, w_in=w_in, w_pool=w_pool, pool_scale=pool_scale, w_a=w_a, g_ret=g_ret, b_ret=b_ret,
                w_r=w_r, g_mem=g_mem, w_mem_kv=w_mem_kv, w_c=w_c, w_out=w_out, g_ffn=g_ffn, w_up=w_up,
                conv_w=conv_w, conv_b=conv_b, w_down=w_down, g_final=g_final)
    m_in = dict(g_mix=m_g_mix, w_in=m_w_in, w_pool=m_w_pool, pool_scale=m_pool_scale, w_a=m_w_a, g_ret=m_g_ret,
                b_ret=m_b_ret, w_r=m_w_r, g_mem=m_g_mem, w_mem_kv=m_w_mem_kv, w_c=m_w_c, w_out=m_w_out,
                g_ffn=m_g_ffn, w_up=m_w_up, conv_w=m_conv_w, conv_b=m_conv_b, w_down=m_w_down, g_final=m_g_final)
    v_in = dict(g_mix=v_g_mix, w_in=v_w_in, w_pool=v_w_pool, pool_scale=v_pool_scale, w_a=v_w_a, g_ret=v_g_ret,
                b_ret=v_b_ret, w_r=v_w_r, g_mem=v_g_mem, w_mem_kv=v_w_mem_kv, w_c=v_w_c, w_out=v_w_out,
                g_ffn=v_g_ffn, w_up=v_w_up, conv_w=v_conv_w, conv_b=v_conv_b, w_down=v_w_down, g_final=v_g_final)

    grads, deltas, new_m, new_v = {}, {}, {}, {}
    for n in names:
        shard = big[n].shape
        outs = _reduce_adam("adam_" + n, received[n], big[n], m_in[n][0], v_in[n][0])
        for store, val in zip((grads, deltas, new_m, new_v), outs):
            store[n] = val.reshape((1,) + shard)

    zeros_tail = [jnp.zeros((3, FFN_HIDDEN), F32), jnp.zeros((1, PACK_LANES), F32)]
    w_small = _pack([args[n] for n in small_names] + zeros_tail)
    m_small = _pack([m_in[n] for n in small_names] + zeros_tail)
    v_small = _pack([v_in[n] for n in small_names] + zeros_tail)
    small_out = _reduce_adam("adam_small", packed_all, w_small, m_small, v_small, tr=w_small.shape[0])
    small_shapes = [args[n].shape for n in small_names] + [(3, FFN_HIDDEN), (1, PACK_LANES)]
    for store, val in zip((grads, deltas, new_m, new_v), small_out):
        parts = _unpack(val, small_shapes)
        for n, p in zip(small_names, parts):
            store[n] = p
        if store is grads:
            g_cw_full, loss_row = parts[-2], parts[-1]
    loss = loss_row[0, 0]
    g_mix_out = _reduce_adam("adam_g_mix", g_mix_all, _pack([g_mix]), _pack([m_g_mix]), _pack([v_g_mix]), tr=8)
    for store, val in zip((grads, deltas, new_m, new_v), g_mix_out):
        store["g_mix"] = _unpack(val, [g_mix.shape])[0]

    shard_cols = FFN_HIDDEN // N_DEV
    g_cw = lax.dynamic_slice_in_dim(g_cw_full, me * shard_cols, shard_cols, axis=1)
    d_, m_, v_ = _adam_only("adam_conv_w", g_cw, conv_w[0], m_conv_w[0], v_conv_w[0])
    grads["conv_w"], deltas["conv_w"], new_m["conv_w"], new_v["conv_w"] = g_cw[None], d_[None], m_[None], v_[None]

    order = ["g_mix", "w_in", "w_pool", "pool_scale", "w_a", "g_ret", "b_ret", "w_r", "g_mem", "w_mem_kv", "w_c",
             "w_out", "g_ffn", "w_up", "conv_w", "conv_b", "w_down", "g_final"]
    return (loss, grad_x.reshape(B, S, D_MODEL), *[grads[n] for n in order], *[deltas[n] for n in order],
            *[new_m[n] for n in order], *[new_v[n] for n in order])
```

```python
import functools
import math

import jax
import jax.numpy as jnp
from jax import lax
from jax.experimental import pallas as pl
from jax.experimental.pallas import tpu as pltpu

F32 = jnp.float32
BF16 = jnp.bfloat16

N_DEV = 8
D_MODEL = 1024
POOL_WINDOWS = (2, 4, 8, 16)
POOL_GROUP_DIM = 128
POOL_WIDTH = 512
POOL_HALO = 16
RET_HEADS = 4
RET_QK_DIM = 128
RET_V_DIM = 256
RET_CHUNK = 128
ROPE_BASE = 10000.0
XA_HEADS = 4
XA_HEAD_DIM = 128
XA_WIDTH = 512
IN_WIDTH = 7168
IN_SHARD = IN_WIDTH // N_DEV
FFN_HIDDEN = 2816
UP_SHARD = 2 * FFN_HIDDEN // N_DEV
FFN_SLABS = FFN_HIDDEN // UP_SHARD
EPS = 1e-6
ADAM_LR = 0.001
ADAM_B1 = 0.9
ADAM_B2 = 0.999
ADAM_EPS = 1e-08
ADAM_WD = 0.01
ADAM_STEP = 10
GELU_C = math.sqrt(2.0 / math.pi)
GELU_A = 0.044715
VMEM_LIMIT = 56 * 1024 * 1024
PACK_LANES = 1024
MM_ROWS = 1024
MM_TOKENS = 1024
MESH = pl.DeviceIdType.MESH

COL_Q, COL_K, COL_V, COL_GR, COL_QX, COL_GL = 512, 1024, 1536, 2560, 3584, 4096

_DIMS = {
    "nn": (((1,), (0,)), ((), ())),
    "nt": (((1,), (1,)), ((), ())),
    "tn": (((0,), (0,)), ((), ())),
}


def _dot(a, b, kind="nn"):
    return lax.dot_general(a.astype(BF16), b.astype(BF16), _DIMS[kind], preferred_element_type=F32)


def _params(sem, vmem=VMEM_LIMIT):
    return pltpu.CompilerParams(dimension_semantics=sem, vmem_limit_bytes=vmem)


def _tile(n, pref):
    t = min(n, pref)
    while n % t:
        t //= 2
    return t


def _mesh_pos():
    return lax.axis_index("x"), lax.axis_index("y"), lax.axis_index("c")


def _dev_index(x, y, c):
    return 4 * x + 2 * y + c


def _my_index():
    return _dev_index(*_mesh_pos())


def _remote(src, dst, send_sems, recv_sems, s, to):
    return pltpu.make_async_remote_copy(src_ref=src, dst_ref=dst, send_sem=send_sems.at[s], recv_sem=recv_sems.at[s],
                                        device_id=to, device_id_type=MESH)


class _Gather:
    def __init__(self, shards):
        self.inputs = list(shards)
        self.out_shapes = [jax.ShapeDtypeStruct((N_DEV,) + s.shape, s.dtype) for s in shards]
        n = len(shards)
        self.sem_shapes = [pltpu.SemaphoreType.DMA((7 * n,)), pltpu.SemaphoreType.DMA((7 * n,)),
                           pltpu.SemaphoreType.DMA((n,))]

    def _places(self):
        x, y, c = _mesh_pos()
        return (x, y, c), (x, y, 1 - c), [(1 - x, y), (x, 1 - y), (1 - x, 1 - y)]

    def _local(self, src, dst, sems):
        me = _my_index()
        return [pltpu.make_async_copy(src[w], dst[w].at[me], sems[2].at[w]) for w in range(len(src))]

    def start(self, src, dst, sems):
        me, sib, chips = self._places()
        for cp in self._local(src, dst, sems):
            cp.start()
        for w in range(len(src)):
            land = dst[w].at[_dev_index(*me)]
            _remote(src[w], land, sems[0], sems[1], 7 * w, sib).start()
            for j, chip in enumerate(chips):
                _remote(src[w], land, sems[0], sems[1], 7 * w + 1 + j, (*chip, me[2])).start()

    def finish(self, src, dst, sems):
        me, sib, chips = self._places()
        n = len(src)
        for j, chip in enumerate(chips):
            for w in range(n):
                block = dst[w].at[_dev_index(*chip, me[2])]
                _remote(src[w], block, sems[0], sems[1], 7 * w + 1 + j, me).wait_recv()
                _remote(block, block, sems[0], sems[1], 7 * w + 4 + j, sib).start()
        for w in range(n):
            _remote(src[w], dst[w].at[_dev_index(*sib)], sems[0], sems[1], 7 * w, me).wait_recv()
            for j, chip in enumerate(chips):
                block = dst[w].at[_dev_index(*chip, sib[2])]
                _remote(block, block, sems[0], sems[1], 7 * w + 4 + j, me).wait_recv()
            for k in range(7):
                _remote(src[w], dst[w].at[0], sems[0], sems[1], 7 * w + k, me).wait_send()
        for cp in self._local(src, dst, sems):
            cp.wait()


class _Exchange:
    def __init__(self, partials, whole=()):
        self.n_part = len(partials)
        self.inputs = list(partials) + list(whole)
        self.out_shapes = [jax.ShapeDtypeStruct(p.shape, p.dtype) for p in partials]
        self.out_shapes += [jax.ShapeDtypeStruct((N_DEV,) + a.shape, a.dtype) for a in whole]
        n = len(self.inputs)
        self.sem_shapes = [pltpu.SemaphoreType.DMA((7 * n,)), pltpu.SemaphoreType.DMA((7 * n,)),
                           pltpu.SemaphoreType.DMA((n,))]

    def _peer(self, k):
        x, y, c = _mesh_pos()
        p = (x ^ ((k >> 2) & 1), y ^ ((k >> 1) & 1), c ^ (k & 1))
        return p, _dev_index(*p)

    def _source(self, src, w, slot):
        return src[w].at[slot] if w < self.n_part else src[w]

    def _local(self, src, dst, sems):
        me = _my_index()
        return [pltpu.make_async_copy(self._source(src, w, me), dst[w].at[me], sems[2].at[w])
                for w in range(len(src))]

    def start(self, src, dst, sems):
        me = _my_index()
        for cp in self._local(src, dst, sems):
            cp.start()
        for k in range(1, N_DEV):
            peer, peer_idx = self._peer(k)
            for w in range(len(src)):
                _remote(self._source(src, w, peer_idx), dst[w].at[me], sems[0], sems[1], 7 * w + k - 1, peer).start()

    def finish(self, src, dst, sems):
        for k in range(1, N_DEV):
            peer, peer_idx = self._peer(k)
            for w in range(len(src)):
                cp = _remote(self._source(src, w, peer_idx), dst[w].at[peer_idx], sems[0], sems[1], 7 * w + k - 1, peer)
                cp.wait_send()
                cp.wait_recv()
        for cp in self._local(src, dst, sems):
            cp.wait()


def _pcall(body, args, *, name, out_shape, grid, in_specs, out_specs, scratch_shapes=(), sem=None, comm=None):
    single = not isinstance(out_shape, (tuple, list))
    outs = [out_shape] if single else list(out_shape)
    ospecs = [out_specs] if single else list(out_specs)
    n_in, n_out, n_scr = len(args), len(outs), len(scratch_shapes)

    def pick(res):
        return res[0] if single else tuple(res[:n_out])

    if comm is None:
        res = pl.pallas_call(
            body, out_shape=outs, grid=grid, in_specs=list(in_specs), out_specs=ospecs,
            scratch_shapes=list(scratch_shapes), name=name, compiler_params=_params(sem),
        )(*args)
        return pick(res), ()

    nci, nco = len(comm.inputs), len(comm.out_shapes)

    def carrier(*refs):
        at = 0
        parts = []
        for size in (n_in, nci, n_out, nco, n_scr, 3):
            parts.append(refs[at:at + size])
            at += size
        ins, cins, o, couts, scr, sems = parts
        ids = [pl.program_id(a) for a in range(len(grid))]
        first = functools.reduce(jnp.logical_and, [i == 0 for i in ids])
        last = functools.reduce(jnp.logical_and, [i == g - 1 for i, g in zip(ids, grid)])

        @pl.when(first)
        def _():
            comm.start(cins, couts, sems)

        body(*ins, *o, *scr)

        @pl.when(last)
        def _():
            comm.finish(cins, couts, sems)

    hbm = pl.BlockSpec(memory_space=pltpu.HBM)
    res = pl.pallas_call(
        carrier, out_shape=outs + comm.out_shapes, grid=grid, in_specs=list(in_specs) + [hbm] * nci,
        out_specs=ospecs + [hbm] * nco, scratch_shapes=list(scratch_shapes) + comm.sem_shapes, name=name,
        compiler_params=_params(("arbitrary",) * len(grid)),
    )(*args, *comm.inputs)
    return pick(res), tuple(res[n_out:])


def _comm_call(name, comm):
    def body(*refs):
        nci, nco = len(comm.inputs), len(comm.out_shapes)
        cins, couts, sems = refs[:nci], refs[nci:nci + nco], refs[nci + nco:]
        comm.start(cins, couts, sems)
        comm.finish(cins, couts, sems)

    hbm = pl.BlockSpec(memory_space=pltpu.HBM)
    return pl.pallas_call(
        body, out_shape=comm.out_shapes, in_specs=[hbm] * len(comm.inputs), out_specs=[hbm] * len(comm.out_shapes),
        scratch_shapes=comm.sem_shapes, name=name,
    )(*comm.inputs)


def _matmul(name, kind, a, b, out_shape, grid, a_spec, b_spec, o_spec, acc_shape, res=None, res_spec=None,
            comm=None):
    nk = grid[-1]
    has_res = res is not None

    def body(*refs):
        a_ref, b_ref = refs[0], refs[1]
        o_ref = refs[2 + has_res]

        def prod():
            return _dot(a_ref[...], b_ref[...], kind)

        def finish(acc):
            if has_res:
                acc = acc + refs[2][...]
            o_ref[...] = acc.astype(o_ref.dtype)

        if nk == 1:
            finish(prod())
        else:
            acc_ref = refs[3 + has_res]
            k = pl.program_id(len(grid) - 1)

            @pl.when(k == 0)
            def _():
                acc_ref[...] = prod()

            @pl.when(k > 0)
            def _():
                acc_ref[...] += prod()

            @pl.when(k == nk - 1)
            def _():
                finish(acc_ref[...])

    in_specs = [a_spec, b_spec] + ([res_spec] if has_res else [])
    args = (a, b) + ((res,) if has_res else ())
    scratch = [pltpu.VMEM(acc_shape, F32)] if nk > 1 else []
    sem = ("parallel",) * (len(grid) - 1) + ("arbitrary",)
    out, landed = _pcall(body, args, name=name, out_shape=out_shape, grid=grid, in_specs=in_specs,
                         out_specs=o_spec, scratch_shapes=scratch, sem=sem, comm=comm)
    return out if comm is None else (out, landed)


def _mm_rows(name, a, w, out_dtype=F32, res=None, kind="nn", tm=MM_ROWS, comm=None):
    M, K = a.shape
    N = w.shape[1] if kind == "nn" else w.shape[0]
    tm = _tile(M, tm)
    res_spec = pl.BlockSpec((tm, N), lambda i, k: (i, 0)) if res is not None else None
    return _matmul(
        name, kind, a, w, jax.ShapeDtypeStruct((M, N), out_dtype), (M // tm, 1),
        pl.BlockSpec((tm, K), lambda i, k: (i, 0)), pl.BlockSpec(w.shape, lambda i, k: (0, 0)),
        pl.BlockSpec((tm, N), lambda i, k: (i, 0)), (tm, N), res, res_spec, comm)


def _mm_tn(name, a, b, out_dtype=F32, tk=MM_TOKENS, comm=None):
    T, M = a.shape
    N = b.shape[1]
    tk = _tile(T, tk)
    return _matmul(
        name, "tn", a, b, jax.ShapeDtypeStruct((M, N), out_dtype), (1, T // tk),
        pl.BlockSpec((tk, M), lambda i, k: (k, 0)), pl.BlockSpec((tk, N), lambda i, k: (k, 0)),
        pl.BlockSpec((M, N), lambda i, k: (0, 0)), (M, N), comm=comm)


def _mm_cols_slab(name, a, w_slabs, out_dtype=F32, tm=MM_ROWS, comm=None):
    M, K = a.shape
    J, _, n = w_slabs.shape
    tm = _tile(M, tm)
    return _matmul(
        name, "nn", a, w_slabs, jax.ShapeDtypeStruct((M, J * n), out_dtype), (J, M // tm, 1),
        pl.BlockSpec((tm, K), lambda j, i, k: (i, 0)), pl.BlockSpec((None, K, n), lambda j, i, k: (j, 0, 0)),
        pl.BlockSpec((tm, n), lambda j, i, k: (i, j)), (tm, n), comm=comm)


def _mm_cols_slab_t(name, a, w_slabs, out_dtype=F32, tm=MM_ROWS, comm=None):
    M = a.shape[0]
    J, K, n = w_slabs.shape
    tm = _tile(M, tm)
    return _matmul(
        name, "nt", a, w_slabs, jax.ShapeDtypeStruct((M, K), out_dtype), (M // tm, J),
        pl.BlockSpec((tm, n), lambda i, j: (i, j)), pl.BlockSpec((None, K, n), lambda i, j: (j, 0, 0)),
        pl.BlockSpec((tm, K), lambda i, j: (i, 0)), (tm, K), comm=comm)


def _mm_tn_slab(name, a, b, n, out_dtype=F32, tk=MM_TOKENS, comm=None):
    T, M = a.shape
    J = b.shape[1] // n
    tk = _tile(T, tk)
    return _matmul(
        name, "tn", a, b, jax.ShapeDtypeStruct((J, M, n), out_dtype), (J, T // tk),
        pl.BlockSpec((tk, M), lambda j, k: (k, 0)), pl.BlockSpec((tk, n), lambda j, k: (k, j)),
        pl.BlockSpec((None, M, n), lambda j, k: (j, 0, 0)), (M, n), comm=comm)


def _rms_fwd(name, x, g, tm=512):
    T, Dm = x.shape
    tm = _tile(T, tm)

    def body(x_ref, g_ref, h_ref):
        xv = x_ref[...]
        r = lax.rsqrt(jnp.mean(xv * xv, axis=-1, keepdims=True) + EPS)
        h_ref[...] = (xv * r * g_ref[...]).astype(h_ref.dtype)

    return pl.pallas_call(
        body, out_shape=jax.ShapeDtypeStruct((T, Dm), BF16), grid=(T // tm,),
        in_specs=[pl.BlockSpec((tm, Dm), lambda i: (i, 0)), pl.BlockSpec((1, Dm), lambda i: (0, 0))],
        out_specs=pl.BlockSpec((tm, Dm), lambda i: (i, 0)), name=name, compiler_params=_params(("parallel",)),
    )(x, g)


def _rms_bwd(name, x, g, dh, dres, tm=512):
    T, Dm = x.shape
    tm = _tile(T, tm)
    want_dx = dres is not None

    def body(*refs):
        if want_dx:
            x_ref, g_ref, dh_ref, dres_ref, dx_ref, dg_ref = refs
        else:
            x_ref, g_ref, dh_ref, dg_ref = refs
        xv = x_ref[...]
        r = lax.rsqrt(jnp.mean(xv * xv, axis=-1, keepdims=True) + EPS)
        xhat = xv * r
        dhv = dh_ref[...]

        @pl.when(pl.program_id(0) == 0)
        def _():
            dg_ref[...] = jnp.zeros_like(dg_ref)

        dg_ref[...] += jnp.sum(dhv * xhat, axis=0, keepdims=True)
        if want_dx:
            dxhat = dhv * g_ref[...]
            dx_ref[...] = dres_ref[...] + r * (dxhat - xhat * jnp.mean(dxhat * xhat, axis=-1, keepdims=True))

    row = pl.BlockSpec((tm, Dm), lambda i: (i, 0))
    vec = pl.BlockSpec((1, Dm), lambda i: (0, 0))
    if want_dx:
        return pl.pallas_call(
            body, out_shape=(jax.ShapeDtypeStruct((T, Dm), F32), jax.ShapeDtypeStruct((1, Dm), F32)),
            grid=(T // tm,), in_specs=[row, vec, row, row], out_specs=(row, vec), name=name,
            compiler_params=_params(("arbitrary",)),
        )(x, g, dh, dres)
    return pl.pallas_call(
        body, out_shape=jax.ShapeDtypeStruct((1, Dm), F32), grid=(T // tm,), in_specs=[row, vec, row],
        out_specs=vec, name=name, compiler_params=_params(("arbitrary",)),
    )(x, g, dh)


def _pool_rows(S):
    return _tile(S, 256)


def _pool_count(c0, rows, w):
    t = c0 + lax.broadcasted_iota(jnp.int32, (rows, 1), 0)
    return jnp.minimum(t + 1, w).astype(F32)


def _pool_fwd(proj, w_pool, scale, B, S):
    CH = _pool_rows(S)

    def body(hp_ref, wp_ref, sc_ref, o_ref, pad_ref):
        pad_ref[0:POOL_HALO, :] = jnp.zeros((POOL_HALO, POOL_WIDTH), F32)
        pad_ref[POOL_HALO:, :] = hp_ref[...]
        for gi, w in enumerate(POOL_WINDOWS):
            cols = slice(gi * POOL_GROUP_DIM, (gi + 1) * POOL_GROUP_DIM)
            for c in range(S // CH):
                base = POOL_HALO + c * CH
                acc = pad_ref[base:base + CH, cols]
                tok = acc
                for j in range(1, w):
                    acc = acc + pad_ref[base - j:base - j + CH, cols]
                pooled = acc / _pool_count(c * CH, CH, w) - tok
                z = _dot(pooled, wp_ref[gi])
                o_ref[c * CH:(c + 1) * CH, cols] = (z * sc_ref[:, cols]).astype(o_ref.dtype)

    return pl.pallas_call(
        body, out_shape=jax.ShapeDtypeStruct((B * S, POOL_WIDTH), BF16), grid=(B,),
        in_specs=[pl.BlockSpec((S, POOL_WIDTH), lambda b: (b, 0)),
                  pl.BlockSpec(w_pool.shape, lambda b: (0, 0, 0)),
                  pl.BlockSpec((1, POOL_WIDTH), lambda b: (0, 0))],
        out_specs=pl.BlockSpec((S, POOL_WIDTH), lambda b: (b, 0)),
        scratch_shapes=[pltpu.VMEM((S + POOL_HALO, POOL_WIDTH), F32)],
        name="pool_fwd", compiler_params=_params(("parallel",)),
    )(proj, w_pool, scale)


def _pool_bwd(proj, d_ypre, w_pool, scale, B, S):
    CH = _pool_rows(S)

    def body(hp_ref, dy_ref, wp_ref, sc_ref, dhp_ref, dwp_ref, dsc_ref, pad_ref, sc_pad_ref, dp_ref):
        @pl.when(pl.program_id(0) == 0)
        def _():
            dwp_ref[...] = jnp.zeros_like(dwp_ref)
            dsc_ref[...] = jnp.zeros_like(dsc_ref)

        pad_ref[0:POOL_HALO, :] = jnp.zeros((POOL_HALO, POOL_WIDTH), F32)
        pad_ref[POOL_HALO:, :] = hp_ref[...]
        sc_pad_ref[S:, :] = jnp.zeros((POOL_HALO, POOL_WIDTH), F32)
        for gi, w in enumerate(POOL_WINDOWS):
            cols = slice(gi * POOL_GROUP_DIM, (gi + 1) * POOL_GROUP_DIM)
            for c in range(S // CH):
                base = POOL_HALO + c * CH
                rows = slice(c * CH, (c + 1) * CH)
                acc = pad_ref[base:base + CH, cols]
                tok = acc
                for j in range(1, w):
                    acc = acc + pad_ref[base - j:base - j + CH, cols]
                cnt = _pool_count(c * CH, CH, w)
                pooled = acc / cnt - tok
                z = _dot(pooled, wp_ref[gi])
                dy = dy_ref[rows, cols]
                dsc_ref[:, cols] += jnp.sum(dy * z, axis=0, keepdims=True)
                dz = dy * sc_ref[:, cols]
                dwp_ref[gi] += _dot(pooled, dz, "tn")
                dpool = _dot(dz, wp_ref[gi], "nt")
                dp_ref[rows, cols] = dpool
                sc_pad_ref[rows, cols] = dpool / cnt
            for c in range(S // CH):
                rows = slice(c * CH, (c + 1) * CH)
                acc = sc_pad_ref[rows, cols]
                for j in range(1, w):
                    acc = acc + sc_pad_ref[c * CH + j:c * CH + j + CH, cols]
                dhp_ref[rows, cols] = (acc - dp_ref[rows, cols]).astype(dhp_ref.dtype)

    seq = pl.BlockSpec((S, POOL_WIDTH), lambda b: (b, 0))
    return pl.pallas_call(
        body,
        out_shape=(jax.ShapeDtypeStruct((B * S, POOL_WIDTH), BF16),
                   jax.ShapeDtypeStruct(w_pool.shape, F32), jax.ShapeDtypeStruct((1, POOL_WIDTH), F32)),
        grid=(B,),
        in_specs=[seq, seq, pl.BlockSpec(w_pool.shape, lambda b: (0, 0, 0)),
                  pl.BlockSpec((1, POOL_WIDTH), lambda b: (0, 0))],
        out_specs=(seq, pl.BlockSpec(w_pool.shape, lambda b: (0, 0, 0)),
                   pl.BlockSpec((1, POOL_WIDTH), lambda b: (0, 0))),
        scratch_shapes=[pltpu.VMEM((S + POOL_HALO, POOL_WIDTH), F32),
                        pltpu.VMEM((S + POOL_HALO, POOL_WIDTH), F32),
                        pltpu.VMEM((S, POOL_WIDTH), F32)],
        name="pool_bwd", compiler_params=_params(("arbitrary",)),
    )(proj, d_ypre, w_pool, scale)


def _ret_tables(S):
    half = RET_QK_DIM // 2
    inv = ROPE_BASE ** (-jnp.arange(half, dtype=F32) / half)
    ang = jnp.arange(S, dtype=F32)[:, None] * inv[None, :]
    cos, sin = jnp.cos(ang), jnp.sin(ang)
    cos_full = jnp.concatenate([cos, cos], axis=-1)
    sin_signed = jnp.concatenate([-sin, sin], axis=-1)
    C = RET_CHUNK
    lg = jnp.log1p(-jnp.exp2(-5.0 - jnp.arange(RET_HEADS, dtype=F32)))[:, None, None]
    idx = jnp.arange(C, dtype=F32)
    rel = idx[:, None] - idx[None, :]
    decay = jnp.where(rel >= 0, jnp.exp(jnp.maximum(rel, 0.0) * lg), 0.0)
    q_decay = jnp.broadcast_to(jnp.exp((idx + 1.0)[None, :, None] * lg), (RET_HEADS, C, RET_QK_DIM))
    k_decay = jnp.broadcast_to(jnp.exp((C - 1.0 - idx)[None, :, None] * lg), (RET_HEADS, C, RET_QK_DIM))
    c_decay = jnp.broadcast_to(jnp.exp(C * lg), (RET_HEADS, 1, RET_V_DIM))
    return cos_full, sin_signed, decay, q_decay, k_decay, c_decay


def _rope(x, cos_full, sin_signed):
    return x * cos_full + pltpu.roll(x, RET_QK_DIM // 2, axis=1) * sin_signed


def _rope_t(dy, cos_full, sin_signed):
    return dy * cos_full + pltpu.roll(dy * sin_signed, RET_QK_DIM // 2, axis=1)


def _ret_specs(S, bh):
    def at(col_of_head, width):
        def index(*ids):
            b, h = bh(*ids)
            return (b, col_of_head + h)
        return pl.BlockSpec((S, width), index)

    def per_head(shape):
        def index(*ids):
            _, h = bh(*ids)
            return (h,) + (0,) * len(shape)
        return pl.BlockSpec((None,) + shape, index)

    def head_vec(width):
        def index(*ids):
            _, h = bh(*ids)
            return (0, h)
        return pl.BlockSpec((1, width), index)

    table = pl.BlockSpec((S, RET_QK_DIM), lambda *ids: (0, 0))
    C = RET_CHUNK
    return dict(
        q=at(COL_Q // RET_QK_DIM, RET_QK_DIM), k=at(COL_K // RET_QK_DIM, RET_QK_DIM),
        v=at(COL_V // RET_V_DIM, RET_V_DIM), gr=at(COL_GR // RET_V_DIM, RET_V_DIM),
        table=table, decay=per_head((C, C)), qd=per_head((C, RET_QK_DIM)), kd=per_head((C, RET_QK_DIM)),
        cd=per_head((1, RET_V_DIM)), vec=head_vec(RET_V_DIM), out_qk=at(0, RET_QK_DIM), out_v=at(0, RET_V_DIM))


def _group_norm(o):
    mu = jnp.mean(o, axis=-1, keepdims=True)
    oc = o - mu
    rstd = lax.rsqrt(jnp.mean(oc * oc, axis=-1, keepdims=True) + EPS)
    return oc * rstd, rstd


def _ret_fwd(proj, g_ret, b_ret, tables, B, S, comm=None):
    C = RET_CHUNK
    cos_t, sin_t, decay, q_decay, k_decay, c_decay = tables
    sp = _ret_specs(S, lambda b, h: (b, h))

    def body(q_ref, k_ref, v_ref, gr_ref, cos_ref, sin_ref, dec_ref, qd_ref, kd_ref, cd_ref, g_ref, b_ref,
             y_ref, r_ref):
        r_ref[...] = jnp.zeros_like(r_ref)

        def chunk(i, carry):
            rows = pl.ds(pl.multiple_of(i * C, C), C)
            cs, sn = cos_ref[rows, :], sin_ref[rows, :]
            q = _rope(q_ref[rows, :], cs, sn)
            k = _rope(k_ref[rows, :], cs, sn) * (RET_QK_DIM ** -0.5)
            v = v_ref[rows, :]
            R = r_ref[...]
            s = _dot(q, k, "nt") * dec_ref[...]
            o = _dot(s, v) + _dot(q * qd_ref[...], R)
            r_ref[...] = cd_ref[...] * R + _dot(k * kd_ref[...], v, "tn")
            on, _ = _group_norm(o)
            gr = gr_ref[rows, :]
            y_ref[rows, :] = (gr * jax.nn.sigmoid(gr) * (on * g_ref[...] + b_ref[...])).astype(y_ref.dtype)
            return carry

        lax.fori_loop(0, S // C, chunk, 0)

    return _pcall(
        body, (proj, proj, proj, proj, cos_t, sin_t, decay, q_decay, k_decay, c_decay, g_ret, b_ret),
        name="ret_fwd", out_shape=jax.ShapeDtypeStruct((B * S, RET_HEADS * RET_V_DIM), BF16), grid=(B, RET_HEADS),
        in_specs=[sp["q"], sp["k"], sp["v"], sp["gr"], sp["table"], sp["table"], sp["decay"], sp["qd"],
                  sp["kd"], sp["cd"], sp["vec"], sp["vec"]],
        out_specs=sp["out_v"], scratch_shapes=[pltpu.VMEM((RET_QK_DIM, RET_V_DIM), F32)],
        sem=("parallel", "parallel"), comm=comm)


def _ret_bwd(proj, d_yr, g_ret, b_ret, tables, B, S, comm=None):
    C = RET_CHUNK
    N = S // C
    cos_t, sin_t, decay, q_decay, k_decay, c_decay = tables
    sp = _ret_specs(S, lambda h, b: (b, h))
    qk_scale = RET_QK_DIM ** -0.5

    def body(q_ref, k_ref, v_ref, gr_ref, dy_ref, cos_ref, sin_ref, dec_ref, qd_ref, kd_ref, cd_ref, g_ref, b_ref,
             dq_ref, dk_ref, dv_ref, dgr_ref, dg_ref, db_ref, qr_ref, kr_ref, rs_ref, dr_ref):
        @pl.when(pl.program_id(1) == 0)
        def _():
            dg_ref[...] = jnp.zeros_like(dg_ref)
            db_ref[...] = jnp.zeros_like(db_ref)

        dr_ref[...] = jnp.zeros_like(dr_ref)

        def sweep(i, R):
            rows = pl.ds(pl.multiple_of(i * C, C), C)
            cs, sn = cos_ref[rows, :], sin_ref[rows, :]
            q = _rope(q_ref[rows, :], cs, sn)
            k = _rope(k_ref[rows, :], cs, sn) * qk_scale
            qr_ref[rows, :] = q
            kr_ref[rows, :] = k
            rs_ref[i] = R
            return cd_ref[...] * R + _dot(k * kd_ref[...], v_ref[rows, :], "tn")

        lax.fori_loop(0, N, sweep, jnp.zeros((RET_QK_DIM, RET_V_DIM), F32))

        def back(step, carry):
            i = N - 1 - step
            rows = pl.ds(pl.multiple_of(i * C, C), C)
            q, k, v = qr_ref[rows, :], kr_ref[rows, :], v_ref[rows, :]
            R, dR = rs_ref[i], dr_ref[...]
            dec, qd, kd = dec_ref[...], qd_ref[...], kd_ref[...]
            s = _dot(q, k, "nt") * dec
            o = _dot(s, v) + _dot(q * qd, R)
            on, rstd = _group_norm(o)
            oaff = on * g_ref[...] + b_ref[...]
            gr = gr_ref[rows, :]
            sg = jax.nn.sigmoid(gr)
            dy = dy_ref[rows, :]
            dgr_ref[rows, :] = (dy * oaff * (sg * (1.0 + gr * (1.0 - sg)))).astype(dgr_ref.dtype)
            doaff = dy * (gr * sg)
            dg_ref[...] += jnp.sum(doaff * on, axis=0, keepdims=True)
            db_ref[...] += jnp.sum(doaff, axis=0, keepdims=True)
            don = doaff * g_ref[...]
            do = rstd * (don - jnp.mean(don, axis=-1, keepdims=True)
                         - on * jnp.mean(don * on, axis=-1, keepdims=True))
            ds = _dot(do, v, "nt") * dec
            dq = _dot(ds, k) + qd * _dot(do, R, "nt")
            dk = _dot(ds, q, "tn") + kd * _dot(v, dR, "nt")
            dv_ref[rows, :] = (_dot(s, do, "tn") + _dot(k * kd, dR)).astype(dv_ref.dtype)
            dr_ref[...] = cd_ref[...] * dR + _dot(q * qd, do, "tn")
            cs, sn = cos_ref[rows, :], sin_ref[rows, :]
            dq_ref[rows, :] = _rope_t(dq, cs, sn).astype(dq_ref.dtype)
            dk_ref[rows, :] = _rope_t(dk * qk_scale, cs, sn).astype(dk_ref.dtype)
            return carry

        lax.fori_loop(0, N, back, 0)

    T = B * S
    qk_shape = jax.ShapeDtypeStruct((T, RET_HEADS * RET_QK_DIM), BF16)
    v_shape = jax.ShapeDtypeStruct((T, RET_HEADS * RET_V_DIM), BF16)
    vec_shape = jax.ShapeDtypeStruct((1, RET_HEADS * RET_V_DIM), F32)
    return _pcall(
        body, (proj, proj, proj, proj, d_yr, cos_t, sin_t, decay, q_decay, k_decay, c_decay, g_ret, b_ret),
        name="ret_bwd", out_shape=(qk_shape, qk_shape, v_shape, v_shape, vec_shape, vec_shape), grid=(RET_HEADS, B),
        in_specs=[sp["q"], sp["k"], sp["v"], sp["gr"], sp["out_v"], sp["table"], sp["table"], sp["decay"],
                  sp["qd"], sp["kd"], sp["cd"], sp["vec"], sp["vec"]],
        out_specs=(sp["out_qk"], sp["out_qk"], sp["out_v"], sp["out_v"], sp["vec"], sp["vec"]),
        scratch_shapes=[pltpu.VMEM((S, RET_QK_DIM), F32), pltpu.VMEM((S, RET_QK_DIM), F32),
                        pltpu.VMEM((N, RET_QK_DIM, RET_V_DIM), F32), pltpu.VMEM((RET_QK_DIM, RET_V_DIM), F32)],
        sem=("parallel", "arbitrary"), comm=comm)


def _xa_rows(S):
    return _tile(S, 256)


def _xa_specs(S, M):
    q = pl.BlockSpec((S, XA_HEAD_DIM), lambda b, h: (b, COL_QX // XA_HEAD_DIM + h))
    k = pl.BlockSpec((M, XA_HEAD_DIM), lambda b, h: (b, h))
    v = pl.BlockSpec((M, XA_HEAD_DIM), lambda b, h: (b, XA_HEADS + h))
    o = pl.BlockSpec((S, XA_HEAD_DIM), lambda b, h: (b, h))
    return q, k, v, o


def _softmax_rows(s):
    e = jnp.exp(s - jnp.max(s, axis=-1, keepdims=True))
    return e / jnp.sum(e, axis=-1, keepdims=True)


def _xa_fwd(proj, kv, B, S, M):
    CH = _xa_rows(S)
    q_spec, k_spec, v_spec, o_spec = _xa_specs(S, M)

    def body(q_ref, k_ref, v_ref, o_ref):
        def chunk(i, carry):
            rows = pl.ds(pl.multiple_of(i * CH, CH), CH)
            p = _softmax_rows(_dot(q_ref[rows, :], k_ref[...], "nt") * (XA_HEAD_DIM ** -0.5))
            o_ref[rows, :] = _dot(p, v_ref[...]).astype(o_ref.dtype)
            return carry

        lax.fori_loop(0, S // CH, chunk, 0)

    return pl.pallas_call(
        body, out_shape=jax.ShapeDtypeStruct((B * S, XA_WIDTH), BF16), grid=(B, XA_HEADS),
        in_specs=[q_spec, k_spec, v_spec], out_specs=o_spec, name="xattn_fwd",
        compiler_params=_params(("parallel", "parallel")),
    )(proj, kv, kv)


def _xa_bwd(proj, kv, d_o, B, S, M):
    CH = _xa_rows(S)
    q_spec, k_spec, v_spec, o_spec = _xa_specs(S, M)
    scale = XA_HEAD_DIM ** -0.5

    def body(q_ref, k_ref, v_ref, do_ref, dq_ref, dk_ref, dv_ref):
        dk_ref[...] = jnp.zeros_like(dk_ref)
        dv_ref[...] = jnp.zeros_like(dv_ref)

        def chunk(i, carry):
            rows = pl.ds(pl.multiple_of(i * CH, CH), CH)
            q, do = q_ref[rows, :], do_ref[rows, :]
            p = _softmax_rows(_dot(q, k_ref[...], "nt") * scale)
            dp = _dot(do, v_ref[...], "nt")
            ds = p * (dp - jnp.sum(dp * p, axis=-1, keepdims=True)) * scale
            dq_ref[rows, :] = _dot(ds, k_ref[...]).astype(dq_ref.dtype)
            dk_ref[...] += _dot(ds, q, "tn")
            dv_ref[...] += _dot(p, do, "tn")
            return carry

        lax.fori_loop(0, S // CH, chunk, 0)

    kv_out = pl.BlockSpec((M, XA_HEAD_DIM), lambda b, h: (b, h))
    return pl.pallas_call(
        body,
        out_shape=(jax.ShapeDtypeStruct((B * S, XA_WIDTH), BF16), jax.ShapeDtypeStruct((B * M, XA_WIDTH), F32),
                   jax.ShapeDtypeStruct((B * M, XA_WIDTH), F32)),
        grid=(B, XA_HEADS), in_specs=[q_spec, k_spec, v_spec, o_spec], out_specs=(o_spec, kv_out, kv_out),
        name="xattn_bwd", compiler_params=_params(("parallel", "parallel")),
    )(proj, kv, kv, d_o)


def _gate_specs(tm):
    n = COL_GL // D_MODEL
    return [pl.BlockSpec((tm, D_MODEL), lambda i, j=j: (i, n + j)) for j in range(3)]


def _merge_fwd(proj, ys, tm=256):
    T = proj.shape[0]
    tm = _tile(T, tm)
    row = pl.BlockSpec((tm, D_MODEL), lambda i: (i, 0))

    def body(g0, g1, g2, y0, y1, y2, o_ref):
        acc = jax.nn.sigmoid(g0[...]) * y0[...]
        acc = acc + jax.nn.sigmoid(g1[...]) * y1[...]
        acc = acc + jax.nn.sigmoid(g2[...]) * y2[...]
        o_ref[...] = acc.astype(o_ref.dtype)

    return pl.pallas_call(
        body, out_shape=jax.ShapeDtypeStruct((T, D_MODEL), BF16), grid=(T // tm,),
        in_specs=_gate_specs(tm) + [row] * 3, out_specs=row, name="merge_fwd",
        compiler_params=_params(("parallel",)),
    )(proj, proj, proj, *ys)


def _merge_bwd(proj, ys, d_merged, tm=256, comm=None):
    T = proj.shape[0]
    tm = _tile(T, tm)
    row = pl.BlockSpec((tm, D_MODEL), lambda i: (i, 0))

    def body(g0, g1, g2, y0, y1, y2, dm_ref, dgl_ref, d0, d1, d2):
        dm = dm_ref[...]
        for j, (g_ref, y_ref, d_ref) in enumerate(((g0, y0, d0), (g1, y1, d1), (g2, y2, d2))):
            sg = jax.nn.sigmoid(g_ref[...])
            d_ref[...] = (dm * sg).astype(d_ref.dtype)
            dgl_ref[:, j * D_MODEL:(j + 1) * D_MODEL] = (dm * y_ref[...] * sg * (1.0 - sg)).astype(dgl_ref.dtype)

    dy = jax.ShapeDtypeStruct((T, D_MODEL), BF16)
    return _pcall(
        body, (proj, proj, proj, *ys, d_merged), name="merge_bwd",
        out_shape=(jax.ShapeDtypeStruct((T, 3 * D_MODEL), BF16), dy, dy, dy), grid=(T // tm,),
        in_specs=_gate_specs(tm) + [row] * 4,
        out_specs=(pl.BlockSpec((tm, 3 * D_MODEL), lambda i: (i, 0)), row, row, row),
        sem=("parallel",), comm=comm)


def _gelu(x):
    return 0.5 * x * (1.0 + jnp.tanh(GELU_C * (x + GELU_A * x * x * x)))


def _gelu_grad(x):
    t = jnp.tanh(GELU_C * (x + GELU_A * x * x * x))
    return 0.5 * (1.0 + t) + 0.5 * x * (1.0 - t * t) * GELU_C * (1.0 + 3.0 * GELU_A * x * x)


def _shift_down(x, prev, n):
    rows = x.shape[0]
    r = lax.broadcasted_iota(jnp.int32, (rows, 1), 0)
    out = pltpu.roll(x, n, axis=0)
    for j in range(n):
        out = jnp.where(r == j, prev[8 - n + j:8 - n + j + 1, :], out)
    return out


def _shift_up(x, nxt, n):
    rows = x.shape[0]
    r = lax.broadcasted_iota(jnp.int32, (rows, 1), 0)
    out = pltpu.roll(x, rows - n, axis=0)
    for j in range(n):
        out = jnp.where(r == rows - n + j, nxt[j:j + 1, :], out)
    return out


def _conv(a, prev, cw, cb):
    return _shift_down(a, prev, 2) * cw[0:1, :] + _shift_down(a, prev, 1) * cw[1:2, :] + a * cw[2:3, :] + cb


def _glu_fwd(up, cw, cb, S, tm=256):
    T = up.shape[2]
    tm = _tile(S, tm)
    per_seq = S // tm

    def body(ab_ref, prev_ref, cw_ref, cb_ref, u_ref):
        i = pl.program_id(1)
        prev = jnp.where(i % per_seq == 0, 0.0, prev_ref[...])
        ac = _conv(ab_ref[0], prev, cw_ref[...], cb_ref[...])
        u_ref[...] = (_gelu(ac) * ab_ref[1]).astype(u_ref.dtype)

    return pl.pallas_call(
        body, out_shape=jax.ShapeDtypeStruct((FFN_SLABS, T, UP_SHARD), BF16), grid=(FFN_SLABS, T // tm),
        in_specs=[pl.BlockSpec((2, None, tm, UP_SHARD), lambda d, i: (0, d, i, 0)),
                  pl.BlockSpec((None, None, 8, UP_SHARD), lambda d, i: (0, d, jnp.maximum(i * (tm // 8) - 1, 0), 0)),
                  pl.BlockSpec((None, 3, UP_SHARD), lambda d, i: (d, 0, 0)),
                  pl.BlockSpec((None, 1, UP_SHARD), lambda d, i: (d, 0, 0))],
        out_specs=pl.BlockSpec((None, tm, UP_SHARD), lambda d, i: (d, i, 0)), name="glu_fwd",
        compiler_params=_params(("parallel", "parallel")),
    )(up, up, cw, cb)


def _glu_bwd(up, d_u, cw, cb, S, tm=256, comm=None):
    T = up.shape[2]
    tm = _tile(S, tm)
    per_seq = S // tm
    n_tiles = T // tm
    last8 = tm // 8

    def body(ab_ref, prev_ref, abn_ref, du_ref, dun_ref, cw_ref, cb_ref, dup_ref, dcw_ref, dcb_ref):
        i = pl.program_id(1)

        @pl.when(i == 0)
        def _():
            dcw_ref[...] = jnp.zeros_like(dcw_ref)
            dcb_ref[...] = jnp.zeros_like(dcb_ref)

        cw, cb = cw_ref[...], cb_ref[...]
        a, b = ab_ref[0], ab_ref[1]
        prev = jnp.where(i % per_seq == 0, 0.0, prev_ref[...])
        a2, a1 = _shift_down(a, prev, 2), _shift_down(a, prev, 1)
        ac = a2 * cw[0:1, :] + a1 * cw[1:2, :] + a * cw[2:3, :] + cb
        du = du_ref[...]
        dup_ref[1] = (du * _gelu(ac)).astype(dup_ref.dtype)
        dac = du * b * _gelu_grad(ac)
        dcb_ref[...] += jnp.sum(dac, axis=0, keepdims=True)
        dcw_ref[0:1, :] += jnp.sum(dac * a2, axis=0, keepdims=True)
        dcw_ref[1:2, :] += jnp.sum(dac * a1, axis=0, keepdims=True)
        dcw_ref[2:3, :] += jnp.sum(dac * a, axis=0, keepdims=True)
        an = abn_ref[0]
        acn = _conv(an, a[tm - 8:, :], cw, cb)
        dacn = jnp.where(i % per_seq == per_seq - 1, 0.0, dun_ref[...] * abn_ref[1] * _gelu_grad(acn))
        da = dac * cw[2:3, :] + _shift_up(dac, dacn, 1) * cw[1:2, :] + _shift_up(dac, dacn, 2) * cw[0:1, :]
        dup_ref[0] = da.astype(dup_ref.dtype)

    def nxt(i):
        return jnp.minimum((i + 1) * last8, T // 8 - 1)

    return _pcall(
        body, (up, up, up, d_u, d_u, cw, cb), name="glu_bwd",
        out_shape=(jax.ShapeDtypeStruct((2, FFN_SLABS, T, UP_SHARD), BF16),
                   jax.ShapeDtypeStruct((FFN_SLABS, 3, UP_SHARD), F32),
                   jax.ShapeDtypeStruct((FFN_SLABS, 1, UP_SHARD), F32)),
        grid=(FFN_SLABS, n_tiles),
        in_specs=[pl.BlockSpec((2, None, tm, UP_SHARD), lambda d, i: (0, d, i, 0)),
                  pl.BlockSpec((None, None, 8, UP_SHARD), lambda d, i: (0, d, jnp.maximum(i * last8 - 1, 0), 0)),
                  pl.BlockSpec((2, None, 8, UP_SHARD), lambda d, i: (0, d, nxt(i), 0)),
                  pl.BlockSpec((None, tm, UP_SHARD), lambda d, i: (d, i, 0)),
                  pl.BlockSpec((None, 8, UP_SHARD), lambda d, i: (d, nxt(i), 0)),
                  pl.BlockSpec((None, 3, UP_SHARD), lambda d, i: (d, 0, 0)),
                  pl.BlockSpec((None, 1, UP_SHARD), lambda d, i: (d, 0, 0))],
        out_specs=(pl.BlockSpec((2, None, tm, UP_SHARD), lambda d, i: (0, d, i, 0)),
                   pl.BlockSpec((None, 3, UP_SHARD), lambda d, i: (d, 0, 0)),
                   pl.BlockSpec((None, 1, UP_SHARD), lambda d, i: (d, 0, 0))),
        sem=("parallel", "arbitrary"), comm=comm)


def _mm_up(h2, w_up, tm=MM_ROWS):
    T, K = h2.shape
    tm = _tile(T, tm)
    return _matmul(
        "mm_up", "nn", h2, w_up, jax.ShapeDtypeStruct((N_DEV, T, UP_SHARD), F32), (N_DEV, T // tm, 1),
        pl.BlockSpec((tm, K), lambda j, i, k: (i, 0)), pl.BlockSpec((None, K, UP_SHARD), lambda j, i, k: (j, 0, 0)),
        pl.BlockSpec((None, tm, UP_SHARD), lambda j, i, k: (j, i, 0)), (tm, UP_SHARD))


def _mm_down(u, w_down, res, tm=MM_ROWS):
    J, T, n = u.shape
    tm = _tile(T, tm)
    row = pl.BlockSpec((tm, D_MODEL), lambda i, d: (i, 0))
    return _matmul(
        "mm_down", "nn", u, w_down, jax.ShapeDtypeStruct((T, D_MODEL), F32), (T // tm, J),
        pl.BlockSpec((None, tm, n), lambda i, d: (d, i, 0)), pl.BlockSpec((None, n, D_MODEL), lambda i, d: (d, 0, 0)),
        row, (tm, D_MODEL), res, row)


def _mm_down_t(dx, w_down, tm=MM_ROWS):
    T = dx.shape[0]
    J, n, _ = w_down.shape
    tm = _tile(T, tm)
    return _matmul(
        "mm_down_t", "nt", dx, w_down, jax.ShapeDtypeStruct((J, T, n), F32), (J, T // tm, 1),
        pl.BlockSpec((tm, D_MODEL), lambda d, i, k: (i, 0)), pl.BlockSpec((None, n, D_MODEL), lambda d, i, k: (d, 0, 0)),
        pl.BlockSpec((None, tm, n), lambda d, i, k: (d, i, 0)), (tm, n))


def _mm_dw_down(u, dx, tk=MM_TOKENS):
    J, T, n = u.shape
    tk = _tile(T, tk)
    return _matmul(
        "mm_dw_down", "tn", u, dx, jax.ShapeDtypeStruct((J, n, D_MODEL), BF16), (J, T // tk),
        pl.BlockSpec((None, tk, n), lambda d, k: (d, k, 0)), pl.BlockSpec((tk, D_MODEL), lambda d, k: (k, 0)),
        pl.BlockSpec((None, n, D_MODEL), lambda d, k: (d, 0, 0)), (n, D_MODEL))


def _mm_dw_up(h2, d_up, tk=MM_TOKENS):
    T, K = h2.shape
    tk = _tile(T, tk)
    return _matmul(
        "mm_dw_up", "tn", h2, d_up, jax.ShapeDtypeStruct((N_DEV, K, UP_SHARD), BF16), (N_DEV, T // tk),
        pl.BlockSpec((tk, K), lambda j, k: (k, 0)), pl.BlockSpec((None, tk, UP_SHARD), lambda j, k: (j, k, 0)),
        pl.BlockSpec((None, K, UP_SHARD), lambda j, k: (j, 0, 0)), (K, UP_SHARD))


def _mm_up_t(d_up, w_up, tm=MM_ROWS, comm=None):
    J, T, n = d_up.shape
    K = w_up.shape[1]
    tm = _tile(T, tm)
    return _matmul(
        "mm_up_t", "nt", d_up, w_up, jax.ShapeDtypeStruct((T, K), F32), (T // tm, J),
        pl.BlockSpec((None, tm, n), lambda i, j: (j, i, 0)), pl.BlockSpec((None, K, n), lambda i, j: (j, 0, 0)),
        pl.BlockSpec((tm, K), lambda i, j: (i, 0)), (tm, K), comm=comm)


def _loss_head(x2, target, g_final, tm=512):
    T, Dm = x2.shape
    tm = _tile(T, tm)

    def body(x_ref, t_ref, g_ref, dx_ref, dg_ref, loss_ref):
        @pl.when(pl.program_id(0) == 0)
        def _():
            dg_ref[...] = jnp.zeros_like(dg_ref)
            loss_ref[...] = jnp.zeros_like(loss_ref)

        xv = x_ref[...]
        r = lax.rsqrt(jnp.mean(xv * xv, axis=-1, keepdims=True) + EPS)
        xhat = xv * r
        err = xhat * g_ref[...] - t_ref[...]
        loss_ref[...] += (0.5 / Dm) * jnp.sum(err * err)
        dy = err * (1.0 / Dm)
        dg_ref[...] += jnp.sum(dy * xhat, axis=0, keepdims=True)
        dxhat = dy * g_ref[...]
        dx_ref[...] = r * (dxhat - xhat * jnp.mean(dxhat * xhat, axis=-1, keepdims=True))

    row = pl.BlockSpec((tm, Dm), lambda i: (i, 0))
    vec = pl.BlockSpec((1, Dm), lambda i: (0, 0))
    return pl.pallas_call(
        body,
        out_shape=(jax.ShapeDtypeStruct((T, Dm), F32), jax.ShapeDtypeStruct((1, Dm), F32),
                   jax.ShapeDtypeStruct((1, Dm), F32)),
        grid=(T // tm,), in_specs=[row, row, vec], out_specs=(row, vec, vec), name="loss_head",
        compiler_params=_params(("arbitrary",)),
    )(x2, target, g_final)


def _cast_shards(shards):
    def body(*refs):
        n = len(refs) // 2
        for src, dst in zip(refs[:n], refs[n:]):
            dst[...] = src[...].astype(dst.dtype)

    return pl.pallas_call(
        body, out_shape=[jax.ShapeDtypeStruct(s.shape, BF16) for s in shards], name="cast_shards",
        compiler_params=pltpu.CompilerParams(vmem_limit_bytes=VMEM_LIMIT),
    )(*shards)


def _adamw(w, g, m, v):
    m = ADAM_B1 * m + (1.0 - ADAM_B1) * g
    v = ADAM_B2 * v + (1.0 - ADAM_B2) * (g * g)
    m_hat = m / (1.0 - ADAM_B1 ** ADAM_STEP)
    v_hat = v / (1.0 - ADAM_B2 ** ADAM_STEP)
    delta = -ADAM_LR * (m_hat / (jnp.sqrt(v_hat) + ADAM_EPS) + ADAM_WD * w)
    return delta, m, v


def _reduce_adam(name, parts, w, m, v, tr=128):
    R, Cn = w.shape
    tr = _tile(R, tr) if R % 8 == 0 else R
    if R // tr * tr != R or (tr % 8 and tr != R):
        tr = R

    def body(p_ref, w_ref, m_ref, v_ref, g_out, d_out, m_out, v_out):
        g = p_ref[0].astype(F32)
        for d in range(1, N_DEV):
            g = g + p_ref[d].astype(F32)
        delta, m_new, v_new = _adamw(w_ref[...], g, m_ref[...], v_ref[...])
        g_out[...] = g
        d_out[...] = delta
        m_out[...] = m_new
        v_out[...] = v_new

    row = pl.BlockSpec((tr, Cn), lambda i: (i, 0))
    shape = jax.ShapeDtypeStruct((R, Cn), F32)
    return pl.pallas_call(
        body, out_shape=(shape,) * 4, grid=(R // tr,),
        in_specs=[pl.BlockSpec((N_DEV, tr, Cn), lambda i: (0, i, 0)), row, row, row], out_specs=(row,) * 4,
        name=name, compiler_params=_params(("parallel",)),
    )(parts, w, m, v)


def _adam_only(name, g, w, m, v):
    def body(g_ref, w_ref, m_ref, v_ref, d_out, m_out, v_out):
        delta, m_new, v_new = _adamw(w_ref[...], g_ref[...], m_ref[...], v_ref[...])
        d_out[...] = delta
        m_out[...] = m_new
        v_out[...] = v_new

    shape = jax.ShapeDtypeStruct(w.shape, F32)
    return pl.pallas_call(body, out_shape=(shape,) * 3, name=name)(g, w, m, v)


def _pack(arrays):
    rows = []
    for a in arrays:
        flat = a.reshape(-1).astype(F32)
        pad = (-flat.shape[0]) % (8 * PACK_LANES)
        rows.append(jnp.pad(flat, (0, pad)).reshape(-1, PACK_LANES))
    return jnp.concatenate(rows, axis=0)


def _unpack(packed, shapes):
    out, r = [], 0
    for shp in shapes:
        size = math.prod(shp)
        nr = 8 * -(-size // (8 * PACK_LANES))
        out.append(packed[r:r + nr].reshape(-1)[:size].reshape(shp))
        r += nr
    return out


def kernel(x, mem, g_mix, w_in, w_pool, pool_scale, w_a, g_ret, b_ret, w_r, g_mem, w_mem_kv, w_c, w_out, g_ffn, w_up, conv_w, conv_b, w_down, g_final, loss_target, m_g_mix, m_w_in, m_w_pool, m_pool_scale, m_w_a, m_g_ret, m_b_ret, m_w_r, m_g_mem, m_w_mem_kv, m_w_c, m_w_out, m_g_ffn, m_w_up, m_conv_w, m_conv_b, m_w_down, m_g_final, v_g_mix, v_w_in, v_w_pool, v_pool_scale, v_w_a, v_g_ret, v_b_ret, v_w_r, v_g_mem, v_w_mem_kv, v_w_c, v_w_out, v_g_ffn, v_w_up, v_conv_w, v_conv_b, v_w_down, v_g_final):
    B, S, _ = x.shape
    M = mem.shape[1]
    T = B * S
    me = _my_index()
    x2d = x.reshape(T, D_MODEL)
    mem2d = mem.reshape(B * M, D_MODEL)
    tgt2d = loss_target.reshape(T, D_MODEL)
    g_final2 = g_final.reshape(1, D_MODEL)

    big = dict(w_in=w_in[0], w_a=w_a[0], w_r=w_r[0], w_mem_kv=w_mem_kv[0], w_c=w_c[0], w_out=w_out[0],
               w_up=w_up[0], w_down=w_down[0])
    names = list(big)
    cast = dict(zip(names, _cast_shards([big[n] for n in names])))
    Win, cw_gathered = _comm_call("gather_w_in", _Gather([cast["w_in"], conv_w[0]]))
    cw_full = cw_gathered.transpose(1, 0, 2).reshape(3, FFN_HIDDEN)
    cw = cw_full.reshape(3, FFN_SLABS, UP_SHARD).transpose(1, 0, 2)
    cb = conv_b[0].reshape(FFN_SLABS, 1, UP_SHARD)
    wp = w_pool[0]
    tables = _ret_tables(S)

    h = _rms_fwd("rms_mix", x2d, g_mix)
    early = ("w_a", "w_r", "w_mem_kv", "w_c", "w_out", "w_down")
    proj, landed = _mm_cols_slab("mm_in", h, Win, comm=_Gather([cast[n] for n in early]))
    W = dict(zip(early, landed))
    Wa = W["w_a"].transpose(1, 0, 2).reshape(POOL_WIDTH, D_MODEL)
    Wc = W["w_c"].transpose(1, 0, 2).reshape(XA_WIDTH, D_MODEL)
    Wr = W["w_r"].reshape(D_MODEL, D_MODEL)
    Wkv = W["w_mem_kv"].reshape(D_MODEL, D_MODEL)
    Wout = W["w_out"].reshape(D_MODEL, D_MODEL)
    Wdown = W["w_down"].reshape(FFN_SLABS, UP_SHARD, D_MODEL)
    ypre = _pool_fwd(proj, wp, pool_scale, B, S)
    y_pool = _mm_rows("mm_a", ypre, Wa)
    yr, (Wup,) = _ret_fwd(proj, g_ret, b_ret, tables, B, S, comm=_Gather([cast["w_up"]]))
    y_ret = _mm_rows("mm_r", yr, Wr)
    mem_n = _rms_fwd("rms_mem", mem2d, g_mem)
    kv = _mm_rows("mm_kv", mem_n, Wkv)
    o_mem = _xa_fwd(proj, kv, B, S, M)
    y_mem = _mm_rows("mm_c", o_mem, Wc)
    ys = (y_pool, y_ret, y_mem)
    merged = _merge_fwd(proj, ys)
    x1 = _mm_rows("mm_out", merged, Wout, res=x2d)
    h2 = _rms_fwd("rms_ffn", x1, g_ffn)
    up = _mm_up(h2, Wup).reshape(2, FFN_SLABS, T, UP_SHARD)
    u = _glu_fwd(up, cw, cb, S)
    x2 = _mm_down(u, Wdown, x1)

    dx2, dg_final, loss_part = _loss_head(x2, tgt2d, g_final2)
    received = {}
    d_u = _mm_down_t(dx2, Wdown)
    dW_down = _mm_dw_down(u, dx2)
    (d_up, d_cw, d_cb), (received["w_down"],) = _glu_bwd(
        up, d_u, cw, cb, S, comm=_Exchange([dW_down.reshape(N_DEV, -1, D_MODEL)]))
    d_up = d_up.reshape(N_DEV, T, UP_SHARD)
    dW_up = _mm_dw_up(h2, d_up)
    d_h2, (received["w_up"],) = _mm_up_t(d_up, Wup, comm=_Exchange([dW_up]))
    dx1, dg_ffn = _rms_bwd("rms_ffn_bwd", x1, g_ffn, d_h2, dx2)
    d_merged = _mm_rows("mm_out_t", dx1, Wout, kind="nt")
    dW_out = _mm_tn("mm_dw_out", merged, dx1, BF16)
    (d_gl, d_y_pool, d_y_ret, d_y_mem), (received["w_out"],) = _merge_bwd(
        proj, ys, d_merged, comm=_Exchange([dW_out.reshape(N_DEV, -1, D_MODEL)]))
    dW_a = _mm_tn("mm_dw_a", ypre, d_y_pool, BF16)
    d_ypre = _mm_rows("mm_a_t", d_y_pool, Wa, kind="nt")
    d_hp, dw_pool, d_scale = _pool_bwd(proj, d_ypre, wp, pool_scale, B, S)
    dW_r = _mm_tn("mm_dw_r", yr, d_y_ret, BF16)
    d_yr = _mm_rows("mm_r_t", d_y_ret, Wr, kind="nt")
    (d_q, d_k, d_v, d_gr, dg_ret, db_ret), (received["w_a"], received["w_r"]) = _ret_bwd(
        proj, d_yr, g_ret, b_ret, tables, B, S,
        comm=_Exchange([dW_a.reshape(POOL_WIDTH, N_DEV, -1).transpose(1, 0, 2), dW_r.reshape(N_DEV, -1, D_MODEL)]))
    dW_c = _mm_tn("mm_dw_c", o_mem, d_y_mem, BF16)
    d_o_mem = _mm_rows("mm_c_t", d_y_mem, Wc, kind="nt")
    d_qx, d_kmem, d_vmem = _xa_bwd(proj, kv, d_o_mem, B, S, M)
    d_kv = jnp.concatenate([d_kmem, d_vmem], axis=1)
    dW_kv = _mm_tn("mm_dw_kv", mem_n, d_kv, BF16)
    d_mem_n = _mm_rows("mm_kv_t", d_kv, Wkv, kind="nt")
    dg_mem = _rms_bwd("rms_mem_bwd", mem2d, g_mem, d_mem_n, None)
    d_cw_full = d_cw.transpose(1, 0, 2).reshape(3, FFN_HIDDEN)
    small_names = ["w_pool", "pool_scale", "g_ret", "b_ret", "g_mem", "g_ffn", "conv_b", "g_final"]
    small_grads = [dw_pool, d_scale, dg_ret, db_ret, dg_mem, dg_ffn, d_cb.reshape(1, FFN_HIDDEN), dg_final]
    packed = _pack(small_grads + [d_cw_full, loss_part])
    d_proj = jnp.concatenate([d_hp, d_q, d_k, d_v, d_gr, d_qx, d_gl], axis=1)
    dW_in, (received["w_c"], received["w_mem_kv"], packed_all) = _mm_tn_slab(
        "mm_dw_in", h, d_proj, IN_SHARD, BF16,
        comm=_Exchange([dW_c.reshape(XA_WIDTH, N_DEV, -1).transpose(1, 0, 2), dW_kv.reshape(N_DEV, -1, D_MODEL)],
                       whole=[packed]))
    d_h, (received["w_in"],) = _mm_cols_slab_t("mm_in_t", d_proj, Win, comm=_Exchange([dW_in]))
    grad_x, dg_mix = _rms_bwd("rms_mix_bwd", x2d, g_mix, d_h, dx1)
    (g_mix_all,) = _comm_call("gather_g_mix", _Exchange([], whole=[_pack([dg_mix])]))

    args = dict(g_mix=g_mix, w_in=w_in, w_pool=w_pool, pool_scale=pool_scale, w_a=w_a, g_ret=g_ret, b_ret=b_ret,
                w_r=w_r, g_mem=g_mem, w_mem_kv=w_mem_kv, w_c=w_c, w_out=w_out, g_ffn=g_ffn, w_up=w_up,
                conv_w=conv_w, conv_b=conv_b, w_down=w_down, g_final=g_final)
    m_in = dict(g_mix=m_g_mix, w_in=m_w_in, w_pool=m_w_pool, pool_scale=m_pool_scale, w_a=m_w_a, g_ret=m_g_ret,
                b_ret=m_b_ret, w_r=m_w_r, g_mem=m_g_mem, w_mem_kv=m_w_mem_kv, w_c=m_w_c, w_out=m_w_out,
                g_ffn=m_g_ffn, w_up=m_w_up, conv_w=m_conv_w, conv_b=m_conv_b, w_down=m_w_down, g_final=m_g_final)
    v_in = dict(g_mix=v_g_mix, w_in=v_w_in, w_pool=v_w_pool, pool_scale=v_pool_scale, w_a=v_w_a, g_ret=v_g_ret,
                b_ret=v_b_ret, w_r=v_w_r, g_mem=v_g_mem, w_mem_kv=v_w_mem_kv, w_c=v_w_c, w_out=v_w_out,
                g_ffn=v_g_ffn, w_up=v_w_up, conv_w=v_conv_w, conv_b=v_conv_b, w_down=v_w_down, g_final=v_g_final)

    grads, deltas, new_m, new_v = {}, {}, {}, {}
    for n in names:
        shard = big[n].shape
        outs = _reduce_adam("adam_" + n, received[n], big[n], m_in[n][0], v_in[n][0])
        for store, val in zip((grads, deltas, new_m, new_v), outs):
            store[n] = val.reshape((1,) + shard)

    zeros_tail = [jnp.zeros((3, FFN_HIDDEN), F32), jnp.zeros((1, PACK_LANES), F32)]
    w_small = _pack([args[n] for n in small_names] + zeros_tail)
    m_small = _pack([m_in[n] for n in small_names] + zeros_tail)
    v_small = _pack([v_in[n] for n in small_names] + zeros_tail)
    small_out = _reduce_adam("adam_small", packed_all, w_small, m_small, v_small, tr=w_small.shape[0])
    small_shapes = [args[n].shape for n in small_names] + [(3, FFN_HIDDEN), (1, PACK_LANES)]
    for store, val in zip((grads, deltas, new_m, new_v), small_out):
        parts = _unpack(val, small_shapes)
        for n, p in zip(small_names, parts):
            store[n] = p
        if store is grads:
            g_cw_full, loss_row = parts[-2], parts[-1]
    loss = loss_row[0, 0]
    g_mix_out = _reduce_adam("adam_g_mix", g_mix_all, _pack([g_mix]), _pack([m_g_mix]), _pack([v_g_mix]), tr=8)
    for store, val in zip((grads, deltas, new_m, new_v), g_mix_out):
        store["g_mix"] = _unpack(val, [g_mix.shape])[0]

    shard_cols = FFN_HIDDEN // N_DEV
    g_cw = lax.dynamic_slice_in_dim(g_cw_full, me * shard_cols, shard_cols, axis=1)
    d_, m_, v_ = _adam_only("adam_conv_w", g_cw, conv_w[0], m_conv_w[0], v_conv_w[0])
    grads["conv_w"], deltas["conv_w"], new_m["conv_w"], new_v["conv_w"] = g_cw[None], d_[None], m_[None], v_[None]

    order = ["g_mix", "w_in", "w_pool", "pool_scale", "w_a", "g_ret", "b_ret", "w_r", "g_mem", "w_mem_kv", "w_c",
             "w_out", "g_ffn", "w_up", "conv_w", "conv_b", "w_down", "g_final"]
    return (loss, grad_x.reshape(B, S, D_MODEL), *[grads[n] for n in order], *[deltas[n] for n in order],
            *[new_m[n] for n in order], *[new_v[n] for n in order])
```

```python
import functools
import math

import jax
import jax.numpy as jnp
from jax import lax
from jax.experimental import pallas as pl
from jax.experimental.pallas import tpu as pltpu

F32 = jnp.float32
BF16 = jnp.bfloat16

N_DEV = 8
D_MODEL = 1024
POOL_WINDOWS = (2, 4, 8, 16)
POOL_GROUP_DIM = 128
POOL_WIDTH = 512
POOL_HALO = 16
RET_HEADS = 4
RET_QK_DIM = 128
RET_V_DIM = 256
RET_CHUNK = 128
ROPE_BASE = 10000.0
XA_HEADS = 4
XA_HEAD_DIM = 128
XA_WIDTH = 512
IN_WIDTH = 7168
IN_SHARD = IN_WIDTH // N_DEV
FFN_HIDDEN = 2816
UP_SHARD = 2 * FFN_HIDDEN // N_DEV
FFN_SLABS = FFN_HIDDEN // UP_SHARD
EPS = 1e-6
ADAM_LR = 0.001
ADAM_B1 = 0.9
ADAM_B2 = 0.999
ADAM_EPS = 1e-08
ADAM_WD = 0.01
ADAM_STEP = 10
GELU_C = math.sqrt(2.0 / math.pi)
GELU_A = 0.044715
VMEM_LIMIT = 56 * 1024 * 1024
MM_ROWS = 1024
MM_TOKENS = 1024
MESH = pl.DeviceIdType.MESH

COL_Q, COL_K, COL_V, COL_GR, COL_QX, COL_GL = 512, 1024, 1536, 2560, 3584, 4096

_DIMS = {
    "nn": (((1,), (0,)), ((), ())),
    "nt": (((1,), (1,)), ((), ())),
    "tn": (((0,), (0,)), ((), ())),
}


def _dot(a, b, kind="nn"):
    return lax.dot_general(a.astype(BF16), b.astype(BF16), _DIMS[kind], preferred_element_type=F32)


def _params(sem, vmem=VMEM_LIMIT):
    return pltpu.CompilerParams(dimension_semantics=sem, vmem_limit_bytes=vmem)


def _tile(n, pref):
    t = min(n, pref)
    while n % t:
        t //= 2
    return t


def _mesh_pos():
    return lax.axis_index("x"), lax.axis_index("y"), lax.axis_index("c")


def _dev_index(x, y, c):
    return 4 * x + 2 * y + c


def _my_index():
    return _dev_index(*_mesh_pos())


def _remote(src, dst, send_sems, recv_sems, s, to):
    return pltpu.make_async_remote_copy(src_ref=src, dst_ref=dst, send_sem=send_sems.at[s], recv_sem=recv_sems.at[s],
                                        device_id=to, device_id_type=MESH)


class _Gather:
    def __init__(self, shards):
        self.inputs = list(shards)
        self.out_shapes = [jax.ShapeDtypeStruct((N_DEV,) + s.shape, s.dtype) for s in shards]
        n = len(shards)
        self.sem_shapes = [pltpu.SemaphoreType.DMA((7 * n,)), pltpu.SemaphoreType.DMA((7 * n,)),
                           pltpu.SemaphoreType.DMA((n,))]

    def _places(self):
        x, y, c = _mesh_pos()
        return (x, y, c), (x, y, 1 - c), [(1 - x, y), (x, 1 - y), (1 - x, 1 - y)]

    def _local(self, src, dst, sems):
        me = _my_index()
        return [pltpu.make_async_copy(src[w], dst[w].at[me], sems[2].at[w]) for w in range(len(src))]

    def start(self, src, dst, sems):
        me, sib, chips = self._places()
        for cp in self._local(src, dst, sems):
            cp.start()
        for w in range(len(src)):
            land = dst[w].at[_dev_index(*me)]
            _remote(src[w], land, sems[0], sems[1], 7 * w, sib).start()
            for j, chip in enumerate(chips):
                _remote(src[w], land, sems[0], sems[1], 7 * w + 1 + j, (*chip, me[2])).start()

    def finish(self, src, dst, sems):
        me, sib, chips = self._places()
        n = len(src)
        for j, chip in enumerate(chips):
            for w in range(n):
                block = dst[w].at[_dev_index(*chip, me[2])]
                _remote(src[w], block, sems[0], sems[1], 7 * w + 1 + j, me).wait_recv()
                _remote(block, block, sems[0], sems[1], 7 * w + 4 + j, sib).start()
        for w in range(n):
            _remote(src[w], dst[w].at[_dev_index(*sib)], sems[0], sems[1], 7 * w, me).wait_recv()
            for j, chip in enumerate(chips):
                block = dst[w].at[_dev_index(*chip, sib[2])]
                _remote(block, block, sems[0], sems[1], 7 * w + 4 + j, me).wait_recv()
            for k in range(7):
                _remote(src[w], dst[w].at[0], sems[0], sems[1], 7 * w + k, me).wait_send()
        for cp in self._local(src, dst, sems):
            cp.wait()


class _Exchange:
    def __init__(self, partials, whole=()):
        self.n_part = len(partials)
        self.inputs = list(partials) + list(whole)
        self.out_shapes = [jax.ShapeDtypeStruct(p.shape, p.dtype) for p in partials]
        self.out_shapes += [jax.ShapeDtypeStruct((N_DEV,) + a.shape, a.dtype) for a in whole]
        n = len(self.inputs)
        self.sem_shapes = [pltpu.SemaphoreType.DMA((7 * n,)), pltpu.SemaphoreType.DMA((7 * n,)),
                           pltpu.SemaphoreType.DMA((n,))]

    def _peer(self, k):
        x, y, c = _mesh_pos()
        p = (x ^ ((k >> 2) & 1), y ^ ((k >> 1) & 1), c ^ (k & 1))
        return p, _dev_index(*p)

    def _source(self, src, w, slot):
        return src[w].at[slot] if w < self.n_part else src[w]

    def _local(self, src, dst, sems):
        me = _my_index()
        return [pltpu.make_async_copy(self._source(src, w, me), dst[w].at[me], sems[2].at[w])
                for w in range(len(src))]

    def start(self, src, dst, sems):
        me = _my_index()
        for cp in self._local(src, dst, sems):
            cp.start()
        for k in range(1, N_DEV):
            peer, peer_idx = self._peer(k)
            for w in range(len(src)):
                _remote(self._source(src, w, peer_idx), dst[w].at[me], sems[0], sems[1], 7 * w + k - 1, peer).start()

    def finish(self, src, dst, sems):
        for k in range(1, N_DEV):
            peer, peer_idx = self._peer(k)
            for w in range(len(src)):
                cp = _remote(self._source(src, w, peer_idx), dst[w].at[peer_idx], sems[0], sems[1], 7 * w + k - 1, peer)
                cp.wait_send()
                cp.wait_recv()
        for cp in self._local(src, dst, sems):
            cp.wait()


def _pcall(body, args, *, name, out_shape, grid, in_specs, out_specs, scratch_shapes=(), sem=None, comm=None):
    single = not isinstance(out_shape, (tuple, list))
    outs = [out_shape] if single else list(out_shape)
    ospecs = [out_specs] if single else list(out_specs)
    n_in, n_out, n_scr = len(args), len(outs), len(scratch_shapes)

    def pick(res):
        return res[0] if single else tuple(res[:n_out])

    if comm is None:
        res = pl.pallas_call(
            body, out_shape=outs, grid=grid, in_specs=list(in_specs), out_specs=ospecs,
            scratch_shapes=list(scratch_shapes), name=name, compiler_params=_params(sem),
        )(*args)
        return pick(res), ()

    nci, nco = len(comm.inputs), len(comm.out_shapes)

    def carrier(*refs):
        at = 0
        parts = []
        for size in (n_in, nci, n_out, nco, n_scr, 3):
            parts.append(refs[at:at + size])
            at += size
        ins, cins, o, couts, scr, sems = parts
        ids = [pl.program_id(a) for a in range(len(grid))]
        first = functools.reduce(jnp.logical_and, [i == 0 for i in ids])
        last = functools.reduce(jnp.logical_and, [i == g - 1 for i, g in zip(ids, grid)])

        @pl.when(first)
        def _():
            comm.start(cins, couts, sems)

        body(*ins, *o, *scr)

        @pl.when(last)
        def _():
            comm.finish(cins, couts, sems)

    hbm = pl.BlockSpec(memory_space=pltpu.HBM)
    res = pl.pallas_call(
        carrier, out_shape=outs + comm.out_shapes, grid=grid, in_specs=list(in_specs) + [hbm] * nci,
        out_specs=ospecs + [hbm] * nco, scratch_shapes=list(scratch_shapes) + comm.sem_shapes, name=name,
        compiler_params=_params(("arbitrary",) * len(grid)),
    )(*args, *comm.inputs)
    return pick(res), tuple(res[n_out:])


def _comm_call(name, comm):
    def body(*refs):
        nci, nco = len(comm.inputs), len(comm.out_shapes)
        cins, couts, sems = refs[:nci], refs[nci:nci + nco], refs[nci + nco:]
        comm.start(cins, couts, sems)
        comm.finish(cins, couts, sems)

    hbm = pl.BlockSpec(memory_space=pltpu.HBM)
    return pl.pallas_call(
        body, out_shape=comm.out_shapes, in_specs=[hbm] * len(comm.inputs), out_specs=[hbm] * len(comm.out_shapes),
        scratch_shapes=comm.sem_shapes, name=name,
    )(*comm.inputs)


def _matmul(name, kind, a, b, out_shape, grid, a_spec, b_spec, o_spec, acc_shape, res=None, res_spec=None,
            comm=None):
    nk = grid[-1]
    has_res = res is not None

    def body(*refs):
        a_ref, b_ref = refs[0], refs[1]
        o_ref = refs[2 + has_res]

        def prod():
            return _dot(a_ref[...], b_ref[...], kind)

        def finish(acc):
            if has_res:
                acc = acc + refs[2][...]
            o_ref[...] = acc.astype(o_ref.dtype)

        if nk == 1:
            finish(prod())
        else:
            acc_ref = refs[3 + has_res]
            k = pl.program_id(len(grid) - 1)

            @pl.when(k == 0)
            def _():
                acc_ref[...] = prod()

            @pl.when(k > 0)
            def _():
                acc_ref[...] += prod()

            @pl.when(k == nk - 1)
            def _():
                finish(acc_ref[...])

    in_specs = [a_spec, b_spec] + ([res_spec] if has_res else [])
    args = (a, b) + ((res,) if has_res else ())
    scratch = [pltpu.VMEM(acc_shape, F32)] if nk > 1 else []
    sem = ("parallel",) * (len(grid) - 1) + ("arbitrary",)
    out, landed = _pcall(body, args, name=name, out_shape=out_shape, grid=grid, in_specs=in_specs,
                         out_specs=o_spec, scratch_shapes=scratch, sem=sem, comm=comm)
    return out if comm is None else (out, landed)


def _mm_rows(name, a, w, out_dtype=F32, res=None, kind="nn", tm=MM_ROWS, comm=None):
    M, K = a.shape
    N = w.shape[1] if kind == "nn" else w.shape[0]
    tm = _tile(M, tm)
    res_spec = pl.BlockSpec((tm, N), lambda i, k: (i, 0)) if res is not None else None
    return _matmul(
        name, kind, a, w, jax.ShapeDtypeStruct((M, N), out_dtype), (M // tm, 1),
        pl.BlockSpec((tm, K), lambda i, k: (i, 0)), pl.BlockSpec(w.shape, lambda i, k: (0, 0)),
        pl.BlockSpec((tm, N), lambda i, k: (i, 0)), (tm, N), res, res_spec, comm)


def _mm_tn(name, a, b, out_dtype=F32, tk=MM_TOKENS, comm=None):
    T, M = a.shape
    N = b.shape[1]
    tk = _tile(T, tk)
    return _matmul(
        name, "tn", a, b, jax.ShapeDtypeStruct((M, N), out_dtype), (1, T // tk),
        pl.BlockSpec((tk, M), lambda i, k: (k, 0)), pl.BlockSpec((tk, N), lambda i, k: (k, 0)),
        pl.BlockSpec((M, N), lambda i, k: (0, 0)), (M, N), comm=comm)


def _mm_cols_slab(name, a, w_slabs, out_dtype=F32, tm=MM_ROWS, comm=None):
    M, K = a.shape
    J, _, n = w_slabs.shape
    tm = _tile(M, tm)
    return _matmul(
        name, "nn", a, w_slabs, jax.ShapeDtypeStruct((M, J * n), out_dtype), (J, M // tm, 1),
        pl.BlockSpec((tm, K), lambda j, i, k: (i, 0)), pl.BlockSpec((None, K, n), lambda j, i, k: (j, 0, 0)),
        pl.BlockSpec((tm, n), lambda j, i, k: (i, j)), (tm, n), comm=comm)


def _mm_cols_slab_t(name, a, w_slabs, out_dtype=F32, tm=MM_ROWS, comm=None):
    M = a.shape[0]
    J, K, n = w_slabs.shape
    tm = _tile(M, tm)
    return _matmul(
        name, "nt", a, w_slabs, jax.ShapeDtypeStruct((M, K), out_dtype), (M // tm, J),
        pl.BlockSpec((tm, n), lambda i, j: (i, j)), pl.BlockSpec((None, K, n), lambda i, j: (j, 0, 0)),
        pl.BlockSpec((tm, K), lambda i, j: (i, 0)), (tm, K), comm=comm)


def _mm_tn_slab(name, a, b, n, out_dtype=F32, tk=MM_TOKENS, comm=None, part=(0, 1)):
    T, M = a.shape
    p, of = part
    M //= of
    J = b.shape[1] // n
    tk = _tile(T, tk)
    return _matmul(
        name, "tn", a, b, jax.ShapeDtypeStruct((J, M, n), out_dtype), (J, T // tk),
        pl.BlockSpec((tk, M), lambda j, k: (k, p)), pl.BlockSpec((tk, n), lambda j, k: (k, j)),
        pl.BlockSpec((None, M, n), lambda j, k: (j, 0, 0)), (M, n), comm=comm)


def _rms_fwd(name, x, g, tm=512):
    T, Dm = x.shape
    tm = _tile(T, tm)

    def body(x_ref, g_ref, h_ref):
        xv = x_ref[...]
        r = lax.rsqrt(jnp.mean(xv * xv, axis=-1, keepdims=True) + EPS)
        h_ref[...] = (xv * r * g_ref[...]).astype(h_ref.dtype)

    return pl.pallas_call(
        body, out_shape=jax.ShapeDtypeStruct((T, Dm), BF16), grid=(T // tm,),
        in_specs=[pl.BlockSpec((tm, Dm), lambda i: (i, 0)), pl.BlockSpec((1, Dm), lambda i: (0, 0))],
        out_specs=pl.BlockSpec((tm, Dm), lambda i: (i, 0)), name=name, compiler_params=_params(("parallel",)),
    )(x, g)


def _rms_bwd(name, x, g, dh, dres, tm=512):
    T, Dm = x.shape
    tm = _tile(T, tm)
    want_dx = dres is not None

    def body(*refs):
        if want_dx:
            x_ref, g_ref, dh_ref, dres_ref, dx_ref, dg_ref = refs
        else:
            x_ref, g_ref, dh_ref, dg_ref = refs
        xv = x_ref[...]
        r = lax.rsqrt(jnp.mean(xv * xv, axis=-1, keepdims=True) + EPS)
        xhat = xv * r
        dhv = dh_ref[...]

        @pl.when(pl.program_id(0) == 0)
        def _():
            dg_ref[...] = jnp.zeros_like(dg_ref)

        dg_ref[...] += jnp.sum(dhv * xhat, axis=0, keepdims=True)
        if want_dx:
            dxhat = dhv * g_ref[...]
            dx_ref[...] = dres_ref[...] + r * (dxhat - xhat * jnp.mean(dxhat * xhat, axis=-1, keepdims=True))

    row = pl.BlockSpec((tm, Dm), lambda i: (i, 0))
    vec = pl.BlockSpec((1, Dm), lambda i: (0, 0))
    if want_dx:
        return pl.pallas_call(
            body, out_shape=(jax.ShapeDtypeStruct((T, Dm), F32), jax.ShapeDtypeStruct((1, Dm), F32)),
            grid=(T // tm,), in_specs=[row, vec, row, row], out_specs=(row, vec), name=name,
            compiler_params=_params(("arbitrary",)),
        )(x, g, dh, dres)
    return pl.pallas_call(
        body, out_shape=jax.ShapeDtypeStruct((1, Dm), F32), grid=(T // tm,), in_specs=[row, vec, row],
        out_specs=vec, name=name, compiler_params=_params(("arbitrary",)),
    )(x, g, dh)


def _pool_rows(S):
    return _tile(S, 256)


def _pool_count(c0, rows, w):
    t = c0 + lax.broadcasted_iota(jnp.int32, (rows, 1), 0)
    return jnp.minimum(t + 1, w).astype(F32)


def _pool_fwd(proj, w_pool, scale, B, S):
    CH = _pool_rows(S)

    def body(hp_ref, wp_ref, sc_ref, o_ref, pad_ref):
        pad_ref[0:POOL_HALO, :] = jnp.zeros((POOL_HALO, POOL_WIDTH), F32)
        pad_ref[POOL_HALO:, :] = hp_ref[...]
        for gi, w in enumerate(POOL_WINDOWS):
            cols = slice(gi * POOL_GROUP_DIM, (gi + 1) * POOL_GROUP_DIM)
            for c in range(S // CH):
                base = POOL_HALO + c * CH
                acc = pad_ref[base:base + CH, cols]
                tok = acc
                for j in range(1, w):
                    acc = acc + pad_ref[base - j:base - j + CH, cols]
                pooled = acc / _pool_count(c * CH, CH, w) - tok
                z = _dot(pooled, wp_ref[gi])
                o_ref[c * CH:(c + 1) * CH, cols] = (z * sc_ref[:, cols]).astype(o_ref.dtype)

    return pl.pallas_call(
        body, out_shape=jax.ShapeDtypeStruct((B * S, POOL_WIDTH), BF16), grid=(B,),
        in_specs=[pl.BlockSpec((S, POOL_WIDTH), lambda b: (b, 0)),
                  pl.BlockSpec(w_pool.shape, lambda b: (0, 0, 0)),
                  pl.BlockSpec((1, POOL_WIDTH), lambda b: (0, 0))],
        out_specs=pl.BlockSpec((S, POOL_WIDTH), lambda b: (b, 0)),
        scratch_shapes=[pltpu.VMEM((S + POOL_HALO, POOL_WIDTH), F32)],
        name="pool_fwd", compiler_params=_params(("parallel",)),
    )(proj, w_pool, scale)


def _pool_bwd(proj, d_ypre, w_pool, scale, B, S):
    CH = _pool_rows(S)

    def body(hp_ref, dy_ref, wp_ref, sc_ref, dhp_ref, dwp_ref, dsc_ref, pad_ref, sc_pad_ref, dp_ref):
        @pl.when(pl.program_id(0) == 0)
        def _():
            dwp_ref[...] = jnp.zeros_like(dwp_ref)
            dsc_ref[...] = jnp.zeros_like(dsc_ref)

        pad_ref[0:POOL_HALO, :] = jnp.zeros((POOL_HALO, POOL_WIDTH), F32)
        pad_ref[POOL_HALO:, :] = hp_ref[...]
        sc_pad_ref[S:, :] = jnp.zeros((POOL_HALO, POOL_WIDTH), F32)
        for gi, w in enumerate(POOL_WINDOWS):
            cols = slice(gi * POOL_GROUP_DIM, (gi + 1) * POOL_GROUP_DIM)
            for c in range(S // CH):
                base = POOL_HALO + c * CH
                rows = slice(c * CH, (c + 1) * CH)
                acc = pad_ref[base:base + CH, cols]
                tok = acc
                for j in range(1, w):
                    acc = acc + pad_ref[base - j:base - j + CH, cols]
                cnt = _pool_count(c * CH, CH, w)
                pooled = acc / cnt - tok
                z = _dot(pooled, wp_ref[gi])
                dy = dy_ref[rows, cols]
                dsc_ref[:, cols] += jnp.sum(dy * z, axis=0, keepdims=True)
                dz = dy * sc_ref[:, cols]
                dwp_ref[gi] += _dot(pooled, dz, "tn")
                dpool = _dot(dz, wp_ref[gi], "nt")
                dp_ref[rows, cols] = dpool
                sc_pad_ref[rows, cols] = dpool / cnt
            for c in range(S // CH):
                rows = slice(c * CH, (c + 1) * CH)
                acc = sc_pad_ref[rows, cols]
                for j in range(1, w):
                    acc = acc + sc_pad_ref[c * CH + j:c * CH + j + CH, cols]
                dhp_ref[rows, cols] = (acc - dp_ref[rows, cols]).astype(dhp_ref.dtype)

    seq = pl.BlockSpec((S, POOL_WIDTH), lambda b: (b, 0))
    return pl.pallas_call(
        body,
        out_shape=(jax.ShapeDtypeStruct((B * S, POOL_WIDTH), BF16),
                   jax.ShapeDtypeStruct(w_pool.shape, F32), jax.ShapeDtypeStruct((1, POOL_WIDTH), F32)),
        grid=(B,),
        in_specs=[seq, seq, pl.BlockSpec(w_pool.shape, lambda b: (0, 0, 0)),
                  pl.BlockSpec((1, POOL_WIDTH), lambda b: (0, 0))],
        out_specs=(seq, pl.BlockSpec(w_pool.shape, lambda b: (0, 0, 0)),
                   pl.BlockSpec((1, POOL_WIDTH), lambda b: (0, 0))),
        scratch_shapes=[pltpu.VMEM((S + POOL_HALO, POOL_WIDTH), F32),
                        pltpu.VMEM((S + POOL_HALO, POOL_WIDTH), F32),
                        pltpu.VMEM((S, POOL_WIDTH), F32)],
        name="pool_bwd", compiler_params=_params(("arbitrary",)),
    )(proj, d_ypre, w_pool, scale)


def _ret_tables(S):
    half = RET_QK_DIM // 2
    inv = ROPE_BASE ** (-jnp.arange(half, dtype=F32) / half)
    ang = jnp.arange(S, dtype=F32)[:, None] * inv[None, :]
    cos, sin = jnp.cos(ang), jnp.sin(ang)
    cos_full = jnp.concatenate([cos, cos], axis=-1)
    sin_signed = jnp.concatenate([-sin, sin], axis=-1)
    C = RET_CHUNK
    lg = jnp.log1p(-jnp.exp2(-5.0 - jnp.arange(RET_HEADS, dtype=F32)))[:, None, None]
    idx = jnp.arange(C, dtype=F32)
    rel = idx[:, None] - idx[None, :]
    decay = jnp.where(rel >= 0, jnp.exp(jnp.maximum(rel, 0.0) * lg), 0.0)
    q_decay = jnp.broadcast_to(jnp.exp((idx + 1.0)[None, :, None] * lg), (RET_HEADS, C, RET_QK_DIM))
    k_decay = jnp.broadcast_to(jnp.exp((C - 1.0 - idx)[None, :, None] * lg), (RET_HEADS, C, RET_QK_DIM))
    c_decay = jnp.broadcast_to(jnp.exp(C * lg), (RET_HEADS, 1, RET_V_DIM))
    return cos_full, sin_signed, decay, q_decay, k_decay, c_decay


def _rope(x, cos_full, sin_signed):
    return x * cos_full + pltpu.roll(x, RET_QK_DIM // 2, axis=1) * sin_signed


def _rope_t(dy, cos_full, sin_signed):
    return dy * cos_full + pltpu.roll(dy * sin_signed, RET_QK_DIM // 2, axis=1)


def _ret_specs(S, bh):
    def at(col_of_head, width):
        def index(*ids):
            b, h = bh(*ids)
            return (b, col_of_head + h)
        return pl.BlockSpec((S, width), index)

    def per_head(shape):
        def index(*ids):
            _, h = bh(*ids)
            return (h,) + (0,) * len(shape)
        return pl.BlockSpec((None,) + shape, index)

    def head_vec(width):
        def index(*ids):
            _, h = bh(*ids)
            return (0, h)
        return pl.BlockSpec((1, width), index)

    table = pl.BlockSpec((S, RET_QK_DIM), lambda *ids: (0, 0))
    C = RET_CHUNK
    return dict(
        q=at(COL_Q // RET_QK_DIM, RET_QK_DIM), k=at(COL_K // RET_QK_DIM, RET_QK_DIM),
        v=at(COL_V // RET_V_DIM, RET_V_DIM), gr=at(COL_GR // RET_V_DIM, RET_V_DIM),
        table=table, decay=per_head((C, C)), qd=per_head((C, RET_QK_DIM)), kd=per_head((C, RET_QK_DIM)),
        cd=per_head((1, RET_V_DIM)), vec=head_vec(RET_V_DIM), out_qk=at(0, RET_QK_DIM), out_v=at(0, RET_V_DIM))


def _group_norm(o):
    mu = jnp.mean(o, axis=-1, keepdims=True)
    oc = o - mu
    rstd = lax.rsqrt(jnp.mean(oc * oc, axis=-1, keepdims=True) + EPS)
    return oc * rstd, rstd


def _ret_fwd(proj, g_ret, b_ret, tables, B, S, comm=None):
    C = RET_CHUNK
    cos_t, sin_t, decay, q_decay, k_decay, c_decay = tables
    sp = _ret_specs(S, lambda b, h: (b, h))

    def body(q_ref, k_ref, v_ref, gr_ref, cos_ref, sin_ref, dec_ref, qd_ref, kd_ref, cd_ref, g_ref, b_ref,
             y_ref, r_ref):
        r_ref[...] = jnp.zeros_like(r_ref)

        def chunk(i, carry):
            rows = pl.ds(pl.multiple_of(i * C, C), C)
            cs, sn = cos_ref[rows, :], sin_ref[rows, :]
            q = _rope(q_ref[rows, :], cs, sn)
            k = _rope(k_ref[rows, :], cs, sn) * (RET_QK_DIM ** -0.5)
            v = v_ref[rows, :]
            R = r_ref[...]
            s = _dot(q, k, "nt") * dec_ref[...]
            o = _dot(s, v) + _dot(q * qd_ref[...], R)
            r_ref[...] = cd_ref[...] * R + _dot(k * kd_ref[...], v, "tn")
            on, _ = _group_norm(o)
            gr = gr_ref[rows, :]
            y_ref[rows, :] = (gr * jax.nn.sigmoid(gr) * (on * g_ref[...] + b_ref[...])).astype(y_ref.dtype)
            return carry

        lax.fori_loop(0, S // C, chunk, 0)

    return _pcall(
        body, (proj, proj, proj, proj, cos_t, sin_t, decay, q_decay, k_decay, c_decay, g_ret, b_ret),
        name="ret_fwd", out_shape=jax.ShapeDtypeStruct((B * S, RET_HEADS * RET_V_DIM), BF16), grid=(B, RET_HEADS),
        in_specs=[sp["q"], sp["k"], sp["v"], sp["gr"], sp["table"], sp["table"], sp["decay"], sp["qd"],
                  sp["kd"], sp["cd"], sp["vec"], sp["vec"]],
        out_specs=sp["out_v"], scratch_shapes=[pltpu.VMEM((RET_QK_DIM, RET_V_DIM), F32)],
        sem=("parallel", "parallel"), comm=comm)


def _ret_bwd(proj, d_yr, g_ret, b_ret, tables, B, S, comm=None):
    C = RET_CHUNK
    N = S // C
    cos_t, sin_t, decay, q_decay, k_decay, c_decay = tables
    sp = _ret_specs(S, lambda h, b: (b, h))
    qk_scale = RET_QK_DIM ** -0.5

    def body(q_ref, k_ref, v_ref, gr_ref, dy_ref, cos_ref, sin_ref, dec_ref, qd_ref, kd_ref, cd_ref, g_ref, b_ref,
             dq_ref, dk_ref, dv_ref, dgr_ref, dg_ref, db_ref, qr_ref, kr_ref, rs_ref, dr_ref):
        @pl.when(pl.program_id(1) == 0)
        def _():
            dg_ref[...] = jnp.zeros_like(dg_ref)
            db_ref[...] = jnp.zeros_like(db_ref)

        dr_ref[...] = jnp.zeros_like(dr_ref)

        def sweep(i, R):
            rows = pl.ds(pl.multiple_of(i * C, C), C)
            cs, sn = cos_ref[rows, :], sin_ref[rows, :]
            q = _rope(q_ref[rows, :], cs, sn)
            k = _rope(k_ref[rows, :], cs, sn) * qk_scale
            qr_ref[rows, :] = q
            kr_ref[rows, :] = k
            rs_ref[i] = R
            return cd_ref[...] * R + _dot(k * kd_ref[...], v_ref[rows, :], "tn")

        lax.fori_loop(0, N, sweep, jnp.zeros((RET_QK_DIM, RET_V_DIM), F32))

        def back(step, carry):
            i = N - 1 - step
            rows = pl.ds(pl.multiple_of(i * C, C), C)
            q, k, v = qr_ref[rows, :], kr_ref[rows, :], v_ref[rows, :]
            R, dR = rs_ref[i], dr_ref[...]
            dec, qd, kd = dec_ref[...], qd_ref[...], kd_ref[...]
            s = _dot(q, k, "nt") * dec
            o = _dot(s, v) + _dot(q * qd, R)
            on, rstd = _group_norm(o)
            oaff = on * g_ref[...] + b_ref[...]
            gr = gr_ref[rows, :]
            sg = jax.nn.sigmoid(gr)
            dy = dy_ref[rows, :]
            dgr_ref[rows, :] = (dy * oaff * (sg * (1.0 + gr * (1.0 - sg)))).astype(dgr_ref.dtype)
            doaff = dy * (gr * sg)
            dg_ref[...] += jnp.sum(doaff * on, axis=0, keepdims=True)
            db_ref[...] += jnp.sum(doaff, axis=0, keepdims=True)
            don = doaff * g_ref[...]
            do = rstd * (don - jnp.mean(don, axis=-1, keepdims=True)
                         - on * jnp.mean(don * on, axis=-1, keepdims=True))
            ds = _dot(do, v, "nt") * dec
            dq = _dot(ds, k) + qd * _dot(do, R, "nt")
            dk = _dot(ds, q, "tn") + kd * _dot(v, dR, "nt")
            dv_ref[rows, :] = (_dot(s, do, "tn") + _dot(k * kd, dR)).astype(dv_ref.dtype)
            dr_ref[...] = cd_ref[...] * dR + _dot(q * qd, do, "tn")
            cs, sn = cos_ref[rows, :], sin_ref[rows, :]
            dq_ref[rows, :] = _rope_t(dq, cs, sn).astype(dq_ref.dtype)
            dk_ref[rows, :] = _rope_t(dk * qk_scale, cs, sn).astype(dk_ref.dtype)
            return carry

        lax.fori_loop(0, N, back, 0)

    T = B * S
    qk_shape = jax.ShapeDtypeStruct((T, RET_HEADS * RET_QK_DIM), BF16)
    v_shape = jax.ShapeDtypeStruct((T, RET_HEADS * RET_V_DIM), BF16)
    vec_shape = jax.ShapeDtypeStruct((1, RET_HEADS * RET_V_DIM), F32)
    return _pcall(
        body, (proj, proj, proj, proj, d_yr, cos_t, sin_t, decay, q_decay, k_decay, c_decay, g_ret, b_ret),
        name="ret_bwd", out_shape=(qk_shape, qk_shape, v_shape, v_shape, vec_shape, vec_shape), grid=(RET_HEADS, B),
        in_specs=[sp["q"], sp["k"], sp["v"], sp["gr"], sp["out_v"], sp["table"], sp["table"], sp["decay"],
                  sp["qd"], sp["kd"], sp["cd"], sp["vec"], sp["vec"]],
        out_specs=(sp["out_qk"], sp["out_qk"], sp["out_v"], sp["out_v"], sp["vec"], sp["vec"]),
        scratch_shapes=[pltpu.VMEM((S, RET_QK_DIM), F32), pltpu.VMEM((S, RET_QK_DIM), F32),
                        pltpu.VMEM((N, RET_QK_DIM, RET_V_DIM), F32), pltpu.VMEM((RET_QK_DIM, RET_V_DIM), F32)],
        sem=("parallel", "arbitrary"), comm=comm)


def _xa_rows(S):
    return _tile(S, 256)


def _xa_specs(S, M):
    q = pl.BlockSpec((S, XA_HEAD_DIM), lambda b, h: (b, COL_QX // XA_HEAD_DIM + h))
    k = pl.BlockSpec((M, XA_HEAD_DIM), lambda b, h: (b, h))
    v = pl.BlockSpec((M, XA_HEAD_DIM), lambda b, h: (b, XA_HEADS + h))
    o = pl.BlockSpec((S, XA_HEAD_DIM), lambda b, h: (b, h))
    return q, k, v, o


def _softmax_rows(s):
    e = jnp.exp(s - jnp.max(s, axis=-1, keepdims=True))
    return e / jnp.sum(e, axis=-1, keepdims=True)


def _xa_fwd(proj, kv, B, S, M):
    CH = _xa_rows(S)
    q_spec, k_spec, v_spec, o_spec = _xa_specs(S, M)

    def body(q_ref, k_ref, v_ref, o_ref):
        def chunk(i, carry):
            rows = pl.ds(pl.multiple_of(i * CH, CH), CH)
            p = _softmax_rows(_dot(q_ref[rows, :], k_ref[...], "nt") * (XA_HEAD_DIM ** -0.5))
            o_ref[rows, :] = _dot(p, v_ref[...]).astype(o_ref.dtype)
            return carry

        lax.fori_loop(0, S // CH, chunk, 0)

    return pl.pallas_call(
        body, out_shape=jax.ShapeDtypeStruct((B * S, XA_WIDTH), BF16), grid=(B, XA_HEADS),
        in_specs=[q_spec, k_spec, v_spec], out_specs=o_spec, name="xattn_fwd",
        compiler_params=_params(("parallel", "parallel")),
    )(proj, kv, kv)


def _xa_bwd(proj, kv, d_o, B, S, M, comm=None):
    CH = _xa_rows(S)
    q_spec, k_spec, v_spec, o_spec = _xa_specs(S, M)
    scale = XA_HEAD_DIM ** -0.5

    def body(q_ref, k_ref, v_ref, do_ref, dq_ref, dk_ref, dv_ref):
        dk_ref[...] = jnp.zeros_like(dk_ref)
        dv_ref[...] = jnp.zeros_like(dv_ref)

        def chunk(i, carry):
            rows = pl.ds(pl.multiple_of(i * CH, CH), CH)
            q, do = q_ref[rows, :], do_ref[rows, :]
            p = _softmax_rows(_dot(q, k_ref[...], "nt") * scale)
            dp = _dot(do, v_ref[...], "nt")
            ds = p * (dp - jnp.sum(dp * p, axis=-1, keepdims=True)) * scale
            dq_ref[rows, :] = _dot(ds, k_ref[...]).astype(dq_ref.dtype)
            dk_ref[...] += _dot(ds, q, "tn")
            dv_ref[...] += _dot(p, do, "tn")
            return carry

        lax.fori_loop(0, S // CH, chunk, 0)

    kv_out = pl.BlockSpec((M, XA_HEAD_DIM), lambda b, h: (b, h))
    return _pcall(
        body, (proj, kv, kv, d_o), name="xattn_bwd",
        out_shape=(jax.ShapeDtypeStruct((B * S, XA_WIDTH), BF16), jax.ShapeDtypeStruct((B * M, XA_WIDTH), F32),
                   jax.ShapeDtypeStruct((B * M, XA_WIDTH), F32)),
        grid=(B, XA_HEADS), in_specs=[q_spec, k_spec, v_spec, o_spec], out_specs=(o_spec, kv_out, kv_out),
        sem=("parallel", "parallel"), comm=comm)


def _gate_specs(tm):
    n = COL_GL // D_MODEL
    return [pl.BlockSpec((tm, D_MODEL), lambda i, j=j: (i, n + j)) for j in range(3)]


def _merge_fwd(proj, ys, tm=256):
    T = proj.shape[0]
    tm = _tile(T, tm)
    row = pl.BlockSpec((tm, D_MODEL), lambda i: (i, 0))

    def body(g0, g1, g2, y0, y1, y2, o_ref):
        acc = jax.nn.sigmoid(g0[...]) * y0[...]
        acc = acc + jax.nn.sigmoid(g1[...]) * y1[...]
        acc = acc + jax.nn.sigmoid(g2[...]) * y2[...]
        o_ref[...] = acc.astype(o_ref.dtype)

    return pl.pallas_call(
        body, out_shape=jax.ShapeDtypeStruct((T, D_MODEL), BF16), grid=(T // tm,),
        in_specs=_gate_specs(tm) + [row] * 3, out_specs=row, name="merge_fwd",
        compiler_params=_params(("parallel",)),
    )(proj, proj, proj, *ys)


def _merge_bwd(proj, ys, d_merged, tm=256, comm=None):
    T = proj.shape[0]
    tm = _tile(T, tm)
    row = pl.BlockSpec((tm, D_MODEL), lambda i: (i, 0))

    def body(g0, g1, g2, y0, y1, y2, dm_ref, dgl_ref, d0, d1, d2):
        dm = dm_ref[...]
        for j, (g_ref, y_ref, d_ref) in enumerate(((g0, y0, d0), (g1, y1, d1), (g2, y2, d2))):
            sg = jax.nn.sigmoid(g_ref[...])
            d_ref[...] = (dm * sg).astype(d_ref.dtype)
            dgl_ref[:, j * D_MODEL:(j + 1) * D_MODEL] = (dm * y_ref[...] * sg * (1.0 - sg)).astype(dgl_ref.dtype)

    dy = jax.ShapeDtypeStruct((T, D_MODEL), BF16)
    return _pcall(
        body, (proj, proj, proj, *ys, d_merged), name="merge_bwd",
        out_shape=(jax.ShapeDtypeStruct((T, 3 * D_MODEL), BF16), dy, dy, dy), grid=(T // tm,),
        in_specs=_gate_specs(tm) + [row] * 4,
        out_specs=(pl.BlockSpec((tm, 3 * D_MODEL), lambda i: (i, 0)), row, row, row),
        sem=("parallel",), comm=comm)


def _gelu(x):
    return 0.5 * x * (1.0 + jnp.tanh(GELU_C * (x + GELU_A * x * x * x)))


def _gelu_grad(x):
    t = jnp.tanh(GELU_C * (x + GELU_A * x * x * x))
    return 0.5 * (1.0 + t) + 0.5 * x * (1.0 - t * t) * GELU_C * (1.0 + 3.0 * GELU_A * x * x)


def _shift_down(x, prev, n):
    rows = x.shape[0]
    r = lax.broadcasted_iota(jnp.int32, (rows, 1), 0)
    out = pltpu.roll(x, n, axis=0)
    for j in range(n):
        out = jnp.where(r == j, prev[8 - n + j:8 - n + j + 1, :], out)
    return out


def _shift_up(x, nxt, n):
    rows = x.shape[0]
    r = lax.broadcasted_iota(jnp.int32, (rows, 1), 0)
    out = pltpu.roll(x, rows - n, axis=0)
    for j in range(n):
        out = jnp.where(r == rows - n + j, nxt[j:j + 1, :], out)
    return out


def _conv(a, prev, cw, cb):
    return _shift_down(a, prev, 2) * cw[0:1, :] + _shift_down(a, prev, 1) * cw[1:2, :] + a * cw[2:3, :] + cb


def _glu_fwd(up, cw, cb, S, tm=256):
    T = up.shape[2]
    tm = _tile(S, tm)
    per_seq = S // tm

    def body(ab_ref, prev_ref, cw_ref, cb_ref, u_ref):
        i = pl.program_id(1)
        prev = jnp.where(i % per_seq == 0, 0.0, prev_ref[...])
        ac = _conv(ab_ref[0], prev, cw_ref[...], cb_ref[...])
        u_ref[...] = (_gelu(ac) * ab_ref[1]).astype(u_ref.dtype)

    return pl.pallas_call(
        body, out_shape=jax.ShapeDtypeStruct((FFN_SLABS, T, UP_SHARD), BF16), grid=(FFN_SLABS, T // tm),
        in_specs=[pl.BlockSpec((2, None, tm, UP_SHARD), lambda d, i: (0, d, i, 0)),
                  pl.BlockSpec((None, None, 8, UP_SHARD), lambda d, i: (0, d, jnp.maximum(i * (tm // 8) - 1, 0), 0)),
                  pl.BlockSpec((None, 3, UP_SHARD), lambda d, i: (d, 0, 0)),
                  pl.BlockSpec((None, 1, UP_SHARD), lambda d, i: (d, 0, 0))],
        out_specs=pl.BlockSpec((None, tm, UP_SHARD), lambda d, i: (d, i, 0)), name="glu_fwd",
        compiler_params=_params(("parallel", "parallel")),
    )(up, up, cw, cb)


def _glu_bwd(up, d_u, cw, cb, S, tm=256, comm=None):
    T = up.shape[2]
    tm = _tile(S, tm)
    per_seq = S // tm
    n_tiles = T // tm
    last8 = tm // 8

    def body(ab_ref, prev_ref, abn_ref, du_ref, dun_ref, cw_ref, cb_ref, dup_ref, dcw_ref, dcb_ref):
        i = pl.program_id(1)

        @pl.when(i == 0)
        def _():
            dcw_ref[...] = jnp.zeros_like(dcw_ref)
            dcb_ref[...] = jnp.zeros_like(dcb_ref)

        cw, cb = cw_ref[...], cb_ref[...]
        a, b = ab_ref[0], ab_ref[1]
        prev = jnp.where(i % per_seq == 0, 0.0, prev_ref[...])
        a2, a1 = _shift_down(a, prev, 2), _shift_down(a, prev, 1)
        ac = a2 * cw[0:1, :] + a1 * cw[1:2, :] + a * cw[2:3, :] + cb
        du = du_ref[...]
        dup_ref[1] = (du * _gelu(ac)).astype(dup_ref.dtype)
        dac = du * b * _gelu_grad(ac)
        dcb_ref[...] += jnp.sum(dac, axis=0, keepdims=True)
        dcw_ref[0:1, :] += jnp.sum(dac * a2, axis=0, keepdims=True)
        dcw_ref[1:2, :] += jnp.sum(dac * a1, axis=0, keepdims=True)
        dcw_ref[2:3, :] += jnp.sum(dac * a, axis=0, keepdims=True)
        an = abn_ref[0]
        acn = _conv(an, a[tm - 8:, :], cw, cb)
        dacn = jnp.where(i % per_seq == per_seq - 1, 0.0, dun_ref[...] * abn_ref[1] * _gelu_grad(acn))
        da = dac * cw[2:3, :] + _shift_up(dac, dacn, 1) * cw[1:2, :] + _shift_up(dac, dacn, 2) * cw[0:1, :]
        dup_ref[0] = da.astype(dup_ref.dtype)

    def nxt(i):
        return jnp.minimum((i + 1) * last8, T // 8 - 1)

    return _pcall(
        body, (up, up, up, d_u, d_u, cw, cb), name="glu_bwd",
        out_shape=(jax.ShapeDtypeStruct((2, FFN_SLABS, T, UP_SHARD), BF16),
                   jax.ShapeDtypeStruct((FFN_SLABS, 3, UP_SHARD), F32),
                   jax.ShapeDtypeStruct((FFN_SLABS, 1, UP_SHARD), F32)),
        grid=(FFN_SLABS, n_tiles),
        in_specs=[pl.BlockSpec((2, None, tm, UP_SHARD), lambda d, i: (0, d, i, 0)),
                  pl.BlockSpec((None, None, 8, UP_SHARD), lambda d, i: (0, d, jnp.maximum(i * last8 - 1, 0), 0)),
                  pl.BlockSpec((2, None, 8, UP_SHARD), lambda d, i: (0, d, nxt(i), 0)),
                  pl.BlockSpec((None, tm, UP_SHARD), lambda d, i: (d, i, 0)),
                  pl.BlockSpec((None, 8, UP_SHARD), lambda d, i: (d, nxt(i), 0)),
                  pl.BlockSpec((None, 3, UP_SHARD), lambda d, i: (d, 0, 0)),
                  pl.BlockSpec((None, 1, UP_SHARD), lambda d, i: (d, 0, 0))],
        out_specs=(pl.BlockSpec((2, None, tm, UP_SHARD), lambda d, i: (0, d, i, 0)),
                   pl.BlockSpec((None, 3, UP_SHARD), lambda d, i: (d, 0, 0)),
                   pl.BlockSpec((None, 1, UP_SHARD), lambda d, i: (d, 0, 0))),
        sem=("parallel", "arbitrary"), comm=comm)


def _mm_up(h2, w_up, tm=MM_ROWS):
    T, K = h2.shape
    tm = _tile(T, tm)
    return _matmul(
        "mm_up", "nn", h2, w_up, jax.ShapeDtypeStruct((N_DEV, T, UP_SHARD), F32), (N_DEV, T // tm, 1),
        pl.BlockSpec((tm, K), lambda j, i, k: (i, 0)), pl.BlockSpec((None, K, UP_SHARD), lambda j, i, k: (j, 0, 0)),
        pl.BlockSpec((None, tm, UP_SHARD), lambda j, i, k: (j, i, 0)), (tm, UP_SHARD))


def _mm_down(u, w_down, res, tm=MM_ROWS):
    J, T, n = u.shape
    tm = _tile(T, tm)
    row = pl.BlockSpec((tm, D_MODEL), lambda i, d: (i, 0))
    return _matmul(
        "mm_down", "nn", u, w_down, jax.ShapeDtypeStruct((T, D_MODEL), F32), (T // tm, J),
        pl.BlockSpec((None, tm, n), lambda i, d: (d, i, 0)), pl.BlockSpec((None, n, D_MODEL), lambda i, d: (d, 0, 0)),
        row, (tm, D_MODEL), res, row)


def _mm_down_t(dx, w_down, tm=MM_ROWS):
    T = dx.shape[0]
    J, n, _ = w_down.shape
    tm = _tile(T, tm)
    return _matmul(
        "mm_down_t", "nt", dx, w_down, jax.ShapeDtypeStruct((J, T, n), F32), (J, T // tm, 1),
        pl.BlockSpec((tm, D_MODEL), lambda d, i, k: (i, 0)), pl.BlockSpec((None, n, D_MODEL), lambda d, i, k: (d, 0, 0)),
        pl.BlockSpec((None, tm, n), lambda d, i, k: (d, i, 0)), (tm, n))


def _mm_dw_down(u, dx, tk=MM_TOKENS):
    J, T, n = u.shape
    tk = _tile(T, tk)
    return _matmul(
        "mm_dw_down", "tn", u, dx, jax.ShapeDtypeStruct((J, n, D_MODEL), BF16), (J, T // tk),
        pl.BlockSpec((None, tk, n), lambda d, k: (d, k, 0)), pl.BlockSpec((tk, D_MODEL), lambda d, k: (k, 0)),
        pl.BlockSpec((None, n, D_MODEL), lambda d, k: (d, 0, 0)), (n, D_MODEL))


def _mm_dw_up(name, h2, d_up, part, tk=MM_TOKENS, comm=None):
    T, K = h2.shape
    p, of = part
    K //= of
    tk = _tile(T, tk)
    return _matmul(
        name, "tn", h2, d_up, jax.ShapeDtypeStruct((N_DEV, K, UP_SHARD), BF16), (N_DEV, T // tk),
        pl.BlockSpec((tk, K), lambda j, k: (k, p)), pl.BlockSpec((None, tk, UP_SHARD), lambda j, k: (j, k, 0)),
        pl.BlockSpec((None, K, UP_SHARD), lambda j, k: (j, 0, 0)), (K, UP_SHARD), comm=comm)


def _mm_up_t(d_up, w_up, tm=MM_ROWS, comm=None):
    J, T, n = d_up.shape
    K = w_up.shape[1]
    tm = _tile(T, tm)
    return _matmul(
        "mm_up_t", "nt", d_up, w_up, jax.ShapeDtypeStruct((T, K), F32), (T // tm, J),
        pl.BlockSpec((None, tm, n), lambda i, j: (j, i, 0)), pl.BlockSpec((None, K, n), lambda i, j: (j, 0, 0)),
        pl.BlockSpec((tm, K), lambda i, j: (i, 0)), (tm, K), comm=comm)


def _loss_head(x2, target, g_final, tm=512):
    T, Dm = x2.shape
    tm = _tile(T, tm)

    def body(x_ref, t_ref, g_ref, dx_ref, dg_ref, loss_ref):
        @pl.when(pl.program_id(0) == 0)
        def _():
            dg_ref[...] = jnp.zeros_like(dg_ref)
            loss_ref[...] = jnp.zeros_like(loss_ref)

        xv = x_ref[...]
        r = lax.rsqrt(jnp.mean(xv * xv, axis=-1, keepdims=True) + EPS)
        xhat = xv * r
        err = xhat * g_ref[...] - t_ref[...]
        loss_ref[...] += (0.5 / Dm) * jnp.sum(err * err)
        dy = err * (1.0 / Dm)
        dg_ref[...] += jnp.sum(dy * xhat, axis=0, keepdims=True)
        dxhat = dy * g_ref[...]
        dx_ref[...] = r * (dxhat - xhat * jnp.mean(dxhat * xhat, axis=-1, keepdims=True))

    row = pl.BlockSpec((tm, Dm), lambda i: (i, 0))
    vec = pl.BlockSpec((1, Dm), lambda i: (0, 0))
    return pl.pallas_call(
        body,
        out_shape=(jax.ShapeDtypeStruct((T, Dm), F32), jax.ShapeDtypeStruct((1, Dm), F32),
                   jax.ShapeDtypeStruct((1, Dm), F32)),
        grid=(T // tm,), in_specs=[row, row, vec], out_specs=(row, vec, vec), name="loss_head",
        compiler_params=_params(("arbitrary",)),
    )(x2, target, g_final)


def _cast_shards(shards):
    def body(*refs):
        n = len(refs) // 2
        for src, dst in zip(refs[:n], refs[n:]):
            dst[...] = src[...].astype(dst.dtype)

    return pl.pallas_call(
        body, out_shape=[jax.ShapeDtypeStruct(s.shape, BF16) for s in shards], name="cast_shards",
        compiler_params=pltpu.CompilerParams(vmem_limit_bytes=VMEM_LIMIT),
    )(*shards)


def _adamw(w, g, m, v):
    m = ADAM_B1 * m + (1.0 - ADAM_B1) * g
    v = ADAM_B2 * v + (1.0 - ADAM_B2) * (g * g)
    m_hat = m / (1.0 - ADAM_B1 ** ADAM_STEP)
    v_hat = v / (1.0 - ADAM_B2 ** ADAM_STEP)
    delta = -ADAM_LR * (m_hat / (jnp.sqrt(v_hat) + ADAM_EPS) + ADAM_WD * w)
    return delta, m, v


def _sum_parts(p_ref):
    g = p_ref[0].astype(F32)
    for d in range(1, N_DEV):
        g = g + p_ref[d].astype(F32)
    return g


def _reduce_adam(name, parts, w, m, v, tr=128):
    R, Cn = w.shape
    tr = math.gcd(tr, *[p.shape[1] for p in parts])
    starts, at = [], 0
    for p in parts:
        starts.append(at // tr)
        at += p.shape[1]
    counts = [p.shape[1] // tr for p in parts]

    def body(*refs):
        p_refs = refs[:len(parts)]
        w_ref, m_ref, v_ref, g_out, d_out, m_out, v_out = refs[len(parts):]
        i = pl.program_id(0)
        for p_ref, t0, n in zip(p_refs, starts, counts):
            @pl.when((i >= t0) & (i < t0 + n))
            def _():
                g = _sum_parts(p_ref)
                delta, m_new, v_new = _adamw(w_ref[...], g, m_ref[...], v_ref[...])
                g_out[...] = g
                d_out[...] = delta
                m_out[...] = m_new
                v_out[...] = v_new

    def part_spec(t0, n):
        return pl.BlockSpec((N_DEV, tr, Cn), lambda i: (0, jnp.clip(i - t0, 0, n - 1), 0))

    row = pl.BlockSpec((tr, Cn), lambda i: (i, 0))
    shape = jax.ShapeDtypeStruct((R, Cn), F32)
    return pl.pallas_call(
        body, out_shape=(shape,) * 4, grid=(R // tr,),
        in_specs=[part_spec(t0, n) for t0, n in zip(starts, counts)] + [row, row, row], out_specs=(row,) * 4,
        name=name, compiler_params=_params(("parallel",)),
    )(*parts, w, m, v)


def _small_adam(name, gathered, params):
    n_g, n_p = len(gathered), len(params)

    def body(*refs):
        g_refs = refs[:n_g]
        wmv = refs[n_g:n_g + 3 * n_p]
        sums = refs[n_g + 3 * n_p:2 * n_g + 3 * n_p]
        upd = refs[2 * n_g + 3 * n_p:]
        for j in range(n_g):
            g = _sum_parts(g_refs[j])
            sums[j][...] = g
            if j < n_p:
                w_ref, m_ref, v_ref = wmv[3 * j:3 * j + 3]
                delta, m_new, v_new = _adamw(w_ref[...], g, m_ref[...], v_ref[...])
                upd[3 * j][...] = delta
                upd[3 * j + 1][...] = m_new
                upd[3 * j + 2][...] = v_new

    flat = [a for wmv in params for a in wmv]
    out_shape = [jax.ShapeDtypeStruct(g.shape[1:], F32) for g in gathered]
    out_shape += [jax.ShapeDtypeStruct(a.shape, F32) for a in flat]
    res = pl.pallas_call(body, out_shape=out_shape, name=name)(*gathered, *flat)
    return res[:n_g], [tuple(res[n_g + 3 * j:n_g + 3 * j + 3]) for j in range(n_p)]


def _adam_only(name, g, w, m, v):
    def body(g_ref, w_ref, m_ref, v_ref, d_out, m_out, v_out):
        delta, m_new, v_new = _adamw(w_ref[...], g_ref[...], m_ref[...], v_ref[...])
        d_out[...] = delta
        m_out[...] = m_new
        v_out[...] = v_new

    shape = jax.ShapeDtypeStruct(w.shape, F32)
    return pl.pallas_call(body, out_shape=(shape,) * 3, name=name)(g, w, m, v)


def kernel(x, mem, g_mix, w_in, w_pool, pool_scale, w_a, g_ret, b_ret, w_r, g_mem, w_mem_kv, w_c, w_out, g_ffn, w_up, conv_w, conv_b, w_down, g_final, loss_target, m_g_mix, m_w_in, m_w_pool, m_pool_scale, m_w_a, m_g_ret, m_b_ret, m_w_r, m_g_mem, m_w_mem_kv, m_w_c, m_w_out, m_g_ffn, m_w_up, m_conv_w, m_conv_b, m_w_down, m_g_final, v_g_mix, v_w_in, v_w_pool, v_pool_scale, v_w_a, v_g_ret, v_b_ret, v_w_r, v_g_mem, v_w_mem_kv, v_w_c, v_w_out, v_g_ffn, v_w_up, v_conv_w, v_conv_b, v_w_down, v_g_final):
    B, S, _ = x.shape
    M = mem.shape[1]
    T = B * S
    me = _my_index()
    x2d = x.reshape(T, D_MODEL)
    mem2d = mem.reshape(B * M, D_MODEL)
    tgt2d = loss_target.reshape(T, D_MODEL)
    g_final2 = g_final.reshape(1, D_MODEL)

    big = dict(w_in=w_in[0], w_a=w_a[0], w_r=w_r[0], w_mem_kv=w_mem_kv[0], w_c=w_c[0], w_out=w_out[0],
               w_up=w_up[0], w_down=w_down[0])
    names = list(big)
    cast = dict(zip(names, _cast_shards([big[n] for n in names])))
    Win, cw_gathered = _comm_call("gather_w_in", _Gather([cast["w_in"], conv_w[0]]))
    cw_full = cw_gathered.transpose(1, 0, 2).reshape(3, FFN_HIDDEN)
    cw = cw_full.reshape(3, FFN_SLABS, UP_SHARD).transpose(1, 0, 2)
    cb = conv_b[0].reshape(FFN_SLABS, 1, UP_SHARD)
    wp = w_pool[0]
    tables = _ret_tables(S)

    h = _rms_fwd("rms_mix", x2d, g_mix)
    early = ("w_a", "w_r", "w_mem_kv", "w_c", "w_out", "w_down")
    proj, landed = _mm_cols_slab("mm_in", h, Win, comm=_Gather([cast[n] for n in early]))
    W = dict(zip(early, landed))
    Wa = W["w_a"].transpose(1, 0, 2).reshape(POOL_WIDTH, D_MODEL)
    Wc = W["w_c"].transpose(1, 0, 2).reshape(XA_WIDTH, D_MODEL)
    Wr = W["w_r"].reshape(D_MODEL, D_MODEL)
    Wkv = W["w_mem_kv"].reshape(D_MODEL, D_MODEL)
    Wout = W["w_out"].reshape(D_MODEL, D_MODEL)
    Wdown = W["w_down"].reshape(FFN_SLABS, UP_SHARD, D_MODEL)
    ypre = _pool_fwd(proj, wp, pool_scale, B, S)
    y_pool = _mm_rows("mm_a", ypre, Wa)
    yr, (Wup,) = _ret_fwd(proj, g_ret, b_ret, tables, B, S, comm=_Gather([cast["w_up"]]))
    y_ret = _mm_rows("mm_r", yr, Wr)
    mem_n = _rms_fwd("rms_mem", mem2d, g_mem)
    kv = _mm_rows("mm_kv", mem_n, Wkv)
    o_mem = _xa_fwd(proj, kv, B, S, M)
    y_mem = _mm_rows("mm_c", o_mem, Wc)
    ys = (y_pool, y_ret, y_mem)
    merged = _merge_fwd(proj, ys)
    x1 = _mm_rows("mm_out", merged, Wout, res=x2d)
    h2 = _rms_fwd("rms_ffn", x1, g_ffn)
    up = _mm_up(h2, Wup).reshape(2, FFN_SLABS, T, UP_SHARD)
    u = _glu_fwd(up, cw, cb, S)
    x2 = _mm_down(u, Wdown, x1)

    dx2, dg_final, loss_part = _loss_head(x2, tgt2d, g_final2)
    received = {}
    d_u = _mm_down_t(dx2, Wdown)
    dW_down = _mm_dw_down(u, dx2)
    (d_up, d_cw, d_cb), (received["w_down"],) = _glu_bwd(
        up, d_u, cw, cb, S, comm=_Exchange([dW_down.reshape(N_DEV, -1, D_MODEL)]))
    d_up = d_up.reshape(N_DEV, T, UP_SHARD)
    dW_up0 = _mm_dw_up("mm_dw_up0", h2, d_up, (0, 2))
    d_h2, (up0,) = _mm_up_t(d_up, Wup, comm=_Exchange([dW_up0]))
    dW_up1 = _mm_dw_up("mm_dw_up1", h2, d_up, (1, 2))
    dx1, dg_ffn = _rms_bwd("rms_ffn_bwd", x1, g_ffn, d_h2, dx2)
    d_merged = _mm_rows("mm_out_t", dx1, Wout, kind="nt")
    dW_out = _mm_tn("mm_dw_out", merged, dx1, BF16)
    (d_gl, d_y_pool, d_y_ret, d_y_mem), (received["w_out"],) = _merge_bwd(
        proj, ys, d_merged, comm=_Exchange([dW_out.reshape(N_DEV, -1, D_MODEL)]))
    dW_c = _mm_tn("mm_dw_c", o_mem, d_y_mem, BF16)
    d_o_mem = _mm_rows("mm_c_t", d_y_mem, Wc, kind="nt")
    (d_qx, d_kmem, d_vmem), (up1,) = _xa_bwd(proj, kv, d_o_mem, B, S, M, comm=_Exchange([dW_up1]))
    received["w_up"] = [up0, up1]
    d_kv = jnp.concatenate([d_kmem, d_vmem], axis=1)
    dW_kv = _mm_tn("mm_dw_kv", mem_n, d_kv, BF16)
    d_mem_n = _mm_rows("mm_kv_t", d_kv, Wkv, kind="nt")
    dg_mem = _rms_bwd("rms_mem_bwd", mem2d, g_mem, d_mem_n, None)
    dW_a = _mm_tn("mm_dw_a", ypre, d_y_pool, BF16)
    d_ypre = _mm_rows("mm_a_t", d_y_pool, Wa, kind="nt")
    d_hp, dw_pool, d_scale = _pool_bwd(proj, d_ypre, wp, pool_scale, B, S)
    dW_r = _mm_tn("mm_dw_r", yr, d_y_ret, BF16)
    d_yr = _mm_rows("mm_r_t", d_y_ret, Wr, kind="nt")
    (d_q, d_k, d_v, d_gr, dg_ret, db_ret), landed = _ret_bwd(
        proj, d_yr, g_ret, b_ret, tables, B, S,
        comm=_Exchange([dW_a.reshape(POOL_WIDTH, N_DEV, -1).transpose(1, 0, 2), dW_r.reshape(N_DEV, -1, D_MODEL),
                        dW_c.reshape(XA_WIDTH, N_DEV, -1).transpose(1, 0, 2), dW_kv.reshape(N_DEV, -1, D_MODEL)]))
    received["w_a"], received["w_r"], received["w_c"], received["w_mem_kv"] = landed
    small_names = ["w_pool", "pool_scale", "g_ret", "b_ret", "g_mem", "g_ffn", "conv_b", "g_final"]
    small_grads = [dw_pool, d_scale, dg_ret, db_ret, dg_mem, dg_ffn, d_cb.reshape(1, FFN_HIDDEN), dg_final,
                   d_cw.transpose(1, 0, 2).reshape(3, FFN_HIDDEN), loss_part]
    d_proj = jnp.concatenate([d_hp, d_q, d_k, d_v, d_gr, d_qx, d_gl], axis=1)
    dW_in0 = _mm_tn_slab("mm_dw_in0", h, d_proj, IN_SHARD, BF16, part=(0, 2))
    dW_in1, (in0, *small_all) = _mm_tn_slab("mm_dw_in1", h, d_proj, IN_SHARD, BF16, part=(1, 2),
                                            comm=_Exchange([dW_in0], whole=small_grads))
    d_h, (in1,) = _mm_cols_slab_t("mm_in_t", d_proj, Win, comm=_Exchange([dW_in1]))
    received["w_in"] = [in0, in1]
    grad_x, dg_mix = _rms_bwd("rms_mix_bwd", x2d, g_mix, d_h, dx1)
    (g_mix_all,) = _comm_call("gather_g_mix", _Exchange([], whole=[dg_mix]))

    args = dict(g_mix=g_mix, w_in=w_in, w_pool=w_pool, pool_scale=pool_scale, w_a=w_a, g_ret=g_ret, b_ret=b_ret,
                w_r=w_r, g_mem=g_mem, w_mem_kv=w_mem_kv, w_c=w_c, w_out=w_out, g_ffn=g_ffn, w_up=w_up,
                conv_w=conv_w, conv_b=conv_b, w_down=w_down, g_final=g_final)
    m_in = dict(g_mix=m_g_mix, w_in=m_w_in, w_pool=m_w_pool, pool_scale=m_pool_scale, w_a=m_w_a, g_ret=m_g_ret,
                b_ret=m_b_ret, w_r=m_w_r, g_mem=m_g_mem, w_mem_kv=m_w_mem_kv, w_c=m_w_c, w_out=m_w_out,
                g_ffn=m_g_ffn, w_up=m_w_up, conv_w=m_conv_w, conv_b=m_conv_b, w_down=m_w_down, g_final=m_g_final)
    v_in = dict(g_mix=v_g_mix, w_in=v_w_in, w_pool=v_w_pool, pool_scale=v_pool_scale, w_a=v_w_a, g_ret=v_g_ret,
                b_ret=v_b_ret, w_r=v_w_r, g_mem=v_g_mem, w_mem_kv=v_w_mem_kv, w_c=v_w_c, w_out=v_w_out,
                g_ffn=v_g_ffn, w_up=v_w_up, conv_w=v_conv_w, conv_b=v_conv_b, w_down=v_w_down, g_final=v_g_final)

    grads, deltas, new_m, new_v = {}, {}, {}, {}
    for n in names:
        shard = big[n].shape
        parts = received[n] if isinstance(received[n], list) else [received[n]]
        outs = _reduce_adam("adam_" + n, parts, big[n], m_in[n][0], v_in[n][0])
        for store, val in zip((grads, deltas, new_m, new_v), outs):
            store[n] = val.reshape((1,) + shard)

    def as_small(a):
        return a.reshape(a.shape[-3:]) if a.ndim > 2 else a.reshape(1, -1)

    def small_update(call_name, param_names, gathered):
        params = [tuple(as_small(d[n]) for d in (args, m_in, v_in)) for n in param_names]
        sums, updates = _small_adam(call_name, gathered, params)
        for n, g, (d_, m_, v_) in zip(param_names, sums, updates):
            shape = args[n].shape
            grads[n], deltas[n], new_m[n], new_v[n] = (a.reshape(shape) for a in (g, d_, m_, v_))
        return sums[len(param_names):]

    g_cw_full, loss_row = small_update("adam_small", small_names, small_all)
    loss = loss_row[0, 0]
    small_update("adam_g_mix", ["g_mix"], [g_mix_all])

    shard_cols = FFN_HIDDEN // N_DEV
    g_cw = lax.dynamic_slice_in_dim(g_cw_full, me * shard_cols, shard_cols, axis=1)
    d_, m_, v_ = _adam_only("adam_conv_w", g_cw, conv_w[0], m_conv_w[0], v_conv_w[0])
    grads["conv_w"], deltas["conv_w"], new_m["conv_w"], new_v["conv_w"] = g_cw[None], d_[None], m_[None], v_[None]

    order = ["g_mix", "w_in", "w_pool", "pool_scale", "w_a", "g_ret", "b_ret", "w_r", "g_mem", "w_mem_kv", "w_c",
             "w_out", "g_ffn", "w_up", "conv_w", "conv_b", "w_down", "g_final"]
    return (loss, grad_x.reshape(B, S, D_MODEL), *[grads[n] for n in order], *[deltas[n] for n in order],
            *[new_m[n] for n in order], *[new_v[n] for n in order])
```

```python
import functools
import math

import jax
import jax.numpy as jnp
from jax import lax
from jax.experimental import pallas as pl
from jax.experimental.pallas import tpu as pltpu

F32 = jnp.float32
BF16 = jnp.bfloat16

N_DEV = 8
D_MODEL = 1024
POOL_WINDOWS = (2, 4, 8, 16)
POOL_GROUP_DIM = 128
POOL_WIDTH = 512
POOL_HALO = 16
RET_HEADS = 4
RET_QK_DIM = 128
RET_V_DIM = 256
RET_CHUNK = 128
ROPE_BASE = 10000.0
XA_HEADS = 4
XA_HEAD_DIM = 128
XA_WIDTH = 512
IN_WIDTH = 7168
IN_SHARD = IN_WIDTH // N_DEV
FFN_HIDDEN = 2816
UP_SHARD = 2 * FFN_HIDDEN // N_DEV
FFN_SLABS = FFN_HIDDEN // UP_SHARD
EPS = 1e-6
ADAM_LR = 0.001
ADAM_B1 = 0.9
ADAM_B2 = 0.999
ADAM_EPS = 1e-08
ADAM_WD = 0.01
ADAM_STEP = 10
GELU_C = math.sqrt(2.0 / math.pi)
GELU_A = 0.044715
VMEM_LIMIT = 56 * 1024 * 1024
MM_ROWS = 1024
MM_TOKENS = 1024
MESH = pl.DeviceIdType.MESH

COL_Q, COL_K, COL_V, COL_GR, COL_QX, COL_GL = 512, 1024, 1536, 2560, 3584, 4096

_DIMS = {
    "nn": (((1,), (0,)), ((), ())),
    "nt": (((1,), (1,)), ((), ())),
    "tn": (((0,), (0,)), ((), ())),
}


def _dot(a, b, kind="nn"):
    return lax.dot_general(a.astype(BF16), b.astype(BF16), _DIMS[kind], preferred_element_type=F32)


def _params(sem, vmem=VMEM_LIMIT):
    return pltpu.CompilerParams(dimension_semantics=sem, vmem_limit_bytes=vmem)


def _tile(n, pref):
    t = min(n, pref)
    while n % t:
        t //= 2
    return t


def _mesh_pos():
    return lax.axis_index("x"), lax.axis_index("y"), lax.axis_index("c")


def _dev_index(x, y, c):
    return 4 * x + 2 * y + c


def _my_index():
    return _dev_index(*_mesh_pos())


def _remote(src, dst, send_sems, recv_sems, s, to):
    return pltpu.make_async_remote_copy(src_ref=src, dst_ref=dst, send_sem=send_sems.at[s], recv_sem=recv_sems.at[s],
                                        device_id=to, device_id_type=MESH)


class _Gather:
    def __init__(self, shards):
        self.inputs = list(shards)
        self.out_shapes = [jax.ShapeDtypeStruct((N_DEV,) + s.shape, s.dtype) for s in shards]
        n = len(shards)
        self.sem_shapes = [pltpu.SemaphoreType.DMA((7 * n,)), pltpu.SemaphoreType.DMA((7 * n,)),
                           pltpu.SemaphoreType.DMA((n,))]

    def _places(self):
        x, y, c = _mesh_pos()
        return (x, y, c), (x, y, 1 - c), [(1 - x, y), (x, 1 - y), (1 - x, 1 - y)]

    def _local(self, src, dst, sems):
        me = _my_index()
        return [pltpu.make_async_copy(src[w], dst[w].at[me], sems[2].at[w]) for w in range(len(src))]

    def start(self, src, dst, sems):
        me, sib, chips = self._places()
        for cp in self._local(src, dst, sems):
            cp.start()
        for w in range(len(src)):
            land = dst[w].at[_dev_index(*me)]
            _remote(src[w], land, sems[0], sems[1], 7 * w, sib).start()
            for j, chip in enumerate(chips):
                _remote(src[w], land, sems[0], sems[1], 7 * w + 1 + j, (*chip, me[2])).start()

    def finish(self, src, dst, sems):
        me, sib, chips = self._places()
        n = len(src)
        for j, chip in enumerate(chips):
            for w in range(n):
                block = dst[w].at[_dev_index(*chip, me[2])]
                _remote(src[w], block, sems[0], sems[1], 7 * w + 1 + j, me).wait_recv()
                _remote(block, block, sems[0], sems[1], 7 * w + 4 + j, sib).start()
        for w in range(n):
            _remote(src[w], dst[w].at[_dev_index(*sib)], sems[0], sems[1], 7 * w, me).wait_recv()
            for j, chip in enumerate(chips):
                block = dst[w].at[_dev_index(*chip, sib[2])]
                _remote(block, block, sems[0], sems[1], 7 * w + 4 + j, me).wait_recv()
            for k in range(7):
                _remote(src[w], dst[w].at[0], sems[0], sems[1], 7 * w + k, me).wait_send()
        for cp in self._local(src, dst, sems):
            cp.wait()


class _Exchange:
    def __init__(self, partials, whole=()):
        self.n_part = len(partials)
        self.inputs = list(partials) + list(whole)
        self.out_shapes = [jax.ShapeDtypeStruct(p.shape, p.dtype) for p in partials]
        self.out_shapes += [jax.ShapeDtypeStruct((N_DEV,) + a.shape, a.dtype) for a in whole]
        n = len(self.inputs)
        self.sem_shapes = [pltpu.SemaphoreType.DMA((7 * n,)), pltpu.SemaphoreType.DMA((7 * n,)),
                           pltpu.SemaphoreType.DMA((n,))]

    def _peer(self, k):
        x, y, c = _mesh_pos()
        p = (x ^ ((k >> 2) & 1), y ^ ((k >> 1) & 1), c ^ (k & 1))
        return p, _dev_index(*p)

    def _source(self, src, w, slot):
        return src[w].at[slot] if w < self.n_part else src[w]

    def _local(self, src, dst, sems):
        me = _my_index()
        return [pltpu.make_async_copy(self._source(src, w, me), dst[w].at[me], sems[2].at[w])
                for w in range(len(src))]

    def start(self, src, dst, sems):
        me = _my_index()
        for cp in self._local(src, dst, sems):
            cp.start()
        for k in range(1, N_DEV):
            peer, peer_idx = self._peer(k)
            for w in range(len(src)):
                _remote(self._source(src, w, peer_idx), dst[w].at[me], sems[0], sems[1], 7 * w + k - 1, peer).start()

    def finish(self, src, dst, sems):
        for k in range(1, N_DEV):
            peer, peer_idx = self._peer(k)
            for w in range(len(src)):
                cp = _remote(self._source(src, w, peer_idx), dst[w].at[peer_idx], sems[0], sems[1], 7 * w + k - 1, peer)
                cp.wait_send()
                cp.wait_recv()
        for cp in self._local(src, dst, sems):
            cp.wait()


class _ExchangeTo:
    def __init__(self, partials, side, offset):
        self.side, self.offset = side, offset
        self.inputs = list(partials)
        self.out_shapes = [jax.ShapeDtypeStruct((N_DEV,) + p.shape[1:], p.dtype) for p in partials]
        n = len(partials)
        self.sem_shapes = [pltpu.SemaphoreType.DMA((7 * n,)), pltpu.SemaphoreType.DMA((7 * n,)),
                           pltpu.SemaphoreType.DMA((n,))]

    def _copies(self, src, dst, sems):
        x, y, c = _mesh_pos()
        me = _dev_index(x, y, c)
        receives = x == self.side
        remote = []
        for k in range(1, N_DEV):
            kx, ky, kc = (k >> 2) & 1, (k >> 1) & 1, k & 1
            peer = (x ^ kx, y ^ ky, c ^ kc)
            sends = x == (self.side ^ kx)
            for w in range(len(src)):
                slab = src[w].at[self.offset + 2 * peer[1] + peer[2]]
                s = 7 * w + k - 1
                remote.append((sends, _remote(slab, dst[w].at[me], sems[0], sems[1], s, peer),
                               _remote(slab, dst[w].at[_dev_index(*peer)], sems[0], sems[1], s, peer)))
        local = [pltpu.make_async_copy(src[w].at[self.offset + 2 * y + c], dst[w].at[me], sems[2].at[w])
                 for w in range(len(src))]
        return receives, remote, local

    def start(self, src, dst, sems):
        receives, remote, local = self._copies(src, dst, sems)

        @pl.when(receives)
        def _():
            for cp in local:
                cp.start()

        for sends, send, _ in remote:
            pl.when(sends)(send.start)

    def finish(self, src, dst, sems):
        receives, remote, local = self._copies(src, dst, sems)
        for sends, send, arrive in remote:
            pl.when(sends)(send.wait_send)
            pl.when(receives)(arrive.wait_recv)

        @pl.when(receives)
        def _():
            for cp in local:
                cp.wait()


class _Both:
    def __init__(self, first, second):
        self.plans = (first, second)
        self.inputs = first.inputs + second.inputs
        self.out_shapes = first.out_shapes + second.out_shapes
        self.sem_shapes = first.sem_shapes + second.sem_shapes

    def _split(self, src, dst, sems):
        a = self.plans[0]
        ni, no = len(a.inputs), len(a.out_shapes)
        return (src[:ni], dst[:no], sems[:3]), (src[ni:], dst[no:], sems[3:])

    def start(self, src, dst, sems):
        for plan, part in zip(self.plans, self._split(src, dst, sems)):
            plan.start(*part)

    def finish(self, src, dst, sems):
        for plan, part in zip(self.plans, self._split(src, dst, sems)):
            plan.finish(*part)


def _pcall(body, args, *, name, out_shape, grid, in_specs, out_specs, scratch_shapes=(), sem=None, comm=None):
    single = not isinstance(out_shape, (tuple, list))
    outs = [out_shape] if single else list(out_shape)
    ospecs = [out_specs] if single else list(out_specs)
    n_in, n_out, n_scr = len(args), len(outs), len(scratch_shapes)

    def pick(res):
        return res[0] if single else tuple(res[:n_out])

    if comm is None:
        res = pl.pallas_call(
            body, out_shape=outs, grid=grid, in_specs=list(in_specs), out_specs=ospecs,
            scratch_shapes=list(scratch_shapes), name=name, compiler_params=_params(sem),
        )(*args)
        return pick(res), ()

    nci, nco = len(comm.inputs), len(comm.out_shapes)

    def carrier(*refs):
        at = 0
        parts = []
        for size in (n_in, nci, n_out, nco, n_scr, len(comm.sem_shapes)):
            parts.append(refs[at:at + size])
            at += size
        ins, cins, o, couts, scr, sems = parts
        ids = [pl.program_id(a) for a in range(len(grid))]
        first = functools.reduce(jnp.logical_and, [i == 0 for i in ids])
        last = functools.reduce(jnp.logical_and, [i == g - 1 for i, g in zip(ids, grid)])

        @pl.when(first)
        def _():
            comm.start(cins, couts, sems)

        body(*ins, *o, *scr)

        @pl.when(last)
        def _():
            comm.finish(cins, couts, sems)

    hbm = pl.BlockSpec(memory_space=pltpu.HBM)
    res = pl.pallas_call(
        carrier, out_shape=outs + comm.out_shapes, grid=grid, in_specs=list(in_specs) + [hbm] * nci,
        out_specs=ospecs + [hbm] * nco, scratch_shapes=list(scratch_shapes) + comm.sem_shapes, name=name,
        compiler_params=_params(("arbitrary",) * len(grid)),
    )(*args, *comm.inputs)
    return pick(res), tuple(res[n_out:])


def _comm_call(name, comm):
    def body(*refs):
        nci, nco = len(comm.inputs), len(comm.out_shapes)
        cins, couts, sems = refs[:nci], refs[nci:nci + nco], refs[nci + nco:]
        comm.start(cins, couts, sems)
        comm.finish(cins, couts, sems)

    hbm = pl.BlockSpec(memory_space=pltpu.HBM)
    return pl.pallas_call(
        body, out_shape=comm.out_shapes, in_specs=[hbm] * len(comm.inputs), out_specs=[hbm] * len(comm.out_shapes),
        scratch_shapes=comm.sem_shapes, name=name,
    )(*comm.inputs)


def _matmul(name, kind, a, b, out_shape, grid, a_spec, b_spec, o_spec, acc_shape, res=None, res_spec=None,
            comm=None):
    nk = grid[-1]
    has_res = res is not None

    def body(*refs):
        a_ref, b_ref = refs[0], refs[1]
        o_ref = refs[2 + has_res]

        def prod():
            return _dot(a_ref[...], b_ref[...], kind)

        def finish(acc):
            if has_res:
                acc = acc + refs[2][...]
            o_ref[...] = acc.astype(o_ref.dtype)

        if nk == 1:
            finish(prod())
        else:
            acc_ref = refs[3 + has_res]
            k = pl.program_id(len(grid) - 1)

            @pl.when(k == 0)
            def _():
                acc_ref[...] = prod()

            @pl.when(k > 0)
            def _():
                acc_ref[...] += prod()

            @pl.when(k == nk - 1)
            def _():
                finish(acc_ref[...])

    in_specs = [a_spec, b_spec] + ([res_spec] if has_res else [])
    args = (a, b) + ((res,) if has_res else ())
    scratch = [pltpu.VMEM(acc_shape, F32)] if nk > 1 else []
    sem = ("parallel",) * (len(grid) - 1) + ("arbitrary",)
    out, landed = _pcall(body, args, name=name, out_shape=out_shape, grid=grid, in_specs=in_specs,
                         out_specs=o_spec, scratch_shapes=scratch, sem=sem, comm=comm)
    return out if comm is None else (out, landed)


def _mm_rows(name, a, w, out_dtype=F32, res=None, kind="nn", tm=MM_ROWS, comm=None):
    M, K = a.shape
    N = w.shape[1] if kind == "nn" else w.shape[0]
    tm = _tile(M, tm)
    res_spec = pl.BlockSpec((tm, N), lambda i, k: (i, 0)) if res is not None else None
    return _matmul(
        name, kind, a, w, jax.ShapeDtypeStruct((M, N), out_dtype), (M // tm, 1),
        pl.BlockSpec((tm, K), lambda i, k: (i, 0)), pl.BlockSpec(w.shape, lambda i, k: (0, 0)),
        pl.BlockSpec((tm, N), lambda i, k: (i, 0)), (tm, N), res, res_spec, comm)


def _mm_tn(name, a, b, out_dtype=F32, tk=MM_TOKENS, comm=None):
    T, M = a.shape
    N = b.shape[1]
    tk = _tile(T, tk)
    return _matmul(
        name, "tn", a, b, jax.ShapeDtypeStruct((M, N), out_dtype), (1, T // tk),
        pl.BlockSpec((tk, M), lambda i, k: (k, 0)), pl.BlockSpec((tk, N), lambda i, k: (k, 0)),
        pl.BlockSpec((M, N), lambda i, k: (0, 0)), (M, N), comm=comm)


def _mm_cols_slab(name, a, w_slabs, out_dtype=F32, tm=MM_ROWS, comm=None):
    M, K = a.shape
    J, _, n = w_slabs.shape
    tm = _tile(M, tm)
    return _matmul(
        name, "nn", a, w_slabs, jax.ShapeDtypeStruct((M, J * n), out_dtype), (J, M // tm, 1),
        pl.BlockSpec((tm, K), lambda j, i, k: (i, 0)), pl.BlockSpec((None, K, n), lambda j, i, k: (j, 0, 0)),
        pl.BlockSpec((tm, n), lambda j, i, k: (i, j)), (tm, n), comm=comm)


def _mm_cols_slab_t(name, a, w_slabs, out_dtype=F32, tm=MM_ROWS, comm=None):
    M = a.shape[0]
    J, K, n = w_slabs.shape
    tm = _tile(M, tm)
    return _matmul(
        name, "nt", a, w_slabs, jax.ShapeDtypeStruct((M, K), out_dtype), (M // tm, J),
        pl.BlockSpec((tm, n), lambda i, j: (i, j)), pl.BlockSpec((None, K, n), lambda i, j: (j, 0, 0)),
        pl.BlockSpec((tm, K), lambda i, j: (i, 0)), (tm, K), comm=comm)


def _mm_in_t(d_lo, d_hi, w_slabs, tm=MM_ROWS, comm=None):
    M = d_lo.shape[0]
    J, K, n = w_slabs.shape
    half = J // 2
    tm = _tile(M, tm)

    def body(lo_ref, hi_ref, w_ref, o_ref, acc_ref):
        j = pl.program_id(1)

        @pl.when(j == 0)
        def _():
            acc_ref[...] = _dot(lo_ref[...], w_ref[...], "nt")

        @pl.when((j > 0) & (j < half))
        def _():
            acc_ref[...] += _dot(lo_ref[...], w_ref[...], "nt")

        @pl.when(j >= half)
        def _():
            acc_ref[...] += _dot(hi_ref[...], w_ref[...], "nt")

        @pl.when(j == J - 1)
        def _():
            o_ref[...] = acc_ref[...]

    out, landed = _pcall(
        body, (d_lo, d_hi, w_slabs), name="mm_in_t", out_shape=jax.ShapeDtypeStruct((M, K), F32), grid=(M // tm, J),
        in_specs=[pl.BlockSpec((tm, n), lambda i, j: (i, jnp.minimum(j, half - 1))),
                  pl.BlockSpec((tm, n), lambda i, j: (i, jnp.maximum(j - half, 0))),
                  pl.BlockSpec((None, K, n), lambda i, j: (j, 0, 0))],
        out_specs=pl.BlockSpec((tm, K), lambda i, j: (i, 0)), scratch_shapes=[pltpu.VMEM((tm, K), F32)],
        sem=("parallel", "arbitrary"), comm=comm)
    return out if comm is None else (out, landed)


def _mm_tn_slab(name, a, b, n, out_dtype=F32, tk=MM_TOKENS, comm=None, part=(0, 1)):
    T, M = a.shape
    p, of = part
    M //= of
    J = b.shape[1] // n
    tk = _tile(T, tk)
    return _matmul(
        name, "tn", a, b, jax.ShapeDtypeStruct((J, M, n), out_dtype), (J, T // tk),
        pl.BlockSpec((tk, M), lambda j, k: (k, p)), pl.BlockSpec((tk, n), lambda j, k: (k, j)),
        pl.BlockSpec((None, M, n), lambda j, k: (j, 0, 0)), (M, n), comm=comm)


def _rms_fwd(name, x, g, tm=512):
    T, Dm = x.shape
    tm = _tile(T, tm)

    def body(x_ref, g_ref, h_ref):
        xv = x_ref[...]
        r = lax.rsqrt(jnp.mean(xv * xv, axis=-1, keepdims=True) + EPS)
        h_ref[...] = (xv * r * g_ref[...]).astype(h_ref.dtype)

    return pl.pallas_call(
        body, out_shape=jax.ShapeDtypeStruct((T, Dm), BF16), grid=(T // tm,),
        in_specs=[pl.BlockSpec((tm, Dm), lambda i: (i, 0)), pl.BlockSpec((1, Dm), lambda i: (0, 0))],
        out_specs=pl.BlockSpec((tm, Dm), lambda i: (i, 0)), name=name, compiler_params=_params(("parallel",)),
    )(x, g)


def _rms_bwd(name, x, g, dh, dres, tm=512):
    T, Dm = x.shape
    tm = _tile(T, tm)
    want_dx = dres is not None

    def body(*refs):
        if want_dx:
            x_ref, g_ref, dh_ref, dres_ref, dx_ref, dg_ref = refs
        else:
            x_ref, g_ref, dh_ref, dg_ref = refs
        xv = x_ref[...]
        r = lax.rsqrt(jnp.mean(xv * xv, axis=-1, keepdims=True) + EPS)
        xhat = xv * r
        dhv = dh_ref[...]

        @pl.when(pl.program_id(0) == 0)
        def _():
            dg_ref[...] = jnp.zeros_like(dg_ref)

        dg_ref[...] += jnp.sum(dhv * xhat, axis=0, keepdims=True)
        if want_dx:
            dxhat = dhv * g_ref[...]
            dx_ref[...] = dres_ref[...] + r * (dxhat - xhat * jnp.mean(dxhat * xhat, axis=-1, keepdims=True))

    row = pl.BlockSpec((tm, Dm), lambda i: (i, 0))
    vec = pl.BlockSpec((1, Dm), lambda i: (0, 0))
    if want_dx:
        return pl.pallas_call(
            body, out_shape=(jax.ShapeDtypeStruct((T, Dm), F32), jax.ShapeDtypeStruct((1, Dm), F32)),
            grid=(T // tm,), in_specs=[row, vec, row, row], out_specs=(row, vec), name=name,
            compiler_params=_params(("arbitrary",)),
        )(x, g, dh, dres)
    return pl.pallas_call(
        body, out_shape=jax.ShapeDtypeStruct((1, Dm), F32), grid=(T // tm,), in_specs=[row, vec, row],
        out_specs=vec, name=name, compiler_params=_params(("arbitrary",)),
    )(x, g, dh)


def _pool_rows(S):
    return _tile(S, 256)


def _pool_count(c0, rows, w):
    t = c0 + lax.broadcasted_iota(jnp.int32, (rows, 1), 0)
    return jnp.minimum(t + 1, w).astype(F32)


def _pool_fwd(proj, w_pool, scale, B, S):
    CH = _pool_rows(S)

    def body(hp_ref, wp_ref, sc_ref, o_ref, pad_ref):
        pad_ref[0:POOL_HALO, :] = jnp.zeros((POOL_HALO, POOL_WIDTH), F32)
        pad_ref[POOL_HALO:, :] = hp_ref[...]
        for gi, w in enumerate(POOL_WINDOWS):
            cols = slice(gi * POOL_GROUP_DIM, (gi + 1) * POOL_GROUP_DIM)
            for c in range(S // CH):
                base = POOL_HALO + c * CH
                acc = pad_ref[base:base + CH, cols]
                tok = acc
                for j in range(1, w):
                    acc = acc + pad_ref[base - j:base - j + CH, cols]
                pooled = acc / _pool_count(c * CH, CH, w) - tok
                z = _dot(pooled, wp_ref[gi])
                o_ref[c * CH:(c + 1) * CH, cols] = (z * sc_ref[:, cols]).astype(o_ref.dtype)

    return pl.pallas_call(
        body, out_shape=jax.ShapeDtypeStruct((B * S, POOL_WIDTH), BF16), grid=(B,),
        in_specs=[pl.BlockSpec((S, POOL_WIDTH), lambda b: (b, 0)),
                  pl.BlockSpec(w_pool.shape, lambda b: (0, 0, 0)),
                  pl.BlockSpec((1, POOL_WIDTH), lambda b: (0, 0))],
        out_specs=pl.BlockSpec((S, POOL_WIDTH), lambda b: (b, 0)),
        scratch_shapes=[pltpu.VMEM((S + POOL_HALO, POOL_WIDTH), F32)],
        name="pool_fwd", compiler_params=_params(("parallel",)),
    )(proj, w_pool, scale)


def _pool_bwd(proj, d_ypre, w_pool, scale, B, S):
    CH = _pool_rows(S)

    def body(hp_ref, dy_ref, wp_ref, sc_ref, dhp_ref, dwp_ref, dsc_ref, pad_ref, sc_pad_ref, dp_ref):
        @pl.when(pl.program_id(0) == 0)
        def _():
            dwp_ref[...] = jnp.zeros_like(dwp_ref)
            dsc_ref[...] = jnp.zeros_like(dsc_ref)

        pad_ref[0:POOL_HALO, :] = jnp.zeros((POOL_HALO, POOL_WIDTH), F32)
        pad_ref[POOL_HALO:, :] = hp_ref[...]
        sc_pad_ref[S:, :] = jnp.zeros((POOL_HALO, POOL_WIDTH), F32)
        for gi, w in enumerate(POOL_WINDOWS):
            cols = slice(gi * POOL_GROUP_DIM, (gi + 1) * POOL_GROUP_DIM)
            for c in range(S // CH):
                base = POOL_HALO + c * CH
                rows = slice(c * CH, (c + 1) * CH)
                acc = pad_ref[base:base + CH, cols]
                tok = acc
                for j in range(1, w):
                    acc = acc + pad_ref[base - j:base - j + CH, cols]
                cnt = _pool_count(c * CH, CH, w)
                pooled = acc / cnt - tok
                z = _dot(pooled, wp_ref[gi])
                dy = dy_ref[rows, cols]
                dsc_ref[:, cols] += jnp.sum(dy * z, axis=0, keepdims=True)
                dz = dy * sc_ref[:, cols]
                dwp_ref[gi] += _dot(pooled, dz, "tn")
                dpool = _dot(dz, wp_ref[gi], "nt")
                dp_ref[rows, cols] = dpool
                sc_pad_ref[rows, cols] = dpool / cnt
            for c in range(S // CH):
                rows = slice(c * CH, (c + 1) * CH)
                acc = sc_pad_ref[rows, cols]
                for j in range(1, w):
                    acc = acc + sc_pad_ref[c * CH + j:c * CH + j + CH, cols]
                dhp_ref[rows, cols] = (acc - dp_ref[rows, cols]).astype(dhp_ref.dtype)

    seq = pl.BlockSpec((S, POOL_WIDTH), lambda b: (b, 0))
    return pl.pallas_call(
        body,
        out_shape=(jax.ShapeDtypeStruct((B * S, POOL_WIDTH), BF16),
                   jax.ShapeDtypeStruct(w_pool.shape, F32), jax.ShapeDtypeStruct((1, POOL_WIDTH), F32)),
        grid=(B,),
        in_specs=[seq, seq, pl.BlockSpec(w_pool.shape, lambda b: (0, 0, 0)),
                  pl.BlockSpec((1, POOL_WIDTH), lambda b: (0, 0))],
        out_specs=(seq, pl.BlockSpec(w_pool.shape, lambda b: (0, 0, 0)),
                   pl.BlockSpec((1, POOL_WIDTH), lambda b: (0, 0))),
        scratch_shapes=[pltpu.VMEM((S + POOL_HALO, POOL_WIDTH), F32),
                        pltpu.VMEM((S + POOL_HALO, POOL_WIDTH), F32),
                        pltpu.VMEM((S, POOL_WIDTH), F32)],
        name="pool_bwd", compiler_params=_params(("arbitrary",)),
    )(proj, d_ypre, w_pool, scale)


def _ret_tables(S):
    half = RET_QK_DIM // 2
    inv = ROPE_BASE ** (-jnp.arange(half, dtype=F32) / half)
    ang = jnp.arange(S, dtype=F32)[:, None] * inv[None, :]
    cos, sin = jnp.cos(ang), jnp.sin(ang)
    cos_full = jnp.concatenate([cos, cos], axis=-1)
    sin_signed = jnp.concatenate([-sin, sin], axis=-1)
    C = RET_CHUNK
    lg = jnp.log1p(-jnp.exp2(-5.0 - jnp.arange(RET_HEADS, dtype=F32)))[:, None, None]
    idx = jnp.arange(C, dtype=F32)
    rel = idx[:, None] - idx[None, :]
    decay = jnp.where(rel >= 0, jnp.exp(jnp.maximum(rel, 0.0) * lg), 0.0)
    q_decay = jnp.broadcast_to(jnp.exp((idx + 1.0)[None, :, None] * lg), (RET_HEADS, C, RET_QK_DIM))
    k_decay = jnp.broadcast_to(jnp.exp((C - 1.0 - idx)[None, :, None] * lg), (RET_HEADS, C, RET_QK_DIM))
    c_decay = jnp.broadcast_to(jnp.exp(C * lg), (RET_HEADS, 1, RET_V_DIM))
    return cos_full, sin_signed, decay, q_decay, k_decay, c_decay


def _rope(x, cos_full, sin_signed):
    return x * cos_full + pltpu.roll(x, RET_QK_DIM // 2, axis=1) * sin_signed


def _rope_t(dy, cos_full, sin_signed):
    return dy * cos_full + pltpu.roll(dy * sin_signed, RET_QK_DIM // 2, axis=1)


def _ret_specs(S, bh):
    def at(col_of_head, width):
        def index(*ids):
            b, h = bh(*ids)
            return (b, col_of_head + h)
        return pl.BlockSpec((S, width), index)

    def per_head(shape):
        def index(*ids):
            _, h = bh(*ids)
            return (h,) + (0,) * len(shape)
        return pl.BlockSpec((None,) + shape, index)

    def head_vec(width):
        def index(*ids):
            _, h = bh(*ids)
            return (0, h)
        return pl.BlockSpec((1, width), index)

    table = pl.BlockSpec((S, RET_QK_DIM), lambda *ids: (0, 0))
    C = RET_CHUNK
    return dict(
        q=at(COL_Q // RET_QK_DIM, RET_QK_DIM), k=at(COL_K // RET_QK_DIM, RET_QK_DIM),
        v=at(COL_V // RET_V_DIM, RET_V_DIM), gr=at(COL_GR // RET_V_DIM, RET_V_DIM),
        table=table, decay=per_head((C, C)), qd=per_head((C, RET_QK_DIM)), kd=per_head((C, RET_QK_DIM)),
        cd=per_head((1, RET_V_DIM)), vec=head_vec(RET_V_DIM), out_qk=at(0, RET_QK_DIM), out_v=at(0, RET_V_DIM))


def _group_norm(o):
    mu = jnp.mean(o, axis=-1, keepdims=True)
    oc = o - mu
    rstd = lax.rsqrt(jnp.mean(oc * oc, axis=-1, keepdims=True) + EPS)
    return oc * rstd, rstd


def _ret_fwd(proj, g_ret, b_ret, tables, B, S, comm=None):
    C = RET_CHUNK
    cos_t, sin_t, decay, q_decay, k_decay, c_decay = tables
    sp = _ret_specs(S, lambda b, h: (b, h))

    def body(q_ref, k_ref, v_ref, gr_ref, cos_ref, sin_ref, dec_ref, qd_ref, kd_ref, cd_ref, g_ref, b_ref,
             y_ref, r_ref):
        r_ref[...] = jnp.zeros_like(r_ref)

        def chunk(i, carry):
            rows = pl.ds(pl.multiple_of(i * C, C), C)
            cs, sn = cos_ref[rows, :], sin_ref[rows, :]
            q = _rope(q_ref[rows, :], cs, sn)
            k = _rope(k_ref[rows, :], cs, sn) * (RET_QK_DIM ** -0.5)
            v = v_ref[rows, :]
            R = r_ref[...]
            s = _dot(q, k, "nt") * dec_ref[...]
            o = _dot(s, v) + _dot(q * qd_ref[...], R)
            r_ref[...] = cd_ref[...] * R + _dot(k * kd_ref[...], v, "tn")
            on, _ = _group_norm(o)
            gr = gr_ref[rows, :]
            y_ref[rows, :] = (gr * jax.nn.sigmoid(gr) * (on * g_ref[...] + b_ref[...])).astype(y_ref.dtype)
            return carry

        lax.fori_loop(0, S // C, chunk, 0)

    return _pcall(
        body, (proj, proj, proj, proj, cos_t, sin_t, decay, q_decay, k_decay, c_decay, g_ret, b_ret),
        name="ret_fwd", out_shape=jax.ShapeDtypeStruct((B * S, RET_HEADS * RET_V_DIM), BF16), grid=(B, RET_HEADS),
        in_specs=[sp["q"], sp["k"], sp["v"], sp["gr"], sp["table"], sp["table"], sp["decay"], sp["qd"],
                  sp["kd"], sp["cd"], sp["vec"], sp["vec"]],
        out_specs=sp["out_v"], scratch_shapes=[pltpu.VMEM((RET_QK_DIM, RET_V_DIM), F32)],
        sem=("parallel", "parallel"), comm=comm)


def _ret_bwd(proj, d_yr, g_ret, b_ret, tables, B, S, comm=None):
    C = RET_CHUNK
    N = S // C
    cos_t, sin_t, decay, q_decay, k_decay, c_decay = tables
    sp = _ret_specs(S, lambda h, b: (b, h))
    qk_scale = RET_QK_DIM ** -0.5

    def body(q_ref, k_ref, v_ref, gr_ref, dy_ref, cos_ref, sin_ref, dec_ref, qd_ref, kd_ref, cd_ref, g_ref, b_ref,
             dq_ref, dk_ref, dv_ref, dgr_ref, dg_ref, db_ref, qr_ref, kr_ref, rs_ref, dr_ref):
        @pl.when(pl.program_id(1) == 0)
        def _():
            dg_ref[...] = jnp.zeros_like(dg_ref)
            db_ref[...] = jnp.zeros_like(db_ref)

        dr_ref[...] = jnp.zeros_like(dr_ref)

        def sweep(i, R):
            rows = pl.ds(pl.multiple_of(i * C, C), C)
            cs, sn = cos_ref[rows, :], sin_ref[rows, :]
            q = _rope(q_ref[rows, :], cs, sn)
            k = _rope(k_ref[rows, :], cs, sn) * qk_scale
            qr_ref[rows, :] = q
            kr_ref[rows, :] = k
            rs_ref[i] = R
            return cd_ref[...] * R + _dot(k * kd_ref[...], v_ref[rows, :], "tn")

        lax.fori_loop(0, N, sweep, jnp.zeros((RET_QK_DIM, RET_V_DIM), F32))

        def back(step, carry):
            i = N - 1 - step
            rows = pl.ds(pl.multiple_of(i * C, C), C)
            q, k, v = qr_ref[rows, :], kr_ref[rows, :], v_ref[rows, :]
            R, dR = rs_ref[i], dr_ref[...]
            dec, qd, kd = dec_ref[...], qd_ref[...], kd_ref[...]
            s = _dot(q, k, "nt") * dec
            o = _dot(s, v) + _dot(q * qd, R)
            on, rstd = _group_norm(o)
            oaff = on * g_ref[...] + b_ref[...]
            gr = gr_ref[rows, :]
            sg = jax.nn.sigmoid(gr)
            dy = dy_ref[rows, :]
            dgr_ref[rows, :] = (dy * oaff * (sg * (1.0 + gr * (1.0 - sg)))).astype(dgr_ref.dtype)
            doaff = dy * (gr * sg)
            dg_ref[...] += jnp.sum(doaff * on, axis=0, keepdims=True)
            db_ref[...] += jnp.sum(doaff, axis=0, keepdims=True)
            don = doaff * g_ref[...]
            do = rstd * (don - jnp.mean(don, axis=-1, keepdims=True)
                         - on * jnp.mean(don * on, axis=-1, keepdims=True))
            ds = _dot(do, v, "nt") * dec
            dq = _dot(ds, k) + qd * _dot(do, R, "nt")
            dk = _dot(ds, q, "tn") + kd * _dot(v, dR, "nt")
            dv_ref[rows, :] = (_dot(s, do, "tn") + _dot(k * kd, dR)).astype(dv_ref.dtype)
            dr_ref[...] = cd_ref[...] * dR + _dot(q * qd, do, "tn")
            cs, sn = cos_ref[rows, :], sin_ref[rows, :]
            dq_ref[rows, :] = _rope_t(dq, cs, sn).astype(dq_ref.dtype)
            dk_ref[rows, :] = _rope_t(dk * qk_scale, cs, sn).astype(dk_ref.dtype)
            return carry

        lax.fori_loop(0, N, back, 0)

    T = B * S
    qk_shape = jax.ShapeDtypeStruct((T, RET_HEADS * RET_QK_DIM), BF16)
    v_shape = jax.ShapeDtypeStruct((T, RET_HEADS * RET_V_DIM), BF16)
    vec_shape = jax.ShapeDtypeStruct((1, RET_HEADS * RET_V_DIM), F32)
    return _pcall(
        body, (proj, proj, proj, proj, d_yr, cos_t, sin_t, decay, q_decay, k_decay, c_decay, g_ret, b_ret),
        name="ret_bwd", out_shape=(qk_shape, qk_shape, v_shape, v_shape, vec_shape, vec_shape), grid=(RET_HEADS, B),
        in_specs=[sp["q"], sp["k"], sp["v"], sp["gr"], sp["out_v"], sp["table"], sp["table"], sp["decay"],
                  sp["qd"], sp["kd"], sp["cd"], sp["vec"], sp["vec"]],
        out_specs=(sp["out_qk"], sp["out_qk"], sp["out_v"], sp["out_v"], sp["vec"], sp["vec"]),
        scratch_shapes=[pltpu.VMEM((S, RET_QK_DIM), F32), pltpu.VMEM((S, RET_QK_DIM), F32),
                        pltpu.VMEM((N, RET_QK_DIM, RET_V_DIM), F32), pltpu.VMEM((RET_QK_DIM, RET_V_DIM), F32)],
        sem=("parallel", "arbitrary"), comm=comm)


def _xa_rows(S):
    return _tile(S, 256)


def _xa_specs(S, M):
    q = pl.BlockSpec((S, XA_HEAD_DIM), lambda b, h: (b, COL_QX // XA_HEAD_DIM + h))
    k = pl.BlockSpec((M, XA_HEAD_DIM), lambda b, h: (b, h))
    v = pl.BlockSpec((M, XA_HEAD_DIM), lambda b, h: (b, XA_HEADS + h))
    o = pl.BlockSpec((S, XA_HEAD_DIM), lambda b, h: (b, h))
    return q, k, v, o


def _softmax_rows(s):
    e = jnp.exp(s - jnp.max(s, axis=-1, keepdims=True))
    return e / jnp.sum(e, axis=-1, keepdims=True)


def _xa_fwd(proj, kv, B, S, M):
    CH = _xa_rows(S)
    q_spec, k_spec, v_spec, o_spec = _xa_specs(S, M)

    def body(q_ref, k_ref, v_ref, o_ref):
        def chunk(i, carry):
            rows = pl.ds(pl.multiple_of(i * CH, CH), CH)
            p = _softmax_rows(_dot(q_ref[rows, :], k_ref[...], "nt") * (XA_HEAD_DIM ** -0.5))
            o_ref[rows, :] = _dot(p, v_ref[...]).astype(o_ref.dtype)
            return carry

        lax.fori_loop(0, S // CH, chunk, 0)

    return pl.pallas_call(
        body, out_shape=jax.ShapeDtypeStruct((B * S, XA_WIDTH), BF16), grid=(B, XA_HEADS),
        in_specs=[q_spec, k_spec, v_spec], out_specs=o_spec, name="xattn_fwd",
        compiler_params=_params(("parallel", "parallel")),
    )(proj, kv, kv)


def _xa_bwd(proj, kv, d_o, B, S, M, comm=None):
    CH = _xa_rows(S)
    q_spec, k_spec, v_spec, o_spec = _xa_specs(S, M)
    scale = XA_HEAD_DIM ** -0.5

    def body(q_ref, k_ref, v_ref, do_ref, dq_ref, dk_ref, dv_ref):
        dk_ref[...] = jnp.zeros_like(dk_ref)
        dv_ref[...] = jnp.zeros_like(dv_ref)

        def chunk(i, carry):
            rows = pl.ds(pl.multiple_of(i * CH, CH), CH)
            q, do = q_ref[rows, :], do_ref[rows, :]
            p = _softmax_rows(_dot(q, k_ref[...], "nt") * scale)
            dp = _dot(do, v_ref[...], "nt")
            ds = p * (dp - jnp.sum(dp * p, axis=-1, keepdims=True)) * scale
            dq_ref[rows, :] = _dot(ds, k_ref[...]).astype(dq_ref.dtype)
            dk_ref[...] += _dot(ds, q, "tn")
            dv_ref[...] += _dot(p, do, "tn")
            return carry

        lax.fori_loop(0, S // CH, chunk, 0)

    kv_out = pl.BlockSpec((M, XA_HEAD_DIM), lambda b, h: (b, h))
    return _pcall(
        body, (proj, kv, kv, d_o), name="xattn_bwd",
        out_shape=(jax.ShapeDtypeStruct((B * S, XA_WIDTH), BF16), jax.ShapeDtypeStruct((B * M, XA_WIDTH), F32),
                   jax.ShapeDtypeStruct((B * M, XA_WIDTH), F32)),
        grid=(B, XA_HEADS), in_specs=[q_spec, k_spec, v_spec, o_spec], out_specs=(o_spec, kv_out, kv_out),
        sem=("parallel", "parallel"), comm=comm)


def _gate_specs(tm):
    n = COL_GL // D_MODEL
    return [pl.BlockSpec((tm, D_MODEL), lambda i, j=j: (i, n + j)) for j in range(3)]


def _merge_fwd(proj, ys, tm=256):
    T = proj.shape[0]
    tm = _tile(T, tm)
    row = pl.BlockSpec((tm, D_MODEL), lambda i: (i, 0))

    def body(g0, g1, g2, y0, y1, y2, o_ref):
        acc = jax.nn.sigmoid(g0[...]) * y0[...]
        acc = acc + jax.nn.sigmoid(g1[...]) * y1[...]
        acc = acc + jax.nn.sigmoid(g2[...]) * y2[...]
        o_ref[...] = acc.astype(o_ref.dtype)

    return pl.pallas_call(
        body, out_shape=jax.ShapeDtypeStruct((T, D_MODEL), BF16), grid=(T // tm,),
        in_specs=_gate_specs(tm) + [row] * 3, out_specs=row, name="merge_fwd",
        compiler_params=_params(("parallel",)),
    )(proj, proj, proj, *ys)


def _merge_bwd(proj, ys, d_merged, tm=256, comm=None):
    T = proj.shape[0]
    tm = _tile(T, tm)
    row = pl.BlockSpec((tm, D_MODEL), lambda i: (i, 0))

    def body(g0, g1, g2, y0, y1, y2, dm_ref, dgl_ref, d0, d1, d2):
        dm = dm_ref[...]
        for j, (g_ref, y_ref, d_ref) in enumerate(((g0, y0, d0), (g1, y1, d1), (g2, y2, d2))):
            sg = jax.nn.sigmoid(g_ref[...])
            d_ref[...] = (dm * sg).astype(d_ref.dtype)
            dgl_ref[:, j * D_MODEL:(j + 1) * D_MODEL] = (dm * y_ref[...] * sg * (1.0 - sg)).astype(dgl_ref.dtype)

    dy = jax.ShapeDtypeStruct((T, D_MODEL), BF16)
    return _pcall(
        body, (proj, proj, proj, *ys, d_merged), name="merge_bwd",
        out_shape=(jax.ShapeDtypeStruct((T, 3 * D_MODEL), BF16), dy, dy, dy), grid=(T // tm,),
        in_specs=_gate_specs(tm) + [row] * 4,
        out_specs=(pl.BlockSpec((tm, 3 * D_MODEL), lambda i: (i, 0)), row, row, row),
        sem=("parallel",), comm=comm)


def _gelu(x):
    return 0.5 * x * (1.0 + jnp.tanh(GELU_C * (x + GELU_A * x * x * x)))


def _gelu_grad(x):
    t = jnp.tanh(GELU_C * (x + GELU_A * x * x * x))
    return 0.5 * (1.0 + t) + 0.5 * x * (1.0 - t * t) * GELU_C * (1.0 + 3.0 * GELU_A * x * x)


def _shift_down(x, prev, n):
    rows = x.shape[0]
    r = lax.broadcasted_iota(jnp.int32, (rows, 1), 0)
    out = pltpu.roll(x, n, axis=0)
    for j in range(n):
        out = jnp.where(r == j, prev[8 - n + j:8 - n + j + 1, :], out)
    return out


def _shift_up(x, nxt, n):
    rows = x.shape[0]
    r = lax.broadcasted_iota(jnp.int32, (rows, 1), 0)
    out = pltpu.roll(x, rows - n, axis=0)
    for j in range(n):
        out = jnp.where(r == rows - n + j, nxt[j:j + 1, :], out)
    return out


def _conv(a, prev, cw, cb):
    return _shift_down(a, prev, 2) * cw[0:1, :] + _shift_down(a, prev, 1) * cw[1:2, :] + a * cw[2:3, :] + cb


def _glu_fwd(up, cw, cb, S, tm=256):
    T = up.shape[2]
    tm = _tile(S, tm)
    per_seq = S // tm

    def body(ab_ref, prev_ref, cw_ref, cb_ref, u_ref):
        i = pl.program_id(1)
        prev = jnp.where(i % per_seq == 0, 0.0, prev_ref[...])
        ac = _conv(ab_ref[0], prev, cw_ref[...], cb_ref[...])
        u_ref[...] = (_gelu(ac) * ab_ref[1]).astype(u_ref.dtype)

    return pl.pallas_call(
        body, out_shape=jax.ShapeDtypeStruct((FFN_SLABS, T, UP_SHARD), BF16), grid=(FFN_SLABS, T // tm),
        in_specs=[pl.BlockSpec((2, None, tm, UP_SHARD), lambda d, i: (0, d, i, 0)),
                  pl.BlockSpec((None, None, 8, UP_SHARD), lambda d, i: (0, d, jnp.maximum(i * (tm // 8) - 1, 0), 0)),
                  pl.BlockSpec((None, 3, UP_SHARD), lambda d, i: (d, 0, 0)),
                  pl.BlockSpec((None, 1, UP_SHARD), lambda d, i: (d, 0, 0))],
        out_specs=pl.BlockSpec((None, tm, UP_SHARD), lambda d, i: (d, i, 0)), name="glu_fwd",
        compiler_params=_params(("parallel", "parallel")),
    )(up, up, cw, cb)


def _glu_bwd(up, d_u, cw, cb, S, tm=256, comm=None):
    T = up.shape[2]
    tm = _tile(S, tm)
    per_seq = S // tm
    n_tiles = T // tm
    last8 = tm // 8

    def body(ab_ref, prev_ref, abn_ref, du_ref, dun_ref, cw_ref, cb_ref, dup_ref, dcw_ref, dcb_ref):
        i = pl.program_id(1)

        @pl.when(i == 0)
        def _():
            dcw_ref[...] = jnp.zeros_like(dcw_ref)
            dcb_ref[...] = jnp.zeros_like(dcb_ref)

        cw, cb = cw_ref[...], cb_ref[...]
        a, b = ab_ref[0], ab_ref[1]
        prev = jnp.where(i % per_seq == 0, 0.0, prev_ref[...])
        a2, a1 = _shift_down(a, prev, 2), _shift_down(a, prev, 1)
        ac = a2 * cw[0:1, :] + a1 * cw[1:2, :] + a * cw[2:3, :] + cb
        du = du_ref[...]
        dup_ref[1] = (du * _gelu(ac)).astype(dup_ref.dtype)
        dac = du * b * _gelu_grad(ac)
        dcb_ref[...] += jnp.sum(dac, axis=0, keepdims=True)
        dcw_ref[0:1, :] += jnp.sum(dac * a2, axis=0, keepdims=True)
        dcw_ref[1:2, :] += jnp.sum(dac * a1, axis=0, keepdims=True)
        dcw_ref[2:3, :] += jnp.sum(dac * a, axis=0, keepdims=True)
        an = abn_ref[0]
        acn = _conv(an, a[tm - 8:, :], cw, cb)
        dacn = jnp.where(i % per_seq == per_seq - 1, 0.0, dun_ref[...] * abn_ref[1] * _gelu_grad(acn))
        da = dac * cw[2:3, :] + _shift_up(dac, dacn, 1) * cw[1:2, :] + _shift_up(dac, dacn, 2) * cw[0:1, :]
        dup_ref[0] = da.astype(dup_ref.dtype)

    def nxt(i):
        return jnp.minimum((i + 1) * last8, T // 8 - 1)

    return _pcall(
        body, (up, up, up, d_u, d_u, cw, cb), name="glu_bwd",
        out_shape=(jax.ShapeDtypeStruct((2, FFN_SLABS, T, UP_SHARD), BF16),
                   jax.ShapeDtypeStruct((FFN_SLABS, 3, UP_SHARD), F32),
                   jax.ShapeDtypeStruct((FFN_SLABS, 1, UP_SHARD), F32)),
        grid=(FFN_SLABS, n_tiles),
        in_specs=[pl.BlockSpec((2, None, tm, UP_SHARD), lambda d, i: (0, d, i, 0)),
                  pl.BlockSpec((None, None, 8, UP_SHARD), lambda d, i: (0, d, jnp.maximum(i * last8 - 1, 0), 0)),
                  pl.BlockSpec((2, None, 8, UP_SHARD), lambda d, i: (0, d, nxt(i), 0)),
                  pl.BlockSpec((None, tm, UP_SHARD), lambda d, i: (d, i, 0)),
                  pl.BlockSpec((None, 8, UP_SHARD), lambda d, i: (d, nxt(i), 0)),
                  pl.BlockSpec((None, 3, UP_SHARD), lambda d, i: (d, 0, 0)),
                  pl.BlockSpec((None, 1, UP_SHARD), lambda d, i: (d, 0, 0))],
        out_specs=(pl.BlockSpec((2, None, tm, UP_SHARD), lambda d, i: (0, d, i, 0)),
                   pl.BlockSpec((None, 3, UP_SHARD), lambda d, i: (d, 0, 0)),
                   pl.BlockSpec((None, 1, UP_SHARD), lambda d, i: (d, 0, 0))),
        sem=("parallel", "arbitrary"), comm=comm)


def _mm_up(h2, w_up, tm=MM_ROWS):
    T, K = h2.shape
    tm = _tile(T, tm)
    return _matmul(
        "mm_up", "nn", h2, w_up, jax.ShapeDtypeStruct((N_DEV, T, UP_SHARD), F32), (N_DEV, T // tm, 1),
        pl.BlockSpec((tm, K), lambda j, i, k: (i, 0)), pl.BlockSpec((None, K, UP_SHARD), lambda j, i, k: (j, 0, 0)),
        pl.BlockSpec((None, tm, UP_SHARD), lambda j, i, k: (j, i, 0)), (tm, UP_SHARD))


def _mm_down(u, w_down, res, tm=MM_ROWS):
    J, T, n = u.shape
    tm = _tile(T, tm)
    row = pl.BlockSpec((tm, D_MODEL), lambda i, d: (i, 0))
    return _matmul(
        "mm_down", "nn", u, w_down, jax.ShapeDtypeStruct((T, D_MODEL), F32), (T // tm, J),
        pl.BlockSpec((None, tm, n), lambda i, d: (d, i, 0)), pl.BlockSpec((None, n, D_MODEL), lambda i, d: (d, 0, 0)),
        row, (tm, D_MODEL), res, row)


def _mm_down_t(dx, w_down, tm=MM_ROWS):
    T = dx.shape[0]
    J, n, _ = w_down.shape
    tm = _tile(T, tm)
    return _matmul(
        "mm_down_t", "nt", dx, w_down, jax.ShapeDtypeStruct((J, T, n), F32), (J, T // tm, 1),
        pl.BlockSpec((tm, D_MODEL), lambda d, i, k: (i, 0)), pl.BlockSpec((None, n, D_MODEL), lambda d, i, k: (d, 0, 0)),
        pl.BlockSpec((None, tm, n), lambda d, i, k: (d, i, 0)), (tm, n))


def _mm_dw_down(u, dx, tk=MM_TOKENS):
    J, T, n = u.shape
    tk = _tile(T, tk)
    return _matmul(
        "mm_dw_down", "tn", u, dx, jax.ShapeDtypeStruct((J, n, D_MODEL), BF16), (J, T // tk),
        pl.BlockSpec((None, tk, n), lambda d, k: (d, k, 0)), pl.BlockSpec((tk, D_MODEL), lambda d, k: (k, 0)),
        pl.BlockSpec((None, n, D_MODEL), lambda d, k: (d, 0, 0)), (n, D_MODEL))


def _mm_dw_up(name, h2, d_up, part, tk=MM_TOKENS, comm=None):
    T, K = h2.shape
    p, of = part
    K //= of
    tk = _tile(T, tk)
    return _matmul(
        name, "tn", h2, d_up, jax.ShapeDtypeStruct((N_DEV, K, UP_SHARD), BF16), (N_DEV, T // tk),
        pl.BlockSpec((tk, K), lambda j, k: (k, p)), pl.BlockSpec((None, tk, UP_SHARD), lambda j, k: (j, k, 0)),
        pl.BlockSpec((None, K, UP_SHARD), lambda j, k: (j, 0, 0)), (K, UP_SHARD), comm=comm)


def _mm_up_t(d_up, w_up, tm=MM_ROWS, comm=None):
    J, T, n = d_up.shape
    K = w_up.shape[1]
    tm = _tile(T, tm)
    return _matmul(
        "mm_up_t", "nt", d_up, w_up, jax.ShapeDtypeStruct((T, K), F32), (T // tm, J),
        pl.BlockSpec((None, tm, n), lambda i, j: (j, i, 0)), pl.BlockSpec((None, K, n), lambda i, j: (j, 0, 0)),
        pl.BlockSpec((tm, K), lambda i, j: (i, 0)), (tm, K), comm=comm)


def _loss_head(x2, target, g_final, tm=512):
    T, Dm = x2.shape
    tm = _tile(T, tm)

    def body(x_ref, t_ref, g_ref, dx_ref, dg_ref, loss_ref):
        @pl.when(pl.program_id(0) == 0)
        def _():
            dg_ref[...] = jnp.zeros_like(dg_ref)
            loss_ref[...] = jnp.zeros_like(loss_ref)

        xv = x_ref[...]
        r = lax.rsqrt(jnp.mean(xv * xv, axis=-1, keepdims=True) + EPS)
        xhat = xv * r
        err = xhat * g_ref[...] - t_ref[...]
        loss_ref[...] += (0.5 / Dm) * jnp.sum(err * err)
        dy = err * (1.0 / Dm)
        dg_ref[...] += jnp.sum(dy * xhat, axis=0, keepdims=True)
        dxhat = dy * g_ref[...]
        dx_ref[...] = r * (dxhat - xhat * jnp.mean(dxhat * xhat, axis=-1, keepdims=True))

    row = pl.BlockSpec((tm, Dm), lambda i: (i, 0))
    vec = pl.BlockSpec((1, Dm), lambda i: (0, 0))
    return pl.pallas_call(
        body,
        out_shape=(jax.ShapeDtypeStruct((T, Dm), F32), jax.ShapeDtypeStruct((1, Dm), F32),
                   jax.ShapeDtypeStruct((1, Dm), F32)),
        grid=(T // tm,), in_specs=[row, row, vec], out_specs=(row, vec, vec), name="loss_head",
        compiler_params=_params(("arbitrary",)),
    )(x2, target, g_final)


def _cast_shards(shards):
    def body(*refs):
        n = len(refs) // 2
        for src, dst in zip(refs[:n], refs[n:]):
            dst[...] = src[...].astype(dst.dtype)

    return pl.pallas_call(
        body, out_shape=[jax.ShapeDtypeStruct(s.shape, BF16) for s in shards], name="cast_shards",
        compiler_params=pltpu.CompilerParams(vmem_limit_bytes=VMEM_LIMIT),
    )(*shards)


def _adamw(w, g, m, v):
    m = ADAM_B1 * m + (1.0 - ADAM_B1) * g
    v = ADAM_B2 * v + (1.0 - ADAM_B2) * (g * g)
    m_hat = m / (1.0 - ADAM_B1 ** ADAM_STEP)
    v_hat = v / (1.0 - ADAM_B2 ** ADAM_STEP)
    delta = -ADAM_LR * (m_hat / (jnp.sqrt(v_hat) + ADAM_EPS) + ADAM_WD * w)
    return delta, m, v


def _sum_parts(p_ref):
    g = p_ref[0].astype(F32)
    for d in range(1, N_DEV):
        g = g + p_ref[d].astype(F32)
    return g


def _reduce_adam(name, parts, w, m, v, tr=128):
    R, Cn = w.shape
    tr = math.gcd(tr, R)

    def body(*refs):
        p_refs = refs[:len(parts)]
        w_ref, m_ref, v_ref, g_out, d_out, m_out, v_out = refs[len(parts):]

        def update(p_ref):
            g = _sum_parts(p_ref)
            delta, m_new, v_new = _adamw(w_ref[...], g, m_ref[...], v_ref[...])
            g_out[...] = g
            d_out[...] = delta
            m_out[...] = m_new
            v_out[...] = v_new

        if len(parts) == 1:
            update(p_refs[0])
        else:
            x = lax.axis_index("x")
            for side, p_ref in enumerate(p_refs):
                pl.when(x == side)(functools.partial(update, p_ref))

    row = pl.BlockSpec((tr, Cn), lambda i: (i, 0))
    shape = jax.ShapeDtypeStruct((R, Cn), F32)
    return pl.pallas_call(
        body, out_shape=(shape,) * 4, grid=(R // tr,),
        in_specs=[pl.BlockSpec((N_DEV, tr, Cn), lambda i: (0, i, 0))] * len(parts) + [row, row, row],
        out_specs=(row,) * 4, name=name, compiler_params=_params(("parallel",)),
    )(*parts, w, m, v)


def _small_adam(name, gathered, params):
    n_g, n_p = len(gathered), len(params)

    def body(*refs):
        g_refs = refs[:n_g]
        wmv = refs[n_g:n_g + 3 * n_p]
        sums = refs[n_g + 3 * n_p:2 * n_g + 3 * n_p]
        upd = refs[2 * n_g + 3 * n_p:]
        for j in range(n_g):
            g = _sum_parts(g_refs[j])
            sums[j][...] = g
            if j < n_p:
                w_ref, m_ref, v_ref = wmv[3 * j:3 * j + 3]
                delta, m_new, v_new = _adamw(w_ref[...], g, m_ref[...], v_ref[...])
                upd[3 * j][...] = delta
                upd[3 * j + 1][...] = m_new
                upd[3 * j + 2][...] = v_new

    flat = [a for wmv in params for a in wmv]
    out_shape = [jax.ShapeDtypeStruct(g.shape[1:], F32) for g in gathered]
    out_shape += [jax.ShapeDtypeStruct(a.shape, F32) for a in flat]
    res = pl.pallas_call(body, out_shape=out_shape, name=name)(*gathered, *flat)
    return res[:n_g], [tuple(res[n_g + 3 * j:n_g + 3 * j + 3]) for j in range(n_p)]


def _adam_only(name, g, w, m, v):
    def body(g_ref, w_ref, m_ref, v_ref, d_out, m_out, v_out):
        delta, m_new, v_new = _adamw(w_ref[...], g_ref[...], m_ref[...], v_ref[...])
        d_out[...] = delta
        m_out[...] = m_new
        v_out[...] = v_new

    shape = jax.ShapeDtypeStruct(w.shape, F32)
    return pl.pallas_call(body, out_shape=(shape,) * 3, name=name)(g, w, m, v)


def kernel(x, mem, g_mix, w_in, w_pool, pool_scale, w_a, g_ret, b_ret, w_r, g_mem, w_mem_kv, w_c, w_out, g_ffn, w_up, conv_w, conv_b, w_down, g_final, loss_target, m_g_mix, m_w_in, m_w_pool, m_pool_scale, m_w_a, m_g_ret, m_b_ret, m_w_r, m_g_mem, m_w_mem_kv, m_w_c, m_w_out, m_g_ffn, m_w_up, m_conv_w, m_conv_b, m_w_down, m_g_final, v_g_mix, v_w_in, v_w_pool, v_pool_scale, v_w_a, v_g_ret, v_b_ret, v_w_r, v_g_mem, v_w_mem_kv, v_w_c, v_w_out, v_g_ffn, v_w_up, v_conv_w, v_conv_b, v_w_down, v_g_final):
    B, S, _ = x.shape
    M = mem.shape[1]
    T = B * S
    me = _my_index()
    x2d = x.reshape(T, D_MODEL)
    mem2d = mem.reshape(B * M, D_MODEL)
    tgt2d = loss_target.reshape(T, D_MODEL)
    g_final2 = g_final.reshape(1, D_MODEL)

    big = dict(w_in=w_in[0], w_a=w_a[0], w_r=w_r[0], w_mem_kv=w_mem_kv[0], w_c=w_c[0], w_out=w_out[0],
               w_up=w_up[0], w_down=w_down[0])
    names = list(big)
    cast = dict(zip(names, _cast_shards([big[n] for n in names])))
    Win, cw_gathered = _comm_call("gather_w_in", _Gather([cast["w_in"], conv_w[0]]))
    cw_full = cw_gathered.transpose(1, 0, 2).reshape(3, FFN_HIDDEN)
    cw = cw_full.reshape(3, FFN_SLABS, UP_SHARD).transpose(1, 0, 2)
    cb = conv_b[0].reshape(FFN_SLABS, 1, UP_SHARD)
    wp = w_pool[0]
    tables = _ret_tables(S)

    h = _rms_fwd("rms_mix", x2d, g_mix)
    early = ("w_a", "w_r", "w_mem_kv", "w_c", "w_out", "w_down")
    proj, landed = _mm_cols_slab("mm_in", h, Win, comm=_Gather([cast[n] for n in early]))
    W = dict(zip(early, landed))
    Wa = W["w_a"].transpose(1, 0, 2).reshape(POOL_WIDTH, D_MODEL)
    Wc = W["w_c"].transpose(1, 0, 2).reshape(XA_WIDTH, D_MODEL)
    Wr = W["w_r"].reshape(D_MODEL, D_MODEL)
    Wkv = W["w_mem_kv"].reshape(D_MODEL, D_MODEL)
    Wout = W["w_out"].reshape(D_MODEL, D_MODEL)
    Wdown = W["w_down"].reshape(FFN_SLABS, UP_SHARD, D_MODEL)
    ypre = _pool_fwd(proj, wp, pool_scale, B, S)
    y_pool = _mm_rows("mm_a", ypre, Wa)
    yr, (Wup,) = _ret_fwd(proj, g_ret, b_ret, tables, B, S, comm=_Gather([cast["w_up"]]))
    y_ret = _mm_rows("mm_r", yr, Wr)
    mem_n = _rms_fwd("rms_mem", mem2d, g_mem)
    kv = _mm_rows("mm_kv", mem_n, Wkv)
    o_mem = _xa_fwd(proj, kv, B, S, M)
    y_mem = _mm_rows("mm_c", o_mem, Wc)
    ys = (y_pool, y_ret, y_mem)
    merged = _merge_fwd(proj, ys)
    x1 = _mm_rows("mm_out", merged, Wout, res=x2d)
    h2 = _rms_fwd("rms_ffn", x1, g_ffn)
    up = _mm_up(h2, Wup).reshape(2, FFN_SLABS, T, UP_SHARD)
    u = _glu_fwd(up, cw, cb, S)
    x2 = _mm_down(u, Wdown, x1)

    dx2, dg_final, loss_part = _loss_head(x2, tgt2d, g_final2)
    received = {}
    d_u = _mm_down_t(dx2, Wdown)
    dW_down = _mm_dw_down(u, dx2)
    (d_up, d_cw, d_cb), (received["w_down"],) = _glu_bwd(
        up, d_u, cw, cb, S, comm=_Exchange([dW_down.reshape(N_DEV, -1, D_MODEL)]))
    d_up = d_up.reshape(N_DEV, T, UP_SHARD)
    dW_up = _mm_dw_up("mm_dw_up", h2, d_up, (0, 1))
    d_h2, (up_x0,) = _mm_up_t(d_up, Wup, comm=_ExchangeTo([dW_up], 0, 0))
    dx1, dg_ffn = _rms_bwd("rms_ffn_bwd", x1, g_ffn, d_h2, dx2)
    d_merged = _mm_rows("mm_out_t", dx1, Wout, kind="nt")
    dW_out = _mm_tn("mm_dw_out", merged, dx1, BF16)
    (d_gl, d_y_pool, d_y_ret, d_y_mem), (up_x1,) = _merge_bwd(
        proj, ys, d_merged, comm=_ExchangeTo([dW_up], 1, N_DEV // 2))
    received["w_up"] = [up_x0, up_x1]
    dW_c = _mm_tn("mm_dw_c", o_mem, d_y_mem, BF16)
    d_o_mem = _mm_rows("mm_c_t", d_y_mem, Wc, kind="nt")
    (d_qx, d_kmem, d_vmem), (received["w_out"],) = _xa_bwd(
        proj, kv, d_o_mem, B, S, M, comm=_Exchange([dW_out.reshape(N_DEV, -1, D_MODEL)]))
    d_kv = jnp.concatenate([d_kmem, d_vmem], axis=1)
    dW_kv = _mm_tn("mm_dw_kv", mem_n, d_kv, BF16)
    d_mem_n = _mm_rows("mm_kv_t", d_kv, Wkv, kind="nt")
    dg_mem = _rms_bwd("rms_mem_bwd", mem2d, g_mem, d_mem_n, None)
    d_proj_hi = jnp.concatenate([d_qx, d_gl], axis=1)
    dW_in_hi, landed = _mm_tn_slab(
        "mm_dw_in_hi", h, d_proj_hi, IN_SHARD, BF16,
        comm=_Exchange([dW_c.reshape(XA_WIDTH, N_DEV, -1).transpose(1, 0, 2), dW_kv.reshape(N_DEV, -1, D_MODEL)]))
    received["w_c"], received["w_mem_kv"] = landed
    dW_a = _mm_tn("mm_dw_a", ypre, d_y_pool, BF16)
    d_ypre = _mm_rows("mm_a_t", d_y_pool, Wa, kind="nt")
    d_hp, dw_pool, d_scale = _pool_bwd(proj, d_ypre, wp, pool_scale, B, S)
    dW_r = _mm_tn("mm_dw_r", yr, d_y_ret, BF16)
    d_yr = _mm_rows("mm_r_t", d_y_ret, Wr, kind="nt")
    (d_q, d_k, d_v, d_gr, dg_ret, db_ret), (in_x1, *landed) = _ret_bwd(
        proj, d_yr, g_ret, b_ret, tables, B, S,
        comm=_Both(_ExchangeTo([dW_in_hi], 1, 0),
                   _Exchange([dW_a.reshape(POOL_WIDTH, N_DEV, -1).transpose(1, 0, 2),
                              dW_r.reshape(N_DEV, -1, D_MODEL)])))
    received["w_a"], received["w_r"] = landed
    small_names = ["w_pool", "pool_scale", "g_ret", "b_ret", "g_mem", "g_ffn", "conv_b", "g_final"]
    small_grads = [dw_pool, d_scale, dg_ret, db_ret, dg_mem, dg_ffn, d_cb.reshape(1, FFN_HIDDEN), dg_final,
                   d_cw.transpose(1, 0, 2).reshape(3, FFN_HIDDEN), loss_part]
    d_proj_lo = jnp.concatenate([d_hp, d_q, d_k, d_v, d_gr], axis=1)
    dW_in_lo, small_all = _mm_tn_slab("mm_dw_in_lo", h, d_proj_lo, IN_SHARD, BF16,
                                      comm=_Exchange([], whole=small_grads))
    d_h, (in_x0,) = _mm_in_t(d_proj_lo, d_proj_hi, Win, comm=_ExchangeTo([dW_in_lo], 0, 0))
    received["w_in"] = [in_x0, in_x1]
    grad_x, dg_mix = _rms_bwd("rms_mix_bwd", x2d, g_mix, d_h, dx1)
    (g_mix_all,) = _comm_call("gather_g_mix", _Exchange([], whole=[dg_mix]))

    args = dict(g_mix=g_mix, w_in=w_in, w_pool=w_pool, pool_scale=pool_scale, w_a=w_a, g_ret=g_ret, b_ret=b_ret,
                w_r=w_r, g_mem=g_mem, w_mem_kv=w_mem_kv, w_c=w_c, w_out=w_out, g_ffn=g_ffn, w_up=w_up,
                conv_w=conv_w, conv_b=conv_b, w_down=w_down, g_final=g_final)
    m_in = dict(g_mix=m_g_mix, w_in=m_w_in, w_pool=m_w_pool, pool_scale=m_pool_scale, w_a=m_w_a, g_ret=m_g_ret,
                b_ret=m_b_ret, w_r=m_w_r, g_mem=m_g_mem, w_mem_kv=m_w_mem_kv, w_c=m_w_c, w_out=m_w_out,
                g_ffn=m_g_ffn, w_up=m_w_up, conv_w=m_conv_w, conv_b=m_conv_b, w_down=m_w_down, g_final=m_g_final)
    v_in = dict(g_mix=v_g_mix, w_in=v_w_in, w_pool=v_w_pool, pool_scale=v_pool_scale, w_a=v_w_a, g_ret=v_g_ret,
                b_ret=v_b_ret, w_r=v_w_r, g_mem=v_g_mem, w_mem_kv=v_w_mem_kv, w_c=v_w_c, w_out=v_w_out,
                g_ffn=v_g_ffn, w_up=v_w_up, conv_w=v_conv_w, conv_b=v_conv_b, w_down=v_w_down, g_final=v_g_final)

    grads, deltas, new_m, new_v = {}, {}, {}, {}
    for n in names:
        shard = big[n].shape
        parts = received[n] if isinstance(received[n], list) else [received[n]]
        outs = _reduce_adam("adam_" + n, parts, big[n], m_in[n][0], v_in[n][0])
        for store, val in zip((grads, deltas, new_m, new_v), outs):
            store[n] = val.reshape((1,) + shard)

    def as_small(a):
        return a.reshape(a.shape[-3:]) if a.ndim > 2 else a.reshape(1, -1)

    def small_update(call_name, param_names, gathered):
        params = [tuple(as_small(d[n]) for d in (args, m_in, v_in)) for n in param_names]
        sums, updates = _small_adam(call_name, gathered, params)
        for n, g, (d_, m_, v_) in zip(param_names, sums, updates):
            shape = args[n].shape
            grads[n], deltas[n], new_m[n], new_v[n] = (a.reshape(shape) for a in (g, d_, m_, v_))
        return sums[len(param_names):]

    g_cw_full, loss_row = small_update("adam_small", small_names, small_all)
    loss = loss_row[0, 0]
    small_update("adam_g_mix", ["g_mix"], [g_mix_all])

    shard_cols = FFN_HIDDEN // N_DEV
    g_cw = lax.dynamic_slice_in_dim(g_cw_full, me * shard_cols, shard_cols, axis=1)
    d_, m_, v_ = _adam_only("adam_conv_w", g_cw, conv_w[0], m_conv_w[0], v_conv_w[0])
    grads["conv_w"], deltas["conv_w"], new_m["conv_w"], new_v["conv_w"] = g_cw[None], d_[None], m_[None], v_[None]

    order = ["g_mix", "w_in", "w_pool", "pool_scale", "w_a", "g_ret", "b_ret", "w_r", "g_mem", "w_mem_kv", "w_c",
             "w_out", "g_ffn", "w_up", "conv_w", "conv_b", "w_down", "g_final"]
    return (loss, grad_x.reshape(B, S, D_MODEL), *[grads[n] for n in order], *[deltas[n] for n in order],
            *[new_m[n] for n in order], *[new_v[n] for n in order])
```

```python
import functools
import math

import jax
import jax.numpy as jnp
from jax import lax
from jax.experimental import pallas as pl
from jax.experimental.pallas import tpu as pltpu

F32 = jnp.float32
BF16 = jnp.bfloat16

N_DEV = 8
D_MODEL = 1024
POOL_WINDOWS = (2, 4, 8, 16)
POOL_GROUP_DIM = 128
POOL_WIDTH = 512
POOL_HALO = 16
RET_HEADS = 4
RET_QK_DIM = 128
RET_V_DIM = 256
RET_CHUNK = 128
ROPE_BASE = 10000.0
XA_HEADS = 4
XA_HEAD_DIM = 128
XA_WIDTH = 512
IN_WIDTH = 7168
IN_SHARD = IN_WIDTH // N_DEV
FFN_HIDDEN = 2816
UP_SHARD = 2 * FFN_HIDDEN // N_DEV
FFN_SLABS = FFN_HIDDEN // UP_SHARD
EPS = 1e-6
ADAM_LR = 0.001
ADAM_B1 = 0.9
ADAM_B2 = 0.999
ADAM_EPS = 1e-08
ADAM_WD = 0.01
ADAM_STEP = 10
GELU_C = math.sqrt(2.0 / math.pi)
GELU_A = 0.044715
VMEM_LIMIT = 56 * 1024 * 1024
MM_ROWS = 1024
MM_TOKENS = 1024
MESH = pl.DeviceIdType.MESH

COL_Q, COL_K, COL_V, COL_GR, COL_QX, COL_GL = 512, 1024, 1536, 2560, 3584, 4096

_DIMS = {
    "nn": (((1,), (0,)), ((), ())),
    "nt": (((1,), (1,)), ((), ())),
    "tn": (((0,), (0,)), ((), ())),
}


def _dot(a, b, kind="nn"):
    return lax.dot_general(a.astype(BF16), b.astype(BF16), _DIMS[kind], preferred_element_type=F32)


def _params(sem, vmem=VMEM_LIMIT):
    return pltpu.CompilerParams(dimension_semantics=sem, vmem_limit_bytes=vmem)


def _tile(n, pref):
    t = min(n, pref)
    while n % t:
        t //= 2
    return t


def _mesh_pos():
    return lax.axis_index("x"), lax.axis_index("y"), lax.axis_index("c")


def _dev_index(x, y, c):
    return 4 * x + 2 * y + c


def _my_index():
    return _dev_index(*_mesh_pos())


def _remote(src, dst, send_sems, recv_sems, s, to):
    return pltpu.make_async_remote_copy(src_ref=src, dst_ref=dst, send_sem=send_sems.at[s], recv_sem=recv_sems.at[s],
                                        device_id=to, device_id_type=MESH)


class _Gather:
    def __init__(self, shards):
        self.inputs = list(shards)
        self.out_shapes = [jax.ShapeDtypeStruct((N_DEV,) + s.shape, s.dtype) for s in shards]
        n = len(shards)
        self.sem_shapes = [pltpu.SemaphoreType.DMA((7 * n,)), pltpu.SemaphoreType.DMA((7 * n,)),
                           pltpu.SemaphoreType.DMA((n,))]

    def _places(self):
        x, y, c = _mesh_pos()
        return (x, y, c), (x, y, 1 - c), [(1 - x, y), (x, 1 - y), (1 - x, 1 - y)]

    def _local(self, src, dst, sems):
        me = _my_index()
        return [pltpu.make_async_copy(src[w], dst[w].at[me], sems[2].at[w]) for w in range(len(src))]

    def start(self, src, dst, sems):
        me, sib, chips = self._places()
        for cp in self._local(src, dst, sems):
            cp.start()
        for w in range(len(src)):
            land = dst[w].at[_dev_index(*me)]
            _remote(src[w], land, sems[0], sems[1], 7 * w, sib).start()
            for j, chip in enumerate(chips):
                _remote(src[w], land, sems[0], sems[1], 7 * w + 1 + j, (*chip, me[2])).start()

    def finish(self, src, dst, sems):
        me, sib, chips = self._places()
        n = len(src)
        for j, chip in enumerate(chips):
            for w in range(n):
                block = dst[w].at[_dev_index(*chip, me[2])]
                _remote(src[w], block, sems[0], sems[1], 7 * w + 1 + j, me).wait_recv()
                _remote(block, block, sems[0], sems[1], 7 * w + 4 + j, sib).start()
        for w in range(n):
            _remote(src[w], dst[w].at[_dev_index(*sib)], sems[0], sems[1], 7 * w, me).wait_recv()
            for j, chip in enumerate(chips):
                block = dst[w].at[_dev_index(*chip, sib[2])]
                _remote(block, block, sems[0], sems[1], 7 * w + 4 + j, me).wait_recv()
            for k in range(7):
                _remote(src[w], dst[w].at[0], sems[0], sems[1], 7 * w + k, me).wait_send()
        for cp in self._local(src, dst, sems):
            cp.wait()


class _Exchange:
    def __init__(self, partials, whole=()):
        self.n_part = len(partials)
        self.inputs = list(partials) + list(whole)
        self.out_shapes = [jax.ShapeDtypeStruct(p.shape, p.dtype) for p in partials]
        self.out_shapes += [jax.ShapeDtypeStruct((N_DEV,) + a.shape, a.dtype) for a in whole]
        n = len(self.inputs)
        self.sem_shapes = [pltpu.SemaphoreType.DMA((7 * n,)), pltpu.SemaphoreType.DMA((7 * n,)),
                           pltpu.SemaphoreType.DMA((n,))]

    def _peer(self, k):
        x, y, c = _mesh_pos()
        p = (x ^ ((k >> 2) & 1), y ^ ((k >> 1) & 1), c ^ (k & 1))
        return p, _dev_index(*p)

    def _source(self, src, w, slot):
        return src[w].at[slot] if w < self.n_part else src[w]

    def _local(self, src, dst, sems):
        me = _my_index()
        return [pltpu.make_async_copy(self._source(src, w, me), dst[w].at[me], sems[2].at[w])
                for w in range(len(src))]

    def start(self, src, dst, sems):
        me = _my_index()
        for cp in self._local(src, dst, sems):
            cp.start()
        for k in range(1, N_DEV):
            peer, peer_idx = self._peer(k)
            for w in range(len(src)):
                _remote(self._source(src, w, peer_idx), dst[w].at[me], sems[0], sems[1], 7 * w + k - 1, peer).start()

    def finish(self, src, dst, sems):
        for k in range(1, N_DEV):
            peer, peer_idx = self._peer(k)
            for w in range(len(src)):
                cp = _remote(self._source(src, w, peer_idx), dst[w].at[peer_idx], sems[0], sems[1], 7 * w + k - 1, peer)
                cp.wait_send()
                cp.wait_recv()
        for cp in self._local(src, dst, sems):
            cp.wait()


class _ExchangeTo:
    def __init__(self, partials, side):
        self.side = side
        self.inputs = list(partials)
        self.out_shapes = [jax.ShapeDtypeStruct(p.shape, p.dtype) for p in partials]
        n = len(partials)
        self.sem_shapes = [pltpu.SemaphoreType.DMA((7 * n,)), pltpu.SemaphoreType.DMA((7 * n,)),
                           pltpu.SemaphoreType.DMA((n,))]

    def _copies(self, src, dst, sems):
        x, y, c = _mesh_pos()
        me = _dev_index(x, y, c)
        receives = c == self.side
        remote = []
        for k in range(1, N_DEV):
            kx, ky, kc = (k >> 2) & 1, (k >> 1) & 1, k & 1
            peer = (x ^ kx, y ^ ky, c ^ kc)
            peer_idx = _dev_index(*peer)
            sends = c == (self.side ^ kc)
            for w in range(len(src)):
                slab = src[w].at[peer_idx]
                s = 7 * w + k - 1
                remote.append((sends, _remote(slab, dst[w].at[me], sems[0], sems[1], s, peer),
                               _remote(slab, dst[w].at[peer_idx], sems[0], sems[1], s, peer)))
        local = [pltpu.make_async_copy(src[w].at[me], dst[w].at[me], sems[2].at[w]) for w in range(len(src))]
        return receives, remote, local

    def start(self, src, dst, sems):
        receives, remote, local = self._copies(src, dst, sems)

        @pl.when(receives)
        def _():
            for cp in local:
                cp.start()

        for sends, send, _ in remote:
            pl.when(sends)(send.start)

    def finish(self, src, dst, sems):
        receives, remote, local = self._copies(src, dst, sems)
        for sends, send, arrive in remote:
            pl.when(sends)(send.wait_send)
            pl.when(receives)(arrive.wait_recv)

        @pl.when(receives)
        def _():
            for cp in local:
                cp.wait()


class _Both:
    def __init__(self, first, second):
        self.plans = (first, second)
        self.inputs = first.inputs + second.inputs
        self.out_shapes = first.out_shapes + second.out_shapes
        self.sem_shapes = first.sem_shapes + second.sem_shapes

    def _split(self, src, dst, sems):
        a = self.plans[0]
        ni, no = len(a.inputs), len(a.out_shapes)
        return (src[:ni], dst[:no], sems[:3]), (src[ni:], dst[no:], sems[3:])

    def start(self, src, dst, sems):
        for plan, part in zip(self.plans, self._split(src, dst, sems)):
            plan.start(*part)

    def finish(self, src, dst, sems):
        for plan, part in zip(self.plans, self._split(src, dst, sems)):
            plan.finish(*part)


def _pcall(body, args, *, name, out_shape, grid, in_specs, out_specs, scratch_shapes=(), sem=None, comm=None):
    single = not isinstance(out_shape, (tuple, list))
    outs = [out_shape] if single else list(out_shape)
    ospecs = [out_specs] if single else list(out_specs)
    n_in, n_out, n_scr = len(args), len(outs), len(scratch_shapes)

    def pick(res):
        return res[0] if single else tuple(res[:n_out])

    if comm is None:
        res = pl.pallas_call(
            body, out_shape=outs, grid=grid, in_specs=list(in_specs), out_specs=ospecs,
            scratch_shapes=list(scratch_shapes), name=name, compiler_params=_params(sem),
        )(*args)
        return pick(res), ()

    nci, nco = len(comm.inputs), len(comm.out_shapes)

    def carrier(*refs):
        at = 0
        parts = []
        for size in (n_in, nci, n_out, nco, n_scr, len(comm.sem_shapes)):
            parts.append(refs[at:at + size])
            at += size
        ins, cins, o, couts, scr, sems = parts
        ids = [pl.program_id(a) for a in range(len(grid))]
        first = functools.reduce(jnp.logical_and, [i == 0 for i in ids])
        last = functools.reduce(jnp.logical_and, [i == g - 1 for i, g in zip(ids, grid)])

        @pl.when(first)
        def _():
            comm.start(cins, couts, sems)

        body(*ins, *o, *scr)

        @pl.when(last)
        def _():
            comm.finish(cins, couts, sems)

    hbm = pl.BlockSpec(memory_space=pltpu.HBM)
    res = pl.pallas_call(
        carrier, out_shape=outs + comm.out_shapes, grid=grid, in_specs=list(in_specs) + [hbm] * nci,
        out_specs=ospecs + [hbm] * nco, scratch_shapes=list(scratch_shapes) + comm.sem_shapes, name=name,
        compiler_params=_params(("arbitrary",) * len(grid)),
    )(*args, *comm.inputs)
    return pick(res), tuple(res[n_out:])


def _comm_call(name, comm):
    def body(*refs):
        nci, nco = len(comm.inputs), len(comm.out_shapes)
        cins, couts, sems = refs[:nci], refs[nci:nci + nco], refs[nci + nco:]
        comm.start(cins, couts, sems)
        comm.finish(cins, couts, sems)

    hbm = pl.BlockSpec(memory_space=pltpu.HBM)
    return pl.pallas_call(
        body, out_shape=comm.out_shapes, in_specs=[hbm] * len(comm.inputs), out_specs=[hbm] * len(comm.out_shapes),
        scratch_shapes=comm.sem_shapes, name=name,
    )(*comm.inputs)


def _matmul(name, kind, a, b, out_shape, grid, a_spec, b_spec, o_spec, acc_shape, res=None, res_spec=None,
            comm=None):
    nk = grid[-1]
    has_res = res is not None

    def body(*refs):
        a_ref, b_ref = refs[0], refs[1]
        o_ref = refs[2 + has_res]

        def prod():
            return _dot(a_ref[...], b_ref[...], kind)

        def finish(acc):
            if has_res:
                acc = acc + refs[2][...]
            o_ref[...] = acc.astype(o_ref.dtype)

        if nk == 1:
            finish(prod())
        else:
            acc_ref = refs[3 + has_res]
            k = pl.program_id(len(grid) - 1)

            @pl.when(k == 0)
            def _():
                acc_ref[...] = prod()

            @pl.when(k > 0)
            def _():
                acc_ref[...] += prod()

            @pl.when(k == nk - 1)
            def _():
                finish(acc_ref[...])

    in_specs = [a_spec, b_spec] + ([res_spec] if has_res else [])
    args = (a, b) + ((res,) if has_res else ())
    scratch = [pltpu.VMEM(acc_shape, F32)] if nk > 1 else []
    sem = ("parallel",) * (len(grid) - 1) + ("arbitrary",)
    out, landed = _pcall(body, args, name=name, out_shape=out_shape, grid=grid, in_specs=in_specs,
                         out_specs=o_spec, scratch_shapes=scratch, sem=sem, comm=comm)
    return out if comm is None else (out, landed)


def _mm_rows(name, a, w, out_dtype=F32, res=None, kind="nn", tm=MM_ROWS, comm=None):
    M, K = a.shape
    N = w.shape[1] if kind == "nn" else w.shape[0]
    tm = _tile(M, tm)
    res_spec = pl.BlockSpec((tm, N), lambda i, k: (i, 0)) if res is not None else None
    return _matmul(
        name, kind, a, w, jax.ShapeDtypeStruct((M, N), out_dtype), (M // tm, 1),
        pl.BlockSpec((tm, K), lambda i, k: (i, 0)), pl.BlockSpec(w.shape, lambda i, k: (0, 0)),
        pl.BlockSpec((tm, N), lambda i, k: (i, 0)), (tm, N), res, res_spec, comm)


def _mm_tn(name, a, b, out_dtype=F32, tk=MM_TOKENS, comm=None):
    T, M = a.shape
    N = b.shape[1]
    tk = _tile(T, tk)
    return _matmul(
        name, "tn", a, b, jax.ShapeDtypeStruct((M, N), out_dtype), (1, T // tk),
        pl.BlockSpec((tk, M), lambda i, k: (k, 0)), pl.BlockSpec((tk, N), lambda i, k: (k, 0)),
        pl.BlockSpec((M, N), lambda i, k: (0, 0)), (M, N), comm=comm)


def _mm_cols_slab(name, a, w_slabs, out_dtype=F32, tm=MM_ROWS, comm=None):
    M, K = a.shape
    J, _, n = w_slabs.shape
    tm = _tile(M, tm)
    return _matmul(
        name, "nn", a, w_slabs, jax.ShapeDtypeStruct((M, J * n), out_dtype), (J, M // tm, 1),
        pl.BlockSpec((tm, K), lambda j, i, k: (i, 0)), pl.BlockSpec((None, K, n), lambda j, i, k: (j, 0, 0)),
        pl.BlockSpec((tm, n), lambda j, i, k: (i, j)), (tm, n), comm=comm)


def _mm_cols_slab_t(name, a, w_slabs, out_dtype=F32, tm=MM_ROWS, comm=None):
    M = a.shape[0]
    J, K, n = w_slabs.shape
    tm = _tile(M, tm)
    return _matmul(
        name, "nt", a, w_slabs, jax.ShapeDtypeStruct((M, K), out_dtype), (M // tm, J),
        pl.BlockSpec((tm, n), lambda i, j: (i, j)), pl.BlockSpec((None, K, n), lambda i, j: (j, 0, 0)),
        pl.BlockSpec((tm, K), lambda i, j: (i, 0)), (tm, K), comm=comm)


def _mm_in_t(d_lo, d_hi, w_slabs, tm=MM_ROWS, comm=None):
    M = d_lo.shape[0]
    J, K, n = w_slabs.shape
    half = J // 2
    tm = _tile(M, tm)

    def body(lo_ref, hi_ref, w_ref, o_ref, acc_ref):
        j = pl.program_id(1)

        @pl.when(j == 0)
        def _():
            acc_ref[...] = _dot(lo_ref[...], w_ref[...], "nt")

        @pl.when((j > 0) & (j < half))
        def _():
            acc_ref[...] += _dot(lo_ref[...], w_ref[...], "nt")

        @pl.when(j >= half)
        def _():
            acc_ref[...] += _dot(hi_ref[...], w_ref[...], "nt")

        @pl.when(j == J - 1)
        def _():
            o_ref[...] = acc_ref[...]

    out, landed = _pcall(
        body, (d_lo, d_hi, w_slabs), name="mm_in_t", out_shape=jax.ShapeDtypeStruct((M, K), F32), grid=(M // tm, J),
        in_specs=[pl.BlockSpec((tm, n), lambda i, j: (i, jnp.minimum(j, half - 1))),
                  pl.BlockSpec((tm, n), lambda i, j: (i, jnp.maximum(j - half, 0))),
                  pl.BlockSpec((None, K, n), lambda i, j: (j, 0, 0))],
        out_specs=pl.BlockSpec((tm, K), lambda i, j: (i, 0)), scratch_shapes=[pltpu.VMEM((tm, K), F32)],
        sem=("parallel", "arbitrary"), comm=comm)
    return out if comm is None else (out, landed)


def _mm_tn_slab(name, a, b, n, out_dtype=F32, tk=MM_TOKENS, comm=None, part=(0, 1)):
    T, M = a.shape
    p, of = part
    M //= of
    J = b.shape[1] // n
    tk = _tile(T, tk)
    return _matmul(
        name, "tn", a, b, jax.ShapeDtypeStruct((J, M, n), out_dtype), (J, T // tk),
        pl.BlockSpec((tk, M), lambda j, k: (k, p)), pl.BlockSpec((tk, n), lambda j, k: (k, j)),
        pl.BlockSpec((None, M, n), lambda j, k: (j, 0, 0)), (M, n), comm=comm)


def _rms_fwd(name, x, g, tm=512):
    T, Dm = x.shape
    tm = _tile(T, tm)

    def body(x_ref, g_ref, h_ref):
        xv = x_ref[...]
        r = lax.rsqrt(jnp.mean(xv * xv, axis=-1, keepdims=True) + EPS)
        h_ref[...] = (xv * r * g_ref[...]).astype(h_ref.dtype)

    return pl.pallas_call(
        body, out_shape=jax.ShapeDtypeStruct((T, Dm), BF16), grid=(T // tm,),
        in_specs=[pl.BlockSpec((tm, Dm), lambda i: (i, 0)), pl.BlockSpec((1, Dm), lambda i: (0, 0))],
        out_specs=pl.BlockSpec((tm, Dm), lambda i: (i, 0)), name=name, compiler_params=_params(("parallel",)),
    )(x, g)


def _rms_bwd(name, x, g, dh, dres, tm=512):
    T, Dm = x.shape
    tm = _tile(T, tm)
    want_dx = dres is not None

    def body(*refs):
        if want_dx:
            x_ref, g_ref, dh_ref, dres_ref, dx_ref, dg_ref = refs
        else:
            x_ref, g_ref, dh_ref, dg_ref = refs
        xv = x_ref[...]
        r = lax.rsqrt(jnp.mean(xv * xv, axis=-1, keepdims=True) + EPS)
        xhat = xv * r
        dhv = dh_ref[...]

        @pl.when(pl.program_id(0) == 0)
        def _():
            dg_ref[...] = jnp.zeros_like(dg_ref)

        dg_ref[...] += jnp.sum(dhv * xhat, axis=0, keepdims=True)
        if want_dx:
            dxhat = dhv * g_ref[...]
            dx_ref[...] = dres_ref[...] + r * (dxhat - xhat * jnp.mean(dxhat * xhat, axis=-1, keepdims=True))

    row = pl.BlockSpec((tm, Dm), lambda i: (i, 0))
    vec = pl.BlockSpec((1, Dm), lambda i: (0, 0))
    if want_dx:
        return pl.pallas_call(
            body, out_shape=(jax.ShapeDtypeStruct((T, Dm), F32), jax.ShapeDtypeStruct((1, Dm), F32)),
            grid=(T // tm,), in_specs=[row, vec, row, row], out_specs=(row, vec), name=name,
            compiler_params=_params(("arbitrary",)),
        )(x, g, dh, dres)
    return pl.pallas_call(
        body, out_shape=jax.ShapeDtypeStruct((1, Dm), F32), grid=(T // tm,), in_specs=[row, vec, row],
        out_specs=vec, name=name, compiler_params=_params(("arbitrary",)),
    )(x, g, dh)


def _pool_rows(S):
    return _tile(S, 256)


def _pool_count(c0, rows, w):
    t = c0 + lax.broadcasted_iota(jnp.int32, (rows, 1), 0)
    return jnp.minimum(t + 1, w).astype(F32)


def _pool_fwd(proj, w_pool, scale, B, S):
    CH = _pool_rows(S)

    def body(hp_ref, wp_ref, sc_ref, o_ref, pad_ref):
        pad_ref[0:POOL_HALO, :] = jnp.zeros((POOL_HALO, POOL_WIDTH), F32)
        pad_ref[POOL_HALO:, :] = hp_ref[...]
        for gi, w in enumerate(POOL_WINDOWS):
            cols = slice(gi * POOL_GROUP_DIM, (gi + 1) * POOL_GROUP_DIM)
            for c in range(S // CH):
                base = POOL_HALO + c * CH
                acc = pad_ref[base:base + CH, cols]
                tok = acc
                for j in range(1, w):
                    acc = acc + pad_ref[base - j:base - j + CH, cols]
                pooled = acc / _pool_count(c * CH, CH, w) - tok
                z = _dot(pooled, wp_ref[gi])
                o_ref[c * CH:(c + 1) * CH, cols] = (z * sc_ref[:, cols]).astype(o_ref.dtype)

    return pl.pallas_call(
        body, out_shape=jax.ShapeDtypeStruct((B * S, POOL_WIDTH), BF16), grid=(B,),
        in_specs=[pl.BlockSpec((S, POOL_WIDTH), lambda b: (b, 0)),
                  pl.BlockSpec(w_pool.shape, lambda b: (0, 0, 0)),
                  pl.BlockSpec((1, POOL_WIDTH), lambda b: (0, 0))],
        out_specs=pl.BlockSpec((S, POOL_WIDTH), lambda b: (b, 0)),
        scratch_shapes=[pltpu.VMEM((S + POOL_HALO, POOL_WIDTH), F32)],
        name="pool_fwd", compiler_params=_params(("parallel",)),
    )(proj, w_pool, scale)


def _pool_bwd(proj, d_ypre, w_pool, scale, B, S):
    CH = _pool_rows(S)

    def body(hp_ref, dy_ref, wp_ref, sc_ref, dhp_ref, dwp_ref, dsc_ref, pad_ref, sc_pad_ref, dp_ref):
        @pl.when(pl.program_id(0) == 0)
        def _():
            dwp_ref[...] = jnp.zeros_like(dwp_ref)
            dsc_ref[...] = jnp.zeros_like(dsc_ref)

        pad_ref[0:POOL_HALO, :] = jnp.zeros((POOL_HALO, POOL_WIDTH), F32)
        pad_ref[POOL_HALO:, :] = hp_ref[...]
        sc_pad_ref[S:, :] = jnp.zeros((POOL_HALO, POOL_WIDTH), F32)
        for gi, w in enumerate(POOL_WINDOWS):
            cols = slice(gi * POOL_GROUP_DIM, (gi + 1) * POOL_GROUP_DIM)
            for c in range(S // CH):
                base = POOL_HALO + c * CH
                rows = slice(c * CH, (c + 1) * CH)
                acc = pad_ref[base:base + CH, cols]
                tok = acc
                for j in range(1, w):
                    acc = acc + pad_ref[base - j:base - j + CH, cols]
                cnt = _pool_count(c * CH, CH, w)
                pooled = acc / cnt - tok
                z = _dot(pooled, wp_ref[gi])
                dy = dy_ref[rows, cols]
                dsc_ref[:, cols] += jnp.sum(dy * z, axis=0, keepdims=True)
                dz = dy * sc_ref[:, cols]
                dwp_ref[gi] += _dot(pooled, dz, "tn")
                dpool = _dot(dz, wp_ref[gi], "nt")
                dp_ref[rows, cols] = dpool
                sc_pad_ref[rows, cols] = dpool / cnt
            for c in range(S // CH):
                rows = slice(c * CH, (c + 1) * CH)
                acc = sc_pad_ref[rows, cols]
                for j in range(1, w):
                    acc = acc + sc_pad_ref[c * CH + j:c * CH + j + CH, cols]
                dhp_ref[rows, cols] = (acc - dp_ref[rows, cols]).astype(dhp_ref.dtype)

    seq = pl.BlockSpec((S, POOL_WIDTH), lambda b: (b, 0))
    return pl.pallas_call(
        body,
        out_shape=(jax.ShapeDtypeStruct((B * S, POOL_WIDTH), BF16),
                   jax.ShapeDtypeStruct(w_pool.shape, F32), jax.ShapeDtypeStruct((1, POOL_WIDTH), F32)),
        grid=(B,),
        in_specs=[seq, seq, pl.BlockSpec(w_pool.shape, lambda b: (0, 0, 0)),
                  pl.BlockSpec((1, POOL_WIDTH), lambda b: (0, 0))],
        out_specs=(seq, pl.BlockSpec(w_pool.shape, lambda b: (0, 0, 0)),
                   pl.BlockSpec((1, POOL_WIDTH), lambda b: (0, 0))),
        scratch_shapes=[pltpu.VMEM((S + POOL_HALO, POOL_WIDTH), F32),
                        pltpu.VMEM((S + POOL_HALO, POOL_WIDTH), F32),
                        pltpu.VMEM((S, POOL_WIDTH), F32)],
        name="pool_bwd", compiler_params=_params(("arbitrary",)),
    )(proj, d_ypre, w_pool, scale)


def _ret_tables(S):
    half = RET_QK_DIM // 2
    inv = ROPE_BASE ** (-jnp.arange(half, dtype=F32) / half)
    ang = jnp.arange(S, dtype=F32)[:, None] * inv[None, :]
    cos, sin = jnp.cos(ang), jnp.sin(ang)
    cos_full = jnp.concatenate([cos, cos], axis=-1)
    sin_signed = jnp.concatenate([-sin, sin], axis=-1)
    C = RET_CHUNK
    lg = jnp.log1p(-jnp.exp2(-5.0 - jnp.arange(RET_HEADS, dtype=F32)))[:, None, None]
    idx = jnp.arange(C, dtype=F32)
    rel = idx[:, None] - idx[None, :]
    decay = jnp.where(rel >= 0, jnp.exp(jnp.maximum(rel, 0.0) * lg), 0.0)
    q_decay = jnp.broadcast_to(jnp.exp((idx + 1.0)[None, :, None] * lg), (RET_HEADS, C, RET_QK_DIM))
    k_decay = jnp.broadcast_to(jnp.exp((C - 1.0 - idx)[None, :, None] * lg), (RET_HEADS, C, RET_QK_DIM))
    c_decay = jnp.broadcast_to(jnp.exp(C * lg), (RET_HEADS, 1, RET_V_DIM))
    return cos_full, sin_signed, decay, q_decay, k_decay, c_decay


def _rope(x, cos_full, sin_signed):
    return x * cos_full + pltpu.roll(x, RET_QK_DIM // 2, axis=1) * sin_signed


def _rope_t(dy, cos_full, sin_signed):
    return dy * cos_full + pltpu.roll(dy * sin_signed, RET_QK_DIM // 2, axis=1)


def _ret_specs(S, bh):
    def at(col_of_head, width):
        def index(*ids):
            b, h = bh(*ids)
            return (b, col_of_head + h)
        return pl.BlockSpec((S, width), index)

    def per_head(shape):
        def index(*ids):
            _, h = bh(*ids)
            return (h,) + (0,) * len(shape)
        return pl.BlockSpec((None,) + shape, index)

    def head_vec(width):
        def index(*ids):
            _, h = bh(*ids)
            return (0, h)
        return pl.BlockSpec((1, width), index)

    table = pl.BlockSpec((S, RET_QK_DIM), lambda *ids: (0, 0))
    C = RET_CHUNK
    return dict(
        q=at(COL_Q // RET_QK_DIM, RET_QK_DIM), k=at(COL_K // RET_QK_DIM, RET_QK_DIM),
        v=at(COL_V // RET_V_DIM, RET_V_DIM), gr=at(COL_GR // RET_V_DIM, RET_V_DIM),
        table=table, decay=per_head((C, C)), qd=per_head((C, RET_QK_DIM)), kd=per_head((C, RET_QK_DIM)),
        cd=per_head((1, RET_V_DIM)), vec=head_vec(RET_V_DIM), out_qk=at(0, RET_QK_DIM), out_v=at(0, RET_V_DIM))


def _group_norm(o):
    mu = jnp.mean(o, axis=-1, keepdims=True)
    oc = o - mu
    rstd = lax.rsqrt(jnp.mean(oc * oc, axis=-1, keepdims=True) + EPS)
    return oc * rstd, rstd


def _ret_fwd(proj, g_ret, b_ret, tables, B, S, comm=None):
    C = RET_CHUNK
    cos_t, sin_t, decay, q_decay, k_decay, c_decay = tables
    sp = _ret_specs(S, lambda b, h: (b, h))

    def body(q_ref, k_ref, v_ref, gr_ref, cos_ref, sin_ref, dec_ref, qd_ref, kd_ref, cd_ref, g_ref, b_ref,
             y_ref, r_ref):
        r_ref[...] = jnp.zeros_like(r_ref)

        def chunk(i, carry):
            rows = pl.ds(pl.multiple_of(i * C, C), C)
            cs, sn = cos_ref[rows, :], sin_ref[rows, :]
            q = _rope(q_ref[rows, :], cs, sn)
            k = _rope(k_ref[rows, :], cs, sn) * (RET_QK_DIM ** -0.5)
            v = v_ref[rows, :]
            R = r_ref[...]
            s = _dot(q, k, "nt") * dec_ref[...]
            o = _dot(s, v) + _dot(q * qd_ref[...], R)
            r_ref[...] = cd_ref[...] * R + _dot(k * kd_ref[...], v, "tn")
            on, _ = _group_norm(o)
            gr = gr_ref[rows, :]
            y_ref[rows, :] = (gr * jax.nn.sigmoid(gr) * (on * g_ref[...] + b_ref[...])).astype(y_ref.dtype)
            return carry

        lax.fori_loop(0, S // C, chunk, 0)

    return _pcall(
        body, (proj, proj, proj, proj, cos_t, sin_t, decay, q_decay, k_decay, c_decay, g_ret, b_ret),
        name="ret_fwd", out_shape=jax.ShapeDtypeStruct((B * S, RET_HEADS * RET_V_DIM), BF16), grid=(B, RET_HEADS),
        in_specs=[sp["q"], sp["k"], sp["v"], sp["gr"], sp["table"], sp["table"], sp["decay"], sp["qd"],
                  sp["kd"], sp["cd"], sp["vec"], sp["vec"]],
        out_specs=sp["out_v"], scratch_shapes=[pltpu.VMEM((RET_QK_DIM, RET_V_DIM), F32)],
        sem=("parallel", "parallel"), comm=comm)


def _ret_bwd(proj, d_yr, g_ret, b_ret, tables, B, S, comm=None):
    C = RET_CHUNK
    N = S // C
    cos_t, sin_t, decay, q_decay, k_decay, c_decay = tables
    sp = _ret_specs(S, lambda h, b: (b, h))
    qk_scale = RET_QK_DIM ** -0.5

    def body(q_ref, k_ref, v_ref, gr_ref, dy_ref, cos_ref, sin_ref, dec_ref, qd_ref, kd_ref, cd_ref, g_ref, b_ref,
             dq_ref, dk_ref, dv_ref, dgr_ref, dg_ref, db_ref, qr_ref, kr_ref, rs_ref, dr_ref):
        @pl.when(pl.program_id(1) == 0)
        def _():
            dg_ref[...] = jnp.zeros_like(dg_ref)
            db_ref[...] = jnp.zeros_like(db_ref)

        dr_ref[...] = jnp.zeros_like(dr_ref)

        def sweep(i, R):
            rows = pl.ds(pl.multiple_of(i * C, C), C)
            cs, sn = cos_ref[rows, :], sin_ref[rows, :]
            q = _rope(q_ref[rows, :], cs, sn)
            k = _rope(k_ref[rows, :], cs, sn) * qk_scale
            qr_ref[rows, :] = q
            kr_ref[rows, :] = k
            rs_ref[i] = R
            return cd_ref[...] * R + _dot(k * kd_ref[...], v_ref[rows, :], "tn")

        lax.fori_loop(0, N, sweep, jnp.zeros((RET_QK_DIM, RET_V_DIM), F32))

        def back(step, carry):
            i = N - 1 - step
            rows = pl.ds(pl.multiple_of(i * C, C), C)
            q, k, v = qr_ref[rows, :], kr_ref[rows, :], v_ref[rows, :]
            R, dR = rs_ref[i], dr_ref[...]
            dec, qd, kd = dec_ref[...], qd_ref[...], kd_ref[...]
            s = _dot(q, k, "nt") * dec
            o = _dot(s, v) + _dot(q * qd, R)
            on, rstd = _group_norm(o)
            oaff = on * g_ref[...] + b_ref[...]
            gr = gr_ref[rows, :]
            sg = jax.nn.sigmoid(gr)
            dy = dy_ref[rows, :]
            dgr_ref[rows, :] = (dy * oaff * (sg * (1.0 + gr * (1.0 - sg)))).astype(dgr_ref.dtype)
            doaff = dy * (gr * sg)
            dg_ref[...] += jnp.sum(doaff * on, axis=0, keepdims=True)
            db_ref[...] += jnp.sum(doaff, axis=0, keepdims=True)
            don = doaff * g_ref[...]
            do = rstd * (don - jnp.mean(don, axis=-1, keepdims=True)
                         - on * jnp.mean(don * on, axis=-1, keepdims=True))
            ds = _dot(do, v, "nt") * dec
            dq = _dot(ds, k) + qd * _dot(do, R, "nt")
            dk = _dot(ds, q, "tn") + kd * _dot(v, dR, "nt")
            dv_ref[rows, :] = (_dot(s, do, "tn") + _dot(k * kd, dR)).astype(dv_ref.dtype)
            dr_ref[...] = cd_ref[...] * dR + _dot(q * qd, do, "tn")
            cs, sn = cos_ref[rows, :], sin_ref[rows, :]
            dq_ref[rows, :] = _rope_t(dq, cs, sn).astype(dq_ref.dtype)
            dk_ref[rows, :] = _rope_t(dk * qk_scale, cs, sn).astype(dk_ref.dtype)
            return carry

        lax.fori_loop(0, N, back, 0)

    T = B * S
    qk_shape = jax.ShapeDtypeStruct((T, RET_HEADS * RET_QK_DIM), BF16)
    v_shape = jax.ShapeDtypeStruct((T, RET_HEADS * RET_V_DIM), BF16)
    vec_shape = jax.ShapeDtypeStruct((1, RET_HEADS * RET_V_DIM), F32)
    return _pcall(
        body, (proj, proj, proj, proj, d_yr, cos_t, sin_t, decay, q_decay, k_decay, c_decay, g_ret, b_ret),
        name="ret_bwd", out_shape=(qk_shape, qk_shape, v_shape, v_shape, vec_shape, vec_shape), grid=(RET_HEADS, B),
        in_specs=[sp["q"], sp["k"], sp["v"], sp["gr"], sp["out_v"], sp["table"], sp["table"], sp["decay"],
                  sp["qd"], sp["kd"], sp["cd"], sp["vec"], sp["vec"]],
        out_specs=(sp["out_qk"], sp["out_qk"], sp["out_v"], sp["out_v"], sp["vec"], sp["vec"]),
        scratch_shapes=[pltpu.VMEM((S, RET_QK_DIM), F32), pltpu.VMEM((S, RET_QK_DIM), F32),
                        pltpu.VMEM((N, RET_QK_DIM, RET_V_DIM), F32), pltpu.VMEM((RET_QK_DIM, RET_V_DIM), F32)],
        sem=("parallel", "arbitrary"), comm=comm)


def _xa_rows(S):
    return _tile(S, 256)


def _xa_specs(S, M):
    q = pl.BlockSpec((S, XA_HEAD_DIM), lambda b, h: (b, COL_QX // XA_HEAD_DIM + h))
    k = pl.BlockSpec((M, XA_HEAD_DIM), lambda b, h: (b, h))
    v = pl.BlockSpec((M, XA_HEAD_DIM), lambda b, h: (b, XA_HEADS + h))
    o = pl.BlockSpec((S, XA_HEAD_DIM), lambda b, h: (b, h))
    return q, k, v, o


def _softmax_rows(s):
    e = jnp.exp(s - jnp.max(s, axis=-1, keepdims=True))
    return e / jnp.sum(e, axis=-1, keepdims=True)


def _xa_fwd(proj, kv, B, S, M):
    CH = _xa_rows(S)
    q_spec, k_spec, v_spec, o_spec = _xa_specs(S, M)

    def body(q_ref, k_ref, v_ref, o_ref):
        def chunk(i, carry):
            rows = pl.ds(pl.multiple_of(i * CH, CH), CH)
            p = _softmax_rows(_dot(q_ref[rows, :], k_ref[...], "nt") * (XA_HEAD_DIM ** -0.5))
            o_ref[rows, :] = _dot(p, v_ref[...]).astype(o_ref.dtype)
            return carry

        lax.fori_loop(0, S // CH, chunk, 0)

    return pl.pallas_call(
        body, out_shape=jax.ShapeDtypeStruct((B * S, XA_WIDTH), BF16), grid=(B, XA_HEADS),
        in_specs=[q_spec, k_spec, v_spec], out_specs=o_spec, name="xattn_fwd",
        compiler_params=_params(("parallel", "parallel")),
    )(proj, kv, kv)


def _xa_bwd(proj, kv, d_o, B, S, M, comm=None):
    CH = _xa_rows(S)
    q_spec, k_spec, v_spec, o_spec = _xa_specs(S, M)
    scale = XA_HEAD_DIM ** -0.5

    def body(q_ref, k_ref, v_ref, do_ref, dq_ref, dk_ref, dv_ref):
        dk_ref[...] = jnp.zeros_like(dk_ref)
        dv_ref[...] = jnp.zeros_like(dv_ref)

        def chunk(i, carry):
            rows = pl.ds(pl.multiple_of(i * CH, CH), CH)
            q, do = q_ref[rows, :], do_ref[rows, :]
            p = _softmax_rows(_dot(q, k_ref[...], "nt") * scale)
            dp = _dot(do, v_ref[...], "nt")
            ds = p * (dp - jnp.sum(dp * p, axis=-1, keepdims=True)) * scale
            dq_ref[rows, :] = _dot(ds, k_ref[...]).astype(dq_ref.dtype)
            dk_ref[...] += _dot(ds, q, "tn")
            dv_ref[...] += _dot(p, do, "tn")
            return carry

        lax.fori_loop(0, S // CH, chunk, 0)

    kv_out = pl.BlockSpec((M, XA_HEAD_DIM), lambda b, h: (b, h))
    return _pcall(
        body, (proj, kv, kv, d_o), name="xattn_bwd",
        out_shape=(jax.ShapeDtypeStruct((B * S, XA_WIDTH), BF16), jax.ShapeDtypeStruct((B * M, XA_WIDTH), F32),
                   jax.ShapeDtypeStruct((B * M, XA_WIDTH), F32)),
        grid=(B, XA_HEADS), in_specs=[q_spec, k_spec, v_spec, o_spec], out_specs=(o_spec, kv_out, kv_out),
        sem=("parallel", "parallel"), comm=comm)


def _gate_specs(tm):
    n = COL_GL // D_MODEL
    return [pl.BlockSpec((tm, D_MODEL), lambda i, j=j: (i, n + j)) for j in range(3)]


def _merge_fwd(proj, ys, tm=256):
    T = proj.shape[0]
    tm = _tile(T, tm)
    row = pl.BlockSpec((tm, D_MODEL), lambda i: (i, 0))

    def body(g0, g1, g2, y0, y1, y2, o_ref):
        acc = jax.nn.sigmoid(g0[...]) * y0[...]
        acc = acc + jax.nn.sigmoid(g1[...]) * y1[...]
        acc = acc + jax.nn.sigmoid(g2[...]) * y2[...]
        o_ref[...] = acc.astype(o_ref.dtype)

    return pl.pallas_call(
        body, out_shape=jax.ShapeDtypeStruct((T, D_MODEL), BF16), grid=(T // tm,),
        in_specs=_gate_specs(tm) + [row] * 3, out_specs=row, name="merge_fwd",
        compiler_params=_params(("parallel",)),
    )(proj, proj, proj, *ys)


def _merge_bwd(proj, ys, d_merged, tm=256, comm=None):
    T = proj.shape[0]
    tm = _tile(T, tm)
    row = pl.BlockSpec((tm, D_MODEL), lambda i: (i, 0))

    def body(g0, g1, g2, y0, y1, y2, dm_ref, dgl_ref, d0, d1, d2):
        dm = dm_ref[...]
        for j, (g_ref, y_ref, d_ref) in enumerate(((g0, y0, d0), (g1, y1, d1), (g2, y2, d2))):
            sg = jax.nn.sigmoid(g_ref[...])
            d_ref[...] = (dm * sg).astype(d_ref.dtype)
            dgl_ref[:, j * D_MODEL:(j + 1) * D_MODEL] = (dm * y_ref[...] * sg * (1.0 - sg)).astype(dgl_ref.dtype)

    dy = jax.ShapeDtypeStruct((T, D_MODEL), BF16)
    return _pcall(
        body, (proj, proj, proj, *ys, d_merged), name="merge_bwd",
        out_shape=(jax.ShapeDtypeStruct((T, 3 * D_MODEL), BF16), dy, dy, dy), grid=(T // tm,),
        in_specs=_gate_specs(tm) + [row] * 4,
        out_specs=(pl.BlockSpec((tm, 3 * D_MODEL), lambda i: (i, 0)), row, row, row),
        sem=("parallel",), comm=comm)


def _gelu(x):
    return 0.5 * x * (1.0 + jnp.tanh(GELU_C * (x + GELU_A * x * x * x)))


def _gelu_grad(x):
    t = jnp.tanh(GELU_C * (x + GELU_A * x * x * x))
    return 0.5 * (1.0 + t) + 0.5 * x * (1.0 - t * t) * GELU_C * (1.0 + 3.0 * GELU_A * x * x)


def _shift_down(x, prev, n):
    rows = x.shape[0]
    r = lax.broadcasted_iota(jnp.int32, (rows, 1), 0)
    out = pltpu.roll(x, n, axis=0)
    for j in range(n):
        out = jnp.where(r == j, prev[8 - n + j:8 - n + j + 1, :], out)
    return out


def _shift_up(x, nxt, n):
    rows = x.shape[0]
    r = lax.broadcasted_iota(jnp.int32, (rows, 1), 0)
    out = pltpu.roll(x, rows - n, axis=0)
    for j in range(n):
        out = jnp.where(r == rows - n + j, nxt[j:j + 1, :], out)
    return out


def _conv(a, prev, cw, cb):
    return _shift_down(a, prev, 2) * cw[0:1, :] + _shift_down(a, prev, 1) * cw[1:2, :] + a * cw[2:3, :] + cb


def _glu_fwd(up, cw, cb, S, tm=256):
    T = up.shape[2]
    tm = _tile(S, tm)
    per_seq = S // tm

    def body(ab_ref, prev_ref, cw_ref, cb_ref, u_ref):
        i = pl.program_id(1)
        prev = jnp.where(i % per_seq == 0, 0.0, prev_ref[...])
        ac = _conv(ab_ref[0], prev, cw_ref[...], cb_ref[...])
        u_ref[...] = (_gelu(ac) * ab_ref[1]).astype(u_ref.dtype)

    return pl.pallas_call(
        body, out_shape=jax.ShapeDtypeStruct((FFN_SLABS, T, UP_SHARD), BF16), grid=(FFN_SLABS, T // tm),
        in_specs=[pl.BlockSpec((2, None, tm, UP_SHARD), lambda d, i: (0, d, i, 0)),
                  pl.BlockSpec((None, None, 8, UP_SHARD), lambda d, i: (0, d, jnp.maximum(i * (tm // 8) - 1, 0), 0)),
                  pl.BlockSpec((None, 3, UP_SHARD), lambda d, i: (d, 0, 0)),
                  pl.BlockSpec((None, 1, UP_SHARD), lambda d, i: (d, 0, 0))],
        out_specs=pl.BlockSpec((None, tm, UP_SHARD), lambda d, i: (d, i, 0)), name="glu_fwd",
        compiler_params=_params(("parallel", "parallel")),
    )(up, up, cw, cb)


def _glu_bwd(up, d_u, cw, cb, S, tm=256, comm=None):
    T = up.shape[2]
    tm = _tile(S, tm)
    per_seq = S // tm
    n_tiles = T // tm
    last8 = tm // 8

    def body(ab_ref, prev_ref, abn_ref, du_ref, dun_ref, cw_ref, cb_ref, dup_ref, dcw_ref, dcb_ref):
        i = pl.program_id(1)

        @pl.when(i == 0)
        def _():
            dcw_ref[...] = jnp.zeros_like(dcw_ref)
            dcb_ref[...] = jnp.zeros_like(dcb_ref)

        cw, cb = cw_ref[...], cb_ref[...]
        a, b = ab_ref[0], ab_ref[1]
        prev = jnp.where(i % per_seq == 0, 0.0, prev_ref[...])
        a2, a1 = _shift_down(a, prev, 2), _shift_down(a, prev, 1)
        ac = a2 * cw[0:1, :] + a1 * cw[1:2, :] + a * cw[2:3, :] + cb
        du = du_ref[...]
        dup_ref[1] = (du * _gelu(ac)).astype(dup_ref.dtype)
        dac = du * b * _gelu_grad(ac)
        dcb_ref[...] += jnp.sum(dac, axis=0, keepdims=True)
        dcw_ref[0:1, :] += jnp.sum(dac * a2, axis=0, keepdims=True)
        dcw_ref[1:2, :] += jnp.sum(dac * a1, axis=0, keepdims=True)
        dcw_ref[2:3, :] += jnp.sum(dac * a, axis=0, keepdims=True)
        an = abn_ref[0]
        acn = _conv(an, a[tm - 8:, :], cw, cb)
        dacn = jnp.where(i % per_seq == per_seq - 1, 0.0, dun_ref[...] * abn_ref[1] * _gelu_grad(acn))
        da = dac * cw[2:3, :] + _shift_up(dac, dacn, 1) * cw[1:2, :] + _shift_up(dac, dacn, 2) * cw[0:1, :]
        dup_ref[0] = da.astype(dup_ref.dtype)

    def nxt(i):
        return jnp.minimum((i + 1) * last8, T // 8 - 1)

    return _pcall(
        body, (up, up, up, d_u, d_u, cw, cb), name="glu_bwd",
        out_shape=(jax.ShapeDtypeStruct((2, FFN_SLABS, T, UP_SHARD), BF16),
                   jax.ShapeDtypeStruct((FFN_SLABS, 3, UP_SHARD), F32),
                   jax.ShapeDtypeStruct((FFN_SLABS, 1, UP_SHARD), F32)),
        grid=(FFN_SLABS, n_tiles),
        in_specs=[pl.BlockSpec((2, None, tm, UP_SHARD), lambda d, i: (0, d, i, 0)),
                  pl.BlockSpec((None, None, 8, UP_SHARD), lambda d, i: (0, d, jnp.maximum(i * last8 - 1, 0), 0)),
                  pl.BlockSpec((2, None, 8, UP_SHARD), lambda d, i: (0, d, nxt(i), 0)),
                  pl.BlockSpec((None, tm, UP_SHARD), lambda d, i: (d, i, 0)),
                  pl.BlockSpec((None, 8, UP_SHARD), lambda d, i: (d, nxt(i), 0)),
                  pl.BlockSpec((None, 3, UP_SHARD), lambda d, i: (d, 0, 0)),
                  pl.BlockSpec((None, 1, UP_SHARD), lambda d, i: (d, 0, 0))],
        out_specs=(pl.BlockSpec((2, None, tm, UP_SHARD), lambda d, i: (0, d, i, 0)),
                   pl.BlockSpec((None, 3, UP_SHARD), lambda d, i: (d, 0, 0)),
                   pl.BlockSpec((None, 1, UP_SHARD), lambda d, i: (d, 0, 0))),
        sem=("parallel", "arbitrary"), comm=comm)


def _mm_up(h2, w_up, tm=MM_ROWS):
    T, K = h2.shape
    tm = _tile(T, tm)
    return _matmul(
        "mm_up", "nn", h2, w_up, jax.ShapeDtypeStruct((N_DEV, T, UP_SHARD), F32), (N_DEV, T // tm, 1),
        pl.BlockSpec((tm, K), lambda j, i, k: (i, 0)), pl.BlockSpec((None, K, UP_SHARD), lambda j, i, k: (j, 0, 0)),
        pl.BlockSpec((None, tm, UP_SHARD), lambda j, i, k: (j, i, 0)), (tm, UP_SHARD))


def _mm_down(u, w_down, res, tm=MM_ROWS):
    J, T, n = u.shape
    tm = _tile(T, tm)
    row = pl.BlockSpec((tm, D_MODEL), lambda i, d: (i, 0))
    return _matmul(
        "mm_down", "nn", u, w_down, jax.ShapeDtypeStruct((T, D_MODEL), F32), (T // tm, J),
        pl.BlockSpec((None, tm, n), lambda i, d: (d, i, 0)), pl.BlockSpec((None, n, D_MODEL), lambda i, d: (d, 0, 0)),
        row, (tm, D_MODEL), res, row)


def _mm_down_t(dx, w_down, tm=MM_ROWS):
    T = dx.shape[0]
    J, n, _ = w_down.shape
    tm = _tile(T, tm)
    return _matmul(
        "mm_down_t", "nt", dx, w_down, jax.ShapeDtypeStruct((J, T, n), F32), (J, T // tm, 1),
        pl.BlockSpec((tm, D_MODEL), lambda d, i, k: (i, 0)), pl.BlockSpec((None, n, D_MODEL), lambda d, i, k: (d, 0, 0)),
        pl.BlockSpec((None, tm, n), lambda d, i, k: (d, i, 0)), (tm, n))


def _mm_dw_down(u, dx, tk=MM_TOKENS):
    J, T, n = u.shape
    tk = _tile(T, tk)
    return _matmul(
        "mm_dw_down", "tn", u, dx, jax.ShapeDtypeStruct((J, n, D_MODEL), BF16), (J, T // tk),
        pl.BlockSpec((None, tk, n), lambda d, k: (d, k, 0)), pl.BlockSpec((tk, D_MODEL), lambda d, k: (k, 0)),
        pl.BlockSpec((None, n, D_MODEL), lambda d, k: (d, 0, 0)), (n, D_MODEL))


def _mm_dw_up(name, h2, d_up, part, tk=MM_TOKENS, comm=None):
    T, K = h2.shape
    p, of = part
    K //= of
    tk = _tile(T, tk)
    return _matmul(
        name, "tn", h2, d_up, jax.ShapeDtypeStruct((N_DEV, K, UP_SHARD), BF16), (N_DEV, T // tk),
        pl.BlockSpec((tk, K), lambda j, k: (k, p)), pl.BlockSpec((None, tk, UP_SHARD), lambda j, k: (j, k, 0)),
        pl.BlockSpec((None, K, UP_SHARD), lambda j, k: (j, 0, 0)), (K, UP_SHARD), comm=comm)


def _mm_up_t(d_up, w_up, tm=MM_ROWS, comm=None):
    J, T, n = d_up.shape
    K = w_up.shape[1]
    tm = _tile(T, tm)
    return _matmul(
        "mm_up_t", "nt", d_up, w_up, jax.ShapeDtypeStruct((T, K), F32), (T // tm, J),
        pl.BlockSpec((None, tm, n), lambda i, j: (j, i, 0)), pl.BlockSpec((None, K, n), lambda i, j: (j, 0, 0)),
        pl.BlockSpec((tm, K), lambda i, j: (i, 0)), (tm, K), comm=comm)


def _loss_head(x2, target, g_final, tm=512):
    T, Dm = x2.shape
    tm = _tile(T, tm)

    def body(x_ref, t_ref, g_ref, dx_ref, dg_ref, loss_ref):
        @pl.when(pl.program_id(0) == 0)
        def _():
            dg_ref[...] = jnp.zeros_like(dg_ref)
            loss_ref[...] = jnp.zeros_like(loss_ref)

        xv = x_ref[...]
        r = lax.rsqrt(jnp.mean(xv * xv, axis=-1, keepdims=True) + EPS)
        xhat = xv * r
        err = xhat * g_ref[...] - t_ref[...]
        loss_ref[...] += (0.5 / Dm) * jnp.sum(err * err)
        dy = err * (1.0 / Dm)
        dg_ref[...] += jnp.sum(dy * xhat, axis=0, keepdims=True)
        dxhat = dy * g_ref[...]
        dx_ref[...] = r * (dxhat - xhat * jnp.mean(dxhat * xhat, axis=-1, keepdims=True))

    row = pl.BlockSpec((tm, Dm), lambda i: (i, 0))
    vec = pl.BlockSpec((1, Dm), lambda i: (0, 0))
    return pl.pallas_call(
        body,
        out_shape=(jax.ShapeDtypeStruct((T, Dm), F32), jax.ShapeDtypeStruct((1, Dm), F32),
                   jax.ShapeDtypeStruct((1, Dm), F32)),
        grid=(T // tm,), in_specs=[row, row, vec], out_specs=(row, vec, vec), name="loss_head",
        compiler_params=_params(("arbitrary",)),
    )(x2, target, g_final)


def _cast_shards(shards):
    def body(*refs):
        n = len(refs) // 2
        for src, dst in zip(refs[:n], refs[n:]):
            dst[...] = src[...].astype(dst.dtype)

    return pl.pallas_call(
        body, out_shape=[jax.ShapeDtypeStruct(s.shape, BF16) for s in shards], name="cast_shards",
        compiler_params=pltpu.CompilerParams(vmem_limit_bytes=VMEM_LIMIT),
    )(*shards)


def _adamw(w, g, m, v):
    m = ADAM_B1 * m + (1.0 - ADAM_B1) * g
    v = ADAM_B2 * v + (1.0 - ADAM_B2) * (g * g)
    m_hat = m / (1.0 - ADAM_B1 ** ADAM_STEP)
    v_hat = v / (1.0 - ADAM_B2 ** ADAM_STEP)
    delta = -ADAM_LR * (m_hat / (jnp.sqrt(v_hat) + ADAM_EPS) + ADAM_WD * w)
    return delta, m, v


def _sum_parts(p_ref):
    g = p_ref[0].astype(F32)
    for d in range(1, N_DEV):
        g = g + p_ref[d].astype(F32)
    return g


def _reduce_adam(name, parts, w, m, v, tr=128):
    R, Cn = w.shape
    by_rows = sum(p.shape[1] for p in parts) == R and len(parts) > 1
    tr = math.gcd(tr, *[p.shape[1] for p in parts])
    n_tiles = [p.shape[1] // tr for p in parts]
    first = [sum(n_tiles[:j]) for j in range(len(parts))] if by_rows else [0] * len(parts)

    def body(*refs):
        p_refs = refs[:len(parts)]
        w_ref, m_ref, v_ref, g_out, d_out, m_out, v_out = refs[len(parts):]

        def update(p_ref):
            g = _sum_parts(p_ref)
            delta, m_new, v_new = _adamw(w_ref[...], g, m_ref[...], v_ref[...])
            g_out[...] = g
            d_out[...] = delta
            m_out[...] = m_new
            v_out[...] = v_new

        if len(parts) == 1:
            update(p_refs[0])
        elif by_rows:
            i = pl.program_id(0)
            for p_ref, t0, n in zip(p_refs, first, n_tiles):
                pl.when((i >= t0) & (i < t0 + n))(functools.partial(update, p_ref))
        else:
            c = lax.axis_index("c")
            for side, p_ref in enumerate(p_refs):
                pl.when(c == side)(functools.partial(update, p_ref))

    def part_spec(t0, n):
        return pl.BlockSpec((N_DEV, tr, Cn), lambda i: (0, jnp.clip(i - t0, 0, n - 1), 0))

    row = pl.BlockSpec((tr, Cn), lambda i: (i, 0))
    shape = jax.ShapeDtypeStruct((R, Cn), F32)
    return pl.pallas_call(
        body, out_shape=(shape,) * 4, grid=(R // tr,),
        in_specs=[part_spec(t0, n) for t0, n in zip(first, n_tiles)] + [row, row, row],
        out_specs=(row,) * 4, name=name, compiler_params=_params(("parallel",)),
    )(*parts, w, m, v)


def _small_adam(name, gathered, params):
    n_g, n_p = len(gathered), len(params)

    def body(*refs):
        g_refs = refs[:n_g]
        wmv = refs[n_g:n_g + 3 * n_p]
        sums = refs[n_g + 3 * n_p:2 * n_g + 3 * n_p]
        upd = refs[2 * n_g + 3 * n_p:]
        for j in range(n_g):
            g = _sum_parts(g_refs[j])
            sums[j][...] = g
            if j < n_p:
                w_ref, m_ref, v_ref = wmv[3 * j:3 * j + 3]
                delta, m_new, v_new = _adamw(w_ref[...], g, m_ref[...], v_ref[...])
                upd[3 * j][...] = delta
                upd[3 * j + 1][...] = m_new
                upd[3 * j + 2][...] = v_new

    flat = [a for wmv in params for a in wmv]
    out_shape = [jax.ShapeDtypeStruct(g.shape[1:], F32) for g in gathered]
    out_shape += [jax.ShapeDtypeStruct(a.shape, F32) for a in flat]
    res = pl.pallas_call(body, out_shape=out_shape, name=name)(*gathered, *flat)
    return res[:n_g], [tuple(res[n_g + 3 * j:n_g + 3 * j + 3]) for j in range(n_p)]


def _adam_only(name, g, w, m, v):
    def body(g_ref, w_ref, m_ref, v_ref, d_out, m_out, v_out):
        delta, m_new, v_new = _adamw(w_ref[...], g_ref[...], m_ref[...], v_ref[...])
        d_out[...] = delta
        m_out[...] = m_new
        v_out[...] = v_new

    shape = jax.ShapeDtypeStruct(w.shape, F32)
    return pl.pallas_call(body, out_shape=(shape,) * 3, name=name)(g, w, m, v)


def kernel(x, mem, g_mix, w_in, w_pool, pool_scale, w_a, g_ret, b_ret, w_r, g_mem, w_mem_kv, w_c, w_out, g_ffn, w_up, conv_w, conv_b, w_down, g_final, loss_target, m_g_mix, m_w_in, m_w_pool, m_pool_scale, m_w_a, m_g_ret, m_b_ret, m_w_r, m_g_mem, m_w_mem_kv, m_w_c, m_w_out, m_g_ffn, m_w_up, m_conv_w, m_conv_b, m_w_down, m_g_final, v_g_mix, v_w_in, v_w_pool, v_pool_scale, v_w_a, v_g_ret, v_b_ret, v_w_r, v_g_mem, v_w_mem_kv, v_w_c, v_w_out, v_g_ffn, v_w_up, v_conv_w, v_conv_b, v_w_down, v_g_final):
    B, S, _ = x.shape
    M = mem.shape[1]
    T = B * S
    me = _my_index()
    x2d = x.reshape(T, D_MODEL)
    mem2d = mem.reshape(B * M, D_MODEL)
    tgt2d = loss_target.reshape(T, D_MODEL)
    g_final2 = g_final.reshape(1, D_MODEL)

    big = dict(w_in=w_in[0], w_a=w_a[0], w_r=w_r[0], w_mem_kv=w_mem_kv[0], w_c=w_c[0], w_out=w_out[0],
               w_up=w_up[0], w_down=w_down[0])
    names = list(big)
    cast = dict(zip(names, _cast_shards([big[n] for n in names])))
    Win, cw_gathered = _comm_call("gather_w_in", _Gather([cast["w_in"], conv_w[0]]))
    cw_full = cw_gathered.transpose(1, 0, 2).reshape(3, FFN_HIDDEN)
    cw = cw_full.reshape(3, FFN_SLABS, UP_SHARD).transpose(1, 0, 2)
    cb = conv_b[0].reshape(FFN_SLABS, 1, UP_SHARD)
    wp = w_pool[0]
    tables = _ret_tables(S)

    h = _rms_fwd("rms_mix", x2d, g_mix)
    early = ("w_a", "w_r", "w_mem_kv", "w_c", "w_out", "w_down")
    proj, landed = _mm_cols_slab("mm_in", h, Win, comm=_Gather([cast[n] for n in early]))
    W = dict(zip(early, landed))
    Wa = W["w_a"].transpose(1, 0, 2).reshape(POOL_WIDTH, D_MODEL)
    Wc = W["w_c"].transpose(1, 0, 2).reshape(XA_WIDTH, D_MODEL)
    Wr = W["w_r"].reshape(D_MODEL, D_MODEL)
    Wkv = W["w_mem_kv"].reshape(D_MODEL, D_MODEL)
    Wout = W["w_out"].reshape(D_MODEL, D_MODEL)
    Wdown = W["w_down"].reshape(FFN_SLABS, UP_SHARD, D_MODEL)
    ypre = _pool_fwd(proj, wp, pool_scale, B, S)
    y_pool = _mm_rows("mm_a", ypre, Wa)
    yr, (Wup,) = _ret_fwd(proj, g_ret, b_ret, tables, B, S, comm=_Gather([cast["w_up"]]))
    y_ret = _mm_rows("mm_r", yr, Wr)
    mem_n = _rms_fwd("rms_mem", mem2d, g_mem)
    kv = _mm_rows("mm_kv", mem_n, Wkv)
    o_mem = _xa_fwd(proj, kv, B, S, M)
    y_mem = _mm_rows("mm_c", o_mem, Wc)
    ys = (y_pool, y_ret, y_mem)
    merged = _merge_fwd(proj, ys)
    x1 = _mm_rows("mm_out", merged, Wout, res=x2d)
    h2 = _rms_fwd("rms_ffn", x1, g_ffn)
    up = _mm_up(h2, Wup).reshape(2, FFN_SLABS, T, UP_SHARD)
    u = _glu_fwd(up, cw, cb, S)
    x2 = _mm_down(u, Wdown, x1)

    dx2, dg_final, loss_part = _loss_head(x2, tgt2d, g_final2)
    received = {}
    d_u = _mm_down_t(dx2, Wdown)
    dW_down = _mm_dw_down(u, dx2)
    (d_up, d_cw, d_cb), (received["w_down"],) = _glu_bwd(
        up, d_u, cw, cb, S, comm=_Exchange([dW_down.reshape(N_DEV, -1, D_MODEL)]))
    d_up = d_up.reshape(N_DEV, T, UP_SHARD)
    dW_up = _mm_dw_up("mm_dw_up", h2, d_up, (0, 1))
    d_h2, (up_c0,) = _mm_up_t(d_up, Wup, comm=_ExchangeTo([dW_up], 0))
    dx1, dg_ffn = _rms_bwd("rms_ffn_bwd", x1, g_ffn, d_h2, dx2)
    d_merged = _mm_rows("mm_out_t", dx1, Wout, kind="nt")
    dW_out = _mm_tn("mm_dw_out", merged, dx1, BF16)
    (d_gl, d_y_pool, d_y_ret, d_y_mem), (up_c1,) = _merge_bwd(proj, ys, d_merged, comm=_ExchangeTo([dW_up], 1))
    received["w_up"] = [up_c0, up_c1]
    dW_c = _mm_tn("mm_dw_c", o_mem, d_y_mem, BF16)
    d_o_mem = _mm_rows("mm_c_t", d_y_mem, Wc, kind="nt")
    (d_qx, d_kmem, d_vmem), (received["w_out"],) = _xa_bwd(
        proj, kv, d_o_mem, B, S, M, comm=_Exchange([dW_out.reshape(N_DEV, -1, D_MODEL)]))
    d_kv = jnp.concatenate([d_kmem, d_vmem], axis=1)
    dW_kv = _mm_tn("mm_dw_kv", mem_n, d_kv, BF16)
    d_mem_n = _mm_rows("mm_kv_t", d_kv, Wkv, kind="nt")
    dg_mem = _rms_bwd("rms_mem_bwd", mem2d, g_mem, d_mem_n, None)
    dW_a = _mm_tn("mm_dw_a", ypre, d_y_pool, BF16)
    d_ypre = _mm_rows("mm_a_t", d_y_pool, Wa, kind="nt")
    d_hp, dw_pool, d_scale = _pool_bwd(proj, d_ypre, wp, pool_scale, B, S)
    dW_r = _mm_tn("mm_dw_r", yr, d_y_ret, BF16)
    d_yr = _mm_rows("mm_r_t", d_y_ret, Wr, kind="nt")
    (d_q, d_k, d_v, d_gr, dg_ret, db_ret), landed = _ret_bwd(
        proj, d_yr, g_ret, b_ret, tables, B, S,
        comm=_Exchange([dW_a.reshape(POOL_WIDTH, N_DEV, -1).transpose(1, 0, 2), dW_r.reshape(N_DEV, -1, D_MODEL),
                        dW_c.reshape(XA_WIDTH, N_DEV, -1).transpose(1, 0, 2), dW_kv.reshape(N_DEV, -1, D_MODEL)]))
    received["w_a"], received["w_r"], received["w_c"], received["w_mem_kv"] = landed
    small_names = ["w_pool", "pool_scale", "g_ret", "b_ret", "g_mem", "g_ffn", "conv_b", "g_final"]
    small_grads = [dw_pool, d_scale, dg_ret, db_ret, dg_mem, dg_ffn, d_cb.reshape(1, FFN_HIDDEN), dg_final,
                   d_cw.transpose(1, 0, 2).reshape(3, FFN_HIDDEN), loss_part]
    d_proj = jnp.concatenate([d_hp, d_q, d_k, d_v, d_gr, d_qx, d_gl], axis=1)
    dW_in0 = _mm_tn_slab("mm_dw_in0", h, d_proj, IN_SHARD, BF16, part=(0, 2))
    dW_in1, (in0,) = _mm_tn_slab("mm_dw_in1", h, d_proj, IN_SHARD, BF16, part=(1, 2), comm=_Exchange([dW_in0]))
    d_h, (in1, *small_all) = _mm_cols_slab_t("mm_in_t", d_proj, Win, comm=_Exchange([dW_in1], whole=small_grads))
    received["w_in"] = [in0, in1]
    grad_x, dg_mix = _rms_bwd("rms_mix_bwd", x2d, g_mix, d_h, dx1)
    (g_mix_all,) = _comm_call("gather_g_mix", _Exchange([], whole=[dg_mix]))

    args = dict(g_mix=g_mix, w_in=w_in, w_pool=w_pool, pool_scale=pool_scale, w_a=w_a, g_ret=g_ret, b_ret=b_ret,
                w_r=w_r, g_mem=g_mem, w_mem_kv=w_mem_kv, w_c=w_c, w_out=w_out, g_ffn=g_ffn, w_up=w_up,
                conv_w=conv_w, conv_b=conv_b, w_down=w_down, g_final=g_final)
    m_in = dict(g_mix=m_g_mix, w_in=m_w_in, w_pool=m_w_pool, pool_scale=m_pool_scale, w_a=m_w_a, g_ret=m_g_ret,
                b_ret=m_b_ret, w_r=m_w_r, g_mem=m_g_mem, w_mem_kv=m_w_mem_kv, w_c=m_w_c, w_out=m_w_out,
                g_ffn=m_g_ffn, w_up=m_w_up, conv_w=m_conv_w, conv_b=m_conv_b, w_down=m_w_down, g_final=m_g_final)
    v_in = dict(g_mix=v_g_mix, w_in=v_w_in, w_pool=v_w_pool, pool_scale=v_pool_scale, w_a=v_w_a, g_ret=v_g_ret,
                b_ret=v_b_ret, w_r=v_w_r, g_mem=v_g_mem, w_mem_kv=v_w_mem_kv, w_c=v_w_c, w_out=v_w_out,
                g_ffn=v_g_ffn, w_up=v_w_up, conv_w=v_conv_w, conv_b=v_conv_b, w_down=v_w_down, g_final=v_g_final)

    grads, deltas, new_m, new_v = {}, {}, {}, {}
    for n in names:
        shard = big[n].shape
        parts = received[n] if isinstance(received[n], list) else [received[n]]
        outs = _reduce_adam("adam_" + n, parts, big[n], m_in[n][0], v_in[n][0])
        for store, val in zip((grads, deltas, new_m, new_v), outs):
            store[n] = val.reshape((1,) + shard)

    def as_small(a):
        return a.reshape(a.shape[-3:]) if a.ndim > 2 else a.reshape(1, -1)

    def small_update(call_name, param_names, gathered):
        params = [tuple(as_small(d[n]) for d in (args, m_in, v_in)) for n in param_names]
        sums, updates = _small_adam(call_name, gathered, params)
        for n, g, (d_, m_, v_) in zip(param_names, sums, updates):
            shape = args[n].shape
            grads[n], deltas[n], new_m[n], new_v[n] = (a.reshape(shape) for a in (g, d_, m_, v_))
        return sums[len(param_names):]

    g_cw_full, loss_row = small_update("adam_small", small_names, small_all)
    loss = loss_row[0, 0]
    small_update("adam_g_mix", ["g_mix"], [g_mix_all])

    shard_cols = FFN_HIDDEN // N_DEV
    g_cw = lax.dynamic_slice_in_dim(g_cw_full, me * shard_cols, shard_cols, axis=1)
    d_, m_, v_ = _adam_only("adam_conv_w", g_cw, conv_w[0], m_conv_w[0], v_conv_w[0])
    grads["conv_w"], deltas["conv_w"], new_m["conv_w"], new_v["conv_w"] = g_cw[None], d_[None], m_[None], v_[None]

    order = ["g_mix", "w_in", "w_pool", "pool_scale", "w_a", "g_ret", "b_ret", "w_r", "g_mem", "w_mem_kv", "w_c",
             "w_out", "g_ffn", "w_up", "conv_w", "conv_b", "w_down", "g_final"]
    return (loss, grad_x.reshape(B, S, D_MODEL), *[grads[n] for n in order], *[deltas[n] for n in order],
            *[new_m[n] for n in order], *[new_v[n] for n in order])
```

```python
import functools
import math

import jax
import jax.numpy as jnp
from jax import lax
from jax.experimental import pallas as pl
from jax.experimental.pallas import tpu as pltpu

F32 = jnp.float32
BF16 = jnp.bfloat16

N_DEV = 8
D_MODEL = 1024
POOL_WINDOWS = (2, 4, 8, 16)
POOL_GROUP_DIM = 128
POOL_WIDTH = 512
POOL_HALO = 16
RET_HEADS = 4
RET_QK_DIM = 128
RET_V_DIM = 256
RET_CHUNK = 128
ROPE_BASE = 10000.0
XA_HEADS = 4
XA_HEAD_DIM = 128
XA_WIDTH = 512
IN_WIDTH = 7168
IN_SHARD = IN_WIDTH // N_DEV
FFN_HIDDEN = 2816
UP_SHARD = 2 * FFN_HIDDEN // N_DEV
FFN_SLABS = FFN_HIDDEN // UP_SHARD
EPS = 1e-6
ADAM_LR = 0.001
ADAM_B1 = 0.9
ADAM_B2 = 0.999
ADAM_EPS = 1e-08
ADAM_WD = 0.01
ADAM_STEP = 10
GELU_C = math.sqrt(2.0 / math.pi)
GELU_A = 0.044715
VMEM_LIMIT = 56 * 1024 * 1024
MM_ROWS = 1024
MM_TOKENS = 1024
MESH = pl.DeviceIdType.MESH

COL_Q, COL_K, COL_V, COL_GR, COL_QX, COL_GL = 512, 1024, 1536, 2560, 3584, 4096

_DIMS = {
    "nn": (((1,), (0,)), ((), ())),
    "nt": (((1,), (1,)), ((), ())),
    "tn": (((0,), (0,)), ((), ())),
}


def _dot(a, b, kind="nn"):
    return lax.dot_general(a.astype(BF16), b.astype(BF16), _DIMS[kind], preferred_element_type=F32)


def _params(sem, vmem=VMEM_LIMIT):
    return pltpu.CompilerParams(dimension_semantics=sem, vmem_limit_bytes=vmem)


def _tile(n, pref):
    t = min(n, pref)
    while n % t:
        t //= 2
    return t


def _mesh_pos():
    return lax.axis_index("x"), lax.axis_index("y"), lax.axis_index("c")


def _dev_index(x, y, c):
    return 4 * x + 2 * y + c


def _my_index():
    return _dev_index(*_mesh_pos())


def _remote(src, dst, send_sems, recv_sems, s, to):
    return pltpu.make_async_remote_copy(src_ref=src, dst_ref=dst, send_sem=send_sems.at[s], recv_sem=recv_sems.at[s],
                                        device_id=to, device_id_type=MESH)


class _Gather:
    def __init__(self, shards):
        self.inputs = list(shards)
        self.out_shapes = [jax.ShapeDtypeStruct((N_DEV,) + s.shape, s.dtype) for s in shards]
        n = len(shards)
        self.sem_shapes = [pltpu.SemaphoreType.DMA((7 * n,)), pltpu.SemaphoreType.DMA((7 * n,)),
                           pltpu.SemaphoreType.DMA((n,))]

    def _places(self):
        x, y, c = _mesh_pos()
        return (x, y, c), (x, y, 1 - c), [(1 - x, y), (x, 1 - y), (1 - x, 1 - y)]

    def _local(self, src, dst, sems):
        me = _my_index()
        return [pltpu.make_async_copy(src[w], dst[w].at[me], sems[2].at[w]) for w in range(len(src))]

    def start(self, src, dst, sems):
        me, sib, chips = self._places()
        for cp in self._local(src, dst, sems):
            cp.start()
        for w in range(len(src)):
            land = dst[w].at[_dev_index(*me)]
            _remote(src[w], land, sems[0], sems[1], 7 * w, sib).start()
            for j, chip in enumerate(chips):
                _remote(src[w], land, sems[0], sems[1], 7 * w + 1 + j, (*chip, me[2])).start()

    def finish(self, src, dst, sems):
        me, sib, chips = self._places()
        n = len(src)
        for j, chip in enumerate(chips):
            for w in range(n):
                block = dst[w].at[_dev_index(*chip, me[2])]
                _remote(src[w], block, sems[0], sems[1], 7 * w + 1 + j, me).wait_recv()
                _remote(block, block, sems[0], sems[1], 7 * w + 4 + j, sib).start()
        for w in range(n):
            _remote(src[w], dst[w].at[_dev_index(*sib)], sems[0], sems[1], 7 * w, me).wait_recv()
            for j, chip in enumerate(chips):
                block = dst[w].at[_dev_index(*chip, sib[2])]
                _remote(block, block, sems[0], sems[1], 7 * w + 4 + j, me).wait_recv()
            for k in range(7):
                _remote(src[w], dst[w].at[0], sems[0], sems[1], 7 * w + k, me).wait_send()
        for cp in self._local(src, dst, sems):
            cp.wait()


class _Exchange:
    def __init__(self, partials, whole=()):
        self.n_part = len(partials)
        self.inputs = list(partials) + list(whole)
        self.out_shapes = [jax.ShapeDtypeStruct(p.shape, p.dtype) for p in partials]
        self.out_shapes += [jax.ShapeDtypeStruct((N_DEV,) + a.shape, a.dtype) for a in whole]
        n = len(self.inputs)
        self.sem_shapes = [pltpu.SemaphoreType.DMA((7 * n,)), pltpu.SemaphoreType.DMA((7 * n,)),
                           pltpu.SemaphoreType.DMA((n,))]

    def _peer(self, k):
        x, y, c = _mesh_pos()
        p = (x ^ ((k >> 2) & 1), y ^ ((k >> 1) & 1), c ^ (k & 1))
        return p, _dev_index(*p)

    def _source(self, src, w, slot):
        return src[w].at[slot] if w < self.n_part else src[w]

    def _local(self, src, dst, sems):
        me = _my_index()
        return [pltpu.make_async_copy(self._source(src, w, me), dst[w].at[me], sems[2].at[w])
                for w in range(len(src))]

    def start(self, src, dst, sems):
        me = _my_index()
        for cp in self._local(src, dst, sems):
            cp.start()
        for k in range(1, N_DEV):
            peer, peer_idx = self._peer(k)
            for w in range(len(src)):
                _remote(self._source(src, w, peer_idx), dst[w].at[me], sems[0], sems[1], 7 * w + k - 1, peer).start()

    def finish(self, src, dst, sems):
        for k in range(1, N_DEV):
            peer, peer_idx = self._peer(k)
            for w in range(len(src)):
                cp = _remote(self._source(src, w, peer_idx), dst[w].at[peer_idx], sems[0], sems[1], 7 * w + k - 1, peer)
                cp.wait_send()
                cp.wait_recv()
        for cp in self._local(src, dst, sems):
            cp.wait()


class _ExchangeTo:
    def __init__(self, partials, side):
        self.side = side
        self.inputs = list(partials)
        self.out_shapes = [jax.ShapeDtypeStruct(p.shape, p.dtype) for p in partials]
        n = len(partials)
        self.sem_shapes = [pltpu.SemaphoreType.DMA((7 * n,)), pltpu.SemaphoreType.DMA((7 * n,)),
                           pltpu.SemaphoreType.DMA((n,))]

    def _copies(self, src, dst, sems):
        x, y, c = _mesh_pos()
        me = _dev_index(x, y, c)
        receives = c == self.side
        remote = []
        for k in range(1, N_DEV):
            kx, ky, kc = (k >> 2) & 1, (k >> 1) & 1, k & 1
            peer = (x ^ kx, y ^ ky, c ^ kc)
            peer_idx = _dev_index(*peer)
            sends = c == (self.side ^ kc)
            for w in range(len(src)):
                slab = src[w].at[peer_idx]
                s = 7 * w + k - 1
                remote.append((sends, _remote(slab, dst[w].at[me], sems[0], sems[1], s, peer),
                               _remote(slab, dst[w].at[peer_idx], sems[0], sems[1], s, peer)))
        local = [pltpu.make_async_copy(src[w].at[me], dst[w].at[me], sems[2].at[w]) for w in range(len(src))]
        return receives, remote, local

    def start(self, src, dst, sems):
        receives, remote, local = self._copies(src, dst, sems)

        @pl.when(receives)
        def _():
            for cp in local:
                cp.start()

        for sends, send, _ in remote:
            pl.when(sends)(send.start)

    def finish(self, src, dst, sems):
        receives, remote, local = self._copies(src, dst, sems)
        for sends, send, arrive in remote:
            pl.when(sends)(send.wait_send)
            pl.when(receives)(arrive.wait_recv)

        @pl.when(receives)
        def _():
            for cp in local:
                cp.wait()


class _Both:
    def __init__(self, first, second):
        self.plans = (first, second)
        self.inputs = first.inputs + second.inputs
        self.out_shapes = first.out_shapes + second.out_shapes
        self.sem_shapes = first.sem_shapes + second.sem_shapes

    def _split(self, src, dst, sems):
        a = self.plans[0]
        ni, no = len(a.inputs), len(a.out_shapes)
        return (src[:ni], dst[:no], sems[:3]), (src[ni:], dst[no:], sems[3:])

    def start(self, src, dst, sems):
        for plan, part in zip(self.plans, self._split(src, dst, sems)):
            plan.start(*part)

    def finish(self, src, dst, sems):
        for plan, part in zip(self.plans, self._split(src, dst, sems)):
            plan.finish(*part)


def _pcall(body, args, *, name, out_shape, grid, in_specs, out_specs, scratch_shapes=(), sem=None, comm=None):
    single = not isinstance(out_shape, (tuple, list))
    outs = [out_shape] if single else list(out_shape)
    ospecs = [out_specs] if single else list(out_specs)
    n_in, n_out, n_scr = len(args), len(outs), len(scratch_shapes)

    def pick(res):
        return res[0] if single else tuple(res[:n_out])

    if comm is None:
        res = pl.pallas_call(
            body, out_shape=outs, grid=grid, in_specs=list(in_specs), out_specs=ospecs,
            scratch_shapes=list(scratch_shapes), name=name, compiler_params=_params(sem),
        )(*args)
        return pick(res), ()

    nci, nco = len(comm.inputs), len(comm.out_shapes)

    def carrier(*refs):
        at = 0
        parts = []
        for size in (n_in, nci, n_out, nco, n_scr, len(comm.sem_shapes)):
            parts.append(refs[at:at + size])
            at += size
        ins, cins, o, couts, scr, sems = parts
        ids = [pl.program_id(a) for a in range(len(grid))]
        first = functools.reduce(jnp.logical_and, [i == 0 for i in ids])
        last = functools.reduce(jnp.logical_and, [i == g - 1 for i, g in zip(ids, grid)])

        @pl.when(first)
        def _():
            comm.start(cins, couts, sems)

        body(*ins, *o, *scr)

        @pl.when(last)
        def _():
            comm.finish(cins, couts, sems)

    hbm = pl.BlockSpec(memory_space=pltpu.HBM)
    res = pl.pallas_call(
        carrier, out_shape=outs + comm.out_shapes, grid=grid, in_specs=list(in_specs) + [hbm] * nci,
        out_specs=ospecs + [hbm] * nco, scratch_shapes=list(scratch_shapes) + comm.sem_shapes, name=name,
        compiler_params=_params(("arbitrary",) * len(grid)),
    )(*args, *comm.inputs)
    return pick(res), tuple(res[n_out:])


def _comm_call(name, comm):
    def body(*refs):
        nci, nco = len(comm.inputs), len(comm.out_shapes)
        cins, couts, sems = refs[:nci], refs[nci:nci + nco], refs[nci + nco:]
        comm.start(cins, couts, sems)
        comm.finish(cins, couts, sems)

    hbm = pl.BlockSpec(memory_space=pltpu.HBM)
    return pl.pallas_call(
        body, out_shape=comm.out_shapes, in_specs=[hbm] * len(comm.inputs), out_specs=[hbm] * len(comm.out_shapes),
        scratch_shapes=comm.sem_shapes, name=name,
    )(*comm.inputs)


def _matmul(name, kind, a, b, out_shape, grid, a_spec, b_spec, o_spec, acc_shape, res=None, res_spec=None,
            comm=None):
    nk = grid[-1]
    has_res = res is not None

    def body(*refs):
        a_ref, b_ref = refs[0], refs[1]
        o_ref = refs[2 + has_res]

        def prod():
            return _dot(a_ref[...], b_ref[...], kind)

        def finish(acc):
            if has_res:
                acc = acc + refs[2][...]
            o_ref[...] = acc.astype(o_ref.dtype)

        if nk == 1:
            finish(prod())
        else:
            acc_ref = refs[3 + has_res]
            k = pl.program_id(len(grid) - 1)

            @pl.when(k == 0)
            def _():
                acc_ref[...] = prod()

            @pl.when(k > 0)
            def _():
                acc_ref[...] += prod()

            @pl.when(k == nk - 1)
            def _():
                finish(acc_ref[...])

    in_specs = [a_spec, b_spec] + ([res_spec] if has_res else [])
    args = (a, b) + ((res,) if has_res else ())
    scratch = [pltpu.VMEM(acc_shape, F32)] if nk > 1 else []
    sem = ("parallel",) * (len(grid) - 1) + ("arbitrary",)
    out, landed = _pcall(body, args, name=name, out_shape=out_shape, grid=grid, in_specs=in_specs,
                         out_specs=o_spec, scratch_shapes=scratch, sem=sem, comm=comm)
    return out if comm is None else (out, landed)


def _mm_rows(name, a, w, out_dtype=F32, res=None, kind="nn", tm=MM_ROWS, comm=None):
    M, K = a.shape
    N = w.shape[1] if kind == "nn" else w.shape[0]
    tm = _tile(M, tm)
    res_spec = pl.BlockSpec((tm, N), lambda i, k: (i, 0)) if res is not None else None
    return _matmul(
        name, kind, a, w, jax.ShapeDtypeStruct((M, N), out_dtype), (M // tm, 1),
        pl.BlockSpec((tm, K), lambda i, k: (i, 0)), pl.BlockSpec(w.shape, lambda i, k: (0, 0)),
        pl.BlockSpec((tm, N), lambda i, k: (i, 0)), (tm, N), res, res_spec, comm)


def _mm_tn(name, a, b, out_dtype=F32, tk=MM_TOKENS, comm=None):
    T, M = a.shape
    N = b.shape[1]
    tk = _tile(T, tk)
    return _matmul(
        name, "tn", a, b, jax.ShapeDtypeStruct((M, N), out_dtype), (1, T // tk),
        pl.BlockSpec((tk, M), lambda i, k: (k, 0)), pl.BlockSpec((tk, N), lambda i, k: (k, 0)),
        pl.BlockSpec((M, N), lambda i, k: (0, 0)), (M, N), comm=comm)


def _mm_cols_slab(name, a, w_slabs, out_dtype=F32, tm=MM_ROWS, comm=None):
    M, K = a.shape
    J, _, n = w_slabs.shape
    tm = _tile(M, tm)
    return _matmul(
        name, "nn", a, w_slabs, jax.ShapeDtypeStruct((M, J * n), out_dtype), (J, M // tm, 1),
        pl.BlockSpec((tm, K), lambda j, i, k: (i, 0)), pl.BlockSpec((None, K, n), lambda j, i, k: (j, 0, 0)),
        pl.BlockSpec((tm, n), lambda j, i, k: (i, j)), (tm, n), comm=comm)


def _mm_cols_slab_t(name, a, w_slabs, out_dtype=F32, tm=MM_ROWS, comm=None):
    M = a.shape[0]
    J, K, n = w_slabs.shape
    tm = _tile(M, tm)
    return _matmul(
        name, "nt", a, w_slabs, jax.ShapeDtypeStruct((M, K), out_dtype), (M // tm, J),
        pl.BlockSpec((tm, n), lambda i, j: (i, j)), pl.BlockSpec((None, K, n), lambda i, j: (j, 0, 0)),
        pl.BlockSpec((tm, K), lambda i, j: (i, 0)), (tm, K), comm=comm)


def _mm_in_t(d_lo, d_hi, w_slabs, tm=MM_ROWS, comm=None):
    M = d_lo.shape[0]
    J, K, n = w_slabs.shape
    half = J // 2
    tm = _tile(M, tm)

    def body(lo_ref, hi_ref, w_ref, o_ref, acc_ref):
        j = pl.program_id(1)

        @pl.when(j == 0)
        def _():
            acc_ref[...] = _dot(lo_ref[...], w_ref[...], "nt")

        @pl.when((j > 0) & (j < half))
        def _():
            acc_ref[...] += _dot(lo_ref[...], w_ref[...], "nt")

        @pl.when(j >= half)
        def _():
            acc_ref[...] += _dot(hi_ref[...], w_ref[...], "nt")

        @pl.when(j == J - 1)
        def _():
            o_ref[...] = acc_ref[...]

    out, landed = _pcall(
        body, (d_lo, d_hi, w_slabs), name="mm_in_t", out_shape=jax.ShapeDtypeStruct((M, K), F32), grid=(M // tm, J),
        in_specs=[pl.BlockSpec((tm, n), lambda i, j: (i, jnp.minimum(j, half - 1))),
                  pl.BlockSpec((tm, n), lambda i, j: (i, jnp.maximum(j - half, 0))),
                  pl.BlockSpec((None, K, n), lambda i, j: (j, 0, 0))],
        out_specs=pl.BlockSpec((tm, K), lambda i, j: (i, 0)), scratch_shapes=[pltpu.VMEM((tm, K), F32)],
        sem=("parallel", "arbitrary"), comm=comm)
    return out if comm is None else (out, landed)


def _mm_tn_slab(name, a, b, n, out_dtype=F32, tk=MM_TOKENS, comm=None, part=(0, 1)):
    T, M = a.shape
    p, of = part
    M //= of
    J = b.shape[1] // n
    tk = _tile(T, tk)
    return _matmul(
        name, "tn", a, b, jax.ShapeDtypeStruct((J, M, n), out_dtype), (J, T // tk),
        pl.BlockSpec((tk, M), lambda j, k: (k, p)), pl.BlockSpec((tk, n), lambda j, k: (k, j)),
        pl.BlockSpec((None, M, n), lambda j, k: (j, 0, 0)), (M, n), comm=comm)


def _rms_fwd(name, x, g, tm=512):
    T, Dm = x.shape
    tm = _tile(T, tm)

    def body(x_ref, g_ref, h_ref):
        xv = x_ref[...]
        r = lax.rsqrt(jnp.mean(xv * xv, axis=-1, keepdims=True) + EPS)
        h_ref[...] = (xv * r * g_ref[...]).astype(h_ref.dtype)

    return pl.pallas_call(
        body, out_shape=jax.ShapeDtypeStruct((T, Dm), BF16), grid=(T // tm,),
        in_specs=[pl.BlockSpec((tm, Dm), lambda i: (i, 0)), pl.BlockSpec((1, Dm), lambda i: (0, 0))],
        out_specs=pl.BlockSpec((tm, Dm), lambda i: (i, 0)), name=name, compiler_params=_params(("parallel",)),
    )(x, g)


def _rms_bwd(name, x, g, dh, dres, tm=512):
    T, Dm = x.shape
    tm = _tile(T, tm)
    want_dx = dres is not None

    def body(*refs):
        if want_dx:
            x_ref, g_ref, dh_ref, dres_ref, dx_ref, dg_ref = refs
        else:
            x_ref, g_ref, dh_ref, dg_ref = refs
        xv = x_ref[...]
        r = lax.rsqrt(jnp.mean(xv * xv, axis=-1, keepdims=True) + EPS)
        xhat = xv * r
        dhv = dh_ref[...]

        @pl.when(pl.program_id(0) == 0)
        def _():
            dg_ref[...] = jnp.zeros_like(dg_ref)

        dg_ref[...] += jnp.sum(dhv * xhat, axis=0, keepdims=True)
        if want_dx:
            dxhat = dhv * g_ref[...]
            dx_ref[...] = dres_ref[...] + r * (dxhat - xhat * jnp.mean(dxhat * xhat, axis=-1, keepdims=True))

    row = pl.BlockSpec((tm, Dm), lambda i: (i, 0))
    vec = pl.BlockSpec((1, Dm), lambda i: (0, 0))
    if want_dx:
        return pl.pallas_call(
            body, out_shape=(jax.ShapeDtypeStruct((T, Dm), F32), jax.ShapeDtypeStruct((1, Dm), F32)),
            grid=(T // tm,), in_specs=[row, vec, row, row], out_specs=(row, vec), name=name,
            compiler_params=_params(("arbitrary",)),
        )(x, g, dh, dres)
    return pl.pallas_call(
        body, out_shape=jax.ShapeDtypeStruct((1, Dm), F32), grid=(T // tm,), in_specs=[row, vec, row],
        out_specs=vec, name=name, compiler_params=_params(("arbitrary",)),
    )(x, g, dh)


def _pool_rows(S):
    return _tile(S, 256)


def _pool_count(c0, rows, w):
    t = c0 + lax.broadcasted_iota(jnp.int32, (rows, 1), 0)
    return jnp.minimum(t + 1, w).astype(F32)


def _pool_fwd(proj, w_pool, scale, B, S):
    CH = _pool_rows(S)

    def body(hp_ref, wp_ref, sc_ref, o_ref, pad_ref):
        pad_ref[0:POOL_HALO, :] = jnp.zeros((POOL_HALO, POOL_WIDTH), F32)
        pad_ref[POOL_HALO:, :] = hp_ref[...]
        for gi, w in enumerate(POOL_WINDOWS):
            cols = slice(gi * POOL_GROUP_DIM, (gi + 1) * POOL_GROUP_DIM)
            for c in range(S // CH):
                base = POOL_HALO + c * CH
                acc = pad_ref[base:base + CH, cols]
                tok = acc
                for j in range(1, w):
                    acc = acc + pad_ref[base - j:base - j + CH, cols]
                pooled = acc / _pool_count(c * CH, CH, w) - tok
                z = _dot(pooled, wp_ref[gi])
                o_ref[c * CH:(c + 1) * CH, cols] = (z * sc_ref[:, cols]).astype(o_ref.dtype)

    return pl.pallas_call(
        body, out_shape=jax.ShapeDtypeStruct((B * S, POOL_WIDTH), BF16), grid=(B,),
        in_specs=[pl.BlockSpec((S, POOL_WIDTH), lambda b: (b, 0)),
                  pl.BlockSpec(w_pool.shape, lambda b: (0, 0, 0)),
                  pl.BlockSpec((1, POOL_WIDTH), lambda b: (0, 0))],
        out_specs=pl.BlockSpec((S, POOL_WIDTH), lambda b: (b, 0)),
        scratch_shapes=[pltpu.VMEM((S + POOL_HALO, POOL_WIDTH), F32)],
        name="pool_fwd", compiler_params=_params(("parallel",)),
    )(proj, w_pool, scale)


def _pool_bwd(proj, d_ypre, w_pool, scale, B, S):
    CH = _pool_rows(S)

    def body(hp_ref, dy_ref, wp_ref, sc_ref, dhp_ref, dwp_ref, dsc_ref, pad_ref, sc_pad_ref, dp_ref):
        @pl.when(pl.program_id(0) == 0)
        def _():
            dwp_ref[...] = jnp.zeros_like(dwp_ref)
            dsc_ref[...] = jnp.zeros_like(dsc_ref)

        pad_ref[0:POOL_HALO, :] = jnp.zeros((POOL_HALO, POOL_WIDTH), F32)
        pad_ref[POOL_HALO:, :] = hp_ref[...]
        sc_pad_ref[S:, :] = jnp.zeros((POOL_HALO, POOL_WIDTH), F32)
        for gi, w in enumerate(POOL_WINDOWS):
            cols = slice(gi * POOL_GROUP_DIM, (gi + 1) * POOL_GROUP_DIM)
            for c in range(S // CH):
                base = POOL_HALO + c * CH
                rows = slice(c * CH, (c + 1) * CH)
                acc = pad_ref[base:base + CH, cols]
                tok = acc
                for j in range(1, w):
                    acc = acc + pad_ref[base - j:base - j + CH, cols]
                cnt = _pool_count(c * CH, CH, w)
                pooled = acc / cnt - tok
                z = _dot(pooled, wp_ref[gi])
                dy = dy_ref[rows, cols]
                dsc_ref[:, cols] += jnp.sum(dy * z, axis=0, keepdims=True)
                dz = dy * sc_ref[:, cols]
                dwp_ref[gi] += _dot(pooled, dz, "tn")
                dpool = _dot(dz, wp_ref[gi], "nt")
                dp_ref[rows, cols] = dpool
                sc_pad_ref[rows, cols] = dpool / cnt
            for c in range(S // CH):
                rows = slice(c * CH, (c + 1) * CH)
                acc = sc_pad_ref[rows, cols]
                for j in range(1, w):
                    acc = acc + sc_pad_ref[c * CH + j:c * CH + j + CH, cols]
                dhp_ref[rows, cols] = (acc - dp_ref[rows, cols]).astype(dhp_ref.dtype)

    seq = pl.BlockSpec((S, POOL_WIDTH), lambda b: (b, 0))
    return pl.pallas_call(
        body,
        out_shape=(jax.ShapeDtypeStruct((B * S, POOL_WIDTH), BF16),
                   jax.ShapeDtypeStruct(w_pool.shape, F32), jax.ShapeDtypeStruct((1, POOL_WIDTH), F32)),
        grid=(B,),
        in_specs=[seq, seq, pl.BlockSpec(w_pool.shape, lambda b: (0, 0, 0)),
                  pl.BlockSpec((1, POOL_WIDTH), lambda b: (0, 0))],
        out_specs=(seq, pl.BlockSpec(w_pool.shape, lambda b: (0, 0, 0)),
                   pl.BlockSpec((1, POOL_WIDTH), lambda b: (0, 0))),
        scratch_shapes=[pltpu.VMEM((S + POOL_HALO, POOL_WIDTH), F32),
                        pltpu.VMEM((S + POOL_HALO, POOL_WIDTH), F32),
                        pltpu.VMEM((S, POOL_WIDTH), F32)],
        name="pool_bwd", compiler_params=_params(("arbitrary",)),
    )(proj, d_ypre, w_pool, scale)


def _ret_tables(S):
    half = RET_QK_DIM // 2
    inv = ROPE_BASE ** (-jnp.arange(half, dtype=F32) / half)
    ang = jnp.arange(S, dtype=F32)[:, None] * inv[None, :]
    cos, sin = jnp.cos(ang), jnp.sin(ang)
    cos_full = jnp.concatenate([cos, cos], axis=-1)
    sin_signed = jnp.concatenate([-sin, sin], axis=-1)
    C = RET_CHUNK
    lg = jnp.log1p(-jnp.exp2(-5.0 - jnp.arange(RET_HEADS, dtype=F32)))[:, None, None]
    idx = jnp.arange(C, dtype=F32)
    rel = idx[:, None] - idx[None, :]
    decay = jnp.where(rel >= 0, jnp.exp(jnp.maximum(rel, 0.0) * lg), 0.0)
    q_decay = jnp.broadcast_to(jnp.exp((idx + 1.0)[None, :, None] * lg), (RET_HEADS, C, RET_QK_DIM))
    k_decay = jnp.broadcast_to(jnp.exp((C - 1.0 - idx)[None, :, None] * lg), (RET_HEADS, C, RET_QK_DIM))
    c_decay = jnp.broadcast_to(jnp.exp(C * lg), (RET_HEADS, 1, RET_V_DIM))
    return cos_full, sin_signed, decay, q_decay, k_decay, c_decay


def _rope(x, cos_full, sin_signed):
    return x * cos_full + pltpu.roll(x, RET_QK_DIM // 2, axis=1) * sin_signed


def _rope_t(dy, cos_full, sin_signed):
    return dy * cos_full + pltpu.roll(dy * sin_signed, RET_QK_DIM // 2, axis=1)


RET_COLS = 512


def _ret_specs(N, chunk_of):
    C = RET_CHUNK

    def rows(width, col=0):
        return pl.BlockSpec((C, width), lambda b, i: (b * N + chunk_of(i), col))

    def whole(shape):
        return pl.BlockSpec(shape, lambda b, i: (0,) * len(shape))

    wide = RET_HEADS * RET_V_DIM
    return dict(
        q=rows(RET_COLS, COL_Q // RET_COLS), k=rows(RET_COLS, COL_K // RET_COLS),
        v=[rows(RET_COLS, COL_V // RET_COLS + j) for j in range(2)],
        gr=[rows(RET_COLS, COL_GR // RET_COLS + j) for j in range(2)],
        table=pl.BlockSpec((C, RET_QK_DIM), lambda b, i: (chunk_of(i), 0)),
        decay=whole((RET_HEADS, C, C)), qd=whole((RET_HEADS, C, RET_QK_DIM)), kd=whole((RET_HEADS, C, RET_QK_DIM)),
        cd=whole((RET_HEADS, 1, RET_V_DIM)), vec=whole((1, wide)), qk_rows=rows(RET_COLS), v_rows=rows(wide),
        state=pl.BlockSpec((None, None, RET_HEADS, RET_QK_DIM, RET_V_DIM), lambda b, i: (b, chunk_of(i), 0, 0, 0)))


def _head_cols(h):
    pair = slice((h % 2) * RET_V_DIM, (h % 2 + 1) * RET_V_DIM)
    return slice(h * RET_QK_DIM, (h + 1) * RET_QK_DIM), h // 2, pair, slice(h * RET_V_DIM, (h + 1) * RET_V_DIM)


def _group_norm(o):
    mu = jnp.mean(o, axis=-1, keepdims=True)
    oc = o - mu
    rstd = lax.rsqrt(jnp.mean(oc * oc, axis=-1, keepdims=True) + EPS)
    return oc * rstd, rstd


def _ret_fwd(proj, g_ret, b_ret, tables, B, S, comm=None):
    N = S // RET_CHUNK
    cos_t, sin_t, decay, q_decay, k_decay, c_decay = tables
    sp = _ret_specs(N, lambda i: i)

    def body(q_ref, k_ref, v0_ref, v1_ref, gr0_ref, gr1_ref, cos_ref, sin_ref, dec_ref, qd_ref, kd_ref, cd_ref,
             g_ref, b_ref, y_ref, rs_ref, r_ref):
        @pl.when(pl.program_id(1) == 0)
        def _():
            r_ref[...] = jnp.zeros_like(r_ref)

        cs, sn = cos_ref[...], sin_ref[...]
        for h in range(RET_HEADS):
            qk, j, pair, wide = _head_cols(h)
            q = _rope(q_ref[:, qk], cs, sn)
            k = _rope(k_ref[:, qk], cs, sn) * (RET_QK_DIM ** -0.5)
            v = (v0_ref, v1_ref)[j][:, pair]
            R = r_ref[h]
            rs_ref[h] = R
            s = _dot(q, k, "nt") * dec_ref[h]
            o = _dot(s, v) + _dot(q * qd_ref[h], R)
            r_ref[h] = cd_ref[h] * R + _dot(k * kd_ref[h], v, "tn")
            on, _ = _group_norm(o)
            gr = (gr0_ref, gr1_ref)[j][:, pair]
            y_ref[:, wide] = (gr * jax.nn.sigmoid(gr) * (on * g_ref[:, wide] + b_ref[:, wide])).astype(y_ref.dtype)

    state = jax.ShapeDtypeStruct((B, N, RET_HEADS, RET_QK_DIM, RET_V_DIM), F32)
    return _pcall(
        body, (proj,) * 6 + (cos_t, sin_t, decay, q_decay, k_decay, c_decay, g_ret, b_ret),
        name="ret_fwd", out_shape=(jax.ShapeDtypeStruct((B * S, RET_HEADS * RET_V_DIM), BF16), state), grid=(B, N),
        in_specs=[sp["q"], sp["k"], *sp["v"], *sp["gr"], sp["table"], sp["table"], sp["decay"], sp["qd"],
                  sp["kd"], sp["cd"], sp["vec"], sp["vec"]],
        out_specs=(sp["v_rows"], sp["state"]),
        scratch_shapes=[pltpu.VMEM((RET_HEADS, RET_QK_DIM, RET_V_DIM), F32)],
        sem=("parallel", "arbitrary"), comm=comm)


def _ret_bwd(proj, states, d_yr, g_ret, b_ret, tables, B, S, comm=None):
    N = S // RET_CHUNK
    cos_t, sin_t, decay, q_decay, k_decay, c_decay = tables
    sp = _ret_specs(N, lambda i: N - 1 - i)
    qk_scale = RET_QK_DIM ** -0.5

    def body(q_ref, k_ref, v0_ref, v1_ref, gr0_ref, gr1_ref, dy_ref, rs_ref, cos_ref, sin_ref, dec_ref, qd_ref,
             kd_ref, cd_ref, g_ref, b_ref, dq_ref, dk_ref, dv_ref, dgr_ref, dg_ref, db_ref, dr_ref):
        @pl.when((pl.program_id(0) == 0) & (pl.program_id(1) == 0))
        def _():
            dg_ref[...] = jnp.zeros_like(dg_ref)
            db_ref[...] = jnp.zeros_like(db_ref)

        @pl.when(pl.program_id(1) == 0)
        def _():
            dr_ref[...] = jnp.zeros_like(dr_ref)

        cs, sn = cos_ref[...], sin_ref[...]
        for h in range(RET_HEADS):
            qk, j, pair, wide = _head_cols(h)
            q = _rope(q_ref[:, qk], cs, sn)
            k = _rope(k_ref[:, qk], cs, sn) * qk_scale
            v = (v0_ref, v1_ref)[j][:, pair]
            R, dR = rs_ref[h], dr_ref[h]
            dec, qd, kd = dec_ref[h], qd_ref[h], kd_ref[h]
            s = _dot(q, k, "nt") * dec
            o = _dot(s, v) + _dot(q * qd, R)
            on, rstd = _group_norm(o)
            g = g_ref[:, wide]
            oaff = on * g + b_ref[:, wide]
            gr = (gr0_ref, gr1_ref)[j][:, pair]
            sg = jax.nn.sigmoid(gr)
            dy = dy_ref[:, wide]
            dgr_ref[:, wide] = (dy * oaff * (sg * (1.0 + gr * (1.0 - sg)))).astype(dgr_ref.dtype)
            doaff = dy * (gr * sg)
            dg_ref[:, wide] += jnp.sum(doaff * on, axis=0, keepdims=True)
            db_ref[:, wide] += jnp.sum(doaff, axis=0, keepdims=True)
            don = doaff * g
            do = rstd * (don - jnp.mean(don, axis=-1, keepdims=True)
                         - on * jnp.mean(don * on, axis=-1, keepdims=True))
            ds = _dot(do, v, "nt") * dec
            dq = _dot(ds, k) + qd * _dot(do, R, "nt")
            dk = _dot(ds, q, "tn") + kd * _dot(v, dR, "nt")
            dv_ref[:, wide] = (_dot(s, do, "tn") + _dot(k * kd, dR)).astype(dv_ref.dtype)
            dr_ref[h] = cd_ref[h] * dR + _dot(q * qd, do, "tn")
            dq_ref[:, qk] = _rope_t(dq, cs, sn).astype(dq_ref.dtype)
            dk_ref[:, qk] = _rope_t(dk * qk_scale, cs, sn).astype(dk_ref.dtype)

    T = B * S
    qk_shape = jax.ShapeDtypeStruct((T, RET_HEADS * RET_QK_DIM), BF16)
    v_shape = jax.ShapeDtypeStruct((T, RET_HEADS * RET_V_DIM), BF16)
    vec_shape = jax.ShapeDtypeStruct((1, RET_HEADS * RET_V_DIM), F32)
    return _pcall(
        body, (proj,) * 6 + (d_yr, states, cos_t, sin_t, decay, q_decay, k_decay, c_decay, g_ret, b_ret),
        name="ret_bwd", out_shape=(qk_shape, qk_shape, v_shape, v_shape, vec_shape, vec_shape), grid=(B, N),
        in_specs=[sp["q"], sp["k"], *sp["v"], *sp["gr"], sp["v_rows"], sp["state"], sp["table"], sp["table"],
                  sp["decay"], sp["qd"], sp["kd"], sp["cd"], sp["vec"], sp["vec"]],
        out_specs=(sp["qk_rows"], sp["qk_rows"], sp["v_rows"], sp["v_rows"], sp["vec"], sp["vec"]),
        scratch_shapes=[pltpu.VMEM((RET_HEADS, RET_QK_DIM, RET_V_DIM), F32)],
        sem=("arbitrary", "arbitrary"), comm=comm)


def _xa_rows(S):
    return _tile(S, 256)


def _xa_specs(S, M):
    q = pl.BlockSpec((S, XA_HEAD_DIM), lambda b, h: (b, COL_QX // XA_HEAD_DIM + h))
    k = pl.BlockSpec((M, XA_HEAD_DIM), lambda b, h: (b, h))
    v = pl.BlockSpec((M, XA_HEAD_DIM), lambda b, h: (b, XA_HEADS + h))
    o = pl.BlockSpec((S, XA_HEAD_DIM), lambda b, h: (b, h))
    return q, k, v, o


def _softmax_rows(s):
    e = jnp.exp(s - jnp.max(s, axis=-1, keepdims=True))
    return e / jnp.sum(e, axis=-1, keepdims=True)


def _xa_fwd(proj, kv, B, S, M):
    CH = _xa_rows(S)
    q_spec, k_spec, v_spec, o_spec = _xa_specs(S, M)

    def body(q_ref, k_ref, v_ref, o_ref):
        def chunk(i, carry):
            rows = pl.ds(pl.multiple_of(i * CH, CH), CH)
            p = _softmax_rows(_dot(q_ref[rows, :], k_ref[...], "nt") * (XA_HEAD_DIM ** -0.5))
            o_ref[rows, :] = _dot(p, v_ref[...]).astype(o_ref.dtype)
            return carry

        lax.fori_loop(0, S // CH, chunk, 0)

    return pl.pallas_call(
        body, out_shape=jax.ShapeDtypeStruct((B * S, XA_WIDTH), BF16), grid=(B, XA_HEADS),
        in_specs=[q_spec, k_spec, v_spec], out_specs=o_spec, name="xattn_fwd",
        compiler_params=_params(("parallel", "parallel")),
    )(proj, kv, kv)


def _xa_bwd(proj, kv, d_o, B, S, M, comm=None):
    CH = _xa_rows(S)
    q_spec, k_spec, v_spec, o_spec = _xa_specs(S, M)
    scale = XA_HEAD_DIM ** -0.5

    def body(q_ref, k_ref, v_ref, do_ref, dq_ref, dk_ref, dv_ref):
        dk_ref[...] = jnp.zeros_like(dk_ref)
        dv_ref[...] = jnp.zeros_like(dv_ref)

        def chunk(i, carry):
            rows = pl.ds(pl.multiple_of(i * CH, CH), CH)
            q, do = q_ref[rows, :], do_ref[rows, :]
            p = _softmax_rows(_dot(q, k_ref[...], "nt") * scale)
            dp = _dot(do, v_ref[...], "nt")
            ds = p * (dp - jnp.sum(dp * p, axis=-1, keepdims=True)) * scale
            dq_ref[rows, :] = _dot(ds, k_ref[...]).astype(dq_ref.dtype)
            dk_ref[...] += _dot(ds, q, "tn")
            dv_ref[...] += _dot(p, do, "tn")
            return carry

        lax.fori_loop(0, S // CH, chunk, 0)

    kv_out = pl.BlockSpec((M, XA_HEAD_DIM), lambda b, h: (b, h))
    return _pcall(
        body, (proj, kv, kv, d_o), name="xattn_bwd",
        out_shape=(jax.ShapeDtypeStruct((B * S, XA_WIDTH), BF16), jax.ShapeDtypeStruct((B * M, XA_WIDTH), F32),
                   jax.ShapeDtypeStruct((B * M, XA_WIDTH), F32)),
        grid=(B, XA_HEADS), in_specs=[q_spec, k_spec, v_spec, o_spec], out_specs=(o_spec, kv_out, kv_out),
        sem=("parallel", "parallel"), comm=comm)


def _gate_specs(tm):
    n = COL_GL // D_MODEL
    return [pl.BlockSpec((tm, D_MODEL), lambda i, j=j: (i, n + j)) for j in range(3)]


def _merge_fwd(proj, ys, tm=256):
    T = proj.shape[0]
    tm = _tile(T, tm)
    row = pl.BlockSpec((tm, D_MODEL), lambda i: (i, 0))

    def body(g0, g1, g2, y0, y1, y2, o_ref):
        acc = jax.nn.sigmoid(g0[...]) * y0[...]
        acc = acc + jax.nn.sigmoid(g1[...]) * y1[...]
        acc = acc + jax.nn.sigmoid(g2[...]) * y2[...]
        o_ref[...] = acc.astype(o_ref.dtype)

    return pl.pallas_call(
        body, out_shape=jax.ShapeDtypeStruct((T, D_MODEL), BF16), grid=(T // tm,),
        in_specs=_gate_specs(tm) + [row] * 3, out_specs=row, name="merge_fwd",
        compiler_params=_params(("parallel",)),
    )(proj, proj, proj, *ys)


def _merge_bwd(proj, ys, d_merged, tm=256, comm=None):
    T = proj.shape[0]
    tm = _tile(T, tm)
    row = pl.BlockSpec((tm, D_MODEL), lambda i: (i, 0))

    def body(g0, g1, g2, y0, y1, y2, dm_ref, dgl_ref, d0, d1, d2):
        dm = dm_ref[...]
        for j, (g_ref, y_ref, d_ref) in enumerate(((g0, y0, d0), (g1, y1, d1), (g2, y2, d2))):
            sg = jax.nn.sigmoid(g_ref[...])
            d_ref[...] = (dm * sg).astype(d_ref.dtype)
            dgl_ref[:, j * D_MODEL:(j + 1) * D_MODEL] = (dm * y_ref[...] * sg * (1.0 - sg)).astype(dgl_ref.dtype)

    dy = jax.ShapeDtypeStruct((T, D_MODEL), BF16)
    return _pcall(
        body, (proj, proj, proj, *ys, d_merged), name="merge_bwd",
        out_shape=(jax.ShapeDtypeStruct((T, 3 * D_MODEL), BF16), dy, dy, dy), grid=(T // tm,),
        in_specs=_gate_specs(tm) + [row] * 4,
        out_specs=(pl.BlockSpec((tm, 3 * D_MODEL), lambda i: (i, 0)), row, row, row),
        sem=("parallel",), comm=comm)


def _gelu(x):
    return 0.5 * x * (1.0 + jnp.tanh(GELU_C * (x + GELU_A * x * x * x)))


def _gelu_grad(x):
    t = jnp.tanh(GELU_C * (x + GELU_A * x * x * x))
    return 0.5 * (1.0 + t) + 0.5 * x * (1.0 - t * t) * GELU_C * (1.0 + 3.0 * GELU_A * x * x)


def _shift_down(x, prev, n):
    rows = x.shape[0]
    r = lax.broadcasted_iota(jnp.int32, (8, 1), 0)
    rolled = pltpu.roll(x, n, axis=0)
    head = rolled[0:8]
    for j in range(n):
        head = jnp.where(r == j, prev[8 - n + j:8 - n + j + 1, :], head)
    return head if rows == 8 else jnp.concatenate([head, rolled[8:]], axis=0)


def _shift_up(x, nxt, n):
    rows = x.shape[0]
    r = lax.broadcasted_iota(jnp.int32, (8, 1), 0)
    rolled = pltpu.roll(x, rows - n, axis=0)
    tail = rolled[rows - 8:]
    for j in range(n):
        tail = jnp.where(r == 8 - n + j, nxt[j:j + 1, :], tail)
    return jnp.concatenate([rolled[:rows - 8], tail], axis=0)


def _conv(a, prev, cw, cb):
    return _shift_down(a, prev, 2) * cw[0:1, :] + _shift_down(a, prev, 1) * cw[1:2, :] + a * cw[2:3, :] + cb


def _glu_fwd(up, cw, cb, S, tm=256):
    T = up.shape[2]
    tm = _tile(S, tm)
    per_seq = S // tm

    def body(ab_ref, prev_ref, cw_ref, cb_ref, u_ref):
        i = pl.program_id(1)
        prev = jnp.where(i % per_seq == 0, 0.0, prev_ref[...])
        ac = _conv(ab_ref[0], prev, cw_ref[...], cb_ref[...])
        u_ref[...] = (_gelu(ac) * ab_ref[1]).astype(u_ref.dtype)

    return pl.pallas_call(
        body, out_shape=jax.ShapeDtypeStruct((FFN_SLABS, T, UP_SHARD), BF16), grid=(FFN_SLABS, T // tm),
        in_specs=[pl.BlockSpec((2, None, tm, UP_SHARD), lambda d, i: (0, d, i, 0)),
                  pl.BlockSpec((None, None, 8, UP_SHARD), lambda d, i: (0, d, jnp.maximum(i * (tm // 8) - 1, 0), 0)),
                  pl.BlockSpec((None, 3, UP_SHARD), lambda d, i: (d, 0, 0)),
                  pl.BlockSpec((None, 1, UP_SHARD), lambda d, i: (d, 0, 0))],
        out_specs=pl.BlockSpec((None, tm, UP_SHARD), lambda d, i: (d, i, 0)), name="glu_fwd",
        compiler_params=_params(("parallel", "parallel")),
    )(up, up, cw, cb)


def _glu_bwd(up, d_u, cw, cb, S, tm=256, comm=None):
    T = up.shape[2]
    tm = _tile(S, tm)
    per_seq = S // tm
    n_tiles = T // tm
    last8 = tm // 8

    def body(ab_ref, prev_ref, abn_ref, du_ref, dun_ref, cw_ref, cb_ref, dup_ref, dcw_ref, dcb_ref):
        i = pl.program_id(1)

        @pl.when(i == 0)
        def _():
            dcw_ref[...] = jnp.zeros_like(dcw_ref)
            dcb_ref[...] = jnp.zeros_like(dcb_ref)

        cw, cb = cw_ref[...], cb_ref[...]
        a, b = ab_ref[0], ab_ref[1]
        prev = jnp.where(i % per_seq == 0, 0.0, prev_ref[...])
        a2, a1 = _shift_down(a, prev, 2), _shift_down(a, prev, 1)
        ac = a2 * cw[0:1, :] + a1 * cw[1:2, :] + a * cw[2:3, :] + cb
        du = du_ref[...]
        dup_ref[1] = (du * _gelu(ac)).astype(dup_ref.dtype)
        dac = du * b * _gelu_grad(ac)
        dcb_ref[...] += jnp.sum(dac, axis=0, keepdims=True)
        dcw_ref[0:1, :] += jnp.sum(dac * a2, axis=0, keepdims=True)
        dcw_ref[1:2, :] += jnp.sum(dac * a1, axis=0, keepdims=True)
        dcw_ref[2:3, :] += jnp.sum(dac * a, axis=0, keepdims=True)
        an = abn_ref[0]
        acn = _conv(an, a[tm - 8:, :], cw, cb)
        dacn = jnp.where(i % per_seq == per_seq - 1, 0.0, dun_ref[...] * abn_ref[1] * _gelu_grad(acn))
        da = dac * cw[2:3, :] + _shift_up(dac, dacn, 1) * cw[1:2, :] + _shift_up(dac, dacn, 2) * cw[0:1, :]
        dup_ref[0] = da.astype(dup_ref.dtype)

    def nxt(i):
        return jnp.minimum((i + 1) * last8, T // 8 - 1)

    return _pcall(
        body, (up, up, up, d_u, d_u, cw, cb), name="glu_bwd",
        out_shape=(jax.ShapeDtypeStruct((2, FFN_SLABS, T, UP_SHARD), BF16),
                   jax.ShapeDtypeStruct((FFN_SLABS, 3, UP_SHARD), F32),
                   jax.ShapeDtypeStruct((FFN_SLABS, 1, UP_SHARD), F32)),
        grid=(FFN_SLABS, n_tiles),
        in_specs=[pl.BlockSpec((2, None, tm, UP_SHARD), lambda d, i: (0, d, i, 0)),
                  pl.BlockSpec((None, None, 8, UP_SHARD), lambda d, i: (0, d, jnp.maximum(i * last8 - 1, 0), 0)),
                  pl.BlockSpec((2, None, 8, UP_SHARD), lambda d, i: (0, d, nxt(i), 0)),
                  pl.BlockSpec((None, tm, UP_SHARD), lambda d, i: (d, i, 0)),
                  pl.BlockSpec((None, 8, UP_SHARD), lambda d, i: (d, nxt(i), 0)),
                  pl.BlockSpec((None, 3, UP_SHARD), lambda d, i: (d, 0, 0)),
                  pl.BlockSpec((None, 1, UP_SHARD), lambda d, i: (d, 0, 0))],
        out_specs=(pl.BlockSpec((2, None, tm, UP_SHARD), lambda d, i: (0, d, i, 0)),
                   pl.BlockSpec((None, 3, UP_SHARD), lambda d, i: (d, 0, 0)),
                   pl.BlockSpec((None, 1, UP_SHARD), lambda d, i: (d, 0, 0))),
        sem=("parallel", "arbitrary"), comm=comm)


def _mm_up(h2, w_up, tm=MM_ROWS):
    T, K = h2.shape
    tm = _tile(T, tm)
    return _matmul(
        "mm_up", "nn", h2, w_up, jax.ShapeDtypeStruct((N_DEV, T, UP_SHARD), F32), (N_DEV, T // tm, 1),
        pl.BlockSpec((tm, K), lambda j, i, k: (i, 0)), pl.BlockSpec((None, K, UP_SHARD), lambda j, i, k: (j, 0, 0)),
        pl.BlockSpec((None, tm, UP_SHARD), lambda j, i, k: (j, i, 0)), (tm, UP_SHARD))


def _mm_down(u, w_down, res, tm=MM_ROWS):
    J, T, n = u.shape
    tm = _tile(T, tm)
    row = pl.BlockSpec((tm, D_MODEL), lambda i, d: (i, 0))
    return _matmul(
        "mm_down", "nn", u, w_down, jax.ShapeDtypeStruct((T, D_MODEL), F32), (T // tm, J),
        pl.BlockSpec((None, tm, n), lambda i, d: (d, i, 0)), pl.BlockSpec((None, n, D_MODEL), lambda i, d: (d, 0, 0)),
        row, (tm, D_MODEL), res, row)


def _mm_down_t(dx, w_down, tm=MM_ROWS):
    T = dx.shape[0]
    J, n, _ = w_down.shape
    tm = _tile(T, tm)
    return _matmul(
        "mm_down_t", "nt", dx, w_down, jax.ShapeDtypeStruct((J, T, n), F32), (J, T // tm, 1),
        pl.BlockSpec((tm, D_MODEL), lambda d, i, k: (i, 0)), pl.BlockSpec((None, n, D_MODEL), lambda d, i, k: (d, 0, 0)),
        pl.BlockSpec((None, tm, n), lambda d, i, k: (d, i, 0)), (tm, n))


def _mm_dw_down(u, dx, tk=MM_TOKENS):
    J, T, n = u.shape
    tk = _tile(T, tk)
    return _matmul(
        "mm_dw_down", "tn", u, dx, jax.ShapeDtypeStruct((J, n, D_MODEL), BF16), (J, T // tk),
        pl.BlockSpec((None, tk, n), lambda d, k: (d, k, 0)), pl.BlockSpec((tk, D_MODEL), lambda d, k: (k, 0)),
        pl.BlockSpec((None, n, D_MODEL), lambda d, k: (d, 0, 0)), (n, D_MODEL))


def _mm_dw_up(name, h2, d_up, part, tk=MM_TOKENS, comm=None):
    T, K = h2.shape
    p, of = part
    K //= of
    tk = _tile(T, tk)
    return _matmul(
        name, "tn", h2, d_up, jax.ShapeDtypeStruct((N_DEV, K, UP_SHARD), BF16), (N_DEV, T // tk),
        pl.BlockSpec((tk, K), lambda j, k: (k, p)), pl.BlockSpec((None, tk, UP_SHARD), lambda j, k: (j, k, 0)),
        pl.BlockSpec((None, K, UP_SHARD), lambda j, k: (j, 0, 0)), (K, UP_SHARD), comm=comm)


def _mm_up_t(d_up, w_up, tm=MM_ROWS, comm=None):
    J, T, n = d_up.shape
    K = w_up.shape[1]
    tm = _tile(T, tm)
    return _matmul(
        "mm_up_t", "nt", d_up, w_up, jax.ShapeDtypeStruct((T, K), F32), (T // tm, J),
        pl.BlockSpec((None, tm, n), lambda i, j: (j, i, 0)), pl.BlockSpec((None, K, n), lambda i, j: (j, 0, 0)),
        pl.BlockSpec((tm, K), lambda i, j: (i, 0)), (tm, K), comm=comm)


def _loss_head(x2, target, g_final, tm=512):
    T, Dm = x2.shape
    tm = _tile(T, tm)

    def body(x_ref, t_ref, g_ref, dx_ref, dg_ref, loss_ref):
        @pl.when(pl.program_id(0) == 0)
        def _():
            dg_ref[...] = jnp.zeros_like(dg_ref)
            loss_ref[...] = jnp.zeros_like(loss_ref)

        xv = x_ref[...]
        r = lax.rsqrt(jnp.mean(xv * xv, axis=-1, keepdims=True) + EPS)
        xhat = xv * r
        err = xhat * g_ref[...] - t_ref[...]
        loss_ref[...] += (0.5 / Dm) * jnp.sum(err * err)
        dy = err * (1.0 / Dm)
        dg_ref[...] += jnp.sum(dy * xhat, axis=0, keepdims=True)
        dxhat = dy * g_ref[...]
        dx_ref[...] = r * (dxhat - xhat * jnp.mean(dxhat * xhat, axis=-1, keepdims=True))

    row = pl.BlockSpec((tm, Dm), lambda i: (i, 0))
    vec = pl.BlockSpec((1, Dm), lambda i: (0, 0))
    return pl.pallas_call(
        body,
        out_shape=(jax.ShapeDtypeStruct((T, Dm), F32), jax.ShapeDtypeStruct((1, Dm), F32),
                   jax.ShapeDtypeStruct((1, Dm), F32)),
        grid=(T // tm,), in_specs=[row, row, vec], out_specs=(row, vec, vec), name="loss_head",
        compiler_params=_params(("arbitrary",)),
    )(x2, target, g_final)


def _cast_shards(shards):
    def body(*refs):
        n = len(refs) // 2
        for src, dst in zip(refs[:n], refs[n:]):
            dst[...] = src[...].astype(dst.dtype)

    return pl.pallas_call(
        body, out_shape=[jax.ShapeDtypeStruct(s.shape, BF16) for s in shards], name="cast_shards",
        compiler_params=pltpu.CompilerParams(vmem_limit_bytes=VMEM_LIMIT),
    )(*shards)


def _adamw(w, g, m, v):
    m = ADAM_B1 * m + (1.0 - ADAM_B1) * g
    v = ADAM_B2 * v + (1.0 - ADAM_B2) * (g * g)
    m_hat = m / (1.0 - ADAM_B1 ** ADAM_STEP)
    v_hat = v / (1.0 - ADAM_B2 ** ADAM_STEP)
    delta = -ADAM_LR * (m_hat / (jnp.sqrt(v_hat) + ADAM_EPS) + ADAM_WD * w)
    return delta, m, v


def _sum_parts(p_ref):
    g = p_ref[0].astype(F32)
    for d in range(1, N_DEV):
        g = g + p_ref[d].astype(F32)
    return g


def _reduce_adam(name, parts, w, m, v, tr=128):
    R, Cn = w.shape
    by_rows = sum(p.shape[1] for p in parts) == R and len(parts) > 1
    tr = math.gcd(tr, *[p.shape[1] for p in parts])
    n_tiles = [p.shape[1] // tr for p in parts]
    first = [sum(n_tiles[:j]) for j in range(len(parts))] if by_rows else [0] * len(parts)

    def body(*refs):
        p_refs = refs[:len(parts)]
        w_ref, m_ref, v_ref, g_out, d_out, m_out, v_out = refs[len(parts):]

        def update(p_ref):
            g = _sum_parts(p_ref)
            delta, m_new, v_new = _adamw(w_ref[...], g, m_ref[...], v_ref[...])
            g_out[...] = g
            d_out[...] = delta
            m_out[...] = m_new
            v_out[...] = v_new

        if len(parts) == 1:
            update(p_refs[0])
        elif by_rows:
            i = pl.program_id(0)
            for p_ref, t0, n in zip(p_refs, first, n_tiles):
                pl.when((i >= t0) & (i < t0 + n))(functools.partial(update, p_ref))
        else:
            c = lax.axis_index("c")
            for side, p_ref in enumerate(p_refs):
                pl.when(c == side)(functools.partial(update, p_ref))

    def part_spec(t0, n):
        return pl.BlockSpec((N_DEV, tr, Cn), lambda i: (0, jnp.clip(i - t0, 0, n - 1), 0))

    row = pl.BlockSpec((tr, Cn), lambda i: (i, 0))
    shape = jax.ShapeDtypeStruct((R, Cn), F32)
    return pl.pallas_call(
        body, out_shape=(shape,) * 4, grid=(R // tr,),
        in_specs=[part_spec(t0, n) for t0, n in zip(first, n_tiles)] + [row, row, row],
        out_specs=(row,) * 4, name=name, compiler_params=_params(("parallel",)),
    )(*parts, w, m, v)


def _small_adam(name, gathered, params):
    n_g, n_p = len(gathered), len(params)

    def body(*refs):
        g_refs = refs[:n_g]
        wmv = refs[n_g:n_g + 3 * n_p]
        sums = refs[n_g + 3 * n_p:2 * n_g + 3 * n_p]
        upd = refs[2 * n_g + 3 * n_p:]
        for j in range(n_g):
            g = _sum_parts(g_refs[j])
            sums[j][...] = g
            if j < n_p:
                w_ref, m_ref, v_ref = wmv[3 * j:3 * j + 3]
                delta, m_new, v_new = _adamw(w_ref[...], g, m_ref[...], v_ref[...])
                upd[3 * j][...] = delta
                upd[3 * j + 1][...] = m_new
                upd[3 * j + 2][...] = v_new

    flat = [a for wmv in params for a in wmv]
    out_shape = [jax.ShapeDtypeStruct(g.shape[1:], F32) for g in gathered]
    out_shape += [jax.ShapeDtypeStruct(a.shape, F32) for a in flat]
    res = pl.pallas_call(body, out_shape=out_shape, name=name)(*gathered, *flat)
    return res[:n_g], [tuple(res[n_g + 3 * j:n_g + 3 * j + 3]) for j in range(n_p)]


def _adam_only(name, g, w, m, v):
    def body(g_ref, w_ref, m_ref, v_ref, d_out, m_out, v_out):
        delta, m_new, v_new = _adamw(w_ref[...], g_ref[...], m_ref[...], v_ref[...])
        d_out[...] = delta
        m_out[...] = m_new
        v_out[...] = v_new

    shape = jax.ShapeDtypeStruct(w.shape, F32)
    return pl.pallas_call(body, out_shape=(shape,) * 3, name=name)(g, w, m, v)


def kernel(x, mem, g_mix, w_in, w_pool, pool_scale, w_a, g_ret, b_ret, w_r, g_mem, w_mem_kv, w_c, w_out, g_ffn, w_up, conv_w, conv_b, w_down, g_final, loss_target, m_g_mix, m_w_in, m_w_pool, m_pool_scale, m_w_a, m_g_ret, m_b_ret, m_w_r, m_g_mem, m_w_mem_kv, m_w_c, m_w_out, m_g_ffn, m_w_up, m_conv_w, m_conv_b, m_w_down, m_g_final, v_g_mix, v_w_in, v_w_pool, v_pool_scale, v_w_a, v_g_ret, v_b_ret, v_w_r, v_g_mem, v_w_mem_kv, v_w_c, v_w_out, v_g_ffn, v_w_up, v_conv_w, v_conv_b, v_w_down, v_g_final):
    B, S, _ = x.shape
    M = mem.shape[1]
    T = B * S
    me = _my_index()
    x2d = x.reshape(T, D_MODEL)
    mem2d = mem.reshape(B * M, D_MODEL)
    tgt2d = loss_target.reshape(T, D_MODEL)
    g_final2 = g_final.reshape(1, D_MODEL)

    big = dict(w_in=w_in[0], w_a=w_a[0], w_r=w_r[0], w_mem_kv=w_mem_kv[0], w_c=w_c[0], w_out=w_out[0],
               w_up=w_up[0], w_down=w_down[0])
    names = list(big)
    cast = dict(zip(names, _cast_shards([big[n] for n in names])))
    Win, cw_gathered = _comm_call("gather_w_in", _Gather([cast["w_in"], conv_w[0]]))
    cw_full = cw_gathered.transpose(1, 0, 2).reshape(3, FFN_HIDDEN)
    cw = cw_full.reshape(3, FFN_SLABS, UP_SHARD).transpose(1, 0, 2)
    cb = conv_b[0].reshape(FFN_SLABS, 1, UP_SHARD)
    wp = w_pool[0]
    tables = _ret_tables(S)

    h = _rms_fwd("rms_mix", x2d, g_mix)
    early = ("w_a", "w_r", "w_mem_kv", "w_c", "w_out", "w_down")
    proj, landed = _mm_cols_slab("mm_in", h, Win, comm=_Gather([cast[n] for n in early]))
    W = dict(zip(early, landed))
    Wa = W["w_a"].transpose(1, 0, 2).reshape(POOL_WIDTH, D_MODEL)
    Wc = W["w_c"].transpose(1, 0, 2).reshape(XA_WIDTH, D_MODEL)
    Wr = W["w_r"].reshape(D_MODEL, D_MODEL)
    Wkv = W["w_mem_kv"].reshape(D_MODEL, D_MODEL)
    Wout = W["w_out"].reshape(D_MODEL, D_MODEL)
    Wdown = W["w_down"].reshape(FFN_SLABS, UP_SHARD, D_MODEL)
    ypre = _pool_fwd(proj, wp, pool_scale, B, S)
    y_pool = _mm_rows("mm_a", ypre, Wa)
    (yr, ret_states), (Wup,) = _ret_fwd(proj, g_ret, b_ret, tables, B, S, comm=_Gather([cast["w_up"]]))
    y_ret = _mm_rows("mm_r", yr, Wr)
    mem_n = _rms_fwd("rms_mem", mem2d, g_mem)
    kv = _mm_rows("mm_kv", mem_n, Wkv)
    o_mem = _xa_fwd(proj, kv, B, S, M)
    y_mem = _mm_rows("mm_c", o_mem, Wc)
    ys = (y_pool, y_ret, y_mem)
    merged = _merge_fwd(proj, ys)
    x1 = _mm_rows("mm_out", merged, Wout, res=x2d)
    h2 = _rms_fwd("rms_ffn", x1, g_ffn)
    up = _mm_up(h2, Wup).reshape(2, FFN_SLABS, T, UP_SHARD)
    u = _glu_fwd(up, cw, cb, S)
    x2 = _mm_down(u, Wdown, x1)

    dx2, dg_final, loss_part = _loss_head(x2, tgt2d, g_final2)
    received = {}
    d_u = _mm_down_t(dx2, Wdown)
    dW_down = _mm_dw_down(u, dx2)
    (d_up, d_cw, d_cb), (received["w_down"],) = _glu_bwd(
        up, d_u, cw, cb, S, comm=_Exchange([dW_down.reshape(N_DEV, -1, D_MODEL)]))
    d_up = d_up.reshape(N_DEV, T, UP_SHARD)
    dW_up = _mm_dw_up("mm_dw_up", h2, d_up, (0, 1))
    d_h2, (up_c0,) = _mm_up_t(d_up, Wup, comm=_ExchangeTo([dW_up], 0))
    dx1, dg_ffn = _rms_bwd("rms_ffn_bwd", x1, g_ffn, d_h2, dx2)
    d_merged = _mm_rows("mm_out_t", dx1, Wout, kind="nt")
    dW_out = _mm_tn("mm_dw_out", merged, dx1, BF16)
    (d_gl, d_y_pool, d_y_ret, d_y_mem), (up_c1,) = _merge_bwd(proj, ys, d_merged, comm=_ExchangeTo([dW_up], 1))
    received["w_up"] = [up_c0, up_c1]
    dW_c = _mm_tn("mm_dw_c", o_mem, d_y_mem, BF16)
    d_o_mem = _mm_rows("mm_c_t", d_y_mem, Wc, kind="nt")
    (d_qx, d_kmem, d_vmem), (received["w_out"],) = _xa_bwd(
        proj, kv, d_o_mem, B, S, M, comm=_Exchange([dW_out.reshape(N_DEV, -1, D_MODEL)]))
    d_kv = jnp.concatenate([d_kmem, d_vmem], axis=1)
    dW_kv = _mm_tn("mm_dw_kv", mem_n, d_kv, BF16)
    d_mem_n = _mm_rows("mm_kv_t", d_kv, Wkv, kind="nt")
    dg_mem = _rms_bwd("rms_mem_bwd", mem2d, g_mem, d_mem_n, None)
    dW_a = _mm_tn("mm_dw_a", ypre, d_y_pool, BF16)
    d_ypre = _mm_rows("mm_a_t", d_y_pool, Wa, kind="nt")
    d_hp, dw_pool, d_scale = _pool_bwd(proj, d_ypre, wp, pool_scale, B, S)
    dW_r = _mm_tn("mm_dw_r", yr, d_y_ret, BF16)
    d_yr = _mm_rows("mm_r_t", d_y_ret, Wr, kind="nt")
    (d_q, d_k, d_v, d_gr, dg_ret, db_ret), landed = _ret_bwd(
        proj, ret_states, d_yr, g_ret, b_ret, tables, B, S,
        comm=_Exchange([dW_a.reshape(POOL_WIDTH, N_DEV, -1).transpose(1, 0, 2), dW_r.reshape(N_DEV, -1, D_MODEL),
                        dW_c.reshape(XA_WIDTH, N_DEV, -1).transpose(1, 0, 2), dW_kv.reshape(N_DEV, -1, D_MODEL)]))
    received["w_a"], received["w_r"], received["w_c"], received["w_mem_kv"] = landed
    small_names = ["w_pool", "pool_scale", "g_ret", "b_ret", "g_mem", "g_ffn", "conv_b", "g_final"]
    small_grads = [dw_pool, d_scale, dg_ret, db_ret, dg_mem, dg_ffn, d_cb.reshape(1, FFN_HIDDEN), dg_final,
                   d_cw.transpose(1, 0, 2).reshape(3, FFN_HIDDEN), loss_part]
    d_proj = jnp.concatenate([d_hp, d_q, d_k, d_v, d_gr, d_qx, d_gl], axis=1)
    dW_in0 = _mm_tn_slab("mm_dw_in0", h, d_proj, IN_SHARD, BF16, part=(0, 2))
    dW_in1, (in0,) = _mm_tn_slab("mm_dw_in1", h, d_proj, IN_SHARD, BF16, part=(1, 2), comm=_Exchange([dW_in0]))
    d_h, (in1, *small_all) = _mm_cols_slab_t("mm_in_t", d_proj, Win, comm=_Exchange([dW_in1], whole=small_grads))
    received["w_in"] = [in0, in1]
    grad_x, dg_mix = _rms_bwd("rms_mix_bwd", x2d, g_mix, d_h, dx1)
    (g_mix_all,) = _comm_call("gather_g_mix", _Exchange([], whole=[dg_mix]))

    args = dict(g_mix=g_mix, w_in=w_in, w_pool=w_pool, pool_scale=pool_scale, w_a=w_a, g_ret=g_ret, b_ret=b_ret,
                w_r=w_r, g_mem=g_mem, w_mem_kv=w_mem_kv, w_c=w_c, w_out=w_out, g_ffn=g_ffn, w_up=w_up,
                conv_w=conv_w, conv_b=conv_b, w_down=w_down, g_final=g_final)
    m_in = dict(g_mix=m_g_mix, w_in=m_w_in, w_pool=m_w_pool, pool_scale=m_pool_scale, w_a=m_w_a, g_ret=m_g_ret,
                b_ret=m_b_ret, w_r=m_w_r, g_mem=m_g_mem, w_mem_kv=m_w_mem_kv, w_c=m_w_c, w_out=m_w_out,
                g_ffn=m_g_ffn, w_up=m_w_up, conv_w=m_conv_w, conv_b=m_conv_b, w_down=m_w_down, g_final=m_g_final)
    v_in = dict(g_mix=v_g_mix, w_in=v_w_in, w_pool=v_w_pool, pool_scale=v_pool_scale, w_a=v_w_a, g_ret=v_g_ret,
                b_ret=v_b_ret, w_r=v_w_r, g_mem=v_g_mem, w_mem_kv=v_w_mem_kv, w_c=v_w_c, w_out=v_w_out,
                g_ffn=v_g_ffn, w_up=v_w_up, conv_w=v_conv_w, conv_b=v_conv_b, w_down=v_w_down, g_final=v_g_final)

    grads, deltas, new_m, new_v = {}, {}, {}, {}
    for n in names:
        shard = big[n].shape
        parts = received[n] if isinstance(received[n], list) else [received[n]]
        outs = _reduce_adam("adam_" + n, parts, big[n], m_in[n][0], v_in[n][0])
        for store, val in zip((grads, deltas, new_m, new_v), outs):
            store[n] = val.reshape((1,) + shard)

    def as_small(a):
        return a.reshape(a.shape[-3:]) if a.ndim > 2 else a.reshape(1, -1)

    def small_update(call_name, param_names, gathered):
        params = [tuple(as_small(d[n]) for d in (args, m_in, v_in)) for n in param_names]
        sums, updates = _small_adam(call_name, gathered, params)
        for n, g, (d_, m_, v_) in zip(param_names, sums, updates):
            shape = args[n].shape
            grads[n], deltas[n], new_m[n], new_v[n] = (a.reshape(shape) for a in (g, d_, m_, v_))
        return sums[len(param_names):]

    g_cw_full, loss_row = small_update("adam_small", small_names, small_all)
    loss = loss_row[0, 0]
    small_update("adam_g_mix", ["g_mix"], [g_mix_all])

    shard_cols = FFN_HIDDEN // N_DEV
    g_cw = lax.dynamic_slice_in_dim(g_cw_full, me * shard_cols, shard_cols, axis=1)
    d_, m_, v_ = _adam_only("adam_conv_w", g_cw, conv_w[0], m_conv_w[0], v_conv_w[0])
    grads["conv_w"], deltas["conv_w"], new_m["conv_w"], new_v["conv_w"] = g_cw[None], d_[None], m_[None], v_[None]

    order = ["g_mix", "w_in", "w_pool", "pool_scale", "w_a", "g_ret", "b_ret", "w_r", "g_mem", "w_mem_kv", "w_c",
             "w_out", "g_ffn", "w_up", "conv_w", "conv_b", "w_down", "g_final"]
    return (loss, grad_x.reshape(B, S, D_MODEL), *[grads[n] for n in order], *[deltas[n] for n in order],
            *[new_m[n] for n in order], *[new_v[n] for n in order])
```

```python
import functools
import math

import jax
import jax.numpy as jnp
from jax import lax
from jax.experimental import pallas as pl
from jax.experimental.pallas import tpu as pltpu

F32 = jnp.float32
BF16 = jnp.bfloat16

N_DEV = 8
D_MODEL = 1024
POOL_WINDOWS = (2, 4, 8, 16)
POOL_GROUP_DIM = 128
POOL_WIDTH = 512
POOL_HALO = 16
RET_HEADS = 4
RET_QK_DIM = 128
RET_V_DIM = 256
RET_CHUNK = 128
ROPE_BASE = 10000.0
XA_HEADS = 4
XA_HEAD_DIM = 128
XA_WIDTH = 512
IN_WIDTH = 7168
IN_SHARD = IN_WIDTH // N_DEV
FFN_HIDDEN = 2816
UP_SHARD = 2 * FFN_HIDDEN // N_DEV
FFN_SLABS = FFN_HIDDEN // UP_SHARD
EPS = 1e-6
ADAM_LR = 0.001
ADAM_B1 = 0.9
ADAM_B2 = 0.999
ADAM_EPS = 1e-08
ADAM_WD = 0.01
ADAM_STEP = 10
GELU_C = math.sqrt(2.0 / math.pi)
GELU_A = 0.044715
VMEM_LIMIT = 56 * 1024 * 1024
MM_ROWS = 2048
MM_ROWS_RES = 1024
MM_TOKENS = 2048
MESH = pl.DeviceIdType.MESH

COL_Q, COL_K, COL_V, COL_GR, COL_QX, COL_GL = 512, 1024, 1536, 2560, 3584, 4096

_DIMS = {
    "nn": (((1,), (0,)), ((), ())),
    "nt": (((1,), (1,)), ((), ())),
    "tn": (((0,), (0,)), ((), ())),
}


def _dot(a, b, kind="nn"):
    return lax.dot_general(a.astype(BF16), b.astype(BF16), _DIMS[kind], preferred_element_type=F32)


def _params(sem, vmem=VMEM_LIMIT):
    return pltpu.CompilerParams(dimension_semantics=sem, vmem_limit_bytes=vmem)


def _tile(n, pref):
    t = min(n, pref)
    while n % t:
        t //= 2
    return t


def _mesh_pos():
    return lax.axis_index("x"), lax.axis_index("y"), lax.axis_index("c")


def _dev_index(x, y, c):
    return 4 * x + 2 * y + c


def _my_index():
    return _dev_index(*_mesh_pos())


def _remote(src, dst, send_sems, recv_sems, s, to):
    return pltpu.make_async_remote_copy(src_ref=src, dst_ref=dst, send_sem=send_sems.at[s], recv_sem=recv_sems.at[s],
                                        device_id=to, device_id_type=MESH)


class _Gather:
    def __init__(self, shards):
        self.inputs = list(shards)
        self.out_shapes = [jax.ShapeDtypeStruct((N_DEV,) + s.shape, s.dtype) for s in shards]
        n = len(shards)
        self.sem_shapes = [pltpu.SemaphoreType.DMA((7 * n,)), pltpu.SemaphoreType.DMA((7 * n,)),
                           pltpu.SemaphoreType.DMA((n,))]

    def _places(self):
        x, y, c = _mesh_pos()
        return (x, y, c), (x, y, 1 - c), [(1 - x, y), (x, 1 - y), (1 - x, 1 - y)]

    def _local(self, src, dst, sems):
        me = _my_index()
        return [pltpu.make_async_copy(src[w], dst[w].at[me], sems[2].at[w]) for w in range(len(src))]

    def start(self, src, dst, sems):
        me, sib, chips = self._places()
        for cp in self._local(src, dst, sems):
            cp.start()
        for w in range(len(src)):
            land = dst[w].at[_dev_index(*me)]
            _remote(src[w], land, sems[0], sems[1], 7 * w, sib).start()
            for j, chip in enumerate(chips):
                _remote(src[w], land, sems[0], sems[1], 7 * w + 1 + j, (*chip, me[2])).start()

    def finish(self, src, dst, sems):
        me, sib, chips = self._places()
        n = len(src)
        for j, chip in enumerate(chips):
            for w in range(n):
                block = dst[w].at[_dev_index(*chip, me[2])]
                _remote(src[w], block, sems[0], sems[1], 7 * w + 1 + j, me).wait_recv()
                _remote(block, block, sems[0], sems[1], 7 * w + 4 + j, sib).start()
        for w in range(n):
            _remote(src[w], dst[w].at[_dev_index(*sib)], sems[0], sems[1], 7 * w, me).wait_recv()
            for j, chip in enumerate(chips):
                block = dst[w].at[_dev_index(*chip, sib[2])]
                _remote(block, block, sems[0], sems[1], 7 * w + 4 + j, me).wait_recv()
            for k in range(7):
                _remote(src[w], dst[w].at[0], sems[0], sems[1], 7 * w + k, me).wait_send()
        for cp in self._local(src, dst, sems):
            cp.wait()


class _Exchange:
    def __init__(self, partials, whole=()):
        self.n_part = len(partials)
        self.inputs = list(partials) + list(whole)
        self.out_shapes = [jax.ShapeDtypeStruct(p.shape, p.dtype) for p in partials]
        self.out_shapes += [jax.ShapeDtypeStruct((N_DEV,) + a.shape, a.dtype) for a in whole]
        n = len(self.inputs)
        self.sem_shapes = [pltpu.SemaphoreType.DMA((7 * n,)), pltpu.SemaphoreType.DMA((7 * n,)),
                           pltpu.SemaphoreType.DMA((n,))]

    def _peer(self, k):
        x, y, c = _mesh_pos()
        p = (x ^ ((k >> 2) & 1), y ^ ((k >> 1) & 1), c ^ (k & 1))
        return p, _dev_index(*p)

    def _source(self, src, w, slot):
        return src[w].at[slot] if w < self.n_part else src[w]

    def _local(self, src, dst, sems):
        me = _my_index()
        return [pltpu.make_async_copy(self._source(src, w, me), dst[w].at[me], sems[2].at[w])
                for w in range(len(src))]

    def start(self, src, dst, sems):
        me = _my_index()
        for cp in self._local(src, dst, sems):
            cp.start()
        for k in range(1, N_DEV):
            peer, peer_idx = self._peer(k)
            for w in range(len(src)):
                _remote(self._source(src, w, peer_idx), dst[w].at[me], sems[0], sems[1], 7 * w + k - 1, peer).start()

    def finish(self, src, dst, sems):
        for k in range(1, N_DEV):
            peer, peer_idx = self._peer(k)
            for w in range(len(src)):
                cp = _remote(self._source(src, w, peer_idx), dst[w].at[peer_idx], sems[0], sems[1], 7 * w + k - 1, peer)
                cp.wait_send()
                cp.wait_recv()
        for cp in self._local(src, dst, sems):
            cp.wait()


class _ExchangeTo:
    def __init__(self, partials, side):
        self.side = side
        self.inputs = list(partials)
        self.out_shapes = [jax.ShapeDtypeStruct(p.shape, p.dtype) for p in partials]
        n = len(partials)
        self.sem_shapes = [pltpu.SemaphoreType.DMA((7 * n,)), pltpu.SemaphoreType.DMA((7 * n,)),
                           pltpu.SemaphoreType.DMA((n,))]

    def _copies(self, src, dst, sems):
        x, y, c = _mesh_pos()
        me = _dev_index(x, y, c)
        receives = c == self.side
        remote = []
        for k in range(1, N_DEV):
            kx, ky, kc = (k >> 2) & 1, (k >> 1) & 1, k & 1
            peer = (x ^ kx, y ^ ky, c ^ kc)
            peer_idx = _dev_index(*peer)
            sends = c == (self.side ^ kc)
            for w in range(len(src)):
                slab = src[w].at[peer_idx]
                s = 7 * w + k - 1
                remote.append((sends, _remote(slab, dst[w].at[me], sems[0], sems[1], s, peer),
                               _remote(slab, dst[w].at[peer_idx], sems[0], sems[1], s, peer)))
        local = [pltpu.make_async_copy(src[w].at[me], dst[w].at[me], sems[2].at[w]) for w in range(len(src))]
        return receives, remote, local

    def start(self, src, dst, sems):
        receives, remote, local = self._copies(src, dst, sems)

        @pl.when(receives)
        def _():
            for cp in local:
                cp.start()

        for sends, send, _ in remote:
            pl.when(sends)(send.start)

    def finish(self, src, dst, sems):
        receives, remote, local = self._copies(src, dst, sems)
        for sends, send, arrive in remote:
            pl.when(sends)(send.wait_send)
            pl.when(receives)(arrive.wait_recv)

        @pl.when(receives)
        def _():
            for cp in local:
                cp.wait()


class _Both:
    def __init__(self, first, second):
        self.plans = (first, second)
        self.inputs = first.inputs + second.inputs
        self.out_shapes = first.out_shapes + second.out_shapes
        self.sem_shapes = first.sem_shapes + second.sem_shapes

    def _split(self, src, dst, sems):
        a = self.plans[0]
        ni, no = len(a.inputs), len(a.out_shapes)
        return (src[:ni], dst[:no], sems[:3]), (src[ni:], dst[no:], sems[3:])

    def start(self, src, dst, sems):
        for plan, part in zip(self.plans, self._split(src, dst, sems)):
            plan.start(*part)

    def finish(self, src, dst, sems):
        for plan, part in zip(self.plans, self._split(src, dst, sems)):
            plan.finish(*part)


def _pcall(body, args, *, name, out_shape, grid, in_specs, out_specs, scratch_shapes=(), sem=None, comm=None):
    single = not isinstance(out_shape, (tuple, list))
    outs = [out_shape] if single else list(out_shape)
    ospecs = [out_specs] if single else list(out_specs)
    n_in, n_out, n_scr = len(args), len(outs), len(scratch_shapes)

    def pick(res):
        return res[0] if single else tuple(res[:n_out])

    if comm is None:
        res = pl.pallas_call(
            body, out_shape=outs, grid=grid, in_specs=list(in_specs), out_specs=ospecs,
            scratch_shapes=list(scratch_shapes), name=name, compiler_params=_params(sem),
        )(*args)
        return pick(res), ()

    nci, nco = len(comm.inputs), len(comm.out_shapes)

    def carrier(*refs):
        at = 0
        parts = []
        for size in (n_in, nci, n_out, nco, n_scr, len(comm.sem_shapes)):
            parts.append(refs[at:at + size])
            at += size
        ins, cins, o, couts, scr, sems = parts
        ids = [pl.program_id(a) for a in range(len(grid))]
        first = functools.reduce(jnp.logical_and, [i == 0 for i in ids])
        last = functools.reduce(jnp.logical_and, [i == g - 1 for i, g in zip(ids, grid)])

        @pl.when(first)
        def _():
            comm.start(cins, couts, sems)

        body(*ins, *o, *scr)

        @pl.when(last)
        def _():
            comm.finish(cins, couts, sems)

    hbm = pl.BlockSpec(memory_space=pltpu.HBM)
    res = pl.pallas_call(
        carrier, out_shape=outs + comm.out_shapes, grid=grid, in_specs=list(in_specs) + [hbm] * nci,
        out_specs=ospecs + [hbm] * nco, scratch_shapes=list(scratch_shapes) + comm.sem_shapes, name=name,
        compiler_params=_params(("arbitrary",) * len(grid)),
    )(*args, *comm.inputs)
    return pick(res), tuple(res[n_out:])


def _comm_call(name, comm):
    def body(*refs):
        nci, nco = len(comm.inputs), len(comm.out_shapes)
        cins, couts, sems = refs[:nci], refs[nci:nci + nco], refs[nci + nco:]
        comm.start(cins, couts, sems)
        comm.finish(cins, couts, sems)

    hbm = pl.BlockSpec(memory_space=pltpu.HBM)
    return pl.pallas_call(
        body, out_shape=comm.out_shapes, in_specs=[hbm] * len(comm.inputs), out_specs=[hbm] * len(comm.out_shapes),
        scratch_shapes=comm.sem_shapes, name=name,
    )(*comm.inputs)


def _matmul(name, kind, a, b, out_shape, grid, a_spec, b_spec, o_spec, acc_shape, res=None, res_spec=None,
            comm=None):
    nk = grid[-1]
    has_res = res is not None

    def body(*refs):
        a_ref, b_ref = refs[0], refs[1]
        o_ref = refs[2 + has_res]

        def prod():
            return _dot(a_ref[...], b_ref[...], kind)

        def finish(acc):
            if has_res:
                acc = acc + refs[2][...]
            o_ref[...] = acc.astype(o_ref.dtype)

        if nk == 1:
            finish(prod())
        else:
            acc_ref = refs[3 + has_res]
            k = pl.program_id(len(grid) - 1)

            @pl.when(k == 0)
            def _():
                acc_ref[...] = prod()

            @pl.when(k > 0)
            def _():
                acc_ref[...] += prod()

            @pl.when(k == nk - 1)
            def _():
                finish(acc_ref[...])

    in_specs = [a_spec, b_spec] + ([res_spec] if has_res else [])
    args = (a, b) + ((res,) if has_res else ())
    scratch = [pltpu.VMEM(acc_shape, F32)] if nk > 1 else []
    sem = ("parallel",) * (len(grid) - 1) + ("arbitrary",)
    out, landed = _pcall(body, args, name=name, out_shape=out_shape, grid=grid, in_specs=in_specs,
                         out_specs=o_spec, scratch_shapes=scratch, sem=sem, comm=comm)
    return out if comm is None else (out, landed)


def _mm_rows(name, a, w, out_dtype=F32, res=None, kind="nn", tm=MM_ROWS, comm=None):
    M, K = a.shape
    N = w.shape[1] if kind == "nn" else w.shape[0]
    tm = _tile(M, tm)
    res_spec = pl.BlockSpec((tm, N), lambda i, k: (i, 0)) if res is not None else None
    return _matmul(
        name, kind, a, w, jax.ShapeDtypeStruct((M, N), out_dtype), (M // tm, 1),
        pl.BlockSpec((tm, K), lambda i, k: (i, 0)), pl.BlockSpec(w.shape, lambda i, k: (0, 0)),
        pl.BlockSpec((tm, N), lambda i, k: (i, 0)), (tm, N), res, res_spec, comm)


def _mm_tn(name, a, b, out_dtype=F32, tk=MM_TOKENS, comm=None):
    T, M = a.shape
    N = b.shape[1]
    tk = _tile(T, tk)
    return _matmul(
        name, "tn", a, b, jax.ShapeDtypeStruct((M, N), out_dtype), (1, T // tk),
        pl.BlockSpec((tk, M), lambda i, k: (k, 0)), pl.BlockSpec((tk, N), lambda i, k: (k, 0)),
        pl.BlockSpec((M, N), lambda i, k: (0, 0)), (M, N), comm=comm)


def _mm_cols_slab(name, a, w_slabs, out_dtype=F32, tm=MM_ROWS, comm=None):
    M, K = a.shape
    J, _, n = w_slabs.shape
    tm = _tile(M, tm)
    return _matmul(
        name, "nn", a, w_slabs, jax.ShapeDtypeStruct((M, J * n), out_dtype), (J, M // tm, 1),
        pl.BlockSpec((tm, K), lambda j, i, k: (i, 0)), pl.BlockSpec((None, K, n), lambda j, i, k: (j, 0, 0)),
        pl.BlockSpec((tm, n), lambda j, i, k: (i, j)), (tm, n), comm=comm)


def _mm_cols_slab_t(name, a, w_slabs, out_dtype=F32, tm=MM_ROWS, comm=None):
    M = a.shape[0]
    J, K, n = w_slabs.shape
    tm = _tile(M, tm)
    return _matmul(
        name, "nt", a, w_slabs, jax.ShapeDtypeStruct((M, K), out_dtype), (M // tm, J),
        pl.BlockSpec((tm, n), lambda i, j: (i, j)), pl.BlockSpec((None, K, n), lambda i, j: (j, 0, 0)),
        pl.BlockSpec((tm, K), lambda i, j: (i, 0)), (tm, K), comm=comm)


def _mm_in_t(d_lo, d_hi, w_slabs, tm=MM_ROWS, comm=None):
    M = d_lo.shape[0]
    J, K, n = w_slabs.shape
    half = J // 2
    tm = _tile(M, tm)

    def body(lo_ref, hi_ref, w_ref, o_ref, acc_ref):
        j = pl.program_id(1)

        @pl.when(j == 0)
        def _():
            acc_ref[...] = _dot(lo_ref[...], w_ref[...], "nt")

        @pl.when((j > 0) & (j < half))
        def _():
            acc_ref[...] += _dot(lo_ref[...], w_ref[...], "nt")

        @pl.when(j >= half)
        def _():
            acc_ref[...] += _dot(hi_ref[...], w_ref[...], "nt")

        @pl.when(j == J - 1)
        def _():
            o_ref[...] = acc_ref[...]

    out, landed = _pcall(
        body, (d_lo, d_hi, w_slabs), name="mm_in_t", out_shape=jax.ShapeDtypeStruct((M, K), F32), grid=(M // tm, J),
        in_specs=[pl.BlockSpec((tm, n), lambda i, j: (i, jnp.minimum(j, half - 1))),
                  pl.BlockSpec((tm, n), lambda i, j: (i, jnp.maximum(j - half, 0))),
                  pl.BlockSpec((None, K, n), lambda i, j: (j, 0, 0))],
        out_specs=pl.BlockSpec((tm, K), lambda i, j: (i, 0)), scratch_shapes=[pltpu.VMEM((tm, K), F32)],
        sem=("parallel", "arbitrary"), comm=comm)
    return out if comm is None else (out, landed)


def _mm_tn_slab(name, a, b, n, out_dtype=F32, tk=MM_TOKENS, comm=None, part=(0, 1)):
    T, M = a.shape
    p, of = part
    M //= of
    J = b.shape[1] // n
    tk = _tile(T, tk)
    return _matmul(
        name, "tn", a, b, jax.ShapeDtypeStruct((J, M, n), out_dtype), (J, T // tk),
        pl.BlockSpec((tk, M), lambda j, k: (k, p)), pl.BlockSpec((tk, n), lambda j, k: (k, j)),
        pl.BlockSpec((None, M, n), lambda j, k: (j, 0, 0)), (M, n), comm=comm)


def _rms_fwd(name, x, g, tm=512):
    T, Dm = x.shape
    tm = _tile(T, tm)

    def body(x_ref, g_ref, h_ref):
        xv = x_ref[...]
        r = lax.rsqrt(jnp.mean(xv * xv, axis=-1, keepdims=True) + EPS)
        h_ref[...] = (xv * r * g_ref[...]).astype(h_ref.dtype)

    return pl.pallas_call(
        body, out_shape=jax.ShapeDtypeStruct((T, Dm), BF16), grid=(T // tm,),
        in_specs=[pl.BlockSpec((tm, Dm), lambda i: (i, 0)), pl.BlockSpec((1, Dm), lambda i: (0, 0))],
        out_specs=pl.BlockSpec((tm, Dm), lambda i: (i, 0)), name=name, compiler_params=_params(("parallel",)),
    )(x, g)


def _rms_bwd(name, x, g, dh, dres, tm=512):
    T, Dm = x.shape
    tm = _tile(T, tm)
    want_dx = dres is not None

    def body(*refs):
        if want_dx:
            x_ref, g_ref, dh_ref, dres_ref, dx_ref, dg_ref = refs
        else:
            x_ref, g_ref, dh_ref, dg_ref = refs
        xv = x_ref[...]
        r = lax.rsqrt(jnp.mean(xv * xv, axis=-1, keepdims=True) + EPS)
        xhat = xv * r
        dhv = dh_ref[...]

        @pl.when(pl.program_id(0) == 0)
        def _():
            dg_ref[...] = jnp.zeros_like(dg_ref)

        dg_ref[...] += jnp.sum(dhv * xhat, axis=0, keepdims=True)
        if want_dx:
            dxhat = dhv * g_ref[...]
            dx_ref[...] = dres_ref[...] + r * (dxhat - xhat * jnp.mean(dxhat * xhat, axis=-1, keepdims=True))

    row = pl.BlockSpec((tm, Dm), lambda i: (i, 0))
    vec = pl.BlockSpec((1, Dm), lambda i: (0, 0))
    if want_dx:
        return pl.pallas_call(
            body, out_shape=(jax.ShapeDtypeStruct((T, Dm), F32), jax.ShapeDtypeStruct((1, Dm), F32)),
            grid=(T // tm,), in_specs=[row, vec, row, row], out_specs=(row, vec), name=name,
            compiler_params=_params(("arbitrary",)),
        )(x, g, dh, dres)
    return pl.pallas_call(
        body, out_shape=jax.ShapeDtypeStruct((1, Dm), F32), grid=(T // tm,), in_specs=[row, vec, row],
        out_specs=vec, name=name, compiler_params=_params(("arbitrary",)),
    )(x, g, dh)


def _pool_rows(S):
    return _tile(S, 256)


def _pool_count(c0, rows, w):
    t = c0 + lax.broadcasted_iota(jnp.int32, (rows, 1), 0)
    return jnp.minimum(t + 1, w).astype(F32)


def _pool_fwd(proj, w_pool, scale, B, S):
    CH = _pool_rows(S)

    def body(hp_ref, wp_ref, sc_ref, o_ref, pad_ref):
        pad_ref[0:POOL_HALO, :] = jnp.zeros((POOL_HALO, POOL_WIDTH), F32)
        pad_ref[POOL_HALO:, :] = hp_ref[...]
        for gi, w in enumerate(POOL_WINDOWS):
            cols = slice(gi * POOL_GROUP_DIM, (gi + 1) * POOL_GROUP_DIM)
            for c in range(S // CH):
                base = POOL_HALO + c * CH
                acc = pad_ref[base:base + CH, cols]
                tok = acc
                for j in range(1, w):
                    acc = acc + pad_ref[base - j:base - j + CH, cols]
                pooled = acc / _pool_count(c * CH, CH, w) - tok
                z = _dot(pooled, wp_ref[gi])
                o_ref[c * CH:(c + 1) * CH, cols] = (z * sc_ref[:, cols]).astype(o_ref.dtype)

    return pl.pallas_call(
        body, out_shape=jax.ShapeDtypeStruct((B * S, POOL_WIDTH), BF16), grid=(B,),
        in_specs=[pl.BlockSpec((S, POOL_WIDTH), lambda b: (b, 0)),
                  pl.BlockSpec(w_pool.shape, lambda b: (0, 0, 0)),
                  pl.BlockSpec((1, POOL_WIDTH), lambda b: (0, 0))],
        out_specs=pl.BlockSpec((S, POOL_WIDTH), lambda b: (b, 0)),
        scratch_shapes=[pltpu.VMEM((S + POOL_HALO, POOL_WIDTH), F32)],
        name="pool_fwd", compiler_params=_params(("parallel",)),
    )(proj, w_pool, scale)


def _pool_bwd(proj, d_ypre, w_pool, scale, B, S):
    CH = _pool_rows(S)

    def body(hp_ref, dy_ref, wp_ref, sc_ref, dhp_ref, dwp_ref, dsc_ref, pad_ref, sc_pad_ref, dp_ref):
        @pl.when(pl.program_id(0) == 0)
        def _():
            dwp_ref[...] = jnp.zeros_like(dwp_ref)
            dsc_ref[...] = jnp.zeros_like(dsc_ref)

        pad_ref[0:POOL_HALO, :] = jnp.zeros((POOL_HALO, POOL_WIDTH), F32)
        pad_ref[POOL_HALO:, :] = hp_ref[...]
        sc_pad_ref[S:, :] = jnp.zeros((POOL_HALO, POOL_WIDTH), F32)
        for gi, w in enumerate(POOL_WINDOWS):
            cols = slice(gi * POOL_GROUP_DIM, (gi + 1) * POOL_GROUP_DIM)
            for c in range(S // CH):
                base = POOL_HALO + c * CH
                rows = slice(c * CH, (c + 1) * CH)
                acc = pad_ref[base:base + CH, cols]
                tok = acc
                for j in range(1, w):
                    acc = acc + pad_ref[base - j:base - j + CH, cols]
                cnt = _pool_count(c * CH, CH, w)
                pooled = acc / cnt - tok
                z = _dot(pooled, wp_ref[gi])
                dy = dy_ref[rows, cols]
                dsc_ref[:, cols] += jnp.sum(dy * z, axis=0, keepdims=True)
                dz = dy * sc_ref[:, cols]
                dwp_ref[gi] += _dot(pooled, dz, "tn")
                dpool = _dot(dz, wp_ref[gi], "nt")
                dp_ref[rows, cols] = dpool
                sc_pad_ref[rows, cols] = dpool / cnt
            for c in range(S // CH):
                rows = slice(c * CH, (c + 1) * CH)
                acc = sc_pad_ref[rows, cols]
                for j in range(1, w):
                    acc = acc + sc_pad_ref[c * CH + j:c * CH + j + CH, cols]
                dhp_ref[rows, cols] = (acc - dp_ref[rows, cols]).astype(dhp_ref.dtype)

    seq = pl.BlockSpec((S, POOL_WIDTH), lambda b: (b, 0))
    return pl.pallas_call(
        body,
        out_shape=(jax.ShapeDtypeStruct((B * S, POOL_WIDTH), BF16),
                   jax.ShapeDtypeStruct(w_pool.shape, F32), jax.ShapeDtypeStruct((1, POOL_WIDTH), F32)),
        grid=(B,),
        in_specs=[seq, seq, pl.BlockSpec(w_pool.shape, lambda b: (0, 0, 0)),
                  pl.BlockSpec((1, POOL_WIDTH), lambda b: (0, 0))],
        out_specs=(seq, pl.BlockSpec(w_pool.shape, lambda b: (0, 0, 0)),
                   pl.BlockSpec((1, POOL_WIDTH), lambda b: (0, 0))),
        scratch_shapes=[pltpu.VMEM((S + POOL_HALO, POOL_WIDTH), F32),
                        pltpu.VMEM((S + POOL_HALO, POOL_WIDTH), F32),
                        pltpu.VMEM((S, POOL_WIDTH), F32)],
        name="pool_bwd", compiler_params=_params(("arbitrary",)),
    )(proj, d_ypre, w_pool, scale)


def _ret_tables(S):
    half = RET_QK_DIM // 2
    inv = ROPE_BASE ** (-jnp.arange(half, dtype=F32) / half)
    ang = jnp.arange(S, dtype=F32)[:, None] * inv[None, :]
    cos, sin = jnp.cos(ang), jnp.sin(ang)
    cos_full = jnp.concatenate([cos, cos], axis=-1)
    sin_signed = jnp.concatenate([-sin, sin], axis=-1)
    C = RET_CHUNK
    lg = jnp.log1p(-jnp.exp2(-5.0 - jnp.arange(RET_HEADS, dtype=F32)))[:, None, None]
    idx = jnp.arange(C, dtype=F32)
    rel = idx[:, None] - idx[None, :]
    decay = jnp.where(rel >= 0, jnp.exp(jnp.maximum(rel, 0.0) * lg), 0.0)
    q_decay = jnp.broadcast_to(jnp.exp((idx + 1.0)[None, :, None] * lg), (RET_HEADS, C, RET_QK_DIM))
    k_decay = jnp.broadcast_to(jnp.exp((C - 1.0 - idx)[None, :, None] * lg), (RET_HEADS, C, RET_QK_DIM))
    c_decay = jnp.broadcast_to(jnp.exp(C * lg), (RET_HEADS, 1, RET_V_DIM))
    return cos_full, sin_signed, decay, q_decay, k_decay, c_decay


def _rope(x, cos_full, sin_signed):
    return x * cos_full + pltpu.roll(x, RET_QK_DIM // 2, axis=1) * sin_signed


def _rope_t(dy, cos_full, sin_signed):
    return dy * cos_full + pltpu.roll(dy * sin_signed, RET_QK_DIM // 2, axis=1)


RET_COLS = 512


def _ret_specs(N, chunk_of):
    C = RET_CHUNK

    def rows(width, col=0):
        return pl.BlockSpec((C, width), lambda b, i: (b * N + chunk_of(i), col))

    def whole(shape):
        return pl.BlockSpec(shape, lambda b, i: (0,) * len(shape))

    wide = RET_HEADS * RET_V_DIM
    return dict(
        q=rows(RET_COLS, COL_Q // RET_COLS), k=rows(RET_COLS, COL_K // RET_COLS),
        v=[rows(RET_COLS, COL_V // RET_COLS + j) for j in range(2)],
        gr=[rows(RET_COLS, COL_GR // RET_COLS + j) for j in range(2)],
        table=pl.BlockSpec((C, RET_QK_DIM), lambda b, i: (chunk_of(i), 0)),
        decay=whole((RET_HEADS, C, C)), qd=whole((RET_HEADS, C, RET_QK_DIM)), kd=whole((RET_HEADS, C, RET_QK_DIM)),
        cd=whole((RET_HEADS, 1, RET_V_DIM)), vec=whole((1, wide)), qk_rows=rows(RET_COLS), v_rows=rows(wide),
        state=pl.BlockSpec((None, None, RET_HEADS, RET_QK_DIM, RET_V_DIM), lambda b, i: (b, chunk_of(i), 0, 0, 0)))


def _head_cols(h):
    pair = slice((h % 2) * RET_V_DIM, (h % 2 + 1) * RET_V_DIM)
    return slice(h * RET_QK_DIM, (h + 1) * RET_QK_DIM), h // 2, pair, slice(h * RET_V_DIM, (h + 1) * RET_V_DIM)


def _group_norm(o):
    mu = jnp.mean(o, axis=-1, keepdims=True)
    oc = o - mu
    rstd = lax.rsqrt(jnp.mean(oc * oc, axis=-1, keepdims=True) + EPS)
    return oc * rstd, rstd


def _ret_fwd(proj, g_ret, b_ret, tables, B, S, comm=None):
    N = S // RET_CHUNK
    cos_t, sin_t, decay, q_decay, k_decay, c_decay = tables
    sp = _ret_specs(N, lambda i: i)

    def body(q_ref, k_ref, v0_ref, v1_ref, gr0_ref, gr1_ref, cos_ref, sin_ref, dec_ref, qd_ref, kd_ref, cd_ref,
             g_ref, b_ref, y_ref, rs_ref, r_ref):
        @pl.when(pl.program_id(1) == 0)
        def _():
            r_ref[...] = jnp.zeros_like(r_ref)

        cs, sn = cos_ref[...], sin_ref[...]
        for h in range(RET_HEADS):
            qk, j, pair, wide = _head_cols(h)
            q = _rope(q_ref[:, qk], cs, sn)
            k = _rope(k_ref[:, qk], cs, sn) * (RET_QK_DIM ** -0.5)
            v = (v0_ref, v1_ref)[j][:, pair]
            R = r_ref[h]
            rs_ref[h] = R
            s = _dot(q, k, "nt") * dec_ref[h]
            o = _dot(s, v) + _dot(q * qd_ref[h], R)
            r_ref[h] = cd_ref[h] * R + _dot(k * kd_ref[h], v, "tn")
            on, _ = _group_norm(o)
            gr = (gr0_ref, gr1_ref)[j][:, pair]
            y_ref[:, wide] = (gr * jax.nn.sigmoid(gr) * (on * g_ref[:, wide] + b_ref[:, wide])).astype(y_ref.dtype)

    state = jax.ShapeDtypeStruct((B, N, RET_HEADS, RET_QK_DIM, RET_V_DIM), F32)
    return _pcall(
        body, (proj,) * 6 + (cos_t, sin_t, decay, q_decay, k_decay, c_decay, g_ret, b_ret),
        name="ret_fwd", out_shape=(jax.ShapeDtypeStruct((B * S, RET_HEADS * RET_V_DIM), BF16), state), grid=(B, N),
        in_specs=[sp["q"], sp["k"], *sp["v"], *sp["gr"], sp["table"], sp["table"], sp["decay"], sp["qd"],
                  sp["kd"], sp["cd"], sp["vec"], sp["vec"]],
        out_specs=(sp["v_rows"], sp["state"]),
        scratch_shapes=[pltpu.VMEM((RET_HEADS, RET_QK_DIM, RET_V_DIM), F32)],
        sem=("parallel", "arbitrary"), comm=comm)


def _ret_bwd(proj, states, d_yr, g_ret, b_ret, tables, B, S, comm=None):
    N = S // RET_CHUNK
    cos_t, sin_t, decay, q_decay, k_decay, c_decay = tables
    sp = _ret_specs(N, lambda i: N - 1 - i)
    qk_scale = RET_QK_DIM ** -0.5

    def body(q_ref, k_ref, v0_ref, v1_ref, gr0_ref, gr1_ref, dy_ref, rs_ref, cos_ref, sin_ref, dec_ref, qd_ref,
             kd_ref, cd_ref, g_ref, b_ref, dq_ref, dk_ref, dv_ref, dgr_ref, dg_ref, db_ref, dr_ref):
        @pl.when((pl.program_id(0) == 0) & (pl.program_id(1) == 0))
        def _():
            dg_ref[...] = jnp.zeros_like(dg_ref)
            db_ref[...] = jnp.zeros_like(db_ref)

        @pl.when(pl.program_id(1) == 0)
        def _():
            dr_ref[...] = jnp.zeros_like(dr_ref)

        cs, sn = cos_ref[...], sin_ref[...]
        for h in range(RET_HEADS):
            qk, j, pair, wide = _head_cols(h)
            q = _rope(q_ref[:, qk], cs, sn)
            k = _rope(k_ref[:, qk], cs, sn) * qk_scale
            v = (v0_ref, v1_ref)[j][:, pair]
            R, dR = rs_ref[h], dr_ref[h]
            dec, qd, kd = dec_ref[h], qd_ref[h], kd_ref[h]
            s = _dot(q, k, "nt") * dec
            o = _dot(s, v) + _dot(q * qd, R)
            on, rstd = _group_norm(o)
            g = g_ref[:, wide]
            oaff = on * g + b_ref[:, wide]
            gr = (gr0_ref, gr1_ref)[j][:, pair]
            sg = jax.nn.sigmoid(gr)
            dy = dy_ref[:, wide]
            dgr_ref[:, wide] = (dy * oaff * (sg * (1.0 + gr * (1.0 - sg)))).astype(dgr_ref.dtype)
            doaff = dy * (gr * sg)
            dg_ref[:, wide] += jnp.sum(doaff * on, axis=0, keepdims=True)
            db_ref[:, wide] += jnp.sum(doaff, axis=0, keepdims=True)
            don = doaff * g
            do = rstd * (don - jnp.mean(don, axis=-1, keepdims=True)
                         - on * jnp.mean(don * on, axis=-1, keepdims=True))
            ds = _dot(do, v, "nt") * dec
            dq = _dot(ds, k) + qd * _dot(do, R, "nt")
            dk = _dot(ds, q, "tn") + kd * _dot(v, dR, "nt")
            dv_ref[:, wide] = (_dot(s, do, "tn") + _dot(k * kd, dR)).astype(dv_ref.dtype)
            dr_ref[h] = cd_ref[h] * dR + _dot(q * qd, do, "tn")
            dq_ref[:, qk] = _rope_t(dq, cs, sn).astype(dq_ref.dtype)
            dk_ref[:, qk] = _rope_t(dk * qk_scale, cs, sn).astype(dk_ref.dtype)

    T = B * S
    qk_shape = jax.ShapeDtypeStruct((T, RET_HEADS * RET_QK_DIM), BF16)
    v_shape = jax.ShapeDtypeStruct((T, RET_HEADS * RET_V_DIM), BF16)
    vec_shape = jax.ShapeDtypeStruct((1, RET_HEADS * RET_V_DIM), F32)
    return _pcall(
        body, (proj,) * 6 + (d_yr, states, cos_t, sin_t, decay, q_decay, k_decay, c_decay, g_ret, b_ret),
        name="ret_bwd", out_shape=(qk_shape, qk_shape, v_shape, v_shape, vec_shape, vec_shape), grid=(B, N),
        in_specs=[sp["q"], sp["k"], *sp["v"], *sp["gr"], sp["v_rows"], sp["state"], sp["table"], sp["table"],
                  sp["decay"], sp["qd"], sp["kd"], sp["cd"], sp["vec"], sp["vec"]],
        out_specs=(sp["qk_rows"], sp["qk_rows"], sp["v_rows"], sp["v_rows"], sp["vec"], sp["vec"]),
        scratch_shapes=[pltpu.VMEM((RET_HEADS, RET_QK_DIM, RET_V_DIM), F32)],
        sem=("arbitrary", "arbitrary"), comm=comm)


def _xa_rows(S):
    return _tile(S, 256)


def _xa_specs(S, M):
    q = pl.BlockSpec((S, XA_HEAD_DIM), lambda b, h: (b, COL_QX // XA_HEAD_DIM + h))
    k = pl.BlockSpec((M, XA_HEAD_DIM), lambda b, h: (b, h))
    v = pl.BlockSpec((M, XA_HEAD_DIM), lambda b, h: (b, XA_HEADS + h))
    o = pl.BlockSpec((S, XA_HEAD_DIM), lambda b, h: (b, h))
    return q, k, v, o


def _softmax_rows(s):
    e = jnp.exp(s - jnp.max(s, axis=-1, keepdims=True))
    return e / jnp.sum(e, axis=-1, keepdims=True)


def _xa_fwd(proj, kv, B, S, M):
    CH = _xa_rows(S)
    q_spec, k_spec, v_spec, o_spec = _xa_specs(S, M)

    def body(q_ref, k_ref, v_ref, o_ref):
        def chunk(i, carry):
            rows = pl.ds(pl.multiple_of(i * CH, CH), CH)
            p = _softmax_rows(_dot(q_ref[rows, :], k_ref[...], "nt") * (XA_HEAD_DIM ** -0.5))
            o_ref[rows, :] = _dot(p, v_ref[...]).astype(o_ref.dtype)
            return carry

        lax.fori_loop(0, S // CH, chunk, 0)

    return pl.pallas_call(
        body, out_shape=jax.ShapeDtypeStruct((B * S, XA_WIDTH), BF16), grid=(B, XA_HEADS),
        in_specs=[q_spec, k_spec, v_spec], out_specs=o_spec, name="xattn_fwd",
        compiler_params=_params(("parallel", "parallel")),
    )(proj, kv, kv)


def _xa_bwd(proj, kv, d_o, B, S, M, comm=None):
    CH = _xa_rows(S)
    q_spec, k_spec, v_spec, o_spec = _xa_specs(S, M)
    scale = XA_HEAD_DIM ** -0.5

    def body(q_ref, k_ref, v_ref, do_ref, dq_ref, dk_ref, dv_ref):
        dk_ref[...] = jnp.zeros_like(dk_ref)
        dv_ref[...] = jnp.zeros_like(dv_ref)

        def chunk(i, carry):
            rows = pl.ds(pl.multiple_of(i * CH, CH), CH)
            q, do = q_ref[rows, :], do_ref[rows, :]
            p = _softmax_rows(_dot(q, k_ref[...], "nt") * scale)
            dp = _dot(do, v_ref[...], "nt")
            ds = p * (dp - jnp.sum(dp * p, axis=-1, keepdims=True)) * scale
            dq_ref[rows, :] = _dot(ds, k_ref[...]).astype(dq_ref.dtype)
            dk_ref[...] += _dot(ds, q, "tn")
            dv_ref[...] += _dot(p, do, "tn")
            return carry

        lax.fori_loop(0, S // CH, chunk, 0)

    kv_out = pl.BlockSpec((M, XA_HEAD_DIM), lambda b, h: (b, h))
    return _pcall(
        body, (proj, kv, kv, d_o), name="xattn_bwd",
        out_shape=(jax.ShapeDtypeStruct((B * S, XA_WIDTH), BF16), jax.ShapeDtypeStruct((B * M, XA_WIDTH), F32),
                   jax.ShapeDtypeStruct((B * M, XA_WIDTH), F32)),
        grid=(B, XA_HEADS), in_specs=[q_spec, k_spec, v_spec, o_spec], out_specs=(o_spec, kv_out, kv_out),
        sem=("parallel", "parallel"), comm=comm)


def _gate_specs(tm):
    n = COL_GL // D_MODEL
    return [pl.BlockSpec((tm, D_MODEL), lambda i, j=j: (i, n + j)) for j in range(3)]


def _merge_fwd(proj, ys, tm=256):
    T = proj.shape[0]
    tm = _tile(T, tm)
    row = pl.BlockSpec((tm, D_MODEL), lambda i: (i, 0))

    def body(g0, g1, g2, y0, y1, y2, o_ref):
        acc = jax.nn.sigmoid(g0[...]) * y0[...]
        acc = acc + jax.nn.sigmoid(g1[...]) * y1[...]
        acc = acc + jax.nn.sigmoid(g2[...]) * y2[...]
        o_ref[...] = acc.astype(o_ref.dtype)

    return pl.pallas_call(
        body, out_shape=jax.ShapeDtypeStruct((T, D_MODEL), BF16), grid=(T // tm,),
        in_specs=_gate_specs(tm) + [row] * 3, out_specs=row, name="merge_fwd",
        compiler_params=_params(("parallel",)),
    )(proj, proj, proj, *ys)


def _merge_bwd(proj, ys, d_merged, tm=256, comm=None):
    T = proj.shape[0]
    tm = _tile(T, tm)
    row = pl.BlockSpec((tm, D_MODEL), lambda i: (i, 0))

    def body(g0, g1, g2, y0, y1, y2, dm_ref, dgl_ref, d0, d1, d2):
        dm = dm_ref[...]
        for j, (g_ref, y_ref, d_ref) in enumerate(((g0, y0, d0), (g1, y1, d1), (g2, y2, d2))):
            sg = jax.nn.sigmoid(g_ref[...])
            d_ref[...] = (dm * sg).astype(d_ref.dtype)
            dgl_ref[:, j * D_MODEL:(j + 1) * D_MODEL] = (dm * y_ref[...] * sg * (1.0 - sg)).astype(dgl_ref.dtype)

    dy = jax.ShapeDtypeStruct((T, D_MODEL), BF16)
    return _pcall(
        body, (proj, proj, proj, *ys, d_merged), name="merge_bwd",
        out_shape=(jax.ShapeDtypeStruct((T, 3 * D_MODEL), BF16), dy, dy, dy), grid=(T // tm,),
        in_specs=_gate_specs(tm) + [row] * 4,
        out_specs=(pl.BlockSpec((tm, 3 * D_MODEL), lambda i: (i, 0)), row, row, row),
        sem=("parallel",), comm=comm)


def _gelu(x):
    return 0.5 * x * (1.0 + jnp.tanh(GELU_C * (x + GELU_A * x * x * x)))


def _gelu_grad(x):
    t = jnp.tanh(GELU_C * (x + GELU_A * x * x * x))
    return 0.5 * (1.0 + t) + 0.5 * x * (1.0 - t * t) * GELU_C * (1.0 + 3.0 * GELU_A * x * x)


def _shift_down(x, prev, n):
    rows = x.shape[0]
    r = lax.broadcasted_iota(jnp.int32, (8, 1), 0)
    rolled = pltpu.roll(x, n, axis=0)
    head = rolled[0:8]
    for j in range(n):
        head = jnp.where(r == j, prev[8 - n + j:8 - n + j + 1, :], head)
    return head if rows == 8 else jnp.concatenate([head, rolled[8:]], axis=0)


def _shift_up(x, nxt, n):
    rows = x.shape[0]
    r = lax.broadcasted_iota(jnp.int32, (8, 1), 0)
    rolled = pltpu.roll(x, rows - n, axis=0)
    tail = rolled[rows - 8:]
    for j in range(n):
        tail = jnp.where(r == 8 - n + j, nxt[j:j + 1, :], tail)
    return jnp.concatenate([rolled[:rows - 8], tail], axis=0)


def _conv(a, prev, cw, cb):
    return _shift_down(a, prev, 2) * cw[0:1, :] + _shift_down(a, prev, 1) * cw[1:2, :] + a * cw[2:3, :] + cb


def _glu_fwd(up, cw, cb, S, tm=256):
    T = up.shape[2]
    tm = _tile(S, tm)
    per_seq = S // tm

    def body(ab_ref, prev_ref, cw_ref, cb_ref, u_ref):
        i = pl.program_id(1)
        prev = jnp.where(i % per_seq == 0, 0.0, prev_ref[...])
        ac = _conv(ab_ref[0], prev, cw_ref[...], cb_ref[...])
        u_ref[...] = (_gelu(ac) * ab_ref[1]).astype(u_ref.dtype)

    return pl.pallas_call(
        body, out_shape=jax.ShapeDtypeStruct((FFN_SLABS, T, UP_SHARD), BF16), grid=(FFN_SLABS, T // tm),
        in_specs=[pl.BlockSpec((2, None, tm, UP_SHARD), lambda d, i: (0, d, i, 0)),
                  pl.BlockSpec((None, None, 8, UP_SHARD), lambda d, i: (0, d, jnp.maximum(i * (tm // 8) - 1, 0), 0)),
                  pl.BlockSpec((None, 3, UP_SHARD), lambda d, i: (d, 0, 0)),
                  pl.BlockSpec((None, 1, UP_SHARD), lambda d, i: (d, 0, 0))],
        out_specs=pl.BlockSpec((None, tm, UP_SHARD), lambda d, i: (d, i, 0)), name="glu_fwd",
        compiler_params=_params(("parallel", "parallel")),
    )(up, up, cw, cb)


def _glu_bwd(up, d_u, cw, cb, S, tm=256, comm=None):
    T = up.shape[2]
    tm = _tile(S, tm)
    per_seq = S // tm
    n_tiles = T // tm
    last8 = tm // 8

    def body(ab_ref, prev_ref, abn_ref, du_ref, dun_ref, cw_ref, cb_ref, dup_ref, dcw_ref, dcb_ref):
        i = pl.program_id(1)

        @pl.when(i == 0)
        def _():
            dcw_ref[...] = jnp.zeros_like(dcw_ref)
            dcb_ref[...] = jnp.zeros_like(dcb_ref)

        cw, cb = cw_ref[...], cb_ref[...]
        a, b = ab_ref[0], ab_ref[1]
        prev = jnp.where(i % per_seq == 0, 0.0, prev_ref[...])
        a2, a1 = _shift_down(a, prev, 2), _shift_down(a, prev, 1)
        ac = a2 * cw[0:1, :] + a1 * cw[1:2, :] + a * cw[2:3, :] + cb
        du = du_ref[...]
        dup_ref[1] = (du * _gelu(ac)).astype(dup_ref.dtype)
        dac = du * b * _gelu_grad(ac)
        dcb_ref[...] += jnp.sum(dac, axis=0, keepdims=True)
        dcw_ref[0:1, :] += jnp.sum(dac * a2, axis=0, keepdims=True)
        dcw_ref[1:2, :] += jnp.sum(dac * a1, axis=0, keepdims=True)
        dcw_ref[2:3, :] += jnp.sum(dac * a, axis=0, keepdims=True)
        an = abn_ref[0]
        acn = _conv(an, a[tm - 8:, :], cw, cb)
        dacn = jnp.where(i % per_seq == per_seq - 1, 0.0, dun_ref[...] * abn_ref[1] * _gelu_grad(acn))
        da = dac * cw[2:3, :] + _shift_up(dac, dacn, 1) * cw[1:2, :] + _shift_up(dac, dacn, 2) * cw[0:1, :]
        dup_ref[0] = da.astype(dup_ref.dtype)

    def nxt(i):
        return jnp.minimum((i + 1) * last8, T // 8 - 1)

    return _pcall(
        body, (up, up, up, d_u, d_u, cw, cb), name="glu_bwd",
        out_shape=(jax.ShapeDtypeStruct((2, FFN_SLABS, T, UP_SHARD), BF16),
                   jax.ShapeDtypeStruct((FFN_SLABS, 3, UP_SHARD), F32),
                   jax.ShapeDtypeStruct((FFN_SLABS, 1, UP_SHARD), F32)),
        grid=(FFN_SLABS, n_tiles),
        in_specs=[pl.BlockSpec((2, None, tm, UP_SHARD), lambda d, i: (0, d, i, 0)),
                  pl.BlockSpec((None, None, 8, UP_SHARD), lambda d, i: (0, d, jnp.maximum(i * last8 - 1, 0), 0)),
                  pl.BlockSpec((2, None, 8, UP_SHARD), lambda d, i: (0, d, nxt(i), 0)),
                  pl.BlockSpec((None, tm, UP_SHARD), lambda d, i: (d, i, 0)),
                  pl.BlockSpec((None, 8, UP_SHARD), lambda d, i: (d, nxt(i), 0)),
                  pl.BlockSpec((None, 3, UP_SHARD), lambda d, i: (d, 0, 0)),
                  pl.BlockSpec((None, 1, UP_SHARD), lambda d, i: (d, 0, 0))],
        out_specs=(pl.BlockSpec((2, None, tm, UP_SHARD), lambda d, i: (0, d, i, 0)),
                   pl.BlockSpec((None, 3, UP_SHARD), lambda d, i: (d, 0, 0)),
                   pl.BlockSpec((None, 1, UP_SHARD), lambda d, i: (d, 0, 0))),
        sem=("parallel", "arbitrary"), comm=comm)


def _mm_up(h2, w_up, tm=MM_ROWS):
    T, K = h2.shape
    tm = _tile(T, tm)
    return _matmul(
        "mm_up", "nn", h2, w_up, jax.ShapeDtypeStruct((N_DEV, T, UP_SHARD), F32), (N_DEV, T // tm, 1),
        pl.BlockSpec((tm, K), lambda j, i, k: (i, 0)), pl.BlockSpec((None, K, UP_SHARD), lambda j, i, k: (j, 0, 0)),
        pl.BlockSpec((None, tm, UP_SHARD), lambda j, i, k: (j, i, 0)), (tm, UP_SHARD))


def _mm_down(u, w_down, res, tm=MM_ROWS_RES):
    J, T, n = u.shape
    tm = _tile(T, tm)
    row = pl.BlockSpec((tm, D_MODEL), lambda i, d: (i, 0))
    return _matmul(
        "mm_down", "nn", u, w_down, jax.ShapeDtypeStruct((T, D_MODEL), F32), (T // tm, J),
        pl.BlockSpec((None, tm, n), lambda i, d: (d, i, 0)), pl.BlockSpec((None, n, D_MODEL), lambda i, d: (d, 0, 0)),
        row, (tm, D_MODEL), res, row)


def _mm_down_t(dx, w_down, tm=MM_ROWS):
    T = dx.shape[0]
    J, n, _ = w_down.shape
    tm = _tile(T, tm)
    return _matmul(
        "mm_down_t", "nt", dx, w_down, jax.ShapeDtypeStruct((J, T, n), F32), (J, T // tm, 1),
        pl.BlockSpec((tm, D_MODEL), lambda d, i, k: (i, 0)), pl.BlockSpec((None, n, D_MODEL), lambda d, i, k: (d, 0, 0)),
        pl.BlockSpec((None, tm, n), lambda d, i, k: (d, i, 0)), (tm, n))


def _mm_dw_down(u, dx, tk=MM_TOKENS):
    J, T, n = u.shape
    tk = _tile(T, tk)
    return _matmul(
        "mm_dw_down", "tn", u, dx, jax.ShapeDtypeStruct((J, n, D_MODEL), BF16), (J, T // tk),
        pl.BlockSpec((None, tk, n), lambda d, k: (d, k, 0)), pl.BlockSpec((tk, D_MODEL), lambda d, k: (k, 0)),
        pl.BlockSpec((None, n, D_MODEL), lambda d, k: (d, 0, 0)), (n, D_MODEL))


def _mm_dw_up(name, h2, d_up, part, tk=MM_TOKENS, comm=None):
    T, K = h2.shape
    p, of = part
    K //= of
    tk = _tile(T, tk)
    return _matmul(
        name, "tn", h2, d_up, jax.ShapeDtypeStruct((N_DEV, K, UP_SHARD), BF16), (N_DEV, T // tk),
        pl.BlockSpec((tk, K), lambda j, k: (k, p)), pl.BlockSpec((None, tk, UP_SHARD), lambda j, k: (j, k, 0)),
        pl.BlockSpec((None, K, UP_SHARD), lambda j, k: (j, 0, 0)), (K, UP_SHARD), comm=comm)


def _mm_up_t(d_up, w_up, tm=MM_ROWS, comm=None):
    J, T, n = d_up.shape
    K = w_up.shape[1]
    tm = _tile(T, tm)
    return _matmul(
        "mm_up_t", "nt", d_up, w_up, jax.ShapeDtypeStruct((T, K), F32), (T // tm, J),
        pl.BlockSpec((None, tm, n), lambda i, j: (j, i, 0)), pl.BlockSpec((None, K, n), lambda i, j: (j, 0, 0)),
        pl.BlockSpec((tm, K), lambda i, j: (i, 0)), (tm, K), comm=comm)


def _loss_head(x2, target, g_final, tm=512):
    T, Dm = x2.shape
    tm = _tile(T, tm)

    def body(x_ref, t_ref, g_ref, dx_ref, dg_ref, loss_ref):
        @pl.when(pl.program_id(0) == 0)
        def _():
            dg_ref[...] = jnp.zeros_like(dg_ref)
            loss_ref[...] = jnp.zeros_like(loss_ref)

        xv = x_ref[...]
        r = lax.rsqrt(jnp.mean(xv * xv, axis=-1, keepdims=True) + EPS)
        xhat = xv * r
        err = xhat * g_ref[...] - t_ref[...]
        loss_ref[...] += (0.5 / Dm) * jnp.sum(err * err)
        dy = err * (1.0 / Dm)
        dg_ref[...] += jnp.sum(dy * xhat, axis=0, keepdims=True)
        dxhat = dy * g_ref[...]
        dx_ref[...] = r * (dxhat - xhat * jnp.mean(dxhat * xhat, axis=-1, keepdims=True))

    row = pl.BlockSpec((tm, Dm), lambda i: (i, 0))
    vec = pl.BlockSpec((1, Dm), lambda i: (0, 0))
    return pl.pallas_call(
        body,
        out_shape=(jax.ShapeDtypeStruct((T, Dm), F32), jax.ShapeDtypeStruct((1, Dm), F32),
                   jax.ShapeDtypeStruct((1, Dm), F32)),
        grid=(T // tm,), in_specs=[row, row, vec], out_specs=(row, vec, vec), name="loss_head",
        compiler_params=_params(("arbitrary",)),
    )(x2, target, g_final)


def _cast_shards(shards):
    def body(*refs):
        n = len(refs) // 2
        for src, dst in zip(refs[:n], refs[n:]):
            dst[...] = src[...].astype(dst.dtype)

    return pl.pallas_call(
        body, out_shape=[jax.ShapeDtypeStruct(s.shape, BF16) for s in shards], name="cast_shards",
        compiler_params=pltpu.CompilerParams(vmem_limit_bytes=VMEM_LIMIT),
    )(*shards)


def _adamw(w, g, m, v):
    m = ADAM_B1 * m + (1.0 - ADAM_B1) * g
    v = ADAM_B2 * v + (1.0 - ADAM_B2) * (g * g)
    m_hat = m / (1.0 - ADAM_B1 ** ADAM_STEP)
    v_hat = v / (1.0 - ADAM_B2 ** ADAM_STEP)
    delta = -ADAM_LR * (m_hat / (jnp.sqrt(v_hat) + ADAM_EPS) + ADAM_WD * w)
    return delta, m, v


def _sum_parts(p_ref):
    g = p_ref[0].astype(F32)
    for d in range(1, N_DEV):
        g = g + p_ref[d].astype(F32)
    return g


def _reduce_adam(name, parts, w, m, v, tr=128):
    R, Cn = w.shape
    by_rows = sum(p.shape[1] for p in parts) == R and len(parts) > 1
    tr = math.gcd(tr, *[p.shape[1] for p in parts])
    n_tiles = [p.shape[1] // tr for p in parts]
    first = [sum(n_tiles[:j]) for j in range(len(parts))] if by_rows else [0] * len(parts)

    def body(*refs):
        p_refs = refs[:len(parts)]
        w_ref, m_ref, v_ref, g_out, d_out, m_out, v_out = refs[len(parts):]

        def update(p_ref):
            g = _sum_parts(p_ref)
            delta, m_new, v_new = _adamw(w_ref[...], g, m_ref[...], v_ref[...])
            g_out[...] = g
            d_out[...] = delta
            m_out[...] = m_new
            v_out[...] = v_new

        if len(parts) == 1:
            update(p_refs[0])
        elif by_rows:
            i = pl.program_id(0)
            for p_ref, t0, n in zip(p_refs, first, n_tiles):
                pl.when((i >= t0) & (i < t0 + n))(functools.partial(update, p_ref))
        else:
            c = lax.axis_index("c")
            for side, p_ref in enumerate(p_refs):
                pl.when(c == side)(functools.partial(update, p_ref))

    def part_spec(t0, n):
        return pl.BlockSpec((N_DEV, tr, Cn), lambda i: (0, jnp.clip(i - t0, 0, n - 1), 0))

    row = pl.BlockSpec((tr, Cn), lambda i: (i, 0))
    shape = jax.ShapeDtypeStruct((R, Cn), F32)
    return pl.pallas_call(
        body, out_shape=(shape,) * 4, grid=(R // tr,),
        in_specs=[part_spec(t0, n) for t0, n in zip(first, n_tiles)] + [row, row, row],
        out_specs=(row,) * 4, name=name, compiler_params=_params(("parallel",)),
    )(*parts, w, m, v)


def _small_adam(name, gathered, params):
    n_g, n_p = len(gathered), len(params)

    def body(*refs):
        g_refs = refs[:n_g]
        wmv = refs[n_g:n_g + 3 * n_p]
        sums = refs[n_g + 3 * n_p:2 * n_g + 3 * n_p]
        upd = refs[2 * n_g + 3 * n_p:]
        for j in range(n_g):
            g = _sum_parts(g_refs[j])
            sums[j][...] = g
            if j < n_p:
                w_ref, m_ref, v_ref = wmv[3 * j:3 * j + 3]
                delta, m_new, v_new = _adamw(w_ref[...], g, m_ref[...], v_ref[...])
                upd[3 * j][...] = delta
                upd[3 * j + 1][...] = m_new
                upd[3 * j + 2][...] = v_new

    flat = [a for wmv in params for a in wmv]
    out_shape = [jax.ShapeDtypeStruct(g.shape[1:], F32) for g in gathered]
    out_shape += [jax.ShapeDtypeStruct(a.shape, F32) for a in flat]
    res = pl.pallas_call(body, out_shape=out_shape, name=name)(*gathered, *flat)
    return res[:n_g], [tuple(res[n_g + 3 * j:n_g + 3 * j + 3]) for j in range(n_p)]


def _adam_only(name, g, w, m, v):
    def body(g_ref, w_ref, m_ref, v_ref, d_out, m_out, v_out):
        delta, m_new, v_new = _adamw(w_ref[...], g_ref[...], m_ref[...], v_ref[...])
        d_out[...] = delta
        m_out[...] = m_new
        v_out[...] = v_new

    shape = jax.ShapeDtypeStruct(w.shape, F32)
    return pl.pallas_call(body, out_shape=(shape,) * 3, name=name)(g, w, m, v)


def kernel(x, mem, g_mix, w_in, w_pool, pool_scale, w_a, g_ret, b_ret, w_r, g_mem, w_mem_kv, w_c, w_out, g_ffn, w_up, conv_w, conv_b, w_down, g_final, loss_target, m_g_mix, m_w_in, m_w_pool, m_pool_scale, m_w_a, m_g_ret, m_b_ret, m_w_r, m_g_mem, m_w_mem_kv, m_w_c, m_w_out, m_g_ffn, m_w_up, m_conv_w, m_conv_b, m_w_down, m_g_final, v_g_mix, v_w_in, v_w_pool, v_pool_scale, v_w_a, v_g_ret, v_b_ret, v_w_r, v_g_mem, v_w_mem_kv, v_w_c, v_w_out, v_g_ffn, v_w_up, v_conv_w, v_conv_b, v_w_down, v_g_final):
    B, S, _ = x.shape
    M = mem.shape[1]
    T = B * S
    me = _my_index()
    x2d = x.reshape(T, D_MODEL)
    mem2d = mem.reshape(B * M, D_MODEL)
    tgt2d = loss_target.reshape(T, D_MODEL)
    g_final2 = g_final.reshape(1, D_MODEL)

    big = dict(w_in=w_in[0], w_a=w_a[0], w_r=w_r[0], w_mem_kv=w_mem_kv[0], w_c=w_c[0], w_out=w_out[0],
               w_up=w_up[0], w_down=w_down[0])
    names = list(big)
    cast = dict(zip(names, _cast_shards([big[n] for n in names])))
    Win, cw_gathered = _comm_call("gather_w_in", _Gather([cast["w_in"], conv_w[0]]))
    cw_full = cw_gathered.transpose(1, 0, 2).reshape(3, FFN_HIDDEN)
    cw = cw_full.reshape(3, FFN_SLABS, UP_SHARD).transpose(1, 0, 2)
    cb = conv_b[0].reshape(FFN_SLABS, 1, UP_SHARD)
    wp = w_pool[0]
    tables = _ret_tables(S)

    h = _rms_fwd("rms_mix", x2d, g_mix)
    early = ("w_a", "w_r", "w_mem_kv", "w_c", "w_out", "w_down")
    proj, landed = _mm_cols_slab("mm_in", h, Win, comm=_Gather([cast[n] for n in early]))
    W = dict(zip(early, landed))
    Wa = W["w_a"].transpose(1, 0, 2).reshape(POOL_WIDTH, D_MODEL)
    Wc = W["w_c"].transpose(1, 0, 2).reshape(XA_WIDTH, D_MODEL)
    Wr = W["w_r"].reshape(D_MODEL, D_MODEL)
    Wkv = W["w_mem_kv"].reshape(D_MODEL, D_MODEL)
    Wout = W["w_out"].reshape(D_MODEL, D_MODEL)
    Wdown = W["w_down"].reshape(FFN_SLABS, UP_SHARD, D_MODEL)
    ypre = _pool_fwd(proj, wp, pool_scale, B, S)
    y_pool = _mm_rows("mm_a", ypre, Wa)
    (yr, ret_states), (Wup,) = _ret_fwd(proj, g_ret, b_ret, tables, B, S, comm=_Gather([cast["w_up"]]))
    y_ret = _mm_rows("mm_r", yr, Wr)
    mem_n = _rms_fwd("rms_mem", mem2d, g_mem)
    kv = _mm_rows("mm_kv", mem_n, Wkv)
    o_mem = _xa_fwd(proj, kv, B, S, M)
    y_mem = _mm_rows("mm_c", o_mem, Wc)
    ys = (y_pool, y_ret, y_mem)
    merged = _merge_fwd(proj, ys)
    x1 = _mm_rows("mm_out", merged, Wout, res=x2d, tm=MM_ROWS_RES)
    h2 = _rms_fwd("rms_ffn", x1, g_ffn)
    up = _mm_up(h2, Wup).reshape(2, FFN_SLABS, T, UP_SHARD)
    u = _glu_fwd(up, cw, cb, S)
    x2 = _mm_down(u, Wdown, x1)

    dx2, dg_final, loss_part = _loss_head(x2, tgt2d, g_final2)
    received = {}
    d_u = _mm_down_t(dx2, Wdown)
    dW_down = _mm_dw_down(u, dx2)
    (d_up, d_cw, d_cb), (received["w_down"],) = _glu_bwd(
        up, d_u, cw, cb, S, comm=_Exchange([dW_down.reshape(N_DEV, -1, D_MODEL)]))
    d_up = d_up.reshape(N_DEV, T, UP_SHARD)
    dW_up = _mm_dw_up("mm_dw_up", h2, d_up, (0, 1))
    d_h2, (up_c0,) = _mm_up_t(d_up, Wup, comm=_ExchangeTo([dW_up], 0))
    dx1, dg_ffn = _rms_bwd("rms_ffn_bwd", x1, g_ffn, d_h2, dx2)
    d_merged = _mm_rows("mm_out_t", dx1, Wout, kind="nt")
    dW_out = _mm_tn("mm_dw_out", merged, dx1, BF16)
    (d_gl, d_y_pool, d_y_ret, d_y_mem), (up_c1,) = _merge_bwd(proj, ys, d_merged, comm=_ExchangeTo([dW_up], 1))
    received["w_up"] = [up_c0, up_c1]
    dW_c = _mm_tn("mm_dw_c", o_mem, d_y_mem, BF16)
    d_o_mem = _mm_rows("mm_c_t", d_y_mem, Wc, kind="nt")
    (d_qx, d_kmem, d_vmem), (received["w_out"],) = _xa_bwd(
        proj, kv, d_o_mem, B, S, M, comm=_Exchange([dW_out.reshape(N_DEV, -1, D_MODEL)]))
    d_kv = jnp.concatenate([d_kmem, d_vmem], axis=1)
    dW_kv = _mm_tn("mm_dw_kv", mem_n, d_kv, BF16)
    d_mem_n = _mm_rows("mm_kv_t", d_kv, Wkv, kind="nt")
    dg_mem = _rms_bwd("rms_mem_bwd", mem2d, g_mem, d_mem_n, None)
    dW_a = _mm_tn("mm_dw_a", ypre, d_y_pool, BF16)
    d_ypre = _mm_rows("mm_a_t", d_y_pool, Wa, kind="nt")
    d_hp, dw_pool, d_scale = _pool_bwd(proj, d_ypre, wp, pool_scale, B, S)
    dW_r = _mm_tn("mm_dw_r", yr, d_y_ret, BF16)
    d_yr = _mm_rows("mm_r_t", d_y_ret, Wr, kind="nt")
    (d_q, d_k, d_v, d_gr, dg_ret, db_ret), landed = _ret_bwd(
        proj, ret_states, d_yr, g_ret, b_ret, tables, B, S,
        comm=_Exchange([dW_a.reshape(POOL_WIDTH, N_DEV, -1).transpose(1, 0, 2), dW_r.reshape(N_DEV, -1, D_MODEL),
                        dW_c.reshape(XA_WIDTH, N_DEV, -1).transpose(1, 0, 2), dW_kv.reshape(N_DEV, -1, D_MODEL)]))
    received["w_a"], received["w_r"], received["w_c"], received["w_mem_kv"] = landed
    small_names = ["w_pool", "pool_scale", "g_ret", "b_ret", "g_mem", "g_ffn", "conv_b", "g_final"]
    small_grads = [dw_pool, d_scale, dg_ret, db_ret, dg_mem, dg_ffn, d_cb.reshape(1, FFN_HIDDEN), dg_final,
                   d_cw.transpose(1, 0, 2).reshape(3, FFN_HIDDEN), loss_part]
    d_proj = jnp.concatenate([d_hp, d_q, d_k, d_v, d_gr, d_qx, d_gl], axis=1)
    dW_in0 = _mm_tn_slab("mm_dw_in0", h, d_proj, IN_SHARD, BF16, part=(0, 2))
    dW_in1, (in0,) = _mm_tn_slab("mm_dw_in1", h, d_proj, IN_SHARD, BF16, part=(1, 2), comm=_Exchange([dW_in0]))
    d_h, (in1, *small_all) = _mm_cols_slab_t("mm_in_t", d_proj, Win, comm=_Exchange([dW_in1], whole=small_grads))
    received["w_in"] = [in0, in1]
    grad_x, dg_mix = _rms_bwd("rms_mix_bwd", x2d, g_mix, d_h, dx1)
    (g_mix_all,) = _comm_call("gather_g_mix", _Exchange([], whole=[dg_mix]))

    args = dict(g_mix=g_mix, w_in=w_in, w_pool=w_pool, pool_scale=pool_scale, w_a=w_a, g_ret=g_ret, b_ret=b_ret,
                w_r=w_r, g_mem=g_mem, w_mem_kv=w_mem_kv, w_c=w_c, w_out=w_out, g_ffn=g_ffn, w_up=w_up,
                conv_w=conv_w, conv_b=conv_b, w_down=w_down, g_final=g_final)
    m_in = dict(g_mix=m_g_mix, w_in=m_w_in, w_pool=m_w_pool, pool_scale=m_pool_scale, w_a=m_w_a, g_ret=m_g_ret,
                b_ret=m_b_ret, w_r=m_w_r, g_mem=m_g_mem, w_mem_kv=m_w_mem_kv, w_c=m_w_c, w_out=m_w_out,
                g_ffn=m_g_ffn, w_up=m_w_up, conv_w=m_conv_w, conv_b=m_conv_b, w_down=m_w_down, g_final=m_g_final)
    v_in = dict(g_mix=v_g_mix, w_in=v_w_in, w_pool=v_w_pool, pool_scale=v_pool_scale, w_a=v_w_a, g_ret=v_g_ret,
                b_ret=v_b_ret, w_r=v_w_r, g_mem=v_g_mem, w_mem_kv=v_w_mem_kv, w_c=v_w_c, w_out=v_w_out,
                g_ffn=v_g_ffn, w_up=v_w_up, conv_w=v_conv_w, conv_b=v_conv_b, w_down=v_w_down, g_final=v_g_final)

    grads, deltas, new_m, new_v = {}, {}, {}, {}
    for n in names:
        shard = big[n].shape
        parts = received[n] if isinstance(received[n], list) else [received[n]]
        outs = _reduce_adam("adam_" + n, parts, big[n], m_in[n][0], v_in[n][0])
        for store, val in zip((grads, deltas, new_m, new_v), outs):
            store[n] = val.reshape((1,) + shard)

    def as_small(a):
        return a.reshape(a.shape[-3:]) if a.ndim > 2 else a.reshape(1, -1)

    def small_update(call_name, param_names, gathered):
        params = [tuple(as_small(d[n]) for d in (args, m_in, v_in)) for n in param_names]
        sums, updates = _small_adam(call_name, gathered, params)
        for n, g, (d_, m_, v_) in zip(param_names, sums, updates):
            shape = args[n].shape
            grads[n], deltas[n], new_m[n], new_v[n] = (a.reshape(shape) for a in (g, d_, m_, v_))
        return sums[len(param_names):]

    g_cw_full, loss_row = small_update("adam_small", small_names, small_all)
    loss = loss_row[0, 0]
    small_update("adam_g_mix", ["g_mix"], [g_mix_all])

    shard_cols = FFN_HIDDEN // N_DEV
    g_cw = lax.dynamic_slice_in_dim(g_cw_full, me * shard_cols, shard_cols, axis=1)
    d_, m_, v_ = _adam_only("adam_conv_w", g_cw, conv_w[0], m_conv_w[0], v_conv_w[0])
    grads["conv_w"], deltas["conv_w"], new_m["conv_w"], new_v["conv_w"] = g_cw[None], d_[None], m_[None], v_[None]

    order = ["g_mix", "w_in", "w_pool", "pool_scale", "w_a", "g_ret", "b_ret", "w_r", "g_mem", "w_mem_kv", "w_c",
             "w_out", "g_ffn", "w_up", "conv_w", "conv_b", "w_down", "g_final"]
    return (loss, grad_x.reshape(B, S, D_MODEL), *[grads[n] for n in order], *[deltas[n] for n in order],
            *[new_m[n] for n in order], *[new_v[n] for n in order])
```

```python
import functools
import math

import jax
import jax.numpy as jnp
from jax import lax
from jax.experimental import pallas as pl
from jax.experimental.pallas import tpu as pltpu

F32 = jnp.float32
BF16 = jnp.bfloat16

N_DEV = 8
D_MODEL = 1024
POOL_WINDOWS = (2, 4, 8, 16)
POOL_GROUP_DIM = 128
POOL_WIDTH = 512
POOL_HALO = 16
RET_HEADS = 4
RET_QK_DIM = 128
RET_V_DIM = 256
RET_CHUNK = 128
ROPE_BASE = 10000.0
XA_HEADS = 4
XA_HEAD_DIM = 128
XA_WIDTH = 512
IN_WIDTH = 7168
IN_SHARD = IN_WIDTH // N_DEV
FFN_HIDDEN = 2816
UP_SHARD = 2 * FFN_HIDDEN // N_DEV
FFN_SLABS = FFN_HIDDEN // UP_SHARD
EPS = 1e-6
ADAM_LR = 0.001
ADAM_B1 = 0.9
ADAM_B2 = 0.999
ADAM_EPS = 1e-08
ADAM_WD = 0.01
ADAM_STEP = 10
GELU_C = math.sqrt(2.0 / math.pi)
GELU_A = 0.044715
VMEM_LIMIT = 56 * 1024 * 1024
MM_ROWS = 2048
MM_ROWS_RES = 1024
MM_TOKENS = 2048
MESH = pl.DeviceIdType.MESH

COL_Q, COL_K, COL_V, COL_GR, COL_QX, COL_GL = 512, 1024, 1536, 2560, 3584, 4096

_DIMS = {
    "nn": (((1,), (0,)), ((), ())),
    "nt": (((1,), (1,)), ((), ())),
    "tn": (((0,), (0,)), ((), ())),
}


def _dot(a, b, kind="nn"):
    return lax.dot_general(a.astype(BF16), b.astype(BF16), _DIMS[kind], preferred_element_type=F32)


def _params(sem, vmem=VMEM_LIMIT):
    return pltpu.CompilerParams(dimension_semantics=sem, vmem_limit_bytes=vmem)


def _tile(n, pref):
    t = min(n, pref)
    while n % t:
        t //= 2
    return t


def _mesh_pos():
    return lax.axis_index("x"), lax.axis_index("y"), lax.axis_index("c")


def _dev_index(x, y, c):
    return 4 * x + 2 * y + c


def _my_index():
    return _dev_index(*_mesh_pos())


def _remote(src, dst, send_sems, recv_sems, s, to):
    return pltpu.make_async_remote_copy(src_ref=src, dst_ref=dst, send_sem=send_sems.at[s], recv_sem=recv_sems.at[s],
                                        device_id=to, device_id_type=MESH)


class _Gather:
    def __init__(self, shards):
        self.inputs = list(shards)
        self.out_shapes = [jax.ShapeDtypeStruct((N_DEV,) + s.shape, s.dtype) for s in shards]
        n = len(shards)
        self.sem_shapes = [pltpu.SemaphoreType.DMA((7 * n,)), pltpu.SemaphoreType.DMA((7 * n,)),
                           pltpu.SemaphoreType.DMA((n,))]

    def _places(self):
        x, y, c = _mesh_pos()
        return (x, y, c), (x, y, 1 - c), [(1 - x, y), (x, 1 - y), (1 - x, 1 - y)]

    def _local(self, src, dst, sems):
        me = _my_index()
        return [pltpu.make_async_copy(src[w], dst[w].at[me], sems[2].at[w]) for w in range(len(src))]

    def start(self, src, dst, sems):
        me, sib, chips = self._places()
        for cp in self._local(src, dst, sems):
            cp.start()
        for w in range(len(src)):
            land = dst[w].at[_dev_index(*me)]
            _remote(src[w], land, sems[0], sems[1], 7 * w, sib).start()
            for j, chip in enumerate(chips):
                _remote(src[w], land, sems[0], sems[1], 7 * w + 1 + j, (*chip, me[2])).start()

    def finish(self, src, dst, sems):
        me, sib, chips = self._places()
        n = len(src)
        for j, chip in enumerate(chips):
            for w in range(n):
                block = dst[w].at[_dev_index(*chip, me[2])]
                _remote(src[w], block, sems[0], sems[1], 7 * w + 1 + j, me).wait_recv()
                _remote(block, block, sems[0], sems[1], 7 * w + 4 + j, sib).start()
        for w in range(n):
            _remote(src[w], dst[w].at[_dev_index(*sib)], sems[0], sems[1], 7 * w, me).wait_recv()
            for j, chip in enumerate(chips):
                block = dst[w].at[_dev_index(*chip, sib[2])]
                _remote(block, block, sems[0], sems[1], 7 * w + 4 + j, me).wait_recv()
            for k in range(7):
                _remote(src[w], dst[w].at[0], sems[0], sems[1], 7 * w + k, me).wait_send()
        for cp in self._local(src, dst, sems):
            cp.wait()


class _Exchange:
    def __init__(self, partials, whole=()):
        self.n_part = len(partials)
        self.inputs = list(partials) + list(whole)
        self.out_shapes = [jax.ShapeDtypeStruct(p.shape, p.dtype) for p in partials]
        self.out_shapes += [jax.ShapeDtypeStruct((N_DEV,) + a.shape, a.dtype) for a in whole]
        n = len(self.inputs)
        self.sem_shapes = [pltpu.SemaphoreType.DMA((7 * n,)), pltpu.SemaphoreType.DMA((7 * n,)),
                           pltpu.SemaphoreType.DMA((n,))]

    def _peer(self, k):
        x, y, c = _mesh_pos()
        p = (x ^ ((k >> 2) & 1), y ^ ((k >> 1) & 1), c ^ (k & 1))
        return p, _dev_index(*p)

    def _source(self, src, w, slot):
        return src[w].at[slot] if w < self.n_part else src[w]

    def _local(self, src, dst, sems):
        me = _my_index()
        return [pltpu.make_async_copy(self._source(src, w, me), dst[w].at[me], sems[2].at[w])
                for w in range(len(src))]

    def start(self, src, dst, sems):
        me = _my_index()
        for cp in self._local(src, dst, sems):
            cp.start()
        for k in range(1, N_DEV):
            peer, peer_idx = self._peer(k)
            for w in range(len(src)):
                _remote(self._source(src, w, peer_idx), dst[w].at[me], sems[0], sems[1], 7 * w + k - 1, peer).start()

    def finish(self, src, dst, sems):
        for k in range(1, N_DEV):
            peer, peer_idx = self._peer(k)
            for w in range(len(src)):
                cp = _remote(self._source(src, w, peer_idx), dst[w].at[peer_idx], sems[0], sems[1], 7 * w + k - 1, peer)
                cp.wait_send()
                cp.wait_recv()
        for cp in self._local(src, dst, sems):
            cp.wait()


class _ExchangeTo:
    def __init__(self, partials, side):
        self.side = side
        self.inputs = list(partials)
        self.out_shapes = [jax.ShapeDtypeStruct(p.shape, p.dtype) for p in partials]
        n = len(partials)
        self.sem_shapes = [pltpu.SemaphoreType.DMA((7 * n,)), pltpu.SemaphoreType.DMA((7 * n,)),
                           pltpu.SemaphoreType.DMA((n,))]

    def _copies(self, src, dst, sems):
        x, y, c = _mesh_pos()
        me = _dev_index(x, y, c)
        receives = c == self.side
        remote = []
        for k in range(1, N_DEV):
            kx, ky, kc = (k >> 2) & 1, (k >> 1) & 1, k & 1
            peer = (x ^ kx, y ^ ky, c ^ kc)
            peer_idx = _dev_index(*peer)
            sends = c == (self.side ^ kc)
            for w in range(len(src)):
                slab = src[w].at[peer_idx]
                s = 7 * w + k - 1
                remote.append((sends, _remote(slab, dst[w].at[me], sems[0], sems[1], s, peer),
                               _remote(slab, dst[w].at[peer_idx], sems[0], sems[1], s, peer)))
        local = [pltpu.make_async_copy(src[w].at[me], dst[w].at[me], sems[2].at[w]) for w in range(len(src))]
        return receives, remote, local

    def start(self, src, dst, sems):
        receives, remote, local = self._copies(src, dst, sems)

        @pl.when(receives)
        def _():
            for cp in local:
                cp.start()

        for sends, send, _ in remote:
            pl.when(sends)(send.start)

    def finish(self, src, dst, sems):
        receives, remote, local = self._copies(src, dst, sems)
        for sends, send, arrive in remote:
            pl.when(sends)(send.wait_send)
            pl.when(receives)(arrive.wait_recv)

        @pl.when(receives)
        def _():
            for cp in local:
                cp.wait()


class _Both:
    def __init__(self, first, second):
        self.plans = (first, second)
        self.inputs = first.inputs + second.inputs
        self.out_shapes = first.out_shapes + second.out_shapes
        self.sem_shapes = first.sem_shapes + second.sem_shapes

    def _split(self, src, dst, sems):
        a = self.plans[0]
        ni, no = len(a.inputs), len(a.out_shapes)
        return (src[:ni], dst[:no], sems[:3]), (src[ni:], dst[no:], sems[3:])

    def start(self, src, dst, sems):
        for plan, part in zip(self.plans, self._split(src, dst, sems)):
            plan.start(*part)

    def finish(self, src, dst, sems):
        for plan, part in zip(self.plans, self._split(src, dst, sems)):
            plan.finish(*part)


def _pcall(body, args, *, name, out_shape, grid, in_specs, out_specs, scratch_shapes=(), sem=None, comm=None):
    single = not isinstance(out_shape, (tuple, list))
    outs = [out_shape] if single else list(out_shape)
    ospecs = [out_specs] if single else list(out_specs)
    n_in, n_out, n_scr = len(args), len(outs), len(scratch_shapes)

    def pick(res):
        return res[0] if single else tuple(res[:n_out])

    if comm is None:
        res = pl.pallas_call(
            body, out_shape=outs, grid=grid, in_specs=list(in_specs), out_specs=ospecs,
            scratch_shapes=list(scratch_shapes), name=name, compiler_params=_params(sem),
        )(*args)
        return pick(res), ()

    nci, nco = len(comm.inputs), len(comm.out_shapes)

    def carrier(*refs):
        at = 0
        parts = []
        for size in (n_in, nci, n_out, nco, n_scr, len(comm.sem_shapes)):
            parts.append(refs[at:at + size])
            at += size
        ins, cins, o, couts, scr, sems = parts
        ids = [pl.program_id(a) for a in range(len(grid))]
        first = functools.reduce(jnp.logical_and, [i == 0 for i in ids])
        last = functools.reduce(jnp.logical_and, [i == g - 1 for i, g in zip(ids, grid)])

        @pl.when(first)
        def _():
            comm.start(cins, couts, sems)

        body(*ins, *o, *scr)

        @pl.when(last)
        def _():
            comm.finish(cins, couts, sems)

    hbm = pl.BlockSpec(memory_space=pltpu.HBM)
    res = pl.pallas_call(
        carrier, out_shape=outs + comm.out_shapes, grid=grid, in_specs=list(in_specs) + [hbm] * nci,
        out_specs=ospecs + [hbm] * nco, scratch_shapes=list(scratch_shapes) + comm.sem_shapes, name=name,
        compiler_params=_params(("arbitrary",) * len(grid)),
    )(*args, *comm.inputs)
    return pick(res), tuple(res[n_out:])


def _comm_call(name, comm):
    def body(*refs):
        nci, nco = len(comm.inputs), len(comm.out_shapes)
        cins, couts, sems = refs[:nci], refs[nci:nci + nco], refs[nci + nco:]
        comm.start(cins, couts, sems)
        comm.finish(cins, couts, sems)

    hbm = pl.BlockSpec(memory_space=pltpu.HBM)
    return pl.pallas_call(
        body, out_shape=comm.out_shapes, in_specs=[hbm] * len(comm.inputs), out_specs=[hbm] * len(comm.out_shapes),
        scratch_shapes=comm.sem_shapes, name=name,
    )(*comm.inputs)


def _matmul(name, kind, a, b, out_shape, grid, a_spec, b_spec, o_spec, acc_shape, res=None, res_spec=None,
            comm=None):
    nk = grid[-1]
    has_res = res is not None

    def body(*refs):
        a_ref, b_ref = refs[0], refs[1]
        o_ref = refs[2 + has_res]

        def prod():
            return _dot(a_ref[...], b_ref[...], kind)

        def finish(acc):
            if has_res:
                acc = acc + refs[2][...]
            o_ref[...] = acc.astype(o_ref.dtype)

        if nk == 1:
            finish(prod())
        else:
            acc_ref = refs[3 + has_res]
            k = pl.program_id(len(grid) - 1)

            @pl.when(k == 0)
            def _():
                acc_ref[...] = prod()

            @pl.when(k > 0)
            def _():
                acc_ref[...] += prod()

            @pl.when(k == nk - 1)
            def _():
                finish(acc_ref[...])

    in_specs = [a_spec, b_spec] + ([res_spec] if has_res else [])
    args = (a, b) + ((res,) if has_res else ())
    scratch = [pltpu.VMEM(acc_shape, F32)] if nk > 1 else []
    sem = ("parallel",) * (len(grid) - 1) + ("arbitrary",)
    out, landed = _pcall(body, args, name=name, out_shape=out_shape, grid=grid, in_specs=in_specs,
                         out_specs=o_spec, scratch_shapes=scratch, sem=sem, comm=comm)
    return out if comm is None else (out, landed)


def _mm_rows(name, a, w, out_dtype=F32, res=None, kind="nn", tm=MM_ROWS, comm=None):
    M, K = a.shape
    N = w.shape[1] if kind == "nn" else w.shape[0]
    tm = _tile(M, tm)
    res_spec = pl.BlockSpec((tm, N), lambda i, k: (i, 0)) if res is not None else None
    return _matmul(
        name, kind, a, w, jax.ShapeDtypeStruct((M, N), out_dtype), (M // tm, 1),
        pl.BlockSpec((tm, K), lambda i, k: (i, 0)), pl.BlockSpec(w.shape, lambda i, k: (0, 0)),
        pl.BlockSpec((tm, N), lambda i, k: (i, 0)), (tm, N), res, res_spec, comm)


def _mm_tn(name, a, b, out_dtype=F32, tk=MM_TOKENS, comm=None):
    T, M = a.shape
    N = b.shape[1]
    tk = _tile(T, tk)
    return _matmul(
        name, "tn", a, b, jax.ShapeDtypeStruct((M, N), out_dtype), (1, T // tk),
        pl.BlockSpec((tk, M), lambda i, k: (k, 0)), pl.BlockSpec((tk, N), lambda i, k: (k, 0)),
        pl.BlockSpec((M, N), lambda i, k: (0, 0)), (M, N), comm=comm)


def _mm_cols_slab(name, a, w_slabs, out_dtype=F32, tm=MM_ROWS, comm=None):
    M, K = a.shape
    J, _, n = w_slabs.shape
    tm = _tile(M, tm)
    return _matmul(
        name, "nn", a, w_slabs, jax.ShapeDtypeStruct((M, J * n), out_dtype), (J, M // tm, 1),
        pl.BlockSpec((tm, K), lambda j, i, k: (i, 0)), pl.BlockSpec((None, K, n), lambda j, i, k: (j, 0, 0)),
        pl.BlockSpec((tm, n), lambda j, i, k: (i, j)), (tm, n), comm=comm)


def _mm_in_gather(h, shard, tm=MM_ROWS, comm=None):
    T, K = h.shape
    n = shard.shape[1]
    tm = _tile(T, tm)
    n_tiles = T // tm
    pair_of_chip_step = {4: 1, 2: 2, 6: 3}

    def slab_of(s):
        x, y, c = _mesh_pos()
        return _dev_index(x ^ ((s >> 2) & 1), y ^ ((s >> 1) & 1), c ^ (s & 1))

    def body(h_ref, shard_ref, proj_ref, win_ref, wbuf, slot_sems, send_sems, recv_sems, local_sem):
        s, i = pl.program_id(0), pl.program_id(1)
        x, y, c = _mesh_pos()
        me, sib = (x, y, c), (x, y, 1 - c)

        def slot_copy(step):
            src = shard_ref if step == 0 else win_ref.at[slab_of(step)]
            return pltpu.make_async_copy(src, wbuf.at[step % 2], slot_sems.at[step % 2])

        def fetch(step):
            if step >= 1:
                block = win_ref.at[slab_of(step)]
                if step == 1:
                    pair = 0
                elif step % 2 == 0:
                    pair = pair_of_chip_step[step]
                else:
                    pair = 3 + pair_of_chip_step[step - 1]
                _remote(block, block, send_sems, recv_sems, pair, me).wait_recv()
                if step % 2 == 0:
                    _remote(block, block, send_sems, recv_sems, 3 + pair, sib).start()
            slot_copy(step).start()

        @pl.when((s == 0) & (i == 0))
        def _():
            land = win_ref.at[_dev_index(*me)]
            pltpu.make_async_copy(shard_ref, land, local_sem).start()
            _remote(shard_ref, land, send_sems, recv_sems, 0, sib).start()
            for step, pair in pair_of_chip_step.items():
                peer = (x ^ ((step >> 2) & 1), y ^ ((step >> 1) & 1), c)
                _remote(shard_ref, land, send_sems, recv_sems, pair, peer).start()
            fetch(0)

        for step in range(N_DEV):
            @pl.when((s == step) & (i == 0))
            def _():
                slot_copy(step).wait()

            if step + 1 < N_DEV:
                @pl.when((s == step) & (i == n_tiles - 1))
                def _():
                    fetch(step + 1)

        proj_ref[...] = _dot(h_ref[...], wbuf[s % 2])

        @pl.when((s == N_DEV - 1) & (i == n_tiles - 1))
        def _():
            for pair in range(7):
                _remote(shard_ref, win_ref.at[0], send_sems, recv_sems, pair, me).wait_send()
            pltpu.make_async_copy(shard_ref, win_ref.at[_dev_index(*me)], local_sem).wait()

    hbm = pl.BlockSpec(memory_space=pltpu.HBM)
    return _pcall(
        body, (h, shard), name="mm_in",
        out_shape=(jax.ShapeDtypeStruct((T, N_DEV * n), F32), jax.ShapeDtypeStruct((N_DEV, K, n), shard.dtype)),
        grid=(N_DEV, n_tiles), in_specs=[pl.BlockSpec((tm, K), lambda s, i: (i, 0)), hbm],
        out_specs=(pl.BlockSpec((tm, n), lambda s, i: (i, slab_of(s))), hbm),
        scratch_shapes=[pltpu.VMEM((2, K, n), shard.dtype), pltpu.SemaphoreType.DMA((2,)),
                        pltpu.SemaphoreType.DMA((7,)), pltpu.SemaphoreType.DMA((7,)), pltpu.SemaphoreType.DMA],
        sem=("arbitrary", "arbitrary"), comm=comm)


def _mm_cols_slab_t(name, a, w_slabs, out_dtype=F32, tm=MM_ROWS, comm=None):
    M = a.shape[0]
    J, K, n = w_slabs.shape
    tm = _tile(M, tm)
    return _matmul(
        name, "nt", a, w_slabs, jax.ShapeDtypeStruct((M, K), out_dtype), (M // tm, J),
        pl.BlockSpec((tm, n), lambda i, j: (i, j)), pl.BlockSpec((None, K, n), lambda i, j: (j, 0, 0)),
        pl.BlockSpec((tm, K), lambda i, j: (i, 0)), (tm, K), comm=comm)


def _mm_in_t(d_lo, d_hi, w_slabs, tm=MM_ROWS, comm=None):
    M = d_lo.shape[0]
    J, K, n = w_slabs.shape
    half = J // 2
    tm = _tile(M, tm)

    def body(lo_ref, hi_ref, w_ref, o_ref, acc_ref):
        j = pl.program_id(1)

        @pl.when(j == 0)
        def _():
            acc_ref[...] = _dot(lo_ref[...], w_ref[...], "nt")

        @pl.when((j > 0) & (j < half))
        def _():
            acc_ref[...] += _dot(lo_ref[...], w_ref[...], "nt")

        @pl.when(j >= half)
        def _():
            acc_ref[...] += _dot(hi_ref[...], w_ref[...], "nt")

        @pl.when(j == J - 1)
        def _():
            o_ref[...] = acc_ref[...]

    out, landed = _pcall(
        body, (d_lo, d_hi, w_slabs), name="mm_in_t", out_shape=jax.ShapeDtypeStruct((M, K), F32), grid=(M // tm, J),
        in_specs=[pl.BlockSpec((tm, n), lambda i, j: (i, jnp.minimum(j, half - 1))),
                  pl.BlockSpec((tm, n), lambda i, j: (i, jnp.maximum(j - half, 0))),
                  pl.BlockSpec((None, K, n), lambda i, j: (j, 0, 0))],
        out_specs=pl.BlockSpec((tm, K), lambda i, j: (i, 0)), scratch_shapes=[pltpu.VMEM((tm, K), F32)],
        sem=("parallel", "arbitrary"), comm=comm)
    return out if comm is None else (out, landed)


def _mm_tn_slab(name, a, b, n, out_dtype=F32, tk=MM_TOKENS, comm=None, part=(0, 1)):
    T, M = a.shape
    p, of = part
    M //= of
    J = b.shape[1] // n
    tk = _tile(T, tk)
    return _matmul(
        name, "tn", a, b, jax.ShapeDtypeStruct((J, M, n), out_dtype), (J, T // tk),
        pl.BlockSpec((tk, M), lambda j, k: (k, p)), pl.BlockSpec((tk, n), lambda j, k: (k, j)),
        pl.BlockSpec((None, M, n), lambda j, k: (j, 0, 0)), (M, n), comm=comm)


def _rms_fwd(name, x, g, tm=512):
    T, Dm = x.shape
    tm = _tile(T, tm)

    def body(x_ref, g_ref, h_ref):
        xv = x_ref[...]
        r = lax.rsqrt(jnp.mean(xv * xv, axis=-1, keepdims=True) + EPS)
        h_ref[...] = (xv * r * g_ref[...]).astype(h_ref.dtype)

    return pl.pallas_call(
        body, out_shape=jax.ShapeDtypeStruct((T, Dm), BF16), grid=(T // tm,),
        in_specs=[pl.BlockSpec((tm, Dm), lambda i: (i, 0)), pl.BlockSpec((1, Dm), lambda i: (0, 0))],
        out_specs=pl.BlockSpec((tm, Dm), lambda i: (i, 0)), name=name, compiler_params=_params(("parallel",)),
    )(x, g)


def _rms_bwd(name, x, g, dh, dres, tm=512):
    T, Dm = x.shape
    tm = _tile(T, tm)
    want_dx = dres is not None

    def body(*refs):
        if want_dx:
            x_ref, g_ref, dh_ref, dres_ref, dx_ref, dg_ref = refs
        else:
            x_ref, g_ref, dh_ref, dg_ref = refs
        xv = x_ref[...]
        r = lax.rsqrt(jnp.mean(xv * xv, axis=-1, keepdims=True) + EPS)
        xhat = xv * r
        dhv = dh_ref[...]

        @pl.when(pl.program_id(0) == 0)
        def _():
            dg_ref[...] = jnp.zeros_like(dg_ref)

        dg_ref[...] += jnp.sum(dhv * xhat, axis=0, keepdims=True)
        if want_dx:
            dxhat = dhv * g_ref[...]
            dx_ref[...] = dres_ref[...] + r * (dxhat - xhat * jnp.mean(dxhat * xhat, axis=-1, keepdims=True))

    row = pl.BlockSpec((tm, Dm), lambda i: (i, 0))
    vec = pl.BlockSpec((1, Dm), lambda i: (0, 0))
    if want_dx:
        return pl.pallas_call(
            body, out_shape=(jax.ShapeDtypeStruct((T, Dm), F32), jax.ShapeDtypeStruct((1, Dm), F32)),
            grid=(T // tm,), in_specs=[row, vec, row, row], out_specs=(row, vec), name=name,
            compiler_params=_params(("arbitrary",)),
        )(x, g, dh, dres)
    return pl.pallas_call(
        body, out_shape=jax.ShapeDtypeStruct((1, Dm), F32), grid=(T // tm,), in_specs=[row, vec, row],
        out_specs=vec, name=name, compiler_params=_params(("arbitrary",)),
    )(x, g, dh)


def _pool_rows(S):
    return _tile(S, 256)


def _pool_count(c0, rows, w):
    t = c0 + lax.broadcasted_iota(jnp.int32, (rows, 1), 0)
    return jnp.minimum(t + 1, w).astype(F32)


def _pool_fwd(proj, w_pool, scale, B, S):
    CH = _pool_rows(S)

    def body(hp_ref, wp_ref, sc_ref, o_ref, pad_ref):
        pad_ref[0:POOL_HALO, :] = jnp.zeros((POOL_HALO, POOL_WIDTH), F32)
        pad_ref[POOL_HALO:, :] = hp_ref[...]
        for gi, w in enumerate(POOL_WINDOWS):
            cols = slice(gi * POOL_GROUP_DIM, (gi + 1) * POOL_GROUP_DIM)
            for c in range(S // CH):
                base = POOL_HALO + c * CH
                acc = pad_ref[base:base + CH, cols]
                tok = acc
                for j in range(1, w):
                    acc = acc + pad_ref[base - j:base - j + CH, cols]
                pooled = acc / _pool_count(c * CH, CH, w) - tok
                z = _dot(pooled, wp_ref[gi])
                o_ref[c * CH:(c + 1) * CH, cols] = (z * sc_ref[:, cols]).astype(o_ref.dtype)

    return pl.pallas_call(
        body, out_shape=jax.ShapeDtypeStruct((B * S, POOL_WIDTH), BF16), grid=(B,),
        in_specs=[pl.BlockSpec((S, POOL_WIDTH), lambda b: (b, 0)),
                  pl.BlockSpec(w_pool.shape, lambda b: (0, 0, 0)),
                  pl.BlockSpec((1, POOL_WIDTH), lambda b: (0, 0))],
        out_specs=pl.BlockSpec((S, POOL_WIDTH), lambda b: (b, 0)),
        scratch_shapes=[pltpu.VMEM((S + POOL_HALO, POOL_WIDTH), F32)],
        name="pool_fwd", compiler_params=_params(("parallel",)),
    )(proj, w_pool, scale)


def _pool_bwd(proj, d_ypre, w_pool, scale, B, S):
    CH = _pool_rows(S)

    def body(hp_ref, dy_ref, wp_ref, sc_ref, dhp_ref, dwp_ref, dsc_ref, pad_ref, sc_pad_ref, dp_ref):
        @pl.when(pl.program_id(0) == 0)
        def _():
            dwp_ref[...] = jnp.zeros_like(dwp_ref)
            dsc_ref[...] = jnp.zeros_like(dsc_ref)

        pad_ref[0:POOL_HALO, :] = jnp.zeros((POOL_HALO, POOL_WIDTH), F32)
        pad_ref[POOL_HALO:, :] = hp_ref[...]
        sc_pad_ref[S:, :] = jnp.zeros((POOL_HALO, POOL_WIDTH), F32)
        for gi, w in enumerate(POOL_WINDOWS):
            cols = slice(gi * POOL_GROUP_DIM, (gi + 1) * POOL_GROUP_DIM)
            for c in range(S // CH):
                base = POOL_HALO + c * CH
                rows = slice(c * CH, (c + 1) * CH)
                acc = pad_ref[base:base + CH, cols]
                tok = acc
                for j in range(1, w):
                    acc = acc + pad_ref[base - j:base - j + CH, cols]
                cnt = _pool_count(c * CH, CH, w)
                pooled = acc / cnt - tok
                z = _dot(pooled, wp_ref[gi])
                dy = dy_ref[rows, cols]
                dsc_ref[:, cols] += jnp.sum(dy * z, axis=0, keepdims=True)
                dz = dy * sc_ref[:, cols]
                dwp_ref[gi] += _dot(pooled, dz, "tn")
                dpool = _dot(dz, wp_ref[gi], "nt")
                dp_ref[rows, cols] = dpool
                sc_pad_ref[rows, cols] = dpool / cnt
            for c in range(S // CH):
                rows = slice(c * CH, (c + 1) * CH)
                acc = sc_pad_ref[rows, cols]
                for j in range(1, w):
                    acc = acc + sc_pad_ref[c * CH + j:c * CH + j + CH, cols]
                dhp_ref[rows, cols] = (acc - dp_ref[rows, cols]).astype(dhp_ref.dtype)

    seq = pl.BlockSpec((S, POOL_WIDTH), lambda b: (b, 0))
    return pl.pallas_call(
        body,
        out_shape=(jax.ShapeDtypeStruct((B * S, POOL_WIDTH), BF16),
                   jax.ShapeDtypeStruct(w_pool.shape, F32), jax.ShapeDtypeStruct((1, POOL_WIDTH), F32)),
        grid=(B,),
        in_specs=[seq, seq, pl.BlockSpec(w_pool.shape, lambda b: (0, 0, 0)),
                  pl.BlockSpec((1, POOL_WIDTH), lambda b: (0, 0))],
        out_specs=(seq, pl.BlockSpec(w_pool.shape, lambda b: (0, 0, 0)),
                   pl.BlockSpec((1, POOL_WIDTH), lambda b: (0, 0))),
        scratch_shapes=[pltpu.VMEM((S + POOL_HALO, POOL_WIDTH), F32),
                        pltpu.VMEM((S + POOL_HALO, POOL_WIDTH), F32),
                        pltpu.VMEM((S, POOL_WIDTH), F32)],
        name="pool_bwd", compiler_params=_params(("arbitrary",)),
    )(proj, d_ypre, w_pool, scale)


def _ret_tables(S):
    half = RET_QK_DIM // 2
    inv = ROPE_BASE ** (-jnp.arange(half, dtype=F32) / half)
    ang = jnp.arange(S, dtype=F32)[:, None] * inv[None, :]
    cos, sin = jnp.cos(ang), jnp.sin(ang)
    cos_full = jnp.concatenate([cos, cos], axis=-1)
    sin_signed = jnp.concatenate([-sin, sin], axis=-1)
    C = RET_CHUNK
    lg = jnp.log1p(-jnp.exp2(-5.0 - jnp.arange(RET_HEADS, dtype=F32)))[:, None, None]
    idx = jnp.arange(C, dtype=F32)
    rel = idx[:, None] - idx[None, :]
    decay = jnp.where(rel >= 0, jnp.exp(jnp.maximum(rel, 0.0) * lg), 0.0)
    q_decay = jnp.broadcast_to(jnp.exp((idx + 1.0)[None, :, None] * lg), (RET_HEADS, C, RET_QK_DIM))
    k_decay = jnp.broadcast_to(jnp.exp((C - 1.0 - idx)[None, :, None] * lg), (RET_HEADS, C, RET_QK_DIM))
    c_decay = jnp.broadcast_to(jnp.exp(C * lg), (RET_HEADS, 1, RET_V_DIM))
    return cos_full, sin_signed, decay, q_decay, k_decay, c_decay


def _rope(x, cos_full, sin_signed):
    return x * cos_full + pltpu.roll(x, RET_QK_DIM // 2, axis=1) * sin_signed


def _rope_t(dy, cos_full, sin_signed):
    return dy * cos_full + pltpu.roll(dy * sin_signed, RET_QK_DIM // 2, axis=1)


RET_COLS = 512


def _ret_specs(N, chunk_of):
    C = RET_CHUNK

    def rows(width, col=0):
        return pl.BlockSpec((C, width), lambda b, i: (b * N + chunk_of(i), col))

    def whole(shape):
        return pl.BlockSpec(shape, lambda b, i: (0,) * len(shape))

    wide = RET_HEADS * RET_V_DIM
    return dict(
        q=rows(RET_COLS, COL_Q // RET_COLS), k=rows(RET_COLS, COL_K // RET_COLS),
        v=[rows(RET_COLS, COL_V // RET_COLS + j) for j in range(2)],
        gr=[rows(RET_COLS, COL_GR // RET_COLS + j) for j in range(2)],
        table=pl.BlockSpec((C, RET_QK_DIM), lambda b, i: (chunk_of(i), 0)),
        decay=whole((RET_HEADS, C, C)), qd=whole((RET_HEADS, C, RET_QK_DIM)), kd=whole((RET_HEADS, C, RET_QK_DIM)),
        cd=whole((RET_HEADS, 1, RET_V_DIM)), vec=whole((1, wide)), qk_rows=rows(RET_COLS), v_rows=rows(wide),
        state=pl.BlockSpec((None, None, RET_HEADS, RET_QK_DIM, RET_V_DIM), lambda b, i: (b, chunk_of(i), 0, 0, 0)))


def _head_cols(h):
    pair = slice((h % 2) * RET_V_DIM, (h % 2 + 1) * RET_V_DIM)
    return slice(h * RET_QK_DIM, (h + 1) * RET_QK_DIM), h // 2, pair, slice(h * RET_V_DIM, (h + 1) * RET_V_DIM)


def _group_norm(o):
    mu = jnp.mean(o, axis=-1, keepdims=True)
    oc = o - mu
    rstd = lax.rsqrt(jnp.mean(oc * oc, axis=-1, keepdims=True) + EPS)
    return oc * rstd, rstd


def _ret_fwd(proj, g_ret, b_ret, tables, B, S, comm=None):
    N = S // RET_CHUNK
    cos_t, sin_t, decay, q_decay, k_decay, c_decay = tables
    sp = _ret_specs(N, lambda i: i)

    def body(q_ref, k_ref, v0_ref, v1_ref, gr0_ref, gr1_ref, cos_ref, sin_ref, dec_ref, qd_ref, kd_ref, cd_ref,
             g_ref, b_ref, y_ref, rs_ref, r_ref):
        @pl.when(pl.program_id(1) == 0)
        def _():
            r_ref[...] = jnp.zeros_like(r_ref)

        cs, sn = cos_ref[...], sin_ref[...]
        for h in range(RET_HEADS):
            qk, j, pair, wide = _head_cols(h)
            q = _rope(q_ref[:, qk], cs, sn)
            k = _rope(k_ref[:, qk], cs, sn) * (RET_QK_DIM ** -0.5)
            v = (v0_ref, v1_ref)[j][:, pair]
            R = r_ref[h]
            rs_ref[h] = R
            s = _dot(q, k, "nt") * dec_ref[h]
            o = _dot(s, v) + _dot(q * qd_ref[h], R)
            r_ref[h] = cd_ref[h] * R + _dot(k * kd_ref[h], v, "tn")
            on, _ = _group_norm(o)
            gr = (gr0_ref, gr1_ref)[j][:, pair]
            y_ref[:, wide] = (gr * jax.nn.sigmoid(gr) * (on * g_ref[:, wide] + b_ref[:, wide])).astype(y_ref.dtype)

    state = jax.ShapeDtypeStruct((B, N, RET_HEADS, RET_QK_DIM, RET_V_DIM), F32)
    return _pcall(
        body, (proj,) * 6 + (cos_t, sin_t, decay, q_decay, k_decay, c_decay, g_ret, b_ret),
        name="ret_fwd", out_shape=(jax.ShapeDtypeStruct((B * S, RET_HEADS * RET_V_DIM), BF16), state), grid=(B, N),
        in_specs=[sp["q"], sp["k"], *sp["v"], *sp["gr"], sp["table"], sp["table"], sp["decay"], sp["qd"],
                  sp["kd"], sp["cd"], sp["vec"], sp["vec"]],
        out_specs=(sp["v_rows"], sp["state"]),
        scratch_shapes=[pltpu.VMEM((RET_HEADS, RET_QK_DIM, RET_V_DIM), F32)],
        sem=("parallel", "arbitrary"), comm=comm)


def _ret_bwd(proj, states, d_yr, g_ret, b_ret, tables, B, S, comm=None):
    N = S // RET_CHUNK
    cos_t, sin_t, decay, q_decay, k_decay, c_decay = tables
    sp = _ret_specs(N, lambda i: N - 1 - i)
    qk_scale = RET_QK_DIM ** -0.5

    def body(q_ref, k_ref, v0_ref, v1_ref, gr0_ref, gr1_ref, dy_ref, rs_ref, cos_ref, sin_ref, dec_ref, qd_ref,
             kd_ref, cd_ref, g_ref, b_ref, dq_ref, dk_ref, dv_ref, dgr_ref, dg_ref, db_ref, dr_ref):
        @pl.when((pl.program_id(0) == 0) & (pl.program_id(1) == 0))
        def _():
            dg_ref[...] = jnp.zeros_like(dg_ref)
            db_ref[...] = jnp.zeros_like(db_ref)

        @pl.when(pl.program_id(1) == 0)
        def _():
            dr_ref[...] = jnp.zeros_like(dr_ref)

        cs, sn = cos_ref[...], sin_ref[...]
        for h in range(RET_HEADS):
            qk, j, pair, wide = _head_cols(h)
            q = _rope(q_ref[:, qk], cs, sn)
            k = _rope(k_ref[:, qk], cs, sn) * qk_scale
            v = (v0_ref, v1_ref)[j][:, pair]
            R, dR = rs_ref[h], dr_ref[h]
            dec, qd, kd = dec_ref[h], qd_ref[h], kd_ref[h]
            s = _dot(q, k, "nt") * dec
            o = _dot(s, v) + _dot(q * qd, R)
            on, rstd = _group_norm(o)
            g = g_ref[:, wide]
            oaff = on * g + b_ref[:, wide]
            gr = (gr0_ref, gr1_ref)[j][:, pair]
            sg = jax.nn.sigmoid(gr)
            dy = dy_ref[:, wide]
            dgr_ref[:, wide] = (dy * oaff * (sg * (1.0 + gr * (1.0 - sg)))).astype(dgr_ref.dtype)
            doaff = dy * (gr * sg)
            dg_ref[:, wide] += jnp.sum(doaff * on, axis=0, keepdims=True)
            db_ref[:, wide] += jnp.sum(doaff, axis=0, keepdims=True)
            don = doaff * g
            do = rstd * (don - jnp.mean(don, axis=-1, keepdims=True)
                         - on * jnp.mean(don * on, axis=-1, keepdims=True))
            ds = _dot(do, v, "nt") * dec
            dq = _dot(ds, k) + qd * _dot(do, R, "nt")
            dk = _dot(ds, q, "tn") + kd * _dot(v, dR, "nt")
            dv_ref[:, wide] = (_dot(s, do, "tn") + _dot(k * kd, dR)).astype(dv_ref.dtype)
            dr_ref[h] = cd_ref[h] * dR + _dot(q * qd, do, "tn")
            dq_ref[:, qk] = _rope_t(dq, cs, sn).astype(dq_ref.dtype)
            dk_ref[:, qk] = _rope_t(dk * qk_scale, cs, sn).astype(dk_ref.dtype)

    T = B * S
    qk_shape = jax.ShapeDtypeStruct((T, RET_HEADS * RET_QK_DIM), BF16)
    v_shape = jax.ShapeDtypeStruct((T, RET_HEADS * RET_V_DIM), BF16)
    vec_shape = jax.ShapeDtypeStruct((1, RET_HEADS * RET_V_DIM), F32)
    return _pcall(
        body, (proj,) * 6 + (d_yr, states, cos_t, sin_t, decay, q_decay, k_decay, c_decay, g_ret, b_ret),
        name="ret_bwd", out_shape=(qk_shape, qk_shape, v_shape, v_shape, vec_shape, vec_shape), grid=(B, N),
        in_specs=[sp["q"], sp["k"], *sp["v"], *sp["gr"], sp["v_rows"], sp["state"], sp["table"], sp["table"],
                  sp["decay"], sp["qd"], sp["kd"], sp["cd"], sp["vec"], sp["vec"]],
        out_specs=(sp["qk_rows"], sp["qk_rows"], sp["v_rows"], sp["v_rows"], sp["vec"], sp["vec"]),
        scratch_shapes=[pltpu.VMEM((RET_HEADS, RET_QK_DIM, RET_V_DIM), F32)],
        sem=("arbitrary", "arbitrary"), comm=comm)


def _xa_rows(S):
    return _tile(S, 256)


def _xa_specs(S, M):
    q = pl.BlockSpec((S, XA_HEAD_DIM), lambda b, h: (b, COL_QX // XA_HEAD_DIM + h))
    k = pl.BlockSpec((M, XA_HEAD_DIM), lambda b, h: (b, h))
    v = pl.BlockSpec((M, XA_HEAD_DIM), lambda b, h: (b, XA_HEADS + h))
    o = pl.BlockSpec((S, XA_HEAD_DIM), lambda b, h: (b, h))
    return q, k, v, o


def _softmax_rows(s):
    e = jnp.exp(s - jnp.max(s, axis=-1, keepdims=True))
    return e / jnp.sum(e, axis=-1, keepdims=True)


def _xa_fwd(proj, kv, B, S, M):
    CH = _xa_rows(S)
    q_spec, k_spec, v_spec, o_spec = _xa_specs(S, M)

    def body(q_ref, k_ref, v_ref, o_ref):
        def chunk(i, carry):
            rows = pl.ds(pl.multiple_of(i * CH, CH), CH)
            p = _softmax_rows(_dot(q_ref[rows, :], k_ref[...], "nt") * (XA_HEAD_DIM ** -0.5))
            o_ref[rows, :] = _dot(p, v_ref[...]).astype(o_ref.dtype)
            return carry

        lax.fori_loop(0, S // CH, chunk, 0)

    return pl.pallas_call(
        body, out_shape=jax.ShapeDtypeStruct((B * S, XA_WIDTH), BF16), grid=(B, XA_HEADS),
        in_specs=[q_spec, k_spec, v_spec], out_specs=o_spec, name="xattn_fwd",
        compiler_params=_params(("parallel", "parallel")),
    )(proj, kv, kv)


def _xa_bwd(proj, kv, d_o, B, S, M, comm=None):
    CH = _xa_rows(S)
    q_spec, k_spec, v_spec, o_spec = _xa_specs(S, M)
    scale = XA_HEAD_DIM ** -0.5

    def body(q_ref, k_ref, v_ref, do_ref, dq_ref, dk_ref, dv_ref):
        dk_ref[...] = jnp.zeros_like(dk_ref)
        dv_ref[...] = jnp.zeros_like(dv_ref)

        def chunk(i, carry):
            rows = pl.ds(pl.multiple_of(i * CH, CH), CH)
            q, do = q_ref[rows, :], do_ref[rows, :]
            p = _softmax_rows(_dot(q, k_ref[...], "nt") * scale)
            dp = _dot(do, v_ref[...], "nt")
            ds = p * (dp - jnp.sum(dp * p, axis=-1, keepdims=True)) * scale
            dq_ref[rows, :] = _dot(ds, k_ref[...]).astype(dq_ref.dtype)
            dk_ref[...] += _dot(ds, q, "tn")
            dv_ref[...] += _dot(p, do, "tn")
            return carry

        lax.fori_loop(0, S // CH, chunk, 0)

    kv_out = pl.BlockSpec((M, XA_HEAD_DIM), lambda b, h: (b, h))
    return _pcall(
        body, (proj, kv, kv, d_o), name="xattn_bwd",
        out_shape=(jax.ShapeDtypeStruct((B * S, XA_WIDTH), BF16), jax.ShapeDtypeStruct((B * M, XA_WIDTH), F32),
                   jax.ShapeDtypeStruct((B * M, XA_WIDTH), F32)),
        grid=(B, XA_HEADS), in_specs=[q_spec, k_spec, v_spec, o_spec], out_specs=(o_spec, kv_out, kv_out),
        sem=("parallel", "parallel"), comm=comm)


def _gate_specs(tm):
    n = COL_GL // D_MODEL
    return [pl.BlockSpec((tm, D_MODEL), lambda i, j=j: (i, n + j)) for j in range(3)]


def _merge_fwd(proj, ys, tm=256):
    T = proj.shape[0]
    tm = _tile(T, tm)
    row = pl.BlockSpec((tm, D_MODEL), lambda i: (i, 0))

    def body(g0, g1, g2, y0, y1, y2, o_ref):
        acc = jax.nn.sigmoid(g0[...]) * y0[...]
        acc = acc + jax.nn.sigmoid(g1[...]) * y1[...]
        acc = acc + jax.nn.sigmoid(g2[...]) * y2[...]
        o_ref[...] = acc.astype(o_ref.dtype)

    return pl.pallas_call(
        body, out_shape=jax.ShapeDtypeStruct((T, D_MODEL), BF16), grid=(T // tm,),
        in_specs=_gate_specs(tm) + [row] * 3, out_specs=row, name="merge_fwd",
        compiler_params=_params(("parallel",)),
    )(proj, proj, proj, *ys)


def _merge_bwd(proj, ys, d_merged, tm=256, comm=None):
    T = proj.shape[0]
    tm = _tile(T, tm)
    row = pl.BlockSpec((tm, D_MODEL), lambda i: (i, 0))

    def body(g0, g1, g2, y0, y1, y2, dm_ref, dgl_ref, d0, d1, d2):
        dm = dm_ref[...]
        for j, (g_ref, y_ref, d_ref) in enumerate(((g0, y0, d0), (g1, y1, d1), (g2, y2, d2))):
            sg = jax.nn.sigmoid(g_ref[...])
            d_ref[...] = (dm * sg).astype(d_ref.dtype)
            dgl_ref[:, j * D_MODEL:(j + 1) * D_MODEL] = (dm * y_ref[...] * sg * (1.0 - sg)).astype(dgl_ref.dtype)

    dy = jax.ShapeDtypeStruct((T, D_MODEL), BF16)
    return _pcall(
        body, (proj, proj, proj, *ys, d_merged), name="merge_bwd",
        out_shape=(jax.ShapeDtypeStruct((T, 3 * D_MODEL), BF16), dy, dy, dy), grid=(T // tm,),
        in_specs=_gate_specs(tm) + [row] * 4,
        out_specs=(pl.BlockSpec((tm, 3 * D_MODEL), lambda i: (i, 0)), row, row, row),
        sem=("parallel",), comm=comm)


def _gelu(x):
    return 0.5 * x * (1.0 + jnp.tanh(GELU_C * (x + GELU_A * x * x * x)))


def _gelu_grad(x):
    t = jnp.tanh(GELU_C * (x + GELU_A * x * x * x))
    return 0.5 * (1.0 + t) + 0.5 * x * (1.0 - t * t) * GELU_C * (1.0 + 3.0 * GELU_A * x * x)


def _shift_down(x, prev, n):
    rows = x.shape[0]
    r = lax.broadcasted_iota(jnp.int32, (8, 1), 0)
    rolled = pltpu.roll(x, n, axis=0)
    head = rolled[0:8]
    for j in range(n):
        head = jnp.where(r == j, prev[8 - n + j:8 - n + j + 1, :], head)
    return head if rows == 8 else jnp.concatenate([head, rolled[8:]], axis=0)


def _shift_up(x, nxt, n):
    rows = x.shape[0]
    r = lax.broadcasted_iota(jnp.int32, (8, 1), 0)
    rolled = pltpu.roll(x, rows - n, axis=0)
    tail = rolled[rows - 8:]
    for j in range(n):
        tail = jnp.where(r == 8 - n + j, nxt[j:j + 1, :], tail)
    return jnp.concatenate([rolled[:rows - 8], tail], axis=0)


def _conv(a, prev, cw, cb):
    return _shift_down(a, prev, 2) * cw[0:1, :] + _shift_down(a, prev, 1) * cw[1:2, :] + a * cw[2:3, :] + cb


def _glu_fwd(up, cw, cb, S, tm=256):
    T = up.shape[2]
    tm = _tile(S, tm)
    per_seq = S // tm

    def body(ab_ref, prev_ref, cw_ref, cb_ref, u_ref):
        i = pl.program_id(1)
        prev = jnp.where(i % per_seq == 0, 0.0, prev_ref[...])
        ac = _conv(ab_ref[0], prev, cw_ref[...], cb_ref[...])
        u_ref[...] = (_gelu(ac) * ab_ref[1]).astype(u_ref.dtype)

    return pl.pallas_call(
        body, out_shape=jax.ShapeDtypeStruct((FFN_SLABS, T, UP_SHARD), BF16), grid=(FFN_SLABS, T // tm),
        in_specs=[pl.BlockSpec((2, None, tm, UP_SHARD), lambda d, i: (0, d, i, 0)),
                  pl.BlockSpec((None, None, 8, UP_SHARD), lambda d, i: (0, d, jnp.maximum(i * (tm // 8) - 1, 0), 0)),
                  pl.BlockSpec((None, 3, UP_SHARD), lambda d, i: (d, 0, 0)),
                  pl.BlockSpec((None, 1, UP_SHARD), lambda d, i: (d, 0, 0))],
        out_specs=pl.BlockSpec((None, tm, UP_SHARD), lambda d, i: (d, i, 0)), name="glu_fwd",
        compiler_params=_params(("parallel", "parallel")),
    )(up, up, cw, cb)


def _glu_bwd(up, d_u, cw, cb, S, tm=256, comm=None):
    T = up.shape[2]
    tm = _tile(S, tm)
    per_seq = S // tm
    n_tiles = T // tm
    last8 = tm // 8

    def body(ab_ref, prev_ref, abn_ref, du_ref, dun_ref, cw_ref, cb_ref, dup_ref, dcw_ref, dcb_ref):
        i = pl.program_id(1)

        @pl.when(i == 0)
        def _():
            dcw_ref[...] = jnp.zeros_like(dcw_ref)
            dcb_ref[...] = jnp.zeros_like(dcb_ref)

        cw, cb = cw_ref[...], cb_ref[...]
        a, b = ab_ref[0], ab_ref[1]
        prev = jnp.where(i % per_seq == 0, 0.0, prev_ref[...])
        a2, a1 = _shift_down(a, prev, 2), _shift_down(a, prev, 1)
        ac = a2 * cw[0:1, :] + a1 * cw[1:2, :] + a * cw[2:3, :] + cb
        du = du_ref[...]
        dup_ref[1] = (du * _gelu(ac)).astype(dup_ref.dtype)
        dac = du * b * _gelu_grad(ac)
        dcb_ref[...] += jnp.sum(dac, axis=0, keepdims=True)
        dcw_ref[0:1, :] += jnp.sum(dac * a2, axis=0, keepdims=True)
        dcw_ref[1:2, :] += jnp.sum(dac * a1, axis=0, keepdims=True)
        dcw_ref[2:3, :] += jnp.sum(dac * a, axis=0, keepdims=True)
        an = abn_ref[0]
        acn = _conv(an, a[tm - 8:, :], cw, cb)
        dacn = jnp.where(i % per_seq == per_seq - 1, 0.0, dun_ref[...] * abn_ref[1] * _gelu_grad(acn))
        da = dac * cw[2:3, :] + _shift_up(dac, dacn, 1) * cw[1:2, :] + _shift_up(dac, dacn, 2) * cw[0:1, :]
        dup_ref[0] = da.astype(dup_ref.dtype)

    def nxt(i):
        return jnp.minimum((i + 1) * last8, T // 8 - 1)

    return _pcall(
        body, (up, up, up, d_u, d_u, cw, cb), name="glu_bwd",
        out_shape=(jax.ShapeDtypeStruct((2, FFN_SLABS, T, UP_SHARD), BF16),
                   jax.ShapeDtypeStruct((FFN_SLABS, 3, UP_SHARD), F32),
                   jax.ShapeDtypeStruct((FFN_SLABS, 1, UP_SHARD), F32)),
        grid=(FFN_SLABS, n_tiles),
        in_specs=[pl.BlockSpec((2, None, tm, UP_SHARD), lambda d, i: (0, d, i, 0)),
                  pl.BlockSpec((None, None, 8, UP_SHARD), lambda d, i: (0, d, jnp.maximum(i * last8 - 1, 0), 0)),
                  pl.BlockSpec((2, None, 8, UP_SHARD), lambda d, i: (0, d, nxt(i), 0)),
                  pl.BlockSpec((None, tm, UP_SHARD), lambda d, i: (d, i, 0)),
                  pl.BlockSpec((None, 8, UP_SHARD), lambda d, i: (d, nxt(i), 0)),
                  pl.BlockSpec((None, 3, UP_SHARD), lambda d, i: (d, 0, 0)),
                  pl.BlockSpec((None, 1, UP_SHARD), lambda d, i: (d, 0, 0))],
        out_specs=(pl.BlockSpec((2, None, tm, UP_SHARD), lambda d, i: (0, d, i, 0)),
                   pl.BlockSpec((None, 3, UP_SHARD), lambda d, i: (d, 0, 0)),
                   pl.BlockSpec((None, 1, UP_SHARD), lambda d, i: (d, 0, 0))),
        sem=("parallel", "arbitrary"), comm=comm)


def _mm_up(h2, w_up, tm=MM_ROWS):
    T, K = h2.shape
    tm = _tile(T, tm)
    return _matmul(
        "mm_up", "nn", h2, w_up, jax.ShapeDtypeStruct((N_DEV, T, UP_SHARD), F32), (N_DEV, T // tm, 1),
        pl.BlockSpec((tm, K), lambda j, i, k: (i, 0)), pl.BlockSpec((None, K, UP_SHARD), lambda j, i, k: (j, 0, 0)),
        pl.BlockSpec((None, tm, UP_SHARD), lambda j, i, k: (j, i, 0)), (tm, UP_SHARD))


def _mm_down(u, w_down, res, tm=MM_ROWS_RES):
    J, T, n = u.shape
    tm = _tile(T, tm)
    row = pl.BlockSpec((tm, D_MODEL), lambda i, d: (i, 0))
    return _matmul(
        "mm_down", "nn", u, w_down, jax.ShapeDtypeStruct((T, D_MODEL), F32), (T // tm, J),
        pl.BlockSpec((None, tm, n), lambda i, d: (d, i, 0)), pl.BlockSpec((None, n, D_MODEL), lambda i, d: (d, 0, 0)),
        row, (tm, D_MODEL), res, row)


def _mm_down_t(dx, w_down, tm=MM_ROWS):
    T = dx.shape[0]
    J, n, _ = w_down.shape
    tm = _tile(T, tm)
    return _matmul(
        "mm_down_t", "nt", dx, w_down, jax.ShapeDtypeStruct((J, T, n), F32), (J, T // tm, 1),
        pl.BlockSpec((tm, D_MODEL), lambda d, i, k: (i, 0)), pl.BlockSpec((None, n, D_MODEL), lambda d, i, k: (d, 0, 0)),
        pl.BlockSpec((None, tm, n), lambda d, i, k: (d, i, 0)), (tm, n))


def _mm_dw_down(u, dx, tk=MM_TOKENS):
    J, T, n = u.shape
    tk = _tile(T, tk)
    return _matmul(
        "mm_dw_down", "tn", u, dx, jax.ShapeDtypeStruct((J, n, D_MODEL), BF16), (J, T // tk),
        pl.BlockSpec((None, tk, n), lambda d, k: (d, k, 0)), pl.BlockSpec((tk, D_MODEL), lambda d, k: (k, 0)),
        pl.BlockSpec((None, n, D_MODEL), lambda d, k: (d, 0, 0)), (n, D_MODEL))


def _mm_dw_up(name, h2, d_up, part, tk=MM_TOKENS, comm=None):
    T, K = h2.shape
    p, of = part
    K //= of
    tk = _tile(T, tk)
    return _matmul(
        name, "tn", h2, d_up, jax.ShapeDtypeStruct((N_DEV, K, UP_SHARD), BF16), (N_DEV, T // tk),
        pl.BlockSpec((tk, K), lambda j, k: (k, p)), pl.BlockSpec((None, tk, UP_SHARD), lambda j, k: (j, k, 0)),
        pl.BlockSpec((None, K, UP_SHARD), lambda j, k: (j, 0, 0)), (K, UP_SHARD), comm=comm)


def _mm_up_t(d_up, w_up, tm=MM_ROWS, comm=None):
    J, T, n = d_up.shape
    K = w_up.shape[1]
    tm = _tile(T, tm)
    return _matmul(
        "mm_up_t", "nt", d_up, w_up, jax.ShapeDtypeStruct((T, K), F32), (T // tm, J),
        pl.BlockSpec((None, tm, n), lambda i, j: (j, i, 0)), pl.BlockSpec((None, K, n), lambda i, j: (j, 0, 0)),
        pl.BlockSpec((tm, K), lambda i, j: (i, 0)), (tm, K), comm=comm)


def _loss_head(x2, target, g_final, tm=512):
    T, Dm = x2.shape
    tm = _tile(T, tm)

    def body(x_ref, t_ref, g_ref, dx_ref, dg_ref, loss_ref):
        @pl.when(pl.program_id(0) == 0)
        def _():
            dg_ref[...] = jnp.zeros_like(dg_ref)
            loss_ref[...] = jnp.zeros_like(loss_ref)

        xv = x_ref[...]
        r = lax.rsqrt(jnp.mean(xv * xv, axis=-1, keepdims=True) + EPS)
        xhat = xv * r
        err = xhat * g_ref[...] - t_ref[...]
        loss_ref[...] += (0.5 / Dm) * jnp.sum(err * err)
        dy = err * (1.0 / Dm)
        dg_ref[...] += jnp.sum(dy * xhat, axis=0, keepdims=True)
        dxhat = dy * g_ref[...]
        dx_ref[...] = r * (dxhat - xhat * jnp.mean(dxhat * xhat, axis=-1, keepdims=True))

    row = pl.BlockSpec((tm, Dm), lambda i: (i, 0))
    vec = pl.BlockSpec((1, Dm), lambda i: (0, 0))
    return pl.pallas_call(
        body,
        out_shape=(jax.ShapeDtypeStruct((T, Dm), F32), jax.ShapeDtypeStruct((1, Dm), F32),
                   jax.ShapeDtypeStruct((1, Dm), F32)),
        grid=(T // tm,), in_specs=[row, row, vec], out_specs=(row, vec, vec), name="loss_head",
        compiler_params=_params(("arbitrary",)),
    )(x2, target, g_final)


def _cast_shards(shards):
    def body(*refs):
        n = len(refs) // 2
        for src, dst in zip(refs[:n], refs[n:]):
            dst[...] = src[...].astype(dst.dtype)

    return pl.pallas_call(
        body, out_shape=[jax.ShapeDtypeStruct(s.shape, BF16) for s in shards], name="cast_shards",
        compiler_params=pltpu.CompilerParams(vmem_limit_bytes=VMEM_LIMIT),
    )(*shards)


def _adamw(w, g, m, v):
    m = ADAM_B1 * m + (1.0 - ADAM_B1) * g
    v = ADAM_B2 * v + (1.0 - ADAM_B2) * (g * g)
    m_hat = m / (1.0 - ADAM_B1 ** ADAM_STEP)
    v_hat = v / (1.0 - ADAM_B2 ** ADAM_STEP)
    delta = -ADAM_LR * (m_hat / (jnp.sqrt(v_hat) + ADAM_EPS) + ADAM_WD * w)
    return delta, m, v


def _sum_parts(p_ref):
    g = p_ref[0].astype(F32)
    for d in range(1, N_DEV):
        g = g + p_ref[d].astype(F32)
    return g


def _reduce_adam(name, parts, w, m, v, tr=128):
    R, Cn = w.shape
    by_rows = sum(p.shape[1] for p in parts) == R and len(parts) > 1
    tr = math.gcd(tr, *[p.shape[1] for p in parts])
    n_tiles = [p.shape[1] // tr for p in parts]
    first = [sum(n_tiles[:j]) for j in range(len(parts))] if by_rows else [0] * len(parts)

    def body(*refs):
        p_refs = refs[:len(parts)]
        w_ref, m_ref, v_ref, g_out, d_out, m_out, v_out = refs[len(parts):]

        def update(p_ref):
            g = _sum_parts(p_ref)
            delta, m_new, v_new = _adamw(w_ref[...], g, m_ref[...], v_ref[...])
            g_out[...] = g
            d_out[...] = delta
            m_out[...] = m_new
            v_out[...] = v_new

        if len(parts) == 1:
            update(p_refs[0])
        elif by_rows:
            i = pl.program_id(0)
            for p_ref, t0, n in zip(p_refs, first, n_tiles):
                pl.when((i >= t0) & (i < t0 + n))(functools.partial(update, p_ref))
        else:
            c = lax.axis_index("c")
            for side, p_ref in enumerate(p_refs):
                pl.when(c == side)(functools.partial(update, p_ref))

    def part_spec(t0, n):
        return pl.BlockSpec((N_DEV, tr, Cn), lambda i: (0, jnp.clip(i - t0, 0, n - 1), 0))

    row = pl.BlockSpec((tr, Cn), lambda i: (i, 0))
    shape = jax.ShapeDtypeStruct((R, Cn), F32)
    return pl.pallas_call(
        body, out_shape=(shape,) * 4, grid=(R // tr,),
        in_specs=[part_spec(t0, n) for t0, n in zip(first, n_tiles)] + [row, row, row],
        out_specs=(row,) * 4, name=name, compiler_params=_params(("parallel",)),
    )(*parts, w, m, v)


def _small_adam(name, gathered, params):
    n_g, n_p = len(gathered), len(params)

    def body(*refs):
        g_refs = refs[:n_g]
        wmv = refs[n_g:n_g + 3 * n_p]
        sums = refs[n_g + 3 * n_p:2 * n_g + 3 * n_p]
        upd = refs[2 * n_g + 3 * n_p:]
        for j in range(n_g):
            g = _sum_parts(g_refs[j])
            sums[j][...] = g
            if j < n_p:
                w_ref, m_ref, v_ref = wmv[3 * j:3 * j + 3]
                delta, m_new, v_new = _adamw(w_ref[...], g, m_ref[...], v_ref[...])
                upd[3 * j][...] = delta
                upd[3 * j + 1][...] = m_new
                upd[3 * j + 2][...] = v_new

    flat = [a for wmv in params for a in wmv]
    out_shape = [jax.ShapeDtypeStruct(g.shape[1:], F32) for g in gathered]
    out_shape += [jax.ShapeDtypeStruct(a.shape, F32) for a in flat]
    res = pl.pallas_call(body, out_shape=out_shape, name=name)(*gathered, *flat)
    return res[:n_g], [tuple(res[n_g + 3 * j:n_g + 3 * j + 3]) for j in range(n_p)]


def _adam_only(name, g, w, m, v):
    def body(g_ref, w_ref, m_ref, v_ref, d_out, m_out, v_out):
        delta, m_new, v_new = _adamw(w_ref[...], g_ref[...], m_ref[...], v_ref[...])
        d_out[...] = delta
        m_out[...] = m_new
        v_out[...] = v_new

    shape = jax.ShapeDtypeStruct(w.shape, F32)
    return pl.pallas_call(body, out_shape=(shape,) * 3, name=name)(g, w, m, v)


def kernel(x, mem, g_mix, w_in, w_pool, pool_scale, w_a, g_ret, b_ret, w_r, g_mem, w_mem_kv, w_c, w_out, g_ffn, w_up, conv_w, conv_b, w_down, g_final, loss_target, m_g_mix, m_w_in, m_w_pool, m_pool_scale, m_w_a, m_g_ret, m_b_ret, m_w_r, m_g_mem, m_w_mem_kv, m_w_c, m_w_out, m_g_ffn, m_w_up, m_conv_w, m_conv_b, m_w_down, m_g_final, v_g_mix, v_w_in, v_w_pool, v_pool_scale, v_w_a, v_g_ret, v_b_ret, v_w_r, v_g_mem, v_w_mem_kv, v_w_c, v_w_out, v_g_ffn, v_w_up, v_conv_w, v_conv_b, v_w_down, v_g_final):
    B, S, _ = x.shape
    M = mem.shape[1]
    T = B * S
    me = _my_index()
    x2d = x.reshape(T, D_MODEL)
    mem2d = mem.reshape(B * M, D_MODEL)
    tgt2d = loss_target.reshape(T, D_MODEL)
    g_final2 = g_final.reshape(1, D_MODEL)

    big = dict(w_in=w_in[0], w_a=w_a[0], w_r=w_r[0], w_mem_kv=w_mem_kv[0], w_c=w_c[0], w_out=w_out[0],
               w_up=w_up[0], w_down=w_down[0])
    names = list(big)
    cast = dict(zip(names, _cast_shards([big[n] for n in names])))
    cb = conv_b[0].reshape(FFN_SLABS, 1, UP_SHARD)
    wp = w_pool[0]
    tables = _ret_tables(S)

    h = _rms_fwd("rms_mix", x2d, g_mix)
    early = ("w_a", "w_r", "w_mem_kv", "w_c", "w_out", "w_down")
    (proj, Win), landed = _mm_in_gather(h, cast["w_in"], comm=_Gather([cast[n] for n in early] + [conv_w[0]]))
    W = dict(zip(early, landed))
    cw_full = landed[-1].transpose(1, 0, 2).reshape(3, FFN_HIDDEN)
    cw = cw_full.reshape(3, FFN_SLABS, UP_SHARD).transpose(1, 0, 2)
    Wa = W["w_a"].transpose(1, 0, 2).reshape(POOL_WIDTH, D_MODEL)
    Wc = W["w_c"].transpose(1, 0, 2).reshape(XA_WIDTH, D_MODEL)
    Wr = W["w_r"].reshape(D_MODEL, D_MODEL)
    Wkv = W["w_mem_kv"].reshape(D_MODEL, D_MODEL)
    Wout = W["w_out"].reshape(D_MODEL, D_MODEL)
    Wdown = W["w_down"].reshape(FFN_SLABS, UP_SHARD, D_MODEL)
    ypre = _pool_fwd(proj, wp, pool_scale, B, S)
    y_pool = _mm_rows("mm_a", ypre, Wa)
    (yr, ret_states), (Wup,) = _ret_fwd(proj, g_ret, b_ret, tables, B, S, comm=_Gather([cast["w_up"]]))
    y_ret = _mm_rows("mm_r", yr, Wr)
    mem_n = _rms_fwd("rms_mem", mem2d, g_mem)
    kv = _mm_rows("mm_kv", mem_n, Wkv)
    o_mem = _xa_fwd(proj, kv, B, S, M)
    y_mem = _mm_rows("mm_c", o_mem, Wc)
    ys = (y_pool, y_ret, y_mem)
    merged = _merge_fwd(proj, ys)
    x1 = _mm_rows("mm_out", merged, Wout, res=x2d, tm=MM_ROWS_RES)
    h2 = _rms_fwd("rms_ffn", x1, g_ffn)
    up = _mm_up(h2, Wup).reshape(2, FFN_SLABS, T, UP_SHARD)
    u = _glu_fwd(up, cw, cb, S)
    x2 = _mm_down(u, Wdown, x1)

    dx2, dg_final, loss_part = _loss_head(x2, tgt2d, g_final2)
    received = {}
    d_u = _mm_down_t(dx2, Wdown)
    dW_down = _mm_dw_down(u, dx2)
    (d_up, d_cw, d_cb), (received["w_down"],) = _glu_bwd(
        up, d_u, cw, cb, S, comm=_Exchange([dW_down.reshape(N_DEV, -1, D_MODEL)]))
    d_up = d_up.reshape(N_DEV, T, UP_SHARD)
    dW_up = _mm_dw_up("mm_dw_up", h2, d_up, (0, 1))
    d_h2, (up_c0,) = _mm_up_t(d_up, Wup, comm=_ExchangeTo([dW_up], 0))
    dx1, dg_ffn = _rms_bwd("rms_ffn_bwd", x1, g_ffn, d_h2, dx2)
    d_merged = _mm_rows("mm_out_t", dx1, Wout, kind="nt")
    dW_out = _mm_tn("mm_dw_out", merged, dx1, BF16)
    (d_gl, d_y_pool, d_y_ret, d_y_mem), (up_c1,) = _merge_bwd(proj, ys, d_merged, comm=_ExchangeTo([dW_up], 1))
    received["w_up"] = [up_c0, up_c1]
    dW_c = _mm_tn("mm_dw_c", o_mem, d_y_mem, BF16)
    d_o_mem = _mm_rows("mm_c_t", d_y_mem, Wc, kind="nt")
    (d_qx, d_kmem, d_vmem), (received["w_out"],) = _xa_bwd(
        proj, kv, d_o_mem, B, S, M, comm=_Exchange([dW_out.reshape(N_DEV, -1, D_MODEL)]))
    d_kv = jnp.concatenate([d_kmem, d_vmem], axis=1)
    dW_kv = _mm_tn("mm_dw_kv", mem_n, d_kv, BF16)
    d_mem_n = _mm_rows("mm_kv_t", d_kv, Wkv, kind="nt")
    dg_mem = _rms_bwd("rms_mem_bwd", mem2d, g_mem, d_mem_n, None)
    dW_a = _mm_tn("mm_dw_a", ypre, d_y_pool, BF16)
    d_ypre = _mm_rows("mm_a_t", d_y_pool, Wa, kind="nt")
    d_hp, dw_pool, d_scale = _pool_bwd(proj, d_ypre, wp, pool_scale, B, S)
    dW_r = _mm_tn("mm_dw_r", yr, d_y_ret, BF16)
    d_yr = _mm_rows("mm_r_t", d_y_ret, Wr, kind="nt")
    (d_q, d_k, d_v, d_gr, dg_ret, db_ret), landed = _ret_bwd(
        proj, ret_states, d_yr, g_ret, b_ret, tables, B, S,
        comm=_Exchange([dW_a.reshape(POOL_WIDTH, N_DEV, -1).transpose(1, 0, 2), dW_r.reshape(N_DEV, -1, D_MODEL),
                        dW_c.reshape(XA_WIDTH, N_DEV, -1).transpose(1, 0, 2), dW_kv.reshape(N_DEV, -1, D_MODEL)]))
    received["w_a"], received["w_r"], received["w_c"], received["w_mem_kv"] = landed
    small_names = ["w_pool", "pool_scale", "g_ret", "b_ret", "g_mem", "g_ffn", "conv_b", "g_final"]
    small_grads = [dw_pool, d_scale, dg_ret, db_ret, dg_mem, dg_ffn, d_cb.reshape(1, FFN_HIDDEN), dg_final,
                   d_cw.transpose(1, 0, 2).reshape(3, FFN_HIDDEN), loss_part]
    d_proj = jnp.concatenate([d_hp, d_q, d_k, d_v, d_gr, d_qx, d_gl], axis=1)
    dW_in0 = _mm_tn_slab("mm_dw_in0", h, d_proj, IN_SHARD, BF16, part=(0, 2))
    dW_in1, (in0,) = _mm_tn_slab("mm_dw_in1", h, d_proj, IN_SHARD, BF16, part=(1, 2), comm=_Exchange([dW_in0]))
    d_h, (in1, *small_all) = _mm_cols_slab_t("mm_in_t", d_proj, Win, comm=_Exchange([dW_in1], whole=small_grads))
    received["w_in"] = [in0, in1]
    grad_x, dg_mix = _rms_bwd("rms_mix_bwd", x2d, g_mix, d_h, dx1)
    (g_mix_all,) = _comm_call("gather_g_mix", _Exchange([], whole=[dg_mix]))

    args = dict(g_mix=g_mix, w_in=w_in, w_pool=w_pool, pool_scale=pool_scale, w_a=w_a, g_ret=g_ret, b_ret=b_ret,
                w_r=w_r, g_mem=g_mem, w_mem_kv=w_mem_kv, w_c=w_c, w_out=w_out, g_ffn=g_ffn, w_up=w_up,
                conv_w=conv_w, conv_b=conv_b, w_down=w_down, g_final=g_final)
    m_in = dict(g_mix=m_g_mix, w_in=m_w_in, w_pool=m_w_pool, pool_scale=m_pool_scale, w_a=m_w_a, g_ret=m_g_ret,
                b_ret=m_b_ret, w_r=m_w_r, g_mem=m_g_mem, w_mem_kv=m_w_mem_kv, w_c=m_w_c, w_out=m_w_out,
                g_ffn=m_g_ffn, w_up=m_w_up, conv_w=m_conv_w, conv_b=m_conv_b, w_down=m_w_down, g_final=m_g_final)
    v_in = dict(g_mix=v_g_mix, w_in=v_w_in, w_pool=v_w_pool, pool_scale=v_pool_scale, w_a=v_w_a, g_ret=v_g_ret,
                b_ret=v_b_ret, w_r=v_w_r, g_mem=v_g_mem, w_mem_kv=v_w_mem_kv, w_c=v_w_c, w_out=v_w_out,
                g_ffn=v_g_ffn, w_up=v_w_up, conv_w=v_conv_w, conv_b=v_conv_b, w_down=v_w_down, g_final=v_g_final)

    grads, deltas, new_m, new_v = {}, {}, {}, {}
    for n in names:
        shard = big[n].shape
        parts = received[n] if isinstance(received[n], list) else [received[n]]
        outs = _reduce_adam("adam_" + n, parts, big[n], m_in[n][0], v_in[n][0])
        for store, val in zip((grads, deltas, new_m, new_v), outs):
            store[n] = val.reshape((1,) + shard)

    def as_small(a):
        return a.reshape(a.shape[-3:]) if a.ndim > 2 else a.reshape(1, -1)

    def small_update(call_name, param_names, gathered):
        params = [tuple(as_small(d[n]) for d in (args, m_in, v_in)) for n in param_names]
        sums, updates = _small_adam(call_name, gathered, params)
        for n, g, (d_, m_, v_) in zip(param_names, sums, updates):
            shape = args[n].shape
            grads[n], deltas[n], new_m[n], new_v[n] = (a.reshape(shape) for a in (g, d_, m_, v_))
        return sums[len(param_names):]

    g_cw_full, loss_row = small_update("adam_small", small_names, small_all)
    loss = loss_row[0, 0]
    small_update("adam_g_mix", ["g_mix"], [g_mix_all])

    shard_cols = FFN_HIDDEN // N_DEV
    g_cw = lax.dynamic_slice_in_dim(g_cw_full, me * shard_cols, shard_cols, axis=1)
    d_, m_, v_ = _adam_only("adam_conv_w", g_cw, conv_w[0], m_conv_w[0], v_conv_w[0])
    grads["conv_w"], deltas["conv_w"], new_m["conv_w"], new_v["conv_w"] = g_cw[None], d_[None], m_[None], v_[None]

    order = ["g_mix", "w_in", "w_pool", "pool_scale", "w_a", "g_ret", "b_ret", "w_r", "g_mem", "w_mem_kv", "w_c",
             "w_out", "g_ffn", "w_up", "conv_w", "conv_b", "w_down", "g_final"]
    return (loss, grad_x.reshape(B, S, D_MODEL), *[grads[n] for n in order], *[deltas[n] for n in order],
            *[new_m[n] for n in order], *[new_v[n] for n in order])
```

```python
import functools
import math

import jax
import jax.numpy as jnp
from jax import lax
from jax.experimental import pallas as pl
from jax.experimental.pallas import tpu as pltpu

F32 = jnp.float32
BF16 = jnp.bfloat16

N_DEV = 8
D_MODEL = 1024
POOL_WINDOWS = (2, 4, 8, 16)
POOL_GROUP_DIM = 128
POOL_WIDTH = 512
POOL_HALO = 16
RET_HEADS = 4
RET_QK_DIM = 128
RET_V_DIM = 256
RET_CHUNK = 128
ROPE_BASE = 10000.0
XA_HEADS = 4
XA_HEAD_DIM = 128
XA_WIDTH = 512
IN_WIDTH = 7168
IN_SHARD = IN_WIDTH // N_DEV
FFN_HIDDEN = 2816
UP_SHARD = 2 * FFN_HIDDEN // N_DEV
FFN_SLABS = FFN_HIDDEN // UP_SHARD
EPS = 1e-6
ADAM_LR = 0.001
ADAM_B1 = 0.9
ADAM_B2 = 0.999
ADAM_EPS = 1e-08
ADAM_WD = 0.01
ADAM_STEP = 10
GELU_C = math.sqrt(2.0 / math.pi)
GELU_A = 0.044715
VMEM_LIMIT = 56 * 1024 * 1024
MM_ROWS = 2048
MM_ROWS_RES = 1024
MM_TOKENS = 2048
MESH = pl.DeviceIdType.MESH

COL_Q, COL_K, COL_V, COL_GR, COL_QX, COL_GL = 512, 1024, 1536, 2560, 3584, 4096

_DIMS = {
    "nn": (((1,), (0,)), ((), ())),
    "nt": (((1,), (1,)), ((), ())),
    "tn": (((0,), (0,)), ((), ())),
}


def _dot(a, b, kind="nn"):
    return lax.dot_general(a.astype(BF16), b.astype(BF16), _DIMS[kind], preferred_element_type=F32)


def _params(sem, vmem=VMEM_LIMIT):
    return pltpu.CompilerParams(dimension_semantics=sem, vmem_limit_bytes=vmem)


def _tile(n, pref):
    t = min(n, pref)
    while n % t:
        t //= 2
    return t


def _mesh_pos():
    return lax.axis_index("x"), lax.axis_index("y"), lax.axis_index("c")


def _dev_index(x, y, c):
    return 4 * x + 2 * y + c


def _my_index():
    return _dev_index(*_mesh_pos())


def _remote(src, dst, send_sems, recv_sems, s, to):
    return pltpu.make_async_remote_copy(src_ref=src, dst_ref=dst, send_sem=send_sems.at[s], recv_sem=recv_sems.at[s],
                                        device_id=to, device_id_type=MESH)


class _Gather:
    def __init__(self, shards):
        self.inputs = list(shards)
        self.out_shapes = [jax.ShapeDtypeStruct((N_DEV,) + s.shape, s.dtype) for s in shards]
        n = len(shards)
        self.sem_shapes = [pltpu.SemaphoreType.DMA((7 * n,)), pltpu.SemaphoreType.DMA((7 * n,)),
                           pltpu.SemaphoreType.DMA((n,))]

    def _places(self):
        x, y, c = _mesh_pos()
        return (x, y, c), (x, y, 1 - c), [(1 - x, y), (x, 1 - y), (1 - x, 1 - y)]

    def _local(self, src, dst, sems):
        me = _my_index()
        return [pltpu.make_async_copy(src[w], dst[w].at[me], sems[2].at[w]) for w in range(len(src))]

    def start(self, src, dst, sems):
        me, sib, chips = self._places()
        for cp in self._local(src, dst, sems):
            cp.start()
        for w in range(len(src)):
            land = dst[w].at[_dev_index(*me)]
            _remote(src[w], land, sems[0], sems[1], 7 * w, sib).start()
            for j, chip in enumerate(chips):
                _remote(src[w], land, sems[0], sems[1], 7 * w + 1 + j, (*chip, me[2])).start()

    def finish(self, src, dst, sems):
        me, sib, chips = self._places()
        n = len(src)
        for j, chip in enumerate(chips):
            for w in range(n):
                block = dst[w].at[_dev_index(*chip, me[2])]
                _remote(src[w], block, sems[0], sems[1], 7 * w + 1 + j, me).wait_recv()
                _remote(block, block, sems[0], sems[1], 7 * w + 4 + j, sib).start()
        for w in range(n):
            _remote(src[w], dst[w].at[_dev_index(*sib)], sems[0], sems[1], 7 * w, me).wait_recv()
            for j, chip in enumerate(chips):
                block = dst[w].at[_dev_index(*chip, sib[2])]
                _remote(block, block, sems[0], sems[1], 7 * w + 4 + j, me).wait_recv()
            for k in range(7):
                _remote(src[w], dst[w].at[0], sems[0], sems[1], 7 * w + k, me).wait_send()
        for cp in self._local(src, dst, sems):
            cp.wait()


class _Exchange:
    def __init__(self, partials, whole=()):
        self.n_part = len(partials)
        self.inputs = list(partials) + list(whole)
        self.out_shapes = [jax.ShapeDtypeStruct(p.shape, p.dtype) for p in partials]
        self.out_shapes += [jax.ShapeDtypeStruct((N_DEV,) + a.shape, a.dtype) for a in whole]
        n = len(self.inputs)
        self.sem_shapes = [pltpu.SemaphoreType.DMA((7 * n,)), pltpu.SemaphoreType.DMA((7 * n,)),
                           pltpu.SemaphoreType.DMA((n,))]

    def _peer(self, k):
        x, y, c = _mesh_pos()
        p = (x ^ ((k >> 2) & 1), y ^ ((k >> 1) & 1), c ^ (k & 1))
        return p, _dev_index(*p)

    def _source(self, src, w, slot):
        return src[w].at[slot] if w < self.n_part else src[w]

    def _local(self, src, dst, sems):
        me = _my_index()
        return [pltpu.make_async_copy(self._source(src, w, me), dst[w].at[me], sems[2].at[w])
                for w in range(len(src))]

    def start(self, src, dst, sems):
        me = _my_index()
        for cp in self._local(src, dst, sems):
            cp.start()
        for k in range(1, N_DEV):
            peer, peer_idx = self._peer(k)
            for w in range(len(src)):
                _remote(self._source(src, w, peer_idx), dst[w].at[me], sems[0], sems[1], 7 * w + k - 1, peer).start()

    def finish(self, src, dst, sems):
        for k in range(1, N_DEV):
            peer, peer_idx = self._peer(k)
            for w in range(len(src)):
                cp = _remote(self._source(src, w, peer_idx), dst[w].at[peer_idx], sems[0], sems[1], 7 * w + k - 1, peer)
                cp.wait_send()
                cp.wait_recv()
        for cp in self._local(src, dst, sems):
            cp.wait()


class _ExchangeTo:
    def __init__(self, partials, side):
        self.side = side
        self.inputs = list(partials)
        self.out_shapes = [jax.ShapeDtypeStruct(p.shape, p.dtype) for p in partials]
        n = len(partials)
        self.sem_shapes = [pltpu.SemaphoreType.DMA((7 * n,)), pltpu.SemaphoreType.DMA((7 * n,)),
                           pltpu.SemaphoreType.DMA((n,))]

    def _copies(self, src, dst, sems):
        x, y, c = _mesh_pos()
        me = _dev_index(x, y, c)
        receives = c == self.side
        remote = []
        for k in range(1, N_DEV):
            kx, ky, kc = (k >> 2) & 1, (k >> 1) & 1, k & 1
            peer = (x ^ kx, y ^ ky, c ^ kc)
            peer_idx = _dev_index(*peer)
            sends = c == (self.side ^ kc)
            for w in range(len(src)):
                slab = src[w].at[peer_idx]
                s = 7 * w + k - 1
                remote.append((sends, _remote(slab, dst[w].at[me], sems[0], sems[1], s, peer),
                               _remote(slab, dst[w].at[peer_idx], sems[0], sems[1], s, peer)))
        local = [pltpu.make_async_copy(src[w].at[me], dst[w].at[me], sems[2].at[w]) for w in range(len(src))]
        return receives, remote, local

    def start(self, src, dst, sems):
        receives, remote, local = self._copies(src, dst, sems)

        @pl.when(receives)
        def _():
            for cp in local:
                cp.start()

        for sends, send, _ in remote:
            pl.when(sends)(send.start)

    def finish(self, src, dst, sems):
        receives, remote, local = self._copies(src, dst, sems)
        for sends, send, arrive in remote:
            pl.when(sends)(send.wait_send)
            pl.when(receives)(arrive.wait_recv)

        @pl.when(receives)
        def _():
            for cp in local:
                cp.wait()


class _Both:
    def __init__(self, first, second):
        self.plans = (first, second)
        self.inputs = first.inputs + second.inputs
        self.out_shapes = first.out_shapes + second.out_shapes
        self.sem_shapes = first.sem_shapes + second.sem_shapes

    def _split(self, src, dst, sems):
        a = self.plans[0]
        ni, no = len(a.inputs), len(a.out_shapes)
        return (src[:ni], dst[:no], sems[:3]), (src[ni:], dst[no:], sems[3:])

    def start(self, src, dst, sems):
        for plan, part in zip(self.plans, self._split(src, dst, sems)):
            plan.start(*part)

    def finish(self, src, dst, sems):
        for plan, part in zip(self.plans, self._split(src, dst, sems)):
            plan.finish(*part)


def _pcall(body, args, *, name, out_shape, grid, in_specs, out_specs, scratch_shapes=(), sem=None, comm=None):
    single = not isinstance(out_shape, (tuple, list))
    outs = [out_shape] if single else list(out_shape)
    ospecs = [out_specs] if single else list(out_specs)
    n_in, n_out, n_scr = len(args), len(outs), len(scratch_shapes)

    def pick(res):
        return res[0] if single else tuple(res[:n_out])

    if comm is None:
        res = pl.pallas_call(
            body, out_shape=outs, grid=grid, in_specs=list(in_specs), out_specs=ospecs,
            scratch_shapes=list(scratch_shapes), name=name, compiler_params=_params(sem),
        )(*args)
        return pick(res), ()

    nci, nco = len(comm.inputs), len(comm.out_shapes)

    def carrier(*refs):
        at = 0
        parts = []
        for size in (n_in, nci, n_out, nco, n_scr, len(comm.sem_shapes)):
            parts.append(refs[at:at + size])
            at += size
        ins, cins, o, couts, scr, sems = parts
        ids = [pl.program_id(a) for a in range(len(grid))]
        first = functools.reduce(jnp.logical_and, [i == 0 for i in ids])
        last = functools.reduce(jnp.logical_and, [i == g - 1 for i, g in zip(ids, grid)])

        body(*ins, *o, *scr)

        @pl.when(first)
        def _():
            comm.start(cins, couts, sems)

        @pl.when(last)
        def _():
            comm.finish(cins, couts, sems)

    hbm = pl.BlockSpec(memory_space=pltpu.HBM)
    res = pl.pallas_call(
        carrier, out_shape=outs + comm.out_shapes, grid=grid, in_specs=list(in_specs) + [hbm] * nci,
        out_specs=ospecs + [hbm] * nco, scratch_shapes=list(scratch_shapes) + comm.sem_shapes, name=name,
        compiler_params=_params(("arbitrary",) * len(grid)),
    )(*args, *comm.inputs)
    return pick(res), tuple(res[n_out:])


def _comm_call(name, comm):
    def body(*refs):
        nci, nco = len(comm.inputs), len(comm.out_shapes)
        cins, couts, sems = refs[:nci], refs[nci:nci + nco], refs[nci + nco:]
        comm.start(cins, couts, sems)
        comm.finish(cins, couts, sems)

    hbm = pl.BlockSpec(memory_space=pltpu.HBM)
    return pl.pallas_call(
        body, out_shape=comm.out_shapes, in_specs=[hbm] * len(comm.inputs), out_specs=[hbm] * len(comm.out_shapes),
        scratch_shapes=comm.sem_shapes, name=name,
    )(*comm.inputs)


def _matmul(name, kind, a, b, out_shape, grid, a_spec, b_spec, o_spec, acc_shape, res=None, res_spec=None,
            comm=None):
    nk = grid[-1]
    has_res = res is not None

    def body(*refs):
        a_ref, b_ref = refs[0], refs[1]
        o_ref = refs[2 + has_res]

        def prod():
            return _dot(a_ref[...], b_ref[...], kind)

        def finish(acc):
            if has_res:
                acc = acc + refs[2][...]
            o_ref[...] = acc.astype(o_ref.dtype)

        if nk == 1:
            finish(prod())
        else:
            acc_ref = refs[3 + has_res]
            k = pl.program_id(len(grid) - 1)

            @pl.when(k == 0)
            def _():
                acc_ref[...] = prod()

            @pl.when(k > 0)
            def _():
                acc_ref[...] += prod()

            @pl.when(k == nk - 1)
            def _():
                finish(acc_ref[...])

    in_specs = [a_spec, b_spec] + ([res_spec] if has_res else [])
    args = (a, b) + ((res,) if has_res else ())
    scratch = [pltpu.VMEM(acc_shape, F32)] if nk > 1 else []
    sem = ("parallel",) * (len(grid) - 1) + ("arbitrary",)
    out, landed = _pcall(body, args, name=name, out_shape=out_shape, grid=grid, in_specs=in_specs,
                         out_specs=o_spec, scratch_shapes=scratch, sem=sem, comm=comm)
    return out if comm is None else (out, landed)


def _mm_rows(name, a, w, out_dtype=F32, res=None, kind="nn", tm=MM_ROWS, comm=None):
    M, K = a.shape
    N = w.shape[1] if kind == "nn" else w.shape[0]
    tm = _tile(M, tm)
    res_spec = pl.BlockSpec((tm, N), lambda i, k: (i, 0)) if res is not None else None
    return _matmul(
        name, kind, a, w, jax.ShapeDtypeStruct((M, N), out_dtype), (M // tm, 1),
        pl.BlockSpec((tm, K), lambda i, k: (i, 0)), pl.BlockSpec(w.shape, lambda i, k: (0, 0)),
        pl.BlockSpec((tm, N), lambda i, k: (i, 0)), (tm, N), res, res_spec, comm)


def _mm_tn(name, a, b, out_dtype=F32, tk=MM_TOKENS, comm=None):
    T, M = a.shape
    N = b.shape[1]
    tk = _tile(T, tk)
    return _matmul(
        name, "tn", a, b, jax.ShapeDtypeStruct((M, N), out_dtype), (1, T // tk),
        pl.BlockSpec((tk, M), lambda i, k: (k, 0)), pl.BlockSpec((tk, N), lambda i, k: (k, 0)),
        pl.BlockSpec((M, N), lambda i, k: (0, 0)), (M, N), comm=comm)


def _mm_cols_slab(name, a, w_slabs, out_dtype=F32, tm=MM_ROWS, comm=None):
    M, K = a.shape
    J, _, n = w_slabs.shape
    tm = _tile(M, tm)
    return _matmul(
        name, "nn", a, w_slabs, jax.ShapeDtypeStruct((M, J * n), out_dtype), (J, M // tm, 1),
        pl.BlockSpec((tm, K), lambda j, i, k: (i, 0)), pl.BlockSpec((None, K, n), lambda j, i, k: (j, 0, 0)),
        pl.BlockSpec((tm, n), lambda j, i, k: (i, j)), (tm, n), comm=comm)


def _mm_in_gather(h, shard, tm=MM_ROWS, comm=None):
    T, K = h.shape
    n = shard.shape[1]
    tm = _tile(T, tm)
    n_tiles = T // tm
    pair_of_chip_step = {4: 1, 2: 2, 6: 3}

    def slab_of(s):
        x, y, c = _mesh_pos()
        return _dev_index(x ^ ((s >> 2) & 1), y ^ ((s >> 1) & 1), c ^ (s & 1))

    def body(h_ref, shard_ref, proj_ref, win_ref, wbuf, slot_sems, send_sems, recv_sems, local_sem):
        s, i = pl.program_id(0), pl.program_id(1)
        x, y, c = _mesh_pos()
        me, sib = (x, y, c), (x, y, 1 - c)

        def slot_copy(step):
            src = shard_ref if step == 0 else win_ref.at[slab_of(step)]
            return pltpu.make_async_copy(src, wbuf.at[step % 2], slot_sems.at[step % 2])

        def fetch(step):
            if step >= 1:
                block = win_ref.at[slab_of(step)]
                if step == 1:
                    pair = 0
                elif step % 2 == 0:
                    pair = pair_of_chip_step[step]
                else:
                    pair = 3 + pair_of_chip_step[step - 1]
                _remote(block, block, send_sems, recv_sems, pair, me).wait_recv()
                if step % 2 == 0:
                    _remote(block, block, send_sems, recv_sems, 3 + pair, sib).start()
            slot_copy(step).start()

        @pl.when((s == 0) & (i == 0))
        def _():
            land = win_ref.at[_dev_index(*me)]
            pltpu.make_async_copy(shard_ref, land, local_sem).start()
            _remote(shard_ref, land, send_sems, recv_sems, 0, sib).start()
            for step, pair in pair_of_chip_step.items():
                peer = (x ^ ((step >> 2) & 1), y ^ ((step >> 1) & 1), c)
                _remote(shard_ref, land, send_sems, recv_sems, pair, peer).start()
            fetch(0)

        for step in range(N_DEV):
            @pl.when((s == step) & (i == 0))
            def _():
                slot_copy(step).wait()

            if step + 1 < N_DEV:
                @pl.when((s == step) & (i == n_tiles - 1))
                def _():
                    fetch(step + 1)

        proj_ref[...] = _dot(h_ref[...], wbuf[s % 2])

        @pl.when((s == N_DEV - 1) & (i == n_tiles - 1))
        def _():
            for pair in range(7):
                _remote(shard_ref, win_ref.at[0], send_sems, recv_sems, pair, me).wait_send()
            pltpu.make_async_copy(shard_ref, win_ref.at[_dev_index(*me)], local_sem).wait()

    hbm = pl.BlockSpec(memory_space=pltpu.HBM)
    return _pcall(
        body, (h, shard), name="mm_in",
        out_shape=(jax.ShapeDtypeStruct((T, N_DEV * n), F32), jax.ShapeDtypeStruct((N_DEV, K, n), shard.dtype)),
        grid=(N_DEV, n_tiles), in_specs=[pl.BlockSpec((tm, K), lambda s, i: (i, 0)), hbm],
        out_specs=(pl.BlockSpec((tm, n), lambda s, i: (i, slab_of(s))), hbm),
        scratch_shapes=[pltpu.VMEM((2, K, n), shard.dtype), pltpu.SemaphoreType.DMA((2,)),
                        pltpu.SemaphoreType.DMA((7,)), pltpu.SemaphoreType.DMA((7,)), pltpu.SemaphoreType.DMA],
        sem=("arbitrary", "arbitrary"), comm=comm)


def _mm_cols_slab_t(name, a, w_slabs, out_dtype=F32, tm=MM_ROWS, comm=None):
    M = a.shape[0]
    J, K, n = w_slabs.shape
    tm = _tile(M, tm)
    return _matmul(
        name, "nt", a, w_slabs, jax.ShapeDtypeStruct((M, K), out_dtype), (M // tm, J),
        pl.BlockSpec((tm, n), lambda i, j: (i, j)), pl.BlockSpec((None, K, n), lambda i, j: (j, 0, 0)),
        pl.BlockSpec((tm, K), lambda i, j: (i, 0)), (tm, K), comm=comm)


def _mm_in_t(d_lo, d_hi, w_slabs, tm=MM_ROWS, comm=None):
    M = d_lo.shape[0]
    J, K, n = w_slabs.shape
    half = J // 2
    tm = _tile(M, tm)

    def body(lo_ref, hi_ref, w_ref, o_ref, acc_ref):
        j = pl.program_id(1)

        @pl.when(j == 0)
        def _():
            acc_ref[...] = _dot(lo_ref[...], w_ref[...], "nt")

        @pl.when((j > 0) & (j < half))
        def _():
            acc_ref[...] += _dot(lo_ref[...], w_ref[...], "nt")

        @pl.when(j >= half)
        def _():
            acc_ref[...] += _dot(hi_ref[...], w_ref[...], "nt")

        @pl.when(j == J - 1)
        def _():
            o_ref[...] = acc_ref[...]

    out, landed = _pcall(
        body, (d_lo, d_hi, w_slabs), name="mm_in_t", out_shape=jax.ShapeDtypeStruct((M, K), F32), grid=(M // tm, J),
        in_specs=[pl.BlockSpec((tm, n), lambda i, j: (i, jnp.minimum(j, half - 1))),
                  pl.BlockSpec((tm, n), lambda i, j: (i, jnp.maximum(j - half, 0))),
                  pl.BlockSpec((None, K, n), lambda i, j: (j, 0, 0))],
        out_specs=pl.BlockSpec((tm, K), lambda i, j: (i, 0)), scratch_shapes=[pltpu.VMEM((tm, K), F32)],
        sem=("parallel", "arbitrary"), comm=comm)
    return out if comm is None else (out, landed)


def _mm_tn_slab(name, a, b, n, out_dtype=F32, tk=MM_TOKENS, comm=None, part=(0, 1)):
    T, M = a.shape
    p, of = part
    M //= of
    J = b.shape[1] // n
    tk = _tile(T, tk)
    return _matmul(
        name, "tn", a, b, jax.ShapeDtypeStruct((J, M, n), out_dtype), (J, T // tk),
        pl.BlockSpec((tk, M), lambda j, k: (k, p)), pl.BlockSpec((tk, n), lambda j, k: (k, j)),
        pl.BlockSpec((None, M, n), lambda j, k: (j, 0, 0)), (M, n), comm=comm)


def _rms_fwd(name, x, g, tm=512):
    T, Dm = x.shape
    tm = _tile(T, tm)

    def body(x_ref, g_ref, h_ref):
        xv = x_ref[...]
        r = lax.rsqrt(jnp.mean(xv * xv, axis=-1, keepdims=True) + EPS)
        h_ref[...] = (xv * r * g_ref[...]).astype(h_ref.dtype)

    return pl.pallas_call(
        body, out_shape=jax.ShapeDtypeStruct((T, Dm), BF16), grid=(T // tm,),
        in_specs=[pl.BlockSpec((tm, Dm), lambda i: (i, 0)), pl.BlockSpec((1, Dm), lambda i: (0, 0))],
        out_specs=pl.BlockSpec((tm, Dm), lambda i: (i, 0)), name=name, compiler_params=_params(("parallel",)),
    )(x, g)


def _rms_bwd(name, x, g, dh, dres, tm=512):
    T, Dm = x.shape
    tm = _tile(T, tm)
    want_dx = dres is not None

    def body(*refs):
        if want_dx:
            x_ref, g_ref, dh_ref, dres_ref, dx_ref, dg_ref = refs
        else:
            x_ref, g_ref, dh_ref, dg_ref = refs
        xv = x_ref[...]
        r = lax.rsqrt(jnp.mean(xv * xv, axis=-1, keepdims=True) + EPS)
        xhat = xv * r
        dhv = dh_ref[...]

        @pl.when(pl.program_id(0) == 0)
        def _():
            dg_ref[...] = jnp.zeros_like(dg_ref)

        dg_ref[...] += jnp.sum(dhv * xhat, axis=0, keepdims=True)
        if want_dx:
            dxhat = dhv * g_ref[...]
            dx_ref[...] = dres_ref[...] + r * (dxhat - xhat * jnp.mean(dxhat * xhat, axis=-1, keepdims=True))

    row = pl.BlockSpec((tm, Dm), lambda i: (i, 0))
    vec = pl.BlockSpec((1, Dm), lambda i: (0, 0))
    if want_dx:
        return pl.pallas_call(
            body, out_shape=(jax.ShapeDtypeStruct((T, Dm), F32), jax.ShapeDtypeStruct((1, Dm), F32)),
            grid=(T // tm,), in_specs=[row, vec, row, row], out_specs=(row, vec), name=name,
            compiler_params=_params(("arbitrary",)),
        )(x, g, dh, dres)
    return pl.pallas_call(
        body, out_shape=jax.ShapeDtypeStruct((1, Dm), F32), grid=(T // tm,), in_specs=[row, vec, row],
        out_specs=vec, name=name, compiler_params=_params(("arbitrary",)),
    )(x, g, dh)


def _pool_rows(S):
    return _tile(S, 256)


def _pool_count(c0, rows, w):
    t = c0 + lax.broadcasted_iota(jnp.int32, (rows, 1), 0)
    return jnp.minimum(t + 1, w).astype(F32)


def _pool_fwd(proj, w_pool, scale, B, S):
    CH = _pool_rows(S)

    def body(hp_ref, wp_ref, sc_ref, o_ref, pad_ref):
        pad_ref[0:POOL_HALO, :] = jnp.zeros((POOL_HALO, POOL_WIDTH), F32)
        pad_ref[POOL_HALO:, :] = hp_ref[...]
        for gi, w in enumerate(POOL_WINDOWS):
            cols = slice(gi * POOL_GROUP_DIM, (gi + 1) * POOL_GROUP_DIM)
            for c in range(S // CH):
                base = POOL_HALO + c * CH
                acc = pad_ref[base:base + CH, cols]
                tok = acc
                for j in range(1, w):
                    acc = acc + pad_ref[base - j:base - j + CH, cols]
                pooled = acc / _pool_count(c * CH, CH, w) - tok
                z = _dot(pooled, wp_ref[gi])
                o_ref[c * CH:(c + 1) * CH, cols] = (z * sc_ref[:, cols]).astype(o_ref.dtype)

    return pl.pallas_call(
        body, out_shape=jax.ShapeDtypeStruct((B * S, POOL_WIDTH), BF16), grid=(B,),
        in_specs=[pl.BlockSpec((S, POOL_WIDTH), lambda b: (b, 0)),
                  pl.BlockSpec(w_pool.shape, lambda b: (0, 0, 0)),
                  pl.BlockSpec((1, POOL_WIDTH), lambda b: (0, 0))],
        out_specs=pl.BlockSpec((S, POOL_WIDTH), lambda b: (b, 0)),
        scratch_shapes=[pltpu.VMEM((S + POOL_HALO, POOL_WIDTH), F32)],
        name="pool_fwd", compiler_params=_params(("parallel",)),
    )(proj, w_pool, scale)


def _pool_bwd(proj, d_ypre, w_pool, scale, B, S):
    CH = _pool_rows(S)

    def body(hp_ref, dy_ref, wp_ref, sc_ref, dhp_ref, dwp_ref, dsc_ref, pad_ref, sc_pad_ref, dp_ref):
        @pl.when(pl.program_id(0) == 0)
        def _():
            dwp_ref[...] = jnp.zeros_like(dwp_ref)
            dsc_ref[...] = jnp.zeros_like(dsc_ref)

        pad_ref[0:POOL_HALO, :] = jnp.zeros((POOL_HALO, POOL_WIDTH), F32)
        pad_ref[POOL_HALO:, :] = hp_ref[...]
        sc_pad_ref[S:, :] = jnp.zeros((POOL_HALO, POOL_WIDTH), F32)
        for gi, w in enumerate(POOL_WINDOWS):
            cols = slice(gi * POOL_GROUP_DIM, (gi + 1) * POOL_GROUP_DIM)
            for c in range(S // CH):
                base = POOL_HALO + c * CH
                rows = slice(c * CH, (c + 1) * CH)
                acc = pad_ref[base:base + CH, cols]
                tok = acc
                for j in range(1, w):
                    acc = acc + pad_ref[base - j:base - j + CH, cols]
                cnt = _pool_count(c * CH, CH, w)
                pooled = acc / cnt - tok
                z = _dot(pooled, wp_ref[gi])
                dy = dy_ref[rows, cols]
                dsc_ref[:, cols] += jnp.sum(dy * z, axis=0, keepdims=True)
                dz = dy * sc_ref[:, cols]
                dwp_ref[gi] += _dot(pooled, dz, "tn")
                dpool = _dot(dz, wp_ref[gi], "nt")
                dp_ref[rows, cols] = dpool
                sc_pad_ref[rows, cols] = dpool / cnt
            for c in range(S // CH):
                rows = slice(c * CH, (c + 1) * CH)
                acc = sc_pad_ref[rows, cols]
                for j in range(1, w):
                    acc = acc + sc_pad_ref[c * CH + j:c * CH + j + CH, cols]
                dhp_ref[rows, cols] = (acc - dp_ref[rows, cols]).astype(dhp_ref.dtype)

    seq = pl.BlockSpec((S, POOL_WIDTH), lambda b: (b, 0))
    return pl.pallas_call(
        body,
        out_shape=(jax.ShapeDtypeStruct((B * S, POOL_WIDTH), BF16),
                   jax.ShapeDtypeStruct(w_pool.shape, F32), jax.ShapeDtypeStruct((1, POOL_WIDTH), F32)),
        grid=(B,),
        in_specs=[seq, seq, pl.BlockSpec(w_pool.shape, lambda b: (0, 0, 0)),
                  pl.BlockSpec((1, POOL_WIDTH), lambda b: (0, 0))],
        out_specs=(seq, pl.BlockSpec(w_pool.shape, lambda b: (0, 0, 0)),
                   pl.BlockSpec((1, POOL_WIDTH), lambda b: (0, 0))),
        scratch_shapes=[pltpu.VMEM((S + POOL_HALO, POOL_WIDTH), F32),
                        pltpu.VMEM((S + POOL_HALO, POOL_WIDTH), F32),
                        pltpu.VMEM((S, POOL_WIDTH), F32)],
        name="pool_bwd", compiler_params=_params(("arbitrary",)),
    )(proj, d_ypre, w_pool, scale)


def _ret_tables(S):
    half = RET_QK_DIM // 2
    inv = ROPE_BASE ** (-jnp.arange(half, dtype=F32) / half)
    ang = jnp.arange(S, dtype=F32)[:, None] * inv[None, :]
    cos, sin = jnp.cos(ang), jnp.sin(ang)
    cos_full = jnp.concatenate([cos, cos], axis=-1)
    sin_signed = jnp.concatenate([-sin, sin], axis=-1)
    C = RET_CHUNK
    lg = jnp.log1p(-jnp.exp2(-5.0 - jnp.arange(RET_HEADS, dtype=F32)))[:, None, None]
    idx = jnp.arange(C, dtype=F32)
    rel = idx[:, None] - idx[None, :]
    decay = jnp.where(rel >= 0, jnp.exp(jnp.maximum(rel, 0.0) * lg), 0.0)
    q_decay = jnp.broadcast_to(jnp.exp((idx + 1.0)[None, :, None] * lg), (RET_HEADS, C, RET_QK_DIM))
    k_decay = jnp.broadcast_to(jnp.exp((C - 1.0 - idx)[None, :, None] * lg), (RET_HEADS, C, RET_QK_DIM))
    c_decay = jnp.broadcast_to(jnp.exp(C * lg), (RET_HEADS, 1, RET_V_DIM))
    return cos_full, sin_signed, decay, q_decay, k_decay, c_decay


def _rope(x, cos_full, sin_signed):
    return x * cos_full + pltpu.roll(x, RET_QK_DIM // 2, axis=1) * sin_signed


def _rope_t(dy, cos_full, sin_signed):
    return dy * cos_full + pltpu.roll(dy * sin_signed, RET_QK_DIM // 2, axis=1)


RET_COLS = 512


def _ret_specs(N, chunk_of):
    C = RET_CHUNK

    def rows(width, col=0):
        return pl.BlockSpec((C, width), lambda b, i: (b * N + chunk_of(i), col))

    def whole(shape):
        return pl.BlockSpec(shape, lambda b, i: (0,) * len(shape))

    wide = RET_HEADS * RET_V_DIM
    return dict(
        q=rows(RET_COLS, COL_Q // RET_COLS), k=rows(RET_COLS, COL_K // RET_COLS),
        v=[rows(RET_COLS, COL_V // RET_COLS + j) for j in range(2)],
        gr=[rows(RET_COLS, COL_GR // RET_COLS + j) for j in range(2)],
        table=pl.BlockSpec((C, RET_QK_DIM), lambda b, i: (chunk_of(i), 0)),
        decay=whole((RET_HEADS, C, C)), qd=whole((RET_HEADS, C, RET_QK_DIM)), kd=whole((RET_HEADS, C, RET_QK_DIM)),
        cd=whole((RET_HEADS, 1, RET_V_DIM)), vec=whole((1, wide)), qk_rows=rows(RET_COLS), v_rows=rows(wide),
        state=pl.BlockSpec((None, None, RET_HEADS, RET_QK_DIM, RET_V_DIM), lambda b, i: (b, chunk_of(i), 0, 0, 0)))


def _head_cols(h):
    pair = slice((h % 2) * RET_V_DIM, (h % 2 + 1) * RET_V_DIM)
    return slice(h * RET_QK_DIM, (h + 1) * RET_QK_DIM), h // 2, pair, slice(h * RET_V_DIM, (h + 1) * RET_V_DIM)


def _group_norm(o):
    mu = jnp.mean(o, axis=-1, keepdims=True)
    oc = o - mu
    rstd = lax.rsqrt(jnp.mean(oc * oc, axis=-1, keepdims=True) + EPS)
    return oc * rstd, rstd


def _ret_fwd(proj, g_ret, b_ret, tables, B, S, comm=None):
    N = S // RET_CHUNK
    cos_t, sin_t, decay, q_decay, k_decay, c_decay = tables
    sp = _ret_specs(N, lambda i: i)

    def body(q_ref, k_ref, v0_ref, v1_ref, gr0_ref, gr1_ref, cos_ref, sin_ref, dec_ref, qd_ref, kd_ref, cd_ref,
             g_ref, b_ref, y_ref, rs_ref, r_ref):
        @pl.when(pl.program_id(1) == 0)
        def _():
            r_ref[...] = jnp.zeros_like(r_ref)

        cs, sn = cos_ref[...], sin_ref[...]
        for h in range(RET_HEADS):
            qk, j, pair, wide = _head_cols(h)
            q = _rope(q_ref[:, qk], cs, sn)
            k = _rope(k_ref[:, qk], cs, sn) * (RET_QK_DIM ** -0.5)
            v = (v0_ref, v1_ref)[j][:, pair]
            R = r_ref[h]
            rs_ref[h] = R
            s = _dot(q, k, "nt") * dec_ref[h]
            o = _dot(s, v) + _dot(q * qd_ref[h], R)
            r_ref[h] = cd_ref[h] * R + _dot(k * kd_ref[h], v, "tn")
            on, _ = _group_norm(o)
            gr = (gr0_ref, gr1_ref)[j][:, pair]
            y_ref[:, wide] = (gr * jax.nn.sigmoid(gr) * (on * g_ref[:, wide] + b_ref[:, wide])).astype(y_ref.dtype)

    state = jax.ShapeDtypeStruct((B, N, RET_HEADS, RET_QK_DIM, RET_V_DIM), F32)
    return _pcall(
        body, (proj,) * 6 + (cos_t, sin_t, decay, q_decay, k_decay, c_decay, g_ret, b_ret),
        name="ret_fwd", out_shape=(jax.ShapeDtypeStruct((B * S, RET_HEADS * RET_V_DIM), BF16), state), grid=(B, N),
        in_specs=[sp["q"], sp["k"], *sp["v"], *sp["gr"], sp["table"], sp["table"], sp["decay"], sp["qd"],
                  sp["kd"], sp["cd"], sp["vec"], sp["vec"]],
        out_specs=(sp["v_rows"], sp["state"]),
        scratch_shapes=[pltpu.VMEM((RET_HEADS, RET_QK_DIM, RET_V_DIM), F32)],
        sem=("parallel", "arbitrary"), comm=comm)


def _ret_bwd(proj, states, d_yr, g_ret, b_ret, tables, B, S, comm=None):
    N = S // RET_CHUNK
    cos_t, sin_t, decay, q_decay, k_decay, c_decay = tables
    sp = _ret_specs(N, lambda i: N - 1 - i)
    qk_scale = RET_QK_DIM ** -0.5

    def body(q_ref, k_ref, v0_ref, v1_ref, gr0_ref, gr1_ref, dy_ref, rs_ref, cos_ref, sin_ref, dec_ref, qd_ref,
             kd_ref, cd_ref, g_ref, b_ref, dq_ref, dk_ref, dv_ref, dgr_ref, dg_ref, db_ref, dr_ref):
        @pl.when((pl.program_id(0) == 0) & (pl.program_id(1) == 0))
        def _():
            dg_ref[...] = jnp.zeros_like(dg_ref)
            db_ref[...] = jnp.zeros_like(db_ref)

        @pl.when(pl.program_id(1) == 0)
        def _():
            dr_ref[...] = jnp.zeros_like(dr_ref)

        cs, sn = cos_ref[...], sin_ref[...]
        for h in range(RET_HEADS):
            qk, j, pair, wide = _head_cols(h)
            q = _rope(q_ref[:, qk], cs, sn)
            k = _rope(k_ref[:, qk], cs, sn) * qk_scale
            v = (v0_ref, v1_ref)[j][:, pair]
            R, dR = rs_ref[h], dr_ref[h]
            dec, qd, kd = dec_ref[h], qd_ref[h], kd_ref[h]
            s = _dot(q, k, "nt") * dec
            o = _dot(s, v) + _dot(q * qd, R)
            on, rstd = _group_norm(o)
            g = g_ref[:, wide]
            oaff = on * g + b_ref[:, wide]
            gr = (gr0_ref, gr1_ref)[j][:, pair]
            sg = jax.nn.sigmoid(gr)
            dy = dy_ref[:, wide]
            dgr_ref[:, wide] = (dy * oaff * (sg * (1.0 + gr * (1.0 - sg)))).astype(dgr_ref.dtype)
            doaff = dy * (gr * sg)
            dg_ref[:, wide] += jnp.sum(doaff * on, axis=0, keepdims=True)
            db_ref[:, wide] += jnp.sum(doaff, axis=0, keepdims=True)
            don = doaff * g
            do = rstd * (don - jnp.mean(don, axis=-1, keepdims=True)
                         - on * jnp.mean(don * on, axis=-1, keepdims=True))
            ds = _dot(do, v, "nt") * dec
            dq = _dot(ds, k) + qd * _dot(do, R, "nt")
            dk = _dot(ds, q, "tn") + kd * _dot(v, dR, "nt")
            dv_ref[:, wide] = (_dot(s, do, "tn") + _dot(k * kd, dR)).astype(dv_ref.dtype)
            dr_ref[h] = cd_ref[h] * dR + _dot(q * qd, do, "tn")
            dq_ref[:, qk] = _rope_t(dq, cs, sn).astype(dq_ref.dtype)
            dk_ref[:, qk] = _rope_t(dk * qk_scale, cs, sn).astype(dk_ref.dtype)

    T = B * S
    qk_shape = jax.ShapeDtypeStruct((T, RET_HEADS * RET_QK_DIM), BF16)
    v_shape = jax.ShapeDtypeStruct((T, RET_HEADS * RET_V_DIM), BF16)
    vec_shape = jax.ShapeDtypeStruct((1, RET_HEADS * RET_V_DIM), F32)
    return _pcall(
        body, (proj,) * 6 + (d_yr, states, cos_t, sin_t, decay, q_decay, k_decay, c_decay, g_ret, b_ret),
        name="ret_bwd", out_shape=(qk_shape, qk_shape, v_shape, v_shape, vec_shape, vec_shape), grid=(B, N),
        in_specs=[sp["q"], sp["k"], *sp["v"], *sp["gr"], sp["v_rows"], sp["state"], sp["table"], sp["table"],
                  sp["decay"], sp["qd"], sp["kd"], sp["cd"], sp["vec"], sp["vec"]],
        out_specs=(sp["qk_rows"], sp["qk_rows"], sp["v_rows"], sp["v_rows"], sp["vec"], sp["vec"]),
        scratch_shapes=[pltpu.VMEM((RET_HEADS, RET_QK_DIM, RET_V_DIM), F32)],
        sem=("arbitrary", "arbitrary"), comm=comm)


def _xa_rows(S):
    return _tile(S, 256)


def _xa_specs(S, M):
    q = pl.BlockSpec((S, XA_HEAD_DIM), lambda b, h: (b, COL_QX // XA_HEAD_DIM + h))
    k = pl.BlockSpec((M, XA_HEAD_DIM), lambda b, h: (b, h))
    v = pl.BlockSpec((M, XA_HEAD_DIM), lambda b, h: (b, XA_HEADS + h))
    o = pl.BlockSpec((S, XA_HEAD_DIM), lambda b, h: (b, h))
    return q, k, v, o


def _softmax_rows(s):
    e = jnp.exp(s - jnp.max(s, axis=-1, keepdims=True))
    return e / jnp.sum(e, axis=-1, keepdims=True)


def _xa_fwd(proj, kv, B, S, M, comm=None):
    CH = _xa_rows(S)
    q_spec, k_spec, v_spec, o_spec = _xa_specs(S, M)

    def body(q_ref, k_ref, v_ref, o_ref):
        def chunk(i, carry):
            rows = pl.ds(pl.multiple_of(i * CH, CH), CH)
            p = _softmax_rows(_dot(q_ref[rows, :], k_ref[...], "nt") * (XA_HEAD_DIM ** -0.5))
            o_ref[rows, :] = _dot(p, v_ref[...]).astype(o_ref.dtype)
            return carry

        lax.fori_loop(0, S // CH, chunk, 0)

    return _pcall(
        body, (proj, kv, kv), name="xattn_fwd", out_shape=jax.ShapeDtypeStruct((B * S, XA_WIDTH), BF16),
        grid=(B, XA_HEADS), in_specs=[q_spec, k_spec, v_spec], out_specs=o_spec,
        sem=("parallel", "parallel"), comm=comm)


def _xa_bwd(proj, kv, d_o, B, S, M, comm=None):
    CH = _xa_rows(S)
    q_spec, k_spec, v_spec, o_spec = _xa_specs(S, M)
    scale = XA_HEAD_DIM ** -0.5

    def body(q_ref, k_ref, v_ref, do_ref, dq_ref, dk_ref, dv_ref):
        dk_ref[...] = jnp.zeros_like(dk_ref)
        dv_ref[...] = jnp.zeros_like(dv_ref)

        def chunk(i, carry):
            rows = pl.ds(pl.multiple_of(i * CH, CH), CH)
            q, do = q_ref[rows, :], do_ref[rows, :]
            p = _softmax_rows(_dot(q, k_ref[...], "nt") * scale)
            dp = _dot(do, v_ref[...], "nt")
            ds = p * (dp - jnp.sum(dp * p, axis=-1, keepdims=True)) * scale
            dq_ref[rows, :] = _dot(ds, k_ref[...]).astype(dq_ref.dtype)
            dk_ref[...] += _dot(ds, q, "tn")
            dv_ref[...] += _dot(p, do, "tn")
            return carry

        lax.fori_loop(0, S // CH, chunk, 0)

    kv_out = pl.BlockSpec((M, XA_HEAD_DIM), lambda b, h: (b, h))
    return _pcall(
        body, (proj, kv, kv, d_o), name="xattn_bwd",
        out_shape=(jax.ShapeDtypeStruct((B * S, XA_WIDTH), BF16), jax.ShapeDtypeStruct((B * M, XA_WIDTH), F32),
                   jax.ShapeDtypeStruct((B * M, XA_WIDTH), F32)),
        grid=(B, XA_HEADS), in_specs=[q_spec, k_spec, v_spec, o_spec], out_specs=(o_spec, kv_out, kv_out),
        sem=("parallel", "parallel"), comm=comm)


def _gate_specs(tm):
    n = COL_GL // D_MODEL
    return [pl.BlockSpec((tm, D_MODEL), lambda i, j=j: (i, n + j)) for j in range(3)]


def _merge_fwd(proj, ys, tm=256):
    T = proj.shape[0]
    tm = _tile(T, tm)
    row = pl.BlockSpec((tm, D_MODEL), lambda i: (i, 0))

    def body(g0, g1, g2, y0, y1, y2, o_ref):
        acc = jax.nn.sigmoid(g0[...]) * y0[...]
        acc = acc + jax.nn.sigmoid(g1[...]) * y1[...]
        acc = acc + jax.nn.sigmoid(g2[...]) * y2[...]
        o_ref[...] = acc.astype(o_ref.dtype)

    return pl.pallas_call(
        body, out_shape=jax.ShapeDtypeStruct((T, D_MODEL), BF16), grid=(T // tm,),
        in_specs=_gate_specs(tm) + [row] * 3, out_specs=row, name="merge_fwd",
        compiler_params=_params(("parallel",)),
    )(proj, proj, proj, *ys)


def _merge_bwd(proj, ys, d_merged, tm=256, comm=None):
    T = proj.shape[0]
    tm = _tile(T, tm)
    row = pl.BlockSpec((tm, D_MODEL), lambda i: (i, 0))

    def body(g0, g1, g2, y0, y1, y2, dm_ref, dgl_ref, d0, d1, d2):
        dm = dm_ref[...]
        for j, (g_ref, y_ref, d_ref) in enumerate(((g0, y0, d0), (g1, y1, d1), (g2, y2, d2))):
            sg = jax.nn.sigmoid(g_ref[...])
            d_ref[...] = (dm * sg).astype(d_ref.dtype)
            dgl_ref[:, j * D_MODEL:(j + 1) * D_MODEL] = (dm * y_ref[...] * sg * (1.0 - sg)).astype(dgl_ref.dtype)

    dy = jax.ShapeDtypeStruct((T, D_MODEL), BF16)
    return _pcall(
        body, (proj, proj, proj, *ys, d_merged), name="merge_bwd",
        out_shape=(jax.ShapeDtypeStruct((T, 3 * D_MODEL), BF16), dy, dy, dy), grid=(T // tm,),
        in_specs=_gate_specs(tm) + [row] * 4,
        out_specs=(pl.BlockSpec((tm, 3 * D_MODEL), lambda i: (i, 0)), row, row, row),
        sem=("parallel",), comm=comm)


def _gelu(x):
    return 0.5 * x * (1.0 + jnp.tanh(GELU_C * (x + GELU_A * x * x * x)))


def _gelu_grad(x):
    t = jnp.tanh(GELU_C * (x + GELU_A * x * x * x))
    return 0.5 * (1.0 + t) + 0.5 * x * (1.0 - t * t) * GELU_C * (1.0 + 3.0 * GELU_A * x * x)


def _shift_down(x, prev, n):
    rows = x.shape[0]
    r = lax.broadcasted_iota(jnp.int32, (8, 1), 0)
    rolled = pltpu.roll(x, n, axis=0)
    head = rolled[0:8]
    for j in range(n):
        head = jnp.where(r == j, prev[8 - n + j:8 - n + j + 1, :], head)
    return head if rows == 8 else jnp.concatenate([head, rolled[8:]], axis=0)


def _shift_up(x, nxt, n):
    rows = x.shape[0]
    r = lax.broadcasted_iota(jnp.int32, (8, 1), 0)
    rolled = pltpu.roll(x, rows - n, axis=0)
    tail = rolled[rows - 8:]
    for j in range(n):
        tail = jnp.where(r == 8 - n + j, nxt[j:j + 1, :], tail)
    return jnp.concatenate([rolled[:rows - 8], tail], axis=0)


def _conv(a, prev, cw, cb):
    return _shift_down(a, prev, 2) * cw[0:1, :] + _shift_down(a, prev, 1) * cw[1:2, :] + a * cw[2:3, :] + cb


def _glu_fwd(up, cw, cb, S, tm=256):
    T = up.shape[2]
    tm = _tile(S, tm)
    per_seq = S // tm

    def body(ab_ref, prev_ref, cw_ref, cb_ref, u_ref):
        i = pl.program_id(1)
        prev = jnp.where(i % per_seq == 0, 0.0, prev_ref[...])
        ac = _conv(ab_ref[0], prev, cw_ref[...], cb_ref[...])
        u_ref[...] = (_gelu(ac) * ab_ref[1]).astype(u_ref.dtype)

    return pl.pallas_call(
        body, out_shape=jax.ShapeDtypeStruct((FFN_SLABS, T, UP_SHARD), BF16), grid=(FFN_SLABS, T // tm),
        in_specs=[pl.BlockSpec((2, None, tm, UP_SHARD), lambda d, i: (0, d, i, 0)),
                  pl.BlockSpec((None, None, 8, UP_SHARD), lambda d, i: (0, d, jnp.maximum(i * (tm // 8) - 1, 0), 0)),
                  pl.BlockSpec((None, 3, UP_SHARD), lambda d, i: (d, 0, 0)),
                  pl.BlockSpec((None, 1, UP_SHARD), lambda d, i: (d, 0, 0))],
        out_specs=pl.BlockSpec((None, tm, UP_SHARD), lambda d, i: (d, i, 0)), name="glu_fwd",
        compiler_params=_params(("parallel", "parallel")),
    )(up, up, cw, cb)


def _glu_bwd(up, d_u, cw, cb, S, tm=256, comm=None):
    T = up.shape[2]
    tm = _tile(S, tm)
    per_seq = S // tm
    n_tiles = T // tm
    last8 = tm // 8

    def body(ab_ref, prev_ref, abn_ref, du_ref, dun_ref, cw_ref, cb_ref, dup_ref, dcw_ref, dcb_ref):
        i = pl.program_id(1)

        @pl.when(i == 0)
        def _():
            dcw_ref[...] = jnp.zeros_like(dcw_ref)
            dcb_ref[...] = jnp.zeros_like(dcb_ref)

        cw, cb = cw_ref[...], cb_ref[...]
        a, b = ab_ref[0], ab_ref[1]
        prev = jnp.where(i % per_seq == 0, 0.0, prev_ref[...])
        a2, a1 = _shift_down(a, prev, 2), _shift_down(a, prev, 1)
        ac = a2 * cw[0:1, :] + a1 * cw[1:2, :] + a * cw[2:3, :] + cb
        du = du_ref[...]
        dup_ref[1] = (du * _gelu(ac)).astype(dup_ref.dtype)
        dac = du * b * _gelu_grad(ac)
        dcb_ref[...] += jnp.sum(dac, axis=0, keepdims=True)
        dcw_ref[0:1, :] += jnp.sum(dac * a2, axis=0, keepdims=True)
        dcw_ref[1:2, :] += jnp.sum(dac * a1, axis=0, keepdims=True)
        dcw_ref[2:3, :] += jnp.sum(dac * a, axis=0, keepdims=True)
        an = abn_ref[0]
        acn = _conv(an, a[tm - 8:, :], cw, cb)
        dacn = jnp.where(i % per_seq == per_seq - 1, 0.0, dun_ref[...] * abn_ref[1] * _gelu_grad(acn))
        da = dac * cw[2:3, :] + _shift_up(dac, dacn, 1) * cw[1:2, :] + _shift_up(dac, dacn, 2) * cw[0:1, :]
        dup_ref[0] = da.astype(dup_ref.dtype)

    def nxt(i):
        return jnp.minimum((i + 1) * last8, T // 8 - 1)

    return _pcall(
        body, (up, up, up, d_u, d_u, cw, cb), name="glu_bwd",
        out_shape=(jax.ShapeDtypeStruct((2, FFN_SLABS, T, UP_SHARD), BF16),
                   jax.ShapeDtypeStruct((FFN_SLABS, 3, UP_SHARD), F32),
                   jax.ShapeDtypeStruct((FFN_SLABS, 1, UP_SHARD), F32)),
        grid=(FFN_SLABS, n_tiles),
        in_specs=[pl.BlockSpec((2, None, tm, UP_SHARD), lambda d, i: (0, d, i, 0)),
                  pl.BlockSpec((None, None, 8, UP_SHARD), lambda d, i: (0, d, jnp.maximum(i * last8 - 1, 0), 0)),
                  pl.BlockSpec((2, None, 8, UP_SHARD), lambda d, i: (0, d, nxt(i), 0)),
                  pl.BlockSpec((None, tm, UP_SHARD), lambda d, i: (d, i, 0)),
                  pl.BlockSpec((None, 8, UP_SHARD), lambda d, i: (d, nxt(i), 0)),
                  pl.BlockSpec((None, 3, UP_SHARD), lambda d, i: (d, 0, 0)),
                  pl.BlockSpec((None, 1, UP_SHARD), lambda d, i: (d, 0, 0))],
        out_specs=(pl.BlockSpec((2, None, tm, UP_SHARD), lambda d, i: (0, d, i, 0)),
                   pl.BlockSpec((None, 3, UP_SHARD), lambda d, i: (d, 0, 0)),
                   pl.BlockSpec((None, 1, UP_SHARD), lambda d, i: (d, 0, 0))),
        sem=("parallel", "arbitrary"), comm=comm)


def _mm_up(h2, w_up, tm=MM_ROWS):
    T, K = h2.shape
    tm = _tile(T, tm)
    return _matmul(
        "mm_up", "nn", h2, w_up, jax.ShapeDtypeStruct((N_DEV, T, UP_SHARD), F32), (N_DEV, T // tm, 1),
        pl.BlockSpec((tm, K), lambda j, i, k: (i, 0)), pl.BlockSpec((None, K, UP_SHARD), lambda j, i, k: (j, 0, 0)),
        pl.BlockSpec((None, tm, UP_SHARD), lambda j, i, k: (j, i, 0)), (tm, UP_SHARD))


def _mm_down(u, w_down, res, tm=MM_ROWS_RES):
    J, T, n = u.shape
    tm = _tile(T, tm)
    row = pl.BlockSpec((tm, D_MODEL), lambda i, d: (i, 0))
    return _matmul(
        "mm_down", "nn", u, w_down, jax.ShapeDtypeStruct((T, D_MODEL), F32), (T // tm, J),
        pl.BlockSpec((None, tm, n), lambda i, d: (d, i, 0)), pl.BlockSpec((None, n, D_MODEL), lambda i, d: (d, 0, 0)),
        row, (tm, D_MODEL), res, row)


def _mm_down_t(dx, w_down, tm=MM_ROWS):
    T = dx.shape[0]
    J, n, _ = w_down.shape
    tm = _tile(T, tm)
    return _matmul(
        "mm_down_t", "nt", dx, w_down, jax.ShapeDtypeStruct((J, T, n), F32), (J, T // tm, 1),
        pl.BlockSpec((tm, D_MODEL), lambda d, i, k: (i, 0)), pl.BlockSpec((None, n, D_MODEL), lambda d, i, k: (d, 0, 0)),
        pl.BlockSpec((None, tm, n), lambda d, i, k: (d, i, 0)), (tm, n))


def _mm_dw_down(u, dx, tk=MM_TOKENS):
    J, T, n = u.shape
    tk = _tile(T, tk)
    return _matmul(
        "mm_dw_down", "tn", u, dx, jax.ShapeDtypeStruct((J, n, D_MODEL), BF16), (J, T // tk),
        pl.BlockSpec((None, tk, n), lambda d, k: (d, k, 0)), pl.BlockSpec((tk, D_MODEL), lambda d, k: (k, 0)),
        pl.BlockSpec((None, n, D_MODEL), lambda d, k: (d, 0, 0)), (n, D_MODEL))


def _mm_dw_up(name, h2, d_up, part, tk=MM_TOKENS, comm=None):
    T, K = h2.shape
    p, of = part
    K //= of
    tk = _tile(T, tk)
    return _matmul(
        name, "tn", h2, d_up, jax.ShapeDtypeStruct((N_DEV, K, UP_SHARD), BF16), (N_DEV, T // tk),
        pl.BlockSpec((tk, K), lambda j, k: (k, p)), pl.BlockSpec((None, tk, UP_SHARD), lambda j, k: (j, k, 0)),
        pl.BlockSpec((None, K, UP_SHARD), lambda j, k: (j, 0, 0)), (K, UP_SHARD), comm=comm)


def _mm_up_t(d_up, w_up, tm=MM_ROWS, comm=None):
    J, T, n = d_up.shape
    K = w_up.shape[1]
    tm = _tile(T, tm)
    return _matmul(
        "mm_up_t", "nt", d_up, w_up, jax.ShapeDtypeStruct((T, K), F32), (T // tm, J),
        pl.BlockSpec((None, tm, n), lambda i, j: (j, i, 0)), pl.BlockSpec((None, K, n), lambda i, j: (j, 0, 0)),
        pl.BlockSpec((tm, K), lambda i, j: (i, 0)), (tm, K), comm=comm)


def _loss_head(x2, target, g_final, tm=512):
    T, Dm = x2.shape
    tm = _tile(T, tm)

    def body(x_ref, t_ref, g_ref, dx_ref, dg_ref, loss_ref):
        @pl.when(pl.program_id(0) == 0)
        def _():
            dg_ref[...] = jnp.zeros_like(dg_ref)
            loss_ref[...] = jnp.zeros_like(loss_ref)

        xv = x_ref[...]
        r = lax.rsqrt(jnp.mean(xv * xv, axis=-1, keepdims=True) + EPS)
        xhat = xv * r
        err = xhat * g_ref[...] - t_ref[...]
        loss_ref[...] += (0.5 / Dm) * jnp.sum(err * err)
        dy = err * (1.0 / Dm)
        dg_ref[...] += jnp.sum(dy * xhat, axis=0, keepdims=True)
        dxhat = dy * g_ref[...]
        dx_ref[...] = r * (dxhat - xhat * jnp.mean(dxhat * xhat, axis=-1, keepdims=True))

    row = pl.BlockSpec((tm, Dm), lambda i: (i, 0))
    vec = pl.BlockSpec((1, Dm), lambda i: (0, 0))
    return pl.pallas_call(
        body,
        out_shape=(jax.ShapeDtypeStruct((T, Dm), F32), jax.ShapeDtypeStruct((1, Dm), F32),
                   jax.ShapeDtypeStruct((1, Dm), F32)),
        grid=(T // tm,), in_specs=[row, row, vec], out_specs=(row, vec, vec), name="loss_head",
        compiler_params=_params(("arbitrary",)),
    )(x2, target, g_final)


def _cast_shards(shards):
    def body(*refs):
        n = len(refs) // 2
        for src, dst in zip(refs[:n], refs[n:]):
            dst[...] = src[...].astype(dst.dtype)

    return pl.pallas_call(
        body, out_shape=[jax.ShapeDtypeStruct(s.shape, BF16) for s in shards], name="cast_shards",
        compiler_params=pltpu.CompilerParams(vmem_limit_bytes=VMEM_LIMIT),
    )(*shards)


def _adamw(w, g, m, v):
    m = ADAM_B1 * m + (1.0 - ADAM_B1) * g
    v = ADAM_B2 * v + (1.0 - ADAM_B2) * (g * g)
    m_hat = m / (1.0 - ADAM_B1 ** ADAM_STEP)
    v_hat = v / (1.0 - ADAM_B2 ** ADAM_STEP)
    delta = -ADAM_LR * (m_hat / (jnp.sqrt(v_hat) + ADAM_EPS) + ADAM_WD * w)
    return delta, m, v


def _sum_parts(p_ref):
    g = p_ref[0].astype(F32)
    for d in range(1, N_DEV):
        g = g + p_ref[d].astype(F32)
    return g


def _reduce_adam(name, parts, w, m, v, tr=128):
    R, Cn = w.shape
    by_rows = sum(p.shape[1] for p in parts) == R and len(parts) > 1
    tr = math.gcd(tr, *[p.shape[1] for p in parts])
    n_tiles = [p.shape[1] // tr for p in parts]
    first = [sum(n_tiles[:j]) for j in range(len(parts))] if by_rows else [0] * len(parts)

    def body(*refs):
        p_refs = refs[:len(parts)]
        w_ref, m_ref, v_ref, g_out, d_out, m_out, v_out = refs[len(parts):]

        def update(p_ref):
            g = _sum_parts(p_ref)
            delta, m_new, v_new = _adamw(w_ref[...], g, m_ref[...], v_ref[...])
            g_out[...] = g
            d_out[...] = delta
            m_out[...] = m_new
            v_out[...] = v_new

        if len(parts) == 1:
            update(p_refs[0])
        elif by_rows:
            i = pl.program_id(0)
            for p_ref, t0, n in zip(p_refs, first, n_tiles):
                pl.when((i >= t0) & (i < t0 + n))(functools.partial(update, p_ref))
        else:
            c = lax.axis_index("c")
            for side, p_ref in enumerate(p_refs):
                pl.when(c == side)(functools.partial(update, p_ref))

    def part_spec(t0, n):
        return pl.BlockSpec((N_DEV, tr, Cn), lambda i: (0, jnp.clip(i - t0, 0, n - 1), 0))

    row = pl.BlockSpec((tr, Cn), lambda i: (i, 0))
    shape = jax.ShapeDtypeStruct((R, Cn), F32)
    return pl.pallas_call(
        body, out_shape=(shape,) * 4, grid=(R // tr,),
        in_specs=[part_spec(t0, n) for t0, n in zip(first, n_tiles)] + [row, row, row],
        out_specs=(row,) * 4, name=name, compiler_params=_params(("parallel",)),
    )(*parts, w, m, v)


def _small_adam(name, gathered, params):
    n_g, n_p = len(gathered), len(params)

    def body(*refs):
        g_refs = refs[:n_g]
        wmv = refs[n_g:n_g + 3 * n_p]
        sums = refs[n_g + 3 * n_p:2 * n_g + 3 * n_p]
        upd = refs[2 * n_g + 3 * n_p:]
        for j in range(n_g):
            g = _sum_parts(g_refs[j])
            sums[j][...] = g
            if j < n_p:
                w_ref, m_ref, v_ref = wmv[3 * j:3 * j + 3]
                delta, m_new, v_new = _adamw(w_ref[...], g, m_ref[...], v_ref[...])
                upd[3 * j][...] = delta
                upd[3 * j + 1][...] = m_new
                upd[3 * j + 2][...] = v_new

    flat = [a for wmv in params for a in wmv]
    out_shape = [jax.ShapeDtypeStruct(g.shape[1:], F32) for g in gathered]
    out_shape += [jax.ShapeDtypeStruct(a.shape, F32) for a in flat]
    res = pl.pallas_call(body, out_shape=out_shape, name=name)(*gathered, *flat)
    return res[:n_g], [tuple(res[n_g + 3 * j:n_g + 3 * j + 3]) for j in range(n_p)]


def _adam_only(name, g, w, m, v):
    def body(g_ref, w_ref, m_ref, v_ref, d_out, m_out, v_out):
        delta, m_new, v_new = _adamw(w_ref[...], g_ref[...], m_ref[...], v_ref[...])
        d_out[...] = delta
        m_out[...] = m_new
        v_out[...] = v_new

    shape = jax.ShapeDtypeStruct(w.shape, F32)
    return pl.pallas_call(body, out_shape=(shape,) * 3, name=name)(g, w, m, v)


def kernel(x, mem, g_mix, w_in, w_pool, pool_scale, w_a, g_ret, b_ret, w_r, g_mem, w_mem_kv, w_c, w_out, g_ffn, w_up, conv_w, conv_b, w_down, g_final, loss_target, m_g_mix, m_w_in, m_w_pool, m_pool_scale, m_w_a, m_g_ret, m_b_ret, m_w_r, m_g_mem, m_w_mem_kv, m_w_c, m_w_out, m_g_ffn, m_w_up, m_conv_w, m_conv_b, m_w_down, m_g_final, v_g_mix, v_w_in, v_w_pool, v_pool_scale, v_w_a, v_g_ret, v_b_ret, v_w_r, v_g_mem, v_w_mem_kv, v_w_c, v_w_out, v_g_ffn, v_w_up, v_conv_w, v_conv_b, v_w_down, v_g_final):
    B, S, _ = x.shape
    M = mem.shape[1]
    T = B * S
    me = _my_index()
    x2d = x.reshape(T, D_MODEL)
    mem2d = mem.reshape(B * M, D_MODEL)
    tgt2d = loss_target.reshape(T, D_MODEL)
    g_final2 = g_final.reshape(1, D_MODEL)

    big = dict(w_in=w_in[0], w_a=w_a[0], w_r=w_r[0], w_mem_kv=w_mem_kv[0], w_c=w_c[0], w_out=w_out[0],
               w_up=w_up[0], w_down=w_down[0])
    names = list(big)
    cast = dict(zip(names, _cast_shards([big[n] for n in names])))
    cb = conv_b[0].reshape(FFN_SLABS, 1, UP_SHARD)
    wp = w_pool[0]
    tables = _ret_tables(S)

    h = _rms_fwd("rms_mix", x2d, g_mix)
    early = ("w_a", "w_r", "w_mem_kv", "w_c", "w_out")
    (proj, Win), landed = _mm_in_gather(h, cast["w_in"], comm=_Gather([cast[n] for n in early] + [conv_w[0]]))
    W = dict(zip(early, landed))
    cw_full = landed[-1].transpose(1, 0, 2).reshape(3, FFN_HIDDEN)
    cw = cw_full.reshape(3, FFN_SLABS, UP_SHARD).transpose(1, 0, 2)
    Wa = W["w_a"].transpose(1, 0, 2).reshape(POOL_WIDTH, D_MODEL)
    Wc = W["w_c"].transpose(1, 0, 2).reshape(XA_WIDTH, D_MODEL)
    Wr = W["w_r"].reshape(D_MODEL, D_MODEL)
    Wkv = W["w_mem_kv"].reshape(D_MODEL, D_MODEL)
    Wout = W["w_out"].reshape(D_MODEL, D_MODEL)
    ypre = _pool_fwd(proj, wp, pool_scale, B, S)
    y_pool = _mm_rows("mm_a", ypre, Wa)
    (yr, ret_states), (Wup,) = _ret_fwd(proj, g_ret, b_ret, tables, B, S, comm=_Gather([cast["w_up"]]))
    y_ret = _mm_rows("mm_r", yr, Wr)
    mem_n = _rms_fwd("rms_mem", mem2d, g_mem)
    kv = _mm_rows("mm_kv", mem_n, Wkv)
    o_mem, (Wdown,) = _xa_fwd(proj, kv, B, S, M, comm=_Gather([cast["w_down"]]))
    Wdown = Wdown.reshape(FFN_SLABS, UP_SHARD, D_MODEL)
    y_mem = _mm_rows("mm_c", o_mem, Wc)
    ys = (y_pool, y_ret, y_mem)
    merged = _merge_fwd(proj, ys)
    x1 = _mm_rows("mm_out", merged, Wout, res=x2d, tm=MM_ROWS_RES)
    h2 = _rms_fwd("rms_ffn", x1, g_ffn)
    up = _mm_up(h2, Wup).reshape(2, FFN_SLABS, T, UP_SHARD)
    u = _glu_fwd(up, cw, cb, S)
    x2 = _mm_down(u, Wdown, x1)

    dx2, dg_final, loss_part = _loss_head(x2, tgt2d, g_final2)
    received = {}
    d_u = _mm_down_t(dx2, Wdown)
    dW_down = _mm_dw_down(u, dx2)
    (d_up, d_cw, d_cb), (received["w_down"],) = _glu_bwd(
        up, d_u, cw, cb, S, comm=_Exchange([dW_down.reshape(N_DEV, -1, D_MODEL)]))
    d_up = d_up.reshape(N_DEV, T, UP_SHARD)
    dW_up = _mm_dw_up("mm_dw_up", h2, d_up, (0, 1))
    d_h2, (up_c0,) = _mm_up_t(d_up, Wup, comm=_ExchangeTo([dW_up], 0))
    dx1, dg_ffn = _rms_bwd("rms_ffn_bwd", x1, g_ffn, d_h2, dx2)
    d_merged = _mm_rows("mm_out_t", dx1, Wout, kind="nt")
    dW_out = _mm_tn("mm_dw_out", merged, dx1, BF16)
    (d_gl, d_y_pool, d_y_ret, d_y_mem), (up_c1,) = _merge_bwd(proj, ys, d_merged, comm=_ExchangeTo([dW_up], 1))
    received["w_up"] = [up_c0, up_c1]
    dW_c = _mm_tn("mm_dw_c", o_mem, d_y_mem, BF16)
    d_o_mem = _mm_rows("mm_c_t", d_y_mem, Wc, kind="nt")
    (d_qx, d_kmem, d_vmem), (received["w_out"],) = _xa_bwd(
        proj, kv, d_o_mem, B, S, M, comm=_Exchange([dW_out.reshape(N_DEV, -1, D_MODEL)]))
    d_kv = jnp.concatenate([d_kmem, d_vmem], axis=1)
    dW_kv = _mm_tn("mm_dw_kv", mem_n, d_kv, BF16)
    d_mem_n = _mm_rows("mm_kv_t", d_kv, Wkv, kind="nt")
    dg_mem = _rms_bwd("rms_mem_bwd", mem2d, g_mem, d_mem_n, None)
    dW_a = _mm_tn("mm_dw_a", ypre, d_y_pool, BF16)
    d_ypre = _mm_rows("mm_a_t", d_y_pool, Wa, kind="nt")
    d_hp, dw_pool, d_scale = _pool_bwd(proj, d_ypre, wp, pool_scale, B, S)
    dW_r = _mm_tn("mm_dw_r", yr, d_y_ret, BF16)
    d_yr = _mm_rows("mm_r_t", d_y_ret, Wr, kind="nt")
    (d_q, d_k, d_v, d_gr, dg_ret, db_ret), landed = _ret_bwd(
        proj, ret_states, d_yr, g_ret, b_ret, tables, B, S,
        comm=_Exchange([dW_a.reshape(POOL_WIDTH, N_DEV, -1).transpose(1, 0, 2), dW_r.reshape(N_DEV, -1, D_MODEL),
                        dW_c.reshape(XA_WIDTH, N_DEV, -1).transpose(1, 0, 2), dW_kv.reshape(N_DEV, -1, D_MODEL)]))
    received["w_a"], received["w_r"], received["w_c"], received["w_mem_kv"] = landed
    small_names = ["w_pool", "pool_scale", "g_ret", "b_ret", "g_mem", "g_ffn", "conv_b", "g_final"]
    small_grads = [dw_pool, d_scale, dg_ret, db_ret, dg_mem, dg_ffn, d_cb.reshape(1, FFN_HIDDEN), dg_final,
                   d_cw.transpose(1, 0, 2).reshape(3, FFN_HIDDEN), loss_part]
    d_proj = jnp.concatenate([d_hp, d_q, d_k, d_v, d_gr, d_qx, d_gl], axis=1)
    dW_in0 = _mm_tn_slab("mm_dw_in0", h, d_proj, IN_SHARD, BF16, part=(0, 2))
    dW_in1, (in0,) = _mm_tn_slab("mm_dw_in1", h, d_proj, IN_SHARD, BF16, part=(1, 2), comm=_Exchange([dW_in0]))
    d_h, (in1, *small_all) = _mm_cols_slab_t("mm_in_t", d_proj, Win, comm=_Exchange([dW_in1], whole=small_grads))
    received["w_in"] = [in0, in1]
    grad_x, dg_mix = _rms_bwd("rms_mix_bwd", x2d, g_mix, d_h, dx1)
    (g_mix_all,) = _comm_call("gather_g_mix", _Exchange([], whole=[dg_mix]))

    args = dict(g_mix=g_mix, w_in=w_in, w_pool=w_pool, pool_scale=pool_scale, w_a=w_a, g_ret=g_ret, b_ret=b_ret,
                w_r=w_r, g_mem=g_mem, w_mem_kv=w_mem_kv, w_c=w_c, w_out=w_out, g_ffn=g_ffn, w_up=w_up,
                conv_w=conv_w, conv_b=conv_b, w_down=w_down, g_final=g_final)
    m_in = dict(g_mix=m_g_mix, w_in=m_w_in, w_pool=m_w_pool, pool_scale=m_pool_scale, w_a=m_w_a, g_ret=m_g_ret,
                b_ret=m_b_ret, w_r=m_w_r, g_mem=m_g_mem, w_mem_kv=m_w_mem_kv, w_c=m_w_c, w_out=m_w_out,
                g_ffn=m_g_ffn, w_up=m_w_up, conv_w=m_conv_w, conv_b=m_conv_b, w_down=m_w_down, g_final=m_g_final)
    v_in = dict(g_mix=v_g_mix, w_in=v_w_in, w_pool=v_w_pool, pool_scale=v_pool_scale, w_a=v_w_a, g_ret=v_g_ret,
                b_ret=v_b_ret, w_r=v_w_r, g_mem=v_g_mem, w_mem_kv=v_w_mem_kv, w_c=v_w_c, w_out=v_w_out,
                g_ffn=v_g_ffn, w_up=v_w_up, conv_w=v_conv_w, conv_b=v_conv_b, w_down=v_w_down, g_final=v_g_final)

    grads, deltas, new_m, new_v = {}, {}, {}, {}
    for n in names:
        shard = big[n].shape
        parts = received[n] if isinstance(received[n], list) else [received[n]]
        outs = _reduce_adam("adam_" + n, parts, big[n], m_in[n][0], v_in[n][0])
        for store, val in zip((grads, deltas, new_m, new_v), outs):
            store[n] = val.reshape((1,) + shard)

    def as_small(a):
        return a.reshape(a.shape[-3:]) if a.ndim > 2 else a.reshape(1, -1)

    def small_update(call_name, param_names, gathered):
        params = [tuple(as_small(d[n]) for d in (args, m_in, v_in)) for n in param_names]
        sums, updates = _small_adam(call_name, gathered, params)
        for n, g, (d_, m_, v_) in zip(param_names, sums, updates):
            shape = args[n].shape
            grads[n], deltas[n], new_m[n], new_v[n] = (a.reshape(shape) for a in (g, d_, m_, v_))
        return sums[len(param_names):]

    g_cw_full, loss_row = small_update("adam_small", small_names, small_all)
    loss = loss_row[0, 0]
    small_update("adam_g_mix", ["g_mix"], [g_mix_all])

    shard_cols = FFN_HIDDEN // N_DEV
    g_cw = lax.dynamic_slice_in_dim(g_cw_full, me * shard_cols, shard_cols, axis=1)
    d_, m_, v_ = _adam_only("adam_conv_w", g_cw, conv_w[0], m_conv_w[0], v_conv_w[0])
    grads["conv_w"], deltas["conv_w"], new_m["conv_w"], new_v["conv_w"] = g_cw[None], d_[None], m_[None], v_[None]

    order = ["g_mix", "w_in", "w_pool", "pool_scale", "w_a", "g_ret", "b_ret", "w_r", "g_mem", "w_mem_kv", "w_c",
             "w_out", "g_ffn", "w_up", "conv_w", "conv_b", "w_down", "g_final"]
    return (loss, grad_x.reshape(B, S, D_MODEL), *[grads[n] for n in order], *[deltas[n] for n in order],
            *[new_m[n] for n in order], *[new_v[n] for n in order])
```

```python
import functools
import math

import jax
import jax.numpy as jnp
from jax import lax
from jax.experimental import pallas as pl
from jax.experimental.pallas import tpu as pltpu

F32 = jnp.float32
BF16 = jnp.bfloat16

N_DEV = 8
D_MODEL = 1024
POOL_WINDOWS = (2, 4, 8, 16)
POOL_GROUP_DIM = 128
POOL_WIDTH = 512
POOL_HALO = 16
RET_HEADS = 4
RET_QK_DIM = 128
RET_V_DIM = 256
RET_CHUNK = 128
ROPE_BASE = 10000.0
XA_HEADS = 4
XA_HEAD_DIM = 128
XA_WIDTH = 512
IN_WIDTH = 7168
IN_SHARD = IN_WIDTH // N_DEV
FFN_HIDDEN = 2816
UP_SHARD = 2 * FFN_HIDDEN // N_DEV
FFN_SLABS = FFN_HIDDEN // UP_SHARD
EPS = 1e-6
ADAM_LR = 0.001
ADAM_B1 = 0.9
ADAM_B2 = 0.999
ADAM_EPS = 1e-08
ADAM_WD = 0.01
ADAM_STEP = 10
GELU_C = math.sqrt(2.0 / math.pi)
GELU_A = 0.044715
VMEM_LIMIT = 56 * 1024 * 1024
MM_ROWS = 2048
MM_ROWS_RES = 1024
MM_TOKENS = 2048
MESH = pl.DeviceIdType.MESH

COL_Q, COL_K, COL_V, COL_GR, COL_QX, COL_GL = 512, 1024, 1536, 2560, 3584, 4096

_DIMS = {
    "nn": (((1,), (0,)), ((), ())),
    "nt": (((1,), (1,)), ((), ())),
    "tn": (((0,), (0,)), ((), ())),
}


def _dot(a, b, kind="nn"):
    return lax.dot_general(a.astype(BF16), b.astype(BF16), _DIMS[kind], preferred_element_type=F32)


def _params(sem, vmem=VMEM_LIMIT):
    return pltpu.CompilerParams(dimension_semantics=sem, vmem_limit_bytes=vmem)


def _tile(n, pref):
    t = min(n, pref)
    while n % t:
        t //= 2
    return t


def _mesh_pos():
    return lax.axis_index("x"), lax.axis_index("y"), lax.axis_index("c")


def _dev_index(x, y, c):
    return 4 * x + 2 * y + c


def _my_index():
    return _dev_index(*_mesh_pos())


def _remote(src, dst, send_sems, recv_sems, s, to):
    return pltpu.make_async_remote_copy(src_ref=src, dst_ref=dst, send_sem=send_sems.at[s], recv_sem=recv_sems.at[s],
                                        device_id=to, device_id_type=MESH)


class _Gather:
    def __init__(self, shards):
        self.inputs = list(shards)
        self.out_shapes = [jax.ShapeDtypeStruct((N_DEV,) + s.shape, s.dtype) for s in shards]
        n = len(shards)
        self.sem_shapes = [pltpu.SemaphoreType.DMA((7 * n,)), pltpu.SemaphoreType.DMA((7 * n,)),
                           pltpu.SemaphoreType.DMA((n,))]

    def _places(self):
        x, y, c = _mesh_pos()
        return (x, y, c), (x, y, 1 - c), [(1 - x, y), (x, 1 - y), (1 - x, 1 - y)]

    def _local(self, src, dst, sems):
        me = _my_index()
        return [pltpu.make_async_copy(src[w], dst[w].at[me], sems[2].at[w]) for w in range(len(src))]

    def start(self, src, dst, sems):
        me, sib, chips = self._places()
        for cp in self._local(src, dst, sems):
            cp.start()
        for w in range(len(src)):
            land = dst[w].at[_dev_index(*me)]
            _remote(src[w], land, sems[0], sems[1], 7 * w, sib).start()
            for j, chip in enumerate(chips):
                _remote(src[w], land, sems[0], sems[1], 7 * w + 1 + j, (*chip, me[2])).start()

    def finish(self, src, dst, sems):
        me, sib, chips = self._places()
        n = len(src)
        for j, chip in enumerate(chips):
            for w in range(n):
                block = dst[w].at[_dev_index(*chip, me[2])]
                _remote(src[w], block, sems[0], sems[1], 7 * w + 1 + j, me).wait_recv()
                _remote(block, block, sems[0], sems[1], 7 * w + 4 + j, sib).start()
        for w in range(n):
            _remote(src[w], dst[w].at[_dev_index(*sib)], sems[0], sems[1], 7 * w, me).wait_recv()
            for j, chip in enumerate(chips):
                block = dst[w].at[_dev_index(*chip, sib[2])]
                _remote(block, block, sems[0], sems[1], 7 * w + 4 + j, me).wait_recv()
            for k in range(7):
                _remote(src[w], dst[w].at[0], sems[0], sems[1], 7 * w + k, me).wait_send()
        for cp in self._local(src, dst, sems):
            cp.wait()


class _Exchange:
    def __init__(self, partials, whole=()):
        self.n_part = len(partials)
        self.inputs = list(partials) + list(whole)
        self.out_shapes = [jax.ShapeDtypeStruct(p.shape, p.dtype) for p in partials]
        self.out_shapes += [jax.ShapeDtypeStruct((N_DEV,) + a.shape, a.dtype) for a in whole]
        n = len(self.inputs)
        self.sem_shapes = [pltpu.SemaphoreType.DMA((7 * n,)), pltpu.SemaphoreType.DMA((7 * n,)),
                           pltpu.SemaphoreType.DMA((n,))]

    def _peer(self, k):
        x, y, c = _mesh_pos()
        p = (x ^ ((k >> 2) & 1), y ^ ((k >> 1) & 1), c ^ (k & 1))
        return p, _dev_index(*p)

    def _source(self, src, w, slot):
        return src[w].at[slot] if w < self.n_part else src[w]

    def _local(self, src, dst, sems):
        me = _my_index()
        return [pltpu.make_async_copy(self._source(src, w, me), dst[w].at[me], sems[2].at[w])
                for w in range(len(src))]

    def start(self, src, dst, sems):
        me = _my_index()
        for cp in self._local(src, dst, sems):
            cp.start()
        for k in range(1, N_DEV):
            peer, peer_idx = self._peer(k)
            for w in range(len(src)):
                _remote(self._source(src, w, peer_idx), dst[w].at[me], sems[0], sems[1], 7 * w + k - 1, peer).start()

    def finish(self, src, dst, sems):
        for k in range(1, N_DEV):
            peer, peer_idx = self._peer(k)
            for w in range(len(src)):
                cp = _remote(self._source(src, w, peer_idx), dst[w].at[peer_idx], sems[0], sems[1], 7 * w + k - 1, peer)
                cp.wait_send()
                cp.wait_recv()
        for cp in self._local(src, dst, sems):
            cp.wait()


class _ExchangeTo:
    def __init__(self, partials, side):
        self.side = side
        self.inputs = list(partials)
        self.out_shapes = [jax.ShapeDtypeStruct(p.shape, p.dtype) for p in partials]
        n = len(partials)
        self.sem_shapes = [pltpu.SemaphoreType.DMA((7 * n,)), pltpu.SemaphoreType.DMA((7 * n,)),
                           pltpu.SemaphoreType.DMA((n,))]

    def _copies(self, src, dst, sems):
        x, y, c = _mesh_pos()
        me = _dev_index(x, y, c)
        receives = c == self.side
        remote = []
        for k in range(1, N_DEV):
            kx, ky, kc = (k >> 2) & 1, (k >> 1) & 1, k & 1
            peer = (x ^ kx, y ^ ky, c ^ kc)
            peer_idx = _dev_index(*peer)
            sends = c == (self.side ^ kc)
            for w in range(len(src)):
                slab = src[w].at[peer_idx]
                s = 7 * w + k - 1
                remote.append((sends, _remote(slab, dst[w].at[me], sems[0], sems[1], s, peer),
                               _remote(slab, dst[w].at[peer_idx], sems[0], sems[1], s, peer)))
        local = [pltpu.make_async_copy(src[w].at[me], dst[w].at[me], sems[2].at[w]) for w in range(len(src))]
        return receives, remote, local

    def start(self, src, dst, sems):
        receives, remote, local = self._copies(src, dst, sems)

        @pl.when(receives)
        def _():
            for cp in local:
                cp.start()

        for sends, send, _ in remote:
            pl.when(sends)(send.start)

    def finish(self, src, dst, sems):
        receives, remote, local = self._copies(src, dst, sems)
        for sends, send, arrive in remote:
            pl.when(sends)(send.wait_send)
            pl.when(receives)(arrive.wait_recv)

        @pl.when(receives)
        def _():
            for cp in local:
                cp.wait()


class _Both:
    def __init__(self, first, second):
        self.plans = (first, second)
        self.inputs = first.inputs + second.inputs
        self.out_shapes = first.out_shapes + second.out_shapes
        self.sem_shapes = first.sem_shapes + second.sem_shapes

    def _split(self, src, dst, sems):
        a = self.plans[0]
        ni, no = len(a.inputs), len(a.out_shapes)
        return (src[:ni], dst[:no], sems[:3]), (src[ni:], dst[no:], sems[3:])

    def start(self, src, dst, sems):
        for plan, part in zip(self.plans, self._split(src, dst, sems)):
            plan.start(*part)

    def finish(self, src, dst, sems):
        for plan, part in zip(self.plans, self._split(src, dst, sems)):
            plan.finish(*part)


def _pcall(body, args, *, name, out_shape, grid, in_specs, out_specs, scratch_shapes=(), sem=None, comm=None):
    single = not isinstance(out_shape, (tuple, list))
    outs = [out_shape] if single else list(out_shape)
    ospecs = [out_specs] if single else list(out_specs)
    n_in, n_out, n_scr = len(args), len(outs), len(scratch_shapes)

    def pick(res):
        return res[0] if single else tuple(res[:n_out])

    if comm is None:
        res = pl.pallas_call(
            body, out_shape=outs, grid=grid, in_specs=list(in_specs), out_specs=ospecs,
            scratch_shapes=list(scratch_shapes), name=name, compiler_params=_params(sem),
        )(*args)
        return pick(res), ()

    nci, nco = len(comm.inputs), len(comm.out_shapes)

    def carrier(*refs):
        at = 0
        parts = []
        for size in (n_in, nci, n_out, nco, n_scr, len(comm.sem_shapes)):
            parts.append(refs[at:at + size])
            at += size
        ins, cins, o, couts, scr, sems = parts
        ids = [pl.program_id(a) for a in range(len(grid))]
        first = functools.reduce(jnp.logical_and, [i == 0 for i in ids])
        last = functools.reduce(jnp.logical_and, [i == g - 1 for i, g in zip(ids, grid)])

        body(*ins, *o, *scr)

        @pl.when(first)
        def _():
            comm.start(cins, couts, sems)

        @pl.when(last)
        def _():
            comm.finish(cins, couts, sems)

    hbm = pl.BlockSpec(memory_space=pltpu.HBM)
    res = pl.pallas_call(
        carrier, out_shape=outs + comm.out_shapes, grid=grid, in_specs=list(in_specs) + [hbm] * nci,
        out_specs=ospecs + [hbm] * nco, scratch_shapes=list(scratch_shapes) + comm.sem_shapes, name=name,
        compiler_params=_params(("arbitrary",) * len(grid)),
    )(*args, *comm.inputs)
    return pick(res), tuple(res[n_out:])


def _comm_call(name, comm):
    def body(*refs):
        nci, nco = len(comm.inputs), len(comm.out_shapes)
        cins, couts, sems = refs[:nci], refs[nci:nci + nco], refs[nci + nco:]
        comm.start(cins, couts, sems)
        comm.finish(cins, couts, sems)

    hbm = pl.BlockSpec(memory_space=pltpu.HBM)
    return pl.pallas_call(
        body, out_shape=comm.out_shapes, in_specs=[hbm] * len(comm.inputs), out_specs=[hbm] * len(comm.out_shapes),
        scratch_shapes=comm.sem_shapes, name=name,
    )(*comm.inputs)


def _matmul(name, kind, a, b, out_shape, grid, a_spec, b_spec, o_spec, acc_shape, res=None, res_spec=None,
            comm=None, epilogue=None):
    nk = grid[-1]
    if epilogue is None:
        extra, extra_specs = ([res], [res_spec]) if res is not None else ([], [])
        n_out = 1
    else:
        extra, extra_specs, n_out = list(res), list(res_spec), len(out_shape)
    n_in = 2 + len(extra)

    def body(*refs):
        a_ref, b_ref = refs[0], refs[1]
        extra_refs, out_refs = refs[2:n_in], refs[n_in:n_in + n_out]

        def prod():
            return _dot(a_ref[...], b_ref[...], kind)

        def finish(acc):
            if epilogue is not None:
                ids = [pl.program_id(ax) for ax in range(len(grid) - 1)]
                first = functools.reduce(jnp.logical_and, [i == 0 for i in ids]) if ids else True
                epilogue(acc, extra_refs, out_refs, first)
                return
            if extra_refs:
                acc = acc + extra_refs[0][...]
            out_refs[0][...] = acc.astype(out_refs[0].dtype)

        if nk == 1:
            finish(prod())
        else:
            acc_ref = refs[n_in + n_out]
            k = pl.program_id(len(grid) - 1)

            @pl.when(k == 0)
            def _():
                acc_ref[...] = prod()

            @pl.when(k > 0)
            def _():
                acc_ref[...] += prod()

            @pl.when(k == nk - 1)
            def _():
                finish(acc_ref[...])

    in_specs = [a_spec, b_spec] + extra_specs
    args = (a, b, *extra)
    scratch = [pltpu.VMEM(acc_shape, F32)] if nk > 1 else []
    sem = ("arbitrary",) * len(grid) if epilogue is not None else ("parallel",) * (len(grid) - 1) + ("arbitrary",)
    out, landed = _pcall(body, args, name=name, out_shape=out_shape, grid=grid, in_specs=in_specs,
                         out_specs=o_spec, scratch_shapes=scratch, sem=sem, comm=comm)
    return out if comm is None else (out, landed)


def _mm_rows(name, a, w, out_dtype=F32, res=None, kind="nn", tm=MM_ROWS, comm=None):
    M, K = a.shape
    N = w.shape[1] if kind == "nn" else w.shape[0]
    tm = _tile(M, tm)
    res_spec = pl.BlockSpec((tm, N), lambda i, k: (i, 0)) if res is not None else None
    return _matmul(
        name, kind, a, w, jax.ShapeDtypeStruct((M, N), out_dtype), (M // tm, 1),
        pl.BlockSpec((tm, K), lambda i, k: (i, 0)), pl.BlockSpec(w.shape, lambda i, k: (0, 0)),
        pl.BlockSpec((tm, N), lambda i, k: (i, 0)), (tm, N), res, res_spec, comm)


def _residual_rms_epilogue(y, operands, outputs, first):
    x_ref, g_ref = operands
    x1_ref, h_ref = outputs
    xv = x_ref[...] + y
    x1_ref[...] = xv
    r = lax.rsqrt(jnp.mean(xv * xv, axis=-1, keepdims=True) + EPS)
    h_ref[...] = (xv * r * g_ref[...]).astype(h_ref.dtype)


def _mm_residual_rms(name, a, w, x, g, tm=MM_ROWS_RES):
    M, K = a.shape
    N = w.shape[1]
    tm = _tile(M, tm)
    row = pl.BlockSpec((tm, N), lambda i, k: (i, 0))
    return _matmul(
        name, "nn", a, w, (jax.ShapeDtypeStruct((M, N), F32), jax.ShapeDtypeStruct((M, N), BF16)), (M // tm, 1),
        pl.BlockSpec((tm, K), lambda i, k: (i, 0)), pl.BlockSpec(w.shape, lambda i, k: (0, 0)),
        (row, row), (tm, N), [x, g], [row, pl.BlockSpec((1, N), lambda i, k: (0, 0))],
        epilogue=_residual_rms_epilogue)


def _mm_tn(name, a, b, out_dtype=F32, tk=MM_TOKENS, comm=None):
    T, M = a.shape
    N = b.shape[1]
    tk = _tile(T, tk)
    return _matmul(
        name, "tn", a, b, jax.ShapeDtypeStruct((M, N), out_dtype), (1, T // tk),
        pl.BlockSpec((tk, M), lambda i, k: (k, 0)), pl.BlockSpec((tk, N), lambda i, k: (k, 0)),
        pl.BlockSpec((M, N), lambda i, k: (0, 0)), (M, N), comm=comm)


def _mm_cols_slab(name, a, w_slabs, out_dtype=F32, tm=MM_ROWS, comm=None):
    M, K = a.shape
    J, _, n = w_slabs.shape
    tm = _tile(M, tm)
    return _matmul(
        name, "nn", a, w_slabs, jax.ShapeDtypeStruct((M, J * n), out_dtype), (J, M // tm, 1),
        pl.BlockSpec((tm, K), lambda j, i, k: (i, 0)), pl.BlockSpec((None, K, n), lambda j, i, k: (j, 0, 0)),
        pl.BlockSpec((tm, n), lambda j, i, k: (i, j)), (tm, n), comm=comm)


def _mm_in_gather(h, shard, tm=MM_ROWS, comm=None):
    T, K = h.shape
    n = shard.shape[1]
    tm = _tile(T, tm)
    n_tiles = T // tm
    pair_of_chip_step = {4: 1, 2: 2, 6: 3}

    def slab_of(s):
        x, y, c = _mesh_pos()
        return _dev_index(x ^ ((s >> 2) & 1), y ^ ((s >> 1) & 1), c ^ (s & 1))

    def body(h_ref, shard_ref, proj_ref, win_ref, wbuf, slot_sems, send_sems, recv_sems, local_sem):
        s, i = pl.program_id(0), pl.program_id(1)
        x, y, c = _mesh_pos()
        me, sib = (x, y, c), (x, y, 1 - c)

        def slot_copy(step):
            src = shard_ref if step == 0 else win_ref.at[slab_of(step)]
            return pltpu.make_async_copy(src, wbuf.at[step % 2], slot_sems.at[step % 2])

        def fetch(step):
            if step >= 1:
                block = win_ref.at[slab_of(step)]
                if step == 1:
                    pair = 0
                elif step % 2 == 0:
                    pair = pair_of_chip_step[step]
                else:
                    pair = 3 + pair_of_chip_step[step - 1]
                _remote(block, block, send_sems, recv_sems, pair, me).wait_recv()
                if step % 2 == 0:
                    _remote(block, block, send_sems, recv_sems, 3 + pair, sib).start()
            slot_copy(step).start()

        @pl.when((s == 0) & (i == 0))
        def _():
            land = win_ref.at[_dev_index(*me)]
            pltpu.make_async_copy(shard_ref, land, local_sem).start()
            _remote(shard_ref, land, send_sems, recv_sems, 0, sib).start()
            for step, pair in pair_of_chip_step.items():
                peer = (x ^ ((step >> 2) & 1), y ^ ((step >> 1) & 1), c)
                _remote(shard_ref, land, send_sems, recv_sems, pair, peer).start()
            fetch(0)

        for step in range(N_DEV):
            @pl.when((s == step) & (i == 0))
            def _():
                slot_copy(step).wait()

            if step + 1 < N_DEV:
                @pl.when((s == step) & (i == n_tiles - 1))
                def _():
                    fetch(step + 1)

        proj_ref[...] = _dot(h_ref[...], wbuf[s % 2])

        @pl.when((s == N_DEV - 1) & (i == n_tiles - 1))
        def _():
            for pair in range(7):
                _remote(shard_ref, win_ref.at[0], send_sems, recv_sems, pair, me).wait_send()
            pltpu.make_async_copy(shard_ref, win_ref.at[_dev_index(*me)], local_sem).wait()

    hbm = pl.BlockSpec(memory_space=pltpu.HBM)
    return _pcall(
        body, (h, shard), name="mm_in",
        out_shape=(jax.ShapeDtypeStruct((T, N_DEV * n), F32), jax.ShapeDtypeStruct((N_DEV, K, n), shard.dtype)),
        grid=(N_DEV, n_tiles), in_specs=[pl.BlockSpec((tm, K), lambda s, i: (i, 0)), hbm],
        out_specs=(pl.BlockSpec((tm, n), lambda s, i: (i, slab_of(s))), hbm),
        scratch_shapes=[pltpu.VMEM((2, K, n), shard.dtype), pltpu.SemaphoreType.DMA((2,)),
                        pltpu.SemaphoreType.DMA((7,)), pltpu.SemaphoreType.DMA((7,)), pltpu.SemaphoreType.DMA],
        sem=("arbitrary", "arbitrary"), comm=comm)


def _rms_bwd_epilogue(dh, operands, outputs, first):
    x_ref, g_ref, dres_ref = operands
    dx_ref, dg_ref = outputs
    xv = x_ref[...]
    r = lax.rsqrt(jnp.mean(xv * xv, axis=-1, keepdims=True) + EPS)
    xhat = xv * r

    @pl.when(first)
    def _():
        dg_ref[...] = jnp.zeros_like(dg_ref)

    dg_ref[...] += jnp.sum(dh * xhat, axis=0, keepdims=True)
    dxhat = dh * g_ref[...]
    dx_ref[...] = dres_ref[...] + r * (dxhat - xhat * jnp.mean(dxhat * xhat, axis=-1, keepdims=True))


def _rms_bwd_fused(M, K, tm, rms):
    row = pl.BlockSpec((tm, K), lambda i, j: (i, 0))
    vec = pl.BlockSpec((1, K), lambda i, j: (0, 0))
    x, g, dres = rms
    return dict(res=[x, g, dres], res_spec=[row, vec, row], epilogue=_rms_bwd_epilogue,
                out_shape=(jax.ShapeDtypeStruct((M, K), F32), jax.ShapeDtypeStruct((1, K), F32)), o_spec=(row, vec))


def _mm_cols_slab_t(name, a, w_slabs, rms, tm=MM_ROWS_RES, comm=None):
    M = a.shape[0]
    J, K, n = w_slabs.shape
    tm = _tile(M, tm)
    fused = _rms_bwd_fused(M, K, tm, rms)
    return _matmul(
        name, "nt", a, w_slabs, fused.pop("out_shape"), (M // tm, J),
        pl.BlockSpec((tm, n), lambda i, j: (i, j)), pl.BlockSpec((None, K, n), lambda i, j: (j, 0, 0)),
        fused.pop("o_spec"), (tm, K), comm=comm, **fused)


def _mm_in_t(d_lo, d_hi, w_slabs, tm=MM_ROWS, comm=None):
    M = d_lo.shape[0]
    J, K, n = w_slabs.shape
    half = J // 2
    tm = _tile(M, tm)

    def body(lo_ref, hi_ref, w_ref, o_ref, acc_ref):
        j = pl.program_id(1)

        @pl.when(j == 0)
        def _():
            acc_ref[...] = _dot(lo_ref[...], w_ref[...], "nt")

        @pl.when((j > 0) & (j < half))
        def _():
            acc_ref[...] += _dot(lo_ref[...], w_ref[...], "nt")

        @pl.when(j >= half)
        def _():
            acc_ref[...] += _dot(hi_ref[...], w_ref[...], "nt")

        @pl.when(j == J - 1)
        def _():
            o_ref[...] = acc_ref[...]

    out, landed = _pcall(
        body, (d_lo, d_hi, w_slabs), name="mm_in_t", out_shape=jax.ShapeDtypeStruct((M, K), F32), grid=(M // tm, J),
        in_specs=[pl.BlockSpec((tm, n), lambda i, j: (i, jnp.minimum(j, half - 1))),
                  pl.BlockSpec((tm, n), lambda i, j: (i, jnp.maximum(j - half, 0))),
                  pl.BlockSpec((None, K, n), lambda i, j: (j, 0, 0))],
        out_specs=pl.BlockSpec((tm, K), lambda i, j: (i, 0)), scratch_shapes=[pltpu.VMEM((tm, K), F32)],
        sem=("parallel", "arbitrary"), comm=comm)
    return out if comm is None else (out, landed)


def _mm_tn_slab(name, a, b, n, out_dtype=F32, tk=MM_TOKENS, comm=None, part=(0, 1)):
    T, M = a.shape
    p, of = part
    M //= of
    J = b.shape[1] // n
    tk = _tile(T, tk)
    return _matmul(
        name, "tn", a, b, jax.ShapeDtypeStruct((J, M, n), out_dtype), (J, T // tk),
        pl.BlockSpec((tk, M), lambda j, k: (k, p)), pl.BlockSpec((tk, n), lambda j, k: (k, j)),
        pl.BlockSpec((None, M, n), lambda j, k: (j, 0, 0)), (M, n), comm=comm)


def _rms_fwd(name, x, g, tm=512):
    T, Dm = x.shape
    tm = _tile(T, tm)

    def body(x_ref, g_ref, h_ref):
        xv = x_ref[...]
        r = lax.rsqrt(jnp.mean(xv * xv, axis=-1, keepdims=True) + EPS)
        h_ref[...] = (xv * r * g_ref[...]).astype(h_ref.dtype)

    return pl.pallas_call(
        body, out_shape=jax.ShapeDtypeStruct((T, Dm), BF16), grid=(T // tm,),
        in_specs=[pl.BlockSpec((tm, Dm), lambda i: (i, 0)), pl.BlockSpec((1, Dm), lambda i: (0, 0))],
        out_specs=pl.BlockSpec((tm, Dm), lambda i: (i, 0)), name=name, compiler_params=_params(("parallel",)),
    )(x, g)


def _rms_bwd(name, x, g, dh, dres, tm=512):
    T, Dm = x.shape
    tm = _tile(T, tm)
    want_dx = dres is not None

    def body(*refs):
        if want_dx:
            x_ref, g_ref, dh_ref, dres_ref, dx_ref, dg_ref = refs
        else:
            x_ref, g_ref, dh_ref, dg_ref = refs
        xv = x_ref[...]
        r = lax.rsqrt(jnp.mean(xv * xv, axis=-1, keepdims=True) + EPS)
        xhat = xv * r
        dhv = dh_ref[...]

        @pl.when(pl.program_id(0) == 0)
        def _():
            dg_ref[...] = jnp.zeros_like(dg_ref)

        dg_ref[...] += jnp.sum(dhv * xhat, axis=0, keepdims=True)
        if want_dx:
            dxhat = dhv * g_ref[...]
            dx_ref[...] = dres_ref[...] + r * (dxhat - xhat * jnp.mean(dxhat * xhat, axis=-1, keepdims=True))

    row = pl.BlockSpec((tm, Dm), lambda i: (i, 0))
    vec = pl.BlockSpec((1, Dm), lambda i: (0, 0))
    if want_dx:
        return pl.pallas_call(
            body, out_shape=(jax.ShapeDtypeStruct((T, Dm), F32), jax.ShapeDtypeStruct((1, Dm), F32)),
            grid=(T // tm,), in_specs=[row, vec, row, row], out_specs=(row, vec), name=name,
            compiler_params=_params(("arbitrary",)),
        )(x, g, dh, dres)
    return pl.pallas_call(
        body, out_shape=jax.ShapeDtypeStruct((1, Dm), F32), grid=(T // tm,), in_specs=[row, vec, row],
        out_specs=vec, name=name, compiler_params=_params(("arbitrary",)),
    )(x, g, dh)


def _pool_rows(S):
    return _tile(S, 256)


def _pool_count(c0, rows, w):
    t = c0 + lax.broadcasted_iota(jnp.int32, (rows, 1), 0)
    return jnp.minimum(t + 1, w).astype(F32)


def _pool_fwd(proj, w_pool, scale, B, S):
    CH = _pool_rows(S)

    def body(hp_ref, wp_ref, sc_ref, o_ref, pad_ref):
        pad_ref[0:POOL_HALO, :] = jnp.zeros((POOL_HALO, POOL_WIDTH), F32)
        pad_ref[POOL_HALO:, :] = hp_ref[...]
        for gi, w in enumerate(POOL_WINDOWS):
            cols = slice(gi * POOL_GROUP_DIM, (gi + 1) * POOL_GROUP_DIM)
            for c in range(S // CH):
                base = POOL_HALO + c * CH
                acc = pad_ref[base:base + CH, cols]
                tok = acc
                for j in range(1, w):
                    acc = acc + pad_ref[base - j:base - j + CH, cols]
                pooled = acc / _pool_count(c * CH, CH, w) - tok
                z = _dot(pooled, wp_ref[gi])
                o_ref[c * CH:(c + 1) * CH, cols] = (z * sc_ref[:, cols]).astype(o_ref.dtype)

    return pl.pallas_call(
        body, out_shape=jax.ShapeDtypeStruct((B * S, POOL_WIDTH), BF16), grid=(B,),
        in_specs=[pl.BlockSpec((S, POOL_WIDTH), lambda b: (b, 0)),
                  pl.BlockSpec(w_pool.shape, lambda b: (0, 0, 0)),
                  pl.BlockSpec((1, POOL_WIDTH), lambda b: (0, 0))],
        out_specs=pl.BlockSpec((S, POOL_WIDTH), lambda b: (b, 0)),
        scratch_shapes=[pltpu.VMEM((S + POOL_HALO, POOL_WIDTH), F32)],
        name="pool_fwd", compiler_params=_params(("parallel",)),
    )(proj, w_pool, scale)


def _pool_bwd(proj, d_ypre, w_pool, scale, B, S):
    CH = _pool_rows(S)

    def body(hp_ref, dy_ref, wp_ref, sc_ref, dhp_ref, dwp_ref, dsc_ref, pad_ref, sc_pad_ref, dp_ref):
        @pl.when(pl.program_id(0) == 0)
        def _():
            dwp_ref[...] = jnp.zeros_like(dwp_ref)
            dsc_ref[...] = jnp.zeros_like(dsc_ref)

        pad_ref[0:POOL_HALO, :] = jnp.zeros((POOL_HALO, POOL_WIDTH), F32)
        pad_ref[POOL_HALO:, :] = hp_ref[...]
        sc_pad_ref[S:, :] = jnp.zeros((POOL_HALO, POOL_WIDTH), F32)
        for gi, w in enumerate(POOL_WINDOWS):
            cols = slice(gi * POOL_GROUP_DIM, (gi + 1) * POOL_GROUP_DIM)
            for c in range(S // CH):
                base = POOL_HALO + c * CH
                rows = slice(c * CH, (c + 1) * CH)
                acc = pad_ref[base:base + CH, cols]
                tok = acc
                for j in range(1, w):
                    acc = acc + pad_ref[base - j:base - j + CH, cols]
                cnt = _pool_count(c * CH, CH, w)
                pooled = acc / cnt - tok
                z = _dot(pooled, wp_ref[gi])
                dy = dy_ref[rows, cols]
                dsc_ref[:, cols] += jnp.sum(dy * z, axis=0, keepdims=True)
                dz = dy * sc_ref[:, cols]
                dwp_ref[gi] += _dot(pooled, dz, "tn")
                dpool = _dot(dz, wp_ref[gi], "nt")
                dp_ref[rows, cols] = dpool
                sc_pad_ref[rows, cols] = dpool / cnt
            for c in range(S // CH):
                rows = slice(c * CH, (c + 1) * CH)
                acc = sc_pad_ref[rows, cols]
                for j in range(1, w):
                    acc = acc + sc_pad_ref[c * CH + j:c * CH + j + CH, cols]
                dhp_ref[rows, cols] = (acc - dp_ref[rows, cols]).astype(dhp_ref.dtype)

    seq = pl.BlockSpec((S, POOL_WIDTH), lambda b: (b, 0))
    return pl.pallas_call(
        body,
        out_shape=(jax.ShapeDtypeStruct((B * S, POOL_WIDTH), BF16),
                   jax.ShapeDtypeStruct(w_pool.shape, F32), jax.ShapeDtypeStruct((1, POOL_WIDTH), F32)),
        grid=(B,),
        in_specs=[seq, seq, pl.BlockSpec(w_pool.shape, lambda b: (0, 0, 0)),
                  pl.BlockSpec((1, POOL_WIDTH), lambda b: (0, 0))],
        out_specs=(seq, pl.BlockSpec(w_pool.shape, lambda b: (0, 0, 0)),
                   pl.BlockSpec((1, POOL_WIDTH), lambda b: (0, 0))),
        scratch_shapes=[pltpu.VMEM((S + POOL_HALO, POOL_WIDTH), F32),
                        pltpu.VMEM((S + POOL_HALO, POOL_WIDTH), F32),
                        pltpu.VMEM((S, POOL_WIDTH), F32)],
        name="pool_bwd", compiler_params=_params(("arbitrary",)),
    )(proj, d_ypre, w_pool, scale)


def _ret_tables(S):
    half = RET_QK_DIM // 2
    inv = ROPE_BASE ** (-jnp.arange(half, dtype=F32) / half)
    ang = jnp.arange(S, dtype=F32)[:, None] * inv[None, :]
    cos, sin = jnp.cos(ang), jnp.sin(ang)
    cos_full = jnp.concatenate([cos, cos], axis=-1)
    sin_signed = jnp.concatenate([-sin, sin], axis=-1)
    C = RET_CHUNK
    lg = jnp.log1p(-jnp.exp2(-5.0 - jnp.arange(RET_HEADS, dtype=F32)))[:, None, None]
    idx = jnp.arange(C, dtype=F32)
    rel = idx[:, None] - idx[None, :]
    decay = jnp.where(rel >= 0, jnp.exp(jnp.maximum(rel, 0.0) * lg), 0.0)
    q_decay = jnp.broadcast_to(jnp.exp((idx + 1.0)[None, :, None] * lg), (RET_HEADS, C, RET_QK_DIM))
    k_decay = jnp.broadcast_to(jnp.exp((C - 1.0 - idx)[None, :, None] * lg), (RET_HEADS, C, RET_QK_DIM))
    c_decay = jnp.broadcast_to(jnp.exp(C * lg), (RET_HEADS, 1, RET_V_DIM))
    return cos_full, sin_signed, decay, q_decay, k_decay, c_decay


def _rope(x, cos_full, sin_signed):
    return x * cos_full + pltpu.roll(x, RET_QK_DIM // 2, axis=1) * sin_signed


def _rope_t(dy, cos_full, sin_signed):
    return dy * cos_full + pltpu.roll(dy * sin_signed, RET_QK_DIM // 2, axis=1)


RET_COLS = 512


def _ret_specs(N, chunk_of):
    C = RET_CHUNK

    def rows(width, col=0):
        return pl.BlockSpec((C, width), lambda b, i: (b * N + chunk_of(i), col))

    def whole(shape):
        return pl.BlockSpec(shape, lambda b, i: (0,) * len(shape))

    wide = RET_HEADS * RET_V_DIM
    return dict(
        q=rows(RET_COLS, COL_Q // RET_COLS), k=rows(RET_COLS, COL_K // RET_COLS),
        v=[rows(RET_COLS, COL_V // RET_COLS + j) for j in range(2)],
        gr=[rows(RET_COLS, COL_GR // RET_COLS + j) for j in range(2)],
        table=pl.BlockSpec((C, RET_QK_DIM), lambda b, i: (chunk_of(i), 0)),
        decay=whole((RET_HEADS, C, C)), qd=whole((RET_HEADS, C, RET_QK_DIM)), kd=whole((RET_HEADS, C, RET_QK_DIM)),
        cd=whole((RET_HEADS, 1, RET_V_DIM)), vec=whole((1, wide)), qk_rows=rows(RET_COLS), v_rows=rows(wide),
        state=pl.BlockSpec((None, None, RET_HEADS, RET_QK_DIM, RET_V_DIM), lambda b, i: (b, chunk_of(i), 0, 0, 0)))


def _head_cols(h):
    pair = slice((h % 2) * RET_V_DIM, (h % 2 + 1) * RET_V_DIM)
    return slice(h * RET_QK_DIM, (h + 1) * RET_QK_DIM), h // 2, pair, slice(h * RET_V_DIM, (h + 1) * RET_V_DIM)


def _group_norm(o):
    mu = jnp.mean(o, axis=-1, keepdims=True)
    oc = o - mu
    rstd = lax.rsqrt(jnp.mean(oc * oc, axis=-1, keepdims=True) + EPS)
    return oc * rstd, rstd


def _ret_fwd(proj, g_ret, b_ret, tables, B, S, comm=None):
    N = S // RET_CHUNK
    cos_t, sin_t, decay, q_decay, k_decay, c_decay = tables
    sp = _ret_specs(N, lambda i: i)

    def body(q_ref, k_ref, v0_ref, v1_ref, gr0_ref, gr1_ref, cos_ref, sin_ref, dec_ref, qd_ref, kd_ref, cd_ref,
             g_ref, b_ref, y_ref, rs_ref, r_ref):
        @pl.when(pl.program_id(1) == 0)
        def _():
            r_ref[...] = jnp.zeros_like(r_ref)

        cs, sn = cos_ref[...], sin_ref[...]
        for h in range(RET_HEADS):
            qk, j, pair, wide = _head_cols(h)
            q = _rope(q_ref[:, qk], cs, sn)
            k = _rope(k_ref[:, qk], cs, sn) * (RET_QK_DIM ** -0.5)
            v = (v0_ref, v1_ref)[j][:, pair]
            R = r_ref[h]
            rs_ref[h] = R
            s = _dot(q, k, "nt") * dec_ref[h]
            o = _dot(s, v) + _dot(q * qd_ref[h], R)
            r_ref[h] = cd_ref[h] * R + _dot(k * kd_ref[h], v, "tn")
            on, _ = _group_norm(o)
            gr = (gr0_ref, gr1_ref)[j][:, pair]
            y_ref[:, wide] = (gr * jax.nn.sigmoid(gr) * (on * g_ref[:, wide] + b_ref[:, wide])).astype(y_ref.dtype)

    state = jax.ShapeDtypeStruct((B, N, RET_HEADS, RET_QK_DIM, RET_V_DIM), F32)
    return _pcall(
        body, (proj,) * 6 + (cos_t, sin_t, decay, q_decay, k_decay, c_decay, g_ret, b_ret),
        name="ret_fwd", out_shape=(jax.ShapeDtypeStruct((B * S, RET_HEADS * RET_V_DIM), BF16), state), grid=(B, N),
        in_specs=[sp["q"], sp["k"], *sp["v"], *sp["gr"], sp["table"], sp["table"], sp["decay"], sp["qd"],
                  sp["kd"], sp["cd"], sp["vec"], sp["vec"]],
        out_specs=(sp["v_rows"], sp["state"]),
        scratch_shapes=[pltpu.VMEM((RET_HEADS, RET_QK_DIM, RET_V_DIM), F32)],
        sem=("parallel", "arbitrary"), comm=comm)


def _ret_bwd(proj, states, d_yr, g_ret, b_ret, tables, B, S, comm=None):
    N = S // RET_CHUNK
    cos_t, sin_t, decay, q_decay, k_decay, c_decay = tables
    sp = _ret_specs(N, lambda i: N - 1 - i)
    qk_scale = RET_QK_DIM ** -0.5

    def body(q_ref, k_ref, v0_ref, v1_ref, gr0_ref, gr1_ref, dy_ref, rs_ref, cos_ref, sin_ref, dec_ref, qd_ref,
             kd_ref, cd_ref, g_ref, b_ref, dq_ref, dk_ref, dv_ref, dgr_ref, dg_ref, db_ref, dr_ref):
        @pl.when((pl.program_id(0) == 0) & (pl.program_id(1) == 0))
        def _():
            dg_ref[...] = jnp.zeros_like(dg_ref)
            db_ref[...] = jnp.zeros_like(db_ref)

        @pl.when(pl.program_id(1) == 0)
        def _():
            dr_ref[...] = jnp.zeros_like(dr_ref)

        cs, sn = cos_ref[...], sin_ref[...]
        for h in range(RET_HEADS):
            qk, j, pair, wide = _head_cols(h)
            q = _rope(q_ref[:, qk], cs, sn)
            k = _rope(k_ref[:, qk], cs, sn) * qk_scale
            v = (v0_ref, v1_ref)[j][:, pair]
            R, dR = rs_ref[h], dr_ref[h]
            dec, qd, kd = dec_ref[h], qd_ref[h], kd_ref[h]
            s = _dot(q, k, "nt") * dec
            o = _dot(s, v) + _dot(q * qd, R)
            on, rstd = _group_norm(o)
            g = g_ref[:, wide]
            oaff = on * g + b_ref[:, wide]
            gr = (gr0_ref, gr1_ref)[j][:, pair]
            sg = jax.nn.sigmoid(gr)
            dy = dy_ref[:, wide]
            dgr_ref[:, wide] = (dy * oaff * (sg * (1.0 + gr * (1.0 - sg)))).astype(dgr_ref.dtype)
            doaff = dy * (gr * sg)
            dg_ref[:, wide] += jnp.sum(doaff * on, axis=0, keepdims=True)
            db_ref[:, wide] += jnp.sum(doaff, axis=0, keepdims=True)
            don = doaff * g
            do = rstd * (don - jnp.mean(don, axis=-1, keepdims=True)
                         - on * jnp.mean(don * on, axis=-1, keepdims=True))
            ds = _dot(do, v, "nt") * dec
            dq = _dot(ds, k) + qd * _dot(do, R, "nt")
            dk = _dot(ds, q, "tn") + kd * _dot(v, dR, "nt")
            dv_ref[:, wide] = (_dot(s, do, "tn") + _dot(k * kd, dR)).astype(dv_ref.dtype)
            dr_ref[h] = cd_ref[h] * dR + _dot(q * qd, do, "tn")
            dq_ref[:, qk] = _rope_t(dq, cs, sn).astype(dq_ref.dtype)
            dk_ref[:, qk] = _rope_t(dk * qk_scale, cs, sn).astype(dk_ref.dtype)

    T = B * S
    qk_shape = jax.ShapeDtypeStruct((T, RET_HEADS * RET_QK_DIM), BF16)
    v_shape = jax.ShapeDtypeStruct((T, RET_HEADS * RET_V_DIM), BF16)
    vec_shape = jax.ShapeDtypeStruct((1, RET_HEADS * RET_V_DIM), F32)
    return _pcall(
        body, (proj,) * 6 + (d_yr, states, cos_t, sin_t, decay, q_decay, k_decay, c_decay, g_ret, b_ret),
        name="ret_bwd", out_shape=(qk_shape, qk_shape, v_shape, v_shape, vec_shape, vec_shape), grid=(B, N),
        in_specs=[sp["q"], sp["k"], *sp["v"], *sp["gr"], sp["v_rows"], sp["state"], sp["table"], sp["table"],
                  sp["decay"], sp["qd"], sp["kd"], sp["cd"], sp["vec"], sp["vec"]],
        out_specs=(sp["qk_rows"], sp["qk_rows"], sp["v_rows"], sp["v_rows"], sp["vec"], sp["vec"]),
        scratch_shapes=[pltpu.VMEM((RET_HEADS, RET_QK_DIM, RET_V_DIM), F32)],
        sem=("arbitrary", "arbitrary"), comm=comm)


def _xa_rows(S):
    return _tile(S, 256)


def _xa_specs(S, M):
    q = pl.BlockSpec((S, XA_HEAD_DIM), lambda b, h: (b, COL_QX // XA_HEAD_DIM + h))
    k = pl.BlockSpec((M, XA_HEAD_DIM), lambda b, h: (b, h))
    v = pl.BlockSpec((M, XA_HEAD_DIM), lambda b, h: (b, XA_HEADS + h))
    o = pl.BlockSpec((S, XA_HEAD_DIM), lambda b, h: (b, h))
    return q, k, v, o


def _softmax_rows(s):
    e = jnp.exp(s - jnp.max(s, axis=-1, keepdims=True))
    return e / jnp.sum(e, axis=-1, keepdims=True)


def _xa_fwd(proj, kv, B, S, M, comm=None):
    CH = _xa_rows(S)
    q_spec, k_spec, v_spec, o_spec = _xa_specs(S, M)

    def body(q_ref, k_ref, v_ref, o_ref):
        def chunk(i, carry):
            rows = pl.ds(pl.multiple_of(i * CH, CH), CH)
            p = _softmax_rows(_dot(q_ref[rows, :], k_ref[...], "nt") * (XA_HEAD_DIM ** -0.5))
            o_ref[rows, :] = _dot(p, v_ref[...]).astype(o_ref.dtype)
            return carry

        lax.fori_loop(0, S // CH, chunk, 0)

    return _pcall(
        body, (proj, kv, kv), name="xattn_fwd", out_shape=jax.ShapeDtypeStruct((B * S, XA_WIDTH), BF16),
        grid=(B, XA_HEADS), in_specs=[q_spec, k_spec, v_spec], out_specs=o_spec,
        sem=("parallel", "parallel"), comm=comm)


def _xa_bwd(proj, kv, d_o, B, S, M, comm=None):
    CH = _xa_rows(S)
    q_spec, k_spec, v_spec, o_spec = _xa_specs(S, M)
    scale = XA_HEAD_DIM ** -0.5

    def body(q_ref, k_ref, v_ref, do_ref, dq_ref, dk_ref, dv_ref):
        dk_ref[...] = jnp.zeros_like(dk_ref)
        dv_ref[...] = jnp.zeros_like(dv_ref)

        def chunk(i, carry):
            rows = pl.ds(pl.multiple_of(i * CH, CH), CH)
            q, do = q_ref[rows, :], do_ref[rows, :]
            p = _softmax_rows(_dot(q, k_ref[...], "nt") * scale)
            dp = _dot(do, v_ref[...], "nt")
            ds = p * (dp - jnp.sum(dp * p, axis=-1, keepdims=True)) * scale
            dq_ref[rows, :] = _dot(ds, k_ref[...]).astype(dq_ref.dtype)
            dk_ref[...] += _dot(ds, q, "tn")
            dv_ref[...] += _dot(p, do, "tn")
            return carry

        lax.fori_loop(0, S // CH, chunk, 0)

    kv_out = pl.BlockSpec((M, XA_HEAD_DIM), lambda b, h: (b, h))
    return _pcall(
        body, (proj, kv, kv, d_o), name="xattn_bwd",
        out_shape=(jax.ShapeDtypeStruct((B * S, XA_WIDTH), BF16), jax.ShapeDtypeStruct((B * M, XA_WIDTH), F32),
                   jax.ShapeDtypeStruct((B * M, XA_WIDTH), F32)),
        grid=(B, XA_HEADS), in_specs=[q_spec, k_spec, v_spec, o_spec], out_specs=(o_spec, kv_out, kv_out),
        sem=("parallel", "parallel"), comm=comm)


def _gate_specs(tm):
    n = COL_GL // D_MODEL
    return [pl.BlockSpec((tm, D_MODEL), lambda i, j=j: (i, n + j)) for j in range(3)]


def _merge_fwd(proj, ys, tm=256):
    T = proj.shape[0]
    tm = _tile(T, tm)
    row = pl.BlockSpec((tm, D_MODEL), lambda i: (i, 0))

    def body(g0, g1, g2, y0, y1, y2, o_ref):
        acc = jax.nn.sigmoid(g0[...]) * y0[...]
        acc = acc + jax.nn.sigmoid(g1[...]) * y1[...]
        acc = acc + jax.nn.sigmoid(g2[...]) * y2[...]
        o_ref[...] = acc.astype(o_ref.dtype)

    return pl.pallas_call(
        body, out_shape=jax.ShapeDtypeStruct((T, D_MODEL), BF16), grid=(T // tm,),
        in_specs=_gate_specs(tm) + [row] * 3, out_specs=row, name="merge_fwd",
        compiler_params=_params(("parallel",)),
    )(proj, proj, proj, *ys)


def _merge_bwd(proj, ys, d_merged, tm=256, comm=None):
    T = proj.shape[0]
    tm = _tile(T, tm)
    row = pl.BlockSpec((tm, D_MODEL), lambda i: (i, 0))

    def body(g0, g1, g2, y0, y1, y2, dm_ref, dgl_ref, d0, d1, d2):
        dm = dm_ref[...]
        for j, (g_ref, y_ref, d_ref) in enumerate(((g0, y0, d0), (g1, y1, d1), (g2, y2, d2))):
            sg = jax.nn.sigmoid(g_ref[...])
            d_ref[...] = (dm * sg).astype(d_ref.dtype)
            dgl_ref[:, j * D_MODEL:(j + 1) * D_MODEL] = (dm * y_ref[...] * sg * (1.0 - sg)).astype(dgl_ref.dtype)

    dy = jax.ShapeDtypeStruct((T, D_MODEL), BF16)
    return _pcall(
        body, (proj, proj, proj, *ys, d_merged), name="merge_bwd",
        out_shape=(jax.ShapeDtypeStruct((T, 3 * D_MODEL), BF16), dy, dy, dy), grid=(T // tm,),
        in_specs=_gate_specs(tm) + [row] * 4,
        out_specs=(pl.BlockSpec((tm, 3 * D_MODEL), lambda i: (i, 0)), row, row, row),
        sem=("parallel",), comm=comm)


def _gelu(x):
    return 0.5 * x * (1.0 + jnp.tanh(GELU_C * (x + GELU_A * x * x * x)))


def _gelu_grad(x):
    t = jnp.tanh(GELU_C * (x + GELU_A * x * x * x))
    return 0.5 * (1.0 + t) + 0.5 * x * (1.0 - t * t) * GELU_C * (1.0 + 3.0 * GELU_A * x * x)


def _shift_down(x, prev, n):
    rows = x.shape[0]
    r = lax.broadcasted_iota(jnp.int32, (8, 1), 0)
    rolled = pltpu.roll(x, n, axis=0)
    head = rolled[0:8]
    for j in range(n):
        head = jnp.where(r == j, prev[8 - n + j:8 - n + j + 1, :], head)
    return head if rows == 8 else jnp.concatenate([head, rolled[8:]], axis=0)


def _shift_up(x, nxt, n):
    rows = x.shape[0]
    r = lax.broadcasted_iota(jnp.int32, (8, 1), 0)
    rolled = pltpu.roll(x, rows - n, axis=0)
    tail = rolled[rows - 8:]
    for j in range(n):
        tail = jnp.where(r == 8 - n + j, nxt[j:j + 1, :], tail)
    return jnp.concatenate([rolled[:rows - 8], tail], axis=0)


def _conv(a, prev, cw, cb):
    return _shift_down(a, prev, 2) * cw[0:1, :] + _shift_down(a, prev, 1) * cw[1:2, :] + a * cw[2:3, :] + cb


def _glu_fwd(up, cw, cb, S, tm=256):
    T = up.shape[2]
    tm = _tile(S, tm)
    per_seq = S // tm

    def body(ab_ref, prev_ref, cw_ref, cb_ref, u_ref):
        i = pl.program_id(1)
        prev = jnp.where(i % per_seq == 0, 0.0, prev_ref[...])
        ac = _conv(ab_ref[0], prev, cw_ref[...], cb_ref[...])
        u_ref[...] = (_gelu(ac) * ab_ref[1]).astype(u_ref.dtype)

    return pl.pallas_call(
        body, out_shape=jax.ShapeDtypeStruct((FFN_SLABS, T, UP_SHARD), BF16), grid=(FFN_SLABS, T // tm),
        in_specs=[pl.BlockSpec((2, None, tm, UP_SHARD), lambda d, i: (0, d, i, 0)),
                  pl.BlockSpec((None, None, 8, UP_SHARD), lambda d, i: (0, d, jnp.maximum(i * (tm // 8) - 1, 0), 0)),
                  pl.BlockSpec((None, 3, UP_SHARD), lambda d, i: (d, 0, 0)),
                  pl.BlockSpec((None, 1, UP_SHARD), lambda d, i: (d, 0, 0))],
        out_specs=pl.BlockSpec((None, tm, UP_SHARD), lambda d, i: (d, i, 0)), name="glu_fwd",
        compiler_params=_params(("parallel", "parallel")),
    )(up, up, cw, cb)


def _glu_bwd(up, d_u, cw, cb, S, tm=256, comm=None):
    T = up.shape[2]
    tm = _tile(S, tm)
    per_seq = S // tm
    n_tiles = T // tm
    last8 = tm // 8

    def body(ab_ref, prev_ref, abn_ref, du_ref, dun_ref, cw_ref, cb_ref, dup_ref, dcw_ref, dcb_ref):
        i = pl.program_id(1)

        @pl.when(i == 0)
        def _():
            dcw_ref[...] = jnp.zeros_like(dcw_ref)
            dcb_ref[...] = jnp.zeros_like(dcb_ref)

        cw, cb = cw_ref[...], cb_ref[...]
        a, b = ab_ref[0], ab_ref[1]
        prev = jnp.where(i % per_seq == 0, 0.0, prev_ref[...])
        a2, a1 = _shift_down(a, prev, 2), _shift_down(a, prev, 1)
        ac = a2 * cw[0:1, :] + a1 * cw[1:2, :] + a * cw[2:3, :] + cb
        du = du_ref[...]
        dup_ref[1] = (du * _gelu(ac)).astype(dup_ref.dtype)
        dac = du * b * _gelu_grad(ac)
        dcb_ref[...] += jnp.sum(dac, axis=0, keepdims=True)
        dcw_ref[0:1, :] += jnp.sum(dac * a2, axis=0, keepdims=True)
        dcw_ref[1:2, :] += jnp.sum(dac * a1, axis=0, keepdims=True)
        dcw_ref[2:3, :] += jnp.sum(dac * a, axis=0, keepdims=True)
        an = abn_ref[0]
        acn = _conv(an, a[tm - 8:, :], cw, cb)
        dacn = jnp.where(i % per_seq == per_seq - 1, 0.0, dun_ref[...] * abn_ref[1] * _gelu_grad(acn))
        da = dac * cw[2:3, :] + _shift_up(dac, dacn, 1) * cw[1:2, :] + _shift_up(dac, dacn, 2) * cw[0:1, :]
        dup_ref[0] = da.astype(dup_ref.dtype)

    def nxt(i):
        return jnp.minimum((i + 1) * last8, T // 8 - 1)

    return _pcall(
        body, (up, up, up, d_u, d_u, cw, cb), name="glu_bwd",
        out_shape=(jax.ShapeDtypeStruct((2, FFN_SLABS, T, UP_SHARD), BF16),
                   jax.ShapeDtypeStruct((FFN_SLABS, 3, UP_SHARD), F32),
                   jax.ShapeDtypeStruct((FFN_SLABS, 1, UP_SHARD), F32)),
        grid=(FFN_SLABS, n_tiles),
        in_specs=[pl.BlockSpec((2, None, tm, UP_SHARD), lambda d, i: (0, d, i, 0)),
                  pl.BlockSpec((None, None, 8, UP_SHARD), lambda d, i: (0, d, jnp.maximum(i * last8 - 1, 0), 0)),
                  pl.BlockSpec((2, None, 8, UP_SHARD), lambda d, i: (0, d, nxt(i), 0)),
                  pl.BlockSpec((None, tm, UP_SHARD), lambda d, i: (d, i, 0)),
                  pl.BlockSpec((None, 8, UP_SHARD), lambda d, i: (d, nxt(i), 0)),
                  pl.BlockSpec((None, 3, UP_SHARD), lambda d, i: (d, 0, 0)),
                  pl.BlockSpec((None, 1, UP_SHARD), lambda d, i: (d, 0, 0))],
        out_specs=(pl.BlockSpec((2, None, tm, UP_SHARD), lambda d, i: (0, d, i, 0)),
                   pl.BlockSpec((None, 3, UP_SHARD), lambda d, i: (d, 0, 0)),
                   pl.BlockSpec((None, 1, UP_SHARD), lambda d, i: (d, 0, 0))),
        sem=("parallel", "arbitrary"), comm=comm)


def _mm_up(h2, w_up, tm=MM_ROWS):
    T, K = h2.shape
    tm = _tile(T, tm)
    return _matmul(
        "mm_up", "nn", h2, w_up, jax.ShapeDtypeStruct((N_DEV, T, UP_SHARD), F32), (N_DEV, T // tm, 1),
        pl.BlockSpec((tm, K), lambda j, i, k: (i, 0)), pl.BlockSpec((None, K, UP_SHARD), lambda j, i, k: (j, 0, 0)),
        pl.BlockSpec((None, tm, UP_SHARD), lambda j, i, k: (j, i, 0)), (tm, UP_SHARD))


def _loss_epilogue(ffn, operands, outputs, first):
    x1_ref, t_ref, g_ref = operands
    dx_ref, dg_ref, loss_ref = outputs

    @pl.when(first)
    def _():
        dg_ref[...] = jnp.zeros_like(dg_ref)
        loss_ref[...] = jnp.zeros_like(loss_ref)

    xv = x1_ref[...] + ffn
    r = lax.rsqrt(jnp.mean(xv * xv, axis=-1, keepdims=True) + EPS)
    xhat = xv * r
    err = xhat * g_ref[...] - t_ref[...]
    loss_ref[...] += (0.5 / D_MODEL) * jnp.sum(err * err)
    dy = err * (1.0 / D_MODEL)
    dg_ref[...] += jnp.sum(dy * xhat, axis=0, keepdims=True)
    dxhat = dy * g_ref[...]
    dx_ref[...] = r * (dxhat - xhat * jnp.mean(dxhat * xhat, axis=-1, keepdims=True))


def _mm_down_loss(u, w_down, x1, target, g_final, tm=MM_ROWS_RES):
    J, T, n = u.shape
    tm = _tile(T, tm)
    row = pl.BlockSpec((tm, D_MODEL), lambda i, d: (i, 0))
    vec = pl.BlockSpec((1, D_MODEL), lambda i, d: (0, 0))
    vec_shape = jax.ShapeDtypeStruct((1, D_MODEL), F32)
    return _matmul(
        "mm_down", "nn", u, w_down, (jax.ShapeDtypeStruct((T, D_MODEL), F32), vec_shape, vec_shape), (T // tm, J),
        pl.BlockSpec((None, tm, n), lambda i, d: (d, i, 0)), pl.BlockSpec((None, n, D_MODEL), lambda i, d: (d, 0, 0)),
        (row, vec, vec), (tm, D_MODEL), [x1, target, g_final], [row, row, vec], epilogue=_loss_epilogue)


def _mm_down_t(dx, w_down, tm=MM_ROWS):
    T = dx.shape[0]
    J, n, _ = w_down.shape
    tm = _tile(T, tm)
    return _matmul(
        "mm_down_t", "nt", dx, w_down, jax.ShapeDtypeStruct((J, T, n), F32), (J, T // tm, 1),
        pl.BlockSpec((tm, D_MODEL), lambda d, i, k: (i, 0)), pl.BlockSpec((None, n, D_MODEL), lambda d, i, k: (d, 0, 0)),
        pl.BlockSpec((None, tm, n), lambda d, i, k: (d, i, 0)), (tm, n))


def _mm_dw_down(u, dx, tk=MM_TOKENS):
    J, T, n = u.shape
    tk = _tile(T, tk)
    return _matmul(
        "mm_dw_down", "tn", u, dx, jax.ShapeDtypeStruct((J, n, D_MODEL), BF16), (J, T // tk),
        pl.BlockSpec((None, tk, n), lambda d, k: (d, k, 0)), pl.BlockSpec((tk, D_MODEL), lambda d, k: (k, 0)),
        pl.BlockSpec((None, n, D_MODEL), lambda d, k: (d, 0, 0)), (n, D_MODEL))


def _mm_dw_up(name, h2, d_up, part, tk=MM_TOKENS, comm=None):
    T, K = h2.shape
    p, of = part
    K //= of
    tk = _tile(T, tk)
    return _matmul(
        name, "tn", h2, d_up, jax.ShapeDtypeStruct((N_DEV, K, UP_SHARD), BF16), (N_DEV, T // tk),
        pl.BlockSpec((tk, K), lambda j, k: (k, p)), pl.BlockSpec((None, tk, UP_SHARD), lambda j, k: (j, k, 0)),
        pl.BlockSpec((None, K, UP_SHARD), lambda j, k: (j, 0, 0)), (K, UP_SHARD), comm=comm)


def _mm_up_t(d_up, w_up, rms, tm=MM_ROWS_RES, comm=None):
    J, T, n = d_up.shape
    K = w_up.shape[1]
    tm = _tile(T, tm)
    fused = _rms_bwd_fused(T, K, tm, rms)
    return _matmul(
        "mm_up_t", "nt", d_up, w_up, fused.pop("out_shape"), (T // tm, J),
        pl.BlockSpec((None, tm, n), lambda i, j: (j, i, 0)), pl.BlockSpec((None, K, n), lambda i, j: (j, 0, 0)),
        fused.pop("o_spec"), (tm, K), comm=comm, **fused)


def _loss_head(x2, target, g_final, tm=512):
    T, Dm = x2.shape
    tm = _tile(T, tm)

    def body(x_ref, t_ref, g_ref, dx_ref, dg_ref, loss_ref):
        @pl.when(pl.program_id(0) == 0)
        def _():
            dg_ref[...] = jnp.zeros_like(dg_ref)
            loss_ref[...] = jnp.zeros_like(loss_ref)

        xv = x_ref[...]
        r = lax.rsqrt(jnp.mean(xv * xv, axis=-1, keepdims=True) + EPS)
        xhat = xv * r
        err = xhat * g_ref[...] - t_ref[...]
        loss_ref[...] += (0.5 / Dm) * jnp.sum(err * err)
        dy = err * (1.0 / Dm)
        dg_ref[...] += jnp.sum(dy * xhat, axis=0, keepdims=True)
        dxhat = dy * g_ref[...]
        dx_ref[...] = r * (dxhat - xhat * jnp.mean(dxhat * xhat, axis=-1, keepdims=True))

    row = pl.BlockSpec((tm, Dm), lambda i: (i, 0))
    vec = pl.BlockSpec((1, Dm), lambda i: (0, 0))
    return pl.pallas_call(
        body,
        out_shape=(jax.ShapeDtypeStruct((T, Dm), F32), jax.ShapeDtypeStruct((1, Dm), F32),
                   jax.ShapeDtypeStruct((1, Dm), F32)),
        grid=(T // tm,), in_specs=[row, row, vec], out_specs=(row, vec, vec), name="loss_head",
        compiler_params=_params(("arbitrary",)),
    )(x2, target, g_final)


def _cast_shards(shards):
    def body(*refs):
        n = len(refs) // 2
        for src, dst in zip(refs[:n], refs[n:]):
            dst[...] = src[...].astype(dst.dtype)

    return pl.pallas_call(
        body, out_shape=[jax.ShapeDtypeStruct(s.shape, BF16) for s in shards], name="cast_shards",
        compiler_params=pltpu.CompilerParams(vmem_limit_bytes=VMEM_LIMIT),
    )(*shards)


def _adamw(w, g, m, v):
    m = ADAM_B1 * m + (1.0 - ADAM_B1) * g
    v = ADAM_B2 * v + (1.0 - ADAM_B2) * (g * g)
    m_hat = m / (1.0 - ADAM_B1 ** ADAM_STEP)
    v_hat = v / (1.0 - ADAM_B2 ** ADAM_STEP)
    delta = -ADAM_LR * (m_hat / (jnp.sqrt(v_hat) + ADAM_EPS) + ADAM_WD * w)
    return delta, m, v


def _sum_parts(p_ref):
    g = p_ref[0].astype(F32)
    for d in range(1, N_DEV):
        g = g + p_ref[d].astype(F32)
    return g


def _reduce_adam(name, parts, w, m, v, tr=128):
    R, Cn = w.shape
    by_rows = sum(p.shape[1] for p in parts) == R and len(parts) > 1
    tr = math.gcd(tr, *[p.shape[1] for p in parts])
    n_tiles = [p.shape[1] // tr for p in parts]
    first = [sum(n_tiles[:j]) for j in range(len(parts))] if by_rows else [0] * len(parts)

    def body(*refs):
        p_refs = refs[:len(parts)]
        w_ref, m_ref, v_ref, g_out, d_out, m_out, v_out = refs[len(parts):]

        def update(p_ref):
            g = _sum_parts(p_ref)
            delta, m_new, v_new = _adamw(w_ref[...], g, m_ref[...], v_ref[...])
            g_out[...] = g
            d_out[...] = delta
            m_out[...] = m_new
            v_out[...] = v_new

        if len(parts) == 1:
            update(p_refs[0])
        elif by_rows:
            i = pl.program_id(0)
            for p_ref, t0, n in zip(p_refs, first, n_tiles):
                pl.when((i >= t0) & (i < t0 + n))(functools.partial(update, p_ref))
        else:
            c = lax.axis_index("c")
            for side, p_ref in enumerate(p_refs):
                pl.when(c == side)(functools.partial(update, p_ref))

    def part_spec(t0, n):
        return pl.BlockSpec((N_DEV, tr, Cn), lambda i: (0, jnp.clip(i - t0, 0, n - 1), 0))

    row = pl.BlockSpec((tr, Cn), lambda i: (i, 0))
    shape = jax.ShapeDtypeStruct((R, Cn), F32)
    return pl.pallas_call(
        body, out_shape=(shape,) * 4, grid=(R // tr,),
        in_specs=[part_spec(t0, n) for t0, n in zip(first, n_tiles)] + [row, row, row],
        out_specs=(row,) * 4, name=name, compiler_params=_params(("parallel",)),
    )(*parts, w, m, v)


def _small_adam(name, gathered, params):
    n_g, n_p = len(gathered), len(params)

    def body(*refs):
        g_refs = refs[:n_g]
        wmv = refs[n_g:n_g + 3 * n_p]
        sums = refs[n_g + 3 * n_p:2 * n_g + 3 * n_p]
        upd = refs[2 * n_g + 3 * n_p:]
        for j in range(n_g):
            g = _sum_parts(g_refs[j])
            sums[j][...] = g
            if j < n_p:
                w_ref, m_ref, v_ref = wmv[3 * j:3 * j + 3]
                delta, m_new, v_new = _adamw(w_ref[...], g, m_ref[...], v_ref[...])
                upd[3 * j][...] = delta
                upd[3 * j + 1][...] = m_new
                upd[3 * j + 2][...] = v_new

    flat = [a for wmv in params for a in wmv]
    out_shape = [jax.ShapeDtypeStruct(g.shape[1:], F32) for g in gathered]
    out_shape += [jax.ShapeDtypeStruct(a.shape, F32) for a in flat]
    res = pl.pallas_call(body, out_shape=out_shape, name=name)(*gathered, *flat)
    return res[:n_g], [tuple(res[n_g + 3 * j:n_g + 3 * j + 3]) for j in range(n_p)]


def _adam_only(name, g, w, m, v):
    def body(g_ref, w_ref, m_ref, v_ref, d_out, m_out, v_out):
        delta, m_new, v_new = _adamw(w_ref[...], g_ref[...], m_ref[...], v_ref[...])
        d_out[...] = delta
        m_out[...] = m_new
        v_out[...] = v_new

    shape = jax.ShapeDtypeStruct(w.shape, F32)
    return pl.pallas_call(body, out_shape=(shape,) * 3, name=name)(g, w, m, v)


def kernel(x, mem, g_mix, w_in, w_pool, pool_scale, w_a, g_ret, b_ret, w_r, g_mem, w_mem_kv, w_c, w_out, g_ffn, w_up, conv_w, conv_b, w_down, g_final, loss_target, m_g_mix, m_w_in, m_w_pool, m_pool_scale, m_w_a, m_g_ret, m_b_ret, m_w_r, m_g_mem, m_w_mem_kv, m_w_c, m_w_out, m_g_ffn, m_w_up, m_conv_w, m_conv_b, m_w_down, m_g_final, v_g_mix, v_w_in, v_w_pool, v_pool_scale, v_w_a, v_g_ret, v_b_ret, v_w_r, v_g_mem, v_w_mem_kv, v_w_c, v_w_out, v_g_ffn, v_w_up, v_conv_w, v_conv_b, v_w_down, v_g_final):
    B, S, _ = x.shape
    M = mem.shape[1]
    T = B * S
    me = _my_index()
    x2d = x.reshape(T, D_MODEL)
    mem2d = mem.reshape(B * M, D_MODEL)
    tgt2d = loss_target.reshape(T, D_MODEL)
    g_final2 = g_final.reshape(1, D_MODEL)

    big = dict(w_in=w_in[0], w_a=w_a[0], w_r=w_r[0], w_mem_kv=w_mem_kv[0], w_c=w_c[0], w_out=w_out[0],
               w_up=w_up[0], w_down=w_down[0])
    names = list(big)
    cast = dict(zip(names, _cast_shards([big[n] for n in names])))
    cb = conv_b[0].reshape(FFN_SLABS, 1, UP_SHARD)
    wp = w_pool[0]
    tables = _ret_tables(S)

    h = _rms_fwd("rms_mix", x2d, g_mix)
    early = ("w_a", "w_r", "w_mem_kv", "w_c", "w_out")
    (proj, Win), landed = _mm_in_gather(h, cast["w_in"], comm=_Gather([cast[n] for n in early] + [conv_w[0]]))
    W = dict(zip(early, landed))
    cw_full = landed[-1].transpose(1, 0, 2).reshape(3, FFN_HIDDEN)
    cw = cw_full.reshape(3, FFN_SLABS, UP_SHARD).transpose(1, 0, 2)
    Wa = W["w_a"].transpose(1, 0, 2).reshape(POOL_WIDTH, D_MODEL)
    Wc = W["w_c"].transpose(1, 0, 2).reshape(XA_WIDTH, D_MODEL)
    Wr = W["w_r"].reshape(D_MODEL, D_MODEL)
    Wkv = W["w_mem_kv"].reshape(D_MODEL, D_MODEL)
    Wout = W["w_out"].reshape(D_MODEL, D_MODEL)
    ypre = _pool_fwd(proj, wp, pool_scale, B, S)
    y_pool = _mm_rows("mm_a", ypre, Wa)
    (yr, ret_states), (Wup,) = _ret_fwd(proj, g_ret, b_ret, tables, B, S, comm=_Gather([cast["w_up"]]))
    y_ret = _mm_rows("mm_r", yr, Wr)
    mem_n = _rms_fwd("rms_mem", mem2d, g_mem)
    kv = _mm_rows("mm_kv", mem_n, Wkv)
    o_mem, (Wdown,) = _xa_fwd(proj, kv, B, S, M, comm=_Gather([cast["w_down"]]))
    Wdown = Wdown.reshape(FFN_SLABS, UP_SHARD, D_MODEL)
    y_mem = _mm_rows("mm_c", o_mem, Wc)
    ys = (y_pool, y_ret, y_mem)
    merged = _merge_fwd(proj, ys)
    x1, h2 = _mm_residual_rms("mm_out", merged, Wout, x2d, g_ffn)
    up = _mm_up(h2, Wup).reshape(2, FFN_SLABS, T, UP_SHARD)
    u = _glu_fwd(up, cw, cb, S)

    dx2, dg_final, loss_part = _mm_down_loss(u, Wdown, x1, tgt2d, g_final2)
    received = {}
    d_u = _mm_down_t(dx2, Wdown)
    dW_down = _mm_dw_down(u, dx2)
    (d_up, d_cw, d_cb), (received["w_down"],) = _glu_bwd(
        up, d_u, cw, cb, S, comm=_Exchange([dW_down.reshape(N_DEV, -1, D_MODEL)]))
    d_up = d_up.reshape(N_DEV, T, UP_SHARD)
    dW_up = _mm_dw_up("mm_dw_up", h2, d_up, (0, 1))
    (dx1, dg_ffn), (up_c0,) = _mm_up_t(d_up, Wup, (x1, g_ffn, dx2), comm=_ExchangeTo([dW_up], 0))
    d_merged = _mm_rows("mm_out_t", dx1, Wout, kind="nt")
    dW_out = _mm_tn("mm_dw_out", merged, dx1, BF16)
    (d_gl, d_y_pool, d_y_ret, d_y_mem), (up_c1,) = _merge_bwd(proj, ys, d_merged, comm=_ExchangeTo([dW_up], 1))
    received["w_up"] = [up_c0, up_c1]
    dW_c = _mm_tn("mm_dw_c", o_mem, d_y_mem, BF16)
    d_o_mem = _mm_rows("mm_c_t", d_y_mem, Wc, kind="nt")
    (d_qx, d_kmem, d_vmem), (received["w_out"],) = _xa_bwd(
        proj, kv, d_o_mem, B, S, M, comm=_Exchange([dW_out.reshape(N_DEV, -1, D_MODEL)]))
    d_kv = jnp.concatenate([d_kmem, d_vmem], axis=1)
    dW_kv = _mm_tn("mm_dw_kv", mem_n, d_kv, BF16)
    d_mem_n = _mm_rows("mm_kv_t", d_kv, Wkv, kind="nt")
    dg_mem = _rms_bwd("rms_mem_bwd", mem2d, g_mem, d_mem_n, None)
    dW_a = _mm_tn("mm_dw_a", ypre, d_y_pool, BF16)
    d_ypre = _mm_rows("mm_a_t", d_y_pool, Wa, kind="nt")
    d_hp, dw_pool, d_scale = _pool_bwd(proj, d_ypre, wp, pool_scale, B, S)
    dW_r = _mm_tn("mm_dw_r", yr, d_y_ret, BF16)
    d_yr = _mm_rows("mm_r_t", d_y_ret, Wr, kind="nt")
    (d_q, d_k, d_v, d_gr, dg_ret, db_ret), landed = _ret_bwd(
        proj, ret_states, d_yr, g_ret, b_ret, tables, B, S,
        comm=_Exchange([dW_a.reshape(POOL_WIDTH, N_DEV, -1).transpose(1, 0, 2), dW_r.reshape(N_DEV, -1, D_MODEL),
                        dW_c.reshape(XA_WIDTH, N_DEV, -1).transpose(1, 0, 2), dW_kv.reshape(N_DEV, -1, D_MODEL)]))
    received["w_a"], received["w_r"], received["w_c"], received["w_mem_kv"] = landed
    small_names = ["w_pool", "pool_scale", "g_ret", "b_ret", "g_mem", "g_ffn", "conv_b", "g_final"]
    small_grads = [dw_pool, d_scale, dg_ret, db_ret, dg_mem, dg_ffn, d_cb.reshape(1, FFN_HIDDEN), dg_final,
                   d_cw.transpose(1, 0, 2).reshape(3, FFN_HIDDEN), loss_part]
    d_proj = jnp.concatenate([d_hp, d_q, d_k, d_v, d_gr, d_qx, d_gl], axis=1)
    dW_in0 = _mm_tn_slab("mm_dw_in0", h, d_proj, IN_SHARD, BF16, part=(0, 2))
    dW_in1, (in0,) = _mm_tn_slab("mm_dw_in1", h, d_proj, IN_SHARD, BF16, part=(1, 2), comm=_Exchange([dW_in0]))
    (grad_x, dg_mix), (in1, *small_all) = _mm_cols_slab_t(
        "mm_in_t", d_proj, Win, (x2d, g_mix, dx1), comm=_Exchange([dW_in1], whole=small_grads))
    received["w_in"] = [in0, in1]
    (g_mix_all,) = _comm_call("gather_g_mix", _Exchange([], whole=[dg_mix]))

    args = dict(g_mix=g_mix, w_in=w_in, w_pool=w_pool, pool_scale=pool_scale, w_a=w_a, g_ret=g_ret, b_ret=b_ret,
                w_r=w_r, g_mem=g_mem, w_mem_kv=w_mem_kv, w_c=w_c, w_out=w_out, g_ffn=g_ffn, w_up=w_up,
                conv_w=conv_w, conv_b=conv_b, w_down=w_down, g_final=g_final)
    m_in = dict(g_mix=m_g_mix, w_in=m_w_in, w_pool=m_w_pool, pool_scale=m_pool_scale, w_a=m_w_a, g_ret=m_g_ret,
                b_ret=m_b_ret, w_r=m_w_r, g_mem=m_g_mem, w_mem_kv=m_w_mem_kv, w_c=m_w_c, w_out=m_w_out,
                g_ffn=m_g_ffn, w_up=m_w_up, conv_w=m_conv_w, conv_b=m_conv_b, w_down=m_w_down, g_final=m_g_final)
    v_in = dict(g_mix=v_g_mix, w_in=v_w_in, w_pool=v_w_pool, pool_scale=v_pool_scale, w_a=v_w_a, g_ret=v_g_ret,
                b_ret=v_b_ret, w_r=v_w_r, g_mem=v_g_mem, w_mem_kv=v_w_mem_kv, w_c=v_w_c, w_out=v_w_out,
                g_ffn=v_g_ffn, w_up=v_w_up, conv_w=v_conv_w, conv_b=v_conv_b, w_down=v_w_down, g_final=v_g_final)

    grads, deltas, new_m, new_v = {}, {}, {}, {}
    for n in names:
        shard = big[n].shape
        parts = received[n] if isinstance(received[n], list) else [received[n]]
        outs = _reduce_adam("adam_" + n, parts, big[n], m_in[n][0], v_in[n][0])
        for store, val in zip((grads, deltas, new_m, new_v), outs):
            store[n] = val.reshape((1,) + shard)

    def as_small(a):
        return a.reshape(a.shape[-3:]) if a.ndim > 2 else a.reshape(1, -1)

    def small_update(call_name, param_names, gathered):
        params = [tuple(as_small(d[n]) for d in (args, m_in, v_in)) for n in param_names]
        sums, updates = _small_adam(call_name, gathered, params)
        for n, g, (d_, m_, v_) in zip(param_names, sums, updates):
            shape = args[n].shape
            grads[n], deltas[n], new_m[n], new_v[n] = (a.reshape(shape) for a in (g, d_, m_, v_))
        return sums[len(param_names):]

    g_cw_full, loss_row = small_update("adam_small", small_names, small_all)
    loss = loss_row[0, 0]
    small_update("adam_g_mix", ["g_mix"], [g_mix_all])

    shard_cols = FFN_HIDDEN // N_DEV
    g_cw = lax.dynamic_slice_in_dim(g_cw_full, me * shard_cols, shard_cols, axis=1)
    d_, m_, v_ = _adam_only("adam_conv_w", g_cw, conv_w[0], m_conv_w[0], v_conv_w[0])
    grads["conv_w"], deltas["conv_w"], new_m["conv_w"], new_v["conv_w"] = g_cw[None], d_[None], m_[None], v_[None]

    order = ["g_mix", "w_in", "w_pool", "pool_scale", "w_a", "g_ret", "b_ret", "w_r", "g_mem", "w_mem_kv", "w_c",
             "w_out", "g_ffn", "w_up", "conv_w", "conv_b", "w_down", "g_final"]
    return (loss, grad_x.reshape(B, S, D_MODEL), *[grads[n] for n in order], *[deltas[n] for n in order],
            *[new_m[n] for n in order], *[new_v[n] for n in order])
```

```python
import functools
import math

import jax
import jax.numpy as jnp
from jax import lax
from jax.experimental import pallas as pl
from jax.experimental.pallas import tpu as pltpu

F32 = jnp.float32
BF16 = jnp.bfloat16

N_DEV = 8
D_MODEL = 1024
POOL_WINDOWS = (2, 4, 8, 16)
POOL_GROUP_DIM = 128
POOL_WIDTH = 512
POOL_HALO = 16
RET_HEADS = 4
RET_QK_DIM = 128
RET_V_DIM = 256
RET_CHUNK = 128
ROPE_BASE = 10000.0
XA_HEADS = 4
XA_HEAD_DIM = 128
XA_WIDTH = 512
IN_WIDTH = 7168
IN_SHARD = IN_WIDTH // N_DEV
FFN_HIDDEN = 2816
UP_SHARD = 2 * FFN_HIDDEN // N_DEV
FFN_SLABS = FFN_HIDDEN // UP_SHARD
EPS = 1e-6
ADAM_LR = 0.001
ADAM_B1 = 0.9
ADAM_B2 = 0.999
ADAM_EPS = 1e-08
ADAM_WD = 0.01
ADAM_STEP = 10
GELU_C = math.sqrt(2.0 / math.pi)
GELU_A = 0.044715
VMEM_LIMIT = 56 * 1024 * 1024
MM_ROWS = 2048
MM_ROWS_RES = 1024
MM_TOKENS = 2048
W_IN_FIRST_ROWS = 384
MESH = pl.DeviceIdType.MESH

COL_Q, COL_K, COL_V, COL_GR, COL_QX, COL_GL = 512, 1024, 1536, 2560, 3584, 4096

_DIMS = {
    "nn": (((1,), (0,)), ((), ())),
    "nt": (((1,), (1,)), ((), ())),
    "tn": (((0,), (0,)), ((), ())),
}


def _dot(a, b, kind="nn"):
    return lax.dot_general(a.astype(BF16), b.astype(BF16), _DIMS[kind], preferred_element_type=F32)


def _params(sem, vmem=VMEM_LIMIT):
    return pltpu.CompilerParams(dimension_semantics=sem, vmem_limit_bytes=vmem)


def _tile(n, pref):
    t = min(n, pref)
    while n % t:
        t //= 2
    return t


def _mesh_pos():
    return lax.axis_index("x"), lax.axis_index("y"), lax.axis_index("c")


def _dev_index(x, y, c):
    return 4 * x + 2 * y + c


def _my_index():
    return _dev_index(*_mesh_pos())


def _remote(src, dst, send_sems, recv_sems, s, to):
    return pltpu.make_async_remote_copy(src_ref=src, dst_ref=dst, send_sem=send_sems.at[s], recv_sem=recv_sems.at[s],
                                        device_id=to, device_id_type=MESH)


class _Gather:
    def __init__(self, shards):
        self.inputs = list(shards)
        self.out_shapes = [jax.ShapeDtypeStruct((N_DEV,) + s.shape, s.dtype) for s in shards]
        n = len(shards)
        self.sem_shapes = [pltpu.SemaphoreType.DMA((7 * n,)), pltpu.SemaphoreType.DMA((7 * n,)),
                           pltpu.SemaphoreType.DMA((n,))]

    def _places(self):
        x, y, c = _mesh_pos()
        return (x, y, c), (x, y, 1 - c), [(1 - x, y), (x, 1 - y), (1 - x, 1 - y)]

    def _local(self, src, dst, sems):
        me = _my_index()
        return [pltpu.make_async_copy(src[w], dst[w].at[me], sems[2].at[w]) for w in range(len(src))]

    def start(self, src, dst, sems):
        me, sib, chips = self._places()
        for cp in self._local(src, dst, sems):
            cp.start()
        for w in range(len(src)):
            land = dst[w].at[_dev_index(*me)]
            _remote(src[w], land, sems[0], sems[1], 7 * w, sib).start()
            for j, chip in enumerate(chips):
                _remote(src[w], land, sems[0], sems[1], 7 * w + 1 + j, (*chip, me[2])).start()

    def middle(self, src, dst, sems):
        me, sib, chips = self._places()
        for j, chip in enumerate(chips):
            for w in range(len(src)):
                block = dst[w].at[_dev_index(*chip, me[2])]
                _remote(src[w], block, sems[0], sems[1], 7 * w + 1 + j, me).wait_recv()
                _remote(block, block, sems[0], sems[1], 7 * w + 4 + j, sib).start()

    def finish(self, src, dst, sems):
        me, sib, chips = self._places()
        n = len(src)
        for w in range(n):
            _remote(src[w], dst[w].at[_dev_index(*sib)], sems[0], sems[1], 7 * w, me).wait_recv()
            for j, chip in enumerate(chips):
                block = dst[w].at[_dev_index(*chip, sib[2])]
                _remote(block, block, sems[0], sems[1], 7 * w + 4 + j, me).wait_recv()
            for k in range(7):
                _remote(src[w], dst[w].at[0], sems[0], sems[1], 7 * w + k, me).wait_send()
        for cp in self._local(src, dst, sems):
            cp.wait()


class _Exchange:
    def __init__(self, partials, whole=()):
        self.n_part = len(partials)
        self.inputs = list(partials) + list(whole)
        self.out_shapes = [jax.ShapeDtypeStruct(p.shape, p.dtype) for p in partials]
        self.out_shapes += [jax.ShapeDtypeStruct((N_DEV,) + a.shape, a.dtype) for a in whole]
        n = len(self.inputs)
        self.sem_shapes = [pltpu.SemaphoreType.DMA((7 * n,)), pltpu.SemaphoreType.DMA((7 * n,)),
                           pltpu.SemaphoreType.DMA((n,))]

    def _peer(self, k):
        x, y, c = _mesh_pos()
        p = (x ^ ((k >> 2) & 1), y ^ ((k >> 1) & 1), c ^ (k & 1))
        return p, _dev_index(*p)

    def _source(self, src, w, slot):
        return src[w].at[slot] if w < self.n_part else src[w]

    def _local(self, src, dst, sems):
        me = _my_index()
        return [pltpu.make_async_copy(self._source(src, w, me), dst[w].at[me], sems[2].at[w])
                for w in range(len(src))]

    def start(self, src, dst, sems):
        me = _my_index()
        for cp in self._local(src, dst, sems):
            cp.start()
        for k in range(1, N_DEV):
            peer, peer_idx = self._peer(k)
            for w in range(len(src)):
                _remote(self._source(src, w, peer_idx), dst[w].at[me], sems[0], sems[1], 7 * w + k - 1, peer).start()

    def finish(self, src, dst, sems):
        for k in range(1, N_DEV):
            peer, peer_idx = self._peer(k)
            for w in range(len(src)):
                cp = _remote(self._source(src, w, peer_idx), dst[w].at[peer_idx], sems[0], sems[1], 7 * w + k - 1, peer)
                cp.wait_send()
                cp.wait_recv()
        for cp in self._local(src, dst, sems):
            cp.wait()


class _ExchangeTo:
    def __init__(self, partials, side):
        self.side = side
        self.inputs = list(partials)
        self.out_shapes = [jax.ShapeDtypeStruct(p.shape, p.dtype) for p in partials]
        n = len(partials)
        self.sem_shapes = [pltpu.SemaphoreType.DMA((7 * n,)), pltpu.SemaphoreType.DMA((7 * n,)),
                           pltpu.SemaphoreType.DMA((n,))]

    def _copies(self, src, dst, sems):
        x, y, c = _mesh_pos()
        me = _dev_index(x, y, c)
        receives = c == self.side
        remote = []
        for k in range(1, N_DEV):
            kx, ky, kc = (k >> 2) & 1, (k >> 1) & 1, k & 1
            peer = (x ^ kx, y ^ ky, c ^ kc)
            peer_idx = _dev_index(*peer)
            sends = c == (self.side ^ kc)
            for w in range(len(src)):
                slab = src[w].at[peer_idx]
                s = 7 * w + k - 1
                remote.append((sends, _remote(slab, dst[w].at[me], sems[0], sems[1], s, peer),
                               _remote(slab, dst[w].at[peer_idx], sems[0], sems[1], s, peer)))
        local = [pltpu.make_async_copy(src[w].at[me], dst[w].at[me], sems[2].at[w]) for w in range(len(src))]
        return receives, remote, local

    def start(self, src, dst, sems):
        receives, remote, local = self._copies(src, dst, sems)

        @pl.when(receives)
        def _():
            for cp in local:
                cp.start()

        for sends, send, _ in remote:
            pl.when(sends)(send.start)

    def finish(self, src, dst, sems):
        receives, remote, local = self._copies(src, dst, sems)
        for sends, send, arrive in remote:
            pl.when(sends)(send.wait_send)
            pl.when(receives)(arrive.wait_recv)

        @pl.when(receives)
        def _():
            for cp in local:
                cp.wait()


def _pcall(body, args, *, name, out_shape, grid, in_specs, out_specs, scratch_shapes=(), sem=None, comm=None):
    single = not isinstance(out_shape, (tuple, list))
    outs = [out_shape] if single else list(out_shape)
    ospecs = [out_specs] if single else list(out_specs)
    n_in, n_out, n_scr = len(args), len(outs), len(scratch_shapes)

    def pick(res):
        return res[0] if single else tuple(res[:n_out])

    if comm is None:
        res = pl.pallas_call(
            body, out_shape=outs, grid=grid, in_specs=list(in_specs), out_specs=ospecs,
            scratch_shapes=list(scratch_shapes), name=name, compiler_params=_params(sem),
        )(*args)
        return pick(res), ()

    nci, nco = len(comm.inputs), len(comm.out_shapes)

    def carrier(*refs):
        at = 0
        parts = []
        for size in (n_in, nci, n_out, nco, n_scr, len(comm.sem_shapes)):
            parts.append(refs[at:at + size])
            at += size
        ins, cins, o, couts, scr, sems = parts
        ids = [pl.program_id(a) for a in range(len(grid))]
        first = functools.reduce(jnp.logical_and, [i == 0 for i in ids])
        last = functools.reduce(jnp.logical_and, [i == g - 1 for i, g in zip(ids, grid)])

        body(*ins, *o, *scr)

        @pl.when(first)
        def _():
            comm.start(cins, couts, sems)

        if hasattr(comm, "middle"):
            steps = math.prod(grid)
            at = functools.reduce(lambda lin, ig: lin * ig[1] + ig[0], zip(ids, grid), 0)

            @pl.when(at == min(steps - 1, (3 * steps) // 4))
            def _():
                comm.middle(cins, couts, sems)

        @pl.when(last)
        def _():
            comm.finish(cins, couts, sems)

    hbm = pl.BlockSpec(memory_space=pltpu.HBM)
    res = pl.pallas_call(
        carrier, out_shape=outs + comm.out_shapes, grid=grid, in_specs=list(in_specs) + [hbm] * nci,
        out_specs=ospecs + [hbm] * nco, scratch_shapes=list(scratch_shapes) + comm.sem_shapes, name=name,
        compiler_params=_params(("arbitrary",) * len(grid)),
    )(*args, *comm.inputs)
    return pick(res), tuple(res[n_out:])


def _comm_call(name, comm):
    def body(*refs):
        nci, nco = len(comm.inputs), len(comm.out_shapes)
        cins, couts, sems = refs[:nci], refs[nci:nci + nco], refs[nci + nco:]
        comm.start(cins, couts, sems)
        if hasattr(comm, "middle"):
            comm.middle(cins, couts, sems)
        comm.finish(cins, couts, sems)

    hbm = pl.BlockSpec(memory_space=pltpu.HBM)
    return pl.pallas_call(
        body, out_shape=comm.out_shapes, in_specs=[hbm] * len(comm.inputs), out_specs=[hbm] * len(comm.out_shapes),
        scratch_shapes=comm.sem_shapes, name=name,
    )(*comm.inputs)


def _matmul(name, kind, a, b, out_shape, grid, a_spec, b_spec, o_spec, acc_shape, res=None, res_spec=None,
            comm=None, epilogue=None):
    nk = grid[-1]
    if epilogue is None:
        extra, extra_specs = ([res], [res_spec]) if res is not None else ([], [])
        n_out = 1
    else:
        extra, extra_specs, n_out = list(res), list(res_spec), len(out_shape)
    n_in = 2 + len(extra)

    def body(*refs):
        a_ref, b_ref = refs[0], refs[1]
        extra_refs, out_refs = refs[2:n_in], refs[n_in:n_in + n_out]

        def prod():
            return _dot(a_ref[...], b_ref[...], kind)

        def finish(acc):
            if epilogue is not None:
                ids = [pl.program_id(ax) for ax in range(len(grid) - 1)]
                first = functools.reduce(jnp.logical_and, [i == 0 for i in ids]) if ids else True
                epilogue(acc, extra_refs, out_refs, first)
                return
            if extra_refs:
                acc = acc + extra_refs[0][...]
            out_refs[0][...] = acc.astype(out_refs[0].dtype)

        if nk == 1:
            finish(prod())
        else:
            acc_ref = refs[n_in + n_out]
            k = pl.program_id(len(grid) - 1)

            @pl.when(k == 0)
            def _():
                acc_ref[...] = prod()

            @pl.when(k > 0)
            def _():
                acc_ref[...] += prod()

            @pl.when(k == nk - 1)
            def _():
                finish(acc_ref[...])

    in_specs = [a_spec, b_spec] + extra_specs
    args = (a, b, *extra)
    scratch = [pltpu.VMEM(acc_shape, F32)] if nk > 1 else []
    sem = ("arbitrary",) * len(grid) if epilogue is not None else ("parallel",) * (len(grid) - 1) + ("arbitrary",)
    out, landed = _pcall(body, args, name=name, out_shape=out_shape, grid=grid, in_specs=in_specs,
                         out_specs=o_spec, scratch_shapes=scratch, sem=sem, comm=comm)
    return out if comm is None else (out, landed)


def _mm_rows(name, a, w, out_dtype=F32, res=None, kind="nn", tm=MM_ROWS, comm=None):
    M, K = a.shape
    N = w.shape[1] if kind == "nn" else w.shape[0]
    tm = _tile(M, tm)
    res_spec = pl.BlockSpec((tm, N), lambda i, k: (i, 0)) if res is not None else None
    return _matmul(
        name, kind, a, w, jax.ShapeDtypeStruct((M, N), out_dtype), (M // tm, 1),
        pl.BlockSpec((tm, K), lambda i, k: (i, 0)), pl.BlockSpec(w.shape, lambda i, k: (0, 0)),
        pl.BlockSpec((tm, N), lambda i, k: (i, 0)), (tm, N), res, res_spec, comm)


def _residual_rms_epilogue(y, operands, outputs, first):
    x_ref, g_ref = operands
    x1_ref, h_ref = outputs
    xv = x_ref[...] + y
    x1_ref[...] = xv
    r = lax.rsqrt(jnp.mean(xv * xv, axis=-1, keepdims=True) + EPS)
    h_ref[...] = (xv * r * g_ref[...]).astype(h_ref.dtype)


def _mm_residual_rms(name, a, w, x, g, tm=MM_ROWS_RES):
    M, K = a.shape
    N = w.shape[1]
    tm = _tile(M, tm)
    row = pl.BlockSpec((tm, N), lambda i, k: (i, 0))
    return _matmul(
        name, "nn", a, w, (jax.ShapeDtypeStruct((M, N), F32), jax.ShapeDtypeStruct((M, N), BF16)), (M // tm, 1),
        pl.BlockSpec((tm, K), lambda i, k: (i, 0)), pl.BlockSpec(w.shape, lambda i, k: (0, 0)),
        (row, row), (tm, N), [x, g], [row, pl.BlockSpec((1, N), lambda i, k: (0, 0))],
        epilogue=_residual_rms_epilogue)


def _mm_tn(name, a, b, out_dtype=F32, tk=MM_TOKENS, comm=None):
    T, M = a.shape
    N = b.shape[1]
    tk = _tile(T, tk)
    return _matmul(
        name, "tn", a, b, jax.ShapeDtypeStruct((M, N), out_dtype), (1, T // tk),
        pl.BlockSpec((tk, M), lambda i, k: (k, 0)), pl.BlockSpec((tk, N), lambda i, k: (k, 0)),
        pl.BlockSpec((M, N), lambda i, k: (0, 0)), (M, N), comm=comm)


def _mm_in_gather(h, shard, tm=MM_ROWS, comm=None):
    T, K = h.shape
    n = shard.shape[1]
    tm = _tile(T, tm)
    n_tiles = T // tm
    pair_of_chip_step = {4: 1, 2: 2, 6: 3}

    def slab_of(s):
        x, y, c = _mesh_pos()
        return _dev_index(x ^ ((s >> 2) & 1), y ^ ((s >> 1) & 1), c ^ (s & 1))

    def body(h_ref, shard_ref, proj_ref, win_ref, wbuf, slot_sems, send_sems, recv_sems, local_sem):
        s, i = pl.program_id(0), pl.program_id(1)
        x, y, c = _mesh_pos()
        me, sib = (x, y, c), (x, y, 1 - c)

        def slot_copy(step):
            src = shard_ref if step == 0 else win_ref.at[slab_of(step)]
            return pltpu.make_async_copy(src, wbuf.at[step % 2], slot_sems.at[step % 2])

        def fetch(step):
            if step >= 1:
                block = win_ref.at[slab_of(step)]
                if step == 1:
                    pair = 0
                elif step % 2 == 0:
                    pair = pair_of_chip_step[step]
                else:
                    pair = 3 + pair_of_chip_step[step - 1]
                _remote(block, block, send_sems, recv_sems, pair, me).wait_recv()
                if step % 2 == 0:
                    _remote(block, block, send_sems, recv_sems, 3 + pair, sib).start()
            slot_copy(step).start()

        @pl.when((s == 0) & (i == 0))
        def _():
            land = win_ref.at[_dev_index(*me)]
            pltpu.make_async_copy(shard_ref, land, local_sem).start()
            _remote(shard_ref, land, send_sems, recv_sems, 0, sib).start()
            for step, pair in pair_of_chip_step.items():
                peer = (x ^ ((step >> 2) & 1), y ^ ((step >> 1) & 1), c)
                _remote(shard_ref, land, send_sems, recv_sems, pair, peer).start()
            fetch(0)

        for step in range(N_DEV):
            @pl.when((s == step) & (i == 0))
            def _():
                slot_copy(step).wait()

            if step + 1 < N_DEV:
                @pl.when((s == step) & (i == n_tiles - 1))
                def _():
                    fetch(step + 1)

        proj_ref[...] = _dot(h_ref[...], wbuf[s % 2])

        @pl.when((s == N_DEV - 1) & (i == n_tiles - 1))
        def _():
            for pair in range(7):
                _remote(shard_ref, win_ref.at[0], send_sems, recv_sems, pair, me).wait_send()
            pltpu.make_async_copy(shard_ref, win_ref.at[_dev_index(*me)], local_sem).wait()

    hbm = pl.BlockSpec(memory_space=pltpu.HBM)
    return _pcall(
        body, (h, shard), name="mm_in",
        out_shape=(jax.ShapeDtypeStruct((T, N_DEV * n), F32), jax.ShapeDtypeStruct((N_DEV, K, n), shard.dtype)),
        grid=(N_DEV, n_tiles), in_specs=[pl.BlockSpec((tm, K), lambda s, i: (i, 0)), hbm],
        out_specs=(pl.BlockSpec((tm, n), lambda s, i: (i, slab_of(s))), hbm),
        scratch_shapes=[pltpu.VMEM((2, K, n), shard.dtype), pltpu.SemaphoreType.DMA((2,)),
                        pltpu.SemaphoreType.DMA((7,)), pltpu.SemaphoreType.DMA((7,)), pltpu.SemaphoreType.DMA],
        sem=("arbitrary", "arbitrary"), comm=comm)


def _rms_bwd_epilogue(dh, operands, outputs, first):
    x_ref, g_ref, dres_ref = operands
    dx_ref, dg_ref = outputs
    xv = x_ref[...]
    r = lax.rsqrt(jnp.mean(xv * xv, axis=-1, keepdims=True) + EPS)
    xhat = xv * r

    @pl.when(first)
    def _():
        dg_ref[...] = jnp.zeros_like(dg_ref)

    dg_ref[...] += jnp.sum(dh * xhat, axis=0, keepdims=True)
    dxhat = dh * g_ref[...]
    dx_ref[...] = dres_ref[...] + r * (dxhat - xhat * jnp.mean(dxhat * xhat, axis=-1, keepdims=True))


def _rms_bwd_fused(M, K, tm, rms):
    row = pl.BlockSpec((tm, K), lambda i, j: (i, 0))
    vec = pl.BlockSpec((1, K), lambda i, j: (0, 0))
    x, g, dres = rms
    return dict(res=[x, g, dres], res_spec=[row, vec, row], epilogue=_rms_bwd_epilogue,
                out_shape=(jax.ShapeDtypeStruct((M, K), F32), jax.ShapeDtypeStruct((1, K), F32)), o_spec=(row, vec))


def _mm_cols_slab_t(name, a, w_slabs, rms, tm=MM_ROWS_RES, comm=None):
    M = a.shape[0]
    J, K, n = w_slabs.shape
    tm = _tile(M, tm)
    fused = _rms_bwd_fused(M, K, tm, rms)
    return _matmul(
        name, "nt", a, w_slabs, fused.pop("out_shape"), (M // tm, J),
        pl.BlockSpec((tm, n), lambda i, j: (i, j)), pl.BlockSpec((None, K, n), lambda i, j: (j, 0, 0)),
        fused.pop("o_spec"), (tm, K), comm=comm, **fused)


def _mm_tn_slab(name, a, b, n, out_dtype=F32, tk=MM_TOKENS, comm=None, part=(0, 1)):
    T, M = a.shape
    p, of = part
    M //= of
    J = b.shape[1] // n
    tk = _tile(T, tk)
    return _matmul(
        name, "tn", a, b, jax.ShapeDtypeStruct((J, M, n), out_dtype), (J, T // tk),
        pl.BlockSpec((tk, M), lambda j, k: (k, p)), pl.BlockSpec((tk, n), lambda j, k: (k, j)),
        pl.BlockSpec((None, M, n), lambda j, k: (j, 0, 0)), (M, n), comm=comm)


def _rms_fwd(name, x, g, tm=512):
    T, Dm = x.shape
    tm = _tile(T, tm)

    def body(x_ref, g_ref, h_ref):
        xv = x_ref[...]
        r = lax.rsqrt(jnp.mean(xv * xv, axis=-1, keepdims=True) + EPS)
        h_ref[...] = (xv * r * g_ref[...]).astype(h_ref.dtype)

    return pl.pallas_call(
        body, out_shape=jax.ShapeDtypeStruct((T, Dm), BF16), grid=(T // tm,),
        in_specs=[pl.BlockSpec((tm, Dm), lambda i: (i, 0)), pl.BlockSpec((1, Dm), lambda i: (0, 0))],
        out_specs=pl.BlockSpec((tm, Dm), lambda i: (i, 0)), name=name, compiler_params=_params(("parallel",)),
    )(x, g)


def _rms_bwd(name, x, g, dh, dres, tm=512):
    T, Dm = x.shape
    tm = _tile(T, tm)
    want_dx = dres is not None

    def body(*refs):
        if want_dx:
            x_ref, g_ref, dh_ref, dres_ref, dx_ref, dg_ref = refs
        else:
            x_ref, g_ref, dh_ref, dg_ref = refs
        xv = x_ref[...]
        r = lax.rsqrt(jnp.mean(xv * xv, axis=-1, keepdims=True) + EPS)
        xhat = xv * r
        dhv = dh_ref[...]

        @pl.when(pl.program_id(0) == 0)
        def _():
            dg_ref[...] = jnp.zeros_like(dg_ref)

        dg_ref[...] += jnp.sum(dhv * xhat, axis=0, keepdims=True)
        if want_dx:
            dxhat = dhv * g_ref[...]
            dx_ref[...] = dres_ref[...] + r * (dxhat - xhat * jnp.mean(dxhat * xhat, axis=-1, keepdims=True))

    row = pl.BlockSpec((tm, Dm), lambda i: (i, 0))
    vec = pl.BlockSpec((1, Dm), lambda i: (0, 0))
    if want_dx:
        return pl.pallas_call(
            body, out_shape=(jax.ShapeDtypeStruct((T, Dm), F32), jax.ShapeDtypeStruct((1, Dm), F32)),
            grid=(T // tm,), in_specs=[row, vec, row, row], out_specs=(row, vec), name=name,
            compiler_params=_params(("arbitrary",)),
        )(x, g, dh, dres)
    return pl.pallas_call(
        body, out_shape=jax.ShapeDtypeStruct((1, Dm), F32), grid=(T // tm,), in_specs=[row, vec, row],
        out_specs=vec, name=name, compiler_params=_params(("arbitrary",)),
    )(x, g, dh)


def _pool_rows(S):
    return _tile(S, 256)


def _pool_count(c0, rows, w):
    t = c0 + lax.broadcasted_iota(jnp.int32, (rows, 1), 0)
    return jnp.minimum(t + 1, w).astype(F32)


def _pool_fwd(proj, w_pool, scale, B, S):
    CH = _pool_rows(S)

    def body(hp_ref, wp_ref, sc_ref, o_ref, pad_ref):
        pad_ref[0:POOL_HALO, :] = jnp.zeros((POOL_HALO, POOL_WIDTH), F32)
        pad_ref[POOL_HALO:, :] = hp_ref[...]
        for gi, w in enumerate(POOL_WINDOWS):
            cols = slice(gi * POOL_GROUP_DIM, (gi + 1) * POOL_GROUP_DIM)
            for c in range(S // CH):
                base = POOL_HALO + c * CH
                acc = pad_ref[base:base + CH, cols]
                tok = acc
                for j in range(1, w):
                    acc = acc + pad_ref[base - j:base - j + CH, cols]
                pooled = acc / _pool_count(c * CH, CH, w) - tok
                z = _dot(pooled, wp_ref[gi])
                o_ref[c * CH:(c + 1) * CH, cols] = (z * sc_ref[:, cols]).astype(o_ref.dtype)

    return pl.pallas_call(
        body, out_shape=jax.ShapeDtypeStruct((B * S, POOL_WIDTH), BF16), grid=(B,),
        in_specs=[pl.BlockSpec((S, POOL_WIDTH), lambda b: (b, 0)),
                  pl.BlockSpec(w_pool.shape, lambda b: (0, 0, 0)),
                  pl.BlockSpec((1, POOL_WIDTH), lambda b: (0, 0))],
        out_specs=pl.BlockSpec((S, POOL_WIDTH), lambda b: (b, 0)),
        scratch_shapes=[pltpu.VMEM((S + POOL_HALO, POOL_WIDTH), F32)],
        name="pool_fwd", compiler_params=_params(("parallel",)),
    )(proj, w_pool, scale)


def _pool_bwd(proj, d_ypre, w_pool, scale, B, S):
    CH = _pool_rows(S)

    def body(hp_ref, dy_ref, wp_ref, sc_ref, dhp_ref, dwp_ref, dsc_ref, pad_ref, sc_pad_ref, dp_ref):
        @pl.when(pl.program_id(0) == 0)
        def _():
            dwp_ref[...] = jnp.zeros_like(dwp_ref)
            dsc_ref[...] = jnp.zeros_like(dsc_ref)

        pad_ref[0:POOL_HALO, :] = jnp.zeros((POOL_HALO, POOL_WIDTH), F32)
        pad_ref[POOL_HALO:, :] = hp_ref[...]
        sc_pad_ref[S:, :] = jnp.zeros((POOL_HALO, POOL_WIDTH), F32)
        for gi, w in enumerate(POOL_WINDOWS):
            cols = slice(gi * POOL_GROUP_DIM, (gi + 1) * POOL_GROUP_DIM)
            for c in range(S // CH):
                base = POOL_HALO + c * CH
                rows = slice(c * CH, (c + 1) * CH)
                acc = pad_ref[base:base + CH, cols]
                tok = acc
                for j in range(1, w):
                    acc = acc + pad_ref[base - j:base - j + CH, cols]
                cnt = _pool_count(c * CH, CH, w)
                pooled = acc / cnt - tok
                z = _dot(pooled, wp_ref[gi])
                dy = dy_ref[rows, cols]
                dsc_ref[:, cols] += jnp.sum(dy * z, axis=0, keepdims=True)
                dz = dy * sc_ref[:, cols]
                dwp_ref[gi] += _dot(pooled, dz, "tn")
                dpool = _dot(dz, wp_ref[gi], "nt")
                dp_ref[rows, cols] = dpool
                sc_pad_ref[rows, cols] = dpool / cnt
            for c in range(S // CH):
                rows = slice(c * CH, (c + 1) * CH)
                acc = sc_pad_ref[rows, cols]
                for j in range(1, w):
                    acc = acc + sc_pad_ref[c * CH + j:c * CH + j + CH, cols]
                dhp_ref[rows, cols] = (acc - dp_ref[rows, cols]).astype(dhp_ref.dtype)

    seq = pl.BlockSpec((S, POOL_WIDTH), lambda b: (b, 0))
    return pl.pallas_call(
        body,
        out_shape=(jax.ShapeDtypeStruct((B * S, POOL_WIDTH), BF16),
                   jax.ShapeDtypeStruct(w_pool.shape, F32), jax.ShapeDtypeStruct((1, POOL_WIDTH), F32)),
        grid=(B,),
        in_specs=[seq, seq, pl.BlockSpec(w_pool.shape, lambda b: (0, 0, 0)),
                  pl.BlockSpec((1, POOL_WIDTH), lambda b: (0, 0))],
        out_specs=(seq, pl.BlockSpec(w_pool.shape, lambda b: (0, 0, 0)),
                   pl.BlockSpec((1, POOL_WIDTH), lambda b: (0, 0))),
        scratch_shapes=[pltpu.VMEM((S + POOL_HALO, POOL_WIDTH), F32),
                        pltpu.VMEM((S + POOL_HALO, POOL_WIDTH), F32),
                        pltpu.VMEM((S, POOL_WIDTH), F32)],
        name="pool_bwd", compiler_params=_params(("arbitrary",)),
    )(proj, d_ypre, w_pool, scale)


def _ret_tables(S):
    half = RET_QK_DIM // 2
    inv = ROPE_BASE ** (-jnp.arange(half, dtype=F32) / half)
    ang = jnp.arange(S, dtype=F32)[:, None] * inv[None, :]
    cos, sin = jnp.cos(ang), jnp.sin(ang)
    cos_full = jnp.concatenate([cos, cos], axis=-1)
    sin_signed = jnp.concatenate([-sin, sin], axis=-1)
    C = RET_CHUNK
    lg = jnp.log1p(-jnp.exp2(-5.0 - jnp.arange(RET_HEADS, dtype=F32)))[:, None, None]
    idx = jnp.arange(C, dtype=F32)
    rel = idx[:, None] - idx[None, :]
    decay = jnp.where(rel >= 0, jnp.exp(jnp.maximum(rel, 0.0) * lg), 0.0)
    q_decay = jnp.broadcast_to(jnp.exp((idx + 1.0)[None, :, None] * lg), (RET_HEADS, C, RET_QK_DIM))
    k_decay = jnp.broadcast_to(jnp.exp((C - 1.0 - idx)[None, :, None] * lg), (RET_HEADS, C, RET_QK_DIM))
    c_decay = jnp.broadcast_to(jnp.exp(C * lg), (RET_HEADS, 1, RET_V_DIM))
    return cos_full, sin_signed, decay, q_decay, k_decay, c_decay


def _rope(x, cos_full, sin_signed):
    return x * cos_full + pltpu.roll(x, RET_QK_DIM // 2, axis=1) * sin_signed


def _rope_t(dy, cos_full, sin_signed):
    return dy * cos_full + pltpu.roll(dy * sin_signed, RET_QK_DIM // 2, axis=1)


RET_COLS = 512


def _ret_specs(N, chunk_of):
    C = RET_CHUNK

    def rows(width, col=0):
        return pl.BlockSpec((C, width), lambda b, i: (b * N + chunk_of(i), col))

    def whole(shape):
        return pl.BlockSpec(shape, lambda b, i: (0,) * len(shape))

    wide = RET_HEADS * RET_V_DIM
    return dict(
        q=rows(RET_COLS, COL_Q // RET_COLS), k=rows(RET_COLS, COL_K // RET_COLS),
        v=[rows(RET_COLS, COL_V // RET_COLS + j) for j in range(2)],
        gr=[rows(RET_COLS, COL_GR // RET_COLS + j) for j in range(2)],
        table=pl.BlockSpec((C, RET_QK_DIM), lambda b, i: (chunk_of(i), 0)),
        decay=whole((RET_HEADS, C, C)), qd=whole((RET_HEADS, C, RET_QK_DIM)), kd=whole((RET_HEADS, C, RET_QK_DIM)),
        cd=whole((RET_HEADS, 1, RET_V_DIM)), vec=whole((1, wide)), qk_rows=rows(RET_COLS), v_rows=rows(wide),
        state=pl.BlockSpec((None, None, RET_HEADS, RET_QK_DIM, RET_V_DIM), lambda b, i: (b, chunk_of(i), 0, 0, 0)))


def _head_cols(h):
    pair = slice((h % 2) * RET_V_DIM, (h % 2 + 1) * RET_V_DIM)
    return slice(h * RET_QK_DIM, (h + 1) * RET_QK_DIM), h // 2, pair, slice(h * RET_V_DIM, (h + 1) * RET_V_DIM)


def _group_norm(o):
    mu = jnp.mean(o, axis=-1, keepdims=True)
    oc = o - mu
    rstd = lax.rsqrt(jnp.mean(oc * oc, axis=-1, keepdims=True) + EPS)
    return oc * rstd, rstd


def _ret_fwd(proj, g_ret, b_ret, tables, B, S, comm=None):
    N = S // RET_CHUNK
    cos_t, sin_t, decay, q_decay, k_decay, c_decay = tables
    sp = _ret_specs(N, lambda i: i)

    def body(q_ref, k_ref, v0_ref, v1_ref, gr0_ref, gr1_ref, cos_ref, sin_ref, dec_ref, qd_ref, kd_ref, cd_ref,
             g_ref, b_ref, y_ref, rs_ref, r_ref):
        @pl.when(pl.program_id(1) == 0)
        def _():
            r_ref[...] = jnp.zeros_like(r_ref)

        cs, sn = cos_ref[...], sin_ref[...]
        for h in range(RET_HEADS):
            qk, j, pair, wide = _head_cols(h)
            q = _rope(q_ref[:, qk], cs, sn)
            k = _rope(k_ref[:, qk], cs, sn) * (RET_QK_DIM ** -0.5)
            v = (v0_ref, v1_ref)[j][:, pair]
            R = r_ref[h]
            rs_ref[h] = R
            s = _dot(q, k, "nt") * dec_ref[h]
            o = _dot(s, v) + _dot(q * qd_ref[h], R)
            r_ref[h] = cd_ref[h] * R + _dot(k * kd_ref[h], v, "tn")
            on, _ = _group_norm(o)
            gr = (gr0_ref, gr1_ref)[j][:, pair]
            y_ref[:, wide] = (gr * jax.nn.sigmoid(gr) * (on * g_ref[:, wide] + b_ref[:, wide])).astype(y_ref.dtype)

    state = jax.ShapeDtypeStruct((B, N, RET_HEADS, RET_QK_DIM, RET_V_DIM), F32)
    return _pcall(
        body, (proj,) * 6 + (cos_t, sin_t, decay, q_decay, k_decay, c_decay, g_ret, b_ret),
        name="ret_fwd", out_shape=(jax.ShapeDtypeStruct((B * S, RET_HEADS * RET_V_DIM), BF16), state), grid=(B, N),
        in_specs=[sp["q"], sp["k"], *sp["v"], *sp["gr"], sp["table"], sp["table"], sp["decay"], sp["qd"],
                  sp["kd"], sp["cd"], sp["vec"], sp["vec"]],
        out_specs=(sp["v_rows"], sp["state"]),
        scratch_shapes=[pltpu.VMEM((RET_HEADS, RET_QK_DIM, RET_V_DIM), F32)],
        sem=("parallel", "arbitrary"), comm=comm)


def _ret_bwd(proj, states, d_yr, g_ret, b_ret, tables, B, S, comm=None):
    N = S // RET_CHUNK
    cos_t, sin_t, decay, q_decay, k_decay, c_decay = tables
    sp = _ret_specs(N, lambda i: N - 1 - i)
    qk_scale = RET_QK_DIM ** -0.5

    def body(q_ref, k_ref, v0_ref, v1_ref, gr0_ref, gr1_ref, dy_ref, rs_ref, cos_ref, sin_ref, dec_ref, qd_ref,
             kd_ref, cd_ref, g_ref, b_ref, dq_ref, dk_ref, dv_ref, dgr_ref, dg_ref, db_ref, dr_ref):
        @pl.when((pl.program_id(0) == 0) & (pl.program_id(1) == 0))
        def _():
            dg_ref[...] = jnp.zeros_like(dg_ref)
            db_ref[...] = jnp.zeros_like(db_ref)

        @pl.when(pl.program_id(1) == 0)
        def _():
            dr_ref[...] = jnp.zeros_like(dr_ref)

        cs, sn = cos_ref[...], sin_ref[...]
        for h in range(RET_HEADS):
            qk, j, pair, wide = _head_cols(h)
            q = _rope(q_ref[:, qk], cs, sn)
            k = _rope(k_ref[:, qk], cs, sn) * qk_scale
            v = (v0_ref, v1_ref)[j][:, pair]
            R, dR = rs_ref[h], dr_ref[h]
            dec, qd, kd = dec_ref[h], qd_ref[h], kd_ref[h]
            s = _dot(q, k, "nt") * dec
            o = _dot(s, v) + _dot(q * qd, R)
            on, rstd = _group_norm(o)
            g = g_ref[:, wide]
            oaff = on * g + b_ref[:, wide]
            gr = (gr0_ref, gr1_ref)[j][:, pair]
            sg = jax.nn.sigmoid(gr)
            dy = dy_ref[:, wide]
            dgr_ref[:, wide] = (dy * oaff * (sg * (1.0 + gr * (1.0 - sg)))).astype(dgr_ref.dtype)
            doaff = dy * (gr * sg)
            dg_ref[:, wide] += jnp.sum(doaff * on, axis=0, keepdims=True)
            db_ref[:, wide] += jnp.sum(doaff, axis=0, keepdims=True)
            don = doaff * g
            do = rstd * (don - jnp.mean(don, axis=-1, keepdims=True)
                         - on * jnp.mean(don * on, axis=-1, keepdims=True))
            ds = _dot(do, v, "nt") * dec
            dq = _dot(ds, k) + qd * _dot(do, R, "nt")
            dk = _dot(ds, q, "tn") + kd * _dot(v, dR, "nt")
            dv_ref[:, wide] = (_dot(s, do, "tn") + _dot(k * kd, dR)).astype(dv_ref.dtype)
            dr_ref[h] = cd_ref[h] * dR + _dot(q * qd, do, "tn")
            dq_ref[:, qk] = _rope_t(dq, cs, sn).astype(dq_ref.dtype)
            dk_ref[:, qk] = _rope_t(dk * qk_scale, cs, sn).astype(dk_ref.dtype)

    T = B * S
    qk_shape = jax.ShapeDtypeStruct((T, RET_HEADS * RET_QK_DIM), BF16)
    v_shape = jax.ShapeDtypeStruct((T, RET_HEADS * RET_V_DIM), BF16)
    vec_shape = jax.ShapeDtypeStruct((1, RET_HEADS * RET_V_DIM), F32)
    return _pcall(
        body, (proj,) * 6 + (d_yr, states, cos_t, sin_t, decay, q_decay, k_decay, c_decay, g_ret, b_ret),
        name="ret_bwd", out_shape=(qk_shape, qk_shape, v_shape, v_shape, vec_shape, vec_shape), grid=(B, N),
        in_specs=[sp["q"], sp["k"], *sp["v"], *sp["gr"], sp["v_rows"], sp["state"], sp["table"], sp["table"],
                  sp["decay"], sp["qd"], sp["kd"], sp["cd"], sp["vec"], sp["vec"]],
        out_specs=(sp["qk_rows"], sp["qk_rows"], sp["v_rows"], sp["v_rows"], sp["vec"], sp["vec"]),
        scratch_shapes=[pltpu.VMEM((RET_HEADS, RET_QK_DIM, RET_V_DIM), F32)],
        sem=("arbitrary", "arbitrary"), comm=comm)


def _xa_rows(S):
    return _tile(S, 256)


def _xa_specs(S, M):
    q = pl.BlockSpec((S, XA_HEAD_DIM), lambda b, h: (b, COL_QX // XA_HEAD_DIM + h))
    k = pl.BlockSpec((M, XA_HEAD_DIM), lambda b, h: (b, h))
    v = pl.BlockSpec((M, XA_HEAD_DIM), lambda b, h: (b, XA_HEADS + h))
    o = pl.BlockSpec((S, XA_HEAD_DIM), lambda b, h: (b, h))
    return q, k, v, o


def _softmax_rows(s):
    e = jnp.exp(s - jnp.max(s, axis=-1, keepdims=True))
    return e / jnp.sum(e, axis=-1, keepdims=True)


def _xa_fwd(proj, kv, B, S, M, comm=None):
    CH = _xa_rows(S)
    q_spec, k_spec, v_spec, o_spec = _xa_specs(S, M)

    def body(q_ref, k_ref, v_ref, o_ref):
        def chunk(i, carry):
            rows = pl.ds(pl.multiple_of(i * CH, CH), CH)
            p = _softmax_rows(_dot(q_ref[rows, :], k_ref[...], "nt") * (XA_HEAD_DIM ** -0.5))
            o_ref[rows, :] = _dot(p, v_ref[...]).astype(o_ref.dtype)
            return carry

        lax.fori_loop(0, S // CH, chunk, 0)

    return _pcall(
        body, (proj, kv, kv), name="xattn_fwd", out_shape=jax.ShapeDtypeStruct((B * S, XA_WIDTH), BF16),
        grid=(B, XA_HEADS), in_specs=[q_spec, k_spec, v_spec], out_specs=o_spec,
        sem=("parallel", "parallel"), comm=comm)


def _xa_bwd(proj, kv, d_o, B, S, M, comm=None):
    CH = _xa_rows(S)
    q_spec, k_spec, v_spec, o_spec = _xa_specs(S, M)
    scale = XA_HEAD_DIM ** -0.5

    def body(q_ref, k_ref, v_ref, do_ref, dq_ref, dk_ref, dv_ref):
        dk_ref[...] = jnp.zeros_like(dk_ref)
        dv_ref[...] = jnp.zeros_like(dv_ref)

        def chunk(i, carry):
            rows = pl.ds(pl.multiple_of(i * CH, CH), CH)
            q, do = q_ref[rows, :], do_ref[rows, :]
            p = _softmax_rows(_dot(q, k_ref[...], "nt") * scale)
            dp = _dot(do, v_ref[...], "nt")
            ds = p * (dp - jnp.sum(dp * p, axis=-1, keepdims=True)) * scale
            dq_ref[rows, :] = _dot(ds, k_ref[...]).astype(dq_ref.dtype)
            dk_ref[...] += _dot(ds, q, "tn")
            dv_ref[...] += _dot(p, do, "tn")
            return carry

        lax.fori_loop(0, S // CH, chunk, 0)

    kv_out = pl.BlockSpec((M, XA_HEAD_DIM), lambda b, h: (b, h))
    return _pcall(
        body, (proj, kv, kv, d_o), name="xattn_bwd",
        out_shape=(jax.ShapeDtypeStruct((B * S, XA_WIDTH), BF16), jax.ShapeDtypeStruct((B * M, XA_WIDTH), F32),
                   jax.ShapeDtypeStruct((B * M, XA_WIDTH), F32)),
        grid=(B, XA_HEADS), in_specs=[q_spec, k_spec, v_spec, o_spec], out_specs=(o_spec, kv_out, kv_out),
        sem=("parallel", "parallel"), comm=comm)


def _gate_specs(tm):
    n = COL_GL // D_MODEL
    return [pl.BlockSpec((tm, D_MODEL), lambda i, j=j: (i, n + j)) for j in range(3)]


def _merge_fwd(proj, ys, tm=256):
    T = proj.shape[0]
    tm = _tile(T, tm)
    row = pl.BlockSpec((tm, D_MODEL), lambda i: (i, 0))

    def body(g0, g1, g2, y0, y1, y2, o_ref):
        acc = jax.nn.sigmoid(g0[...]) * y0[...]
        acc = acc + jax.nn.sigmoid(g1[...]) * y1[...]
        acc = acc + jax.nn.sigmoid(g2[...]) * y2[...]
        o_ref[...] = acc.astype(o_ref.dtype)

    return pl.pallas_call(
        body, out_shape=jax.ShapeDtypeStruct((T, D_MODEL), BF16), grid=(T // tm,),
        in_specs=_gate_specs(tm) + [row] * 3, out_specs=row, name="merge_fwd",
        compiler_params=_params(("parallel",)),
    )(proj, proj, proj, *ys)


def _merge_bwd(proj, ys, d_merged, tm=256, comm=None):
    T = proj.shape[0]
    tm = _tile(T, tm)
    row = pl.BlockSpec((tm, D_MODEL), lambda i: (i, 0))

    def body(g0, g1, g2, y0, y1, y2, dm_ref, dgl_ref, d0, d1, d2):
        dm = dm_ref[...]
        for j, (g_ref, y_ref, d_ref) in enumerate(((g0, y0, d0), (g1, y1, d1), (g2, y2, d2))):
            sg = jax.nn.sigmoid(g_ref[...])
            d_ref[...] = (dm * sg).astype(d_ref.dtype)
            dgl_ref[:, j * D_MODEL:(j + 1) * D_MODEL] = (dm * y_ref[...] * sg * (1.0 - sg)).astype(dgl_ref.dtype)

    dy = jax.ShapeDtypeStruct((T, D_MODEL), BF16)
    return _pcall(
        body, (proj, proj, proj, *ys, d_merged), name="merge_bwd",
        out_shape=(jax.ShapeDtypeStruct((T, 3 * D_MODEL), BF16), dy, dy, dy), grid=(T // tm,),
        in_specs=_gate_specs(tm) + [row] * 4,
        out_specs=(pl.BlockSpec((tm, 3 * D_MODEL), lambda i: (i, 0)), row, row, row),
        sem=("parallel",), comm=comm)


def _gelu(x):
    return 0.5 * x * (1.0 + jnp.tanh(GELU_C * (x + GELU_A * x * x * x)))


def _gelu_grad(x):
    t = jnp.tanh(GELU_C * (x + GELU_A * x * x * x))
    return 0.5 * (1.0 + t) + 0.5 * x * (1.0 - t * t) * GELU_C * (1.0 + 3.0 * GELU_A * x * x)


def _shift_down(x, prev, n):
    rows = x.shape[0]
    r = lax.broadcasted_iota(jnp.int32, (8, 1), 0)
    rolled = pltpu.roll(x, n, axis=0)
    head = rolled[0:8]
    for j in range(n):
        head = jnp.where(r == j, prev[8 - n + j:8 - n + j + 1, :], head)
    return head if rows == 8 else jnp.concatenate([head, rolled[8:]], axis=0)


def _shift_up(x, nxt, n):
    rows = x.shape[0]
    r = lax.broadcasted_iota(jnp.int32, (8, 1), 0)
    rolled = pltpu.roll(x, rows - n, axis=0)
    tail = rolled[rows - 8:]
    for j in range(n):
        tail = jnp.where(r == 8 - n + j, nxt[j:j + 1, :], tail)
    return jnp.concatenate([rolled[:rows - 8], tail], axis=0)


def _conv(a, prev, cw, cb):
    return _shift_down(a, prev, 2) * cw[0:1, :] + _shift_down(a, prev, 1) * cw[1:2, :] + a * cw[2:3, :] + cb


def _glu_fwd(up, cw, cb, S, tm=256):
    T = up.shape[2]
    tm = _tile(S, tm)
    per_seq = S // tm

    def body(ab_ref, prev_ref, cw_ref, cb_ref, u_ref):
        i = pl.program_id(1)
        prev = jnp.where(i % per_seq == 0, 0.0, prev_ref[...])
        ac = _conv(ab_ref[0], prev, cw_ref[...], cb_ref[...])
        u_ref[...] = (_gelu(ac) * ab_ref[1]).astype(u_ref.dtype)

    return pl.pallas_call(
        body, out_shape=jax.ShapeDtypeStruct((FFN_SLABS, T, UP_SHARD), BF16), grid=(FFN_SLABS, T // tm),
        in_specs=[pl.BlockSpec((2, None, tm, UP_SHARD), lambda d, i: (0, d, i, 0)),
                  pl.BlockSpec((None, None, 8, UP_SHARD), lambda d, i: (0, d, jnp.maximum(i * (tm // 8) - 1, 0), 0)),
                  pl.BlockSpec((None, 3, UP_SHARD), lambda d, i: (d, 0, 0)),
                  pl.BlockSpec((None, 1, UP_SHARD), lambda d, i: (d, 0, 0))],
        out_specs=pl.BlockSpec((None, tm, UP_SHARD), lambda d, i: (d, i, 0)), name="glu_fwd",
        compiler_params=_params(("parallel", "parallel")),
    )(up, up, cw, cb)


def _glu_bwd(up, d_u, cw, cb, S, tm=256, comm=None):
    T = up.shape[2]
    tm = _tile(S, tm)
    per_seq = S // tm
    n_tiles = T // tm
    last8 = tm // 8

    def body(ab_ref, prev_ref, abn_ref, du_ref, dun_ref, cw_ref, cb_ref, dup_ref, dcw_ref, dcb_ref):
        i = pl.program_id(1)

        @pl.when(i == 0)
        def _():
            dcw_ref[...] = jnp.zeros_like(dcw_ref)
            dcb_ref[...] = jnp.zeros_like(dcb_ref)

        cw, cb = cw_ref[...], cb_ref[...]
        a, b = ab_ref[0], ab_ref[1]
        prev = jnp.where(i % per_seq == 0, 0.0, prev_ref[...])
        a2, a1 = _shift_down(a, prev, 2), _shift_down(a, prev, 1)
        ac = a2 * cw[0:1, :] + a1 * cw[1:2, :] + a * cw[2:3, :] + cb
        du = du_ref[...]
        dup_ref[1] = (du * _gelu(ac)).astype(dup_ref.dtype)
        dac = du * b * _gelu_grad(ac)
        dcb_ref[...] += jnp.sum(dac, axis=0, keepdims=True)
        dcw_ref[0:1, :] += jnp.sum(dac * a2, axis=0, keepdims=True)
        dcw_ref[1:2, :] += jnp.sum(dac * a1, axis=0, keepdims=True)
        dcw_ref[2:3, :] += jnp.sum(dac * a, axis=0, keepdims=True)
        an = abn_ref[0]
        acn = _conv(an, a[tm - 8:, :], cw, cb)
        dacn = jnp.where(i % per_seq == per_seq - 1, 0.0, dun_ref[...] * abn_ref[1] * _gelu_grad(acn))
        da = dac * cw[2:3, :] + _shift_up(dac, dacn, 1) * cw[1:2, :] + _shift_up(dac, dacn, 2) * cw[0:1, :]
        dup_ref[0] = da.astype(dup_ref.dtype)

    def nxt(i):
        return jnp.minimum((i + 1) * last8, T // 8 - 1)

    return _pcall(
        body, (up, up, up, d_u, d_u, cw, cb), name="glu_bwd",
        out_shape=(jax.ShapeDtypeStruct((2, FFN_SLABS, T, UP_SHARD), BF16),
                   jax.ShapeDtypeStruct((FFN_SLABS, 3, UP_SHARD), F32),
                   jax.ShapeDtypeStruct((FFN_SLABS, 1, UP_SHARD), F32)),
        grid=(FFN_SLABS, n_tiles),
        in_specs=[pl.BlockSpec((2, None, tm, UP_SHARD), lambda d, i: (0, d, i, 0)),
                  pl.BlockSpec((None, None, 8, UP_SHARD), lambda d, i: (0, d, jnp.maximum(i * last8 - 1, 0), 0)),
                  pl.BlockSpec((2, None, 8, UP_SHARD), lambda d, i: (0, d, nxt(i), 0)),
                  pl.BlockSpec((None, tm, UP_SHARD), lambda d, i: (d, i, 0)),
                  pl.BlockSpec((None, 8, UP_SHARD), lambda d, i: (d, nxt(i), 0)),
                  pl.BlockSpec((None, 3, UP_SHARD), lambda d, i: (d, 0, 0)),
                  pl.BlockSpec((None, 1, UP_SHARD), lambda d, i: (d, 0, 0))],
        out_specs=(pl.BlockSpec((2, None, tm, UP_SHARD), lambda d, i: (0, d, i, 0)),
                   pl.BlockSpec((None, 3, UP_SHARD), lambda d, i: (d, 0, 0)),
                   pl.BlockSpec((None, 1, UP_SHARD), lambda d, i: (d, 0, 0))),
        sem=("parallel", "arbitrary"), comm=comm)


def _mm_up(h2, w_up, tm=MM_ROWS):
    T, K = h2.shape
    tm = _tile(T, tm)
    return _matmul(
        "mm_up", "nn", h2, w_up, jax.ShapeDtypeStruct((N_DEV, T, UP_SHARD), F32), (N_DEV, T // tm, 1),
        pl.BlockSpec((tm, K), lambda j, i, k: (i, 0)), pl.BlockSpec((None, K, UP_SHARD), lambda j, i, k: (j, 0, 0)),
        pl.BlockSpec((None, tm, UP_SHARD), lambda j, i, k: (j, i, 0)), (tm, UP_SHARD))


def _loss_epilogue(ffn, operands, outputs, first):
    x1_ref, t_ref, g_ref = operands
    dx_ref, dg_ref, loss_ref = outputs

    @pl.when(first)
    def _():
        dg_ref[...] = jnp.zeros_like(dg_ref)
        loss_ref[...] = jnp.zeros_like(loss_ref)

    xv = x1_ref[...] + ffn
    r = lax.rsqrt(jnp.mean(xv * xv, axis=-1, keepdims=True) + EPS)
    xhat = xv * r
    err = xhat * g_ref[...] - t_ref[...]
    loss_ref[...] += (0.5 / D_MODEL) * jnp.sum(err * err)
    dy = err * (1.0 / D_MODEL)
    dg_ref[...] += jnp.sum(dy * xhat, axis=0, keepdims=True)
    dxhat = dy * g_ref[...]
    dx_ref[...] = r * (dxhat - xhat * jnp.mean(dxhat * xhat, axis=-1, keepdims=True))


def _mm_down_loss(u, w_down, x1, target, g_final, tm=MM_ROWS_RES):
    J, T, n = u.shape
    tm = _tile(T, tm)
    row = pl.BlockSpec((tm, D_MODEL), lambda i, d: (i, 0))
    vec = pl.BlockSpec((1, D_MODEL), lambda i, d: (0, 0))
    vec_shape = jax.ShapeDtypeStruct((1, D_MODEL), F32)
    return _matmul(
        "mm_down", "nn", u, w_down, (jax.ShapeDtypeStruct((T, D_MODEL), F32), vec_shape, vec_shape), (T // tm, J),
        pl.BlockSpec((None, tm, n), lambda i, d: (d, i, 0)), pl.BlockSpec((None, n, D_MODEL), lambda i, d: (d, 0, 0)),
        (row, vec, vec), (tm, D_MODEL), [x1, target, g_final], [row, row, vec], epilogue=_loss_epilogue)


def _mm_down_t(dx, w_down, tm=MM_ROWS):
    T = dx.shape[0]
    J, n, _ = w_down.shape
    tm = _tile(T, tm)
    return _matmul(
        "mm_down_t", "nt", dx, w_down, jax.ShapeDtypeStruct((J, T, n), F32), (J, T // tm, 1),
        pl.BlockSpec((tm, D_MODEL), lambda d, i, k: (i, 0)), pl.BlockSpec((None, n, D_MODEL), lambda d, i, k: (d, 0, 0)),
        pl.BlockSpec((None, tm, n), lambda d, i, k: (d, i, 0)), (tm, n))


def _mm_dw_down(u, dx, tk=MM_TOKENS):
    J, T, n = u.shape
    tk = _tile(T, tk)
    return _matmul(
        "mm_dw_down", "tn", u, dx, jax.ShapeDtypeStruct((J, n, D_MODEL), BF16), (J, T // tk),
        pl.BlockSpec((None, tk, n), lambda d, k: (d, k, 0)), pl.BlockSpec((tk, D_MODEL), lambda d, k: (k, 0)),
        pl.BlockSpec((None, n, D_MODEL), lambda d, k: (d, 0, 0)), (n, D_MODEL))


def _mm_dw_up(name, h2, d_up, part, tk=MM_TOKENS, comm=None):
    T, K = h2.shape
    p, of = part
    K //= of
    tk = _tile(T, tk)
    return _matmul(
        name, "tn", h2, d_up, jax.ShapeDtypeStruct((N_DEV, K, UP_SHARD), BF16), (N_DEV, T // tk),
        pl.BlockSpec((tk, K), lambda j, k: (k, p)), pl.BlockSpec((None, tk, UP_SHARD), lambda j, k: (j, k, 0)),
        pl.BlockSpec((None, K, UP_SHARD), lambda j, k: (j, 0, 0)), (K, UP_SHARD), comm=comm)


def _mm_up_t(d_up, w_up, rms, tm=MM_ROWS_RES, comm=None):
    J, T, n = d_up.shape
    K = w_up.shape[1]
    tm = _tile(T, tm)
    fused = _rms_bwd_fused(T, K, tm, rms)
    return _matmul(
        "mm_up_t", "nt", d_up, w_up, fused.pop("out_shape"), (T // tm, J),
        pl.BlockSpec((None, tm, n), lambda i, j: (j, i, 0)), pl.BlockSpec((None, K, n), lambda i, j: (j, 0, 0)),
        fused.pop("o_spec"), (tm, K), comm=comm, **fused)


def _cast_shards(shards):
    def body(*refs):
        n = len(refs) // 2
        for src, dst in zip(refs[:n], refs[n:]):
            dst[...] = src[...].astype(dst.dtype)

    return pl.pallas_call(
        body, out_shape=[jax.ShapeDtypeStruct(s.shape, BF16) for s in shards], name="cast_shards",
        compiler_params=pltpu.CompilerParams(vmem_limit_bytes=VMEM_LIMIT),
    )(*shards)


def _adamw(w, g, m, v):
    m = ADAM_B1 * m + (1.0 - ADAM_B1) * g
    v = ADAM_B2 * v + (1.0 - ADAM_B2) * (g * g)
    m_hat = m / (1.0 - ADAM_B1 ** ADAM_STEP)
    v_hat = v / (1.0 - ADAM_B2 ** ADAM_STEP)
    delta = -ADAM_LR * (m_hat / (jnp.sqrt(v_hat) + ADAM_EPS) + ADAM_WD * w)
    return delta, m, v


def _sum_parts(p_ref):
    g = p_ref[0].astype(F32)
    for d in range(1, N_DEV):
        g = g + p_ref[d].astype(F32)
    return g


def _reduce_adam(name, parts, w, m, v, tr=128):
    R, Cn = w.shape
    by_rows = sum(p.shape[1] for p in parts) == R and len(parts) > 1
    tr = math.gcd(tr, *[p.shape[1] for p in parts])
    n_tiles = [p.shape[1] // tr for p in parts]
    first = [sum(n_tiles[:j]) for j in range(len(parts))] if by_rows else [0] * len(parts)

    def body(*refs):
        p_refs = refs[:len(parts)]
        w_ref, m_ref, v_ref, g_out, d_out, m_out, v_out = refs[len(parts):]

        def update(p_ref):
            g = _sum_parts(p_ref)
            delta, m_new, v_new = _adamw(w_ref[...], g, m_ref[...], v_ref[...])
            g_out[...] = g
            d_out[...] = delta
            m_out[...] = m_new
            v_out[...] = v_new

        if len(parts) == 1:
            update(p_refs[0])
        elif by_rows:
            i = pl.program_id(0)
            for p_ref, t0, n in zip(p_refs, first, n_tiles):
                pl.when((i >= t0) & (i < t0 + n))(functools.partial(update, p_ref))
        else:
            c = lax.axis_index("c")
            for side, p_ref in enumerate(p_refs):
                pl.when(c == side)(functools.partial(update, p_ref))

    def part_spec(t0, n):
        return pl.BlockSpec((N_DEV, tr, Cn), lambda i: (0, jnp.clip(i - t0, 0, n - 1), 0))

    row = pl.BlockSpec((tr, Cn), lambda i: (i, 0))
    shape = jax.ShapeDtypeStruct((R, Cn), F32)
    return pl.pallas_call(
        body, out_shape=(shape,) * 4, grid=(R // tr,),
        in_specs=[part_spec(t0, n) for t0, n in zip(first, n_tiles)] + [row, row, row],
        out_specs=(row,) * 4, name=name, compiler_params=_params(("parallel",)),
    )(*parts, w, m, v)


def _small_adam(name, gathered, params):
    n_g, n_p = len(gathered), len(params)

    def body(*refs):
        g_refs = refs[:n_g]
        wmv = refs[n_g:n_g + 3 * n_p]
        sums = refs[n_g + 3 * n_p:2 * n_g + 3 * n_p]
        upd = refs[2 * n_g + 3 * n_p:]
        for j in range(n_g):
            g = _sum_parts(g_refs[j])
            sums[j][...] = g
            if j < n_p:
                w_ref, m_ref, v_ref = wmv[3 * j:3 * j + 3]
                delta, m_new, v_new = _adamw(w_ref[...], g, m_ref[...], v_ref[...])
                upd[3 * j][...] = delta
                upd[3 * j + 1][...] = m_new
                upd[3 * j + 2][...] = v_new

    flat = [a for wmv in params for a in wmv]
    out_shape = [jax.ShapeDtypeStruct(g.shape[1:], F32) for g in gathered]
    out_shape += [jax.ShapeDtypeStruct(a.shape, F32) for a in flat]
    res = pl.pallas_call(body, out_shape=out_shape, name=name)(*gathered, *flat)
    return res[:n_g], [tuple(res[n_g + 3 * j:n_g + 3 * j + 3]) for j in range(n_p)]


def _adam_only(name, g, w, m, v):
    def body(g_ref, w_ref, m_ref, v_ref, d_out, m_out, v_out):
        delta, m_new, v_new = _adamw(w_ref[...], g_ref[...], m_ref[...], v_ref[...])
        d_out[...] = delta
        m_out[...] = m_new
        v_out[...] = v_new

    shape = jax.ShapeDtypeStruct(w.shape, F32)
    return pl.pallas_call(body, out_shape=(shape,) * 3, name=name)(g, w, m, v)


def kernel(x, mem, g_mix, w_in, w_pool, pool_scale, w_a, g_ret, b_ret, w_r, g_mem, w_mem_kv, w_c, w_out, g_ffn, w_up, conv_w, conv_b, w_down, g_final, loss_target, m_g_mix, m_w_in, m_w_pool, m_pool_scale, m_w_a, m_g_ret, m_b_ret, m_w_r, m_g_mem, m_w_mem_kv, m_w_c, m_w_out, m_g_ffn, m_w_up, m_conv_w, m_conv_b, m_w_down, m_g_final, v_g_mix, v_w_in, v_w_pool, v_pool_scale, v_w_a, v_g_ret, v_b_ret, v_w_r, v_g_mem, v_w_mem_kv, v_w_c, v_w_out, v_g_ffn, v_w_up, v_conv_w, v_conv_b, v_w_down, v_g_final):
    B, S, _ = x.shape
    M = mem.shape[1]
    T = B * S
    me = _my_index()
    x2d = x.reshape(T, D_MODEL)
    mem2d = mem.reshape(B * M, D_MODEL)
    tgt2d = loss_target.reshape(T, D_MODEL)
    g_final2 = g_final.reshape(1, D_MODEL)

    big = dict(w_in=w_in[0], w_a=w_a[0], w_r=w_r[0], w_mem_kv=w_mem_kv[0], w_c=w_c[0], w_out=w_out[0],
               w_up=w_up[0], w_down=w_down[0])
    names = list(big)
    cast = dict(zip(names, _cast_shards([big[n] for n in names])))
    cb = conv_b[0].reshape(FFN_SLABS, 1, UP_SHARD)
    wp = w_pool[0]
    tables = _ret_tables(S)

    h = _rms_fwd("rms_mix", x2d, g_mix)
    early = ("w_a", "w_r", "w_mem_kv", "w_c", "w_out")
    (proj, Win), landed = _mm_in_gather(h, cast["w_in"], comm=_Gather([cast[n] for n in early] + [conv_w[0]]))
    W = dict(zip(early, landed))
    cw_full = landed[-1].transpose(1, 0, 2).reshape(3, FFN_HIDDEN)
    cw = cw_full.reshape(3, FFN_SLABS, UP_SHARD).transpose(1, 0, 2)
    Wa = W["w_a"].transpose(1, 0, 2).reshape(POOL_WIDTH, D_MODEL)
    Wc = W["w_c"].transpose(1, 0, 2).reshape(XA_WIDTH, D_MODEL)
    Wr = W["w_r"].reshape(D_MODEL, D_MODEL)
    Wkv = W["w_mem_kv"].reshape(D_MODEL, D_MODEL)
    Wout = W["w_out"].reshape(D_MODEL, D_MODEL)
    ypre = _pool_fwd(proj, wp, pool_scale, B, S)
    y_pool = _mm_rows("mm_a", ypre, Wa)
    (yr, ret_states), (Wup,) = _ret_fwd(proj, g_ret, b_ret, tables, B, S, comm=_Gather([cast["w_up"]]))
    y_ret = _mm_rows("mm_r", yr, Wr)
    mem_n = _rms_fwd("rms_mem", mem2d, g_mem)
    kv = _mm_rows("mm_kv", mem_n, Wkv)
    o_mem, (Wdown,) = _xa_fwd(proj, kv, B, S, M, comm=_Gather([cast["w_down"]]))
    Wdown = Wdown.reshape(FFN_SLABS, UP_SHARD, D_MODEL)
    y_mem = _mm_rows("mm_c", o_mem, Wc)
    ys = (y_pool, y_ret, y_mem)
    merged = _merge_fwd(proj, ys)
    x1, h2 = _mm_residual_rms("mm_out", merged, Wout, x2d, g_ffn)
    up = _mm_up(h2, Wup).reshape(2, FFN_SLABS, T, UP_SHARD)
    u = _glu_fwd(up, cw, cb, S)

    dx2, dg_final, loss_part = _mm_down_loss(u, Wdown, x1, tgt2d, g_final2)
    received = {}
    d_u = _mm_down_t(dx2, Wdown)
    dW_down = _mm_dw_down(u, dx2)
    (d_up, d_cw, d_cb), (received["w_down"],) = _glu_bwd(
        up, d_u, cw, cb, S, comm=_Exchange([dW_down.reshape(N_DEV, -1, D_MODEL)]))
    d_up = d_up.reshape(N_DEV, T, UP_SHARD)
    dW_up = _mm_dw_up("mm_dw_up", h2, d_up, (0, 1))
    (dx1, dg_ffn), (up_c0,) = _mm_up_t(d_up, Wup, (x1, g_ffn, dx2), comm=_ExchangeTo([dW_up], 0))
    d_merged = _mm_rows("mm_out_t", dx1, Wout, kind="nt")
    dW_out = _mm_tn("mm_dw_out", merged, dx1, BF16)
    (d_gl, d_y_pool, d_y_ret, d_y_mem), (received["w_out"],) = _merge_bwd(
        proj, ys, d_merged, comm=_Exchange([dW_out.reshape(N_DEV, -1, D_MODEL)]))
    dW_c = _mm_tn("mm_dw_c", o_mem, d_y_mem, BF16)
    d_o_mem = _mm_rows("mm_c_t", d_y_mem, Wc, kind="nt")
    (d_qx, d_kmem, d_vmem), (up_c1,) = _xa_bwd(proj, kv, d_o_mem, B, S, M, comm=_ExchangeTo([dW_up], 1))
    received["w_up"] = [up_c0, up_c1]
    d_kv = jnp.concatenate([d_kmem, d_vmem], axis=1)
    dW_kv = _mm_tn("mm_dw_kv", mem_n, d_kv, BF16)
    d_mem_n = _mm_rows("mm_kv_t", d_kv, Wkv, kind="nt")
    dg_mem = _rms_bwd("rms_mem_bwd", mem2d, g_mem, d_mem_n, None)
    dW_a = _mm_tn("mm_dw_a", ypre, d_y_pool, BF16)
    d_ypre = _mm_rows("mm_a_t", d_y_pool, Wa, kind="nt")
    d_hp, dw_pool, d_scale = _pool_bwd(proj, d_ypre, wp, pool_scale, B, S)
    dW_r = _mm_tn("mm_dw_r", yr, d_y_ret, BF16)
    d_yr = _mm_rows("mm_r_t", d_y_ret, Wr, kind="nt")
    (d_q, d_k, d_v, d_gr, dg_ret, db_ret), landed = _ret_bwd(
        proj, ret_states, d_yr, g_ret, b_ret, tables, B, S,
        comm=_Exchange([dW_a.reshape(POOL_WIDTH, N_DEV, -1).transpose(1, 0, 2), dW_r.reshape(N_DEV, -1, D_MODEL),
                        dW_c.reshape(XA_WIDTH, N_DEV, -1).transpose(1, 0, 2), dW_kv.reshape(N_DEV, -1, D_MODEL)]))
    received["w_a"], received["w_r"], received["w_c"], received["w_mem_kv"] = landed
    small_names = ["w_pool", "pool_scale", "g_ret", "b_ret", "g_mem", "g_ffn", "conv_b", "g_final"]
    small_grads = [dw_pool, d_scale, dg_ret, db_ret, dg_mem, dg_ffn, d_cb.reshape(1, FFN_HIDDEN), dg_final,
                   d_cw.transpose(1, 0, 2).reshape(3, FFN_HIDDEN), loss_part]
    d_proj = jnp.concatenate([d_hp, d_q, d_k, d_v, d_gr, d_qx, d_gl], axis=1)
    dW_in0, small_all = _mm_tn_slab("mm_dw_in0", h[:, :W_IN_FIRST_ROWS], d_proj, IN_SHARD, BF16,
                                    comm=_Exchange([], whole=small_grads))
    dW_in1, (in0,) = _mm_tn_slab("mm_dw_in1", h[:, W_IN_FIRST_ROWS:], d_proj, IN_SHARD, BF16,
                                 comm=_Exchange([dW_in0]))
    (grad_x, dg_mix), (in1,) = _mm_cols_slab_t("mm_in_t", d_proj, Win, (x2d, g_mix, dx1), comm=_Exchange([dW_in1]))
    received["w_in"] = [in0, in1]
    (g_mix_all,) = _comm_call("gather_g_mix", _Exchange([], whole=[dg_mix]))

    args = dict(g_mix=g_mix, w_in=w_in, w_pool=w_pool, pool_scale=pool_scale, w_a=w_a, g_ret=g_ret, b_ret=b_ret,
                w_r=w_r, g_mem=g_mem, w_mem_kv=w_mem_kv, w_c=w_c, w_out=w_out, g_ffn=g_ffn, w_up=w_up,
                conv_w=conv_w, conv_b=conv_b, w_down=w_down, g_final=g_final)
    m_in = dict(g_mix=m_g_mix, w_in=m_w_in, w_pool=m_w_pool, pool_scale=m_pool_scale, w_a=m_w_a, g_ret=m_g_ret,
                b_ret=m_b_ret, w_r=m_w_r, g_mem=m_g_mem, w_mem_kv=m_w_mem_kv, w_c=m_w_c, w_out=m_w_out,
                g_ffn=m_g_ffn, w_up=m_w_up, conv_w=m_conv_w, conv_b=m_conv_b, w_down=m_w_down, g_final=m_g_final)
    v_in = dict(g_mix=v_g_mix, w_in=v_w_in, w_pool=v_w_pool, pool_scale=v_pool_scale, w_a=v_w_a, g_ret=v_g_ret,
                b_ret=v_b_ret, w_r=v_w_r, g_mem=v_g_mem, w_mem_kv=v_w_mem_kv, w_c=v_w_c, w_out=v_w_out,
                g_ffn=v_g_ffn, w_up=v_w_up, conv_w=v_conv_w, conv_b=v_conv_b, w_down=v_w_down, g_final=v_g_final)

    grads, deltas, new_m, new_v = {}, {}, {}, {}
    for n in names:
        shard = big[n].shape
        parts = received[n] if isinstance(received[n], list) else [received[n]]
        outs = _reduce_adam("adam_" + n, parts, big[n], m_in[n][0], v_in[n][0])
        for store, val in zip((grads, deltas, new_m, new_v), outs):
            store[n] = val.reshape((1,) + shard)

    def as_small(a):
        return a.reshape(a.shape[-3:]) if a.ndim > 2 else a.reshape(1, -1)

    def small_update(call_name, param_names, gathered):
        params = [tuple(as_small(d[n]) for d in (args, m_in, v_in)) for n in param_names]
        sums, updates = _small_adam(call_name, gathered, params)
        for n, g, (d_, m_, v_) in zip(param_names, sums, updates):
            shape = args[n].shape
            grads[n], deltas[n], new_m[n], new_v[n] = (a.reshape(shape) for a in (g, d_, m_, v_))
        return sums[len(param_names):]

    g_cw_full, loss_row = small_update("adam_small", small_names, small_all)
    loss = loss_row[0, 0]
    small_update("adam_g_mix", ["g_mix"], [g_mix_all])

    shard_cols = FFN_HIDDEN // N_DEV
    g_cw = lax.dynamic_slice_in_dim(g_cw_full, me * shard_cols, shard_cols, axis=1)
    d_, m_, v_ = _adam_only("adam_conv_w", g_cw, conv_w[0], m_conv_w[0], v_conv_w[0])
    grads["conv_w"], deltas["conv_w"], new_m["conv_w"], new_v["conv_w"] = g_cw[None], d_[None], m_[None], v_[None]

    order = ["g_mix", "w_in", "w_pool", "pool_scale", "w_a", "g_ret", "b_ret", "w_r", "g_mem", "w_mem_kv", "w_c",
             "w_out", "g_ffn", "w_up", "conv_w", "conv_b", "w_down", "g_final"]
    return (loss, grad_x.reshape(B, S, D_MODEL), *[grads[n] for n in order], *[deltas[n] for n in order],
            *[new_m[n] for n in order], *[new_v[n] for n in order])
```

```python
import functools
import math

import jax
import jax.numpy as jnp
from jax import lax
from jax.experimental import pallas as pl
from jax.experimental.pallas import tpu as pltpu

F32 = jnp.float32
BF16 = jnp.bfloat16

N_DEV = 8
D_MODEL = 1024
POOL_WINDOWS = (2, 4, 8, 16)
POOL_GROUP_DIM = 128
POOL_WIDTH = 512
POOL_HALO = 16
RET_HEADS = 4
RET_QK_DIM = 128
RET_V_DIM = 256
RET_CHUNK = 128
ROPE_BASE = 10000.0
XA_HEADS = 4
XA_HEAD_DIM = 128
XA_WIDTH = 512
IN_WIDTH = 7168
IN_SHARD = IN_WIDTH // N_DEV
FFN_HIDDEN = 2816
UP_SHARD = 2 * FFN_HIDDEN // N_DEV
FFN_SLABS = FFN_HIDDEN // UP_SHARD
EPS = 1e-6
ADAM_LR = 0.001
ADAM_B1 = 0.9
ADAM_B2 = 0.999
ADAM_EPS = 1e-08
ADAM_WD = 0.01
ADAM_STEP = 10
GELU_C = math.sqrt(2.0 / math.pi)
GELU_A = 0.044715
VMEM_LIMIT = 56 * 1024 * 1024
MM_ROWS = 2048
MM_ROWS_RES = 1024
MM_TOKENS = 2048
W_IN_FIRST_ROWS = 384
MESH = pl.DeviceIdType.MESH

COL_Q, COL_K, COL_V, COL_GR, COL_QX, COL_GL = 512, 1024, 1536, 2560, 3584, 4096

_DIMS = {
    "nn": (((1,), (0,)), ((), ())),
    "nt": (((1,), (1,)), ((), ())),
    "tn": (((0,), (0,)), ((), ())),
}


def _dot(a, b, kind="nn"):
    return lax.dot_general(a.astype(BF16), b.astype(BF16), _DIMS[kind], preferred_element_type=F32)


def _params(sem, vmem=VMEM_LIMIT):
    return pltpu.CompilerParams(dimension_semantics=sem, vmem_limit_bytes=vmem)


def _tile(n, pref):
    t = min(n, pref)
    while n % t:
        t //= 2
    return t


def _mesh_pos():
    return lax.axis_index("x"), lax.axis_index("y"), lax.axis_index("c")


def _dev_index(x, y, c):
    return 4 * x + 2 * y + c


def _my_index():
    return _dev_index(*_mesh_pos())


def _remote(src, dst, send_sems, recv_sems, s, to):
    return pltpu.make_async_remote_copy(src_ref=src, dst_ref=dst, send_sem=send_sems.at[s], recv_sem=recv_sems.at[s],
                                        device_id=to, device_id_type=MESH)


class _Gather:
    def __init__(self, shards):
        self.inputs = list(shards)
        self.out_shapes = [jax.ShapeDtypeStruct((N_DEV,) + s.shape, s.dtype) for s in shards]
        n = len(shards)
        self.sem_shapes = [pltpu.SemaphoreType.DMA((7 * n,)), pltpu.SemaphoreType.DMA((7 * n,)),
                           pltpu.SemaphoreType.DMA((n,))]

    def _places(self):
        x, y, c = _mesh_pos()
        return (x, y, c), (x, y, 1 - c), [(1 - x, y), (x, 1 - y), (1 - x, 1 - y)]

    def _local(self, src, dst, sems):
        me = _my_index()
        return [pltpu.make_async_copy(src[w], dst[w].at[me], sems[2].at[w]) for w in range(len(src))]

    def start(self, src, dst, sems):
        me, sib, chips = self._places()
        for cp in self._local(src, dst, sems):
            cp.start()
        for w in range(len(src)):
            land = dst[w].at[_dev_index(*me)]
            _remote(src[w], land, sems[0], sems[1], 7 * w, sib).start()
            for j, chip in enumerate(chips):
                _remote(src[w], land, sems[0], sems[1], 7 * w + 1 + j, (*chip, me[2])).start()

    def middle(self, src, dst, sems):
        me, sib, chips = self._places()
        for j, chip in enumerate(chips):
            for w in range(len(src)):
                block = dst[w].at[_dev_index(*chip, me[2])]
                _remote(src[w], block, sems[0], sems[1], 7 * w + 1 + j, me).wait_recv()
                _remote(block, block, sems[0], sems[1], 7 * w + 4 + j, sib).start()

    def finish(self, src, dst, sems):
        me, sib, chips = self._places()
        n = len(src)
        for w in range(n):
            _remote(src[w], dst[w].at[_dev_index(*sib)], sems[0], sems[1], 7 * w, me).wait_recv()
            for j, chip in enumerate(chips):
                block = dst[w].at[_dev_index(*chip, sib[2])]
                _remote(block, block, sems[0], sems[1], 7 * w + 4 + j, me).wait_recv()
            for k in range(7):
                _remote(src[w], dst[w].at[0], sems[0], sems[1], 7 * w + k, me).wait_send()
        for cp in self._local(src, dst, sems):
            cp.wait()


class _Exchange:
    def __init__(self, partials, whole=()):
        self.n_part = len(partials)
        self.inputs = list(partials) + list(whole)
        self.out_shapes = [jax.ShapeDtypeStruct(p.shape, p.dtype) for p in partials]
        self.out_shapes += [jax.ShapeDtypeStruct((N_DEV,) + a.shape, a.dtype) for a in whole]
        n = len(self.inputs)
        self.sem_shapes = [pltpu.SemaphoreType.DMA((7 * n,)), pltpu.SemaphoreType.DMA((7 * n,)),
                           pltpu.SemaphoreType.DMA((n,))]

    def _peer(self, k):
        x, y, c = _mesh_pos()
        p = (x ^ ((k >> 2) & 1), y ^ ((k >> 1) & 1), c ^ (k & 1))
        return p, _dev_index(*p)

    def _source(self, src, w, slot):
        return src[w].at[slot] if w < self.n_part else src[w]

    def _local(self, src, dst, sems):
        me = _my_index()
        return [pltpu.make_async_copy(self._source(src, w, me), dst[w].at[me], sems[2].at[w])
                for w in range(len(src))]

    def start(self, src, dst, sems):
        me = _my_index()
        for cp in self._local(src, dst, sems):
            cp.start()
        for k in range(1, N_DEV):
            peer, peer_idx = self._peer(k)
            for w in range(len(src)):
                _remote(self._source(src, w, peer_idx), dst[w].at[me], sems[0], sems[1], 7 * w + k - 1, peer).start()

    def finish(self, src, dst, sems):
        for k in range(1, N_DEV):
            peer, peer_idx = self._peer(k)
            for w in range(len(src)):
                cp = _remote(self._source(src, w, peer_idx), dst[w].at[peer_idx], sems[0], sems[1], 7 * w + k - 1, peer)
                cp.wait_send()
                cp.wait_recv()
        for cp in self._local(src, dst, sems):
            cp.wait()


class _ExchangeTo:
    def __init__(self, partials, side):
        self.side = side
        self.inputs = list(partials)
        self.out_shapes = [jax.ShapeDtypeStruct(p.shape, p.dtype) for p in partials]
        n = len(partials)
        self.sem_shapes = [pltpu.SemaphoreType.DMA((7 * n,)), pltpu.SemaphoreType.DMA((7 * n,)),
                           pltpu.SemaphoreType.DMA((n,))]

    def _copies(self, src, dst, sems):
        x, y, c = _mesh_pos()
        me = _dev_index(x, y, c)
        receives = c == self.side
        remote = []
        for k in range(1, N_DEV):
            kx, ky, kc = (k >> 2) & 1, (k >> 1) & 1, k & 1
            peer = (x ^ kx, y ^ ky, c ^ kc)
            peer_idx = _dev_index(*peer)
            sends = c == (self.side ^ kc)
            for w in range(len(src)):
                slab = src[w].at[peer_idx]
                s = 7 * w + k - 1
                remote.append((sends, _remote(slab, dst[w].at[me], sems[0], sems[1], s, peer),
                               _remote(slab, dst[w].at[peer_idx], sems[0], sems[1], s, peer)))
        local = [pltpu.make_async_copy(src[w].at[me], dst[w].at[me], sems[2].at[w]) for w in range(len(src))]
        return receives, remote, local

    def start(self, src, dst, sems):
        receives, remote, local = self._copies(src, dst, sems)

        @pl.when(receives)
        def _():
            for cp in local:
                cp.start()

        for sends, send, _ in remote:
            pl.when(sends)(send.start)

    def finish(self, src, dst, sems):
        receives, remote, local = self._copies(src, dst, sems)
        for sends, send, arrive in remote:
            pl.when(sends)(send.wait_send)
            pl.when(receives)(arrive.wait_recv)

        @pl.when(receives)
        def _():
            for cp in local:
                cp.wait()


def _pcall(body, args, *, name, out_shape, grid, in_specs, out_specs, scratch_shapes=(), sem=None, comm=None):
    single = not isinstance(out_shape, (tuple, list))
    outs = [out_shape] if single else list(out_shape)
    ospecs = [out_specs] if single else list(out_specs)
    n_in, n_out, n_scr = len(args), len(outs), len(scratch_shapes)

    def pick(res):
        return res[0] if single else tuple(res[:n_out])

    if comm is None:
        res = pl.pallas_call(
            body, out_shape=outs, grid=grid, in_specs=list(in_specs), out_specs=ospecs,
            scratch_shapes=list(scratch_shapes), name=name, compiler_params=_params(sem),
        )(*args)
        return pick(res), ()

    nci, nco = len(comm.inputs), len(comm.out_shapes)

    def carrier(*refs):
        at = 0
        parts = []
        for size in (n_in, nci, n_out, nco, n_scr, len(comm.sem_shapes)):
            parts.append(refs[at:at + size])
            at += size
        ins, cins, o, couts, scr, sems = parts
        ids = [pl.program_id(a) for a in range(len(grid))]
        first = functools.reduce(jnp.logical_and, [i == 0 for i in ids])
        last = functools.reduce(jnp.logical_and, [i == g - 1 for i, g in zip(ids, grid)])

        body(*ins, *o, *scr)

        @pl.when(first)
        def _():
            comm.start(cins, couts, sems)

        if hasattr(comm, "middle"):
            steps = math.prod(grid)
            at = functools.reduce(lambda lin, ig: lin * ig[1] + ig[0], zip(ids, grid), 0)

            @pl.when(at == min(steps - 1, (3 * steps) // 4))
            def _():
                comm.middle(cins, couts, sems)

        @pl.when(last)
        def _():
            comm.finish(cins, couts, sems)

    hbm = pl.BlockSpec(memory_space=pltpu.HBM)
    res = pl.pallas_call(
        carrier, out_shape=outs + comm.out_shapes, grid=grid, in_specs=list(in_specs) + [hbm] * nci,
        out_specs=ospecs + [hbm] * nco, scratch_shapes=list(scratch_shapes) + comm.sem_shapes, name=name,
        compiler_params=_params(("arbitrary",) * len(grid)),
    )(*args, *comm.inputs)
    return pick(res), tuple(res[n_out:])


def _comm_call(name, comm):
    def body(*refs):
        nci, nco = len(comm.inputs), len(comm.out_shapes)
        cins, couts, sems = refs[:nci], refs[nci:nci + nco], refs[nci + nco:]
        comm.start(cins, couts, sems)
        if hasattr(comm, "middle"):
            comm.middle(cins, couts, sems)
        comm.finish(cins, couts, sems)

    hbm = pl.BlockSpec(memory_space=pltpu.HBM)
    return pl.pallas_call(
        body, out_shape=comm.out_shapes, in_specs=[hbm] * len(comm.inputs), out_specs=[hbm] * len(comm.out_shapes),
        scratch_shapes=comm.sem_shapes, name=name,
    )(*comm.inputs)


def _matmul(name, kind, a, b, out_shape, grid, a_spec, b_spec, o_spec, acc_shape, res=None, res_spec=None,
            comm=None, epilogue=None):
    nk = grid[-1]
    if epilogue is None:
        extra, extra_specs = ([res], [res_spec]) if res is not None else ([], [])
        n_out = 1
    else:
        extra, extra_specs, n_out = list(res), list(res_spec), len(out_shape)
    n_in = 2 + len(extra)

    def body(*refs):
        a_ref, b_ref = refs[0], refs[1]
        extra_refs, out_refs = refs[2:n_in], refs[n_in:n_in + n_out]

        def prod():
            return _dot(a_ref[...], b_ref[...], kind)

        def finish(acc):
            if epilogue is not None:
                ids = [pl.program_id(ax) for ax in range(len(grid) - 1)]
                first = functools.reduce(jnp.logical_and, [i == 0 for i in ids]) if ids else True
                epilogue(acc, extra_refs, out_refs, first)
                return
            if extra_refs:
                acc = acc + extra_refs[0][...]
            out_refs[0][...] = acc.astype(out_refs[0].dtype)

        if nk == 1:
            finish(prod())
        else:
            acc_ref = refs[n_in + n_out]
            k = pl.program_id(len(grid) - 1)

            @pl.when(k == 0)
            def _():
                acc_ref[...] = prod()

            @pl.when(k > 0)
            def _():
                acc_ref[...] += prod()

            @pl.when(k == nk - 1)
            def _():
                finish(acc_ref[...])

    in_specs = [a_spec, b_spec] + extra_specs
    args = (a, b, *extra)
    scratch = [pltpu.VMEM(acc_shape, F32)] if nk > 1 else []
    sem = ("arbitrary",) * len(grid) if epilogue is not None else ("parallel",) * (len(grid) - 1) + ("arbitrary",)
    out, landed = _pcall(body, args, name=name, out_shape=out_shape, grid=grid, in_specs=in_specs,
                         out_specs=o_spec, scratch_shapes=scratch, sem=sem, comm=comm)
    return out if comm is None else (out, landed)


def _mm_rows(name, a, w, out_dtype=F32, res=None, kind="nn", tm=MM_ROWS, comm=None):
    M, K = a.shape
    N = w.shape[1] if kind == "nn" else w.shape[0]
    tm = _tile(M, tm)
    res_spec = pl.BlockSpec((tm, N), lambda i, k: (i, 0)) if res is not None else None
    return _matmul(
        name, kind, a, w, jax.ShapeDtypeStruct((M, N), out_dtype), (M // tm, 1),
        pl.BlockSpec((tm, K), lambda i, k: (i, 0)), pl.BlockSpec(w.shape, lambda i, k: (0, 0)),
        pl.BlockSpec((tm, N), lambda i, k: (i, 0)), (tm, N), res, res_spec, comm)


def _residual_rms_epilogue(y, operands, outputs, first):
    x_ref, g_ref = operands
    x1_ref, h_ref = outputs
    xv = x_ref[...] + y
    x1_ref[...] = xv
    r = lax.rsqrt(jnp.mean(xv * xv, axis=-1, keepdims=True) + EPS)
    h_ref[...] = (xv * r * g_ref[...]).astype(h_ref.dtype)


def _mm_residual_rms(name, a, w, x, g, tm=MM_ROWS_RES):
    M, K = a.shape
    N = w.shape[1]
    tm = _tile(M, tm)
    row = pl.BlockSpec((tm, N), lambda i, k: (i, 0))
    return _matmul(
        name, "nn", a, w, (jax.ShapeDtypeStruct((M, N), F32), jax.ShapeDtypeStruct((M, N), BF16)), (M // tm, 1),
        pl.BlockSpec((tm, K), lambda i, k: (i, 0)), pl.BlockSpec(w.shape, lambda i, k: (0, 0)),
        (row, row), (tm, N), [x, g], [row, pl.BlockSpec((1, N), lambda i, k: (0, 0))],
        epilogue=_residual_rms_epilogue)


def _mm_tn(name, a, b, out_dtype=F32, tk=MM_TOKENS, comm=None):
    T, M = a.shape
    N = b.shape[1]
    tk = _tile(T, tk)
    return _matmul(
        name, "tn", a, b, jax.ShapeDtypeStruct((M, N), out_dtype), (1, T // tk),
        pl.BlockSpec((tk, M), lambda i, k: (k, 0)), pl.BlockSpec((tk, N), lambda i, k: (k, 0)),
        pl.BlockSpec((M, N), lambda i, k: (0, 0)), (M, N), comm=comm)


def _mm_in_gather(h, shard, tm=MM_ROWS, comm=None):
    T, K = h.shape
    n = shard.shape[1]
    tm = _tile(T, tm)
    n_tiles = T // tm
    pair_of_chip_step = {4: 1, 2: 2, 6: 3}

    def slab_of(s):
        x, y, c = _mesh_pos()
        return _dev_index(x ^ ((s >> 2) & 1), y ^ ((s >> 1) & 1), c ^ (s & 1))

    def body(h_ref, shard_ref, proj_ref, win_ref, wbuf, slot_sems, send_sems, recv_sems, local_sem):
        s, i = pl.program_id(0), pl.program_id(1)
        x, y, c = _mesh_pos()
        me, sib = (x, y, c), (x, y, 1 - c)

        def slot_copy(step):
            src = shard_ref if step == 0 else win_ref.at[slab_of(step)]
            return pltpu.make_async_copy(src, wbuf.at[step % 2], slot_sems.at[step % 2])

        def fetch(step):
            if step >= 1:
                block = win_ref.at[slab_of(step)]
                if step == 1:
                    pair = 0
                elif step % 2 == 0:
                    pair = pair_of_chip_step[step]
                else:
                    pair = 3 + pair_of_chip_step[step - 1]
                _remote(block, block, send_sems, recv_sems, pair, me).wait_recv()
                if step % 2 == 0:
                    _remote(block, block, send_sems, recv_sems, 3 + pair, sib).start()
            slot_copy(step).start()

        @pl.when((s == 0) & (i == 0))
        def _():
            land = win_ref.at[_dev_index(*me)]
            pltpu.make_async_copy(shard_ref, land, local_sem).start()
            _remote(shard_ref, land, send_sems, recv_sems, 0, sib).start()
            for step, pair in pair_of_chip_step.items():
                peer = (x ^ ((step >> 2) & 1), y ^ ((step >> 1) & 1), c)
                _remote(shard_ref, land, send_sems, recv_sems, pair, peer).start()
            fetch(0)

        for step in range(N_DEV):
            @pl.when((s == step) & (i == 0))
            def _():
                slot_copy(step).wait()

            if step + 1 < N_DEV:
                @pl.when((s == step) & (i == n_tiles - 1))
                def _():
                    fetch(step + 1)

        proj_ref[...] = _dot(h_ref[...], wbuf[s % 2])

        @pl.when((s == N_DEV - 1) & (i == n_tiles - 1))
        def _():
            for pair in range(7):
                _remote(shard_ref, win_ref.at[0], send_sems, recv_sems, pair, me).wait_send()
            pltpu.make_async_copy(shard_ref, win_ref.at[_dev_index(*me)], local_sem).wait()

    hbm = pl.BlockSpec(memory_space=pltpu.HBM)
    return _pcall(
        body, (h, shard), name="mm_in",
        out_shape=(jax.ShapeDtypeStruct((T, N_DEV * n), F32), jax.ShapeDtypeStruct((N_DEV, K, n), shard.dtype)),
        grid=(N_DEV, n_tiles), in_specs=[pl.BlockSpec((tm, K), lambda s, i: (i, 0)), hbm],
        out_specs=(pl.BlockSpec((tm, n), lambda s, i: (i, slab_of(s))), hbm),
        scratch_shapes=[pltpu.VMEM((2, K, n), shard.dtype), pltpu.SemaphoreType.DMA((2,)),
                        pltpu.SemaphoreType.DMA((7,)), pltpu.SemaphoreType.DMA((7,)), pltpu.SemaphoreType.DMA],
        sem=("arbitrary", "arbitrary"), comm=comm)


def _rms_bwd_epilogue(dh, operands, outputs, first):
    x_ref, g_ref, dres_ref = operands
    dx_ref, dg_ref = outputs
    xv = x_ref[...]
    r = lax.rsqrt(jnp.mean(xv * xv, axis=-1, keepdims=True) + EPS)
    xhat = xv * r

    @pl.when(first)
    def _():
        dg_ref[...] = jnp.zeros_like(dg_ref)

    dg_ref[...] += jnp.sum(dh * xhat, axis=0, keepdims=True)
    dxhat = dh * g_ref[...]
    dx_ref[...] = dres_ref[...] + r * (dxhat - xhat * jnp.mean(dxhat * xhat, axis=-1, keepdims=True))


def _rms_bwd_fused(M, K, tm, rms):
    row = pl.BlockSpec((tm, K), lambda i, j: (i, 0))
    vec = pl.BlockSpec((1, K), lambda i, j: (0, 0))
    x, g, dres = rms
    return dict(res=[x, g, dres], res_spec=[row, vec, row], epilogue=_rms_bwd_epilogue,
                out_shape=(jax.ShapeDtypeStruct((M, K), F32), jax.ShapeDtypeStruct((1, K), F32)), o_spec=(row, vec))


def _mm_cols_slab_t(name, a, w_slabs, rms, tm=MM_ROWS_RES, comm=None):
    M = a.shape[0]
    J, K, n = w_slabs.shape
    tm = _tile(M, tm)
    fused = _rms_bwd_fused(M, K, tm, rms)
    return _matmul(
        name, "nt", a, w_slabs, fused.pop("out_shape"), (M // tm, J),
        pl.BlockSpec((tm, n), lambda i, j: (i, j)), pl.BlockSpec((None, K, n), lambda i, j: (j, 0, 0)),
        fused.pop("o_spec"), (tm, K), comm=comm, **fused)


def _mm_tn_slab(name, a, b, n, out_dtype=F32, tk=MM_TOKENS, comm=None, part=(0, 1)):
    T, M = a.shape
    p, of = part
    M //= of
    J = b.shape[1] // n
    tk = _tile(T, tk)
    return _matmul(
        name, "tn", a, b, jax.ShapeDtypeStruct((J, M, n), out_dtype), (J, T // tk),
        pl.BlockSpec((tk, M), lambda j, k: (k, p)), pl.BlockSpec((tk, n), lambda j, k: (k, j)),
        pl.BlockSpec((None, M, n), lambda j, k: (j, 0, 0)), (M, n), comm=comm)


def _rms_fwd(name, x, g, tm=512):
    T, Dm = x.shape
    tm = _tile(T, tm)

    def body(x_ref, g_ref, h_ref):
        xv = x_ref[...]
        r = lax.rsqrt(jnp.mean(xv * xv, axis=-1, keepdims=True) + EPS)
        h_ref[...] = (xv * r * g_ref[...]).astype(h_ref.dtype)

    return pl.pallas_call(
        body, out_shape=jax.ShapeDtypeStruct((T, Dm), BF16), grid=(T // tm,),
        in_specs=[pl.BlockSpec((tm, Dm), lambda i: (i, 0)), pl.BlockSpec((1, Dm), lambda i: (0, 0))],
        out_specs=pl.BlockSpec((tm, Dm), lambda i: (i, 0)), name=name, compiler_params=_params(("parallel",)),
    )(x, g)


def _rms_bwd(name, x, g, dh, dres, tm=512):
    T, Dm = x.shape
    tm = _tile(T, tm)
    want_dx = dres is not None

    def body(*refs):
        if want_dx:
            x_ref, g_ref, dh_ref, dres_ref, dx_ref, dg_ref = refs
        else:
            x_ref, g_ref, dh_ref, dg_ref = refs
        xv = x_ref[...]
        r = lax.rsqrt(jnp.mean(xv * xv, axis=-1, keepdims=True) + EPS)
        xhat = xv * r
        dhv = dh_ref[...]

        @pl.when(pl.program_id(0) == 0)
        def _():
            dg_ref[...] = jnp.zeros_like(dg_ref)

        dg_ref[...] += jnp.sum(dhv * xhat, axis=0, keepdims=True)
        if want_dx:
            dxhat = dhv * g_ref[...]
            dx_ref[...] = dres_ref[...] + r * (dxhat - xhat * jnp.mean(dxhat * xhat, axis=-1, keepdims=True))

    row = pl.BlockSpec((tm, Dm), lambda i: (i, 0))
    vec = pl.BlockSpec((1, Dm), lambda i: (0, 0))
    if want_dx:
        return pl.pallas_call(
            body, out_shape=(jax.ShapeDtypeStruct((T, Dm), F32), jax.ShapeDtypeStruct((1, Dm), F32)),
            grid=(T // tm,), in_specs=[row, vec, row, row], out_specs=(row, vec), name=name,
            compiler_params=_params(("arbitrary",)),
        )(x, g, dh, dres)
    return pl.pallas_call(
        body, out_shape=jax.ShapeDtypeStruct((1, Dm), F32), grid=(T // tm,), in_specs=[row, vec, row],
        out_specs=vec, name=name, compiler_params=_params(("arbitrary",)),
    )(x, g, dh)


def _pool_rows(S):
    return _tile(S, 256)


def _pool_count(c0, rows, w):
    t = c0 + lax.broadcasted_iota(jnp.int32, (rows, 1), 0)
    return jnp.minimum(t + 1, w).astype(F32)


def _pool_fwd(proj, w_pool, scale, B, S):
    CH = _pool_rows(S)

    def body(hp_ref, wp_ref, sc_ref, o_ref, pad_ref):
        pad_ref[0:POOL_HALO, :] = jnp.zeros((POOL_HALO, POOL_WIDTH), F32)
        pad_ref[POOL_HALO:, :] = hp_ref[...]
        for gi, w in enumerate(POOL_WINDOWS):
            cols = slice(gi * POOL_GROUP_DIM, (gi + 1) * POOL_GROUP_DIM)
            for c in range(S // CH):
                base = POOL_HALO + c * CH
                acc = pad_ref[base:base + CH, cols]
                tok = acc
                for j in range(1, w):
                    acc = acc + pad_ref[base - j:base - j + CH, cols]
                pooled = acc / _pool_count(c * CH, CH, w) - tok
                z = _dot(pooled, wp_ref[gi])
                o_ref[c * CH:(c + 1) * CH, cols] = (z * sc_ref[:, cols]).astype(o_ref.dtype)

    return pl.pallas_call(
        body, out_shape=jax.ShapeDtypeStruct((B * S, POOL_WIDTH), BF16), grid=(B,),
        in_specs=[pl.BlockSpec((S, POOL_WIDTH), lambda b: (b, 0)),
                  pl.BlockSpec(w_pool.shape, lambda b: (0, 0, 0)),
                  pl.BlockSpec((1, POOL_WIDTH), lambda b: (0, 0))],
        out_specs=pl.BlockSpec((S, POOL_WIDTH), lambda b: (b, 0)),
        scratch_shapes=[pltpu.VMEM((S + POOL_HALO, POOL_WIDTH), F32)],
        name="pool_fwd", compiler_params=_params(("parallel",)),
    )(proj, w_pool, scale)


def _pool_bwd(proj, d_ypre, w_pool, scale, B, S):
    CH = _pool_rows(S)

    def body(hp_ref, dy_ref, wp_ref, sc_ref, dhp_ref, dwp_ref, dsc_ref, pad_ref, sc_pad_ref, dp_ref):
        @pl.when(pl.program_id(0) == 0)
        def _():
            dwp_ref[...] = jnp.zeros_like(dwp_ref)
            dsc_ref[...] = jnp.zeros_like(dsc_ref)

        pad_ref[0:POOL_HALO, :] = jnp.zeros((POOL_HALO, POOL_WIDTH), F32)
        pad_ref[POOL_HALO:, :] = hp_ref[...]
        sc_pad_ref[S:, :] = jnp.zeros((POOL_HALO, POOL_WIDTH), F32)
        for gi, w in enumerate(POOL_WINDOWS):
            cols = slice(gi * POOL_GROUP_DIM, (gi + 1) * POOL_GROUP_DIM)
            for c in range(S // CH):
                base = POOL_HALO + c * CH
                rows = slice(c * CH, (c + 1) * CH)
                acc = pad_ref[base:base + CH, cols]
                tok = acc
                for j in range(1, w):
                    acc = acc + pad_ref[base - j:base - j + CH, cols]
                cnt = _pool_count(c * CH, CH, w)
                pooled = acc / cnt - tok
                z = _dot(pooled, wp_ref[gi])
                dy = dy_ref[rows, cols]
                dsc_ref[:, cols] += jnp.sum(dy * z, axis=0, keepdims=True)
                dz = dy * sc_ref[:, cols]
                dwp_ref[gi] += _dot(pooled, dz, "tn")
                dpool = _dot(dz, wp_ref[gi], "nt")
                dp_ref[rows, cols] = dpool
                sc_pad_ref[rows, cols] = dpool / cnt
            for c in range(S // CH):
                rows = slice(c * CH, (c + 1) * CH)
                acc = sc_pad_ref[rows, cols]
                for j in range(1, w):
                    acc = acc + sc_pad_ref[c * CH + j:c * CH + j + CH, cols]
                dhp_ref[rows, cols] = (acc - dp_ref[rows, cols]).astype(dhp_ref.dtype)

    seq = pl.BlockSpec((S, POOL_WIDTH), lambda b: (b, 0))
    return pl.pallas_call(
        body,
        out_shape=(jax.ShapeDtypeStruct((B * S, POOL_WIDTH), BF16),
                   jax.ShapeDtypeStruct(w_pool.shape, F32), jax.ShapeDtypeStruct((1, POOL_WIDTH), F32)),
        grid=(B,),
        in_specs=[seq, seq, pl.BlockSpec(w_pool.shape, lambda b: (0, 0, 0)),
                  pl.BlockSpec((1, POOL_WIDTH), lambda b: (0, 0))],
        out_specs=(seq, pl.BlockSpec(w_pool.shape, lambda b: (0, 0, 0)),
                   pl.BlockSpec((1, POOL_WIDTH), lambda b: (0, 0))),
        scratch_shapes=[pltpu.VMEM((S + POOL_HALO, POOL_WIDTH), F32),
                        pltpu.VMEM((S + POOL_HALO, POOL_WIDTH), F32),
                        pltpu.VMEM((S, POOL_WIDTH), F32)],
        name="pool_bwd", compiler_params=_params(("arbitrary",)),
    )(proj, d_ypre, w_pool, scale)


def _ret_tables(S):
    half = RET_QK_DIM // 2
    inv = ROPE_BASE ** (-jnp.arange(half, dtype=F32) / half)
    ang = jnp.arange(S, dtype=F32)[:, None] * inv[None, :]
    cos, sin = jnp.cos(ang), jnp.sin(ang)
    cos_full = jnp.concatenate([cos, cos], axis=-1)
    sin_signed = jnp.concatenate([-sin, sin], axis=-1)
    C = RET_CHUNK
    lg = jnp.log1p(-jnp.exp2(-5.0 - jnp.arange(RET_HEADS, dtype=F32)))[:, None, None]
    idx = jnp.arange(C, dtype=F32)
    rel = idx[:, None] - idx[None, :]
    decay = jnp.where(rel >= 0, jnp.exp(jnp.maximum(rel, 0.0) * lg), 0.0)
    q_decay = jnp.broadcast_to(jnp.exp((idx + 1.0)[None, :, None] * lg), (RET_HEADS, C, RET_QK_DIM))
    k_decay = jnp.broadcast_to(jnp.exp((C - 1.0 - idx)[None, :, None] * lg), (RET_HEADS, C, RET_QK_DIM))
    c_decay = jnp.broadcast_to(jnp.exp(C * lg), (RET_HEADS, 1, RET_V_DIM))
    return cos_full, sin_signed, decay, q_decay, k_decay, c_decay


def _rope(x, cos_full, sin_signed):
    return x * cos_full + pltpu.roll(x, RET_QK_DIM // 2, axis=1) * sin_signed


def _rope_t(dy, cos_full, sin_signed):
    return dy * cos_full + pltpu.roll(dy * sin_signed, RET_QK_DIM // 2, axis=1)


RET_COLS = 512


def _ret_specs(N, chunk_of):
    C = RET_CHUNK

    def rows(width, col=0):
        return pl.BlockSpec((C, width), lambda b, i: (b * N + chunk_of(i), col))

    def whole(shape):
        return pl.BlockSpec(shape, lambda b, i: (0,) * len(shape))

    wide = RET_HEADS * RET_V_DIM
    return dict(
        q=rows(RET_COLS, COL_Q // RET_COLS), k=rows(RET_COLS, COL_K // RET_COLS),
        v=[rows(RET_COLS, COL_V // RET_COLS + j) for j in range(2)],
        gr=[rows(RET_COLS, COL_GR // RET_COLS + j) for j in range(2)],
        table=pl.BlockSpec((C, RET_QK_DIM), lambda b, i: (chunk_of(i), 0)),
        decay=whole((RET_HEADS, C, C)), qd=whole((RET_HEADS, C, RET_QK_DIM)), kd=whole((RET_HEADS, C, RET_QK_DIM)),
        cd=whole((RET_HEADS, 1, RET_V_DIM)), vec=whole((1, wide)), qk_rows=rows(RET_COLS), v_rows=rows(wide),
        state=pl.BlockSpec((None, None, RET_HEADS, RET_QK_DIM, RET_V_DIM), lambda b, i: (b, chunk_of(i), 0, 0, 0)))


def _head_cols(h):
    pair = slice((h % 2) * RET_V_DIM, (h % 2 + 1) * RET_V_DIM)
    return slice(h * RET_QK_DIM, (h + 1) * RET_QK_DIM), h // 2, pair, slice(h * RET_V_DIM, (h + 1) * RET_V_DIM)


def _group_norm(o):
    mu = jnp.mean(o, axis=-1, keepdims=True)
    oc = o - mu
    rstd = lax.rsqrt(jnp.mean(oc * oc, axis=-1, keepdims=True) + EPS)
    return oc * rstd, rstd


def _ret_fwd(proj, g_ret, b_ret, tables, B, S, comm=None):
    N = S // RET_CHUNK
    cos_t, sin_t, decay, q_decay, k_decay, c_decay = tables
    sp = _ret_specs(N, lambda i: i)

    def body(q_ref, k_ref, v0_ref, v1_ref, gr0_ref, gr1_ref, cos_ref, sin_ref, dec_ref, qd_ref, kd_ref, cd_ref,
             g_ref, b_ref, y_ref, rs_ref, r_ref):
        @pl.when(pl.program_id(1) == 0)
        def _():
            r_ref[...] = jnp.zeros_like(r_ref)

        cs, sn = cos_ref[...], sin_ref[...]
        for h in range(RET_HEADS):
            qk, j, pair, wide = _head_cols(h)
            q = _rope(q_ref[:, qk], cs, sn)
            k = _rope(k_ref[:, qk], cs, sn) * (RET_QK_DIM ** -0.5)
            v = (v0_ref, v1_ref)[j][:, pair]
            R = r_ref[h]
            rs_ref[h] = R
            s = _dot(q, k, "nt") * dec_ref[h]
            o = _dot(s, v) + _dot(q * qd_ref[h], R)
            r_ref[h] = cd_ref[h] * R + _dot(k * kd_ref[h], v, "tn")
            on, _ = _group_norm(o)
            gr = (gr0_ref, gr1_ref)[j][:, pair]
            y_ref[:, wide] = (gr * jax.nn.sigmoid(gr) * (on * g_ref[:, wide] + b_ref[:, wide])).astype(y_ref.dtype)

    state = jax.ShapeDtypeStruct((B, N, RET_HEADS, RET_QK_DIM, RET_V_DIM), F32)
    return _pcall(
        body, (proj,) * 6 + (cos_t, sin_t, decay, q_decay, k_decay, c_decay, g_ret, b_ret),
        name="ret_fwd", out_shape=(jax.ShapeDtypeStruct((B * S, RET_HEADS * RET_V_DIM), BF16), state), grid=(B, N),
        in_specs=[sp["q"], sp["k"], *sp["v"], *sp["gr"], sp["table"], sp["table"], sp["decay"], sp["qd"],
                  sp["kd"], sp["cd"], sp["vec"], sp["vec"]],
        out_specs=(sp["v_rows"], sp["state"]),
        scratch_shapes=[pltpu.VMEM((RET_HEADS, RET_QK_DIM, RET_V_DIM), F32)],
        sem=("parallel", "arbitrary"), comm=comm)


def _ret_bwd(proj, states, d_yr, g_ret, b_ret, tables, B, S, comm=None):
    N = S // RET_CHUNK
    cos_t, sin_t, decay, q_decay, k_decay, c_decay = tables
    sp = _ret_specs(N, lambda i: N - 1 - i)
    qk_scale = RET_QK_DIM ** -0.5

    def body(q_ref, k_ref, v0_ref, v1_ref, gr0_ref, gr1_ref, dy_ref, rs_ref, cos_ref, sin_ref, dec_ref, qd_ref,
             kd_ref, cd_ref, g_ref, b_ref, dq_ref, dk_ref, dv_ref, dgr_ref, dg_ref, db_ref, dr_ref):
        @pl.when((pl.program_id(0) == 0) & (pl.program_id(1) == 0))
        def _():
            dg_ref[...] = jnp.zeros_like(dg_ref)
            db_ref[...] = jnp.zeros_like(db_ref)

        @pl.when(pl.program_id(1) == 0)
        def _():
            dr_ref[...] = jnp.zeros_like(dr_ref)

        cs, sn = cos_ref[...], sin_ref[...]
        for h in range(RET_HEADS):
            qk, j, pair, wide = _head_cols(h)
            q = _rope(q_ref[:, qk], cs, sn)
            k = _rope(k_ref[:, qk], cs, sn) * qk_scale
            v = (v0_ref, v1_ref)[j][:, pair]
            R, dR = rs_ref[h], dr_ref[h]
            dec, qd, kd = dec_ref[h], qd_ref[h], kd_ref[h]
            s = _dot(q, k, "nt") * dec
            o = _dot(s, v) + _dot(q * qd, R)
            on, rstd = _group_norm(o)
            g = g_ref[:, wide]
            oaff = on * g + b_ref[:, wide]
            gr = (gr0_ref, gr1_ref)[j][:, pair]
            sg = jax.nn.sigmoid(gr)
            dy = dy_ref[:, wide]
            dgr_ref[:, wide] = (dy * oaff * (sg * (1.0 + gr * (1.0 - sg)))).astype(dgr_ref.dtype)
            doaff = dy * (gr * sg)
            dg_ref[:, wide] += jnp.sum(doaff * on, axis=0, keepdims=True)
            db_ref[:, wide] += jnp.sum(doaff, axis=0, keepdims=True)
            don = doaff * g
            do = rstd * (don - jnp.mean(don, axis=-1, keepdims=True)
                         - on * jnp.mean(don * on, axis=-1, keepdims=True))
            ds = _dot(do, v, "nt") * dec
            dq = _dot(ds, k) + qd * _dot(do, R, "nt")
            dk = _dot(ds, q, "tn") + kd * _dot(v, dR, "nt")
            dv_ref[:, wide] = (_dot(s, do, "tn") + _dot(k * kd, dR)).astype(dv_ref.dtype)
            dr_ref[h] = cd_ref[h] * dR + _dot(q * qd, do, "tn")
            dq_ref[:, qk] = _rope_t(dq, cs, sn).astype(dq_ref.dtype)
            dk_ref[:, qk] = _rope_t(dk * qk_scale, cs, sn).astype(dk_ref.dtype)

    T = B * S
    qk_shape = jax.ShapeDtypeStruct((T, RET_HEADS * RET_QK_DIM), BF16)
    v_shape = jax.ShapeDtypeStruct((T, RET_HEADS * RET_V_DIM), BF16)
    vec_shape = jax.ShapeDtypeStruct((1, RET_HEADS * RET_V_DIM), F32)
    return _pcall(
        body, (proj,) * 6 + (d_yr, states, cos_t, sin_t, decay, q_decay, k_decay, c_decay, g_ret, b_ret),
        name="ret_bwd", out_shape=(qk_shape, qk_shape, v_shape, v_shape, vec_shape, vec_shape), grid=(B, N),
        in_specs=[sp["q"], sp["k"], *sp["v"], *sp["gr"], sp["v_rows"], sp["state"], sp["table"], sp["table"],
                  sp["decay"], sp["qd"], sp["kd"], sp["cd"], sp["vec"], sp["vec"]],
        out_specs=(sp["qk_rows"], sp["qk_rows"], sp["v_rows"], sp["v_rows"], sp["vec"], sp["vec"]),
        scratch_shapes=[pltpu.VMEM((RET_HEADS, RET_QK_DIM, RET_V_DIM), F32)],
        sem=("arbitrary", "arbitrary"), comm=comm)


def _xa_rows(S):
    return _tile(S, 256)


def _xa_specs(S, M):
    q = pl.BlockSpec((S, XA_HEAD_DIM), lambda b, h: (b, COL_QX // XA_HEAD_DIM + h))
    k = pl.BlockSpec((M, XA_HEAD_DIM), lambda b, h: (b, h))
    v = pl.BlockSpec((M, XA_HEAD_DIM), lambda b, h: (b, XA_HEADS + h))
    o = pl.BlockSpec((S, XA_HEAD_DIM), lambda b, h: (b, h))
    return q, k, v, o


def _softmax_rows(s):
    e = jnp.exp(s - jnp.max(s, axis=-1, keepdims=True))
    return e / jnp.sum(e, axis=-1, keepdims=True)


def _xa_fwd(proj, kv, B, S, M, comm=None):
    CH = _xa_rows(S)
    q_spec, k_spec, v_spec, o_spec = _xa_specs(S, M)

    def body(q_ref, k_ref, v_ref, o_ref):
        def chunk(i, carry):
            rows = pl.ds(pl.multiple_of(i * CH, CH), CH)
            p = _softmax_rows(_dot(q_ref[rows, :], k_ref[...], "nt") * (XA_HEAD_DIM ** -0.5))
            o_ref[rows, :] = _dot(p, v_ref[...]).astype(o_ref.dtype)
            return carry

        lax.fori_loop(0, S // CH, chunk, 0)

    return _pcall(
        body, (proj, kv, kv), name="xattn_fwd", out_shape=jax.ShapeDtypeStruct((B * S, XA_WIDTH), BF16),
        grid=(B, XA_HEADS), in_specs=[q_spec, k_spec, v_spec], out_specs=o_spec,
        sem=("parallel", "parallel"), comm=comm)


def _xa_bwd(proj, kv, d_o, B, S, M, comm=None):
    CH = _xa_rows(S)
    q_spec, k_spec, v_spec, o_spec = _xa_specs(S, M)
    scale = XA_HEAD_DIM ** -0.5

    def body(q_ref, k_ref, v_ref, do_ref, dq_ref, dk_ref, dv_ref):
        dk_ref[...] = jnp.zeros_like(dk_ref)
        dv_ref[...] = jnp.zeros_like(dv_ref)

        def chunk(i, carry):
            rows = pl.ds(pl.multiple_of(i * CH, CH), CH)
            q, do = q_ref[rows, :], do_ref[rows, :]
            p = _softmax_rows(_dot(q, k_ref[...], "nt") * scale)
            dp = _dot(do, v_ref[...], "nt")
            ds = p * (dp - jnp.sum(dp * p, axis=-1, keepdims=True)) * scale
            dq_ref[rows, :] = _dot(ds, k_ref[...]).astype(dq_ref.dtype)
            dk_ref[...] += _dot(ds, q, "tn")
            dv_ref[...] += _dot(p, do, "tn")
            return carry

        lax.fori_loop(0, S // CH, chunk, 0)

    kv_out = pl.BlockSpec((M, XA_HEAD_DIM), lambda b, h: (b, h))
    return _pcall(
        body, (proj, kv, kv, d_o), name="xattn_bwd",
        out_shape=(jax.ShapeDtypeStruct((B * S, XA_WIDTH), BF16), jax.ShapeDtypeStruct((B * M, XA_WIDTH), F32),
                   jax.ShapeDtypeStruct((B * M, XA_WIDTH), F32)),
        grid=(B, XA_HEADS), in_specs=[q_spec, k_spec, v_spec, o_spec], out_specs=(o_spec, kv_out, kv_out),
        sem=("parallel", "parallel"), comm=comm)


def _gate_specs(tm):
    n = COL_GL // D_MODEL
    return [pl.BlockSpec((tm, D_MODEL), lambda i, j=j: (i, n + j)) for j in range(3)]


def _merge_fwd(proj, ys, tm=256):
    T = proj.shape[0]
    tm = _tile(T, tm)
    row = pl.BlockSpec((tm, D_MODEL), lambda i: (i, 0))

    def body(g0, g1, g2, y0, y1, y2, o_ref):
        acc = jax.nn.sigmoid(g0[...]) * y0[...]
        acc = acc + jax.nn.sigmoid(g1[...]) * y1[...]
        acc = acc + jax.nn.sigmoid(g2[...]) * y2[...]
        o_ref[...] = acc.astype(o_ref.dtype)

    return pl.pallas_call(
        body, out_shape=jax.ShapeDtypeStruct((T, D_MODEL), BF16), grid=(T // tm,),
        in_specs=_gate_specs(tm) + [row] * 3, out_specs=row, name="merge_fwd",
        compiler_params=_params(("parallel",)),
    )(proj, proj, proj, *ys)


def _merge_bwd(proj, ys, d_merged, tm=256, comm=None):
    T = proj.shape[0]
    tm = _tile(T, tm)
    row = pl.BlockSpec((tm, D_MODEL), lambda i: (i, 0))

    def body(g0, g1, g2, y0, y1, y2, dm_ref, dgl_ref, d0, d1, d2):
        dm = dm_ref[...]
        for j, (g_ref, y_ref, d_ref) in enumerate(((g0, y0, d0), (g1, y1, d1), (g2, y2, d2))):
            sg = jax.nn.sigmoid(g_ref[...])
            d_ref[...] = (dm * sg).astype(d_ref.dtype)
            dgl_ref[:, j * D_MODEL:(j + 1) * D_MODEL] = (dm * y_ref[...] * sg * (1.0 - sg)).astype(dgl_ref.dtype)

    dy = jax.ShapeDtypeStruct((T, D_MODEL), BF16)
    return _pcall(
        body, (proj, proj, proj, *ys, d_merged), name="merge_bwd",
        out_shape=(jax.ShapeDtypeStruct((T, 3 * D_MODEL), BF16), dy, dy, dy), grid=(T // tm,),
        in_specs=_gate_specs(tm) + [row] * 4,
        out_specs=(pl.BlockSpec((tm, 3 * D_MODEL), lambda i: (i, 0)), row, row, row),
        sem=("parallel",), comm=comm)


def _gelu(x):
    return 0.5 * x * (1.0 + jnp.tanh(GELU_C * (x + GELU_A * x * x * x)))


def _gelu_grad(x):
    t = jnp.tanh(GELU_C * (x + GELU_A * x * x * x))
    return 0.5 * (1.0 + t) + 0.5 * x * (1.0 - t * t) * GELU_C * (1.0 + 3.0 * GELU_A * x * x)


def _shift_down(x, prev, n):
    rows = x.shape[0]
    r = lax.broadcasted_iota(jnp.int32, (8, 1), 0)
    rolled = pltpu.roll(x, n, axis=0)
    head = rolled[0:8]
    for j in range(n):
        head = jnp.where(r == j, prev[8 - n + j:8 - n + j + 1, :], head)
    return head if rows == 8 else jnp.concatenate([head, rolled[8:]], axis=0)


def _shift_up(x, nxt, n):
    rows = x.shape[0]
    r = lax.broadcasted_iota(jnp.int32, (8, 1), 0)
    rolled = pltpu.roll(x, rows - n, axis=0)
    tail = rolled[rows - 8:]
    for j in range(n):
        tail = jnp.where(r == 8 - n + j, nxt[j:j + 1, :], tail)
    return jnp.concatenate([rolled[:rows - 8], tail], axis=0)


def _conv(a, prev, cw, cb):
    return _shift_down(a, prev, 2) * cw[0:1, :] + _shift_down(a, prev, 1) * cw[1:2, :] + a * cw[2:3, :] + cb


def _glu_fwd(up, cw, cb, S, tm=256):
    T = up.shape[2]
    tm = _tile(S, tm)
    per_seq = S // tm

    def body(ab_ref, prev_ref, cw_ref, cb_ref, u_ref):
        i = pl.program_id(1)
        prev = jnp.where(i % per_seq == 0, 0.0, prev_ref[...])
        ac = _conv(ab_ref[0], prev, cw_ref[...], cb_ref[...])
        u_ref[...] = (_gelu(ac) * ab_ref[1]).astype(u_ref.dtype)

    return pl.pallas_call(
        body, out_shape=jax.ShapeDtypeStruct((FFN_SLABS, T, UP_SHARD), BF16), grid=(FFN_SLABS, T // tm),
        in_specs=[pl.BlockSpec((2, None, tm, UP_SHARD), lambda d, i: (0, d, i, 0)),
                  pl.BlockSpec((None, None, 8, UP_SHARD), lambda d, i: (0, d, jnp.maximum(i * (tm // 8) - 1, 0), 0)),
                  pl.BlockSpec((None, 3, UP_SHARD), lambda d, i: (d, 0, 0)),
                  pl.BlockSpec((None, 1, UP_SHARD), lambda d, i: (d, 0, 0))],
        out_specs=pl.BlockSpec((None, tm, UP_SHARD), lambda d, i: (d, i, 0)), name="glu_fwd",
        compiler_params=_params(("parallel", "parallel")),
    )(up, up, cw, cb)


def _glu_bwd(up, d_u, cw, cb, S, tm=256, comm=None):
    T = up.shape[2]
    tm = _tile(S, tm)
    per_seq = S // tm
    n_tiles = T // tm
    last8 = tm // 8

    def body(ab_ref, prev_ref, abn_ref, du_ref, dun_ref, cw_ref, cb_ref, dup_ref, dcw_ref, dcb_ref):
        i = pl.program_id(1)

        @pl.when(i == 0)
        def _():
            dcw_ref[...] = jnp.zeros_like(dcw_ref)
            dcb_ref[...] = jnp.zeros_like(dcb_ref)

        cw, cb = cw_ref[...], cb_ref[...]
        a, b = ab_ref[0], ab_ref[1]
        prev = jnp.where(i % per_seq == 0, 0.0, prev_ref[...])
        a2, a1 = _shift_down(a, prev, 2), _shift_down(a, prev, 1)
        ac = a2 * cw[0:1, :] + a1 * cw[1:2, :] + a * cw[2:3, :] + cb
        du = du_ref[...]
        dup_ref[1] = (du * _gelu(ac)).astype(dup_ref.dtype)
        dac = du * b * _gelu_grad(ac)
        dcb_ref[...] += jnp.sum(dac, axis=0, keepdims=True)
        dcw_ref[0:1, :] += jnp.sum(dac * a2, axis=0, keepdims=True)
        dcw_ref[1:2, :] += jnp.sum(dac * a1, axis=0, keepdims=True)
        dcw_ref[2:3, :] += jnp.sum(dac * a, axis=0, keepdims=True)
        an = abn_ref[0]
        acn = _conv(an, a[tm - 8:, :], cw, cb)
        dacn = jnp.where(i % per_seq == per_seq - 1, 0.0, dun_ref[...] * abn_ref[1] * _gelu_grad(acn))
        da = dac * cw[2:3, :] + _shift_up(dac, dacn, 1) * cw[1:2, :] + _shift_up(dac, dacn, 2) * cw[0:1, :]
        dup_ref[0] = da.astype(dup_ref.dtype)

    def nxt(i):
        return jnp.minimum((i + 1) * last8, T // 8 - 1)

    return _pcall(
        body, (up, up, up, d_u, d_u, cw, cb), name="glu_bwd",
        out_shape=(jax.ShapeDtypeStruct((2, FFN_SLABS, T, UP_SHARD), BF16),
                   jax.ShapeDtypeStruct((FFN_SLABS, 3, UP_SHARD), F32),
                   jax.ShapeDtypeStruct((FFN_SLABS, 1, UP_SHARD), F32)),
        grid=(FFN_SLABS, n_tiles),
        in_specs=[pl.BlockSpec((2, None, tm, UP_SHARD), lambda d, i: (0, d, i, 0)),
                  pl.BlockSpec((None, None, 8, UP_SHARD), lambda d, i: (0, d, jnp.maximum(i * last8 - 1, 0), 0)),
                  pl.BlockSpec((2, None, 8, UP_SHARD), lambda d, i: (0, d, nxt(i), 0)),
                  pl.BlockSpec((None, tm, UP_SHARD), lambda d, i: (d, i, 0)),
                  pl.BlockSpec((None, 8, UP_SHARD), lambda d, i: (d, nxt(i), 0)),
                  pl.BlockSpec((None, 3, UP_SHARD), lambda d, i: (d, 0, 0)),
                  pl.BlockSpec((None, 1, UP_SHARD), lambda d, i: (d, 0, 0))],
        out_specs=(pl.BlockSpec((2, None, tm, UP_SHARD), lambda d, i: (0, d, i, 0)),
                   pl.BlockSpec((None, 3, UP_SHARD), lambda d, i: (d, 0, 0)),
                   pl.BlockSpec((None, 1, UP_SHARD), lambda d, i: (d, 0, 0))),
        sem=("parallel", "arbitrary"), comm=comm)


def _mm_up(h2, w_up_t, tm=MM_ROWS):
    T, K = h2.shape
    tm = _tile(T, tm)
    return _matmul(
        "mm_up", "nt", h2, w_up_t, jax.ShapeDtypeStruct((N_DEV, T, UP_SHARD), F32), (N_DEV, T // tm, 1),
        pl.BlockSpec((tm, K), lambda j, i, k: (i, 0)), pl.BlockSpec((None, UP_SHARD, K), lambda j, i, k: (j, 0, 0)),
        pl.BlockSpec((None, tm, UP_SHARD), lambda j, i, k: (j, i, 0)), (tm, UP_SHARD))


def _loss_epilogue(ffn, operands, outputs, first):
    x1_ref, t_ref, g_ref = operands
    dx_ref, dg_ref, loss_ref = outputs

    @pl.when(first)
    def _():
        dg_ref[...] = jnp.zeros_like(dg_ref)
        loss_ref[...] = jnp.zeros_like(loss_ref)

    xv = x1_ref[...] + ffn
    r = lax.rsqrt(jnp.mean(xv * xv, axis=-1, keepdims=True) + EPS)
    xhat = xv * r
    err = xhat * g_ref[...] - t_ref[...]
    loss_ref[...] += (0.5 / D_MODEL) * jnp.sum(err * err)
    dy = err * (1.0 / D_MODEL)
    dg_ref[...] += jnp.sum(dy * xhat, axis=0, keepdims=True)
    dxhat = dy * g_ref[...]
    dx_ref[...] = r * (dxhat - xhat * jnp.mean(dxhat * xhat, axis=-1, keepdims=True))


def _mm_down_loss(u, w_down, x1, target, g_final, tm=MM_ROWS_RES):
    J, T, n = u.shape
    tm = _tile(T, tm)
    row = pl.BlockSpec((tm, D_MODEL), lambda i, d: (i, 0))
    vec = pl.BlockSpec((1, D_MODEL), lambda i, d: (0, 0))
    vec_shape = jax.ShapeDtypeStruct((1, D_MODEL), F32)
    return _matmul(
        "mm_down", "nn", u, w_down, (jax.ShapeDtypeStruct((T, D_MODEL), F32), vec_shape, vec_shape), (T // tm, J),
        pl.BlockSpec((None, tm, n), lambda i, d: (d, i, 0)), pl.BlockSpec((None, n, D_MODEL), lambda i, d: (d, 0, 0)),
        (row, vec, vec), (tm, D_MODEL), [x1, target, g_final], [row, row, vec], epilogue=_loss_epilogue)


def _mm_down_t(dx, w_down, tm=MM_ROWS):
    T = dx.shape[0]
    J, n, _ = w_down.shape
    tm = _tile(T, tm)
    return _matmul(
        "mm_down_t", "nt", dx, w_down, jax.ShapeDtypeStruct((J, T, n), F32), (J, T // tm, 1),
        pl.BlockSpec((tm, D_MODEL), lambda d, i, k: (i, 0)), pl.BlockSpec((None, n, D_MODEL), lambda d, i, k: (d, 0, 0)),
        pl.BlockSpec((None, tm, n), lambda d, i, k: (d, i, 0)), (tm, n))


def _mm_dw_down(u, dx, tk=MM_TOKENS):
    J, T, n = u.shape
    tk = _tile(T, tk)
    return _matmul(
        "mm_dw_down", "tn", u, dx, jax.ShapeDtypeStruct((J, n, D_MODEL), BF16), (J, T // tk),
        pl.BlockSpec((None, tk, n), lambda d, k: (d, k, 0)), pl.BlockSpec((tk, D_MODEL), lambda d, k: (k, 0)),
        pl.BlockSpec((None, n, D_MODEL), lambda d, k: (d, 0, 0)), (n, D_MODEL))


def _mm_dw_up(h2, d_up, tk=MM_TOKENS):
    T, K = h2.shape
    tk = _tile(T, tk)
    return _matmul(
        "mm_dw_up", "tn", d_up, h2, jax.ShapeDtypeStruct((N_DEV, UP_SHARD, K), BF16), (N_DEV, T // tk),
        pl.BlockSpec((None, tk, UP_SHARD), lambda j, k: (j, k, 0)), pl.BlockSpec((tk, K), lambda j, k: (k, 0)),
        pl.BlockSpec((None, UP_SHARD, K), lambda j, k: (j, 0, 0)), (UP_SHARD, K))


def _mm_up_t(d_up, w_up_t, rms, tm=MM_ROWS_RES, comm=None):
    J, T, n = d_up.shape
    K = w_up_t.shape[2]
    tm = _tile(T, tm)
    fused = _rms_bwd_fused(T, K, tm, rms)
    return _matmul(
        "mm_up_t", "nn", d_up, w_up_t, fused.pop("out_shape"), (T // tm, J),
        pl.BlockSpec((None, tm, n), lambda i, j: (j, i, 0)), pl.BlockSpec((None, n, K), lambda i, j: (j, 0, 0)),
        fused.pop("o_spec"), (tm, K), comm=comm, **fused)


def _cast_shards(shards):
    def body(*refs):
        n = len(refs) // 2
        for src, dst in zip(refs[:n], refs[n:]):
            dst[...] = src[...].astype(dst.dtype)

    return pl.pallas_call(
        body, out_shape=[jax.ShapeDtypeStruct(s.shape, BF16) for s in shards], name="cast_shards",
        compiler_params=pltpu.CompilerParams(vmem_limit_bytes=VMEM_LIMIT),
    )(*shards)


def _adamw(w, g, m, v):
    m = ADAM_B1 * m + (1.0 - ADAM_B1) * g
    v = ADAM_B2 * v + (1.0 - ADAM_B2) * (g * g)
    m_hat = m / (1.0 - ADAM_B1 ** ADAM_STEP)
    v_hat = v / (1.0 - ADAM_B2 ** ADAM_STEP)
    delta = -ADAM_LR * (m_hat / (jnp.sqrt(v_hat) + ADAM_EPS) + ADAM_WD * w)
    return delta, m, v


def _sum_parts(p_ref):
    g = p_ref[0].astype(F32)
    for d in range(1, N_DEV):
        g = g + p_ref[d].astype(F32)
    return g


def _reduce_adam(name, parts, w, m, v, tr=128):
    R, Cn = w.shape
    by_rows = sum(p.shape[1] for p in parts) == R and len(parts) > 1
    tr = math.gcd(tr, *[p.shape[1] for p in parts])
    n_tiles = [p.shape[1] // tr for p in parts]
    first = [sum(n_tiles[:j]) for j in range(len(parts))] if by_rows else [0] * len(parts)

    def body(*refs):
        p_refs = refs[:len(parts)]
        w_ref, m_ref, v_ref, g_out, d_out, m_out, v_out = refs[len(parts):]

        def update(p_ref):
            g = _sum_parts(p_ref)
            delta, m_new, v_new = _adamw(w_ref[...], g, m_ref[...], v_ref[...])
            g_out[...] = g
            d_out[...] = delta
            m_out[...] = m_new
            v_out[...] = v_new

        if len(parts) == 1:
            update(p_refs[0])
        elif by_rows:
            i = pl.program_id(0)
            for p_ref, t0, n in zip(p_refs, first, n_tiles):
                pl.when((i >= t0) & (i < t0 + n))(functools.partial(update, p_ref))
        else:
            c = lax.axis_index("c")
            for side, p_ref in enumerate(p_refs):
                pl.when(c == side)(functools.partial(update, p_ref))

    def part_spec(t0, n):
        return pl.BlockSpec((N_DEV, tr, Cn), lambda i: (0, jnp.clip(i - t0, 0, n - 1), 0))

    row = pl.BlockSpec((tr, Cn), lambda i: (i, 0))
    shape = jax.ShapeDtypeStruct((R, Cn), F32)
    return pl.pallas_call(
        body, out_shape=(shape,) * 4, grid=(R // tr,),
        in_specs=[part_spec(t0, n) for t0, n in zip(first, n_tiles)] + [row, row, row],
        out_specs=(row,) * 4, name=name, compiler_params=_params(("parallel",)),
    )(*parts, w, m, v)


def _small_adam(name, gathered, params):
    n_g, n_p = len(gathered), len(params)

    def body(*refs):
        g_refs = refs[:n_g]
        wmv = refs[n_g:n_g + 3 * n_p]
        sums = refs[n_g + 3 * n_p:2 * n_g + 3 * n_p]
        upd = refs[2 * n_g + 3 * n_p:]
        for j in range(n_g):
            g = _sum_parts(g_refs[j])
            sums[j][...] = g
            if j < n_p:
                w_ref, m_ref, v_ref = wmv[3 * j:3 * j + 3]
                delta, m_new, v_new = _adamw(w_ref[...], g, m_ref[...], v_ref[...])
                upd[3 * j][...] = delta
                upd[3 * j + 1][...] = m_new
                upd[3 * j + 2][...] = v_new

    flat = [a for wmv in params for a in wmv]
    out_shape = [jax.ShapeDtypeStruct(g.shape[1:], F32) for g in gathered]
    out_shape += [jax.ShapeDtypeStruct(a.shape, F32) for a in flat]
    res = pl.pallas_call(body, out_shape=out_shape, name=name)(*gathered, *flat)
    return res[:n_g], [tuple(res[n_g + 3 * j:n_g + 3 * j + 3]) for j in range(n_p)]


def _adam_only(name, g, w, m, v):
    def body(g_ref, w_ref, m_ref, v_ref, d_out, m_out, v_out):
        delta, m_new, v_new = _adamw(w_ref[...], g_ref[...], m_ref[...], v_ref[...])
        d_out[...] = delta
        m_out[...] = m_new
        v_out[...] = v_new

    shape = jax.ShapeDtypeStruct(w.shape, F32)
    return pl.pallas_call(body, out_shape=(shape,) * 3, name=name)(g, w, m, v)


def kernel(x, mem, g_mix, w_in, w_pool, pool_scale, w_a, g_ret, b_ret, w_r, g_mem, w_mem_kv, w_c, w_out, g_ffn, w_up, conv_w, conv_b, w_down, g_final, loss_target, m_g_mix, m_w_in, m_w_pool, m_pool_scale, m_w_a, m_g_ret, m_b_ret, m_w_r, m_g_mem, m_w_mem_kv, m_w_c, m_w_out, m_g_ffn, m_w_up, m_conv_w, m_conv_b, m_w_down, m_g_final, v_g_mix, v_w_in, v_w_pool, v_pool_scale, v_w_a, v_g_ret, v_b_ret, v_w_r, v_g_mem, v_w_mem_kv, v_w_c, v_w_out, v_g_ffn, v_w_up, v_conv_w, v_conv_b, v_w_down, v_g_final):
    B, S, _ = x.shape
    M = mem.shape[1]
    T = B * S
    me = _my_index()
    x2d = x.reshape(T, D_MODEL)
    mem2d = mem.reshape(B * M, D_MODEL)
    tgt2d = loss_target.reshape(T, D_MODEL)
    g_final2 = g_final.reshape(1, D_MODEL)

    big = dict(w_in=w_in[0], w_a=w_a[0], w_r=w_r[0], w_mem_kv=w_mem_kv[0], w_c=w_c[0], w_out=w_out[0],
               w_up=w_up[0].T, w_down=w_down[0])
    names = list(big)
    cast = dict(zip(names, _cast_shards([big[n] for n in names])))
    cb = conv_b[0].reshape(FFN_SLABS, 1, UP_SHARD)
    wp = w_pool[0]
    tables = _ret_tables(S)

    h = _rms_fwd("rms_mix", x2d, g_mix)
    early = ("w_a", "w_r", "w_mem_kv", "w_c", "w_out")
    (proj, Win), landed = _mm_in_gather(h, cast["w_in"], comm=_Gather([cast[n] for n in early] + [conv_w[0]]))
    W = dict(zip(early, landed))
    cw_full = landed[-1].transpose(1, 0, 2).reshape(3, FFN_HIDDEN)
    cw = cw_full.reshape(3, FFN_SLABS, UP_SHARD).transpose(1, 0, 2)
    Wa = W["w_a"].transpose(1, 0, 2).reshape(POOL_WIDTH, D_MODEL)
    Wc = W["w_c"].transpose(1, 0, 2).reshape(XA_WIDTH, D_MODEL)
    Wr = W["w_r"].reshape(D_MODEL, D_MODEL)
    Wkv = W["w_mem_kv"].reshape(D_MODEL, D_MODEL)
    Wout = W["w_out"].reshape(D_MODEL, D_MODEL)
    ypre = _pool_fwd(proj, wp, pool_scale, B, S)
    y_pool = _mm_rows("mm_a", ypre, Wa, BF16)
    (yr, ret_states), (Wup,) = _ret_fwd(proj, g_ret, b_ret, tables, B, S, comm=_Gather([cast["w_up"]]))
    y_ret = _mm_rows("mm_r", yr, Wr, BF16)
    mem_n = _rms_fwd("rms_mem", mem2d, g_mem)
    kv = _mm_rows("mm_kv", mem_n, Wkv)
    o_mem, (Wdown,) = _xa_fwd(proj, kv, B, S, M, comm=_Gather([cast["w_down"]]))
    Wdown = Wdown.reshape(FFN_SLABS, UP_SHARD, D_MODEL)
    y_mem = _mm_rows("mm_c", o_mem, Wc, BF16)
    ys = (y_pool, y_ret, y_mem)
    merged = _merge_fwd(proj, ys)
    x1, h2 = _mm_residual_rms("mm_out", merged, Wout, x2d, g_ffn)
    up = _mm_up(h2, Wup).reshape(2, FFN_SLABS, T, UP_SHARD)
    u = _glu_fwd(up, cw, cb, S)

    dx2, dg_final, loss_part = _mm_down_loss(u, Wdown, x1, tgt2d, g_final2)
    received = {}
    d_u = _mm_down_t(dx2, Wdown)
    dW_down = _mm_dw_down(u, dx2)
    (d_up, d_cw, d_cb), (received["w_down"],) = _glu_bwd(
        up, d_u, cw, cb, S, comm=_Exchange([dW_down.reshape(N_DEV, -1, D_MODEL)]))
    d_up = d_up.reshape(N_DEV, T, UP_SHARD)
    dW_up = _mm_dw_up(h2, d_up)
    (dx1, dg_ffn), (up_c0,) = _mm_up_t(d_up, Wup, (x1, g_ffn, dx2), comm=_ExchangeTo([dW_up], 0))
    d_merged = _mm_rows("mm_out_t", dx1, Wout, kind="nt")
    dW_out = _mm_tn("mm_dw_out", merged, dx1, BF16)
    (d_gl, d_y_pool, d_y_ret, d_y_mem), (received["w_out"],) = _merge_bwd(
        proj, ys, d_merged, comm=_Exchange([dW_out.reshape(N_DEV, -1, D_MODEL)]))
    dW_c = _mm_tn("mm_dw_c", o_mem, d_y_mem, BF16)
    d_o_mem = _mm_rows("mm_c_t", d_y_mem, Wc, kind="nt")
    (d_qx, d_kmem, d_vmem), (up_c1,) = _xa_bwd(proj, kv, d_o_mem, B, S, M, comm=_ExchangeTo([dW_up], 1))
    received["w_up"] = [up_c0, up_c1]
    d_kv = jnp.concatenate([d_kmem, d_vmem], axis=1)
    dW_kv = _mm_tn("mm_dw_kv", mem_n, d_kv, BF16)
    d_mem_n = _mm_rows("mm_kv_t", d_kv, Wkv, kind="nt")
    dg_mem = _rms_bwd("rms_mem_bwd", mem2d, g_mem, d_mem_n, None)
    dW_a = _mm_tn("mm_dw_a", ypre, d_y_pool, BF16)
    d_ypre = _mm_rows("mm_a_t", d_y_pool, Wa, kind="nt")
    d_hp, dw_pool, d_scale = _pool_bwd(proj, d_ypre, wp, pool_scale, B, S)
    dW_r = _mm_tn("mm_dw_r", yr, d_y_ret, BF16)
    d_yr = _mm_rows("mm_r_t", d_y_ret, Wr, kind="nt")
    (d_q, d_k, d_v, d_gr, dg_ret, db_ret), landed = _ret_bwd(
        proj, ret_states, d_yr, g_ret, b_ret, tables, B, S,
        comm=_Exchange([dW_a.reshape(POOL_WIDTH, N_DEV, -1).transpose(1, 0, 2), dW_r.reshape(N_DEV, -1, D_MODEL),
                        dW_c.reshape(XA_WIDTH, N_DEV, -1).transpose(1, 0, 2), dW_kv.reshape(N_DEV, -1, D_MODEL)]))
    received["w_a"], received["w_r"], received["w_c"], received["w_mem_kv"] = landed
    small_names = ["w_pool", "pool_scale", "g_ret", "b_ret", "g_mem", "g_ffn", "conv_b", "g_final"]
    small_grads = [dw_pool, d_scale, dg_ret, db_ret, dg_mem, dg_ffn, d_cb.reshape(1, FFN_HIDDEN), dg_final,
                   d_cw.transpose(1, 0, 2).reshape(3, FFN_HIDDEN), loss_part]
    d_proj = jnp.concatenate([d_hp, d_q, d_k, d_v, d_gr, d_qx, d_gl], axis=1)
    dW_in0, small_all = _mm_tn_slab("mm_dw_in0", h[:, :W_IN_FIRST_ROWS], d_proj, IN_SHARD, BF16,
                                    comm=_Exchange([], whole=small_grads))
    dW_in1, (in0,) = _mm_tn_slab("mm_dw_in1", h[:, W_IN_FIRST_ROWS:], d_proj, IN_SHARD, BF16,
                                 comm=_Exchange([dW_in0]))
    (grad_x, dg_mix), (in1,) = _mm_cols_slab_t("mm_in_t", d_proj, Win, (x2d, g_mix, dx1), comm=_Exchange([dW_in1]))
    received["w_in"] = [in0, in1]
    (g_mix_all,) = _comm_call("gather_g_mix", _Exchange([], whole=[dg_mix]))

    args = dict(g_mix=g_mix, w_in=w_in, w_pool=w_pool, pool_scale=pool_scale, w_a=w_a, g_ret=g_ret, b_ret=b_ret,
                w_r=w_r, g_mem=g_mem, w_mem_kv=w_mem_kv, w_c=w_c, w_out=w_out, g_ffn=g_ffn, w_up=w_up,
                conv_w=conv_w, conv_b=conv_b, w_down=w_down, g_final=g_final)
    m_in = dict(g_mix=m_g_mix, w_in=m_w_in, w_pool=m_w_pool, pool_scale=m_pool_scale, w_a=m_w_a, g_ret=m_g_ret,
                b_ret=m_b_ret, w_r=m_w_r, g_mem=m_g_mem, w_mem_kv=m_w_mem_kv, w_c=m_w_c, w_out=m_w_out,
                g_ffn=m_g_ffn, w_up=m_w_up, conv_w=m_conv_w, conv_b=m_conv_b, w_down=m_w_down, g_final=m_g_final)
    v_in = dict(g_mix=v_g_mix, w_in=v_w_in, w_pool=v_w_pool, pool_scale=v_pool_scale, w_a=v_w_a, g_ret=v_g_ret,
                b_ret=v_b_ret, w_r=v_w_r, g_mem=v_g_mem, w_mem_kv=v_w_mem_kv, w_c=v_w_c, w_out=v_w_out,
                g_ffn=v_g_ffn, w_up=v_w_up, conv_w=v_conv_w, conv_b=v_conv_b, w_down=v_w_down, g_final=v_g_final)

    grads, deltas, new_m, new_v = {}, {}, {}, {}
    for n in names:
        parts = received[n] if isinstance(received[n], list) else [received[n]]
        flip = (lambda a: a.T) if n == "w_up" else (lambda a: a)
        outs = _reduce_adam("adam_" + n, parts, big[n], flip(m_in[n][0]), flip(v_in[n][0]))
        for store, val in zip((grads, deltas, new_m, new_v), outs):
            store[n] = flip(val)[None]

    def as_small(a):
        return a.reshape(a.shape[-3:]) if a.ndim > 2 else a.reshape(1, -1)

    def small_update(call_name, param_names, gathered):
        params = [tuple(as_small(d[n]) for d in (args, m_in, v_in)) for n in param_names]
        sums, updates = _small_adam(call_name, gathered, params)
        for n, g, (d_, m_, v_) in zip(param_names, sums, updates):
            shape = args[n].shape
            grads[n], deltas[n], new_m[n], new_v[n] = (a.reshape(shape) for a in (g, d_, m_, v_))
        return sums[len(param_names):]

    g_cw_full, loss_row = small_update("adam_small", small_names, small_all)
    loss = loss_row[0, 0]
    small_update("adam_g_mix", ["g_mix"], [g_mix_all])

    shard_cols = FFN_HIDDEN // N_DEV
    g_cw = lax.dynamic_slice_in_dim(g_cw_full, me * shard_cols, shard_cols, axis=1)
    d_, m_, v_ = _adam_only("adam_conv_w", g_cw, conv_w[0], m_conv_w[0], v_conv_w[0])
    grads["conv_w"], deltas["conv_w"], new_m["conv_w"], new_v["conv_w"] = g_cw[None], d_[None], m_[None], v_[None]

    order = ["g_mix", "w_in", "w_pool", "pool_scale", "w_a", "g_ret", "b_ret", "w_r", "g_mem", "w_mem_kv", "w_c",
             "w_out", "g_ffn", "w_up", "conv_w", "conv_b", "w_down", "g_final"]
    return (loss, grad_x.reshape(B, S, D_MODEL), *[grads[n] for n in order], *[deltas[n] for n in order],
            *[new_m[n] for n in order], *[new_v[n] for n in order])
```

```python
import functools
import math

import jax
import jax.numpy as jnp
from jax import lax
from jax.experimental import pallas as pl
from jax.experimental.pallas import tpu as pltpu

F32 = jnp.float32
BF16 = jnp.bfloat16

N_DEV = 8
D_MODEL = 1024
POOL_WINDOWS = (2, 4, 8, 16)
POOL_GROUP_DIM = 128
POOL_WIDTH = 512
POOL_HALO = 16
RET_HEADS = 4
RET_QK_DIM = 128
RET_V_DIM = 256
RET_CHUNK = 128
ROPE_BASE = 10000.0
XA_HEADS = 4
XA_HEAD_DIM = 128
XA_WIDTH = 512
IN_WIDTH = 7168
IN_SHARD = IN_WIDTH // N_DEV
FFN_HIDDEN = 2816
UP_SHARD = 2 * FFN_HIDDEN // N_DEV
FFN_SLABS = FFN_HIDDEN // UP_SHARD
EPS = 1e-6
ADAM_LR = 0.001
ADAM_B1 = 0.9
ADAM_B2 = 0.999
ADAM_EPS = 1e-08
ADAM_WD = 0.01
ADAM_STEP = 10
GELU_C = math.sqrt(2.0 / math.pi)
GELU_A = 0.044715
VMEM_LIMIT = 56 * 1024 * 1024
MM_ROWS = 2048
MM_ROWS_RES = 1024
MM_TOKENS = 2048
W_IN_FIRST_ROWS = 384
MESH = pl.DeviceIdType.MESH

COL_Q, COL_K, COL_V, COL_GR, COL_QX, COL_GL = 512, 1024, 1536, 2560, 3584, 4096

_DIMS = {
    "nn": (((1,), (0,)), ((), ())),
    "nt": (((1,), (1,)), ((), ())),
    "tn": (((0,), (0,)), ((), ())),
}


def _dot(a, b, kind="nn"):
    return lax.dot_general(a.astype(BF16), b.astype(BF16), _DIMS[kind], preferred_element_type=F32)


def _params(sem, vmem=VMEM_LIMIT):
    return pltpu.CompilerParams(dimension_semantics=sem, vmem_limit_bytes=vmem)


def _tile(n, pref):
    t = min(n, pref)
    while n % t:
        t //= 2
    return t


def _mesh_pos():
    return lax.axis_index("x"), lax.axis_index("y"), lax.axis_index("c")


def _dev_index(x, y, c):
    return 4 * x + 2 * y + c


def _my_index():
    return _dev_index(*_mesh_pos())


def _remote(src, dst, send_sems, recv_sems, s, to):
    return pltpu.make_async_remote_copy(src_ref=src, dst_ref=dst, send_sem=send_sems.at[s], recv_sem=recv_sems.at[s],
                                        device_id=to, device_id_type=MESH)


class _Gather:
    def __init__(self, shards):
        self.inputs = list(shards)
        self.out_shapes = [jax.ShapeDtypeStruct((N_DEV,) + s.shape, s.dtype) for s in shards]
        n = len(shards)
        self.sem_shapes = [pltpu.SemaphoreType.DMA((7 * n,)), pltpu.SemaphoreType.DMA((7 * n,)),
                           pltpu.SemaphoreType.DMA((n,))]

    def _places(self):
        x, y, c = _mesh_pos()
        return (x, y, c), (x, y, 1 - c), [(1 - x, y), (x, 1 - y), (1 - x, 1 - y)]

    def _local(self, src, dst, sems):
        me = _my_index()
        return [pltpu.make_async_copy(src[w], dst[w].at[me], sems[2].at[w]) for w in range(len(src))]

    def start(self, src, dst, sems):
        me, sib, chips = self._places()
        for cp in self._local(src, dst, sems):
            cp.start()
        for w in range(len(src)):
            land = dst[w].at[_dev_index(*me)]
            _remote(src[w], land, sems[0], sems[1], 7 * w, sib).start()
            for j, chip in enumerate(chips):
                _remote(src[w], land, sems[0], sems[1], 7 * w + 1 + j, (*chip, me[2])).start()

    def middle(self, src, dst, sems):
        me, sib, chips = self._places()
        for j, chip in enumerate(chips):
            for w in range(len(src)):
                block = dst[w].at[_dev_index(*chip, me[2])]
                _remote(src[w], block, sems[0], sems[1], 7 * w + 1 + j, me).wait_recv()
                _remote(block, block, sems[0], sems[1], 7 * w + 4 + j, sib).start()

    def finish(self, src, dst, sems):
        me, sib, chips = self._places()
        n = len(src)
        for w in range(n):
            _remote(src[w], dst[w].at[_dev_index(*sib)], sems[0], sems[1], 7 * w, me).wait_recv()
            for j, chip in enumerate(chips):
                block = dst[w].at[_dev_index(*chip, sib[2])]
                _remote(block, block, sems[0], sems[1], 7 * w + 4 + j, me).wait_recv()
            for k in range(7):
                _remote(src[w], dst[w].at[0], sems[0], sems[1], 7 * w + k, me).wait_send()
        for cp in self._local(src, dst, sems):
            cp.wait()


class _Exchange:
    def __init__(self, partials, whole=()):
        self.n_part = len(partials)
        self.inputs = list(partials) + list(whole)
        self.out_shapes = [jax.ShapeDtypeStruct(p.shape, p.dtype) for p in partials]
        self.out_shapes += [jax.ShapeDtypeStruct((N_DEV,) + a.shape, a.dtype) for a in whole]
        n = len(self.inputs)
        self.sem_shapes = [pltpu.SemaphoreType.DMA((7 * n,)), pltpu.SemaphoreType.DMA((7 * n,)),
                           pltpu.SemaphoreType.DMA((n,))]

    def _peer(self, k):
        x, y, c = _mesh_pos()
        p = (x ^ ((k >> 2) & 1), y ^ ((k >> 1) & 1), c ^ (k & 1))
        return p, _dev_index(*p)

    def _source(self, src, w, slot):
        return src[w].at[slot] if w < self.n_part else src[w]

    def _local(self, src, dst, sems):
        me = _my_index()
        return [pltpu.make_async_copy(self._source(src, w, me), dst[w].at[me], sems[2].at[w])
                for w in range(len(src))]

    def start(self, src, dst, sems):
        me = _my_index()
        for cp in self._local(src, dst, sems):
            cp.start()
        for k in range(1, N_DEV):
            peer, peer_idx = self._peer(k)
            for w in range(len(src)):
                _remote(self._source(src, w, peer_idx), dst[w].at[me], sems[0], sems[1], 7 * w + k - 1, peer).start()

    def finish(self, src, dst, sems):
        for k in range(1, N_DEV):
            peer, peer_idx = self._peer(k)
            for w in range(len(src)):
                cp = _remote(self._source(src, w, peer_idx), dst[w].at[peer_idx], sems[0], sems[1], 7 * w + k - 1, peer)
                cp.wait_send()
                cp.wait_recv()
        for cp in self._local(src, dst, sems):
            cp.wait()


class _ExchangeTo:
    def __init__(self, partials, side):
        self.side = side
        self.inputs = list(partials)
        self.out_shapes = [jax.ShapeDtypeStruct(p.shape, p.dtype) for p in partials]
        n = len(partials)
        self.sem_shapes = [pltpu.SemaphoreType.DMA((7 * n,)), pltpu.SemaphoreType.DMA((7 * n,)),
                           pltpu.SemaphoreType.DMA((n,))]

    def _copies(self, src, dst, sems):
        x, y, c = _mesh_pos()
        me = _dev_index(x, y, c)
        receives = c == self.side
        remote = []
        for k in range(1, N_DEV):
            kx, ky, kc = (k >> 2) & 1, (k >> 1) & 1, k & 1
            peer = (x ^ kx, y ^ ky, c ^ kc)
            peer_idx = _dev_index(*peer)
            sends = c == (self.side ^ kc)
            for w in range(len(src)):
                slab = src[w].at[peer_idx]
                s = 7 * w + k - 1
                remote.append((sends, _remote(slab, dst[w].at[me], sems[0], sems[1], s, peer),
                               _remote(slab, dst[w].at[peer_idx], sems[0], sems[1], s, peer)))
        local = [pltpu.make_async_copy(src[w].at[me], dst[w].at[me], sems[2].at[w]) for w in range(len(src))]
        return receives, remote, local

    def start(self, src, dst, sems):
        receives, remote, local = self._copies(src, dst, sems)

        @pl.when(receives)
        def _():
            for cp in local:
                cp.start()

        for sends, send, _ in remote:
            pl.when(sends)(send.start)

    def finish(self, src, dst, sems):
        receives, remote, local = self._copies(src, dst, sems)
        for sends, send, arrive in remote:
            pl.when(sends)(send.wait_send)
            pl.when(receives)(arrive.wait_recv)

        @pl.when(receives)
        def _():
            for cp in local:
                cp.wait()


def _pcall(body, args, *, name, out_shape, grid, in_specs, out_specs, scratch_shapes=(), sem=None, comm=None):
    single = not isinstance(out_shape, (tuple, list))
    outs = [out_shape] if single else list(out_shape)
    ospecs = [out_specs] if single else list(out_specs)
    n_in, n_out, n_scr = len(args), len(outs), len(scratch_shapes)

    def pick(res):
        return res[0] if single else tuple(res[:n_out])

    if comm is None:
        res = pl.pallas_call(
            body, out_shape=outs, grid=grid, in_specs=list(in_specs), out_specs=ospecs,
            scratch_shapes=list(scratch_shapes), name=name, compiler_params=_params(sem),
        )(*args)
        return pick(res), ()

    nci, nco = len(comm.inputs), len(comm.out_shapes)

    def carrier(*refs):
        at = 0
        parts = []
        for size in (n_in, nci, n_out, nco, n_scr, len(comm.sem_shapes)):
            parts.append(refs[at:at + size])
            at += size
        ins, cins, o, couts, scr, sems = parts
        ids = [pl.program_id(a) for a in range(len(grid))]
        first = functools.reduce(jnp.logical_and, [i == 0 for i in ids])
        last = functools.reduce(jnp.logical_and, [i == g - 1 for i, g in zip(ids, grid)])

        body(*ins, *o, *scr)

        @pl.when(first)
        def _():
            comm.start(cins, couts, sems)

        if hasattr(comm, "middle"):
            steps = math.prod(grid)
            at = functools.reduce(lambda lin, ig: lin * ig[1] + ig[0], zip(ids, grid), 0)

            @pl.when(at == min(steps - 1, (3 * steps) // 4))
            def _():
                comm.middle(cins, couts, sems)

        @pl.when(last)
        def _():
            comm.finish(cins, couts, sems)

    hbm = pl.BlockSpec(memory_space=pltpu.HBM)
    res = pl.pallas_call(
        carrier, out_shape=outs + comm.out_shapes, grid=grid, in_specs=list(in_specs) + [hbm] * nci,
        out_specs=ospecs + [hbm] * nco, scratch_shapes=list(scratch_shapes) + comm.sem_shapes, name=name,
        compiler_params=_params(("arbitrary",) * len(grid)),
    )(*args, *comm.inputs)
    return pick(res), tuple(res[n_out:])


def _comm_call(name, comm):
    def body(*refs):
        nci, nco = len(comm.inputs), len(comm.out_shapes)
        cins, couts, sems = refs[:nci], refs[nci:nci + nco], refs[nci + nco:]
        comm.start(cins, couts, sems)
        if hasattr(comm, "middle"):
            comm.middle(cins, couts, sems)
        comm.finish(cins, couts, sems)

    hbm = pl.BlockSpec(memory_space=pltpu.HBM)
    return pl.pallas_call(
        body, out_shape=comm.out_shapes, in_specs=[hbm] * len(comm.inputs), out_specs=[hbm] * len(comm.out_shapes),
        scratch_shapes=comm.sem_shapes, name=name,
    )(*comm.inputs)


def _matmul(name, kind, a, b, out_shape, grid, a_spec, b_spec, o_spec, acc_shape, res=None, res_spec=None,
            comm=None, epilogue=None):
    nk = grid[-1]
    if epilogue is None:
        extra, extra_specs = ([res], [res_spec]) if res is not None else ([], [])
        n_out = 1
    else:
        extra, extra_specs, n_out = list(res), list(res_spec), len(out_shape)
    n_in = 2 + len(extra)

    def body(*refs):
        a_ref, b_ref = refs[0], refs[1]
        extra_refs, out_refs = refs[2:n_in], refs[n_in:n_in + n_out]

        def prod():
            return _dot(a_ref[...], b_ref[...], kind)

        def finish(acc):
            if epilogue is not None:
                ids = [pl.program_id(ax) for ax in range(len(grid) - 1)]
                first = functools.reduce(jnp.logical_and, [i == 0 for i in ids]) if ids else True
                epilogue(acc, extra_refs, out_refs, first)
                return
            if extra_refs:
                acc = acc + extra_refs[0][...]
            out_refs[0][...] = acc.astype(out_refs[0].dtype)

        if nk == 1:
            finish(prod())
        else:
            acc_ref = refs[n_in + n_out]
            k = pl.program_id(len(grid) - 1)

            @pl.when(k == 0)
            def _():
                acc_ref[...] = prod()

            @pl.when(k > 0)
            def _():
                acc_ref[...] += prod()

            @pl.when(k == nk - 1)
            def _():
                finish(acc_ref[...])

    in_specs = [a_spec, b_spec] + extra_specs
    args = (a, b, *extra)
    scratch = [pltpu.VMEM(acc_shape, F32)] if nk > 1 else []
    sem = ("arbitrary",) * len(grid) if epilogue is not None else ("parallel",) * (len(grid) - 1) + ("arbitrary",)
    out, landed = _pcall(body, args, name=name, out_shape=out_shape, grid=grid, in_specs=in_specs,
                         out_specs=o_spec, scratch_shapes=scratch, sem=sem, comm=comm)
    return out if comm is None else (out, landed)


def _mm_rows(name, a, w, out_dtype=F32, res=None, kind="nn", tm=MM_ROWS, comm=None):
    M, K = a.shape
    N = w.shape[1] if kind == "nn" else w.shape[0]
    tm = _tile(M, tm)
    res_spec = pl.BlockSpec((tm, N), lambda i, k: (i, 0)) if res is not None else None
    return _matmul(
        name, kind, a, w, jax.ShapeDtypeStruct((M, N), out_dtype), (M // tm, 1),
        pl.BlockSpec((tm, K), lambda i, k: (i, 0)), pl.BlockSpec(w.shape, lambda i, k: (0, 0)),
        pl.BlockSpec((tm, N), lambda i, k: (i, 0)), (tm, N), res, res_spec, comm)


def _residual_rms_epilogue(y, operands, outputs, first):
    x_ref, g_ref = operands
    x1_ref, h_ref = outputs
    xv = x_ref[...] + y
    x1_ref[...] = xv
    r = lax.rsqrt(jnp.mean(xv * xv, axis=-1, keepdims=True) + EPS)
    h_ref[...] = (xv * r * g_ref[...]).astype(h_ref.dtype)


def _mm_residual_rms(name, a, w, x, g, tm=MM_ROWS_RES):
    M, K = a.shape
    N = w.shape[1]
    tm = _tile(M, tm)
    row = pl.BlockSpec((tm, N), lambda i, k: (i, 0))
    return _matmul(
        name, "nn", a, w, (jax.ShapeDtypeStruct((M, N), F32), jax.ShapeDtypeStruct((M, N), BF16)), (M // tm, 1),
        pl.BlockSpec((tm, K), lambda i, k: (i, 0)), pl.BlockSpec(w.shape, lambda i, k: (0, 0)),
        (row, row), (tm, N), [x, g], [row, pl.BlockSpec((1, N), lambda i, k: (0, 0))],
        epilogue=_residual_rms_epilogue)


def _mm_tn(name, a, b, out_dtype=F32, tk=MM_TOKENS, comm=None):
    T, M = a.shape
    N = b.shape[1]
    tk = _tile(T, tk)
    return _matmul(
        name, "tn", a, b, jax.ShapeDtypeStruct((M, N), out_dtype), (1, T // tk),
        pl.BlockSpec((tk, M), lambda i, k: (k, 0)), pl.BlockSpec((tk, N), lambda i, k: (k, 0)),
        pl.BlockSpec((M, N), lambda i, k: (0, 0)), (M, N), comm=comm)


def _mm_in_gather(h, shard, tm=MM_ROWS, comm=None):
    T, K = h.shape
    n = shard.shape[1]
    tm = _tile(T, tm)
    n_tiles = T // tm
    pair_of_chip_step = {4: 1, 2: 2, 6: 3}

    def slab_of(s):
        x, y, c = _mesh_pos()
        return _dev_index(x ^ ((s >> 2) & 1), y ^ ((s >> 1) & 1), c ^ (s & 1))

    def body(h_ref, shard_ref, proj_ref, win_ref, wbuf, slot_sems, send_sems, recv_sems, local_sem):
        s, i = pl.program_id(0), pl.program_id(1)
        x, y, c = _mesh_pos()
        me, sib = (x, y, c), (x, y, 1 - c)

        def slot_copy(step):
            src = shard_ref if step == 0 else win_ref.at[slab_of(step)]
            return pltpu.make_async_copy(src, wbuf.at[step % 2], slot_sems.at[step % 2])

        def fetch(step):
            if step >= 1:
                block = win_ref.at[slab_of(step)]
                if step == 1:
                    pair = 0
                elif step % 2 == 0:
                    pair = pair_of_chip_step[step]
                else:
                    pair = 3 + pair_of_chip_step[step - 1]
                _remote(block, block, send_sems, recv_sems, pair, me).wait_recv()
                if step % 2 == 0:
                    _remote(block, block, send_sems, recv_sems, 3 + pair, sib).start()
            slot_copy(step).start()

        @pl.when((s == 0) & (i == 0))
        def _():
            land = win_ref.at[_dev_index(*me)]
            pltpu.make_async_copy(shard_ref, land, local_sem).start()
            _remote(shard_ref, land, send_sems, recv_sems, 0, sib).start()
            for step, pair in pair_of_chip_step.items():
                peer = (x ^ ((step >> 2) & 1), y ^ ((step >> 1) & 1), c)
                _remote(shard_ref, land, send_sems, recv_sems, pair, peer).start()
            fetch(0)

        for step in range(N_DEV):
            @pl.when((s == step) & (i == 0))
            def _():
                slot_copy(step).wait()

            if step + 1 < N_DEV:
                @pl.when((s == step) & (i == n_tiles - 1))
                def _():
                    fetch(step + 1)

        proj_ref[...] = _dot(h_ref[...], wbuf[s % 2])

        @pl.when((s == N_DEV - 1) & (i == n_tiles - 1))
        def _():
            for pair in range(7):
                _remote(shard_ref, win_ref.at[0], send_sems, recv_sems, pair, me).wait_send()
            pltpu.make_async_copy(shard_ref, win_ref.at[_dev_index(*me)], local_sem).wait()

    hbm = pl.BlockSpec(memory_space=pltpu.HBM)
    return _pcall(
        body, (h, shard), name="mm_in",
        out_shape=(jax.ShapeDtypeStruct((T, N_DEV * n), F32), jax.ShapeDtypeStruct((N_DEV, K, n), shard.dtype)),
        grid=(N_DEV, n_tiles), in_specs=[pl.BlockSpec((tm, K), lambda s, i: (i, 0)), hbm],
        out_specs=(pl.BlockSpec((tm, n), lambda s, i: (i, slab_of(s))), hbm),
        scratch_shapes=[pltpu.VMEM((2, K, n), shard.dtype), pltpu.SemaphoreType.DMA((2,)),
                        pltpu.SemaphoreType.DMA((7,)), pltpu.SemaphoreType.DMA((7,)), pltpu.SemaphoreType.DMA],
        sem=("arbitrary", "arbitrary"), comm=comm)


def _rms_bwd_epilogue(dh, operands, outputs, first):
    x_ref, g_ref, dres_ref = operands
    dx_ref, dg_ref = outputs
    xv = x_ref[...]
    r = lax.rsqrt(jnp.mean(xv * xv, axis=-1, keepdims=True) + EPS)
    xhat = xv * r

    @pl.when(first)
    def _():
        dg_ref[...] = jnp.zeros_like(dg_ref)

    dg_ref[...] += jnp.sum(dh * xhat, axis=0, keepdims=True)
    dxhat = dh * g_ref[...]
    dx_ref[...] = dres_ref[...] + r * (dxhat - xhat * jnp.mean(dxhat * xhat, axis=-1, keepdims=True))


def _rms_bwd_fused(M, K, tm, rms):
    row = pl.BlockSpec((tm, K), lambda i, j: (i, 0))
    vec = pl.BlockSpec((1, K), lambda i, j: (0, 0))
    x, g, dres = rms
    return dict(res=[x, g, dres], res_spec=[row, vec, row], epilogue=_rms_bwd_epilogue,
                out_shape=(jax.ShapeDtypeStruct((M, K), F32), jax.ShapeDtypeStruct((1, K), F32)), o_spec=(row, vec))


def _mm_cols_slab_t(name, a, w_slabs, rms, tm=MM_ROWS_RES, comm=None):
    M = a.shape[0]
    J, K, n = w_slabs.shape
    tm = _tile(M, tm)
    fused = _rms_bwd_fused(M, K, tm, rms)
    return _matmul(
        name, "nt", a, w_slabs, fused.pop("out_shape"), (M // tm, J),
        pl.BlockSpec((tm, n), lambda i, j: (i, j)), pl.BlockSpec((None, K, n), lambda i, j: (j, 0, 0)),
        fused.pop("o_spec"), (tm, K), comm=comm, **fused)


def _mm_tn_slab(name, a, b, n, out_dtype=F32, tk=MM_TOKENS, comm=None, part=(0, 1)):
    T, M = a.shape
    p, of = part
    M //= of
    J = b.shape[1] // n
    tk = _tile(T, tk)
    return _matmul(
        name, "tn", a, b, jax.ShapeDtypeStruct((J, M, n), out_dtype), (J, T // tk),
        pl.BlockSpec((tk, M), lambda j, k: (k, p)), pl.BlockSpec((tk, n), lambda j, k: (k, j)),
        pl.BlockSpec((None, M, n), lambda j, k: (j, 0, 0)), (M, n), comm=comm)


def _rms_fwd(name, x, g, tm=512):
    T, Dm = x.shape
    tm = _tile(T, tm)

    def body(x_ref, g_ref, h_ref):
        xv = x_ref[...]
        r = lax.rsqrt(jnp.mean(xv * xv, axis=-1, keepdims=True) + EPS)
        h_ref[...] = (xv * r * g_ref[...]).astype(h_ref.dtype)

    return pl.pallas_call(
        body, out_shape=jax.ShapeDtypeStruct((T, Dm), BF16), grid=(T // tm,),
        in_specs=[pl.BlockSpec((tm, Dm), lambda i: (i, 0)), pl.BlockSpec((1, Dm), lambda i: (0, 0))],
        out_specs=pl.BlockSpec((tm, Dm), lambda i: (i, 0)), name=name, compiler_params=_params(("parallel",)),
    )(x, g)


def _rms_bwd(name, x, g, dh, dres, tm=512):
    T, Dm = x.shape
    tm = _tile(T, tm)
    want_dx = dres is not None

    def body(*refs):
        if want_dx:
            x_ref, g_ref, dh_ref, dres_ref, dx_ref, dg_ref = refs
        else:
            x_ref, g_ref, dh_ref, dg_ref = refs
        xv = x_ref[...]
        r = lax.rsqrt(jnp.mean(xv * xv, axis=-1, keepdims=True) + EPS)
        xhat = xv * r
        dhv = dh_ref[...]

        @pl.when(pl.program_id(0) == 0)
        def _():
            dg_ref[...] = jnp.zeros_like(dg_ref)

        dg_ref[...] += jnp.sum(dhv * xhat, axis=0, keepdims=True)
        if want_dx:
            dxhat = dhv * g_ref[...]
            dx_ref[...] = dres_ref[...] + r * (dxhat - xhat * jnp.mean(dxhat * xhat, axis=-1, keepdims=True))

    row = pl.BlockSpec((tm, Dm), lambda i: (i, 0))
    vec = pl.BlockSpec((1, Dm), lambda i: (0, 0))
    if want_dx:
        return pl.pallas_call(
            body, out_shape=(jax.ShapeDtypeStruct((T, Dm), F32), jax.ShapeDtypeStruct((1, Dm), F32)),
            grid=(T // tm,), in_specs=[row, vec, row, row], out_specs=(row, vec), name=name,
            compiler_params=_params(("arbitrary",)),
        )(x, g, dh, dres)
    return pl.pallas_call(
        body, out_shape=jax.ShapeDtypeStruct((1, Dm), F32), grid=(T // tm,), in_specs=[row, vec, row],
        out_specs=vec, name=name, compiler_params=_params(("arbitrary",)),
    )(x, g, dh)


def _pool_rows(S):
    return _tile(S, 256)


def _pool_count(c0, rows, w):
    t = c0 + lax.broadcasted_iota(jnp.int32, (rows, 1), 0)
    return jnp.minimum(t + 1, w).astype(F32)


def _pool_fwd(proj, w_pool, scale, B, S):
    CH = _pool_rows(S)

    def body(hp_ref, wp_ref, sc_ref, o_ref, pad_ref):
        pad_ref[0:POOL_HALO, :] = jnp.zeros((POOL_HALO, POOL_WIDTH), F32)
        pad_ref[POOL_HALO:, :] = hp_ref[...]
        for gi, w in enumerate(POOL_WINDOWS):
            cols = slice(gi * POOL_GROUP_DIM, (gi + 1) * POOL_GROUP_DIM)
            for c in range(S // CH):
                base = POOL_HALO + c * CH
                acc = pad_ref[base:base + CH, cols]
                tok = acc
                for j in range(1, w):
                    acc = acc + pad_ref[base - j:base - j + CH, cols]
                pooled = acc / _pool_count(c * CH, CH, w) - tok
                z = _dot(pooled, wp_ref[gi])
                o_ref[c * CH:(c + 1) * CH, cols] = (z * sc_ref[:, cols]).astype(o_ref.dtype)

    return pl.pallas_call(
        body, out_shape=jax.ShapeDtypeStruct((B * S, POOL_WIDTH), BF16), grid=(B,),
        in_specs=[pl.BlockSpec((S, POOL_WIDTH), lambda b: (b, 0)),
                  pl.BlockSpec(w_pool.shape, lambda b: (0, 0, 0)),
                  pl.BlockSpec((1, POOL_WIDTH), lambda b: (0, 0))],
        out_specs=pl.BlockSpec((S, POOL_WIDTH), lambda b: (b, 0)),
        scratch_shapes=[pltpu.VMEM((S + POOL_HALO, POOL_WIDTH), F32)],
        name="pool_fwd", compiler_params=_params(("parallel",)),
    )(proj, w_pool, scale)


def _pool_bwd(proj, d_ypre, w_pool, scale, B, S):
    CH = _pool_rows(S)

    def body(hp_ref, dy_ref, wp_ref, sc_ref, dhp_ref, dwp_ref, dsc_ref, pad_ref, sc_pad_ref, dp_ref):
        @pl.when(pl.program_id(0) == 0)
        def _():
            dwp_ref[...] = jnp.zeros_like(dwp_ref)
            dsc_ref[...] = jnp.zeros_like(dsc_ref)

        pad_ref[0:POOL_HALO, :] = jnp.zeros((POOL_HALO, POOL_WIDTH), F32)
        pad_ref[POOL_HALO:, :] = hp_ref[...]
        sc_pad_ref[S:, :] = jnp.zeros((POOL_HALO, POOL_WIDTH), F32)
        for gi, w in enumerate(POOL_WINDOWS):
            cols = slice(gi * POOL_GROUP_DIM, (gi + 1) * POOL_GROUP_DIM)
            for c in range(S // CH):
                base = POOL_HALO + c * CH
                rows = slice(c * CH, (c + 1) * CH)
                acc = pad_ref[base:base + CH, cols]
                tok = acc
                for j in range(1, w):
                    acc = acc + pad_ref[base - j:base - j + CH, cols]
                cnt = _pool_count(c * CH, CH, w)
                pooled = acc / cnt - tok
                z = _dot(pooled, wp_ref[gi])
                dy = dy_ref[rows, cols]
                dsc_ref[:, cols] += jnp.sum(dy * z, axis=0, keepdims=True)
                dz = dy * sc_ref[:, cols]
                dwp_ref[gi] += _dot(pooled, dz, "tn")
                dpool = _dot(dz, wp_ref[gi], "nt")
                dp_ref[rows, cols] = dpool
                sc_pad_ref[rows, cols] = dpool / cnt
            for c in range(S // CH):
                rows = slice(c * CH, (c + 1) * CH)
                acc = sc_pad_ref[rows, cols]
                for j in range(1, w):
                    acc = acc + sc_pad_ref[c * CH + j:c * CH + j + CH, cols]
                dhp_ref[rows, cols] = (acc - dp_ref[rows, cols]).astype(dhp_ref.dtype)

    seq = pl.BlockSpec((S, POOL_WIDTH), lambda b: (b, 0))
    return pl.pallas_call(
        body,
        out_shape=(jax.ShapeDtypeStruct((B * S, POOL_WIDTH), BF16),
                   jax.ShapeDtypeStruct(w_pool.shape, F32), jax.ShapeDtypeStruct((1, POOL_WIDTH), F32)),
        grid=(B,),
        in_specs=[seq, seq, pl.BlockSpec(w_pool.shape, lambda b: (0, 0, 0)),
                  pl.BlockSpec((1, POOL_WIDTH), lambda b: (0, 0))],
        out_specs=(seq, pl.BlockSpec(w_pool.shape, lambda b: (0, 0, 0)),
                   pl.BlockSpec((1, POOL_WIDTH), lambda b: (0, 0))),
        scratch_shapes=[pltpu.VMEM((S + POOL_HALO, POOL_WIDTH), F32),
                        pltpu.VMEM((S + POOL_HALO, POOL_WIDTH), F32),
                        pltpu.VMEM((S, POOL_WIDTH), F32)],
        name="pool_bwd", compiler_params=_params(("arbitrary",)),
    )(proj, d_ypre, w_pool, scale)


def _ret_tables(S):
    half = RET_QK_DIM // 2
    inv = ROPE_BASE ** (-jnp.arange(half, dtype=F32) / half)
    ang = jnp.arange(S, dtype=F32)[:, None] * inv[None, :]
    cos, sin = jnp.cos(ang), jnp.sin(ang)
    cos_full = jnp.concatenate([cos, cos], axis=-1)
    sin_signed = jnp.concatenate([-sin, sin], axis=-1)
    C = RET_CHUNK
    lg = jnp.log1p(-jnp.exp2(-5.0 - jnp.arange(RET_HEADS, dtype=F32)))[:, None, None]
    idx = jnp.arange(C, dtype=F32)
    rel = idx[:, None] - idx[None, :]
    decay = jnp.where(rel >= 0, jnp.exp(jnp.maximum(rel, 0.0) * lg), 0.0)
    q_decay = jnp.broadcast_to(jnp.exp((idx + 1.0)[None, :, None] * lg), (RET_HEADS, C, RET_QK_DIM))
    k_decay = jnp.broadcast_to(jnp.exp((C - 1.0 - idx)[None, :, None] * lg), (RET_HEADS, C, RET_QK_DIM))
    c_decay = jnp.broadcast_to(jnp.exp(C * lg), (RET_HEADS, 1, RET_V_DIM))
    return cos_full, sin_signed, decay, q_decay, k_decay, c_decay


def _rope(x, cos_full, sin_signed):
    return x * cos_full + pltpu.roll(x, RET_QK_DIM // 2, axis=1) * sin_signed


def _rope_t(dy, cos_full, sin_signed):
    return dy * cos_full + pltpu.roll(dy * sin_signed, RET_QK_DIM // 2, axis=1)


RET_COLS = 512


def _ret_specs(N, chunk_of):
    C = RET_CHUNK

    def rows(width, col=0):
        return pl.BlockSpec((C, width), lambda b, i: (b * N + chunk_of(i), col))

    def whole(shape):
        return pl.BlockSpec(shape, lambda b, i: (0,) * len(shape))

    wide = RET_HEADS * RET_V_DIM
    return dict(
        q=rows(RET_COLS, COL_Q // RET_COLS), k=rows(RET_COLS, COL_K // RET_COLS),
        v=[rows(RET_COLS, COL_V // RET_COLS + j) for j in range(2)],
        gr=[rows(RET_COLS, COL_GR // RET_COLS + j) for j in range(2)],
        table=pl.BlockSpec((C, RET_QK_DIM), lambda b, i: (chunk_of(i), 0)),
        decay=whole((RET_HEADS, C, C)), qd=whole((RET_HEADS, C, RET_QK_DIM)), kd=whole((RET_HEADS, C, RET_QK_DIM)),
        cd=whole((RET_HEADS, 1, RET_V_DIM)), vec=whole((1, wide)), qk_rows=rows(RET_COLS), v_rows=rows(wide),
        state=pl.BlockSpec((None, None, RET_HEADS, RET_QK_DIM, RET_V_DIM), lambda b, i: (b, chunk_of(i), 0, 0, 0)))


def _head_cols(h):
    pair = slice((h % 2) * RET_V_DIM, (h % 2 + 1) * RET_V_DIM)
    return slice(h * RET_QK_DIM, (h + 1) * RET_QK_DIM), h // 2, pair, slice(h * RET_V_DIM, (h + 1) * RET_V_DIM)


def _group_norm(o):
    mu = jnp.mean(o, axis=-1, keepdims=True)
    oc = o - mu
    rstd = lax.rsqrt(jnp.mean(oc * oc, axis=-1, keepdims=True) + EPS)
    return oc * rstd, rstd


def _ret_fwd(proj, g_ret, b_ret, tables, B, S, comm=None):
    N = S // RET_CHUNK
    cos_t, sin_t, decay, q_decay, k_decay, c_decay = tables
    sp = _ret_specs(N, lambda i: i)

    def body(q_ref, k_ref, v0_ref, v1_ref, gr0_ref, gr1_ref, cos_ref, sin_ref, dec_ref, qd_ref, kd_ref, cd_ref,
             g_ref, b_ref, y_ref, rs_ref, r_ref):
        @pl.when(pl.program_id(1) == 0)
        def _():
            r_ref[...] = jnp.zeros_like(r_ref)

        cs, sn = cos_ref[...], sin_ref[...]
        for h in range(RET_HEADS):
            qk, j, pair, wide = _head_cols(h)
            q = _rope(q_ref[:, qk], cs, sn)
            k = _rope(k_ref[:, qk], cs, sn) * (RET_QK_DIM ** -0.5)
            v = (v0_ref, v1_ref)[j][:, pair]
            R = r_ref[h]
            rs_ref[h] = R
            s = _dot(q, k, "nt") * dec_ref[h]
            o = _dot(s, v) + _dot(q * qd_ref[h], R)
            r_ref[h] = cd_ref[h] * R + _dot(k * kd_ref[h], v, "tn")
            on, _ = _group_norm(o)
            gr = (gr0_ref, gr1_ref)[j][:, pair]
            y_ref[:, wide] = (gr * jax.nn.sigmoid(gr) * (on * g_ref[:, wide] + b_ref[:, wide])).astype(y_ref.dtype)

    state = jax.ShapeDtypeStruct((B, N, RET_HEADS, RET_QK_DIM, RET_V_DIM), F32)
    return _pcall(
        body, (proj,) * 6 + (cos_t, sin_t, decay, q_decay, k_decay, c_decay, g_ret, b_ret),
        name="ret_fwd", out_shape=(jax.ShapeDtypeStruct((B * S, RET_HEADS * RET_V_DIM), BF16), state), grid=(B, N),
        in_specs=[sp["q"], sp["k"], *sp["v"], *sp["gr"], sp["table"], sp["table"], sp["decay"], sp["qd"],
                  sp["kd"], sp["cd"], sp["vec"], sp["vec"]],
        out_specs=(sp["v_rows"], sp["state"]),
        scratch_shapes=[pltpu.VMEM((RET_HEADS, RET_QK_DIM, RET_V_DIM), F32)],
        sem=("parallel", "arbitrary"), comm=comm)


def _ret_bwd(proj, states, d_yr, g_ret, b_ret, tables, B, S, comm=None):
    N = S // RET_CHUNK
    cos_t, sin_t, decay, q_decay, k_decay, c_decay = tables
    sp = _ret_specs(N, lambda i: N - 1 - i)
    qk_scale = RET_QK_DIM ** -0.5

    def body(q_ref, k_ref, v0_ref, v1_ref, gr0_ref, gr1_ref, dy_ref, rs_ref, cos_ref, sin_ref, dec_ref, qd_ref,
             kd_ref, cd_ref, g_ref, b_ref, dq_ref, dk_ref, dv_ref, dgr_ref, dg_ref, db_ref, dr_ref):
        @pl.when((pl.program_id(0) == 0) & (pl.program_id(1) == 0))
        def _():
            dg_ref[...] = jnp.zeros_like(dg_ref)
            db_ref[...] = jnp.zeros_like(db_ref)

        @pl.when(pl.program_id(1) == 0)
        def _():
            dr_ref[...] = jnp.zeros_like(dr_ref)

        cs, sn = cos_ref[...], sin_ref[...]
        for h in range(RET_HEADS):
            qk, j, pair, wide = _head_cols(h)
            q = _rope(q_ref[:, qk], cs, sn)
            k = _rope(k_ref[:, qk], cs, sn) * qk_scale
            v = (v0_ref, v1_ref)[j][:, pair]
            R, dR = rs_ref[h], dr_ref[h]
            dec, qd, kd = dec_ref[h], qd_ref[h], kd_ref[h]
            s = _dot(q, k, "nt") * dec
            o = _dot(s, v) + _dot(q * qd, R)
            on, rstd = _group_norm(o)
            g = g_ref[:, wide]
            oaff = on * g + b_ref[:, wide]
            gr = (gr0_ref, gr1_ref)[j][:, pair]
            sg = jax.nn.sigmoid(gr)
            dy = dy_ref[:, wide]
            dgr_ref[:, wide] = (dy * oaff * (sg * (1.0 + gr * (1.0 - sg)))).astype(dgr_ref.dtype)
            doaff = dy * (gr * sg)
            dg_ref[:, wide] += jnp.sum(doaff * on, axis=0, keepdims=True)
            db_ref[:, wide] += jnp.sum(doaff, axis=0, keepdims=True)
            don = doaff * g
            do = rstd * (don - jnp.mean(don, axis=-1, keepdims=True)
                         - on * jnp.mean(don * on, axis=-1, keepdims=True))
            ds = _dot(do, v, "nt") * dec
            dq = _dot(ds, k) + qd * _dot(do, R, "nt")
            dk = _dot(ds, q, "tn") + kd * _dot(v, dR, "nt")
            dv_ref[:, wide] = (_dot(s, do, "tn") + _dot(k * kd, dR)).astype(dv_ref.dtype)
            dr_ref[h] = cd_ref[h] * dR + _dot(q * qd, do, "tn")
            dq_ref[:, qk] = _rope_t(dq, cs, sn).astype(dq_ref.dtype)
            dk_ref[:, qk] = _rope_t(dk * qk_scale, cs, sn).astype(dk_ref.dtype)

    T = B * S
    qk_shape = jax.ShapeDtypeStruct((T, RET_HEADS * RET_QK_DIM), BF16)
    v_shape = jax.ShapeDtypeStruct((T, RET_HEADS * RET_V_DIM), BF16)
    vec_shape = jax.ShapeDtypeStruct((1, RET_HEADS * RET_V_DIM), F32)
    return _pcall(
        body, (proj,) * 6 + (d_yr, states, cos_t, sin_t, decay, q_decay, k_decay, c_decay, g_ret, b_ret),
        name="ret_bwd", out_shape=(qk_shape, qk_shape, v_shape, v_shape, vec_shape, vec_shape), grid=(B, N),
        in_specs=[sp["q"], sp["k"], *sp["v"], *sp["gr"], sp["v_rows"], sp["state"], sp["table"], sp["table"],
                  sp["decay"], sp["qd"], sp["kd"], sp["cd"], sp["vec"], sp["vec"]],
        out_specs=(sp["qk_rows"], sp["qk_rows"], sp["v_rows"], sp["v_rows"], sp["vec"], sp["vec"]),
        scratch_shapes=[pltpu.VMEM((RET_HEADS, RET_QK_DIM, RET_V_DIM), F32)],
        sem=("arbitrary", "arbitrary"), comm=comm)


def _xa_rows(S):
    return _tile(S, 256)


def _xa_specs(S, M):
    q = pl.BlockSpec((S, XA_HEAD_DIM), lambda b, h: (b, COL_QX // XA_HEAD_DIM + h))
    k = pl.BlockSpec((M, XA_HEAD_DIM), lambda b, h: (b, h))
    v = pl.BlockSpec((M, XA_HEAD_DIM), lambda b, h: (b, XA_HEADS + h))
    o = pl.BlockSpec((S, XA_HEAD_DIM), lambda b, h: (b, h))
    return q, k, v, o


def _softmax_rows(s):
    e = jnp.exp(s - jnp.max(s, axis=-1, keepdims=True))
    return e / jnp.sum(e, axis=-1, keepdims=True)


def _xa_fwd(proj, kv, B, S, M, comm=None):
    CH = _xa_rows(S)
    q_spec, k_spec, v_spec, o_spec = _xa_specs(S, M)

    def body(q_ref, k_ref, v_ref, o_ref):
        def chunk(i, carry):
            rows = pl.ds(pl.multiple_of(i * CH, CH), CH)
            p = _softmax_rows(_dot(q_ref[rows, :], k_ref[...], "nt") * (XA_HEAD_DIM ** -0.5))
            o_ref[rows, :] = _dot(p, v_ref[...]).astype(o_ref.dtype)
            return carry

        lax.fori_loop(0, S // CH, chunk, 0)

    return _pcall(
        body, (proj, kv, kv), name="xattn_fwd", out_shape=jax.ShapeDtypeStruct((B * S, XA_WIDTH), BF16),
        grid=(B, XA_HEADS), in_specs=[q_spec, k_spec, v_spec], out_specs=o_spec,
        sem=("parallel", "parallel"), comm=comm)


def _xa_bwd(proj, kv, d_o, B, S, M, comm=None):
    CH = _xa_rows(S)
    q_spec, k_spec, v_spec, o_spec = _xa_specs(S, M)
    scale = XA_HEAD_DIM ** -0.5

    def body(q_ref, k_ref, v_ref, do_ref, dq_ref, dk_ref, dv_ref):
        dk_ref[...] = jnp.zeros_like(dk_ref)
        dv_ref[...] = jnp.zeros_like(dv_ref)

        def chunk(i, carry):
            rows = pl.ds(pl.multiple_of(i * CH, CH), CH)
            q, do = q_ref[rows, :], do_ref[rows, :]
            p = _softmax_rows(_dot(q, k_ref[...], "nt") * scale)
            dp = _dot(do, v_ref[...], "nt")
            ds = p * (dp - jnp.sum(dp * p, axis=-1, keepdims=True)) * scale
            dq_ref[rows, :] = _dot(ds, k_ref[...]).astype(dq_ref.dtype)
            dk_ref[...] += _dot(ds, q, "tn")
            dv_ref[...] += _dot(p, do, "tn")
            return carry

        lax.fori_loop(0, S // CH, chunk, 0)

    kv_out = pl.BlockSpec((M, XA_HEAD_DIM), lambda b, h: (b, h))
    return _pcall(
        body, (proj, kv, kv, d_o), name="xattn_bwd",
        out_shape=(jax.ShapeDtypeStruct((B * S, XA_WIDTH), BF16), jax.ShapeDtypeStruct((B * M, XA_WIDTH), F32),
                   jax.ShapeDtypeStruct((B * M, XA_WIDTH), F32)),
        grid=(B, XA_HEADS), in_specs=[q_spec, k_spec, v_spec, o_spec], out_specs=(o_spec, kv_out, kv_out),
        sem=("parallel", "parallel"), comm=comm)


def _gate_specs(tm):
    n = COL_GL // D_MODEL
    return [pl.BlockSpec((tm, D_MODEL), lambda i, j=j: (i, n + j)) for j in range(3)]


def _merge_fwd(proj, ys, tm=256, comm=None):
    T = proj.shape[0]
    tm = _tile(T, tm)
    row = pl.BlockSpec((tm, D_MODEL), lambda i: (i, 0))

    def body(g0, g1, g2, y0, y1, y2, o_ref):
        acc = jax.nn.sigmoid(g0[...]) * y0[...]
        acc = acc + jax.nn.sigmoid(g1[...]) * y1[...]
        acc = acc + jax.nn.sigmoid(g2[...]) * y2[...]
        o_ref[...] = acc.astype(o_ref.dtype)

    return _pcall(
        body, (proj, proj, proj, *ys), name="merge_fwd", out_shape=jax.ShapeDtypeStruct((T, D_MODEL), BF16),
        grid=(T // tm,), in_specs=_gate_specs(tm) + [row] * 3, out_specs=row, sem=("parallel",), comm=comm)


def _merge_bwd(proj, ys, d_merged, tm=256, comm=None):
    T = proj.shape[0]
    tm = _tile(T, tm)
    row = pl.BlockSpec((tm, D_MODEL), lambda i: (i, 0))

    def body(g0, g1, g2, y0, y1, y2, dm_ref, dgl_ref, d0, d1, d2):
        dm = dm_ref[...]
        for j, (g_ref, y_ref, d_ref) in enumerate(((g0, y0, d0), (g1, y1, d1), (g2, y2, d2))):
            sg = jax.nn.sigmoid(g_ref[...])
            d_ref[...] = (dm * sg).astype(d_ref.dtype)
            dgl_ref[:, j * D_MODEL:(j + 1) * D_MODEL] = (dm * y_ref[...] * sg * (1.0 - sg)).astype(dgl_ref.dtype)

    dy = jax.ShapeDtypeStruct((T, D_MODEL), BF16)
    return _pcall(
        body, (proj, proj, proj, *ys, d_merged), name="merge_bwd",
        out_shape=(jax.ShapeDtypeStruct((T, 3 * D_MODEL), BF16), dy, dy, dy), grid=(T // tm,),
        in_specs=_gate_specs(tm) + [row] * 4,
        out_specs=(pl.BlockSpec((tm, 3 * D_MODEL), lambda i: (i, 0)), row, row, row),
        sem=("parallel",), comm=comm)


def _gelu(x):
    return 0.5 * x * (1.0 + jnp.tanh(GELU_C * (x + GELU_A * x * x * x)))


def _gelu_grad(x):
    t = jnp.tanh(GELU_C * (x + GELU_A * x * x * x))
    return 0.5 * (1.0 + t) + 0.5 * x * (1.0 - t * t) * GELU_C * (1.0 + 3.0 * GELU_A * x * x)


def _shift_down(x, prev, n):
    rows = x.shape[0]
    r = lax.broadcasted_iota(jnp.int32, (8, 1), 0)
    rolled = pltpu.roll(x, n, axis=0)
    head = rolled[0:8]
    for j in range(n):
        head = jnp.where(r == j, prev[8 - n + j:8 - n + j + 1, :], head)
    return head if rows == 8 else jnp.concatenate([head, rolled[8:]], axis=0)


def _shift_up(x, nxt, n):
    rows = x.shape[0]
    r = lax.broadcasted_iota(jnp.int32, (8, 1), 0)
    rolled = pltpu.roll(x, rows - n, axis=0)
    tail = rolled[rows - 8:]
    for j in range(n):
        tail = jnp.where(r == 8 - n + j, nxt[j:j + 1, :], tail)
    return jnp.concatenate([rolled[:rows - 8], tail], axis=0)


def _conv(a, prev, cw, cb):
    return _shift_down(a, prev, 2) * cw[0:1, :] + _shift_down(a, prev, 1) * cw[1:2, :] + a * cw[2:3, :] + cb


def _glu_fwd(up, cw, cb, S, tm=256, comm=None):
    T = up.shape[2]
    tm = _tile(S, tm)
    per_seq = S // tm

    def body(ab_ref, prev_ref, cw_ref, cb_ref, u_ref):
        i = pl.program_id(1)
        prev = jnp.where(i % per_seq == 0, 0.0, prev_ref[...])
        ac = _conv(ab_ref[0], prev, cw_ref[...], cb_ref[...])
        u_ref[...] = (_gelu(ac) * ab_ref[1]).astype(u_ref.dtype)

    return _pcall(
        body, (up, up, cw, cb), name="glu_fwd",
        out_shape=jax.ShapeDtypeStruct((FFN_SLABS, T, UP_SHARD), BF16), grid=(FFN_SLABS, T // tm),
        in_specs=[pl.BlockSpec((2, None, tm, UP_SHARD), lambda d, i: (0, d, i, 0)),
                  pl.BlockSpec((None, None, 8, UP_SHARD), lambda d, i: (0, d, jnp.maximum(i * (tm // 8) - 1, 0), 0)),
                  pl.BlockSpec((None, 3, UP_SHARD), lambda d, i: (d, 0, 0)),
                  pl.BlockSpec((None, 1, UP_SHARD), lambda d, i: (d, 0, 0))],
        out_specs=pl.BlockSpec((None, tm, UP_SHARD), lambda d, i: (d, i, 0)),
        sem=("parallel", "parallel"), comm=comm)


def _glu_bwd(up, d_u, cw, cb, S, tm=256, comm=None):
    T = up.shape[2]
    tm = _tile(S, tm)
    per_seq = S // tm
    n_tiles = T // tm
    last8 = tm // 8

    def body(ab_ref, prev_ref, abn_ref, du_ref, dun_ref, cw_ref, cb_ref, dup_ref, dcw_ref, dcb_ref):
        i = pl.program_id(1)

        @pl.when(i == 0)
        def _():
            dcw_ref[...] = jnp.zeros_like(dcw_ref)
            dcb_ref[...] = jnp.zeros_like(dcb_ref)

        cw, cb = cw_ref[...], cb_ref[...]
        a, b = ab_ref[0], ab_ref[1]
        prev = jnp.where(i % per_seq == 0, 0.0, prev_ref[...])
        a2, a1 = _shift_down(a, prev, 2), _shift_down(a, prev, 1)
        ac = a2 * cw[0:1, :] + a1 * cw[1:2, :] + a * cw[2:3, :] + cb
        du = du_ref[...]
        dup_ref[1] = (du * _gelu(ac)).astype(dup_ref.dtype)
        dac = du * b * _gelu_grad(ac)
        dcb_ref[...] += jnp.sum(dac, axis=0, keepdims=True)
        dcw_ref[0:1, :] += jnp.sum(dac * a2, axis=0, keepdims=True)
        dcw_ref[1:2, :] += jnp.sum(dac * a1, axis=0, keepdims=True)
        dcw_ref[2:3, :] += jnp.sum(dac * a, axis=0, keepdims=True)
        an = abn_ref[0]
        acn = _conv(an, a[tm - 8:, :], cw, cb)
        dacn = jnp.where(i % per_seq == per_seq - 1, 0.0, dun_ref[...] * abn_ref[1] * _gelu_grad(acn))
        da = dac * cw[2:3, :] + _shift_up(dac, dacn, 1) * cw[1:2, :] + _shift_up(dac, dacn, 2) * cw[0:1, :]
        dup_ref[0] = da.astype(dup_ref.dtype)

    def nxt(i):
        return jnp.minimum((i + 1) * last8, T // 8 - 1)

    return _pcall(
        body, (up, up, up, d_u, d_u, cw, cb), name="glu_bwd",
        out_shape=(jax.ShapeDtypeStruct((2, FFN_SLABS, T, UP_SHARD), BF16),
                   jax.ShapeDtypeStruct((FFN_SLABS, 3, UP_SHARD), F32),
                   jax.ShapeDtypeStruct((FFN_SLABS, 1, UP_SHARD), F32)),
        grid=(FFN_SLABS, n_tiles),
        in_specs=[pl.BlockSpec((2, None, tm, UP_SHARD), lambda d, i: (0, d, i, 0)),
                  pl.BlockSpec((None, None, 8, UP_SHARD), lambda d, i: (0, d, jnp.maximum(i * last8 - 1, 0), 0)),
                  pl.BlockSpec((2, None, 8, UP_SHARD), lambda d, i: (0, d, nxt(i), 0)),
                  pl.BlockSpec((None, tm, UP_SHARD), lambda d, i: (d, i, 0)),
                  pl.BlockSpec((None, 8, UP_SHARD), lambda d, i: (d, nxt(i), 0)),
                  pl.BlockSpec((None, 3, UP_SHARD), lambda d, i: (d, 0, 0)),
                  pl.BlockSpec((None, 1, UP_SHARD), lambda d, i: (d, 0, 0))],
        out_specs=(pl.BlockSpec((2, None, tm, UP_SHARD), lambda d, i: (0, d, i, 0)),
                   pl.BlockSpec((None, 3, UP_SHARD), lambda d, i: (d, 0, 0)),
                   pl.BlockSpec((None, 1, UP_SHARD), lambda d, i: (d, 0, 0))),
        sem=("parallel", "arbitrary"), comm=comm)


def _mm_up(h2, w_up_t, tm=MM_ROWS):
    T, K = h2.shape
    tm = _tile(T, tm)
    return _matmul(
        "mm_up", "nt", h2, w_up_t, jax.ShapeDtypeStruct((N_DEV, T, UP_SHARD), F32), (N_DEV, T // tm, 1),
        pl.BlockSpec((tm, K), lambda j, i, k: (i, 0)), pl.BlockSpec((None, UP_SHARD, K), lambda j, i, k: (j, 0, 0)),
        pl.BlockSpec((None, tm, UP_SHARD), lambda j, i, k: (j, i, 0)), (tm, UP_SHARD))


def _loss_epilogue(ffn, operands, outputs, first):
    x1_ref, t_ref, g_ref = operands
    dx_ref, dg_ref, loss_ref = outputs

    @pl.when(first)
    def _():
        dg_ref[...] = jnp.zeros_like(dg_ref)
        loss_ref[...] = jnp.zeros_like(loss_ref)

    xv = x1_ref[...] + ffn
    r = lax.rsqrt(jnp.mean(xv * xv, axis=-1, keepdims=True) + EPS)
    xhat = xv * r
    err = xhat * g_ref[...] - t_ref[...]
    loss_ref[...] += (0.5 / D_MODEL) * jnp.sum(err * err)
    dy = err * (1.0 / D_MODEL)
    dg_ref[...] += jnp.sum(dy * xhat, axis=0, keepdims=True)
    dxhat = dy * g_ref[...]
    dx_ref[...] = r * (dxhat - xhat * jnp.mean(dxhat * xhat, axis=-1, keepdims=True))


def _mm_down_loss(u, w_down, x1, target, g_final, tm=MM_ROWS_RES):
    J, T, n = u.shape
    tm = _tile(T, tm)
    row = pl.BlockSpec((tm, D_MODEL), lambda i, d: (i, 0))
    vec = pl.BlockSpec((1, D_MODEL), lambda i, d: (0, 0))
    vec_shape = jax.ShapeDtypeStruct((1, D_MODEL), F32)
    return _matmul(
        "mm_down", "nn", u, w_down, (jax.ShapeDtypeStruct((T, D_MODEL), F32), vec_shape, vec_shape), (T // tm, J),
        pl.BlockSpec((None, tm, n), lambda i, d: (d, i, 0)), pl.BlockSpec((None, n, D_MODEL), lambda i, d: (d, 0, 0)),
        (row, vec, vec), (tm, D_MODEL), [x1, target, g_final], [row, row, vec], epilogue=_loss_epilogue)


def _mm_down_t(dx, w_down, tm=MM_ROWS):
    T = dx.shape[0]
    J, n, _ = w_down.shape
    tm = _tile(T, tm)
    return _matmul(
        "mm_down_t", "nt", dx, w_down, jax.ShapeDtypeStruct((J, T, n), F32), (J, T // tm, 1),
        pl.BlockSpec((tm, D_MODEL), lambda d, i, k: (i, 0)), pl.BlockSpec((None, n, D_MODEL), lambda d, i, k: (d, 0, 0)),
        pl.BlockSpec((None, tm, n), lambda d, i, k: (d, i, 0)), (tm, n))


def _mm_dw_down(u, dx, tk=MM_TOKENS):
    J, T, n = u.shape
    tk = _tile(T, tk)
    return _matmul(
        "mm_dw_down", "tn", u, dx, jax.ShapeDtypeStruct((J, n, D_MODEL), BF16), (J, T // tk),
        pl.BlockSpec((None, tk, n), lambda d, k: (d, k, 0)), pl.BlockSpec((tk, D_MODEL), lambda d, k: (k, 0)),
        pl.BlockSpec((None, n, D_MODEL), lambda d, k: (d, 0, 0)), (n, D_MODEL))


def _mm_dw_up(h2, d_up, tk=MM_TOKENS):
    T, K = h2.shape
    tk = _tile(T, tk)
    return _matmul(
        "mm_dw_up", "tn", d_up, h2, jax.ShapeDtypeStruct((N_DEV, UP_SHARD, K), BF16), (N_DEV, T // tk),
        pl.BlockSpec((None, tk, UP_SHARD), lambda j, k: (j, k, 0)), pl.BlockSpec((tk, K), lambda j, k: (k, 0)),
        pl.BlockSpec((None, UP_SHARD, K), lambda j, k: (j, 0, 0)), (UP_SHARD, K))


def _mm_up_t(d_up, w_up_t, rms, tm=MM_ROWS_RES, comm=None):
    J, T, n = d_up.shape
    K = w_up_t.shape[2]
    tm = _tile(T, tm)
    fused = _rms_bwd_fused(T, K, tm, rms)
    return _matmul(
        "mm_up_t", "nn", d_up, w_up_t, fused.pop("out_shape"), (T // tm, J),
        pl.BlockSpec((None, tm, n), lambda i, j: (j, i, 0)), pl.BlockSpec((None, n, K), lambda i, j: (j, 0, 0)),
        fused.pop("o_spec"), (tm, K), comm=comm, **fused)


def _cast_shards(shards):
    def body(*refs):
        n = len(refs) // 2
        for src, dst in zip(refs[:n], refs[n:]):
            dst[...] = src[...].astype(dst.dtype)

    return pl.pallas_call(
        body, out_shape=[jax.ShapeDtypeStruct(s.shape, BF16) for s in shards], name="cast_shards",
        compiler_params=pltpu.CompilerParams(vmem_limit_bytes=VMEM_LIMIT),
    )(*shards)


def _adamw(w, g, m, v):
    m = ADAM_B1 * m + (1.0 - ADAM_B1) * g
    v = ADAM_B2 * v + (1.0 - ADAM_B2) * (g * g)
    m_hat = m / (1.0 - ADAM_B1 ** ADAM_STEP)
    v_hat = v / (1.0 - ADAM_B2 ** ADAM_STEP)
    delta = -ADAM_LR * (m_hat / (jnp.sqrt(v_hat) + ADAM_EPS) + ADAM_WD * w)
    return delta, m, v


def _sum_parts(p_ref):
    g = p_ref[0].astype(F32)
    for d in range(1, N_DEV):
        g = g + p_ref[d].astype(F32)
    return g


def _reduce_adam(name, parts, w, m, v, tr=128):
    R, Cn = w.shape
    by_rows = sum(p.shape[1] for p in parts) == R and len(parts) > 1
    tr = math.gcd(tr, *[p.shape[1] for p in parts])
    n_tiles = [p.shape[1] // tr for p in parts]
    first = [sum(n_tiles[:j]) for j in range(len(parts))] if by_rows else [0] * len(parts)

    def body(*refs):
        p_refs = refs[:len(parts)]
        w_ref, m_ref, v_ref, g_out, d_out, m_out, v_out = refs[len(parts):]

        def update(p_ref):
            g = _sum_parts(p_ref)
            delta, m_new, v_new = _adamw(w_ref[...], g, m_ref[...], v_ref[...])
            g_out[...] = g
            d_out[...] = delta
            m_out[...] = m_new
            v_out[...] = v_new

        if len(parts) == 1:
            update(p_refs[0])
        elif by_rows:
            i = pl.program_id(0)
            for p_ref, t0, n in zip(p_refs, first, n_tiles):
                pl.when((i >= t0) & (i < t0 + n))(functools.partial(update, p_ref))
        else:
            c = lax.axis_index("c")
            for side, p_ref in enumerate(p_refs):
                pl.when(c == side)(functools.partial(update, p_ref))

    def part_spec(t0, n):
        return pl.BlockSpec((N_DEV, tr, Cn), lambda i: (0, jnp.clip(i - t0, 0, n - 1), 0))

    row = pl.BlockSpec((tr, Cn), lambda i: (i, 0))
    shape = jax.ShapeDtypeStruct((R, Cn), F32)
    return pl.pallas_call(
        body, out_shape=(shape,) * 4, grid=(R // tr,),
        in_specs=[part_spec(t0, n) for t0, n in zip(first, n_tiles)] + [row, row, row],
        out_specs=(row,) * 4, name=name, compiler_params=_params(("parallel",)),
    )(*parts, w, m, v)


def _small_adam(name, gathered, params):
    n_g, n_p = len(gathered), len(params)

    def body(*refs):
        g_refs = refs[:n_g]
        wmv = refs[n_g:n_g + 3 * n_p]
        sums = refs[n_g + 3 * n_p:2 * n_g + 3 * n_p]
        upd = refs[2 * n_g + 3 * n_p:]
        for j in range(n_g):
            g = _sum_parts(g_refs[j])
            sums[j][...] = g
            if j < n_p:
                w_ref, m_ref, v_ref = wmv[3 * j:3 * j + 3]
                delta, m_new, v_new = _adamw(w_ref[...], g, m_ref[...], v_ref[...])
                upd[3 * j][...] = delta
                upd[3 * j + 1][...] = m_new
                upd[3 * j + 2][...] = v_new

    flat = [a for wmv in params for a in wmv]
    out_shape = [jax.ShapeDtypeStruct(g.shape[1:], F32) for g in gathered]
    out_shape += [jax.ShapeDtypeStruct(a.shape, F32) for a in flat]
    res = pl.pallas_call(body, out_shape=out_shape, name=name)(*gathered, *flat)
    return res[:n_g], [tuple(res[n_g + 3 * j:n_g + 3 * j + 3]) for j in range(n_p)]


def _adam_only(name, g, w, m, v):
    def body(g_ref, w_ref, m_ref, v_ref, d_out, m_out, v_out):
        delta, m_new, v_new = _adamw(w_ref[...], g_ref[...], m_ref[...], v_ref[...])
        d_out[...] = delta
        m_out[...] = m_new
        v_out[...] = v_new

    shape = jax.ShapeDtypeStruct(w.shape, F32)
    return pl.pallas_call(body, out_shape=(shape,) * 3, name=name)(g, w, m, v)


def kernel(x, mem, g_mix, w_in, w_pool, pool_scale, w_a, g_ret, b_ret, w_r, g_mem, w_mem_kv, w_c, w_out, g_ffn, w_up, conv_w, conv_b, w_down, g_final, loss_target, m_g_mix, m_w_in, m_w_pool, m_pool_scale, m_w_a, m_g_ret, m_b_ret, m_w_r, m_g_mem, m_w_mem_kv, m_w_c, m_w_out, m_g_ffn, m_w_up, m_conv_w, m_conv_b, m_w_down, m_g_final, v_g_mix, v_w_in, v_w_pool, v_pool_scale, v_w_a, v_g_ret, v_b_ret, v_w_r, v_g_mem, v_w_mem_kv, v_w_c, v_w_out, v_g_ffn, v_w_up, v_conv_w, v_conv_b, v_w_down, v_g_final):
    B, S, _ = x.shape
    M = mem.shape[1]
    T = B * S
    me = _my_index()
    x2d = x.reshape(T, D_MODEL)
    mem2d = mem.reshape(B * M, D_MODEL)
    tgt2d = loss_target.reshape(T, D_MODEL)
    g_final2 = g_final.reshape(1, D_MODEL)

    big = dict(w_in=w_in[0], w_a=w_a[0], w_r=w_r[0], w_mem_kv=w_mem_kv[0], w_c=w_c[0], w_out=w_out[0],
               w_up=w_up[0].T, w_down=w_down[0])
    names = list(big)
    cast = dict(zip(names, _cast_shards([big[n] for n in names])))
    cb = conv_b[0].reshape(FFN_SLABS, 1, UP_SHARD)
    wp = w_pool[0]
    tables = _ret_tables(S)

    h = _rms_fwd("rms_mix", x2d, g_mix)
    proj, Win = _mm_in_gather(h, cast["w_in"])[0]
    early = ("w_a", "w_r", "w_mem_kv", "w_c", "w_out")
    (yr, ret_states), landed = _ret_fwd(proj, g_ret, b_ret, tables, B, S,
                                        comm=_Gather([cast[n] for n in early] + [conv_w[0]]))
    W = dict(zip(early, landed))
    cw_full = landed[-1].transpose(1, 0, 2).reshape(3, FFN_HIDDEN)
    cw = cw_full.reshape(3, FFN_SLABS, UP_SHARD).transpose(1, 0, 2)
    Wa = W["w_a"].transpose(1, 0, 2).reshape(POOL_WIDTH, D_MODEL)
    Wc = W["w_c"].transpose(1, 0, 2).reshape(XA_WIDTH, D_MODEL)
    Wr = W["w_r"].reshape(D_MODEL, D_MODEL)
    Wkv = W["w_mem_kv"].reshape(D_MODEL, D_MODEL)
    Wout = W["w_out"].reshape(D_MODEL, D_MODEL)
    ypre = _pool_fwd(proj, wp, pool_scale, B, S)
    y_pool = _mm_rows("mm_a", ypre, Wa, BF16)
    y_ret = _mm_rows("mm_r", yr, Wr, BF16)
    mem_n = _rms_fwd("rms_mem", mem2d, g_mem)
    kv = _mm_rows("mm_kv", mem_n, Wkv)
    half = D_MODEL // 2
    o_mem, (Wup_lo,) = _xa_fwd(proj, kv, B, S, M, comm=_Gather([cast["w_up"][:, :half]]))
    y_mem = _mm_rows("mm_c", o_mem, Wc, BF16)
    ys = (y_pool, y_ret, y_mem)
    merged, (Wup_hi,) = _merge_fwd(proj, ys, comm=_Gather([cast["w_up"][:, half:]]))
    Wup = jnp.concatenate([Wup_lo, Wup_hi], axis=2)
    x1, h2 = _mm_residual_rms("mm_out", merged, Wout, x2d, g_ffn)
    up = _mm_up(h2, Wup).reshape(2, FFN_SLABS, T, UP_SHARD)
    u, (Wdown,) = _glu_fwd(up, cw, cb, S, comm=_Gather([cast["w_down"]]))
    Wdown = Wdown.reshape(FFN_SLABS, UP_SHARD, D_MODEL)

    dx2, dg_final, loss_part = _mm_down_loss(u, Wdown, x1, tgt2d, g_final2)
    received = {}
    d_u = _mm_down_t(dx2, Wdown)
    dW_down = _mm_dw_down(u, dx2)
    (d_up, d_cw, d_cb), (received["w_down"],) = _glu_bwd(
        up, d_u, cw, cb, S, comm=_Exchange([dW_down.reshape(N_DEV, -1, D_MODEL)]))
    d_up = d_up.reshape(N_DEV, T, UP_SHARD)
    dW_up = _mm_dw_up(h2, d_up)
    (dx1, dg_ffn), (up_c0,) = _mm_up_t(d_up, Wup, (x1, g_ffn, dx2), comm=_ExchangeTo([dW_up], 0))
    d_merged = _mm_rows("mm_out_t", dx1, Wout, kind="nt")
    dW_out = _mm_tn("mm_dw_out", merged, dx1, BF16)
    (d_gl, d_y_pool, d_y_ret, d_y_mem), (received["w_out"],) = _merge_bwd(
        proj, ys, d_merged, comm=_Exchange([dW_out.reshape(N_DEV, -1, D_MODEL)]))
    dW_c = _mm_tn("mm_dw_c", o_mem, d_y_mem, BF16)
    d_o_mem = _mm_rows("mm_c_t", d_y_mem, Wc, kind="nt")
    (d_qx, d_kmem, d_vmem), (up_c1,) = _xa_bwd(proj, kv, d_o_mem, B, S, M, comm=_ExchangeTo([dW_up], 1))
    received["w_up"] = [up_c0, up_c1]
    d_kv = jnp.concatenate([d_kmem, d_vmem], axis=1)
    dW_kv = _mm_tn("mm_dw_kv", mem_n, d_kv, BF16)
    d_mem_n = _mm_rows("mm_kv_t", d_kv, Wkv, kind="nt")
    dg_mem = _rms_bwd("rms_mem_bwd", mem2d, g_mem, d_mem_n, None)
    dW_a = _mm_tn("mm_dw_a", ypre, d_y_pool, BF16)
    d_ypre = _mm_rows("mm_a_t", d_y_pool, Wa, kind="nt")
    d_hp, dw_pool, d_scale = _pool_bwd(proj, d_ypre, wp, pool_scale, B, S)
    dW_r = _mm_tn("mm_dw_r", yr, d_y_ret, BF16)
    d_yr = _mm_rows("mm_r_t", d_y_ret, Wr, kind="nt")
    (d_q, d_k, d_v, d_gr, dg_ret, db_ret), landed = _ret_bwd(
        proj, ret_states, d_yr, g_ret, b_ret, tables, B, S,
        comm=_Exchange([dW_a.reshape(POOL_WIDTH, N_DEV, -1).transpose(1, 0, 2), dW_r.reshape(N_DEV, -1, D_MODEL),
                        dW_c.reshape(XA_WIDTH, N_DEV, -1).transpose(1, 0, 2), dW_kv.reshape(N_DEV, -1, D_MODEL)]))
    received["w_a"], received["w_r"], received["w_c"], received["w_mem_kv"] = landed
    small_names = ["w_pool", "pool_scale", "g_ret", "b_ret", "g_mem", "g_ffn", "conv_b", "g_final"]
    small_grads = [dw_pool, d_scale, dg_ret, db_ret, dg_mem, dg_ffn, d_cb.reshape(1, FFN_HIDDEN), dg_final,
                   d_cw.transpose(1, 0, 2).reshape(3, FFN_HIDDEN), loss_part]
    d_proj = jnp.concatenate([d_hp, d_q, d_k, d_v, d_gr, d_qx, d_gl], axis=1)
    dW_in0, small_all = _mm_tn_slab("mm_dw_in0", h[:, :W_IN_FIRST_ROWS], d_proj, IN_SHARD, BF16,
                                    comm=_Exchange([], whole=small_grads))
    dW_in1, (in0,) = _mm_tn_slab("mm_dw_in1", h[:, W_IN_FIRST_ROWS:], d_proj, IN_SHARD, BF16,
                                 comm=_Exchange([dW_in0]))
    (grad_x, dg_mix), (in1,) = _mm_cols_slab_t("mm_in_t", d_proj, Win, (x2d, g_mix, dx1), comm=_Exchange([dW_in1]))
    received["w_in"] = [in0, in1]
    (g_mix_all,) = _comm_call("gather_g_mix", _Exchange([], whole=[dg_mix]))

    args = dict(g_mix=g_mix, w_in=w_in, w_pool=w_pool, pool_scale=pool_scale, w_a=w_a, g_ret=g_ret, b_ret=b_ret,
                w_r=w_r, g_mem=g_mem, w_mem_kv=w_mem_kv, w_c=w_c, w_out=w_out, g_ffn=g_ffn, w_up=w_up,
                conv_w=conv_w, conv_b=conv_b, w_down=w_down, g_final=g_final)
    m_in = dict(g_mix=m_g_mix, w_in=m_w_in, w_pool=m_w_pool, pool_scale=m_pool_scale, w_a=m_w_a, g_ret=m_g_ret,
                b_ret=m_b_ret, w_r=m_w_r, g_mem=m_g_mem, w_mem_kv=m_w_mem_kv, w_c=m_w_c, w_out=m_w_out,
                g_ffn=m_g_ffn, w_up=m_w_up, conv_w=m_conv_w, conv_b=m_conv_b, w_down=m_w_down, g_final=m_g_final)
    v_in = dict(g_mix=v_g_mix, w_in=v_w_in, w_pool=v_w_pool, pool_scale=v_pool_scale, w_a=v_w_a, g_ret=v_g_ret,
                b_ret=v_b_ret, w_r=v_w_r, g_mem=v_g_mem, w_mem_kv=v_w_mem_kv, w_c=v_w_c, w_out=v_w_out,
                g_ffn=v_g_ffn, w_up=v_w_up, conv_w=v_conv_w, conv_b=v_conv_b, w_down=v_w_down, g_final=v_g_final)

    grads, deltas, new_m, new_v = {}, {}, {}, {}
    for n in names:
        parts = received[n] if isinstance(received[n], list) else [received[n]]
        flip = (lambda a: a.T) if n == "w_up" else (lambda a: a)
        outs = _reduce_adam("adam_" + n, parts, big[n], flip(m_in[n][0]), flip(v_in[n][0]))
        for store, val in zip((grads, deltas, new_m, new_v), outs):
            store[n] = flip(val)[None]

    def as_small(a):
        return a.reshape(a.shape[-3:]) if a.ndim > 2 else a.reshape(1, -1)

    def small_update(call_name, param_names, gathered):
        params = [tuple(as_small(d[n]) for d in (args, m_in, v_in)) for n in param_names]
        sums, updates = _small_adam(call_name, gathered, params)
        for n, g, (d_, m_, v_) in zip(param_names, sums, updates):
            shape = args[n].shape
            grads[n], deltas[n], new_m[n], new_v[n] = (a.reshape(shape) for a in (g, d_, m_, v_))
        return sums[len(param_names):]

    g_cw_full, loss_row = small_update("adam_small", small_names, small_all)
    loss = loss_row[0, 0]
    small_update("adam_g_mix", ["g_mix"], [g_mix_all])

    shard_cols = FFN_HIDDEN // N_DEV
    g_cw = lax.dynamic_slice_in_dim(g_cw_full, me * shard_cols, shard_cols, axis=1)
    d_, m_, v_ = _adam_only("adam_conv_w", g_cw, conv_w[0], m_conv_w[0], v_conv_w[0])
    grads["conv_w"], deltas["conv_w"], new_m["conv_w"], new_v["conv_w"] = g_cw[None], d_[None], m_[None], v_[None]

    order = ["g_mix", "w_in", "w_pool", "pool_scale", "w_a", "g_ret", "b_ret", "w_r", "g_mem", "w_mem_kv", "w_c",
             "w_out", "g_ffn", "w_up", "conv_w", "conv_b", "w_down", "g_final"]
    return (loss, grad_x.reshape(B, S, D_MODEL), *[grads[n] for n in order], *[deltas[n] for n in order],
            *[new_m[n] for n in order], *[new_v[n] for n in order])
```

```python
import functools
import math

import jax
import jax.numpy as jnp
from jax import lax
from jax.experimental import pallas as pl
from jax.experimental.pallas import tpu as pltpu

F32 = jnp.float32
BF16 = jnp.bfloat16

N_DEV = 8
D_MODEL = 1024
POOL_WINDOWS = (2, 4, 8, 16)
POOL_GROUP_DIM = 128
POOL_WIDTH = 512
POOL_HALO = 16
RET_HEADS = 4
RET_QK_DIM = 128
RET_V_DIM = 256
RET_CHUNK = 128
ROPE_BASE = 10000.0
XA_HEADS = 4
XA_HEAD_DIM = 128
XA_WIDTH = 512
IN_WIDTH = 7168
IN_SHARD = IN_WIDTH // N_DEV
FFN_HIDDEN = 2816
UP_SHARD = 2 * FFN_HIDDEN // N_DEV
FFN_SLABS = FFN_HIDDEN // UP_SHARD
EPS = 1e-6
ADAM_LR = 0.001
ADAM_B1 = 0.9
ADAM_B2 = 0.999
ADAM_EPS = 1e-08
ADAM_WD = 0.01
ADAM_STEP = 10
GELU_C = math.sqrt(2.0 / math.pi)
GELU_A = 0.044715
VMEM_LIMIT = 56 * 1024 * 1024
MM_ROWS = 2048
MM_ROWS_RES = 1024
MM_TOKENS = 2048
W_IN_FIRST_ROWS = 384
MESH = pl.DeviceIdType.MESH

COL_Q, COL_K, COL_V, COL_GR, COL_QX, COL_GL = 512, 1024, 1536, 2560, 3584, 4096

_DIMS = {
    "nn": (((1,), (0,)), ((), ())),
    "nt": (((1,), (1,)), ((), ())),
    "tn": (((0,), (0,)), ((), ())),
}


def _dot(a, b, kind="nn"):
    return lax.dot_general(a.astype(BF16), b.astype(BF16), _DIMS[kind], preferred_element_type=F32)


def _params(sem, vmem=VMEM_LIMIT):
    return pltpu.CompilerParams(dimension_semantics=sem, vmem_limit_bytes=vmem)


def _tile(n, pref):
    t = min(n, pref)
    while n % t:
        t //= 2
    return t


def _mesh_pos():
    return lax.axis_index("x"), lax.axis_index("y"), lax.axis_index("c")


def _dev_index(x, y, c):
    return 4 * x + 2 * y + c


def _my_index():
    return _dev_index(*_mesh_pos())


def _remote(src, dst, send_sems, recv_sems, s, to):
    return pltpu.make_async_remote_copy(src_ref=src, dst_ref=dst, send_sem=send_sems.at[s], recv_sem=recv_sems.at[s],
                                        device_id=to, device_id_type=MESH)


class _Gather:
    def __init__(self, shards):
        self.inputs = list(shards)
        self.out_shapes = [jax.ShapeDtypeStruct((N_DEV,) + s.shape, s.dtype) for s in shards]
        n = len(shards)
        self.sem_shapes = [pltpu.SemaphoreType.DMA((7 * n,)), pltpu.SemaphoreType.DMA((7 * n,)),
                           pltpu.SemaphoreType.DMA((n,))]

    def _places(self):
        x, y, c = _mesh_pos()
        return (x, y, c), (x, y, 1 - c), [(1 - x, y), (x, 1 - y), (1 - x, 1 - y)]

    def _local(self, src, dst, sems):
        me = _my_index()
        return [pltpu.make_async_copy(src[w], dst[w].at[me], sems[2].at[w]) for w in range(len(src))]

    def start(self, src, dst, sems):
        me, sib, chips = self._places()
        for cp in self._local(src, dst, sems):
            cp.start()
        for w in range(len(src)):
            land = dst[w].at[_dev_index(*me)]
            _remote(src[w], land, sems[0], sems[1], 7 * w, sib).start()
            for j, chip in enumerate(chips):
                _remote(src[w], land, sems[0], sems[1], 7 * w + 1 + j, (*chip, me[2])).start()

    def middle(self, src, dst, sems):
        me, sib, chips = self._places()
        for j, chip in enumerate(chips):
            for w in range(len(src)):
                block = dst[w].at[_dev_index(*chip, me[2])]
                _remote(src[w], block, sems[0], sems[1], 7 * w + 1 + j, me).wait_recv()
                _remote(block, block, sems[0], sems[1], 7 * w + 4 + j, sib).start()

    def finish(self, src, dst, sems):
        me, sib, chips = self._places()
        n = len(src)
        for w in range(n):
            _remote(src[w], dst[w].at[_dev_index(*sib)], sems[0], sems[1], 7 * w, me).wait_recv()
            for j, chip in enumerate(chips):
                block = dst[w].at[_dev_index(*chip, sib[2])]
                _remote(block, block, sems[0], sems[1], 7 * w + 4 + j, me).wait_recv()
            for k in range(7):
                _remote(src[w], dst[w].at[0], sems[0], sems[1], 7 * w + k, me).wait_send()
        for cp in self._local(src, dst, sems):
            cp.wait()


class _Exchange:
    def __init__(self, partials, whole=()):
        self.n_part = len(partials)
        self.inputs = list(partials) + list(whole)
        self.out_shapes = [jax.ShapeDtypeStruct(p.shape, p.dtype) for p in partials]
        self.out_shapes += [jax.ShapeDtypeStruct((N_DEV,) + a.shape, a.dtype) for a in whole]
        n = len(self.inputs)
        self.sem_shapes = [pltpu.SemaphoreType.DMA((7 * n,)), pltpu.SemaphoreType.DMA((7 * n,)),
                           pltpu.SemaphoreType.DMA((n,))]

    def _peer(self, k):
        x, y, c = _mesh_pos()
        p = (x ^ ((k >> 2) & 1), y ^ ((k >> 1) & 1), c ^ (k & 1))
        return p, _dev_index(*p)

    def _source(self, src, w, slot):
        return src[w].at[slot] if w < self.n_part else src[w]

    def _local(self, src, dst, sems):
        me = _my_index()
        return [pltpu.make_async_copy(self._source(src, w, me), dst[w].at[me], sems[2].at[w])
                for w in range(len(src))]

    def start(self, src, dst, sems):
        me = _my_index()
        for cp in self._local(src, dst, sems):
            cp.start()
        for k in range(1, N_DEV):
            peer, peer_idx = self._peer(k)
            for w in range(len(src)):
                _remote(self._source(src, w, peer_idx), dst[w].at[me], sems[0], sems[1], 7 * w + k - 1, peer).start()

    def finish(self, src, dst, sems):
        for k in range(1, N_DEV):
            peer, peer_idx = self._peer(k)
            for w in range(len(src)):
                cp = _remote(self._source(src, w, peer_idx), dst[w].at[peer_idx], sems[0], sems[1], 7 * w + k - 1, peer)
                cp.wait_send()
                cp.wait_recv()
        for cp in self._local(src, dst, sems):
            cp.wait()


class _ExchangeTo:
    def __init__(self, partials, side):
        self.side = side
        self.inputs = list(partials)
        self.out_shapes = [jax.ShapeDtypeStruct(p.shape, p.dtype) for p in partials]
        n = len(partials)
        self.sem_shapes = [pltpu.SemaphoreType.DMA((7 * n,)), pltpu.SemaphoreType.DMA((7 * n,)),
                           pltpu.SemaphoreType.DMA((n,))]

    def _copies(self, src, dst, sems):
        x, y, c = _mesh_pos()
        me = _dev_index(x, y, c)
        receives = c == self.side
        remote = []
        for k in range(1, N_DEV):
            kx, ky, kc = (k >> 2) & 1, (k >> 1) & 1, k & 1
            peer = (x ^ kx, y ^ ky, c ^ kc)
            peer_idx = _dev_index(*peer)
            sends = c == (self.side ^ kc)
            for w in range(len(src)):
                slab = src[w].at[peer_idx]
                s = 7 * w + k - 1
                remote.append((sends, _remote(slab, dst[w].at[me], sems[0], sems[1], s, peer),
                               _remote(slab, dst[w].at[peer_idx], sems[0], sems[1], s, peer)))
        local = [pltpu.make_async_copy(src[w].at[me], dst[w].at[me], sems[2].at[w]) for w in range(len(src))]
        return receives, remote, local

    def start(self, src, dst, sems):
        receives, remote, local = self._copies(src, dst, sems)

        @pl.when(receives)
        def _():
            for cp in local:
                cp.start()

        for sends, send, _ in remote:
            pl.when(sends)(send.start)

    def finish(self, src, dst, sems):
        receives, remote, local = self._copies(src, dst, sems)
        for sends, send, arrive in remote:
            pl.when(sends)(send.wait_send)
            pl.when(receives)(arrive.wait_recv)

        @pl.when(receives)
        def _():
            for cp in local:
                cp.wait()


def _pcall(body, args, *, name, out_shape, grid, in_specs, out_specs, scratch_shapes=(), sem=None, comm=None):
    single = not isinstance(out_shape, (tuple, list))
    outs = [out_shape] if single else list(out_shape)
    ospecs = [out_specs] if single else list(out_specs)
    n_in, n_out, n_scr = len(args), len(outs), len(scratch_shapes)

    def pick(res):
        return res[0] if single else tuple(res[:n_out])

    if comm is None:
        res = pl.pallas_call(
            body, out_shape=outs, grid=grid, in_specs=list(in_specs), out_specs=ospecs,
            scratch_shapes=list(scratch_shapes), name=name, compiler_params=_params(sem),
        )(*args)
        return pick(res), ()

    nci, nco = len(comm.inputs), len(comm.out_shapes)

    def carrier(*refs):
        at = 0
        parts = []
        for size in (n_in, nci, n_out, nco, n_scr, len(comm.sem_shapes)):
            parts.append(refs[at:at + size])
            at += size
        ins, cins, o, couts, scr, sems = parts
        ids = [pl.program_id(a) for a in range(len(grid))]
        first = functools.reduce(jnp.logical_and, [i == 0 for i in ids])
        last = functools.reduce(jnp.logical_and, [i == g - 1 for i, g in zip(ids, grid)])

        body(*ins, *o, *scr)

        @pl.when(first)
        def _():
            comm.start(cins, couts, sems)

        if hasattr(comm, "middle"):
            steps = math.prod(grid)
            at = functools.reduce(lambda lin, ig: lin * ig[1] + ig[0], zip(ids, grid), 0)

            @pl.when(at == min(steps - 1, (3 * steps) // 4))
            def _():
                comm.middle(cins, couts, sems)

        @pl.when(last)
        def _():
            comm.finish(cins, couts, sems)

    hbm = pl.BlockSpec(memory_space=pltpu.HBM)
    res = pl.pallas_call(
        carrier, out_shape=outs + comm.out_shapes, grid=grid, in_specs=list(in_specs) + [hbm] * nci,
        out_specs=ospecs + [hbm] * nco, scratch_shapes=list(scratch_shapes) + comm.sem_shapes, name=name,
        compiler_params=_params(("arbitrary",) * len(grid)),
    )(*args, *comm.inputs)
    return pick(res), tuple(res[n_out:])


def _comm_call(name, comm):
    def body(*refs):
        nci, nco = len(comm.inputs), len(comm.out_shapes)
        cins, couts, sems = refs[:nci], refs[nci:nci + nco], refs[nci + nco:]
        comm.start(cins, couts, sems)
        if hasattr(comm, "middle"):
            comm.middle(cins, couts, sems)
        comm.finish(cins, couts, sems)

    hbm = pl.BlockSpec(memory_space=pltpu.HBM)
    return pl.pallas_call(
        body, out_shape=comm.out_shapes, in_specs=[hbm] * len(comm.inputs), out_specs=[hbm] * len(comm.out_shapes),
        scratch_shapes=comm.sem_shapes, name=name,
    )(*comm.inputs)


def _matmul(name, kind, a, b, out_shape, grid, a_spec, b_spec, o_spec, acc_shape, res=None, res_spec=None,
            comm=None, epilogue=None):
    nk = grid[-1]
    if epilogue is None:
        extra, extra_specs = ([res], [res_spec]) if res is not None else ([], [])
        n_out = 1
    else:
        extra, extra_specs, n_out = list(res), list(res_spec), len(out_shape)
    n_in = 2 + len(extra)

    def body(*refs):
        a_ref, b_ref = refs[0], refs[1]
        extra_refs, out_refs = refs[2:n_in], refs[n_in:n_in + n_out]

        def prod():
            return _dot(a_ref[...], b_ref[...], kind)

        def finish(acc):
            if epilogue is not None:
                ids = [pl.program_id(ax) for ax in range(len(grid) - 1)]
                first = functools.reduce(jnp.logical_and, [i == 0 for i in ids]) if ids else True
                epilogue(acc, extra_refs, out_refs, first)
                return
            if extra_refs:
                acc = acc + extra_refs[0][...]
            out_refs[0][...] = acc.astype(out_refs[0].dtype)

        if nk == 1:
            finish(prod())
        else:
            acc_ref = refs[n_in + n_out]
            k = pl.program_id(len(grid) - 1)

            @pl.when(k == 0)
            def _():
                acc_ref[...] = prod()

            @pl.when(k > 0)
            def _():
                acc_ref[...] += prod()

            @pl.when(k == nk - 1)
            def _():
                finish(acc_ref[...])

    in_specs = [a_spec, b_spec] + extra_specs
    args = (a, b, *extra)
    scratch = [pltpu.VMEM(acc_shape, F32)] if nk > 1 else []
    sem = ("arbitrary",) * len(grid) if epilogue is not None else ("parallel",) * (len(grid) - 1) + ("arbitrary",)
    out, landed = _pcall(body, args, name=name, out_shape=out_shape, grid=grid, in_specs=in_specs,
                         out_specs=o_spec, scratch_shapes=scratch, sem=sem, comm=comm)
    return out if comm is None else (out, landed)


def _mm_rows(name, a, w, out_dtype=F32, res=None, kind="nn", tm=MM_ROWS, comm=None):
    M, K = a.shape
    N = w.shape[1] if kind == "nn" else w.shape[0]
    tm = _tile(M, tm)
    res_spec = pl.BlockSpec((tm, N), lambda i, k: (i, 0)) if res is not None else None
    return _matmul(
        name, kind, a, w, jax.ShapeDtypeStruct((M, N), out_dtype), (M // tm, 1),
        pl.BlockSpec((tm, K), lambda i, k: (i, 0)), pl.BlockSpec(w.shape, lambda i, k: (0, 0)),
        pl.BlockSpec((tm, N), lambda i, k: (i, 0)), (tm, N), res, res_spec, comm)


def _residual_rms_epilogue(y, operands, outputs, first):
    x_ref, g_ref = operands
    x1_ref, h_ref = outputs
    xv = x_ref[...] + y
    x1_ref[...] = xv
    r = lax.rsqrt(jnp.mean(xv * xv, axis=-1, keepdims=True) + EPS)
    h_ref[...] = (xv * r * g_ref[...]).astype(h_ref.dtype)


def _mm_residual_rms(name, a, w, x, g, tm=MM_ROWS_RES):
    M, K = a.shape
    N = w.shape[1]
    tm = _tile(M, tm)
    row = pl.BlockSpec((tm, N), lambda i, k: (i, 0))
    return _matmul(
        name, "nn", a, w, (jax.ShapeDtypeStruct((M, N), F32), jax.ShapeDtypeStruct((M, N), BF16)), (M // tm, 1),
        pl.BlockSpec((tm, K), lambda i, k: (i, 0)), pl.BlockSpec(w.shape, lambda i, k: (0, 0)),
        (row, row), (tm, N), [x, g], [row, pl.BlockSpec((1, N), lambda i, k: (0, 0))],
        epilogue=_residual_rms_epilogue)


def _mm_tn(name, a, b, out_dtype=F32, tk=MM_TOKENS, comm=None):
    T, M = a.shape
    N = b.shape[1]
    tk = _tile(T, tk)
    return _matmul(
        name, "tn", a, b, jax.ShapeDtypeStruct((M, N), out_dtype), (1, T // tk),
        pl.BlockSpec((tk, M), lambda i, k: (k, 0)), pl.BlockSpec((tk, N), lambda i, k: (k, 0)),
        pl.BlockSpec((M, N), lambda i, k: (0, 0)), (M, N), comm=comm)


def _mm_in_gather(h, shard, tm=MM_ROWS, comm=None):
    T, K = h.shape
    n = shard.shape[1]
    tm = _tile(T, tm)
    n_tiles = T // tm
    pair_of_chip_step = {4: 1, 2: 2, 6: 3}

    def slab_of(s):
        x, y, c = _mesh_pos()
        return _dev_index(x ^ ((s >> 2) & 1), y ^ ((s >> 1) & 1), c ^ (s & 1))

    def body(h_ref, shard_ref, proj_ref, win_ref, wbuf, slot_sems, send_sems, recv_sems, local_sem):
        s, i = pl.program_id(0), pl.program_id(1)
        x, y, c = _mesh_pos()
        me, sib = (x, y, c), (x, y, 1 - c)

        def slot_copy(step):
            src = shard_ref if step == 0 else win_ref.at[slab_of(step)]
            return pltpu.make_async_copy(src, wbuf.at[step % 2], slot_sems.at[step % 2])

        def fetch(step):
            if step >= 1:
                block = win_ref.at[slab_of(step)]
                if step == 1:
                    pair = 0
                elif step % 2 == 0:
                    pair = pair_of_chip_step[step]
                else:
                    pair = 3 + pair_of_chip_step[step - 1]
                _remote(block, block, send_sems, recv_sems, pair, me).wait_recv()
                if step % 2 == 0:
                    _remote(block, block, send_sems, recv_sems, 3 + pair, sib).start()
            slot_copy(step).start()

        @pl.when((s == 0) & (i == 0))
        def _():
            land = win_ref.at[_dev_index(*me)]
            pltpu.make_async_copy(shard_ref, land, local_sem).start()
            _remote(shard_ref, land, send_sems, recv_sems, 0, sib).start()
            for step, pair in pair_of_chip_step.items():
                peer = (x ^ ((step >> 2) & 1), y ^ ((step >> 1) & 1), c)
                _remote(shard_ref, land, send_sems, recv_sems, pair, peer).start()
            fetch(0)

        for step in range(N_DEV):
            @pl.when((s == step) & (i == 0))
            def _():
                slot_copy(step).wait()

            if step + 1 < N_DEV:
                @pl.when((s == step) & (i == n_tiles - 1))
                def _():
                    fetch(step + 1)

        proj_ref[...] = _dot(h_ref[...], wbuf[s % 2])

        @pl.when((s == N_DEV - 1) & (i == n_tiles - 1))
        def _():
            for pair in range(7):
                _remote(shard_ref, win_ref.at[0], send_sems, recv_sems, pair, me).wait_send()
            pltpu.make_async_copy(shard_ref, win_ref.at[_dev_index(*me)], local_sem).wait()

    hbm = pl.BlockSpec(memory_space=pltpu.HBM)
    return _pcall(
        body, (h, shard), name="mm_in",
        out_shape=(jax.ShapeDtypeStruct((T, N_DEV * n), F32), jax.ShapeDtypeStruct((N_DEV, K, n), shard.dtype)),
        grid=(N_DEV, n_tiles), in_specs=[pl.BlockSpec((tm, K), lambda s, i: (i, 0)), hbm],
        out_specs=(pl.BlockSpec((tm, n), lambda s, i: (i, slab_of(s))), hbm),
        scratch_shapes=[pltpu.VMEM((2, K, n), shard.dtype), pltpu.SemaphoreType.DMA((2,)),
                        pltpu.SemaphoreType.DMA((7,)), pltpu.SemaphoreType.DMA((7,)), pltpu.SemaphoreType.DMA],
        sem=("arbitrary", "arbitrary"), comm=comm)


def _rms_bwd_epilogue(dh, operands, outputs, first):
    x_ref, g_ref, dres_ref = operands
    dx_ref, dg_ref = outputs
    xv = x_ref[...]
    r = lax.rsqrt(jnp.mean(xv * xv, axis=-1, keepdims=True) + EPS)
    xhat = xv * r

    @pl.when(first)
    def _():
        dg_ref[...] = jnp.zeros_like(dg_ref)

    dg_ref[...] += jnp.sum(dh * xhat, axis=0, keepdims=True)
    dxhat = dh * g_ref[...]
    dx_ref[...] = dres_ref[...] + r * (dxhat - xhat * jnp.mean(dxhat * xhat, axis=-1, keepdims=True))


def _rms_bwd_fused(M, K, tm, rms):
    row = pl.BlockSpec((tm, K), lambda i, j: (i, 0))
    vec = pl.BlockSpec((1, K), lambda i, j: (0, 0))
    x, g, dres = rms
    return dict(res=[x, g, dres], res_spec=[row, vec, row], epilogue=_rms_bwd_epilogue,
                out_shape=(jax.ShapeDtypeStruct((M, K), F32), jax.ShapeDtypeStruct((1, K), F32)), o_spec=(row, vec))


def _mm_cols_slab_t(name, a, w_slabs, rms, tm=MM_ROWS_RES, comm=None):
    M = a.shape[0]
    J, K, n = w_slabs.shape
    tm = _tile(M, tm)
    fused = _rms_bwd_fused(M, K, tm, rms)
    return _matmul(
        name, "nt", a, w_slabs, fused.pop("out_shape"), (M // tm, J),
        pl.BlockSpec((tm, n), lambda i, j: (i, j)), pl.BlockSpec((None, K, n), lambda i, j: (j, 0, 0)),
        fused.pop("o_spec"), (tm, K), comm=comm, **fused)


def _mm_tn_slab(name, a, b, n, out_dtype=F32, tk=MM_TOKENS, comm=None, part=(0, 1)):
    T, M = a.shape
    p, of = part
    M //= of
    J = b.shape[1] // n
    tk = _tile(T, tk)
    return _matmul(
        name, "tn", a, b, jax.ShapeDtypeStruct((J, M, n), out_dtype), (J, T // tk),
        pl.BlockSpec((tk, M), lambda j, k: (k, p)), pl.BlockSpec((tk, n), lambda j, k: (k, j)),
        pl.BlockSpec((None, M, n), lambda j, k: (j, 0, 0)), (M, n), comm=comm)


def _rms_fwd(name, x, g, tm=512):
    T, Dm = x.shape
    tm = _tile(T, tm)

    def body(x_ref, g_ref, h_ref):
        xv = x_ref[...]
        r = lax.rsqrt(jnp.mean(xv * xv, axis=-1, keepdims=True) + EPS)
        h_ref[...] = (xv * r * g_ref[...]).astype(h_ref.dtype)

    return pl.pallas_call(
        body, out_shape=jax.ShapeDtypeStruct((T, Dm), BF16), grid=(T // tm,),
        in_specs=[pl.BlockSpec((tm, Dm), lambda i: (i, 0)), pl.BlockSpec((1, Dm), lambda i: (0, 0))],
        out_specs=pl.BlockSpec((tm, Dm), lambda i: (i, 0)), name=name, compiler_params=_params(("parallel",)),
    )(x, g)


def _rms_bwd(name, x, g, dh, dres, tm=512):
    T, Dm = x.shape
    tm = _tile(T, tm)
    want_dx = dres is not None

    def body(*refs):
        if want_dx:
            x_ref, g_ref, dh_ref, dres_ref, dx_ref, dg_ref = refs
        else:
            x_ref, g_ref, dh_ref, dg_ref = refs
        xv = x_ref[...]
        r = lax.rsqrt(jnp.mean(xv * xv, axis=-1, keepdims=True) + EPS)
        xhat = xv * r
        dhv = dh_ref[...]

        @pl.when(pl.program_id(0) == 0)
        def _():
            dg_ref[...] = jnp.zeros_like(dg_ref)

        dg_ref[...] += jnp.sum(dhv * xhat, axis=0, keepdims=True)
        if want_dx:
            dxhat = dhv * g_ref[...]
            dx_ref[...] = dres_ref[...] + r * (dxhat - xhat * jnp.mean(dxhat * xhat, axis=-1, keepdims=True))

    row = pl.BlockSpec((tm, Dm), lambda i: (i, 0))
    vec = pl.BlockSpec((1, Dm), lambda i: (0, 0))
    if want_dx:
        return pl.pallas_call(
            body, out_shape=(jax.ShapeDtypeStruct((T, Dm), F32), jax.ShapeDtypeStruct((1, Dm), F32)),
            grid=(T // tm,), in_specs=[row, vec, row, row], out_specs=(row, vec), name=name,
            compiler_params=_params(("arbitrary",)),
        )(x, g, dh, dres)
    return pl.pallas_call(
        body, out_shape=jax.ShapeDtypeStruct((1, Dm), F32), grid=(T // tm,), in_specs=[row, vec, row],
        out_specs=vec, name=name, compiler_params=_params(("arbitrary",)),
    )(x, g, dh)


def _pool_rows(S):
    return _tile(S, 256)


def _pool_count(c0, rows, w):
    t = c0 + lax.broadcasted_iota(jnp.int32, (rows, 1), 0)
    return jnp.minimum(t + 1, w).astype(F32)


def _pool_fwd(proj, w_pool, scale, B, S):
    CH = _pool_rows(S)

    def body(hp_ref, wp_ref, sc_ref, o_ref, pad_ref):
        pad_ref[0:POOL_HALO, :] = jnp.zeros((POOL_HALO, POOL_WIDTH), F32)
        pad_ref[POOL_HALO:, :] = hp_ref[...]
        for gi, w in enumerate(POOL_WINDOWS):
            cols = slice(gi * POOL_GROUP_DIM, (gi + 1) * POOL_GROUP_DIM)
            for c in range(S // CH):
                base = POOL_HALO + c * CH
                acc = pad_ref[base:base + CH, cols]
                tok = acc
                for j in range(1, w):
                    acc = acc + pad_ref[base - j:base - j + CH, cols]
                pooled = acc / _pool_count(c * CH, CH, w) - tok
                z = _dot(pooled, wp_ref[gi])
                o_ref[c * CH:(c + 1) * CH, cols] = (z * sc_ref[:, cols]).astype(o_ref.dtype)

    return pl.pallas_call(
        body, out_shape=jax.ShapeDtypeStruct((B * S, POOL_WIDTH), BF16), grid=(B,),
        in_specs=[pl.BlockSpec((S, POOL_WIDTH), lambda b: (b, 0)),
                  pl.BlockSpec(w_pool.shape, lambda b: (0, 0, 0)),
                  pl.BlockSpec((1, POOL_WIDTH), lambda b: (0, 0))],
        out_specs=pl.BlockSpec((S, POOL_WIDTH), lambda b: (b, 0)),
        scratch_shapes=[pltpu.VMEM((S + POOL_HALO, POOL_WIDTH), F32)],
        name="pool_fwd", compiler_params=_params(("parallel",)),
    )(proj, w_pool, scale)


def _pool_bwd(proj, d_ypre, w_pool, scale, B, S):
    CH = _pool_rows(S)

    def body(hp_ref, dy_ref, wp_ref, sc_ref, dhp_ref, dwp_ref, dsc_ref, pad_ref, sc_pad_ref, dp_ref):
        @pl.when(pl.program_id(0) == 0)
        def _():
            dwp_ref[...] = jnp.zeros_like(dwp_ref)
            dsc_ref[...] = jnp.zeros_like(dsc_ref)

        pad_ref[0:POOL_HALO, :] = jnp.zeros((POOL_HALO, POOL_WIDTH), F32)
        pad_ref[POOL_HALO:, :] = hp_ref[...]
        sc_pad_ref[S:, :] = jnp.zeros((POOL_HALO, POOL_WIDTH), F32)
        for gi, w in enumerate(POOL_WINDOWS):
            cols = slice(gi * POOL_GROUP_DIM, (gi + 1) * POOL_GROUP_DIM)
            for c in range(S // CH):
                base = POOL_HALO + c * CH
                rows = slice(c * CH, (c + 1) * CH)
                acc = pad_ref[base:base + CH, cols]
                tok = acc
                for j in range(1, w):
                    acc = acc + pad_ref[base - j:base - j + CH, cols]
                cnt = _pool_count(c * CH, CH, w)
                pooled = acc / cnt - tok
                z = _dot(pooled, wp_ref[gi])
                dy = dy_ref[rows, cols]
                dsc_ref[:, cols] += jnp.sum(dy * z, axis=0, keepdims=True)
                dz = dy * sc_ref[:, cols]
                dwp_ref[gi] += _dot(pooled, dz, "tn")
                dpool = _dot(dz, wp_ref[gi], "nt")
                dp_ref[rows, cols] = dpool
                sc_pad_ref[rows, cols] = dpool / cnt
            for c in range(S // CH):
                rows = slice(c * CH, (c + 1) * CH)
                acc = sc_pad_ref[rows, cols]
                for j in range(1, w):
                    acc = acc + sc_pad_ref[c * CH + j:c * CH + j + CH, cols]
                dhp_ref[rows, cols] = (acc - dp_ref[rows, cols]).astype(dhp_ref.dtype)

    seq = pl.BlockSpec((S, POOL_WIDTH), lambda b: (b, 0))
    return pl.pallas_call(
        body,
        out_shape=(jax.ShapeDtypeStruct((B * S, POOL_WIDTH), BF16),
                   jax.ShapeDtypeStruct(w_pool.shape, F32), jax.ShapeDtypeStruct((1, POOL_WIDTH), F32)),
        grid=(B,),
        in_specs=[seq, seq, pl.BlockSpec(w_pool.shape, lambda b: (0, 0, 0)),
                  pl.BlockSpec((1, POOL_WIDTH), lambda b: (0, 0))],
        out_specs=(seq, pl.BlockSpec(w_pool.shape, lambda b: (0, 0, 0)),
                   pl.BlockSpec((1, POOL_WIDTH), lambda b: (0, 0))),
        scratch_shapes=[pltpu.VMEM((S + POOL_HALO, POOL_WIDTH), F32),
                        pltpu.VMEM((S + POOL_HALO, POOL_WIDTH), F32),
                        pltpu.VMEM((S, POOL_WIDTH), F32)],
        name="pool_bwd", compiler_params=_params(("arbitrary",)),
    )(proj, d_ypre, w_pool, scale)


def _ret_tables(S):
    half = RET_QK_DIM // 2
    inv = ROPE_BASE ** (-jnp.arange(half, dtype=F32) / half)
    ang = jnp.arange(S, dtype=F32)[:, None] * inv[None, :]
    cos, sin = jnp.cos(ang), jnp.sin(ang)
    cos_full = jnp.concatenate([cos, cos], axis=-1)
    sin_signed = jnp.concatenate([-sin, sin], axis=-1)
    C = RET_CHUNK
    lg = jnp.log1p(-jnp.exp2(-5.0 - jnp.arange(RET_HEADS, dtype=F32)))[:, None, None]
    idx = jnp.arange(C, dtype=F32)
    rel = idx[:, None] - idx[None, :]
    decay = jnp.where(rel >= 0, jnp.exp(jnp.maximum(rel, 0.0) * lg), 0.0)
    q_decay = jnp.broadcast_to(jnp.exp((idx + 1.0)[None, :, None] * lg), (RET_HEADS, C, RET_QK_DIM))
    k_decay = jnp.broadcast_to(jnp.exp((C - 1.0 - idx)[None, :, None] * lg), (RET_HEADS, C, RET_QK_DIM))
    c_decay = jnp.broadcast_to(jnp.exp(C * lg), (RET_HEADS, 1, RET_V_DIM))
    return cos_full, sin_signed, decay, q_decay, k_decay, c_decay


def _rope(x, cos_full, sin_signed):
    return x * cos_full + pltpu.roll(x, RET_QK_DIM // 2, axis=1) * sin_signed


def _rope_t(dy, cos_full, sin_signed):
    return dy * cos_full + pltpu.roll(dy * sin_signed, RET_QK_DIM // 2, axis=1)


RET_COLS = 512


def _ret_specs(N, chunk_of):
    C = RET_CHUNK

    def rows(width, col=0):
        return pl.BlockSpec((C, width), lambda b, i: (b * N + chunk_of(i), col))

    def whole(shape):
        return pl.BlockSpec(shape, lambda b, i: (0,) * len(shape))

    wide = RET_HEADS * RET_V_DIM
    return dict(
        q=rows(RET_COLS, COL_Q // RET_COLS), k=rows(RET_COLS, COL_K // RET_COLS),
        v=[rows(RET_COLS, COL_V // RET_COLS + j) for j in range(2)],
        gr=[rows(RET_COLS, COL_GR // RET_COLS + j) for j in range(2)],
        table=pl.BlockSpec((C, RET_QK_DIM), lambda b, i: (chunk_of(i), 0)),
        decay=whole((RET_HEADS, C, C)), qd=whole((RET_HEADS, C, RET_QK_DIM)), kd=whole((RET_HEADS, C, RET_QK_DIM)),
        cd=whole((RET_HEADS, 1, RET_V_DIM)), vec=whole((1, wide)), qk_rows=rows(RET_COLS), v_rows=rows(wide),
        state=pl.BlockSpec((None, None, RET_HEADS, RET_QK_DIM, RET_V_DIM), lambda b, i: (b, chunk_of(i), 0, 0, 0)))


def _head_cols(h):
    pair = slice((h % 2) * RET_V_DIM, (h % 2 + 1) * RET_V_DIM)
    return slice(h * RET_QK_DIM, (h + 1) * RET_QK_DIM), h // 2, pair, slice(h * RET_V_DIM, (h + 1) * RET_V_DIM)


def _group_norm(o):
    mu = jnp.mean(o, axis=-1, keepdims=True)
    oc = o - mu
    rstd = lax.rsqrt(jnp.mean(oc * oc, axis=-1, keepdims=True) + EPS)
    return oc * rstd, rstd


def _ret_fwd(proj, g_ret, b_ret, tables, B, S, comm=None):
    N = S // RET_CHUNK
    cos_t, sin_t, decay, q_decay, k_decay, c_decay = tables
    sp = _ret_specs(N, lambda i: i)

    def body(q_ref, k_ref, v0_ref, v1_ref, gr0_ref, gr1_ref, cos_ref, sin_ref, dec_ref, qd_ref, kd_ref, cd_ref,
             g_ref, b_ref, y_ref, rs_ref, r_ref):
        @pl.when(pl.program_id(1) == 0)
        def _():
            r_ref[...] = jnp.zeros_like(r_ref)

        cs, sn = cos_ref[...], sin_ref[...]
        for h in range(RET_HEADS):
            qk, j, pair, wide = _head_cols(h)
            q = _rope(q_ref[:, qk], cs, sn)
            k = _rope(k_ref[:, qk], cs, sn) * (RET_QK_DIM ** -0.5)
            v = (v0_ref, v1_ref)[j][:, pair]
            R = r_ref[h]
            rs_ref[h] = R
            s = _dot(q, k, "nt") * dec_ref[h]
            o = _dot(s, v) + _dot(q * qd_ref[h], R)
            r_ref[h] = cd_ref[h] * R + _dot(k * kd_ref[h], v, "tn")
            on, _ = _group_norm(o)
            gr = (gr0_ref, gr1_ref)[j][:, pair]
            y_ref[:, wide] = (gr * jax.nn.sigmoid(gr) * (on * g_ref[:, wide] + b_ref[:, wide])).astype(y_ref.dtype)

    state = jax.ShapeDtypeStruct((B, N, RET_HEADS, RET_QK_DIM, RET_V_DIM), F32)
    return _pcall(
        body, (proj,) * 6 + (cos_t, sin_t, decay, q_decay, k_decay, c_decay, g_ret, b_ret),
        name="ret_fwd", out_shape=(jax.ShapeDtypeStruct((B * S, RET_HEADS * RET_V_DIM), BF16), state), grid=(B, N),
        in_specs=[sp["q"], sp["k"], *sp["v"], *sp["gr"], sp["table"], sp["table"], sp["decay"], sp["qd"],
                  sp["kd"], sp["cd"], sp["vec"], sp["vec"]],
        out_specs=(sp["v_rows"], sp["state"]),
        scratch_shapes=[pltpu.VMEM((RET_HEADS, RET_QK_DIM, RET_V_DIM), F32)],
        sem=("parallel", "arbitrary"), comm=comm)


def _ret_bwd(proj, states, d_yr, g_ret, b_ret, tables, B, S, comm=None):
    N = S // RET_CHUNK
    cos_t, sin_t, decay, q_decay, k_decay, c_decay = tables
    sp = _ret_specs(N, lambda i: N - 1 - i)
    qk_scale = RET_QK_DIM ** -0.5

    def body(q_ref, k_ref, v0_ref, v1_ref, gr0_ref, gr1_ref, dy_ref, rs_ref, cos_ref, sin_ref, dec_ref, qd_ref,
             kd_ref, cd_ref, g_ref, b_ref, dq_ref, dk_ref, dv_ref, dgr_ref, dg_ref, db_ref, dr_ref):
        @pl.when((pl.program_id(0) == 0) & (pl.program_id(1) == 0))
        def _():
            dg_ref[...] = jnp.zeros_like(dg_ref)
            db_ref[...] = jnp.zeros_like(db_ref)

        @pl.when(pl.program_id(1) == 0)
        def _():
            dr_ref[...] = jnp.zeros_like(dr_ref)

        cs, sn = cos_ref[...], sin_ref[...]
        for h in range(RET_HEADS):
            qk, j, pair, wide = _head_cols(h)
            q = _rope(q_ref[:, qk], cs, sn)
            k = _rope(k_ref[:, qk], cs, sn) * qk_scale
            v = (v0_ref, v1_ref)[j][:, pair]
            R, dR = rs_ref[h], dr_ref[h]
            dec, qd, kd = dec_ref[h], qd_ref[h], kd_ref[h]
            s = _dot(q, k, "nt") * dec
            o = _dot(s, v) + _dot(q * qd, R)
            on, rstd = _group_norm(o)
            g = g_ref[:, wide]
            oaff = on * g + b_ref[:, wide]
            gr = (gr0_ref, gr1_ref)[j][:, pair]
            sg = jax.nn.sigmoid(gr)
            dy = dy_ref[:, wide]
            dgr_ref[:, wide] = (dy * oaff * (sg * (1.0 + gr * (1.0 - sg)))).astype(dgr_ref.dtype)
            doaff = dy * (gr * sg)
            dg_ref[:, wide] += jnp.sum(doaff * on, axis=0, keepdims=True)
            db_ref[:, wide] += jnp.sum(doaff, axis=0, keepdims=True)
            don = doaff * g
            do = rstd * (don - jnp.mean(don, axis=-1, keepdims=True)
                         - on * jnp.mean(don * on, axis=-1, keepdims=True))
            ds = _dot(do, v, "nt") * dec
            dq = _dot(ds, k) + qd * _dot(do, R, "nt")
            dk = _dot(ds, q, "tn") + kd * _dot(v, dR, "nt")
            dv_ref[:, wide] = (_dot(s, do, "tn") + _dot(k * kd, dR)).astype(dv_ref.dtype)
            dr_ref[h] = cd_ref[h] * dR + _dot(q * qd, do, "tn")
            dq_ref[:, qk] = _rope_t(dq, cs, sn).astype(dq_ref.dtype)
            dk_ref[:, qk] = _rope_t(dk * qk_scale, cs, sn).astype(dk_ref.dtype)

    T = B * S
    qk_shape = jax.ShapeDtypeStruct((T, RET_HEADS * RET_QK_DIM), BF16)
    v_shape = jax.ShapeDtypeStruct((T, RET_HEADS * RET_V_DIM), BF16)
    vec_shape = jax.ShapeDtypeStruct((1, RET_HEADS * RET_V_DIM), F32)
    return _pcall(
        body, (proj,) * 6 + (d_yr, states, cos_t, sin_t, decay, q_decay, k_decay, c_decay, g_ret, b_ret),
        name="ret_bwd", out_shape=(qk_shape, qk_shape, v_shape, v_shape, vec_shape, vec_shape), grid=(B, N),
        in_specs=[sp["q"], sp["k"], *sp["v"], *sp["gr"], sp["v_rows"], sp["state"], sp["table"], sp["table"],
                  sp["decay"], sp["qd"], sp["kd"], sp["cd"], sp["vec"], sp["vec"]],
        out_specs=(sp["qk_rows"], sp["qk_rows"], sp["v_rows"], sp["v_rows"], sp["vec"], sp["vec"]),
        scratch_shapes=[pltpu.VMEM((RET_HEADS, RET_QK_DIM, RET_V_DIM), F32)],
        sem=("arbitrary", "arbitrary"), comm=comm)


def _xa_rows(S):
    return _tile(S, 256)


def _xa_specs(S, M):
    q = pl.BlockSpec((S, XA_HEAD_DIM), lambda b, h: (b, COL_QX // XA_HEAD_DIM + h))
    k = pl.BlockSpec((M, XA_HEAD_DIM), lambda b, h: (b, h))
    v = pl.BlockSpec((M, XA_HEAD_DIM), lambda b, h: (b, XA_HEADS + h))
    o = pl.BlockSpec((S, XA_HEAD_DIM), lambda b, h: (b, h))
    return q, k, v, o


def _softmax_rows(s):
    e = jnp.exp(s - jnp.max(s, axis=-1, keepdims=True))
    return e / jnp.sum(e, axis=-1, keepdims=True)


def _xa_fwd(proj, kv, B, S, M, comm=None):
    CH = _xa_rows(S)
    q_spec, k_spec, v_spec, o_spec = _xa_specs(S, M)

    def body(q_ref, k_ref, v_ref, o_ref):
        def chunk(i, carry):
            rows = pl.ds(pl.multiple_of(i * CH, CH), CH)
            p = _softmax_rows(_dot(q_ref[rows, :], k_ref[...], "nt") * (XA_HEAD_DIM ** -0.5))
            o_ref[rows, :] = _dot(p, v_ref[...]).astype(o_ref.dtype)
            return carry

        lax.fori_loop(0, S // CH, chunk, 0)

    return _pcall(
        body, (proj, kv, kv), name="xattn_fwd", out_shape=jax.ShapeDtypeStruct((B * S, XA_WIDTH), BF16),
        grid=(B, XA_HEADS), in_specs=[q_spec, k_spec, v_spec], out_specs=o_spec,
        sem=("parallel", "parallel"), comm=comm)


def _xa_bwd(proj, kv, d_o, B, S, M, comm=None):
    CH = _xa_rows(S)
    q_spec, k_spec, v_spec, o_spec = _xa_specs(S, M)
    scale = XA_HEAD_DIM ** -0.5

    def body(q_ref, k_ref, v_ref, do_ref, dq_ref, dk_ref, dv_ref):
        dk_ref[...] = jnp.zeros_like(dk_ref)
        dv_ref[...] = jnp.zeros_like(dv_ref)

        def chunk(i, carry):
            rows = pl.ds(pl.multiple_of(i * CH, CH), CH)
            q, do = q_ref[rows, :], do_ref[rows, :]
            p = _softmax_rows(_dot(q, k_ref[...], "nt") * scale)
            dp = _dot(do, v_ref[...], "nt")
            ds = p * (dp - jnp.sum(dp * p, axis=-1, keepdims=True)) * scale
            dq_ref[rows, :] = _dot(ds, k_ref[...]).astype(dq_ref.dtype)
            dk_ref[...] += _dot(ds, q, "tn")
            dv_ref[...] += _dot(p, do, "tn")
            return carry

        lax.fori_loop(0, S // CH, chunk, 0)

    kv_out = pl.BlockSpec((M, XA_HEAD_DIM), lambda b, h: (b, h))
    return _pcall(
        body, (proj, kv, kv, d_o), name="xattn_bwd",
        out_shape=(jax.ShapeDtypeStruct((B * S, XA_WIDTH), BF16), jax.ShapeDtypeStruct((B * M, XA_WIDTH), F32),
                   jax.ShapeDtypeStruct((B * M, XA_WIDTH), F32)),
        grid=(B, XA_HEADS), in_specs=[q_spec, k_spec, v_spec, o_spec], out_specs=(o_spec, kv_out, kv_out),
        sem=("parallel", "parallel"), comm=comm)


def _gate_specs(tm):
    n = COL_GL // D_MODEL
    return [pl.BlockSpec((tm, D_MODEL), lambda i, j=j: (i, n + j)) for j in range(3)]


def _merge_fwd(proj, ys, tm=256, comm=None):
    T = proj.shape[0]
    tm = _tile(T, tm)
    row = pl.BlockSpec((tm, D_MODEL), lambda i: (i, 0))

    def body(g0, g1, g2, y0, y1, y2, o_ref):
        acc = jax.nn.sigmoid(g0[...]) * y0[...]
        acc = acc + jax.nn.sigmoid(g1[...]) * y1[...]
        acc = acc + jax.nn.sigmoid(g2[...]) * y2[...]
        o_ref[...] = acc.astype(o_ref.dtype)

    return _pcall(
        body, (proj, proj, proj, *ys), name="merge_fwd", out_shape=jax.ShapeDtypeStruct((T, D_MODEL), BF16),
        grid=(T // tm,), in_specs=_gate_specs(tm) + [row] * 3, out_specs=row, sem=("parallel",), comm=comm)


def _merge_bwd(proj, ys, d_merged, tm=256, comm=None):
    T = proj.shape[0]
    tm = _tile(T, tm)
    row = pl.BlockSpec((tm, D_MODEL), lambda i: (i, 0))

    def body(g0, g1, g2, y0, y1, y2, dm_ref, dgl_ref, d0, d1, d2):
        dm = dm_ref[...]
        for j, (g_ref, y_ref, d_ref) in enumerate(((g0, y0, d0), (g1, y1, d1), (g2, y2, d2))):
            sg = jax.nn.sigmoid(g_ref[...])
            d_ref[...] = (dm * sg).astype(d_ref.dtype)
            dgl_ref[:, j * D_MODEL:(j + 1) * D_MODEL] = (dm * y_ref[...] * sg * (1.0 - sg)).astype(dgl_ref.dtype)

    dy = jax.ShapeDtypeStruct((T, D_MODEL), BF16)
    return _pcall(
        body, (proj, proj, proj, *ys, d_merged), name="merge_bwd",
        out_shape=(jax.ShapeDtypeStruct((T, 3 * D_MODEL), BF16), dy, dy, dy), grid=(T // tm,),
        in_specs=_gate_specs(tm) + [row] * 4,
        out_specs=(pl.BlockSpec((tm, 3 * D_MODEL), lambda i: (i, 0)), row, row, row),
        sem=("parallel",), comm=comm)


def _gelu(x):
    return 0.5 * x * (1.0 + jnp.tanh(GELU_C * (x + GELU_A * x * x * x)))


def _gelu_grad(x):
    t = jnp.tanh(GELU_C * (x + GELU_A * x * x * x))
    return 0.5 * (1.0 + t) + 0.5 * x * (1.0 - t * t) * GELU_C * (1.0 + 3.0 * GELU_A * x * x)


def _shift_down(x, prev, n):
    rows = x.shape[0]
    r = lax.broadcasted_iota(jnp.int32, (8, 1), 0)
    rolled = pltpu.roll(x, n, axis=0)
    head = rolled[0:8]
    for j in range(n):
        head = jnp.where(r == j, prev[8 - n + j:8 - n + j + 1, :], head)
    return head if rows == 8 else jnp.concatenate([head, rolled[8:]], axis=0)


def _shift_up(x, nxt, n):
    rows = x.shape[0]
    r = lax.broadcasted_iota(jnp.int32, (8, 1), 0)
    rolled = pltpu.roll(x, rows - n, axis=0)
    tail = rolled[rows - 8:]
    for j in range(n):
        tail = jnp.where(r == 8 - n + j, nxt[j:j + 1, :], tail)
    return jnp.concatenate([rolled[:rows - 8], tail], axis=0)


def _conv(a, prev, cw, cb):
    return _shift_down(a, prev, 2) * cw[0:1, :] + _shift_down(a, prev, 1) * cw[1:2, :] + a * cw[2:3, :] + cb


def _glu_fwd(up, cw, cb, S, tm=256, comm=None):
    T = up.shape[2]
    tm = _tile(S, tm)
    per_seq = S // tm

    def body(ab_ref, prev_ref, cw_ref, cb_ref, u_ref, ac_ref):
        i = pl.program_id(1)
        prev = jnp.where(i % per_seq == 0, 0.0, prev_ref[...])
        ac = _conv(ab_ref[0], prev, cw_ref[...], cb_ref[...])
        ac_ref[...] = ac
        u_ref[...] = (_gelu(ac) * ab_ref[1]).astype(u_ref.dtype)

    tile = pl.BlockSpec((None, tm, UP_SHARD), lambda d, i: (d, i, 0))
    return _pcall(
        body, (up, up, cw, cb), name="glu_fwd",
        out_shape=(jax.ShapeDtypeStruct((FFN_SLABS, T, UP_SHARD), BF16),
                   jax.ShapeDtypeStruct((FFN_SLABS, T, UP_SHARD), F32)), grid=(FFN_SLABS, T // tm),
        in_specs=[pl.BlockSpec((2, None, tm, UP_SHARD), lambda d, i: (0, d, i, 0)),
                  pl.BlockSpec((None, None, 8, UP_SHARD), lambda d, i: (0, d, jnp.maximum(i * (tm // 8) - 1, 0), 0)),
                  pl.BlockSpec((None, 3, UP_SHARD), lambda d, i: (d, 0, 0)),
                  pl.BlockSpec((None, 1, UP_SHARD), lambda d, i: (d, 0, 0))],
        out_specs=(tile, tile), sem=("parallel", "parallel"), comm=comm)


def _glu_bwd(up, ac, d_u, cw, S, tm=256, comm=None):
    T = up.shape[2]
    tm = _tile(S, tm)
    per_seq = S // tm
    n_tiles = T // tm
    last8 = tm // 8

    def body(ab_ref, abn_ref, ac_ref, acn_ref, du_ref, dun_ref, cw_ref, dup_ref, dcw_ref, dcb_ref):
        i = pl.program_id(1)

        @pl.when(i == 0)
        def _():
            dcw_ref[...] = jnp.zeros_like(dcw_ref)
            dcb_ref[...] = jnp.zeros_like(dcb_ref)

        cw = cw_ref[...]
        a, b, ac, du = ab_ref[0], ab_ref[1], ac_ref[...], du_ref[...]
        dup_ref[1] = (du * _gelu(ac)).astype(dup_ref.dtype)
        dac = du * b * _gelu_grad(ac)
        dacn = jnp.where(i % per_seq == per_seq - 1, 0.0, dun_ref[...] * abn_ref[1] * _gelu_grad(acn_ref[...]))
        up1, up2 = _shift_up(dac, dacn, 1), _shift_up(dac, dacn, 2)
        dup_ref[0] = (dac * cw[2:3, :] + up1 * cw[1:2, :] + up2 * cw[0:1, :]).astype(dup_ref.dtype)
        dcb_ref[...] += jnp.sum(dac, axis=0, keepdims=True)
        dcw_ref[0:1, :] += jnp.sum(a * up2, axis=0, keepdims=True)
        dcw_ref[1:2, :] += jnp.sum(a * up1, axis=0, keepdims=True)
        dcw_ref[2:3, :] += jnp.sum(a * dac, axis=0, keepdims=True)

    def nxt(i):
        return jnp.minimum((i + 1) * last8, T // 8 - 1)

    tile = pl.BlockSpec((None, tm, UP_SHARD), lambda d, i: (d, i, 0))
    after = pl.BlockSpec((None, 8, UP_SHARD), lambda d, i: (d, nxt(i), 0))
    return _pcall(
        body, (up, up, ac, ac, d_u, d_u, cw), name="glu_bwd",
        out_shape=(jax.ShapeDtypeStruct((2, FFN_SLABS, T, UP_SHARD), BF16),
                   jax.ShapeDtypeStruct((FFN_SLABS, 3, UP_SHARD), F32),
                   jax.ShapeDtypeStruct((FFN_SLABS, 1, UP_SHARD), F32)),
        grid=(FFN_SLABS, n_tiles),
        in_specs=[pl.BlockSpec((2, None, tm, UP_SHARD), lambda d, i: (0, d, i, 0)),
                  pl.BlockSpec((2, None, 8, UP_SHARD), lambda d, i: (0, d, nxt(i), 0)),
                  tile, after, tile, after,
                  pl.BlockSpec((None, 3, UP_SHARD), lambda d, i: (d, 0, 0))],
        out_specs=(pl.BlockSpec((2, None, tm, UP_SHARD), lambda d, i: (0, d, i, 0)),
                   pl.BlockSpec((None, 3, UP_SHARD), lambda d, i: (d, 0, 0)),
                   pl.BlockSpec((None, 1, UP_SHARD), lambda d, i: (d, 0, 0))),
        sem=("parallel", "arbitrary"), comm=comm)


def _mm_up(h2, w_up_t, tm=MM_ROWS):
    T, K = h2.shape
    tm = _tile(T, tm)
    return _matmul(
        "mm_up", "nt", h2, w_up_t, jax.ShapeDtypeStruct((N_DEV, T, UP_SHARD), F32), (N_DEV, T // tm, 1),
        pl.BlockSpec((tm, K), lambda j, i, k: (i, 0)), pl.BlockSpec((None, UP_SHARD, K), lambda j, i, k: (j, 0, 0)),
        pl.BlockSpec((None, tm, UP_SHARD), lambda j, i, k: (j, i, 0)), (tm, UP_SHARD))


def _loss_epilogue(ffn, operands, outputs, first):
    x1_ref, t_ref, g_ref = operands
    dx_ref, dg_ref, loss_ref = outputs

    @pl.when(first)
    def _():
        dg_ref[...] = jnp.zeros_like(dg_ref)
        loss_ref[...] = jnp.zeros_like(loss_ref)

    xv = x1_ref[...] + ffn
    r = lax.rsqrt(jnp.mean(xv * xv, axis=-1, keepdims=True) + EPS)
    xhat = xv * r
    err = xhat * g_ref[...] - t_ref[...]
    loss_ref[...] += (0.5 / D_MODEL) * jnp.sum(err * err)
    dy = err * (1.0 / D_MODEL)
    dg_ref[...] += jnp.sum(dy * xhat, axis=0, keepdims=True)
    dxhat = dy * g_ref[...]
    dx_ref[...] = r * (dxhat - xhat * jnp.mean(dxhat * xhat, axis=-1, keepdims=True))


def _mm_down_loss(u, w_down, x1, target, g_final, tm=MM_ROWS_RES):
    J, T, n = u.shape
    tm = _tile(T, tm)
    row = pl.BlockSpec((tm, D_MODEL), lambda i, d: (i, 0))
    vec = pl.BlockSpec((1, D_MODEL), lambda i, d: (0, 0))
    vec_shape = jax.ShapeDtypeStruct((1, D_MODEL), F32)
    return _matmul(
        "mm_down", "nn", u, w_down, (jax.ShapeDtypeStruct((T, D_MODEL), F32), vec_shape, vec_shape), (T // tm, J),
        pl.BlockSpec((None, tm, n), lambda i, d: (d, i, 0)), pl.BlockSpec((None, n, D_MODEL), lambda i, d: (d, 0, 0)),
        (row, vec, vec), (tm, D_MODEL), [x1, target, g_final], [row, row, vec], epilogue=_loss_epilogue)


def _mm_down_t(dx, w_down, tm=MM_ROWS):
    T = dx.shape[0]
    J, n, _ = w_down.shape
    tm = _tile(T, tm)
    return _matmul(
        "mm_down_t", "nt", dx, w_down, jax.ShapeDtypeStruct((J, T, n), F32), (J, T // tm, 1),
        pl.BlockSpec((tm, D_MODEL), lambda d, i, k: (i, 0)), pl.BlockSpec((None, n, D_MODEL), lambda d, i, k: (d, 0, 0)),
        pl.BlockSpec((None, tm, n), lambda d, i, k: (d, i, 0)), (tm, n))


def _mm_dw_down(u, dx, tk=MM_TOKENS):
    J, T, n = u.shape
    tk = _tile(T, tk)
    return _matmul(
        "mm_dw_down", "tn", u, dx, jax.ShapeDtypeStruct((J, n, D_MODEL), BF16), (J, T // tk),
        pl.BlockSpec((None, tk, n), lambda d, k: (d, k, 0)), pl.BlockSpec((tk, D_MODEL), lambda d, k: (k, 0)),
        pl.BlockSpec((None, n, D_MODEL), lambda d, k: (d, 0, 0)), (n, D_MODEL))


def _mm_dw_up(h2, d_up, tk=MM_TOKENS):
    T, K = h2.shape
    tk = _tile(T, tk)
    return _matmul(
        "mm_dw_up", "tn", d_up, h2, jax.ShapeDtypeStruct((N_DEV, UP_SHARD, K), BF16), (N_DEV, T // tk),
        pl.BlockSpec((None, tk, UP_SHARD), lambda j, k: (j, k, 0)), pl.BlockSpec((tk, K), lambda j, k: (k, 0)),
        pl.BlockSpec((None, UP_SHARD, K), lambda j, k: (j, 0, 0)), (UP_SHARD, K))


def _mm_up_t(d_up, w_up_t, rms, tm=MM_ROWS_RES, comm=None):
    J, T, n = d_up.shape
    K = w_up_t.shape[2]
    tm = _tile(T, tm)
    fused = _rms_bwd_fused(T, K, tm, rms)
    return _matmul(
        "mm_up_t", "nn", d_up, w_up_t, fused.pop("out_shape"), (T // tm, J),
        pl.BlockSpec((None, tm, n), lambda i, j: (j, i, 0)), pl.BlockSpec((None, n, K), lambda i, j: (j, 0, 0)),
        fused.pop("o_spec"), (tm, K), comm=comm, **fused)


def _cast_shards(shards):
    def body(*refs):
        n = len(refs) // 2
        for src, dst in zip(refs[:n], refs[n:]):
            dst[...] = src[...].astype(dst.dtype)

    return pl.pallas_call(
        body, out_shape=[jax.ShapeDtypeStruct(s.shape, BF16) for s in shards], name="cast_shards",
        compiler_params=pltpu.CompilerParams(vmem_limit_bytes=VMEM_LIMIT),
    )(*shards)


def _adamw(w, g, m, v):
    m = ADAM_B1 * m + (1.0 - ADAM_B1) * g
    v = ADAM_B2 * v + (1.0 - ADAM_B2) * (g * g)
    m_hat = m / (1.0 - ADAM_B1 ** ADAM_STEP)
    v_hat = v / (1.0 - ADAM_B2 ** ADAM_STEP)
    delta = -ADAM_LR * (m_hat / (jnp.sqrt(v_hat) + ADAM_EPS) + ADAM_WD * w)
    return delta, m, v


def _sum_parts(p_ref):
    g = p_ref[0].astype(F32)
    for d in range(1, N_DEV):
        g = g + p_ref[d].astype(F32)
    return g


def _reduce_adam(name, parts, w, m, v, tr=128):
    R, Cn = w.shape
    by_rows = sum(p.shape[1] for p in parts) == R and len(parts) > 1
    tr = math.gcd(tr, *[p.shape[1] for p in parts])
    n_tiles = [p.shape[1] // tr for p in parts]
    first = [sum(n_tiles[:j]) for j in range(len(parts))] if by_rows else [0] * len(parts)

    def body(*refs):
        p_refs = refs[:len(parts)]
        w_ref, m_ref, v_ref, g_out, d_out, m_out, v_out = refs[len(parts):]

        def update(p_ref):
            g = _sum_parts(p_ref)
            delta, m_new, v_new = _adamw(w_ref[...], g, m_ref[...], v_ref[...])
            g_out[...] = g
            d_out[...] = delta
            m_out[...] = m_new
            v_out[...] = v_new

        if len(parts) == 1:
            update(p_refs[0])
        elif by_rows:
            i = pl.program_id(0)
            for p_ref, t0, n in zip(p_refs, first, n_tiles):
                pl.when((i >= t0) & (i < t0 + n))(functools.partial(update, p_ref))
        else:
            c = lax.axis_index("c")
            for side, p_ref in enumerate(p_refs):
                pl.when(c == side)(functools.partial(update, p_ref))

    def part_spec(t0, n):
        return pl.BlockSpec((N_DEV, tr, Cn), lambda i: (0, jnp.clip(i - t0, 0, n - 1), 0))

    row = pl.BlockSpec((tr, Cn), lambda i: (i, 0))
    shape = jax.ShapeDtypeStruct((R, Cn), F32)
    return pl.pallas_call(
        body, out_shape=(shape,) * 4, grid=(R // tr,),
        in_specs=[part_spec(t0, n) for t0, n in zip(first, n_tiles)] + [row, row, row],
        out_specs=(row,) * 4, name=name, compiler_params=_params(("parallel",)),
    )(*parts, w, m, v)


def _small_adam(name, gathered, params):
    n_g, n_p = len(gathered), len(params)

    def body(*refs):
        g_refs = refs[:n_g]
        wmv = refs[n_g:n_g + 3 * n_p]
        sums = refs[n_g + 3 * n_p:2 * n_g + 3 * n_p]
        upd = refs[2 * n_g + 3 * n_p:]
        for j in range(n_g):
            g = _sum_parts(g_refs[j])
            sums[j][...] = g
            if j < n_p:
                w_ref, m_ref, v_ref = wmv[3 * j:3 * j + 3]
                delta, m_new, v_new = _adamw(w_ref[...], g, m_ref[...], v_ref[...])
                upd[3 * j][...] = delta
                upd[3 * j + 1][...] = m_new
                upd[3 * j + 2][...] = v_new

    flat = [a for wmv in params for a in wmv]
    out_shape = [jax.ShapeDtypeStruct(g.shape[1:], F32) for g in gathered]
    out_shape += [jax.ShapeDtypeStruct(a.shape, F32) for a in flat]
    res = pl.pallas_call(body, out_shape=out_shape, name=name)(*gathered, *flat)
    return res[:n_g], [tuple(res[n_g + 3 * j:n_g + 3 * j + 3]) for j in range(n_p)]


def _adam_only(name, g, w, m, v):
    def body(g_ref, w_ref, m_ref, v_ref, d_out, m_out, v_out):
        delta, m_new, v_new = _adamw(w_ref[...], g_ref[...], m_ref[...], v_ref[...])
        d_out[...] = delta
        m_out[...] = m_new
        v_out[...] = v_new

    shape = jax.ShapeDtypeStruct(w.shape, F32)
    return pl.pallas_call(body, out_shape=(shape,) * 3, name=name)(g, w, m, v)


def kernel(x, mem, g_mix, w_in, w_pool, pool_scale, w_a, g_ret, b_ret, w_r, g_mem, w_mem_kv, w_c, w_out, g_ffn, w_up, conv_w, conv_b, w_down, g_final, loss_target, m_g_mix, m_w_in, m_w_pool, m_pool_scale, m_w_a, m_g_ret, m_b_ret, m_w_r, m_g_mem, m_w_mem_kv, m_w_c, m_w_out, m_g_ffn, m_w_up, m_conv_w, m_conv_b, m_w_down, m_g_final, v_g_mix, v_w_in, v_w_pool, v_pool_scale, v_w_a, v_g_ret, v_b_ret, v_w_r, v_g_mem, v_w_mem_kv, v_w_c, v_w_out, v_g_ffn, v_w_up, v_conv_w, v_conv_b, v_w_down, v_g_final):
    B, S, _ = x.shape
    M = mem.shape[1]
    T = B * S
    me = _my_index()
    x2d = x.reshape(T, D_MODEL)
    mem2d = mem.reshape(B * M, D_MODEL)
    tgt2d = loss_target.reshape(T, D_MODEL)
    g_final2 = g_final.reshape(1, D_MODEL)

    big = dict(w_in=w_in[0], w_a=w_a[0], w_r=w_r[0], w_mem_kv=w_mem_kv[0], w_c=w_c[0], w_out=w_out[0],
               w_up=w_up[0].T, w_down=w_down[0])
    names = list(big)
    cast = dict(zip(names, _cast_shards([big[n] for n in names])))
    cb = conv_b[0].reshape(FFN_SLABS, 1, UP_SHARD)
    wp = w_pool[0]
    tables = _ret_tables(S)

    h = _rms_fwd("rms_mix", x2d, g_mix)
    early = ("w_a", "w_r", "w_mem_kv", "w_c", "w_out")
    (proj, Win), landed = _mm_in_gather(h, cast["w_in"], comm=_Gather([cast[n] for n in early] + [conv_w[0]]))
    W = dict(zip(early, landed))
    (yr, ret_states), (Wup,) = _ret_fwd(proj, g_ret, b_ret, tables, B, S, comm=_Gather([cast["w_up"]]))
    cw_full = landed[-1].transpose(1, 0, 2).reshape(3, FFN_HIDDEN)
    cw = cw_full.reshape(3, FFN_SLABS, UP_SHARD).transpose(1, 0, 2)
    Wa = W["w_a"].transpose(1, 0, 2).reshape(POOL_WIDTH, D_MODEL)
    Wc = W["w_c"].transpose(1, 0, 2).reshape(XA_WIDTH, D_MODEL)
    Wr = W["w_r"].reshape(D_MODEL, D_MODEL)
    Wkv = W["w_mem_kv"].reshape(D_MODEL, D_MODEL)
    Wout = W["w_out"].reshape(D_MODEL, D_MODEL)
    ypre = _pool_fwd(proj, wp, pool_scale, B, S)
    y_pool = _mm_rows("mm_a", ypre, Wa, BF16)
    y_ret = _mm_rows("mm_r", yr, Wr, BF16)
    mem_n = _rms_fwd("rms_mem", mem2d, g_mem)
    kv = _mm_rows("mm_kv", mem_n, Wkv)
    o_mem = _xa_fwd(proj, kv, B, S, M)[0]
    y_mem = _mm_rows("mm_c", o_mem, Wc, BF16)
    ys = (y_pool, y_ret, y_mem)
    merged = _merge_fwd(proj, ys)[0]
    x1, h2 = _mm_residual_rms("mm_out", merged, Wout, x2d, g_ffn)
    up = _mm_up(h2, Wup).reshape(2, FFN_SLABS, T, UP_SHARD)
    (u, ac), (Wdown,) = _glu_fwd(up, cw, cb, S, comm=_Gather([cast["w_down"]]))
    Wdown = Wdown.reshape(FFN_SLABS, UP_SHARD, D_MODEL)

    dx2, dg_final, loss_part = _mm_down_loss(u, Wdown, x1, tgt2d, g_final2)
    received = {}
    d_u = _mm_down_t(dx2, Wdown)
    dW_down = _mm_dw_down(u, dx2)
    (d_up, d_cw, d_cb), (received["w_down"],) = _glu_bwd(
        up, ac, d_u, cw, S, comm=_Exchange([dW_down.reshape(N_DEV, -1, D_MODEL)]))
    d_up = d_up.reshape(N_DEV, T, UP_SHARD)
    dW_up = _mm_dw_up(h2, d_up)
    (dx1, dg_ffn), (up_c0,) = _mm_up_t(d_up, Wup, (x1, g_ffn, dx2), comm=_ExchangeTo([dW_up], 0))
    d_merged = _mm_rows("mm_out_t", dx1, Wout, kind="nt")
    dW_out = _mm_tn("mm_dw_out", merged, dx1, BF16)
    (d_gl, d_y_pool, d_y_ret, d_y_mem), (received["w_out"],) = _merge_bwd(
        proj, ys, d_merged, comm=_Exchange([dW_out.reshape(N_DEV, -1, D_MODEL)]))
    dW_c = _mm_tn("mm_dw_c", o_mem, d_y_mem, BF16)
    d_o_mem = _mm_rows("mm_c_t", d_y_mem, Wc, kind="nt")
    (d_qx, d_kmem, d_vmem), (up_c1,) = _xa_bwd(proj, kv, d_o_mem, B, S, M, comm=_ExchangeTo([dW_up], 1))
    received["w_up"] = [up_c0, up_c1]
    d_kv = jnp.concatenate([d_kmem, d_vmem], axis=1)
    dW_kv = _mm_tn("mm_dw_kv", mem_n, d_kv, BF16)
    d_mem_n = _mm_rows("mm_kv_t", d_kv, Wkv, kind="nt")
    dg_mem = _rms_bwd("rms_mem_bwd", mem2d, g_mem, d_mem_n, None)
    dW_a = _mm_tn("mm_dw_a", ypre, d_y_pool, BF16)
    d_ypre = _mm_rows("mm_a_t", d_y_pool, Wa, kind="nt")
    d_hp, dw_pool, d_scale = _pool_bwd(proj, d_ypre, wp, pool_scale, B, S)
    dW_r = _mm_tn("mm_dw_r", yr, d_y_ret, BF16)
    d_yr = _mm_rows("mm_r_t", d_y_ret, Wr, kind="nt")
    (d_q, d_k, d_v, d_gr, dg_ret, db_ret), landed = _ret_bwd(
        proj, ret_states, d_yr, g_ret, b_ret, tables, B, S,
        comm=_Exchange([dW_a.reshape(POOL_WIDTH, N_DEV, -1).transpose(1, 0, 2), dW_r.reshape(N_DEV, -1, D_MODEL),
                        dW_c.reshape(XA_WIDTH, N_DEV, -1).transpose(1, 0, 2), dW_kv.reshape(N_DEV, -1, D_MODEL)]))
    received["w_a"], received["w_r"], received["w_c"], received["w_mem_kv"] = landed
    small_names = ["w_pool", "pool_scale", "g_ret", "b_ret", "g_mem", "g_ffn", "conv_b", "g_final"]
    small_grads = [dw_pool, d_scale, dg_ret, db_ret, dg_mem, dg_ffn, d_cb.reshape(1, FFN_HIDDEN), dg_final,
                   d_cw.transpose(1, 0, 2).reshape(3, FFN_HIDDEN), loss_part]
    d_proj = jnp.concatenate([d_hp, d_q, d_k, d_v, d_gr, d_qx, d_gl], axis=1)
    dW_in0, small_all = _mm_tn_slab("mm_dw_in0", h[:, :W_IN_FIRST_ROWS], d_proj, IN_SHARD, BF16,
                                    comm=_Exchange([], whole=small_grads))
    dW_in1, (in0,) = _mm_tn_slab("mm_dw_in1", h[:, W_IN_FIRST_ROWS:], d_proj, IN_SHARD, BF16,
                                 comm=_Exchange([dW_in0]))
    (grad_x, dg_mix), (in1,) = _mm_cols_slab_t("mm_in_t", d_proj, Win, (x2d, g_mix, dx1), comm=_Exchange([dW_in1]))
    received["w_in"] = [in0, in1]
    (g_mix_all,) = _comm_call("gather_g_mix", _Exchange([], whole=[dg_mix]))

    args = dict(g_mix=g_mix, w_in=w_in, w_pool=w_pool, pool_scale=pool_scale, w_a=w_a, g_ret=g_ret, b_ret=b_ret,
                w_r=w_r, g_mem=g_mem, w_mem_kv=w_mem_kv, w_c=w_c, w_out=w_out, g_ffn=g_ffn, w_up=w_up,
                conv_w=conv_w, conv_b=conv_b, w_down=w_down, g_final=g_final)
    m_in = dict(g_mix=m_g_mix, w_in=m_w_in, w_pool=m_w_pool, pool_scale=m_pool_scale, w_a=m_w_a, g_ret=m_g_ret,
                b_ret=m_b_ret, w_r=m_w_r, g_mem=m_g_mem, w_mem_kv=m_w_mem_kv, w_c=m_w_c, w_out=m_w_out,
                g_ffn=m_g_ffn, w_up=m_w_up, conv_w=m_conv_w, conv_b=m_conv_b, w_down=m_w_down, g_final=m_g_final)
    v_in = dict(g_mix=v_g_mix, w_in=v_w_in, w_pool=v_w_pool, pool_scale=v_pool_scale, w_a=v_w_a, g_ret=v_g_ret,
                b_ret=v_b_ret, w_r=v_w_r, g_mem=v_g_mem, w_mem_kv=v_w_mem_kv, w_c=v_w_c, w_out=v_w_out,
                g_ffn=v_g_ffn, w_up=v_w_up, conv_w=v_conv_w, conv_b=v_conv_b, w_down=v_w_down, g_final=v_g_final)

    grads, deltas, new_m, new_v = {}, {}, {}, {}
    for n in names:
        parts = received[n] if isinstance(received[n], list) else [received[n]]
        flip = (lambda a: a.T) if n == "w_up" else (lambda a: a)
        outs = _reduce_adam("adam_" + n, parts, big[n], flip(m_in[n][0]), flip(v_in[n][0]))
        for store, val in zip((grads, deltas, new_m, new_v), outs):
            store[n] = flip(val)[None]

    def as_small(a):
        return a.reshape(a.shape[-3:]) if a.ndim > 2 else a.reshape(1, -1)

    def small_update(call_name, param_names, gathered):
        params = [tuple(as_small(d[n]) for d in (args, m_in, v_in)) for n in param_names]
        sums, updates = _small_adam(call_name, gathered, params)
        for n, g, (d_, m_, v_) in zip(param_names, sums, updates):
            shape = args[n].shape
            grads[n], deltas[n], new_m[n], new_v[n] = (a.reshape(shape) for a in (g, d_, m_, v_))
        return sums[len(param_names):]

    g_cw_full, loss_row = small_update("adam_small", small_names, small_all)
    loss = loss_row[0, 0]
    small_update("adam_g_mix", ["g_mix"], [g_mix_all])

    shard_cols = FFN_HIDDEN // N_DEV
    g_cw = lax.dynamic_slice_in_dim(g_cw_full, me * shard_cols, shard_cols, axis=1)
    d_, m_, v_ = _adam_only("adam_conv_w", g_cw, conv_w[0], m_conv_w[0], v_conv_w[0])
    grads["conv_w"], deltas["conv_w"], new_m["conv_w"], new_v["conv_w"] = g_cw[None], d_[None], m_[None], v_[None]

    order = ["g_mix", "w_in", "w_pool", "pool_scale", "w_a", "g_ret", "b_ret", "w_r", "g_mem", "w_mem_kv", "w_c",
             "w_out", "g_ffn", "w_up", "conv_w", "conv_b", "w_down", "g_final"]
    return (loss, grad_x.reshape(B, S, D_MODEL), *[grads[n] for n in order], *[deltas[n] for n in order],
            *[new_m[n] for n in order], *[new_v[n] for n in order])
```

```python
import functools
import math

import jax
import jax.numpy as jnp
from jax import lax
from jax.experimental import pallas as pl
from jax.experimental.pallas import tpu as pltpu

F32 = jnp.float32
BF16 = jnp.bfloat16

N_DEV = 8
D_MODEL = 1024
POOL_WINDOWS = (2, 4, 8, 16)
POOL_GROUP_DIM = 128
POOL_WIDTH = 512
POOL_HALO = 16
RET_HEADS = 4
RET_QK_DIM = 128
RET_V_DIM = 256
RET_CHUNK = 128
ROPE_BASE = 10000.0
XA_HEADS = 4
XA_HEAD_DIM = 128
XA_WIDTH = 512
IN_WIDTH = 7168
IN_SHARD = IN_WIDTH // N_DEV
FFN_HIDDEN = 2816
UP_SHARD = 2 * FFN_HIDDEN // N_DEV
FFN_SLABS = FFN_HIDDEN // UP_SHARD
EPS = 1e-6
ADAM_LR = 0.001
ADAM_B1 = 0.9
ADAM_B2 = 0.999
ADAM_EPS = 1e-08
ADAM_WD = 0.01
ADAM_STEP = 10
GELU_C = math.sqrt(2.0 / math.pi)
GELU_A = 0.044715
VMEM_LIMIT = 56 * 1024 * 1024
MM_ROWS = 2048
MM_ROWS_RES = 1024
MM_TOKENS = 2048
W_IN_FIRST_ROWS = 384
MESH = pl.DeviceIdType.MESH

COL_Q, COL_K, COL_V, COL_GR, COL_QX, COL_GL = 512, 1024, 1536, 2560, 3584, 4096

_DIMS = {
    "nn": (((1,), (0,)), ((), ())),
    "nt": (((1,), (1,)), ((), ())),
    "tn": (((0,), (0,)), ((), ())),
}


def _dot(a, b, kind="nn"):
    return lax.dot_general(a.astype(BF16), b.astype(BF16), _DIMS[kind], preferred_element_type=F32)


def _params(sem, vmem=VMEM_LIMIT):
    return pltpu.CompilerParams(dimension_semantics=sem, vmem_limit_bytes=vmem)


def _tile(n, pref):
    t = min(n, pref)
    while n % t:
        t //= 2
    return t


def _mesh_pos():
    return lax.axis_index("x"), lax.axis_index("y"), lax.axis_index("c")


def _dev_index(x, y, c):
    return 4 * x + 2 * y + c


def _my_index():
    return _dev_index(*_mesh_pos())


def _remote(src, dst, send_sems, recv_sems, s, to):
    return pltpu.make_async_remote_copy(src_ref=src, dst_ref=dst, send_sem=send_sems.at[s], recv_sem=recv_sems.at[s],
                                        device_id=to, device_id_type=MESH)


class _Gather:
    def __init__(self, shards):
        self.inputs = list(shards)
        self.out_shapes = [jax.ShapeDtypeStruct((N_DEV,) + s.shape, s.dtype) for s in shards]
        n = len(shards)
        self.sem_shapes = [pltpu.SemaphoreType.DMA((7 * n,)), pltpu.SemaphoreType.DMA((7 * n,)),
                           pltpu.SemaphoreType.DMA((n,))]

    def _places(self):
        x, y, c = _mesh_pos()
        return (x, y, c), (x, y, 1 - c), [(1 - x, y), (x, 1 - y), (1 - x, 1 - y)]

    def _local(self, src, dst, sems):
        me = _my_index()
        return [pltpu.make_async_copy(src[w], dst[w].at[me], sems[2].at[w]) for w in range(len(src))]

    def start(self, src, dst, sems):
        me, sib, chips = self._places()
        for cp in self._local(src, dst, sems):
            cp.start()
        for w in range(len(src)):
            land = dst[w].at[_dev_index(*me)]
            _remote(src[w], land, sems[0], sems[1], 7 * w, sib).start()
            for j, chip in enumerate(chips):
                _remote(src[w], land, sems[0], sems[1], 7 * w + 1 + j, (*chip, me[2])).start()

    def middle(self, src, dst, sems):
        me, sib, chips = self._places()
        for j, chip in enumerate(chips):
            for w in range(len(src)):
                block = dst[w].at[_dev_index(*chip, me[2])]
                _remote(src[w], block, sems[0], sems[1], 7 * w + 1 + j, me).wait_recv()
                _remote(block, block, sems[0], sems[1], 7 * w + 4 + j, sib).start()

    def finish(self, src, dst, sems):
        me, sib, chips = self._places()
        n = len(src)
        for w in range(n):
            _remote(src[w], dst[w].at[_dev_index(*sib)], sems[0], sems[1], 7 * w, me).wait_recv()
            for j, chip in enumerate(chips):
                block = dst[w].at[_dev_index(*chip, sib[2])]
                _remote(block, block, sems[0], sems[1], 7 * w + 4 + j, me).wait_recv()
            for k in range(7):
                _remote(src[w], dst[w].at[0], sems[0], sems[1], 7 * w + k, me).wait_send()
        for cp in self._local(src, dst, sems):
            cp.wait()


class _Exchange:
    def __init__(self, partials, whole=()):
        self.n_part = len(partials)
        self.inputs = list(partials) + list(whole)
        self.out_shapes = [jax.ShapeDtypeStruct(p.shape, p.dtype) for p in partials]
        self.out_shapes += [jax.ShapeDtypeStruct((N_DEV,) + a.shape, a.dtype) for a in whole]
        n = len(self.inputs)
        self.sem_shapes = [pltpu.SemaphoreType.DMA((7 * n,)), pltpu.SemaphoreType.DMA((7 * n,)),
                           pltpu.SemaphoreType.DMA((n,))]

    def _peer(self, k):
        x, y, c = _mesh_pos()
        p = (x ^ ((k >> 2) & 1), y ^ ((k >> 1) & 1), c ^ (k & 1))
        return p, _dev_index(*p)

    def _source(self, src, w, slot):
        return src[w].at[slot] if w < self.n_part else src[w]

    def _local(self, src, dst, sems):
        me = _my_index()
        return [pltpu.make_async_copy(self._source(src, w, me), dst[w].at[me], sems[2].at[w])
                for w in range(len(src))]

    def start(self, src, dst, sems):
        me = _my_index()
        for cp in self._local(src, dst, sems):
            cp.start()
        for k in range(1, N_DEV):
            peer, peer_idx = self._peer(k)
            for w in range(len(src)):
                _remote(self._source(src, w, peer_idx), dst[w].at[me], sems[0], sems[1], 7 * w + k - 1, peer).start()

    def finish(self, src, dst, sems):
        for k in range(1, N_DEV):
            peer, peer_idx = self._peer(k)
            for w in range(len(src)):
                cp = _remote(self._source(src, w, peer_idx), dst[w].at[peer_idx], sems[0], sems[1], 7 * w + k - 1, peer)
                cp.wait_send()
                cp.wait_recv()
        for cp in self._local(src, dst, sems):
            cp.wait()


class _ExchangeTo:
    def __init__(self, partials, side):
        self.side = side
        self.inputs = list(partials)
        self.out_shapes = [jax.ShapeDtypeStruct(p.shape, p.dtype) for p in partials]
        n = len(partials)
        self.sem_shapes = [pltpu.SemaphoreType.DMA((7 * n,)), pltpu.SemaphoreType.DMA((7 * n,)),
                           pltpu.SemaphoreType.DMA((n,))]

    def _copies(self, src, dst, sems):
        x, y, c = _mesh_pos()
        me = _dev_index(x, y, c)
        receives = c == self.side
        remote = []
        for k in range(1, N_DEV):
            kx, ky, kc = (k >> 2) & 1, (k >> 1) & 1, k & 1
            peer = (x ^ kx, y ^ ky, c ^ kc)
            peer_idx = _dev_index(*peer)
            sends = c == (self.side ^ kc)
            for w in range(len(src)):
                slab = src[w].at[peer_idx]
                s = 7 * w + k - 1
                remote.append((sends, _remote(slab, dst[w].at[me], sems[0], sems[1], s, peer),
                               _remote(slab, dst[w].at[peer_idx], sems[0], sems[1], s, peer)))
        local = [pltpu.make_async_copy(src[w].at[me], dst[w].at[me], sems[2].at[w]) for w in range(len(src))]
        return receives, remote, local

    def start(self, src, dst, sems):
        receives, remote, local = self._copies(src, dst, sems)

        @pl.when(receives)
        def _():
            for cp in local:
                cp.start()

        for sends, send, _ in remote:
            pl.when(sends)(send.start)

    def finish(self, src, dst, sems):
        receives, remote, local = self._copies(src, dst, sems)
        for sends, send, arrive in remote:
            pl.when(sends)(send.wait_send)
            pl.when(receives)(arrive.wait_recv)

        @pl.when(receives)
        def _():
            for cp in local:
                cp.wait()


def _pcall(body, args, *, name, out_shape, grid, in_specs, out_specs, scratch_shapes=(), sem=None, comm=None):
    single = not isinstance(out_shape, (tuple, list))
    outs = [out_shape] if single else list(out_shape)
    ospecs = [out_specs] if single else list(out_specs)
    n_in, n_out, n_scr = len(args), len(outs), len(scratch_shapes)

    def pick(res):
        return res[0] if single else tuple(res[:n_out])

    if comm is None:
        res = pl.pallas_call(
            body, out_shape=outs, grid=grid, in_specs=list(in_specs), out_specs=ospecs,
            scratch_shapes=list(scratch_shapes), name=name, compiler_params=_params(sem),
        )(*args)
        return pick(res), ()

    nci, nco = len(comm.inputs), len(comm.out_shapes)

    def carrier(*refs):
        at = 0
        parts = []
        for size in (n_in, nci, n_out, nco, n_scr, len(comm.sem_shapes)):
            parts.append(refs[at:at + size])
            at += size
        ins, cins, o, couts, scr, sems = parts
        ids = [pl.program_id(a) for a in range(len(grid))]
        first = functools.reduce(jnp.logical_and, [i == 0 for i in ids])
        last = functools.reduce(jnp.logical_and, [i == g - 1 for i, g in zip(ids, grid)])

        body(*ins, *o, *scr)

        @pl.when(first)
        def _():
            comm.start(cins, couts, sems)

        if hasattr(comm, "middle"):
            steps = math.prod(grid)
            at = functools.reduce(lambda lin, ig: lin * ig[1] + ig[0], zip(ids, grid), 0)

            @pl.when(at == min(steps - 1, (3 * steps) // 4))
            def _():
                comm.middle(cins, couts, sems)

        @pl.when(last)
        def _():
            comm.finish(cins, couts, sems)

    hbm = pl.BlockSpec(memory_space=pltpu.HBM)
    res = pl.pallas_call(
        carrier, out_shape=outs + comm.out_shapes, grid=grid, in_specs=list(in_specs) + [hbm] * nci,
        out_specs=ospecs + [hbm] * nco, scratch_shapes=list(scratch_shapes) + comm.sem_shapes, name=name,
        compiler_params=_params(("arbitrary",) * len(grid)),
    )(*args, *comm.inputs)
    return pick(res), tuple(res[n_out:])


def _comm_call(name, comm):
    def body(*refs):
        nci, nco = len(comm.inputs), len(comm.out_shapes)
        cins, couts, sems = refs[:nci], refs[nci:nci + nco], refs[nci + nco:]
        comm.start(cins, couts, sems)
        if hasattr(comm, "middle"):
            comm.middle(cins, couts, sems)
        comm.finish(cins, couts, sems)

    hbm = pl.BlockSpec(memory_space=pltpu.HBM)
    return pl.pallas_call(
        body, out_shape=comm.out_shapes, in_specs=[hbm] * len(comm.inputs), out_specs=[hbm] * len(comm.out_shapes),
        scratch_shapes=comm.sem_shapes, name=name,
    )(*comm.inputs)


def _matmul(name, kind, a, b, out_shape, grid, a_spec, b_spec, o_spec, acc_shape, res=None, res_spec=None,
            comm=None, epilogue=None):
    nk = grid[-1]
    if epilogue is None:
        extra, extra_specs = ([res], [res_spec]) if res is not None else ([], [])
        n_out = 1
    else:
        extra, extra_specs, n_out = list(res), list(res_spec), len(out_shape)
    n_in = 2 + len(extra)

    def body(*refs):
        a_ref, b_ref = refs[0], refs[1]
        extra_refs, out_refs = refs[2:n_in], refs[n_in:n_in + n_out]

        def prod():
            return _dot(a_ref[...], b_ref[...], kind)

        def finish(acc):
            if epilogue is not None:
                ids = [pl.program_id(ax) for ax in range(len(grid) - 1)]
                first = functools.reduce(jnp.logical_and, [i == 0 for i in ids]) if ids else True
                epilogue(acc, extra_refs, out_refs, first)
                return
            if extra_refs:
                acc = acc + extra_refs[0][...]
            out_refs[0][...] = acc.astype(out_refs[0].dtype)

        if nk == 1:
            finish(prod())
        else:
            acc_ref = refs[n_in + n_out]
            k = pl.program_id(len(grid) - 1)

            @pl.when(k == 0)
            def _():
                acc_ref[...] = prod()

            @pl.when(k > 0)
            def _():
                acc_ref[...] += prod()

            @pl.when(k == nk - 1)
            def _():
                finish(acc_ref[...])

    in_specs = [a_spec, b_spec] + extra_specs
    args = (a, b, *extra)
    scratch = [pltpu.VMEM(acc_shape, F32)] if nk > 1 else []
    sem = ("arbitrary",) * len(grid) if epilogue is not None else ("parallel",) * (len(grid) - 1) + ("arbitrary",)
    out, landed = _pcall(body, args, name=name, out_shape=out_shape, grid=grid, in_specs=in_specs,
                         out_specs=o_spec, scratch_shapes=scratch, sem=sem, comm=comm)
    return out if comm is None else (out, landed)


def _mm_rows(name, a, w, out_dtype=F32, res=None, kind="nn", tm=MM_ROWS, comm=None):
    M, K = a.shape
    N = w.shape[1] if kind == "nn" else w.shape[0]
    tm = _tile(M, tm)
    res_spec = pl.BlockSpec((tm, N), lambda i, k: (i, 0)) if res is not None else None
    return _matmul(
        name, kind, a, w, jax.ShapeDtypeStruct((M, N), out_dtype), (M // tm, 1),
        pl.BlockSpec((tm, K), lambda i, k: (i, 0)), pl.BlockSpec(w.shape, lambda i, k: (0, 0)),
        pl.BlockSpec((tm, N), lambda i, k: (i, 0)), (tm, N), res, res_spec, comm)


def _residual_rms_epilogue(y, operands, outputs, first):
    x_ref, g_ref = operands
    x1_ref, h_ref = outputs
    xv = x_ref[...] + y
    x1_ref[...] = xv
    r = lax.rsqrt(jnp.mean(xv * xv, axis=-1, keepdims=True) + EPS)
    h_ref[...] = (xv * r * g_ref[...]).astype(h_ref.dtype)


def _mm_residual_rms(name, a, w, x, g, tm=MM_ROWS_RES):
    M, K = a.shape
    N = w.shape[1]
    tm = _tile(M, tm)
    row = pl.BlockSpec((tm, N), lambda i, k: (i, 0))
    return _matmul(
        name, "nn", a, w, (jax.ShapeDtypeStruct((M, N), F32), jax.ShapeDtypeStruct((M, N), BF16)), (M // tm, 1),
        pl.BlockSpec((tm, K), lambda i, k: (i, 0)), pl.BlockSpec(w.shape, lambda i, k: (0, 0)),
        (row, row), (tm, N), [x, g], [row, pl.BlockSpec((1, N), lambda i, k: (0, 0))],
        epilogue=_residual_rms_epilogue)


def _mm_tn(name, a, b, out_dtype=F32, tk=MM_TOKENS, comm=None):
    T, M = a.shape
    N = b.shape[1]
    tk = _tile(T, tk)
    return _matmul(
        name, "tn", a, b, jax.ShapeDtypeStruct((M, N), out_dtype), (1, T // tk),
        pl.BlockSpec((tk, M), lambda i, k: (k, 0)), pl.BlockSpec((tk, N), lambda i, k: (k, 0)),
        pl.BlockSpec((M, N), lambda i, k: (0, 0)), (M, N), comm=comm)


def _mm_in_gather(h, shard, tm=MM_ROWS, comm=None):
    T, K = h.shape
    n = shard.shape[1]
    tm = _tile(T, tm)
    n_tiles = T // tm
    pair_of_chip_step = {4: 1, 2: 2, 6: 3}

    def slab_of(s):
        x, y, c = _mesh_pos()
        return _dev_index(x ^ ((s >> 2) & 1), y ^ ((s >> 1) & 1), c ^ (s & 1))

    def body(h_ref, shard_ref, proj_ref, win_ref, wbuf, slot_sems, send_sems, recv_sems, local_sem):
        s, i = pl.program_id(0), pl.program_id(1)
        x, y, c = _mesh_pos()
        me, sib = (x, y, c), (x, y, 1 - c)

        def slot_copy(step):
            src = shard_ref if step == 0 else win_ref.at[slab_of(step)]
            return pltpu.make_async_copy(src, wbuf.at[step % 2], slot_sems.at[step % 2])

        def fetch(step):
            if step >= 1:
                block = win_ref.at[slab_of(step)]
                if step == 1:
                    pair = 0
                elif step % 2 == 0:
                    pair = pair_of_chip_step[step]
                else:
                    pair = 3 + pair_of_chip_step[step - 1]
                _remote(block, block, send_sems, recv_sems, pair, me).wait_recv()
                if step % 2 == 0:
                    _remote(block, block, send_sems, recv_sems, 3 + pair, sib).start()
            slot_copy(step).start()

        @pl.when((s == 0) & (i == 0))
        def _():
            land = win_ref.at[_dev_index(*me)]
            pltpu.make_async_copy(shard_ref, land, local_sem).start()
            _remote(shard_ref, land, send_sems, recv_sems, 0, sib).start()
            for step, pair in pair_of_chip_step.items():
                peer = (x ^ ((step >> 2) & 1), y ^ ((step >> 1) & 1), c)
                _remote(shard_ref, land, send_sems, recv_sems, pair, peer).start()
            fetch(0)

        for step in range(N_DEV):
            @pl.when((s == step) & (i == 0))
            def _():
                slot_copy(step).wait()

            if step + 1 < N_DEV:
                @pl.when((s == step) & (i == n_tiles - 1))
                def _():
                    fetch(step + 1)

        proj_ref[...] = _dot(h_ref[...], wbuf[s % 2])

        @pl.when((s == N_DEV - 1) & (i == n_tiles - 1))
        def _():
            for pair in range(7):
                _remote(shard_ref, win_ref.at[0], send_sems, recv_sems, pair, me).wait_send()
            pltpu.make_async_copy(shard_ref, win_ref.at[_dev_index(*me)], local_sem).wait()

    hbm = pl.BlockSpec(memory_space=pltpu.HBM)
    return _pcall(
        body, (h, shard), name="mm_in",
        out_shape=(jax.ShapeDtypeStruct((T, N_DEV * n), F32), jax.ShapeDtypeStruct((N_DEV, K, n), shard.dtype)),
        grid=(N_DEV, n_tiles), in_specs=[pl.BlockSpec((tm, K), lambda s, i: (i, 0)), hbm],
        out_specs=(pl.BlockSpec((tm, n), lambda s, i: (i, slab_of(s))), hbm),
        scratch_shapes=[pltpu.VMEM((2, K, n), shard.dtype), pltpu.SemaphoreType.DMA((2,)),
                        pltpu.SemaphoreType.DMA((7,)), pltpu.SemaphoreType.DMA((7,)), pltpu.SemaphoreType.DMA],
        sem=("arbitrary", "arbitrary"), comm=comm)


def _rms_bwd_epilogue(dh, operands, outputs, first):
    x_ref, g_ref, dres_ref = operands
    dx_ref, dg_ref = outputs
    xv = x_ref[...]
    r = lax.rsqrt(jnp.mean(xv * xv, axis=-1, keepdims=True) + EPS)
    xhat = xv * r

    @pl.when(first)
    def _():
        dg_ref[...] = jnp.zeros_like(dg_ref)

    dg_ref[...] += jnp.sum(dh * xhat, axis=0, keepdims=True)
    dxhat = dh * g_ref[...]
    dx_ref[...] = dres_ref[...] + r * (dxhat - xhat * jnp.mean(dxhat * xhat, axis=-1, keepdims=True))


def _rms_bwd_fused(M, K, tm, rms):
    row = pl.BlockSpec((tm, K), lambda i, j: (i, 0))
    vec = pl.BlockSpec((1, K), lambda i, j: (0, 0))
    x, g, dres = rms
    return dict(res=[x, g, dres], res_spec=[row, vec, row], epilogue=_rms_bwd_epilogue,
                out_shape=(jax.ShapeDtypeStruct((M, K), F32), jax.ShapeDtypeStruct((1, K), F32)), o_spec=(row, vec))


def _mm_cols_slab_t(name, a, w_slabs, rms, tm=MM_ROWS_RES, comm=None):
    M = a.shape[0]
    J, K, n = w_slabs.shape
    tm = _tile(M, tm)
    fused = _rms_bwd_fused(M, K, tm, rms)
    return _matmul(
        name, "nt", a, w_slabs, fused.pop("out_shape"), (M // tm, J),
        pl.BlockSpec((tm, n), lambda i, j: (i, j)), pl.BlockSpec((None, K, n), lambda i, j: (j, 0, 0)),
        fused.pop("o_spec"), (tm, K), comm=comm, **fused)


def _mm_tn_slab(name, a, b, n, out_dtype=F32, tk=MM_TOKENS, comm=None, part=(0, 1)):
    T, M = a.shape
    p, of = part
    M //= of
    J = b.shape[1] // n
    tk = _tile(T, tk)
    return _matmul(
        name, "tn", a, b, jax.ShapeDtypeStruct((J, M, n), out_dtype), (J, T // tk),
        pl.BlockSpec((tk, M), lambda j, k: (k, p)), pl.BlockSpec((tk, n), lambda j, k: (k, j)),
        pl.BlockSpec((None, M, n), lambda j, k: (j, 0, 0)), (M, n), comm=comm)


def _rms_fwd(name, x, g, tm=512):
    T, Dm = x.shape
    tm = _tile(T, tm)

    def body(x_ref, g_ref, h_ref):
        xv = x_ref[...]
        r = lax.rsqrt(jnp.mean(xv * xv, axis=-1, keepdims=True) + EPS)
        h_ref[...] = (xv * r * g_ref[...]).astype(h_ref.dtype)

    return pl.pallas_call(
        body, out_shape=jax.ShapeDtypeStruct((T, Dm), BF16), grid=(T // tm,),
        in_specs=[pl.BlockSpec((tm, Dm), lambda i: (i, 0)), pl.BlockSpec((1, Dm), lambda i: (0, 0))],
        out_specs=pl.BlockSpec((tm, Dm), lambda i: (i, 0)), name=name, compiler_params=_params(("parallel",)),
    )(x, g)


def _rms_bwd(name, x, g, dh, dres, tm=512):
    T, Dm = x.shape
    tm = _tile(T, tm)
    want_dx = dres is not None

    def body(*refs):
        if want_dx:
            x_ref, g_ref, dh_ref, dres_ref, dx_ref, dg_ref = refs
        else:
            x_ref, g_ref, dh_ref, dg_ref = refs
        xv = x_ref[...]
        r = lax.rsqrt(jnp.mean(xv * xv, axis=-1, keepdims=True) + EPS)
        xhat = xv * r
        dhv = dh_ref[...]

        @pl.when(pl.program_id(0) == 0)
        def _():
            dg_ref[...] = jnp.zeros_like(dg_ref)

        dg_ref[...] += jnp.sum(dhv * xhat, axis=0, keepdims=True)
        if want_dx:
            dxhat = dhv * g_ref[...]
            dx_ref[...] = dres_ref[...] + r * (dxhat - xhat * jnp.mean(dxhat * xhat, axis=-1, keepdims=True))

    row = pl.BlockSpec((tm, Dm), lambda i: (i, 0))
    vec = pl.BlockSpec((1, Dm), lambda i: (0, 0))
    if want_dx:
        return pl.pallas_call(
            body, out_shape=(jax.ShapeDtypeStruct((T, Dm), F32), jax.ShapeDtypeStruct((1, Dm), F32)),
            grid=(T // tm,), in_specs=[row, vec, row, row], out_specs=(row, vec), name=name,
            compiler_params=_params(("arbitrary",)),
        )(x, g, dh, dres)
    return pl.pallas_call(
        body, out_shape=jax.ShapeDtypeStruct((1, Dm), F32), grid=(T // tm,), in_specs=[row, vec, row],
        out_specs=vec, name=name, compiler_params=_params(("arbitrary",)),
    )(x, g, dh)


def _pool_rows(S):
    return _tile(S, 256)


def _pool_count(c0, rows, w):
    t = c0 + lax.broadcasted_iota(jnp.int32, (rows, 1), 0)
    return jnp.minimum(t + 1, w).astype(F32)


def _pool_fwd(proj, w_pool, scale, B, S):
    CH = _pool_rows(S)

    def body(hp_ref, wp_ref, sc_ref, o_ref, pad_ref):
        pad_ref[0:POOL_HALO, :] = jnp.zeros((POOL_HALO, POOL_WIDTH), F32)
        pad_ref[POOL_HALO:, :] = hp_ref[...]
        for gi, w in enumerate(POOL_WINDOWS):
            cols = slice(gi * POOL_GROUP_DIM, (gi + 1) * POOL_GROUP_DIM)
            for c in range(S // CH):
                base = POOL_HALO + c * CH
                acc = pad_ref[base:base + CH, cols]
                tok = acc
                for j in range(1, w):
                    acc = acc + pad_ref[base - j:base - j + CH, cols]
                pooled = acc / _pool_count(c * CH, CH, w) - tok
                z = _dot(pooled, wp_ref[gi])
                o_ref[c * CH:(c + 1) * CH, cols] = (z * sc_ref[:, cols]).astype(o_ref.dtype)

    return pl.pallas_call(
        body, out_shape=jax.ShapeDtypeStruct((B * S, POOL_WIDTH), BF16), grid=(B,),
        in_specs=[pl.BlockSpec((S, POOL_WIDTH), lambda b: (b, 0)),
                  pl.BlockSpec(w_pool.shape, lambda b: (0, 0, 0)),
                  pl.BlockSpec((1, POOL_WIDTH), lambda b: (0, 0))],
        out_specs=pl.BlockSpec((S, POOL_WIDTH), lambda b: (b, 0)),
        scratch_shapes=[pltpu.VMEM((S + POOL_HALO, POOL_WIDTH), F32)],
        name="pool_fwd", compiler_params=_params(("parallel",)),
    )(proj, w_pool, scale)


def _pool_bwd(proj, d_ypre, w_pool, scale, B, S):
    CH = _pool_rows(S)

    def body(hp_ref, dy_ref, wp_ref, sc_ref, dhp_ref, dwp_ref, dsc_ref, pad_ref, sc_pad_ref, dp_ref):
        @pl.when(pl.program_id(0) == 0)
        def _():
            dwp_ref[...] = jnp.zeros_like(dwp_ref)
            dsc_ref[...] = jnp.zeros_like(dsc_ref)

        pad_ref[0:POOL_HALO, :] = jnp.zeros((POOL_HALO, POOL_WIDTH), F32)
        pad_ref[POOL_HALO:, :] = hp_ref[...]
        sc_pad_ref[S:, :] = jnp.zeros((POOL_HALO, POOL_WIDTH), F32)
        for gi, w in enumerate(POOL_WINDOWS):
            cols = slice(gi * POOL_GROUP_DIM, (gi + 1) * POOL_GROUP_DIM)
            for c in range(S // CH):
                base = POOL_HALO + c * CH
                rows = slice(c * CH, (c + 1) * CH)
                acc = pad_ref[base:base + CH, cols]
                tok = acc
                for j in range(1, w):
                    acc = acc + pad_ref[base - j:base - j + CH, cols]
                cnt = _pool_count(c * CH, CH, w)
                pooled = acc / cnt - tok
                z = _dot(pooled, wp_ref[gi])
                dy = dy_ref[rows, cols]
                dsc_ref[:, cols] += jnp.sum(dy * z, axis=0, keepdims=True)
                dz = dy * sc_ref[:, cols]
                dwp_ref[gi] += _dot(pooled, dz, "tn")
                dpool = _dot(dz, wp_ref[gi], "nt")
                dp_ref[rows, cols] = dpool
                sc_pad_ref[rows, cols] = dpool / cnt
            for c in range(S // CH):
                rows = slice(c * CH, (c + 1) * CH)
                acc = sc_pad_ref[rows, cols]
                for j in range(1, w):
                    acc = acc + sc_pad_ref[c * CH + j:c * CH + j + CH, cols]
                dhp_ref[rows, cols] = (acc - dp_ref[rows, cols]).astype(dhp_ref.dtype)

    seq = pl.BlockSpec((S, POOL_WIDTH), lambda b: (b, 0))
    return pl.pallas_call(
        body,
        out_shape=(jax.ShapeDtypeStruct((B * S, POOL_WIDTH), BF16),
                   jax.ShapeDtypeStruct(w_pool.shape, F32), jax.ShapeDtypeStruct((1, POOL_WIDTH), F32)),
        grid=(B,),
        in_specs=[seq, seq, pl.BlockSpec(w_pool.shape, lambda b: (0, 0, 0)),
                  pl.BlockSpec((1, POOL_WIDTH), lambda b: (0, 0))],
        out_specs=(seq, pl.BlockSpec(w_pool.shape, lambda b: (0, 0, 0)),
                   pl.BlockSpec((1, POOL_WIDTH), lambda b: (0, 0))),
        scratch_shapes=[pltpu.VMEM((S + POOL_HALO, POOL_WIDTH), F32),
                        pltpu.VMEM((S + POOL_HALO, POOL_WIDTH), F32),
                        pltpu.VMEM((S, POOL_WIDTH), F32)],
        name="pool_bwd", compiler_params=_params(("arbitrary",)),
    )(proj, d_ypre, w_pool, scale)


def _ret_tables(S):
    half = RET_QK_DIM // 2
    inv = ROPE_BASE ** (-jnp.arange(half, dtype=F32) / half)
    ang = jnp.arange(S, dtype=F32)[:, None] * inv[None, :]
    cos, sin = jnp.cos(ang), jnp.sin(ang)
    cos_full = jnp.concatenate([cos, cos], axis=-1)
    sin_signed = jnp.concatenate([-sin, sin], axis=-1)
    C = RET_CHUNK
    lg = jnp.log1p(-jnp.exp2(-5.0 - jnp.arange(RET_HEADS, dtype=F32)))[:, None, None]
    idx = jnp.arange(C, dtype=F32)
    rel = idx[:, None] - idx[None, :]
    decay = jnp.where(rel >= 0, jnp.exp(jnp.maximum(rel, 0.0) * lg), 0.0)
    q_decay = jnp.broadcast_to(jnp.exp((idx + 1.0)[None, :, None] * lg), (RET_HEADS, C, RET_QK_DIM))
    k_decay = jnp.broadcast_to(jnp.exp((C - 1.0 - idx)[None, :, None] * lg), (RET_HEADS, C, RET_QK_DIM))
    c_decay = jnp.broadcast_to(jnp.exp(C * lg), (RET_HEADS, 1, RET_V_DIM))
    return cos_full, sin_signed, decay, q_decay, k_decay, c_decay


def _rope(x, cos_full, sin_signed):
    return x * cos_full + pltpu.roll(x, RET_QK_DIM // 2, axis=1) * sin_signed


def _rope_t(dy, cos_full, sin_signed):
    return dy * cos_full + pltpu.roll(dy * sin_signed, RET_QK_DIM // 2, axis=1)


RET_COLS = 512


def _ret_specs(N, chunk_of):
    C = RET_CHUNK

    def rows(width, col=0):
        return pl.BlockSpec((C, width), lambda b, i: (b * N + chunk_of(i), col))

    def whole(shape):
        return pl.BlockSpec(shape, lambda b, i: (0,) * len(shape))

    wide = RET_HEADS * RET_V_DIM
    return dict(
        q=rows(RET_COLS, COL_Q // RET_COLS), k=rows(RET_COLS, COL_K // RET_COLS),
        v=[rows(RET_COLS, COL_V // RET_COLS + j) for j in range(2)],
        gr=[rows(RET_COLS, COL_GR // RET_COLS + j) for j in range(2)],
        table=pl.BlockSpec((C, RET_QK_DIM), lambda b, i: (chunk_of(i), 0)),
        decay=whole((RET_HEADS, C, C)), qd=whole((RET_HEADS, C, RET_QK_DIM)), kd=whole((RET_HEADS, C, RET_QK_DIM)),
        cd=whole((RET_HEADS, 1, RET_V_DIM)), vec=whole((1, wide)), qk_rows=rows(RET_COLS), v_rows=rows(wide),
        state=pl.BlockSpec((None, None, RET_HEADS, RET_QK_DIM, RET_V_DIM), lambda b, i: (b, chunk_of(i), 0, 0, 0)))


def _head_cols(h):
    pair = slice((h % 2) * RET_V_DIM, (h % 2 + 1) * RET_V_DIM)
    return slice(h * RET_QK_DIM, (h + 1) * RET_QK_DIM), h // 2, pair, slice(h * RET_V_DIM, (h + 1) * RET_V_DIM)


def _group_norm(o):
    mu = jnp.mean(o, axis=-1, keepdims=True)
    oc = o - mu
    rstd = lax.rsqrt(jnp.mean(oc * oc, axis=-1, keepdims=True) + EPS)
    return oc * rstd, rstd


def _ret_fwd(proj, g_ret, b_ret, tables, B, S, comm=None):
    N = S // RET_CHUNK
    cos_t, sin_t, decay, q_decay, k_decay, c_decay = tables
    sp = _ret_specs(N, lambda i: i)

    def body(q_ref, k_ref, v0_ref, v1_ref, gr0_ref, gr1_ref, cos_ref, sin_ref, dec_ref, qd_ref, kd_ref, cd_ref,
             g_ref, b_ref, y_ref, rs_ref, r_ref):
        @pl.when(pl.program_id(1) == 0)
        def _():
            r_ref[...] = jnp.zeros_like(r_ref)

        cs, sn = cos_ref[...], sin_ref[...]
        for h in range(RET_HEADS):
            qk, j, pair, wide = _head_cols(h)
            q = _rope(q_ref[:, qk], cs, sn)
            k = _rope(k_ref[:, qk], cs, sn) * (RET_QK_DIM ** -0.5)
            v = (v0_ref, v1_ref)[j][:, pair]
            R = r_ref[h]
            rs_ref[h] = R
            s = _dot(q, k, "nt") * dec_ref[h]
            o = _dot(s, v) + _dot(q * qd_ref[h], R)
            r_ref[h] = cd_ref[h] * R + _dot(k * kd_ref[h], v, "tn")
            on, _ = _group_norm(o)
            gr = (gr0_ref, gr1_ref)[j][:, pair]
            y_ref[:, wide] = (gr * jax.nn.sigmoid(gr) * (on * g_ref[:, wide] + b_ref[:, wide])).astype(y_ref.dtype)

    state = jax.ShapeDtypeStruct((B, N, RET_HEADS, RET_QK_DIM, RET_V_DIM), F32)
    return _pcall(
        body, (proj,) * 6 + (cos_t, sin_t, decay, q_decay, k_decay, c_decay, g_ret, b_ret),
        name="ret_fwd", out_shape=(jax.ShapeDtypeStruct((B * S, RET_HEADS * RET_V_DIM), BF16), state), grid=(B, N),
        in_specs=[sp["q"], sp["k"], *sp["v"], *sp["gr"], sp["table"], sp["table"], sp["decay"], sp["qd"],
                  sp["kd"], sp["cd"], sp["vec"], sp["vec"]],
        out_specs=(sp["v_rows"], sp["state"]),
        scratch_shapes=[pltpu.VMEM((RET_HEADS, RET_QK_DIM, RET_V_DIM), F32)],
        sem=("parallel", "arbitrary"), comm=comm)


def _ret_bwd(proj, states, d_yr, g_ret, b_ret, tables, B, S, comm=None):
    N = S // RET_CHUNK
    cos_t, sin_t, decay, q_decay, k_decay, c_decay = tables
    sp = _ret_specs(N, lambda i: N - 1 - i)
    qk_scale = RET_QK_DIM ** -0.5

    def body(q_ref, k_ref, v0_ref, v1_ref, gr0_ref, gr1_ref, dy_ref, rs_ref, cos_ref, sin_ref, dec_ref, qd_ref,
             kd_ref, cd_ref, g_ref, b_ref, dq_ref, dk_ref, dv_ref, dgr_ref, dg_ref, db_ref, dr_ref):
        @pl.when((pl.program_id(0) == 0) & (pl.program_id(1) == 0))
        def _():
            dg_ref[...] = jnp.zeros_like(dg_ref)
            db_ref[...] = jnp.zeros_like(db_ref)

        @pl.when(pl.program_id(1) == 0)
        def _():
            dr_ref[...] = jnp.zeros_like(dr_ref)

        cs, sn = cos_ref[...], sin_ref[...]
        for h in range(RET_HEADS):
            qk, j, pair, wide = _head_cols(h)
            q = _rope(q_ref[:, qk], cs, sn)
            k = _rope(k_ref[:, qk], cs, sn) * qk_scale
            v = (v0_ref, v1_ref)[j][:, pair]
            R, dR = rs_ref[h], dr_ref[h]
            dec, qd, kd = dec_ref[h], qd_ref[h], kd_ref[h]
            s = _dot(q, k, "nt") * dec
            o = _dot(s, v) + _dot(q * qd, R)
            on, rstd = _group_norm(o)
            g = g_ref[:, wide]
            oaff = on * g + b_ref[:, wide]
            gr = (gr0_ref, gr1_ref)[j][:, pair]
            sg = jax.nn.sigmoid(gr)
            dy = dy_ref[:, wide]
            dgr_ref[:, wide] = (dy * oaff * (sg * (1.0 + gr * (1.0 - sg)))).astype(dgr_ref.dtype)
            doaff = dy * (gr * sg)
            dg_ref[:, wide] += jnp.sum(doaff * on, axis=0, keepdims=True)
            db_ref[:, wide] += jnp.sum(doaff, axis=0, keepdims=True)
            don = doaff * g
            do = rstd * (don - jnp.mean(don, axis=-1, keepdims=True)
                         - on * jnp.mean(don * on, axis=-1, keepdims=True))
            ds = _dot(do, v, "nt") * dec
            dq = _dot(ds, k) + qd * _dot(do, R, "nt")
            dk = _dot(ds, q, "tn") + kd * _dot(v, dR, "nt")
            dv_ref[:, wide] = (_dot(s, do, "tn") + _dot(k * kd, dR)).astype(dv_ref.dtype)
            dr_ref[h] = cd_ref[h] * dR + _dot(q * qd, do, "tn")
            dq_ref[:, qk] = _rope_t(dq, cs, sn).astype(dq_ref.dtype)
            dk_ref[:, qk] = _rope_t(dk * qk_scale, cs, sn).astype(dk_ref.dtype)

    T = B * S
    qk_shape = jax.ShapeDtypeStruct((T, RET_HEADS * RET_QK_DIM), BF16)
    v_shape = jax.ShapeDtypeStruct((T, RET_HEADS * RET_V_DIM), BF16)
    vec_shape = jax.ShapeDtypeStruct((1, RET_HEADS * RET_V_DIM), F32)
    return _pcall(
        body, (proj,) * 6 + (d_yr, states, cos_t, sin_t, decay, q_decay, k_decay, c_decay, g_ret, b_ret),
        name="ret_bwd", out_shape=(qk_shape, qk_shape, v_shape, v_shape, vec_shape, vec_shape), grid=(B, N),
        in_specs=[sp["q"], sp["k"], *sp["v"], *sp["gr"], sp["v_rows"], sp["state"], sp["table"], sp["table"],
                  sp["decay"], sp["qd"], sp["kd"], sp["cd"], sp["vec"], sp["vec"]],
        out_specs=(sp["qk_rows"], sp["qk_rows"], sp["v_rows"], sp["v_rows"], sp["vec"], sp["vec"]),
        scratch_shapes=[pltpu.VMEM((RET_HEADS, RET_QK_DIM, RET_V_DIM), F32)],
        sem=("arbitrary", "arbitrary"), comm=comm)


def _xa_rows(S):
    return _tile(S, 256)


def _xa_specs(S, M):
    q = pl.BlockSpec((S, XA_HEAD_DIM), lambda b, h: (b, COL_QX // XA_HEAD_DIM + h))
    k = pl.BlockSpec((M, XA_HEAD_DIM), lambda b, h: (b, h))
    v = pl.BlockSpec((M, XA_HEAD_DIM), lambda b, h: (b, XA_HEADS + h))
    o = pl.BlockSpec((S, XA_HEAD_DIM), lambda b, h: (b, h))
    return q, k, v, o


def _softmax_rows(s):
    e = jnp.exp(s - jnp.max(s, axis=-1, keepdims=True))
    return e / jnp.sum(e, axis=-1, keepdims=True)


def _xa_fwd(proj, kv, B, S, M, comm=None):
    CH = _xa_rows(S)
    q_spec, k_spec, v_spec, o_spec = _xa_specs(S, M)

    def body(q_ref, k_ref, v_ref, o_ref):
        def chunk(i, carry):
            rows = pl.ds(pl.multiple_of(i * CH, CH), CH)
            p = _softmax_rows(_dot(q_ref[rows, :], k_ref[...], "nt") * (XA_HEAD_DIM ** -0.5))
            o_ref[rows, :] = _dot(p, v_ref[...]).astype(o_ref.dtype)
            return carry

        lax.fori_loop(0, S // CH, chunk, 0)

    return _pcall(
        body, (proj, kv, kv), name="xattn_fwd", out_shape=jax.ShapeDtypeStruct((B * S, XA_WIDTH), BF16),
        grid=(B, XA_HEADS), in_specs=[q_spec, k_spec, v_spec], out_specs=o_spec,
        sem=("parallel", "parallel"), comm=comm)


def _xa_bwd(proj, kv, d_o, B, S, M, comm=None):
    CH = _xa_rows(S)
    q_spec, k_spec, v_spec, o_spec = _xa_specs(S, M)
    scale = XA_HEAD_DIM ** -0.5

    def body(q_ref, k_ref, v_ref, do_ref, dq_ref, dk_ref, dv_ref):
        dk_ref[...] = jnp.zeros_like(dk_ref)
        dv_ref[...] = jnp.zeros_like(dv_ref)

        def chunk(i, carry):
            rows = pl.ds(pl.multiple_of(i * CH, CH), CH)
            q, do = q_ref[rows, :], do_ref[rows, :]
            p = _softmax_rows(_dot(q, k_ref[...], "nt") * scale)
            dp = _dot(do, v_ref[...], "nt")
            ds = p * (dp - jnp.sum(dp * p, axis=-1, keepdims=True)) * scale
            dq_ref[rows, :] = _dot(ds, k_ref[...]).astype(dq_ref.dtype)
            dk_ref[...] += _dot(ds, q, "tn")
            dv_ref[...] += _dot(p, do, "tn")
            return carry

        lax.fori_loop(0, S // CH, chunk, 0)

    kv_out = pl.BlockSpec((M, XA_HEAD_DIM), lambda b, h: (b, h))
    return _pcall(
        body, (proj, kv, kv, d_o), name="xattn_bwd",
        out_shape=(jax.ShapeDtypeStruct((B * S, XA_WIDTH), BF16), jax.ShapeDtypeStruct((B * M, XA_WIDTH), F32),
                   jax.ShapeDtypeStruct((B * M, XA_WIDTH), F32)),
        grid=(B, XA_HEADS), in_specs=[q_spec, k_spec, v_spec, o_spec], out_specs=(o_spec, kv_out, kv_out),
        sem=("parallel", "parallel"), comm=comm)


def _gate_specs(tm):
    n = COL_GL // D_MODEL
    return [pl.BlockSpec((tm, D_MODEL), lambda i, j=j: (i, n + j)) for j in range(3)]


def _merge_fwd(proj, ys, tm=256, comm=None):
    T = proj.shape[0]
    tm = _tile(T, tm)
    row = pl.BlockSpec((tm, D_MODEL), lambda i: (i, 0))

    def body(g0, g1, g2, y0, y1, y2, o_ref):
        acc = jax.nn.sigmoid(g0[...]) * y0[...]
        acc = acc + jax.nn.sigmoid(g1[...]) * y1[...]
        acc = acc + jax.nn.sigmoid(g2[...]) * y2[...]
        o_ref[...] = acc.astype(o_ref.dtype)

    return _pcall(
        body, (proj, proj, proj, *ys), name="merge_fwd", out_shape=jax.ShapeDtypeStruct((T, D_MODEL), BF16),
        grid=(T // tm,), in_specs=_gate_specs(tm) + [row] * 3, out_specs=row, sem=("parallel",), comm=comm)


def _merge_bwd(proj, ys, d_merged, tm=256, comm=None):
    T = proj.shape[0]
    tm = _tile(T, tm)
    row = pl.BlockSpec((tm, D_MODEL), lambda i: (i, 0))

    def body(g0, g1, g2, y0, y1, y2, dm_ref, dgl_ref, d0, d1, d2):
        dm = dm_ref[...]
        for j, (g_ref, y_ref, d_ref) in enumerate(((g0, y0, d0), (g1, y1, d1), (g2, y2, d2))):
            sg = jax.nn.sigmoid(g_ref[...])
            d_ref[...] = (dm * sg).astype(d_ref.dtype)
            dgl_ref[:, j * D_MODEL:(j + 1) * D_MODEL] = (dm * y_ref[...] * sg * (1.0 - sg)).astype(dgl_ref.dtype)

    dy = jax.ShapeDtypeStruct((T, D_MODEL), BF16)
    return _pcall(
        body, (proj, proj, proj, *ys, d_merged), name="merge_bwd",
        out_shape=(jax.ShapeDtypeStruct((T, 3 * D_MODEL), BF16), dy, dy, dy), grid=(T // tm,),
        in_specs=_gate_specs(tm) + [row] * 4,
        out_specs=(pl.BlockSpec((tm, 3 * D_MODEL), lambda i: (i, 0)), row, row, row),
        sem=("parallel",), comm=comm)


def _gelu(x):
    return 0.5 * x * (1.0 + jnp.tanh(GELU_C * (x + GELU_A * x * x * x)))


def _gelu_grad(x):
    t = jnp.tanh(GELU_C * (x + GELU_A * x * x * x))
    return 0.5 * (1.0 + t) + 0.5 * x * (1.0 - t * t) * GELU_C * (1.0 + 3.0 * GELU_A * x * x)


GLU_HALO = 16


def _shift_down(x, prev, n):
    last = prev.shape[0]
    r = lax.broadcasted_iota(jnp.int32, (8, 1), 0)
    rolled = pltpu.roll(x, n, axis=0)
    head = rolled[0:8]
    for j in range(n):
        head = jnp.where(r == j, prev[last - n + j:last - n + j + 1, :], head)
    return jnp.concatenate([head, rolled[8:]], axis=0)


def _shift_up(x, nxt, n):
    rows = x.shape[0]
    r = lax.broadcasted_iota(jnp.int32, (8, 1), 0)
    rolled = pltpu.roll(x, rows - n, axis=0)
    tail = rolled[rows - 8:]
    for j in range(n):
        tail = jnp.where(r == 8 - n + j, nxt[j:j + 1, :], tail)
    return jnp.concatenate([rolled[:rows - 8], tail], axis=0)


def _conv(a, prev, cw, cb):
    return _shift_down(a, prev, 2) * cw[0:1, :] + _shift_down(a, prev, 1) * cw[1:2, :] + a * cw[2:3, :] + cb


def _glu_fwd(up, cw, cb, S, tm=256, comm=None):
    T = up.shape[2]
    tm = _tile(S, tm)
    per_seq = S // tm

    def body(ab_ref, prev_ref, cw_ref, cb_ref, u_ref):
        i = pl.program_id(1)
        prev = jnp.where(i % per_seq == 0, 0.0, prev_ref[...].astype(F32))
        ac = _conv(ab_ref[0].astype(F32), prev, cw_ref[...], cb_ref[...])
        u_ref[...] = (_gelu(ac) * ab_ref[1].astype(F32)).astype(u_ref.dtype)

    before = tm // GLU_HALO
    return _pcall(
        body, (up, up, cw, cb), name="glu_fwd",
        out_shape=jax.ShapeDtypeStruct((FFN_SLABS, T, UP_SHARD), BF16), grid=(FFN_SLABS, T // tm),
        in_specs=[pl.BlockSpec((2, None, tm, UP_SHARD), lambda d, i: (0, d, i, 0)),
                  pl.BlockSpec((None, None, GLU_HALO, UP_SHARD),
                               lambda d, i: (0, d, jnp.maximum(i * before - 1, 0), 0)),
                  pl.BlockSpec((None, 3, UP_SHARD), lambda d, i: (d, 0, 0)),
                  pl.BlockSpec((None, 1, UP_SHARD), lambda d, i: (d, 0, 0))],
        out_specs=pl.BlockSpec((None, tm, UP_SHARD), lambda d, i: (d, i, 0)),
        sem=("parallel", "parallel"), comm=comm)


def _glu_bwd(up, d_u, cw, cb, S, tm=256, comm=None):
    T = up.shape[2]
    tm = _tile(S, tm)
    per_seq = S // tm
    n_tiles = T // tm
    per_tile = tm // GLU_HALO

    def body(ab_ref, prev_ref, abn_ref, du_ref, dun_ref, cw_ref, cb_ref, dup_ref, dcw_ref, dcb_ref):
        i = pl.program_id(1)

        @pl.when(i == 0)
        def _():
            dcw_ref[...] = jnp.zeros_like(dcw_ref)
            dcb_ref[...] = jnp.zeros_like(dcb_ref)

        cw, cb = cw_ref[...], cb_ref[...]
        a, b = ab_ref[0].astype(F32), ab_ref[1].astype(F32)
        prev = jnp.where(i % per_seq == 0, 0.0, prev_ref[...].astype(F32))
        a2, a1 = _shift_down(a, prev, 2), _shift_down(a, prev, 1)
        ac = a2 * cw[0:1, :] + a1 * cw[1:2, :] + a * cw[2:3, :] + cb
        du = du_ref[...].astype(F32)
        dup_ref[1] = (du * _gelu(ac)).astype(dup_ref.dtype)
        dac = du * b * _gelu_grad(ac)
        dcb_ref[...] += jnp.sum(dac, axis=0, keepdims=True)
        dcw_ref[0:1, :] += jnp.sum(dac * a2, axis=0, keepdims=True)
        dcw_ref[1:2, :] += jnp.sum(dac * a1, axis=0, keepdims=True)
        dcw_ref[2:3, :] += jnp.sum(dac * a, axis=0, keepdims=True)
        acn = _conv(abn_ref[0].astype(F32), a[tm - GLU_HALO:, :], cw, cb)
        dacn = jnp.where(i % per_seq == per_seq - 1, 0.0,
                         dun_ref[...].astype(F32) * abn_ref[1].astype(F32) * _gelu_grad(acn))
        da = dac * cw[2:3, :] + _shift_up(dac, dacn, 1) * cw[1:2, :] + _shift_up(dac, dacn, 2) * cw[0:1, :]
        dup_ref[0] = da.astype(dup_ref.dtype)

    def nxt(i):
        return jnp.minimum((i + 1) * per_tile, T // GLU_HALO - 1)

    return _pcall(
        body, (up, up, up, d_u, d_u, cw, cb), name="glu_bwd",
        out_shape=(jax.ShapeDtypeStruct((2, FFN_SLABS, T, UP_SHARD), BF16),
                   jax.ShapeDtypeStruct((FFN_SLABS, 3, UP_SHARD), F32),
                   jax.ShapeDtypeStruct((FFN_SLABS, 1, UP_SHARD), F32)),
        grid=(FFN_SLABS, n_tiles),
        in_specs=[pl.BlockSpec((2, None, tm, UP_SHARD), lambda d, i: (0, d, i, 0)),
                  pl.BlockSpec((None, None, GLU_HALO, UP_SHARD),
                               lambda d, i: (0, d, jnp.maximum(i * per_tile - 1, 0), 0)),
                  pl.BlockSpec((2, None, GLU_HALO, UP_SHARD), lambda d, i: (0, d, nxt(i), 0)),
                  pl.BlockSpec((None, tm, UP_SHARD), lambda d, i: (d, i, 0)),
                  pl.BlockSpec((None, GLU_HALO, UP_SHARD), lambda d, i: (d, nxt(i), 0)),
                  pl.BlockSpec((None, 3, UP_SHARD), lambda d, i: (d, 0, 0)),
                  pl.BlockSpec((None, 1, UP_SHARD), lambda d, i: (d, 0, 0))],
        out_specs=(pl.BlockSpec((2, None, tm, UP_SHARD), lambda d, i: (0, d, i, 0)),
                   pl.BlockSpec((None, 3, UP_SHARD), lambda d, i: (d, 0, 0)),
                   pl.BlockSpec((None, 1, UP_SHARD), lambda d, i: (d, 0, 0))),
        sem=("parallel", "arbitrary"), comm=comm)


def _mm_up(h2, w_up_t, tm=MM_ROWS):
    T, K = h2.shape
    tm = _tile(T, tm)
    return _matmul(
        "mm_up", "nt", h2, w_up_t, jax.ShapeDtypeStruct((N_DEV, T, UP_SHARD), BF16), (N_DEV, T // tm, 1),
        pl.BlockSpec((tm, K), lambda j, i, k: (i, 0)), pl.BlockSpec((None, UP_SHARD, K), lambda j, i, k: (j, 0, 0)),
        pl.BlockSpec((None, tm, UP_SHARD), lambda j, i, k: (j, i, 0)), (tm, UP_SHARD))


def _loss_epilogue(ffn, operands, outputs, first):
    x1_ref, t_ref, g_ref = operands
    dx_ref, dg_ref, loss_ref = outputs

    @pl.when(first)
    def _():
        dg_ref[...] = jnp.zeros_like(dg_ref)
        loss_ref[...] = jnp.zeros_like(loss_ref)

    xv = x1_ref[...] + ffn
    r = lax.rsqrt(jnp.mean(xv * xv, axis=-1, keepdims=True) + EPS)
    xhat = xv * r
    err = xhat * g_ref[...] - t_ref[...]
    loss_ref[...] += (0.5 / D_MODEL) * jnp.sum(err * err)
    dy = err * (1.0 / D_MODEL)
    dg_ref[...] += jnp.sum(dy * xhat, axis=0, keepdims=True)
    dxhat = dy * g_ref[...]
    dx_ref[...] = r * (dxhat - xhat * jnp.mean(dxhat * xhat, axis=-1, keepdims=True))


def _mm_down_loss(u, w_down, x1, target, g_final, tm=MM_ROWS_RES):
    J, T, n = u.shape
    tm = _tile(T, tm)
    row = pl.BlockSpec((tm, D_MODEL), lambda i, d: (i, 0))
    vec = pl.BlockSpec((1, D_MODEL), lambda i, d: (0, 0))
    vec_shape = jax.ShapeDtypeStruct((1, D_MODEL), F32)
    return _matmul(
        "mm_down", "nn", u, w_down, (jax.ShapeDtypeStruct((T, D_MODEL), F32), vec_shape, vec_shape), (T // tm, J),
        pl.BlockSpec((None, tm, n), lambda i, d: (d, i, 0)), pl.BlockSpec((None, n, D_MODEL), lambda i, d: (d, 0, 0)),
        (row, vec, vec), (tm, D_MODEL), [x1, target, g_final], [row, row, vec], epilogue=_loss_epilogue)


def _mm_down_t(dx, w_down, tm=MM_ROWS):
    T = dx.shape[0]
    J, n, _ = w_down.shape
    tm = _tile(T, tm)
    return _matmul(
        "mm_down_t", "nt", dx, w_down, jax.ShapeDtypeStruct((J, T, n), BF16), (J, T // tm, 1),
        pl.BlockSpec((tm, D_MODEL), lambda d, i, k: (i, 0)), pl.BlockSpec((None, n, D_MODEL), lambda d, i, k: (d, 0, 0)),
        pl.BlockSpec((None, tm, n), lambda d, i, k: (d, i, 0)), (tm, n))


def _mm_dw_down(u, dx, tk=MM_TOKENS):
    J, T, n = u.shape
    tk = _tile(T, tk)
    return _matmul(
        "mm_dw_down", "tn", u, dx, jax.ShapeDtypeStruct((J, n, D_MODEL), BF16), (J, T // tk),
        pl.BlockSpec((None, tk, n), lambda d, k: (d, k, 0)), pl.BlockSpec((tk, D_MODEL), lambda d, k: (k, 0)),
        pl.BlockSpec((None, n, D_MODEL), lambda d, k: (d, 0, 0)), (n, D_MODEL))


def _mm_dw_up(h2, d_up, tk=MM_TOKENS):
    T, K = h2.shape
    tk = _tile(T, tk)
    return _matmul(
        "mm_dw_up", "tn", d_up, h2, jax.ShapeDtypeStruct((N_DEV, UP_SHARD, K), BF16), (N_DEV, T // tk),
        pl.BlockSpec((None, tk, UP_SHARD), lambda j, k: (j, k, 0)), pl.BlockSpec((tk, K), lambda j, k: (k, 0)),
        pl.BlockSpec((None, UP_SHARD, K), lambda j, k: (j, 0, 0)), (UP_SHARD, K))


def _mm_up_t(d_up, w_up_t, rms, tm=MM_ROWS_RES, comm=None):
    J, T, n = d_up.shape
    K = w_up_t.shape[2]
    tm = _tile(T, tm)
    fused = _rms_bwd_fused(T, K, tm, rms)
    return _matmul(
        "mm_up_t", "nn", d_up, w_up_t, fused.pop("out_shape"), (T // tm, J),
        pl.BlockSpec((None, tm, n), lambda i, j: (j, i, 0)), pl.BlockSpec((None, n, K), lambda i, j: (j, 0, 0)),
        fused.pop("o_spec"), (tm, K), comm=comm, **fused)


def _cast_shards(shards):
    def body(*refs):
        n = len(refs) // 2
        for src, dst in zip(refs[:n], refs[n:]):
            dst[...] = src[...].astype(dst.dtype)

    return pl.pallas_call(
        body, out_shape=[jax.ShapeDtypeStruct(s.shape, BF16) for s in shards], name="cast_shards",
        compiler_params=pltpu.CompilerParams(vmem_limit_bytes=VMEM_LIMIT),
    )(*shards)


def _adamw(w, g, m, v):
    m = ADAM_B1 * m + (1.0 - ADAM_B1) * g
    v = ADAM_B2 * v + (1.0 - ADAM_B2) * (g * g)
    m_hat = m / (1.0 - ADAM_B1 ** ADAM_STEP)
    v_hat = v / (1.0 - ADAM_B2 ** ADAM_STEP)
    delta = -ADAM_LR * (m_hat / (jnp.sqrt(v_hat) + ADAM_EPS) + ADAM_WD * w)
    return delta, m, v


def _sum_parts(p_ref):
    g = p_ref[0].astype(F32)
    for d in range(1, N_DEV):
        g = g + p_ref[d].astype(F32)
    return g


def _reduce_adam(name, parts, w, m, v, tr=128):
    R, Cn = w.shape
    by_rows = sum(p.shape[1] for p in parts) == R and len(parts) > 1
    tr = math.gcd(tr, *[p.shape[1] for p in parts])
    n_tiles = [p.shape[1] // tr for p in parts]
    first = [sum(n_tiles[:j]) for j in range(len(parts))] if by_rows else [0] * len(parts)

    def body(*refs):
        p_refs = refs[:len(parts)]
        w_ref, m_ref, v_ref, g_out, d_out, m_out, v_out = refs[len(parts):]

        def update(p_ref):
            g = _sum_parts(p_ref)
            delta, m_new, v_new = _adamw(w_ref[...], g, m_ref[...], v_ref[...])
            g_out[...] = g
            d_out[...] = delta
            m_out[...] = m_new
            v_out[...] = v_new

        if len(parts) == 1:
            update(p_refs[0])
        elif by_rows:
            i = pl.program_id(0)
            for p_ref, t0, n in zip(p_refs, first, n_tiles):
                pl.when((i >= t0) & (i < t0 + n))(functools.partial(update, p_ref))
        else:
            c = lax.axis_index("c")
            for side, p_ref in enumerate(p_refs):
                pl.when(c == side)(functools.partial(update, p_ref))

    def part_spec(t0, n):
        return pl.BlockSpec((N_DEV, tr, Cn), lambda i: (0, jnp.clip(i - t0, 0, n - 1), 0))

    row = pl.BlockSpec((tr, Cn), lambda i: (i, 0))
    shape = jax.ShapeDtypeStruct((R, Cn), F32)
    return pl.pallas_call(
        body, out_shape=(shape,) * 4, grid=(R // tr,),
        in_specs=[part_spec(t0, n) for t0, n in zip(first, n_tiles)] + [row, row, row],
        out_specs=(row,) * 4, name=name, compiler_params=_params(("parallel",)),
    )(*parts, w, m, v)


def _small_adam(name, gathered, params):
    n_g, n_p = len(gathered), len(params)

    def body(*refs):
        g_refs = refs[:n_g]
        wmv = refs[n_g:n_g + 3 * n_p]
        sums = refs[n_g + 3 * n_p:2 * n_g + 3 * n_p]
        upd = refs[2 * n_g + 3 * n_p:]
        for j in range(n_g):
            g = _sum_parts(g_refs[j])
            sums[j][...] = g
            if j < n_p:
                w_ref, m_ref, v_ref = wmv[3 * j:3 * j + 3]
                delta, m_new, v_new = _adamw(w_ref[...], g, m_ref[...], v_ref[...])
                upd[3 * j][...] = delta
                upd[3 * j + 1][...] = m_new
                upd[3 * j + 2][...] = v_new

    flat = [a for wmv in params for a in wmv]
    out_shape = [jax.ShapeDtypeStruct(g.shape[1:], F32) for g in gathered]
    out_shape += [jax.ShapeDtypeStruct(a.shape, F32) for a in flat]
    res = pl.pallas_call(body, out_shape=out_shape, name=name)(*gathered, *flat)
    return res[:n_g], [tuple(res[n_g + 3 * j:n_g + 3 * j + 3]) for j in range(n_p)]


def _adam_only(name, g, w, m, v):
    def body(g_ref, w_ref, m_ref, v_ref, d_out, m_out, v_out):
        delta, m_new, v_new = _adamw(w_ref[...], g_ref[...], m_ref[...], v_ref[...])
        d_out[...] = delta
        m_out[...] = m_new
        v_out[...] = v_new

    shape = jax.ShapeDtypeStruct(w.shape, F32)
    return pl.pallas_call(body, out_shape=(shape,) * 3, name=name)(g, w, m, v)


def kernel(x, mem, g_mix, w_in, w_pool, pool_scale, w_a, g_ret, b_ret, w_r, g_mem, w_mem_kv, w_c, w_out, g_ffn, w_up, conv_w, conv_b, w_down, g_final, loss_target, m_g_mix, m_w_in, m_w_pool, m_pool_scale, m_w_a, m_g_ret, m_b_ret, m_w_r, m_g_mem, m_w_mem_kv, m_w_c, m_w_out, m_g_ffn, m_w_up, m_conv_w, m_conv_b, m_w_down, m_g_final, v_g_mix, v_w_in, v_w_pool, v_pool_scale, v_w_a, v_g_ret, v_b_ret, v_w_r, v_g_mem, v_w_mem_kv, v_w_c, v_w_out, v_g_ffn, v_w_up, v_conv_w, v_conv_b, v_w_down, v_g_final):
    B, S, _ = x.shape
    M = mem.shape[1]
    T = B * S
    me = _my_index()
    x2d = x.reshape(T, D_MODEL)
    mem2d = mem.reshape(B * M, D_MODEL)
    tgt2d = loss_target.reshape(T, D_MODEL)
    g_final2 = g_final.reshape(1, D_MODEL)

    big = dict(w_in=w_in[0], w_a=w_a[0], w_r=w_r[0], w_mem_kv=w_mem_kv[0], w_c=w_c[0], w_out=w_out[0],
               w_up=w_up[0].T, w_down=w_down[0])
    names = list(big)
    cast = dict(zip(names, _cast_shards([big[n] for n in names])))
    cb = conv_b[0].reshape(FFN_SLABS, 1, UP_SHARD)
    wp = w_pool[0]
    tables = _ret_tables(S)

    h = _rms_fwd("rms_mix", x2d, g_mix)
    early = ("w_a", "w_r", "w_mem_kv", "w_c", "w_out")
    (proj, Win), landed = _mm_in_gather(h, cast["w_in"], comm=_Gather([cast[n] for n in early] + [conv_w[0]]))
    W = dict(zip(early, landed))
    (yr, ret_states), (Wup,) = _ret_fwd(proj, g_ret, b_ret, tables, B, S, comm=_Gather([cast["w_up"]]))
    cw_full = landed[-1].transpose(1, 0, 2).reshape(3, FFN_HIDDEN)
    cw = cw_full.reshape(3, FFN_SLABS, UP_SHARD).transpose(1, 0, 2)
    Wa = W["w_a"].transpose(1, 0, 2).reshape(POOL_WIDTH, D_MODEL)
    Wc = W["w_c"].transpose(1, 0, 2).reshape(XA_WIDTH, D_MODEL)
    Wr = W["w_r"].reshape(D_MODEL, D_MODEL)
    Wkv = W["w_mem_kv"].reshape(D_MODEL, D_MODEL)
    Wout = W["w_out"].reshape(D_MODEL, D_MODEL)
    ypre = _pool_fwd(proj, wp, pool_scale, B, S)
    y_pool = _mm_rows("mm_a", ypre, Wa, BF16)
    y_ret = _mm_rows("mm_r", yr, Wr, BF16)
    mem_n = _rms_fwd("rms_mem", mem2d, g_mem)
    kv = _mm_rows("mm_kv", mem_n, Wkv)
    o_mem, (Wdown,) = _xa_fwd(proj, kv, B, S, M, comm=_Gather([cast["w_down"]]))
    Wdown = Wdown.reshape(FFN_SLABS, UP_SHARD, D_MODEL)
    y_mem = _mm_rows("mm_c", o_mem, Wc, BF16)
    ys = (y_pool, y_ret, y_mem)
    merged = _merge_fwd(proj, ys)[0]
    x1, h2 = _mm_residual_rms("mm_out", merged, Wout, x2d, g_ffn)
    up = _mm_up(h2, Wup).reshape(2, FFN_SLABS, T, UP_SHARD)
    u = _glu_fwd(up, cw, cb, S)[0]

    dx2, dg_final, loss_part = _mm_down_loss(u, Wdown, x1, tgt2d, g_final2)
    received = {}
    d_u = _mm_down_t(dx2, Wdown)
    dW_down = _mm_dw_down(u, dx2)
    (d_up, d_cw, d_cb), (received["w_down"],) = _glu_bwd(
        up, d_u, cw, cb, S, comm=_Exchange([dW_down.reshape(N_DEV, -1, D_MODEL)]))
    d_up = d_up.reshape(N_DEV, T, UP_SHARD)
    dW_up = _mm_dw_up(h2, d_up)
    (dx1, dg_ffn), (up_c0,) = _mm_up_t(d_up, Wup, (x1, g_ffn, dx2), comm=_ExchangeTo([dW_up], 0))
    d_merged = _mm_rows("mm_out_t", dx1, Wout, kind="nt")
    dW_out = _mm_tn("mm_dw_out", merged, dx1, BF16)
    (d_gl, d_y_pool, d_y_ret, d_y_mem), (received["w_out"],) = _merge_bwd(
        proj, ys, d_merged, comm=_Exchange([dW_out.reshape(N_DEV, -1, D_MODEL)]))
    dW_c = _mm_tn("mm_dw_c", o_mem, d_y_mem, BF16)
    d_o_mem = _mm_rows("mm_c_t", d_y_mem, Wc, kind="nt")
    (d_qx, d_kmem, d_vmem), (up_c1,) = _xa_bwd(proj, kv, d_o_mem, B, S, M, comm=_ExchangeTo([dW_up], 1))
    received["w_up"] = [up_c0, up_c1]
    d_kv = jnp.concatenate([d_kmem, d_vmem], axis=1)
    dW_kv = _mm_tn("mm_dw_kv", mem_n, d_kv, BF16)
    d_mem_n = _mm_rows("mm_kv_t", d_kv, Wkv, kind="nt")
    dg_mem = _rms_bwd("rms_mem_bwd", mem2d, g_mem, d_mem_n, None)
    dW_a = _mm_tn("mm_dw_a", ypre, d_y_pool, BF16)
    d_ypre = _mm_rows("mm_a_t", d_y_pool, Wa, kind="nt")
    d_hp, dw_pool, d_scale = _pool_bwd(proj, d_ypre, wp, pool_scale, B, S)
    dW_r = _mm_tn("mm_dw_r", yr, d_y_ret, BF16)
    d_yr = _mm_rows("mm_r_t", d_y_ret, Wr, kind="nt")
    (d_q, d_k, d_v, d_gr, dg_ret, db_ret), landed = _ret_bwd(
        proj, ret_states, d_yr, g_ret, b_ret, tables, B, S,
        comm=_Exchange([dW_a.reshape(POOL_WIDTH, N_DEV, -1).transpose(1, 0, 2), dW_r.reshape(N_DEV, -1, D_MODEL),
                        dW_c.reshape(XA_WIDTH, N_DEV, -1).transpose(1, 0, 2), dW_kv.reshape(N_DEV, -1, D_MODEL)]))
    received["w_a"], received["w_r"], received["w_c"], received["w_mem_kv"] = landed
    small_names = ["w_pool", "pool_scale", "g_ret", "b_ret", "g_mem", "g_ffn", "conv_b", "g_final"]
    small_grads = [dw_pool, d_scale, dg_ret, db_ret, dg_mem, dg_ffn, d_cb.reshape(1, FFN_HIDDEN), dg_final,
                   d_cw.transpose(1, 0, 2).reshape(3, FFN_HIDDEN), loss_part]
    d_proj = jnp.concatenate([d_hp, d_q, d_k, d_v, d_gr, d_qx, d_gl], axis=1)
    dW_in0, small_all = _mm_tn_slab("mm_dw_in0", h[:, :W_IN_FIRST_ROWS], d_proj, IN_SHARD, BF16,
                                    comm=_Exchange([], whole=small_grads))
    dW_in1, (in0,) = _mm_tn_slab("mm_dw_in1", h[:, W_IN_FIRST_ROWS:], d_proj, IN_SHARD, BF16,
                                 comm=_Exchange([dW_in0]))
    (grad_x, dg_mix), (in1,) = _mm_cols_slab_t("mm_in_t", d_proj, Win, (x2d, g_mix, dx1), comm=_Exchange([dW_in1]))
    received["w_in"] = [in0, in1]
    (g_mix_all,) = _comm_call("gather_g_mix", _Exchange([], whole=[dg_mix]))

    args = dict(g_mix=g_mix, w_in=w_in, w_pool=w_pool, pool_scale=pool_scale, w_a=w_a, g_ret=g_ret, b_ret=b_ret,
                w_r=w_r, g_mem=g_mem, w_mem_kv=w_mem_kv, w_c=w_c, w_out=w_out, g_ffn=g_ffn, w_up=w_up,
                conv_w=conv_w, conv_b=conv_b, w_down=w_down, g_final=g_final)
    m_in = dict(g_mix=m_g_mix, w_in=m_w_in, w_pool=m_w_pool, pool_scale=m_pool_scale, w_a=m_w_a, g_ret=m_g_ret,
                b_ret=m_b_ret, w_r=m_w_r, g_mem=m_g_mem, w_mem_kv=m_w_mem_kv, w_c=m_w_c, w_out=m_w_out,
                g_ffn=m_g_ffn, w_up=m_w_up, conv_w=m_conv_w, conv_b=m_conv_b, w_down=m_w_down, g_final=m_g_final)
    v_in = dict(g_mix=v_g_mix, w_in=v_w_in, w_pool=v_w_pool, pool_scale=v_pool_scale, w_a=v_w_a, g_ret=v_g_ret,
                b_ret=v_b_ret, w_r=v_w_r, g_mem=v_g_mem, w_mem_kv=v_w_mem_kv, w_c=v_w_c, w_out=v_w_out,
                g_ffn=v_g_ffn, w_up=v_w_up, conv_w=v_conv_w, conv_b=v_conv_b, w_down=v_w_down, g_final=v_g_final)

    grads, deltas, new_m, new_v = {}, {}, {}, {}
    for n in names:
        parts = received[n] if isinstance(received[n], list) else [received[n]]
        flip = (lambda a: a.T) if n == "w_up" else (lambda a: a)
        outs = _reduce_adam("adam_" + n, parts, big[n], flip(m_in[n][0]), flip(v_in[n][0]))
        for store, val in zip((grads, deltas, new_m, new_v), outs):
            store[n] = flip(val)[None]

    def as_small(a):
        return a.reshape(a.shape[-3:]) if a.ndim > 2 else a.reshape(1, -1)

    def small_update(call_name, param_names, gathered):
        params = [tuple(as_small(d[n]) for d in (args, m_in, v_in)) for n in param_names]
        sums, updates = _small_adam(call_name, gathered, params)
        for n, g, (d_, m_, v_) in zip(param_names, sums, updates):
            shape = args[n].shape
            grads[n], deltas[n], new_m[n], new_v[n] = (a.reshape(shape) for a in (g, d_, m_, v_))
        return sums[len(param_names):]

    g_cw_full, loss_row = small_update("adam_small", small_names, small_all)
    loss = loss_row[0, 0]
    small_update("adam_g_mix", ["g_mix"], [g_mix_all])

    shard_cols = FFN_HIDDEN // N_DEV
    g_cw = lax.dynamic_slice_in_dim(g_cw_full, me * shard_cols, shard_cols, axis=1)
    d_, m_, v_ = _adam_only("adam_conv_w", g_cw, conv_w[0], m_conv_w[0], v_conv_w[0])
    grads["conv_w"], deltas["conv_w"], new_m["conv_w"], new_v["conv_w"] = g_cw[None], d_[None], m_[None], v_[None]

    order = ["g_mix", "w_in", "w_pool", "pool_scale", "w_a", "g_ret", "b_ret", "w_r", "g_mem", "w_mem_kv", "w_c",
             "w_out", "g_ffn", "w_up", "conv_w", "conv_b", "w_down", "g_final"]
    return (loss, grad_x.reshape(B, S, D_MODEL), *[grads[n] for n in order], *[deltas[n] for n in order],
            *[new_m[n] for n in order], *[new_v[n] for n in order])
```

```python
import functools
import math

import jax
import jax.numpy as jnp
from jax import lax
from jax.experimental import pallas as pl
from jax.experimental.pallas import tpu as pltpu

F32 = jnp.float32
BF16 = jnp.bfloat16

N_DEV = 8
D_MODEL = 1024
POOL_WINDOWS = (2, 4, 8, 16)
POOL_GROUP_DIM = 128
POOL_WIDTH = 512
POOL_HALO = 16
RET_HEADS = 4
RET_QK_DIM = 128
RET_V_DIM = 256
RET_CHUNK = 128
ROPE_BASE = 10000.0
XA_HEADS = 4
XA_HEAD_DIM = 128
XA_WIDTH = 512
IN_WIDTH = 7168
IN_SHARD = IN_WIDTH // N_DEV
FFN_HIDDEN = 2816
UP_SHARD = 2 * FFN_HIDDEN // N_DEV
FFN_SLABS = FFN_HIDDEN // UP_SHARD
EPS = 1e-6
ADAM_LR = 0.001
ADAM_B1 = 0.9
ADAM_B2 = 0.999
ADAM_EPS = 1e-08
ADAM_WD = 0.01
ADAM_STEP = 10
GELU_C = math.sqrt(2.0 / math.pi)
GELU_A = 0.044715
VMEM_LIMIT = 56 * 1024 * 1024
MM_ROWS = 2048
MM_ROWS_RES = 1024
MM_TOKENS = 2048
W_IN_FIRST_ROWS = 384
MESH = pl.DeviceIdType.MESH

COL_Q, COL_K, COL_V, COL_GR, COL_QX, COL_GL = 512, 1024, 1536, 2560, 3584, 4096

_DIMS = {
    "nn": (((1,), (0,)), ((), ())),
    "nt": (((1,), (1,)), ((), ())),
    "tn": (((0,), (0,)), ((), ())),
}


def _dot(a, b, kind="nn"):
    return lax.dot_general(a.astype(BF16), b.astype(BF16), _DIMS[kind], preferred_element_type=F32)


def _params(sem, vmem=VMEM_LIMIT):
    return pltpu.CompilerParams(dimension_semantics=sem, vmem_limit_bytes=vmem)


def _tile(n, pref):
    t = min(n, pref)
    while n % t:
        t //= 2
    return t


def _mesh_pos():
    return lax.axis_index("x"), lax.axis_index("y"), lax.axis_index("c")


def _dev_index(x, y, c):
    return 4 * x + 2 * y + c


def _my_index():
    return _dev_index(*_mesh_pos())


def _remote(src, dst, send_sems, recv_sems, s, to):
    return pltpu.make_async_remote_copy(src_ref=src, dst_ref=dst, send_sem=send_sems.at[s], recv_sem=recv_sems.at[s],
                                        device_id=to, device_id_type=MESH)


class _Gather:
    def __init__(self, shards):
        self.inputs = list(shards)
        self.out_shapes = [jax.ShapeDtypeStruct((N_DEV,) + s.shape, s.dtype) for s in shards]
        n = len(shards)
        self.sem_shapes = [pltpu.SemaphoreType.DMA((7 * n,)), pltpu.SemaphoreType.DMA((7 * n,)),
                           pltpu.SemaphoreType.DMA((n,))]

    def _places(self):
        x, y, c = _mesh_pos()
        return (x, y, c), (x, y, 1 - c), [(1 - x, y), (x, 1 - y), (1 - x, 1 - y)]

    def _local(self, src, dst, sems):
        me = _my_index()
        return [pltpu.make_async_copy(src[w], dst[w].at[me], sems[2].at[w]) for w in range(len(src))]

    def start(self, src, dst, sems):
        me, sib, chips = self._places()
        for cp in self._local(src, dst, sems):
            cp.start()
        for w in range(len(src)):
            land = dst[w].at[_dev_index(*me)]
            _remote(src[w], land, sems[0], sems[1], 7 * w, sib).start()
            for j, chip in enumerate(chips):
                _remote(src[w], land, sems[0], sems[1], 7 * w + 1 + j, (*chip, me[2])).start()

    def middle(self, src, dst, sems):
        me, sib, chips = self._places()
        for j, chip in enumerate(chips):
            for w in range(len(src)):
                block = dst[w].at[_dev_index(*chip, me[2])]
                _remote(src[w], block, sems[0], sems[1], 7 * w + 1 + j, me).wait_recv()
                _remote(block, block, sems[0], sems[1], 7 * w + 4 + j, sib).start()

    def finish(self, src, dst, sems):
        me, sib, chips = self._places()
        n = len(src)
        for w in range(n):
            _remote(src[w], dst[w].at[_dev_index(*sib)], sems[0], sems[1], 7 * w, me).wait_recv()
            for j, chip in enumerate(chips):
                block = dst[w].at[_dev_index(*chip, sib[2])]
                _remote(block, block, sems[0], sems[1], 7 * w + 4 + j, me).wait_recv()
            for k in range(7):
                _remote(src[w], dst[w].at[0], sems[0], sems[1], 7 * w + k, me).wait_send()
        for cp in self._local(src, dst, sems):
            cp.wait()


class _Exchange:
    def __init__(self, partials, whole=()):
        self.n_part = len(partials)
        self.inputs = list(partials) + list(whole)
        self.out_shapes = [jax.ShapeDtypeStruct(p.shape, p.dtype) for p in partials]
        self.out_shapes += [jax.ShapeDtypeStruct((N_DEV,) + a.shape, a.dtype) for a in whole]
        n = len(self.inputs)
        self.sem_shapes = [pltpu.SemaphoreType.DMA((7 * n,)), pltpu.SemaphoreType.DMA((7 * n,)),
                           pltpu.SemaphoreType.DMA((n,))]

    def _peer(self, k):
        x, y, c = _mesh_pos()
        p = (x ^ ((k >> 2) & 1), y ^ ((k >> 1) & 1), c ^ (k & 1))
        return p, _dev_index(*p)

    def _source(self, src, w, slot):
        return src[w].at[slot] if w < self.n_part else src[w]

    def _local(self, src, dst, sems):
        me = _my_index()
        return [pltpu.make_async_copy(self._source(src, w, me), dst[w].at[me], sems[2].at[w])
                for w in range(len(src))]

    def start(self, src, dst, sems):
        me = _my_index()
        for cp in self._local(src, dst, sems):
            cp.start()
        for k in range(1, N_DEV):
            peer, peer_idx = self._peer(k)
            for w in range(len(src)):
                _remote(self._source(src, w, peer_idx), dst[w].at[me], sems[0], sems[1], 7 * w + k - 1, peer).start()

    def finish(self, src, dst, sems):
        for k in range(1, N_DEV):
            peer, peer_idx = self._peer(k)
            for w in range(len(src)):
                cp = _remote(self._source(src, w, peer_idx), dst[w].at[peer_idx], sems[0], sems[1], 7 * w + k - 1, peer)
                cp.wait_send()
                cp.wait_recv()
        for cp in self._local(src, dst, sems):
            cp.wait()


class _ExchangeTo:
    def __init__(self, partials, side):
        self.side = side
        self.inputs = list(partials)
        self.out_shapes = [jax.ShapeDtypeStruct(p.shape, p.dtype) for p in partials]
        n = len(partials)
        self.sem_shapes = [pltpu.SemaphoreType.DMA((7 * n,)), pltpu.SemaphoreType.DMA((7 * n,)),
                           pltpu.SemaphoreType.DMA((n,))]

    def _copies(self, src, dst, sems):
        x, y, c = _mesh_pos()
        me = _dev_index(x, y, c)
        receives = c == self.side
        remote = []
        for k in range(1, N_DEV):
            kx, ky, kc = (k >> 2) & 1, (k >> 1) & 1, k & 1
            peer = (x ^ kx, y ^ ky, c ^ kc)
            peer_idx = _dev_index(*peer)
            sends = c == (self.side ^ kc)
            for w in range(len(src)):
                slab = src[w].at[peer_idx]
                s = 7 * w + k - 1
                remote.append((sends, _remote(slab, dst[w].at[me], sems[0], sems[1], s, peer),
                               _remote(slab, dst[w].at[peer_idx], sems[0], sems[1], s, peer)))
        local = [pltpu.make_async_copy(src[w].at[me], dst[w].at[me], sems[2].at[w]) for w in range(len(src))]
        return receives, remote, local

    def start(self, src, dst, sems):
        receives, remote, local = self._copies(src, dst, sems)

        @pl.when(receives)
        def _():
            for cp in local:
                cp.start()

        for sends, send, _ in remote:
            pl.when(sends)(send.start)

    def finish(self, src, dst, sems):
        receives, remote, local = self._copies(src, dst, sems)
        for sends, send, arrive in remote:
            pl.when(sends)(send.wait_send)
            pl.when(receives)(arrive.wait_recv)

        @pl.when(receives)
        def _():
            for cp in local:
                cp.wait()


def _pcall(body, args, *, name, out_shape, grid, in_specs, out_specs, scratch_shapes=(), sem=None, comm=None):
    single = not isinstance(out_shape, (tuple, list))
    outs = [out_shape] if single else list(out_shape)
    ospecs = [out_specs] if single else list(out_specs)
    n_in, n_out, n_scr = len(args), len(outs), len(scratch_shapes)

    def pick(res):
        return res[0] if single else tuple(res[:n_out])

    if comm is None:
        res = pl.pallas_call(
            body, out_shape=outs, grid=grid, in_specs=list(in_specs), out_specs=ospecs,
            scratch_shapes=list(scratch_shapes), name=name, compiler_params=_params(sem),
        )(*args)
        return pick(res), ()

    nci, nco = len(comm.inputs), len(comm.out_shapes)

    def carrier(*refs):
        at = 0
        parts = []
        for size in (n_in, nci, n_out, nco, n_scr, len(comm.sem_shapes)):
            parts.append(refs[at:at + size])
            at += size
        ins, cins, o, couts, scr, sems = parts
        ids = [pl.program_id(a) for a in range(len(grid))]
        first = functools.reduce(jnp.logical_and, [i == 0 for i in ids])
        last = functools.reduce(jnp.logical_and, [i == g - 1 for i, g in zip(ids, grid)])

        body(*ins, *o, *scr)

        @pl.when(first)
        def _():
            comm.start(cins, couts, sems)

        if hasattr(comm, "middle"):
            steps = math.prod(grid)
            at = functools.reduce(lambda lin, ig: lin * ig[1] + ig[0], zip(ids, grid), 0)

            @pl.when(at == min(steps - 1, (3 * steps) // 4))
            def _():
                comm.middle(cins, couts, sems)

        @pl.when(last)
        def _():
            comm.finish(cins, couts, sems)

    hbm = pl.BlockSpec(memory_space=pltpu.HBM)
    res = pl.pallas_call(
        carrier, out_shape=outs + comm.out_shapes, grid=grid, in_specs=list(in_specs) + [hbm] * nci,
        out_specs=ospecs + [hbm] * nco, scratch_shapes=list(scratch_shapes) + comm.sem_shapes, name=name,
        compiler_params=_params(("arbitrary",) * len(grid)),
    )(*args, *comm.inputs)
    return pick(res), tuple(res[n_out:])


def _comm_call(name, comm):
    def body(*refs):
        nci, nco = len(comm.inputs), len(comm.out_shapes)
        cins, couts, sems = refs[:nci], refs[nci:nci + nco], refs[nci + nco:]
        comm.start(cins, couts, sems)
        if hasattr(comm, "middle"):
            comm.middle(cins, couts, sems)
        comm.finish(cins, couts, sems)

    hbm = pl.BlockSpec(memory_space=pltpu.HBM)
    return pl.pallas_call(
        body, out_shape=comm.out_shapes, in_specs=[hbm] * len(comm.inputs), out_specs=[hbm] * len(comm.out_shapes),
        scratch_shapes=comm.sem_shapes, name=name,
    )(*comm.inputs)


def _matmul(name, kind, a, b, out_shape, grid, a_spec, b_spec, o_spec, acc_shape, res=None, res_spec=None,
            comm=None, epilogue=None):
    nk = grid[-1]
    if epilogue is None:
        extra, extra_specs = ([res], [res_spec]) if res is not None else ([], [])
        n_out = 1
    else:
        extra, extra_specs, n_out = list(res), list(res_spec), len(out_shape)
    n_in = 2 + len(extra)

    def body(*refs):
        a_ref, b_ref = refs[0], refs[1]
        extra_refs, out_refs = refs[2:n_in], refs[n_in:n_in + n_out]

        def prod():
            return _dot(a_ref[...], b_ref[...], kind)

        def finish(acc):
            if epilogue is not None:
                ids = [pl.program_id(ax) for ax in range(len(grid) - 1)]
                first = functools.reduce(jnp.logical_and, [i == 0 for i in ids]) if ids else True
                epilogue(acc, extra_refs, out_refs, first)
                return
            if extra_refs:
                acc = acc + extra_refs[0][...]
            out_refs[0][...] = acc.astype(out_refs[0].dtype)

        if nk == 1:
            finish(prod())
        else:
            acc_ref = refs[n_in + n_out]
            k = pl.program_id(len(grid) - 1)

            @pl.when(k == 0)
            def _():
                acc_ref[...] = prod()

            @pl.when(k > 0)
            def _():
                acc_ref[...] += prod()

            @pl.when(k == nk - 1)
            def _():
                finish(acc_ref[...])

    in_specs = [a_spec, b_spec] + extra_specs
    args = (a, b, *extra)
    scratch = [pltpu.VMEM(acc_shape, F32)] if nk > 1 else []
    sem = ("arbitrary",) * len(grid) if epilogue is not None else ("parallel",) * (len(grid) - 1) + ("arbitrary",)
    out, landed = _pcall(body, args, name=name, out_shape=out_shape, grid=grid, in_specs=in_specs,
                         out_specs=o_spec, scratch_shapes=scratch, sem=sem, comm=comm)
    return out if comm is None else (out, landed)


def _mm_rows(name, a, w, out_dtype=F32, res=None, kind="nn", tm=MM_ROWS, comm=None):
    M, K = a.shape
    N = w.shape[1] if kind == "nn" else w.shape[0]
    tm = _tile(M, tm)
    res_spec = pl.BlockSpec((tm, N), lambda i, k: (i, 0)) if res is not None else None
    return _matmul(
        name, kind, a, w, jax.ShapeDtypeStruct((M, N), out_dtype), (M // tm, 1),
        pl.BlockSpec((tm, K), lambda i, k: (i, 0)), pl.BlockSpec(w.shape, lambda i, k: (0, 0)),
        pl.BlockSpec((tm, N), lambda i, k: (i, 0)), (tm, N), res, res_spec, comm)


def _residual_rms_epilogue(y, operands, outputs, first):
    x_ref, g_ref = operands
    x1_ref, h_ref = outputs
    xv = x_ref[...] + y
    x1_ref[...] = xv
    r = lax.rsqrt(jnp.mean(xv * xv, axis=-1, keepdims=True) + EPS)
    h_ref[...] = (xv * r * g_ref[...]).astype(h_ref.dtype)


def _mm_residual_rms(name, a, w, x, g, tm=MM_ROWS_RES):
    M, K = a.shape
    N = w.shape[1]
    tm = _tile(M, tm)
    row = pl.BlockSpec((tm, N), lambda i, k: (i, 0))
    return _matmul(
        name, "nn", a, w, (jax.ShapeDtypeStruct((M, N), F32), jax.ShapeDtypeStruct((M, N), BF16)), (M // tm, 1),
        pl.BlockSpec((tm, K), lambda i, k: (i, 0)), pl.BlockSpec(w.shape, lambda i, k: (0, 0)),
        (row, row), (tm, N), [x, g], [row, pl.BlockSpec((1, N), lambda i, k: (0, 0))],
        epilogue=_residual_rms_epilogue)


def _mm_tn(name, a, b, out_dtype=F32, tk=MM_TOKENS, comm=None):
    T, M = a.shape
    N = b.shape[1]
    tk = _tile(T, tk)
    return _matmul(
        name, "tn", a, b, jax.ShapeDtypeStruct((M, N), out_dtype), (1, T // tk),
        pl.BlockSpec((tk, M), lambda i, k: (k, 0)), pl.BlockSpec((tk, N), lambda i, k: (k, 0)),
        pl.BlockSpec((M, N), lambda i, k: (0, 0)), (M, N), comm=comm)


def _mm_in_gather(h, shard, tm=MM_ROWS, comm=None):
    T, K = h.shape
    n = shard.shape[1]
    tm = _tile(T, tm)
    n_tiles = T // tm
    pair_of_chip_step = {4: 1, 2: 2, 6: 3}

    def slab_of(s):
        x, y, c = _mesh_pos()
        return _dev_index(x ^ ((s >> 2) & 1), y ^ ((s >> 1) & 1), c ^ (s & 1))

    def body(h_ref, shard_ref, proj_ref, win_ref, wbuf, slot_sems, send_sems, recv_sems, local_sem):
        s, i = pl.program_id(0), pl.program_id(1)
        x, y, c = _mesh_pos()
        me, sib = (x, y, c), (x, y, 1 - c)

        def slot_copy(step):
            src = shard_ref if step == 0 else win_ref.at[slab_of(step)]
            return pltpu.make_async_copy(src, wbuf.at[step % 2], slot_sems.at[step % 2])

        def fetch(step):
            if step >= 1:
                block = win_ref.at[slab_of(step)]
                if step == 1:
                    pair = 0
                elif step % 2 == 0:
                    pair = pair_of_chip_step[step]
                else:
                    pair = 3 + pair_of_chip_step[step - 1]
                _remote(block, block, send_sems, recv_sems, pair, me).wait_recv()
                if step % 2 == 0:
                    _remote(block, block, send_sems, recv_sems, 3 + pair, sib).start()
            slot_copy(step).start()

        @pl.when((s == 0) & (i == 0))
        def _():
            land = win_ref.at[_dev_index(*me)]
            pltpu.make_async_copy(shard_ref, land, local_sem).start()
            _remote(shard_ref, land, send_sems, recv_sems, 0, sib).start()
            for step, pair in pair_of_chip_step.items():
                peer = (x ^ ((step >> 2) & 1), y ^ ((step >> 1) & 1), c)
                _remote(shard_ref, land, send_sems, recv_sems, pair, peer).start()
            fetch(0)

        for step in range(N_DEV):
            @pl.when((s == step) & (i == 0))
            def _():
                slot_copy(step).wait()

            if step + 1 < N_DEV:
                @pl.when((s == step) & (i == n_tiles - 1))
                def _():
                    fetch(step + 1)

        proj_ref[...] = _dot(h_ref[...], wbuf[s % 2])

        @pl.when((s == N_DEV - 1) & (i == n_tiles - 1))
        def _():
            for pair in range(7):
                _remote(shard_ref, win_ref.at[0], send_sems, recv_sems, pair, me).wait_send()
            pltpu.make_async_copy(shard_ref, win_ref.at[_dev_index(*me)], local_sem).wait()

    hbm = pl.BlockSpec(memory_space=pltpu.HBM)
    return _pcall(
        body, (h, shard), name="mm_in",
        out_shape=(jax.ShapeDtypeStruct((T, N_DEV * n), F32), jax.ShapeDtypeStruct((N_DEV, K, n), shard.dtype)),
        grid=(N_DEV, n_tiles), in_specs=[pl.BlockSpec((tm, K), lambda s, i: (i, 0)), hbm],
        out_specs=(pl.BlockSpec((tm, n), lambda s, i: (i, slab_of(s))), hbm),
        scratch_shapes=[pltpu.VMEM((2, K, n), shard.dtype), pltpu.SemaphoreType.DMA((2,)),
                        pltpu.SemaphoreType.DMA((7,)), pltpu.SemaphoreType.DMA((7,)), pltpu.SemaphoreType.DMA],
        sem=("arbitrary", "arbitrary"), comm=comm)


def _rms_bwd_epilogue(dh, operands, outputs, first):
    x_ref, g_ref, dres_ref = operands
    dx_ref, dg_ref = outputs
    xv = x_ref[...]
    r = lax.rsqrt(jnp.mean(xv * xv, axis=-1, keepdims=True) + EPS)
    xhat = xv * r

    @pl.when(first)
    def _():
        dg_ref[...] = jnp.zeros_like(dg_ref)

    dg_ref[...] += jnp.sum(dh * xhat, axis=0, keepdims=True)
    dxhat = dh * g_ref[...]
    dx_ref[...] = dres_ref[...] + r * (dxhat - xhat * jnp.mean(dxhat * xhat, axis=-1, keepdims=True))


def _rms_bwd_fused(M, K, tm, rms):
    row = pl.BlockSpec((tm, K), lambda i, j: (i, 0))
    vec = pl.BlockSpec((1, K), lambda i, j: (0, 0))
    x, g, dres = rms
    return dict(res=[x, g, dres], res_spec=[row, vec, row], epilogue=_rms_bwd_epilogue,
                out_shape=(jax.ShapeDtypeStruct((M, K), F32), jax.ShapeDtypeStruct((1, K), F32)), o_spec=(row, vec))


def _mm_cols_slab_t(name, a, w_slabs, rms, tm=MM_ROWS_RES, comm=None):
    M = a.shape[0]
    J, K, n = w_slabs.shape
    tm = _tile(M, tm)
    fused = _rms_bwd_fused(M, K, tm, rms)
    return _matmul(
        name, "nt", a, w_slabs, fused.pop("out_shape"), (M // tm, J),
        pl.BlockSpec((tm, n), lambda i, j: (i, j)), pl.BlockSpec((None, K, n), lambda i, j: (j, 0, 0)),
        fused.pop("o_spec"), (tm, K), comm=comm, **fused)


def _mm_tn_slab(name, a, b, n, out_dtype=F32, tk=MM_TOKENS, comm=None, part=(0, 1)):
    T, M = a.shape
    p, of = part
    M //= of
    J = b.shape[1] // n
    tk = _tile(T, tk)
    return _matmul(
        name, "tn", a, b, jax.ShapeDtypeStruct((J, M, n), out_dtype), (J, T // tk),
        pl.BlockSpec((tk, M), lambda j, k: (k, p)), pl.BlockSpec((tk, n), lambda j, k: (k, j)),
        pl.BlockSpec((None, M, n), lambda j, k: (j, 0, 0)), (M, n), comm=comm)


def _rms_fwd(name, x, g, tm=512):
    T, Dm = x.shape
    tm = _tile(T, tm)

    def body(x_ref, g_ref, h_ref):
        xv = x_ref[...]
        r = lax.rsqrt(jnp.mean(xv * xv, axis=-1, keepdims=True) + EPS)
        h_ref[...] = (xv * r * g_ref[...]).astype(h_ref.dtype)

    return pl.pallas_call(
        body, out_shape=jax.ShapeDtypeStruct((T, Dm), BF16), grid=(T // tm,),
        in_specs=[pl.BlockSpec((tm, Dm), lambda i: (i, 0)), pl.BlockSpec((1, Dm), lambda i: (0, 0))],
        out_specs=pl.BlockSpec((tm, Dm), lambda i: (i, 0)), name=name, compiler_params=_params(("parallel",)),
    )(x, g)


def _rms_bwd(name, x, g, dh, dres, tm=512):
    T, Dm = x.shape
    tm = _tile(T, tm)
    want_dx = dres is not None

    def body(*refs):
        if want_dx:
            x_ref, g_ref, dh_ref, dres_ref, dx_ref, dg_ref = refs
        else:
            x_ref, g_ref, dh_ref, dg_ref = refs
        xv = x_ref[...]
        r = lax.rsqrt(jnp.mean(xv * xv, axis=-1, keepdims=True) + EPS)
        xhat = xv * r
        dhv = dh_ref[...]

        @pl.when(pl.program_id(0) == 0)
        def _():
            dg_ref[...] = jnp.zeros_like(dg_ref)

        dg_ref[...] += jnp.sum(dhv * xhat, axis=0, keepdims=True)
        if want_dx:
            dxhat = dhv * g_ref[...]
            dx_ref[...] = dres_ref[...] + r * (dxhat - xhat * jnp.mean(dxhat * xhat, axis=-1, keepdims=True))

    row = pl.BlockSpec((tm, Dm), lambda i: (i, 0))
    vec = pl.BlockSpec((1, Dm), lambda i: (0, 0))
    if want_dx:
        return pl.pallas_call(
            body, out_shape=(jax.ShapeDtypeStruct((T, Dm), F32), jax.ShapeDtypeStruct((1, Dm), F32)),
            grid=(T // tm,), in_specs=[row, vec, row, row], out_specs=(row, vec), name=name,
            compiler_params=_params(("arbitrary",)),
        )(x, g, dh, dres)
    return pl.pallas_call(
        body, out_shape=jax.ShapeDtypeStruct((1, Dm), F32), grid=(T // tm,), in_specs=[row, vec, row],
        out_specs=vec, name=name, compiler_params=_params(("arbitrary",)),
    )(x, g, dh)


def _pool_rows(S):
    return _tile(S, 256)


def _pool_count(c0, rows, w):
    t = c0 + lax.broadcasted_iota(jnp.int32, (rows, 1), 0)
    return jnp.minimum(t + 1, w).astype(F32)


def _pool_fwd(proj, w_pool, scale, B, S):
    CH = _pool_rows(S)

    def body(hp_ref, wp_ref, sc_ref, o_ref, pad_ref):
        pad_ref[0:POOL_HALO, :] = jnp.zeros((POOL_HALO, POOL_WIDTH), F32)
        pad_ref[POOL_HALO:, :] = hp_ref[...]
        for gi, w in enumerate(POOL_WINDOWS):
            cols = slice(gi * POOL_GROUP_DIM, (gi + 1) * POOL_GROUP_DIM)
            for c in range(S // CH):
                base = POOL_HALO + c * CH
                acc = pad_ref[base:base + CH, cols]
                tok = acc
                for j in range(1, w):
                    acc = acc + pad_ref[base - j:base - j + CH, cols]
                pooled = acc / _pool_count(c * CH, CH, w) - tok
                z = _dot(pooled, wp_ref[gi])
                o_ref[c * CH:(c + 1) * CH, cols] = (z * sc_ref[:, cols]).astype(o_ref.dtype)

    return pl.pallas_call(
        body, out_shape=jax.ShapeDtypeStruct((B * S, POOL_WIDTH), BF16), grid=(B,),
        in_specs=[pl.BlockSpec((S, POOL_WIDTH), lambda b: (b, 0)),
                  pl.BlockSpec(w_pool.shape, lambda b: (0, 0, 0)),
                  pl.BlockSpec((1, POOL_WIDTH), lambda b: (0, 0))],
        out_specs=pl.BlockSpec((S, POOL_WIDTH), lambda b: (b, 0)),
        scratch_shapes=[pltpu.VMEM((S + POOL_HALO, POOL_WIDTH), F32)],
        name="pool_fwd", compiler_params=_params(("parallel",)),
    )(proj, w_pool, scale)


def _pool_bwd(proj, d_ypre, w_pool, scale, B, S):
    CH = _pool_rows(S)

    def body(hp_ref, dy_ref, wp_ref, sc_ref, dhp_ref, dwp_ref, dsc_ref, pad_ref, sc_pad_ref, dp_ref):
        @pl.when(pl.program_id(0) == 0)
        def _():
            dwp_ref[...] = jnp.zeros_like(dwp_ref)
            dsc_ref[...] = jnp.zeros_like(dsc_ref)

        pad_ref[0:POOL_HALO, :] = jnp.zeros((POOL_HALO, POOL_WIDTH), F32)
        pad_ref[POOL_HALO:, :] = hp_ref[...]
        sc_pad_ref[S:, :] = jnp.zeros((POOL_HALO, POOL_WIDTH), F32)
        for gi, w in enumerate(POOL_WINDOWS):
            cols = slice(gi * POOL_GROUP_DIM, (gi + 1) * POOL_GROUP_DIM)
            for c in range(S // CH):
                base = POOL_HALO + c * CH
                rows = slice(c * CH, (c + 1) * CH)
                acc = pad_ref[base:base + CH, cols]
                tok = acc
                for j in range(1, w):
                    acc = acc + pad_ref[base - j:base - j + CH, cols]
                cnt = _pool_count(c * CH, CH, w)
                pooled = acc / cnt - tok
                z = _dot(pooled, wp_ref[gi])
                dy = dy_ref[rows, cols]
                dsc_ref[:, cols] += jnp.sum(dy * z, axis=0, keepdims=True)
                dz = dy * sc_ref[:, cols]
                dwp_ref[gi] += _dot(pooled, dz, "tn")
                dpool = _dot(dz, wp_ref[gi], "nt")
                dp_ref[rows, cols] = dpool
                sc_pad_ref[rows, cols] = dpool / cnt
            for c in range(S // CH):
                rows = slice(c * CH, (c + 1) * CH)
                acc = sc_pad_ref[rows, cols]
                for j in range(1, w):
                    acc = acc + sc_pad_ref[c * CH + j:c * CH + j + CH, cols]
                dhp_ref[rows, cols] = (acc - dp_ref[rows, cols]).astype(dhp_ref.dtype)

    seq = pl.BlockSpec((S, POOL_WIDTH), lambda b: (b, 0))
    return pl.pallas_call(
        body,
        out_shape=(jax.ShapeDtypeStruct((B * S, POOL_WIDTH), BF16),
                   jax.ShapeDtypeStruct(w_pool.shape, F32), jax.ShapeDtypeStruct((1, POOL_WIDTH), F32)),
        grid=(B,),
        in_specs=[seq, seq, pl.BlockSpec(w_pool.shape, lambda b: (0, 0, 0)),
                  pl.BlockSpec((1, POOL_WIDTH), lambda b: (0, 0))],
        out_specs=(seq, pl.BlockSpec(w_pool.shape, lambda b: (0, 0, 0)),
                   pl.BlockSpec((1, POOL_WIDTH), lambda b: (0, 0))),
        scratch_shapes=[pltpu.VMEM((S + POOL_HALO, POOL_WIDTH), F32),
                        pltpu.VMEM((S + POOL_HALO, POOL_WIDTH), F32),
                        pltpu.VMEM((S, POOL_WIDTH), F32)],
        name="pool_bwd", compiler_params=_params(("arbitrary",)),
    )(proj, d_ypre, w_pool, scale)


def _ret_tables(S):
    half = RET_QK_DIM // 2
    inv = ROPE_BASE ** (-jnp.arange(half, dtype=F32) / half)
    ang = jnp.arange(S, dtype=F32)[:, None] * inv[None, :]
    cos, sin = jnp.cos(ang), jnp.sin(ang)
    cos_full = jnp.concatenate([cos, cos], axis=-1)
    sin_signed = jnp.concatenate([-sin, sin], axis=-1)
    C = RET_CHUNK
    lg = jnp.log1p(-jnp.exp2(-5.0 - jnp.arange(RET_HEADS, dtype=F32)))[:, None, None]
    idx = jnp.arange(C, dtype=F32)
    rel = idx[:, None] - idx[None, :]
    decay = jnp.where(rel >= 0, jnp.exp(jnp.maximum(rel, 0.0) * lg), 0.0)
    q_decay = jnp.broadcast_to(jnp.exp((idx + 1.0)[None, :, None] * lg), (RET_HEADS, C, RET_QK_DIM))
    k_decay = jnp.broadcast_to(jnp.exp((C - 1.0 - idx)[None, :, None] * lg), (RET_HEADS, C, RET_QK_DIM))
    c_decay = jnp.broadcast_to(jnp.exp(C * lg), (RET_HEADS, 1, RET_V_DIM))
    return cos_full, sin_signed, decay, q_decay, k_decay, c_decay


def _rope(x, cos_full, sin_signed):
    return x * cos_full + pltpu.roll(x, RET_QK_DIM // 2, axis=1) * sin_signed


def _rope_t(dy, cos_full, sin_signed):
    return dy * cos_full + pltpu.roll(dy * sin_signed, RET_QK_DIM // 2, axis=1)


RET_COLS = 512


def _ret_specs(N, chunk_of):
    C = RET_CHUNK

    def rows(width, col=0):
        return pl.BlockSpec((C, width), lambda b, i: (b * N + chunk_of(i), col))

    def whole(shape):
        return pl.BlockSpec(shape, lambda b, i: (0,) * len(shape))

    wide = RET_HEADS * RET_V_DIM
    return dict(
        q=rows(RET_COLS, COL_Q // RET_COLS), k=rows(RET_COLS, COL_K // RET_COLS),
        v=[rows(RET_COLS, COL_V // RET_COLS + j) for j in range(2)],
        gr=[rows(RET_COLS, COL_GR // RET_COLS + j) for j in range(2)],
        table=pl.BlockSpec((C, RET_QK_DIM), lambda b, i: (chunk_of(i), 0)),
        decay=whole((RET_HEADS, C, C)), qd=whole((RET_HEADS, C, RET_QK_DIM)), kd=whole((RET_HEADS, C, RET_QK_DIM)),
        cd=whole((RET_HEADS, 1, RET_V_DIM)), vec=whole((1, wide)), qk_rows=rows(RET_COLS), v_rows=rows(wide),
        state=pl.BlockSpec((None, None, RET_HEADS, RET_QK_DIM, RET_V_DIM), lambda b, i: (b, chunk_of(i), 0, 0, 0)))


def _head_cols(h):
    pair = slice((h % 2) * RET_V_DIM, (h % 2 + 1) * RET_V_DIM)
    return slice(h * RET_QK_DIM, (h + 1) * RET_QK_DIM), h // 2, pair, slice(h * RET_V_DIM, (h + 1) * RET_V_DIM)


def _group_norm(o):
    mu = jnp.mean(o, axis=-1, keepdims=True)
    oc = o - mu
    rstd = lax.rsqrt(jnp.mean(oc * oc, axis=-1, keepdims=True) + EPS)
    return oc * rstd, rstd


def _ret_fwd(proj, g_ret, b_ret, tables, B, S, comm=None):
    N = S // RET_CHUNK
    cos_t, sin_t, decay, q_decay, k_decay, c_decay = tables
    sp = _ret_specs(N, lambda i: i)

    def body(q_ref, k_ref, v0_ref, v1_ref, gr0_ref, gr1_ref, cos_ref, sin_ref, dec_ref, qd_ref, kd_ref, cd_ref,
             g_ref, b_ref, y_ref, rs_ref, r_ref):
        @pl.when(pl.program_id(1) == 0)
        def _():
            r_ref[...] = jnp.zeros_like(r_ref)

        cs, sn = cos_ref[...], sin_ref[...]
        for h in range(RET_HEADS):
            qk, j, pair, wide = _head_cols(h)
            q = _rope(q_ref[:, qk], cs, sn)
            k = _rope(k_ref[:, qk], cs, sn) * (RET_QK_DIM ** -0.5)
            v = (v0_ref, v1_ref)[j][:, pair]
            R = r_ref[h]
            rs_ref[h] = R
            s = _dot(q, k, "nt") * dec_ref[h]
            o = _dot(s, v) + _dot(q * qd_ref[h], R)
            r_ref[h] = cd_ref[h] * R + _dot(k * kd_ref[h], v, "tn")
            on, _ = _group_norm(o)
            gr = (gr0_ref, gr1_ref)[j][:, pair]
            y_ref[:, wide] = (gr * jax.nn.sigmoid(gr) * (on * g_ref[:, wide] + b_ref[:, wide])).astype(y_ref.dtype)

    state = jax.ShapeDtypeStruct((B, N, RET_HEADS, RET_QK_DIM, RET_V_DIM), F32)
    return _pcall(
        body, (proj,) * 6 + (cos_t, sin_t, decay, q_decay, k_decay, c_decay, g_ret, b_ret),
        name="ret_fwd", out_shape=(jax.ShapeDtypeStruct((B * S, RET_HEADS * RET_V_DIM), BF16), state), grid=(B, N),
        in_specs=[sp["q"], sp["k"], *sp["v"], *sp["gr"], sp["table"], sp["table"], sp["decay"], sp["qd"],
                  sp["kd"], sp["cd"], sp["vec"], sp["vec"]],
        out_specs=(sp["v_rows"], sp["state"]),
        scratch_shapes=[pltpu.VMEM((RET_HEADS, RET_QK_DIM, RET_V_DIM), F32)],
        sem=("parallel", "arbitrary"), comm=comm)


def _ret_bwd(proj, states, d_yr, g_ret, b_ret, tables, B, S, comm=None):
    N = S // RET_CHUNK
    cos_t, sin_t, decay, q_decay, k_decay, c_decay = tables
    sp = _ret_specs(N, lambda i: N - 1 - i)
    qk_scale = RET_QK_DIM ** -0.5

    def body(q_ref, k_ref, v0_ref, v1_ref, gr0_ref, gr1_ref, dy_ref, rs_ref, cos_ref, sin_ref, dec_ref, qd_ref,
             kd_ref, cd_ref, g_ref, b_ref, dq_ref, dk_ref, dv_ref, dgr_ref, dg_ref, db_ref, dr_ref):
        @pl.when((pl.program_id(0) == 0) & (pl.program_id(1) == 0))
        def _():
            dg_ref[...] = jnp.zeros_like(dg_ref)
            db_ref[...] = jnp.zeros_like(db_ref)

        @pl.when(pl.program_id(1) == 0)
        def _():
            dr_ref[...] = jnp.zeros_like(dr_ref)

        cs, sn = cos_ref[...], sin_ref[...]
        for h in range(RET_HEADS):
            qk, j, pair, wide = _head_cols(h)
            q = _rope(q_ref[:, qk], cs, sn)
            k = _rope(k_ref[:, qk], cs, sn) * qk_scale
            v = (v0_ref, v1_ref)[j][:, pair]
            R, dR = rs_ref[h], dr_ref[h]
            dec, qd, kd = dec_ref[h], qd_ref[h], kd_ref[h]
            s = _dot(q, k, "nt") * dec
            o = _dot(s, v) + _dot(q * qd, R)
            on, rstd = _group_norm(o)
            g = g_ref[:, wide]
            oaff = on * g + b_ref[:, wide]
            gr = (gr0_ref, gr1_ref)[j][:, pair]
            sg = jax.nn.sigmoid(gr)
            dy = dy_ref[:, wide]
            dgr_ref[:, wide] = (dy * oaff * (sg * (1.0 + gr * (1.0 - sg)))).astype(dgr_ref.dtype)
            doaff = dy * (gr * sg)
            dg_ref[:, wide] += jnp.sum(doaff * on, axis=0, keepdims=True)
            db_ref[:, wide] += jnp.sum(doaff, axis=0, keepdims=True)
            don = doaff * g
            do = rstd * (don - jnp.mean(don, axis=-1, keepdims=True)
                         - on * jnp.mean(don * on, axis=-1, keepdims=True))
            ds = _dot(do, v, "nt") * dec
            dq = _dot(ds, k) + qd * _dot(do, R, "nt")
            dk = _dot(ds, q, "tn") + kd * _dot(v, dR, "nt")
            dv_ref[:, wide] = (_dot(s, do, "tn") + _dot(k * kd, dR)).astype(dv_ref.dtype)
            dr_ref[h] = cd_ref[h] * dR + _dot(q * qd, do, "tn")
            dq_ref[:, qk] = _rope_t(dq, cs, sn).astype(dq_ref.dtype)
            dk_ref[:, qk] = _rope_t(dk * qk_scale, cs, sn).astype(dk_ref.dtype)

    T = B * S
    qk_shape = jax.ShapeDtypeStruct((T, RET_HEADS * RET_QK_DIM), BF16)
    v_shape = jax.ShapeDtypeStruct((T, RET_HEADS * RET_V_DIM), BF16)
    vec_shape = jax.ShapeDtypeStruct((1, RET_HEADS * RET_V_DIM), F32)
    return _pcall(
        body, (proj,) * 6 + (d_yr, states, cos_t, sin_t, decay, q_decay, k_decay, c_decay, g_ret, b_ret),
        name="ret_bwd", out_shape=(qk_shape, qk_shape, v_shape, v_shape, vec_shape, vec_shape), grid=(B, N),
        in_specs=[sp["q"], sp["k"], *sp["v"], *sp["gr"], sp["v_rows"], sp["state"], sp["table"], sp["table"],
                  sp["decay"], sp["qd"], sp["kd"], sp["cd"], sp["vec"], sp["vec"]],
        out_specs=(sp["qk_rows"], sp["qk_rows"], sp["v_rows"], sp["v_rows"], sp["vec"], sp["vec"]),
        scratch_shapes=[pltpu.VMEM((RET_HEADS, RET_QK_DIM, RET_V_DIM), F32)],
        sem=("arbitrary", "arbitrary"), comm=comm)


def _xa_rows(S):
    return _tile(S, 256)


def _xa_specs(S, M):
    q = pl.BlockSpec((S, XA_HEAD_DIM), lambda b, h: (b, COL_QX // XA_HEAD_DIM + h))
    k = pl.BlockSpec((M, XA_HEAD_DIM), lambda b, h: (b, h))
    v = pl.BlockSpec((M, XA_HEAD_DIM), lambda b, h: (b, XA_HEADS + h))
    o = pl.BlockSpec((S, XA_HEAD_DIM), lambda b, h: (b, h))
    return q, k, v, o


def _softmax_rows(s):
    e = jnp.exp(s - jnp.max(s, axis=-1, keepdims=True))
    return e / jnp.sum(e, axis=-1, keepdims=True)


def _xa_fwd(proj, kv, B, S, M, comm=None):
    CH = _xa_rows(S)
    q_spec, k_spec, v_spec, o_spec = _xa_specs(S, M)

    def body(q_ref, k_ref, v_ref, o_ref):
        def chunk(i, carry):
            rows = pl.ds(pl.multiple_of(i * CH, CH), CH)
            p = _softmax_rows(_dot(q_ref[rows, :], k_ref[...], "nt") * (XA_HEAD_DIM ** -0.5))
            o_ref[rows, :] = _dot(p, v_ref[...]).astype(o_ref.dtype)
            return carry

        lax.fori_loop(0, S // CH, chunk, 0, unroll=True)

    return _pcall(
        body, (proj, kv, kv), name="xattn_fwd", out_shape=jax.ShapeDtypeStruct((B * S, XA_WIDTH), BF16),
        grid=(B, XA_HEADS), in_specs=[q_spec, k_spec, v_spec], out_specs=o_spec,
        sem=("parallel", "parallel"), comm=comm)


def _xa_bwd(proj, kv, d_o, B, S, M, comm=None):
    CH = _xa_rows(S)
    q_spec, k_spec, v_spec, o_spec = _xa_specs(S, M)
    scale = XA_HEAD_DIM ** -0.5

    def body(q_ref, k_ref, v_ref, do_ref, dq_ref, dk_ref, dv_ref):
        dk_ref[...] = jnp.zeros_like(dk_ref)
        dv_ref[...] = jnp.zeros_like(dv_ref)

        def chunk(i, carry):
            rows = pl.ds(pl.multiple_of(i * CH, CH), CH)
            q, do = q_ref[rows, :], do_ref[rows, :]
            p = _softmax_rows(_dot(q, k_ref[...], "nt") * scale)
            dp = _dot(do, v_ref[...], "nt")
            ds = p * (dp - jnp.sum(dp * p, axis=-1, keepdims=True)) * scale
            dq_ref[rows, :] = _dot(ds, k_ref[...]).astype(dq_ref.dtype)
            dk_ref[...] += _dot(ds, q, "tn")
            dv_ref[...] += _dot(p, do, "tn")
            return carry

        lax.fori_loop(0, S // CH, chunk, 0, unroll=True)

    kv_out = pl.BlockSpec((M, XA_HEAD_DIM), lambda b, h: (b, h))
    return _pcall(
        body, (proj, kv, kv, d_o), name="xattn_bwd",
        out_shape=(jax.ShapeDtypeStruct((B * S, XA_WIDTH), BF16), jax.ShapeDtypeStruct((B * M, XA_WIDTH), F32),
                   jax.ShapeDtypeStruct((B * M, XA_WIDTH), F32)),
        grid=(B, XA_HEADS), in_specs=[q_spec, k_spec, v_spec, o_spec], out_specs=(o_spec, kv_out, kv_out),
        sem=("parallel", "parallel"), comm=comm)


def _gate_specs(tm):
    n = COL_GL // D_MODEL
    return [pl.BlockSpec((tm, D_MODEL), lambda i, j=j: (i, n + j)) for j in range(3)]


def _merge_fwd(proj, ys, tm=256, comm=None):
    T = proj.shape[0]
    tm = _tile(T, tm)
    row = pl.BlockSpec((tm, D_MODEL), lambda i: (i, 0))

    def body(g0, g1, g2, y0, y1, y2, o_ref):
        acc = jax.nn.sigmoid(g0[...]) * y0[...]
        acc = acc + jax.nn.sigmoid(g1[...]) * y1[...]
        acc = acc + jax.nn.sigmoid(g2[...]) * y2[...]
        o_ref[...] = acc.astype(o_ref.dtype)

    return _pcall(
        body, (proj, proj, proj, *ys), name="merge_fwd", out_shape=jax.ShapeDtypeStruct((T, D_MODEL), BF16),
        grid=(T // tm,), in_specs=_gate_specs(tm) + [row] * 3, out_specs=row, sem=("parallel",), comm=comm)


def _merge_bwd(proj, ys, d_merged, tm=256, comm=None):
    T = proj.shape[0]
    tm = _tile(T, tm)
    row = pl.BlockSpec((tm, D_MODEL), lambda i: (i, 0))

    def body(g0, g1, g2, y0, y1, y2, dm_ref, dgl_ref, d0, d1, d2):
        dm = dm_ref[...]
        for j, (g_ref, y_ref, d_ref) in enumerate(((g0, y0, d0), (g1, y1, d1), (g2, y2, d2))):
            sg = jax.nn.sigmoid(g_ref[...])
            d_ref[...] = (dm * sg).astype(d_ref.dtype)
            dgl_ref[:, j * D_MODEL:(j + 1) * D_MODEL] = (dm * y_ref[...] * sg * (1.0 - sg)).astype(dgl_ref.dtype)

    dy = jax.ShapeDtypeStruct((T, D_MODEL), BF16)
    return _pcall(
        body, (proj, proj, proj, *ys, d_merged), name="merge_bwd",
        out_shape=(jax.ShapeDtypeStruct((T, 3 * D_MODEL), BF16), dy, dy, dy), grid=(T // tm,),
        in_specs=_gate_specs(tm) + [row] * 4,
        out_specs=(pl.BlockSpec((tm, 3 * D_MODEL), lambda i: (i, 0)), row, row, row),
        sem=("parallel",), comm=comm)


def _gelu(x):
    return 0.5 * x * (1.0 + jnp.tanh(GELU_C * (x + GELU_A * x * x * x)))


def _gelu_grad(x):
    t = jnp.tanh(GELU_C * (x + GELU_A * x * x * x))
    return 0.5 * (1.0 + t) + 0.5 * x * (1.0 - t * t) * GELU_C * (1.0 + 3.0 * GELU_A * x * x)


GLU_HALO = 16


def _shift_down(x, prev, n):
    last = prev.shape[0]
    r = lax.broadcasted_iota(jnp.int32, (8, 1), 0)
    rolled = pltpu.roll(x, n, axis=0)
    head = rolled[0:8]
    for j in range(n):
        head = jnp.where(r == j, prev[last - n + j:last - n + j + 1, :], head)
    return jnp.concatenate([head, rolled[8:]], axis=0)


def _shift_up(x, nxt, n):
    rows = x.shape[0]
    r = lax.broadcasted_iota(jnp.int32, (8, 1), 0)
    rolled = pltpu.roll(x, rows - n, axis=0)
    tail = rolled[rows - 8:]
    for j in range(n):
        tail = jnp.where(r == 8 - n + j, nxt[j:j + 1, :], tail)
    return jnp.concatenate([rolled[:rows - 8], tail], axis=0)


def _conv(a, prev, cw, cb):
    return _shift_down(a, prev, 2) * cw[0:1, :] + _shift_down(a, prev, 1) * cw[1:2, :] + a * cw[2:3, :] + cb


def _glu_fwd(up, cw, cb, S, tm=256, comm=None):
    T = up.shape[2]
    tm = _tile(S, tm)
    per_seq = S // tm

    def body(ab_ref, prev_ref, cw_ref, cb_ref, u_ref):
        i = pl.program_id(1)
        prev = jnp.where(i % per_seq == 0, 0.0, prev_ref[...].astype(F32))
        ac = _conv(ab_ref[0].astype(F32), prev, cw_ref[...], cb_ref[...])
        u_ref[...] = (_gelu(ac) * ab_ref[1].astype(F32)).astype(u_ref.dtype)

    before = tm // GLU_HALO
    return _pcall(
        body, (up, up, cw, cb), name="glu_fwd",
        out_shape=jax.ShapeDtypeStruct((FFN_SLABS, T, UP_SHARD), BF16), grid=(FFN_SLABS, T // tm),
        in_specs=[pl.BlockSpec((2, None, tm, UP_SHARD), lambda d, i: (0, d, i, 0)),
                  pl.BlockSpec((None, None, GLU_HALO, UP_SHARD),
                               lambda d, i: (0, d, jnp.maximum(i * before - 1, 0), 0)),
                  pl.BlockSpec((None, 3, UP_SHARD), lambda d, i: (d, 0, 0)),
                  pl.BlockSpec((None, 1, UP_SHARD), lambda d, i: (d, 0, 0))],
        out_specs=pl.BlockSpec((None, tm, UP_SHARD), lambda d, i: (d, i, 0)),
        sem=("parallel", "parallel"), comm=comm)


def _glu_bwd(up, d_u, cw, cb, S, tm=256, comm=None):
    T = up.shape[2]
    tm = _tile(S, tm)
    per_seq = S // tm
    n_tiles = T // tm
    per_tile = tm // GLU_HALO

    def body(ab_ref, prev_ref, abn_ref, du_ref, dun_ref, cw_ref, cb_ref, dup_ref, dcw_ref, dcb_ref):
        i = pl.program_id(1)

        @pl.when(i == 0)
        def _():
            dcw_ref[...] = jnp.zeros_like(dcw_ref)
            dcb_ref[...] = jnp.zeros_like(dcb_ref)

        cw, cb = cw_ref[...], cb_ref[...]
        a, b = ab_ref[0].astype(F32), ab_ref[1].astype(F32)
        prev = jnp.where(i % per_seq == 0, 0.0, prev_ref[...].astype(F32))
        a2, a1 = _shift_down(a, prev, 2), _shift_down(a, prev, 1)
        ac = a2 * cw[0:1, :] + a1 * cw[1:2, :] + a * cw[2:3, :] + cb
        du = du_ref[...].astype(F32)
        dup_ref[1] = (du * _gelu(ac)).astype(dup_ref.dtype)
        dac = du * b * _gelu_grad(ac)
        dcb_ref[...] += jnp.sum(dac, axis=0, keepdims=True)
        dcw_ref[0:1, :] += jnp.sum(dac * a2, axis=0, keepdims=True)
        dcw_ref[1:2, :] += jnp.sum(dac * a1, axis=0, keepdims=True)
        dcw_ref[2:3, :] += jnp.sum(dac * a, axis=0, keepdims=True)
        acn = _conv(abn_ref[0].astype(F32), a[tm - GLU_HALO:, :], cw, cb)
        dacn = jnp.where(i % per_seq == per_seq - 1, 0.0,
                         dun_ref[...].astype(F32) * abn_ref[1].astype(F32) * _gelu_grad(acn))
        da = dac * cw[2:3, :] + _shift_up(dac, dacn, 1) * cw[1:2, :] + _shift_up(dac, dacn, 2) * cw[0:1, :]
        dup_ref[0] = da.astype(dup_ref.dtype)

    def nxt(i):
        return jnp.minimum((i + 1) * per_tile, T // GLU_HALO - 1)

    return _pcall(
        body, (up, up, up, d_u, d_u, cw, cb), name="glu_bwd",
        out_shape=(jax.ShapeDtypeStruct((2, FFN_SLABS, T, UP_SHARD), BF16),
                   jax.ShapeDtypeStruct((FFN_SLABS, 3, UP_SHARD), F32),
                   jax.ShapeDtypeStruct((FFN_SLABS, 1, UP_SHARD), F32)),
        grid=(FFN_SLABS, n_tiles),
        in_specs=[pl.BlockSpec((2, None, tm, UP_SHARD), lambda d, i: (0, d, i, 0)),
                  pl.BlockSpec((None, None, GLU_HALO, UP_SHARD),
                               lambda d, i: (0, d, jnp.maximum(i * per_tile - 1, 0), 0)),
                  pl.BlockSpec((2, None, GLU_HALO, UP_SHARD), lambda d, i: (0, d, nxt(i), 0)),
                  pl.BlockSpec((None, tm, UP_SHARD), lambda d, i: (d, i, 0)),
                  pl.BlockSpec((None, GLU_HALO, UP_SHARD), lambda d, i: (d, nxt(i), 0)),
                  pl.BlockSpec((None, 3, UP_SHARD), lambda d, i: (d, 0, 0)),
                  pl.BlockSpec((None, 1, UP_SHARD), lambda d, i: (d, 0, 0))],
        out_specs=(pl.BlockSpec((2, None, tm, UP_SHARD), lambda d, i: (0, d, i, 0)),
                   pl.BlockSpec((None, 3, UP_SHARD), lambda d, i: (d, 0, 0)),
                   pl.BlockSpec((None, 1, UP_SHARD), lambda d, i: (d, 0, 0))),
        sem=("parallel", "arbitrary"), comm=comm)


def _mm_up(h2, w_up_t, tm=MM_ROWS):
    T, K = h2.shape
    tm = _tile(T, tm)
    return _matmul(
        "mm_up", "nt", h2, w_up_t, jax.ShapeDtypeStruct((N_DEV, T, UP_SHARD), BF16), (N_DEV, T // tm, 1),
        pl.BlockSpec((tm, K), lambda j, i, k: (i, 0)), pl.BlockSpec((None, UP_SHARD, K), lambda j, i, k: (j, 0, 0)),
        pl.BlockSpec((None, tm, UP_SHARD), lambda j, i, k: (j, i, 0)), (tm, UP_SHARD))


def _loss_epilogue(ffn, operands, outputs, first):
    x1_ref, t_ref, g_ref = operands
    dx_ref, dg_ref, loss_ref = outputs

    @pl.when(first)
    def _():
        dg_ref[...] = jnp.zeros_like(dg_ref)
        loss_ref[...] = jnp.zeros_like(loss_ref)

    xv = x1_ref[...] + ffn
    r = lax.rsqrt(jnp.mean(xv * xv, axis=-1, keepdims=True) + EPS)
    xhat = xv * r
    err = xhat * g_ref[...] - t_ref[...]
    loss_ref[...] += (0.5 / D_MODEL) * jnp.sum(err * err)
    dy = err * (1.0 / D_MODEL)
    dg_ref[...] += jnp.sum(dy * xhat, axis=0, keepdims=True)
    dxhat = dy * g_ref[...]
    dx_ref[...] = r * (dxhat - xhat * jnp.mean(dxhat * xhat, axis=-1, keepdims=True))


def _mm_down_loss(u, w_down, x1, target, g_final, tm=MM_ROWS_RES):
    J, T, n = u.shape
    tm = _tile(T, tm)
    row = pl.BlockSpec((tm, D_MODEL), lambda i, d: (i, 0))
    vec = pl.BlockSpec((1, D_MODEL), lambda i, d: (0, 0))
    vec_shape = jax.ShapeDtypeStruct((1, D_MODEL), F32)
    return _matmul(
        "mm_down", "nn", u, w_down, (jax.ShapeDtypeStruct((T, D_MODEL), F32), vec_shape, vec_shape), (T // tm, J),
        pl.BlockSpec((None, tm, n), lambda i, d: (d, i, 0)), pl.BlockSpec((None, n, D_MODEL), lambda i, d: (d, 0, 0)),
        (row, vec, vec), (tm, D_MODEL), [x1, target, g_final], [row, row, vec], epilogue=_loss_epilogue)


def _mm_down_t(dx, w_down, tm=MM_ROWS):
    T = dx.shape[0]
    J, n, _ = w_down.shape
    tm = _tile(T, tm)
    return _matmul(
        "mm_down_t", "nt", dx, w_down, jax.ShapeDtypeStruct((J, T, n), BF16), (J, T // tm, 1),
        pl.BlockSpec((tm, D_MODEL), lambda d, i, k: (i, 0)), pl.BlockSpec((None, n, D_MODEL), lambda d, i, k: (d, 0, 0)),
        pl.BlockSpec((None, tm, n), lambda d, i, k: (d, i, 0)), (tm, n))


def _mm_dw_down(u, dx, tk=MM_TOKENS):
    J, T, n = u.shape
    tk = _tile(T, tk)
    return _matmul(
        "mm_dw_down", "tn", u, dx, jax.ShapeDtypeStruct((J, n, D_MODEL), BF16), (J, T // tk),
        pl.BlockSpec((None, tk, n), lambda d, k: (d, k, 0)), pl.BlockSpec((tk, D_MODEL), lambda d, k: (k, 0)),
        pl.BlockSpec((None, n, D_MODEL), lambda d, k: (d, 0, 0)), (n, D_MODEL))


def _mm_dw_up(h2, d_up, tk=MM_TOKENS):
    T, K = h2.shape
    tk = _tile(T, tk)
    return _matmul(
        "mm_dw_up", "tn", d_up, h2, jax.ShapeDtypeStruct((N_DEV, UP_SHARD, K), BF16), (N_DEV, T // tk),
        pl.BlockSpec((None, tk, UP_SHARD), lambda j, k: (j, k, 0)), pl.BlockSpec((tk, K), lambda j, k: (k, 0)),
        pl.BlockSpec((None, UP_SHARD, K), lambda j, k: (j, 0, 0)), (UP_SHARD, K))


def _mm_up_t(d_up, w_up_t, rms, tm=MM_ROWS_RES, comm=None):
    J, T, n = d_up.shape
    K = w_up_t.shape[2]
    tm = _tile(T, tm)
    fused = _rms_bwd_fused(T, K, tm, rms)
    return _matmul(
        "mm_up_t", "nn", d_up, w_up_t, fused.pop("out_shape"), (T // tm, J),
        pl.BlockSpec((None, tm, n), lambda i, j: (j, i, 0)), pl.BlockSpec((None, n, K), lambda i, j: (j, 0, 0)),
        fused.pop("o_spec"), (tm, K), comm=comm, **fused)


def _cast_shards(shards):
    def body(*refs):
        n = len(refs) // 2
        for src, dst in zip(refs[:n], refs[n:]):
            dst[...] = src[...].astype(dst.dtype)

    return pl.pallas_call(
        body, out_shape=[jax.ShapeDtypeStruct(s.shape, BF16) for s in shards], name="cast_shards",
        compiler_params=pltpu.CompilerParams(vmem_limit_bytes=VMEM_LIMIT),
    )(*shards)


def _adamw(w, g, m, v):
    m = ADAM_B1 * m + (1.0 - ADAM_B1) * g
    v = ADAM_B2 * v + (1.0 - ADAM_B2) * (g * g)
    m_hat = m / (1.0 - ADAM_B1 ** ADAM_STEP)
    v_hat = v / (1.0 - ADAM_B2 ** ADAM_STEP)
    delta = -ADAM_LR * (m_hat / (jnp.sqrt(v_hat) + ADAM_EPS) + ADAM_WD * w)
    return delta, m, v


def _sum_parts(p_ref):
    g = p_ref[0].astype(F32)
    for d in range(1, N_DEV):
        g = g + p_ref[d].astype(F32)
    return g


def _reduce_adam(name, parts, w, m, v, tr=128):
    R, Cn = w.shape
    by_rows = sum(p.shape[1] for p in parts) == R and len(parts) > 1
    tr = math.gcd(tr, *[p.shape[1] for p in parts])
    n_tiles = [p.shape[1] // tr for p in parts]
    first = [sum(n_tiles[:j]) for j in range(len(parts))] if by_rows else [0] * len(parts)

    def body(*refs):
        p_refs = refs[:len(parts)]
        w_ref, m_ref, v_ref, g_out, d_out, m_out, v_out = refs[len(parts):]

        def update(p_ref):
            g = _sum_parts(p_ref)
            delta, m_new, v_new = _adamw(w_ref[...], g, m_ref[...], v_ref[...])
            g_out[...] = g
            d_out[...] = delta
            m_out[...] = m_new
            v_out[...] = v_new

        if len(parts) == 1:
            update(p_refs[0])
        elif by_rows:
            i = pl.program_id(0)
            for p_ref, t0, n in zip(p_refs, first, n_tiles):
                pl.when((i >= t0) & (i < t0 + n))(functools.partial(update, p_ref))
        else:
            c = lax.axis_index("c")
            for side, p_ref in enumerate(p_refs):
                pl.when(c == side)(functools.partial(update, p_ref))

    def part_spec(t0, n):
        return pl.BlockSpec((N_DEV, tr, Cn), lambda i: (0, jnp.clip(i - t0, 0, n - 1), 0))

    row = pl.BlockSpec((tr, Cn), lambda i: (i, 0))
    shape = jax.ShapeDtypeStruct((R, Cn), F32)
    return pl.pallas_call(
        body, out_shape=(shape,) * 4, grid=(R // tr,),
        in_specs=[part_spec(t0, n) for t0, n in zip(first, n_tiles)] + [row, row, row],
        out_specs=(row,) * 4, name=name, compiler_params=_params(("parallel",)),
    )(*parts, w, m, v)


def _small_adam(name, gathered, params):
    n_g, n_p = len(gathered), len(params)

    def body(*refs):
        g_refs = refs[:n_g]
        wmv = refs[n_g:n_g + 3 * n_p]
        sums = refs[n_g + 3 * n_p:2 * n_g + 3 * n_p]
        upd = refs[2 * n_g + 3 * n_p:]
        for j in range(n_g):
            g = _sum_parts(g_refs[j])
            sums[j][...] = g
            if j < n_p:
                w_ref, m_ref, v_ref = wmv[3 * j:3 * j + 3]
                delta, m_new, v_new = _adamw(w_ref[...], g, m_ref[...], v_ref[...])
                upd[3 * j][...] = delta
                upd[3 * j + 1][...] = m_new
                upd[3 * j + 2][...] = v_new

    flat = [a for wmv in params for a in wmv]
    out_shape = [jax.ShapeDtypeStruct(g.shape[1:], F32) for g in gathered]
    out_shape += [jax.ShapeDtypeStruct(a.shape, F32) for a in flat]
    res = pl.pallas_call(body, out_shape=out_shape, name=name)(*gathered, *flat)
    return res[:n_g], [tuple(res[n_g + 3 * j:n_g + 3 * j + 3]) for j in range(n_p)]


def _adam_only(name, g, w, m, v):
    def body(g_ref, w_ref, m_ref, v_ref, d_out, m_out, v_out):
        delta, m_new, v_new = _adamw(w_ref[...], g_ref[...], m_ref[...], v_ref[...])
        d_out[...] = delta
        m_out[...] = m_new
        v_out[...] = v_new

    shape = jax.ShapeDtypeStruct(w.shape, F32)
    return pl.pallas_call(body, out_shape=(shape,) * 3, name=name)(g, w, m, v)


def kernel(x, mem, g_mix, w_in, w_pool, pool_scale, w_a, g_ret, b_ret, w_r, g_mem, w_mem_kv, w_c, w_out, g_ffn, w_up, conv_w, conv_b, w_down, g_final, loss_target, m_g_mix, m_w_in, m_w_pool, m_pool_scale, m_w_a, m_g_ret, m_b_ret, m_w_r, m_g_mem, m_w_mem_kv, m_w_c, m_w_out, m_g_ffn, m_w_up, m_conv_w, m_conv_b, m_w_down, m_g_final, v_g_mix, v_w_in, v_w_pool, v_pool_scale, v_w_a, v_g_ret, v_b_ret, v_w_r, v_g_mem, v_w_mem_kv, v_w_c, v_w_out, v_g_ffn, v_w_up, v_conv_w, v_conv_b, v_w_down, v_g_final):
    B, S, _ = x.shape
    M = mem.shape[1]
    T = B * S
    me = _my_index()
    x2d = x.reshape(T, D_MODEL)
    mem2d = mem.reshape(B * M, D_MODEL)
    tgt2d = loss_target.reshape(T, D_MODEL)
    g_final2 = g_final.reshape(1, D_MODEL)

    big = dict(w_in=w_in[0], w_a=w_a[0], w_r=w_r[0], w_mem_kv=w_mem_kv[0], w_c=w_c[0], w_out=w_out[0],
               w_up=w_up[0].T, w_down=w_down[0])
    names = list(big)
    cast = dict(zip(names, _cast_shards([big[n] for n in names])))
    cb = conv_b[0].reshape(FFN_SLABS, 1, UP_SHARD)
    wp = w_pool[0]
    tables = _ret_tables(S)

    h = _rms_fwd("rms_mix", x2d, g_mix)
    early = ("w_a", "w_r", "w_mem_kv", "w_c", "w_out")
    (proj, Win), landed = _mm_in_gather(h, cast["w_in"], comm=_Gather([cast[n] for n in early] + [conv_w[0]]))
    W = dict(zip(early, landed))
    (yr, ret_states), (Wup,) = _ret_fwd(proj, g_ret, b_ret, tables, B, S, comm=_Gather([cast["w_up"]]))
    cw_full = landed[-1].transpose(1, 0, 2).reshape(3, FFN_HIDDEN)
    cw = cw_full.reshape(3, FFN_SLABS, UP_SHARD).transpose(1, 0, 2)
    Wa = W["w_a"].transpose(1, 0, 2).reshape(POOL_WIDTH, D_MODEL)
    Wc = W["w_c"].transpose(1, 0, 2).reshape(XA_WIDTH, D_MODEL)
    Wr = W["w_r"].reshape(D_MODEL, D_MODEL)
    Wkv = W["w_mem_kv"].reshape(D_MODEL, D_MODEL)
    Wout = W["w_out"].reshape(D_MODEL, D_MODEL)
    ypre = _pool_fwd(proj, wp, pool_scale, B, S)
    y_pool = _mm_rows("mm_a", ypre, Wa, BF16)
    y_ret = _mm_rows("mm_r", yr, Wr, BF16)
    mem_n = _rms_fwd("rms_mem", mem2d, g_mem)
    kv = _mm_rows("mm_kv", mem_n, Wkv)
    o_mem, (Wdown,) = _xa_fwd(proj, kv, B, S, M, comm=_Gather([cast["w_down"]]))
    Wdown = Wdown.reshape(FFN_SLABS, UP_SHARD, D_MODEL)
    y_mem = _mm_rows("mm_c", o_mem, Wc, BF16)
    ys = (y_pool, y_ret, y_mem)
    merged = _merge_fwd(proj, ys)[0]
    x1, h2 = _mm_residual_rms("mm_out", merged, Wout, x2d, g_ffn)
    up = _mm_up(h2, Wup).reshape(2, FFN_SLABS, T, UP_SHARD)
    u = _glu_fwd(up, cw, cb, S)[0]

    dx2, dg_final, loss_part = _mm_down_loss(u, Wdown, x1, tgt2d, g_final2)
    received = {}
    d_u = _mm_down_t(dx2, Wdown)
    dW_down = _mm_dw_down(u, dx2)
    (d_up, d_cw, d_cb), (received["w_down"],) = _glu_bwd(
        up, d_u, cw, cb, S, comm=_Exchange([dW_down.reshape(N_DEV, -1, D_MODEL)]))
    d_up = d_up.reshape(N_DEV, T, UP_SHARD)
    dW_up = _mm_dw_up(h2, d_up)
    (dx1, dg_ffn), (up_c0,) = _mm_up_t(d_up, Wup, (x1, g_ffn, dx2), comm=_ExchangeTo([dW_up], 0))
    d_merged = _mm_rows("mm_out_t", dx1, Wout, kind="nt")
    dW_out = _mm_tn("mm_dw_out", merged, dx1, BF16)
    (d_gl, d_y_pool, d_y_ret, d_y_mem), (received["w_out"],) = _merge_bwd(
        proj, ys, d_merged, comm=_Exchange([dW_out.reshape(N_DEV, -1, D_MODEL)]))
    dW_c = _mm_tn("mm_dw_c", o_mem, d_y_mem, BF16)
    d_o_mem = _mm_rows("mm_c_t", d_y_mem, Wc, kind="nt")
    (d_qx, d_kmem, d_vmem), (up_c1,) = _xa_bwd(proj, kv, d_o_mem, B, S, M, comm=_ExchangeTo([dW_up], 1))
    received["w_up"] = [up_c0, up_c1]
    d_kv = jnp.concatenate([d_kmem, d_vmem], axis=1)
    dW_kv = _mm_tn("mm_dw_kv", mem_n, d_kv, BF16)
    d_mem_n = _mm_rows("mm_kv_t", d_kv, Wkv, kind="nt")
    dg_mem = _rms_bwd("rms_mem_bwd", mem2d, g_mem, d_mem_n, None)
    dW_a = _mm_tn("mm_dw_a", ypre, d_y_pool, BF16)
    d_ypre = _mm_rows("mm_a_t", d_y_pool, Wa, kind="nt")
    d_hp, dw_pool, d_scale = _pool_bwd(proj, d_ypre, wp, pool_scale, B, S)
    dW_r = _mm_tn("mm_dw_r", yr, d_y_ret, BF16)
    d_yr = _mm_rows("mm_r_t", d_y_ret, Wr, kind="nt")
    (d_q, d_k, d_v, d_gr, dg_ret, db_ret), landed = _ret_bwd(
        proj, ret_states, d_yr, g_ret, b_ret, tables, B, S,
        comm=_Exchange([dW_a.reshape(POOL_WIDTH, N_DEV, -1).transpose(1, 0, 2), dW_r.reshape(N_DEV, -1, D_MODEL),
                        dW_c.reshape(XA_WIDTH, N_DEV, -1).transpose(1, 0, 2), dW_kv.reshape(N_DEV, -1, D_MODEL)]))
    received["w_a"], received["w_r"], received["w_c"], received["w_mem_kv"] = landed
    small_names = ["w_pool", "pool_scale", "g_ret", "b_ret", "g_mem", "g_ffn", "conv_b", "g_final"]
    small_grads = [dw_pool, d_scale, dg_ret, db_ret, dg_mem, dg_ffn, d_cb.reshape(1, FFN_HIDDEN), dg_final,
                   d_cw.transpose(1, 0, 2).reshape(3, FFN_HIDDEN), loss_part]
    d_proj = jnp.concatenate([d_hp, d_q, d_k, d_v, d_gr, d_qx, d_gl], axis=1)
    dW_in0, small_all = _mm_tn_slab("mm_dw_in0", h[:, :W_IN_FIRST_ROWS], d_proj, IN_SHARD, BF16,
                                    comm=_Exchange([], whole=small_grads))
    dW_in1, (in0,) = _mm_tn_slab("mm_dw_in1", h[:, W_IN_FIRST_ROWS:], d_proj, IN_SHARD, BF16,
                                 comm=_Exchange([dW_in0]))
    (grad_x, dg_mix), (in1,) = _mm_cols_slab_t("mm_in_t", d_proj, Win, (x2d, g_mix, dx1), comm=_Exchange([dW_in1]))
    received["w_in"] = [in0, in1]
    (g_mix_all,) = _comm_call("gather_g_mix", _Exchange([], whole=[dg_mix]))

    args = dict(g_mix=g_mix, w_in=w_in, w_pool=w_pool, pool_scale=pool_scale, w_a=w_a, g_ret=g_ret, b_ret=b_ret,
                w_r=w_r, g_mem=g_mem, w_mem_kv=w_mem_kv, w_c=w_c, w_out=w_out, g_ffn=g_ffn, w_up=w_up,
                conv_w=conv_w, conv_b=conv_b, w_down=w_down, g_final=g_final)
    m_in = dict(g_mix=m_g_mix, w_in=m_w_in, w_pool=m_w_pool, pool_scale=m_pool_scale, w_a=m_w_a, g_ret=m_g_ret,
                b_ret=m_b_ret, w_r=m_w_r, g_mem=m_g_mem, w_mem_kv=m_w_mem_kv, w_c=m_w_c, w_out=m_w_out,
                g_ffn=m_g_ffn, w_up=m_w_up, conv_w=m_conv_w, conv_b=m_conv_b, w_down=m_w_down, g_final=m_g_final)
    v_in = dict(g_mix=v_g_mix, w_in=v_w_in, w_pool=v_w_pool, pool_scale=v_pool_scale, w_a=v_w_a, g_ret=v_g_ret,
                b_ret=v_b_ret, w_r=v_w_r, g_mem=v_g_mem, w_mem_kv=v_w_mem_kv, w_c=v_w_c, w_out=v_w_out,
                g_ffn=v_g_ffn, w_up=v_w_up, conv_w=v_conv_w, conv_b=v_conv_b, w_down=v_w_down, g_final=v_g_final)

    grads, deltas, new_m, new_v = {}, {}, {}, {}
    for n in names:
        parts = received[n] if isinstance(received[n], list) else [received[n]]
        flip = (lambda a: a.T) if n == "w_up" else (lambda a: a)
        outs = _reduce_adam("adam_" + n, parts, big[n], flip(m_in[n][0]), flip(v_in[n][0]))
        for store, val in zip((grads, deltas, new_m, new_v), outs):
            store[n] = flip(val)[None]

    def as_small(a):
        return a.reshape(a.shape[-3:]) if a.ndim > 2 else a.reshape(1, -1)

    def small_update(call_name, param_names, gathered):
        params = [tuple(as_small(d[n]) for d in (args, m_in, v_in)) for n in param_names]
        sums, updates = _small_adam(call_name, gathered, params)
        for n, g, (d_, m_, v_) in zip(param_names, sums, updates):
            shape = args[n].shape
            grads[n], deltas[n], new_m[n], new_v[n] = (a.reshape(shape) for a in (g, d_, m_, v_))
        return sums[len(param_names):]

    g_cw_full, loss_row = small_update("adam_small", small_names, small_all)
    loss = loss_row[0, 0]
    small_update("adam_g_mix", ["g_mix"], [g_mix_all])

    shard_cols = FFN_HIDDEN // N_DEV
    g_cw = lax.dynamic_slice_in_dim(g_cw_full, me * shard_cols, shard_cols, axis=1)
    d_, m_, v_ = _adam_only("adam_conv_w", g_cw, conv_w[0], m_conv_w[0], v_conv_w[0])
    grads["conv_w"], deltas["conv_w"], new_m["conv_w"], new_v["conv_w"] = g_cw[None], d_[None], m_[None], v_[None]

    order = ["g_mix", "w_in", "w_pool", "pool_scale", "w_a", "g_ret", "b_ret", "w_r", "g_mem", "w_mem_kv", "w_c",
             "w_out", "g_ffn", "w_up", "conv_w", "conv_b", "w_down", "g_final"]
    return (loss, grad_x.reshape(B, S, D_MODEL), *[grads[n] for n in order], *[deltas[n] for n in order],
            *[new_m[n] for n in order], *[new_v[n] for n in order])
```

```python
import functools
import math

import jax
import jax.numpy as jnp
from jax import lax
from jax.experimental import pallas as pl
from jax.experimental.pallas import tpu as pltpu

F32 = jnp.float32
BF16 = jnp.bfloat16

N_DEV = 8
D_MODEL = 1024
POOL_WINDOWS = (2, 4, 8, 16)
POOL_GROUP_DIM = 128
POOL_WIDTH = 512
POOL_HALO = 16
RET_HEADS = 4
RET_QK_DIM = 128
RET_V_DIM = 256
RET_CHUNK = 128
ROPE_BASE = 10000.0
XA_HEADS = 4
XA_HEAD_DIM = 128
XA_WIDTH = 512
IN_WIDTH = 7168
IN_SHARD = IN_WIDTH // N_DEV
FFN_HIDDEN = 2816
UP_SHARD = 2 * FFN_HIDDEN // N_DEV
FFN_SLABS = FFN_HIDDEN // UP_SHARD
EPS = 1e-6
ADAM_LR = 0.001
ADAM_B1 = 0.9
ADAM_B2 = 0.999
ADAM_EPS = 1e-08
ADAM_WD = 0.01
ADAM_STEP = 10
GELU_C = math.sqrt(2.0 / math.pi)
GELU_A = 0.044715
VMEM_LIMIT = 56 * 1024 * 1024
MM_ROWS = 2048
MM_ROWS_RES = 1024
MM_TOKENS = 2048
W_IN_FIRST_ROWS = 384
MESH = pl.DeviceIdType.MESH

COL_Q, COL_K, COL_V, COL_GR, COL_QX, COL_GL = 512, 1024, 1536, 2560, 3584, 4096

_DIMS = {
    "nn": (((1,), (0,)), ((), ())),
    "nt": (((1,), (1,)), ((), ())),
    "tn": (((0,), (0,)), ((), ())),
}


def _dot(a, b, kind="nn"):
    return lax.dot_general(a.astype(BF16), b.astype(BF16), _DIMS[kind], preferred_element_type=F32)


def _params(sem, vmem=VMEM_LIMIT):
    return pltpu.CompilerParams(dimension_semantics=sem, vmem_limit_bytes=vmem)


def _tile(n, pref):
    t = min(n, pref)
    while n % t:
        t //= 2
    return t


def _mesh_pos():
    return lax.axis_index("x"), lax.axis_index("y"), lax.axis_index("c")


def _dev_index(x, y, c):
    return 4 * x + 2 * y + c


def _my_index():
    return _dev_index(*_mesh_pos())


def _remote(src, dst, send_sems, recv_sems, s, to):
    return pltpu.make_async_remote_copy(src_ref=src, dst_ref=dst, send_sem=send_sems.at[s], recv_sem=recv_sems.at[s],
                                        device_id=to, device_id_type=MESH)


class _Gather:
    def __init__(self, shards):
        self.inputs = list(shards)
        self.out_shapes = [jax.ShapeDtypeStruct((N_DEV,) + s.shape, s.dtype) for s in shards]
        n = len(shards)
        self.sem_shapes = [pltpu.SemaphoreType.DMA((7 * n,)), pltpu.SemaphoreType.DMA((7 * n,)),
                           pltpu.SemaphoreType.DMA((n,))]

    def _places(self):
        x, y, c = _mesh_pos()
        return (x, y, c), (x, y, 1 - c), [(1 - x, y), (x, 1 - y), (1 - x, 1 - y)]

    def _local(self, src, dst, sems):
        me = _my_index()
        return [pltpu.make_async_copy(src[w], dst[w].at[me], sems[2].at[w]) for w in range(len(src))]

    def start(self, src, dst, sems):
        me, sib, chips = self._places()
        for cp in self._local(src, dst, sems):
            cp.start()
        for w in range(len(src)):
            land = dst[w].at[_dev_index(*me)]
            _remote(src[w], land, sems[0], sems[1], 7 * w, sib).start()
            for j, chip in enumerate(chips):
                _remote(src[w], land, sems[0], sems[1], 7 * w + 1 + j, (*chip, me[2])).start()

    def middle(self, src, dst, sems):
        me, sib, chips = self._places()
        for j, chip in enumerate(chips):
            for w in range(len(src)):
                block = dst[w].at[_dev_index(*chip, me[2])]
                _remote(src[w], block, sems[0], sems[1], 7 * w + 1 + j, me).wait_recv()
                _remote(block, block, sems[0], sems[1], 7 * w + 4 + j, sib).start()

    def finish(self, src, dst, sems):
        me, sib, chips = self._places()
        n = len(src)
        for w in range(n):
            _remote(src[w], dst[w].at[_dev_index(*sib)], sems[0], sems[1], 7 * w, me).wait_recv()
            for j, chip in enumerate(chips):
                block = dst[w].at[_dev_index(*chip, sib[2])]
                _remote(block, block, sems[0], sems[1], 7 * w + 4 + j, me).wait_recv()
            for k in range(7):
                _remote(src[w], dst[w].at[0], sems[0], sems[1], 7 * w + k, me).wait_send()
        for cp in self._local(src, dst, sems):
            cp.wait()


class _Exchange:
    def __init__(self, partials, whole=()):
        self.n_part = len(partials)
        self.inputs = list(partials) + list(whole)
        self.out_shapes = [jax.ShapeDtypeStruct(p.shape, p.dtype) for p in partials]
        self.out_shapes += [jax.ShapeDtypeStruct((N_DEV,) + a.shape, a.dtype) for a in whole]
        n = len(self.inputs)
        self.sem_shapes = [pltpu.SemaphoreType.DMA((7 * n,)), pltpu.SemaphoreType.DMA((7 * n,)),
                           pltpu.SemaphoreType.DMA((n,))]

    def _peer(self, k):
        x, y, c = _mesh_pos()
        p = (x ^ ((k >> 2) & 1), y ^ ((k >> 1) & 1), c ^ (k & 1))
        return p, _dev_index(*p)

    def _source(self, src, w, slot):
        return src[w].at[slot] if w < self.n_part else src[w]

    def _local(self, src, dst, sems):
        me = _my_index()
        return [pltpu.make_async_copy(self._source(src, w, me), dst[w].at[me], sems[2].at[w])
                for w in range(len(src))]

    def start(self, src, dst, sems):
        me = _my_index()
        for cp in self._local(src, dst, sems):
            cp.start()
        for k in range(1, N_DEV):
            peer, peer_idx = self._peer(k)
            for w in range(len(src)):
                _remote(self._source(src, w, peer_idx), dst[w].at[me], sems[0], sems[1], 7 * w + k - 1, peer).start()

    def finish(self, src, dst, sems):
        for k in range(1, N_DEV):
            peer, peer_idx = self._peer(k)
            for w in range(len(src)):
                cp = _remote(self._source(src, w, peer_idx), dst[w].at[peer_idx], sems[0], sems[1], 7 * w + k - 1, peer)
                cp.wait_send()
                cp.wait_recv()
        for cp in self._local(src, dst, sems):
            cp.wait()


class _ExchangeTo:
    def __init__(self, partials, side):
        self.side = side
        self.inputs = list(partials)
        self.out_shapes = [jax.ShapeDtypeStruct(p.shape, p.dtype) for p in partials]
        n = len(partials)
        self.sem_shapes = [pltpu.SemaphoreType.DMA((7 * n,)), pltpu.SemaphoreType.DMA((7 * n,)),
                           pltpu.SemaphoreType.DMA((n,))]

    def _copies(self, src, dst, sems):
        x, y, c = _mesh_pos()
        me = _dev_index(x, y, c)
        receives = c == self.side
        remote = []
        for k in range(1, N_DEV):
            kx, ky, kc = (k >> 2) & 1, (k >> 1) & 1, k & 1
            peer = (x ^ kx, y ^ ky, c ^ kc)
            peer_idx = _dev_index(*peer)
            sends = c == (self.side ^ kc)
            for w in range(len(src)):
                slab = src[w].at[peer_idx]
                s = 7 * w + k - 1
                remote.append((sends, _remote(slab, dst[w].at[me], sems[0], sems[1], s, peer),
                               _remote(slab, dst[w].at[peer_idx], sems[0], sems[1], s, peer)))
        local = [pltpu.make_async_copy(src[w].at[me], dst[w].at[me], sems[2].at[w]) for w in range(len(src))]
        return receives, remote, local

    def start(self, src, dst, sems):
        receives, remote, local = self._copies(src, dst, sems)

        @pl.when(receives)
        def _():
            for cp in local:
                cp.start()

        for sends, send, _ in remote:
            pl.when(sends)(send.start)

    def finish(self, src, dst, sems):
        receives, remote, local = self._copies(src, dst, sems)
        for sends, send, arrive in remote:
            pl.when(sends)(send.wait_send)
            pl.when(receives)(arrive.wait_recv)

        @pl.when(receives)
        def _():
            for cp in local:
                cp.wait()


def _pcall(body, args, *, name, out_shape, grid, in_specs, out_specs, scratch_shapes=(), sem=None, comm=None):
    single = not isinstance(out_shape, (tuple, list))
    outs = [out_shape] if single else list(out_shape)
    ospecs = [out_specs] if single else list(out_specs)
    n_in, n_out, n_scr = len(args), len(outs), len(scratch_shapes)

    def pick(res):
        return res[0] if single else tuple(res[:n_out])

    if comm is None:
        res = pl.pallas_call(
            body, out_shape=outs, grid=grid, in_specs=list(in_specs), out_specs=ospecs,
            scratch_shapes=list(scratch_shapes), name=name, compiler_params=_params(sem),
        )(*args)
        return pick(res), ()

    nci, nco = len(comm.inputs), len(comm.out_shapes)

    def carrier(*refs):
        at = 0
        parts = []
        for size in (n_in, nci, n_out, nco, n_scr, len(comm.sem_shapes)):
            parts.append(refs[at:at + size])
            at += size
        ins, cins, o, couts, scr, sems = parts
        ids = [pl.program_id(a) for a in range(len(grid))]
        first = functools.reduce(jnp.logical_and, [i == 0 for i in ids])
        last = functools.reduce(jnp.logical_and, [i == g - 1 for i, g in zip(ids, grid)])

        body(*ins, *o, *scr)

        @pl.when(first)
        def _():
            comm.start(cins, couts, sems)

        if hasattr(comm, "middle"):
            steps = math.prod(grid)
            at = functools.reduce(lambda lin, ig: lin * ig[1] + ig[0], zip(ids, grid), 0)

            @pl.when(at == min(steps - 1, (3 * steps) // 4))
            def _():
                comm.middle(cins, couts, sems)

        @pl.when(last)
        def _():
            comm.finish(cins, couts, sems)

    hbm = pl.BlockSpec(memory_space=pltpu.HBM)
    res = pl.pallas_call(
        carrier, out_shape=outs + comm.out_shapes, grid=grid, in_specs=list(in_specs) + [hbm] * nci,
        out_specs=ospecs + [hbm] * nco, scratch_shapes=list(scratch_shapes) + comm.sem_shapes, name=name,
        compiler_params=_params(("arbitrary",) * len(grid)),
    )(*args, *comm.inputs)
    return pick(res), tuple(res[n_out:])


def _comm_call(name, comm):
    def body(*refs):
        nci, nco = len(comm.inputs), len(comm.out_shapes)
        cins, couts, sems = refs[:nci], refs[nci:nci + nco], refs[nci + nco:]
        comm.start(cins, couts, sems)
        if hasattr(comm, "middle"):
            comm.middle(cins, couts, sems)
        comm.finish(cins, couts, sems)

    hbm = pl.BlockSpec(memory_space=pltpu.HBM)
    return pl.pallas_call(
        body, out_shape=comm.out_shapes, in_specs=[hbm] * len(comm.inputs), out_specs=[hbm] * len(comm.out_shapes),
        scratch_shapes=comm.sem_shapes, name=name,
    )(*comm.inputs)


def _matmul(name, kind, a, b, out_shape, grid, a_spec, b_spec, o_spec, acc_shape, res=None, res_spec=None,
            comm=None, epilogue=None):
    nk = grid[-1]
    if epilogue is None:
        extra, extra_specs = ([res], [res_spec]) if res is not None else ([], [])
        n_out = 1
    else:
        extra, extra_specs, n_out = list(res), list(res_spec), len(out_shape)
    n_in = 2 + len(extra)

    def body(*refs):
        a_ref, b_ref = refs[0], refs[1]
        extra_refs, out_refs = refs[2:n_in], refs[n_in:n_in + n_out]

        def prod():
            return _dot(a_ref[...], b_ref[...], kind)

        def finish(acc):
            if epilogue is not None:
                ids = [pl.program_id(ax) for ax in range(len(grid) - 1)]
                first = functools.reduce(jnp.logical_and, [i == 0 for i in ids]) if ids else True
                epilogue(acc, extra_refs, out_refs, first)
                return
            if extra_refs:
                acc = acc + extra_refs[0][...]
            out_refs[0][...] = acc.astype(out_refs[0].dtype)

        if nk == 1:
            finish(prod())
        else:
            acc_ref = refs[n_in + n_out]
            k = pl.program_id(len(grid) - 1)

            @pl.when(k == 0)
            def _():
                acc_ref[...] = prod()

            @pl.when(k > 0)
            def _():
                acc_ref[...] += prod()

            @pl.when(k == nk - 1)
            def _():
                finish(acc_ref[...])

    in_specs = [a_spec, b_spec] + extra_specs
    args = (a, b, *extra)
    scratch = [pltpu.VMEM(acc_shape, F32)] if nk > 1 else []
    sem = ("arbitrary",) * len(grid) if epilogue is not None else ("parallel",) * (len(grid) - 1) + ("arbitrary",)
    out, landed = _pcall(body, args, name=name, out_shape=out_shape, grid=grid, in_specs=in_specs,
                         out_specs=o_spec, scratch_shapes=scratch, sem=sem, comm=comm)
    return out if comm is None else (out, landed)


def _mm_rows(name, a, w, out_dtype=F32, res=None, kind="nn", tm=MM_ROWS, comm=None):
    M, K = a.shape
    N = w.shape[1] if kind == "nn" else w.shape[0]
    tm = _tile(M, tm)
    res_spec = pl.BlockSpec((tm, N), lambda i, k: (i, 0)) if res is not None else None
    return _matmul(
        name, kind, a, w, jax.ShapeDtypeStruct((M, N), out_dtype), (M // tm, 1),
        pl.BlockSpec((tm, K), lambda i, k: (i, 0)), pl.BlockSpec(w.shape, lambda i, k: (0, 0)),
        pl.BlockSpec((tm, N), lambda i, k: (i, 0)), (tm, N), res, res_spec, comm)


def _residual_rms_epilogue(y, operands, outputs, first):
    x_ref, g_ref = operands
    x1_ref, h_ref = outputs
    xv = x_ref[...] + y
    x1_ref[...] = xv
    r = lax.rsqrt(jnp.mean(xv * xv, axis=-1, keepdims=True) + EPS)
    h_ref[...] = (xv * r * g_ref[...]).astype(h_ref.dtype)


def _mm_residual_rms(name, a, w, x, g, tm=MM_ROWS_RES):
    M, K = a.shape
    N = w.shape[1]
    tm = _tile(M, tm)
    row = pl.BlockSpec((tm, N), lambda i, k: (i, 0))
    return _matmul(
        name, "nn", a, w, (jax.ShapeDtypeStruct((M, N), F32), jax.ShapeDtypeStruct((M, N), BF16)), (M // tm, 1),
        pl.BlockSpec((tm, K), lambda i, k: (i, 0)), pl.BlockSpec(w.shape, lambda i, k: (0, 0)),
        (row, row), (tm, N), [x, g], [row, pl.BlockSpec((1, N), lambda i, k: (0, 0))],
        epilogue=_residual_rms_epilogue)


def _mm_tn(name, a, b, out_dtype=F32, tk=MM_TOKENS, comm=None):
    T, M = a.shape
    N = b.shape[1]
    tk = _tile(T, tk)
    return _matmul(
        name, "tn", a, b, jax.ShapeDtypeStruct((M, N), out_dtype), (1, T // tk),
        pl.BlockSpec((tk, M), lambda i, k: (k, 0)), pl.BlockSpec((tk, N), lambda i, k: (k, 0)),
        pl.BlockSpec((M, N), lambda i, k: (0, 0)), (M, N), comm=comm)


def _mm_in_gather(h, shard, tm=MM_ROWS, comm=None):
    T, K = h.shape
    n = shard.shape[1]
    tm = _tile(T, tm)
    n_tiles = T // tm
    pair_of_chip_step = {4: 1, 2: 2, 6: 3}

    def slab_of(s):
        x, y, c = _mesh_pos()
        return _dev_index(x ^ ((s >> 2) & 1), y ^ ((s >> 1) & 1), c ^ (s & 1))

    def body(h_ref, shard_ref, proj_ref, win_ref, wbuf, slot_sems, send_sems, recv_sems, local_sem):
        s, i = pl.program_id(0), pl.program_id(1)
        x, y, c = _mesh_pos()
        me, sib = (x, y, c), (x, y, 1 - c)

        def slot_copy(step):
            src = shard_ref if step == 0 else win_ref.at[slab_of(step)]
            return pltpu.make_async_copy(src, wbuf.at[step % 2], slot_sems.at[step % 2])

        def fetch(step):
            if step >= 1:
                block = win_ref.at[slab_of(step)]
                if step == 1:
                    pair = 0
                elif step % 2 == 0:
                    pair = pair_of_chip_step[step]
                else:
                    pair = 3 + pair_of_chip_step[step - 1]
                _remote(block, block, send_sems, recv_sems, pair, me).wait_recv()
                if step % 2 == 0:
                    _remote(block, block, send_sems, recv_sems, 3 + pair, sib).start()
            slot_copy(step).start()

        @pl.when((s == 0) & (i == 0))
        def _():
            land = win_ref.at[_dev_index(*me)]
            pltpu.make_async_copy(shard_ref, land, local_sem).start()
            _remote(shard_ref, land, send_sems, recv_sems, 0, sib).start()
            for step, pair in pair_of_chip_step.items():
                peer = (x ^ ((step >> 2) & 1), y ^ ((step >> 1) & 1), c)
                _remote(shard_ref, land, send_sems, recv_sems, pair, peer).start()
            fetch(0)

        for step in range(N_DEV):
            @pl.when((s == step) & (i == 0))
            def _():
                slot_copy(step).wait()

            if step + 1 < N_DEV:
                @pl.when((s == step) & (i == n_tiles - 1))
                def _():
                    fetch(step + 1)

        proj_ref[...] = _dot(h_ref[...], wbuf[s % 2])

        @pl.when((s == N_DEV - 1) & (i == n_tiles - 1))
        def _():
            for pair in range(7):
                _remote(shard_ref, win_ref.at[0], send_sems, recv_sems, pair, me).wait_send()
            pltpu.make_async_copy(shard_ref, win_ref.at[_dev_index(*me)], local_sem).wait()

    hbm = pl.BlockSpec(memory_space=pltpu.HBM)
    return _pcall(
        body, (h, shard), name="mm_in",
        out_shape=(jax.ShapeDtypeStruct((T, N_DEV * n), F32), jax.ShapeDtypeStruct((N_DEV, K, n), shard.dtype)),
        grid=(N_DEV, n_tiles), in_specs=[pl.BlockSpec((tm, K), lambda s, i: (i, 0)), hbm],
        out_specs=(pl.BlockSpec((tm, n), lambda s, i: (i, slab_of(s))), hbm),
        scratch_shapes=[pltpu.VMEM((2, K, n), shard.dtype), pltpu.SemaphoreType.DMA((2,)),
                        pltpu.SemaphoreType.DMA((7,)), pltpu.SemaphoreType.DMA((7,)), pltpu.SemaphoreType.DMA],
        sem=("arbitrary", "arbitrary"), comm=comm)


def _rms_bwd_epilogue(dh, operands, outputs, first):
    x_ref, g_ref, dres_ref = operands
    dx_ref, dg_ref = outputs
    xv = x_ref[...]
    r = lax.rsqrt(jnp.mean(xv * xv, axis=-1, keepdims=True) + EPS)
    xhat = xv * r

    @pl.when(first)
    def _():
        dg_ref[...] = jnp.zeros_like(dg_ref)

    dg_ref[...] += jnp.sum(dh * xhat, axis=0, keepdims=True)
    dxhat = dh * g_ref[...]
    dx_ref[...] = dres_ref[...] + r * (dxhat - xhat * jnp.mean(dxhat * xhat, axis=-1, keepdims=True))


def _rms_bwd_fused(M, K, tm, rms):
    row = pl.BlockSpec((tm, K), lambda i, j: (i, 0))
    vec = pl.BlockSpec((1, K), lambda i, j: (0, 0))
    x, g, dres = rms
    return dict(res=[x, g, dres], res_spec=[row, vec, row], epilogue=_rms_bwd_epilogue,
                out_shape=(jax.ShapeDtypeStruct((M, K), F32), jax.ShapeDtypeStruct((1, K), F32)), o_spec=(row, vec))


def _mm_cols_slab_t(name, a, w_slabs, rms, tm=MM_ROWS_RES, comm=None):
    M = a.shape[0]
    J, K, n = w_slabs.shape
    tm = _tile(M, tm)
    fused = _rms_bwd_fused(M, K, tm, rms)
    return _matmul(
        name, "nt", a, w_slabs, fused.pop("out_shape"), (M // tm, J),
        pl.BlockSpec((tm, n), lambda i, j: (i, j)), pl.BlockSpec((None, K, n), lambda i, j: (j, 0, 0)),
        fused.pop("o_spec"), (tm, K), comm=comm, **fused)


def _mm_tn_slab(name, a, b, n, out_dtype=F32, tk=MM_TOKENS, comm=None, part=(0, 1)):
    T, M = a.shape
    p, of = part
    M //= of
    J = b.shape[1] // n
    tk = _tile(T, tk)
    return _matmul(
        name, "tn", a, b, jax.ShapeDtypeStruct((J, M, n), out_dtype), (J, T // tk),
        pl.BlockSpec((tk, M), lambda j, k: (k, p)), pl.BlockSpec((tk, n), lambda j, k: (k, j)),
        pl.BlockSpec((None, M, n), lambda j, k: (j, 0, 0)), (M, n), comm=comm)


def _rms_fwd(name, x, g, tm=512):
    T, Dm = x.shape
    tm = _tile(T, tm)

    def body(x_ref, g_ref, h_ref):
        xv = x_ref[...]
        r = lax.rsqrt(jnp.mean(xv * xv, axis=-1, keepdims=True) + EPS)
        h_ref[...] = (xv * r * g_ref[...]).astype(h_ref.dtype)

    return pl.pallas_call(
        body, out_shape=jax.ShapeDtypeStruct((T, Dm), BF16), grid=(T // tm,),
        in_specs=[pl.BlockSpec((tm, Dm), lambda i: (i, 0)), pl.BlockSpec((1, Dm), lambda i: (0, 0))],
        out_specs=pl.BlockSpec((tm, Dm), lambda i: (i, 0)), name=name, compiler_params=_params(("parallel",)),
    )(x, g)


def _rms_bwd(name, x, g, dh, dres, tm=512):
    T, Dm = x.shape
    tm = _tile(T, tm)
    want_dx = dres is not None

    def body(*refs):
        if want_dx:
            x_ref, g_ref, dh_ref, dres_ref, dx_ref, dg_ref = refs
        else:
            x_ref, g_ref, dh_ref, dg_ref = refs
        xv = x_ref[...]
        r = lax.rsqrt(jnp.mean(xv * xv, axis=-1, keepdims=True) + EPS)
        xhat = xv * r
        dhv = dh_ref[...]

        @pl.when(pl.program_id(0) == 0)
        def _():
            dg_ref[...] = jnp.zeros_like(dg_ref)

        dg_ref[...] += jnp.sum(dhv * xhat, axis=0, keepdims=True)
        if want_dx:
            dxhat = dhv * g_ref[...]
            dx_ref[...] = dres_ref[...] + r * (dxhat - xhat * jnp.mean(dxhat * xhat, axis=-1, keepdims=True))

    row = pl.BlockSpec((tm, Dm), lambda i: (i, 0))
    vec = pl.BlockSpec((1, Dm), lambda i: (0, 0))
    if want_dx:
        return pl.pallas_call(
            body, out_shape=(jax.ShapeDtypeStruct((T, Dm), F32), jax.ShapeDtypeStruct((1, Dm), F32)),
            grid=(T // tm,), in_specs=[row, vec, row, row], out_specs=(row, vec), name=name,
            compiler_params=_params(("arbitrary",)),
        )(x, g, dh, dres)
    return pl.pallas_call(
        body, out_shape=jax.ShapeDtypeStruct((1, Dm), F32), grid=(T // tm,), in_specs=[row, vec, row],
        out_specs=vec, name=name, compiler_params=_params(("arbitrary",)),
    )(x, g, dh)


def _pool_rows(S):
    return _tile(S, 256)


def _pool_count(c0, rows, w):
    t = c0 + lax.broadcasted_iota(jnp.int32, (rows, 1), 0)
    return jnp.minimum(t + 1, w).astype(F32)


def _pool_fwd(proj, w_pool, scale, B, S):
    CH = _pool_rows(S)

    def body(hp_ref, wp_ref, sc_ref, o_ref, pad_ref):
        pad_ref[0:POOL_HALO, :] = jnp.zeros((POOL_HALO, POOL_WIDTH), F32)
        pad_ref[POOL_HALO:, :] = hp_ref[...]
        for gi, w in enumerate(POOL_WINDOWS):
            cols = slice(gi * POOL_GROUP_DIM, (gi + 1) * POOL_GROUP_DIM)
            for c in range(S // CH):
                base = POOL_HALO + c * CH
                acc = pad_ref[base:base + CH, cols]
                tok = acc
                for j in range(1, w):
                    acc = acc + pad_ref[base - j:base - j + CH, cols]
                pooled = acc / _pool_count(c * CH, CH, w) - tok
                z = _dot(pooled, wp_ref[gi])
                o_ref[c * CH:(c + 1) * CH, cols] = (z * sc_ref[:, cols]).astype(o_ref.dtype)

    return pl.pallas_call(
        body, out_shape=jax.ShapeDtypeStruct((B * S, POOL_WIDTH), BF16), grid=(B,),
        in_specs=[pl.BlockSpec((S, POOL_WIDTH), lambda b: (b, 0)),
                  pl.BlockSpec(w_pool.shape, lambda b: (0, 0, 0)),
                  pl.BlockSpec((1, POOL_WIDTH), lambda b: (0, 0))],
        out_specs=pl.BlockSpec((S, POOL_WIDTH), lambda b: (b, 0)),
        scratch_shapes=[pltpu.VMEM((S + POOL_HALO, POOL_WIDTH), F32)],
        name="pool_fwd", compiler_params=_params(("parallel",)),
    )(proj, w_pool, scale)


def _pool_bwd(proj, d_ypre, w_pool, scale, B, S):
    CH = _pool_rows(S)

    def body(hp_ref, dy_ref, wp_ref, sc_ref, dhp_ref, dwp_ref, dsc_ref, pad_ref, sc_pad_ref, dp_ref):
        @pl.when(pl.program_id(0) == 0)
        def _():
            dwp_ref[...] = jnp.zeros_like(dwp_ref)
            dsc_ref[...] = jnp.zeros_like(dsc_ref)

        pad_ref[0:POOL_HALO, :] = jnp.zeros((POOL_HALO, POOL_WIDTH), F32)
        pad_ref[POOL_HALO:, :] = hp_ref[...]
        sc_pad_ref[S:, :] = jnp.zeros((POOL_HALO, POOL_WIDTH), F32)
        for gi, w in enumerate(POOL_WINDOWS):
            cols = slice(gi * POOL_GROUP_DIM, (gi + 1) * POOL_GROUP_DIM)
            for c in range(S // CH):
                base = POOL_HALO + c * CH
                rows = slice(c * CH, (c + 1) * CH)
                acc = pad_ref[base:base + CH, cols]
                tok = acc
                for j in range(1, w):
                    acc = acc + pad_ref[base - j:base - j + CH, cols]
                cnt = _pool_count(c * CH, CH, w)
                pooled = acc / cnt - tok
                z = _dot(pooled, wp_ref[gi])
                dy = dy_ref[rows, cols]
                dsc_ref[:, cols] += jnp.sum(dy * z, axis=0, keepdims=True)
                dz = dy * sc_ref[:, cols]
                dwp_ref[gi] += _dot(pooled, dz, "tn")
                dpool = _dot(dz, wp_ref[gi], "nt")
                dp_ref[rows, cols] = dpool
                sc_pad_ref[rows, cols] = dpool / cnt
            for c in range(S // CH):
                rows = slice(c * CH, (c + 1) * CH)
                acc = sc_pad_ref[rows, cols]
                for j in range(1, w):
                    acc = acc + sc_pad_ref[c * CH + j:c * CH + j + CH, cols]
                dhp_ref[rows, cols] = (acc - dp_ref[rows, cols]).astype(dhp_ref.dtype)

    seq = pl.BlockSpec((S, POOL_WIDTH), lambda b: (b, 0))
    return pl.pallas_call(
        body,
        out_shape=(jax.ShapeDtypeStruct((B * S, POOL_WIDTH), BF16),
                   jax.ShapeDtypeStruct(w_pool.shape, F32), jax.ShapeDtypeStruct((1, POOL_WIDTH), F32)),
        grid=(B,),
        in_specs=[seq, seq, pl.BlockSpec(w_pool.shape, lambda b: (0, 0, 0)),
                  pl.BlockSpec((1, POOL_WIDTH), lambda b: (0, 0))],
        out_specs=(seq, pl.BlockSpec(w_pool.shape, lambda b: (0, 0, 0)),
                   pl.BlockSpec((1, POOL_WIDTH), lambda b: (0, 0))),
        scratch_shapes=[pltpu.VMEM((S + POOL_HALO, POOL_WIDTH), F32),
                        pltpu.VMEM((S + POOL_HALO, POOL_WIDTH), F32),
                        pltpu.VMEM((S, POOL_WIDTH), F32)],
        name="pool_bwd", compiler_params=_params(("arbitrary",)),
    )(proj, d_ypre, w_pool, scale)


def _ret_tables(S):
    half = RET_QK_DIM // 2
    inv = ROPE_BASE ** (-jnp.arange(half, dtype=F32) / half)
    ang = jnp.arange(S, dtype=F32)[:, None] * inv[None, :]
    cos, sin = jnp.cos(ang), jnp.sin(ang)
    cos_full = jnp.concatenate([cos, cos], axis=-1)
    sin_signed = jnp.concatenate([-sin, sin], axis=-1)
    C = RET_CHUNK
    lg = jnp.log1p(-jnp.exp2(-5.0 - jnp.arange(RET_HEADS, dtype=F32)))[:, None, None]
    idx = jnp.arange(C, dtype=F32)
    rel = idx[:, None] - idx[None, :]
    decay = jnp.where(rel >= 0, jnp.exp(jnp.maximum(rel, 0.0) * lg), 0.0)
    q_decay = jnp.broadcast_to(jnp.exp((idx + 1.0)[None, :, None] * lg), (RET_HEADS, C, RET_QK_DIM))
    k_decay = jnp.broadcast_to(jnp.exp((C - 1.0 - idx)[None, :, None] * lg), (RET_HEADS, C, RET_QK_DIM))
    c_decay = jnp.broadcast_to(jnp.exp(C * lg), (RET_HEADS, 1, RET_V_DIM))
    return cos_full, sin_signed, decay, q_decay, k_decay, c_decay


def _rope(x, cos_full, sin_signed):
    return x * cos_full + pltpu.roll(x, RET_QK_DIM // 2, axis=1) * sin_signed


def _rope_t(dy, cos_full, sin_signed):
    return dy * cos_full + pltpu.roll(dy * sin_signed, RET_QK_DIM // 2, axis=1)


RET_COLS = 512


def _ret_specs(N, chunk_of):
    C = RET_CHUNK

    def rows(width, col=0):
        return pl.BlockSpec((C, width), lambda b, i: (b * N + chunk_of(i), col))

    def whole(shape):
        return pl.BlockSpec(shape, lambda b, i: (0,) * len(shape))

    wide = RET_HEADS * RET_V_DIM
    return dict(
        q=rows(RET_COLS, COL_Q // RET_COLS), k=rows(RET_COLS, COL_K // RET_COLS),
        v=[rows(RET_COLS, COL_V // RET_COLS + j) for j in range(2)],
        gr=[rows(RET_COLS, COL_GR // RET_COLS + j) for j in range(2)],
        table=pl.BlockSpec((C, RET_QK_DIM), lambda b, i: (chunk_of(i), 0)),
        decay=whole((RET_HEADS, C, C)), qd=whole((RET_HEADS, C, RET_QK_DIM)), kd=whole((RET_HEADS, C, RET_QK_DIM)),
        cd=whole((RET_HEADS, 1, RET_V_DIM)), vec=whole((1, wide)), qk_rows=rows(RET_COLS), v_rows=rows(wide),
        state=pl.BlockSpec((None, None, RET_HEADS, RET_QK_DIM, RET_V_DIM), lambda b, i: (b, chunk_of(i), 0, 0, 0)))


def _head_cols(h):
    pair = slice((h % 2) * RET_V_DIM, (h % 2 + 1) * RET_V_DIM)
    return slice(h * RET_QK_DIM, (h + 1) * RET_QK_DIM), h // 2, pair, slice(h * RET_V_DIM, (h + 1) * RET_V_DIM)


def _group_norm(o):
    mu = jnp.mean(o, axis=-1, keepdims=True)
    oc = o - mu
    rstd = lax.rsqrt(jnp.mean(oc * oc, axis=-1, keepdims=True) + EPS)
    return oc * rstd, rstd


def _ret_fwd(proj, g_ret, b_ret, tables, B, S, comm=None):
    N = S // RET_CHUNK
    cos_t, sin_t, decay, q_decay, k_decay, c_decay = tables
    sp = _ret_specs(N, lambda i: i)

    def body(q_ref, k_ref, v0_ref, v1_ref, gr0_ref, gr1_ref, cos_ref, sin_ref, dec_ref, qd_ref, kd_ref, cd_ref,
             g_ref, b_ref, y_ref, rs_ref, r_ref):
        @pl.when(pl.program_id(1) == 0)
        def _():
            r_ref[...] = jnp.zeros_like(r_ref)

        cs, sn = cos_ref[...], sin_ref[...]
        for h in range(RET_HEADS):
            qk, j, pair, wide = _head_cols(h)
            q = _rope(q_ref[:, qk], cs, sn)
            k = _rope(k_ref[:, qk], cs, sn) * (RET_QK_DIM ** -0.5)
            v = (v0_ref, v1_ref)[j][:, pair]
            R = r_ref[h]
            rs_ref[h] = R
            s = _dot(q, k, "nt") * dec_ref[h]
            o = _dot(s, v) + _dot(q * qd_ref[h], R)
            r_ref[h] = cd_ref[h] * R + _dot(k * kd_ref[h], v, "tn")
            on, _ = _group_norm(o)
            gr = (gr0_ref, gr1_ref)[j][:, pair]
            y_ref[:, wide] = (gr * jax.nn.sigmoid(gr) * (on * g_ref[:, wide] + b_ref[:, wide])).astype(y_ref.dtype)

    state = jax.ShapeDtypeStruct((B, N, RET_HEADS, RET_QK_DIM, RET_V_DIM), F32)
    return _pcall(
        body, (proj,) * 6 + (cos_t, sin_t, decay, q_decay, k_decay, c_decay, g_ret, b_ret),
        name="ret_fwd", out_shape=(jax.ShapeDtypeStruct((B * S, RET_HEADS * RET_V_DIM), BF16), state), grid=(B, N),
        in_specs=[sp["q"], sp["k"], *sp["v"], *sp["gr"], sp["table"], sp["table"], sp["decay"], sp["qd"],
                  sp["kd"], sp["cd"], sp["vec"], sp["vec"]],
        out_specs=(sp["v_rows"], sp["state"]),
        scratch_shapes=[pltpu.VMEM((RET_HEADS, RET_QK_DIM, RET_V_DIM), F32)],
        sem=("parallel", "arbitrary"), comm=comm)


def _ret_bwd(proj, states, d_yr, g_ret, b_ret, tables, B, S, comm=None):
    N = S // RET_CHUNK
    cos_t, sin_t, decay, q_decay, k_decay, c_decay = tables
    sp = _ret_specs(N, lambda i: N - 1 - i)
    qk_scale = RET_QK_DIM ** -0.5

    def body(q_ref, k_ref, v0_ref, v1_ref, gr0_ref, gr1_ref, dy_ref, rs_ref, cos_ref, sin_ref, dec_ref, qd_ref,
             kd_ref, cd_ref, g_ref, b_ref, dq_ref, dk_ref, dv_ref, dgr_ref, dg_ref, db_ref, dr_ref):
        @pl.when((pl.program_id(0) == 0) & (pl.program_id(1) == 0))
        def _():
            dg_ref[...] = jnp.zeros_like(dg_ref)
            db_ref[...] = jnp.zeros_like(db_ref)

        @pl.when(pl.program_id(1) == 0)
        def _():
            dr_ref[...] = jnp.zeros_like(dr_ref)

        cs, sn = cos_ref[...], sin_ref[...]
        for h in range(RET_HEADS):
            qk, j, pair, wide = _head_cols(h)
            q = _rope(q_ref[:, qk], cs, sn)
            k = _rope(k_ref[:, qk], cs, sn) * qk_scale
            v = (v0_ref, v1_ref)[j][:, pair]
            R, dR = rs_ref[h], dr_ref[h]
            dec, qd, kd = dec_ref[h], qd_ref[h], kd_ref[h]
            s = _dot(q, k, "nt") * dec
            o = _dot(s, v) + _dot(q * qd, R)
            on, rstd = _group_norm(o)
            g = g_ref[:, wide]
            oaff = on * g + b_ref[:, wide]
            gr = (gr0_ref, gr1_ref)[j][:, pair]
            sg = jax.nn.sigmoid(gr)
            dy = dy_ref[:, wide]
            dgr_ref[:, wide] = (dy * oaff * (sg * (1.0 + gr * (1.0 - sg)))).astype(dgr_ref.dtype)
            doaff = dy * (gr * sg)
            dg_ref[:, wide] += jnp.sum(doaff * on, axis=0, keepdims=True)
            db_ref[:, wide] += jnp.sum(doaff, axis=0, keepdims=True)
            don = doaff * g
            do = rstd * (don - jnp.mean(don, axis=-1, keepdims=True)
                         - on * jnp.mean(don * on, axis=-1, keepdims=True))
            ds = _dot(do, v, "nt") * dec
            dq = _dot(ds, k) + qd * _dot(do, R, "nt")
            dk = _dot(ds, q, "tn") + kd * _dot(v, dR, "nt")
            dv_ref[:, wide] = (_dot(s, do, "tn") + _dot(k * kd, dR)).astype(dv_ref.dtype)
            dr_ref[h] = cd_ref[h] * dR + _dot(q * qd, do, "tn")
            dq_ref[:, qk] = _rope_t(dq, cs, sn).astype(dq_ref.dtype)
            dk_ref[:, qk] = _rope_t(dk * qk_scale, cs, sn).astype(dk_ref.dtype)

    T = B * S
    qk_shape = jax.ShapeDtypeStruct((T, RET_HEADS * RET_QK_DIM), BF16)
    v_shape = jax.ShapeDtypeStruct((T, RET_HEADS * RET_V_DIM), BF16)
    vec_shape = jax.ShapeDtypeStruct((1, RET_HEADS * RET_V_DIM), F32)
    return _pcall(
        body, (proj,) * 6 + (d_yr, states, cos_t, sin_t, decay, q_decay, k_decay, c_decay, g_ret, b_ret),
        name="ret_bwd", out_shape=(qk_shape, qk_shape, v_shape, v_shape, vec_shape, vec_shape), grid=(B, N),
        in_specs=[sp["q"], sp["k"], *sp["v"], *sp["gr"], sp["v_rows"], sp["state"], sp["table"], sp["table"],
                  sp["decay"], sp["qd"], sp["kd"], sp["cd"], sp["vec"], sp["vec"]],
        out_specs=(sp["qk_rows"], sp["qk_rows"], sp["v_rows"], sp["v_rows"], sp["vec"], sp["vec"]),
        scratch_shapes=[pltpu.VMEM((RET_HEADS, RET_QK_DIM, RET_V_DIM), F32)],
        sem=("arbitrary", "arbitrary"), comm=comm)


def _xa_rows(S):
    return _tile(S, 256)


def _xa_specs(S, M):
    q = pl.BlockSpec((S, XA_HEAD_DIM), lambda b, h: (b, COL_QX // XA_HEAD_DIM + h))
    k = pl.BlockSpec((M, XA_HEAD_DIM), lambda b, h: (b, h))
    v = pl.BlockSpec((M, XA_HEAD_DIM), lambda b, h: (b, XA_HEADS + h))
    o = pl.BlockSpec((S, XA_HEAD_DIM), lambda b, h: (b, h))
    return q, k, v, o


def _softmax_rows(s):
    e = jnp.exp(s - jnp.max(s, axis=-1, keepdims=True))
    return e / jnp.sum(e, axis=-1, keepdims=True)


def _xa_fwd(proj, kv, B, S, M, comm=None):
    CH = _xa_rows(S)
    q_spec, k_spec, v_spec, o_spec = _xa_specs(S, M)

    def body(q_ref, k_ref, v_ref, o_ref):
        def chunk(i, carry):
            rows = pl.ds(pl.multiple_of(i * CH, CH), CH)
            p = _softmax_rows(_dot(q_ref[rows, :], k_ref[...], "nt") * (XA_HEAD_DIM ** -0.5))
            o_ref[rows, :] = _dot(p, v_ref[...]).astype(o_ref.dtype)
            return carry

        lax.fori_loop(0, S // CH, chunk, 0, unroll=True)

    return _pcall(
        body, (proj, kv, kv), name="xattn_fwd", out_shape=jax.ShapeDtypeStruct((B * S, XA_WIDTH), BF16),
        grid=(B, XA_HEADS), in_specs=[q_spec, k_spec, v_spec], out_specs=o_spec,
        sem=("parallel", "parallel"), comm=comm)


def _xa_bwd(proj, kv, d_o, B, S, M, comm=None):
    CH = _xa_rows(S)
    q_spec, k_spec, v_spec, o_spec = _xa_specs(S, M)
    scale = XA_HEAD_DIM ** -0.5

    def body(q_ref, k_ref, v_ref, do_ref, dq_ref, dk_ref, dv_ref):
        dk_ref[...] = jnp.zeros_like(dk_ref)
        dv_ref[...] = jnp.zeros_like(dv_ref)

        def chunk(i, carry):
            rows = pl.ds(pl.multiple_of(i * CH, CH), CH)
            q, do = q_ref[rows, :], do_ref[rows, :]
            p = _softmax_rows(_dot(q, k_ref[...], "nt") * scale)
            dp = _dot(do, v_ref[...], "nt")
            ds = p * (dp - jnp.sum(dp * p, axis=-1, keepdims=True)) * scale
            dq_ref[rows, :] = _dot(ds, k_ref[...]).astype(dq_ref.dtype)
            dk_ref[...] += _dot(ds, q, "tn")
            dv_ref[...] += _dot(p, do, "tn")
            return carry

        lax.fori_loop(0, S // CH, chunk, 0, unroll=True)

    kv_out = pl.BlockSpec((M, XA_HEAD_DIM), lambda b, h: (b, h))
    return _pcall(
        body, (proj, kv, kv, d_o), name="xattn_bwd",
        out_shape=(jax.ShapeDtypeStruct((B * S, XA_WIDTH), BF16), jax.ShapeDtypeStruct((B * M, XA_WIDTH), F32),
                   jax.ShapeDtypeStruct((B * M, XA_WIDTH), F32)),
        grid=(B, XA_HEADS), in_specs=[q_spec, k_spec, v_spec, o_spec], out_specs=(o_spec, kv_out, kv_out),
        sem=("parallel", "parallel"), comm=comm)


def _gate_specs(tm):
    n = COL_GL // D_MODEL
    return [pl.BlockSpec((tm, D_MODEL), lambda i, j=j: (i, n + j)) for j in range(3)]


def _merge_fwd(proj, ys, tm=512, comm=None):
    T = proj.shape[0]
    tm = _tile(T, tm)
    row = pl.BlockSpec((tm, D_MODEL), lambda i: (i, 0))

    def body(g0, g1, g2, y0, y1, y2, o_ref):
        acc = jax.nn.sigmoid(g0[...]) * y0[...]
        acc = acc + jax.nn.sigmoid(g1[...]) * y1[...]
        acc = acc + jax.nn.sigmoid(g2[...]) * y2[...]
        o_ref[...] = acc.astype(o_ref.dtype)

    return _pcall(
        body, (proj, proj, proj, *ys), name="merge_fwd", out_shape=jax.ShapeDtypeStruct((T, D_MODEL), BF16),
        grid=(T // tm,), in_specs=_gate_specs(tm) + [row] * 3, out_specs=row, sem=("parallel",), comm=comm)


def _merge_bwd(proj, ys, d_merged, tm=512, comm=None):
    T = proj.shape[0]
    tm = _tile(T, tm)
    row = pl.BlockSpec((tm, D_MODEL), lambda i: (i, 0))

    def body(g0, g1, g2, y0, y1, y2, dm_ref, dgl_ref, d0, d1, d2):
        dm = dm_ref[...]
        for j, (g_ref, y_ref, d_ref) in enumerate(((g0, y0, d0), (g1, y1, d1), (g2, y2, d2))):
            sg = jax.nn.sigmoid(g_ref[...])
            d_ref[...] = (dm * sg).astype(d_ref.dtype)
            dgl_ref[:, j * D_MODEL:(j + 1) * D_MODEL] = (dm * y_ref[...] * sg * (1.0 - sg)).astype(dgl_ref.dtype)

    dy = jax.ShapeDtypeStruct((T, D_MODEL), BF16)
    return _pcall(
        body, (proj, proj, proj, *ys, d_merged), name="merge_bwd",
        out_shape=(jax.ShapeDtypeStruct((T, 3 * D_MODEL), BF16), dy, dy, dy), grid=(T // tm,),
        in_specs=_gate_specs(tm) + [row] * 4,
        out_specs=(pl.BlockSpec((tm, 3 * D_MODEL), lambda i: (i, 0)), row, row, row),
        sem=("parallel",), comm=comm)


def _gelu(x):
    return 0.5 * x * (1.0 + jnp.tanh(GELU_C * (x + GELU_A * x * x * x)))


def _gelu_grad(x):
    t = jnp.tanh(GELU_C * (x + GELU_A * x * x * x))
    return 0.5 * (1.0 + t) + 0.5 * x * (1.0 - t * t) * GELU_C * (1.0 + 3.0 * GELU_A * x * x)


GLU_HALO = 16


def _shift_down(x, prev, n):
    last = prev.shape[0]
    r = lax.broadcasted_iota(jnp.int32, (8, 1), 0)
    rolled = pltpu.roll(x, n, axis=0)
    head = rolled[0:8]
    for j in range(n):
        head = jnp.where(r == j, prev[last - n + j:last - n + j + 1, :], head)
    return jnp.concatenate([head, rolled[8:]], axis=0)


def _shift_up(x, nxt, n):
    rows = x.shape[0]
    r = lax.broadcasted_iota(jnp.int32, (8, 1), 0)
    rolled = pltpu.roll(x, rows - n, axis=0)
    tail = rolled[rows - 8:]
    for j in range(n):
        tail = jnp.where(r == 8 - n + j, nxt[j:j + 1, :], tail)
    return jnp.concatenate([rolled[:rows - 8], tail], axis=0)


def _conv(a, prev, cw, cb):
    return _shift_down(a, prev, 2) * cw[0:1, :] + _shift_down(a, prev, 1) * cw[1:2, :] + a * cw[2:3, :] + cb


def _glu_fwd(up, cw, cb, S, tm=512, comm=None):
    T = up.shape[2]
    tm = _tile(S, tm)
    per_seq = S // tm

    def body(ab_ref, prev_ref, cw_ref, cb_ref, u_ref):
        i = pl.program_id(1)
        prev = jnp.where(i % per_seq == 0, 0.0, prev_ref[...].astype(F32))
        ac = _conv(ab_ref[0].astype(F32), prev, cw_ref[...], cb_ref[...])
        u_ref[...] = (_gelu(ac) * ab_ref[1].astype(F32)).astype(u_ref.dtype)

    before = tm // GLU_HALO
    return _pcall(
        body, (up, up, cw, cb), name="glu_fwd",
        out_shape=jax.ShapeDtypeStruct((FFN_SLABS, T, UP_SHARD), BF16), grid=(FFN_SLABS, T // tm),
        in_specs=[pl.BlockSpec((2, None, tm, UP_SHARD), lambda d, i: (0, d, i, 0)),
                  pl.BlockSpec((None, None, GLU_HALO, UP_SHARD),
                               lambda d, i: (0, d, jnp.maximum(i * before - 1, 0), 0)),
                  pl.BlockSpec((None, 3, UP_SHARD), lambda d, i: (d, 0, 0)),
                  pl.BlockSpec((None, 1, UP_SHARD), lambda d, i: (d, 0, 0))],
        out_specs=pl.BlockSpec((None, tm, UP_SHARD), lambda d, i: (d, i, 0)),
        sem=("parallel", "parallel"), comm=comm)


def _glu_bwd(up, d_u, cw, cb, S, tm=256, comm=None):
    T = up.shape[2]
    tm = _tile(S, tm)
    per_seq = S // tm
    n_tiles = T // tm
    per_tile = tm // GLU_HALO

    def body(ab_ref, prev_ref, abn_ref, du_ref, dun_ref, cw_ref, cb_ref, dup_ref, dcw_ref, dcb_ref):
        i = pl.program_id(1)

        @pl.when(i == 0)
        def _():
            dcw_ref[...] = jnp.zeros_like(dcw_ref)
            dcb_ref[...] = jnp.zeros_like(dcb_ref)

        cw, cb = cw_ref[...], cb_ref[...]
        a, b = ab_ref[0].astype(F32), ab_ref[1].astype(F32)
        prev = jnp.where(i % per_seq == 0, 0.0, prev_ref[...].astype(F32))
        a2, a1 = _shift_down(a, prev, 2), _shift_down(a, prev, 1)
        ac = a2 * cw[0:1, :] + a1 * cw[1:2, :] + a * cw[2:3, :] + cb
        du = du_ref[...].astype(F32)
        dup_ref[1] = (du * _gelu(ac)).astype(dup_ref.dtype)
        dac = du * b * _gelu_grad(ac)
        dcb_ref[...] += jnp.sum(dac, axis=0, keepdims=True)
        dcw_ref[0:1, :] += jnp.sum(dac * a2, axis=0, keepdims=True)
        dcw_ref[1:2, :] += jnp.sum(dac * a1, axis=0, keepdims=True)
        dcw_ref[2:3, :] += jnp.sum(dac * a, axis=0, keepdims=True)
        acn = _conv(abn_ref[0].astype(F32), a[tm - GLU_HALO:, :], cw, cb)
        dacn = jnp.where(i % per_seq == per_seq - 1, 0.0,
                         dun_ref[...].astype(F32) * abn_ref[1].astype(F32) * _gelu_grad(acn))
        da = dac * cw[2:3, :] + _shift_up(dac, dacn, 1) * cw[1:2, :] + _shift_up(dac, dacn, 2) * cw[0:1, :]
        dup_ref[0] = da.astype(dup_ref.dtype)

    def nxt(i):
        return jnp.minimum((i + 1) * per_tile, T // GLU_HALO - 1)

    return _pcall(
        body, (up, up, up, d_u, d_u, cw, cb), name="glu_bwd",
        out_shape=(jax.ShapeDtypeStruct((2, FFN_SLABS, T, UP_SHARD), BF16),
                   jax.ShapeDtypeStruct((FFN_SLABS, 3, UP_SHARD), F32),
                   jax.ShapeDtypeStruct((FFN_SLABS, 1, UP_SHARD), F32)),
        grid=(FFN_SLABS, n_tiles),
        in_specs=[pl.BlockSpec((2, None, tm, UP_SHARD), lambda d, i: (0, d, i, 0)),
                  pl.BlockSpec((None, None, GLU_HALO, UP_SHARD),
                               lambda d, i: (0, d, jnp.maximum(i * per_tile - 1, 0), 0)),
                  pl.BlockSpec((2, None, GLU_HALO, UP_SHARD), lambda d, i: (0, d, nxt(i), 0)),
                  pl.BlockSpec((None, tm, UP_SHARD), lambda d, i: (d, i, 0)),
                  pl.BlockSpec((None, GLU_HALO, UP_SHARD), lambda d, i: (d, nxt(i), 0)),
                  pl.BlockSpec((None, 3, UP_SHARD), lambda d, i: (d, 0, 0)),
                  pl.BlockSpec((None, 1, UP_SHARD), lambda d, i: (d, 0, 0))],
        out_specs=(pl.BlockSpec((2, None, tm, UP_SHARD), lambda d, i: (0, d, i, 0)),
                   pl.BlockSpec((None, 3, UP_SHARD), lambda d, i: (d, 0, 0)),
                   pl.BlockSpec((None, 1, UP_SHARD), lambda d, i: (d, 0, 0))),
        sem=("parallel", "arbitrary"), comm=comm)


def _mm_up(h2, w_up_t, tm=MM_ROWS):
    T, K = h2.shape
    tm = _tile(T, tm)
    return _matmul(
        "mm_up", "nt", h2, w_up_t, jax.ShapeDtypeStruct((N_DEV, T, UP_SHARD), BF16), (N_DEV, T // tm, 1),
        pl.BlockSpec((tm, K), lambda j, i, k: (i, 0)), pl.BlockSpec((None, UP_SHARD, K), lambda j, i, k: (j, 0, 0)),
        pl.BlockSpec((None, tm, UP_SHARD), lambda j, i, k: (j, i, 0)), (tm, UP_SHARD))


def _loss_epilogue(ffn, operands, outputs, first):
    x1_ref, t_ref, g_ref = operands
    dx_ref, dg_ref, loss_ref = outputs

    @pl.when(first)
    def _():
        dg_ref[...] = jnp.zeros_like(dg_ref)
        loss_ref[...] = jnp.zeros_like(loss_ref)

    xv = x1_ref[...] + ffn
    r = lax.rsqrt(jnp.mean(xv * xv, axis=-1, keepdims=True) + EPS)
    xhat = xv * r
    err = xhat * g_ref[...] - t_ref[...]
    loss_ref[...] += (0.5 / D_MODEL) * jnp.sum(err * err)
    dy = err * (1.0 / D_MODEL)
    dg_ref[...] += jnp.sum(dy * xhat, axis=0, keepdims=True)
    dxhat = dy * g_ref[...]
    dx_ref[...] = r * (dxhat - xhat * jnp.mean(dxhat * xhat, axis=-1, keepdims=True))


def _mm_down_loss(u, w_down, x1, target, g_final, tm=MM_ROWS_RES):
    J, T, n = u.shape
    tm = _tile(T, tm)
    row = pl.BlockSpec((tm, D_MODEL), lambda i, d: (i, 0))
    vec = pl.BlockSpec((1, D_MODEL), lambda i, d: (0, 0))
    vec_shape = jax.ShapeDtypeStruct((1, D_MODEL), F32)
    return _matmul(
        "mm_down", "nn", u, w_down, (jax.ShapeDtypeStruct((T, D_MODEL), F32), vec_shape, vec_shape), (T // tm, J),
        pl.BlockSpec((None, tm, n), lambda i, d: (d, i, 0)), pl.BlockSpec((None, n, D_MODEL), lambda i, d: (d, 0, 0)),
        (row, vec, vec), (tm, D_MODEL), [x1, target, g_final], [row, row, vec], epilogue=_loss_epilogue)


def _mm_down_t(dx, w_down, tm=MM_ROWS):
    T = dx.shape[0]
    J, n, _ = w_down.shape
    tm = _tile(T, tm)
    return _matmul(
        "mm_down_t", "nt", dx, w_down, jax.ShapeDtypeStruct((J, T, n), BF16), (J, T // tm, 1),
        pl.BlockSpec((tm, D_MODEL), lambda d, i, k: (i, 0)), pl.BlockSpec((None, n, D_MODEL), lambda d, i, k: (d, 0, 0)),
        pl.BlockSpec((None, tm, n), lambda d, i, k: (d, i, 0)), (tm, n))


def _mm_dw_down(u, dx, tk=MM_TOKENS):
    J, T, n = u.shape
    tk = _tile(T, tk)
    return _matmul(
        "mm_dw_down", "tn", u, dx, jax.ShapeDtypeStruct((J, n, D_MODEL), BF16), (J, T // tk),
        pl.BlockSpec((None, tk, n), lambda d, k: (d, k, 0)), pl.BlockSpec((tk, D_MODEL), lambda d, k: (k, 0)),
        pl.BlockSpec((None, n, D_MODEL), lambda d, k: (d, 0, 0)), (n, D_MODEL))


def _mm_dw_up(h2, d_up, tk=MM_TOKENS):
    T, K = h2.shape
    tk = _tile(T, tk)
    return _matmul(
        "mm_dw_up", "tn", d_up, h2, jax.ShapeDtypeStruct((N_DEV, UP_SHARD, K), BF16), (N_DEV, T // tk),
        pl.BlockSpec((None, tk, UP_SHARD), lambda j, k: (j, k, 0)), pl.BlockSpec((tk, K), lambda j, k: (k, 0)),
        pl.BlockSpec((None, UP_SHARD, K), lambda j, k: (j, 0, 0)), (UP_SHARD, K))


def _mm_up_t(d_up, w_up_t, rms, tm=MM_ROWS_RES, comm=None):
    J, T, n = d_up.shape
    K = w_up_t.shape[2]
    tm = _tile(T, tm)
    fused = _rms_bwd_fused(T, K, tm, rms)
    return _matmul(
        "mm_up_t", "nn", d_up, w_up_t, fused.pop("out_shape"), (T // tm, J),
        pl.BlockSpec((None, tm, n), lambda i, j: (j, i, 0)), pl.BlockSpec((None, n, K), lambda i, j: (j, 0, 0)),
        fused.pop("o_spec"), (tm, K), comm=comm, **fused)


def _cast_shards(shards):
    def body(*refs):
        n = len(refs) // 2
        for src, dst in zip(refs[:n], refs[n:]):
            dst[...] = src[...].astype(dst.dtype)

    return pl.pallas_call(
        body, out_shape=[jax.ShapeDtypeStruct(s.shape, BF16) for s in shards], name="cast_shards",
        compiler_params=pltpu.CompilerParams(vmem_limit_bytes=VMEM_LIMIT),
    )(*shards)


def _adamw(w, g, m, v):
    m = ADAM_B1 * m + (1.0 - ADAM_B1) * g
    v = ADAM_B2 * v + (1.0 - ADAM_B2) * (g * g)
    m_hat = m / (1.0 - ADAM_B1 ** ADAM_STEP)
    v_hat = v / (1.0 - ADAM_B2 ** ADAM_STEP)
    delta = -ADAM_LR * (m_hat / (jnp.sqrt(v_hat) + ADAM_EPS) + ADAM_WD * w)
    return delta, m, v


def _sum_parts(p_ref):
    g = p_ref[0].astype(F32)
    for d in range(1, N_DEV):
        g = g + p_ref[d].astype(F32)
    return g


def _reduce_adam(name, parts, w, m, v, tr=128):
    R, Cn = w.shape
    by_rows = sum(p.shape[1] for p in parts) == R and len(parts) > 1
    tr = math.gcd(tr, *[p.shape[1] for p in parts])
    n_tiles = [p.shape[1] // tr for p in parts]
    first = [sum(n_tiles[:j]) for j in range(len(parts))] if by_rows else [0] * len(parts)

    def body(*refs):
        p_refs = refs[:len(parts)]
        w_ref, m_ref, v_ref, g_out, d_out, m_out, v_out = refs[len(parts):]

        def update(p_ref):
            g = _sum_parts(p_ref)
            delta, m_new, v_new = _adamw(w_ref[...], g, m_ref[...], v_ref[...])
            g_out[...] = g
            d_out[...] = delta
            m_out[...] = m_new
            v_out[...] = v_new

        if len(parts) == 1:
            update(p_refs[0])
        elif by_rows:
            i = pl.program_id(0)
            for p_ref, t0, n in zip(p_refs, first, n_tiles):
                pl.when((i >= t0) & (i < t0 + n))(functools.partial(update, p_ref))
        else:
            c = lax.axis_index("c")
            for side, p_ref in enumerate(p_refs):
                pl.when(c == side)(functools.partial(update, p_ref))

    def part_spec(t0, n):
        return pl.BlockSpec((N_DEV, tr, Cn), lambda i: (0, jnp.clip(i - t0, 0, n - 1), 0))

    row = pl.BlockSpec((tr, Cn), lambda i: (i, 0))
    shape = jax.ShapeDtypeStruct((R, Cn), F32)
    return pl.pallas_call(
        body, out_shape=(shape,) * 4, grid=(R // tr,),
        in_specs=[part_spec(t0, n) for t0, n in zip(first, n_tiles)] + [row, row, row],
        out_specs=(row,) * 4, name=name, compiler_params=_params(("parallel",)),
    )(*parts, w, m, v)


def _small_adam(name, gathered, params):
    n_g, n_p = len(gathered), len(params)

    def body(*refs):
        g_refs = refs[:n_g]
        wmv = refs[n_g:n_g + 3 * n_p]
        sums = refs[n_g + 3 * n_p:2 * n_g + 3 * n_p]
        upd = refs[2 * n_g + 3 * n_p:]
        for j in range(n_g):
            g = _sum_parts(g_refs[j])
            sums[j][...] = g
            if j < n_p:
                w_ref, m_ref, v_ref = wmv[3 * j:3 * j + 3]
                delta, m_new, v_new = _adamw(w_ref[...], g, m_ref[...], v_ref[...])
                upd[3 * j][...] = delta
                upd[3 * j + 1][...] = m_new
                upd[3 * j + 2][...] = v_new

    flat = [a for wmv in params for a in wmv]
    out_shape = [jax.ShapeDtypeStruct(g.shape[1:], F32) for g in gathered]
    out_shape += [jax.ShapeDtypeStruct(a.shape, F32) for a in flat]
    res = pl.pallas_call(body, out_shape=out_shape, name=name)(*gathered, *flat)
    return res[:n_g], [tuple(res[n_g + 3 * j:n_g + 3 * j + 3]) for j in range(n_p)]


def _adam_only(name, g, w, m, v):
    def body(g_ref, w_ref, m_ref, v_ref, d_out, m_out, v_out):
        delta, m_new, v_new = _adamw(w_ref[...], g_ref[...], m_ref[...], v_ref[...])
        d_out[...] = delta
        m_out[...] = m_new
        v_out[...] = v_new

    shape = jax.ShapeDtypeStruct(w.shape, F32)
    return pl.pallas_call(body, out_shape=(shape,) * 3, name=name)(g, w, m, v)


def kernel(x, mem, g_mix, w_in, w_pool, pool_scale, w_a, g_ret, b_ret, w_r, g_mem, w_mem_kv, w_c, w_out, g_ffn, w_up, conv_w, conv_b, w_down, g_final, loss_target, m_g_mix, m_w_in, m_w_pool, m_pool_scale, m_w_a, m_g_ret, m_b_ret, m_w_r, m_g_mem, m_w_mem_kv, m_w_c, m_w_out, m_g_ffn, m_w_up, m_conv_w, m_conv_b, m_w_down, m_g_final, v_g_mix, v_w_in, v_w_pool, v_pool_scale, v_w_a, v_g_ret, v_b_ret, v_w_r, v_g_mem, v_w_mem_kv, v_w_c, v_w_out, v_g_ffn, v_w_up, v_conv_w, v_conv_b, v_w_down, v_g_final):
    B, S, _ = x.shape
    M = mem.shape[1]
    T = B * S
    me = _my_index()
    x2d = x.reshape(T, D_MODEL)
    mem2d = mem.reshape(B * M, D_MODEL)
    tgt2d = loss_target.reshape(T, D_MODEL)
    g_final2 = g_final.reshape(1, D_MODEL)

    big = dict(w_in=w_in[0], w_a=w_a[0], w_r=w_r[0], w_mem_kv=w_mem_kv[0], w_c=w_c[0], w_out=w_out[0],
               w_up=w_up[0].T, w_down=w_down[0])
    names = list(big)
    cast = dict(zip(names, _cast_shards([big[n] for n in names])))
    cb = conv_b[0].reshape(FFN_SLABS, 1, UP_SHARD)
    wp = w_pool[0]
    tables = _ret_tables(S)

    h = _rms_fwd("rms_mix", x2d, g_mix)
    early = ("w_a", "w_r", "w_mem_kv", "w_c", "w_out")
    (proj, Win), landed = _mm_in_gather(h, cast["w_in"], comm=_Gather([cast[n] for n in early] + [conv_w[0]]))
    W = dict(zip(early, landed))
    (yr, ret_states), (Wup,) = _ret_fwd(proj, g_ret, b_ret, tables, B, S, comm=_Gather([cast["w_up"]]))
    cw_full = landed[-1].transpose(1, 0, 2).reshape(3, FFN_HIDDEN)
    cw = cw_full.reshape(3, FFN_SLABS, UP_SHARD).transpose(1, 0, 2)
    Wa = W["w_a"].transpose(1, 0, 2).reshape(POOL_WIDTH, D_MODEL)
    Wc = W["w_c"].transpose(1, 0, 2).reshape(XA_WIDTH, D_MODEL)
    Wr = W["w_r"].reshape(D_MODEL, D_MODEL)
    Wkv = W["w_mem_kv"].reshape(D_MODEL, D_MODEL)
    Wout = W["w_out"].reshape(D_MODEL, D_MODEL)
    ypre = _pool_fwd(proj, wp, pool_scale, B, S)
    y_pool = _mm_rows("mm_a", ypre, Wa, BF16)
    y_ret = _mm_rows("mm_r", yr, Wr, BF16)
    mem_n = _rms_fwd("rms_mem", mem2d, g_mem)
    kv = _mm_rows("mm_kv", mem_n, Wkv)
    o_mem, (Wdown,) = _xa_fwd(proj, kv, B, S, M, comm=_Gather([cast["w_down"]]))
    Wdown = Wdown.reshape(FFN_SLABS, UP_SHARD, D_MODEL)
    y_mem = _mm_rows("mm_c", o_mem, Wc, BF16)
    ys = (y_pool, y_ret, y_mem)
    merged = _merge_fwd(proj, ys)[0]
    x1, h2 = _mm_residual_rms("mm_out", merged, Wout, x2d, g_ffn)
    up = _mm_up(h2, Wup).reshape(2, FFN_SLABS, T, UP_SHARD)
    u = _glu_fwd(up, cw, cb, S)[0]

    dx2, dg_final, loss_part = _mm_down_loss(u, Wdown, x1, tgt2d, g_final2)
    received = {}
    d_u = _mm_down_t(dx2, Wdown)
    dW_down = _mm_dw_down(u, dx2)
    (d_up, d_cw, d_cb), (received["w_down"],) = _glu_bwd(
        up, d_u, cw, cb, S, comm=_Exchange([dW_down.reshape(N_DEV, -1, D_MODEL)]))
    d_up = d_up.reshape(N_DEV, T, UP_SHARD)
    dW_up = _mm_dw_up(h2, d_up)
    (dx1, dg_ffn), (up_c0,) = _mm_up_t(d_up, Wup, (x1, g_ffn, dx2), comm=_ExchangeTo([dW_up], 0))
    d_merged = _mm_rows("mm_out_t", dx1, Wout, kind="nt")
    dW_out = _mm_tn("mm_dw_out", merged, dx1, BF16)
    (d_gl, d_y_pool, d_y_ret, d_y_mem), (received["w_out"],) = _merge_bwd(
        proj, ys, d_merged, comm=_Exchange([dW_out.reshape(N_DEV, -1, D_MODEL)]))
    dW_c = _mm_tn("mm_dw_c", o_mem, d_y_mem, BF16)
    d_o_mem = _mm_rows("mm_c_t", d_y_mem, Wc, kind="nt")
    (d_qx, d_kmem, d_vmem), (up_c1,) = _xa_bwd(proj, kv, d_o_mem, B, S, M, comm=_ExchangeTo([dW_up], 1))
    received["w_up"] = [up_c0, up_c1]
    d_kv = jnp.concatenate([d_kmem, d_vmem], axis=1)
    dW_kv = _mm_tn("mm_dw_kv", mem_n, d_kv, BF16)
    d_mem_n = _mm_rows("mm_kv_t", d_kv, Wkv, kind="nt")
    dg_mem = _rms_bwd("rms_mem_bwd", mem2d, g_mem, d_mem_n, None)
    dW_a = _mm_tn("mm_dw_a", ypre, d_y_pool, BF16)
    d_ypre = _mm_rows("mm_a_t", d_y_pool, Wa, kind="nt")
    d_hp, dw_pool, d_scale = _pool_bwd(proj, d_ypre, wp, pool_scale, B, S)
    dW_r = _mm_tn("mm_dw_r", yr, d_y_ret, BF16)
    d_yr = _mm_rows("mm_r_t", d_y_ret, Wr, kind="nt")
    (d_q, d_k, d_v, d_gr, dg_ret, db_ret), landed = _ret_bwd(
        proj, ret_states, d_yr, g_ret, b_ret, tables, B, S,
        comm=_Exchange([dW_a.reshape(POOL_WIDTH, N_DEV, -1).transpose(1, 0, 2), dW_r.reshape(N_DEV, -1, D_MODEL),
                        dW_c.reshape(XA_WIDTH, N_DEV, -1).transpose(1, 0, 2), dW_kv.reshape(N_DEV, -1, D_MODEL)]))
    received["w_a"], received["w_r"], received["w_c"], received["w_mem_kv"] = landed
    small_names = ["w_pool", "pool_scale", "g_ret", "b_ret", "g_mem", "g_ffn", "conv_b", "g_final"]
    small_grads = [dw_pool, d_scale, dg_ret, db_ret, dg_mem, dg_ffn, d_cb.reshape(1, FFN_HIDDEN), dg_final,
                   d_cw.transpose(1, 0, 2).reshape(3, FFN_HIDDEN), loss_part]
    d_proj = jnp.concatenate([d_hp, d_q, d_k, d_v, d_gr, d_qx, d_gl], axis=1)
    dW_in0, small_all = _mm_tn_slab("mm_dw_in0", h[:, :W_IN_FIRST_ROWS], d_proj, IN_SHARD, BF16,
                                    comm=_Exchange([], whole=small_grads))
    dW_in1, (in0,) = _mm_tn_slab("mm_dw_in1", h[:, W_IN_FIRST_ROWS:], d_proj, IN_SHARD, BF16,
                                 comm=_Exchange([dW_in0]))
    (grad_x, dg_mix), (in1,) = _mm_cols_slab_t("mm_in_t", d_proj, Win, (x2d, g_mix, dx1), comm=_Exchange([dW_in1]))
    received["w_in"] = [in0, in1]
    (g_mix_all,) = _comm_call("gather_g_mix", _Exchange([], whole=[dg_mix]))

    args = dict(g_mix=g_mix, w_in=w_in, w_pool=w_pool, pool_scale=pool_scale, w_a=w_a, g_ret=g_ret, b_ret=b_ret,
                w_r=w_r, g_mem=g_mem, w_mem_kv=w_mem_kv, w_c=w_c, w_out=w_out, g_ffn=g_ffn, w_up=w_up,
                conv_w=conv_w, conv_b=conv_b, w_down=w_down, g_final=g_final)
    m_in = dict(g_mix=m_g_mix, w_in=m_w_in, w_pool=m_w_pool, pool_scale=m_pool_scale, w_a=m_w_a, g_ret=m_g_ret,
                b_ret=m_b_ret, w_r=m_w_r, g_mem=m_g_mem, w_mem_kv=m_w_mem_kv, w_c=m_w_c, w_out=m_w_out,
                g_ffn=m_g_ffn, w_up=m_w_up, conv_w=m_conv_w, conv_b=m_conv_b, w_down=m_w_down, g_final=m_g_final)
    v_in = dict(g_mix=v_g_mix, w_in=v_w_in, w_pool=v_w_pool, pool_scale=v_pool_scale, w_a=v_w_a, g_ret=v_g_ret,
                b_ret=v_b_ret, w_r=v_w_r, g_mem=v_g_mem, w_mem_kv=v_w_mem_kv, w_c=v_w_c, w_out=v_w_out,
                g_ffn=v_g_ffn, w_up=v_w_up, conv_w=v_conv_w, conv_b=v_conv_b, w_down=v_w_down, g_final=v_g_final)

    grads, deltas, new_m, new_v = {}, {}, {}, {}
    for n in names:
        parts = received[n] if isinstance(received[n], list) else [received[n]]
        flip = (lambda a: a.T) if n == "w_up" else (lambda a: a)
        outs = _reduce_adam("adam_" + n, parts, big[n], flip(m_in[n][0]), flip(v_in[n][0]))
        for store, val in zip((grads, deltas, new_m, new_v), outs):
            store[n] = flip(val)[None]

    def as_small(a):
        return a.reshape(a.shape[-3:]) if a.ndim > 2 else a.reshape(1, -1)

    def small_update(call_name, param_names, gathered):
        params = [tuple(as_small(d[n]) for d in (args, m_in, v_in)) for n in param_names]
        sums, updates = _small_adam(call_name, gathered, params)
        for n, g, (d_, m_, v_) in zip(param_names, sums, updates):
            shape = args[n].shape
            grads[n], deltas[n], new_m[n], new_v[n] = (a.reshape(shape) for a in (g, d_, m_, v_))
        return sums[len(param_names):]

    g_cw_full, loss_row = small_update("adam_small", small_names, small_all)
    loss = loss_row[0, 0]
    small_update("adam_g_mix", ["g_mix"], [g_mix_all])

    shard_cols = FFN_HIDDEN // N_DEV
    g_cw = lax.dynamic_slice_in_dim(g_cw_full, me * shard_cols, shard_cols, axis=1)
    d_, m_, v_ = _adam_only("adam_conv_w", g_cw, conv_w[0], m_conv_w[0], v_conv_w[0])
    grads["conv_w"], deltas["conv_w"], new_m["conv_w"], new_v["conv_w"] = g_cw[None], d_[None], m_[None], v_[None]

    order = ["g_mix", "w_in", "w_pool", "pool_scale", "w_a", "g_ret", "b_ret", "w_r", "g_mem", "w_mem_kv", "w_c",
             "w_out", "g_ffn", "w_up", "conv_w", "conv_b", "w_down", "g_final"]
    return (loss, grad_x.reshape(B, S, D_MODEL), *[grads[n] for n in order], *[deltas[n] for n in order],
            *[new_m[n] for n in order], *[new_v[n] for n in order])
```

```python
import functools
import math

import jax
import jax.numpy as jnp
from jax import lax
from jax.experimental import pallas as pl
from jax.experimental.pallas import tpu as pltpu

F32 = jnp.float32
BF16 = jnp.bfloat16

N_DEV = 8
D_MODEL = 1024
POOL_WINDOWS = (2, 4, 8, 16)
POOL_GROUP_DIM = 128
POOL_WIDTH = 512
POOL_HALO = 16
RET_HEADS = 4
RET_QK_DIM = 128
RET_V_DIM = 256
RET_CHUNK = 128
ROPE_BASE = 10000.0
XA_HEADS = 4
XA_HEAD_DIM = 128
XA_WIDTH = 512
IN_WIDTH = 7168
IN_SHARD = IN_WIDTH // N_DEV
FFN_HIDDEN = 2816
UP_SHARD = 2 * FFN_HIDDEN // N_DEV
FFN_SLABS = FFN_HIDDEN // UP_SHARD
EPS = 1e-6
ADAM_LR = 0.001
ADAM_B1 = 0.9
ADAM_B2 = 0.999
ADAM_EPS = 1e-08
ADAM_WD = 0.01
ADAM_STEP = 10
GELU_C = math.sqrt(2.0 / math.pi)
GELU_A = 0.044715
VMEM_LIMIT = 56 * 1024 * 1024
MM_ROWS = 2048
MM_ROWS_RES = 1024
MM_TOKENS = 2048
W_IN_FIRST_ROWS = 384
MESH = pl.DeviceIdType.MESH

COL_Q, COL_K, COL_V, COL_GR, COL_QX, COL_GL = 512, 1024, 1536, 2560, 3584, 4096

_DIMS = {
    "nn": (((1,), (0,)), ((), ())),
    "nt": (((1,), (1,)), ((), ())),
    "tn": (((0,), (0,)), ((), ())),
}


def _dot(a, b, kind="nn"):
    return lax.dot_general(a.astype(BF16), b.astype(BF16), _DIMS[kind], preferred_element_type=F32)


def _params(sem, vmem=VMEM_LIMIT):
    return pltpu.CompilerParams(dimension_semantics=sem, vmem_limit_bytes=vmem)


def _tile(n, pref):
    t = min(n, pref)
    while n % t:
        t //= 2
    return t


def _mesh_pos():
    return lax.axis_index("x"), lax.axis_index("y"), lax.axis_index("c")


def _dev_index(x, y, c):
    return 4 * x + 2 * y + c


def _my_index():
    return _dev_index(*_mesh_pos())


def _remote(src, dst, send_sems, recv_sems, s, to):
    return pltpu.make_async_remote_copy(src_ref=src, dst_ref=dst, send_sem=send_sems.at[s], recv_sem=recv_sems.at[s],
                                        device_id=to, device_id_type=MESH)


class _Gather:
    def __init__(self, shards):
        self.inputs = list(shards)
        self.out_shapes = [jax.ShapeDtypeStruct((N_DEV,) + s.shape, s.dtype) for s in shards]
        n = len(shards)
        self.sem_shapes = [pltpu.SemaphoreType.DMA((7 * n,)), pltpu.SemaphoreType.DMA((7 * n,)),
                           pltpu.SemaphoreType.DMA((n,))]

    def _places(self):
        x, y, c = _mesh_pos()
        return (x, y, c), (x, y, 1 - c), [(1 - x, y), (x, 1 - y), (1 - x, 1 - y)]

    def _local(self, src, dst, sems):
        me = _my_index()
        return [pltpu.make_async_copy(src[w], dst[w].at[me], sems[2].at[w]) for w in range(len(src))]

    def start(self, src, dst, sems):
        me, sib, chips = self._places()
        for cp in self._local(src, dst, sems):
            cp.start()
        for w in range(len(src)):
            land = dst[w].at[_dev_index(*me)]
            _remote(src[w], land, sems[0], sems[1], 7 * w, sib).start()
            for j, chip in enumerate(chips):
                _remote(src[w], land, sems[0], sems[1], 7 * w + 1 + j, (*chip, me[2])).start()

    def middle(self, src, dst, sems):
        me, sib, chips = self._places()
        for j, chip in enumerate(chips):
            for w in range(len(src)):
                block = dst[w].at[_dev_index(*chip, me[2])]
                _remote(src[w], block, sems[0], sems[1], 7 * w + 1 + j, me).wait_recv()
                _remote(block, block, sems[0], sems[1], 7 * w + 4 + j, sib).start()

    def finish(self, src, dst, sems):
        me, sib, chips = self._places()
        n = len(src)
        for w in range(n):
            _remote(src[w], dst[w].at[_dev_index(*sib)], sems[0], sems[1], 7 * w, me).wait_recv()
            for j, chip in enumerate(chips):
                block = dst[w].at[_dev_index(*chip, sib[2])]
                _remote(block, block, sems[0], sems[1], 7 * w + 4 + j, me).wait_recv()
            for k in range(7):
                _remote(src[w], dst[w].at[0], sems[0], sems[1], 7 * w + k, me).wait_send()
        for cp in self._local(src, dst, sems):
            cp.wait()


class _Exchange:
    def __init__(self, partials, whole=()):
        self.n_part = len(partials)
        self.inputs = list(partials) + list(whole)
        self.out_shapes = [jax.ShapeDtypeStruct(p.shape, p.dtype) for p in partials]
        self.out_shapes += [jax.ShapeDtypeStruct((N_DEV,) + a.shape, a.dtype) for a in whole]
        n = len(self.inputs)
        self.sem_shapes = [pltpu.SemaphoreType.DMA((7 * n,)), pltpu.SemaphoreType.DMA((7 * n,)),
                           pltpu.SemaphoreType.DMA((n,))]

    def _peer(self, k):
        x, y, c = _mesh_pos()
        p = (x ^ ((k >> 2) & 1), y ^ ((k >> 1) & 1), c ^ (k & 1))
        return p, _dev_index(*p)

    def _source(self, src, w, slot):
        return src[w].at[slot] if w < self.n_part else src[w]

    def _local(self, src, dst, sems):
        me = _my_index()
        return [pltpu.make_async_copy(self._source(src, w, me), dst[w].at[me], sems[2].at[w])
                for w in range(len(src))]

    def start(self, src, dst, sems):
        me = _my_index()
        for cp in self._local(src, dst, sems):
            cp.start()
        for k in range(1, N_DEV):
            peer, peer_idx = self._peer(k)
            for w in range(len(src)):
                _remote(self._source(src, w, peer_idx), dst[w].at[me], sems[0], sems[1], 7 * w + k - 1, peer).start()

    def finish(self, src, dst, sems):
        for k in range(1, N_DEV):
            peer, peer_idx = self._peer(k)
            for w in range(len(src)):
                cp = _remote(self._source(src, w, peer_idx), dst[w].at[peer_idx], sems[0], sems[1], 7 * w + k - 1, peer)
                cp.wait_send()
                cp.wait_recv()
        for cp in self._local(src, dst, sems):
            cp.wait()


class _ExchangeTo:
    def __init__(self, partials, side):
        self.side = side
        self.inputs = list(partials)
        self.out_shapes = [jax.ShapeDtypeStruct(p.shape, p.dtype) for p in partials]
        n = len(partials)
        self.sem_shapes = [pltpu.SemaphoreType.DMA((7 * n,)), pltpu.SemaphoreType.DMA((7 * n,)),
                           pltpu.SemaphoreType.DMA((n,))]

    def _copies(self, src, dst, sems):
        x, y, c = _mesh_pos()
        me = _dev_index(x, y, c)
        receives = c == self.side
        remote = []
        for k in range(1, N_DEV):
            kx, ky, kc = (k >> 2) & 1, (k >> 1) & 1, k & 1
            peer = (x ^ kx, y ^ ky, c ^ kc)
            peer_idx = _dev_index(*peer)
            sends = c == (self.side ^ kc)
            for w in range(len(src)):
                slab = src[w].at[peer_idx]
                s = 7 * w + k - 1
                remote.append((sends, _remote(slab, dst[w].at[me], sems[0], sems[1], s, peer),
                               _remote(slab, dst[w].at[peer_idx], sems[0], sems[1], s, peer)))
        local = [pltpu.make_async_copy(src[w].at[me], dst[w].at[me], sems[2].at[w]) for w in range(len(src))]
        return receives, remote, local

    def start(self, src, dst, sems):
        receives, remote, local = self._copies(src, dst, sems)

        @pl.when(receives)
        def _():
            for cp in local:
                cp.start()

        for sends, send, _ in remote:
            pl.when(sends)(send.start)

    def finish(self, src, dst, sems):
        receives, remote, local = self._copies(src, dst, sems)
        for sends, send, arrive in remote:
            pl.when(sends)(send.wait_send)
            pl.when(receives)(arrive.wait_recv)

        @pl.when(receives)
        def _():
            for cp in local:
                cp.wait()


def _pcall(body, args, *, name, out_shape, grid, in_specs, out_specs, scratch_shapes=(), sem=None, comm=None):
    single = not isinstance(out_shape, (tuple, list))
    outs = [out_shape] if single else list(out_shape)
    ospecs = [out_specs] if single else list(out_specs)
    n_in, n_out, n_scr = len(args), len(outs), len(scratch_shapes)

    def pick(res):
        return res[0] if single else tuple(res[:n_out])

    if comm is None:
        res = pl.pallas_call(
            body, out_shape=outs, grid=grid, in_specs=list(in_specs), out_specs=ospecs,
            scratch_shapes=list(scratch_shapes), name=name, compiler_params=_params(sem),
        )(*args)
        return pick(res), ()

    nci, nco = len(comm.inputs), len(comm.out_shapes)

    def carrier(*refs):
        at = 0
        parts = []
        for size in (n_in, nci, n_out, nco, n_scr, len(comm.sem_shapes)):
            parts.append(refs[at:at + size])
            at += size
        ins, cins, o, couts, scr, sems = parts
        ids = [pl.program_id(a) for a in range(len(grid))]
        first = functools.reduce(jnp.logical_and, [i == 0 for i in ids])
        last = functools.reduce(jnp.logical_and, [i == g - 1 for i, g in zip(ids, grid)])

        body(*ins, *o, *scr)

        @pl.when(first)
        def _():
            comm.start(cins, couts, sems)

        if hasattr(comm, "middle"):
            steps = math.prod(grid)
            at = functools.reduce(lambda lin, ig: lin * ig[1] + ig[0], zip(ids, grid), 0)

            @pl.when(at == min(steps - 1, (3 * steps) // 4))
            def _():
                comm.middle(cins, couts, sems)

        @pl.when(last)
        def _():
            comm.finish(cins, couts, sems)

    hbm = pl.BlockSpec(memory_space=pltpu.HBM)
    res = pl.pallas_call(
        carrier, out_shape=outs + comm.out_shapes, grid=grid, in_specs=list(in_specs) + [hbm] * nci,
        out_specs=ospecs + [hbm] * nco, scratch_shapes=list(scratch_shapes) + comm.sem_shapes, name=name,
        compiler_params=_params(("arbitrary",) * len(grid)),
    )(*args, *comm.inputs)
    return pick(res), tuple(res[n_out:])


def _comm_call(name, comm):
    def body(*refs):
        nci, nco = len(comm.inputs), len(comm.out_shapes)
        cins, couts, sems = refs[:nci], refs[nci:nci + nco], refs[nci + nco:]
        comm.start(cins, couts, sems)
        if hasattr(comm, "middle"):
            comm.middle(cins, couts, sems)
        comm.finish(cins, couts, sems)

    hbm = pl.BlockSpec(memory_space=pltpu.HBM)
    return pl.pallas_call(
        body, out_shape=comm.out_shapes, in_specs=[hbm] * len(comm.inputs), out_specs=[hbm] * len(comm.out_shapes),
        scratch_shapes=comm.sem_shapes, name=name,
    )(*comm.inputs)


def _matmul(name, kind, a, b, out_shape, grid, a_spec, b_spec, o_spec, acc_shape, res=None, res_spec=None,
            comm=None, epilogue=None):
    nk = grid[-1]
    if epilogue is None:
        extra, extra_specs = ([res], [res_spec]) if res is not None else ([], [])
        n_out = 1
    else:
        extra, extra_specs, n_out = list(res), list(res_spec), len(out_shape)
    n_in = 2 + len(extra)

    def body(*refs):
        a_ref, b_ref = refs[0], refs[1]
        extra_refs, out_refs = refs[2:n_in], refs[n_in:n_in + n_out]

        def prod():
            return _dot(a_ref[...], b_ref[...], kind)

        def finish(acc):
            if epilogue is not None:
                ids = [pl.program_id(ax) for ax in range(len(grid) - 1)]
                first = functools.reduce(jnp.logical_and, [i == 0 for i in ids]) if ids else True
                epilogue(acc, extra_refs, out_refs, first)
                return
            if extra_refs:
                acc = acc + extra_refs[0][...]
            out_refs[0][...] = acc.astype(out_refs[0].dtype)

        if nk == 1:
            finish(prod())
        else:
            acc_ref = refs[n_in + n_out]
            k = pl.program_id(len(grid) - 1)

            @pl.when(k == 0)
            def _():
                acc_ref[...] = prod()

            @pl.when(k > 0)
            def _():
                acc_ref[...] += prod()

            @pl.when(k == nk - 1)
            def _():
                finish(acc_ref[...])

    in_specs = [a_spec, b_spec] + extra_specs
    args = (a, b, *extra)
    scratch = [pltpu.VMEM(acc_shape, F32)] if nk > 1 else []
    sem = ("arbitrary",) * len(grid) if epilogue is not None else ("parallel",) * (len(grid) - 1) + ("arbitrary",)
    out, landed = _pcall(body, args, name=name, out_shape=out_shape, grid=grid, in_specs=in_specs,
                         out_specs=o_spec, scratch_shapes=scratch, sem=sem, comm=comm)
    return out if comm is None else (out, landed)


def _mm_rows(name, a, w, out_dtype=F32, res=None, kind="nn", tm=MM_ROWS, comm=None):
    M, K = a.shape
    N = w.shape[1] if kind == "nn" else w.shape[0]
    tm = _tile(M, tm)
    res_spec = pl.BlockSpec((tm, N), lambda i, k: (i, 0)) if res is not None else None
    return _matmul(
        name, kind, a, w, jax.ShapeDtypeStruct((M, N), out_dtype), (M // tm, 1),
        pl.BlockSpec((tm, K), lambda i, k: (i, 0)), pl.BlockSpec(w.shape, lambda i, k: (0, 0)),
        pl.BlockSpec((tm, N), lambda i, k: (i, 0)), (tm, N), res, res_spec, comm)


def _residual_rms_epilogue(y, operands, outputs, first):
    x_ref, g_ref = operands
    x1_ref, h_ref = outputs
    xv = x_ref[...] + y
    x1_ref[...] = xv
    r = lax.rsqrt(jnp.mean(xv * xv, axis=-1, keepdims=True) + EPS)
    h_ref[...] = (xv * r * g_ref[...]).astype(h_ref.dtype)


def _mm_residual_rms(name, a, w, x, g, tm=MM_ROWS_RES):
    M, K = a.shape
    N = w.shape[1]
    tm = _tile(M, tm)
    row = pl.BlockSpec((tm, N), lambda i, k: (i, 0))
    return _matmul(
        name, "nn", a, w, (jax.ShapeDtypeStruct((M, N), F32), jax.ShapeDtypeStruct((M, N), BF16)), (M // tm, 1),
        pl.BlockSpec((tm, K), lambda i, k: (i, 0)), pl.BlockSpec(w.shape, lambda i, k: (0, 0)),
        (row, row), (tm, N), [x, g], [row, pl.BlockSpec((1, N), lambda i, k: (0, 0))],
        epilogue=_residual_rms_epilogue)


def _mm_tn(name, a, b, out_dtype=F32, tk=MM_TOKENS, comm=None):
    T, M = a.shape
    N = b.shape[1]
    tk = _tile(T, tk)
    return _matmul(
        name, "tn", a, b, jax.ShapeDtypeStruct((M, N), out_dtype), (1, T // tk),
        pl.BlockSpec((tk, M), lambda i, k: (k, 0)), pl.BlockSpec((tk, N), lambda i, k: (k, 0)),
        pl.BlockSpec((M, N), lambda i, k: (0, 0)), (M, N), comm=comm)


def _mm_in_gather(h, shard, tm=MM_ROWS, comm=None):
    T, K = h.shape
    n = shard.shape[1]
    tm = _tile(T, tm)
    n_tiles = T // tm
    pair_of_chip_step = {4: 1, 2: 2, 6: 3}

    def slab_of(s):
        x, y, c = _mesh_pos()
        return _dev_index(x ^ ((s >> 2) & 1), y ^ ((s >> 1) & 1), c ^ (s & 1))

    def body(h_ref, shard_ref, proj_ref, win_ref, wbuf, slot_sems, send_sems, recv_sems, local_sem):
        s, i = pl.program_id(0), pl.program_id(1)
        x, y, c = _mesh_pos()
        me, sib = (x, y, c), (x, y, 1 - c)

        def slot_copy(step):
            src = shard_ref if step == 0 else win_ref.at[slab_of(step)]
            return pltpu.make_async_copy(src, wbuf.at[step % 2], slot_sems.at[step % 2])

        def fetch(step):
            if step >= 1:
                block = win_ref.at[slab_of(step)]
                if step == 1:
                    pair = 0
                elif step % 2 == 0:
                    pair = pair_of_chip_step[step]
                else:
                    pair = 3 + pair_of_chip_step[step - 1]
                _remote(block, block, send_sems, recv_sems, pair, me).wait_recv()
                if step % 2 == 0:
                    _remote(block, block, send_sems, recv_sems, 3 + pair, sib).start()
            slot_copy(step).start()

        @pl.when((s == 0) & (i == 0))
        def _():
            land = win_ref.at[_dev_index(*me)]
            pltpu.make_async_copy(shard_ref, land, local_sem).start()
            _remote(shard_ref, land, send_sems, recv_sems, 0, sib).start()
            for step, pair in pair_of_chip_step.items():
                peer = (x ^ ((step >> 2) & 1), y ^ ((step >> 1) & 1), c)
                _remote(shard_ref, land, send_sems, recv_sems, pair, peer).start()
            fetch(0)

        for step in range(N_DEV):
            @pl.when((s == step) & (i == 0))
            def _():
                slot_copy(step).wait()

            if step + 1 < N_DEV:
                @pl.when((s == step) & (i == n_tiles - 1))
                def _():
                    fetch(step + 1)

        proj_ref[...] = _dot(h_ref[...], wbuf[s % 2])

        @pl.when((s == N_DEV - 1) & (i == n_tiles - 1))
        def _():
            for pair in range(7):
                _remote(shard_ref, win_ref.at[0], send_sems, recv_sems, pair, me).wait_send()
            pltpu.make_async_copy(shard_ref, win_ref.at[_dev_index(*me)], local_sem).wait()

    hbm = pl.BlockSpec(memory_space=pltpu.HBM)
    return _pcall(
        body, (h, shard), name="mm_in",
        out_shape=(jax.ShapeDtypeStruct((T, N_DEV * n), F32), jax.ShapeDtypeStruct((N_DEV, K, n), shard.dtype)),
        grid=(N_DEV, n_tiles), in_specs=[pl.BlockSpec((tm, K), lambda s, i: (i, 0)), hbm],
        out_specs=(pl.BlockSpec((tm, n), lambda s, i: (i, slab_of(s))), hbm),
        scratch_shapes=[pltpu.VMEM((2, K, n), shard.dtype), pltpu.SemaphoreType.DMA((2,)),
                        pltpu.SemaphoreType.DMA((7,)), pltpu.SemaphoreType.DMA((7,)), pltpu.SemaphoreType.DMA],
        sem=("arbitrary", "arbitrary"), comm=comm)


def _rms_bwd_epilogue(dh, operands, outputs, first):
    x_ref, g_ref, dres_ref = operands
    dx_ref, dg_ref = outputs
    xv = x_ref[...]
    r = lax.rsqrt(jnp.mean(xv * xv, axis=-1, keepdims=True) + EPS)
    xhat = xv * r

    @pl.when(first)
    def _():
        dg_ref[...] = jnp.zeros_like(dg_ref)

    dg_ref[...] += jnp.sum(dh * xhat, axis=0, keepdims=True)
    dxhat = dh * g_ref[...]
    dx_ref[...] = dres_ref[...] + r * (dxhat - xhat * jnp.mean(dxhat * xhat, axis=-1, keepdims=True))


def _rms_bwd_fused(M, K, tm, rms):
    row = pl.BlockSpec((tm, K), lambda i, j: (i, 0))
    vec = pl.BlockSpec((1, K), lambda i, j: (0, 0))
    x, g, dres = rms
    return dict(res=[x, g, dres], res_spec=[row, vec, row], epilogue=_rms_bwd_epilogue,
                out_shape=(jax.ShapeDtypeStruct((M, K), F32), jax.ShapeDtypeStruct((1, K), F32)), o_spec=(row, vec))


def _mm_cols_slab_t(name, a, w_slabs, rms, tm=MM_ROWS_RES, comm=None):
    M = a.shape[0]
    J, K, n = w_slabs.shape
    tm = _tile(M, tm)
    fused = _rms_bwd_fused(M, K, tm, rms)
    return _matmul(
        name, "nt", a, w_slabs, fused.pop("out_shape"), (M // tm, J),
        pl.BlockSpec((tm, n), lambda i, j: (i, j)), pl.BlockSpec((None, K, n), lambda i, j: (j, 0, 0)),
        fused.pop("o_spec"), (tm, K), comm=comm, **fused)


def _mm_tn_slab(name, a, b, n, out_dtype=F32, tk=MM_TOKENS, comm=None, part=(0, 1)):
    T, M = a.shape
    p, of = part
    M //= of
    J = b.shape[1] // n
    tk = _tile(T, tk)
    return _matmul(
        name, "tn", a, b, jax.ShapeDtypeStruct((J, M, n), out_dtype), (J, T // tk),
        pl.BlockSpec((tk, M), lambda j, k: (k, p)), pl.BlockSpec((tk, n), lambda j, k: (k, j)),
        pl.BlockSpec((None, M, n), lambda j, k: (j, 0, 0)), (M, n), comm=comm)


def _rms_fwd(name, x, g, tm=512):
    T, Dm = x.shape
    tm = _tile(T, tm)

    def body(x_ref, g_ref, h_ref):
        xv = x_ref[...]
        r = lax.rsqrt(jnp.mean(xv * xv, axis=-1, keepdims=True) + EPS)
        h_ref[...] = (xv * r * g_ref[...]).astype(h_ref.dtype)

    return pl.pallas_call(
        body, out_shape=jax.ShapeDtypeStruct((T, Dm), BF16), grid=(T // tm,),
        in_specs=[pl.BlockSpec((tm, Dm), lambda i: (i, 0)), pl.BlockSpec((1, Dm), lambda i: (0, 0))],
        out_specs=pl.BlockSpec((tm, Dm), lambda i: (i, 0)), name=name, compiler_params=_params(("parallel",)),
    )(x, g)


def _rms_bwd(name, x, g, dh, dres, tm=512):
    T, Dm = x.shape
    tm = _tile(T, tm)
    want_dx = dres is not None

    def body(*refs):
        if want_dx:
            x_ref, g_ref, dh_ref, dres_ref, dx_ref, dg_ref = refs
        else:
            x_ref, g_ref, dh_ref, dg_ref = refs
        xv = x_ref[...]
        r = lax.rsqrt(jnp.mean(xv * xv, axis=-1, keepdims=True) + EPS)
        xhat = xv * r
        dhv = dh_ref[...]

        @pl.when(pl.program_id(0) == 0)
        def _():
            dg_ref[...] = jnp.zeros_like(dg_ref)

        dg_ref[...] += jnp.sum(dhv * xhat, axis=0, keepdims=True)
        if want_dx:
            dxhat = dhv * g_ref[...]
            dx_ref[...] = dres_ref[...] + r * (dxhat - xhat * jnp.mean(dxhat * xhat, axis=-1, keepdims=True))

    row = pl.BlockSpec((tm, Dm), lambda i: (i, 0))
    vec = pl.BlockSpec((1, Dm), lambda i: (0, 0))
    if want_dx:
        return pl.pallas_call(
            body, out_shape=(jax.ShapeDtypeStruct((T, Dm), F32), jax.ShapeDtypeStruct((1, Dm), F32)),
            grid=(T // tm,), in_specs=[row, vec, row, row], out_specs=(row, vec), name=name,
            compiler_params=_params(("arbitrary",)),
        )(x, g, dh, dres)
    return pl.pallas_call(
        body, out_shape=jax.ShapeDtypeStruct((1, Dm), F32), grid=(T // tm,), in_specs=[row, vec, row],
        out_specs=vec, name=name, compiler_params=_params(("arbitrary",)),
    )(x, g, dh)


def _pool_rows(S):
    return _tile(S, 256)


def _pool_count(c0, rows, w):
    t = c0 + lax.broadcasted_iota(jnp.int32, (rows, 1), 0)
    return jnp.minimum(t + 1, w).astype(F32)


def _pool_fwd(proj, w_pool, scale, B, S):
    CH = _pool_rows(S)

    def body(hp_ref, wp_ref, sc_ref, o_ref, pad_ref):
        pad_ref[0:POOL_HALO, :] = jnp.zeros((POOL_HALO, POOL_WIDTH), F32)
        pad_ref[POOL_HALO:, :] = hp_ref[...]
        for gi, w in enumerate(POOL_WINDOWS):
            cols = slice(gi * POOL_GROUP_DIM, (gi + 1) * POOL_GROUP_DIM)
            for c in range(S // CH):
                base = POOL_HALO + c * CH
                acc = pad_ref[base:base + CH, cols]
                tok = acc
                for j in range(1, w):
                    acc = acc + pad_ref[base - j:base - j + CH, cols]
                pooled = acc / _pool_count(c * CH, CH, w) - tok
                z = _dot(pooled, wp_ref[gi])
                o_ref[c * CH:(c + 1) * CH, cols] = (z * sc_ref[:, cols]).astype(o_ref.dtype)

    return pl.pallas_call(
        body, out_shape=jax.ShapeDtypeStruct((B * S, POOL_WIDTH), BF16), grid=(B,),
        in_specs=[pl.BlockSpec((S, POOL_WIDTH), lambda b: (b, 0)),
                  pl.BlockSpec(w_pool.shape, lambda b: (0, 0, 0)),
                  pl.BlockSpec((1, POOL_WIDTH), lambda b: (0, 0))],
        out_specs=pl.BlockSpec((S, POOL_WIDTH), lambda b: (b, 0)),
        scratch_shapes=[pltpu.VMEM((S + POOL_HALO, POOL_WIDTH), F32)],
        name="pool_fwd", compiler_params=_params(("parallel",)),
    )(proj, w_pool, scale)


def _pool_bwd(proj, d_ypre, w_pool, scale, B, S):
    CH = _pool_rows(S)

    def body(hp_ref, dy_ref, wp_ref, sc_ref, dhp_ref, dwp_ref, dsc_ref, pad_ref, sc_pad_ref, dp_ref):
        @pl.when(pl.program_id(0) == 0)
        def _():
            dwp_ref[...] = jnp.zeros_like(dwp_ref)
            dsc_ref[...] = jnp.zeros_like(dsc_ref)

        pad_ref[0:POOL_HALO, :] = jnp.zeros((POOL_HALO, POOL_WIDTH), F32)
        pad_ref[POOL_HALO:, :] = hp_ref[...]
        sc_pad_ref[S:, :] = jnp.zeros((POOL_HALO, POOL_WIDTH), F32)
        for gi, w in enumerate(POOL_WINDOWS):
            cols = slice(gi * POOL_GROUP_DIM, (gi + 1) * POOL_GROUP_DIM)
            for c in range(S // CH):
                base = POOL_HALO + c * CH
                rows = slice(c * CH, (c + 1) * CH)
                acc = pad_ref[base:base + CH, cols]
                tok = acc
                for j in range(1, w):
                    acc = acc + pad_ref[base - j:base - j + CH, cols]
                cnt = _pool_count(c * CH, CH, w)
                pooled = acc / cnt - tok
                z = _dot(pooled, wp_ref[gi])
                dy = dy_ref[rows, cols]
                dsc_ref[:, cols] += jnp.sum(dy * z, axis=0, keepdims=True)
                dz = dy * sc_ref[:, cols]
                dwp_ref[gi] += _dot(pooled, dz, "tn")
                dpool = _dot(dz, wp_ref[gi], "nt")
                dp_ref[rows, cols] = dpool
                sc_pad_ref[rows, cols] = dpool / cnt
            for c in range(S // CH):
                rows = slice(c * CH, (c + 1) * CH)
                acc = sc_pad_ref[rows, cols]
                for j in range(1, w):
                    acc = acc + sc_pad_ref[c * CH + j:c * CH + j + CH, cols]
                dhp_ref[rows, cols] = (acc - dp_ref[rows, cols]).astype(dhp_ref.dtype)

    seq = pl.BlockSpec((S, POOL_WIDTH), lambda b: (b, 0))
    return pl.pallas_call(
        body,
        out_shape=(jax.ShapeDtypeStruct((B * S, POOL_WIDTH), BF16),
                   jax.ShapeDtypeStruct(w_pool.shape, F32), jax.ShapeDtypeStruct((1, POOL_WIDTH), F32)),
        grid=(B,),
        in_specs=[seq, seq, pl.BlockSpec(w_pool.shape, lambda b: (0, 0, 0)),
                  pl.BlockSpec((1, POOL_WIDTH), lambda b: (0, 0))],
        out_specs=(seq, pl.BlockSpec(w_pool.shape, lambda b: (0, 0, 0)),
                   pl.BlockSpec((1, POOL_WIDTH), lambda b: (0, 0))),
        scratch_shapes=[pltpu.VMEM((S + POOL_HALO, POOL_WIDTH), F32),
                        pltpu.VMEM((S + POOL_HALO, POOL_WIDTH), F32),
                        pltpu.VMEM((S, POOL_WIDTH), F32)],
        name="pool_bwd", compiler_params=_params(("arbitrary",)),
    )(proj, d_ypre, w_pool, scale)


def _ret_tables(S):
    half = RET_QK_DIM // 2
    inv = ROPE_BASE ** (-jnp.arange(half, dtype=F32) / half)
    ang = jnp.arange(S, dtype=F32)[:, None] * inv[None, :]
    cos, sin = jnp.cos(ang), jnp.sin(ang)
    cos_full = jnp.concatenate([cos, cos], axis=-1)
    sin_signed = jnp.concatenate([-sin, sin], axis=-1)
    C = RET_CHUNK
    lg = jnp.log1p(-jnp.exp2(-5.0 - jnp.arange(RET_HEADS, dtype=F32)))[:, None, None]
    idx = jnp.arange(C, dtype=F32)
    rel = idx[:, None] - idx[None, :]
    decay = jnp.where(rel >= 0, jnp.exp(jnp.maximum(rel, 0.0) * lg), 0.0)
    q_decay = jnp.broadcast_to(jnp.exp((idx + 1.0)[None, :, None] * lg), (RET_HEADS, C, RET_QK_DIM))
    k_decay = jnp.broadcast_to(jnp.exp((C - 1.0 - idx)[None, :, None] * lg), (RET_HEADS, C, RET_QK_DIM))
    c_decay = jnp.broadcast_to(jnp.exp(C * lg), (RET_HEADS, 1, RET_V_DIM))
    return cos_full, sin_signed, decay, q_decay, k_decay, c_decay


def _rope(x, cos_full, sin_signed):
    return x * cos_full + pltpu.roll(x, RET_QK_DIM // 2, axis=1) * sin_signed


def _rope_t(dy, cos_full, sin_signed):
    return dy * cos_full + pltpu.roll(dy * sin_signed, RET_QK_DIM // 2, axis=1)


RET_COLS = 512


def _ret_specs(N, chunk_of):
    C = RET_CHUNK

    def rows(width, col=0):
        return pl.BlockSpec((C, width), lambda b, i: (b * N + chunk_of(i), col))

    def whole(shape):
        return pl.BlockSpec(shape, lambda b, i: (0,) * len(shape))

    wide = RET_HEADS * RET_V_DIM
    return dict(
        q=rows(RET_COLS, COL_Q // RET_COLS), k=rows(RET_COLS, COL_K // RET_COLS),
        v=[rows(RET_COLS, COL_V // RET_COLS + j) for j in range(2)],
        gr=[rows(RET_COLS, COL_GR // RET_COLS + j) for j in range(2)],
        table=pl.BlockSpec((C, RET_QK_DIM), lambda b, i: (chunk_of(i), 0)),
        decay=whole((RET_HEADS, C, C)), qd=whole((RET_HEADS, C, RET_QK_DIM)), kd=whole((RET_HEADS, C, RET_QK_DIM)),
        cd=whole((RET_HEADS, 1, RET_V_DIM)), vec=whole((1, wide)), qk_rows=rows(RET_COLS), v_rows=rows(wide),
        state=pl.BlockSpec((None, None, RET_HEADS, RET_QK_DIM, RET_V_DIM), lambda b, i: (b, chunk_of(i), 0, 0, 0)))


def _head_cols(h):
    pair = slice((h % 2) * RET_V_DIM, (h % 2 + 1) * RET_V_DIM)
    return slice(h * RET_QK_DIM, (h + 1) * RET_QK_DIM), h // 2, pair, slice(h * RET_V_DIM, (h + 1) * RET_V_DIM)


def _group_norm(o):
    mu = jnp.mean(o, axis=-1, keepdims=True)
    oc = o - mu
    rstd = lax.rsqrt(jnp.mean(oc * oc, axis=-1, keepdims=True) + EPS)
    return oc * rstd, rstd


def _ret_fwd(proj, g_ret, b_ret, tables, B, S, comm=None):
    N = S // RET_CHUNK
    cos_t, sin_t, decay, q_decay, k_decay, c_decay = tables
    sp = _ret_specs(N, lambda i: i)

    def body(q_ref, k_ref, v0_ref, v1_ref, gr0_ref, gr1_ref, cos_ref, sin_ref, dec_ref, qd_ref, kd_ref, cd_ref,
             g_ref, b_ref, y_ref, rs_ref, r_ref):
        @pl.when(pl.program_id(1) == 0)
        def _():
            r_ref[...] = jnp.zeros_like(r_ref)

        cs, sn = cos_ref[...], sin_ref[...]
        for h in range(RET_HEADS):
            qk, j, pair, wide = _head_cols(h)
            q = _rope(q_ref[:, qk], cs, sn)
            k = _rope(k_ref[:, qk], cs, sn) * (RET_QK_DIM ** -0.5)
            v = (v0_ref, v1_ref)[j][:, pair]
            R = r_ref[h]
            rs_ref[h] = R
            s = _dot(q, k, "nt") * dec_ref[h]
            o = _dot(s, v) + _dot(q * qd_ref[h], R)
            r_ref[h] = cd_ref[h] * R + _dot(k * kd_ref[h], v, "tn")
            on, _ = _group_norm(o)
            gr = (gr0_ref, gr1_ref)[j][:, pair]
            y_ref[:, wide] = (gr * jax.nn.sigmoid(gr) * (on * g_ref[:, wide] + b_ref[:, wide])).astype(y_ref.dtype)

    state = jax.ShapeDtypeStruct((B, N, RET_HEADS, RET_QK_DIM, RET_V_DIM), F32)
    return _pcall(
        body, (proj,) * 6 + (cos_t, sin_t, decay, q_decay, k_decay, c_decay, g_ret, b_ret),
        name="ret_fwd", out_shape=(jax.ShapeDtypeStruct((B * S, RET_HEADS * RET_V_DIM), BF16), state), grid=(B, N),
        in_specs=[sp["q"], sp["k"], *sp["v"], *sp["gr"], sp["table"], sp["table"], sp["decay"], sp["qd"],
                  sp["kd"], sp["cd"], sp["vec"], sp["vec"]],
        out_specs=(sp["v_rows"], sp["state"]),
        scratch_shapes=[pltpu.VMEM((RET_HEADS, RET_QK_DIM, RET_V_DIM), F32)],
        sem=("parallel", "arbitrary"), comm=comm)


def _ret_bwd(proj, states, d_yr, g_ret, b_ret, tables, B, S, comm=None):
    N = S // RET_CHUNK
    cos_t, sin_t, decay, q_decay, k_decay, c_decay = tables
    sp = _ret_specs(N, lambda i: N - 1 - i)
    qk_scale = RET_QK_DIM ** -0.5

    def body(q_ref, k_ref, v0_ref, v1_ref, gr0_ref, gr1_ref, dy_ref, rs_ref, cos_ref, sin_ref, dec_ref, qd_ref,
             kd_ref, cd_ref, g_ref, b_ref, dq_ref, dk_ref, dv_ref, dgr_ref, dg_ref, db_ref, dr_ref):
        @pl.when((pl.program_id(0) == 0) & (pl.program_id(1) == 0))
        def _():
            dg_ref[...] = jnp.zeros_like(dg_ref)
            db_ref[...] = jnp.zeros_like(db_ref)

        @pl.when(pl.program_id(1) == 0)
        def _():
            dr_ref[...] = jnp.zeros_like(dr_ref)

        cs, sn = cos_ref[...], sin_ref[...]
        for h in range(RET_HEADS):
            qk, j, pair, wide = _head_cols(h)
            q = _rope(q_ref[:, qk], cs, sn)
            k = _rope(k_ref[:, qk], cs, sn) * qk_scale
            v = (v0_ref, v1_ref)[j][:, pair]
            R, dR = rs_ref[h], dr_ref[h]
            dec, qd, kd = dec_ref[h], qd_ref[h], kd_ref[h]
            s = _dot(q, k, "nt") * dec
            o = _dot(s, v) + _dot(q * qd, R)
            on, rstd = _group_norm(o)
            g = g_ref[:, wide]
            oaff = on * g + b_ref[:, wide]
            gr = (gr0_ref, gr1_ref)[j][:, pair]
            sg = jax.nn.sigmoid(gr)
            dy = dy_ref[:, wide]
            dgr_ref[:, wide] = (dy * oaff * (sg * (1.0 + gr * (1.0 - sg)))).astype(dgr_ref.dtype)
            doaff = dy * (gr * sg)
            dg_ref[:, wide] += jnp.sum(doaff * on, axis=0, keepdims=True)
            db_ref[:, wide] += jnp.sum(doaff, axis=0, keepdims=True)
            don = doaff * g
            do = rstd * (don - jnp.mean(don, axis=-1, keepdims=True)
                         - on * jnp.mean(don * on, axis=-1, keepdims=True))
            ds = _dot(do, v, "nt") * dec
            dq = _dot(ds, k) + qd * _dot(do, R, "nt")
            dk = _dot(ds, q, "tn") + kd * _dot(v, dR, "nt")
            dv_ref[:, wide] = (_dot(s, do, "tn") + _dot(k * kd, dR)).astype(dv_ref.dtype)
            dr_ref[h] = cd_ref[h] * dR + _dot(q * qd, do, "tn")
            dq_ref[:, qk] = _rope_t(dq, cs, sn).astype(dq_ref.dtype)
            dk_ref[:, qk] = _rope_t(dk * qk_scale, cs, sn).astype(dk_ref.dtype)

    T = B * S
    qk_shape = jax.ShapeDtypeStruct((T, RET_HEADS * RET_QK_DIM), BF16)
    v_shape = jax.ShapeDtypeStruct((T, RET_HEADS * RET_V_DIM), BF16)
    vec_shape = jax.ShapeDtypeStruct((1, RET_HEADS * RET_V_DIM), F32)
    return _pcall(
        body, (proj,) * 6 + (d_yr, states, cos_t, sin_t, decay, q_decay, k_decay, c_decay, g_ret, b_ret),
        name="ret_bwd", out_shape=(qk_shape, qk_shape, v_shape, v_shape, vec_shape, vec_shape), grid=(B, N),
        in_specs=[sp["q"], sp["k"], *sp["v"], *sp["gr"], sp["v_rows"], sp["state"], sp["table"], sp["table"],
                  sp["decay"], sp["qd"], sp["kd"], sp["cd"], sp["vec"], sp["vec"]],
        out_specs=(sp["qk_rows"], sp["qk_rows"], sp["v_rows"], sp["v_rows"], sp["vec"], sp["vec"]),
        scratch_shapes=[pltpu.VMEM((RET_HEADS, RET_QK_DIM, RET_V_DIM), F32)],
        sem=("arbitrary", "arbitrary"), comm=comm)


def _xa_rows(S):
    return _tile(S, 256)


def _xa_specs(S, M):
    q = pl.BlockSpec((S, XA_HEAD_DIM), lambda b, h: (b, COL_QX // XA_HEAD_DIM + h))
    k = pl.BlockSpec((M, XA_HEAD_DIM), lambda b, h: (b, h))
    v = pl.BlockSpec((M, XA_HEAD_DIM), lambda b, h: (b, XA_HEADS + h))
    o = pl.BlockSpec((S, XA_HEAD_DIM), lambda b, h: (b, h))
    return q, k, v, o


def _softmax_rows(s):
    e = jnp.exp(s - jnp.max(s, axis=-1, keepdims=True))
    return e / jnp.sum(e, axis=-1, keepdims=True)


def _xa_fwd(proj, kv, B, S, M, comm=None):
    CH = _xa_rows(S)
    q_spec, k_spec, v_spec, o_spec = _xa_specs(S, M)

    def body(q_ref, k_ref, v_ref, o_ref):
        def chunk(i, carry):
            rows = pl.ds(pl.multiple_of(i * CH, CH), CH)
            p = _softmax_rows(_dot(q_ref[rows, :], k_ref[...], "nt") * (XA_HEAD_DIM ** -0.5))
            o_ref[rows, :] = _dot(p, v_ref[...]).astype(o_ref.dtype)
            return carry

        lax.fori_loop(0, S // CH, chunk, 0, unroll=True)

    return _pcall(
        body, (proj, kv, kv), name="xattn_fwd", out_shape=jax.ShapeDtypeStruct((B * S, XA_WIDTH), BF16),
        grid=(B, XA_HEADS), in_specs=[q_spec, k_spec, v_spec], out_specs=o_spec,
        sem=("parallel", "parallel"), comm=comm)


def _xa_bwd(proj, kv, d_o, B, S, M, comm=None):
    CH = _xa_rows(S)
    q_spec, k_spec, v_spec, o_spec = _xa_specs(S, M)
    scale = XA_HEAD_DIM ** -0.5

    def body(q_ref, k_ref, v_ref, do_ref, dq_ref, dk_ref, dv_ref):
        dk_ref[...] = jnp.zeros_like(dk_ref)
        dv_ref[...] = jnp.zeros_like(dv_ref)

        def chunk(i, carry):
            rows = pl.ds(pl.multiple_of(i * CH, CH), CH)
            q, do = q_ref[rows, :], do_ref[rows, :]
            p = _softmax_rows(_dot(q, k_ref[...], "nt") * scale)
            dp = _dot(do, v_ref[...], "nt")
            ds = p * (dp - jnp.sum(dp * p, axis=-1, keepdims=True)) * scale
            dq_ref[rows, :] = _dot(ds, k_ref[...]).astype(dq_ref.dtype)
            dk_ref[...] += _dot(ds, q, "tn")
            dv_ref[...] += _dot(p, do, "tn")
            return carry

        lax.fori_loop(0, S // CH, chunk, 0, unroll=True)

    kv_out = pl.BlockSpec((M, XA_HEAD_DIM), lambda b, h: (b, h))
    return _pcall(
        body, (proj, kv, kv, d_o), name="xattn_bwd",
        out_shape=(jax.ShapeDtypeStruct((B * S, XA_WIDTH), BF16), jax.ShapeDtypeStruct((B * M, XA_WIDTH), F32),
                   jax.ShapeDtypeStruct((B * M, XA_WIDTH), F32)),
        grid=(B, XA_HEADS), in_specs=[q_spec, k_spec, v_spec, o_spec], out_specs=(o_spec, kv_out, kv_out),
        sem=("parallel", "parallel"), comm=comm)


def _gate_specs(tm):
    n = COL_GL // D_MODEL
    return [pl.BlockSpec((tm, D_MODEL), lambda i, j=j: (i, n + j)) for j in range(3)]


def _merge_fwd(proj, ys, tm=512, comm=None):
    T = proj.shape[0]
    tm = _tile(T, tm)
    row = pl.BlockSpec((tm, D_MODEL), lambda i: (i, 0))

    def body(g0, g1, g2, y0, y1, y2, o_ref):
        acc = jax.nn.sigmoid(g0[...]) * y0[...]
        acc = acc + jax.nn.sigmoid(g1[...]) * y1[...]
        acc = acc + jax.nn.sigmoid(g2[...]) * y2[...]
        o_ref[...] = acc.astype(o_ref.dtype)

    return _pcall(
        body, (proj, proj, proj, *ys), name="merge_fwd", out_shape=jax.ShapeDtypeStruct((T, D_MODEL), BF16),
        grid=(T // tm,), in_specs=_gate_specs(tm) + [row] * 3, out_specs=row, sem=("parallel",), comm=comm)


def _merge_bwd(proj, ys, d_merged, tm=512, comm=None):
    T = proj.shape[0]
    tm = _tile(T, tm)
    row = pl.BlockSpec((tm, D_MODEL), lambda i: (i, 0))

    def body(g0, g1, g2, y0, y1, y2, dm_ref, dgl_ref, d0, d1, d2):
        dm = dm_ref[...]
        for j, (g_ref, y_ref, d_ref) in enumerate(((g0, y0, d0), (g1, y1, d1), (g2, y2, d2))):
            sg = jax.nn.sigmoid(g_ref[...])
            d_ref[...] = (dm * sg).astype(d_ref.dtype)
            dgl_ref[:, j * D_MODEL:(j + 1) * D_MODEL] = (dm * y_ref[...] * sg * (1.0 - sg)).astype(dgl_ref.dtype)

    dy = jax.ShapeDtypeStruct((T, D_MODEL), BF16)
    return _pcall(
        body, (proj, proj, proj, *ys, d_merged), name="merge_bwd",
        out_shape=(jax.ShapeDtypeStruct((T, 3 * D_MODEL), BF16), dy, dy, dy), grid=(T // tm,),
        in_specs=_gate_specs(tm) + [row] * 4,
        out_specs=(pl.BlockSpec((tm, 3 * D_MODEL), lambda i: (i, 0)), row, row, row),
        sem=("parallel",), comm=comm)


def _gelu(x):
    return 0.5 * x * (1.0 + jnp.tanh(GELU_C * (x + GELU_A * x * x * x)))


def _gelu_grad(x):
    t = jnp.tanh(GELU_C * (x + GELU_A * x * x * x))
    return 0.5 * (1.0 + t) + 0.5 * x * (1.0 - t * t) * GELU_C * (1.0 + 3.0 * GELU_A * x * x)


GLU_HALO = 16


def _shift_down(x, prev, n):
    last = prev.shape[0]
    r = lax.broadcasted_iota(jnp.int32, (8, 1), 0)
    rolled = pltpu.roll(x, n, axis=0)
    head = rolled[0:8]
    for j in range(n):
        head = jnp.where(r == j, prev[last - n + j:last - n + j + 1, :], head)
    return jnp.concatenate([head, rolled[8:]], axis=0)


def _shift_up(x, nxt, n):
    rows = x.shape[0]
    r = lax.broadcasted_iota(jnp.int32, (8, 1), 0)
    rolled = pltpu.roll(x, rows - n, axis=0)
    tail = rolled[rows - 8:]
    for j in range(n):
        tail = jnp.where(r == 8 - n + j, nxt[j:j + 1, :], tail)
    return jnp.concatenate([rolled[:rows - 8], tail], axis=0)


def _conv(a, prev, cw, cb):
    return _shift_down(a, prev, 2) * cw[0:1, :] + _shift_down(a, prev, 1) * cw[1:2, :] + a * cw[2:3, :] + cb


def _glu_fwd(up, cw, cb, S, tm=1024, comm=None):
    T = up.shape[2]
    tm = _tile(S, tm)
    per_seq = S // tm

    def body(ab_ref, prev_ref, cw_ref, cb_ref, u_ref):
        i = pl.program_id(1)
        prev = jnp.where(i % per_seq == 0, 0.0, prev_ref[...].astype(F32))
        ac = _conv(ab_ref[0].astype(F32), prev, cw_ref[...], cb_ref[...])
        u_ref[...] = (_gelu(ac) * ab_ref[1].astype(F32)).astype(u_ref.dtype)

    before = tm // GLU_HALO
    return _pcall(
        body, (up, up, cw, cb), name="glu_fwd",
        out_shape=jax.ShapeDtypeStruct((FFN_SLABS, T, UP_SHARD), BF16), grid=(FFN_SLABS, T // tm),
        in_specs=[pl.BlockSpec((2, None, tm, UP_SHARD), lambda d, i: (0, d, i, 0)),
                  pl.BlockSpec((None, None, GLU_HALO, UP_SHARD),
                               lambda d, i: (0, d, jnp.maximum(i * before - 1, 0), 0)),
                  pl.BlockSpec((None, 3, UP_SHARD), lambda d, i: (d, 0, 0)),
                  pl.BlockSpec((None, 1, UP_SHARD), lambda d, i: (d, 0, 0))],
        out_specs=pl.BlockSpec((None, tm, UP_SHARD), lambda d, i: (d, i, 0)),
        sem=("parallel", "parallel"), comm=comm)


def _glu_bwd(up, d_u, cw, cb, S, tm=512, comm=None):
    T = up.shape[2]
    tm = _tile(S, tm)
    per_seq = S // tm
    n_tiles = T // tm
    per_tile = tm // GLU_HALO

    def body(ab_ref, prev_ref, abn_ref, du_ref, dun_ref, cw_ref, cb_ref, dup_ref, dcw_ref, dcb_ref):
        i = pl.program_id(1)

        @pl.when(i == 0)
        def _():
            dcw_ref[...] = jnp.zeros_like(dcw_ref)
            dcb_ref[...] = jnp.zeros_like(dcb_ref)

        cw, cb = cw_ref[...], cb_ref[...]
        a, b = ab_ref[0].astype(F32), ab_ref[1].astype(F32)
        prev = jnp.where(i % per_seq == 0, 0.0, prev_ref[...].astype(F32))
        a2, a1 = _shift_down(a, prev, 2), _shift_down(a, prev, 1)
        ac = a2 * cw[0:1, :] + a1 * cw[1:2, :] + a * cw[2:3, :] + cb
        du = du_ref[...].astype(F32)
        dup_ref[1] = (du * _gelu(ac)).astype(dup_ref.dtype)
        dac = du * b * _gelu_grad(ac)
        dcb_ref[...] += jnp.sum(dac, axis=0, keepdims=True)
        dcw_ref[0:1, :] += jnp.sum(dac * a2, axis=0, keepdims=True)
        dcw_ref[1:2, :] += jnp.sum(dac * a1, axis=0, keepdims=True)
        dcw_ref[2:3, :] += jnp.sum(dac * a, axis=0, keepdims=True)
        acn = _conv(abn_ref[0].astype(F32), a[tm - GLU_HALO:, :], cw, cb)
        dacn = jnp.where(i % per_seq == per_seq - 1, 0.0,
                         dun_ref[...].astype(F32) * abn_ref[1].astype(F32) * _gelu_grad(acn))
        da = dac * cw[2:3, :] + _shift_up(dac, dacn, 1) * cw[1:2, :] + _shift_up(dac, dacn, 2) * cw[0:1, :]
        dup_ref[0] = da.astype(dup_ref.dtype)

    def nxt(i):
        return jnp.minimum((i + 1) * per_tile, T // GLU_HALO - 1)

    return _pcall(
        body, (up, up, up, d_u, d_u, cw, cb), name="glu_bwd",
        out_shape=(jax.ShapeDtypeStruct((2, FFN_SLABS, T, UP_SHARD), BF16),
                   jax.ShapeDtypeStruct((FFN_SLABS, 3, UP_SHARD), F32),
                   jax.ShapeDtypeStruct((FFN_SLABS, 1, UP_SHARD), F32)),
        grid=(FFN_SLABS, n_tiles),
        in_specs=[pl.BlockSpec((2, None, tm, UP_SHARD), lambda d, i: (0, d, i, 0)),
                  pl.BlockSpec((None, None, GLU_HALO, UP_SHARD),
                               lambda d, i: (0, d, jnp.maximum(i * per_tile - 1, 0), 0)),
                  pl.BlockSpec((2, None, GLU_HALO, UP_SHARD), lambda d, i: (0, d, nxt(i), 0)),
                  pl.BlockSpec((None, tm, UP_SHARD), lambda d, i: (d, i, 0)),
                  pl.BlockSpec((None, GLU_HALO, UP_SHARD), lambda d, i: (d, nxt(i), 0)),
                  pl.BlockSpec((None, 3, UP_SHARD), lambda d, i: (d, 0, 0)),
                  pl.BlockSpec((None, 1, UP_SHARD), lambda d, i: (d, 0, 0))],
        out_specs=(pl.BlockSpec((2, None, tm, UP_SHARD), lambda d, i: (0, d, i, 0)),
                   pl.BlockSpec((None, 3, UP_SHARD), lambda d, i: (d, 0, 0)),
                   pl.BlockSpec((None, 1, UP_SHARD), lambda d, i: (d, 0, 0))),
        sem=("parallel", "arbitrary"), comm=comm)


def _mm_up(h2, w_up_t, tm=MM_ROWS):
    T, K = h2.shape
    tm = _tile(T, tm)
    return _matmul(
        "mm_up", "nt", h2, w_up_t, jax.ShapeDtypeStruct((N_DEV, T, UP_SHARD), BF16), (N_DEV, T // tm, 1),
        pl.BlockSpec((tm, K), lambda j, i, k: (i, 0)), pl.BlockSpec((None, UP_SHARD, K), lambda j, i, k: (j, 0, 0)),
        pl.BlockSpec((None, tm, UP_SHARD), lambda j, i, k: (j, i, 0)), (tm, UP_SHARD))


def _loss_epilogue(ffn, operands, outputs, first):
    x1_ref, t_ref, g_ref = operands
    dx_ref, dg_ref, loss_ref = outputs

    @pl.when(first)
    def _():
        dg_ref[...] = jnp.zeros_like(dg_ref)
        loss_ref[...] = jnp.zeros_like(loss_ref)

    xv = x1_ref[...] + ffn
    r = lax.rsqrt(jnp.mean(xv * xv, axis=-1, keepdims=True) + EPS)
    xhat = xv * r
    err = xhat * g_ref[...] - t_ref[...]
    loss_ref[...] += (0.5 / D_MODEL) * jnp.sum(err * err)
    dy = err * (1.0 / D_MODEL)
    dg_ref[...] += jnp.sum(dy * xhat, axis=0, keepdims=True)
    dxhat = dy * g_ref[...]
    dx_ref[...] = r * (dxhat - xhat * jnp.mean(dxhat * xhat, axis=-1, keepdims=True))


def _mm_down_loss(u, w_down, x1, target, g_final, tm=MM_ROWS_RES):
    J, T, n = u.shape
    tm = _tile(T, tm)
    row = pl.BlockSpec((tm, D_MODEL), lambda i, d: (i, 0))
    vec = pl.BlockSpec((1, D_MODEL), lambda i, d: (0, 0))
    vec_shape = jax.ShapeDtypeStruct((1, D_MODEL), F32)
    return _matmul(
        "mm_down", "nn", u, w_down, (jax.ShapeDtypeStruct((T, D_MODEL), F32), vec_shape, vec_shape), (T // tm, J),
        pl.BlockSpec((None, tm, n), lambda i, d: (d, i, 0)), pl.BlockSpec((None, n, D_MODEL), lambda i, d: (d, 0, 0)),
        (row, vec, vec), (tm, D_MODEL), [x1, target, g_final], [row, row, vec], epilogue=_loss_epilogue)


def _mm_down_t(dx, w_down, tm=MM_ROWS):
    T = dx.shape[0]
    J, n, _ = w_down.shape
    tm = _tile(T, tm)
    return _matmul(
        "mm_down_t", "nt", dx, w_down, jax.ShapeDtypeStruct((J, T, n), BF16), (J, T // tm, 1),
        pl.BlockSpec((tm, D_MODEL), lambda d, i, k: (i, 0)), pl.BlockSpec((None, n, D_MODEL), lambda d, i, k: (d, 0, 0)),
        pl.BlockSpec((None, tm, n), lambda d, i, k: (d, i, 0)), (tm, n))


def _mm_dw_down(u, dx, tk=MM_TOKENS):
    J, T, n = u.shape
    tk = _tile(T, tk)
    return _matmul(
        "mm_dw_down", "tn", u, dx, jax.ShapeDtypeStruct((J, n, D_MODEL), BF16), (J, T // tk),
        pl.BlockSpec((None, tk, n), lambda d, k: (d, k, 0)), pl.BlockSpec((tk, D_MODEL), lambda d, k: (k, 0)),
        pl.BlockSpec((None, n, D_MODEL), lambda d, k: (d, 0, 0)), (n, D_MODEL))


def _mm_dw_up(h2, d_up, tk=MM_TOKENS):
    T, K = h2.shape
    tk = _tile(T, tk)
    return _matmul(
        "mm_dw_up", "tn", d_up, h2, jax.ShapeDtypeStruct((N_DEV, UP_SHARD, K), BF16), (N_DEV, T // tk),
        pl.BlockSpec((None, tk, UP_SHARD), lambda j, k: (j, k, 0)), pl.BlockSpec((tk, K), lambda j, k: (k, 0)),
        pl.BlockSpec((None, UP_SHARD, K), lambda j, k: (j, 0, 0)), (UP_SHARD, K))


def _mm_up_t(d_up, w_up_t, rms, tm=MM_ROWS_RES, comm=None):
    J, T, n = d_up.shape
    K = w_up_t.shape[2]
    tm = _tile(T, tm)
    fused = _rms_bwd_fused(T, K, tm, rms)
    return _matmul(
        "mm_up_t", "nn", d_up, w_up_t, fused.pop("out_shape"), (T // tm, J),
        pl.BlockSpec((None, tm, n), lambda i, j: (j, i, 0)), pl.BlockSpec((None, n, K), lambda i, j: (j, 0, 0)),
        fused.pop("o_spec"), (tm, K), comm=comm, **fused)


def _cast_shards(shards):
    def body(*refs):
        n = len(refs) // 2
        for src, dst in zip(refs[:n], refs[n:]):
            dst[...] = src[...].astype(dst.dtype)

    return pl.pallas_call(
        body, out_shape=[jax.ShapeDtypeStruct(s.shape, BF16) for s in shards], name="cast_shards",
        compiler_params=pltpu.CompilerParams(vmem_limit_bytes=VMEM_LIMIT),
    )(*shards)


def _adamw(w, g, m, v):
    m = ADAM_B1 * m + (1.0 - ADAM_B1) * g
    v = ADAM_B2 * v + (1.0 - ADAM_B2) * (g * g)
    m_hat = m / (1.0 - ADAM_B1 ** ADAM_STEP)
    v_hat = v / (1.0 - ADAM_B2 ** ADAM_STEP)
    delta = -ADAM_LR * (m_hat / (jnp.sqrt(v_hat) + ADAM_EPS) + ADAM_WD * w)
    return delta, m, v


def _sum_parts(p_ref):
    g = p_ref[0].astype(F32)
    for d in range(1, N_DEV):
        g = g + p_ref[d].astype(F32)
    return g


def _reduce_adam(name, parts, w, m, v, tr=128):
    R, Cn = w.shape
    by_rows = sum(p.shape[1] for p in parts) == R and len(parts) > 1
    tr = math.gcd(tr, *[p.shape[1] for p in parts])
    n_tiles = [p.shape[1] // tr for p in parts]
    first = [sum(n_tiles[:j]) for j in range(len(parts))] if by_rows else [0] * len(parts)

    def body(*refs):
        p_refs = refs[:len(parts)]
        w_ref, m_ref, v_ref, g_out, d_out, m_out, v_out = refs[len(parts):]

        def update(p_ref):
            g = _sum_parts(p_ref)
            delta, m_new, v_new = _adamw(w_ref[...], g, m_ref[...], v_ref[...])
            g_out[...] = g
            d_out[...] = delta
            m_out[...] = m_new
            v_out[...] = v_new

        if len(parts) == 1:
            update(p_refs[0])
        elif by_rows:
            i = pl.program_id(0)
            for p_ref, t0, n in zip(p_refs, first, n_tiles):
                pl.when((i >= t0) & (i < t0 + n))(functools.partial(update, p_ref))
        else:
            c = lax.axis_index("c")
            for side, p_ref in enumerate(p_refs):
                pl.when(c == side)(functools.partial(update, p_ref))

    def part_spec(t0, n):
        return pl.BlockSpec((N_DEV, tr, Cn), lambda i: (0, jnp.clip(i - t0, 0, n - 1), 0))

    row = pl.BlockSpec((tr, Cn), lambda i: (i, 0))
    shape = jax.ShapeDtypeStruct((R, Cn), F32)
    return pl.pallas_call(
        body, out_shape=(shape,) * 4, grid=(R // tr,),
        in_specs=[part_spec(t0, n) for t0, n in zip(first, n_tiles)] + [row, row, row],
        out_specs=(row,) * 4, name=name, compiler_params=_params(("parallel",)),
    )(*parts, w, m, v)


def _small_adam(name, gathered, params):
    n_g, n_p = len(gathered), len(params)

    def body(*refs):
        g_refs = refs[:n_g]
        wmv = refs[n_g:n_g + 3 * n_p]
        sums = refs[n_g + 3 * n_p:2 * n_g + 3 * n_p]
        upd = refs[2 * n_g + 3 * n_p:]
        for j in range(n_g):
            g = _sum_parts(g_refs[j])
            sums[j][...] = g
            if j < n_p:
                w_ref, m_ref, v_ref = wmv[3 * j:3 * j + 3]
                delta, m_new, v_new = _adamw(w_ref[...], g, m_ref[...], v_ref[...])
                upd[3 * j][...] = delta
                upd[3 * j + 1][...] = m_new
                upd[3 * j + 2][...] = v_new

    flat = [a for wmv in params for a in wmv]
    out_shape = [jax.ShapeDtypeStruct(g.shape[1:], F32) for g in gathered]
    out_shape += [jax.ShapeDtypeStruct(a.shape, F32) for a in flat]
    res = pl.pallas_call(body, out_shape=out_shape, name=name)(*gathered, *flat)
    return res[:n_g], [tuple(res[n_g + 3 * j:n_g + 3 * j + 3]) for j in range(n_p)]


def _adam_only(name, g, w, m, v):
    def body(g_ref, w_ref, m_ref, v_ref, d_out, m_out, v_out):
        delta, m_new, v_new = _adamw(w_ref[...], g_ref[...], m_ref[...], v_ref[...])
        d_out[...] = delta
        m_out[...] = m_new
        v_out[...] = v_new

    shape = jax.ShapeDtypeStruct(w.shape, F32)
    return pl.pallas_call(body, out_shape=(shape,) * 3, name=name)(g, w, m, v)


def kernel(x, mem, g_mix, w_in, w_pool, pool_scale, w_a, g_ret, b_ret, w_r, g_mem, w_mem_kv, w_c, w_out, g_ffn, w_up, conv_w, conv_b, w_down, g_final, loss_target, m_g_mix, m_w_in, m_w_pool, m_pool_scale, m_w_a, m_g_ret, m_b_ret, m_w_r, m_g_mem, m_w_mem_kv, m_w_c, m_w_out, m_g_ffn, m_w_up, m_conv_w, m_conv_b, m_w_down, m_g_final, v_g_mix, v_w_in, v_w_pool, v_pool_scale, v_w_a, v_g_ret, v_b_ret, v_w_r, v_g_mem, v_w_mem_kv, v_w_c, v_w_out, v_g_ffn, v_w_up, v_conv_w, v_conv_b, v_w_down, v_g_final):
    B, S, _ = x.shape
    M = mem.shape[1]
    T = B * S
    me = _my_index()
    x2d = x.reshape(T, D_MODEL)
    mem2d = mem.reshape(B * M, D_MODEL)
    tgt2d = loss_target.reshape(T, D_MODEL)
    g_final2 = g_final.reshape(1, D_MODEL)

    big = dict(w_in=w_in[0], w_a=w_a[0], w_r=w_r[0], w_mem_kv=w_mem_kv[0], w_c=w_c[0], w_out=w_out[0],
               w_up=w_up[0].T, w_down=w_down[0])
    names = list(big)
    cast = dict(zip(names, _cast_shards([big[n] for n in names])))
    cb = conv_b[0].reshape(FFN_SLABS, 1, UP_SHARD)
    wp = w_pool[0]
    tables = _ret_tables(S)

    h = _rms_fwd("rms_mix", x2d, g_mix)
    early = ("w_a", "w_r", "w_mem_kv", "w_c", "w_out")
    (proj, Win), landed = _mm_in_gather(h, cast["w_in"], comm=_Gather([cast[n] for n in early] + [conv_w[0]]))
    W = dict(zip(early, landed))
    (yr, ret_states), (Wup,) = _ret_fwd(proj, g_ret, b_ret, tables, B, S, comm=_Gather([cast["w_up"]]))
    cw_full = landed[-1].transpose(1, 0, 2).reshape(3, FFN_HIDDEN)
    cw = cw_full.reshape(3, FFN_SLABS, UP_SHARD).transpose(1, 0, 2)
    Wa = W["w_a"].transpose(1, 0, 2).reshape(POOL_WIDTH, D_MODEL)
    Wc = W["w_c"].transpose(1, 0, 2).reshape(XA_WIDTH, D_MODEL)
    Wr = W["w_r"].reshape(D_MODEL, D_MODEL)
    Wkv = W["w_mem_kv"].reshape(D_MODEL, D_MODEL)
    Wout = W["w_out"].reshape(D_MODEL, D_MODEL)
    ypre = _pool_fwd(proj, wp, pool_scale, B, S)
    y_pool = _mm_rows("mm_a", ypre, Wa, BF16)
    y_ret = _mm_rows("mm_r", yr, Wr, BF16)
    mem_n = _rms_fwd("rms_mem", mem2d, g_mem)
    kv = _mm_rows("mm_kv", mem_n, Wkv)
    o_mem, (Wdown,) = _xa_fwd(proj, kv, B, S, M, comm=_Gather([cast["w_down"]]))
    Wdown = Wdown.reshape(FFN_SLABS, UP_SHARD, D_MODEL)
    y_mem = _mm_rows("mm_c", o_mem, Wc, BF16)
    ys = (y_pool, y_ret, y_mem)
    merged = _merge_fwd(proj, ys)[0]
    x1, h2 = _mm_residual_rms("mm_out", merged, Wout, x2d, g_ffn)
    up = _mm_up(h2, Wup).reshape(2, FFN_SLABS, T, UP_SHARD)
    u = _glu_fwd(up, cw, cb, S)[0]

    dx2, dg_final, loss_part = _mm_down_loss(u, Wdown, x1, tgt2d, g_final2)
    received = {}
    d_u = _mm_down_t(dx2, Wdown)
    dW_down = _mm_dw_down(u, dx2)
    (d_up, d_cw, d_cb), (received["w_down"],) = _glu_bwd(
        up, d_u, cw, cb, S, comm=_Exchange([dW_down.reshape(N_DEV, -1, D_MODEL)]))
    d_up = d_up.reshape(N_DEV, T, UP_SHARD)
    dW_up = _mm_dw_up(h2, d_up)
    (dx1, dg_ffn), (up_c0,) = _mm_up_t(d_up, Wup, (x1, g_ffn, dx2), comm=_ExchangeTo([dW_up], 0))
    d_merged = _mm_rows("mm_out_t", dx1, Wout, kind="nt")
    dW_out = _mm_tn("mm_dw_out", merged, dx1, BF16)
    (d_gl, d_y_pool, d_y_ret, d_y_mem), (received["w_out"],) = _merge_bwd(
        proj, ys, d_merged, comm=_Exchange([dW_out.reshape(N_DEV, -1, D_MODEL)]))
    dW_c = _mm_tn("mm_dw_c", o_mem, d_y_mem, BF16)
    d_o_mem = _mm_rows("mm_c_t", d_y_mem, Wc, kind="nt")
    (d_qx, d_kmem, d_vmem), (up_c1,) = _xa_bwd(proj, kv, d_o_mem, B, S, M, comm=_ExchangeTo([dW_up], 1))
    received["w_up"] = [up_c0, up_c1]
    d_kv = jnp.concatenate([d_kmem, d_vmem], axis=1)
    dW_kv = _mm_tn("mm_dw_kv", mem_n, d_kv, BF16)
    d_mem_n = _mm_rows("mm_kv_t", d_kv, Wkv, kind="nt")
    dg_mem = _rms_bwd("rms_mem_bwd", mem2d, g_mem, d_mem_n, None)
    dW_a = _mm_tn("mm_dw_a", ypre, d_y_pool, BF16)
    d_ypre = _mm_rows("mm_a_t", d_y_pool, Wa, kind="nt")
    d_hp, dw_pool, d_scale = _pool_bwd(proj, d_ypre, wp, pool_scale, B, S)
    dW_r = _mm_tn("mm_dw_r", yr, d_y_ret, BF16)
    d_yr = _mm_rows("mm_r_t", d_y_ret, Wr, kind="nt")
    (d_q, d_k, d_v, d_gr, dg_ret, db_ret), landed = _ret_bwd(
        proj, ret_states, d_yr, g_ret, b_ret, tables, B, S,
        comm=_Exchange([dW_a.reshape(POOL_WIDTH, N_DEV, -1).transpose(1, 0, 2), dW_r.reshape(N_DEV, -1, D_MODEL),
                        dW_c.reshape(XA_WIDTH, N_DEV, -1).transpose(1, 0, 2), dW_kv.reshape(N_DEV, -1, D_MODEL)]))
    received["w_a"], received["w_r"], received["w_c"], received["w_mem_kv"] = landed
    small_names = ["w_pool", "pool_scale", "g_ret", "b_ret", "g_mem", "g_ffn", "conv_b", "g_final"]
    small_grads = [dw_pool, d_scale, dg_ret, db_ret, dg_mem, dg_ffn, d_cb.reshape(1, FFN_HIDDEN), dg_final,
                   d_cw.transpose(1, 0, 2).reshape(3, FFN_HIDDEN), loss_part]
    d_proj = jnp.concatenate([d_hp, d_q, d_k, d_v, d_gr, d_qx, d_gl], axis=1)
    dW_in0, small_all = _mm_tn_slab("mm_dw_in0", h[:, :W_IN_FIRST_ROWS], d_proj, IN_SHARD, BF16,
                                    comm=_Exchange([], whole=small_grads))
    dW_in1, (in0,) = _mm_tn_slab("mm_dw_in1", h[:, W_IN_FIRST_ROWS:], d_proj, IN_SHARD, BF16,
                                 comm=_Exchange([dW_in0]))
    (grad_x, dg_mix), (in1,) = _mm_cols_slab_t("mm_in_t", d_proj, Win, (x2d, g_mix, dx1), comm=_Exchange([dW_in1]))
    received["w_in"] = [in0, in1]
    (g_mix_all,) = _comm_call("gather_g_mix", _Exchange([], whole=[dg_mix]))

    args = dict(g_mix=g_mix, w_in=w_in, w_pool=w_pool, pool_scale=pool_scale, w_a=w_a, g_ret=g_ret, b_ret=b_ret,
                w_r=w_r, g_mem=g_mem, w_mem_kv=w_mem_kv, w_c=w_c, w_out=w_out, g_ffn=g_ffn, w_up=w_up,
                conv_w=conv_w, conv_b=conv_b, w_down=w_down, g_final=g_final)
    m_in = dict(g_mix=m_g_mix, w_in=m_w_in, w_pool=m_w_pool, pool_scale=m_pool_scale, w_a=m_w_a, g_ret=m_g_ret,
                b_ret=m_b_ret, w_r=m_w_r, g_mem=m_g_mem, w_mem_kv=m_w_mem_kv, w_c=m_w_c, w_out=m_w_out,
                g_ffn=m_g_ffn, w_up=m_w_up, conv_w=m_conv_w, conv_b=m_conv_b, w_down=m_w_down, g_final=m_g_final)
    v_in = dict(g_mix=v_g_mix, w_in=v_w_in, w_pool=v_w_pool, pool_scale=v_pool_scale, w_a=v_w_a, g_ret=v_g_ret,
                b_ret=v_b_ret, w_r=v_w_r, g_mem=v_g_mem, w_mem_kv=v_w_mem_kv, w_c=v_w_c, w_out=v_w_out,
                g_ffn=v_g_ffn, w_up=v_w_up, conv_w=v_conv_w, conv_b=v_conv_b, w_down=v_w_down, g_final=v_g_final)

    grads, deltas, new_m, new_v = {}, {}, {}, {}
    for n in names:
        parts = received[n] if isinstance(received[n], list) else [received[n]]
        flip = (lambda a: a.T) if n == "w_up" else (lambda a: a)
        outs = _reduce_adam("adam_" + n, parts, big[n], flip(m_in[n][0]), flip(v_in[n][0]))
        for store, val in zip((grads, deltas, new_m, new_v), outs):
            store[n] = flip(val)[None]

    def as_small(a):
        return a.reshape(a.shape[-3:]) if a.ndim > 2 else a.reshape(1, -1)

    def small_update(call_name, param_names, gathered):
        params = [tuple(as_small(d[n]) for d in (args, m_in, v_in)) for n in param_names]
        sums, updates = _small_adam(call_name, gathered, params)
        for n, g, (d_, m_, v_) in zip(param_names, sums, updates):
            shape = args[n].shape
            grads[n], deltas[n], new_m[n], new_v[n] = (a.reshape(shape) for a in (g, d_, m_, v_))
        return sums[len(param_names):]

    g_cw_full, loss_row = small_update("adam_small", small_names, small_all)
    loss = loss_row[0, 0]
    small_update("adam_g_mix", ["g_mix"], [g_mix_all])

    shard_cols = FFN_HIDDEN // N_DEV
    g_cw = lax.dynamic_slice_in_dim(g_cw_full, me * shard_cols, shard_cols, axis=1)
    d_, m_, v_ = _adam_only("adam_conv_w", g_cw, conv_w[0], m_conv_w[0], v_conv_w[0])
    grads["conv_w"], deltas["conv_w"], new_m["conv_w"], new_v["conv_w"] = g_cw[None], d_[None], m_[None], v_[None]

    order = ["g_mix", "w_in", "w_pool", "pool_scale", "w_a", "g_ret", "b_ret", "w_r", "g_mem", "w_mem_kv", "w_c",
             "w_out", "g_ffn", "w_up", "conv_w", "conv_b", "w_down", "g_final"]
    return (loss, grad_x.reshape(B, S, D_MODEL), *[grads[n] for n in order], *[deltas[n] for n in order],
            *[new_m[n] for n in order], *[new_v[n] for n in order])
```

```python
import functools
import math

import jax
import jax.numpy as jnp
from jax import lax
from jax.experimental import pallas as pl
from jax.experimental.pallas import tpu as pltpu

F32 = jnp.float32
BF16 = jnp.bfloat16

N_DEV = 8
D_MODEL = 1024
POOL_WINDOWS = (2, 4, 8, 16)
POOL_GROUP_DIM = 128
POOL_WIDTH = 512
POOL_HALO = 16
RET_HEADS = 4
RET_QK_DIM = 128
RET_V_DIM = 256
RET_CHUNK = 128
ROPE_BASE = 10000.0
XA_HEADS = 4
XA_HEAD_DIM = 128
XA_WIDTH = 512
IN_WIDTH = 7168
IN_SHARD = IN_WIDTH // N_DEV
FFN_HIDDEN = 2816
UP_SHARD = 2 * FFN_HIDDEN // N_DEV
FFN_SLABS = FFN_HIDDEN // UP_SHARD
EPS = 1e-6
ADAM_LR = 0.001
ADAM_B1 = 0.9
ADAM_B2 = 0.999
ADAM_EPS = 1e-08
ADAM_WD = 0.01
ADAM_STEP = 10
GELU_C = math.sqrt(2.0 / math.pi)
GELU_A = 0.044715
VMEM_LIMIT = 56 * 1024 * 1024
MM_ROWS = 2048
MM_ROWS_RES = 1024
MM_TOKENS = 2048
W_IN_FIRST_ROWS = 384
MESH = pl.DeviceIdType.MESH

COL_Q, COL_K, COL_V, COL_GR, COL_QX, COL_GL = 512, 1024, 1536, 2560, 3584, 4096

_DIMS = {
    "nn": (((1,), (0,)), ((), ())),
    "nt": (((1,), (1,)), ((), ())),
    "tn": (((0,), (0,)), ((), ())),
}


def _dot(a, b, kind="nn"):
    return lax.dot_general(a.astype(BF16), b.astype(BF16), _DIMS[kind], preferred_element_type=F32)


def _params(sem, vmem=VMEM_LIMIT):
    return pltpu.CompilerParams(dimension_semantics=sem, vmem_limit_bytes=vmem)


def _tile(n, pref):
    t = min(n, pref)
    while n % t:
        t //= 2
    return t


def _mesh_pos():
    return lax.axis_index("x"), lax.axis_index("y"), lax.axis_index("c")


def _dev_index(x, y, c):
    return 4 * x + 2 * y + c


def _my_index():
    return _dev_index(*_mesh_pos())


def _remote(src, dst, send_sems, recv_sems, s, to):
    return pltpu.make_async_remote_copy(src_ref=src, dst_ref=dst, send_sem=send_sems.at[s], recv_sem=recv_sems.at[s],
                                        device_id=to, device_id_type=MESH)


class _Gather:
    def __init__(self, shards):
        self.inputs = list(shards)
        self.out_shapes = [jax.ShapeDtypeStruct((N_DEV,) + s.shape, s.dtype) for s in shards]
        n = len(shards)
        self.sem_shapes = [pltpu.SemaphoreType.DMA((7 * n,)), pltpu.SemaphoreType.DMA((7 * n,)),
                           pltpu.SemaphoreType.DMA((n,))]

    def _places(self):
        x, y, c = _mesh_pos()
        return (x, y, c), (x, y, 1 - c), [(1 - x, y), (x, 1 - y), (1 - x, 1 - y)]

    def _local(self, src, dst, sems):
        me = _my_index()
        return [pltpu.make_async_copy(src[w], dst[w].at[me], sems[2].at[w]) for w in range(len(src))]

    def start(self, src, dst, sems):
        me, sib, chips = self._places()
        for cp in self._local(src, dst, sems):
            cp.start()
        for w in range(len(src)):
            land = dst[w].at[_dev_index(*me)]
            _remote(src[w], land, sems[0], sems[1], 7 * w, sib).start()
            for j, chip in enumerate(chips):
                _remote(src[w], land, sems[0], sems[1], 7 * w + 1 + j, (*chip, me[2])).start()

    def middle(self, src, dst, sems):
        me, sib, chips = self._places()
        for j, chip in enumerate(chips):
            for w in range(len(src)):
                block = dst[w].at[_dev_index(*chip, me[2])]
                _remote(src[w], block, sems[0], sems[1], 7 * w + 1 + j, me).wait_recv()
                _remote(block, block, sems[0], sems[1], 7 * w + 4 + j, sib).start()

    def finish(self, src, dst, sems):
        me, sib, chips = self._places()
        n = len(src)
        for w in range(n):
            _remote(src[w], dst[w].at[_dev_index(*sib)], sems[0], sems[1], 7 * w, me).wait_recv()
            for j, chip in enumerate(chips):
                block = dst[w].at[_dev_index(*chip, sib[2])]
                _remote(block, block, sems[0], sems[1], 7 * w + 4 + j, me).wait_recv()
            for k in range(7):
                _remote(src[w], dst[w].at[0], sems[0], sems[1], 7 * w + k, me).wait_send()
        for cp in self._local(src, dst, sems):
            cp.wait()


class _Exchange:
    def __init__(self, partials, whole=()):
        self.n_part = len(partials)
        self.inputs = list(partials) + list(whole)
        self.out_shapes = [jax.ShapeDtypeStruct(p.shape, p.dtype) for p in partials]
        self.out_shapes += [jax.ShapeDtypeStruct((N_DEV,) + a.shape, a.dtype) for a in whole]
        n = len(self.inputs)
        self.sem_shapes = [pltpu.SemaphoreType.DMA((7 * n,)), pltpu.SemaphoreType.DMA((7 * n,)),
                           pltpu.SemaphoreType.DMA((n,))]

    def _peer(self, k):
        x, y, c = _mesh_pos()
        p = (x ^ ((k >> 2) & 1), y ^ ((k >> 1) & 1), c ^ (k & 1))
        return p, _dev_index(*p)

    def _source(self, src, w, slot):
        return src[w].at[slot] if w < self.n_part else src[w]

    def _local(self, src, dst, sems):
        me = _my_index()
        return [pltpu.make_async_copy(self._source(src, w, me), dst[w].at[me], sems[2].at[w])
                for w in range(len(src))]

    def start(self, src, dst, sems):
        me = _my_index()
        for cp in self._local(src, dst, sems):
            cp.start()
        for k in range(1, N_DEV):
            peer, peer_idx = self._peer(k)
            for w in range(len(src)):
                _remote(self._source(src, w, peer_idx), dst[w].at[me], sems[0], sems[1], 7 * w + k - 1, peer).start()

    def finish(self, src, dst, sems):
        for k in range(1, N_DEV):
            peer, peer_idx = self._peer(k)
            for w in range(len(src)):
                cp = _remote(self._source(src, w, peer_idx), dst[w].at[peer_idx], sems[0], sems[1], 7 * w + k - 1, peer)
                cp.wait_send()
                cp.wait_recv()
        for cp in self._local(src, dst, sems):
            cp.wait()


class _ExchangeTo:
    def __init__(self, partials, side):
        self.side = side
        self.inputs = list(partials)
        self.out_shapes = [jax.ShapeDtypeStruct(p.shape, p.dtype) for p in partials]
        n = len(partials)
        self.sem_shapes = [pltpu.SemaphoreType.DMA((7 * n,)), pltpu.SemaphoreType.DMA((7 * n,)),
                           pltpu.SemaphoreType.DMA((n,))]

    def _copies(self, src, dst, sems):
        x, y, c = _mesh_pos()
        me = _dev_index(x, y, c)
        receives = c == self.side
        remote = []
        for k in range(1, N_DEV):
            kx, ky, kc = (k >> 2) & 1, (k >> 1) & 1, k & 1
            peer = (x ^ kx, y ^ ky, c ^ kc)
            peer_idx = _dev_index(*peer)
            sends = c == (self.side ^ kc)
            for w in range(len(src)):
                slab = src[w].at[peer_idx]
                s = 7 * w + k - 1
                remote.append((sends, _remote(slab, dst[w].at[me], sems[0], sems[1], s, peer),
                               _remote(slab, dst[w].at[peer_idx], sems[0], sems[1], s, peer)))
        local = [pltpu.make_async_copy(src[w].at[me], dst[w].at[me], sems[2].at[w]) for w in range(len(src))]
        return receives, remote, local

    def start(self, src, dst, sems):
        receives, remote, local = self._copies(src, dst, sems)

        @pl.when(receives)
        def _():
            for cp in local:
                cp.start()

        for sends, send, _ in remote:
            pl.when(sends)(send.start)

    def finish(self, src, dst, sems):
        receives, remote, local = self._copies(src, dst, sems)
        for sends, send, arrive in remote:
            pl.when(sends)(send.wait_send)
            pl.when(receives)(arrive.wait_recv)

        @pl.when(receives)
        def _():
            for cp in local:
                cp.wait()


def _pcall(body, args, *, name, out_shape, grid, in_specs, out_specs, scratch_shapes=(), sem=None, comm=None):
    single = not isinstance(out_shape, (tuple, list))
    outs = [out_shape] if single else list(out_shape)
    ospecs = [out_specs] if single else list(out_specs)
    n_in, n_out, n_scr = len(args), len(outs), len(scratch_shapes)

    def pick(res):
        return res[0] if single else tuple(res[:n_out])

    if comm is None:
        res = pl.pallas_call(
            body, out_shape=outs, grid=grid, in_specs=list(in_specs), out_specs=ospecs,
            scratch_shapes=list(scratch_shapes), name=name, compiler_params=_params(sem),
        )(*args)
        return pick(res), ()

    nci, nco = len(comm.inputs), len(comm.out_shapes)

    def carrier(*refs):
        at = 0
        parts = []
        for size in (n_in, nci, n_out, nco, n_scr, len(comm.sem_shapes)):
            parts.append(refs[at:at + size])
            at += size
        ins, cins, o, couts, scr, sems = parts
        ids = [pl.program_id(a) for a in range(len(grid))]
        first = functools.reduce(jnp.logical_and, [i == 0 for i in ids])
        last = functools.reduce(jnp.logical_and, [i == g - 1 for i, g in zip(ids, grid)])

        body(*ins, *o, *scr)

        @pl.when(first)
        def _():
            comm.start(cins, couts, sems)

        if hasattr(comm, "middle"):
            steps = math.prod(grid)
            at = functools.reduce(lambda lin, ig: lin * ig[1] + ig[0], zip(ids, grid), 0)

            @pl.when(at == min(steps - 1, (3 * steps) // 4))
            def _():
                comm.middle(cins, couts, sems)

        @pl.when(last)
        def _():
            comm.finish(cins, couts, sems)

    hbm = pl.BlockSpec(memory_space=pltpu.HBM)
    res = pl.pallas_call(
        carrier, out_shape=outs + comm.out_shapes, grid=grid, in_specs=list(in_specs) + [hbm] * nci,
        out_specs=ospecs + [hbm] * nco, scratch_shapes=list(scratch_shapes) + comm.sem_shapes, name=name,
        compiler_params=_params(("arbitrary",) * len(grid)),
    )(*args, *comm.inputs)
    return pick(res), tuple(res[n_out:])


def _comm_call(name, comm):
    def body(*refs):
        nci, nco = len(comm.inputs), len(comm.out_shapes)
        cins, couts, sems = refs[:nci], refs[nci:nci + nco], refs[nci + nco:]
        comm.start(cins, couts, sems)
        if hasattr(comm, "middle"):
            comm.middle(cins, couts, sems)
        comm.finish(cins, couts, sems)

    hbm = pl.BlockSpec(memory_space=pltpu.HBM)
    return pl.pallas_call(
        body, out_shape=comm.out_shapes, in_specs=[hbm] * len(comm.inputs), out_specs=[hbm] * len(comm.out_shapes),
        scratch_shapes=comm.sem_shapes, name=name,
    )(*comm.inputs)


def _matmul(name, kind, a, b, out_shape, grid, a_spec, b_spec, o_spec, acc_shape, res=None, res_spec=None,
            comm=None, epilogue=None):
    nk = grid[-1]
    if epilogue is None:
        extra, extra_specs = ([res], [res_spec]) if res is not None else ([], [])
        n_out = 1
    else:
        extra, extra_specs, n_out = list(res), list(res_spec), len(out_shape)
    n_in = 2 + len(extra)

    def body(*refs):
        a_ref, b_ref = refs[0], refs[1]
        extra_refs, out_refs = refs[2:n_in], refs[n_in:n_in + n_out]

        def prod():
            return _dot(a_ref[...], b_ref[...], kind)

        def finish(acc):
            if epilogue is not None:
                ids = [pl.program_id(ax) for ax in range(len(grid) - 1)]
                first = functools.reduce(jnp.logical_and, [i == 0 for i in ids]) if ids else True
                epilogue(acc, extra_refs, out_refs, first)
                return
            if extra_refs:
                acc = acc + extra_refs[0][...]
            out_refs[0][...] = acc.astype(out_refs[0].dtype)

        if nk == 1:
            finish(prod())
        else:
            acc_ref = refs[n_in + n_out]
            k = pl.program_id(len(grid) - 1)

            @pl.when(k == 0)
            def _():
                acc_ref[...] = prod()

            @pl.when(k > 0)
            def _():
                acc_ref[...] += prod()

            @pl.when(k == nk - 1)
            def _():
                finish(acc_ref[...])

    in_specs = [a_spec, b_spec] + extra_specs
    args = (a, b, *extra)
    scratch = [pltpu.VMEM(acc_shape, F32)] if nk > 1 else []
    sem = ("arbitrary",) * len(grid) if epilogue is not None else ("parallel",) * (len(grid) - 1) + ("arbitrary",)
    out, landed = _pcall(body, args, name=name, out_shape=out_shape, grid=grid, in_specs=in_specs,
                         out_specs=o_spec, scratch_shapes=scratch, sem=sem, comm=comm)
    return out if comm is None else (out, landed)


def _mm_rows(name, a, w, out_dtype=F32, res=None, kind="nn", tm=MM_ROWS, comm=None):
    M, K = a.shape
    N = w.shape[1] if kind == "nn" else w.shape[0]
    tm = _tile(M, tm)
    res_spec = pl.BlockSpec((tm, N), lambda i, k: (i, 0)) if res is not None else None
    return _matmul(
        name, kind, a, w, jax.ShapeDtypeStruct((M, N), out_dtype), (M // tm, 1),
        pl.BlockSpec((tm, K), lambda i, k: (i, 0)), pl.BlockSpec(w.shape, lambda i, k: (0, 0)),
        pl.BlockSpec((tm, N), lambda i, k: (i, 0)), (tm, N), res, res_spec, comm)


def _residual_rms_epilogue(y, operands, outputs, first):
    x_ref, g_ref = operands
    x1_ref, h_ref = outputs
    xv = x_ref[...] + y
    x1_ref[...] = xv
    r = lax.rsqrt(jnp.mean(xv * xv, axis=-1, keepdims=True) + EPS)
    h_ref[...] = (xv * r * g_ref[...]).astype(h_ref.dtype)


def _mm_residual_rms(name, a, w, x, g, tm=MM_ROWS_RES):
    M, K = a.shape
    N = w.shape[1]
    tm = _tile(M, tm)
    row = pl.BlockSpec((tm, N), lambda i, k: (i, 0))
    return _matmul(
        name, "nn", a, w, (jax.ShapeDtypeStruct((M, N), F32), jax.ShapeDtypeStruct((M, N), BF16)), (M // tm, 1),
        pl.BlockSpec((tm, K), lambda i, k: (i, 0)), pl.BlockSpec(w.shape, lambda i, k: (0, 0)),
        (row, row), (tm, N), [x, g], [row, pl.BlockSpec((1, N), lambda i, k: (0, 0))],
        epilogue=_residual_rms_epilogue)


def _mm_tn(name, a, b, out_dtype=F32, tk=MM_TOKENS, comm=None):
    T, M = a.shape
    N = b.shape[1]
    tk = _tile(T, tk)
    return _matmul(
        name, "tn", a, b, jax.ShapeDtypeStruct((M, N), out_dtype), (1, T // tk),
        pl.BlockSpec((tk, M), lambda i, k: (k, 0)), pl.BlockSpec((tk, N), lambda i, k: (k, 0)),
        pl.BlockSpec((M, N), lambda i, k: (0, 0)), (M, N), comm=comm)


def _mm_in_gather(h, shard, tm=MM_ROWS, comm=None):
    T, K = h.shape
    n = shard.shape[1]
    tm = _tile(T, tm)
    n_tiles = T // tm
    pair_of_chip_step = {4: 1, 2: 2, 6: 3}

    def slab_of(s):
        x, y, c = _mesh_pos()
        return _dev_index(x ^ ((s >> 2) & 1), y ^ ((s >> 1) & 1), c ^ (s & 1))

    def body(h_ref, shard_ref, proj_ref, win_ref, wbuf, slot_sems, send_sems, recv_sems, local_sem):
        s, i = pl.program_id(0), pl.program_id(1)
        x, y, c = _mesh_pos()
        me, sib = (x, y, c), (x, y, 1 - c)

        def slot_copy(step):
            src = shard_ref if step == 0 else win_ref.at[slab_of(step)]
            return pltpu.make_async_copy(src, wbuf.at[step % 2], slot_sems.at[step % 2])

        def fetch(step):
            if step >= 1:
                block = win_ref.at[slab_of(step)]
                if step == 1:
                    pair = 0
                elif step % 2 == 0:
                    pair = pair_of_chip_step[step]
                else:
                    pair = 3 + pair_of_chip_step[step - 1]
                _remote(block, block, send_sems, recv_sems, pair, me).wait_recv()
                if step % 2 == 0:
                    _remote(block, block, send_sems, recv_sems, 3 + pair, sib).start()
            slot_copy(step).start()

        @pl.when((s == 0) & (i == 0))
        def _():
            land = win_ref.at[_dev_index(*me)]
            pltpu.make_async_copy(shard_ref, land, local_sem).start()
            _remote(shard_ref, land, send_sems, recv_sems, 0, sib).start()
            for step, pair in pair_of_chip_step.items():
                peer = (x ^ ((step >> 2) & 1), y ^ ((step >> 1) & 1), c)
                _remote(shard_ref, land, send_sems, recv_sems, pair, peer).start()
            fetch(0)

        for step in range(N_DEV):
            @pl.when((s == step) & (i == 0))
            def _():
                slot_copy(step).wait()

            if step + 1 < N_DEV:
                @pl.when((s == step) & (i == n_tiles - 1))
                def _():
                    fetch(step + 1)

        proj_ref[...] = _dot(h_ref[...], wbuf[s % 2])

        @pl.when((s == N_DEV - 1) & (i == n_tiles - 1))
        def _():
            for pair in range(7):
                _remote(shard_ref, win_ref.at[0], send_sems, recv_sems, pair, me).wait_send()
            pltpu.make_async_copy(shard_ref, win_ref.at[_dev_index(*me)], local_sem).wait()

    hbm = pl.BlockSpec(memory_space=pltpu.HBM)
    return _pcall(
        body, (h, shard), name="mm_in",
        out_shape=(jax.ShapeDtypeStruct((T, N_DEV * n), F32), jax.ShapeDtypeStruct((N_DEV, K, n), shard.dtype)),
        grid=(N_DEV, n_tiles), in_specs=[pl.BlockSpec((tm, K), lambda s, i: (i, 0)), hbm],
        out_specs=(pl.BlockSpec((tm, n), lambda s, i: (i, slab_of(s))), hbm),
        scratch_shapes=[pltpu.VMEM((2, K, n), shard.dtype), pltpu.SemaphoreType.DMA((2,)),
                        pltpu.SemaphoreType.DMA((7,)), pltpu.SemaphoreType.DMA((7,)), pltpu.SemaphoreType.DMA],
        sem=("arbitrary", "arbitrary"), comm=comm)


def _rms_bwd_epilogue(dh, operands, outputs, first):
    x_ref, g_ref, dres_ref = operands
    dx_ref, dg_ref = outputs
    xv = x_ref[...]
    r = lax.rsqrt(jnp.mean(xv * xv, axis=-1, keepdims=True) + EPS)
    xhat = xv * r

    @pl.when(first)
    def _():
        dg_ref[...] = jnp.zeros_like(dg_ref)

    dg_ref[...] += jnp.sum(dh * xhat, axis=0, keepdims=True)
    dxhat = dh * g_ref[...]
    dx_ref[...] = dres_ref[...] + r * (dxhat - xhat * jnp.mean(dxhat * xhat, axis=-1, keepdims=True))


def _rms_bwd_fused(M, K, tm, rms):
    row = pl.BlockSpec((tm, K), lambda i, j: (i, 0))
    vec = pl.BlockSpec((1, K), lambda i, j: (0, 0))
    x, g, dres = rms
    return dict(res=[x, g, dres], res_spec=[row, vec, row], epilogue=_rms_bwd_epilogue,
                out_shape=(jax.ShapeDtypeStruct((M, K), F32), jax.ShapeDtypeStruct((1, K), F32)), o_spec=(row, vec))


def _mm_cols_slab_t(name, a, w_slabs, rms, tm=MM_ROWS_RES, comm=None):
    M = a.shape[0]
    J, K, n = w_slabs.shape
    tm = _tile(M, tm)
    fused = _rms_bwd_fused(M, K, tm, rms)
    return _matmul(
        name, "nt", a, w_slabs, fused.pop("out_shape"), (M // tm, J),
        pl.BlockSpec((tm, n), lambda i, j: (i, j)), pl.BlockSpec((None, K, n), lambda i, j: (j, 0, 0)),
        fused.pop("o_spec"), (tm, K), comm=comm, **fused)


def _mm_tn_slab(name, a, b, n, out_dtype=F32, tk=MM_TOKENS, comm=None, part=(0, 1)):
    T, M = a.shape
    p, of = part
    M //= of
    J = b.shape[1] // n
    tk = _tile(T, tk)
    return _matmul(
        name, "tn", a, b, jax.ShapeDtypeStruct((J, M, n), out_dtype), (J, T // tk),
        pl.BlockSpec((tk, M), lambda j, k: (k, p)), pl.BlockSpec((tk, n), lambda j, k: (k, j)),
        pl.BlockSpec((None, M, n), lambda j, k: (j, 0, 0)), (M, n), comm=comm)


def _rms_fwd(name, x, g, tm=512):
    T, Dm = x.shape
    tm = _tile(T, tm)

    def body(x_ref, g_ref, h_ref):
        xv = x_ref[...]
        r = lax.rsqrt(jnp.mean(xv * xv, axis=-1, keepdims=True) + EPS)
        h_ref[...] = (xv * r * g_ref[...]).astype(h_ref.dtype)

    return pl.pallas_call(
        body, out_shape=jax.ShapeDtypeStruct((T, Dm), BF16), grid=(T // tm,),
        in_specs=[pl.BlockSpec((tm, Dm), lambda i: (i, 0)), pl.BlockSpec((1, Dm), lambda i: (0, 0))],
        out_specs=pl.BlockSpec((tm, Dm), lambda i: (i, 0)), name=name, compiler_params=_params(("parallel",)),
    )(x, g)


def _rms_bwd(name, x, g, dh, dres, tm=512):
    T, Dm = x.shape
    tm = _tile(T, tm)
    want_dx = dres is not None

    def body(*refs):
        if want_dx:
            x_ref, g_ref, dh_ref, dres_ref, dx_ref, dg_ref = refs
        else:
            x_ref, g_ref, dh_ref, dg_ref = refs
        xv = x_ref[...]
        r = lax.rsqrt(jnp.mean(xv * xv, axis=-1, keepdims=True) + EPS)
        xhat = xv * r
        dhv = dh_ref[...]

        @pl.when(pl.program_id(0) == 0)
        def _():
            dg_ref[...] = jnp.zeros_like(dg_ref)

        dg_ref[...] += jnp.sum(dhv * xhat, axis=0, keepdims=True)
        if want_dx:
            dxhat = dhv * g_ref[...]
            dx_ref[...] = dres_ref[...] + r * (dxhat - xhat * jnp.mean(dxhat * xhat, axis=-1, keepdims=True))

    row = pl.BlockSpec((tm, Dm), lambda i: (i, 0))
    vec = pl.BlockSpec((1, Dm), lambda i: (0, 0))
    if want_dx:
        return pl.pallas_call(
            body, out_shape=(jax.ShapeDtypeStruct((T, Dm), F32), jax.ShapeDtypeStruct((1, Dm), F32)),
            grid=(T // tm,), in_specs=[row, vec, row, row], out_specs=(row, vec), name=name,
            compiler_params=_params(("arbitrary",)),
        )(x, g, dh, dres)
    return pl.pallas_call(
        body, out_shape=jax.ShapeDtypeStruct((1, Dm), F32), grid=(T // tm,), in_specs=[row, vec, row],
        out_specs=vec, name=name, compiler_params=_params(("arbitrary",)),
    )(x, g, dh)


def _pool_rows(S):
    return _tile(S, 256)


def _pool_count(c0, rows, w):
    t = c0 + lax.broadcasted_iota(jnp.int32, (rows, 1), 0)
    return jnp.minimum(t + 1, w).astype(F32)


def _pool_fwd(proj, w_pool, scale, B, S):
    CH = _pool_rows(S)

    def body(hp_ref, wp_ref, sc_ref, o_ref, pad_ref):
        pad_ref[0:POOL_HALO, :] = jnp.zeros((POOL_HALO, POOL_WIDTH), F32)
        pad_ref[POOL_HALO:, :] = hp_ref[...]
        for gi, w in enumerate(POOL_WINDOWS):
            cols = slice(gi * POOL_GROUP_DIM, (gi + 1) * POOL_GROUP_DIM)
            for c in range(S // CH):
                base = POOL_HALO + c * CH
                acc = pad_ref[base:base + CH, cols]
                tok = acc
                for j in range(1, w):
                    acc = acc + pad_ref[base - j:base - j + CH, cols]
                pooled = acc / _pool_count(c * CH, CH, w) - tok
                z = _dot(pooled, wp_ref[gi])
                o_ref[c * CH:(c + 1) * CH, cols] = (z * sc_ref[:, cols]).astype(o_ref.dtype)

    return pl.pallas_call(
        body, out_shape=jax.ShapeDtypeStruct((B * S, POOL_WIDTH), BF16), grid=(B,),
        in_specs=[pl.BlockSpec((S, POOL_WIDTH), lambda b: (b, 0)),
                  pl.BlockSpec(w_pool.shape, lambda b: (0, 0, 0)),
                  pl.BlockSpec((1, POOL_WIDTH), lambda b: (0, 0))],
        out_specs=pl.BlockSpec((S, POOL_WIDTH), lambda b: (b, 0)),
        scratch_shapes=[pltpu.VMEM((S + POOL_HALO, POOL_WIDTH), F32)],
        name="pool_fwd", compiler_params=_params(("parallel",)),
    )(proj, w_pool, scale)


def _pool_bwd(proj, d_ypre, w_pool, scale, B, S):
    CH = _pool_rows(S)

    def body(hp_ref, dy_ref, wp_ref, sc_ref, dhp_ref, dwp_ref, dsc_ref, pad_ref, sc_pad_ref, dp_ref):
        @pl.when(pl.program_id(0) == 0)
        def _():
            dwp_ref[...] = jnp.zeros_like(dwp_ref)
            dsc_ref[...] = jnp.zeros_like(dsc_ref)

        pad_ref[0:POOL_HALO, :] = jnp.zeros((POOL_HALO, POOL_WIDTH), F32)
        pad_ref[POOL_HALO:, :] = hp_ref[...]
        sc_pad_ref[S:, :] = jnp.zeros((POOL_HALO, POOL_WIDTH), F32)
        for gi, w in enumerate(POOL_WINDOWS):
            cols = slice(gi * POOL_GROUP_DIM, (gi + 1) * POOL_GROUP_DIM)
            for c in range(S // CH):
                base = POOL_HALO + c * CH
                rows = slice(c * CH, (c + 1) * CH)
                acc = pad_ref[base:base + CH, cols]
                tok = acc
                for j in range(1, w):
                    acc = acc + pad_ref[base - j:base - j + CH, cols]
                cnt = _pool_count(c * CH, CH, w)
                pooled = acc / cnt - tok
                z = _dot(pooled, wp_ref[gi])
                dy = dy_ref[rows, cols]
                dsc_ref[:, cols] += jnp.sum(dy * z, axis=0, keepdims=True)
                dz = dy * sc_ref[:, cols]
                dwp_ref[gi] += _dot(pooled, dz, "tn")
                dpool = _dot(dz, wp_ref[gi], "nt")
                dp_ref[rows, cols] = dpool
                sc_pad_ref[rows, cols] = dpool / cnt
            for c in range(S // CH):
                rows = slice(c * CH, (c + 1) * CH)
                acc = sc_pad_ref[rows, cols]
                for j in range(1, w):
                    acc = acc + sc_pad_ref[c * CH + j:c * CH + j + CH, cols]
                dhp_ref[rows, cols] = (acc - dp_ref[rows, cols]).astype(dhp_ref.dtype)

    seq = pl.BlockSpec((S, POOL_WIDTH), lambda b: (b, 0))
    return pl.pallas_call(
        body,
        out_shape=(jax.ShapeDtypeStruct((B * S, POOL_WIDTH), BF16),
                   jax.ShapeDtypeStruct(w_pool.shape, F32), jax.ShapeDtypeStruct((1, POOL_WIDTH), F32)),
        grid=(B,),
        in_specs=[seq, seq, pl.BlockSpec(w_pool.shape, lambda b: (0, 0, 0)),
                  pl.BlockSpec((1, POOL_WIDTH), lambda b: (0, 0))],
        out_specs=(seq, pl.BlockSpec(w_pool.shape, lambda b: (0, 0, 0)),
                   pl.BlockSpec((1, POOL_WIDTH), lambda b: (0, 0))),
        scratch_shapes=[pltpu.VMEM((S + POOL_HALO, POOL_WIDTH), F32),
                        pltpu.VMEM((S + POOL_HALO, POOL_WIDTH), F32),
                        pltpu.VMEM((S, POOL_WIDTH), F32)],
        name="pool_bwd", compiler_params=_params(("arbitrary",)),
    )(proj, d_ypre, w_pool, scale)


def _ret_tables(S):
    half = RET_QK_DIM // 2
    inv = ROPE_BASE ** (-jnp.arange(half, dtype=F32) / half)
    ang = jnp.arange(S, dtype=F32)[:, None] * inv[None, :]
    cos, sin = jnp.cos(ang), jnp.sin(ang)
    cos_full = jnp.concatenate([cos, cos], axis=-1)
    sin_signed = jnp.concatenate([-sin, sin], axis=-1)
    C = RET_CHUNK
    lg = jnp.log1p(-jnp.exp2(-5.0 - jnp.arange(RET_HEADS, dtype=F32)))[:, None, None]
    idx = jnp.arange(C, dtype=F32)
    rel = idx[:, None] - idx[None, :]
    decay = jnp.where(rel >= 0, jnp.exp(jnp.maximum(rel, 0.0) * lg), 0.0)
    q_decay = jnp.broadcast_to(jnp.exp((idx + 1.0)[None, :, None] * lg), (RET_HEADS, C, RET_QK_DIM))
    k_decay = jnp.broadcast_to(jnp.exp((C - 1.0 - idx)[None, :, None] * lg), (RET_HEADS, C, RET_QK_DIM))
    c_decay = jnp.broadcast_to(jnp.exp(C * lg), (RET_HEADS, 1, RET_V_DIM))
    return cos_full, sin_signed, decay, q_decay, k_decay, c_decay


def _rope(x, cos_full, sin_signed):
    return x * cos_full + pltpu.roll(x, RET_QK_DIM // 2, axis=1) * sin_signed


def _rope_t(dy, cos_full, sin_signed):
    return dy * cos_full + pltpu.roll(dy * sin_signed, RET_QK_DIM // 2, axis=1)


RET_COLS = 512


def _ret_specs(N, chunk_of):
    C = RET_CHUNK

    def rows(width, col=0):
        return pl.BlockSpec((C, width), lambda b, i: (b * N + chunk_of(i), col))

    def whole(shape):
        return pl.BlockSpec(shape, lambda b, i: (0,) * len(shape))

    wide = RET_HEADS * RET_V_DIM
    return dict(
        q=rows(RET_COLS, COL_Q // RET_COLS), k=rows(RET_COLS, COL_K // RET_COLS),
        v=[rows(RET_COLS, COL_V // RET_COLS + j) for j in range(2)],
        gr=[rows(RET_COLS, COL_GR // RET_COLS + j) for j in range(2)],
        table=pl.BlockSpec((C, RET_QK_DIM), lambda b, i: (chunk_of(i), 0)),
        decay=whole((RET_HEADS, C, C)), qd=whole((RET_HEADS, C, RET_QK_DIM)), kd=whole((RET_HEADS, C, RET_QK_DIM)),
        cd=whole((RET_HEADS, 1, RET_V_DIM)), vec=whole((1, wide)), qk_rows=rows(RET_COLS), v_rows=rows(wide),
        state=pl.BlockSpec((None, None, RET_HEADS, RET_QK_DIM, RET_V_DIM), lambda b, i: (b, chunk_of(i), 0, 0, 0)))


def _head_cols(h):
    pair = slice((h % 2) * RET_V_DIM, (h % 2 + 1) * RET_V_DIM)
    return slice(h * RET_QK_DIM, (h + 1) * RET_QK_DIM), h // 2, pair, slice(h * RET_V_DIM, (h + 1) * RET_V_DIM)


def _group_norm(o):
    mu = jnp.mean(o, axis=-1, keepdims=True)
    oc = o - mu
    rstd = lax.rsqrt(jnp.mean(oc * oc, axis=-1, keepdims=True) + EPS)
    return oc * rstd, rstd


def _ret_fwd(proj, g_ret, b_ret, tables, B, S, comm=None):
    N = S // RET_CHUNK
    cos_t, sin_t, decay, q_decay, k_decay, c_decay = tables
    sp = _ret_specs(N, lambda i: i)

    def body(q_ref, k_ref, v0_ref, v1_ref, gr0_ref, gr1_ref, cos_ref, sin_ref, dec_ref, qd_ref, kd_ref, cd_ref,
             g_ref, b_ref, y_ref, rs_ref, r_ref):
        @pl.when(pl.program_id(1) == 0)
        def _():
            r_ref[...] = jnp.zeros_like(r_ref)

        cs, sn = cos_ref[...], sin_ref[...]
        heads = range(RET_HEADS)
        cols = [_head_cols(h) for h in heads]
        q = [_rope(q_ref[:, cols[h][0]], cs, sn) for h in heads]
        k = [_rope(k_ref[:, cols[h][0]], cs, sn) * (RET_QK_DIM ** -0.5) for h in heads]
        v = [(v0_ref, v1_ref)[cols[h][1]][:, cols[h][2]] for h in heads]
        R = [r_ref[h] for h in heads]
        s = [_dot(q[h], k[h], "nt") * dec_ref[h] for h in heads]
        o = [_dot(s[h], v[h]) + _dot(q[h] * qd_ref[h], R[h]) for h in heads]
        r_new = [cd_ref[h] * R[h] + _dot(k[h] * kd_ref[h], v[h], "tn") for h in heads]
        for h in heads:
            _, j, pair, wide = cols[h]
            rs_ref[h] = R[h]
            r_ref[h] = r_new[h]
            on, _ = _group_norm(o[h])
            gr = (gr0_ref, gr1_ref)[j][:, pair]
            y_ref[:, wide] = (gr * jax.nn.sigmoid(gr) * (on * g_ref[:, wide] + b_ref[:, wide])).astype(y_ref.dtype)

    state = jax.ShapeDtypeStruct((B, N, RET_HEADS, RET_QK_DIM, RET_V_DIM), F32)
    return _pcall(
        body, (proj,) * 6 + (cos_t, sin_t, decay, q_decay, k_decay, c_decay, g_ret, b_ret),
        name="ret_fwd", out_shape=(jax.ShapeDtypeStruct((B * S, RET_HEADS * RET_V_DIM), BF16), state), grid=(B, N),
        in_specs=[sp["q"], sp["k"], *sp["v"], *sp["gr"], sp["table"], sp["table"], sp["decay"], sp["qd"],
                  sp["kd"], sp["cd"], sp["vec"], sp["vec"]],
        out_specs=(sp["v_rows"], sp["state"]),
        scratch_shapes=[pltpu.VMEM((RET_HEADS, RET_QK_DIM, RET_V_DIM), F32)],
        sem=("parallel", "arbitrary"), comm=comm)


def _ret_bwd(proj, states, d_yr, g_ret, b_ret, tables, B, S, comm=None):
    N = S // RET_CHUNK
    cos_t, sin_t, decay, q_decay, k_decay, c_decay = tables
    sp = _ret_specs(N, lambda i: N - 1 - i)
    qk_scale = RET_QK_DIM ** -0.5

    def body(q_ref, k_ref, v0_ref, v1_ref, gr0_ref, gr1_ref, dy_ref, rs_ref, cos_ref, sin_ref, dec_ref, qd_ref,
             kd_ref, cd_ref, g_ref, b_ref, dq_ref, dk_ref, dv_ref, dgr_ref, dg_ref, db_ref, dr_ref):
        @pl.when((pl.program_id(0) == 0) & (pl.program_id(1) == 0))
        def _():
            dg_ref[...] = jnp.zeros_like(dg_ref)
            db_ref[...] = jnp.zeros_like(db_ref)

        @pl.when(pl.program_id(1) == 0)
        def _():
            dr_ref[...] = jnp.zeros_like(dr_ref)

        cs, sn = cos_ref[...], sin_ref[...]
        heads = range(RET_HEADS)
        cols = [_head_cols(h) for h in heads]
        q = [_rope(q_ref[:, cols[h][0]], cs, sn) for h in heads]
        k = [_rope(k_ref[:, cols[h][0]], cs, sn) * qk_scale for h in heads]
        v = [(v0_ref, v1_ref)[cols[h][1]][:, cols[h][2]] for h in heads]
        s = [_dot(q[h], k[h], "nt") * dec_ref[h] for h in heads]
        o = [_dot(s[h], v[h]) + _dot(q[h] * qd_ref[h], rs_ref[h]) for h in heads]
        do = []
        for h in heads:
            _, j, pair, wide = cols[h]
            on, rstd = _group_norm(o[h])
            g = g_ref[:, wide]
            oaff = on * g + b_ref[:, wide]
            gr = (gr0_ref, gr1_ref)[j][:, pair]
            sg = jax.nn.sigmoid(gr)
            dy = dy_ref[:, wide]
            dgr_ref[:, wide] = (dy * oaff * (sg * (1.0 + gr * (1.0 - sg)))).astype(dgr_ref.dtype)
            doaff = dy * (gr * sg)
            dg_ref[:, wide] += jnp.sum(doaff * on, axis=0, keepdims=True)
            db_ref[:, wide] += jnp.sum(doaff, axis=0, keepdims=True)
            don = doaff * g
            do.append(rstd * (don - jnp.mean(don, axis=-1, keepdims=True)
                              - on * jnp.mean(don * on, axis=-1, keepdims=True)))
        ds = [_dot(do[h], v[h], "nt") * dec_ref[h] for h in heads]
        dq = [_dot(ds[h], k[h]) + qd_ref[h] * _dot(do[h], rs_ref[h], "nt") for h in heads]
        dk = [_dot(ds[h], q[h], "tn") + kd_ref[h] * _dot(v[h], dr_ref[h], "nt") for h in heads]
        dv = [_dot(s[h], do[h], "tn") + _dot(k[h] * kd_ref[h], dr_ref[h]) for h in heads]
        dr = [cd_ref[h] * dr_ref[h] + _dot(q[h] * qd_ref[h], do[h], "tn") for h in heads]
        for h in heads:
            qk, _, _, wide = cols[h]
            dv_ref[:, wide] = dv[h].astype(dv_ref.dtype)
            dr_ref[h] = dr[h]
            dq_ref[:, qk] = _rope_t(dq[h], cs, sn).astype(dq_ref.dtype)
            dk_ref[:, qk] = _rope_t(dk[h] * qk_scale, cs, sn).astype(dk_ref.dtype)

    T = B * S
    qk_shape = jax.ShapeDtypeStruct((T, RET_HEADS * RET_QK_DIM), BF16)
    v_shape = jax.ShapeDtypeStruct((T, RET_HEADS * RET_V_DIM), BF16)
    vec_shape = jax.ShapeDtypeStruct((1, RET_HEADS * RET_V_DIM), F32)
    return _pcall(
        body, (proj,) * 6 + (d_yr, states, cos_t, sin_t, decay, q_decay, k_decay, c_decay, g_ret, b_ret),
        name="ret_bwd", out_shape=(qk_shape, qk_shape, v_shape, v_shape, vec_shape, vec_shape), grid=(B, N),
        in_specs=[sp["q"], sp["k"], *sp["v"], *sp["gr"], sp["v_rows"], sp["state"], sp["table"], sp["table"],
                  sp["decay"], sp["qd"], sp["kd"], sp["cd"], sp["vec"], sp["vec"]],
        out_specs=(sp["qk_rows"], sp["qk_rows"], sp["v_rows"], sp["v_rows"], sp["vec"], sp["vec"]),
        scratch_shapes=[pltpu.VMEM((RET_HEADS, RET_QK_DIM, RET_V_DIM), F32)],
        sem=("arbitrary", "arbitrary"), comm=comm)


def _xa_rows(S):
    return _tile(S, 256)


def _xa_specs(S, M):
    q = pl.BlockSpec((S, XA_HEAD_DIM), lambda b, h: (b, COL_QX // XA_HEAD_DIM + h))
    k = pl.BlockSpec((M, XA_HEAD_DIM), lambda b, h: (b, h))
    v = pl.BlockSpec((M, XA_HEAD_DIM), lambda b, h: (b, XA_HEADS + h))
    o = pl.BlockSpec((S, XA_HEAD_DIM), lambda b, h: (b, h))
    return q, k, v, o


def _softmax_rows(s):
    e = jnp.exp(s - jnp.max(s, axis=-1, keepdims=True))
    return e / jnp.sum(e, axis=-1, keepdims=True)


def _xa_fwd(proj, kv, B, S, M, comm=None):
    CH = _xa_rows(S)
    q_spec, k_spec, v_spec, o_spec = _xa_specs(S, M)

    def body(q_ref, k_ref, v_ref, o_ref):
        def chunk(i, carry):
            rows = pl.ds(pl.multiple_of(i * CH, CH), CH)
            p = _softmax_rows(_dot(q_ref[rows, :], k_ref[...], "nt") * (XA_HEAD_DIM ** -0.5))
            o_ref[rows, :] = _dot(p, v_ref[...]).astype(o_ref.dtype)
            return carry

        lax.fori_loop(0, S // CH, chunk, 0, unroll=True)

    return _pcall(
        body, (proj, kv, kv), name="xattn_fwd", out_shape=jax.ShapeDtypeStruct((B * S, XA_WIDTH), BF16),
        grid=(B, XA_HEADS), in_specs=[q_spec, k_spec, v_spec], out_specs=o_spec,
        sem=("parallel", "parallel"), comm=comm)


def _xa_bwd(proj, kv, d_o, B, S, M, comm=None):
    CH = _xa_rows(S)
    q_spec, k_spec, v_spec, o_spec = _xa_specs(S, M)
    scale = XA_HEAD_DIM ** -0.5

    def body(q_ref, k_ref, v_ref, do_ref, dq_ref, dk_ref, dv_ref):
        dk_ref[...] = jnp.zeros_like(dk_ref)
        dv_ref[...] = jnp.zeros_like(dv_ref)

        def chunk(i, carry):
            rows = pl.ds(pl.multiple_of(i * CH, CH), CH)
            q, do = q_ref[rows, :], do_ref[rows, :]
            p = _softmax_rows(_dot(q, k_ref[...], "nt") * scale)
            dp = _dot(do, v_ref[...], "nt")
            ds = p * (dp - jnp.sum(dp * p, axis=-1, keepdims=True)) * scale
            dq_ref[rows, :] = _dot(ds, k_ref[...]).astype(dq_ref.dtype)
            dk_ref[...] += _dot(ds, q, "tn")
            dv_ref[...] += _dot(p, do, "tn")
            return carry

        lax.fori_loop(0, S // CH, chunk, 0, unroll=True)

    kv_out = pl.BlockSpec((M, XA_HEAD_DIM), lambda b, h: (b, h))
    return _pcall(
        body, (proj, kv, kv, d_o), name="xattn_bwd",
        out_shape=(jax.ShapeDtypeStruct((B * S, XA_WIDTH), BF16), jax.ShapeDtypeStruct((B * M, XA_WIDTH), F32),
                   jax.ShapeDtypeStruct((B * M, XA_WIDTH), F32)),
        grid=(B, XA_HEADS), in_specs=[q_spec, k_spec, v_spec, o_spec], out_specs=(o_spec, kv_out, kv_out),
        sem=("parallel", "parallel"), comm=comm)


def _gate_specs(tm):
    n = COL_GL // D_MODEL
    return [pl.BlockSpec((tm, D_MODEL), lambda i, j=j: (i, n + j)) for j in range(3)]


def _merge_fwd(proj, ys, tm=512, comm=None):
    T = proj.shape[0]
    tm = _tile(T, tm)
    row = pl.BlockSpec((tm, D_MODEL), lambda i: (i, 0))

    def body(g0, g1, g2, y0, y1, y2, o_ref):
        acc = jax.nn.sigmoid(g0[...]) * y0[...]
        acc = acc + jax.nn.sigmoid(g1[...]) * y1[...]
        acc = acc + jax.nn.sigmoid(g2[...]) * y2[...]
        o_ref[...] = acc.astype(o_ref.dtype)

    return _pcall(
        body, (proj, proj, proj, *ys), name="merge_fwd", out_shape=jax.ShapeDtypeStruct((T, D_MODEL), BF16),
        grid=(T // tm,), in_specs=_gate_specs(tm) + [row] * 3, out_specs=row, sem=("parallel",), comm=comm)


def _merge_bwd(proj, ys, d_merged, tm=512, comm=None):
    T = proj.shape[0]
    tm = _tile(T, tm)
    row = pl.BlockSpec((tm, D_MODEL), lambda i: (i, 0))

    def body(g0, g1, g2, y0, y1, y2, dm_ref, dgl_ref, d0, d1, d2):
        dm = dm_ref[...]
        for j, (g_ref, y_ref, d_ref) in enumerate(((g0, y0, d0), (g1, y1, d1), (g2, y2, d2))):
            sg = jax.nn.sigmoid(g_ref[...])
            d_ref[...] = (dm * sg).astype(d_ref.dtype)
            dgl_ref[:, j * D_MODEL:(j + 1) * D_MODEL] = (dm * y_ref[...] * sg * (1.0 - sg)).astype(dgl_ref.dtype)

    dy = jax.ShapeDtypeStruct((T, D_MODEL), BF16)
    return _pcall(
        body, (proj, proj, proj, *ys, d_merged), name="merge_bwd",
        out_shape=(jax.ShapeDtypeStruct((T, 3 * D_MODEL), BF16), dy, dy, dy), grid=(T // tm,),
        in_specs=_gate_specs(tm) + [row] * 4,
        out_specs=(pl.BlockSpec((tm, 3 * D_MODEL), lambda i: (i, 0)), row, row, row),
        sem=("parallel",), comm=comm)


def _gelu(x):
    return 0.5 * x * (1.0 + jnp.tanh(GELU_C * (x + GELU_A * x * x * x)))


def _gelu_grad(x):
    t = jnp.tanh(GELU_C * (x + GELU_A * x * x * x))
    return 0.5 * (1.0 + t) + 0.5 * x * (1.0 - t * t) * GELU_C * (1.0 + 3.0 * GELU_A * x * x)


GLU_HALO = 16


def _shift_down(x, prev, n):
    last = prev.shape[0]
    r = lax.broadcasted_iota(jnp.int32, (8, 1), 0)
    rolled = pltpu.roll(x, n, axis=0)
    head = rolled[0:8]
    for j in range(n):
        head = jnp.where(r == j, prev[last - n + j:last - n + j + 1, :], head)
    return jnp.concatenate([head, rolled[8:]], axis=0)


def _shift_up(x, nxt, n):
    rows = x.shape[0]
    r = lax.broadcasted_iota(jnp.int32, (8, 1), 0)
    rolled = pltpu.roll(x, rows - n, axis=0)
    tail = rolled[rows - 8:]
    for j in range(n):
        tail = jnp.where(r == 8 - n + j, nxt[j:j + 1, :], tail)
    return jnp.concatenate([rolled[:rows - 8], tail], axis=0)


def _conv(a, prev, cw, cb):
    return _shift_down(a, prev, 2) * cw[0:1, :] + _shift_down(a, prev, 1) * cw[1:2, :] + a * cw[2:3, :] + cb


def _glu_fwd(up, cw, cb, S, tm=1024, comm=None):
    T = up.shape[2]
    tm = _tile(S, tm)
    per_seq = S // tm

    def body(ab_ref, prev_ref, cw_ref, cb_ref, u_ref):
        i = pl.program_id(1)
        prev = jnp.where(i % per_seq == 0, 0.0, prev_ref[...].astype(F32))
        ac = _conv(ab_ref[0].astype(F32), prev, cw_ref[...], cb_ref[...])
        u_ref[...] = (_gelu(ac) * ab_ref[1].astype(F32)).astype(u_ref.dtype)

    before = tm // GLU_HALO
    return _pcall(
        body, (up, up, cw, cb), name="glu_fwd",
        out_shape=jax.ShapeDtypeStruct((FFN_SLABS, T, UP_SHARD), BF16), grid=(FFN_SLABS, T // tm),
        in_specs=[pl.BlockSpec((2, None, tm, UP_SHARD), lambda d, i: (0, d, i, 0)),
                  pl.BlockSpec((None, None, GLU_HALO, UP_SHARD),
                               lambda d, i: (0, d, jnp.maximum(i * before - 1, 0), 0)),
                  pl.BlockSpec((None, 3, UP_SHARD), lambda d, i: (d, 0, 0)),
                  pl.BlockSpec((None, 1, UP_SHARD), lambda d, i: (d, 0, 0))],
        out_specs=pl.BlockSpec((None, tm, UP_SHARD), lambda d, i: (d, i, 0)),
        sem=("parallel", "parallel"), comm=comm)


def _glu_bwd(up, d_u, cw, cb, S, tm=512, comm=None):
    T = up.shape[2]
    tm = _tile(S, tm)
    per_seq = S // tm
    n_tiles = T // tm
    per_tile = tm // GLU_HALO

    def body(ab_ref, prev_ref, abn_ref, du_ref, dun_ref, cw_ref, cb_ref, dup_ref, dcw_ref, dcb_ref):
        i = pl.program_id(1)

        @pl.when(i == 0)
        def _():
            dcw_ref[...] = jnp.zeros_like(dcw_ref)
            dcb_ref[...] = jnp.zeros_like(dcb_ref)

        cw, cb = cw_ref[...], cb_ref[...]
        a, b = ab_ref[0].astype(F32), ab_ref[1].astype(F32)
        prev = jnp.where(i % per_seq == 0, 0.0, prev_ref[...].astype(F32))
        a2, a1 = _shift_down(a, prev, 2), _shift_down(a, prev, 1)
        ac = a2 * cw[0:1, :] + a1 * cw[1:2, :] + a * cw[2:3, :] + cb
        du = du_ref[...].astype(F32)
        dup_ref[1] = (du * _gelu(ac)).astype(dup_ref.dtype)
        dac = du * b * _gelu_grad(ac)
        dcb_ref[...] += jnp.sum(dac, axis=0, keepdims=True)
        dcw_ref[0:1, :] += jnp.sum(dac * a2, axis=0, keepdims=True)
        dcw_ref[1:2, :] += jnp.sum(dac * a1, axis=0, keepdims=True)
        dcw_ref[2:3, :] += jnp.sum(dac * a, axis=0, keepdims=True)
        acn = _conv(abn_ref[0].astype(F32), a[tm - GLU_HALO:, :], cw, cb)
        dacn = jnp.where(i % per_seq == per_seq - 1, 0.0,
                         dun_ref[...].astype(F32) * abn_ref[1].astype(F32) * _gelu_grad(acn))
        da = dac * cw[2:3, :] + _shift_up(dac, dacn, 1) * cw[1:2, :] + _shift_up(dac, dacn, 2) * cw[0:1, :]
        dup_ref[0] = da.astype(dup_ref.dtype)

    def nxt(i):
        return jnp.minimum((i + 1) * per_tile, T // GLU_HALO - 1)

    return _pcall(
        body, (up, up, up, d_u, d_u, cw, cb), name="glu_bwd",
        out_shape=(jax.ShapeDtypeStruct((2, FFN_SLABS, T, UP_SHARD), BF16),
                   jax.ShapeDtypeStruct((FFN_SLABS, 3, UP_SHARD), F32),
                   jax.ShapeDtypeStruct((FFN_SLABS, 1, UP_SHARD), F32)),
        grid=(FFN_SLABS, n_tiles),
        in_specs=[pl.BlockSpec((2, None, tm, UP_SHARD), lambda d, i: (0, d, i, 0)),
                  pl.BlockSpec((None, None, GLU_HALO, UP_SHARD),
                               lambda d, i: (0, d, jnp.maximum(i * per_tile - 1, 0), 0)),
                  pl.BlockSpec((2, None, GLU_HALO, UP_SHARD), lambda d, i: (0, d, nxt(i), 0)),
                  pl.BlockSpec((None, tm, UP_SHARD), lambda d, i: (d, i, 0)),
                  pl.BlockSpec((None, GLU_HALO, UP_SHARD), lambda d, i: (d, nxt(i), 0)),
                  pl.BlockSpec((None, 3, UP_SHARD), lambda d, i: (d, 0, 0)),
                  pl.BlockSpec((None, 1, UP_SHARD), lambda d, i: (d, 0, 0))],
        out_specs=(pl.BlockSpec((2, None, tm, UP_SHARD), lambda d, i: (0, d, i, 0)),
                   pl.BlockSpec((None, 3, UP_SHARD), lambda d, i: (d, 0, 0)),
                   pl.BlockSpec((None, 1, UP_SHARD), lambda d, i: (d, 0, 0))),
        sem=("parallel", "arbitrary"), comm=comm)


def _mm_up(h2, w_up_t, tm=MM_ROWS):
    T, K = h2.shape
    tm = _tile(T, tm)
    return _matmul(
        "mm_up", "nt", h2, w_up_t, jax.ShapeDtypeStruct((N_DEV, T, UP_SHARD), BF16), (N_DEV, T // tm, 1),
        pl.BlockSpec((tm, K), lambda j, i, k: (i, 0)), pl.BlockSpec((None, UP_SHARD, K), lambda j, i, k: (j, 0, 0)),
        pl.BlockSpec((None, tm, UP_SHARD), lambda j, i, k: (j, i, 0)), (tm, UP_SHARD))


def _loss_epilogue(ffn, operands, outputs, first):
    x1_ref, t_ref, g_ref = operands
    dx_ref, dg_ref, loss_ref = outputs

    @pl.when(first)
    def _():
        dg_ref[...] = jnp.zeros_like(dg_ref)
        loss_ref[...] = jnp.zeros_like(loss_ref)

    xv = x1_ref[...] + ffn
    r = lax.rsqrt(jnp.mean(xv * xv, axis=-1, keepdims=True) + EPS)
    xhat = xv * r
    err = xhat * g_ref[...] - t_ref[...]
    loss_ref[...] += (0.5 / D_MODEL) * jnp.sum(err * err)
    dy = err * (1.0 / D_MODEL)
    dg_ref[...] += jnp.sum(dy * xhat, axis=0, keepdims=True)
    dxhat = dy * g_ref[...]
    dx_ref[...] = r * (dxhat - xhat * jnp.mean(dxhat * xhat, axis=-1, keepdims=True))


def _mm_down_loss(u, w_down, x1, target, g_final, tm=MM_ROWS_RES):
    J, T, n = u.shape
    tm = _tile(T, tm)
    row = pl.BlockSpec((tm, D_MODEL), lambda i, d: (i, 0))
    vec = pl.BlockSpec((1, D_MODEL), lambda i, d: (0, 0))
    vec_shape = jax.ShapeDtypeStruct((1, D_MODEL), F32)
    return _matmul(
        "mm_down", "nn", u, w_down, (jax.ShapeDtypeStruct((T, D_MODEL), F32), vec_shape, vec_shape), (T // tm, J),
        pl.BlockSpec((None, tm, n), lambda i, d: (d, i, 0)), pl.BlockSpec((None, n, D_MODEL), lambda i, d: (d, 0, 0)),
        (row, vec, vec), (tm, D_MODEL), [x1, target, g_final], [row, row, vec], epilogue=_loss_epilogue)


def _mm_down_t(dx, w_down, tm=MM_ROWS):
    T = dx.shape[0]
    J, n, _ = w_down.shape
    tm = _tile(T, tm)
    return _matmul(
        "mm_down_t", "nt", dx, w_down, jax.ShapeDtypeStruct((J, T, n), BF16), (J, T // tm, 1),
        pl.BlockSpec((tm, D_MODEL), lambda d, i, k: (i, 0)), pl.BlockSpec((None, n, D_MODEL), lambda d, i, k: (d, 0, 0)),
        pl.BlockSpec((None, tm, n), lambda d, i, k: (d, i, 0)), (tm, n))


def _mm_dw_down(u, dx, tk=MM_TOKENS):
    J, T, n = u.shape
    tk = _tile(T, tk)
    return _matmul(
        "mm_dw_down", "tn", u, dx, jax.ShapeDtypeStruct((J, n, D_MODEL), BF16), (J, T // tk),
        pl.BlockSpec((None, tk, n), lambda d, k: (d, k, 0)), pl.BlockSpec((tk, D_MODEL), lambda d, k: (k, 0)),
        pl.BlockSpec((None, n, D_MODEL), lambda d, k: (d, 0, 0)), (n, D_MODEL))


def _mm_dw_up(h2, d_up, tk=MM_TOKENS):
    T, K = h2.shape
    tk = _tile(T, tk)
    return _matmul(
        "mm_dw_up", "tn", d_up, h2, jax.ShapeDtypeStruct((N_DEV, UP_SHARD, K), BF16), (N_DEV, T // tk),
        pl.BlockSpec((None, tk, UP_SHARD), lambda j, k: (j, k, 0)), pl.BlockSpec((tk, K), lambda j, k: (k, 0)),
        pl.BlockSpec((None, UP_SHARD, K), lambda j, k: (j, 0, 0)), (UP_SHARD, K))


def _mm_up_t(d_up, w_up_t, rms, tm=MM_ROWS_RES, comm=None):
    J, T, n = d_up.shape
    K = w_up_t.shape[2]
    tm = _tile(T, tm)
    fused = _rms_bwd_fused(T, K, tm, rms)
    return _matmul(
        "mm_up_t", "nn", d_up, w_up_t, fused.pop("out_shape"), (T // tm, J),
        pl.BlockSpec((None, tm, n), lambda i, j: (j, i, 0)), pl.BlockSpec((None, n, K), lambda i, j: (j, 0, 0)),
        fused.pop("o_spec"), (tm, K), comm=comm, **fused)


def _cast_shards(shards):
    def body(*refs):
        n = len(refs) // 2
        for src, dst in zip(refs[:n], refs[n:]):
            dst[...] = src[...].astype(dst.dtype)

    return pl.pallas_call(
        body, out_shape=[jax.ShapeDtypeStruct(s.shape, BF16) for s in shards], name="cast_shards",
        compiler_params=pltpu.CompilerParams(vmem_limit_bytes=VMEM_LIMIT),
    )(*shards)


def _adamw(w, g, m, v):
    m = ADAM_B1 * m + (1.0 - ADAM_B1) * g
    v = ADAM_B2 * v + (1.0 - ADAM_B2) * (g * g)
    m_hat = m / (1.0 - ADAM_B1 ** ADAM_STEP)
    v_hat = v / (1.0 - ADAM_B2 ** ADAM_STEP)
    delta = -ADAM_LR * (m_hat / (jnp.sqrt(v_hat) + ADAM_EPS) + ADAM_WD * w)
    return delta, m, v


def _sum_parts(p_ref):
    g = p_ref[0].astype(F32)
    for d in range(1, N_DEV):
        g = g + p_ref[d].astype(F32)
    return g


def _reduce_adam(name, parts, w, m, v, tr=128):
    R, Cn = w.shape
    by_rows = sum(p.shape[1] for p in parts) == R and len(parts) > 1
    tr = math.gcd(tr, *[p.shape[1] for p in parts])
    n_tiles = [p.shape[1] // tr for p in parts]
    first = [sum(n_tiles[:j]) for j in range(len(parts))] if by_rows else [0] * len(parts)

    def body(*refs):
        p_refs = refs[:len(parts)]
        w_ref, m_ref, v_ref, g_out, d_out, m_out, v_out = refs[len(parts):]

        def update(p_ref):
            g = _sum_parts(p_ref)
            delta, m_new, v_new = _adamw(w_ref[...], g, m_ref[...], v_ref[...])
            g_out[...] = g
            d_out[...] = delta
            m_out[...] = m_new
            v_out[...] = v_new

        if len(parts) == 1:
            update(p_refs[0])
        elif by_rows:
            i = pl.program_id(0)
            for p_ref, t0, n in zip(p_refs, first, n_tiles):
                pl.when((i >= t0) & (i < t0 + n))(functools.partial(update, p_ref))
        else:
            c = lax.axis_index("c")
            for side, p_ref in enumerate(p_refs):
                pl.when(c == side)(functools.partial(update, p_ref))

    def part_spec(t0, n):
        return pl.BlockSpec((N_DEV, tr, Cn), lambda i: (0, jnp.clip(i - t0, 0, n - 1), 0))

    row = pl.BlockSpec((tr, Cn), lambda i: (i, 0))
    shape = jax.ShapeDtypeStruct((R, Cn), F32)
    return pl.pallas_call(
        body, out_shape=(shape,) * 4, grid=(R // tr,),
        in_specs=[part_spec(t0, n) for t0, n in zip(first, n_tiles)] + [row, row, row],
        out_specs=(row,) * 4, name=name, compiler_params=_params(("parallel",)),
    )(*parts, w, m, v)


def _small_adam(name, gathered, params):
    n_g, n_p = len(gathered), len(params)

    def body(*refs):
        g_refs = refs[:n_g]
        wmv = refs[n_g:n_g + 3 * n_p]
        sums = refs[n_g + 3 * n_p:2 * n_g + 3 * n_p]
        upd = refs[2 * n_g + 3 * n_p:]
        for j in range(n_g):
            g = _sum_parts(g_refs[j])
            sums[j][...] = g
            if j < n_p:
                w_ref, m_ref, v_ref = wmv[3 * j:3 * j + 3]
                delta, m_new, v_new = _adamw(w_ref[...], g, m_ref[...], v_ref[...])
                upd[3 * j][...] = delta
                upd[3 * j + 1][...] = m_new
                upd[3 * j + 2][...] = v_new

    flat = [a for wmv in params for a in wmv]
    out_shape = [jax.ShapeDtypeStruct(g.shape[1:], F32) for g in gathered]
    out_shape += [jax.ShapeDtypeStruct(a.shape, F32) for a in flat]
    res = pl.pallas_call(body, out_shape=out_shape, name=name)(*gathered, *flat)
    return res[:n_g], [tuple(res[n_g + 3 * j:n_g + 3 * j + 3]) for j in range(n_p)]


def _adam_only(name, g, w, m, v):
    def body(g_ref, w_ref, m_ref, v_ref, d_out, m_out, v_out):
        delta, m_new, v_new = _adamw(w_ref[...], g_ref[...], m_ref[...], v_ref[...])
        d_out[...] = delta
        m_out[...] = m_new
        v_out[...] = v_new

    shape = jax.ShapeDtypeStruct(w.shape, F32)
    return pl.pallas_call(body, out_shape=(shape,) * 3, name=name)(g, w, m, v)


def kernel(x, mem, g_mix, w_in, w_pool, pool_scale, w_a, g_ret, b_ret, w_r, g_mem, w_mem_kv, w_c, w_out, g_ffn, w_up, conv_w, conv_b, w_down, g_final, loss_target, m_g_mix, m_w_in, m_w_pool, m_pool_scale, m_w_a, m_g_ret, m_b_ret, m_w_r, m_g_mem, m_w_mem_kv, m_w_c, m_w_out, m_g_ffn, m_w_up, m_conv_w, m_conv_b, m_w_down, m_g_final, v_g_mix, v_w_in, v_w_pool, v_pool_scale, v_w_a, v_g_ret, v_b_ret, v_w_r, v_g_mem, v_w_mem_kv, v_w_c, v_w_out, v_g_ffn, v_w_up, v_conv_w, v_conv_b, v_w_down, v_g_final):
    B, S, _ = x.shape
    M = mem.shape[1]
    T = B * S
    me = _my_index()
    x2d = x.reshape(T, D_MODEL)
    mem2d = mem.reshape(B * M, D_MODEL)
    tgt2d = loss_target.reshape(T, D_MODEL)
    g_final2 = g_final.reshape(1, D_MODEL)

    big = dict(w_in=w_in[0], w_a=w_a[0], w_r=w_r[0], w_mem_kv=w_mem_kv[0], w_c=w_c[0], w_out=w_out[0],
               w_up=w_up[0].T, w_down=w_down[0])
    names = list(big)
    cast = dict(zip(names, _cast_shards([big[n] for n in names])))
    cb = conv_b[0].reshape(FFN_SLABS, 1, UP_SHARD)
    wp = w_pool[0]
    tables = _ret_tables(S)

    h = _rms_fwd("rms_mix", x2d, g_mix)
    early = ("w_a", "w_r", "w_mem_kv", "w_c", "w_out")
    (proj, Win), landed = _mm_in_gather(h, cast["w_in"], comm=_Gather([cast[n] for n in early] + [conv_w[0]]))
    W = dict(zip(early, landed))
    (yr, ret_states), (Wup,) = _ret_fwd(proj, g_ret, b_ret, tables, B, S, comm=_Gather([cast["w_up"]]))
    cw_full = landed[-1].transpose(1, 0, 2).reshape(3, FFN_HIDDEN)
    cw = cw_full.reshape(3, FFN_SLABS, UP_SHARD).transpose(1, 0, 2)
    Wa = W["w_a"].transpose(1, 0, 2).reshape(POOL_WIDTH, D_MODEL)
    Wc = W["w_c"].transpose(1, 0, 2).reshape(XA_WIDTH, D_MODEL)
    Wr = W["w_r"].reshape(D_MODEL, D_MODEL)
    Wkv = W["w_mem_kv"].reshape(D_MODEL, D_MODEL)
    Wout = W["w_out"].reshape(D_MODEL, D_MODEL)
    ypre = _pool_fwd(proj, wp, pool_scale, B, S)
    y_pool = _mm_rows("mm_a", ypre, Wa, BF16)
    y_ret = _mm_rows("mm_r", yr, Wr, BF16)
    mem_n = _rms_fwd("rms_mem", mem2d, g_mem)
    kv = _mm_rows("mm_kv", mem_n, Wkv)
    o_mem, (Wdown,) = _xa_fwd(proj, kv, B, S, M, comm=_Gather([cast["w_down"]]))
    Wdown = Wdown.reshape(FFN_SLABS, UP_SHARD, D_MODEL)
    y_mem = _mm_rows("mm_c", o_mem, Wc, BF16)
    ys = (y_pool, y_ret, y_mem)
    merged = _merge_fwd(proj, ys)[0]
    x1, h2 = _mm_residual_rms("mm_out", merged, Wout, x2d, g_ffn)
    up = _mm_up(h2, Wup).reshape(2, FFN_SLABS, T, UP_SHARD)
    u = _glu_fwd(up, cw, cb, S)[0]

    dx2, dg_final, loss_part = _mm_down_loss(u, Wdown, x1, tgt2d, g_final2)
    received = {}
    d_u = _mm_down_t(dx2, Wdown)
    dW_down = _mm_dw_down(u, dx2)
    (d_up, d_cw, d_cb), (received["w_down"],) = _glu_bwd(
        up, d_u, cw, cb, S, comm=_Exchange([dW_down.reshape(N_DEV, -1, D_MODEL)]))
    d_up = d_up.reshape(N_DEV, T, UP_SHARD)
    dW_up = _mm_dw_up(h2, d_up)
    (dx1, dg_ffn), (up_c0,) = _mm_up_t(d_up, Wup, (x1, g_ffn, dx2), comm=_ExchangeTo([dW_up], 0))
    d_merged = _mm_rows("mm_out_t", dx1, Wout, kind="nt")
    dW_out = _mm_tn("mm_dw_out", merged, dx1, BF16)
    (d_gl, d_y_pool, d_y_ret, d_y_mem), (received["w_out"],) = _merge_bwd(
        proj, ys, d_merged, comm=_Exchange([dW_out.reshape(N_DEV, -1, D_MODEL)]))
    dW_c = _mm_tn("mm_dw_c", o_mem, d_y_mem, BF16)
    d_o_mem = _mm_rows("mm_c_t", d_y_mem, Wc, kind="nt")
    (d_qx, d_kmem, d_vmem), (up_c1,) = _xa_bwd(proj, kv, d_o_mem, B, S, M, comm=_ExchangeTo([dW_up], 1))
    received["w_up"] = [up_c0, up_c1]
    d_kv = jnp.concatenate([d_kmem, d_vmem], axis=1)
    dW_kv = _mm_tn("mm_dw_kv", mem_n, d_kv, BF16)
    d_mem_n = _mm_rows("mm_kv_t", d_kv, Wkv, kind="nt")
    dg_mem = _rms_bwd("rms_mem_bwd", mem2d, g_mem, d_mem_n, None)
    dW_a = _mm_tn("mm_dw_a", ypre, d_y_pool, BF16)
    d_ypre = _mm_rows("mm_a_t", d_y_pool, Wa, kind="nt")
    d_hp, dw_pool, d_scale = _pool_bwd(proj, d_ypre, wp, pool_scale, B, S)
    dW_r = _mm_tn("mm_dw_r", yr, d_y_ret, BF16)
    d_yr = _mm_rows("mm_r_t", d_y_ret, Wr, kind="nt")
    (d_q, d_k, d_v, d_gr, dg_ret, db_ret), landed = _ret_bwd(
        proj, ret_states, d_yr, g_ret, b_ret, tables, B, S,
        comm=_Exchange([dW_a.reshape(POOL_WIDTH, N_DEV, -1).transpose(1, 0, 2), dW_r.reshape(N_DEV, -1, D_MODEL),
                        dW_c.reshape(XA_WIDTH, N_DEV, -1).transpose(1, 0, 2), dW_kv.reshape(N_DEV, -1, D_MODEL)]))
    received["w_a"], received["w_r"], received["w_c"], received["w_mem_kv"] = landed
    small_names = ["w_pool", "pool_scale", "g_ret", "b_ret", "g_mem", "g_ffn", "conv_b", "g_final"]
    small_grads = [dw_pool, d_scale, dg_ret, db_ret, dg_mem, dg_ffn, d_cb.reshape(1, FFN_HIDDEN), dg_final,
                   d_cw.transpose(1, 0, 2).reshape(3, FFN_HIDDEN), loss_part]
    d_proj = jnp.concatenate([d_hp, d_q, d_k, d_v, d_gr, d_qx, d_gl], axis=1)
    dW_in0, small_all = _mm_tn_slab("mm_dw_in0", h[:, :W_IN_FIRST_ROWS], d_proj, IN_SHARD, BF16,
                                    comm=_Exchange([], whole=small_grads))
    dW_in1, (in0,) = _mm_tn_slab("mm_dw_in1", h[:, W_IN_FIRST_ROWS:], d_proj, IN_SHARD, BF16,
                                 comm=_Exchange([dW_in0]))
    (grad_x, dg_mix), (in1,) = _mm_cols_slab_t("mm_in_t", d_proj, Win, (x2d, g_mix, dx1), comm=_Exchange([dW_in1]))
    received["w_in"] = [in0, in1]
    (g_mix_all,) = _comm_call("gather_g_mix", _Exchange([], whole=[dg_mix]))

    args = dict(g_mix=g_mix, w_in=w_in, w_pool=w_pool, pool_scale=pool_scale, w_a=w_a, g_ret=g_ret, b_ret=b_ret,
                w_r=w_r, g_mem=g_mem, w_mem_kv=w_mem_kv, w_c=w_c, w_out=w_out, g_ffn=g_ffn, w_up=w_up,
                conv_w=conv_w, conv_b=conv_b, w_down=w_down, g_final=g_final)
    m_in = dict(g_mix=m_g_mix, w_in=m_w_in, w_pool=m_w_pool, pool_scale=m_pool_scale, w_a=m_w_a, g_ret=m_g_ret,
                b_ret=m_b_ret, w_r=m_w_r, g_mem=m_g_mem, w_mem_kv=m_w_mem_kv, w_c=m_w_c, w_out=m_w_out,
                g_ffn=m_g_ffn, w_up=m_w_up, conv_w=m_conv_w, conv_b=m_conv_b, w_down=m_w_down, g_final=m_g_final)
    v_in = dict(g_mix=v_g_mix, w_in=v_w_in, w_pool=v_w_pool, pool_scale=v_pool_scale, w_a=v_w_a, g_ret=v_g_ret,
                b_ret=v_b_ret, w_r=v_w_r, g_mem=v_g_mem, w_mem_kv=v_w_mem_kv, w_c=v_w_c, w_out=v_w_out,
                g_ffn=v_g_ffn, w_up=v_w_up, conv_w=v_conv_w, conv_b=v_conv_b, w_down=v_w_down, g_final=v_g_final)

    grads, deltas, new_m, new_v = {}, {}, {}, {}
    for n in names:
        parts = received[n] if isinstance(received[n], list) else [received[n]]
        flip = (lambda a: a.T) if n == "w_up" else (lambda a: a)
        outs = _reduce_adam("adam_" + n, parts, big[n], flip(m_in[n][0]), flip(v_in[n][0]))
        for store, val in zip((grads, deltas, new_m, new_v), outs):
            store[n] = flip(val)[None]

    def as_small(a):
        return a.reshape(a.shape[-3:]) if a.ndim > 2 else a.reshape(1, -1)

    def small_update(call_name, param_names, gathered):
        params = [tuple(as_small(d[n]) for d in (args, m_in, v_in)) for n in param_names]
        sums, updates = _small_adam(call_name, gathered, params)
        for n, g, (d_, m_, v_) in zip(param_names, sums, updates):
            shape = args[n].shape
            grads[n], deltas[n], new_m[n], new_v[n] = (a.reshape(shape) for a in (g, d_, m_, v_))
        return sums[len(param_names):]

    g_cw_full, loss_row = small_update("adam_small", small_names, small_all)
    loss = loss_row[0, 0]
    small_update("adam_g_mix", ["g_mix"], [g_mix_all])

    shard_cols = FFN_HIDDEN // N_DEV
    g_cw = lax.dynamic_slice_in_dim(g_cw_full, me * shard_cols, shard_cols, axis=1)
    d_, m_, v_ = _adam_only("adam_conv_w", g_cw, conv_w[0], m_conv_w[0], v_conv_w[0])
    grads["conv_w"], deltas["conv_w"], new_m["conv_w"], new_v["conv_w"] = g_cw[None], d_[None], m_[None], v_[None]

    order = ["g_mix", "w_in", "w_pool", "pool_scale", "w_a", "g_ret", "b_ret", "w_r", "g_mem", "w_mem_kv", "w_c",
             "w_out", "g_ffn", "w_up", "conv_w", "conv_b", "w_down", "g_final"]
    return (loss, grad_x.reshape(B, S, D_MODEL), *[grads[n] for n in order], *[deltas[n] for n in order],
            *[new_m[n] for n in order], *[new_v[n] for n in order])
```

```python
import functools
import math

import jax
import jax.numpy as jnp
from jax import lax
from jax.experimental import pallas as pl
from jax.experimental.pallas import tpu as pltpu

F32 = jnp.float32
BF16 = jnp.bfloat16

N_DEV = 8
D_MODEL = 1024
POOL_WINDOWS = (2, 4, 8, 16)
POOL_GROUP_DIM = 128
POOL_WIDTH = 512
POOL_HALO = 16
RET_HEADS = 4
RET_QK_DIM = 128
RET_V_DIM = 256
RET_CHUNK = 128
ROPE_BASE = 10000.0
XA_HEADS = 4
XA_HEAD_DIM = 128
XA_WIDTH = 512
IN_WIDTH = 7168
IN_SHARD = IN_WIDTH // N_DEV
FFN_HIDDEN = 2816
UP_SHARD = 2 * FFN_HIDDEN // N_DEV
FFN_SLABS = FFN_HIDDEN // UP_SHARD
EPS = 1e-6
ADAM_LR = 0.001
ADAM_B1 = 0.9
ADAM_B2 = 0.999
ADAM_EPS = 1e-08
ADAM_WD = 0.01
ADAM_STEP = 10
GELU_C = math.sqrt(2.0 / math.pi)
GELU_A = 0.044715
VMEM_LIMIT = 56 * 1024 * 1024
MM_ROWS = 2048
MM_ROWS_RES = 1024
MM_TOKENS = 2048
W_IN_FIRST_ROWS = 384
MESH = pl.DeviceIdType.MESH

COL_Q, COL_K, COL_V, COL_GR, COL_QX, COL_GL = 512, 1024, 1536, 2560, 3584, 4096

_DIMS = {
    "nn": (((1,), (0,)), ((), ())),
    "nt": (((1,), (1,)), ((), ())),
    "tn": (((0,), (0,)), ((), ())),
}


def _dot(a, b, kind="nn"):
    return lax.dot_general(a.astype(BF16), b.astype(BF16), _DIMS[kind], preferred_element_type=F32)


def _params(sem, vmem=VMEM_LIMIT):
    return pltpu.CompilerParams(dimension_semantics=sem, vmem_limit_bytes=vmem)


def _tile(n, pref):
    t = min(n, pref)
    while n % t:
        t //= 2
    return t


def _mesh_pos():
    return lax.axis_index("x"), lax.axis_index("y"), lax.axis_index("c")


def _dev_index(x, y, c):
    return 4 * x + 2 * y + c


def _my_index():
    return _dev_index(*_mesh_pos())


def _remote(src, dst, send_sems, recv_sems, s, to):
    return pltpu.make_async_remote_copy(src_ref=src, dst_ref=dst, send_sem=send_sems.at[s], recv_sem=recv_sems.at[s],
                                        device_id=to, device_id_type=MESH)


class _Gather:
    def __init__(self, shards):
        self.inputs = list(shards)
        self.out_shapes = [jax.ShapeDtypeStruct((N_DEV,) + s.shape, s.dtype) for s in shards]
        n = len(shards)
        self.sem_shapes = [pltpu.SemaphoreType.DMA((7 * n,)), pltpu.SemaphoreType.DMA((7 * n,)),
                           pltpu.SemaphoreType.DMA((n,))]

    def _places(self):
        x, y, c = _mesh_pos()
        return (x, y, c), (x, y, 1 - c), [(1 - x, y), (x, 1 - y), (1 - x, 1 - y)]

    def _local(self, src, dst, sems):
        me = _my_index()
        return [pltpu.make_async_copy(src[w], dst[w].at[me], sems[2].at[w]) for w in range(len(src))]

    def start(self, src, dst, sems):
        me, sib, chips = self._places()
        for cp in self._local(src, dst, sems):
            cp.start()
        for w in range(len(src)):
            land = dst[w].at[_dev_index(*me)]
            _remote(src[w], land, sems[0], sems[1], 7 * w, sib).start()
            for j, chip in enumerate(chips):
                _remote(src[w], land, sems[0], sems[1], 7 * w + 1 + j, (*chip, me[2])).start()

    def middle(self, src, dst, sems):
        me, sib, chips = self._places()
        for j, chip in enumerate(chips):
            for w in range(len(src)):
                block = dst[w].at[_dev_index(*chip, me[2])]
                _remote(src[w], block, sems[0], sems[1], 7 * w + 1 + j, me).wait_recv()
                _remote(block, block, sems[0], sems[1], 7 * w + 4 + j, sib).start()

    def finish(self, src, dst, sems):
        me, sib, chips = self._places()
        n = len(src)
        for w in range(n):
            _remote(src[w], dst[w].at[_dev_index(*sib)], sems[0], sems[1], 7 * w, me).wait_recv()
            for j, chip in enumerate(chips):
                block = dst[w].at[_dev_index(*chip, sib[2])]
                _remote(block, block, sems[0], sems[1], 7 * w + 4 + j, me).wait_recv()
            for k in range(7):
                _remote(src[w], dst[w].at[0], sems[0], sems[1], 7 * w + k, me).wait_send()
        for cp in self._local(src, dst, sems):
            cp.wait()


class _Exchange:
    def __init__(self, partials, whole=()):
        self.n_part = len(partials)
        self.inputs = list(partials) + list(whole)
        self.out_shapes = [jax.ShapeDtypeStruct(p.shape, p.dtype) for p in partials]
        self.out_shapes += [jax.ShapeDtypeStruct((N_DEV,) + a.shape, a.dtype) for a in whole]
        n = len(self.inputs)
        self.sem_shapes = [pltpu.SemaphoreType.DMA((7 * n,)), pltpu.SemaphoreType.DMA((7 * n,)),
                           pltpu.SemaphoreType.DMA((n,))]

    def _peer(self, k):
        x, y, c = _mesh_pos()
        p = (x ^ ((k >> 2) & 1), y ^ ((k >> 1) & 1), c ^ (k & 1))
        return p, _dev_index(*p)

    def _source(self, src, w, slot):
        return src[w].at[slot] if w < self.n_part else src[w]

    def _local(self, src, dst, sems):
        me = _my_index()
        return [pltpu.make_async_copy(self._source(src, w, me), dst[w].at[me], sems[2].at[w])
                for w in range(len(src))]

    def start(self, src, dst, sems):
        me = _my_index()
        for cp in self._local(src, dst, sems):
            cp.start()
        for k in range(1, N_DEV):
            peer, peer_idx = self._peer(k)
            for w in range(len(src)):
                _remote(self._source(src, w, peer_idx), dst[w].at[me], sems[0], sems[1], 7 * w + k - 1, peer).start()

    def finish(self, src, dst, sems):
        for k in range(1, N_DEV):
            peer, peer_idx = self._peer(k)
            for w in range(len(src)):
                cp = _remote(self._source(src, w, peer_idx), dst[w].at[peer_idx], sems[0], sems[1], 7 * w + k - 1, peer)
                cp.wait_send()
                cp.wait_recv()
        for cp in self._local(src, dst, sems):
            cp.wait()


class _ExchangeTo:
    def __init__(self, partials, side):
        self.side = side
        self.inputs = list(partials)
        self.out_shapes = [jax.ShapeDtypeStruct(p.shape, p.dtype) for p in partials]
        n = len(partials)
        self.sem_shapes = [pltpu.SemaphoreType.DMA((7 * n,)), pltpu.SemaphoreType.DMA((7 * n,)),
                           pltpu.SemaphoreType.DMA((n,))]

    def _copies(self, src, dst, sems):
        x, y, c = _mesh_pos()
        me = _dev_index(x, y, c)
        receives = c == self.side
        remote = []
        for k in range(1, N_DEV):
            kx, ky, kc = (k >> 2) & 1, (k >> 1) & 1, k & 1
            peer = (x ^ kx, y ^ ky, c ^ kc)
            peer_idx = _dev_index(*peer)
            sends = c == (self.side ^ kc)
            for w in range(len(src)):
                slab = src[w].at[peer_idx]
                s = 7 * w + k - 1
                remote.append((sends, _remote(slab, dst[w].at[me], sems[0], sems[1], s, peer),
                               _remote(slab, dst[w].at[peer_idx], sems[0], sems[1], s, peer)))
        local = [pltpu.make_async_copy(src[w].at[me], dst[w].at[me], sems[2].at[w]) for w in range(len(src))]
        return receives, remote, local

    def start(self, src, dst, sems):
        receives, remote, local = self._copies(src, dst, sems)

        @pl.when(receives)
        def _():
            for cp in local:
                cp.start()

        for sends, send, _ in remote:
            pl.when(sends)(send.start)

    def finish(self, src, dst, sems):
        receives, remote, local = self._copies(src, dst, sems)
        for sends, send, arrive in remote:
            pl.when(sends)(send.wait_send)
            pl.when(receives)(arrive.wait_recv)

        @pl.when(receives)
        def _():
            for cp in local:
                cp.wait()


def _pcall(body, args, *, name, out_shape, grid, in_specs, out_specs, scratch_shapes=(), sem=None, comm=None):
    single = not isinstance(out_shape, (tuple, list))
    outs = [out_shape] if single else list(out_shape)
    ospecs = [out_specs] if single else list(out_specs)
    n_in, n_out, n_scr = len(args), len(outs), len(scratch_shapes)

    def pick(res):
        return res[0] if single else tuple(res[:n_out])

    if comm is None:
        res = pl.pallas_call(
            body, out_shape=outs, grid=grid, in_specs=list(in_specs), out_specs=ospecs,
            scratch_shapes=list(scratch_shapes), name=name, compiler_params=_params(sem),
        )(*args)
        return pick(res), ()

    nci, nco = len(comm.inputs), len(comm.out_shapes)

    def carrier(*refs):
        at = 0
        parts = []
        for size in (n_in, nci, n_out, nco, n_scr, len(comm.sem_shapes)):
            parts.append(refs[at:at + size])
            at += size
        ins, cins, o, couts, scr, sems = parts
        ids = [pl.program_id(a) for a in range(len(grid))]
        first = functools.reduce(jnp.logical_and, [i == 0 for i in ids])
        last = functools.reduce(jnp.logical_and, [i == g - 1 for i, g in zip(ids, grid)])

        body(*ins, *o, *scr)

        @pl.when(first)
        def _():
            comm.start(cins, couts, sems)

        if hasattr(comm, "middle"):
            steps = math.prod(grid)
            at = functools.reduce(lambda lin, ig: lin * ig[1] + ig[0], zip(ids, grid), 0)

            @pl.when(at == min(steps - 1, (3 * steps) // 4))
            def _():
                comm.middle(cins, couts, sems)

        @pl.when(last)
        def _():
            comm.finish(cins, couts, sems)

    hbm = pl.BlockSpec(memory_space=pltpu.HBM)
    res = pl.pallas_call(
        carrier, out_shape=outs + comm.out_shapes, grid=grid, in_specs=list(in_specs) + [hbm] * nci,
        out_specs=ospecs + [hbm] * nco, scratch_shapes=list(scratch_shapes) + comm.sem_shapes, name=name,
        compiler_params=_params(("arbitrary",) * len(grid)),
    )(*args, *comm.inputs)
    return pick(res), tuple(res[n_out:])


def _comm_call(name, comm):
    def body(*refs):
        nci, nco = len(comm.inputs), len(comm.out_shapes)
        cins, couts, sems = refs[:nci], refs[nci:nci + nco], refs[nci + nco:]
        comm.start(cins, couts, sems)
        if hasattr(comm, "middle"):
            comm.middle(cins, couts, sems)
        comm.finish(cins, couts, sems)

    hbm = pl.BlockSpec(memory_space=pltpu.HBM)
    return pl.pallas_call(
        body, out_shape=comm.out_shapes, in_specs=[hbm] * len(comm.inputs), out_specs=[hbm] * len(comm.out_shapes),
        scratch_shapes=comm.sem_shapes, name=name,
    )(*comm.inputs)


def _matmul(name, kind, a, b, out_shape, grid, a_spec, b_spec, o_spec, acc_shape, res=None, res_spec=None,
            comm=None, epilogue=None):
    nk = grid[-1]
    if epilogue is None:
        extra, extra_specs = ([res], [res_spec]) if res is not None else ([], [])
        n_out = 1
    else:
        extra, extra_specs, n_out = list(res), list(res_spec), len(out_shape)
    n_in = 2 + len(extra)

    def body(*refs):
        a_ref, b_ref = refs[0], refs[1]
        extra_refs, out_refs = refs[2:n_in], refs[n_in:n_in + n_out]

        def prod():
            return _dot(a_ref[...], b_ref[...], kind)

        def finish(acc):
            if epilogue is not None:
                ids = [pl.program_id(ax) for ax in range(len(grid) - 1)]
                first = functools.reduce(jnp.logical_and, [i == 0 for i in ids]) if ids else True
                epilogue(acc, extra_refs, out_refs, first)
                return
            if extra_refs:
                acc = acc + extra_refs[0][...]
            out_refs[0][...] = acc.astype(out_refs[0].dtype)

        if nk == 1:
            finish(prod())
        else:
            acc_ref = refs[n_in + n_out]
            k = pl.program_id(len(grid) - 1)

            @pl.when(k == 0)
            def _():
                acc_ref[...] = prod()

            @pl.when(k > 0)
            def _():
                acc_ref[...] += prod()

            @pl.when(k == nk - 1)
            def _():
                finish(acc_ref[...])

    in_specs = [a_spec, b_spec] + extra_specs
    args = (a, b, *extra)
    scratch = [pltpu.VMEM(acc_shape, F32)] if nk > 1 else []
    sem = ("arbitrary",) * len(grid) if epilogue is not None else ("parallel",) * (len(grid) - 1) + ("arbitrary",)
    out, landed = _pcall(body, args, name=name, out_shape=out_shape, grid=grid, in_specs=in_specs,
                         out_specs=o_spec, scratch_shapes=scratch, sem=sem, comm=comm)
    return out if comm is None else (out, landed)


def _mm_rows(name, a, w, out_dtype=F32, res=None, kind="nn", tm=MM_ROWS, comm=None):
    M, K = a.shape
    N = w.shape[1] if kind == "nn" else w.shape[0]
    tm = _tile(M, tm)
    res_spec = pl.BlockSpec((tm, N), lambda i, k: (i, 0)) if res is not None else None
    return _matmul(
        name, kind, a, w, jax.ShapeDtypeStruct((M, N), out_dtype), (M // tm, 1),
        pl.BlockSpec((tm, K), lambda i, k: (i, 0)), pl.BlockSpec(w.shape, lambda i, k: (0, 0)),
        pl.BlockSpec((tm, N), lambda i, k: (i, 0)), (tm, N), res, res_spec, comm)


def _residual_rms_epilogue(y, operands, outputs, first):
    x_ref, g_ref = operands
    x1_ref, h_ref = outputs
    xv = x_ref[...] + y
    x1_ref[...] = xv
    r = lax.rsqrt(jnp.mean(xv * xv, axis=-1, keepdims=True) + EPS)
    h_ref[...] = (xv * r * g_ref[...]).astype(h_ref.dtype)


def _mm_residual_rms(name, a, w, x, g, tm=MM_ROWS_RES):
    M, K = a.shape
    N = w.shape[1]
    tm = _tile(M, tm)
    row = pl.BlockSpec((tm, N), lambda i, k: (i, 0))
    return _matmul(
        name, "nn", a, w, (jax.ShapeDtypeStruct((M, N), F32), jax.ShapeDtypeStruct((M, N), BF16)), (M // tm, 1),
        pl.BlockSpec((tm, K), lambda i, k: (i, 0)), pl.BlockSpec(w.shape, lambda i, k: (0, 0)),
        (row, row), (tm, N), [x, g], [row, pl.BlockSpec((1, N), lambda i, k: (0, 0))],
        epilogue=_residual_rms_epilogue)


def _mm_tn(name, a, b, out_dtype=F32, tk=MM_TOKENS, comm=None):
    T, M = a.shape
    N = b.shape[1]
    tk = _tile(T, tk)
    return _matmul(
        name, "tn", a, b, jax.ShapeDtypeStruct((M, N), out_dtype), (1, T // tk),
        pl.BlockSpec((tk, M), lambda i, k: (k, 0)), pl.BlockSpec((tk, N), lambda i, k: (k, 0)),
        pl.BlockSpec((M, N), lambda i, k: (0, 0)), (M, N), comm=comm)


def _mm_in_gather(h, shard, tm=MM_ROWS, comm=None):
    T, K = h.shape
    n = shard.shape[1]
    tm = _tile(T, tm)
    n_tiles = T // tm
    pair_of_chip_step = {4: 1, 2: 2, 6: 3}

    def slab_of(s):
        x, y, c = _mesh_pos()
        return _dev_index(x ^ ((s >> 2) & 1), y ^ ((s >> 1) & 1), c ^ (s & 1))

    def body(h_ref, shard_ref, proj_ref, win_ref, wbuf, slot_sems, send_sems, recv_sems, local_sem):
        s, i = pl.program_id(0), pl.program_id(1)
        x, y, c = _mesh_pos()
        me, sib = (x, y, c), (x, y, 1 - c)

        def slot_copy(step):
            src = shard_ref if step == 0 else win_ref.at[slab_of(step)]
            return pltpu.make_async_copy(src, wbuf.at[step % 2], slot_sems.at[step % 2])

        def fetch(step):
            if step >= 1:
                block = win_ref.at[slab_of(step)]
                if step == 1:
                    pair = 0
                elif step % 2 == 0:
                    pair = pair_of_chip_step[step]
                else:
                    pair = 3 + pair_of_chip_step[step - 1]
                _remote(block, block, send_sems, recv_sems, pair, me).wait_recv()
                if step % 2 == 0:
                    _remote(block, block, send_sems, recv_sems, 3 + pair, sib).start()
            slot_copy(step).start()

        @pl.when((s == 0) & (i == 0))
        def _():
            land = win_ref.at[_dev_index(*me)]
            pltpu.make_async_copy(shard_ref, land, local_sem).start()
            _remote(shard_ref, land, send_sems, recv_sems, 0, sib).start()
            for step, pair in pair_of_chip_step.items():
                peer = (x ^ ((step >> 2) & 1), y ^ ((step >> 1) & 1), c)
                _remote(shard_ref, land, send_sems, recv_sems, pair, peer).start()
            fetch(0)

        for step in range(N_DEV):
            @pl.when((s == step) & (i == 0))
            def _():
                slot_copy(step).wait()

            if step + 1 < N_DEV:
                @pl.when((s == step) & (i == n_tiles - 1))
                def _():
                    fetch(step + 1)

        proj_ref[...] = _dot(h_ref[...], wbuf[s % 2])

        @pl.when((s == N_DEV - 1) & (i == n_tiles - 1))
        def _():
            for pair in range(7):
                _remote(shard_ref, win_ref.at[0], send_sems, recv_sems, pair, me).wait_send()
            pltpu.make_async_copy(shard_ref, win_ref.at[_dev_index(*me)], local_sem).wait()

    hbm = pl.BlockSpec(memory_space=pltpu.HBM)
    return _pcall(
        body, (h, shard), name="mm_in",
        out_shape=(jax.ShapeDtypeStruct((T, N_DEV * n), F32), jax.ShapeDtypeStruct((N_DEV, K, n), shard.dtype)),
        grid=(N_DEV, n_tiles), in_specs=[pl.BlockSpec((tm, K), lambda s, i: (i, 0)), hbm],
        out_specs=(pl.BlockSpec((tm, n), lambda s, i: (i, slab_of(s))), hbm),
        scratch_shapes=[pltpu.VMEM((2, K, n), shard.dtype), pltpu.SemaphoreType.DMA((2,)),
                        pltpu.SemaphoreType.DMA((7,)), pltpu.SemaphoreType.DMA((7,)), pltpu.SemaphoreType.DMA],
        sem=("arbitrary", "arbitrary"), comm=comm)


def _rms_bwd_epilogue(dh, operands, outputs, first):
    x_ref, g_ref, dres_ref = operands
    dx_ref, dg_ref = outputs
    xv = x_ref[...]
    r = lax.rsqrt(jnp.mean(xv * xv, axis=-1, keepdims=True) + EPS)
    xhat = xv * r

    @pl.when(first)
    def _():
        dg_ref[...] = jnp.zeros_like(dg_ref)

    dg_ref[...] += jnp.sum(dh * xhat, axis=0, keepdims=True)
    dxhat = dh * g_ref[...]
    dx_ref[...] = dres_ref[...] + r * (dxhat - xhat * jnp.mean(dxhat * xhat, axis=-1, keepdims=True))


def _rms_bwd_fused(M, K, tm, rms):
    row = pl.BlockSpec((tm, K), lambda i, j: (i, 0))
    vec = pl.BlockSpec((1, K), lambda i, j: (0, 0))
    x, g, dres = rms
    return dict(res=[x, g, dres], res_spec=[row, vec, row], epilogue=_rms_bwd_epilogue,
                out_shape=(jax.ShapeDtypeStruct((M, K), F32), jax.ShapeDtypeStruct((1, K), F32)), o_spec=(row, vec))


def _mm_cols_slab_t(name, a, w_slabs, rms, tm=MM_ROWS_RES, comm=None):
    M = a.shape[0]
    J, K, n = w_slabs.shape
    tm = _tile(M, tm)
    fused = _rms_bwd_fused(M, K, tm, rms)
    return _matmul(
        name, "nt", a, w_slabs, fused.pop("out_shape"), (M // tm, J),
        pl.BlockSpec((tm, n), lambda i, j: (i, j)), pl.BlockSpec((None, K, n), lambda i, j: (j, 0, 0)),
        fused.pop("o_spec"), (tm, K), comm=comm, **fused)


def _mm_tn_slab(name, a, b, n, out_dtype=F32, tk=MM_TOKENS, comm=None, part=(0, 1)):
    T, M = a.shape
    p, of = part
    M //= of
    J = b.shape[1] // n
    tk = _tile(T, tk)
    return _matmul(
        name, "tn", a, b, jax.ShapeDtypeStruct((J, M, n), out_dtype), (J, T // tk),
        pl.BlockSpec((tk, M), lambda j, k: (k, p)), pl.BlockSpec((tk, n), lambda j, k: (k, j)),
        pl.BlockSpec((None, M, n), lambda j, k: (j, 0, 0)), (M, n), comm=comm)


def _rms_fwd(name, x, g, tm=512):
    T, Dm = x.shape
    tm = _tile(T, tm)

    def body(x_ref, g_ref, h_ref):
        xv = x_ref[...]
        r = lax.rsqrt(jnp.mean(xv * xv, axis=-1, keepdims=True) + EPS)
        h_ref[...] = (xv * r * g_ref[...]).astype(h_ref.dtype)

    return pl.pallas_call(
        body, out_shape=jax.ShapeDtypeStruct((T, Dm), BF16), grid=(T // tm,),
        in_specs=[pl.BlockSpec((tm, Dm), lambda i: (i, 0)), pl.BlockSpec((1, Dm), lambda i: (0, 0))],
        out_specs=pl.BlockSpec((tm, Dm), lambda i: (i, 0)), name=name, compiler_params=_params(("parallel",)),
    )(x, g)


def _rms_bwd(name, x, g, dh, dres, tm=512):
    T, Dm = x.shape
    tm = _tile(T, tm)
    want_dx = dres is not None

    def body(*refs):
        if want_dx:
            x_ref, g_ref, dh_ref, dres_ref, dx_ref, dg_ref = refs
        else:
            x_ref, g_ref, dh_ref, dg_ref = refs
        xv = x_ref[...]
        r = lax.rsqrt(jnp.mean(xv * xv, axis=-1, keepdims=True) + EPS)
        xhat = xv * r
        dhv = dh_ref[...]

        @pl.when(pl.program_id(0) == 0)
        def _():
            dg_ref[...] = jnp.zeros_like(dg_ref)

        dg_ref[...] += jnp.sum(dhv * xhat, axis=0, keepdims=True)
        if want_dx:
            dxhat = dhv * g_ref[...]
            dx_ref[...] = dres_ref[...] + r * (dxhat - xhat * jnp.mean(dxhat * xhat, axis=-1, keepdims=True))

    row = pl.BlockSpec((tm, Dm), lambda i: (i, 0))
    vec = pl.BlockSpec((1, Dm), lambda i: (0, 0))
    if want_dx:
        return pl.pallas_call(
            body, out_shape=(jax.ShapeDtypeStruct((T, Dm), F32), jax.ShapeDtypeStruct((1, Dm), F32)),
            grid=(T // tm,), in_specs=[row, vec, row, row], out_specs=(row, vec), name=name,
            compiler_params=_params(("arbitrary",)),
        )(x, g, dh, dres)
    return pl.pallas_call(
        body, out_shape=jax.ShapeDtypeStruct((1, Dm), F32), grid=(T // tm,), in_specs=[row, vec, row],
        out_specs=vec, name=name, compiler_params=_params(("arbitrary",)),
    )(x, g, dh)


def _pool_rows(S):
    return _tile(S, 256)


def _pool_count(c0, rows, w):
    t = c0 + lax.broadcasted_iota(jnp.int32, (rows, 1), 0)
    return jnp.minimum(t + 1, w).astype(F32)


def _pool_fwd(proj, w_pool, scale, B, S):
    CH = _pool_rows(S)

    def body(hp_ref, wp_ref, sc_ref, o_ref, pad_ref):
        pad_ref[0:POOL_HALO, :] = jnp.zeros((POOL_HALO, POOL_WIDTH), F32)
        pad_ref[POOL_HALO:, :] = hp_ref[...]
        for gi, w in enumerate(POOL_WINDOWS):
            cols = slice(gi * POOL_GROUP_DIM, (gi + 1) * POOL_GROUP_DIM)
            for c in range(S // CH):
                base = POOL_HALO + c * CH
                acc = pad_ref[base:base + CH, cols]
                tok = acc
                for j in range(1, w):
                    acc = acc + pad_ref[base - j:base - j + CH, cols]
                pooled = acc / _pool_count(c * CH, CH, w) - tok
                z = _dot(pooled, wp_ref[gi])
                o_ref[c * CH:(c + 1) * CH, cols] = (z * sc_ref[:, cols]).astype(o_ref.dtype)

    return pl.pallas_call(
        body, out_shape=jax.ShapeDtypeStruct((B * S, POOL_WIDTH), BF16), grid=(B,),
        in_specs=[pl.BlockSpec((S, POOL_WIDTH), lambda b: (b, 0)),
                  pl.BlockSpec(w_pool.shape, lambda b: (0, 0, 0)),
                  pl.BlockSpec((1, POOL_WIDTH), lambda b: (0, 0))],
        out_specs=pl.BlockSpec((S, POOL_WIDTH), lambda b: (b, 0)),
        scratch_shapes=[pltpu.VMEM((S + POOL_HALO, POOL_WIDTH), F32)],
        name="pool_fwd", compiler_params=_params(("parallel",)),
    )(proj, w_pool, scale)


def _pool_bwd(proj, d_ypre, w_pool, scale, B, S):
    CH = _pool_rows(S)

    def body(hp_ref, dy_ref, wp_ref, sc_ref, dhp_ref, dwp_ref, dsc_ref, pad_ref, sc_pad_ref, dp_ref):
        @pl.when(pl.program_id(0) == 0)
        def _():
            dwp_ref[...] = jnp.zeros_like(dwp_ref)
            dsc_ref[...] = jnp.zeros_like(dsc_ref)

        pad_ref[0:POOL_HALO, :] = jnp.zeros((POOL_HALO, POOL_WIDTH), F32)
        pad_ref[POOL_HALO:, :] = hp_ref[...]
        sc_pad_ref[S:, :] = jnp.zeros((POOL_HALO, POOL_WIDTH), F32)
        for gi, w in enumerate(POOL_WINDOWS):
            cols = slice(gi * POOL_GROUP_DIM, (gi + 1) * POOL_GROUP_DIM)
            for c in range(S // CH):
                base = POOL_HALO + c * CH
                rows = slice(c * CH, (c + 1) * CH)
                acc = pad_ref[base:base + CH, cols]
                tok = acc
                for j in range(1, w):
                    acc = acc + pad_ref[base - j:base - j + CH, cols]
                cnt = _pool_count(c * CH, CH, w)
                pooled = acc / cnt - tok
                z = _dot(pooled, wp_ref[gi])
                dy = dy_ref[rows, cols]
                dsc_ref[:, cols] += jnp.sum(dy * z, axis=0, keepdims=True)
                dz = dy * sc_ref[:, cols]
                dwp_ref[gi] += _dot(pooled, dz, "tn")
                dpool = _dot(dz, wp_ref[gi], "nt")
                dp_ref[rows, cols] = dpool
                sc_pad_ref[rows, cols] = dpool / cnt
            for c in range(S // CH):
                rows = slice(c * CH, (c + 1) * CH)
                acc = sc_pad_ref[rows, cols]
                for j in range(1, w):
                    acc = acc + sc_pad_ref[c * CH + j:c * CH + j + CH, cols]
                dhp_ref[rows, cols] = (acc - dp_ref[rows, cols]).astype(dhp_ref.dtype)

    seq = pl.BlockSpec((S, POOL_WIDTH), lambda b: (b, 0))
    return pl.pallas_call(
        body,
        out_shape=(jax.ShapeDtypeStruct((B * S, POOL_WIDTH), BF16),
                   jax.ShapeDtypeStruct(w_pool.shape, F32), jax.ShapeDtypeStruct((1, POOL_WIDTH), F32)),
        grid=(B,),
        in_specs=[seq, seq, pl.BlockSpec(w_pool.shape, lambda b: (0, 0, 0)),
                  pl.BlockSpec((1, POOL_WIDTH), lambda b: (0, 0))],
        out_specs=(seq, pl.BlockSpec(w_pool.shape, lambda b: (0, 0, 0)),
                   pl.BlockSpec((1, POOL_WIDTH), lambda b: (0, 0))),
        scratch_shapes=[pltpu.VMEM((S + POOL_HALO, POOL_WIDTH), F32),
                        pltpu.VMEM((S + POOL_HALO, POOL_WIDTH), F32),
                        pltpu.VMEM((S, POOL_WIDTH), F32)],
        name="pool_bwd", compiler_params=_params(("arbitrary",)),
    )(proj, d_ypre, w_pool, scale)


def _ret_tables(S):
    half = RET_QK_DIM // 2
    inv = ROPE_BASE ** (-jnp.arange(half, dtype=F32) / half)
    ang = jnp.arange(S, dtype=F32)[:, None] * inv[None, :]
    cos, sin = jnp.cos(ang), jnp.sin(ang)
    cos_full = jnp.concatenate([cos, cos], axis=-1)
    sin_signed = jnp.concatenate([-sin, sin], axis=-1)
    C = RET_CHUNK
    lg = jnp.log1p(-jnp.exp2(-5.0 - jnp.arange(RET_HEADS, dtype=F32)))[:, None, None]
    idx = jnp.arange(C, dtype=F32)
    rel = idx[:, None] - idx[None, :]
    decay = jnp.where(rel >= 0, jnp.exp(jnp.maximum(rel, 0.0) * lg), 0.0)
    q_decay = jnp.broadcast_to(jnp.exp((idx + 1.0)[None, :, None] * lg), (RET_HEADS, C, RET_QK_DIM))
    k_decay = jnp.broadcast_to(jnp.exp((C - 1.0 - idx)[None, :, None] * lg), (RET_HEADS, C, RET_QK_DIM))
    c_decay = jnp.broadcast_to(jnp.exp(C * lg), (RET_HEADS, 1, RET_V_DIM))
    return cos_full, sin_signed, decay, q_decay, k_decay, c_decay


def _rope(x, cos_full, sin_signed):
    return x * cos_full + pltpu.roll(x, RET_QK_DIM // 2, axis=1) * sin_signed


def _rope_t(dy, cos_full, sin_signed):
    return dy * cos_full + pltpu.roll(dy * sin_signed, RET_QK_DIM // 2, axis=1)


RET_COLS = 512


def _ret_specs(N, chunk_of):
    C = RET_CHUNK

    def rows(width, col=0):
        return pl.BlockSpec((C, width), lambda b, i: (b * N + chunk_of(i), col))

    def whole(shape):
        return pl.BlockSpec(shape, lambda b, i: (0,) * len(shape))

    wide = RET_HEADS * RET_V_DIM
    return dict(
        q=rows(RET_COLS, COL_Q // RET_COLS), k=rows(RET_COLS, COL_K // RET_COLS),
        v=[rows(RET_COLS, COL_V // RET_COLS + j) for j in range(2)],
        gr=[rows(RET_COLS, COL_GR // RET_COLS + j) for j in range(2)],
        table=pl.BlockSpec((C, RET_QK_DIM), lambda b, i: (chunk_of(i), 0)),
        decay=whole((RET_HEADS, C, C)), qd=whole((RET_HEADS, C, RET_QK_DIM)), kd=whole((RET_HEADS, C, RET_QK_DIM)),
        cd=whole((RET_HEADS, 1, RET_V_DIM)), vec=whole((1, wide)), qk_rows=rows(RET_COLS), v_rows=rows(wide),
        state=pl.BlockSpec((None, None, RET_HEADS, RET_QK_DIM, RET_V_DIM), lambda b, i: (b, chunk_of(i), 0, 0, 0)))


def _head_cols(h):
    pair = slice((h % 2) * RET_V_DIM, (h % 2 + 1) * RET_V_DIM)
    return slice(h * RET_QK_DIM, (h + 1) * RET_QK_DIM), h // 2, pair, slice(h * RET_V_DIM, (h + 1) * RET_V_DIM)


def _group_norm(o):
    mu = jnp.mean(o, axis=-1, keepdims=True)
    oc = o - mu
    rstd = lax.rsqrt(jnp.mean(oc * oc, axis=-1, keepdims=True) + EPS)
    return oc * rstd, rstd


def _ret_fwd(proj, g_ret, b_ret, tables, B, S, comm=None):
    N = S // RET_CHUNK
    cos_t, sin_t, decay, q_decay, k_decay, c_decay = tables
    sp = _ret_specs(N, lambda i: i)

    def body(q_ref, k_ref, v0_ref, v1_ref, gr0_ref, gr1_ref, cos_ref, sin_ref, dec_ref, qd_ref, kd_ref, cd_ref,
             g_ref, b_ref, y_ref, rs_ref, r_ref):
        @pl.when(pl.program_id(1) == 0)
        def _():
            r_ref[...] = jnp.zeros_like(r_ref)

        cs, sn = cos_ref[...], sin_ref[...]
        heads = range(RET_HEADS)
        cols = [_head_cols(h) for h in heads]
        q = [_rope(q_ref[:, cols[h][0]], cs, sn) for h in heads]
        k = [_rope(k_ref[:, cols[h][0]], cs, sn) * (RET_QK_DIM ** -0.5) for h in heads]
        v = [(v0_ref, v1_ref)[cols[h][1]][:, cols[h][2]] for h in heads]
        R = [r_ref[h] for h in heads]
        s = [_dot(q[h], k[h], "nt") * dec_ref[h] for h in heads]
        o = [_dot(s[h], v[h]) + _dot(q[h] * qd_ref[h], R[h]) for h in heads]
        r_new = [cd_ref[h] * R[h] + _dot(k[h] * kd_ref[h], v[h], "tn") for h in heads]
        for h in heads:
            _, j, pair, wide = cols[h]
            rs_ref[h] = R[h]
            r_ref[h] = r_new[h]
            on, _ = _group_norm(o[h])
            gr = (gr0_ref, gr1_ref)[j][:, pair]
            y_ref[:, wide] = (gr * jax.nn.sigmoid(gr) * (on * g_ref[:, wide] + b_ref[:, wide])).astype(y_ref.dtype)

    state = jax.ShapeDtypeStruct((B, N, RET_HEADS, RET_QK_DIM, RET_V_DIM), F32)
    return _pcall(
        body, (proj,) * 6 + (cos_t, sin_t, decay, q_decay, k_decay, c_decay, g_ret, b_ret),
        name="ret_fwd", out_shape=(jax.ShapeDtypeStruct((B * S, RET_HEADS * RET_V_DIM), BF16), state), grid=(B, N),
        in_specs=[sp["q"], sp["k"], *sp["v"], *sp["gr"], sp["table"], sp["table"], sp["decay"], sp["qd"],
                  sp["kd"], sp["cd"], sp["vec"], sp["vec"]],
        out_specs=(sp["v_rows"], sp["state"]),
        scratch_shapes=[pltpu.VMEM((RET_HEADS, RET_QK_DIM, RET_V_DIM), F32)],
        sem=("parallel", "arbitrary"), comm=comm)


def _ret_bwd(proj, states, d_yr, g_ret, b_ret, tables, B, S, comm=None):
    N = S // RET_CHUNK
    cos_t, sin_t, decay, q_decay, k_decay, c_decay = tables
    sp = _ret_specs(N, lambda i: N - 1 - i)
    qk_scale = RET_QK_DIM ** -0.5

    def body(q_ref, k_ref, v0_ref, v1_ref, gr0_ref, gr1_ref, dy_ref, rs_ref, cos_ref, sin_ref, dec_ref, qd_ref,
             kd_ref, cd_ref, g_ref, b_ref, dq_ref, dk_ref, dv_ref, dgr_ref, dg_ref, db_ref, dr_ref):
        @pl.when((pl.program_id(0) == 0) & (pl.program_id(1) == 0))
        def _():
            dg_ref[...] = jnp.zeros_like(dg_ref)
            db_ref[...] = jnp.zeros_like(db_ref)

        @pl.when(pl.program_id(1) == 0)
        def _():
            dr_ref[...] = jnp.zeros_like(dr_ref)

        cs, sn = cos_ref[...], sin_ref[...]
        heads = range(RET_HEADS)
        cols = [_head_cols(h) for h in heads]
        q = [_rope(q_ref[:, cols[h][0]], cs, sn) for h in heads]
        k = [_rope(k_ref[:, cols[h][0]], cs, sn) * qk_scale for h in heads]
        v = [(v0_ref, v1_ref)[cols[h][1]][:, cols[h][2]] for h in heads]
        s = [_dot(q[h], k[h], "nt") * dec_ref[h] for h in heads]
        o = [_dot(s[h], v[h]) + _dot(q[h] * qd_ref[h], rs_ref[h]) for h in heads]
        do = []
        for h in heads:
            _, j, pair, wide = cols[h]
            on, rstd = _group_norm(o[h])
            g = g_ref[:, wide]
            oaff = on * g + b_ref[:, wide]
            gr = (gr0_ref, gr1_ref)[j][:, pair]
            sg = jax.nn.sigmoid(gr)
            dy = dy_ref[:, wide]
            dgr_ref[:, wide] = (dy * oaff * (sg * (1.0 + gr * (1.0 - sg)))).astype(dgr_ref.dtype)
            doaff = dy * (gr * sg)
            dg_ref[:, wide] += jnp.sum(doaff * on, axis=0, keepdims=True)
            db_ref[:, wide] += jnp.sum(doaff, axis=0, keepdims=True)
            don = doaff * g
            do.append(rstd * (don - jnp.mean(don, axis=-1, keepdims=True)
                              - on * jnp.mean(don * on, axis=-1, keepdims=True)))
        ds = [_dot(do[h], v[h], "nt") * dec_ref[h] for h in heads]
        dq = [_dot(ds[h], k[h]) + qd_ref[h] * _dot(do[h], rs_ref[h], "nt") for h in heads]
        dk = [_dot(ds[h], q[h], "tn") + kd_ref[h] * _dot(v[h], dr_ref[h], "nt") for h in heads]
        dv = [_dot(s[h], do[h], "tn") + _dot(k[h] * kd_ref[h], dr_ref[h]) for h in heads]
        dr = [cd_ref[h] * dr_ref[h] + _dot(q[h] * qd_ref[h], do[h], "tn") for h in heads]
        for h in heads:
            qk, _, _, wide = cols[h]
            dv_ref[:, wide] = dv[h].astype(dv_ref.dtype)
            dr_ref[h] = dr[h]
            dq_ref[:, qk] = _rope_t(dq[h], cs, sn).astype(dq_ref.dtype)
            dk_ref[:, qk] = _rope_t(dk[h] * qk_scale, cs, sn).astype(dk_ref.dtype)

    T = B * S
    qk_shape = jax.ShapeDtypeStruct((T, RET_HEADS * RET_QK_DIM), BF16)
    v_shape = jax.ShapeDtypeStruct((T, RET_HEADS * RET_V_DIM), BF16)
    vec_shape = jax.ShapeDtypeStruct((1, RET_HEADS * RET_V_DIM), F32)
    return _pcall(
        body, (proj,) * 6 + (d_yr, states, cos_t, sin_t, decay, q_decay, k_decay, c_decay, g_ret, b_ret),
        name="ret_bwd", out_shape=(qk_shape, qk_shape, v_shape, v_shape, vec_shape, vec_shape), grid=(B, N),
        in_specs=[sp["q"], sp["k"], *sp["v"], *sp["gr"], sp["v_rows"], sp["state"], sp["table"], sp["table"],
                  sp["decay"], sp["qd"], sp["kd"], sp["cd"], sp["vec"], sp["vec"]],
        out_specs=(sp["qk_rows"], sp["qk_rows"], sp["v_rows"], sp["v_rows"], sp["vec"], sp["vec"]),
        scratch_shapes=[pltpu.VMEM((RET_HEADS, RET_QK_DIM, RET_V_DIM), F32)],
        sem=("arbitrary", "arbitrary"), comm=comm)


def _xa_rows(S):
    return _tile(S, 256)


def _xa_groups(S, rows, size=4):
    chunks = [slice(r, r + rows) for r in range(0, S, rows)]
    return [chunks[g:g + size] for g in range(0, len(chunks), size)]


def _xa_specs(S, M):
    q = pl.BlockSpec((S, XA_HEAD_DIM), lambda b, h: (b, COL_QX // XA_HEAD_DIM + h))
    k = pl.BlockSpec((M, XA_HEAD_DIM), lambda b, h: (b, h))
    v = pl.BlockSpec((M, XA_HEAD_DIM), lambda b, h: (b, XA_HEADS + h))
    o = pl.BlockSpec((S, XA_HEAD_DIM), lambda b, h: (b, h))
    return q, k, v, o


def _softmax_rows(s):
    e = jnp.exp(s - jnp.max(s, axis=-1, keepdims=True))
    return e / jnp.sum(e, axis=-1, keepdims=True)


def _xa_fwd(proj, kv, B, S, M, comm=None):
    CH = _xa_rows(S)
    q_spec, k_spec, v_spec, o_spec = _xa_specs(S, M)

    def body(q_ref, k_ref, v_ref, o_ref):
        for group in _xa_groups(S, CH):
            sc = [_dot(q_ref[rows, :], k_ref[...], "nt") * (XA_HEAD_DIM ** -0.5) for rows in group]
            p = [_softmax_rows(s) for s in sc]
            for rows, pg in zip(group, p):
                o_ref[rows, :] = _dot(pg, v_ref[...]).astype(o_ref.dtype)

    return _pcall(
        body, (proj, kv, kv), name="xattn_fwd", out_shape=jax.ShapeDtypeStruct((B * S, XA_WIDTH), BF16),
        grid=(B, XA_HEADS), in_specs=[q_spec, k_spec, v_spec], out_specs=o_spec,
        sem=("parallel", "parallel"), comm=comm)


def _xa_bwd(proj, kv, d_o, B, S, M, comm=None):
    CH = _xa_rows(S)
    q_spec, k_spec, v_spec, o_spec = _xa_specs(S, M)
    scale = XA_HEAD_DIM ** -0.5

    def body(q_ref, k_ref, v_ref, do_ref, dq_ref, dk_ref, dv_ref):
        dk_ref[...] = jnp.zeros_like(dk_ref)
        dv_ref[...] = jnp.zeros_like(dv_ref)
        for group in _xa_groups(S, CH):
            q = [q_ref[rows, :] for rows in group]
            do = [do_ref[rows, :] for rows in group]
            p = [_softmax_rows(_dot(qg, k_ref[...], "nt") * scale) for qg in q]
            dp = [_dot(dg, v_ref[...], "nt") for dg in do]
            ds = [pg * (dpg - jnp.sum(dpg * pg, axis=-1, keepdims=True)) * scale for pg, dpg in zip(p, dp)]
            for rows, dsg in zip(group, ds):
                dq_ref[rows, :] = _dot(dsg, k_ref[...]).astype(dq_ref.dtype)
            dk_ref[...] += sum(_dot(dsg, qg, "tn") for dsg, qg in zip(ds, q))
            dv_ref[...] += sum(_dot(pg, dg, "tn") for pg, dg in zip(p, do))

    kv_out = pl.BlockSpec((M, XA_HEAD_DIM), lambda b, h: (b, h))
    return _pcall(
        body, (proj, kv, kv, d_o), name="xattn_bwd",
        out_shape=(jax.ShapeDtypeStruct((B * S, XA_WIDTH), BF16), jax.ShapeDtypeStruct((B * M, XA_WIDTH), F32),
                   jax.ShapeDtypeStruct((B * M, XA_WIDTH), F32)),
        grid=(B, XA_HEADS), in_specs=[q_spec, k_spec, v_spec, o_spec], out_specs=(o_spec, kv_out, kv_out),
        sem=("parallel", "parallel"), comm=comm)


def _gate_specs(tm):
    n = COL_GL // D_MODEL
    return [pl.BlockSpec((tm, D_MODEL), lambda i, j=j: (i, n + j)) for j in range(3)]


def _merge_fwd(proj, ys, tm=512, comm=None):
    T = proj.shape[0]
    tm = _tile(T, tm)
    row = pl.BlockSpec((tm, D_MODEL), lambda i: (i, 0))

    def body(g0, g1, g2, y0, y1, y2, o_ref):
        acc = jax.nn.sigmoid(g0[...]) * y0[...]
        acc = acc + jax.nn.sigmoid(g1[...]) * y1[...]
        acc = acc + jax.nn.sigmoid(g2[...]) * y2[...]
        o_ref[...] = acc.astype(o_ref.dtype)

    return _pcall(
        body, (proj, proj, proj, *ys), name="merge_fwd", out_shape=jax.ShapeDtypeStruct((T, D_MODEL), BF16),
        grid=(T // tm,), in_specs=_gate_specs(tm) + [row] * 3, out_specs=row, sem=("parallel",), comm=comm)


def _merge_bwd(proj, ys, d_merged, tm=512, comm=None):
    T = proj.shape[0]
    tm = _tile(T, tm)
    row = pl.BlockSpec((tm, D_MODEL), lambda i: (i, 0))

    def body(g0, g1, g2, y0, y1, y2, dm_ref, dgl_ref, d0, d1, d2):
        dm = dm_ref[...]
        for j, (g_ref, y_ref, d_ref) in enumerate(((g0, y0, d0), (g1, y1, d1), (g2, y2, d2))):
            sg = jax.nn.sigmoid(g_ref[...])
            d_ref[...] = (dm * sg).astype(d_ref.dtype)
            dgl_ref[:, j * D_MODEL:(j + 1) * D_MODEL] = (dm * y_ref[...] * sg * (1.0 - sg)).astype(dgl_ref.dtype)

    dy = jax.ShapeDtypeStruct((T, D_MODEL), BF16)
    return _pcall(
        body, (proj, proj, proj, *ys, d_merged), name="merge_bwd",
        out_shape=(jax.ShapeDtypeStruct((T, 3 * D_MODEL), BF16), dy, dy, dy), grid=(T // tm,),
        in_specs=_gate_specs(tm) + [row] * 4,
        out_specs=(pl.BlockSpec((tm, 3 * D_MODEL), lambda i: (i, 0)), row, row, row),
        sem=("parallel",), comm=comm)


def _gelu(x):
    return 0.5 * x * (1.0 + jnp.tanh(GELU_C * (x + GELU_A * x * x * x)))


def _gelu_grad(x):
    t = jnp.tanh(GELU_C * (x + GELU_A * x * x * x))
    return 0.5 * (1.0 + t) + 0.5 * x * (1.0 - t * t) * GELU_C * (1.0 + 3.0 * GELU_A * x * x)


GLU_HALO = 16


def _shift_down(x, prev, n):
    last = prev.shape[0]
    r = lax.broadcasted_iota(jnp.int32, (8, 1), 0)
    rolled = pltpu.roll(x, n, axis=0)
    head = rolled[0:8]
    for j in range(n):
        head = jnp.where(r == j, prev[last - n + j:last - n + j + 1, :], head)
    return jnp.concatenate([head, rolled[8:]], axis=0)


def _shift_up(x, nxt, n):
    rows = x.shape[0]
    r = lax.broadcasted_iota(jnp.int32, (8, 1), 0)
    rolled = pltpu.roll(x, rows - n, axis=0)
    tail = rolled[rows - 8:]
    for j in range(n):
        tail = jnp.where(r == 8 - n + j, nxt[j:j + 1, :], tail)
    return jnp.concatenate([rolled[:rows - 8], tail], axis=0)


def _conv(a, prev, cw, cb):
    return _shift_down(a, prev, 2) * cw[0:1, :] + _shift_down(a, prev, 1) * cw[1:2, :] + a * cw[2:3, :] + cb


def _glu_fwd(up, cw, cb, S, tm=1024, comm=None):
    T = up.shape[2]
    tm = _tile(S, tm)
    per_seq = S // tm

    def body(ab_ref, prev_ref, cw_ref, cb_ref, u_ref):
        i = pl.program_id(1)
        prev = jnp.where(i % per_seq == 0, 0.0, prev_ref[...].astype(F32))
        ac = _conv(ab_ref[0].astype(F32), prev, cw_ref[...], cb_ref[...])
        u_ref[...] = (_gelu(ac) * ab_ref[1].astype(F32)).astype(u_ref.dtype)

    before = tm // GLU_HALO
    return _pcall(
        body, (up, up, cw, cb), name="glu_fwd",
        out_shape=jax.ShapeDtypeStruct((FFN_SLABS, T, UP_SHARD), BF16), grid=(FFN_SLABS, T // tm),
        in_specs=[pl.BlockSpec((2, None, tm, UP_SHARD), lambda d, i: (0, d, i, 0)),
                  pl.BlockSpec((None, None, GLU_HALO, UP_SHARD),
                               lambda d, i: (0, d, jnp.maximum(i * before - 1, 0), 0)),
                  pl.BlockSpec((None, 3, UP_SHARD), lambda d, i: (d, 0, 0)),
                  pl.BlockSpec((None, 1, UP_SHARD), lambda d, i: (d, 0, 0))],
        out_specs=pl.BlockSpec((None, tm, UP_SHARD), lambda d, i: (d, i, 0)),
        sem=("parallel", "parallel"), comm=comm)


def _glu_bwd(up, d_u, cw, cb, S, tm=512, comm=None):
    T = up.shape[2]
    tm = _tile(S, tm)
    per_seq = S // tm
    n_tiles = T // tm
    per_tile = tm // GLU_HALO

    def body(ab_ref, prev_ref, abn_ref, du_ref, dun_ref, cw_ref, cb_ref, dup_ref, dcw_ref, dcb_ref):
        i = pl.program_id(1)

        @pl.when(i == 0)
        def _():
            dcw_ref[...] = jnp.zeros_like(dcw_ref)
            dcb_ref[...] = jnp.zeros_like(dcb_ref)

        cw, cb = cw_ref[...], cb_ref[...]
        a, b = ab_ref[0].astype(F32), ab_ref[1].astype(F32)
        prev = jnp.where(i % per_seq == 0, 0.0, prev_ref[...].astype(F32))
        a2, a1 = _shift_down(a, prev, 2), _shift_down(a, prev, 1)
        ac = a2 * cw[0:1, :] + a1 * cw[1:2, :] + a * cw[2:3, :] + cb
        du = du_ref[...].astype(F32)
        dup_ref[1] = (du * _gelu(ac)).astype(dup_ref.dtype)
        dac = du * b * _gelu_grad(ac)
        dcb_ref[...] += jnp.sum(dac, axis=0, keepdims=True)
        dcw_ref[0:1, :] += jnp.sum(dac * a2, axis=0, keepdims=True)
        dcw_ref[1:2, :] += jnp.sum(dac * a1, axis=0, keepdims=True)
        dcw_ref[2:3, :] += jnp.sum(dac * a, axis=0, keepdims=True)
        acn = _conv(abn_ref[0].astype(F32), a[tm - GLU_HALO:, :], cw, cb)
        dacn = jnp.where(i % per_seq == per_seq - 1, 0.0,
                         dun_ref[...].astype(F32) * abn_ref[1].astype(F32) * _gelu_grad(acn))
        da = dac * cw[2:3, :] + _shift_up(dac, dacn, 1) * cw[1:2, :] + _shift_up(dac, dacn, 2) * cw[0:1, :]
        dup_ref[0] = da.astype(dup_ref.dtype)

    def nxt(i):
        return jnp.minimum((i + 1) * per_tile, T // GLU_HALO - 1)

    return _pcall(
        body, (up, up, up, d_u, d_u, cw, cb), name="glu_bwd",
        out_shape=(jax.ShapeDtypeStruct((2, FFN_SLABS, T, UP_SHARD), BF16),
                   jax.ShapeDtypeStruct((FFN_SLABS, 3, UP_SHARD), F32),
                   jax.ShapeDtypeStruct((FFN_SLABS, 1, UP_SHARD), F32)),
        grid=(FFN_SLABS, n_tiles),
        in_specs=[pl.BlockSpec((2, None, tm, UP_SHARD), lambda d, i: (0, d, i, 0)),
                  pl.BlockSpec((None, None, GLU_HALO, UP_SHARD),
                               lambda d, i: (0, d, jnp.maximum(i * per_tile - 1, 0), 0)),
                  pl.BlockSpec((2, None, GLU_HALO, UP_SHARD), lambda d, i: (0, d, nxt(i), 0)),
                  pl.BlockSpec((None, tm, UP_SHARD), lambda d, i: (d, i, 0)),
                  pl.BlockSpec((None, GLU_HALO, UP_SHARD), lambda d, i: (d, nxt(i), 0)),
                  pl.BlockSpec((None, 3, UP_SHARD), lambda d, i: (d, 0, 0)),
                  pl.BlockSpec((None, 1, UP_SHARD), lambda d, i: (d, 0, 0))],
        out_specs=(pl.BlockSpec((2, None, tm, UP_SHARD), lambda d, i: (0, d, i, 0)),
                   pl.BlockSpec((None, 3, UP_SHARD), lambda d, i: (d, 0, 0)),
                   pl.BlockSpec((None, 1, UP_SHARD), lambda d, i: (d, 0, 0))),
        sem=("parallel", "arbitrary"), comm=comm)


def _mm_up(h2, w_up_t, tm=MM_ROWS, comm=None):
    T, K = h2.shape
    tm = _tile(T, tm)
    return _matmul(
        "mm_up", "nt", h2, w_up_t, jax.ShapeDtypeStruct((N_DEV, T, UP_SHARD), BF16), (N_DEV, T // tm, 1),
        pl.BlockSpec((tm, K), lambda j, i, k: (i, 0)), pl.BlockSpec((None, UP_SHARD, K), lambda j, i, k: (j, 0, 0)),
        pl.BlockSpec((None, tm, UP_SHARD), lambda j, i, k: (j, i, 0)), (tm, UP_SHARD), comm=comm)


def _loss_epilogue(ffn, operands, outputs, first):
    x1_ref, t_ref, g_ref = operands
    dx_ref, dg_ref, loss_ref = outputs

    @pl.when(first)
    def _():
        dg_ref[...] = jnp.zeros_like(dg_ref)
        loss_ref[...] = jnp.zeros_like(loss_ref)

    xv = x1_ref[...] + ffn
    r = lax.rsqrt(jnp.mean(xv * xv, axis=-1, keepdims=True) + EPS)
    xhat = xv * r
    err = xhat * g_ref[...] - t_ref[...]
    loss_ref[...] += (0.5 / D_MODEL) * jnp.sum(err * err)
    dy = err * (1.0 / D_MODEL)
    dg_ref[...] += jnp.sum(dy * xhat, axis=0, keepdims=True)
    dxhat = dy * g_ref[...]
    dx_ref[...] = r * (dxhat - xhat * jnp.mean(dxhat * xhat, axis=-1, keepdims=True))


def _mm_down_loss(u, w_down, x1, target, g_final, tm=MM_ROWS_RES):
    J, T, n = u.shape
    tm = _tile(T, tm)
    row = pl.BlockSpec((tm, D_MODEL), lambda i, d: (i, 0))
    vec = pl.BlockSpec((1, D_MODEL), lambda i, d: (0, 0))
    vec_shape = jax.ShapeDtypeStruct((1, D_MODEL), F32)
    return _matmul(
        "mm_down", "nn", u, w_down, (jax.ShapeDtypeStruct((T, D_MODEL), F32), vec_shape, vec_shape), (T // tm, J),
        pl.BlockSpec((None, tm, n), lambda i, d: (d, i, 0)), pl.BlockSpec((None, n, D_MODEL), lambda i, d: (d, 0, 0)),
        (row, vec, vec), (tm, D_MODEL), [x1, target, g_final], [row, row, vec], epilogue=_loss_epilogue)


def _mm_down_t(dx, w_down, tm=MM_ROWS):
    T = dx.shape[0]
    J, n, _ = w_down.shape
    tm = _tile(T, tm)
    return _matmul(
        "mm_down_t", "nt", dx, w_down, jax.ShapeDtypeStruct((J, T, n), BF16), (J, T // tm, 1),
        pl.BlockSpec((tm, D_MODEL), lambda d, i, k: (i, 0)), pl.BlockSpec((None, n, D_MODEL), lambda d, i, k: (d, 0, 0)),
        pl.BlockSpec((None, tm, n), lambda d, i, k: (d, i, 0)), (tm, n))


def _mm_dw_down(u, dx, tk=MM_TOKENS):
    J, T, n = u.shape
    tk = _tile(T, tk)
    return _matmul(
        "mm_dw_down", "tn", u, dx, jax.ShapeDtypeStruct((J, n, D_MODEL), BF16), (J, T // tk),
        pl.BlockSpec((None, tk, n), lambda d, k: (d, k, 0)), pl.BlockSpec((tk, D_MODEL), lambda d, k: (k, 0)),
        pl.BlockSpec((None, n, D_MODEL), lambda d, k: (d, 0, 0)), (n, D_MODEL))


def _mm_dw_up(h2, d_up, tk=MM_TOKENS):
    T, K = h2.shape
    tk = _tile(T, tk)
    return _matmul(
        "mm_dw_up", "tn", d_up, h2, jax.ShapeDtypeStruct((N_DEV, UP_SHARD, K), BF16), (N_DEV, T // tk),
        pl.BlockSpec((None, tk, UP_SHARD), lambda j, k: (j, k, 0)), pl.BlockSpec((tk, K), lambda j, k: (k, 0)),
        pl.BlockSpec((None, UP_SHARD, K), lambda j, k: (j, 0, 0)), (UP_SHARD, K))


def _mm_up_t(d_up, w_up_t, rms, tm=MM_ROWS_RES, comm=None):
    J, T, n = d_up.shape
    K = w_up_t.shape[2]
    tm = _tile(T, tm)
    fused = _rms_bwd_fused(T, K, tm, rms)
    return _matmul(
        "mm_up_t", "nn", d_up, w_up_t, fused.pop("out_shape"), (T // tm, J),
        pl.BlockSpec((None, tm, n), lambda i, j: (j, i, 0)), pl.BlockSpec((None, n, K), lambda i, j: (j, 0, 0)),
        fused.pop("o_spec"), (tm, K), comm=comm, **fused)


def _cast_shards(shards):
    def body(*refs):
        n = len(refs) // 2
        for src, dst in zip(refs[:n], refs[n:]):
            dst[...] = src[...].astype(dst.dtype)

    return pl.pallas_call(
        body, out_shape=[jax.ShapeDtypeStruct(s.shape, BF16) for s in shards], name="cast_shards",
        compiler_params=pltpu.CompilerParams(vmem_limit_bytes=VMEM_LIMIT),
    )(*shards)


def _adamw(w, g, m, v):
    m = ADAM_B1 * m + (1.0 - ADAM_B1) * g
    v = ADAM_B2 * v + (1.0 - ADAM_B2) * (g * g)
    m_hat = m / (1.0 - ADAM_B1 ** ADAM_STEP)
    v_hat = v / (1.0 - ADAM_B2 ** ADAM_STEP)
    delta = -ADAM_LR * (m_hat / (jnp.sqrt(v_hat) + ADAM_EPS) + ADAM_WD * w)
    return delta, m, v


def _sum_parts(p_ref):
    g = p_ref[0].astype(F32)
    for d in range(1, N_DEV):
        g = g + p_ref[d].astype(F32)
    return g


def _reduce_adam(name, parts, w, m, v, tr=128):
    R, Cn = w.shape
    by_rows = sum(p.shape[1] for p in parts) == R and len(parts) > 1
    tr = math.gcd(tr, *[p.shape[1] for p in parts])
    n_tiles = [p.shape[1] // tr for p in parts]
    first = [sum(n_tiles[:j]) for j in range(len(parts))] if by_rows else [0] * len(parts)

    def body(*refs):
        p_refs = refs[:len(parts)]
        w_ref, m_ref, v_ref, g_out, d_out, m_out, v_out = refs[len(parts):]

        def update(p_ref):
            g = _sum_parts(p_ref)
            delta, m_new, v_new = _adamw(w_ref[...], g, m_ref[...], v_ref[...])
            g_out[...] = g
            d_out[...] = delta
            m_out[...] = m_new
            v_out[...] = v_new

        if len(parts) == 1:
            update(p_refs[0])
        elif by_rows:
            i = pl.program_id(0)
            for p_ref, t0, n in zip(p_refs, first, n_tiles):
                pl.when((i >= t0) & (i < t0 + n))(functools.partial(update, p_ref))
        else:
            c = lax.axis_index("c")
            for side, p_ref in enumerate(p_refs):
                pl.when(c == side)(functools.partial(update, p_ref))

    def part_spec(t0, n):
        return pl.BlockSpec((N_DEV, tr, Cn), lambda i: (0, jnp.clip(i - t0, 0, n - 1), 0))

    row = pl.BlockSpec((tr, Cn), lambda i: (i, 0))
    shape = jax.ShapeDtypeStruct((R, Cn), F32)
    return pl.pallas_call(
        body, out_shape=(shape,) * 4, grid=(R // tr,),
        in_specs=[part_spec(t0, n) for t0, n in zip(first, n_tiles)] + [row, row, row],
        out_specs=(row,) * 4, name=name, compiler_params=_params(("parallel",)),
    )(*parts, w, m, v)


def _small_adam(name, gathered, params):
    n_g, n_p = len(gathered), len(params)

    def body(*refs):
        g_refs = refs[:n_g]
        wmv = refs[n_g:n_g + 3 * n_p]
        sums = refs[n_g + 3 * n_p:2 * n_g + 3 * n_p]
        upd = refs[2 * n_g + 3 * n_p:]
        for j in range(n_g):
            g = _sum_parts(g_refs[j])
            sums[j][...] = g
            if j < n_p:
                w_ref, m_ref, v_ref = wmv[3 * j:3 * j + 3]
                delta, m_new, v_new = _adamw(w_ref[...], g, m_ref[...], v_ref[...])
                upd[3 * j][...] = delta
                upd[3 * j + 1][...] = m_new
                upd[3 * j + 2][...] = v_new

    flat = [a for wmv in params for a in wmv]
    out_shape = [jax.ShapeDtypeStruct(g.shape[1:], F32) for g in gathered]
    out_shape += [jax.ShapeDtypeStruct(a.shape, F32) for a in flat]
    res = pl.pallas_call(body, out_shape=out_shape, name=name)(*gathered, *flat)
    return res[:n_g], [tuple(res[n_g + 3 * j:n_g + 3 * j + 3]) for j in range(n_p)]


def _adam_only(name, g, w, m, v):
    def body(g_ref, w_ref, m_ref, v_ref, d_out, m_out, v_out):
        delta, m_new, v_new = _adamw(w_ref[...], g_ref[...], m_ref[...], v_ref[...])
        d_out[...] = delta
        m_out[...] = m_new
        v_out[...] = v_new

    shape = jax.ShapeDtypeStruct(w.shape, F32)
    return pl.pallas_call(body, out_shape=(shape,) * 3, name=name)(g, w, m, v)


def kernel(x, mem, g_mix, w_in, w_pool, pool_scale, w_a, g_ret, b_ret, w_r, g_mem, w_mem_kv, w_c, w_out, g_ffn, w_up, conv_w, conv_b, w_down, g_final, loss_target, m_g_mix, m_w_in, m_w_pool, m_pool_scale, m_w_a, m_g_ret, m_b_ret, m_w_r, m_g_mem, m_w_mem_kv, m_w_c, m_w_out, m_g_ffn, m_w_up, m_conv_w, m_conv_b, m_w_down, m_g_final, v_g_mix, v_w_in, v_w_pool, v_pool_scale, v_w_a, v_g_ret, v_b_ret, v_w_r, v_g_mem, v_w_mem_kv, v_w_c, v_w_out, v_g_ffn, v_w_up, v_conv_w, v_conv_b, v_w_down, v_g_final):
    B, S, _ = x.shape
    M = mem.shape[1]
    T = B * S
    me = _my_index()
    x2d = x.reshape(T, D_MODEL)
    mem2d = mem.reshape(B * M, D_MODEL)
    tgt2d = loss_target.reshape(T, D_MODEL)
    g_final2 = g_final.reshape(1, D_MODEL)

    big = dict(w_in=w_in[0], w_a=w_a[0], w_r=w_r[0], w_mem_kv=w_mem_kv[0], w_c=w_c[0], w_out=w_out[0],
               w_up=w_up[0].T, w_down=w_down[0])
    names = list(big)
    cast = dict(zip(names, _cast_shards([big[n] for n in names])))
    cb = conv_b[0].reshape(FFN_SLABS, 1, UP_SHARD)
    wp = w_pool[0]
    tables = _ret_tables(S)

    h = _rms_fwd("rms_mix", x2d, g_mix)
    early = ("w_a", "w_r", "w_mem_kv", "w_c", "w_out")
    (proj, Win), landed = _mm_in_gather(h, cast["w_in"], comm=_Gather([cast[n] for n in early] + [conv_w[0]]))
    W = dict(zip(early, landed))
    (yr, ret_states), (Wup,) = _ret_fwd(proj, g_ret, b_ret, tables, B, S, comm=_Gather([cast["w_up"]]))
    cw_full = landed[-1].transpose(1, 0, 2).reshape(3, FFN_HIDDEN)
    cw = cw_full.reshape(3, FFN_SLABS, UP_SHARD).transpose(1, 0, 2)
    Wa = W["w_a"].transpose(1, 0, 2).reshape(POOL_WIDTH, D_MODEL)
    Wc = W["w_c"].transpose(1, 0, 2).reshape(XA_WIDTH, D_MODEL)
    Wr = W["w_r"].reshape(D_MODEL, D_MODEL)
    Wkv = W["w_mem_kv"].reshape(D_MODEL, D_MODEL)
    Wout = W["w_out"].reshape(D_MODEL, D_MODEL)
    ypre = _pool_fwd(proj, wp, pool_scale, B, S)
    y_pool = _mm_rows("mm_a", ypre, Wa, BF16)
    y_ret = _mm_rows("mm_r", yr, Wr, BF16)
    mem_n = _rms_fwd("rms_mem", mem2d, g_mem)
    kv = _mm_rows("mm_kv", mem_n, Wkv)
    o_mem = _xa_fwd(proj, kv, B, S, M)[0]
    y_mem = _mm_rows("mm_c", o_mem, Wc, BF16)
    ys = (y_pool, y_ret, y_mem)
    merged = _merge_fwd(proj, ys)[0]
    x1, h2 = _mm_residual_rms("mm_out", merged, Wout, x2d, g_ffn)
    up, (Wdown,) = _mm_up(h2, Wup, comm=_Gather([cast["w_down"]]))
    up = up.reshape(2, FFN_SLABS, T, UP_SHARD)
    Wdown = Wdown.reshape(FFN_SLABS, UP_SHARD, D_MODEL)
    u = _glu_fwd(up, cw, cb, S)[0]

    dx2, dg_final, loss_part = _mm_down_loss(u, Wdown, x1, tgt2d, g_final2)
    received = {}
    d_u = _mm_down_t(dx2, Wdown)
    dW_down = _mm_dw_down(u, dx2)
    (d_up, d_cw, d_cb), (received["w_down"],) = _glu_bwd(
        up, d_u, cw, cb, S, comm=_Exchange([dW_down.reshape(N_DEV, -1, D_MODEL)]))
    d_up = d_up.reshape(N_DEV, T, UP_SHARD)
    dW_up = _mm_dw_up(h2, d_up)
    (dx1, dg_ffn), (up_c0,) = _mm_up_t(d_up, Wup, (x1, g_ffn, dx2), comm=_ExchangeTo([dW_up], 0))
    d_merged = _mm_rows("mm_out_t", dx1, Wout, kind="nt")
    dW_out = _mm_tn("mm_dw_out", merged, dx1, BF16)
    (d_gl, d_y_pool, d_y_ret, d_y_mem), (received["w_out"],) = _merge_bwd(
        proj, ys, d_merged, comm=_Exchange([dW_out.reshape(N_DEV, -1, D_MODEL)]))
    dW_c = _mm_tn("mm_dw_c", o_mem, d_y_mem, BF16)
    d_o_mem = _mm_rows("mm_c_t", d_y_mem, Wc, kind="nt")
    (d_qx, d_kmem, d_vmem), (up_c1,) = _xa_bwd(proj, kv, d_o_mem, B, S, M, comm=_ExchangeTo([dW_up], 1))
    received["w_up"] = [up_c0, up_c1]
    d_kv = jnp.concatenate([d_kmem, d_vmem], axis=1)
    dW_kv = _mm_tn("mm_dw_kv", mem_n, d_kv, BF16)
    d_mem_n = _mm_rows("mm_kv_t", d_kv, Wkv, kind="nt")
    dg_mem = _rms_bwd("rms_mem_bwd", mem2d, g_mem, d_mem_n, None)
    dW_a = _mm_tn("mm_dw_a", ypre, d_y_pool, BF16)
    d_ypre = _mm_rows("mm_a_t", d_y_pool, Wa, kind="nt")
    d_hp, dw_pool, d_scale = _pool_bwd(proj, d_ypre, wp, pool_scale, B, S)
    dW_r = _mm_tn("mm_dw_r", yr, d_y_ret, BF16)
    d_yr = _mm_rows("mm_r_t", d_y_ret, Wr, kind="nt")
    (d_q, d_k, d_v, d_gr, dg_ret, db_ret), landed = _ret_bwd(
        proj, ret_states, d_yr, g_ret, b_ret, tables, B, S,
        comm=_Exchange([dW_a.reshape(POOL_WIDTH, N_DEV, -1).transpose(1, 0, 2), dW_r.reshape(N_DEV, -1, D_MODEL),
                        dW_c.reshape(XA_WIDTH, N_DEV, -1).transpose(1, 0, 2), dW_kv.reshape(N_DEV, -1, D_MODEL)]))
    received["w_a"], received["w_r"], received["w_c"], received["w_mem_kv"] = landed
    small_names = ["w_pool", "pool_scale", "g_ret", "b_ret", "g_mem", "g_ffn", "conv_b", "g_final"]
    small_grads = [dw_pool, d_scale, dg_ret, db_ret, dg_mem, dg_ffn, d_cb.reshape(1, FFN_HIDDEN), dg_final,
                   d_cw.transpose(1, 0, 2).reshape(3, FFN_HIDDEN), loss_part]
    d_proj = jnp.concatenate([d_hp, d_q, d_k, d_v, d_gr, d_qx, d_gl], axis=1)
    dW_in0, small_all = _mm_tn_slab("mm_dw_in0", h[:, :W_IN_FIRST_ROWS], d_proj, IN_SHARD, BF16,
                                    comm=_Exchange([], whole=small_grads))
    dW_in1, (in0,) = _mm_tn_slab("mm_dw_in1", h[:, W_IN_FIRST_ROWS:], d_proj, IN_SHARD, BF16,
                                 comm=_Exchange([dW_in0]))
    (grad_x, dg_mix), (in1,) = _mm_cols_slab_t("mm_in_t", d_proj, Win, (x2d, g_mix, dx1), comm=_Exchange([dW_in1]))
    received["w_in"] = [in0, in1]
    (g_mix_all,) = _comm_call("gather_g_mix", _Exchange([], whole=[dg_mix]))

    args = dict(g_mix=g_mix, w_in=w_in, w_pool=w_pool, pool_scale=pool_scale, w_a=w_a, g_ret=g_ret, b_ret=b_ret,
                w_r=w_r, g_mem=g_mem, w_mem_kv=w_mem_kv, w_c=w_c, w_out=w_out, g_ffn=g_ffn, w_up=w_up,
                conv_w=conv_w, conv_b=conv_b, w_down=w_down, g_final=g_final)
    m_in = dict(g_mix=m_g_mix, w_in=m_w_in, w_pool=m_w_pool, pool_scale=m_pool_scale, w_a=m_w_a, g_ret=m_g_ret,
                b_ret=m_b_ret, w_r=m_w_r, g_mem=m_g_mem, w_mem_kv=m_w_mem_kv, w_c=m_w_c, w_out=m_w_out,
                g_ffn=m_g_ffn, w_up=m_w_up, conv_w=m_conv_w, conv_b=m_conv_b, w_down=m_w_down, g_final=m_g_final)
    v_in = dict(g_mix=v_g_mix, w_in=v_w_in, w_pool=v_w_pool, pool_scale=v_pool_scale, w_a=v_w_a, g_ret=v_g_ret,
                b_ret=v_b_ret, w_r=v_w_r, g_mem=v_g_mem, w_mem_kv=v_w_mem_kv, w_c=v_w_c, w_out=v_w_out,
                g_ffn=v_g_ffn, w_up=v_w_up, conv_w=v_conv_w, conv_b=v_conv_b, w_down=v_w_down, g_final=v_g_final)

    grads, deltas, new_m, new_v = {}, {}, {}, {}
    for n in names:
        parts = received[n] if isinstance(received[n], list) else [received[n]]
        flip = (lambda a: a.T) if n == "w_up" else (lambda a: a)
        outs = _reduce_adam("adam_" + n, parts, big[n], flip(m_in[n][0]), flip(v_in[n][0]))
        for store, val in zip((grads, deltas, new_m, new_v), outs):
            store[n] = flip(val)[None]

    def as_small(a):
        return a.reshape(a.shape[-3:]) if a.ndim > 2 else a.reshape(1, -1)

    def small_update(call_name, param_names, gathered):
        params = [tuple(as_small(d[n]) for d in (args, m_in, v_in)) for n in param_names]
        sums, updates = _small_adam(call_name, gathered, params)
        for n, g, (d_, m_, v_) in zip(param_names, sums, updates):
            shape = args[n].shape
            grads[n], deltas[n], new_m[n], new_v[n] = (a.reshape(shape) for a in (g, d_, m_, v_))
        return sums[len(param_names):]

    g_cw_full, loss_row = small_update("adam_small", small_names, small_all)
    loss = loss_row[0, 0]
    small_update("adam_g_mix", ["g_mix"], [g_mix_all])

    shard_cols = FFN_HIDDEN // N_DEV
    g_cw = lax.dynamic_slice_in_dim(g_cw_full, me * shard_cols, shard_cols, axis=1)
    d_, m_, v_ = _adam_only("adam_conv_w", g_cw, conv_w[0], m_conv_w[0], v_conv_w[0])
    grads["conv_w"], deltas["conv_w"], new_m["conv_w"], new_v["conv_w"] = g_cw[None], d_[None], m_[None], v_[None]

    order = ["g_mix", "w_in", "w_pool", "pool_scale", "w_a", "g_ret", "b_ret", "w_r", "g_mem", "w_mem_kv", "w_c",
             "w_out", "g_ffn", "w_up", "conv_w", "conv_b", "w_down", "g_final"]
    return (loss, grad_x.reshape(B, S, D_MODEL), *[grads[n] for n in order], *[deltas[n] for n in order],
            *[new_m[n] for n in order], *[new_v[n] for n in order])
```

```python
import functools
import math

import jax
import jax.numpy as jnp
from jax import lax
from jax.experimental import pallas as pl
from jax.experimental.pallas import tpu as pltpu

F32 = jnp.float32
BF16 = jnp.bfloat16

N_DEV = 8
D_MODEL = 1024
POOL_WINDOWS = (2, 4, 8, 16)
POOL_GROUP_DIM = 128
POOL_WIDTH = 512
POOL_HALO = 16
RET_HEADS = 4
RET_QK_DIM = 128
RET_V_DIM = 256
RET_CHUNK = 128
ROPE_BASE = 10000.0
XA_HEADS = 4
XA_HEAD_DIM = 128
XA_WIDTH = 512
IN_WIDTH = 7168
IN_SHARD = IN_WIDTH // N_DEV
FFN_HIDDEN = 2816
UP_SHARD = 2 * FFN_HIDDEN // N_DEV
FFN_SLABS = FFN_HIDDEN // UP_SHARD
EPS = 1e-6
ADAM_LR = 0.001
ADAM_B1 = 0.9
ADAM_B2 = 0.999
ADAM_EPS = 1e-08
ADAM_WD = 0.01
ADAM_STEP = 10
GELU_C = math.sqrt(2.0 / math.pi)
GELU_A = 0.044715
VMEM_LIMIT = 56 * 1024 * 1024
MM_ROWS = 2048
MM_ROWS_RES = 1024
MM_TOKENS = 2048
W_IN_FIRST_ROWS = 384
MESH = pl.DeviceIdType.MESH

COL_Q, COL_K, COL_V, COL_GR, COL_QX, COL_GL = 512, 1024, 1536, 2560, 3584, 4096

_DIMS = {
    "nn": (((1,), (0,)), ((), ())),
    "nt": (((1,), (1,)), ((), ())),
    "tn": (((0,), (0,)), ((), ())),
}


def _dot(a, b, kind="nn"):
    return lax.dot_general(a.astype(BF16), b.astype(BF16), _DIMS[kind], preferred_element_type=F32)


def _params(sem, vmem=VMEM_LIMIT):
    return pltpu.CompilerParams(dimension_semantics=sem, vmem_limit_bytes=vmem)


def _tile(n, pref):
    t = min(n, pref)
    while n % t:
        t //= 2
    return t


def _mesh_pos():
    return lax.axis_index("x"), lax.axis_index("y"), lax.axis_index("c")


def _dev_index(x, y, c):
    return 4 * x + 2 * y + c


def _my_index():
    return _dev_index(*_mesh_pos())


def _remote(src, dst, send_sems, recv_sems, s, to):
    return pltpu.make_async_remote_copy(src_ref=src, dst_ref=dst, send_sem=send_sems.at[s], recv_sem=recv_sems.at[s],
                                        device_id=to, device_id_type=MESH)


class _Gather:
    def __init__(self, shards):
        self.inputs = list(shards)
        self.out_shapes = [jax.ShapeDtypeStruct((N_DEV,) + s.shape, s.dtype) for s in shards]
        n = len(shards)
        self.sem_shapes = [pltpu.SemaphoreType.DMA((7 * n,)), pltpu.SemaphoreType.DMA((7 * n,)),
                           pltpu.SemaphoreType.DMA((n,))]

    def _places(self):
        x, y, c = _mesh_pos()
        return (x, y, c), (x, y, 1 - c), [(1 - x, y), (x, 1 - y), (1 - x, 1 - y)]

    def _local(self, src, dst, sems):
        me = _my_index()
        return [pltpu.make_async_copy(src[w], dst[w].at[me], sems[2].at[w]) for w in range(len(src))]

    def start(self, src, dst, sems):
        me, sib, chips = self._places()
        for cp in self._local(src, dst, sems):
            cp.start()
        for w in range(len(src)):
            land = dst[w].at[_dev_index(*me)]
            _remote(src[w], land, sems[0], sems[1], 7 * w, sib).start()
            for j, chip in enumerate(chips):
                _remote(src[w], land, sems[0], sems[1], 7 * w + 1 + j, (*chip, me[2])).start()

    def middle(self, src, dst, sems):
        me, sib, chips = self._places()
        for j, chip in enumerate(chips):
            for w in range(len(src)):
                block = dst[w].at[_dev_index(*chip, me[2])]
                _remote(src[w], block, sems[0], sems[1], 7 * w + 1 + j, me).wait_recv()
                _remote(block, block, sems[0], sems[1], 7 * w + 4 + j, sib).start()

    def finish(self, src, dst, sems):
        me, sib, chips = self._places()
        n = len(src)
        for w in range(n):
            _remote(src[w], dst[w].at[_dev_index(*sib)], sems[0], sems[1], 7 * w, me).wait_recv()
            for j, chip in enumerate(chips):
                block = dst[w].at[_dev_index(*chip, sib[2])]
                _remote(block, block, sems[0], sems[1], 7 * w + 4 + j, me).wait_recv()
            for k in range(7):
                _remote(src[w], dst[w].at[0], sems[0], sems[1], 7 * w + k, me).wait_send()
        for cp in self._local(src, dst, sems):
            cp.wait()


class _Exchange:
    def __init__(self, partials, whole=()):
        self.n_part = len(partials)
        self.inputs = list(partials) + list(whole)
        self.out_shapes = [jax.ShapeDtypeStruct(p.shape, p.dtype) for p in partials]
        self.out_shapes += [jax.ShapeDtypeStruct((N_DEV,) + a.shape, a.dtype) for a in whole]
        n = len(self.inputs)
        self.sem_shapes = [pltpu.SemaphoreType.DMA((7 * n,)), pltpu.SemaphoreType.DMA((7 * n,)),
                           pltpu.SemaphoreType.DMA((n,))]

    def _peer(self, k):
        x, y, c = _mesh_pos()
        p = (x ^ ((k >> 2) & 1), y ^ ((k >> 1) & 1), c ^ (k & 1))
        return p, _dev_index(*p)

    def _source(self, src, w, slot):
        return src[w].at[slot] if w < self.n_part else src[w]

    def _local(self, src, dst, sems):
        me = _my_index()
        return [pltpu.make_async_copy(self._source(src, w, me), dst[w].at[me], sems[2].at[w])
                for w in range(len(src))]

    def start(self, src, dst, sems):
        me = _my_index()
        for cp in self._local(src, dst, sems):
            cp.start()
        for k in range(1, N_DEV):
            peer, peer_idx = self._peer(k)
            for w in range(len(src)):
                _remote(self._source(src, w, peer_idx), dst[w].at[me], sems[0], sems[1], 7 * w + k - 1, peer).start()

    def finish(self, src, dst, sems):
        for k in range(1, N_DEV):
            peer, peer_idx = self._peer(k)
            for w in range(len(src)):
                cp = _remote(self._source(src, w, peer_idx), dst[w].at[peer_idx], sems[0], sems[1], 7 * w + k - 1, peer)
                cp.wait_send()
                cp.wait_recv()
        for cp in self._local(src, dst, sems):
            cp.wait()


class _ExchangeTo:
    def __init__(self, partials, side):
        self.side = side
        self.inputs = list(partials)
        self.out_shapes = [jax.ShapeDtypeStruct(p.shape, p.dtype) for p in partials]
        n = len(partials)
        self.sem_shapes = [pltpu.SemaphoreType.DMA((7 * n,)), pltpu.SemaphoreType.DMA((7 * n,)),
                           pltpu.SemaphoreType.DMA((n,))]

    def _copies(self, src, dst, sems):
        x, y, c = _mesh_pos()
        me = _dev_index(x, y, c)
        receives = c == self.side
        remote = []
        for k in range(1, N_DEV):
            kx, ky, kc = (k >> 2) & 1, (k >> 1) & 1, k & 1
            peer = (x ^ kx, y ^ ky, c ^ kc)
            peer_idx = _dev_index(*peer)
            sends = c == (self.side ^ kc)
            for w in range(len(src)):
                slab = src[w].at[peer_idx]
                s = 7 * w + k - 1
                remote.append((sends, _remote(slab, dst[w].at[me], sems[0], sems[1], s, peer),
                               _remote(slab, dst[w].at[peer_idx], sems[0], sems[1], s, peer)))
        local = [pltpu.make_async_copy(src[w].at[me], dst[w].at[me], sems[2].at[w]) for w in range(len(src))]
        return receives, remote, local

    def start(self, src, dst, sems):
        receives, remote, local = self._copies(src, dst, sems)

        @pl.when(receives)
        def _():
            for cp in local:
                cp.start()

        for sends, send, _ in remote:
            pl.when(sends)(send.start)

    def finish(self, src, dst, sems):
        receives, remote, local = self._copies(src, dst, sems)
        for sends, send, arrive in remote:
            pl.when(sends)(send.wait_send)
            pl.when(receives)(arrive.wait_recv)

        @pl.when(receives)
        def _():
            for cp in local:
                cp.wait()


def _pcall(body, args, *, name, out_shape, grid, in_specs, out_specs, scratch_shapes=(), sem=None, comm=None):
    single = not isinstance(out_shape, (tuple, list))
    outs = [out_shape] if single else list(out_shape)
    ospecs = [out_specs] if single else list(out_specs)
    n_in, n_out, n_scr = len(args), len(outs), len(scratch_shapes)

    def pick(res):
        return res[0] if single else tuple(res[:n_out])

    if comm is None:
        res = pl.pallas_call(
            body, out_shape=outs, grid=grid, in_specs=list(in_specs), out_specs=ospecs,
            scratch_shapes=list(scratch_shapes), name=name, compiler_params=_params(sem),
        )(*args)
        return pick(res), ()

    nci, nco = len(comm.inputs), len(comm.out_shapes)

    def carrier(*refs):
        at = 0
        parts = []
        for size in (n_in, nci, n_out, nco, n_scr, len(comm.sem_shapes)):
            parts.append(refs[at:at + size])
            at += size
        ins, cins, o, couts, scr, sems = parts
        ids = [pl.program_id(a) for a in range(len(grid))]
        first = functools.reduce(jnp.logical_and, [i == 0 for i in ids])
        last = functools.reduce(jnp.logical_and, [i == g - 1 for i, g in zip(ids, grid)])

        body(*ins, *o, *scr)

        @pl.when(first)
        def _():
            comm.start(cins, couts, sems)

        if hasattr(comm, "middle"):
            steps = math.prod(grid)
            at = functools.reduce(lambda lin, ig: lin * ig[1] + ig[0], zip(ids, grid), 0)

            @pl.when(at == min(steps - 1, (3 * steps) // 4))
            def _():
                comm.middle(cins, couts, sems)

        @pl.when(last)
        def _():
            comm.finish(cins, couts, sems)

    hbm = pl.BlockSpec(memory_space=pltpu.HBM)
    res = pl.pallas_call(
        carrier, out_shape=outs + comm.out_shapes, grid=grid, in_specs=list(in_specs) + [hbm] * nci,
        out_specs=ospecs + [hbm] * nco, scratch_shapes=list(scratch_shapes) + comm.sem_shapes, name=name,
        compiler_params=_params(("arbitrary",) * len(grid)),
    )(*args, *comm.inputs)
    return pick(res), tuple(res[n_out:])


def _comm_call(name, comm):
    def body(*refs):
        nci, nco = len(comm.inputs), len(comm.out_shapes)
        cins, couts, sems = refs[:nci], refs[nci:nci + nco], refs[nci + nco:]
        comm.start(cins, couts, sems)
        if hasattr(comm, "middle"):
            comm.middle(cins, couts, sems)
        comm.finish(cins, couts, sems)

    hbm = pl.BlockSpec(memory_space=pltpu.HBM)
    return pl.pallas_call(
        body, out_shape=comm.out_shapes, in_specs=[hbm] * len(comm.inputs), out_specs=[hbm] * len(comm.out_shapes),
        scratch_shapes=comm.sem_shapes, name=name,
    )(*comm.inputs)


def _matmul(name, kind, a, b, out_shape, grid, a_spec, b_spec, o_spec, acc_shape, res=None, res_spec=None,
            comm=None, epilogue=None):
    nk = grid[-1]
    if epilogue is None:
        extra, extra_specs = ([res], [res_spec]) if res is not None else ([], [])
        n_out = 1
    else:
        extra, extra_specs, n_out = list(res), list(res_spec), len(out_shape)
    n_in = 2 + len(extra)

    def body(*refs):
        a_ref, b_ref = refs[0], refs[1]
        extra_refs, out_refs = refs[2:n_in], refs[n_in:n_in + n_out]

        def prod():
            return _dot(a_ref[...], b_ref[...], kind)

        def finish(acc):
            if epilogue is not None:
                ids = [pl.program_id(ax) for ax in range(len(grid) - 1)]
                first = functools.reduce(jnp.logical_and, [i == 0 for i in ids]) if ids else True
                epilogue(acc, extra_refs, out_refs, first)
                return
            if extra_refs:
                acc = acc + extra_refs[0][...]
            out_refs[0][...] = acc.astype(out_refs[0].dtype)

        if nk == 1:
            finish(prod())
        else:
            acc_ref = refs[n_in + n_out]
            k = pl.program_id(len(grid) - 1)

            @pl.when(k == 0)
            def _():
                acc_ref[...] = prod()

            @pl.when(k > 0)
            def _():
                acc_ref[...] += prod()

            @pl.when(k == nk - 1)
            def _():
                finish(acc_ref[...])

    in_specs = [a_spec, b_spec] + extra_specs
    args = (a, b, *extra)
    scratch = [pltpu.VMEM(acc_shape, F32)] if nk > 1 else []
    sem = ("arbitrary",) * len(grid) if epilogue is not None else ("parallel",) * (len(grid) - 1) + ("arbitrary",)
    out, landed = _pcall(body, args, name=name, out_shape=out_shape, grid=grid, in_specs=in_specs,
                         out_specs=o_spec, scratch_shapes=scratch, sem=sem, comm=comm)
    return out if comm is None else (out, landed)


def _mm_rows(name, a, w, out_dtype=F32, res=None, kind="nn", tm=MM_ROWS, comm=None):
    M, K = a.shape
    N = w.shape[1] if kind == "nn" else w.shape[0]
    tm = _tile(M, tm)
    res_spec = pl.BlockSpec((tm, N), lambda i, k: (i, 0)) if res is not None else None
    return _matmul(
        name, kind, a, w, jax.ShapeDtypeStruct((M, N), out_dtype), (M // tm, 1),
        pl.BlockSpec((tm, K), lambda i, k: (i, 0)), pl.BlockSpec(w.shape, lambda i, k: (0, 0)),
        pl.BlockSpec((tm, N), lambda i, k: (i, 0)), (tm, N), res, res_spec, comm)


def _residual_rms_epilogue(y, operands, outputs, first):
    x_ref, g_ref = operands
    x1_ref, h_ref = outputs
    xv = x_ref[...] + y
    x1_ref[...] = xv
    r = lax.rsqrt(jnp.mean(xv * xv, axis=-1, keepdims=True) + EPS)
    h_ref[...] = (xv * r * g_ref[...]).astype(h_ref.dtype)


def _mm_residual_rms(name, a, w, x, g, tm=MM_ROWS_RES):
    M, K = a.shape
    N = w.shape[1]
    tm = _tile(M, tm)
    row = pl.BlockSpec((tm, N), lambda i, k: (i, 0))
    return _matmul(
        name, "nn", a, w, (jax.ShapeDtypeStruct((M, N), F32), jax.ShapeDtypeStruct((M, N), BF16)), (M // tm, 1),
        pl.BlockSpec((tm, K), lambda i, k: (i, 0)), pl.BlockSpec(w.shape, lambda i, k: (0, 0)),
        (row, row), (tm, N), [x, g], [row, pl.BlockSpec((1, N), lambda i, k: (0, 0))],
        epilogue=_residual_rms_epilogue)


def _mm_tn(name, a, b, out_dtype=F32, tk=MM_TOKENS, comm=None):
    T, M = a.shape
    N = b.shape[1]
    tk = _tile(T, tk)
    return _matmul(
        name, "tn", a, b, jax.ShapeDtypeStruct((M, N), out_dtype), (1, T // tk),
        pl.BlockSpec((tk, M), lambda i, k: (k, 0)), pl.BlockSpec((tk, N), lambda i, k: (k, 0)),
        pl.BlockSpec((M, N), lambda i, k: (0, 0)), (M, N), comm=comm)


def _mm_in_gather(h, shard, tm=MM_ROWS, comm=None):
    T, K = h.shape
    n = shard.shape[1]
    tm = _tile(T, tm)
    n_tiles = T // tm
    pair_of_chip_step = {4: 1, 2: 2, 6: 3}

    def slab_of(s):
        x, y, c = _mesh_pos()
        return _dev_index(x ^ ((s >> 2) & 1), y ^ ((s >> 1) & 1), c ^ (s & 1))

    def body(h_ref, shard_ref, proj_ref, win_ref, wbuf, slot_sems, send_sems, recv_sems, local_sem):
        s, i = pl.program_id(0), pl.program_id(1)
        x, y, c = _mesh_pos()
        me, sib = (x, y, c), (x, y, 1 - c)

        def slot_copy(step):
            src = shard_ref if step == 0 else win_ref.at[slab_of(step)]
            return pltpu.make_async_copy(src, wbuf.at[step % 2], slot_sems.at[step % 2])

        def fetch(step):
            if step >= 1:
                block = win_ref.at[slab_of(step)]
                if step == 1:
                    pair = 0
                elif step % 2 == 0:
                    pair = pair_of_chip_step[step]
                else:
                    pair = 3 + pair_of_chip_step[step - 1]
                _remote(block, block, send_sems, recv_sems, pair, me).wait_recv()
                if step % 2 == 0:
                    _remote(block, block, send_sems, recv_sems, 3 + pair, sib).start()
            slot_copy(step).start()

        @pl.when((s == 0) & (i == 0))
        def _():
            land = win_ref.at[_dev_index(*me)]
            pltpu.make_async_copy(shard_ref, land, local_sem).start()
            _remote(shard_ref, land, send_sems, recv_sems, 0, sib).start()
            for step, pair in pair_of_chip_step.items():
                peer = (x ^ ((step >> 2) & 1), y ^ ((step >> 1) & 1), c)
                _remote(shard_ref, land, send_sems, recv_sems, pair, peer).start()
            fetch(0)

        for step in range(N_DEV):
            @pl.when((s == step) & (i == 0))
            def _():
                slot_copy(step).wait()

            if step + 1 < N_DEV:
                @pl.when((s == step) & (i == n_tiles - 1))
                def _():
                    fetch(step + 1)

        proj_ref[...] = _dot(h_ref[...], wbuf[s % 2])

        @pl.when((s == N_DEV - 1) & (i == n_tiles - 1))
        def _():
            for pair in range(7):
                _remote(shard_ref, win_ref.at[0], send_sems, recv_sems, pair, me).wait_send()
            pltpu.make_async_copy(shard_ref, win_ref.at[_dev_index(*me)], local_sem).wait()

    hbm = pl.BlockSpec(memory_space=pltpu.HBM)
    return _pcall(
        body, (h, shard), name="mm_in",
        out_shape=(jax.ShapeDtypeStruct((T, N_DEV * n), F32), jax.ShapeDtypeStruct((N_DEV, K, n), shard.dtype)),
        grid=(N_DEV, n_tiles), in_specs=[pl.BlockSpec((tm, K), lambda s, i: (i, 0)), hbm],
        out_specs=(pl.BlockSpec((tm, n), lambda s, i: (i, slab_of(s))), hbm),
        scratch_shapes=[pltpu.VMEM((2, K, n), shard.dtype), pltpu.SemaphoreType.DMA((2,)),
                        pltpu.SemaphoreType.DMA((7,)), pltpu.SemaphoreType.DMA((7,)), pltpu.SemaphoreType.DMA],
        sem=("arbitrary", "arbitrary"), comm=comm)


def _rms_bwd_epilogue(dh, operands, outputs, first):
    x_ref, g_ref, dres_ref = operands
    dx_ref, dg_ref = outputs
    xv = x_ref[...]
    r = lax.rsqrt(jnp.mean(xv * xv, axis=-1, keepdims=True) + EPS)
    xhat = xv * r

    @pl.when(first)
    def _():
        dg_ref[...] = jnp.zeros_like(dg_ref)

    dg_ref[...] += jnp.sum(dh * xhat, axis=0, keepdims=True)
    dxhat = dh * g_ref[...]
    dx_ref[...] = dres_ref[...] + r * (dxhat - xhat * jnp.mean(dxhat * xhat, axis=-1, keepdims=True))


def _rms_bwd_fused(M, K, tm, rms):
    row = pl.BlockSpec((tm, K), lambda i, j: (i, 0))
    vec = pl.BlockSpec((1, K), lambda i, j: (0, 0))
    x, g, dres = rms
    return dict(res=[x, g, dres], res_spec=[row, vec, row], epilogue=_rms_bwd_epilogue,
                out_shape=(jax.ShapeDtypeStruct((M, K), F32), jax.ShapeDtypeStruct((1, K), F32)), o_spec=(row, vec))


def _mm_cols_slab_t(name, a, w_slabs, rms, tm=MM_ROWS_RES, comm=None):
    M = a.shape[0]
    J, K, n = w_slabs.shape
    tm = _tile(M, tm)
    fused = _rms_bwd_fused(M, K, tm, rms)
    return _matmul(
        name, "nt", a, w_slabs, fused.pop("out_shape"), (M // tm, J),
        pl.BlockSpec((tm, n), lambda i, j: (i, j)), pl.BlockSpec((None, K, n), lambda i, j: (j, 0, 0)),
        fused.pop("o_spec"), (tm, K), comm=comm, **fused)


def _mm_tn_slab(name, a, b, n, out_dtype=F32, tk=MM_TOKENS, comm=None, part=(0, 1)):
    T, M = a.shape
    p, of = part
    M //= of
    J = b.shape[1] // n
    tk = _tile(T, tk)
    return _matmul(
        name, "tn", a, b, jax.ShapeDtypeStruct((J, M, n), out_dtype), (J, T // tk),
        pl.BlockSpec((tk, M), lambda j, k: (k, p)), pl.BlockSpec((tk, n), lambda j, k: (k, j)),
        pl.BlockSpec((None, M, n), lambda j, k: (j, 0, 0)), (M, n), comm=comm)


def _rms_fwd(name, x, g, tm=512):
    T, Dm = x.shape
    tm = _tile(T, tm)

    def body(x_ref, g_ref, h_ref):
        xv = x_ref[...]
        r = lax.rsqrt(jnp.mean(xv * xv, axis=-1, keepdims=True) + EPS)
        h_ref[...] = (xv * r * g_ref[...]).astype(h_ref.dtype)

    return pl.pallas_call(
        body, out_shape=jax.ShapeDtypeStruct((T, Dm), BF16), grid=(T // tm,),
        in_specs=[pl.BlockSpec((tm, Dm), lambda i: (i, 0)), pl.BlockSpec((1, Dm), lambda i: (0, 0))],
        out_specs=pl.BlockSpec((tm, Dm), lambda i: (i, 0)), name=name, compiler_params=_params(("parallel",)),
    )(x, g)


def _rms_bwd(name, x, g, dh, dres, tm=512):
    T, Dm = x.shape
    tm = _tile(T, tm)
    want_dx = dres is not None

    def body(*refs):
        if want_dx:
            x_ref, g_ref, dh_ref, dres_ref, dx_ref, dg_ref = refs
        else:
            x_ref, g_ref, dh_ref, dg_ref = refs
        xv = x_ref[...]
        r = lax.rsqrt(jnp.mean(xv * xv, axis=-1, keepdims=True) + EPS)
        xhat = xv * r
        dhv = dh_ref[...]

        @pl.when(pl.program_id(0) == 0)
        def _():
            dg_ref[...] = jnp.zeros_like(dg_ref)

        dg_ref[...] += jnp.sum(dhv * xhat, axis=0, keepdims=True)
        if want_dx:
            dxhat = dhv * g_ref[...]
            dx_ref[...] = dres_ref[...] + r * (dxhat - xhat * jnp.mean(dxhat * xhat, axis=-1, keepdims=True))

    row = pl.BlockSpec((tm, Dm), lambda i: (i, 0))
    vec = pl.BlockSpec((1, Dm), lambda i: (0, 0))
    if want_dx:
        return pl.pallas_call(
            body, out_shape=(jax.ShapeDtypeStruct((T, Dm), F32), jax.ShapeDtypeStruct((1, Dm), F32)),
            grid=(T // tm,), in_specs=[row, vec, row, row], out_specs=(row, vec), name=name,
            compiler_params=_params(("arbitrary",)),
        )(x, g, dh, dres)
    return pl.pallas_call(
        body, out_shape=jax.ShapeDtypeStruct((1, Dm), F32), grid=(T // tm,), in_specs=[row, vec, row],
        out_specs=vec, name=name, compiler_params=_params(("arbitrary",)),
    )(x, g, dh)


def _pool_rows(S):
    return _tile(S, 256)


def _pool_count(c0, rows, w):
    t = c0 + lax.broadcasted_iota(jnp.int32, (rows, 1), 0)
    return jnp.minimum(t + 1, w).astype(F32)


def _pool_fwd(proj, w_pool, scale, B, S):
    CH = _pool_rows(S)

    def body(hp_ref, wp_ref, sc_ref, o_ref, pad_ref):
        pad_ref[0:POOL_HALO, :] = jnp.zeros((POOL_HALO, POOL_WIDTH), F32)
        pad_ref[POOL_HALO:, :] = hp_ref[...]
        for gi, w in enumerate(POOL_WINDOWS):
            cols = slice(gi * POOL_GROUP_DIM, (gi + 1) * POOL_GROUP_DIM)
            for c in range(S // CH):
                base = POOL_HALO + c * CH
                acc = pad_ref[base:base + CH, cols]
                tok = acc
                for j in range(1, w):
                    acc = acc + pad_ref[base - j:base - j + CH, cols]
                pooled = acc / _pool_count(c * CH, CH, w) - tok
                z = _dot(pooled, wp_ref[gi])
                o_ref[c * CH:(c + 1) * CH, cols] = (z * sc_ref[:, cols]).astype(o_ref.dtype)

    return pl.pallas_call(
        body, out_shape=jax.ShapeDtypeStruct((B * S, POOL_WIDTH), BF16), grid=(B,),
        in_specs=[pl.BlockSpec((S, POOL_WIDTH), lambda b: (b, 0)),
                  pl.BlockSpec(w_pool.shape, lambda b: (0, 0, 0)),
                  pl.BlockSpec((1, POOL_WIDTH), lambda b: (0, 0))],
        out_specs=pl.BlockSpec((S, POOL_WIDTH), lambda b: (b, 0)),
        scratch_shapes=[pltpu.VMEM((S + POOL_HALO, POOL_WIDTH), F32)],
        name="pool_fwd", compiler_params=_params(("parallel",)),
    )(proj, w_pool, scale)


def _pool_bwd(proj, d_ypre, w_pool, scale, B, S, comm=None):
    CH = _pool_rows(S)

    def body(hp_ref, dy_ref, wp_ref, sc_ref, dhp_ref, dwp_ref, dsc_ref, pad_ref, sc_pad_ref, dp_ref):
        @pl.when(pl.program_id(0) == 0)
        def _():
            dwp_ref[...] = jnp.zeros_like(dwp_ref)
            dsc_ref[...] = jnp.zeros_like(dsc_ref)

        pad_ref[0:POOL_HALO, :] = jnp.zeros((POOL_HALO, POOL_WIDTH), F32)
        pad_ref[POOL_HALO:, :] = hp_ref[...]
        sc_pad_ref[S:, :] = jnp.zeros((POOL_HALO, POOL_WIDTH), F32)
        for gi, w in enumerate(POOL_WINDOWS):
            cols = slice(gi * POOL_GROUP_DIM, (gi + 1) * POOL_GROUP_DIM)
            for c in range(S // CH):
                base = POOL_HALO + c * CH
                rows = slice(c * CH, (c + 1) * CH)
                acc = pad_ref[base:base + CH, cols]
                tok = acc
                for j in range(1, w):
                    acc = acc + pad_ref[base - j:base - j + CH, cols]
                cnt = _pool_count(c * CH, CH, w)
                pooled = acc / cnt - tok
                z = _dot(pooled, wp_ref[gi])
                dy = dy_ref[rows, cols]
                dsc_ref[:, cols] += jnp.sum(dy * z, axis=0, keepdims=True)
                dz = dy * sc_ref[:, cols]
                dwp_ref[gi] += _dot(pooled, dz, "tn")
                dpool = _dot(dz, wp_ref[gi], "nt")
                dp_ref[rows, cols] = dpool
                sc_pad_ref[rows, cols] = dpool / cnt
            for c in range(S // CH):
                rows = slice(c * CH, (c + 1) * CH)
                acc = sc_pad_ref[rows, cols]
                for j in range(1, w):
                    acc = acc + sc_pad_ref[c * CH + j:c * CH + j + CH, cols]
                dhp_ref[rows, cols] = (acc - dp_ref[rows, cols]).astype(dhp_ref.dtype)

    seq = pl.BlockSpec((S, POOL_WIDTH), lambda b: (b, 0))
    return _pcall(
        body, (proj, d_ypre, w_pool, scale), name="pool_bwd",
        out_shape=(jax.ShapeDtypeStruct((B * S, POOL_WIDTH), BF16),
                   jax.ShapeDtypeStruct(w_pool.shape, F32), jax.ShapeDtypeStruct((1, POOL_WIDTH), F32)),
        grid=(B,),
        in_specs=[seq, seq, pl.BlockSpec(w_pool.shape, lambda b: (0, 0, 0)),
                  pl.BlockSpec((1, POOL_WIDTH), lambda b: (0, 0))],
        out_specs=(seq, pl.BlockSpec(w_pool.shape, lambda b: (0, 0, 0)),
                   pl.BlockSpec((1, POOL_WIDTH), lambda b: (0, 0))),
        scratch_shapes=[pltpu.VMEM((S + POOL_HALO, POOL_WIDTH), F32),
                        pltpu.VMEM((S + POOL_HALO, POOL_WIDTH), F32),
                        pltpu.VMEM((S, POOL_WIDTH), F32)],
        sem=("arbitrary",), comm=comm)


def _ret_tables(S):
    half = RET_QK_DIM // 2
    inv = ROPE_BASE ** (-jnp.arange(half, dtype=F32) / half)
    ang = jnp.arange(S, dtype=F32)[:, None] * inv[None, :]
    cos, sin = jnp.cos(ang), jnp.sin(ang)
    cos_full = jnp.concatenate([cos, cos], axis=-1)
    sin_signed = jnp.concatenate([-sin, sin], axis=-1)
    C = RET_CHUNK
    lg = jnp.log1p(-jnp.exp2(-5.0 - jnp.arange(RET_HEADS, dtype=F32)))[:, None, None]
    idx = jnp.arange(C, dtype=F32)
    rel = idx[:, None] - idx[None, :]
    decay = jnp.where(rel >= 0, jnp.exp(jnp.maximum(rel, 0.0) * lg), 0.0)
    q_decay = jnp.broadcast_to(jnp.exp((idx + 1.0)[None, :, None] * lg), (RET_HEADS, C, RET_QK_DIM))
    k_decay = jnp.broadcast_to(jnp.exp((C - 1.0 - idx)[None, :, None] * lg), (RET_HEADS, C, RET_QK_DIM))
    c_decay = jnp.broadcast_to(jnp.exp(C * lg), (RET_HEADS, 1, RET_V_DIM))
    return cos_full, sin_signed, decay, q_decay, k_decay, c_decay


def _rope(x, cos_full, sin_signed):
    return x * cos_full + pltpu.roll(x, RET_QK_DIM // 2, axis=1) * sin_signed


def _rope_t(dy, cos_full, sin_signed):
    return dy * cos_full + pltpu.roll(dy * sin_signed, RET_QK_DIM // 2, axis=1)


RET_COLS = 512


def _ret_specs(N, chunk_of):
    C = RET_CHUNK

    def rows(width, col=0):
        return pl.BlockSpec((C, width), lambda b, i: (b * N + chunk_of(i), col))

    def whole(shape):
        return pl.BlockSpec(shape, lambda b, i: (0,) * len(shape))

    wide = RET_HEADS * RET_V_DIM
    return dict(
        q=rows(RET_COLS, COL_Q // RET_COLS), k=rows(RET_COLS, COL_K // RET_COLS),
        v=[rows(RET_COLS, COL_V // RET_COLS + j) for j in range(2)],
        gr=[rows(RET_COLS, COL_GR // RET_COLS + j) for j in range(2)],
        table=pl.BlockSpec((C, RET_QK_DIM), lambda b, i: (chunk_of(i), 0)),
        decay=whole((RET_HEADS, C, C)), qd=whole((RET_HEADS, C, RET_QK_DIM)), kd=whole((RET_HEADS, C, RET_QK_DIM)),
        cd=whole((RET_HEADS, 1, RET_V_DIM)), vec=whole((1, wide)), qk_rows=rows(RET_COLS), v_rows=rows(wide),
        state=pl.BlockSpec((None, None, RET_HEADS, RET_QK_DIM, RET_V_DIM), lambda b, i: (b, chunk_of(i), 0, 0, 0)))


def _head_cols(h):
    pair = slice((h % 2) * RET_V_DIM, (h % 2 + 1) * RET_V_DIM)
    return slice(h * RET_QK_DIM, (h + 1) * RET_QK_DIM), h // 2, pair, slice(h * RET_V_DIM, (h + 1) * RET_V_DIM)


def _group_norm(o):
    mu = jnp.mean(o, axis=-1, keepdims=True)
    oc = o - mu
    rstd = lax.rsqrt(jnp.mean(oc * oc, axis=-1, keepdims=True) + EPS)
    return oc * rstd, rstd


def _ret_fwd(proj, g_ret, b_ret, tables, B, S, comm=None):
    N = S // RET_CHUNK
    cos_t, sin_t, decay, q_decay, k_decay, c_decay = tables
    sp = _ret_specs(N, lambda i: i)

    def body(q_ref, k_ref, v0_ref, v1_ref, gr0_ref, gr1_ref, cos_ref, sin_ref, dec_ref, qd_ref, kd_ref, cd_ref,
             g_ref, b_ref, y_ref, rs_ref, r_ref):
        @pl.when(pl.program_id(1) == 0)
        def _():
            r_ref[...] = jnp.zeros_like(r_ref)

        cs, sn = cos_ref[...], sin_ref[...]
        heads = range(RET_HEADS)
        cols = [_head_cols(h) for h in heads]
        q = [_rope(q_ref[:, cols[h][0]], cs, sn) for h in heads]
        k = [_rope(k_ref[:, cols[h][0]], cs, sn) * (RET_QK_DIM ** -0.5) for h in heads]
        v = [(v0_ref, v1_ref)[cols[h][1]][:, cols[h][2]] for h in heads]
        R = [r_ref[h] for h in heads]
        s = [_dot(q[h], k[h], "nt") * dec_ref[h] for h in heads]
        o = [_dot(s[h], v[h]) + _dot(q[h] * qd_ref[h], R[h]) for h in heads]
        r_new = [cd_ref[h] * R[h] + _dot(k[h] * kd_ref[h], v[h], "tn") for h in heads]
        for h in heads:
            _, j, pair, wide = cols[h]
            rs_ref[h] = R[h]
            r_ref[h] = r_new[h]
            on, _ = _group_norm(o[h])
            gr = (gr0_ref, gr1_ref)[j][:, pair]
            y_ref[:, wide] = (gr * jax.nn.sigmoid(gr) * (on * g_ref[:, wide] + b_ref[:, wide])).astype(y_ref.dtype)

    state = jax.ShapeDtypeStruct((B, N, RET_HEADS, RET_QK_DIM, RET_V_DIM), F32)
    return _pcall(
        body, (proj,) * 6 + (cos_t, sin_t, decay, q_decay, k_decay, c_decay, g_ret, b_ret),
        name="ret_fwd", out_shape=(jax.ShapeDtypeStruct((B * S, RET_HEADS * RET_V_DIM), BF16), state), grid=(B, N),
        in_specs=[sp["q"], sp["k"], *sp["v"], *sp["gr"], sp["table"], sp["table"], sp["decay"], sp["qd"],
                  sp["kd"], sp["cd"], sp["vec"], sp["vec"]],
        out_specs=(sp["v_rows"], sp["state"]),
        scratch_shapes=[pltpu.VMEM((RET_HEADS, RET_QK_DIM, RET_V_DIM), F32)],
        sem=("parallel", "arbitrary"), comm=comm)


def _ret_bwd(proj, states, d_yr, g_ret, b_ret, tables, B, S, comm=None):
    N = S // RET_CHUNK
    cos_t, sin_t, decay, q_decay, k_decay, c_decay = tables
    sp = _ret_specs(N, lambda i: N - 1 - i)
    qk_scale = RET_QK_DIM ** -0.5

    def body(q_ref, k_ref, v0_ref, v1_ref, gr0_ref, gr1_ref, dy_ref, rs_ref, cos_ref, sin_ref, dec_ref, qd_ref,
             kd_ref, cd_ref, g_ref, b_ref, dq_ref, dk_ref, dv_ref, dgr_ref, dg_ref, db_ref, dr_ref):
        @pl.when((pl.program_id(0) == 0) & (pl.program_id(1) == 0))
        def _():
            dg_ref[...] = jnp.zeros_like(dg_ref)
            db_ref[...] = jnp.zeros_like(db_ref)

        @pl.when(pl.program_id(1) == 0)
        def _():
            dr_ref[...] = jnp.zeros_like(dr_ref)

        cs, sn = cos_ref[...], sin_ref[...]
        heads = range(RET_HEADS)
        cols = [_head_cols(h) for h in heads]
        q = [_rope(q_ref[:, cols[h][0]], cs, sn) for h in heads]
        k = [_rope(k_ref[:, cols[h][0]], cs, sn) * qk_scale for h in heads]
        v = [(v0_ref, v1_ref)[cols[h][1]][:, cols[h][2]] for h in heads]
        s = [_dot(q[h], k[h], "nt") * dec_ref[h] for h in heads]
        o = [_dot(s[h], v[h]) + _dot(q[h] * qd_ref[h], rs_ref[h]) for h in heads]
        do = []
        for h in heads:
            _, j, pair, wide = cols[h]
            on, rstd = _group_norm(o[h])
            g = g_ref[:, wide]
            oaff = on * g + b_ref[:, wide]
            gr = (gr0_ref, gr1_ref)[j][:, pair]
            sg = jax.nn.sigmoid(gr)
            dy = dy_ref[:, wide]
            dgr_ref[:, wide] = (dy * oaff * (sg * (1.0 + gr * (1.0 - sg)))).astype(dgr_ref.dtype)
            doaff = dy * (gr * sg)
            dg_ref[:, wide] += jnp.sum(doaff * on, axis=0, keepdims=True)
            db_ref[:, wide] += jnp.sum(doaff, axis=0, keepdims=True)
            don = doaff * g
            do.append(rstd * (don - jnp.mean(don, axis=-1, keepdims=True)
                              - on * jnp.mean(don * on, axis=-1, keepdims=True)))
        ds = [_dot(do[h], v[h], "nt") * dec_ref[h] for h in heads]
        dq = [_dot(ds[h], k[h]) + qd_ref[h] * _dot(do[h], rs_ref[h], "nt") for h in heads]
        dk = [_dot(ds[h], q[h], "tn") + kd_ref[h] * _dot(v[h], dr_ref[h], "nt") for h in heads]
        dv = [_dot(s[h], do[h], "tn") + _dot(k[h] * kd_ref[h], dr_ref[h]) for h in heads]
        dr = [cd_ref[h] * dr_ref[h] + _dot(q[h] * qd_ref[h], do[h], "tn") for h in heads]
        for h in heads:
            qk, _, _, wide = cols[h]
            dv_ref[:, wide] = dv[h].astype(dv_ref.dtype)
            dr_ref[h] = dr[h]
            dq_ref[:, qk] = _rope_t(dq[h], cs, sn).astype(dq_ref.dtype)
            dk_ref[:, qk] = _rope_t(dk[h] * qk_scale, cs, sn).astype(dk_ref.dtype)

    T = B * S
    qk_shape = jax.ShapeDtypeStruct((T, RET_HEADS * RET_QK_DIM), BF16)
    v_shape = jax.ShapeDtypeStruct((T, RET_HEADS * RET_V_DIM), BF16)
    vec_shape = jax.ShapeDtypeStruct((1, RET_HEADS * RET_V_DIM), F32)
    return _pcall(
        body, (proj,) * 6 + (d_yr, states, cos_t, sin_t, decay, q_decay, k_decay, c_decay, g_ret, b_ret),
        name="ret_bwd", out_shape=(qk_shape, qk_shape, v_shape, v_shape, vec_shape, vec_shape), grid=(B, N),
        in_specs=[sp["q"], sp["k"], *sp["v"], *sp["gr"], sp["v_rows"], sp["state"], sp["table"], sp["table"],
                  sp["decay"], sp["qd"], sp["kd"], sp["cd"], sp["vec"], sp["vec"]],
        out_specs=(sp["qk_rows"], sp["qk_rows"], sp["v_rows"], sp["v_rows"], sp["vec"], sp["vec"]),
        scratch_shapes=[pltpu.VMEM((RET_HEADS, RET_QK_DIM, RET_V_DIM), F32)],
        sem=("arbitrary", "arbitrary"), comm=comm)


def _xa_rows(S):
    return _tile(S, 256)


def _xa_groups(S, rows, size=4):
    chunks = [slice(r, r + rows) for r in range(0, S, rows)]
    return [chunks[g:g + size] for g in range(0, len(chunks), size)]


def _xa_specs(S, M):
    q = pl.BlockSpec((S, XA_HEAD_DIM), lambda b, h: (b, COL_QX // XA_HEAD_DIM + h))
    k = pl.BlockSpec((M, XA_HEAD_DIM), lambda b, h: (b, h))
    v = pl.BlockSpec((M, XA_HEAD_DIM), lambda b, h: (b, XA_HEADS + h))
    o = pl.BlockSpec((S, XA_HEAD_DIM), lambda b, h: (b, h))
    return q, k, v, o


def _softmax_rows(s):
    e = jnp.exp(s - jnp.max(s, axis=-1, keepdims=True))
    return e / jnp.sum(e, axis=-1, keepdims=True)


def _xa_fwd(proj, kv, B, S, M, comm=None):
    CH = _xa_rows(S)
    q_spec, k_spec, v_spec, o_spec = _xa_specs(S, M)

    def body(q_ref, k_ref, v_ref, o_ref):
        for group in _xa_groups(S, CH):
            sc = [_dot(q_ref[rows, :], k_ref[...], "nt") * (XA_HEAD_DIM ** -0.5) for rows in group]
            p = [_softmax_rows(s) for s in sc]
            for rows, pg in zip(group, p):
                o_ref[rows, :] = _dot(pg, v_ref[...]).astype(o_ref.dtype)

    return _pcall(
        body, (proj, kv, kv), name="xattn_fwd", out_shape=jax.ShapeDtypeStruct((B * S, XA_WIDTH), BF16),
        grid=(B, XA_HEADS), in_specs=[q_spec, k_spec, v_spec], out_specs=o_spec,
        sem=("parallel", "parallel"), comm=comm)


def _xa_bwd(proj, kv, d_o, B, S, M, comm=None):
    CH = _xa_rows(S)
    q_spec, k_spec, v_spec, o_spec = _xa_specs(S, M)
    scale = XA_HEAD_DIM ** -0.5

    def body(q_ref, k_ref, v_ref, do_ref, dq_ref, dk_ref, dv_ref):
        dk_ref[...] = jnp.zeros_like(dk_ref)
        dv_ref[...] = jnp.zeros_like(dv_ref)
        for group in _xa_groups(S, CH):
            q = [q_ref[rows, :] for rows in group]
            do = [do_ref[rows, :] for rows in group]
            p = [_softmax_rows(_dot(qg, k_ref[...], "nt") * scale) for qg in q]
            dp = [_dot(dg, v_ref[...], "nt") for dg in do]
            ds = [pg * (dpg - jnp.sum(dpg * pg, axis=-1, keepdims=True)) * scale for pg, dpg in zip(p, dp)]
            for rows, dsg in zip(group, ds):
                dq_ref[rows, :] = _dot(dsg, k_ref[...]).astype(dq_ref.dtype)
            dk_ref[...] += sum(_dot(dsg, qg, "tn") for dsg, qg in zip(ds, q))
            dv_ref[...] += sum(_dot(pg, dg, "tn") for pg, dg in zip(p, do))

    kv_out = pl.BlockSpec((M, XA_HEAD_DIM), lambda b, h: (b, h))
    return _pcall(
        body, (proj, kv, kv, d_o), name="xattn_bwd",
        out_shape=(jax.ShapeDtypeStruct((B * S, XA_WIDTH), BF16), jax.ShapeDtypeStruct((B * M, XA_WIDTH), F32),
                   jax.ShapeDtypeStruct((B * M, XA_WIDTH), F32)),
        grid=(B, XA_HEADS), in_specs=[q_spec, k_spec, v_spec, o_spec], out_specs=(o_spec, kv_out, kv_out),
        sem=("parallel", "parallel"), comm=comm)


def _gate_specs(tm):
    n = COL_GL // D_MODEL
    return [pl.BlockSpec((tm, D_MODEL), lambda i, j=j: (i, n + j)) for j in range(3)]


def _merge_fwd(proj, ys, tm=512, comm=None):
    T = proj.shape[0]
    tm = _tile(T, tm)
    row = pl.BlockSpec((tm, D_MODEL), lambda i: (i, 0))

    def body(g0, g1, g2, y0, y1, y2, o_ref):
        acc = jax.nn.sigmoid(g0[...]) * y0[...]
        acc = acc + jax.nn.sigmoid(g1[...]) * y1[...]
        acc = acc + jax.nn.sigmoid(g2[...]) * y2[...]
        o_ref[...] = acc.astype(o_ref.dtype)

    return _pcall(
        body, (proj, proj, proj, *ys), name="merge_fwd", out_shape=jax.ShapeDtypeStruct((T, D_MODEL), BF16),
        grid=(T // tm,), in_specs=_gate_specs(tm) + [row] * 3, out_specs=row, sem=("parallel",), comm=comm)


def _merge_bwd(proj, ys, d_merged, tm=512, comm=None):
    T = proj.shape[0]
    tm = _tile(T, tm)
    row = pl.BlockSpec((tm, D_MODEL), lambda i: (i, 0))

    def body(g0, g1, g2, y0, y1, y2, dm_ref, dgl_ref, d0, d1, d2):
        dm = dm_ref[...]
        for j, (g_ref, y_ref, d_ref) in enumerate(((g0, y0, d0), (g1, y1, d1), (g2, y2, d2))):
            sg = jax.nn.sigmoid(g_ref[...])
            d_ref[...] = (dm * sg).astype(d_ref.dtype)
            dgl_ref[:, j * D_MODEL:(j + 1) * D_MODEL] = (dm * y_ref[...] * sg * (1.0 - sg)).astype(dgl_ref.dtype)

    dy = jax.ShapeDtypeStruct((T, D_MODEL), BF16)
    return _pcall(
        body, (proj, proj, proj, *ys, d_merged), name="merge_bwd",
        out_shape=(jax.ShapeDtypeStruct((T, 3 * D_MODEL), BF16), dy, dy, dy), grid=(T // tm,),
        in_specs=_gate_specs(tm) + [row] * 4,
        out_specs=(pl.BlockSpec((tm, 3 * D_MODEL), lambda i: (i, 0)), row, row, row),
        sem=("parallel",), comm=comm)


def _gelu(x):
    return 0.5 * x * (1.0 + jnp.tanh(GELU_C * (x + GELU_A * x * x * x)))


def _gelu_grad(x):
    t = jnp.tanh(GELU_C * (x + GELU_A * x * x * x))
    return 0.5 * (1.0 + t) + 0.5 * x * (1.0 - t * t) * GELU_C * (1.0 + 3.0 * GELU_A * x * x)


GLU_HALO = 16


def _shift_down(x, prev, n):
    last = prev.shape[0]
    r = lax.broadcasted_iota(jnp.int32, (8, 1), 0)
    rolled = pltpu.roll(x, n, axis=0)
    head = rolled[0:8]
    for j in range(n):
        head = jnp.where(r == j, prev[last - n + j:last - n + j + 1, :], head)
    return jnp.concatenate([head, rolled[8:]], axis=0)


def _shift_up(x, nxt, n):
    rows = x.shape[0]
    r = lax.broadcasted_iota(jnp.int32, (8, 1), 0)
    rolled = pltpu.roll(x, rows - n, axis=0)
    tail = rolled[rows - 8:]
    for j in range(n):
        tail = jnp.where(r == 8 - n + j, nxt[j:j + 1, :], tail)
    return jnp.concatenate([rolled[:rows - 8], tail], axis=0)


def _conv(a, prev, cw, cb):
    return _shift_down(a, prev, 2) * cw[0:1, :] + _shift_down(a, prev, 1) * cw[1:2, :] + a * cw[2:3, :] + cb


def _glu_fwd(up, cw, cb, S, tm=1024, comm=None):
    T = up.shape[2]
    tm = _tile(S, tm)
    per_seq = S // tm

    def body(ab_ref, prev_ref, cw_ref, cb_ref, u_ref):
        i = pl.program_id(1)
        prev = jnp.where(i % per_seq == 0, 0.0, prev_ref[...].astype(F32))
        ac = _conv(ab_ref[0].astype(F32), prev, cw_ref[...], cb_ref[...])
        u_ref[...] = (_gelu(ac) * ab_ref[1].astype(F32)).astype(u_ref.dtype)

    before = tm // GLU_HALO
    return _pcall(
        body, (up, up, cw, cb), name="glu_fwd",
        out_shape=jax.ShapeDtypeStruct((FFN_SLABS, T, UP_SHARD), BF16), grid=(FFN_SLABS, T // tm),
        in_specs=[pl.BlockSpec((2, None, tm, UP_SHARD), lambda d, i: (0, d, i, 0)),
                  pl.BlockSpec((None, None, GLU_HALO, UP_SHARD),
                               lambda d, i: (0, d, jnp.maximum(i * before - 1, 0), 0)),
                  pl.BlockSpec((None, 3, UP_SHARD), lambda d, i: (d, 0, 0)),
                  pl.BlockSpec((None, 1, UP_SHARD), lambda d, i: (d, 0, 0))],
        out_specs=pl.BlockSpec((None, tm, UP_SHARD), lambda d, i: (d, i, 0)),
        sem=("parallel", "parallel"), comm=comm)


def _glu_bwd(up, d_u, cw, cb, S, tm=512, comm=None):
    T = up.shape[2]
    tm = _tile(S, tm)
    per_seq = S // tm
    n_tiles = T // tm
    per_tile = tm // GLU_HALO

    def body(ab_ref, prev_ref, abn_ref, du_ref, dun_ref, cw_ref, cb_ref, dup_ref, dcw_ref, dcb_ref):
        i = pl.program_id(1)

        @pl.when(i == 0)
        def _():
            dcw_ref[...] = jnp.zeros_like(dcw_ref)
            dcb_ref[...] = jnp.zeros_like(dcb_ref)

        cw, cb = cw_ref[...], cb_ref[...]
        a, b = ab_ref[0].astype(F32), ab_ref[1].astype(F32)
        prev = jnp.where(i % per_seq == 0, 0.0, prev_ref[...].astype(F32))
        a2, a1 = _shift_down(a, prev, 2), _shift_down(a, prev, 1)
        ac = a2 * cw[0:1, :] + a1 * cw[1:2, :] + a * cw[2:3, :] + cb
        du = du_ref[...].astype(F32)
        dup_ref[1] = (du * _gelu(ac)).astype(dup_ref.dtype)
        dac = du * b * _gelu_grad(ac)
        dcb_ref[...] += jnp.sum(dac, axis=0, keepdims=True)
        dcw_ref[0:1, :] += jnp.sum(dac * a2, axis=0, keepdims=True)
        dcw_ref[1:2, :] += jnp.sum(dac * a1, axis=0, keepdims=True)
        dcw_ref[2:3, :] += jnp.sum(dac * a, axis=0, keepdims=True)
        acn = _conv(abn_ref[0].astype(F32), a[tm - GLU_HALO:, :], cw, cb)
        dacn = jnp.where(i % per_seq == per_seq - 1, 0.0,
                         dun_ref[...].astype(F32) * abn_ref[1].astype(F32) * _gelu_grad(acn))
        da = dac * cw[2:3, :] + _shift_up(dac, dacn, 1) * cw[1:2, :] + _shift_up(dac, dacn, 2) * cw[0:1, :]
        dup_ref[0] = da.astype(dup_ref.dtype)

    def nxt(i):
        return jnp.minimum((i + 1) * per_tile, T // GLU_HALO - 1)

    return _pcall(
        body, (up, up, up, d_u, d_u, cw, cb), name="glu_bwd",
        out_shape=(jax.ShapeDtypeStruct((2, FFN_SLABS, T, UP_SHARD), BF16),
                   jax.ShapeDtypeStruct((FFN_SLABS, 3, UP_SHARD), F32),
                   jax.ShapeDtypeStruct((FFN_SLABS, 1, UP_SHARD), F32)),
        grid=(FFN_SLABS, n_tiles),
        in_specs=[pl.BlockSpec((2, None, tm, UP_SHARD), lambda d, i: (0, d, i, 0)),
                  pl.BlockSpec((None, None, GLU_HALO, UP_SHARD),
                               lambda d, i: (0, d, jnp.maximum(i * per_tile - 1, 0), 0)),
                  pl.BlockSpec((2, None, GLU_HALO, UP_SHARD), lambda d, i: (0, d, nxt(i), 0)),
                  pl.BlockSpec((None, tm, UP_SHARD), lambda d, i: (d, i, 0)),
                  pl.BlockSpec((None, GLU_HALO, UP_SHARD), lambda d, i: (d, nxt(i), 0)),
                  pl.BlockSpec((None, 3, UP_SHARD), lambda d, i: (d, 0, 0)),
                  pl.BlockSpec((None, 1, UP_SHARD), lambda d, i: (d, 0, 0))],
        out_specs=(pl.BlockSpec((2, None, tm, UP_SHARD), lambda d, i: (0, d, i, 0)),
                   pl.BlockSpec((None, 3, UP_SHARD), lambda d, i: (d, 0, 0)),
                   pl.BlockSpec((None, 1, UP_SHARD), lambda d, i: (d, 0, 0))),
        sem=("parallel", "arbitrary"), comm=comm)


def _mm_up(h2, w_up_t, tm=MM_ROWS, comm=None):
    T, K = h2.shape
    tm = _tile(T, tm)
    return _matmul(
        "mm_up", "nt", h2, w_up_t, jax.ShapeDtypeStruct((N_DEV, T, UP_SHARD), BF16), (N_DEV, T // tm, 1),
        pl.BlockSpec((tm, K), lambda j, i, k: (i, 0)), pl.BlockSpec((None, UP_SHARD, K), lambda j, i, k: (j, 0, 0)),
        pl.BlockSpec((None, tm, UP_SHARD), lambda j, i, k: (j, i, 0)), (tm, UP_SHARD), comm=comm)


def _loss_epilogue(ffn, operands, outputs, first):
    x1_ref, t_ref, g_ref = operands
    dx_ref, dg_ref, loss_ref = outputs

    @pl.when(first)
    def _():
        dg_ref[...] = jnp.zeros_like(dg_ref)
        loss_ref[...] = jnp.zeros_like(loss_ref)

    xv = x1_ref[...] + ffn
    r = lax.rsqrt(jnp.mean(xv * xv, axis=-1, keepdims=True) + EPS)
    xhat = xv * r
    err = xhat * g_ref[...] - t_ref[...]
    loss_ref[...] += (0.5 / D_MODEL) * jnp.sum(err * err)
    dy = err * (1.0 / D_MODEL)
    dg_ref[...] += jnp.sum(dy * xhat, axis=0, keepdims=True)
    dxhat = dy * g_ref[...]
    dx_ref[...] = r * (dxhat - xhat * jnp.mean(dxhat * xhat, axis=-1, keepdims=True))


def _mm_down_loss(u, w_down, x1, target, g_final, tm=MM_ROWS_RES):
    J, T, n = u.shape
    tm = _tile(T, tm)
    row = pl.BlockSpec((tm, D_MODEL), lambda i, d: (i, 0))
    vec = pl.BlockSpec((1, D_MODEL), lambda i, d: (0, 0))
    vec_shape = jax.ShapeDtypeStruct((1, D_MODEL), F32)
    return _matmul(
        "mm_down", "nn", u, w_down, (jax.ShapeDtypeStruct((T, D_MODEL), F32), vec_shape, vec_shape), (T // tm, J),
        pl.BlockSpec((None, tm, n), lambda i, d: (d, i, 0)), pl.BlockSpec((None, n, D_MODEL), lambda i, d: (d, 0, 0)),
        (row, vec, vec), (tm, D_MODEL), [x1, target, g_final], [row, row, vec], epilogue=_loss_epilogue)


def _mm_down_t(dx, w_down, tm=MM_ROWS):
    T = dx.shape[0]
    J, n, _ = w_down.shape
    tm = _tile(T, tm)
    return _matmul(
        "mm_down_t", "nt", dx, w_down, jax.ShapeDtypeStruct((J, T, n), BF16), (J, T // tm, 1),
        pl.BlockSpec((tm, D_MODEL), lambda d, i, k: (i, 0)), pl.BlockSpec((None, n, D_MODEL), lambda d, i, k: (d, 0, 0)),
        pl.BlockSpec((None, tm, n), lambda d, i, k: (d, i, 0)), (tm, n))


def _mm_dw_down(u, dx, tk=MM_TOKENS):
    J, T, n = u.shape
    tk = _tile(T, tk)
    return _matmul(
        "mm_dw_down", "tn", u, dx, jax.ShapeDtypeStruct((J, n, D_MODEL), BF16), (J, T // tk),
        pl.BlockSpec((None, tk, n), lambda d, k: (d, k, 0)), pl.BlockSpec((tk, D_MODEL), lambda d, k: (k, 0)),
        pl.BlockSpec((None, n, D_MODEL), lambda d, k: (d, 0, 0)), (n, D_MODEL))


def _mm_dw_up(h2, d_up, tk=MM_TOKENS):
    T, K = h2.shape
    tk = _tile(T, tk)
    return _matmul(
        "mm_dw_up", "tn", d_up, h2, jax.ShapeDtypeStruct((N_DEV, UP_SHARD, K), BF16), (N_DEV, T // tk),
        pl.BlockSpec((None, tk, UP_SHARD), lambda j, k: (j, k, 0)), pl.BlockSpec((tk, K), lambda j, k: (k, 0)),
        pl.BlockSpec((None, UP_SHARD, K), lambda j, k: (j, 0, 0)), (UP_SHARD, K))


def _mm_up_t(d_up, w_up_t, rms, tm=MM_ROWS_RES, comm=None):
    J, T, n = d_up.shape
    K = w_up_t.shape[2]
    tm = _tile(T, tm)
    fused = _rms_bwd_fused(T, K, tm, rms)
    return _matmul(
        "mm_up_t", "nn", d_up, w_up_t, fused.pop("out_shape"), (T // tm, J),
        pl.BlockSpec((None, tm, n), lambda i, j: (j, i, 0)), pl.BlockSpec((None, n, K), lambda i, j: (j, 0, 0)),
        fused.pop("o_spec"), (tm, K), comm=comm, **fused)


def _cast_shards(shards):
    def body(*refs):
        n = len(refs) // 2
        for src, dst in zip(refs[:n], refs[n:]):
            dst[...] = src[...].astype(dst.dtype)

    return pl.pallas_call(
        body, out_shape=[jax.ShapeDtypeStruct(s.shape, BF16) for s in shards], name="cast_shards",
        compiler_params=pltpu.CompilerParams(vmem_limit_bytes=VMEM_LIMIT),
    )(*shards)


def _adamw(w, g, m, v):
    m = ADAM_B1 * m + (1.0 - ADAM_B1) * g
    v = ADAM_B2 * v + (1.0 - ADAM_B2) * (g * g)
    m_hat = m / (1.0 - ADAM_B1 ** ADAM_STEP)
    v_hat = v / (1.0 - ADAM_B2 ** ADAM_STEP)
    delta = -ADAM_LR * (m_hat / (jnp.sqrt(v_hat) + ADAM_EPS) + ADAM_WD * w)
    return delta, m, v


def _sum_parts(p_ref):
    g = p_ref[0].astype(F32)
    for d in range(1, N_DEV):
        g = g + p_ref[d].astype(F32)
    return g


def _reduce_adam(name, parts, w, m, v, tr=128):
    R, Cn = w.shape
    by_rows = sum(p.shape[1] for p in parts) == R and len(parts) > 1
    tr = math.gcd(tr, *[p.shape[1] for p in parts])
    n_tiles = [p.shape[1] // tr for p in parts]
    first = [sum(n_tiles[:j]) for j in range(len(parts))] if by_rows else [0] * len(parts)

    def body(*refs):
        p_refs = refs[:len(parts)]
        w_ref, m_ref, v_ref, g_out, d_out, m_out, v_out = refs[len(parts):]

        def update(p_ref):
            g = _sum_parts(p_ref)
            delta, m_new, v_new = _adamw(w_ref[...], g, m_ref[...], v_ref[...])
            g_out[...] = g
            d_out[...] = delta
            m_out[...] = m_new
            v_out[...] = v_new

        if len(parts) == 1:
            update(p_refs[0])
        elif by_rows:
            i = pl.program_id(0)
            for p_ref, t0, n in zip(p_refs, first, n_tiles):
                pl.when((i >= t0) & (i < t0 + n))(functools.partial(update, p_ref))
        else:
            c = lax.axis_index("c")
            for side, p_ref in enumerate(p_refs):
                pl.when(c == side)(functools.partial(update, p_ref))

    def part_spec(t0, n):
        return pl.BlockSpec((N_DEV, tr, Cn), lambda i: (0, jnp.clip(i - t0, 0, n - 1), 0))

    row = pl.BlockSpec((tr, Cn), lambda i: (i, 0))
    shape = jax.ShapeDtypeStruct((R, Cn), F32)
    return pl.pallas_call(
        body, out_shape=(shape,) * 4, grid=(R // tr,),
        in_specs=[part_spec(t0, n) for t0, n in zip(first, n_tiles)] + [row, row, row],
        out_specs=(row,) * 4, name=name, compiler_params=_params(("parallel",)),
    )(*parts, w, m, v)


def _small_adam(name, gathered, params):
    n_g, n_p = len(gathered), len(params)

    def body(*refs):
        g_refs = refs[:n_g]
        wmv = refs[n_g:n_g + 3 * n_p]
        sums = refs[n_g + 3 * n_p:2 * n_g + 3 * n_p]
        upd = refs[2 * n_g + 3 * n_p:]
        for j in range(n_g):
            g = _sum_parts(g_refs[j])
            sums[j][...] = g
            if j < n_p:
                w_ref, m_ref, v_ref = wmv[3 * j:3 * j + 3]
                delta, m_new, v_new = _adamw(w_ref[...], g, m_ref[...], v_ref[...])
                upd[3 * j][...] = delta
                upd[3 * j + 1][...] = m_new
                upd[3 * j + 2][...] = v_new

    flat = [a for wmv in params for a in wmv]
    out_shape = [jax.ShapeDtypeStruct(g.shape[1:], F32) for g in gathered]
    out_shape += [jax.ShapeDtypeStruct(a.shape, F32) for a in flat]
    res = pl.pallas_call(body, out_shape=out_shape, name=name)(*gathered, *flat)
    return res[:n_g], [tuple(res[n_g + 3 * j:n_g + 3 * j + 3]) for j in range(n_p)]


def _adam_only(name, g, w, m, v):
    def body(g_ref, w_ref, m_ref, v_ref, d_out, m_out, v_out):
        delta, m_new, v_new = _adamw(w_ref[...], g_ref[...], m_ref[...], v_ref[...])
        d_out[...] = delta
        m_out[...] = m_new
        v_out[...] = v_new

    shape = jax.ShapeDtypeStruct(w.shape, F32)
    return pl.pallas_call(body, out_shape=(shape,) * 3, name=name)(g, w, m, v)


def kernel(x, mem, g_mix, w_in, w_pool, pool_scale, w_a, g_ret, b_ret, w_r, g_mem, w_mem_kv, w_c, w_out, g_ffn, w_up, conv_w, conv_b, w_down, g_final, loss_target, m_g_mix, m_w_in, m_w_pool, m_pool_scale, m_w_a, m_g_ret, m_b_ret, m_w_r, m_g_mem, m_w_mem_kv, m_w_c, m_w_out, m_g_ffn, m_w_up, m_conv_w, m_conv_b, m_w_down, m_g_final, v_g_mix, v_w_in, v_w_pool, v_pool_scale, v_w_a, v_g_ret, v_b_ret, v_w_r, v_g_mem, v_w_mem_kv, v_w_c, v_w_out, v_g_ffn, v_w_up, v_conv_w, v_conv_b, v_w_down, v_g_final):
    B, S, _ = x.shape
    M = mem.shape[1]
    T = B * S
    me = _my_index()
    x2d = x.reshape(T, D_MODEL)
    mem2d = mem.reshape(B * M, D_MODEL)
    tgt2d = loss_target.reshape(T, D_MODEL)
    g_final2 = g_final.reshape(1, D_MODEL)

    big = dict(w_in=w_in[0], w_a=w_a[0], w_r=w_r[0], w_mem_kv=w_mem_kv[0], w_c=w_c[0], w_out=w_out[0],
               w_up=w_up[0].T, w_down=w_down[0])
    names = list(big)
    cast = dict(zip(names, _cast_shards([big[n] for n in names])))
    cb = conv_b[0].reshape(FFN_SLABS, 1, UP_SHARD)
    wp = w_pool[0]
    tables = _ret_tables(S)

    h = _rms_fwd("rms_mix", x2d, g_mix)
    early = ("w_a", "w_r", "w_mem_kv", "w_c", "w_out")
    (proj, Win), landed = _mm_in_gather(h, cast["w_in"], comm=_Gather([cast[n] for n in early] + [conv_w[0]]))
    W = dict(zip(early, landed))
    half = D_MODEL // 2
    (yr, ret_states), (Wup_lo,) = _ret_fwd(proj, g_ret, b_ret, tables, B, S,
                                           comm=_Gather([cast["w_up"][:, :half]]))
    cw_full = landed[-1].transpose(1, 0, 2).reshape(3, FFN_HIDDEN)
    cw = cw_full.reshape(3, FFN_SLABS, UP_SHARD).transpose(1, 0, 2)
    Wa = W["w_a"].transpose(1, 0, 2).reshape(POOL_WIDTH, D_MODEL)
    Wc = W["w_c"].transpose(1, 0, 2).reshape(XA_WIDTH, D_MODEL)
    Wr = W["w_r"].reshape(D_MODEL, D_MODEL)
    Wkv = W["w_mem_kv"].reshape(D_MODEL, D_MODEL)
    Wout = W["w_out"].reshape(D_MODEL, D_MODEL)
    ypre = _pool_fwd(proj, wp, pool_scale, B, S)
    y_pool = _mm_rows("mm_a", ypre, Wa, BF16)
    y_ret = _mm_rows("mm_r", yr, Wr, BF16)
    mem_n = _rms_fwd("rms_mem", mem2d, g_mem)
    kv = _mm_rows("mm_kv", mem_n, Wkv)
    o_mem = _xa_fwd(proj, kv, B, S, M)[0]
    y_mem = _mm_rows("mm_c", o_mem, Wc, BF16)
    ys = (y_pool, y_ret, y_mem)
    merged, (Wup_hi,) = _merge_fwd(proj, ys, comm=_Gather([cast["w_up"][:, half:]]))
    Wup = jnp.concatenate([Wup_lo, Wup_hi], axis=2)
    x1, h2 = _mm_residual_rms("mm_out", merged, Wout, x2d, g_ffn)
    up, (Wdown,) = _mm_up(h2, Wup, comm=_Gather([cast["w_down"]]))
    up = up.reshape(2, FFN_SLABS, T, UP_SHARD)
    Wdown = Wdown.reshape(FFN_SLABS, UP_SHARD, D_MODEL)
    u = _glu_fwd(up, cw, cb, S)[0]

    dx2, dg_final, loss_part = _mm_down_loss(u, Wdown, x1, tgt2d, g_final2)
    received = {}
    d_u = _mm_down_t(dx2, Wdown)
    dW_down = _mm_dw_down(u, dx2)
    (d_up, d_cw, d_cb), (received["w_down"],) = _glu_bwd(
        up, d_u, cw, cb, S, comm=_Exchange([dW_down.reshape(N_DEV, -1, D_MODEL)]))
    d_up = d_up.reshape(N_DEV, T, UP_SHARD)
    dW_up = _mm_dw_up(h2, d_up)
    (dx1, dg_ffn), (up_c0,) = _mm_up_t(d_up, Wup, (x1, g_ffn, dx2), comm=_ExchangeTo([dW_up], 0))
    d_merged = _mm_rows("mm_out_t", dx1, Wout, kind="nt")
    dW_out = _mm_tn("mm_dw_out", merged, dx1, BF16)
    (d_gl, d_y_pool, d_y_ret, d_y_mem), (up_c1,) = _merge_bwd(proj, ys, d_merged, comm=_ExchangeTo([dW_up], 1))
    received["w_up"] = [up_c0, up_c1]
    dW_c = _mm_tn("mm_dw_c", o_mem, d_y_mem, BF16)
    d_o_mem = _mm_rows("mm_c_t", d_y_mem, Wc, kind="nt")
    (d_qx, d_kmem, d_vmem), (received["w_out"],) = _xa_bwd(
        proj, kv, d_o_mem, B, S, M, comm=_Exchange([dW_out.reshape(N_DEV, -1, D_MODEL)]))
    d_kv = jnp.concatenate([d_kmem, d_vmem], axis=1)
    dW_kv = _mm_tn("mm_dw_kv", mem_n, d_kv, BF16)
    d_mem_n = _mm_rows("mm_kv_t", d_kv, Wkv, kind="nt")
    dg_mem = _rms_bwd("rms_mem_bwd", mem2d, g_mem, d_mem_n, None)
    dW_a = _mm_tn("mm_dw_a", ypre, d_y_pool, BF16)
    d_ypre = _mm_rows("mm_a_t", d_y_pool, Wa, kind="nt")
    (d_hp, dw_pool, d_scale), (received["w_a"],) = _pool_bwd(
        proj, d_ypre, wp, pool_scale, B, S,
        comm=_Exchange([dW_a.reshape(POOL_WIDTH, N_DEV, -1).transpose(1, 0, 2)]))
    dW_r = _mm_tn("mm_dw_r", yr, d_y_ret, BF16)
    d_yr = _mm_rows("mm_r_t", d_y_ret, Wr, kind="nt")
    (d_q, d_k, d_v, d_gr, dg_ret, db_ret), landed = _ret_bwd(
        proj, ret_states, d_yr, g_ret, b_ret, tables, B, S,
        comm=_Exchange([dW_r.reshape(N_DEV, -1, D_MODEL), dW_c.reshape(XA_WIDTH, N_DEV, -1).transpose(1, 0, 2),
                        dW_kv.reshape(N_DEV, -1, D_MODEL)]))
    received["w_r"], received["w_c"], received["w_mem_kv"] = landed
    small_names = ["w_pool", "pool_scale", "g_ret", "b_ret", "g_mem", "g_ffn", "conv_b", "g_final"]
    small_grads = [dw_pool, d_scale, dg_ret, db_ret, dg_mem, dg_ffn, d_cb.reshape(1, FFN_HIDDEN), dg_final,
                   d_cw.transpose(1, 0, 2).reshape(3, FFN_HIDDEN), loss_part]
    d_proj = jnp.concatenate([d_hp, d_q, d_k, d_v, d_gr, d_qx, d_gl], axis=1)
    dW_in0, small_all = _mm_tn_slab("mm_dw_in0", h[:, :W_IN_FIRST_ROWS], d_proj, IN_SHARD, BF16,
                                    comm=_Exchange([], whole=small_grads))
    dW_in1, (in0,) = _mm_tn_slab("mm_dw_in1", h[:, W_IN_FIRST_ROWS:], d_proj, IN_SHARD, BF16,
                                 comm=_Exchange([dW_in0]))
    (grad_x, dg_mix), (in1,) = _mm_cols_slab_t("mm_in_t", d_proj, Win, (x2d, g_mix, dx1), comm=_Exchange([dW_in1]))
    received["w_in"] = [in0, in1]
    (g_mix_all,) = _comm_call("gather_g_mix", _Exchange([], whole=[dg_mix]))

    args = dict(g_mix=g_mix, w_in=w_in, w_pool=w_pool, pool_scale=pool_scale, w_a=w_a, g_ret=g_ret, b_ret=b_ret,
                w_r=w_r, g_mem=g_mem, w_mem_kv=w_mem_kv, w_c=w_c, w_out=w_out, g_ffn=g_ffn, w_up=w_up,
                conv_w=conv_w, conv_b=conv_b, w_down=w_down, g_final=g_final)
    m_in = dict(g_mix=m_g_mix, w_in=m_w_in, w_pool=m_w_pool, pool_scale=m_pool_scale, w_a=m_w_a, g_ret=m_g_ret,
                b_ret=m_b_ret, w_r=m_w_r, g_mem=m_g_mem, w_mem_kv=m_w_mem_kv, w_c=m_w_c, w_out=m_w_out,
                g_ffn=m_g_ffn, w_up=m_w_up, conv_w=m_conv_w, conv_b=m_conv_b, w_down=m_w_down, g_final=m_g_final)
    v_in = dict(g_mix=v_g_mix, w_in=v_w_in, w_pool=v_w_pool, pool_scale=v_pool_scale, w_a=v_w_a, g_ret=v_g_ret,
                b_ret=v_b_ret, w_r=v_w_r, g_mem=v_g_mem, w_mem_kv=v_w_mem_kv, w_c=v_w_c, w_out=v_w_out,
                g_ffn=v_g_ffn, w_up=v_w_up, conv_w=v_conv_w, conv_b=v_conv_b, w_down=v_w_down, g_final=v_g_final)

    grads, deltas, new_m, new_v = {}, {}, {}, {}
    for n in names:
        parts = received[n] if isinstance(received[n], list) else [received[n]]
        flip = (lambda a: a.T) if n == "w_up" else (lambda a: a)
        outs = _reduce_adam("adam_" + n, parts, big[n], flip(m_in[n][0]), flip(v_in[n][0]))
        for store, val in zip((grads, deltas, new_m, new_v), outs):
            store[n] = flip(val)[None]

    def as_small(a):
        return a.reshape(a.shape[-3:]) if a.ndim > 2 else a.reshape(1, -1)

    def small_update(call_name, param_names, gathered):
        params = [tuple(as_small(d[n]) for d in (args, m_in, v_in)) for n in param_names]
        sums, updates = _small_adam(call_name, gathered, params)
        for n, g, (d_, m_, v_) in zip(param_names, sums, updates):
            shape = args[n].shape
            grads[n], deltas[n], new_m[n], new_v[n] = (a.reshape(shape) for a in (g, d_, m_, v_))
        return sums[len(param_names):]

    g_cw_full, loss_row = small_update("adam_small", small_names, small_all)
    loss = loss_row[0, 0]
    small_update("adam_g_mix", ["g_mix"], [g_mix_all])

    shard_cols = FFN_HIDDEN // N_DEV
    g_cw = lax.dynamic_slice_in_dim(g_cw_full, me * shard_cols, shard_cols, axis=1)
    d_, m_, v_ = _adam_only("adam_conv_w", g_cw, conv_w[0], m_conv_w[0], v_conv_w[0])
    grads["conv_w"], deltas["conv_w"], new_m["conv_w"], new_v["conv_w"] = g_cw[None], d_[None], m_[None], v_[None]

    order = ["g_mix", "w_in", "w_pool", "pool_scale", "w_a", "g_ret", "b_ret", "w_r", "g_mem", "w_mem_kv", "w_c",
             "w_out", "g_ffn", "w_up", "conv_w", "conv_b", "w_down", "g_final"]
    return (loss, grad_x.reshape(B, S, D_MODEL), *[grads[n] for n in order], *[deltas[n] for n in order],
            *[new_m[n] for n in order], *[new_v[n] for n in order])
```

```python
import functools
import math

import jax
import jax.numpy as jnp
from jax import lax
from jax.experimental import pallas as pl
from jax.experimental.pallas import tpu as pltpu

F32 = jnp.float32
BF16 = jnp.bfloat16

N_DEV = 8
D_MODEL = 1024
POOL_WINDOWS = (2, 4, 8, 16)
POOL_GROUP_DIM = 128
POOL_WIDTH = 512
POOL_HALO = 16
RET_HEADS = 4
RET_QK_DIM = 128
RET_V_DIM = 256
RET_CHUNK = 128
ROPE_BASE = 10000.0
XA_HEADS = 4
XA_HEAD_DIM = 128
XA_WIDTH = 512
IN_WIDTH = 7168
IN_SHARD = IN_WIDTH // N_DEV
FFN_HIDDEN = 2816
UP_SHARD = 2 * FFN_HIDDEN // N_DEV
FFN_SLABS = FFN_HIDDEN // UP_SHARD
EPS = 1e-6
ADAM_LR = 0.001
ADAM_B1 = 0.9
ADAM_B2 = 0.999
ADAM_EPS = 1e-08
ADAM_WD = 0.01
ADAM_STEP = 10
GELU_C = math.sqrt(2.0 / math.pi)
GELU_A = 0.044715
VMEM_LIMIT = 56 * 1024 * 1024
MM_ROWS = 2048
MM_ROWS_RES = 1024
MM_TOKENS = 2048
W_IN_FIRST_ROWS = 384
MESH = pl.DeviceIdType.MESH

COL_Q, COL_K, COL_V, COL_GR, COL_QX, COL_GL = 512, 1024, 1536, 2560, 3584, 4096

_DIMS = {
    "nn": (((1,), (0,)), ((), ())),
    "nt": (((1,), (1,)), ((), ())),
    "tn": (((0,), (0,)), ((), ())),
}


def _dot(a, b, kind="nn"):
    return lax.dot_general(a.astype(BF16), b.astype(BF16), _DIMS[kind], preferred_element_type=F32)


def _params(sem, vmem=VMEM_LIMIT):
    return pltpu.CompilerParams(dimension_semantics=sem, vmem_limit_bytes=vmem)


def _tile(n, pref):
    t = min(n, pref)
    while n % t:
        t //= 2
    return t


def _mesh_pos():
    return lax.axis_index("x"), lax.axis_index("y"), lax.axis_index("c")


def _dev_index(x, y, c):
    return 4 * x + 2 * y + c


def _my_index():
    return _dev_index(*_mesh_pos())


def _remote(src, dst, send_sems, recv_sems, s, to):
    return pltpu.make_async_remote_copy(src_ref=src, dst_ref=dst, send_sem=send_sems.at[s], recv_sem=recv_sems.at[s],
                                        device_id=to, device_id_type=MESH)


class _Gather:
    def __init__(self, shards):
        self.inputs = list(shards)
        self.out_shapes = [jax.ShapeDtypeStruct((N_DEV,) + s.shape, s.dtype) for s in shards]
        n = len(shards)
        self.sem_shapes = [pltpu.SemaphoreType.DMA((7 * n,)), pltpu.SemaphoreType.DMA((7 * n,)),
                           pltpu.SemaphoreType.DMA((n,))]

    def _places(self):
        x, y, c = _mesh_pos()
        return (x, y, c), (x, y, 1 - c), [(1 - x, y), (x, 1 - y), (1 - x, 1 - y)]

    def _local(self, src, dst, sems):
        me = _my_index()
        return [pltpu.make_async_copy(src[w], dst[w].at[me], sems[2].at[w]) for w in range(len(src))]

    def start(self, src, dst, sems):
        me, sib, chips = self._places()
        for cp in self._local(src, dst, sems):
            cp.start()
        for w in range(len(src)):
            land = dst[w].at[_dev_index(*me)]
            _remote(src[w], land, sems[0], sems[1], 7 * w, sib).start()
            for j, chip in enumerate(chips):
                _remote(src[w], land, sems[0], sems[1], 7 * w + 1 + j, (*chip, me[2])).start()

    def middle(self, src, dst, sems):
        me, sib, chips = self._places()
        for j, chip in enumerate(chips):
            for w in range(len(src)):
                block = dst[w].at[_dev_index(*chip, me[2])]
                _remote(src[w], block, sems[0], sems[1], 7 * w + 1 + j, me).wait_recv()
                _remote(block, block, sems[0], sems[1], 7 * w + 4 + j, sib).start()

    def finish(self, src, dst, sems):
        me, sib, chips = self._places()
        n = len(src)
        for w in range(n):
            _remote(src[w], dst[w].at[_dev_index(*sib)], sems[0], sems[1], 7 * w, me).wait_recv()
            for j, chip in enumerate(chips):
                block = dst[w].at[_dev_index(*chip, sib[2])]
                _remote(block, block, sems[0], sems[1], 7 * w + 4 + j, me).wait_recv()
            for k in range(7):
                _remote(src[w], dst[w].at[0], sems[0], sems[1], 7 * w + k, me).wait_send()
        for cp in self._local(src, dst, sems):
            cp.wait()


class _Exchange:
    def __init__(self, partials, whole=()):
        self.n_part = len(partials)
        self.inputs = list(partials) + list(whole)
        self.out_shapes = [jax.ShapeDtypeStruct(p.shape, p.dtype) for p in partials]
        self.out_shapes += [jax.ShapeDtypeStruct((N_DEV,) + a.shape, a.dtype) for a in whole]
        n = len(self.inputs)
        self.sem_shapes = [pltpu.SemaphoreType.DMA((7 * n,)), pltpu.SemaphoreType.DMA((7 * n,)),
                           pltpu.SemaphoreType.DMA((n,))]

    def _peer(self, k):
        x, y, c = _mesh_pos()
        p = (x ^ ((k >> 2) & 1), y ^ ((k >> 1) & 1), c ^ (k & 1))
        return p, _dev_index(*p)

    def _source(self, src, w, slot):
        return src[w].at[slot] if w < self.n_part else src[w]

    def _local(self, src, dst, sems):
        me = _my_index()
        return [pltpu.make_async_copy(self._source(src, w, me), dst[w].at[me], sems[2].at[w])
                for w in range(len(src))]

    def start(self, src, dst, sems):
        me = _my_index()
        for cp in self._local(src, dst, sems):
            cp.start()
        for k in range(1, N_DEV):
            peer, peer_idx = self._peer(k)
            for w in range(len(src)):
                _remote(self._source(src, w, peer_idx), dst[w].at[me], sems[0], sems[1], 7 * w + k - 1, peer).start()

    def finish(self, src, dst, sems):
        for k in range(1, N_DEV):
            peer, peer_idx = self._peer(k)
            for w in range(len(src)):
                cp = _remote(self._source(src, w, peer_idx), dst[w].at[peer_idx], sems[0], sems[1], 7 * w + k - 1, peer)
                cp.wait_send()
                cp.wait_recv()
        for cp in self._local(src, dst, sems):
            cp.wait()


class _ExchangeTo:
    def __init__(self, partials, side):
        self.side = side
        self.inputs = list(partials)
        self.out_shapes = [jax.ShapeDtypeStruct(p.shape, p.dtype) for p in partials]
        n = len(partials)
        self.sem_shapes = [pltpu.SemaphoreType.DMA((7 * n,)), pltpu.SemaphoreType.DMA((7 * n,)),
                           pltpu.SemaphoreType.DMA((n,))]

    def _copies(self, src, dst, sems):
        x, y, c = _mesh_pos()
        me = _dev_index(x, y, c)
        receives = c == self.side
        remote = []
        for k in range(1, N_DEV):
            kx, ky, kc = (k >> 2) & 1, (k >> 1) & 1, k & 1
            peer = (x ^ kx, y ^ ky, c ^ kc)
            peer_idx = _dev_index(*peer)
            sends = c == (self.side ^ kc)
            for w in range(len(src)):
                slab = src[w].at[peer_idx]
                s = 7 * w + k - 1
                remote.append((sends, _remote(slab, dst[w].at[me], sems[0], sems[1], s, peer),
                               _remote(slab, dst[w].at[peer_idx], sems[0], sems[1], s, peer)))
        local = [pltpu.make_async_copy(src[w].at[me], dst[w].at[me], sems[2].at[w]) for w in range(len(src))]
        return receives, remote, local

    def start(self, src, dst, sems):
        receives, remote, local = self._copies(src, dst, sems)

        @pl.when(receives)
        def _():
            for cp in local:
                cp.start()

        for sends, send, _ in remote:
            pl.when(sends)(send.start)

    def finish(self, src, dst, sems):
        receives, remote, local = self._copies(src, dst, sems)
        for sends, send, arrive in remote:
            pl.when(sends)(send.wait_send)
            pl.when(receives)(arrive.wait_recv)

        @pl.when(receives)
        def _():
            for cp in local:
                cp.wait()


def _pcall(body, args, *, name, out_shape, grid, in_specs, out_specs, scratch_shapes=(), sem=None, comm=None):
    single = not isinstance(out_shape, (tuple, list))
    outs = [out_shape] if single else list(out_shape)
    ospecs = [out_specs] if single else list(out_specs)
    n_in, n_out, n_scr = len(args), len(outs), len(scratch_shapes)

    def pick(res):
        return res[0] if single else tuple(res[:n_out])

    if comm is None:
        res = pl.pallas_call(
            body, out_shape=outs, grid=grid, in_specs=list(in_specs), out_specs=ospecs,
            scratch_shapes=list(scratch_shapes), name=name, compiler_params=_params(sem),
        )(*args)
        return pick(res), ()

    nci, nco = len(comm.inputs), len(comm.out_shapes)

    def carrier(*refs):
        at = 0
        parts = []
        for size in (n_in, nci, n_out, nco, n_scr, len(comm.sem_shapes)):
            parts.append(refs[at:at + size])
            at += size
        ins, cins, o, couts, scr, sems = parts
        ids = [pl.program_id(a) for a in range(len(grid))]
        first = functools.reduce(jnp.logical_and, [i == 0 for i in ids])
        last = functools.reduce(jnp.logical_and, [i == g - 1 for i, g in zip(ids, grid)])

        body(*ins, *o, *scr)

        @pl.when(first)
        def _():
            comm.start(cins, couts, sems)

        if hasattr(comm, "middle"):
            steps = math.prod(grid)
            at = functools.reduce(lambda lin, ig: lin * ig[1] + ig[0], zip(ids, grid), 0)

            @pl.when(at == min(steps - 1, (3 * steps) // 4))
            def _():
                comm.middle(cins, couts, sems)

        @pl.when(last)
        def _():
            comm.finish(cins, couts, sems)

    hbm = pl.BlockSpec(memory_space=pltpu.HBM)
    res = pl.pallas_call(
        carrier, out_shape=outs + comm.out_shapes, grid=grid, in_specs=list(in_specs) + [hbm] * nci,
        out_specs=ospecs + [hbm] * nco, scratch_shapes=list(scratch_shapes) + comm.sem_shapes, name=name,
        compiler_params=_params(("arbitrary",) * len(grid)),
    )(*args, *comm.inputs)
    return pick(res), tuple(res[n_out:])


def _comm_call(name, comm):
    def body(*refs):
        nci, nco = len(comm.inputs), len(comm.out_shapes)
        cins, couts, sems = refs[:nci], refs[nci:nci + nco], refs[nci + nco:]
        comm.start(cins, couts, sems)
        if hasattr(comm, "middle"):
            comm.middle(cins, couts, sems)
        comm.finish(cins, couts, sems)

    hbm = pl.BlockSpec(memory_space=pltpu.HBM)
    return pl.pallas_call(
        body, out_shape=comm.out_shapes, in_specs=[hbm] * len(comm.inputs), out_specs=[hbm] * len(comm.out_shapes),
        scratch_shapes=comm.sem_shapes, name=name,
    )(*comm.inputs)


def _matmul(name, kind, a, b, out_shape, grid, a_spec, b_spec, o_spec, acc_shape, res=None, res_spec=None,
            comm=None, epilogue=None):
    nk = grid[-1]
    if epilogue is None:
        extra, extra_specs = ([res], [res_spec]) if res is not None else ([], [])
        n_out = 1
    else:
        extra, extra_specs, n_out = list(res), list(res_spec), len(out_shape)
    n_in = 2 + len(extra)

    def body(*refs):
        a_ref, b_ref = refs[0], refs[1]
        extra_refs, out_refs = refs[2:n_in], refs[n_in:n_in + n_out]

        def prod():
            return _dot(a_ref[...], b_ref[...], kind)

        def finish(acc):
            if epilogue is not None:
                ids = [pl.program_id(ax) for ax in range(len(grid) - 1)]
                first = functools.reduce(jnp.logical_and, [i == 0 for i in ids]) if ids else True
                epilogue(acc, extra_refs, out_refs, first)
                return
            if extra_refs:
                acc = acc + extra_refs[0][...]
            out_refs[0][...] = acc.astype(out_refs[0].dtype)

        if nk == 1:
            finish(prod())
        else:
            acc_ref = refs[n_in + n_out]
            k = pl.program_id(len(grid) - 1)

            @pl.when(k == 0)
            def _():
                acc_ref[...] = prod()

            @pl.when(k > 0)
            def _():
                acc_ref[...] += prod()

            @pl.when(k == nk - 1)
            def _():
                finish(acc_ref[...])

    in_specs = [a_spec, b_spec] + extra_specs
    args = (a, b, *extra)
    scratch = [pltpu.VMEM(acc_shape, F32)] if nk > 1 else []
    sem = ("arbitrary",) * len(grid) if epilogue is not None else ("parallel",) * (len(grid) - 1) + ("arbitrary",)
    out, landed = _pcall(body, args, name=name, out_shape=out_shape, grid=grid, in_specs=in_specs,
                         out_specs=o_spec, scratch_shapes=scratch, sem=sem, comm=comm)
    return out if comm is None else (out, landed)


def _mm_rows(name, a, w, out_dtype=F32, res=None, kind="nn", tm=MM_ROWS, comm=None):
    M, K = a.shape
    N = w.shape[1] if kind == "nn" else w.shape[0]
    tm = _tile(M, tm)
    res_spec = pl.BlockSpec((tm, N), lambda i, k: (i, 0)) if res is not None else None
    return _matmul(
        name, kind, a, w, jax.ShapeDtypeStruct((M, N), out_dtype), (M // tm, 1),
        pl.BlockSpec((tm, K), lambda i, k: (i, 0)), pl.BlockSpec(w.shape, lambda i, k: (0, 0)),
        pl.BlockSpec((tm, N), lambda i, k: (i, 0)), (tm, N), res, res_spec, comm)


def _residual_rms_epilogue(y, operands, outputs, first):
    x_ref, g_ref = operands
    x1_ref, h_ref = outputs
    xv = x_ref[...] + y
    x1_ref[...] = xv
    r = lax.rsqrt(jnp.mean(xv * xv, axis=-1, keepdims=True) + EPS)
    h_ref[...] = (xv * r * g_ref[...]).astype(h_ref.dtype)


def _mm_residual_rms(name, a, w, x, g, tm=MM_ROWS_RES):
    M, K = a.shape
    N = w.shape[1]
    tm = _tile(M, tm)
    row = pl.BlockSpec((tm, N), lambda i, k: (i, 0))
    return _matmul(
        name, "nn", a, w, (jax.ShapeDtypeStruct((M, N), F32), jax.ShapeDtypeStruct((M, N), BF16)), (M // tm, 1),
        pl.BlockSpec((tm, K), lambda i, k: (i, 0)), pl.BlockSpec(w.shape, lambda i, k: (0, 0)),
        (row, row), (tm, N), [x, g], [row, pl.BlockSpec((1, N), lambda i, k: (0, 0))],
        epilogue=_residual_rms_epilogue)


def _mm_tn(name, a, b, out_dtype=F32, tk=MM_TOKENS, comm=None):
    T, M = a.shape
    N = b.shape[1]
    tk = _tile(T, tk)
    return _matmul(
        name, "tn", a, b, jax.ShapeDtypeStruct((M, N), out_dtype), (1, T // tk),
        pl.BlockSpec((tk, M), lambda i, k: (k, 0)), pl.BlockSpec((tk, N), lambda i, k: (k, 0)),
        pl.BlockSpec((M, N), lambda i, k: (0, 0)), (M, N), comm=comm)


def _mm_in_gather(h, shard, tm=MM_ROWS, comm=None):
    T, K = h.shape
    n = shard.shape[1]
    tm = _tile(T, tm)
    n_tiles = T // tm
    pair_of_chip_step = {4: 1, 2: 2, 6: 3}

    def slab_of(s):
        x, y, c = _mesh_pos()
        return _dev_index(x ^ ((s >> 2) & 1), y ^ ((s >> 1) & 1), c ^ (s & 1))

    def body(h_ref, shard_ref, proj_ref, win_ref, wbuf, slot_sems, send_sems, recv_sems, local_sem):
        s, i = pl.program_id(0), pl.program_id(1)
        x, y, c = _mesh_pos()
        me, sib = (x, y, c), (x, y, 1 - c)

        def slot_copy(step):
            src = shard_ref if step == 0 else win_ref.at[slab_of(step)]
            return pltpu.make_async_copy(src, wbuf.at[step % 2], slot_sems.at[step % 2])

        def fetch(step):
            if step >= 1:
                block = win_ref.at[slab_of(step)]
                if step == 1:
                    pair = 0
                elif step % 2 == 0:
                    pair = pair_of_chip_step[step]
                else:
                    pair = 3 + pair_of_chip_step[step - 1]
                _remote(block, block, send_sems, recv_sems, pair, me).wait_recv()
                if step % 2 == 0:
                    _remote(block, block, send_sems, recv_sems, 3 + pair, sib).start()
            slot_copy(step).start()

        @pl.when((s == 0) & (i == 0))
        def _():
            land = win_ref.at[_dev_index(*me)]
            pltpu.make_async_copy(shard_ref, land, local_sem).start()
            _remote(shard_ref, land, send_sems, recv_sems, 0, sib).start()
            for step, pair in pair_of_chip_step.items():
                peer = (x ^ ((step >> 2) & 1), y ^ ((step >> 1) & 1), c)
                _remote(shard_ref, land, send_sems, recv_sems, pair, peer).start()
            fetch(0)

        for step in range(N_DEV):
            @pl.when((s == step) & (i == 0))
            def _():
                slot_copy(step).wait()

            if step + 1 < N_DEV:
                @pl.when((s == step) & (i == n_tiles - 1))
                def _():
                    fetch(step + 1)

        proj_ref[...] = _dot(h_ref[...], wbuf[s % 2])

        @pl.when((s == N_DEV - 1) & (i == n_tiles - 1))
        def _():
            for pair in range(7):
                _remote(shard_ref, win_ref.at[0], send_sems, recv_sems, pair, me).wait_send()
            pltpu.make_async_copy(shard_ref, win_ref.at[_dev_index(*me)], local_sem).wait()

    hbm = pl.BlockSpec(memory_space=pltpu.HBM)
    return _pcall(
        body, (h, shard), name="mm_in",
        out_shape=(jax.ShapeDtypeStruct((T, N_DEV * n), F32), jax.ShapeDtypeStruct((N_DEV, K, n), shard.dtype)),
        grid=(N_DEV, n_tiles), in_specs=[pl.BlockSpec((tm, K), lambda s, i: (i, 0)), hbm],
        out_specs=(pl.BlockSpec((tm, n), lambda s, i: (i, slab_of(s))), hbm),
        scratch_shapes=[pltpu.VMEM((2, K, n), shard.dtype), pltpu.SemaphoreType.DMA((2,)),
                        pltpu.SemaphoreType.DMA((7,)), pltpu.SemaphoreType.DMA((7,)), pltpu.SemaphoreType.DMA],
        sem=("arbitrary", "arbitrary"), comm=comm)


def _rms_bwd_epilogue(dh, operands, outputs, first):
    x_ref, g_ref, dres_ref = operands
    dx_ref, dg_ref = outputs
    xv = x_ref[...]
    r = lax.rsqrt(jnp.mean(xv * xv, axis=-1, keepdims=True) + EPS)
    xhat = xv * r

    @pl.when(first)
    def _():
        dg_ref[...] = jnp.zeros_like(dg_ref)

    dg_ref[...] += jnp.sum(dh * xhat, axis=0, keepdims=True)
    dxhat = dh * g_ref[...]
    dx_ref[...] = dres_ref[...] + r * (dxhat - xhat * jnp.mean(dxhat * xhat, axis=-1, keepdims=True))


def _rms_bwd_fused(M, K, tm, rms):
    row = pl.BlockSpec((tm, K), lambda i, j: (i, 0))
    vec = pl.BlockSpec((1, K), lambda i, j: (0, 0))
    x, g, dres = rms
    return dict(res=[x, g, dres], res_spec=[row, vec, row], epilogue=_rms_bwd_epilogue,
                out_shape=(jax.ShapeDtypeStruct((M, K), F32), jax.ShapeDtypeStruct((1, K), F32)), o_spec=(row, vec))


def _mm_cols_slab_t(name, a, w_slabs, rms, tm=MM_ROWS_RES, comm=None):
    M = a.shape[0]
    J, K, n = w_slabs.shape
    tm = _tile(M, tm)
    fused = _rms_bwd_fused(M, K, tm, rms)
    return _matmul(
        name, "nt", a, w_slabs, fused.pop("out_shape"), (M // tm, J),
        pl.BlockSpec((tm, n), lambda i, j: (i, j)), pl.BlockSpec((None, K, n), lambda i, j: (j, 0, 0)),
        fused.pop("o_spec"), (tm, K), comm=comm, **fused)


def _mm_tn_slab(name, a, b, n, out_dtype=F32, tk=MM_TOKENS, comm=None):
    T, M = a.shape
    J = b.shape[1] // n
    tk = _tile(T, tk)
    return _matmul(
        name, "tn", a, b, jax.ShapeDtypeStruct((J, M, n), out_dtype), (J, T // tk),
        pl.BlockSpec((tk, M), lambda j, k: (k, 0)), pl.BlockSpec((tk, n), lambda j, k: (k, j)),
        pl.BlockSpec((None, M, n), lambda j, k: (j, 0, 0)), (M, n), comm=comm)


def _rms_fwd(name, x, g, tm=512):
    T, Dm = x.shape
    tm = _tile(T, tm)

    def body(x_ref, g_ref, h_ref):
        xv = x_ref[...]
        r = lax.rsqrt(jnp.mean(xv * xv, axis=-1, keepdims=True) + EPS)
        h_ref[...] = (xv * r * g_ref[...]).astype(h_ref.dtype)

    return pl.pallas_call(
        body, out_shape=jax.ShapeDtypeStruct((T, Dm), BF16), grid=(T // tm,),
        in_specs=[pl.BlockSpec((tm, Dm), lambda i: (i, 0)), pl.BlockSpec((1, Dm), lambda i: (0, 0))],
        out_specs=pl.BlockSpec((tm, Dm), lambda i: (i, 0)), name=name, compiler_params=_params(("parallel",)),
    )(x, g)


def _rms_bwd(name, x, g, dh, dres, tm=512):
    T, Dm = x.shape
    tm = _tile(T, tm)
    want_dx = dres is not None

    def body(*refs):
        if want_dx:
            x_ref, g_ref, dh_ref, dres_ref, dx_ref, dg_ref = refs
        else:
            x_ref, g_ref, dh_ref, dg_ref = refs
        xv = x_ref[...]
        r = lax.rsqrt(jnp.mean(xv * xv, axis=-1, keepdims=True) + EPS)
        xhat = xv * r
        dhv = dh_ref[...]

        @pl.when(pl.program_id(0) == 0)
        def _():
            dg_ref[...] = jnp.zeros_like(dg_ref)

        dg_ref[...] += jnp.sum(dhv * xhat, axis=0, keepdims=True)
        if want_dx:
            dxhat = dhv * g_ref[...]
            dx_ref[...] = dres_ref[...] + r * (dxhat - xhat * jnp.mean(dxhat * xhat, axis=-1, keepdims=True))

    row = pl.BlockSpec((tm, Dm), lambda i: (i, 0))
    vec = pl.BlockSpec((1, Dm), lambda i: (0, 0))
    if want_dx:
        return pl.pallas_call(
            body, out_shape=(jax.ShapeDtypeStruct((T, Dm), F32), jax.ShapeDtypeStruct((1, Dm), F32)),
            grid=(T // tm,), in_specs=[row, vec, row, row], out_specs=(row, vec), name=name,
            compiler_params=_params(("arbitrary",)),
        )(x, g, dh, dres)
    return pl.pallas_call(
        body, out_shape=jax.ShapeDtypeStruct((1, Dm), F32), grid=(T // tm,), in_specs=[row, vec, row],
        out_specs=vec, name=name, compiler_params=_params(("arbitrary",)),
    )(x, g, dh)


def _pool_rows(S):
    return _tile(S, 256)


def _pool_count(c0, rows, w):
    t = c0 + lax.broadcasted_iota(jnp.int32, (rows, 1), 0)
    return jnp.minimum(t + 1, w).astype(F32)


def _pool_fwd(proj, w_pool, scale, B, S):
    CH = _pool_rows(S)

    def body(hp_ref, wp_ref, sc_ref, o_ref, pad_ref):
        pad_ref[0:POOL_HALO, :] = jnp.zeros((POOL_HALO, POOL_WIDTH), F32)
        pad_ref[POOL_HALO:, :] = hp_ref[...]
        for gi, w in enumerate(POOL_WINDOWS):
            cols = slice(gi * POOL_GROUP_DIM, (gi + 1) * POOL_GROUP_DIM)
            for c in range(S // CH):
                base = POOL_HALO + c * CH
                acc = pad_ref[base:base + CH, cols]
                tok = acc
                for j in range(1, w):
                    acc = acc + pad_ref[base - j:base - j + CH, cols]
                pooled = acc / _pool_count(c * CH, CH, w) - tok
                z = _dot(pooled, wp_ref[gi])
                o_ref[c * CH:(c + 1) * CH, cols] = (z * sc_ref[:, cols]).astype(o_ref.dtype)

    return pl.pallas_call(
        body, out_shape=jax.ShapeDtypeStruct((B * S, POOL_WIDTH), BF16), grid=(B,),
        in_specs=[pl.BlockSpec((S, POOL_WIDTH), lambda b: (b, 0)),
                  pl.BlockSpec(w_pool.shape, lambda b: (0, 0, 0)),
                  pl.BlockSpec((1, POOL_WIDTH), lambda b: (0, 0))],
        out_specs=pl.BlockSpec((S, POOL_WIDTH), lambda b: (b, 0)),
        scratch_shapes=[pltpu.VMEM((S + POOL_HALO, POOL_WIDTH), F32)],
        name="pool_fwd", compiler_params=_params(("parallel",)),
    )(proj, w_pool, scale)


def _pool_bwd(proj, d_ypre, w_pool, scale, B, S, comm=None):
    CH = _pool_rows(S)

    def body(hp_ref, dy_ref, wp_ref, sc_ref, dhp_ref, dwp_ref, dsc_ref, pad_ref, sc_pad_ref, dp_ref):
        @pl.when(pl.program_id(0) == 0)
        def _():
            dwp_ref[...] = jnp.zeros_like(dwp_ref)
            dsc_ref[...] = jnp.zeros_like(dsc_ref)

        pad_ref[0:POOL_HALO, :] = jnp.zeros((POOL_HALO, POOL_WIDTH), F32)
        pad_ref[POOL_HALO:, :] = hp_ref[...]
        sc_pad_ref[S:, :] = jnp.zeros((POOL_HALO, POOL_WIDTH), F32)
        for gi, w in enumerate(POOL_WINDOWS):
            cols = slice(gi * POOL_GROUP_DIM, (gi + 1) * POOL_GROUP_DIM)
            for c in range(S // CH):
                base = POOL_HALO + c * CH
                rows = slice(c * CH, (c + 1) * CH)
                acc = pad_ref[base:base + CH, cols]
                tok = acc
                for j in range(1, w):
                    acc = acc + pad_ref[base - j:base - j + CH, cols]
                cnt = _pool_count(c * CH, CH, w)
                pooled = acc / cnt - tok
                z = _dot(pooled, wp_ref[gi])
                dy = dy_ref[rows, cols]
                dsc_ref[:, cols] += jnp.sum(dy * z, axis=0, keepdims=True)
                dz = dy * sc_ref[:, cols]
                dwp_ref[gi] += _dot(pooled, dz, "tn")
                dpool = _dot(dz, wp_ref[gi], "nt")
                dp_ref[rows, cols] = dpool
                sc_pad_ref[rows, cols] = dpool / cnt
            for c in range(S // CH):
                rows = slice(c * CH, (c + 1) * CH)
                acc = sc_pad_ref[rows, cols]
                for j in range(1, w):
                    acc = acc + sc_pad_ref[c * CH + j:c * CH + j + CH, cols]
                dhp_ref[rows, cols] = (acc - dp_ref[rows, cols]).astype(dhp_ref.dtype)

    seq = pl.BlockSpec((S, POOL_WIDTH), lambda b: (b, 0))
    return _pcall(
        body, (proj, d_ypre, w_pool, scale), name="pool_bwd",
        out_shape=(jax.ShapeDtypeStruct((B * S, POOL_WIDTH), BF16),
                   jax.ShapeDtypeStruct(w_pool.shape, F32), jax.ShapeDtypeStruct((1, POOL_WIDTH), F32)),
        grid=(B,),
        in_specs=[seq, seq, pl.BlockSpec(w_pool.shape, lambda b: (0, 0, 0)),
                  pl.BlockSpec((1, POOL_WIDTH), lambda b: (0, 0))],
        out_specs=(seq, pl.BlockSpec(w_pool.shape, lambda b: (0, 0, 0)),
                   pl.BlockSpec((1, POOL_WIDTH), lambda b: (0, 0))),
        scratch_shapes=[pltpu.VMEM((S + POOL_HALO, POOL_WIDTH), F32),
                        pltpu.VMEM((S + POOL_HALO, POOL_WIDTH), F32),
                        pltpu.VMEM((S, POOL_WIDTH), F32)],
        sem=("arbitrary",), comm=comm)


def _ret_tables(S):
    half = RET_QK_DIM // 2
    inv = ROPE_BASE ** (-jnp.arange(half, dtype=F32) / half)
    ang = jnp.arange(S, dtype=F32)[:, None] * inv[None, :]
    cos, sin = jnp.cos(ang), jnp.sin(ang)
    cos_full = jnp.concatenate([cos, cos], axis=-1)
    sin_signed = jnp.concatenate([-sin, sin], axis=-1)
    C = RET_CHUNK
    lg = jnp.log1p(-jnp.exp2(-5.0 - jnp.arange(RET_HEADS, dtype=F32)))[:, None, None]
    idx = jnp.arange(C, dtype=F32)
    rel = idx[:, None] - idx[None, :]
    decay = jnp.where(rel >= 0, jnp.exp(jnp.maximum(rel, 0.0) * lg), 0.0)
    q_decay = jnp.broadcast_to(jnp.exp((idx + 1.0)[None, :, None] * lg), (RET_HEADS, C, RET_QK_DIM))
    k_decay = jnp.broadcast_to(jnp.exp((C - 1.0 - idx)[None, :, None] * lg), (RET_HEADS, C, RET_QK_DIM))
    c_decay = jnp.broadcast_to(jnp.exp(C * lg), (RET_HEADS, 1, RET_V_DIM))
    return cos_full, sin_signed, decay, q_decay, k_decay, c_decay


def _rope(x, cos_full, sin_signed):
    return x * cos_full + pltpu.roll(x, RET_QK_DIM // 2, axis=1) * sin_signed


def _rope_t(dy, cos_full, sin_signed):
    return dy * cos_full + pltpu.roll(dy * sin_signed, RET_QK_DIM // 2, axis=1)


RET_COLS = 512


def _ret_specs(N, chunk_of):
    C = RET_CHUNK

    def rows(width, col=0):
        return pl.BlockSpec((C, width), lambda b, i: (b * N + chunk_of(i), col))

    def whole(shape):
        return pl.BlockSpec(shape, lambda b, i: (0,) * len(shape))

    wide = RET_HEADS * RET_V_DIM
    return dict(
        q=rows(RET_COLS, COL_Q // RET_COLS), k=rows(RET_COLS, COL_K // RET_COLS),
        v=[rows(RET_COLS, COL_V // RET_COLS + j) for j in range(2)],
        gr=[rows(RET_COLS, COL_GR // RET_COLS + j) for j in range(2)],
        table=pl.BlockSpec((C, RET_QK_DIM), lambda b, i: (chunk_of(i), 0)),
        decay=whole((RET_HEADS, C, C)), qd=whole((RET_HEADS, C, RET_QK_DIM)), kd=whole((RET_HEADS, C, RET_QK_DIM)),
        cd=whole((RET_HEADS, 1, RET_V_DIM)), vec=whole((1, wide)), qk_rows=rows(RET_COLS), v_rows=rows(wide),
        state=pl.BlockSpec((None, None, RET_HEADS, RET_QK_DIM, RET_V_DIM), lambda b, i: (b, chunk_of(i), 0, 0, 0)))


def _head_cols(h):
    pair = slice((h % 2) * RET_V_DIM, (h % 2 + 1) * RET_V_DIM)
    return slice(h * RET_QK_DIM, (h + 1) * RET_QK_DIM), h // 2, pair, slice(h * RET_V_DIM, (h + 1) * RET_V_DIM)


def _group_norm(o):
    mu = jnp.mean(o, axis=-1, keepdims=True)
    oc = o - mu
    rstd = lax.rsqrt(jnp.mean(oc * oc, axis=-1, keepdims=True) + EPS)
    return oc * rstd, rstd


def _ret_fwd(proj, g_ret, b_ret, tables, B, S, comm=None):
    N = S // RET_CHUNK
    cos_t, sin_t, decay, q_decay, k_decay, c_decay = tables
    sp = _ret_specs(N, lambda i: i)

    def body(q_ref, k_ref, v0_ref, v1_ref, gr0_ref, gr1_ref, cos_ref, sin_ref, dec_ref, qd_ref, kd_ref, cd_ref,
             g_ref, b_ref, y_ref, rs_ref, r_ref):
        @pl.when(pl.program_id(1) == 0)
        def _():
            r_ref[...] = jnp.zeros_like(r_ref)

        cs, sn = cos_ref[...], sin_ref[...]
        heads = range(RET_HEADS)
        cols = [_head_cols(h) for h in heads]
        q = [_rope(q_ref[:, cols[h][0]], cs, sn) for h in heads]
        k = [_rope(k_ref[:, cols[h][0]], cs, sn) * (RET_QK_DIM ** -0.5) for h in heads]
        v = [(v0_ref, v1_ref)[cols[h][1]][:, cols[h][2]] for h in heads]
        R = [r_ref[h] for h in heads]
        s = [_dot(q[h], k[h], "nt") * dec_ref[h] for h in heads]
        o = [_dot(s[h], v[h]) + _dot(q[h] * qd_ref[h], R[h]) for h in heads]
        r_new = [cd_ref[h] * R[h] + _dot(k[h] * kd_ref[h], v[h], "tn") for h in heads]
        for h in heads:
            _, j, pair, wide = cols[h]
            rs_ref[h] = R[h]
            r_ref[h] = r_new[h]
            on, _ = _group_norm(o[h])
            gr = (gr0_ref, gr1_ref)[j][:, pair]
            y_ref[:, wide] = (gr * jax.nn.sigmoid(gr) * (on * g_ref[:, wide] + b_ref[:, wide])).astype(y_ref.dtype)

    state = jax.ShapeDtypeStruct((B, N, RET_HEADS, RET_QK_DIM, RET_V_DIM), F32)
    return _pcall(
        body, (proj,) * 6 + (cos_t, sin_t, decay, q_decay, k_decay, c_decay, g_ret, b_ret),
        name="ret_fwd", out_shape=(jax.ShapeDtypeStruct((B * S, RET_HEADS * RET_V_DIM), BF16), state), grid=(B, N),
        in_specs=[sp["q"], sp["k"], *sp["v"], *sp["gr"], sp["table"], sp["table"], sp["decay"], sp["qd"],
                  sp["kd"], sp["cd"], sp["vec"], sp["vec"]],
        out_specs=(sp["v_rows"], sp["state"]),
        scratch_shapes=[pltpu.VMEM((RET_HEADS, RET_QK_DIM, RET_V_DIM), F32)],
        sem=("parallel", "arbitrary"), comm=comm)


def _ret_bwd(proj, states, d_yr, g_ret, b_ret, tables, B, S, comm=None):
    N = S // RET_CHUNK
    cos_t, sin_t, decay, q_decay, k_decay, c_decay = tables
    sp = _ret_specs(N, lambda i: N - 1 - i)
    qk_scale = RET_QK_DIM ** -0.5

    def body(q_ref, k_ref, v0_ref, v1_ref, gr0_ref, gr1_ref, dy_ref, rs_ref, cos_ref, sin_ref, dec_ref, qd_ref,
             kd_ref, cd_ref, g_ref, b_ref, dq_ref, dk_ref, dv_ref, dgr_ref, dg_ref, db_ref, dr_ref):
        @pl.when((pl.program_id(0) == 0) & (pl.program_id(1) == 0))
        def _():
            dg_ref[...] = jnp.zeros_like(dg_ref)
            db_ref[...] = jnp.zeros_like(db_ref)

        @pl.when(pl.program_id(1) == 0)
        def _():
            dr_ref[...] = jnp.zeros_like(dr_ref)

        cs, sn = cos_ref[...], sin_ref[...]
        heads = range(RET_HEADS)
        cols = [_head_cols(h) for h in heads]
        q = [_rope(q_ref[:, cols[h][0]], cs, sn) for h in heads]
        k = [_rope(k_ref[:, cols[h][0]], cs, sn) * qk_scale for h in heads]
        v = [(v0_ref, v1_ref)[cols[h][1]][:, cols[h][2]] for h in heads]
        s = [_dot(q[h], k[h], "nt") * dec_ref[h] for h in heads]
        o = [_dot(s[h], v[h]) + _dot(q[h] * qd_ref[h], rs_ref[h]) for h in heads]
        do = []
        for h in heads:
            _, j, pair, wide = cols[h]
            on, rstd = _group_norm(o[h])
            g = g_ref[:, wide]
            oaff = on * g + b_ref[:, wide]
            gr = (gr0_ref, gr1_ref)[j][:, pair]
            sg = jax.nn.sigmoid(gr)
            dy = dy_ref[:, wide]
            dgr_ref[:, wide] = (dy * oaff * (sg * (1.0 + gr * (1.0 - sg)))).astype(dgr_ref.dtype)
            doaff = dy * (gr * sg)
            dg_ref[:, wide] += jnp.sum(doaff * on, axis=0, keepdims=True)
            db_ref[:, wide] += jnp.sum(doaff, axis=0, keepdims=True)
            don = doaff * g
            do.append(rstd * (don - jnp.mean(don, axis=-1, keepdims=True)
                              - on * jnp.mean(don * on, axis=-1, keepdims=True)))
        ds = [_dot(do[h], v[h], "nt") * dec_ref[h] for h in heads]
        dq = [_dot(ds[h], k[h]) + qd_ref[h] * _dot(do[h], rs_ref[h], "nt") for h in heads]
        dk = [_dot(ds[h], q[h], "tn") + kd_ref[h] * _dot(v[h], dr_ref[h], "nt") for h in heads]
        dv = [_dot(s[h], do[h], "tn") + _dot(k[h] * kd_ref[h], dr_ref[h]) for h in heads]
        dr = [cd_ref[h] * dr_ref[h] + _dot(q[h] * qd_ref[h], do[h], "tn") for h in heads]
        for h in heads:
            qk, _, _, wide = cols[h]
            dv_ref[:, wide] = dv[h].astype(dv_ref.dtype)
            dr_ref[h] = dr[h]
            dq_ref[:, qk] = _rope_t(dq[h], cs, sn).astype(dq_ref.dtype)
            dk_ref[:, qk] = _rope_t(dk[h] * qk_scale, cs, sn).astype(dk_ref.dtype)

    T = B * S
    qk_shape = jax.ShapeDtypeStruct((T, RET_HEADS * RET_QK_DIM), BF16)
    v_shape = jax.ShapeDtypeStruct((T, RET_HEADS * RET_V_DIM), BF16)
    vec_shape = jax.ShapeDtypeStruct((1, RET_HEADS * RET_V_DIM), F32)
    return _pcall(
        body, (proj,) * 6 + (d_yr, states, cos_t, sin_t, decay, q_decay, k_decay, c_decay, g_ret, b_ret),
        name="ret_bwd", out_shape=(qk_shape, qk_shape, v_shape, v_shape, vec_shape, vec_shape), grid=(B, N),
        in_specs=[sp["q"], sp["k"], *sp["v"], *sp["gr"], sp["v_rows"], sp["state"], sp["table"], sp["table"],
                  sp["decay"], sp["qd"], sp["kd"], sp["cd"], sp["vec"], sp["vec"]],
        out_specs=(sp["qk_rows"], sp["qk_rows"], sp["v_rows"], sp["v_rows"], sp["vec"], sp["vec"]),
        scratch_shapes=[pltpu.VMEM((RET_HEADS, RET_QK_DIM, RET_V_DIM), F32)],
        sem=("arbitrary", "arbitrary"), comm=comm)


def _xa_rows(S):
    return _tile(S, 256)


def _xa_groups(S, rows, size=4):
    chunks = [slice(r, r + rows) for r in range(0, S, rows)]
    return [chunks[g:g + size] for g in range(0, len(chunks), size)]


def _xa_specs(S, M):
    q = pl.BlockSpec((S, XA_HEAD_DIM), lambda b, h: (b, COL_QX // XA_HEAD_DIM + h))
    k = pl.BlockSpec((M, XA_HEAD_DIM), lambda b, h: (b, h))
    v = pl.BlockSpec((M, XA_HEAD_DIM), lambda b, h: (b, XA_HEADS + h))
    o = pl.BlockSpec((S, XA_HEAD_DIM), lambda b, h: (b, h))
    return q, k, v, o


def _softmax_rows(s):
    e = jnp.exp(s - jnp.max(s, axis=-1, keepdims=True))
    return e / jnp.sum(e, axis=-1, keepdims=True)


def _xa_fwd(proj, kv, B, S, M, comm=None):
    CH = _xa_rows(S)
    q_spec, k_spec, v_spec, o_spec = _xa_specs(S, M)

    def body(q_ref, k_ref, v_ref, o_ref):
        for group in _xa_groups(S, CH):
            sc = [_dot(q_ref[rows, :], k_ref[...], "nt") * (XA_HEAD_DIM ** -0.5) for rows in group]
            p = [_softmax_rows(s) for s in sc]
            for rows, pg in zip(group, p):
                o_ref[rows, :] = _dot(pg, v_ref[...]).astype(o_ref.dtype)

    return _pcall(
        body, (proj, kv, kv), name="xattn_fwd", out_shape=jax.ShapeDtypeStruct((B * S, XA_WIDTH), BF16),
        grid=(B, XA_HEADS), in_specs=[q_spec, k_spec, v_spec], out_specs=o_spec,
        sem=("parallel", "parallel"), comm=comm)


def _xa_bwd(proj, kv, d_o, B, S, M, comm=None):
    CH = _xa_rows(S)
    q_spec, k_spec, v_spec, o_spec = _xa_specs(S, M)
    scale = XA_HEAD_DIM ** -0.5

    def body(q_ref, k_ref, v_ref, do_ref, dq_ref, dk_ref, dv_ref):
        dk_ref[...] = jnp.zeros_like(dk_ref)
        dv_ref[...] = jnp.zeros_like(dv_ref)
        for group in _xa_groups(S, CH):
            q = [q_ref[rows, :] for rows in group]
            do = [do_ref[rows, :] for rows in group]
            p = [_softmax_rows(_dot(qg, k_ref[...], "nt") * scale) for qg in q]
            dp = [_dot(dg, v_ref[...], "nt") for dg in do]
            ds = [pg * (dpg - jnp.sum(dpg * pg, axis=-1, keepdims=True)) * scale for pg, dpg in zip(p, dp)]
            for rows, dsg in zip(group, ds):
                dq_ref[rows, :] = _dot(dsg, k_ref[...]).astype(dq_ref.dtype)
            dk_ref[...] += sum(_dot(dsg, qg, "tn") for dsg, qg in zip(ds, q))
            dv_ref[...] += sum(_dot(pg, dg, "tn") for pg, dg in zip(p, do))

    kv_out = pl.BlockSpec((M, XA_HEAD_DIM), lambda b, h: (b, h))
    return _pcall(
        body, (proj, kv, kv, d_o), name="xattn_bwd",
        out_shape=(jax.ShapeDtypeStruct((B * S, XA_WIDTH), BF16), jax.ShapeDtypeStruct((B * M, XA_WIDTH), F32),
                   jax.ShapeDtypeStruct((B * M, XA_WIDTH), F32)),
        grid=(B, XA_HEADS), in_specs=[q_spec, k_spec, v_spec, o_spec], out_specs=(o_spec, kv_out, kv_out),
        sem=("parallel", "parallel"), comm=comm)


def _gate_specs(tm):
    n = COL_GL // D_MODEL
    return [pl.BlockSpec((tm, D_MODEL), lambda i, j=j: (i, n + j)) for j in range(3)]


def _merge_fwd(proj, ys, tm=512, comm=None):
    T = proj.shape[0]
    tm = _tile(T, tm)
    row = pl.BlockSpec((tm, D_MODEL), lambda i: (i, 0))

    def body(g0, g1, g2, y0, y1, y2, o_ref):
        acc = jax.nn.sigmoid(g0[...]) * y0[...]
        acc = acc + jax.nn.sigmoid(g1[...]) * y1[...]
        acc = acc + jax.nn.sigmoid(g2[...]) * y2[...]
        o_ref[...] = acc.astype(o_ref.dtype)

    return _pcall(
        body, (proj, proj, proj, *ys), name="merge_fwd", out_shape=jax.ShapeDtypeStruct((T, D_MODEL), BF16),
        grid=(T // tm,), in_specs=_gate_specs(tm) + [row] * 3, out_specs=row, sem=("parallel",), comm=comm)


def _merge_bwd(proj, ys, d_merged, tm=512, comm=None):
    T = proj.shape[0]
    tm = _tile(T, tm)
    row = pl.BlockSpec((tm, D_MODEL), lambda i: (i, 0))

    def body(g0, g1, g2, y0, y1, y2, dm_ref, dgl_ref, d0, d1, d2):
        dm = dm_ref[...]
        for j, (g_ref, y_ref, d_ref) in enumerate(((g0, y0, d0), (g1, y1, d1), (g2, y2, d2))):
            sg = jax.nn.sigmoid(g_ref[...])
            d_ref[...] = (dm * sg).astype(d_ref.dtype)
            dgl_ref[:, j * D_MODEL:(j + 1) * D_MODEL] = (dm * y_ref[...] * sg * (1.0 - sg)).astype(dgl_ref.dtype)

    dy = jax.ShapeDtypeStruct((T, D_MODEL), BF16)
    return _pcall(
        body, (proj, proj, proj, *ys, d_merged), name="merge_bwd",
        out_shape=(jax.ShapeDtypeStruct((T, 3 * D_MODEL), BF16), dy, dy, dy), grid=(T // tm,),
        in_specs=_gate_specs(tm) + [row] * 4,
        out_specs=(pl.BlockSpec((tm, 3 * D_MODEL), lambda i: (i, 0)), row, row, row),
        sem=("parallel",), comm=comm)


def _gelu(x):
    return 0.5 * x * (1.0 + jnp.tanh(GELU_C * (x + GELU_A * x * x * x)))


def _gelu_grad(x):
    t = jnp.tanh(GELU_C * (x + GELU_A * x * x * x))
    return 0.5 * (1.0 + t) + 0.5 * x * (1.0 - t * t) * GELU_C * (1.0 + 3.0 * GELU_A * x * x)


GLU_HALO = 16


def _shift_down(x, prev, n):
    last = prev.shape[0]
    r = lax.broadcasted_iota(jnp.int32, (8, 1), 0)
    rolled = pltpu.roll(x, n, axis=0)
    head = rolled[0:8]
    for j in range(n):
        head = jnp.where(r == j, prev[last - n + j:last - n + j + 1, :], head)
    return jnp.concatenate([head, rolled[8:]], axis=0)


def _shift_up(x, nxt, n):
    rows = x.shape[0]
    r = lax.broadcasted_iota(jnp.int32, (8, 1), 0)
    rolled = pltpu.roll(x, rows - n, axis=0)
    tail = rolled[rows - 8:]
    for j in range(n):
        tail = jnp.where(r == 8 - n + j, nxt[j:j + 1, :], tail)
    return jnp.concatenate([rolled[:rows - 8], tail], axis=0)


def _conv(a, prev, cw, cb):
    return _shift_down(a, prev, 2) * cw[0:1, :] + _shift_down(a, prev, 1) * cw[1:2, :] + a * cw[2:3, :] + cb


def _glu_fwd(up, cw, cb, S, tm=1024, comm=None):
    T = up.shape[2]
    tm = _tile(S, tm)
    per_seq = S // tm

    def body(ab_ref, prev_ref, cw_ref, cb_ref, u_ref):
        i = pl.program_id(1)
        prev = jnp.where(i % per_seq == 0, 0.0, prev_ref[...].astype(F32))
        ac = _conv(ab_ref[0].astype(F32), prev, cw_ref[...], cb_ref[...])
        u_ref[...] = (_gelu(ac) * ab_ref[1].astype(F32)).astype(u_ref.dtype)

    before = tm // GLU_HALO
    return _pcall(
        body, (up, up, cw, cb), name="glu_fwd",
        out_shape=jax.ShapeDtypeStruct((FFN_SLABS, T, UP_SHARD), BF16), grid=(FFN_SLABS, T // tm),
        in_specs=[pl.BlockSpec((2, None, tm, UP_SHARD), lambda d, i: (0, d, i, 0)),
                  pl.BlockSpec((None, None, GLU_HALO, UP_SHARD),
                               lambda d, i: (0, d, jnp.maximum(i * before - 1, 0), 0)),
                  pl.BlockSpec((None, 3, UP_SHARD), lambda d, i: (d, 0, 0)),
                  pl.BlockSpec((None, 1, UP_SHARD), lambda d, i: (d, 0, 0))],
        out_specs=pl.BlockSpec((None, tm, UP_SHARD), lambda d, i: (d, i, 0)),
        sem=("parallel", "parallel"), comm=comm)


def _glu_bwd(up, d_u, cw, cb, S, tm=512, comm=None):
    T = up.shape[2]
    tm = _tile(S, tm)
    per_seq = S // tm
    n_tiles = T // tm
    per_tile = tm // GLU_HALO

    def body(ab_ref, prev_ref, abn_ref, du_ref, dun_ref, cw_ref, cb_ref, dup_ref, dcw_ref, dcb_ref):
        i = pl.program_id(1)

        @pl.when(i == 0)
        def _():
            dcw_ref[...] = jnp.zeros_like(dcw_ref)
            dcb_ref[...] = jnp.zeros_like(dcb_ref)

        cw, cb = cw_ref[...], cb_ref[...]
        a, b = ab_ref[0].astype(F32), ab_ref[1].astype(F32)
        prev = jnp.where(i % per_seq == 0, 0.0, prev_ref[...].astype(F32))
        a2, a1 = _shift_down(a, prev, 2), _shift_down(a, prev, 1)
        ac = a2 * cw[0:1, :] + a1 * cw[1:2, :] + a * cw[2:3, :] + cb
        du = du_ref[...].astype(F32)
        dup_ref[1] = (du * _gelu(ac)).astype(dup_ref.dtype)
        dac = du * b * _gelu_grad(ac)
        dcb_ref[...] += jnp.sum(dac, axis=0, keepdims=True)
        dcw_ref[0:1, :] += jnp.sum(dac * a2, axis=0, keepdims=True)
        dcw_ref[1:2, :] += jnp.sum(dac * a1, axis=0, keepdims=True)
        dcw_ref[2:3, :] += jnp.sum(dac * a, axis=0, keepdims=True)
        acn = _conv(abn_ref[0].astype(F32), a[tm - GLU_HALO:, :], cw, cb)
        dacn = jnp.where(i % per_seq == per_seq - 1, 0.0,
                         dun_ref[...].astype(F32) * abn_ref[1].astype(F32) * _gelu_grad(acn))
        da = dac * cw[2:3, :] + _shift_up(dac, dacn, 1) * cw[1:2, :] + _shift_up(dac, dacn, 2) * cw[0:1, :]
        dup_ref[0] = da.astype(dup_ref.dtype)

    def nxt(i):
        return jnp.minimum((i + 1) * per_tile, T // GLU_HALO - 1)

    return _pcall(
        body, (up, up, up, d_u, d_u, cw, cb), name="glu_bwd",
        out_shape=(jax.ShapeDtypeStruct((2, FFN_SLABS, T, UP_SHARD), BF16),
                   jax.ShapeDtypeStruct((FFN_SLABS, 3, UP_SHARD), F32),
                   jax.ShapeDtypeStruct((FFN_SLABS, 1, UP_SHARD), F32)),
        grid=(FFN_SLABS, n_tiles),
        in_specs=[pl.BlockSpec((2, None, tm, UP_SHARD), lambda d, i: (0, d, i, 0)),
                  pl.BlockSpec((None, None, GLU_HALO, UP_SHARD),
                               lambda d, i: (0, d, jnp.maximum(i * per_tile - 1, 0), 0)),
                  pl.BlockSpec((2, None, GLU_HALO, UP_SHARD), lambda d, i: (0, d, nxt(i), 0)),
                  pl.BlockSpec((None, tm, UP_SHARD), lambda d, i: (d, i, 0)),
                  pl.BlockSpec((None, GLU_HALO, UP_SHARD), lambda d, i: (d, nxt(i), 0)),
                  pl.BlockSpec((None, 3, UP_SHARD), lambda d, i: (d, 0, 0)),
                  pl.BlockSpec((None, 1, UP_SHARD), lambda d, i: (d, 0, 0))],
        out_specs=(pl.BlockSpec((2, None, tm, UP_SHARD), lambda d, i: (0, d, i, 0)),
                   pl.BlockSpec((None, 3, UP_SHARD), lambda d, i: (d, 0, 0)),
                   pl.BlockSpec((None, 1, UP_SHARD), lambda d, i: (d, 0, 0))),
        sem=("parallel", "arbitrary"), comm=comm)


def _mm_up(h2, w_up_t, tm=MM_ROWS, comm=None):
    T, K = h2.shape
    tm = _tile(T, tm)
    return _matmul(
        "mm_up", "nt", h2, w_up_t, jax.ShapeDtypeStruct((N_DEV, T, UP_SHARD), BF16), (N_DEV, T // tm, 1),
        pl.BlockSpec((tm, K), lambda j, i, k: (i, 0)), pl.BlockSpec((None, UP_SHARD, K), lambda j, i, k: (j, 0, 0)),
        pl.BlockSpec((None, tm, UP_SHARD), lambda j, i, k: (j, i, 0)), (tm, UP_SHARD), comm=comm)


def _loss_epilogue(ffn, operands, outputs, first):
    x1_ref, t_ref, g_ref = operands
    dx_ref, dg_ref, loss_ref = outputs

    @pl.when(first)
    def _():
        dg_ref[...] = jnp.zeros_like(dg_ref)
        loss_ref[...] = jnp.zeros_like(loss_ref)

    xv = x1_ref[...] + ffn
    r = lax.rsqrt(jnp.mean(xv * xv, axis=-1, keepdims=True) + EPS)
    xhat = xv * r
    err = xhat * g_ref[...] - t_ref[...]
    loss_ref[...] += (0.5 / D_MODEL) * jnp.sum(err * err)
    dy = err * (1.0 / D_MODEL)
    dg_ref[...] += jnp.sum(dy * xhat, axis=0, keepdims=True)
    dxhat = dy * g_ref[...]
    dx_ref[...] = r * (dxhat - xhat * jnp.mean(dxhat * xhat, axis=-1, keepdims=True))


def _mm_down_loss(u, w_down, x1, target, g_final, tm=MM_ROWS_RES):
    J, T, n = u.shape
    tm = _tile(T, tm)
    row = pl.BlockSpec((tm, D_MODEL), lambda i, d: (i, 0))
    vec = pl.BlockSpec((1, D_MODEL), lambda i, d: (0, 0))
    vec_shape = jax.ShapeDtypeStruct((1, D_MODEL), F32)
    return _matmul(
        "mm_down", "nn", u, w_down, (jax.ShapeDtypeStruct((T, D_MODEL), F32), vec_shape, vec_shape), (T // tm, J),
        pl.BlockSpec((None, tm, n), lambda i, d: (d, i, 0)), pl.BlockSpec((None, n, D_MODEL), lambda i, d: (d, 0, 0)),
        (row, vec, vec), (tm, D_MODEL), [x1, target, g_final], [row, row, vec], epilogue=_loss_epilogue)


def _mm_down_t(dx, w_down, tm=MM_ROWS):
    T = dx.shape[0]
    J, n, _ = w_down.shape
    tm = _tile(T, tm)
    return _matmul(
        "mm_down_t", "nt", dx, w_down, jax.ShapeDtypeStruct((J, T, n), BF16), (J, T // tm, 1),
        pl.BlockSpec((tm, D_MODEL), lambda d, i, k: (i, 0)), pl.BlockSpec((None, n, D_MODEL), lambda d, i, k: (d, 0, 0)),
        pl.BlockSpec((None, tm, n), lambda d, i, k: (d, i, 0)), (tm, n))


def _mm_dw_down(u, dx, tk=MM_TOKENS):
    J, T, n = u.shape
    tk = _tile(T, tk)
    return _matmul(
        "mm_dw_down", "tn", u, dx, jax.ShapeDtypeStruct((J, n, D_MODEL), BF16), (J, T // tk),
        pl.BlockSpec((None, tk, n), lambda d, k: (d, k, 0)), pl.BlockSpec((tk, D_MODEL), lambda d, k: (k, 0)),
        pl.BlockSpec((None, n, D_MODEL), lambda d, k: (d, 0, 0)), (n, D_MODEL))


def _mm_dw_up(h2, d_up, tk=MM_TOKENS):
    T, K = h2.shape
    tk = _tile(T, tk)
    return _matmul(
        "mm_dw_up", "tn", d_up, h2, jax.ShapeDtypeStruct((N_DEV, UP_SHARD, K), BF16), (N_DEV, T // tk),
        pl.BlockSpec((None, tk, UP_SHARD), lambda j, k: (j, k, 0)), pl.BlockSpec((tk, K), lambda j, k: (k, 0)),
        pl.BlockSpec((None, UP_SHARD, K), lambda j, k: (j, 0, 0)), (UP_SHARD, K))


def _mm_up_t(d_up, w_up_t, rms, tm=MM_ROWS_RES, comm=None):
    J, T, n = d_up.shape
    K = w_up_t.shape[2]
    tm = _tile(T, tm)
    fused = _rms_bwd_fused(T, K, tm, rms)
    return _matmul(
        "mm_up_t", "nn", d_up, w_up_t, fused.pop("out_shape"), (T // tm, J),
        pl.BlockSpec((None, tm, n), lambda i, j: (j, i, 0)), pl.BlockSpec((None, n, K), lambda i, j: (j, 0, 0)),
        fused.pop("o_spec"), (tm, K), comm=comm, **fused)


def _cast_shards(shards):
    def body(*refs):
        n = len(refs) // 2
        for src, dst in zip(refs[:n], refs[n:]):
            dst[...] = src[...].astype(dst.dtype)

    return pl.pallas_call(
        body, out_shape=[jax.ShapeDtypeStruct(s.shape, BF16) for s in shards], name="cast_shards",
        compiler_params=pltpu.CompilerParams(vmem_limit_bytes=VMEM_LIMIT),
    )(*shards)


def _adamw(w, g, m, v):
    m = ADAM_B1 * m + (1.0 - ADAM_B1) * g
    v = ADAM_B2 * v + (1.0 - ADAM_B2) * (g * g)
    m_hat = m / (1.0 - ADAM_B1 ** ADAM_STEP)
    v_hat = v / (1.0 - ADAM_B2 ** ADAM_STEP)
    delta = -ADAM_LR * (m_hat / (jnp.sqrt(v_hat) + ADAM_EPS) + ADAM_WD * w)
    return delta, m, v


def _sum_parts(p_ref):
    g = p_ref[0].astype(F32)
    for d in range(1, N_DEV):
        g = g + p_ref[d].astype(F32)
    return g


ADAM_ROWS = 512


def _reduce_adam(name, parts, w, m, v):
    R, Cn = w.shape
    by_rows = sum(p.shape[1] for p in parts) == R and len(parts) > 1
    common = math.gcd(*[p.shape[1] for p in parts])
    tr = max(t for t in range(8, min(common, ADAM_ROWS) + 1, 8) if common % t == 0)
    n_tiles = [p.shape[1] // tr for p in parts]
    first = [sum(n_tiles[:j]) for j in range(len(parts))] if by_rows else [0] * len(parts)

    def body(*refs):
        p_refs = refs[:len(parts)]
        w_ref, m_ref, v_ref, g_out, d_out, m_out, v_out = refs[len(parts):]

        def update(p_ref):
            g = _sum_parts(p_ref)
            delta, m_new, v_new = _adamw(w_ref[...], g, m_ref[...], v_ref[...])
            g_out[...] = g
            d_out[...] = delta
            m_out[...] = m_new
            v_out[...] = v_new

        if len(parts) == 1:
            update(p_refs[0])
        elif by_rows:
            i = pl.program_id(0)
            for p_ref, t0, n in zip(p_refs, first, n_tiles):
                pl.when((i >= t0) & (i < t0 + n))(functools.partial(update, p_ref))
        else:
            c = lax.axis_index("c")
            for side, p_ref in enumerate(p_refs):
                pl.when(c == side)(functools.partial(update, p_ref))

    def part_spec(t0, n):
        return pl.BlockSpec((N_DEV, tr, Cn), lambda i: (0, jnp.clip(i - t0, 0, n - 1), 0))

    row = pl.BlockSpec((tr, Cn), lambda i: (i, 0))
    shape = jax.ShapeDtypeStruct((R, Cn), F32)
    return pl.pallas_call(
        body, out_shape=(shape,) * 4, grid=(R // tr,),
        in_specs=[part_spec(t0, n) for t0, n in zip(first, n_tiles)] + [row, row, row],
        out_specs=(row,) * 4, name=name, compiler_params=_params(("parallel",)),
    )(*parts, w, m, v)


def _small_adam(name, gathered, params):
    n_g, n_p = len(gathered), len(params)

    def body(*refs):
        g_refs = refs[:n_g]
        wmv = refs[n_g:n_g + 3 * n_p]
        sums = refs[n_g + 3 * n_p:2 * n_g + 3 * n_p]
        upd = refs[2 * n_g + 3 * n_p:]
        for j in range(n_g):
            g = _sum_parts(g_refs[j])
            sums[j][...] = g
            if j < n_p:
                w_ref, m_ref, v_ref = wmv[3 * j:3 * j + 3]
                delta, m_new, v_new = _adamw(w_ref[...], g, m_ref[...], v_ref[...])
                upd[3 * j][...] = delta
                upd[3 * j + 1][...] = m_new
                upd[3 * j + 2][...] = v_new

    flat = [a for wmv in params for a in wmv]
    out_shape = [jax.ShapeDtypeStruct(g.shape[1:], F32) for g in gathered]
    out_shape += [jax.ShapeDtypeStruct(a.shape, F32) for a in flat]
    res = pl.pallas_call(body, out_shape=out_shape, name=name)(*gathered, *flat)
    return res[:n_g], [tuple(res[n_g + 3 * j:n_g + 3 * j + 3]) for j in range(n_p)]


def _adam_only(name, g, w, m, v):
    def body(g_ref, w_ref, m_ref, v_ref, d_out, m_out, v_out):
        delta, m_new, v_new = _adamw(w_ref[...], g_ref[...], m_ref[...], v_ref[...])
        d_out[...] = delta
        m_out[...] = m_new
        v_out[...] = v_new

    shape = jax.ShapeDtypeStruct(w.shape, F32)
    return pl.pallas_call(body, out_shape=(shape,) * 3, name=name)(g, w, m, v)


def kernel(x, mem, g_mix, w_in, w_pool, pool_scale, w_a, g_ret, b_ret, w_r, g_mem, w_mem_kv, w_c, w_out, g_ffn, w_up, conv_w, conv_b, w_down, g_final, loss_target, m_g_mix, m_w_in, m_w_pool, m_pool_scale, m_w_a, m_g_ret, m_b_ret, m_w_r, m_g_mem, m_w_mem_kv, m_w_c, m_w_out, m_g_ffn, m_w_up, m_conv_w, m_conv_b, m_w_down, m_g_final, v_g_mix, v_w_in, v_w_pool, v_pool_scale, v_w_a, v_g_ret, v_b_ret, v_w_r, v_g_mem, v_w_mem_kv, v_w_c, v_w_out, v_g_ffn, v_w_up, v_conv_w, v_conv_b, v_w_down, v_g_final):
    B, S, _ = x.shape
    M = mem.shape[1]
    T = B * S
    me = _my_index()
    x2d = x.reshape(T, D_MODEL)
    mem2d = mem.reshape(B * M, D_MODEL)
    tgt2d = loss_target.reshape(T, D_MODEL)
    g_final2 = g_final.reshape(1, D_MODEL)

    big = dict(w_in=w_in[0], w_a=w_a[0], w_r=w_r[0], w_mem_kv=w_mem_kv[0], w_c=w_c[0], w_out=w_out[0],
               w_up=w_up[0].T, w_down=w_down[0])
    names = list(big)
    cast = dict(zip(names, _cast_shards([big[n] for n in names])))
    cb = conv_b[0].reshape(FFN_SLABS, 1, UP_SHARD)
    wp = w_pool[0]
    tables = _ret_tables(S)

    h = _rms_fwd("rms_mix", x2d, g_mix)
    early = ("w_a", "w_r", "w_mem_kv", "w_c", "w_out")
    (proj, Win), landed = _mm_in_gather(h, cast["w_in"], comm=_Gather([cast[n] for n in early] + [conv_w[0]]))
    W = dict(zip(early, landed))
    half = D_MODEL // 2
    (yr, ret_states), (Wup_lo,) = _ret_fwd(proj, g_ret, b_ret, tables, B, S,
                                           comm=_Gather([cast["w_up"][:, :half]]))
    cw_full = landed[-1].transpose(1, 0, 2).reshape(3, FFN_HIDDEN)
    cw = cw_full.reshape(3, FFN_SLABS, UP_SHARD).transpose(1, 0, 2)
    Wa = W["w_a"].transpose(1, 0, 2).reshape(POOL_WIDTH, D_MODEL)
    Wc = W["w_c"].transpose(1, 0, 2).reshape(XA_WIDTH, D_MODEL)
    Wr = W["w_r"].reshape(D_MODEL, D_MODEL)
    Wkv = W["w_mem_kv"].reshape(D_MODEL, D_MODEL)
    Wout = W["w_out"].reshape(D_MODEL, D_MODEL)
    ypre = _pool_fwd(proj, wp, pool_scale, B, S)
    y_pool = _mm_rows("mm_a", ypre, Wa, BF16)
    y_ret = _mm_rows("mm_r", yr, Wr, BF16)
    mem_n = _rms_fwd("rms_mem", mem2d, g_mem)
    kv = _mm_rows("mm_kv", mem_n, Wkv)
    o_mem = _xa_fwd(proj, kv, B, S, M)[0]
    y_mem = _mm_rows("mm_c", o_mem, Wc, BF16)
    ys = (y_pool, y_ret, y_mem)
    merged, (Wup_hi,) = _merge_fwd(proj, ys, comm=_Gather([cast["w_up"][:, half:]]))
    Wup = jnp.concatenate([Wup_lo, Wup_hi], axis=2)
    x1, h2 = _mm_residual_rms("mm_out", merged, Wout, x2d, g_ffn)
    up, (Wdown,) = _mm_up(h2, Wup, comm=_Gather([cast["w_down"]]))
    up = up.reshape(2, FFN_SLABS, T, UP_SHARD)
    Wdown = Wdown.reshape(FFN_SLABS, UP_SHARD, D_MODEL)
    u = _glu_fwd(up, cw, cb, S)[0]

    dx2, dg_final, loss_part = _mm_down_loss(u, Wdown, x1, tgt2d, g_final2)
    received = {}
    d_u = _mm_down_t(dx2, Wdown)
    dW_down = _mm_dw_down(u, dx2)
    (d_up, d_cw, d_cb), (received["w_down"],) = _glu_bwd(
        up, d_u, cw, cb, S, comm=_Exchange([dW_down.reshape(N_DEV, -1, D_MODEL)]))
    d_up = d_up.reshape(N_DEV, T, UP_SHARD)
    dW_up = _mm_dw_up(h2, d_up)
    (dx1, dg_ffn), (up_c0,) = _mm_up_t(d_up, Wup, (x1, g_ffn, dx2), comm=_ExchangeTo([dW_up], 0))
    d_merged = _mm_rows("mm_out_t", dx1, Wout, kind="nt")
    dW_out = _mm_tn("mm_dw_out", merged, dx1, BF16)
    (d_gl, d_y_pool, d_y_ret, d_y_mem), (up_c1,) = _merge_bwd(proj, ys, d_merged, comm=_ExchangeTo([dW_up], 1))
    received["w_up"] = [up_c0, up_c1]
    dW_c = _mm_tn("mm_dw_c", o_mem, d_y_mem, BF16)
    d_o_mem = _mm_rows("mm_c_t", d_y_mem, Wc, kind="nt")
    (d_qx, d_kmem, d_vmem), (received["w_out"],) = _xa_bwd(
        proj, kv, d_o_mem, B, S, M, comm=_Exchange([dW_out.reshape(N_DEV, -1, D_MODEL)]))
    d_kv = jnp.concatenate([d_kmem, d_vmem], axis=1)
    dW_kv = _mm_tn("mm_dw_kv", mem_n, d_kv, BF16)
    d_mem_n = _mm_rows("mm_kv_t", d_kv, Wkv, kind="nt")
    dg_mem = _rms_bwd("rms_mem_bwd", mem2d, g_mem, d_mem_n, None)
    dW_a = _mm_tn("mm_dw_a", ypre, d_y_pool, BF16)
    d_ypre = _mm_rows("mm_a_t", d_y_pool, Wa, kind="nt")
    (d_hp, dw_pool, d_scale), (received["w_a"],) = _pool_bwd(
        proj, d_ypre, wp, pool_scale, B, S,
        comm=_Exchange([dW_a.reshape(POOL_WIDTH, N_DEV, -1).transpose(1, 0, 2)]))
    dW_r = _mm_tn("mm_dw_r", yr, d_y_ret, BF16)
    d_yr = _mm_rows("mm_r_t", d_y_ret, Wr, kind="nt")
    (d_q, d_k, d_v, d_gr, dg_ret, db_ret), landed = _ret_bwd(
        proj, ret_states, d_yr, g_ret, b_ret, tables, B, S,
        comm=_Exchange([dW_r.reshape(N_DEV, -1, D_MODEL), dW_c.reshape(XA_WIDTH, N_DEV, -1).transpose(1, 0, 2),
                        dW_kv.reshape(N_DEV, -1, D_MODEL)]))
    received["w_r"], received["w_c"], received["w_mem_kv"] = landed
    small_names = ["w_pool", "pool_scale", "g_ret", "b_ret", "g_mem", "g_ffn", "conv_b", "g_final"]
    small_grads = [dw_pool, d_scale, dg_ret, db_ret, dg_mem, dg_ffn, d_cb.reshape(1, FFN_HIDDEN), dg_final,
                   d_cw.transpose(1, 0, 2).reshape(3, FFN_HIDDEN), loss_part]
    d_proj = jnp.concatenate([d_hp, d_q, d_k, d_v, d_gr, d_qx, d_gl], axis=1)
    dW_in0, small_all = _mm_tn_slab("mm_dw_in0", h[:, :W_IN_FIRST_ROWS], d_proj, IN_SHARD, BF16,
                                    comm=_Exchange([], whole=small_grads))
    dW_in1, (in0,) = _mm_tn_slab("mm_dw_in1", h[:, W_IN_FIRST_ROWS:], d_proj, IN_SHARD, BF16,
                                 comm=_Exchange([dW_in0]))
    (grad_x, dg_mix), (in1,) = _mm_cols_slab_t("mm_in_t", d_proj, Win, (x2d, g_mix, dx1), comm=_Exchange([dW_in1]))
    received["w_in"] = [in0, in1]
    (g_mix_all,) = _comm_call("gather_g_mix", _Exchange([], whole=[dg_mix]))

    args = dict(g_mix=g_mix, w_in=w_in, w_pool=w_pool, pool_scale=pool_scale, w_a=w_a, g_ret=g_ret, b_ret=b_ret,
                w_r=w_r, g_mem=g_mem, w_mem_kv=w_mem_kv, w_c=w_c, w_out=w_out, g_ffn=g_ffn, w_up=w_up,
                conv_w=conv_w, conv_b=conv_b, w_down=w_down, g_final=g_final)
    m_in = dict(g_mix=m_g_mix, w_in=m_w_in, w_pool=m_w_pool, pool_scale=m_pool_scale, w_a=m_w_a, g_ret=m_g_ret,
                b_ret=m_b_ret, w_r=m_w_r, g_mem=m_g_mem, w_mem_kv=m_w_mem_kv, w_c=m_w_c, w_out=m_w_out,
                g_ffn=m_g_ffn, w_up=m_w_up, conv_w=m_conv_w, conv_b=m_conv_b, w_down=m_w_down, g_final=m_g_final)
    v_in = dict(g_mix=v_g_mix, w_in=v_w_in, w_pool=v_w_pool, pool_scale=v_pool_scale, w_a=v_w_a, g_ret=v_g_ret,
                b_ret=v_b_ret, w_r=v_w_r, g_mem=v_g_mem, w_mem_kv=v_w_mem_kv, w_c=v_w_c, w_out=v_w_out,
                g_ffn=v_g_ffn, w_up=v_w_up, conv_w=v_conv_w, conv_b=v_conv_b, w_down=v_w_down, g_final=v_g_final)

    grads, deltas, new_m, new_v = {}, {}, {}, {}
    for n in names:
        parts = received[n] if isinstance(received[n], list) else [received[n]]
        flip = (lambda a: a.T) if n == "w_up" else (lambda a: a)
        outs = _reduce_adam("adam_" + n, parts, big[n], flip(m_in[n][0]), flip(v_in[n][0]))
        for store, val in zip((grads, deltas, new_m, new_v), outs):
            store[n] = flip(val)[None]

    def as_small(a):
        return a.reshape(a.shape[-3:]) if a.ndim > 2 else a.reshape(1, -1)

    def small_update(call_name, param_names, gathered):
        params = [tuple(as_small(d[n]) for d in (args, m_in, v_in)) for n in param_names]
        sums, updates = _small_adam(call_name, gathered, params)
        for n, g, (d_, m_, v_) in zip(param_names, sums, updates):
            shape = args[n].shape
            grads[n], deltas[n], new_m[n], new_v[n] = (a.reshape(shape) for a in (g, d_, m_, v_))
        return sums[len(param_names):]

    g_cw_full, loss_row = small_update("adam_small", small_names, small_all)
    loss = loss_row[0, 0]
    small_update("adam_g_mix", ["g_mix"], [g_mix_all])

    shard_cols = FFN_HIDDEN // N_DEV
    g_cw = lax.dynamic_slice_in_dim(g_cw_full, me * shard_cols, shard_cols, axis=1)
    d_, m_, v_ = _adam_only("adam_conv_w", g_cw, conv_w[0], m_conv_w[0], v_conv_w[0])
    grads["conv_w"], deltas["conv_w"], new_m["conv_w"], new_v["conv_w"] = g_cw[None], d_[None], m_[None], v_[None]

    order = ["g_mix", "w_in", "w_pool", "pool_scale", "w_a", "g_ret", "b_ret", "w_r", "g_mem", "w_mem_kv", "w_c",
             "w_out", "g_ffn", "w_up", "conv_w", "conv_b", "w_down", "g_final"]
    return (loss, grad_x.reshape(B, S, D_MODEL), *[grads[n] for n in order], *[deltas[n] for n in order],
            *[new_m[n] for n in order], *[new_v[n] for n in order])
```

```python
import functools
import math

import jax
import jax.numpy as jnp
from jax import lax
from jax.experimental import pallas as pl
from jax.experimental.pallas import tpu as pltpu

F32 = jnp.float32
BF16 = jnp.bfloat16

N_DEV = 8
D_MODEL = 1024
POOL_WINDOWS = (2, 4, 8, 16)
POOL_GROUP_DIM = 128
POOL_WIDTH = 512
POOL_HALO = 16
RET_HEADS = 4
RET_QK_DIM = 128
RET_V_DIM = 256
RET_CHUNK = 128
ROPE_BASE = 10000.0
XA_HEADS = 4
XA_HEAD_DIM = 128
XA_WIDTH = 512
IN_WIDTH = 7168
IN_SHARD = IN_WIDTH // N_DEV
FFN_HIDDEN = 2816
UP_SHARD = 2 * FFN_HIDDEN // N_DEV
FFN_SLABS = FFN_HIDDEN // UP_SHARD
EPS = 1e-6
ADAM_LR = 0.001
ADAM_B1 = 0.9
ADAM_B2 = 0.999
ADAM_EPS = 1e-08
ADAM_WD = 0.01
ADAM_STEP = 10
GELU_C = math.sqrt(2.0 / math.pi)
GELU_A = 0.044715
VMEM_LIMIT = 56 * 1024 * 1024
MM_ROWS = 2048
MM_ROWS_RES = 1024
MM_TOKENS = 2048
W_IN_FIRST_ROWS = 384
MESH = pl.DeviceIdType.MESH

COL_Q, COL_K, COL_V, COL_GR, COL_QX, COL_GL = 512, 1024, 1536, 2560, 3584, 4096

_DIMS = {
    "nn": (((1,), (0,)), ((), ())),
    "nt": (((1,), (1,)), ((), ())),
    "tn": (((0,), (0,)), ((), ())),
}


def _dot(a, b, kind="nn"):
    return lax.dot_general(a.astype(BF16), b.astype(BF16), _DIMS[kind], preferred_element_type=F32)


def _params(sem, vmem=VMEM_LIMIT):
    return pltpu.CompilerParams(dimension_semantics=sem, vmem_limit_bytes=vmem)


def _tile(n, pref):
    t = min(n, pref)
    while n % t:
        t //= 2
    return t


def _mesh_pos():
    return lax.axis_index("x"), lax.axis_index("y"), lax.axis_index("c")


def _dev_index(x, y, c):
    return 4 * x + 2 * y + c


def _my_index():
    return _dev_index(*_mesh_pos())


def _remote(src, dst, send_sems, recv_sems, s, to):
    return pltpu.make_async_remote_copy(src_ref=src, dst_ref=dst, send_sem=send_sems.at[s], recv_sem=recv_sems.at[s],
                                        device_id=to, device_id_type=MESH)


class _Gather:
    def __init__(self, shards):
        self.inputs = list(shards)
        self.out_shapes = [jax.ShapeDtypeStruct((N_DEV,) + s.shape, s.dtype) for s in shards]
        n = len(shards)
        self.sem_shapes = [pltpu.SemaphoreType.DMA((7 * n,)), pltpu.SemaphoreType.DMA((7 * n,)),
                           pltpu.SemaphoreType.DMA((n,))]

    def _places(self):
        x, y, c = _mesh_pos()
        return (x, y, c), (x, y, 1 - c), [(1 - x, y), (x, 1 - y), (1 - x, 1 - y)]

    def _local(self, src, dst, sems):
        me = _my_index()
        return [pltpu.make_async_copy(src[w], dst[w].at[me], sems[2].at[w]) for w in range(len(src))]

    def start(self, src, dst, sems):
        me, sib, chips = self._places()
        for cp in self._local(src, dst, sems):
            cp.start()
        for w in range(len(src)):
            land = dst[w].at[_dev_index(*me)]
            _remote(src[w], land, sems[0], sems[1], 7 * w, sib).start()
            for j, chip in enumerate(chips):
                _remote(src[w], land, sems[0], sems[1], 7 * w + 1 + j, (*chip, me[2])).start()

    def middle(self, src, dst, sems):
        me, sib, chips = self._places()
        for j, chip in enumerate(chips):
            for w in range(len(src)):
                block = dst[w].at[_dev_index(*chip, me[2])]
                _remote(src[w], block, sems[0], sems[1], 7 * w + 1 + j, me).wait_recv()
                _remote(block, block, sems[0], sems[1], 7 * w + 4 + j, sib).start()

    def finish(self, src, dst, sems):
        me, sib, chips = self._places()
        n = len(src)
        for w in range(n):
            _remote(src[w], dst[w].at[_dev_index(*sib)], sems[0], sems[1], 7 * w, me).wait_recv()
            for j, chip in enumerate(chips):
                block = dst[w].at[_dev_index(*chip, sib[2])]
                _remote(block, block, sems[0], sems[1], 7 * w + 4 + j, me).wait_recv()
            for k in range(7):
                _remote(src[w], dst[w].at[0], sems[0], sems[1], 7 * w + k, me).wait_send()
        for cp in self._local(src, dst, sems):
            cp.wait()


class _Exchange:
    def __init__(self, partials, whole=()):
        self.n_part = len(partials)
        self.inputs = list(partials) + list(whole)
        self.out_shapes = [jax.ShapeDtypeStruct(p.shape, p.dtype) for p in partials]
        self.out_shapes += [jax.ShapeDtypeStruct((N_DEV,) + a.shape, a.dtype) for a in whole]
        n = len(self.inputs)
        self.sem_shapes = [pltpu.SemaphoreType.DMA((7 * n,)), pltpu.SemaphoreType.DMA((7 * n,)),
                           pltpu.SemaphoreType.DMA((n,))]

    def _peer(self, k):
        x, y, c = _mesh_pos()
        p = (x ^ ((k >> 2) & 1), y ^ ((k >> 1) & 1), c ^ (k & 1))
        return p, _dev_index(*p)

    def _source(self, src, w, slot):
        return src[w].at[slot] if w < self.n_part else src[w]

    def _local(self, src, dst, sems):
        me = _my_index()
        return [pltpu.make_async_copy(self._source(src, w, me), dst[w].at[me], sems[2].at[w])
                for w in range(len(src))]

    def start(self, src, dst, sems):
        me = _my_index()
        for cp in self._local(src, dst, sems):
            cp.start()
        for k in range(1, N_DEV):
            peer, peer_idx = self._peer(k)
            for w in range(len(src)):
                _remote(self._source(src, w, peer_idx), dst[w].at[me], sems[0], sems[1], 7 * w + k - 1, peer).start()

    def finish(self, src, dst, sems):
        for k in range(1, N_DEV):
            peer, peer_idx = self._peer(k)
            for w in range(len(src)):
                cp = _remote(self._source(src, w, peer_idx), dst[w].at[peer_idx], sems[0], sems[1], 7 * w + k - 1, peer)
                cp.wait_send()
                cp.wait_recv()
        for cp in self._local(src, dst, sems):
            cp.wait()


class _ExchangeTo:
    def __init__(self, partials, side):
        self.side = side
        self.inputs = list(partials)
        self.out_shapes = [jax.ShapeDtypeStruct(p.shape, p.dtype) for p in partials]
        n = len(partials)
        self.sem_shapes = [pltpu.SemaphoreType.DMA((7 * n,)), pltpu.SemaphoreType.DMA((7 * n,)),
                           pltpu.SemaphoreType.DMA((n,))]

    def _copies(self, src, dst, sems):
        x, y, c = _mesh_pos()
        me = _dev_index(x, y, c)
        receives = c == self.side
        remote = []
        for k in range(1, N_DEV):
            kx, ky, kc = (k >> 2) & 1, (k >> 1) & 1, k & 1
            peer = (x ^ kx, y ^ ky, c ^ kc)
            peer_idx = _dev_index(*peer)
            sends = c == (self.side ^ kc)
            for w in range(len(src)):
                slab = src[w].at[peer_idx]
                s = 7 * w + k - 1
                remote.append((sends, _remote(slab, dst[w].at[me], sems[0], sems[1], s, peer),
                               _remote(slab, dst[w].at[peer_idx], sems[0], sems[1], s, peer)))
        local = [pltpu.make_async_copy(src[w].at[me], dst[w].at[me], sems[2].at[w]) for w in range(len(src))]
        return receives, remote, local

    def start(self, src, dst, sems):
        receives, remote, local = self._copies(src, dst, sems)

        @pl.when(receives)
        def _():
            for cp in local:
                cp.start()

        for sends, send, _ in remote:
            pl.when(sends)(send.start)

    def finish(self, src, dst, sems):
        receives, remote, local = self._copies(src, dst, sems)
        for sends, send, arrive in remote:
            pl.when(sends)(send.wait_send)
            pl.when(receives)(arrive.wait_recv)

        @pl.when(receives)
        def _():
            for cp in local:
                cp.wait()


def _pcall(body, args, *, name, out_shape, grid, in_specs, out_specs, scratch_shapes=(), sem=None, comm=None):
    single = not isinstance(out_shape, (tuple, list))
    outs = [out_shape] if single else list(out_shape)
    ospecs = [out_specs] if single else list(out_specs)
    n_in, n_out, n_scr = len(args), len(outs), len(scratch_shapes)

    def pick(res):
        return res[0] if single else tuple(res[:n_out])

    if comm is None:
        res = pl.pallas_call(
            body, out_shape=outs, grid=grid, in_specs=list(in_specs), out_specs=ospecs,
            scratch_shapes=list(scratch_shapes), name=name, compiler_params=_params(sem),
        )(*args)
        return pick(res), ()

    nci, nco = len(comm.inputs), len(comm.out_shapes)

    def carrier(*refs):
        at = 0
        parts = []
        for size in (n_in, nci, n_out, nco, n_scr, len(comm.sem_shapes)):
            parts.append(refs[at:at + size])
            at += size
        ins, cins, o, couts, scr, sems = parts
        ids = [pl.program_id(a) for a in range(len(grid))]
        first = functools.reduce(jnp.logical_and, [i == 0 for i in ids])
        last = functools.reduce(jnp.logical_and, [i == g - 1 for i, g in zip(ids, grid)])

        body(*ins, *o, *scr)

        @pl.when(first)
        def _():
            comm.start(cins, couts, sems)

        if hasattr(comm, "middle"):
            steps = math.prod(grid)
            at = functools.reduce(lambda lin, ig: lin * ig[1] + ig[0], zip(ids, grid), 0)

            @pl.when(at == min(steps - 1, (3 * steps) // 4))
            def _():
                comm.middle(cins, couts, sems)

        @pl.when(last)
        def _():
            comm.finish(cins, couts, sems)

    hbm = pl.BlockSpec(memory_space=pltpu.HBM)
    res = pl.pallas_call(
        carrier, out_shape=outs + comm.out_shapes, grid=grid, in_specs=list(in_specs) + [hbm] * nci,
        out_specs=ospecs + [hbm] * nco, scratch_shapes=list(scratch_shapes) + comm.sem_shapes, name=name,
        compiler_params=_params(("arbitrary",) * len(grid)),
    )(*args, *comm.inputs)
    return pick(res), tuple(res[n_out:])


def _comm_call(name, comm):
    def body(*refs):
        nci, nco = len(comm.inputs), len(comm.out_shapes)
        cins, couts, sems = refs[:nci], refs[nci:nci + nco], refs[nci + nco:]
        comm.start(cins, couts, sems)
        if hasattr(comm, "middle"):
            comm.middle(cins, couts, sems)
        comm.finish(cins, couts, sems)

    hbm = pl.BlockSpec(memory_space=pltpu.HBM)
    return pl.pallas_call(
        body, out_shape=comm.out_shapes, in_specs=[hbm] * len(comm.inputs), out_specs=[hbm] * len(comm.out_shapes),
        scratch_shapes=comm.sem_shapes, name=name,
    )(*comm.inputs)


def _matmul(name, kind, a, b, out_shape, grid, a_spec, b_spec, o_spec, acc_shape, res=None, res_spec=None,
            comm=None, epilogue=None):
    nk = grid[-1]
    if epilogue is None:
        extra, extra_specs = ([res], [res_spec]) if res is not None else ([], [])
        n_out = 1
    else:
        extra, extra_specs, n_out = list(res), list(res_spec), len(out_shape)
    n_in = 2 + len(extra)

    def body(*refs):
        a_ref, b_ref = refs[0], refs[1]
        extra_refs, out_refs = refs[2:n_in], refs[n_in:n_in + n_out]

        def prod():
            return _dot(a_ref[...], b_ref[...], kind)

        def finish(acc):
            if epilogue is not None:
                ids = [pl.program_id(ax) for ax in range(len(grid) - 1)]
                first = functools.reduce(jnp.logical_and, [i == 0 for i in ids]) if ids else True
                epilogue(acc, extra_refs, out_refs, first)
                return
            if extra_refs:
                acc = acc + extra_refs[0][...]
            out_refs[0][...] = acc.astype(out_refs[0].dtype)

        if nk == 1:
            finish(prod())
        else:
            acc_ref = refs[n_in + n_out]
            k = pl.program_id(len(grid) - 1)

            @pl.when(k == 0)
            def _():
                acc_ref[...] = prod()

            @pl.when(k > 0)
            def _():
                acc_ref[...] += prod()

            @pl.when(k == nk - 1)
            def _():
                finish(acc_ref[...])

    in_specs = [a_spec, b_spec] + extra_specs
    args = (a, b, *extra)
    scratch = [pltpu.VMEM(acc_shape, F32)] if nk > 1 else []
    sem = ("arbitrary",) * len(grid) if epilogue is not None else ("parallel",) * (len(grid) - 1) + ("arbitrary",)
    out, landed = _pcall(body, args, name=name, out_shape=out_shape, grid=grid, in_specs=in_specs,
                         out_specs=o_spec, scratch_shapes=scratch, sem=sem, comm=comm)
    return out if comm is None else (out, landed)


def _mm_rows(name, a, w, out_dtype=F32, res=None, kind="nn", tm=MM_ROWS, comm=None):
    M, K = a.shape
    N = w.shape[1] if kind == "nn" else w.shape[0]
    tm = _tile(M, tm)
    res_spec = pl.BlockSpec((tm, N), lambda i, k: (i, 0)) if res is not None else None
    return _matmul(
        name, kind, a, w, jax.ShapeDtypeStruct((M, N), out_dtype), (M // tm, 1),
        pl.BlockSpec((tm, K), lambda i, k: (i, 0)), pl.BlockSpec(w.shape, lambda i, k: (0, 0)),
        pl.BlockSpec((tm, N), lambda i, k: (i, 0)), (tm, N), res, res_spec, comm)


def _residual_rms_epilogue(y, operands, outputs, first):
    x_ref, g_ref = operands
    x1_ref, h_ref = outputs
    xv = x_ref[...] + y
    x1_ref[...] = xv
    r = lax.rsqrt(jnp.mean(xv * xv, axis=-1, keepdims=True) + EPS)
    h_ref[...] = (xv * r * g_ref[...]).astype(h_ref.dtype)


def _mm_residual_rms(name, a, w, x, g, tm=MM_ROWS_RES):
    M, K = a.shape
    N = w.shape[1]
    tm = _tile(M, tm)
    row = pl.BlockSpec((tm, N), lambda i, k: (i, 0))
    return _matmul(
        name, "nn", a, w, (jax.ShapeDtypeStruct((M, N), F32), jax.ShapeDtypeStruct((M, N), BF16)), (M // tm, 1),
        pl.BlockSpec((tm, K), lambda i, k: (i, 0)), pl.BlockSpec(w.shape, lambda i, k: (0, 0)),
        (row, row), (tm, N), [x, g], [row, pl.BlockSpec((1, N), lambda i, k: (0, 0))],
        epilogue=_residual_rms_epilogue)


def _mm_tn(name, a, b, out_dtype=F32, tk=MM_TOKENS, comm=None):
    T, M = a.shape
    N = b.shape[1]
    tk = _tile(T, tk)
    return _matmul(
        name, "tn", a, b, jax.ShapeDtypeStruct((M, N), out_dtype), (1, T // tk),
        pl.BlockSpec((tk, M), lambda i, k: (k, 0)), pl.BlockSpec((tk, N), lambda i, k: (k, 0)),
        pl.BlockSpec((M, N), lambda i, k: (0, 0)), (M, N), comm=comm)


def _mm_in_gather(h, shard, tm=MM_ROWS, comm=None):
    T, K = h.shape
    n = shard.shape[1]
    tm = _tile(T, tm)
    n_tiles = T // tm
    pair_of_chip_step = {4: 1, 2: 2, 6: 3}

    def slab_of(s):
        x, y, c = _mesh_pos()
        return _dev_index(x ^ ((s >> 2) & 1), y ^ ((s >> 1) & 1), c ^ (s & 1))

    def body(h_ref, shard_ref, proj_ref, win_ref, wbuf, slot_sems, send_sems, recv_sems, local_sem):
        s, i = pl.program_id(0), pl.program_id(1)
        x, y, c = _mesh_pos()
        me, sib = (x, y, c), (x, y, 1 - c)

        def slot_copy(step):
            src = shard_ref if step == 0 else win_ref.at[slab_of(step)]
            return pltpu.make_async_copy(src, wbuf.at[step % 2], slot_sems.at[step % 2])

        def fetch(step):
            if step >= 1:
                block = win_ref.at[slab_of(step)]
                if step == 1:
                    pair = 0
                elif step % 2 == 0:
                    pair = pair_of_chip_step[step]
                else:
                    pair = 3 + pair_of_chip_step[step - 1]
                _remote(block, block, send_sems, recv_sems, pair, me).wait_recv()
                if step % 2 == 0:
                    _remote(block, block, send_sems, recv_sems, 3 + pair, sib).start()
            slot_copy(step).start()

        @pl.when((s == 0) & (i == 0))
        def _():
            land = win_ref.at[_dev_index(*me)]
            pltpu.make_async_copy(shard_ref, land, local_sem).start()
            _remote(shard_ref, land, send_sems, recv_sems, 0, sib).start()
            for step, pair in pair_of_chip_step.items():
                peer = (x ^ ((step >> 2) & 1), y ^ ((step >> 1) & 1), c)
                _remote(shard_ref, land, send_sems, recv_sems, pair, peer).start()
            fetch(0)

        for step in range(N_DEV):
            @pl.when((s == step) & (i == 0))
            def _():
                slot_copy(step).wait()

            if step + 1 < N_DEV:
                @pl.when((s == step) & (i == n_tiles - 1))
                def _():
                    fetch(step + 1)

        proj_ref[...] = _dot(h_ref[...], wbuf[s % 2])

        @pl.when((s == N_DEV - 1) & (i == n_tiles - 1))
        def _():
            for pair in range(7):
                _remote(shard_ref, win_ref.at[0], send_sems, recv_sems, pair, me).wait_send()
            pltpu.make_async_copy(shard_ref, win_ref.at[_dev_index(*me)], local_sem).wait()

    hbm = pl.BlockSpec(memory_space=pltpu.HBM)
    return _pcall(
        body, (h, shard), name="mm_in",
        out_shape=(jax.ShapeDtypeStruct((T, N_DEV * n), F32), jax.ShapeDtypeStruct((N_DEV, K, n), shard.dtype)),
        grid=(N_DEV, n_tiles), in_specs=[pl.BlockSpec((tm, K), lambda s, i: (i, 0)), hbm],
        out_specs=(pl.BlockSpec((tm, n), lambda s, i: (i, slab_of(s))), hbm),
        scratch_shapes=[pltpu.VMEM((2, K, n), shard.dtype), pltpu.SemaphoreType.DMA((2,)),
                        pltpu.SemaphoreType.DMA((7,)), pltpu.SemaphoreType.DMA((7,)), pltpu.SemaphoreType.DMA],
        sem=("arbitrary", "arbitrary"), comm=comm)


def _rms_bwd_epilogue(dh, operands, outputs, first):
    x_ref, g_ref, dres_ref = operands
    dx_ref, dg_ref = outputs
    xv = x_ref[...]
    r = lax.rsqrt(jnp.mean(xv * xv, axis=-1, keepdims=True) + EPS)
    xhat = xv * r

    @pl.when(first)
    def _():
        dg_ref[...] = jnp.zeros_like(dg_ref)

    dg_ref[...] += jnp.sum(dh * xhat, axis=0, keepdims=True)
    dxhat = dh * g_ref[...]
    dx_ref[...] = dres_ref[...] + r * (dxhat - xhat * jnp.mean(dxhat * xhat, axis=-1, keepdims=True))


def _rms_bwd_fused(M, K, tm, rms):
    row = pl.BlockSpec((tm, K), lambda i, j: (i, 0))
    vec = pl.BlockSpec((1, K), lambda i, j: (0, 0))
    x, g, dres = rms
    return dict(res=[x, g, dres], res_spec=[row, vec, row], epilogue=_rms_bwd_epilogue,
                out_shape=(jax.ShapeDtypeStruct((M, K), F32), jax.ShapeDtypeStruct((1, K), F32)), o_spec=(row, vec))


def _mm_cols_slab_t(name, a, w_slabs, rms, tm=MM_ROWS_RES, comm=None):
    M = a.shape[0]
    J, K, n = w_slabs.shape
    tm = _tile(M, tm)
    fused = _rms_bwd_fused(M, K, tm, rms)
    return _matmul(
        name, "nt", a, w_slabs, fused.pop("out_shape"), (M // tm, J),
        pl.BlockSpec((tm, n), lambda i, j: (i, j)), pl.BlockSpec((None, K, n), lambda i, j: (j, 0, 0)),
        fused.pop("o_spec"), (tm, K), comm=comm, **fused)


def _mm_tn_slab(name, a, b, n, out_dtype=F32, tk=MM_TOKENS, comm=None):
    T, M = a.shape
    J = b.shape[1] // n
    tk = _tile(T, tk)
    return _matmul(
        name, "tn", a, b, jax.ShapeDtypeStruct((J, M, n), out_dtype), (J, T // tk),
        pl.BlockSpec((tk, M), lambda j, k: (k, 0)), pl.BlockSpec((tk, n), lambda j, k: (k, j)),
        pl.BlockSpec((None, M, n), lambda j, k: (j, 0, 0)), (M, n), comm=comm)


def _rms_fwd(name, x, g, tm=512):
    T, Dm = x.shape
    tm = _tile(T, tm)

    def body(x_ref, g_ref, h_ref):
        xv = x_ref[...]
        r = lax.rsqrt(jnp.mean(xv * xv, axis=-1, keepdims=True) + EPS)
        h_ref[...] = (xv * r * g_ref[...]).astype(h_ref.dtype)

    return pl.pallas_call(
        body, out_shape=jax.ShapeDtypeStruct((T, Dm), BF16), grid=(T // tm,),
        in_specs=[pl.BlockSpec((tm, Dm), lambda i: (i, 0)), pl.BlockSpec((1, Dm), lambda i: (0, 0))],
        out_specs=pl.BlockSpec((tm, Dm), lambda i: (i, 0)), name=name, compiler_params=_params(("parallel",)),
    )(x, g)


def _rms_bwd(name, x, g, dh, dres, tm=512):
    T, Dm = x.shape
    tm = _tile(T, tm)
    want_dx = dres is not None

    def body(*refs):
        if want_dx:
            x_ref, g_ref, dh_ref, dres_ref, dx_ref, dg_ref = refs
        else:
            x_ref, g_ref, dh_ref, dg_ref = refs
        xv = x_ref[...]
        r = lax.rsqrt(jnp.mean(xv * xv, axis=-1, keepdims=True) + EPS)
        xhat = xv * r
        dhv = dh_ref[...]

        @pl.when(pl.program_id(0) == 0)
        def _():
            dg_ref[...] = jnp.zeros_like(dg_ref)

        dg_ref[...] += jnp.sum(dhv * xhat, axis=0, keepdims=True)
        if want_dx:
            dxhat = dhv * g_ref[...]
            dx_ref[...] = dres_ref[...] + r * (dxhat - xhat * jnp.mean(dxhat * xhat, axis=-1, keepdims=True))

    row = pl.BlockSpec((tm, Dm), lambda i: (i, 0))
    vec = pl.BlockSpec((1, Dm), lambda i: (0, 0))
    if want_dx:
        return pl.pallas_call(
            body, out_shape=(jax.ShapeDtypeStruct((T, Dm), F32), jax.ShapeDtypeStruct((1, Dm), F32)),
            grid=(T // tm,), in_specs=[row, vec, row, row], out_specs=(row, vec), name=name,
            compiler_params=_params(("arbitrary",)),
        )(x, g, dh, dres)
    return pl.pallas_call(
        body, out_shape=jax.ShapeDtypeStruct((1, Dm), F32), grid=(T // tm,), in_specs=[row, vec, row],
        out_specs=vec, name=name, compiler_params=_params(("arbitrary",)),
    )(x, g, dh)


def _pool_rows(S):
    return _tile(S, 256)


def _pool_count(c0, rows, w):
    t = c0 + lax.broadcasted_iota(jnp.int32, (rows, 1), 0)
    return jnp.minimum(t + 1, w).astype(F32)


def _pool_fwd(proj, w_pool, scale, B, S):
    CH = _pool_rows(S)

    def body(hp_ref, wp_ref, sc_ref, o_ref, pad_ref):
        pad_ref[0:POOL_HALO, :] = jnp.zeros((POOL_HALO, POOL_WIDTH), F32)
        pad_ref[POOL_HALO:, :] = hp_ref[...]
        for gi, w in enumerate(POOL_WINDOWS):
            cols = slice(gi * POOL_GROUP_DIM, (gi + 1) * POOL_GROUP_DIM)
            for c in range(S // CH):
                base = POOL_HALO + c * CH
                acc = pad_ref[base:base + CH, cols]
                tok = acc
                for j in range(1, w):
                    acc = acc + pad_ref[base - j:base - j + CH, cols]
                pooled = acc / _pool_count(c * CH, CH, w) - tok
                z = _dot(pooled, wp_ref[gi])
                o_ref[c * CH:(c + 1) * CH, cols] = (z * sc_ref[:, cols]).astype(o_ref.dtype)

    return pl.pallas_call(
        body, out_shape=jax.ShapeDtypeStruct((B * S, POOL_WIDTH), BF16), grid=(B,),
        in_specs=[pl.BlockSpec((S, POOL_WIDTH), lambda b: (b, 0)),
                  pl.BlockSpec(w_pool.shape, lambda b: (0, 0, 0)),
                  pl.BlockSpec((1, POOL_WIDTH), lambda b: (0, 0))],
        out_specs=pl.BlockSpec((S, POOL_WIDTH), lambda b: (b, 0)),
        scratch_shapes=[pltpu.VMEM((S + POOL_HALO, POOL_WIDTH), F32)],
        name="pool_fwd", compiler_params=_params(("parallel",)),
    )(proj, w_pool, scale)


def _pool_bwd(proj, d_ypre, w_pool, scale, B, S, comm=None):
    CH = _pool_rows(S)

    def body(hp_ref, dy_ref, wp_ref, sc_ref, dhp_ref, dwp_ref, dsc_ref, pad_ref, sc_pad_ref, dp_ref):
        @pl.when(pl.program_id(0) == 0)
        def _():
            dwp_ref[...] = jnp.zeros_like(dwp_ref)
            dsc_ref[...] = jnp.zeros_like(dsc_ref)

        pad_ref[0:POOL_HALO, :] = jnp.zeros((POOL_HALO, POOL_WIDTH), F32)
        pad_ref[POOL_HALO:, :] = hp_ref[...]
        sc_pad_ref[S:, :] = jnp.zeros((POOL_HALO, POOL_WIDTH), F32)
        for gi, w in enumerate(POOL_WINDOWS):
            cols = slice(gi * POOL_GROUP_DIM, (gi + 1) * POOL_GROUP_DIM)
            for c in range(S // CH):
                base = POOL_HALO + c * CH
                rows = slice(c * CH, (c + 1) * CH)
                acc = pad_ref[base:base + CH, cols]
                tok = acc
                for j in range(1, w):
                    acc = acc + pad_ref[base - j:base - j + CH, cols]
                cnt = _pool_count(c * CH, CH, w)
                pooled = acc / cnt - tok
                z = _dot(pooled, wp_ref[gi])
                dy = dy_ref[rows, cols]
                dsc_ref[:, cols] += jnp.sum(dy * z, axis=0, keepdims=True)
                dz = dy * sc_ref[:, cols]
                dwp_ref[gi] += _dot(pooled, dz, "tn")
                dpool = _dot(dz, wp_ref[gi], "nt")
                dp_ref[rows, cols] = dpool
                sc_pad_ref[rows, cols] = dpool / cnt
            for c in range(S // CH):
                rows = slice(c * CH, (c + 1) * CH)
                acc = sc_pad_ref[rows, cols]
                for j in range(1, w):
                    acc = acc + sc_pad_ref[c * CH + j:c * CH + j + CH, cols]
                dhp_ref[rows, cols] = (acc - dp_ref[rows, cols]).astype(dhp_ref.dtype)

    seq = pl.BlockSpec((S, POOL_WIDTH), lambda b: (b, 0))
    return _pcall(
        body, (proj, d_ypre, w_pool, scale), name="pool_bwd",
        out_shape=(jax.ShapeDtypeStruct((B * S, POOL_WIDTH), BF16),
                   jax.ShapeDtypeStruct(w_pool.shape, F32), jax.ShapeDtypeStruct((1, POOL_WIDTH), F32)),
        grid=(B,),
        in_specs=[seq, seq, pl.BlockSpec(w_pool.shape, lambda b: (0, 0, 0)),
                  pl.BlockSpec((1, POOL_WIDTH), lambda b: (0, 0))],
        out_specs=(seq, pl.BlockSpec(w_pool.shape, lambda b: (0, 0, 0)),
                   pl.BlockSpec((1, POOL_WIDTH), lambda b: (0, 0))),
        scratch_shapes=[pltpu.VMEM((S + POOL_HALO, POOL_WIDTH), F32),
                        pltpu.VMEM((S + POOL_HALO, POOL_WIDTH), F32),
                        pltpu.VMEM((S, POOL_WIDTH), F32)],
        sem=("arbitrary",), comm=comm)


def _ret_tables(S):
    half = RET_QK_DIM // 2
    inv = ROPE_BASE ** (-jnp.arange(half, dtype=F32) / half)
    ang = jnp.arange(S, dtype=F32)[:, None] * inv[None, :]
    cos, sin = jnp.cos(ang), jnp.sin(ang)
    cos_full = jnp.concatenate([cos, cos], axis=-1)
    sin_signed = jnp.concatenate([-sin, sin], axis=-1)
    C = RET_CHUNK
    lg = jnp.log1p(-jnp.exp2(-5.0 - jnp.arange(RET_HEADS, dtype=F32)))[:, None, None]
    idx = jnp.arange(C, dtype=F32)
    rel = idx[:, None] - idx[None, :]
    decay = jnp.where(rel >= 0, jnp.exp(jnp.maximum(rel, 0.0) * lg), 0.0)
    q_decay = jnp.broadcast_to(jnp.exp((idx + 1.0)[None, :, None] * lg), (RET_HEADS, C, RET_QK_DIM))
    k_decay = jnp.broadcast_to(jnp.exp((C - 1.0 - idx)[None, :, None] * lg), (RET_HEADS, C, RET_QK_DIM))
    c_decay = jnp.broadcast_to(jnp.exp(C * lg), (RET_HEADS, 1, RET_V_DIM))
    return cos_full, sin_signed, decay, q_decay, k_decay, c_decay


def _rope(x, cos_full, sin_signed):
    return x * cos_full + pltpu.roll(x, RET_QK_DIM // 2, axis=1) * sin_signed


def _rope_t(dy, cos_full, sin_signed):
    return dy * cos_full + pltpu.roll(dy * sin_signed, RET_QK_DIM // 2, axis=1)


RET_COLS = 512


def _ret_specs(N, chunk_of):
    C = RET_CHUNK

    def rows(width, col=0):
        return pl.BlockSpec((C, width), lambda b, i: (b * N + chunk_of(i), col))

    def whole(shape):
        return pl.BlockSpec(shape, lambda b, i: (0,) * len(shape))

    wide = RET_HEADS * RET_V_DIM
    return dict(
        q=rows(RET_COLS, COL_Q // RET_COLS), k=rows(RET_COLS, COL_K // RET_COLS),
        v=[rows(RET_COLS, COL_V // RET_COLS + j) for j in range(2)],
        gr=[rows(RET_COLS, COL_GR // RET_COLS + j) for j in range(2)],
        table=pl.BlockSpec((C, RET_QK_DIM), lambda b, i: (chunk_of(i), 0)),
        decay=whole((RET_HEADS, C, C)), qd=whole((RET_HEADS, C, RET_QK_DIM)), kd=whole((RET_HEADS, C, RET_QK_DIM)),
        cd=whole((RET_HEADS, 1, RET_V_DIM)), vec=whole((1, wide)), qk_rows=rows(RET_COLS), v_rows=rows(wide),
        state=pl.BlockSpec((None, None, RET_HEADS, RET_QK_DIM, RET_V_DIM), lambda b, i: (b, chunk_of(i), 0, 0, 0)))


def _head_cols(h):
    pair = slice((h % 2) * RET_V_DIM, (h % 2 + 1) * RET_V_DIM)
    return slice(h * RET_QK_DIM, (h + 1) * RET_QK_DIM), h // 2, pair, slice(h * RET_V_DIM, (h + 1) * RET_V_DIM)


def _group_norm(o):
    mu = jnp.mean(o, axis=-1, keepdims=True)
    oc = o - mu
    rstd = lax.rsqrt(jnp.mean(oc * oc, axis=-1, keepdims=True) + EPS)
    return oc * rstd, rstd


def _ret_fwd(proj, g_ret, b_ret, tables, B, S, comm=None):
    N = S // RET_CHUNK
    cos_t, sin_t, decay, q_decay, k_decay, c_decay = tables
    sp = _ret_specs(N, lambda i: i)

    def body(q_ref, k_ref, v0_ref, v1_ref, gr0_ref, gr1_ref, cos_ref, sin_ref, dec_ref, qd_ref, kd_ref, cd_ref,
             g_ref, b_ref, y_ref, rs_ref, r_ref):
        @pl.when(pl.program_id(1) == 0)
        def _():
            r_ref[...] = jnp.zeros_like(r_ref)

        cs, sn = cos_ref[...], sin_ref[...]
        heads = range(RET_HEADS)
        cols = [_head_cols(h) for h in heads]
        q = [_rope(q_ref[:, cols[h][0]], cs, sn) for h in heads]
        k = [_rope(k_ref[:, cols[h][0]], cs, sn) * (RET_QK_DIM ** -0.5) for h in heads]
        v = [(v0_ref, v1_ref)[cols[h][1]][:, cols[h][2]] for h in heads]
        R = [r_ref[h] for h in heads]
        s = [_dot(q[h], k[h], "nt") * dec_ref[h] for h in heads]
        o = [_dot(s[h], v[h]) + _dot(q[h] * qd_ref[h], R[h]) for h in heads]
        r_new = [cd_ref[h] * R[h] + _dot(k[h] * kd_ref[h], v[h], "tn") for h in heads]
        for h in heads:
            _, j, pair, wide = cols[h]
            rs_ref[h] = R[h]
            r_ref[h] = r_new[h]
            on, _ = _group_norm(o[h])
            gr = (gr0_ref, gr1_ref)[j][:, pair]
            y_ref[:, wide] = (gr * jax.nn.sigmoid(gr) * (on * g_ref[:, wide] + b_ref[:, wide])).astype(y_ref.dtype)

    state = jax.ShapeDtypeStruct((B, N, RET_HEADS, RET_QK_DIM, RET_V_DIM), F32)
    return _pcall(
        body, (proj,) * 6 + (cos_t, sin_t, decay, q_decay, k_decay, c_decay, g_ret, b_ret),
        name="ret_fwd", out_shape=(jax.ShapeDtypeStruct((B * S, RET_HEADS * RET_V_DIM), BF16), state), grid=(B, N),
        in_specs=[sp["q"], sp["k"], *sp["v"], *sp["gr"], sp["table"], sp["table"], sp["decay"], sp["qd"],
                  sp["kd"], sp["cd"], sp["vec"], sp["vec"]],
        out_specs=(sp["v_rows"], sp["state"]),
        scratch_shapes=[pltpu.VMEM((RET_HEADS, RET_QK_DIM, RET_V_DIM), F32)],
        sem=("parallel", "arbitrary"), comm=comm)


def _ret_bwd(proj, states, d_yr, g_ret, b_ret, tables, B, S, comm=None):
    N = S // RET_CHUNK
    cos_t, sin_t, decay, q_decay, k_decay, c_decay = tables
    sp = _ret_specs(N, lambda i: N - 1 - i)
    qk_scale = RET_QK_DIM ** -0.5

    def body(q_ref, k_ref, v0_ref, v1_ref, gr0_ref, gr1_ref, dy_ref, rs_ref, cos_ref, sin_ref, dec_ref, qd_ref,
             kd_ref, cd_ref, g_ref, b_ref, dq_ref, dk_ref, dv_ref, dgr_ref, dg_ref, db_ref, dr_ref):
        @pl.when((pl.program_id(0) == 0) & (pl.program_id(1) == 0))
        def _():
            dg_ref[...] = jnp.zeros_like(dg_ref)
            db_ref[...] = jnp.zeros_like(db_ref)

        @pl.when(pl.program_id(1) == 0)
        def _():
            dr_ref[...] = jnp.zeros_like(dr_ref)

        cs, sn = cos_ref[...], sin_ref[...]
        heads = range(RET_HEADS)
        cols = [_head_cols(h) for h in heads]
        q = [_rope(q_ref[:, cols[h][0]], cs, sn) for h in heads]
        k = [_rope(k_ref[:, cols[h][0]], cs, sn) * qk_scale for h in heads]
        v = [(v0_ref, v1_ref)[cols[h][1]][:, cols[h][2]] for h in heads]
        s = [_dot(q[h], k[h], "nt") * dec_ref[h] for h in heads]
        o = [_dot(s[h], v[h]) + _dot(q[h] * qd_ref[h], rs_ref[h]) for h in heads]
        do = []
        for h in heads:
            _, j, pair, wide = cols[h]
            on, rstd = _group_norm(o[h])
            g = g_ref[:, wide]
            oaff = on * g + b_ref[:, wide]
            gr = (gr0_ref, gr1_ref)[j][:, pair]
            sg = jax.nn.sigmoid(gr)
            dy = dy_ref[:, wide]
            dgr_ref[:, wide] = (dy * oaff * (sg * (1.0 + gr * (1.0 - sg)))).astype(dgr_ref.dtype)
            doaff = dy * (gr * sg)
            dg_ref[:, wide] += jnp.sum(doaff * on, axis=0, keepdims=True)
            db_ref[:, wide] += jnp.sum(doaff, axis=0, keepdims=True)
            don = doaff * g
            do.append(rstd * (don - jnp.mean(don, axis=-1, keepdims=True)
                              - on * jnp.mean(don * on, axis=-1, keepdims=True)))
        ds = [_dot(do[h], v[h], "nt") * dec_ref[h] for h in heads]
        dq = [_dot(ds[h], k[h]) + qd_ref[h] * _dot(do[h], rs_ref[h], "nt") for h in heads]
        dk = [_dot(ds[h], q[h], "tn") + kd_ref[h] * _dot(v[h], dr_ref[h], "nt") for h in heads]
        dv = [_dot(s[h], do[h], "tn") + _dot(k[h] * kd_ref[h], dr_ref[h]) for h in heads]
        dr = [cd_ref[h] * dr_ref[h] + _dot(q[h] * qd_ref[h], do[h], "tn") for h in heads]
        for h in heads:
            qk, _, _, wide = cols[h]
            dv_ref[:, wide] = dv[h].astype(dv_ref.dtype)
            dr_ref[h] = dr[h]
            dq_ref[:, qk] = _rope_t(dq[h], cs, sn).astype(dq_ref.dtype)
            dk_ref[:, qk] = _rope_t(dk[h] * qk_scale, cs, sn).astype(dk_ref.dtype)

    T = B * S
    qk_shape = jax.ShapeDtypeStruct((T, RET_HEADS * RET_QK_DIM), BF16)
    v_shape = jax.ShapeDtypeStruct((T, RET_HEADS * RET_V_DIM), BF16)
    vec_shape = jax.ShapeDtypeStruct((1, RET_HEADS * RET_V_DIM), F32)
    return _pcall(
        body, (proj,) * 6 + (d_yr, states, cos_t, sin_t, decay, q_decay, k_decay, c_decay, g_ret, b_ret),
        name="ret_bwd", out_shape=(qk_shape, qk_shape, v_shape, v_shape, vec_shape, vec_shape), grid=(B, N),
        in_specs=[sp["q"], sp["k"], *sp["v"], *sp["gr"], sp["v_rows"], sp["state"], sp["table"], sp["table"],
                  sp["decay"], sp["qd"], sp["kd"], sp["cd"], sp["vec"], sp["vec"]],
        out_specs=(sp["qk_rows"], sp["qk_rows"], sp["v_rows"], sp["v_rows"], sp["vec"], sp["vec"]),
        scratch_shapes=[pltpu.VMEM((RET_HEADS, RET_QK_DIM, RET_V_DIM), F32)],
        sem=("arbitrary", "arbitrary"), comm=comm)


def _xa_rows(S):
    return _tile(S, 256)


def _xa_groups(S, rows, size=4):
    chunks = [slice(r, r + rows) for r in range(0, S, rows)]
    return [chunks[g:g + size] for g in range(0, len(chunks), size)]


def _xa_specs(S, M):
    q = pl.BlockSpec((S, XA_HEAD_DIM), lambda b, h: (b, COL_QX // XA_HEAD_DIM + h))
    k = pl.BlockSpec((M, XA_HEAD_DIM), lambda b, h: (b, h))
    v = pl.BlockSpec((M, XA_HEAD_DIM), lambda b, h: (b, XA_HEADS + h))
    o = pl.BlockSpec((S, XA_HEAD_DIM), lambda b, h: (b, h))
    return q, k, v, o


def _softmax_rows(s):
    e = jnp.exp(s - jnp.max(s, axis=-1, keepdims=True))
    return e / jnp.sum(e, axis=-1, keepdims=True)


def _xa_fwd(proj, kv, B, S, M, comm=None):
    CH = _xa_rows(S)
    q_spec, k_spec, v_spec, o_spec = _xa_specs(S, M)

    def body(q_ref, k_ref, v_ref, o_ref):
        for group in _xa_groups(S, CH):
            sc = [_dot(q_ref[rows, :], k_ref[...], "nt") * (XA_HEAD_DIM ** -0.5) for rows in group]
            p = [_softmax_rows(s) for s in sc]
            for rows, pg in zip(group, p):
                o_ref[rows, :] = _dot(pg, v_ref[...]).astype(o_ref.dtype)

    return _pcall(
        body, (proj, kv, kv), name="xattn_fwd", out_shape=jax.ShapeDtypeStruct((B * S, XA_WIDTH), BF16),
        grid=(B, XA_HEADS), in_specs=[q_spec, k_spec, v_spec], out_specs=o_spec,
        sem=("parallel", "parallel"), comm=comm)


def _xa_bwd(proj, kv, d_o, B, S, M, comm=None):
    CH = _xa_rows(S)
    q_spec, k_spec, v_spec, o_spec = _xa_specs(S, M)
    scale = XA_HEAD_DIM ** -0.5

    def body(q_ref, k_ref, v_ref, do_ref, dq_ref, dk_ref, dv_ref):
        dk_ref[...] = jnp.zeros_like(dk_ref)
        dv_ref[...] = jnp.zeros_like(dv_ref)
        for group in _xa_groups(S, CH):
            q = [q_ref[rows, :] for rows in group]
            do = [do_ref[rows, :] for rows in group]
            p = [_softmax_rows(_dot(qg, k_ref[...], "nt") * scale) for qg in q]
            dp = [_dot(dg, v_ref[...], "nt") for dg in do]
            ds = [pg * (dpg - jnp.sum(dpg * pg, axis=-1, keepdims=True)) * scale for pg, dpg in zip(p, dp)]
            for rows, dsg in zip(group, ds):
                dq_ref[rows, :] = _dot(dsg, k_ref[...]).astype(dq_ref.dtype)
            dk_ref[...] += sum(_dot(dsg, qg, "tn") for dsg, qg in zip(ds, q))
            dv_ref[...] += sum(_dot(pg, dg, "tn") for pg, dg in zip(p, do))

    kv_out = pl.BlockSpec((M, XA_HEAD_DIM), lambda b, h: (b, h))
    return _pcall(
        body, (proj, kv, kv, d_o), name="xattn_bwd",
        out_shape=(jax.ShapeDtypeStruct((B * S, XA_WIDTH), BF16), jax.ShapeDtypeStruct((B * M, XA_WIDTH), F32),
                   jax.ShapeDtypeStruct((B * M, XA_WIDTH), F32)),
        grid=(B, XA_HEADS), in_specs=[q_spec, k_spec, v_spec, o_spec], out_specs=(o_spec, kv_out, kv_out),
        sem=("parallel", "parallel"), comm=comm)


def _gate_specs(tm):
    n = COL_GL // D_MODEL
    return [pl.BlockSpec((tm, D_MODEL), lambda i, j=j: (i, n + j)) for j in range(3)]


def _merge_fwd(proj, ys, tm=512, comm=None):
    T = proj.shape[0]
    tm = _tile(T, tm)
    row = pl.BlockSpec((tm, D_MODEL), lambda i: (i, 0))

    def body(g0, g1, g2, y0, y1, y2, o_ref):
        acc = jax.nn.sigmoid(g0[...]) * y0[...]
        acc = acc + jax.nn.sigmoid(g1[...]) * y1[...]
        acc = acc + jax.nn.sigmoid(g2[...]) * y2[...]
        o_ref[...] = acc.astype(o_ref.dtype)

    return _pcall(
        body, (proj, proj, proj, *ys), name="merge_fwd", out_shape=jax.ShapeDtypeStruct((T, D_MODEL), BF16),
        grid=(T // tm,), in_specs=_gate_specs(tm) + [row] * 3, out_specs=row, sem=("parallel",), comm=comm)


def _merge_bwd(proj, ys, d_merged, tm=512, comm=None):
    T = proj.shape[0]
    tm = _tile(T, tm)
    row = pl.BlockSpec((tm, D_MODEL), lambda i: (i, 0))

    def body(g0, g1, g2, y0, y1, y2, dm_ref, dgl_ref, d0, d1, d2):
        dm = dm_ref[...]
        for j, (g_ref, y_ref, d_ref) in enumerate(((g0, y0, d0), (g1, y1, d1), (g2, y2, d2))):
            sg = jax.nn.sigmoid(g_ref[...])
            d_ref[...] = (dm * sg).astype(d_ref.dtype)
            dgl_ref[:, j * D_MODEL:(j + 1) * D_MODEL] = (dm * y_ref[...] * sg * (1.0 - sg)).astype(dgl_ref.dtype)

    dy = jax.ShapeDtypeStruct((T, D_MODEL), BF16)
    return _pcall(
        body, (proj, proj, proj, *ys, d_merged), name="merge_bwd",
        out_shape=(jax.ShapeDtypeStruct((T, 3 * D_MODEL), BF16), dy, dy, dy), grid=(T // tm,),
        in_specs=_gate_specs(tm) + [row] * 4,
        out_specs=(pl.BlockSpec((tm, 3 * D_MODEL), lambda i: (i, 0)), row, row, row),
        sem=("parallel",), comm=comm)


def _gelu(x):
    return 0.5 * x * (1.0 + jnp.tanh(GELU_C * (x + GELU_A * x * x * x)))


def _gelu_grad(x):
    t = jnp.tanh(GELU_C * (x + GELU_A * x * x * x))
    return 0.5 * (1.0 + t) + 0.5 * x * (1.0 - t * t) * GELU_C * (1.0 + 3.0 * GELU_A * x * x)


GLU_HALO = 16


def _shift_down(x, prev, n):
    last = prev.shape[0]
    r = lax.broadcasted_iota(jnp.int32, (8, 1), 0)
    rolled = pltpu.roll(x, n, axis=0)
    head = rolled[0:8]
    for j in range(n):
        head = jnp.where(r == j, prev[last - n + j:last - n + j + 1, :], head)
    return jnp.concatenate([head, rolled[8:]], axis=0)


def _shift_up(x, nxt, n):
    rows = x.shape[0]
    r = lax.broadcasted_iota(jnp.int32, (8, 1), 0)
    rolled = pltpu.roll(x, rows - n, axis=0)
    tail = rolled[rows - 8:]
    for j in range(n):
        tail = jnp.where(r == 8 - n + j, nxt[j:j + 1, :], tail)
    return jnp.concatenate([rolled[:rows - 8], tail], axis=0)


def _conv(a, prev, cw, cb):
    return _shift_down(a, prev, 2) * cw[0:1, :] + _shift_down(a, prev, 1) * cw[1:2, :] + a * cw[2:3, :] + cb


def _glu_fwd(up, cw, cb, S, tm=1024, comm=None):
    T = up.shape[2]
    tm = _tile(S, tm)
    per_seq = S // tm

    def body(ab_ref, prev_ref, cw_ref, cb_ref, u_ref):
        i = pl.program_id(1)
        prev = jnp.where(i % per_seq == 0, 0.0, prev_ref[...].astype(F32))
        ac = _conv(ab_ref[0].astype(F32), prev, cw_ref[...], cb_ref[...])
        u_ref[...] = (_gelu(ac) * ab_ref[1].astype(F32)).astype(u_ref.dtype)

    before = tm // GLU_HALO
    return _pcall(
        body, (up, up, cw, cb), name="glu_fwd",
        out_shape=jax.ShapeDtypeStruct((FFN_SLABS, T, UP_SHARD), BF16), grid=(FFN_SLABS, T // tm),
        in_specs=[pl.BlockSpec((2, None, tm, UP_SHARD), lambda d, i: (0, d, i, 0)),
                  pl.BlockSpec((None, None, GLU_HALO, UP_SHARD),
                               lambda d, i: (0, d, jnp.maximum(i * before - 1, 0), 0)),
                  pl.BlockSpec((None, 3, UP_SHARD), lambda d, i: (d, 0, 0)),
                  pl.BlockSpec((None, 1, UP_SHARD), lambda d, i: (d, 0, 0))],
        out_specs=pl.BlockSpec((None, tm, UP_SHARD), lambda d, i: (d, i, 0)),
        sem=("parallel", "parallel"), comm=comm)


def _glu_bwd(up, d_u, cw, cb, S, tm=1024, comm=None):
    T = up.shape[2]
    tm = _tile(S, tm)
    per_seq = S // tm
    n_tiles = T // tm
    per_tile = tm // GLU_HALO

    def body(ab_ref, prev_ref, abn_ref, du_ref, dun_ref, cw_ref, cb_ref, dup_ref, dcw_ref, dcb_ref):
        i = pl.program_id(1)

        @pl.when(i == 0)
        def _():
            dcw_ref[...] = jnp.zeros_like(dcw_ref)
            dcb_ref[...] = jnp.zeros_like(dcb_ref)

        cw, cb = cw_ref[...], cb_ref[...]
        a, b = ab_ref[0].astype(F32), ab_ref[1].astype(F32)
        prev = jnp.where(i % per_seq == 0, 0.0, prev_ref[...].astype(F32))
        a2, a1 = _shift_down(a, prev, 2), _shift_down(a, prev, 1)
        ac = a2 * cw[0:1, :] + a1 * cw[1:2, :] + a * cw[2:3, :] + cb
        du = du_ref[...].astype(F32)
        dup_ref[1] = (du * _gelu(ac)).astype(dup_ref.dtype)
        dac = du * b * _gelu_grad(ac)
        dcb_ref[...] += jnp.sum(dac, axis=0, keepdims=True)
        dcw_ref[0:1, :] += jnp.sum(dac * a2, axis=0, keepdims=True)
        dcw_ref[1:2, :] += jnp.sum(dac * a1, axis=0, keepdims=True)
        dcw_ref[2:3, :] += jnp.sum(dac * a, axis=0, keepdims=True)
        acn = _conv(abn_ref[0].astype(F32), a[tm - GLU_HALO:, :], cw, cb)
        dacn = jnp.where(i % per_seq == per_seq - 1, 0.0,
                         dun_ref[...].astype(F32) * abn_ref[1].astype(F32) * _gelu_grad(acn))
        da = dac * cw[2:3, :] + _shift_up(dac, dacn, 1) * cw[1:2, :] + _shift_up(dac, dacn, 2) * cw[0:1, :]
        dup_ref[0] = da.astype(dup_ref.dtype)

    def nxt(i):
        return jnp.minimum((i + 1) * per_tile, T // GLU_HALO - 1)

    return _pcall(
        body, (up, up, up, d_u, d_u, cw, cb), name="glu_bwd",
        out_shape=(jax.ShapeDtypeStruct((2, FFN_SLABS, T, UP_SHARD), BF16),
                   jax.ShapeDtypeStruct((FFN_SLABS, 3, UP_SHARD), F32),
                   jax.ShapeDtypeStruct((FFN_SLABS, 1, UP_SHARD), F32)),
        grid=(FFN_SLABS, n_tiles),
        in_specs=[pl.BlockSpec((2, None, tm, UP_SHARD), lambda d, i: (0, d, i, 0)),
                  pl.BlockSpec((None, None, GLU_HALO, UP_SHARD),
                               lambda d, i: (0, d, jnp.maximum(i * per_tile - 1, 0), 0)),
                  pl.BlockSpec((2, None, GLU_HALO, UP_SHARD), lambda d, i: (0, d, nxt(i), 0)),
                  pl.BlockSpec((None, tm, UP_SHARD), lambda d, i: (d, i, 0)),
                  pl.BlockSpec((None, GLU_HALO, UP_SHARD), lambda d, i: (d, nxt(i), 0)),
                  pl.BlockSpec((None, 3, UP_SHARD), lambda d, i: (d, 0, 0)),
                  pl.BlockSpec((None, 1, UP_SHARD), lambda d, i: (d, 0, 0))],
        out_specs=(pl.BlockSpec((2, None, tm, UP_SHARD), lambda d, i: (0, d, i, 0)),
                   pl.BlockSpec((None, 3, UP_SHARD), lambda d, i: (d, 0, 0)),
                   pl.BlockSpec((None, 1, UP_SHARD), lambda d, i: (d, 0, 0))),
        sem=("parallel", "arbitrary"), comm=comm)


def _mm_up(h2, w_up_t, tm=MM_ROWS, comm=None):
    T, K = h2.shape
    tm = _tile(T, tm)
    return _matmul(
        "mm_up", "nt", h2, w_up_t, jax.ShapeDtypeStruct((N_DEV, T, UP_SHARD), BF16), (N_DEV, T // tm, 1),
        pl.BlockSpec((tm, K), lambda j, i, k: (i, 0)), pl.BlockSpec((None, UP_SHARD, K), lambda j, i, k: (j, 0, 0)),
        pl.BlockSpec((None, tm, UP_SHARD), lambda j, i, k: (j, i, 0)), (tm, UP_SHARD), comm=comm)


def _loss_epilogue(ffn, operands, outputs, first):
    x1_ref, t_ref, g_ref = operands
    dx_ref, dg_ref, loss_ref = outputs

    @pl.when(first)
    def _():
        dg_ref[...] = jnp.zeros_like(dg_ref)
        loss_ref[...] = jnp.zeros_like(loss_ref)

    xv = x1_ref[...] + ffn
    r = lax.rsqrt(jnp.mean(xv * xv, axis=-1, keepdims=True) + EPS)
    xhat = xv * r
    err = xhat * g_ref[...] - t_ref[...]
    loss_ref[...] += (0.5 / D_MODEL) * jnp.sum(err * err)
    dy = err * (1.0 / D_MODEL)
    dg_ref[...] += jnp.sum(dy * xhat, axis=0, keepdims=True)
    dxhat = dy * g_ref[...]
    dx_ref[...] = r * (dxhat - xhat * jnp.mean(dxhat * xhat, axis=-1, keepdims=True))


def _mm_down_loss(u, w_down, x1, target, g_final, tm=MM_ROWS_RES):
    J, T, n = u.shape
    tm = _tile(T, tm)
    row = pl.BlockSpec((tm, D_MODEL), lambda i, d: (i, 0))
    vec = pl.BlockSpec((1, D_MODEL), lambda i, d: (0, 0))
    vec_shape = jax.ShapeDtypeStruct((1, D_MODEL), F32)
    return _matmul(
        "mm_down", "nn", u, w_down, (jax.ShapeDtypeStruct((T, D_MODEL), F32), vec_shape, vec_shape), (T // tm, J),
        pl.BlockSpec((None, tm, n), lambda i, d: (d, i, 0)), pl.BlockSpec((None, n, D_MODEL), lambda i, d: (d, 0, 0)),
        (row, vec, vec), (tm, D_MODEL), [x1, target, g_final], [row, row, vec], epilogue=_loss_epilogue)


def _mm_down_t(dx, w_down, tm=MM_ROWS):
    T = dx.shape[0]
    J, n, _ = w_down.shape
    tm = _tile(T, tm)
    return _matmul(
        "mm_down_t", "nt", dx, w_down, jax.ShapeDtypeStruct((J, T, n), BF16), (J, T // tm, 1),
        pl.BlockSpec((tm, D_MODEL), lambda d, i, k: (i, 0)), pl.BlockSpec((None, n, D_MODEL), lambda d, i, k: (d, 0, 0)),
        pl.BlockSpec((None, tm, n), lambda d, i, k: (d, i, 0)), (tm, n))


def _mm_dw_down(u, dx, tk=MM_TOKENS):
    J, T, n = u.shape
    tk = _tile(T, tk)
    return _matmul(
        "mm_dw_down", "tn", u, dx, jax.ShapeDtypeStruct((J, n, D_MODEL), BF16), (J, T // tk),
        pl.BlockSpec((None, tk, n), lambda d, k: (d, k, 0)), pl.BlockSpec((tk, D_MODEL), lambda d, k: (k, 0)),
        pl.BlockSpec((None, n, D_MODEL), lambda d, k: (d, 0, 0)), (n, D_MODEL))


def _mm_dw_up(h2, d_up, tk=MM_TOKENS):
    T, K = h2.shape
    tk = _tile(T, tk)
    return _matmul(
        "mm_dw_up", "tn", d_up, h2, jax.ShapeDtypeStruct((N_DEV, UP_SHARD, K), BF16), (N_DEV, T // tk),
        pl.BlockSpec((None, tk, UP_SHARD), lambda j, k: (j, k, 0)), pl.BlockSpec((tk, K), lambda j, k: (k, 0)),
        pl.BlockSpec((None, UP_SHARD, K), lambda j, k: (j, 0, 0)), (UP_SHARD, K))


def _mm_up_t(d_up, w_up_t, rms, tm=MM_ROWS_RES, comm=None):
    J, T, n = d_up.shape
    K = w_up_t.shape[2]
    tm = _tile(T, tm)
    fused = _rms_bwd_fused(T, K, tm, rms)
    return _matmul(
        "mm_up_t", "nn", d_up, w_up_t, fused.pop("out_shape"), (T // tm, J),
        pl.BlockSpec((None, tm, n), lambda i, j: (j, i, 0)), pl.BlockSpec((None, n, K), lambda i, j: (j, 0, 0)),
        fused.pop("o_spec"), (tm, K), comm=comm, **fused)


def _cast_shards(shards):
    def body(*refs):
        n = len(refs) // 2
        for src, dst in zip(refs[:n], refs[n:]):
            dst[...] = src[...].astype(dst.dtype)

    return pl.pallas_call(
        body, out_shape=[jax.ShapeDtypeStruct(s.shape, BF16) for s in shards], name="cast_shards",
        compiler_params=pltpu.CompilerParams(vmem_limit_bytes=VMEM_LIMIT),
    )(*shards)


def _adamw(w, g, m, v):
    m = ADAM_B1 * m + (1.0 - ADAM_B1) * g
    v = ADAM_B2 * v + (1.0 - ADAM_B2) * (g * g)
    m_hat = m / (1.0 - ADAM_B1 ** ADAM_STEP)
    v_hat = v / (1.0 - ADAM_B2 ** ADAM_STEP)
    delta = -ADAM_LR * (m_hat / (jnp.sqrt(v_hat) + ADAM_EPS) + ADAM_WD * w)
    return delta, m, v


def _sum_parts(p_ref):
    g = p_ref[0].astype(F32)
    for d in range(1, N_DEV):
        g = g + p_ref[d].astype(F32)
    return g


ADAM_ROWS = 512


def _reduce_adam(name, parts, w, m, v):
    R, Cn = w.shape
    by_rows = sum(p.shape[1] for p in parts) == R and len(parts) > 1
    common = math.gcd(*[p.shape[1] for p in parts])
    tr = max(t for t in range(8, min(common, ADAM_ROWS) + 1, 8) if common % t == 0)
    n_tiles = [p.shape[1] // tr for p in parts]
    first = [sum(n_tiles[:j]) for j in range(len(parts))] if by_rows else [0] * len(parts)

    def body(*refs):
        p_refs = refs[:len(parts)]
        w_ref, m_ref, v_ref, g_out, d_out, m_out, v_out = refs[len(parts):]

        def update(p_ref):
            g = _sum_parts(p_ref)
            delta, m_new, v_new = _adamw(w_ref[...], g, m_ref[...], v_ref[...])
            g_out[...] = g
            d_out[...] = delta
            m_out[...] = m_new
            v_out[...] = v_new

        if len(parts) == 1:
            update(p_refs[0])
        elif by_rows:
            i = pl.program_id(0)
            for p_ref, t0, n in zip(p_refs, first, n_tiles):
                pl.when((i >= t0) & (i < t0 + n))(functools.partial(update, p_ref))
        else:
            c = lax.axis_index("c")
            for side, p_ref in enumerate(p_refs):
                pl.when(c == side)(functools.partial(update, p_ref))

    def part_spec(t0, n):
        return pl.BlockSpec((N_DEV, tr, Cn), lambda i: (0, jnp.clip(i - t0, 0, n - 1), 0))

    row = pl.BlockSpec((tr, Cn), lambda i: (i, 0))
    shape = jax.ShapeDtypeStruct((R, Cn), F32)
    return pl.pallas_call(
        body, out_shape=(shape,) * 4, grid=(R // tr,),
        in_specs=[part_spec(t0, n) for t0, n in zip(first, n_tiles)] + [row, row, row],
        out_specs=(row,) * 4, name=name, compiler_params=_params(("parallel",)),
    )(*parts, w, m, v)


def _small_adam(name, gathered, params):
    n_g, n_p = len(gathered), len(params)

    def body(*refs):
        g_refs = refs[:n_g]
        wmv = refs[n_g:n_g + 3 * n_p]
        sums = refs[n_g + 3 * n_p:2 * n_g + 3 * n_p]
        upd = refs[2 * n_g + 3 * n_p:]
        for j in range(n_g):
            g = _sum_parts(g_refs[j])
            sums[j][...] = g
            if j < n_p:
                w_ref, m_ref, v_ref = wmv[3 * j:3 * j + 3]
                delta, m_new, v_new = _adamw(w_ref[...], g, m_ref[...], v_ref[...])
                upd[3 * j][...] = delta
                upd[3 * j + 1][...] = m_new
                upd[3 * j + 2][...] = v_new

    flat = [a for wmv in params for a in wmv]
    out_shape = [jax.ShapeDtypeStruct(g.shape[1:], F32) for g in gathered]
    out_shape += [jax.ShapeDtypeStruct(a.shape, F32) for a in flat]
    res = pl.pallas_call(body, out_shape=out_shape, name=name)(*gathered, *flat)
    return res[:n_g], [tuple(res[n_g + 3 * j:n_g + 3 * j + 3]) for j in range(n_p)]


def _adam_only(name, g, w, m, v):
    def body(g_ref, w_ref, m_ref, v_ref, d_out, m_out, v_out):
        delta, m_new, v_new = _adamw(w_ref[...], g_ref[...], m_ref[...], v_ref[...])
        d_out[...] = delta
        m_out[...] = m_new
        v_out[...] = v_new

    shape = jax.ShapeDtypeStruct(w.shape, F32)
    return pl.pallas_call(body, out_shape=(shape,) * 3, name=name)(g, w, m, v)


def kernel(x, mem, g_mix, w_in, w_pool, pool_scale, w_a, g_ret, b_ret, w_r, g_mem, w_mem_kv, w_c, w_out, g_ffn, w_up, conv_w, conv_b, w_down, g_final, loss_target, m_g_mix, m_w_in, m_w_pool, m_pool_scale, m_w_a, m_g_ret, m_b_ret, m_w_r, m_g_mem, m_w_mem_kv, m_w_c, m_w_out, m_g_ffn, m_w_up, m_conv_w, m_conv_b, m_w_down, m_g_final, v_g_mix, v_w_in, v_w_pool, v_pool_scale, v_w_a, v_g_ret, v_b_ret, v_w_r, v_g_mem, v_w_mem_kv, v_w_c, v_w_out, v_g_ffn, v_w_up, v_conv_w, v_conv_b, v_w_down, v_g_final):
    B, S, _ = x.shape
    M = mem.shape[1]
    T = B * S
    me = _my_index()
    x2d = x.reshape(T, D_MODEL)
    mem2d = mem.reshape(B * M, D_MODEL)
    tgt2d = loss_target.reshape(T, D_MODEL)
    g_final2 = g_final.reshape(1, D_MODEL)

    big = dict(w_in=w_in[0], w_a=w_a[0], w_r=w_r[0], w_mem_kv=w_mem_kv[0], w_c=w_c[0], w_out=w_out[0],
               w_up=w_up[0].T, w_down=w_down[0])
    names = list(big)
    cast = dict(zip(names, _cast_shards([big[n] for n in names])))
    cb = conv_b[0].reshape(FFN_SLABS, 1, UP_SHARD)
    wp = w_pool[0]
    tables = _ret_tables(S)

    h = _rms_fwd("rms_mix", x2d, g_mix)
    early = ("w_a", "w_r", "w_mem_kv", "w_c", "w_out")
    (proj, Win), landed = _mm_in_gather(h, cast["w_in"], comm=_Gather([cast[n] for n in early] + [conv_w[0]]))
    W = dict(zip(early, landed))
    half = D_MODEL // 2
    (yr, ret_states), (Wup_lo,) = _ret_fwd(proj, g_ret, b_ret, tables, B, S,
                                           comm=_Gather([cast["w_up"][:, :half]]))
    cw_full = landed[-1].transpose(1, 0, 2).reshape(3, FFN_HIDDEN)
    cw = cw_full.reshape(3, FFN_SLABS, UP_SHARD).transpose(1, 0, 2)
    Wa = W["w_a"].transpose(1, 0, 2).reshape(POOL_WIDTH, D_MODEL)
    Wc = W["w_c"].transpose(1, 0, 2).reshape(XA_WIDTH, D_MODEL)
    Wr = W["w_r"].reshape(D_MODEL, D_MODEL)
    Wkv = W["w_mem_kv"].reshape(D_MODEL, D_MODEL)
    Wout = W["w_out"].reshape(D_MODEL, D_MODEL)
    ypre = _pool_fwd(proj, wp, pool_scale, B, S)
    y_pool = _mm_rows("mm_a", ypre, Wa, BF16)
    y_ret = _mm_rows("mm_r", yr, Wr, BF16)
    mem_n = _rms_fwd("rms_mem", mem2d, g_mem)
    kv = _mm_rows("mm_kv", mem_n, Wkv)
    o_mem = _xa_fwd(proj, kv, B, S, M)[0]
    y_mem = _mm_rows("mm_c", o_mem, Wc, BF16)
    ys = (y_pool, y_ret, y_mem)
    merged, (Wup_hi,) = _merge_fwd(proj, ys, comm=_Gather([cast["w_up"][:, half:]]))
    Wup = jnp.concatenate([Wup_lo, Wup_hi], axis=2)
    x1, h2 = _mm_residual_rms("mm_out", merged, Wout, x2d, g_ffn)
    up, (Wdown,) = _mm_up(h2, Wup, comm=_Gather([cast["w_down"]]))
    up = up.reshape(2, FFN_SLABS, T, UP_SHARD)
    Wdown = Wdown.reshape(FFN_SLABS, UP_SHARD, D_MODEL)
    u = _glu_fwd(up, cw, cb, S)[0]

    dx2, dg_final, loss_part = _mm_down_loss(u, Wdown, x1, tgt2d, g_final2)
    received = {}
    d_u = _mm_down_t(dx2, Wdown)
    dW_down = _mm_dw_down(u, dx2)
    (d_up, d_cw, d_cb), (received["w_down"],) = _glu_bwd(
        up, d_u, cw, cb, S, comm=_Exchange([dW_down.reshape(N_DEV, -1, D_MODEL)]))
    d_up = d_up.reshape(N_DEV, T, UP_SHARD)
    dW_up = _mm_dw_up(h2, d_up)
    (dx1, dg_ffn), (up_c0,) = _mm_up_t(d_up, Wup, (x1, g_ffn, dx2), comm=_ExchangeTo([dW_up], 0))
    d_merged = _mm_rows("mm_out_t", dx1, Wout, kind="nt")
    dW_out = _mm_tn("mm_dw_out", merged, dx1, BF16)
    (d_gl, d_y_pool, d_y_ret, d_y_mem), (up_c1,) = _merge_bwd(proj, ys, d_merged, comm=_ExchangeTo([dW_up], 1))
    received["w_up"] = [up_c0, up_c1]
    dW_c = _mm_tn("mm_dw_c", o_mem, d_y_mem, BF16)
    d_o_mem = _mm_rows("mm_c_t", d_y_mem, Wc, kind="nt")
    (d_qx, d_kmem, d_vmem), (received["w_out"],) = _xa_bwd(
        proj, kv, d_o_mem, B, S, M, comm=_Exchange([dW_out.reshape(N_DEV, -1, D_MODEL)]))
    d_kv = jnp.concatenate([d_kmem, d_vmem], axis=1)
    dW_kv = _mm_tn("mm_dw_kv", mem_n, d_kv, BF16)
    d_mem_n = _mm_rows("mm_kv_t", d_kv, Wkv, kind="nt")
    dg_mem = _rms_bwd("rms_mem_bwd", mem2d, g_mem, d_mem_n, None)
    dW_a = _mm_tn("mm_dw_a", ypre, d_y_pool, BF16)
    d_ypre = _mm_rows("mm_a_t", d_y_pool, Wa, kind="nt")
    (d_hp, dw_pool, d_scale), (received["w_a"],) = _pool_bwd(
        proj, d_ypre, wp, pool_scale, B, S,
        comm=_Exchange([dW_a.reshape(POOL_WIDTH, N_DEV, -1).transpose(1, 0, 2)]))
    dW_r = _mm_tn("mm_dw_r", yr, d_y_ret, BF16)
    d_yr = _mm_rows("mm_r_t", d_y_ret, Wr, kind="nt")
    (d_q, d_k, d_v, d_gr, dg_ret, db_ret), landed = _ret_bwd(
        proj, ret_states, d_yr, g_ret, b_ret, tables, B, S,
        comm=_Exchange([dW_r.reshape(N_DEV, -1, D_MODEL), dW_c.reshape(XA_WIDTH, N_DEV, -1).transpose(1, 0, 2),
                        dW_kv.reshape(N_DEV, -1, D_MODEL)]))
    received["w_r"], received["w_c"], received["w_mem_kv"] = landed
    small_names = ["w_pool", "pool_scale", "g_ret", "b_ret", "g_mem", "g_ffn", "conv_b", "g_final"]
    small_grads = [dw_pool, d_scale, dg_ret, db_ret, dg_mem, dg_ffn, d_cb.reshape(1, FFN_HIDDEN), dg_final,
                   d_cw.transpose(1, 0, 2).reshape(3, FFN_HIDDEN), loss_part]
    d_proj = jnp.concatenate([d_hp, d_q, d_k, d_v, d_gr, d_qx, d_gl], axis=1)
    dW_in0, small_all = _mm_tn_slab("mm_dw_in0", h[:, :W_IN_FIRST_ROWS], d_proj, IN_SHARD, BF16,
                                    comm=_Exchange([], whole=small_grads))
    dW_in1, (in0,) = _mm_tn_slab("mm_dw_in1", h[:, W_IN_FIRST_ROWS:], d_proj, IN_SHARD, BF16,
                                 comm=_Exchange([dW_in0]))
    (grad_x, dg_mix), (in1,) = _mm_cols_slab_t("mm_in_t", d_proj, Win, (x2d, g_mix, dx1), comm=_Exchange([dW_in1]))
    received["w_in"] = [in0, in1]
    (g_mix_all,) = _comm_call("gather_g_mix", _Exchange([], whole=[dg_mix]))

    args = dict(g_mix=g_mix, w_in=w_in, w_pool=w_pool, pool_scale=pool_scale, w_a=w_a, g_ret=g_ret, b_ret=b_ret,
                w_r=w_r, g_mem=g_mem, w_mem_kv=w_mem_kv, w_c=w_c, w_out=w_out, g_ffn=g_ffn, w_up=w_up,
                conv_w=conv_w, conv_b=conv_b, w_down=w_down, g_final=g_final)
    m_in = dict(g_mix=m_g_mix, w_in=m_w_in, w_pool=m_w_pool, pool_scale=m_pool_scale, w_a=m_w_a, g_ret=m_g_ret,
                b_ret=m_b_ret, w_r=m_w_r, g_mem=m_g_mem, w_mem_kv=m_w_mem_kv, w_c=m_w_c, w_out=m_w_out,
                g_ffn=m_g_ffn, w_up=m_w_up, conv_w=m_conv_w, conv_b=m_conv_b, w_down=m_w_down, g_final=m_g_final)
    v_in = dict(g_mix=v_g_mix, w_in=v_w_in, w_pool=v_w_pool, pool_scale=v_pool_scale, w_a=v_w_a, g_ret=v_g_ret,
                b_ret=v_b_ret, w_r=v_w_r, g_mem=v_g_mem, w_mem_kv=v_w_mem_kv, w_c=v_w_c, w_out=v_w_out,
                g_ffn=v_g_ffn, w_up=v_w_up, conv_w=v_conv_w, conv_b=v_conv_b, w_down=v_w_down, g_final=v_g_final)

    grads, deltas, new_m, new_v = {}, {}, {}, {}
    for n in names:
        parts = received[n] if isinstance(received[n], list) else [received[n]]
        flip = (lambda a: a.T) if n == "w_up" else (lambda a: a)
        outs = _reduce_adam("adam_" + n, parts, big[n], flip(m_in[n][0]), flip(v_in[n][0]))
        for store, val in zip((grads, deltas, new_m, new_v), outs):
            store[n] = flip(val)[None]

    def as_small(a):
        return a.reshape(a.shape[-3:]) if a.ndim > 2 else a.reshape(1, -1)

    def small_update(call_name, param_names, gathered):
        params = [tuple(as_small(d[n]) for d in (args, m_in, v_in)) for n in param_names]
        sums, updates = _small_adam(call_name, gathered, params)
        for n, g, (d_, m_, v_) in zip(param_names, sums, updates):
            shape = args[n].shape
            grads[n], deltas[n], new_m[n], new_v[n] = (a.reshape(shape) for a in (g, d_, m_, v_))
        return sums[len(param_names):]

    g_cw_full, loss_row = small_update("adam_small", small_names, small_all)
    loss = loss_row[0, 0]
    small_update("adam_g_mix", ["g_mix"], [g_mix_all])

    shard_cols = FFN_HIDDEN // N_DEV
    g_cw = lax.dynamic_slice_in_dim(g_cw_full, me * shard_cols, shard_cols, axis=1)
    d_, m_, v_ = _adam_only("adam_conv_w", g_cw, conv_w[0], m_conv_w[0], v_conv_w[0])
    grads["conv_w"], deltas["conv_w"], new_m["conv_w"], new_v["conv_w"] = g_cw[None], d_[None], m_[None], v_[None]

    order = ["g_mix", "w_in", "w_pool", "pool_scale", "w_a", "g_ret", "b_ret", "w_r", "g_mem", "w_mem_kv", "w_c",
             "w_out", "g_ffn", "w_up", "conv_w", "conv_b", "w_down", "g_final"]
    return (loss, grad_x.reshape(B, S, D_MODEL), *[grads[n] for n in order], *[deltas[n] for n in order],
            *[new_m[n] for n in order], *[new_v[n] for n in order])
```

```python
import functools
import math

import jax
import jax.numpy as jnp
from jax import lax
from jax.experimental import pallas as pl
from jax.experimental.pallas import tpu as pltpu

F32 = jnp.float32
BF16 = jnp.bfloat16

N_DEV = 8
D_MODEL = 1024
POOL_WINDOWS = (2, 4, 8, 16)
POOL_GROUP_DIM = 128
POOL_WIDTH = 512
POOL_HALO = 16
RET_HEADS = 4
RET_QK_DIM = 128
RET_V_DIM = 256
RET_CHUNK = 128
ROPE_BASE = 10000.0
XA_HEADS = 4
XA_HEAD_DIM = 128
XA_WIDTH = 512
IN_WIDTH = 7168
IN_SHARD = IN_WIDTH // N_DEV
FFN_HIDDEN = 2816
UP_SHARD = 2 * FFN_HIDDEN // N_DEV
FFN_SLABS = FFN_HIDDEN // UP_SHARD
EPS = 1e-6
ADAM_LR = 0.001
ADAM_B1 = 0.9
ADAM_B2 = 0.999
ADAM_EPS = 1e-08
ADAM_WD = 0.01
ADAM_STEP = 10
GELU_C = math.sqrt(2.0 / math.pi)
GELU_A = 0.044715
VMEM_LIMIT = 56 * 1024 * 1024
MM_ROWS = 2048
MM_ROWS_RES = 1024
MM_TOKENS = 2048
W_IN_FIRST_ROWS = 384
MESH = pl.DeviceIdType.MESH

COL_Q, COL_K, COL_V, COL_GR, COL_QX, COL_GL = 512, 1024, 1536, 2560, 3584, 4096

_DIMS = {
    "nn": (((1,), (0,)), ((), ())),
    "nt": (((1,), (1,)), ((), ())),
    "tn": (((0,), (0,)), ((), ())),
}


def _dot(a, b, kind="nn"):
    return lax.dot_general(a.astype(BF16), b.astype(BF16), _DIMS[kind], preferred_element_type=F32)


def _params(sem, vmem=VMEM_LIMIT):
    return pltpu.CompilerParams(dimension_semantics=sem, vmem_limit_bytes=vmem)


def _tile(n, pref):
    t = min(n, pref)
    while n % t:
        t //= 2
    return t


def _mesh_pos():
    return lax.axis_index("x"), lax.axis_index("y"), lax.axis_index("c")


def _dev_index(x, y, c):
    return 4 * x + 2 * y + c


def _my_index():
    return _dev_index(*_mesh_pos())


def _remote(src, dst, send_sems, recv_sems, s, to):
    return pltpu.make_async_remote_copy(src_ref=src, dst_ref=dst, send_sem=send_sems.at[s], recv_sem=recv_sems.at[s],
                                        device_id=to, device_id_type=MESH)


class _Gather:
    def __init__(self, shards):
        self.inputs = list(shards)
        self.out_shapes = [jax.ShapeDtypeStruct((N_DEV,) + s.shape, s.dtype) for s in shards]
        n = len(shards)
        self.sem_shapes = [pltpu.SemaphoreType.DMA((7 * n,)), pltpu.SemaphoreType.DMA((7 * n,)),
                           pltpu.SemaphoreType.DMA((n,))]

    def _places(self):
        x, y, c = _mesh_pos()
        return (x, y, c), (x, y, 1 - c), [(1 - x, y), (x, 1 - y), (1 - x, 1 - y)]

    def _local(self, src, dst, sems):
        me = _my_index()
        return [pltpu.make_async_copy(src[w], dst[w].at[me], sems[2].at[w]) for w in range(len(src))]

    def start(self, src, dst, sems):
        me, sib, chips = self._places()
        for cp in self._local(src, dst, sems):
            cp.start()
        for w in range(len(src)):
            land = dst[w].at[_dev_index(*me)]
            _remote(src[w], land, sems[0], sems[1], 7 * w, sib).start()
            for j, chip in enumerate(chips):
                _remote(src[w], land, sems[0], sems[1], 7 * w + 1 + j, (*chip, me[2])).start()

    def middle(self, src, dst, sems):
        me, sib, chips = self._places()
        for j, chip in enumerate(chips):
            for w in range(len(src)):
                block = dst[w].at[_dev_index(*chip, me[2])]
                _remote(src[w], block, sems[0], sems[1], 7 * w + 1 + j, me).wait_recv()
                _remote(block, block, sems[0], sems[1], 7 * w + 4 + j, sib).start()

    def finish(self, src, dst, sems):
        me, sib, chips = self._places()
        n = len(src)
        for w in range(n):
            _remote(src[w], dst[w].at[_dev_index(*sib)], sems[0], sems[1], 7 * w, me).wait_recv()
            for j, chip in enumerate(chips):
                block = dst[w].at[_dev_index(*chip, sib[2])]
                _remote(block, block, sems[0], sems[1], 7 * w + 4 + j, me).wait_recv()
            for k in range(7):
                _remote(src[w], dst[w].at[0], sems[0], sems[1], 7 * w + k, me).wait_send()
        for cp in self._local(src, dst, sems):
            cp.wait()


class _Exchange:
    def __init__(self, partials, whole=()):
        self.n_part = len(partials)
        self.inputs = list(partials) + list(whole)
        self.out_shapes = [jax.ShapeDtypeStruct(p.shape, p.dtype) for p in partials]
        self.out_shapes += [jax.ShapeDtypeStruct((N_DEV,) + a.shape, a.dtype) for a in whole]
        n = len(self.inputs)
        self.sem_shapes = [pltpu.SemaphoreType.DMA((7 * n,)), pltpu.SemaphoreType.DMA((7 * n,)),
                           pltpu.SemaphoreType.DMA((n,))]

    def _peer(self, k):
        x, y, c = _mesh_pos()
        p = (x ^ ((k >> 2) & 1), y ^ ((k >> 1) & 1), c ^ (k & 1))
        return p, _dev_index(*p)

    def _source(self, src, w, slot):
        return src[w].at[slot] if w < self.n_part else src[w]

    def _local(self, src, dst, sems):
        me = _my_index()
        return [pltpu.make_async_copy(self._source(src, w, me), dst[w].at[me], sems[2].at[w])
                for w in range(len(src))]

    def start(self, src, dst, sems):
        me = _my_index()
        for cp in self._local(src, dst, sems):
            cp.start()
        for k in range(1, N_DEV):
            peer, peer_idx = self._peer(k)
            for w in range(len(src)):
                _remote(self._source(src, w, peer_idx), dst[w].at[me], sems[0], sems[1], 7 * w + k - 1, peer).start()

    def finish(self, src, dst, sems):
        for k in range(1, N_DEV):
            peer, peer_idx = self._peer(k)
            for w in range(len(src)):
                cp = _remote(self._source(src, w, peer_idx), dst[w].at[peer_idx], sems[0], sems[1], 7 * w + k - 1, peer)
                cp.wait_send()
                cp.wait_recv()
        for cp in self._local(src, dst, sems):
            cp.wait()


class _ExchangeTo:
    def __init__(self, partials, side):
        self.side = side
        self.inputs = list(partials)
        self.out_shapes = [jax.ShapeDtypeStruct(p.shape, p.dtype) for p in partials]
        n = len(partials)
        self.sem_shapes = [pltpu.SemaphoreType.DMA((7 * n,)), pltpu.SemaphoreType.DMA((7 * n,)),
                           pltpu.SemaphoreType.DMA((n,))]

    def _copies(self, src, dst, sems):
        x, y, c = _mesh_pos()
        me = _dev_index(x, y, c)
        receives = c == self.side
        remote = []
        for k in range(1, N_DEV):
            kx, ky, kc = (k >> 2) & 1, (k >> 1) & 1, k & 1
            peer = (x ^ kx, y ^ ky, c ^ kc)
            peer_idx = _dev_index(*peer)
            sends = c == (self.side ^ kc)
            for w in range(len(src)):
                slab = src[w].at[peer_idx]
                s = 7 * w + k - 1
                remote.append((sends, _remote(slab, dst[w].at[me], sems[0], sems[1], s, peer),
                               _remote(slab, dst[w].at[peer_idx], sems[0], sems[1], s, peer)))
        local = [pltpu.make_async_copy(src[w].at[me], dst[w].at[me], sems[2].at[w]) for w in range(len(src))]
        return receives, remote, local

    def start(self, src, dst, sems):
        receives, remote, local = self._copies(src, dst, sems)

        @pl.when(receives)
        def _():
            for cp in local:
                cp.start()

        for sends, send, _ in remote:
            pl.when(sends)(send.start)

    def finish(self, src, dst, sems):
        receives, remote, local = self._copies(src, dst, sems)
        for sends, send, arrive in remote:
            pl.when(sends)(send.wait_send)
            pl.when(receives)(arrive.wait_recv)

        @pl.when(receives)
        def _():
            for cp in local:
                cp.wait()


def _pcall(body, args, *, name, out_shape, grid, in_specs, out_specs, scratch_shapes=(), sem=None, comm=None):
    single = not isinstance(out_shape, (tuple, list))
    outs = [out_shape] if single else list(out_shape)
    ospecs = [out_specs] if single else list(out_specs)
    n_in, n_out, n_scr = len(args), len(outs), len(scratch_shapes)

    def pick(res):
        return res[0] if single else tuple(res[:n_out])

    if comm is None:
        res = pl.pallas_call(
            body, out_shape=outs, grid=grid, in_specs=list(in_specs), out_specs=ospecs,
            scratch_shapes=list(scratch_shapes), name=name, compiler_params=_params(sem),
        )(*args)
        return pick(res), ()

    nci, nco = len(comm.inputs), len(comm.out_shapes)

    def carrier(*refs):
        at = 0
        parts = []
        for size in (n_in, nci, n_out, nco, n_scr, len(comm.sem_shapes)):
            parts.append(refs[at:at + size])
            at += size
        ins, cins, o, couts, scr, sems = parts
        ids = [pl.program_id(a) for a in range(len(grid))]
        first = functools.reduce(jnp.logical_and, [i == 0 for i in ids])
        last = functools.reduce(jnp.logical_and, [i == g - 1 for i, g in zip(ids, grid)])

        body(*ins, *o, *scr)

        @pl.when(first)
        def _():
            comm.start(cins, couts, sems)

        if hasattr(comm, "middle"):
            steps = math.prod(grid)
            at = functools.reduce(lambda lin, ig: lin * ig[1] + ig[0], zip(ids, grid), 0)

            @pl.when(at == min(steps - 1, (3 * steps) // 4))
            def _():
                comm.middle(cins, couts, sems)

        @pl.when(last)
        def _():
            comm.finish(cins, couts, sems)

    hbm = pl.BlockSpec(memory_space=pltpu.HBM)
    res = pl.pallas_call(
        carrier, out_shape=outs + comm.out_shapes, grid=grid, in_specs=list(in_specs) + [hbm] * nci,
        out_specs=ospecs + [hbm] * nco, scratch_shapes=list(scratch_shapes) + comm.sem_shapes, name=name,
        compiler_params=_params(("arbitrary",) * len(grid)),
    )(*args, *comm.inputs)
    return pick(res), tuple(res[n_out:])


def _comm_call(name, comm):
    def body(*refs):
        nci, nco = len(comm.inputs), len(comm.out_shapes)
        cins, couts, sems = refs[:nci], refs[nci:nci + nco], refs[nci + nco:]
        comm.start(cins, couts, sems)
        if hasattr(comm, "middle"):
            comm.middle(cins, couts, sems)
        comm.finish(cins, couts, sems)

    hbm = pl.BlockSpec(memory_space=pltpu.HBM)
    return pl.pallas_call(
        body, out_shape=comm.out_shapes, in_specs=[hbm] * len(comm.inputs), out_specs=[hbm] * len(comm.out_shapes),
        scratch_shapes=comm.sem_shapes, name=name,
    )(*comm.inputs)


def _matmul(name, kind, a, b, out_shape, grid, a_spec, b_spec, o_spec, acc_shape, res=None, res_spec=None,
            comm=None, epilogue=None):
    nk = grid[-1]
    if epilogue is None:
        extra, extra_specs = ([res], [res_spec]) if res is not None else ([], [])
        n_out = 1
    else:
        extra, extra_specs, n_out = list(res), list(res_spec), len(out_shape)
    n_in = 2 + len(extra)

    def body(*refs):
        a_ref, b_ref = refs[0], refs[1]
        extra_refs, out_refs = refs[2:n_in], refs[n_in:n_in + n_out]

        def prod():
            return _dot(a_ref[...], b_ref[...], kind)

        def finish(acc):
            if epilogue is not None:
                ids = [pl.program_id(ax) for ax in range(len(grid) - 1)]
                first = functools.reduce(jnp.logical_and, [i == 0 for i in ids]) if ids else True
                epilogue(acc, extra_refs, out_refs, first)
                return
            if extra_refs:
                acc = acc + extra_refs[0][...]
            out_refs[0][...] = acc.astype(out_refs[0].dtype)

        if nk == 1:
            finish(prod())
        else:
            acc_ref = refs[n_in + n_out]
            k = pl.program_id(len(grid) - 1)

            @pl.when(k == 0)
            def _():
                acc_ref[...] = prod()

            @pl.when(k > 0)
            def _():
                acc_ref[...] += prod()

            @pl.when(k == nk - 1)
            def _():
                finish(acc_ref[...])

    in_specs = [a_spec, b_spec] + extra_specs
    args = (a, b, *extra)
    scratch = [pltpu.VMEM(acc_shape, F32)] if nk > 1 else []
    sem = ("arbitrary",) * len(grid) if epilogue is not None else ("parallel",) * (len(grid) - 1) + ("arbitrary",)
    out, landed = _pcall(body, args, name=name, out_shape=out_shape, grid=grid, in_specs=in_specs,
                         out_specs=o_spec, scratch_shapes=scratch, sem=sem, comm=comm)
    return out if comm is None else (out, landed)


def _mm_rows(name, a, w, out_dtype=F32, res=None, kind="nn", tm=MM_ROWS, comm=None):
    M, K = a.shape
    N = w.shape[1] if kind == "nn" else w.shape[0]
    tm = _tile(M, tm)
    res_spec = pl.BlockSpec((tm, N), lambda i, k: (i, 0)) if res is not None else None
    return _matmul(
        name, kind, a, w, jax.ShapeDtypeStruct((M, N), out_dtype), (M // tm, 1),
        pl.BlockSpec((tm, K), lambda i, k: (i, 0)), pl.BlockSpec(w.shape, lambda i, k: (0, 0)),
        pl.BlockSpec((tm, N), lambda i, k: (i, 0)), (tm, N), res, res_spec, comm)


def _mm_tn(name, a, b, out_dtype=F32, tk=MM_TOKENS, comm=None):
    T, M = a.shape
    N = b.shape[1]
    tk = _tile(T, tk)
    return _matmul(
        name, "tn", a, b, jax.ShapeDtypeStruct((M, N), out_dtype), (1, T // tk),
        pl.BlockSpec((tk, M), lambda i, k: (k, 0)), pl.BlockSpec((tk, N), lambda i, k: (k, 0)),
        pl.BlockSpec((M, N), lambda i, k: (0, 0)), (M, N), comm=comm)


def _mm_in_gather(h, shard, tm=MM_ROWS, comm=None):
    T, K = h.shape
    n = shard.shape[1]
    tm = _tile(T, tm)
    n_tiles = T // tm
    pair_of_chip_step = {4: 1, 2: 2, 6: 3}

    def slab_of(s):
        x, y, c = _mesh_pos()
        return _dev_index(x ^ ((s >> 2) & 1), y ^ ((s >> 1) & 1), c ^ (s & 1))

    def body(h_ref, shard_ref, proj_ref, win_ref, wbuf, slot_sems, send_sems, recv_sems, local_sem):
        s, i = pl.program_id(0), pl.program_id(1)
        x, y, c = _mesh_pos()
        me, sib = (x, y, c), (x, y, 1 - c)

        def slot_copy(step):
            src = shard_ref if step == 0 else win_ref.at[slab_of(step)]
            return pltpu.make_async_copy(src, wbuf.at[step % 2], slot_sems.at[step % 2])

        def fetch(step):
            if step >= 1:
                block = win_ref.at[slab_of(step)]
                if step == 1:
                    pair = 0
                elif step % 2 == 0:
                    pair = pair_of_chip_step[step]
                else:
                    pair = 3 + pair_of_chip_step[step - 1]
                _remote(block, block, send_sems, recv_sems, pair, me).wait_recv()
                if step % 2 == 0:
                    _remote(block, block, send_sems, recv_sems, 3 + pair, sib).start()
            slot_copy(step).start()

        @pl.when((s == 0) & (i == 0))
        def _():
            land = win_ref.at[_dev_index(*me)]
            pltpu.make_async_copy(shard_ref, land, local_sem).start()
            _remote(shard_ref, land, send_sems, recv_sems, 0, sib).start()
            for step, pair in pair_of_chip_step.items():
                peer = (x ^ ((step >> 2) & 1), y ^ ((step >> 1) & 1), c)
                _remote(shard_ref, land, send_sems, recv_sems, pair, peer).start()
            fetch(0)

        for step in range(N_DEV):
            @pl.when((s == step) & (i == 0))
            def _():
                slot_copy(step).wait()

            if step + 1 < N_DEV:
                @pl.when((s == step) & (i == n_tiles - 1))
                def _():
                    fetch(step + 1)

        proj_ref[...] = _dot(h_ref[...], wbuf[s % 2])

        @pl.when((s == N_DEV - 1) & (i == n_tiles - 1))
        def _():
            for pair in range(7):
                _remote(shard_ref, win_ref.at[0], send_sems, recv_sems, pair, me).wait_send()
            pltpu.make_async_copy(shard_ref, win_ref.at[_dev_index(*me)], local_sem).wait()

    hbm = pl.BlockSpec(memory_space=pltpu.HBM)
    return _pcall(
        body, (h, shard), name="mm_in",
        out_shape=(jax.ShapeDtypeStruct((T, N_DEV * n), F32), jax.ShapeDtypeStruct((N_DEV, K, n), shard.dtype)),
        grid=(N_DEV, n_tiles), in_specs=[pl.BlockSpec((tm, K), lambda s, i: (i, 0)), hbm],
        out_specs=(pl.BlockSpec((tm, n), lambda s, i: (i, slab_of(s))), hbm),
        scratch_shapes=[pltpu.VMEM((2, K, n), shard.dtype), pltpu.SemaphoreType.DMA((2,)),
                        pltpu.SemaphoreType.DMA((7,)), pltpu.SemaphoreType.DMA((7,)), pltpu.SemaphoreType.DMA],
        sem=("arbitrary", "arbitrary"), comm=comm)


def _rms_bwd_epilogue(dh, operands, outputs, first):
    x_ref, g_ref, dres_ref = operands
    dx_ref, dg_ref = outputs
    xv = x_ref[...]
    r = lax.rsqrt(jnp.mean(xv * xv, axis=-1, keepdims=True) + EPS)
    xhat = xv * r

    @pl.when(first)
    def _():
        dg_ref[...] = jnp.zeros_like(dg_ref)

    dg_ref[...] += jnp.sum(dh * xhat, axis=0, keepdims=True)
    dxhat = dh * g_ref[...]
    dx_ref[...] = dres_ref[...] + r * (dxhat - xhat * jnp.mean(dxhat * xhat, axis=-1, keepdims=True))


def _rms_bwd_fused(M, K, tm, rms):
    row = pl.BlockSpec((tm, K), lambda i, j: (i, 0))
    vec = pl.BlockSpec((1, K), lambda i, j: (0, 0))
    x, g, dres = rms
    return dict(res=[x, g, dres], res_spec=[row, vec, row], epilogue=_rms_bwd_epilogue,
                out_shape=(jax.ShapeDtypeStruct((M, K), F32), jax.ShapeDtypeStruct((1, K), F32)), o_spec=(row, vec))


def _mm_cols_slab_t(name, a, w_slabs, rms, tm=MM_ROWS_RES, comm=None):
    M = a.shape[0]
    J, K, n = w_slabs.shape
    tm = _tile(M, tm)
    fused = _rms_bwd_fused(M, K, tm, rms)
    return _matmul(
        name, "nt", a, w_slabs, fused.pop("out_shape"), (M // tm, J),
        pl.BlockSpec((tm, n), lambda i, j: (i, j)), pl.BlockSpec((None, K, n), lambda i, j: (j, 0, 0)),
        fused.pop("o_spec"), (tm, K), comm=comm, **fused)


def _mm_tn_slab(name, a, b, n, out_dtype=F32, tk=MM_TOKENS, comm=None):
    T, M = a.shape
    J = b.shape[1] // n
    tk = _tile(T, tk)
    return _matmul(
        name, "tn", a, b, jax.ShapeDtypeStruct((J, M, n), out_dtype), (J, T // tk),
        pl.BlockSpec((tk, M), lambda j, k: (k, 0)), pl.BlockSpec((tk, n), lambda j, k: (k, j)),
        pl.BlockSpec((None, M, n), lambda j, k: (j, 0, 0)), (M, n), comm=comm)


def _rms_fwd(name, x, g, tm=512):
    T, Dm = x.shape
    tm = _tile(T, tm)

    def body(x_ref, g_ref, h_ref):
        xv = x_ref[...]
        r = lax.rsqrt(jnp.mean(xv * xv, axis=-1, keepdims=True) + EPS)
        h_ref[...] = (xv * r * g_ref[...]).astype(h_ref.dtype)

    return pl.pallas_call(
        body, out_shape=jax.ShapeDtypeStruct((T, Dm), BF16), grid=(T // tm,),
        in_specs=[pl.BlockSpec((tm, Dm), lambda i: (i, 0)), pl.BlockSpec((1, Dm), lambda i: (0, 0))],
        out_specs=pl.BlockSpec((tm, Dm), lambda i: (i, 0)), name=name, compiler_params=_params(("parallel",)),
    )(x, g)


def _rms_bwd(name, x, g, dh, dres, tm=512):
    T, Dm = x.shape
    tm = _tile(T, tm)
    want_dx = dres is not None

    def body(*refs):
        if want_dx:
            x_ref, g_ref, dh_ref, dres_ref, dx_ref, dg_ref = refs
        else:
            x_ref, g_ref, dh_ref, dg_ref = refs
        xv = x_ref[...]
        r = lax.rsqrt(jnp.mean(xv * xv, axis=-1, keepdims=True) + EPS)
        xhat = xv * r
        dhv = dh_ref[...]

        @pl.when(pl.program_id(0) == 0)
        def _():
            dg_ref[...] = jnp.zeros_like(dg_ref)

        dg_ref[...] += jnp.sum(dhv * xhat, axis=0, keepdims=True)
        if want_dx:
            dxhat = dhv * g_ref[...]
            dx_ref[...] = dres_ref[...] + r * (dxhat - xhat * jnp.mean(dxhat * xhat, axis=-1, keepdims=True))

    row = pl.BlockSpec((tm, Dm), lambda i: (i, 0))
    vec = pl.BlockSpec((1, Dm), lambda i: (0, 0))
    if want_dx:
        return pl.pallas_call(
            body, out_shape=(jax.ShapeDtypeStruct((T, Dm), F32), jax.ShapeDtypeStruct((1, Dm), F32)),
            grid=(T // tm,), in_specs=[row, vec, row, row], out_specs=(row, vec), name=name,
            compiler_params=_params(("arbitrary",)),
        )(x, g, dh, dres)
    return pl.pallas_call(
        body, out_shape=jax.ShapeDtypeStruct((1, Dm), F32), grid=(T // tm,), in_specs=[row, vec, row],
        out_specs=vec, name=name, compiler_params=_params(("arbitrary",)),
    )(x, g, dh)


def _pool_rows(S):
    return _tile(S, 256)


def _pool_count(c0, rows, w):
    t = c0 + lax.broadcasted_iota(jnp.int32, (rows, 1), 0)
    return jnp.minimum(t + 1, w).astype(F32)


def _pool_fwd(proj, w_pool, scale, B, S):
    CH = _pool_rows(S)

    def body(hp_ref, wp_ref, sc_ref, o_ref, pad_ref):
        pad_ref[0:POOL_HALO, :] = jnp.zeros((POOL_HALO, POOL_WIDTH), F32)
        pad_ref[POOL_HALO:, :] = hp_ref[...]
        for gi, w in enumerate(POOL_WINDOWS):
            cols = slice(gi * POOL_GROUP_DIM, (gi + 1) * POOL_GROUP_DIM)
            for c in range(S // CH):
                base = POOL_HALO + c * CH
                acc = pad_ref[base:base + CH, cols]
                tok = acc
                for j in range(1, w):
                    acc = acc + pad_ref[base - j:base - j + CH, cols]
                pooled = acc / _pool_count(c * CH, CH, w) - tok
                z = _dot(pooled, wp_ref[gi])
                o_ref[c * CH:(c + 1) * CH, cols] = (z * sc_ref[:, cols]).astype(o_ref.dtype)

    return pl.pallas_call(
        body, out_shape=jax.ShapeDtypeStruct((B * S, POOL_WIDTH), BF16), grid=(B,),
        in_specs=[pl.BlockSpec((S, POOL_WIDTH), lambda b: (b, 0)),
                  pl.BlockSpec(w_pool.shape, lambda b: (0, 0, 0)),
                  pl.BlockSpec((1, POOL_WIDTH), lambda b: (0, 0))],
        out_specs=pl.BlockSpec((S, POOL_WIDTH), lambda b: (b, 0)),
        scratch_shapes=[pltpu.VMEM((S + POOL_HALO, POOL_WIDTH), F32)],
        name="pool_fwd", compiler_params=_params(("parallel",)),
    )(proj, w_pool, scale)


def _pool_bwd(proj, d_ypre, w_pool, scale, B, S, comm=None):
    CH = _pool_rows(S)

    def body(hp_ref, dy_ref, wp_ref, sc_ref, dhp_ref, dwp_ref, dsc_ref, pad_ref, sc_pad_ref, dp_ref):
        @pl.when(pl.program_id(0) == 0)
        def _():
            dwp_ref[...] = jnp.zeros_like(dwp_ref)
            dsc_ref[...] = jnp.zeros_like(dsc_ref)

        pad_ref[0:POOL_HALO, :] = jnp.zeros((POOL_HALO, POOL_WIDTH), F32)
        pad_ref[POOL_HALO:, :] = hp_ref[...]
        sc_pad_ref[S:, :] = jnp.zeros((POOL_HALO, POOL_WIDTH), F32)
        for gi, w in enumerate(POOL_WINDOWS):
            cols = slice(gi * POOL_GROUP_DIM, (gi + 1) * POOL_GROUP_DIM)
            for c in range(S // CH):
                base = POOL_HALO + c * CH
                rows = slice(c * CH, (c + 1) * CH)
                acc = pad_ref[base:base + CH, cols]
                tok = acc
                for j in range(1, w):
                    acc = acc + pad_ref[base - j:base - j + CH, cols]
                cnt = _pool_count(c * CH, CH, w)
                pooled = acc / cnt - tok
                z = _dot(pooled, wp_ref[gi])
                dy = dy_ref[rows, cols]
                dsc_ref[:, cols] += jnp.sum(dy * z, axis=0, keepdims=True)
                dz = dy * sc_ref[:, cols]
                dwp_ref[gi] += _dot(pooled, dz, "tn")
                dpool = _dot(dz, wp_ref[gi], "nt")
                dp_ref[rows, cols] = dpool
                sc_pad_ref[rows, cols] = dpool / cnt
            for c in range(S // CH):
                rows = slice(c * CH, (c + 1) * CH)
                acc = sc_pad_ref[rows, cols]
                for j in range(1, w):
                    acc = acc + sc_pad_ref[c * CH + j:c * CH + j + CH, cols]
                dhp_ref[rows, cols] = (acc - dp_ref[rows, cols]).astype(dhp_ref.dtype)

    seq = pl.BlockSpec((S, POOL_WIDTH), lambda b: (b, 0))
    return _pcall(
        body, (proj, d_ypre, w_pool, scale), name="pool_bwd",
        out_shape=(jax.ShapeDtypeStruct((B * S, POOL_WIDTH), BF16),
                   jax.ShapeDtypeStruct(w_pool.shape, F32), jax.ShapeDtypeStruct((1, POOL_WIDTH), F32)),
        grid=(B,),
        in_specs=[seq, seq, pl.BlockSpec(w_pool.shape, lambda b: (0, 0, 0)),
                  pl.BlockSpec((1, POOL_WIDTH), lambda b: (0, 0))],
        out_specs=(seq, pl.BlockSpec(w_pool.shape, lambda b: (0, 0, 0)),
                   pl.BlockSpec((1, POOL_WIDTH), lambda b: (0, 0))),
        scratch_shapes=[pltpu.VMEM((S + POOL_HALO, POOL_WIDTH), F32),
                        pltpu.VMEM((S + POOL_HALO, POOL_WIDTH), F32),
                        pltpu.VMEM((S, POOL_WIDTH), F32)],
        sem=("arbitrary",), comm=comm)


def _ret_tables(S):
    half = RET_QK_DIM // 2
    inv = ROPE_BASE ** (-jnp.arange(half, dtype=F32) / half)
    ang = jnp.arange(S, dtype=F32)[:, None] * inv[None, :]
    cos, sin = jnp.cos(ang), jnp.sin(ang)
    cos_full = jnp.concatenate([cos, cos], axis=-1)
    sin_signed = jnp.concatenate([-sin, sin], axis=-1)
    C = RET_CHUNK
    lg = jnp.log1p(-jnp.exp2(-5.0 - jnp.arange(RET_HEADS, dtype=F32)))[:, None, None]
    idx = jnp.arange(C, dtype=F32)
    rel = idx[:, None] - idx[None, :]
    decay = jnp.where(rel >= 0, jnp.exp(jnp.maximum(rel, 0.0) * lg), 0.0)
    q_decay = jnp.broadcast_to(jnp.exp((idx + 1.0)[None, :, None] * lg), (RET_HEADS, C, RET_QK_DIM))
    k_decay = jnp.broadcast_to(jnp.exp((C - 1.0 - idx)[None, :, None] * lg), (RET_HEADS, C, RET_QK_DIM))
    c_decay = jnp.broadcast_to(jnp.exp(C * lg), (RET_HEADS, 1, RET_V_DIM))
    return cos_full, sin_signed, decay, q_decay, k_decay, c_decay


def _rope(x, cos_full, sin_signed):
    return x * cos_full + pltpu.roll(x, RET_QK_DIM // 2, axis=1) * sin_signed


def _rope_t(dy, cos_full, sin_signed):
    return dy * cos_full + pltpu.roll(dy * sin_signed, RET_QK_DIM // 2, axis=1)


RET_COLS = 512


def _ret_specs(N, chunk_of):
    C = RET_CHUNK

    def rows(width, col=0):
        return pl.BlockSpec((C, width), lambda b, i: (b * N + chunk_of(i), col))

    def whole(shape):
        return pl.BlockSpec(shape, lambda b, i: (0,) * len(shape))

    wide = RET_HEADS * RET_V_DIM
    return dict(
        q=rows(RET_COLS, COL_Q // RET_COLS), k=rows(RET_COLS, COL_K // RET_COLS),
        v=[rows(RET_COLS, COL_V // RET_COLS + j) for j in range(2)],
        gr=[rows(RET_COLS, COL_GR // RET_COLS + j) for j in range(2)],
        table=pl.BlockSpec((C, RET_QK_DIM), lambda b, i: (chunk_of(i), 0)),
        decay=whole((RET_HEADS, C, C)), qd=whole((RET_HEADS, C, RET_QK_DIM)), kd=whole((RET_HEADS, C, RET_QK_DIM)),
        cd=whole((RET_HEADS, 1, RET_V_DIM)), vec=whole((1, wide)), qk_rows=rows(RET_COLS), v_rows=rows(wide),
        state=pl.BlockSpec((None, None, RET_HEADS, RET_QK_DIM, RET_V_DIM), lambda b, i: (b, chunk_of(i), 0, 0, 0)))


def _head_cols(h):
    pair = slice((h % 2) * RET_V_DIM, (h % 2 + 1) * RET_V_DIM)
    return slice(h * RET_QK_DIM, (h + 1) * RET_QK_DIM), h // 2, pair, slice(h * RET_V_DIM, (h + 1) * RET_V_DIM)


def _group_norm(o):
    mu = jnp.mean(o, axis=-1, keepdims=True)
    oc = o - mu
    rstd = lax.rsqrt(jnp.mean(oc * oc, axis=-1, keepdims=True) + EPS)
    return oc * rstd, rstd


def _ret_fwd(proj, g_ret, b_ret, tables, B, S, comm=None):
    N = S // RET_CHUNK
    cos_t, sin_t, decay, q_decay, k_decay, c_decay = tables
    sp = _ret_specs(N, lambda i: i)

    def body(q_ref, k_ref, v0_ref, v1_ref, gr0_ref, gr1_ref, cos_ref, sin_ref, dec_ref, qd_ref, kd_ref, cd_ref,
             g_ref, b_ref, y_ref, rs_ref, r_ref):
        @pl.when(pl.program_id(1) == 0)
        def _():
            r_ref[...] = jnp.zeros_like(r_ref)

        cs, sn = cos_ref[...], sin_ref[...]
        heads = range(RET_HEADS)
        cols = [_head_cols(h) for h in heads]
        q = [_rope(q_ref[:, cols[h][0]], cs, sn) for h in heads]
        k = [_rope(k_ref[:, cols[h][0]], cs, sn) * (RET_QK_DIM ** -0.5) for h in heads]
        v = [(v0_ref, v1_ref)[cols[h][1]][:, cols[h][2]] for h in heads]
        R = [r_ref[h] for h in heads]
        s = [_dot(q[h], k[h], "nt") * dec_ref[h] for h in heads]
        o = [_dot(s[h], v[h]) + _dot(q[h] * qd_ref[h], R[h]) for h in heads]
        r_new = [cd_ref[h] * R[h] + _dot(k[h] * kd_ref[h], v[h], "tn") for h in heads]
        for h in heads:
            _, j, pair, wide = cols[h]
            rs_ref[h] = R[h]
            r_ref[h] = r_new[h]
            on, _ = _group_norm(o[h])
            gr = (gr0_ref, gr1_ref)[j][:, pair]
            y_ref[:, wide] = (gr * jax.nn.sigmoid(gr) * (on * g_ref[:, wide] + b_ref[:, wide])).astype(y_ref.dtype)

    state = jax.ShapeDtypeStruct((B, N, RET_HEADS, RET_QK_DIM, RET_V_DIM), F32)
    return _pcall(
        body, (proj,) * 6 + (cos_t, sin_t, decay, q_decay, k_decay, c_decay, g_ret, b_ret),
        name="ret_fwd", out_shape=(jax.ShapeDtypeStruct((B * S, RET_HEADS * RET_V_DIM), BF16), state), grid=(B, N),
        in_specs=[sp["q"], sp["k"], *sp["v"], *sp["gr"], sp["table"], sp["table"], sp["decay"], sp["qd"],
                  sp["kd"], sp["cd"], sp["vec"], sp["vec"]],
        out_specs=(sp["v_rows"], sp["state"]),
        scratch_shapes=[pltpu.VMEM((RET_HEADS, RET_QK_DIM, RET_V_DIM), F32)],
        sem=("parallel", "arbitrary"), comm=comm)


def _ret_bwd(proj, states, d_yr, g_ret, b_ret, tables, B, S, comm=None):
    N = S // RET_CHUNK
    cos_t, sin_t, decay, q_decay, k_decay, c_decay = tables
    sp = _ret_specs(N, lambda i: N - 1 - i)
    qk_scale = RET_QK_DIM ** -0.5

    def body(q_ref, k_ref, v0_ref, v1_ref, gr0_ref, gr1_ref, dy_ref, rs_ref, cos_ref, sin_ref, dec_ref, qd_ref,
             kd_ref, cd_ref, g_ref, b_ref, dq_ref, dk_ref, dv_ref, dgr_ref, dg_ref, db_ref, dr_ref):
        @pl.when((pl.program_id(0) == 0) & (pl.program_id(1) == 0))
        def _():
            dg_ref[...] = jnp.zeros_like(dg_ref)
            db_ref[...] = jnp.zeros_like(db_ref)

        @pl.when(pl.program_id(1) == 0)
        def _():
            dr_ref[...] = jnp.zeros_like(dr_ref)

        cs, sn = cos_ref[...], sin_ref[...]
        heads = range(RET_HEADS)
        cols = [_head_cols(h) for h in heads]
        q = [_rope(q_ref[:, cols[h][0]], cs, sn) for h in heads]
        k = [_rope(k_ref[:, cols[h][0]], cs, sn) * qk_scale for h in heads]
        v = [(v0_ref, v1_ref)[cols[h][1]][:, cols[h][2]] for h in heads]
        s = [_dot(q[h], k[h], "nt") * dec_ref[h] for h in heads]
        o = [_dot(s[h], v[h]) + _dot(q[h] * qd_ref[h], rs_ref[h]) for h in heads]
        do = []
        for h in heads:
            _, j, pair, wide = cols[h]
            on, rstd = _group_norm(o[h])
            g = g_ref[:, wide]
            oaff = on * g + b_ref[:, wide]
            gr = (gr0_ref, gr1_ref)[j][:, pair]
            sg = jax.nn.sigmoid(gr)
            dy = dy_ref[:, wide]
            dgr_ref[:, wide] = (dy * oaff * (sg * (1.0 + gr * (1.0 - sg)))).astype(dgr_ref.dtype)
            doaff = dy * (gr * sg)
            dg_ref[:, wide] += jnp.sum(doaff * on, axis=0, keepdims=True)
            db_ref[:, wide] += jnp.sum(doaff, axis=0, keepdims=True)
            don = doaff * g
            do.append(rstd * (don - jnp.mean(don, axis=-1, keepdims=True)
                              - on * jnp.mean(don * on, axis=-1, keepdims=True)))
        ds = [_dot(do[h], v[h], "nt") * dec_ref[h] for h in heads]
        dq = [_dot(ds[h], k[h]) + qd_ref[h] * _dot(do[h], rs_ref[h], "nt") for h in heads]
        dk = [_dot(ds[h], q[h], "tn") + kd_ref[h] * _dot(v[h], dr_ref[h], "nt") for h in heads]
        dv = [_dot(s[h], do[h], "tn") + _dot(k[h] * kd_ref[h], dr_ref[h]) for h in heads]
        dr = [cd_ref[h] * dr_ref[h] + _dot(q[h] * qd_ref[h], do[h], "tn") for h in heads]
        for h in heads:
            qk, _, _, wide = cols[h]
            dv_ref[:, wide] = dv[h].astype(dv_ref.dtype)
            dr_ref[h] = dr[h]
            dq_ref[:, qk] = _rope_t(dq[h], cs, sn).astype(dq_ref.dtype)
            dk_ref[:, qk] = _rope_t(dk[h] * qk_scale, cs, sn).astype(dk_ref.dtype)

    T = B * S
    qk_shape = jax.ShapeDtypeStruct((T, RET_HEADS * RET_QK_DIM), BF16)
    v_shape = jax.ShapeDtypeStruct((T, RET_HEADS * RET_V_DIM), BF16)
    vec_shape = jax.ShapeDtypeStruct((1, RET_HEADS * RET_V_DIM), F32)
    return _pcall(
        body, (proj,) * 6 + (d_yr, states, cos_t, sin_t, decay, q_decay, k_decay, c_decay, g_ret, b_ret),
        name="ret_bwd", out_shape=(qk_shape, qk_shape, v_shape, v_shape, vec_shape, vec_shape), grid=(B, N),
        in_specs=[sp["q"], sp["k"], *sp["v"], *sp["gr"], sp["v_rows"], sp["state"], sp["table"], sp["table"],
                  sp["decay"], sp["qd"], sp["kd"], sp["cd"], sp["vec"], sp["vec"]],
        out_specs=(sp["qk_rows"], sp["qk_rows"], sp["v_rows"], sp["v_rows"], sp["vec"], sp["vec"]),
        scratch_shapes=[pltpu.VMEM((RET_HEADS, RET_QK_DIM, RET_V_DIM), F32)],
        sem=("arbitrary", "arbitrary"), comm=comm)


def _xa_rows(S):
    return _tile(S, 256)


def _xa_groups(S, rows, size=4):
    chunks = [slice(r, r + rows) for r in range(0, S, rows)]
    return [chunks[g:g + size] for g in range(0, len(chunks), size)]


def _xa_specs(S, M):
    q = pl.BlockSpec((S, XA_HEAD_DIM), lambda b, h: (b, COL_QX // XA_HEAD_DIM + h))
    k = pl.BlockSpec((M, XA_HEAD_DIM), lambda b, h: (b, h))
    v = pl.BlockSpec((M, XA_HEAD_DIM), lambda b, h: (b, XA_HEADS + h))
    o = pl.BlockSpec((S, XA_HEAD_DIM), lambda b, h: (b, h))
    return q, k, v, o


def _softmax_rows(s):
    e = jnp.exp(s - jnp.max(s, axis=-1, keepdims=True))
    return e / jnp.sum(e, axis=-1, keepdims=True)


def _xa_fwd(proj, kv, B, S, M, comm=None):
    CH = _xa_rows(S)
    q_spec, k_spec, v_spec, o_spec = _xa_specs(S, M)

    def body(q_ref, k_ref, v_ref, o_ref):
        for group in _xa_groups(S, CH):
            sc = [_dot(q_ref[rows, :], k_ref[...], "nt") * (XA_HEAD_DIM ** -0.5) for rows in group]
            p = [_softmax_rows(s) for s in sc]
            for rows, pg in zip(group, p):
                o_ref[rows, :] = _dot(pg, v_ref[...]).astype(o_ref.dtype)

    return _pcall(
        body, (proj, kv, kv), name="xattn_fwd", out_shape=jax.ShapeDtypeStruct((B * S, XA_WIDTH), BF16),
        grid=(B, XA_HEADS), in_specs=[q_spec, k_spec, v_spec], out_specs=o_spec,
        sem=("parallel", "parallel"), comm=comm)


def _xa_bwd(proj, kv, d_o, B, S, M, comm=None):
    CH = _xa_rows(S)
    q_spec, k_spec, v_spec, o_spec = _xa_specs(S, M)
    scale = XA_HEAD_DIM ** -0.5

    def body(q_ref, k_ref, v_ref, do_ref, dq_ref, dk_ref, dv_ref):
        dk_ref[...] = jnp.zeros_like(dk_ref)
        dv_ref[...] = jnp.zeros_like(dv_ref)
        for group in _xa_groups(S, CH):
            q = [q_ref[rows, :] for rows in group]
            do = [do_ref[rows, :] for rows in group]
            p = [_softmax_rows(_dot(qg, k_ref[...], "nt") * scale) for qg in q]
            dp = [_dot(dg, v_ref[...], "nt") for dg in do]
            ds = [pg * (dpg - jnp.sum(dpg * pg, axis=-1, keepdims=True)) * scale for pg, dpg in zip(p, dp)]
            for rows, dsg in zip(group, ds):
                dq_ref[rows, :] = _dot(dsg, k_ref[...]).astype(dq_ref.dtype)
            dk_ref[...] += sum(_dot(dsg, qg, "tn") for dsg, qg in zip(ds, q))
            dv_ref[...] += sum(_dot(pg, dg, "tn") for pg, dg in zip(p, do))

    kv_out = pl.BlockSpec((M, XA_HEAD_DIM), lambda b, h: (b, h))
    return _pcall(
        body, (proj, kv, kv, d_o), name="xattn_bwd",
        out_shape=(jax.ShapeDtypeStruct((B * S, XA_WIDTH), BF16), jax.ShapeDtypeStruct((B * M, XA_WIDTH), F32),
                   jax.ShapeDtypeStruct((B * M, XA_WIDTH), F32)),
        grid=(B, XA_HEADS), in_specs=[q_spec, k_spec, v_spec, o_spec], out_specs=(o_spec, kv_out, kv_out),
        sem=("parallel", "parallel"), comm=comm)


def _gate_specs(tm):
    n = COL_GL // D_MODEL
    return [pl.BlockSpec((tm, D_MODEL), lambda i, j=j: (i, n + j)) for j in range(3)]


def _merge_out(proj, ys, w_out, x, g, tm=512, comm=None):
    T = proj.shape[0]
    tm = _tile(T, tm)
    row = pl.BlockSpec((tm, D_MODEL), lambda i: (i, 0))
    vec = pl.BlockSpec((1, D_MODEL), lambda i: (0, 0))

    def body(g0, g1, g2, y0, y1, y2, w_ref, x_ref, gf_ref, m_ref, x1_ref, h_ref):
        acc = jax.nn.sigmoid(g0[...]) * y0[...]
        acc = acc + jax.nn.sigmoid(g1[...]) * y1[...]
        acc = acc + jax.nn.sigmoid(g2[...]) * y2[...]
        merged = acc.astype(m_ref.dtype)
        m_ref[...] = merged
        xv = x_ref[...] + _dot(merged, w_ref[...])
        x1_ref[...] = xv
        r = lax.rsqrt(jnp.mean(xv * xv, axis=-1, keepdims=True) + EPS)
        h_ref[...] = (xv * r * gf_ref[...]).astype(h_ref.dtype)

    bf16_rows = jax.ShapeDtypeStruct((T, D_MODEL), BF16)
    return _pcall(
        body, (proj, proj, proj, *ys, w_out, x, g), name="merge_out",
        out_shape=(bf16_rows, jax.ShapeDtypeStruct((T, D_MODEL), F32), bf16_rows), grid=(T // tm,),
        in_specs=_gate_specs(tm) + [row] * 3 + [pl.BlockSpec(w_out.shape, lambda i: (0, 0)), row, vec],
        out_specs=(row, row, row), sem=("parallel",), comm=comm)


def _merge_bwd(proj, ys, d_merged, tm=512, comm=None):
    T = proj.shape[0]
    tm = _tile(T, tm)
    row = pl.BlockSpec((tm, D_MODEL), lambda i: (i, 0))

    def body(g0, g1, g2, y0, y1, y2, dm_ref, dgl_ref, d0, d1, d2):
        dm = dm_ref[...]
        for j, (g_ref, y_ref, d_ref) in enumerate(((g0, y0, d0), (g1, y1, d1), (g2, y2, d2))):
            sg = jax.nn.sigmoid(g_ref[...])
            d_ref[...] = (dm * sg).astype(d_ref.dtype)
            dgl_ref[:, j * D_MODEL:(j + 1) * D_MODEL] = (dm * y_ref[...] * sg * (1.0 - sg)).astype(dgl_ref.dtype)

    dy = jax.ShapeDtypeStruct((T, D_MODEL), BF16)
    return _pcall(
        body, (proj, proj, proj, *ys, d_merged), name="merge_bwd",
        out_shape=(jax.ShapeDtypeStruct((T, 3 * D_MODEL), BF16), dy, dy, dy), grid=(T // tm,),
        in_specs=_gate_specs(tm) + [row] * 4,
        out_specs=(pl.BlockSpec((tm, 3 * D_MODEL), lambda i: (i, 0)), row, row, row),
        sem=("parallel",), comm=comm)


def _gelu(x):
    return 0.5 * x * (1.0 + jnp.tanh(GELU_C * (x + GELU_A * x * x * x)))


def _gelu_grad(x):
    t = jnp.tanh(GELU_C * (x + GELU_A * x * x * x))
    return 0.5 * (1.0 + t) + 0.5 * x * (1.0 - t * t) * GELU_C * (1.0 + 3.0 * GELU_A * x * x)


GLU_HALO = 16


def _shift_down(x, prev, n):
    last = prev.shape[0]
    r = lax.broadcasted_iota(jnp.int32, (8, 1), 0)
    rolled = pltpu.roll(x, n, axis=0)
    head = rolled[0:8]
    for j in range(n):
        head = jnp.where(r == j, prev[last - n + j:last - n + j + 1, :], head)
    return jnp.concatenate([head, rolled[8:]], axis=0)


def _shift_up(x, nxt, n):
    rows = x.shape[0]
    r = lax.broadcasted_iota(jnp.int32, (8, 1), 0)
    rolled = pltpu.roll(x, rows - n, axis=0)
    tail = rolled[rows - 8:]
    for j in range(n):
        tail = jnp.where(r == 8 - n + j, nxt[j:j + 1, :], tail)
    return jnp.concatenate([rolled[:rows - 8], tail], axis=0)


def _conv(a, prev, cw, cb):
    return _shift_down(a, prev, 2) * cw[0:1, :] + _shift_down(a, prev, 1) * cw[1:2, :] + a * cw[2:3, :] + cb


def _glu_fwd(up, cw, cb, S, tm=1024, comm=None):
    T = up.shape[2]
    tm = _tile(S, tm)
    per_seq = S // tm

    def body(ab_ref, prev_ref, cw_ref, cb_ref, u_ref):
        i = pl.program_id(1)
        prev = jnp.where(i % per_seq == 0, 0.0, prev_ref[...].astype(F32))
        ac = _conv(ab_ref[0].astype(F32), prev, cw_ref[...], cb_ref[...])
        u_ref[...] = (_gelu(ac) * ab_ref[1].astype(F32)).astype(u_ref.dtype)

    before = tm // GLU_HALO
    return _pcall(
        body, (up, up, cw, cb), name="glu_fwd",
        out_shape=jax.ShapeDtypeStruct((FFN_SLABS, T, UP_SHARD), BF16), grid=(FFN_SLABS, T // tm),
        in_specs=[pl.BlockSpec((2, None, tm, UP_SHARD), lambda d, i: (0, d, i, 0)),
                  pl.BlockSpec((None, None, GLU_HALO, UP_SHARD),
                               lambda d, i: (0, d, jnp.maximum(i * before - 1, 0), 0)),
                  pl.BlockSpec((None, 3, UP_SHARD), lambda d, i: (d, 0, 0)),
                  pl.BlockSpec((None, 1, UP_SHARD), lambda d, i: (d, 0, 0))],
        out_specs=pl.BlockSpec((None, tm, UP_SHARD), lambda d, i: (d, i, 0)),
        sem=("parallel", "parallel"), comm=comm)


def _glu_bwd(up, d_u, cw, cb, S, tm=1024, comm=None):
    T = up.shape[2]
    tm = _tile(S, tm)
    per_seq = S // tm
    n_tiles = T // tm
    per_tile = tm // GLU_HALO

    def body(ab_ref, prev_ref, abn_ref, du_ref, dun_ref, cw_ref, cb_ref, dup_ref, dcw_ref, dcb_ref):
        i = pl.program_id(1)

        @pl.when(i == 0)
        def _():
            dcw_ref[...] = jnp.zeros_like(dcw_ref)
            dcb_ref[...] = jnp.zeros_like(dcb_ref)

        cw, cb = cw_ref[...], cb_ref[...]
        a, b = ab_ref[0].astype(F32), ab_ref[1].astype(F32)
        prev = jnp.where(i % per_seq == 0, 0.0, prev_ref[...].astype(F32))
        a2, a1 = _shift_down(a, prev, 2), _shift_down(a, prev, 1)
        ac = a2 * cw[0:1, :] + a1 * cw[1:2, :] + a * cw[2:3, :] + cb
        du = du_ref[...].astype(F32)
        dup_ref[1] = (du * _gelu(ac)).astype(dup_ref.dtype)
        dac = du * b * _gelu_grad(ac)
        dcb_ref[...] += jnp.sum(dac, axis=0, keepdims=True)
        dcw_ref[0:1, :] += jnp.sum(dac * a2, axis=0, keepdims=True)
        dcw_ref[1:2, :] += jnp.sum(dac * a1, axis=0, keepdims=True)
        dcw_ref[2:3, :] += jnp.sum(dac * a, axis=0, keepdims=True)
        acn = _conv(abn_ref[0].astype(F32), a[tm - GLU_HALO:, :], cw, cb)
        dacn = jnp.where(i % per_seq == per_seq - 1, 0.0,
                         dun_ref[...].astype(F32) * abn_ref[1].astype(F32) * _gelu_grad(acn))
        da = dac * cw[2:3, :] + _shift_up(dac, dacn, 1) * cw[1:2, :] + _shift_up(dac, dacn, 2) * cw[0:1, :]
        dup_ref[0] = da.astype(dup_ref.dtype)

    def nxt(i):
        return jnp.minimum((i + 1) * per_tile, T // GLU_HALO - 1)

    return _pcall(
        body, (up, up, up, d_u, d_u, cw, cb), name="glu_bwd",
        out_shape=(jax.ShapeDtypeStruct((2, FFN_SLABS, T, UP_SHARD), BF16),
                   jax.ShapeDtypeStruct((FFN_SLABS, 3, UP_SHARD), F32),
                   jax.ShapeDtypeStruct((FFN_SLABS, 1, UP_SHARD), F32)),
        grid=(FFN_SLABS, n_tiles),
        in_specs=[pl.BlockSpec((2, None, tm, UP_SHARD), lambda d, i: (0, d, i, 0)),
                  pl.BlockSpec((None, None, GLU_HALO, UP_SHARD),
                               lambda d, i: (0, d, jnp.maximum(i * per_tile - 1, 0), 0)),
                  pl.BlockSpec((2, None, GLU_HALO, UP_SHARD), lambda d, i: (0, d, nxt(i), 0)),
                  pl.BlockSpec((None, tm, UP_SHARD), lambda d, i: (d, i, 0)),
                  pl.BlockSpec((None, GLU_HALO, UP_SHARD), lambda d, i: (d, nxt(i), 0)),
                  pl.BlockSpec((None, 3, UP_SHARD), lambda d, i: (d, 0, 0)),
                  pl.BlockSpec((None, 1, UP_SHARD), lambda d, i: (d, 0, 0))],
        out_specs=(pl.BlockSpec((2, None, tm, UP_SHARD), lambda d, i: (0, d, i, 0)),
                   pl.BlockSpec((None, 3, UP_SHARD), lambda d, i: (d, 0, 0)),
                   pl.BlockSpec((None, 1, UP_SHARD), lambda d, i: (d, 0, 0))),
        sem=("parallel", "arbitrary"), comm=comm)


def _mm_up(h2, w_up_t, tm=MM_ROWS, comm=None):
    T, K = h2.shape
    tm = _tile(T, tm)
    return _matmul(
        "mm_up", "nt", h2, w_up_t, jax.ShapeDtypeStruct((N_DEV, T, UP_SHARD), BF16), (N_DEV, T // tm, 1),
        pl.BlockSpec((tm, K), lambda j, i, k: (i, 0)), pl.BlockSpec((None, UP_SHARD, K), lambda j, i, k: (j, 0, 0)),
        pl.BlockSpec((None, tm, UP_SHARD), lambda j, i, k: (j, i, 0)), (tm, UP_SHARD), comm=comm)


def _loss_epilogue(ffn, operands, outputs, first):
    x1_ref, t_ref, g_ref = operands
    dx_ref, dg_ref, loss_ref = outputs

    @pl.when(first)
    def _():
        dg_ref[...] = jnp.zeros_like(dg_ref)
        loss_ref[...] = jnp.zeros_like(loss_ref)

    xv = x1_ref[...] + ffn
    r = lax.rsqrt(jnp.mean(xv * xv, axis=-1, keepdims=True) + EPS)
    xhat = xv * r
    err = xhat * g_ref[...] - t_ref[...]
    loss_ref[...] += (0.5 / D_MODEL) * jnp.sum(err * err)
    dy = err * (1.0 / D_MODEL)
    dg_ref[...] += jnp.sum(dy * xhat, axis=0, keepdims=True)
    dxhat = dy * g_ref[...]
    dx_ref[...] = r * (dxhat - xhat * jnp.mean(dxhat * xhat, axis=-1, keepdims=True))


def _mm_down_loss(u, w_down, x1, target, g_final, tm=MM_ROWS_RES):
    J, T, n = u.shape
    tm = _tile(T, tm)
    row = pl.BlockSpec((tm, D_MODEL), lambda i, d: (i, 0))
    vec = pl.BlockSpec((1, D_MODEL), lambda i, d: (0, 0))
    vec_shape = jax.ShapeDtypeStruct((1, D_MODEL), F32)
    return _matmul(
        "mm_down", "nn", u, w_down, (jax.ShapeDtypeStruct((T, D_MODEL), F32), vec_shape, vec_shape), (T // tm, J),
        pl.BlockSpec((None, tm, n), lambda i, d: (d, i, 0)), pl.BlockSpec((None, n, D_MODEL), lambda i, d: (d, 0, 0)),
        (row, vec, vec), (tm, D_MODEL), [x1, target, g_final], [row, row, vec], epilogue=_loss_epilogue)


def _mm_down_t(dx, w_down, tm=MM_ROWS):
    T = dx.shape[0]
    J, n, _ = w_down.shape
    tm = _tile(T, tm)
    return _matmul(
        "mm_down_t", "nt", dx, w_down, jax.ShapeDtypeStruct((J, T, n), BF16), (J, T // tm, 1),
        pl.BlockSpec((tm, D_MODEL), lambda d, i, k: (i, 0)), pl.BlockSpec((None, n, D_MODEL), lambda d, i, k: (d, 0, 0)),
        pl.BlockSpec((None, tm, n), lambda d, i, k: (d, i, 0)), (tm, n))


def _mm_dw_down(u, dx, tk=MM_TOKENS):
    J, T, n = u.shape
    tk = _tile(T, tk)
    return _matmul(
        "mm_dw_down", "tn", u, dx, jax.ShapeDtypeStruct((J, n, D_MODEL), BF16), (J, T // tk),
        pl.BlockSpec((None, tk, n), lambda d, k: (d, k, 0)), pl.BlockSpec((tk, D_MODEL), lambda d, k: (k, 0)),
        pl.BlockSpec((None, n, D_MODEL), lambda d, k: (d, 0, 0)), (n, D_MODEL))


def _mm_dw_up(h2, d_up, tk=MM_TOKENS):
    T, K = h2.shape
    tk = _tile(T, tk)
    return _matmul(
        "mm_dw_up", "tn", d_up, h2, jax.ShapeDtypeStruct((N_DEV, UP_SHARD, K), BF16), (N_DEV, T // tk),
        pl.BlockSpec((None, tk, UP_SHARD), lambda j, k: (j, k, 0)), pl.BlockSpec((tk, K), lambda j, k: (k, 0)),
        pl.BlockSpec((None, UP_SHARD, K), lambda j, k: (j, 0, 0)), (UP_SHARD, K))


def _mm_up_t(d_up, w_up_t, rms, tm=MM_ROWS_RES, comm=None):
    J, T, n = d_up.shape
    K = w_up_t.shape[2]
    tm = _tile(T, tm)
    fused = _rms_bwd_fused(T, K, tm, rms)
    return _matmul(
        "mm_up_t", "nn", d_up, w_up_t, fused.pop("out_shape"), (T // tm, J),
        pl.BlockSpec((None, tm, n), lambda i, j: (j, i, 0)), pl.BlockSpec((None, n, K), lambda i, j: (j, 0, 0)),
        fused.pop("o_spec"), (tm, K), comm=comm, **fused)


def _cast_shards(shards):
    def body(*refs):
        n = len(refs) // 2
        for src, dst in zip(refs[:n], refs[n:]):
            dst[...] = src[...].astype(dst.dtype)

    return pl.pallas_call(
        body, out_shape=[jax.ShapeDtypeStruct(s.shape, BF16) for s in shards], name="cast_shards",
        compiler_params=pltpu.CompilerParams(vmem_limit_bytes=VMEM_LIMIT),
    )(*shards)


def _adamw(w, g, m, v):
    m = ADAM_B1 * m + (1.0 - ADAM_B1) * g
    v = ADAM_B2 * v + (1.0 - ADAM_B2) * (g * g)
    m_hat = m / (1.0 - ADAM_B1 ** ADAM_STEP)
    v_hat = v / (1.0 - ADAM_B2 ** ADAM_STEP)
    delta = -ADAM_LR * (m_hat / (jnp.sqrt(v_hat) + ADAM_EPS) + ADAM_WD * w)
    return delta, m, v


def _sum_parts(p_ref):
    g = p_ref[0].astype(F32)
    for d in range(1, N_DEV):
        g = g + p_ref[d].astype(F32)
    return g


ADAM_ROWS = 512


def _reduce_adam(name, parts, w, m, v):
    R, Cn = w.shape
    by_rows = sum(p.shape[1] for p in parts) == R and len(parts) > 1
    common = math.gcd(*[p.shape[1] for p in parts])
    tr = max(t for t in range(8, min(common, ADAM_ROWS) + 1, 8) if common % t == 0)
    n_tiles = [p.shape[1] // tr for p in parts]
    first = [sum(n_tiles[:j]) for j in range(len(parts))] if by_rows else [0] * len(parts)

    def body(*refs):
        p_refs = refs[:len(parts)]
        w_ref, m_ref, v_ref, g_out, d_out, m_out, v_out = refs[len(parts):]

        def update(p_ref):
            g = _sum_parts(p_ref)
            delta, m_new, v_new = _adamw(w_ref[...], g, m_ref[...], v_ref[...])
            g_out[...] = g
            d_out[...] = delta
            m_out[...] = m_new
            v_out[...] = v_new

        if len(parts) == 1:
            update(p_refs[0])
        elif by_rows:
            i = pl.program_id(0)
            for p_ref, t0, n in zip(p_refs, first, n_tiles):
                pl.when((i >= t0) & (i < t0 + n))(functools.partial(update, p_ref))
        else:
            c = lax.axis_index("c")
            for side, p_ref in enumerate(p_refs):
                pl.when(c == side)(functools.partial(update, p_ref))

    def part_spec(t0, n):
        return pl.BlockSpec((N_DEV, tr, Cn), lambda i: (0, jnp.clip(i - t0, 0, n - 1), 0))

    row = pl.BlockSpec((tr, Cn), lambda i: (i, 0))
    shape = jax.ShapeDtypeStruct((R, Cn), F32)
    return pl.pallas_call(
        body, out_shape=(shape,) * 4, grid=(R // tr,),
        in_specs=[part_spec(t0, n) for t0, n in zip(first, n_tiles)] + [row, row, row],
        out_specs=(row,) * 4, name=name, compiler_params=_params(("parallel",)),
    )(*parts, w, m, v)


def _small_adam(name, gathered, params):
    n_g, n_p = len(gathered), len(params)

    def body(*refs):
        g_refs = refs[:n_g]
        wmv = refs[n_g:n_g + 3 * n_p]
        sums = refs[n_g + 3 * n_p:2 * n_g + 3 * n_p]
        upd = refs[2 * n_g + 3 * n_p:]
        for j in range(n_g):
            g = _sum_parts(g_refs[j])
            sums[j][...] = g
            if j < n_p:
                w_ref, m_ref, v_ref = wmv[3 * j:3 * j + 3]
                delta, m_new, v_new = _adamw(w_ref[...], g, m_ref[...], v_ref[...])
                upd[3 * j][...] = delta
                upd[3 * j + 1][...] = m_new
                upd[3 * j + 2][...] = v_new

    flat = [a for wmv in params for a in wmv]
    out_shape = [jax.ShapeDtypeStruct(g.shape[1:], F32) for g in gathered]
    out_shape += [jax.ShapeDtypeStruct(a.shape, F32) for a in flat]
    res = pl.pallas_call(body, out_shape=out_shape, name=name)(*gathered, *flat)
    return res[:n_g], [tuple(res[n_g + 3 * j:n_g + 3 * j + 3]) for j in range(n_p)]


def _adam_only(name, g, w, m, v):
    def body(g_ref, w_ref, m_ref, v_ref, d_out, m_out, v_out):
        delta, m_new, v_new = _adamw(w_ref[...], g_ref[...], m_ref[...], v_ref[...])
        d_out[...] = delta
        m_out[...] = m_new
        v_out[...] = v_new

    shape = jax.ShapeDtypeStruct(w.shape, F32)
    return pl.pallas_call(body, out_shape=(shape,) * 3, name=name)(g, w, m, v)


def kernel(x, mem, g_mix, w_in, w_pool, pool_scale, w_a, g_ret, b_ret, w_r, g_mem, w_mem_kv, w_c, w_out, g_ffn, w_up, conv_w, conv_b, w_down, g_final, loss_target, m_g_mix, m_w_in, m_w_pool, m_pool_scale, m_w_a, m_g_ret, m_b_ret, m_w_r, m_g_mem, m_w_mem_kv, m_w_c, m_w_out, m_g_ffn, m_w_up, m_conv_w, m_conv_b, m_w_down, m_g_final, v_g_mix, v_w_in, v_w_pool, v_pool_scale, v_w_a, v_g_ret, v_b_ret, v_w_r, v_g_mem, v_w_mem_kv, v_w_c, v_w_out, v_g_ffn, v_w_up, v_conv_w, v_conv_b, v_w_down, v_g_final):
    B, S, _ = x.shape
    M = mem.shape[1]
    T = B * S
    me = _my_index()
    x2d = x.reshape(T, D_MODEL)
    mem2d = mem.reshape(B * M, D_MODEL)
    tgt2d = loss_target.reshape(T, D_MODEL)
    g_final2 = g_final.reshape(1, D_MODEL)

    big = dict(w_in=w_in[0], w_a=w_a[0], w_r=w_r[0], w_mem_kv=w_mem_kv[0], w_c=w_c[0], w_out=w_out[0],
               w_up=w_up[0].T, w_down=w_down[0])
    names = list(big)
    cast = dict(zip(names, _cast_shards([big[n] for n in names])))
    cb = conv_b[0].reshape(FFN_SLABS, 1, UP_SHARD)
    wp = w_pool[0]
    tables = _ret_tables(S)

    h = _rms_fwd("rms_mix", x2d, g_mix)
    early = ("w_a", "w_r", "w_mem_kv", "w_c", "w_out")
    (proj, Win), landed = _mm_in_gather(h, cast["w_in"], comm=_Gather([cast[n] for n in early] + [conv_w[0]]))
    W = dict(zip(early, landed))
    half = D_MODEL // 2
    (yr, ret_states), (Wup_lo,) = _ret_fwd(proj, g_ret, b_ret, tables, B, S,
                                           comm=_Gather([cast["w_up"][:, :half]]))
    cw_full = landed[-1].transpose(1, 0, 2).reshape(3, FFN_HIDDEN)
    cw = cw_full.reshape(3, FFN_SLABS, UP_SHARD).transpose(1, 0, 2)
    Wa = W["w_a"].transpose(1, 0, 2).reshape(POOL_WIDTH, D_MODEL)
    Wc = W["w_c"].transpose(1, 0, 2).reshape(XA_WIDTH, D_MODEL)
    Wr = W["w_r"].reshape(D_MODEL, D_MODEL)
    Wkv = W["w_mem_kv"].reshape(D_MODEL, D_MODEL)
    Wout = W["w_out"].reshape(D_MODEL, D_MODEL)
    ypre = _pool_fwd(proj, wp, pool_scale, B, S)
    y_pool = _mm_rows("mm_a", ypre, Wa, BF16)
    y_ret = _mm_rows("mm_r", yr, Wr, BF16)
    mem_n = _rms_fwd("rms_mem", mem2d, g_mem)
    kv = _mm_rows("mm_kv", mem_n, Wkv)
    o_mem = _xa_fwd(proj, kv, B, S, M)[0]
    y_mem = _mm_rows("mm_c", o_mem, Wc, BF16)
    ys = (y_pool, y_ret, y_mem)
    (merged, x1, h2), (Wup_hi,) = _merge_out(proj, ys, Wout, x2d, g_ffn, comm=_Gather([cast["w_up"][:, half:]]))
    Wup = jnp.concatenate([Wup_lo, Wup_hi], axis=2)
    up, (Wdown,) = _mm_up(h2, Wup, comm=_Gather([cast["w_down"]]))
    up = up.reshape(2, FFN_SLABS, T, UP_SHARD)
    Wdown = Wdown.reshape(FFN_SLABS, UP_SHARD, D_MODEL)
    u = _glu_fwd(up, cw, cb, S)[0]

    dx2, dg_final, loss_part = _mm_down_loss(u, Wdown, x1, tgt2d, g_final2)
    received = {}
    d_u = _mm_down_t(dx2, Wdown)
    dW_down = _mm_dw_down(u, dx2)
    (d_up, d_cw, d_cb), (received["w_down"],) = _glu_bwd(
        up, d_u, cw, cb, S, comm=_Exchange([dW_down.reshape(N_DEV, -1, D_MODEL)]))
    d_up = d_up.reshape(N_DEV, T, UP_SHARD)
    dW_up = _mm_dw_up(h2, d_up)
    (dx1, dg_ffn), (up_c0,) = _mm_up_t(d_up, Wup, (x1, g_ffn, dx2), comm=_ExchangeTo([dW_up], 0))
    d_merged = _mm_rows("mm_out_t", dx1, Wout, kind="nt")
    dW_out = _mm_tn("mm_dw_out", merged, dx1, BF16)
    (d_gl, d_y_pool, d_y_ret, d_y_mem), (up_c1,) = _merge_bwd(proj, ys, d_merged, comm=_ExchangeTo([dW_up], 1))
    received["w_up"] = [up_c0, up_c1]
    dW_c = _mm_tn("mm_dw_c", o_mem, d_y_mem, BF16)
    d_o_mem = _mm_rows("mm_c_t", d_y_mem, Wc, kind="nt")
    (d_qx, d_kmem, d_vmem), (received["w_out"],) = _xa_bwd(
        proj, kv, d_o_mem, B, S, M, comm=_Exchange([dW_out.reshape(N_DEV, -1, D_MODEL)]))
    d_kv = jnp.concatenate([d_kmem, d_vmem], axis=1)
    dW_kv = _mm_tn("mm_dw_kv", mem_n, d_kv, BF16)
    d_mem_n = _mm_rows("mm_kv_t", d_kv, Wkv, kind="nt")
    dg_mem = _rms_bwd("rms_mem_bwd", mem2d, g_mem, d_mem_n, None)
    dW_a = _mm_tn("mm_dw_a", ypre, d_y_pool, BF16)
    d_ypre = _mm_rows("mm_a_t", d_y_pool, Wa, kind="nt")
    (d_hp, dw_pool, d_scale), (received["w_a"],) = _pool_bwd(
        proj, d_ypre, wp, pool_scale, B, S,
        comm=_Exchange([dW_a.reshape(POOL_WIDTH, N_DEV, -1).transpose(1, 0, 2)]))
    dW_r = _mm_tn("mm_dw_r", yr, d_y_ret, BF16)
    d_yr = _mm_rows("mm_r_t", d_y_ret, Wr, kind="nt")
    (d_q, d_k, d_v, d_gr, dg_ret, db_ret), landed = _ret_bwd(
        proj, ret_states, d_yr, g_ret, b_ret, tables, B, S,
        comm=_Exchange([dW_r.reshape(N_DEV, -1, D_MODEL), dW_c.reshape(XA_WIDTH, N_DEV, -1).transpose(1, 0, 2),
                        dW_kv.reshape(N_DEV, -1, D_MODEL)]))
    received["w_r"], received["w_c"], received["w_mem_kv"] = landed
    small_names = ["w_pool", "pool_scale", "g_ret", "b_ret", "g_mem", "g_ffn", "conv_b", "g_final"]
    small_grads = [dw_pool, d_scale, dg_ret, db_ret, dg_mem, dg_ffn, d_cb.reshape(1, FFN_HIDDEN), dg_final,
                   d_cw.transpose(1, 0, 2).reshape(3, FFN_HIDDEN), loss_part]
    d_proj = jnp.concatenate([d_hp, d_q, d_k, d_v, d_gr, d_qx, d_gl], axis=1)
    dW_in0, small_all = _mm_tn_slab("mm_dw_in0", h[:, :W_IN_FIRST_ROWS], d_proj, IN_SHARD, BF16,
                                    comm=_Exchange([], whole=small_grads))
    dW_in1, (in0,) = _mm_tn_slab("mm_dw_in1", h[:, W_IN_FIRST_ROWS:], d_proj, IN_SHARD, BF16,
                                 comm=_Exchange([dW_in0]))
    (grad_x, dg_mix), (in1,) = _mm_cols_slab_t("mm_in_t", d_proj, Win, (x2d, g_mix, dx1), comm=_Exchange([dW_in1]))
    received["w_in"] = [in0, in1]
    (g_mix_all,) = _comm_call("gather_g_mix", _Exchange([], whole=[dg_mix]))

    args = dict(g_mix=g_mix, w_in=w_in, w_pool=w_pool, pool_scale=pool_scale, w_a=w_a, g_ret=g_ret, b_ret=b_ret,
                w_r=w_r, g_mem=g_mem, w_mem_kv=w_mem_kv, w_c=w_c, w_out=w_out, g_ffn=g_ffn, w_up=w_up,
                conv_w=conv_w, conv_b=conv_b, w_down=w_down, g_final=g_final)
    m_in = dict(g_mix=m_g_mix, w_in=m_w_in, w_pool=m_w_pool, pool_scale=m_pool_scale, w_a=m_w_a, g_ret=m_g_ret,
                b_ret=m_b_ret, w_r=m_w_r, g_mem=m_g_mem, w_mem_kv=m_w_mem_kv, w_c=m_w_c, w_out=m_w_out,
                g_ffn=m_g_ffn, w_up=m_w_up, conv_w=m_conv_w, conv_b=m_conv_b, w_down=m_w_down, g_final=m_g_final)
    v_in = dict(g_mix=v_g_mix, w_in=v_w_in, w_pool=v_w_pool, pool_scale=v_pool_scale, w_a=v_w_a, g_ret=v_g_ret,
                b_ret=v_b_ret, w_r=v_w_r, g_mem=v_g_mem, w_mem_kv=v_w_mem_kv, w_c=v_w_c, w_out=v_w_out,
                g_ffn=v_g_ffn, w_up=v_w_up, conv_w=v_conv_w, conv_b=v_conv_b, w_down=v_w_down, g_final=v_g_final)

    grads, deltas, new_m, new_v = {}, {}, {}, {}
    for n in names:
        parts = received[n] if isinstance(received[n], list) else [received[n]]
        flip = (lambda a: a.T) if n == "w_up" else (lambda a: a)
        outs = _reduce_adam("adam_" + n, parts, big[n], flip(m_in[n][0]), flip(v_in[n][0]))
        for store, val in zip((grads, deltas, new_m, new_v), outs):
            store[n] = flip(val)[None]

    def as_small(a):
        return a.reshape(a.shape[-3:]) if a.ndim > 2 else a.reshape(1, -1)

    def small_update(call_name, param_names, gathered):
        params = [tuple(as_small(d[n]) for d in (args, m_in, v_in)) for n in param_names]
        sums, updates = _small_adam(call_name, gathered, params)
        for n, g, (d_, m_, v_) in zip(param_names, sums, updates):
            shape = args[n].shape
            grads[n], deltas[n], new_m[n], new_v[n] = (a.reshape(shape) for a in (g, d_, m_, v_))
        return sums[len(param_names):]

    g_cw_full, loss_row = small_update("adam_small", small_names, small_all)
    loss = loss_row[0, 0]
    small_update("adam_g_mix", ["g_mix"], [g_mix_all])

    shard_cols = FFN_HIDDEN // N_DEV
    g_cw = lax.dynamic_slice_in_dim(g_cw_full, me * shard_cols, shard_cols, axis=1)
    d_, m_, v_ = _adam_only("adam_conv_w", g_cw, conv_w[0], m_conv_w[0], v_conv_w[0])
    grads["conv_w"], deltas["conv_w"], new_m["conv_w"], new_v["conv_w"] = g_cw[None], d_[None], m_[None], v_[None]

    order = ["g_mix", "w_in", "w_pool", "pool_scale", "w_a", "g_ret", "b_ret", "w_r", "g_mem", "w_mem_kv", "w_c",
             "w_out", "g_ffn", "w_up", "conv_w", "conv_b", "w_down", "g_final"]
    return (loss, grad_x.reshape(B, S, D_MODEL), *[grads[n] for n in order], *[deltas[n] for n in order],
            *[new_m[n] for n in order], *[new_v[n] for n in order])
```

```python
import functools
import math

import jax
import jax.numpy as jnp
from jax import lax
from jax.experimental import pallas as pl
from jax.experimental.pallas import tpu as pltpu

F32 = jnp.float32
BF16 = jnp.bfloat16

N_DEV = 8
D_MODEL = 1024
POOL_WINDOWS = (2, 4, 8, 16)
POOL_GROUP_DIM = 128
POOL_WIDTH = 512
POOL_HALO = 16
RET_HEADS = 4
RET_QK_DIM = 128
RET_V_DIM = 256
RET_CHUNK = 128
ROPE_BASE = 10000.0
XA_HEADS = 4
XA_HEAD_DIM = 128
XA_WIDTH = 512
IN_WIDTH = 7168
IN_SHARD = IN_WIDTH // N_DEV
FFN_HIDDEN = 2816
UP_SHARD = 2 * FFN_HIDDEN // N_DEV
FFN_SLABS = FFN_HIDDEN // UP_SHARD
EPS = 1e-6
ADAM_LR = 0.001
ADAM_B1 = 0.9
ADAM_B2 = 0.999
ADAM_EPS = 1e-08
ADAM_WD = 0.01
ADAM_STEP = 10
GELU_C = math.sqrt(2.0 / math.pi)
GELU_A = 0.044715
VMEM_LIMIT = 56 * 1024 * 1024
MM_ROWS = 2048
MM_ROWS_RES = 1024
MM_TOKENS = 2048
W_IN_FIRST_ROWS = 384
MESH = pl.DeviceIdType.MESH

COL_Q, COL_K, COL_V, COL_GR, COL_QX, COL_GL = 512, 1024, 1536, 2560, 3584, 4096

_DIMS = {
    "nn": (((1,), (0,)), ((), ())),
    "nt": (((1,), (1,)), ((), ())),
    "tn": (((0,), (0,)), ((), ())),
}


def _dot(a, b, kind="nn"):
    return lax.dot_general(a.astype(BF16), b.astype(BF16), _DIMS[kind], preferred_element_type=F32)


def _params(sem, vmem=VMEM_LIMIT):
    return pltpu.CompilerParams(dimension_semantics=sem, vmem_limit_bytes=vmem)


def _tile(n, pref):
    t = min(n, pref)
    while n % t:
        t //= 2
    return t


def _mesh_pos():
    return lax.axis_index("x"), lax.axis_index("y"), lax.axis_index("c")


def _dev_index(x, y, c):
    return 4 * x + 2 * y + c


def _my_index():
    return _dev_index(*_mesh_pos())


def _remote(src, dst, send_sems, recv_sems, s, to):
    return pltpu.make_async_remote_copy(src_ref=src, dst_ref=dst, send_sem=send_sems.at[s], recv_sem=recv_sems.at[s],
                                        device_id=to, device_id_type=MESH)


class _Gather:
    def __init__(self, shards):
        self.inputs = list(shards)
        self.out_shapes = [jax.ShapeDtypeStruct((N_DEV,) + s.shape, s.dtype) for s in shards]
        n = len(shards)
        self.sem_shapes = [pltpu.SemaphoreType.DMA((7 * n,)), pltpu.SemaphoreType.DMA((7 * n,)),
                           pltpu.SemaphoreType.DMA((n,))]

    def _places(self):
        x, y, c = _mesh_pos()
        return (x, y, c), (x, y, 1 - c), [(1 - x, y), (x, 1 - y), (1 - x, 1 - y)]

    def _local(self, src, dst, sems):
        me = _my_index()
        return [pltpu.make_async_copy(src[w], dst[w].at[me], sems[2].at[w]) for w in range(len(src))]

    def start(self, src, dst, sems):
        me, sib, chips = self._places()
        for cp in self._local(src, dst, sems):
            cp.start()
        for w in range(len(src)):
            land = dst[w].at[_dev_index(*me)]
            _remote(src[w], land, sems[0], sems[1], 7 * w, sib).start()
            for j, chip in enumerate(chips):
                _remote(src[w], land, sems[0], sems[1], 7 * w + 1 + j, (*chip, me[2])).start()

    def middle(self, src, dst, sems):
        me, sib, chips = self._places()
        for j, chip in enumerate(chips):
            for w in range(len(src)):
                block = dst[w].at[_dev_index(*chip, me[2])]
                _remote(src[w], block, sems[0], sems[1], 7 * w + 1 + j, me).wait_recv()
                _remote(block, block, sems[0], sems[1], 7 * w + 4 + j, sib).start()

    def finish(self, src, dst, sems):
        me, sib, chips = self._places()
        n = len(src)
        for w in range(n):
            _remote(src[w], dst[w].at[_dev_index(*sib)], sems[0], sems[1], 7 * w, me).wait_recv()
            for j, chip in enumerate(chips):
                block = dst[w].at[_dev_index(*chip, sib[2])]
                _remote(block, block, sems[0], sems[1], 7 * w + 4 + j, me).wait_recv()
            for k in range(7):
                _remote(src[w], dst[w].at[0], sems[0], sems[1], 7 * w + k, me).wait_send()
        for cp in self._local(src, dst, sems):
            cp.wait()


class _Exchange:
    def __init__(self, partials, whole=()):
        self.n_part = len(partials)
        self.inputs = list(partials) + list(whole)
        self.out_shapes = [jax.ShapeDtypeStruct(p.shape, p.dtype) for p in partials]
        self.out_shapes += [jax.ShapeDtypeStruct((N_DEV,) + a.shape, a.dtype) for a in whole]
        n = len(self.inputs)
        self.sem_shapes = [pltpu.SemaphoreType.DMA((7 * n,)), pltpu.SemaphoreType.DMA((7 * n,)),
                           pltpu.SemaphoreType.DMA((n,))]

    def _peer(self, k):
        x, y, c = _mesh_pos()
        p = (x ^ ((k >> 2) & 1), y ^ ((k >> 1) & 1), c ^ (k & 1))
        return p, _dev_index(*p)

    def _source(self, src, w, slot):
        return src[w].at[slot] if w < self.n_part else src[w]

    def _local(self, src, dst, sems):
        me = _my_index()
        return [pltpu.make_async_copy(self._source(src, w, me), dst[w].at[me], sems[2].at[w])
                for w in range(len(src))]

    def start(self, src, dst, sems):
        me = _my_index()
        for cp in self._local(src, dst, sems):
            cp.start()
        for k in range(1, N_DEV):
            peer, peer_idx = self._peer(k)
            for w in range(len(src)):
                _remote(self._source(src, w, peer_idx), dst[w].at[me], sems[0], sems[1], 7 * w + k - 1, peer).start()

    def finish(self, src, dst, sems):
        for k in range(1, N_DEV):
            peer, peer_idx = self._peer(k)
            for w in range(len(src)):
                cp = _remote(self._source(src, w, peer_idx), dst[w].at[peer_idx], sems[0], sems[1], 7 * w + k - 1, peer)
                cp.wait_send()
                cp.wait_recv()
        for cp in self._local(src, dst, sems):
            cp.wait()


class _ExchangeTo:
    def __init__(self, partials, side):
        self.side = side
        self.inputs = list(partials)
        self.out_shapes = [jax.ShapeDtypeStruct(p.shape, p.dtype) for p in partials]
        n = len(partials)
        self.sem_shapes = [pltpu.SemaphoreType.DMA((7 * n,)), pltpu.SemaphoreType.DMA((7 * n,)),
                           pltpu.SemaphoreType.DMA((n,))]

    def _copies(self, src, dst, sems):
        x, y, c = _mesh_pos()
        me = _dev_index(x, y, c)
        receives = c == self.side
        remote = []
        for k in range(1, N_DEV):
            kx, ky, kc = (k >> 2) & 1, (k >> 1) & 1, k & 1
            peer = (x ^ kx, y ^ ky, c ^ kc)
            peer_idx = _dev_index(*peer)
            sends = c == (self.side ^ kc)
            for w in range(len(src)):
                slab = src[w].at[peer_idx]
                s = 7 * w + k - 1
                remote.append((sends, _remote(slab, dst[w].at[me], sems[0], sems[1], s, peer),
                               _remote(slab, dst[w].at[peer_idx], sems[0], sems[1], s, peer)))
        local = [pltpu.make_async_copy(src[w].at[me], dst[w].at[me], sems[2].at[w]) for w in range(len(src))]
        return receives, remote, local

    def start(self, src, dst, sems):
        receives, remote, local = self._copies(src, dst, sems)

        @pl.when(receives)
        def _():
            for cp in local:
                cp.start()

        for sends, send, _ in remote:
            pl.when(sends)(send.start)

    def finish(self, src, dst, sems):
        receives, remote, local = self._copies(src, dst, sems)
        for sends, send, arrive in remote:
            pl.when(sends)(send.wait_send)
            pl.when(receives)(arrive.wait_recv)

        @pl.when(receives)
        def _():
            for cp in local:
                cp.wait()


def _pcall(body, args, *, name, out_shape, grid, in_specs, out_specs, scratch_shapes=(), sem=None, comm=None):
    single = not isinstance(out_shape, (tuple, list))
    outs = [out_shape] if single else list(out_shape)
    ospecs = [out_specs] if single else list(out_specs)
    n_in, n_out, n_scr = len(args), len(outs), len(scratch_shapes)

    def pick(res):
        return res[0] if single else tuple(res[:n_out])

    if comm is None:
        res = pl.pallas_call(
            body, out_shape=outs, grid=grid, in_specs=list(in_specs), out_specs=ospecs,
            scratch_shapes=list(scratch_shapes), name=name, compiler_params=_params(sem),
        )(*args)
        return pick(res), ()

    nci, nco = len(comm.inputs), len(comm.out_shapes)

    def carrier(*refs):
        at = 0
        parts = []
        for size in (n_in, nci, n_out, nco, n_scr, len(comm.sem_shapes)):
            parts.append(refs[at:at + size])
            at += size
        ins, cins, o, couts, scr, sems = parts
        ids = [pl.program_id(a) for a in range(len(grid))]
        first = functools.reduce(jnp.logical_and, [i == 0 for i in ids])
        last = functools.reduce(jnp.logical_and, [i == g - 1 for i, g in zip(ids, grid)])

        body(*ins, *o, *scr)

        @pl.when(first)
        def _():
            comm.start(cins, couts, sems)

        if hasattr(comm, "middle"):
            steps = math.prod(grid)
            at = functools.reduce(lambda lin, ig: lin * ig[1] + ig[0], zip(ids, grid), 0)

            @pl.when(at == min(steps - 1, (3 * steps) // 4))
            def _():
                comm.middle(cins, couts, sems)

        @pl.when(last)
        def _():
            comm.finish(cins, couts, sems)

    hbm = pl.BlockSpec(memory_space=pltpu.HBM)
    res = pl.pallas_call(
        carrier, out_shape=outs + comm.out_shapes, grid=grid, in_specs=list(in_specs) + [hbm] * nci,
        out_specs=ospecs + [hbm] * nco, scratch_shapes=list(scratch_shapes) + comm.sem_shapes, name=name,
        compiler_params=_params(("arbitrary",) * len(grid)),
    )(*args, *comm.inputs)
    return pick(res), tuple(res[n_out:])


def _comm_call(name, comm):
    def body(*refs):
        nci, nco = len(comm.inputs), len(comm.out_shapes)
        cins, couts, sems = refs[:nci], refs[nci:nci + nco], refs[nci + nco:]
        comm.start(cins, couts, sems)
        if hasattr(comm, "middle"):
            comm.middle(cins, couts, sems)
        comm.finish(cins, couts, sems)

    hbm = pl.BlockSpec(memory_space=pltpu.HBM)
    return pl.pallas_call(
        body, out_shape=comm.out_shapes, in_specs=[hbm] * len(comm.inputs), out_specs=[hbm] * len(comm.out_shapes),
        scratch_shapes=comm.sem_shapes, name=name,
    )(*comm.inputs)


def _matmul(name, kind, a, b, out_shape, grid, a_spec, b_spec, o_spec, acc_shape, res=None, res_spec=None,
            comm=None, epilogue=None):
    nk = grid[-1]
    if epilogue is None:
        extra, extra_specs = ([res], [res_spec]) if res is not None else ([], [])
        n_out = 1
    else:
        extra, extra_specs, n_out = list(res), list(res_spec), len(out_shape)
    n_in = 2 + len(extra)

    def body(*refs):
        a_ref, b_ref = refs[0], refs[1]
        extra_refs, out_refs = refs[2:n_in], refs[n_in:n_in + n_out]

        def prod():
            return _dot(a_ref[...], b_ref[...], kind)

        def finish(acc):
            if epilogue is not None:
                ids = [pl.program_id(ax) for ax in range(len(grid) - 1)]
                first = functools.reduce(jnp.logical_and, [i == 0 for i in ids]) if ids else True
                epilogue(acc, extra_refs, out_refs, first)
                return
            if extra_refs:
                acc = acc + extra_refs[0][...]
            out_refs[0][...] = acc.astype(out_refs[0].dtype)

        if nk == 1:
            finish(prod())
        else:
            acc_ref = refs[n_in + n_out]
            k = pl.program_id(len(grid) - 1)

            @pl.when(k == 0)
            def _():
                acc_ref[...] = prod()

            @pl.when(k > 0)
            def _():
                acc_ref[...] += prod()

            @pl.when(k == nk - 1)
            def _():
                finish(acc_ref[...])

    in_specs = [a_spec, b_spec] + extra_specs
    args = (a, b, *extra)
    scratch = [pltpu.VMEM(acc_shape, F32)] if nk > 1 else []
    sem = ("arbitrary",) * len(grid) if epilogue is not None else ("parallel",) * (len(grid) - 1) + ("arbitrary",)
    out, landed = _pcall(body, args, name=name, out_shape=out_shape, grid=grid, in_specs=in_specs,
                         out_specs=o_spec, scratch_shapes=scratch, sem=sem, comm=comm)
    return out if comm is None else (out, landed)


def _mm_rows(name, a, w, out_dtype=F32, res=None, kind="nn", tm=MM_ROWS, comm=None):
    M, K = a.shape
    N = w.shape[1] if kind == "nn" else w.shape[0]
    tm = _tile(M, tm)
    res_spec = pl.BlockSpec((tm, N), lambda i, k: (i, 0)) if res is not None else None
    return _matmul(
        name, kind, a, w, jax.ShapeDtypeStruct((M, N), out_dtype), (M // tm, 1),
        pl.BlockSpec((tm, K), lambda i, k: (i, 0)), pl.BlockSpec(w.shape, lambda i, k: (0, 0)),
        pl.BlockSpec((tm, N), lambda i, k: (i, 0)), (tm, N), res, res_spec, comm)


def _mm_tn(name, a, b, out_dtype=F32, tk=MM_TOKENS, comm=None):
    T, M = a.shape
    N = b.shape[1]
    tk = _tile(T, tk)
    return _matmul(
        name, "tn", a, b, jax.ShapeDtypeStruct((M, N), out_dtype), (1, T // tk),
        pl.BlockSpec((tk, M), lambda i, k: (k, 0)), pl.BlockSpec((tk, N), lambda i, k: (k, 0)),
        pl.BlockSpec((M, N), lambda i, k: (0, 0)), (M, N), comm=comm)


def _mm_in_gather(h, shard, tm=MM_ROWS, comm=None):
    T, K = h.shape
    n = shard.shape[1]
    tm = _tile(T, tm)
    n_tiles = T // tm
    pair_of_chip_step = {4: 1, 2: 2, 6: 3}

    def slab_of(s):
        x, y, c = _mesh_pos()
        return _dev_index(x ^ ((s >> 2) & 1), y ^ ((s >> 1) & 1), c ^ (s & 1))

    def body(h_ref, shard_ref, proj_ref, win_ref, wbuf, slot_sems, send_sems, recv_sems, local_sem):
        s, i = pl.program_id(0), pl.program_id(1)
        x, y, c = _mesh_pos()
        me, sib = (x, y, c), (x, y, 1 - c)

        def slot_copy(step):
            src = shard_ref if step == 0 else win_ref.at[slab_of(step)]
            return pltpu.make_async_copy(src, wbuf.at[step % 2], slot_sems.at[step % 2])

        def fetch(step):
            if step >= 1:
                block = win_ref.at[slab_of(step)]
                if step == 1:
                    pair = 0
                elif step % 2 == 0:
                    pair = pair_of_chip_step[step]
                else:
                    pair = 3 + pair_of_chip_step[step - 1]
                _remote(block, block, send_sems, recv_sems, pair, me).wait_recv()
                if step % 2 == 0:
                    _remote(block, block, send_sems, recv_sems, 3 + pair, sib).start()
            slot_copy(step).start()

        @pl.when((s == 0) & (i == 0))
        def _():
            land = win_ref.at[_dev_index(*me)]
            pltpu.make_async_copy(shard_ref, land, local_sem).start()
            _remote(shard_ref, land, send_sems, recv_sems, 0, sib).start()
            for step, pair in pair_of_chip_step.items():
                peer = (x ^ ((step >> 2) & 1), y ^ ((step >> 1) & 1), c)
                _remote(shard_ref, land, send_sems, recv_sems, pair, peer).start()
            fetch(0)

        for step in range(N_DEV):
            @pl.when((s == step) & (i == 0))
            def _():
                slot_copy(step).wait()

            if step + 1 < N_DEV:
                @pl.when((s == step) & (i == n_tiles - 1))
                def _():
                    fetch(step + 1)

        proj_ref[...] = _dot(h_ref[...], wbuf[s % 2])

        @pl.when((s == N_DEV - 1) & (i == n_tiles - 1))
        def _():
            for pair in range(7):
                _remote(shard_ref, win_ref.at[0], send_sems, recv_sems, pair, me).wait_send()
            pltpu.make_async_copy(shard_ref, win_ref.at[_dev_index(*me)], local_sem).wait()

    hbm = pl.BlockSpec(memory_space=pltpu.HBM)
    return _pcall(
        body, (h, shard), name="mm_in",
        out_shape=(jax.ShapeDtypeStruct((T, N_DEV * n), F32), jax.ShapeDtypeStruct((N_DEV, K, n), shard.dtype)),
        grid=(N_DEV, n_tiles), in_specs=[pl.BlockSpec((tm, K), lambda s, i: (i, 0)), hbm],
        out_specs=(pl.BlockSpec((tm, n), lambda s, i: (i, slab_of(s))), hbm),
        scratch_shapes=[pltpu.VMEM((2, K, n), shard.dtype), pltpu.SemaphoreType.DMA((2,)),
                        pltpu.SemaphoreType.DMA((7,)), pltpu.SemaphoreType.DMA((7,)), pltpu.SemaphoreType.DMA],
        sem=("arbitrary", "arbitrary"), comm=comm)


def _rms_bwd_epilogue(dh, operands, outputs, first):
    x_ref, g_ref, dres_ref = operands
    dx_ref, dg_ref = outputs
    xv = x_ref[...]
    r = lax.rsqrt(jnp.mean(xv * xv, axis=-1, keepdims=True) + EPS)
    xhat = xv * r

    @pl.when(first)
    def _():
        dg_ref[...] = jnp.zeros_like(dg_ref)

    dg_ref[...] += jnp.sum(dh * xhat, axis=0, keepdims=True)
    dxhat = dh * g_ref[...]
    dx_ref[...] = dres_ref[...] + r * (dxhat - xhat * jnp.mean(dxhat * xhat, axis=-1, keepdims=True))


def _rms_bwd_fused(M, K, tm, rms):
    row = pl.BlockSpec((tm, K), lambda i, j: (i, 0))
    vec = pl.BlockSpec((1, K), lambda i, j: (0, 0))
    x, g, dres = rms
    return dict(res=[x, g, dres], res_spec=[row, vec, row], epilogue=_rms_bwd_epilogue,
                out_shape=(jax.ShapeDtypeStruct((M, K), F32), jax.ShapeDtypeStruct((1, K), F32)), o_spec=(row, vec))


def _mm_cols_slab_t(name, a, w_slabs, rms, tm=MM_ROWS_RES, comm=None):
    M = a.shape[0]
    J, K, n = w_slabs.shape
    tm = _tile(M, tm)
    fused = _rms_bwd_fused(M, K, tm, rms)
    return _matmul(
        name, "nt", a, w_slabs, fused.pop("out_shape"), (M // tm, J),
        pl.BlockSpec((tm, n), lambda i, j: (i, j)), pl.BlockSpec((None, K, n), lambda i, j: (j, 0, 0)),
        fused.pop("o_spec"), (tm, K), comm=comm, **fused)


def _mm_tn_slab(name, a, b, n, out_dtype=F32, tk=MM_TOKENS, comm=None):
    T, M = a.shape
    J = b.shape[1] // n
    tk = _tile(T, tk)
    return _matmul(
        name, "tn", a, b, jax.ShapeDtypeStruct((J, M, n), out_dtype), (J, T // tk),
        pl.BlockSpec((tk, M), lambda j, k: (k, 0)), pl.BlockSpec((tk, n), lambda j, k: (k, j)),
        pl.BlockSpec((None, M, n), lambda j, k: (j, 0, 0)), (M, n), comm=comm)


def _rms_fwd(name, x, g, tm=512):
    T, Dm = x.shape
    tm = _tile(T, tm)

    def body(x_ref, g_ref, h_ref):
        xv = x_ref[...]
        r = lax.rsqrt(jnp.mean(xv * xv, axis=-1, keepdims=True) + EPS)
        h_ref[...] = (xv * r * g_ref[...]).astype(h_ref.dtype)

    return pl.pallas_call(
        body, out_shape=jax.ShapeDtypeStruct((T, Dm), BF16), grid=(T // tm,),
        in_specs=[pl.BlockSpec((tm, Dm), lambda i: (i, 0)), pl.BlockSpec((1, Dm), lambda i: (0, 0))],
        out_specs=pl.BlockSpec((tm, Dm), lambda i: (i, 0)), name=name, compiler_params=_params(("parallel",)),
    )(x, g)


def _rms_bwd(name, x, g, dh, dres, tm=512):
    T, Dm = x.shape
    tm = _tile(T, tm)
    want_dx = dres is not None

    def body(*refs):
        if want_dx:
            x_ref, g_ref, dh_ref, dres_ref, dx_ref, dg_ref = refs
        else:
            x_ref, g_ref, dh_ref, dg_ref = refs
        xv = x_ref[...]
        r = lax.rsqrt(jnp.mean(xv * xv, axis=-1, keepdims=True) + EPS)
        xhat = xv * r
        dhv = dh_ref[...]

        @pl.when(pl.program_id(0) == 0)
        def _():
            dg_ref[...] = jnp.zeros_like(dg_ref)

        dg_ref[...] += jnp.sum(dhv * xhat, axis=0, keepdims=True)
        if want_dx:
            dxhat = dhv * g_ref[...]
            dx_ref[...] = dres_ref[...] + r * (dxhat - xhat * jnp.mean(dxhat * xhat, axis=-1, keepdims=True))

    row = pl.BlockSpec((tm, Dm), lambda i: (i, 0))
    vec = pl.BlockSpec((1, Dm), lambda i: (0, 0))
    if want_dx:
        return pl.pallas_call(
            body, out_shape=(jax.ShapeDtypeStruct((T, Dm), F32), jax.ShapeDtypeStruct((1, Dm), F32)),
            grid=(T // tm,), in_specs=[row, vec, row, row], out_specs=(row, vec), name=name,
            compiler_params=_params(("arbitrary",)),
        )(x, g, dh, dres)
    return pl.pallas_call(
        body, out_shape=jax.ShapeDtypeStruct((1, Dm), F32), grid=(T // tm,), in_specs=[row, vec, row],
        out_specs=vec, name=name, compiler_params=_params(("arbitrary",)),
    )(x, g, dh)


def _pool_rows(S):
    return _tile(S, 256)


def _pool_count(c0, rows, w):
    t = c0 + lax.broadcasted_iota(jnp.int32, (rows, 1), 0)
    return jnp.minimum(t + 1, w).astype(F32)


def _pool_fwd(proj, w_pool, scale, B, S):
    CH = _pool_rows(S)

    def body(hp_ref, wp_ref, sc_ref, o_ref, pad_ref):
        pad_ref[0:POOL_HALO, :] = jnp.zeros((POOL_HALO, POOL_WIDTH), F32)
        pad_ref[POOL_HALO:, :] = hp_ref[...]
        for gi, w in enumerate(POOL_WINDOWS):
            cols = slice(gi * POOL_GROUP_DIM, (gi + 1) * POOL_GROUP_DIM)
            for c in range(S // CH):
                base = POOL_HALO + c * CH
                acc = pad_ref[base:base + CH, cols]
                tok = acc
                for j in range(1, w):
                    acc = acc + pad_ref[base - j:base - j + CH, cols]
                pooled = acc / _pool_count(c * CH, CH, w) - tok
                z = _dot(pooled, wp_ref[gi])
                o_ref[c * CH:(c + 1) * CH, cols] = (z * sc_ref[:, cols]).astype(o_ref.dtype)

    return pl.pallas_call(
        body, out_shape=jax.ShapeDtypeStruct((B * S, POOL_WIDTH), BF16), grid=(B,),
        in_specs=[pl.BlockSpec((S, POOL_WIDTH), lambda b: (b, 0)),
                  pl.BlockSpec(w_pool.shape, lambda b: (0, 0, 0)),
                  pl.BlockSpec((1, POOL_WIDTH), lambda b: (0, 0))],
        out_specs=pl.BlockSpec((S, POOL_WIDTH), lambda b: (b, 0)),
        scratch_shapes=[pltpu.VMEM((S + POOL_HALO, POOL_WIDTH), F32)],
        name="pool_fwd", compiler_params=_params(("parallel",)),
    )(proj, w_pool, scale)


def _pool_bwd(proj, d_ypre, w_pool, scale, B, S, comm=None):
    CH = _pool_rows(S)

    def body(hp_ref, dy_ref, wp_ref, sc_ref, dhp_ref, dwp_ref, dsc_ref, pad_ref, sc_pad_ref, dp_ref):
        @pl.when(pl.program_id(0) == 0)
        def _():
            dwp_ref[...] = jnp.zeros_like(dwp_ref)
            dsc_ref[...] = jnp.zeros_like(dsc_ref)

        pad_ref[0:POOL_HALO, :] = jnp.zeros((POOL_HALO, POOL_WIDTH), F32)
        pad_ref[POOL_HALO:, :] = hp_ref[...]
        sc_pad_ref[S:, :] = jnp.zeros((POOL_HALO, POOL_WIDTH), F32)
        for gi, w in enumerate(POOL_WINDOWS):
            cols = slice(gi * POOL_GROUP_DIM, (gi + 1) * POOL_GROUP_DIM)
            for c in range(S // CH):
                base = POOL_HALO + c * CH
                rows = slice(c * CH, (c + 1) * CH)
                acc = pad_ref[base:base + CH, cols]
                tok = acc
                for j in range(1, w):
                    acc = acc + pad_ref[base - j:base - j + CH, cols]
                cnt = _pool_count(c * CH, CH, w)
                pooled = acc / cnt - tok
                z = _dot(pooled, wp_ref[gi])
                dy = dy_ref[rows, cols]
                dsc_ref[:, cols] += jnp.sum(dy * z, axis=0, keepdims=True)
                dz = dy * sc_ref[:, cols]
                dwp_ref[gi] += _dot(pooled, dz, "tn")
                dpool = _dot(dz, wp_ref[gi], "nt")
                dp_ref[rows, cols] = dpool
                sc_pad_ref[rows, cols] = dpool / cnt
            for c in range(S // CH):
                rows = slice(c * CH, (c + 1) * CH)
                acc = sc_pad_ref[rows, cols]
                for j in range(1, w):
                    acc = acc + sc_pad_ref[c * CH + j:c * CH + j + CH, cols]
                dhp_ref[rows, cols] = (acc - dp_ref[rows, cols]).astype(dhp_ref.dtype)

    seq = pl.BlockSpec((S, POOL_WIDTH), lambda b: (b, 0))
    return _pcall(
        body, (proj, d_ypre, w_pool, scale), name="pool_bwd",
        out_shape=(jax.ShapeDtypeStruct((B * S, POOL_WIDTH), BF16),
                   jax.ShapeDtypeStruct(w_pool.shape, F32), jax.ShapeDtypeStruct((1, POOL_WIDTH), F32)),
        grid=(B,),
        in_specs=[seq, seq, pl.BlockSpec(w_pool.shape, lambda b: (0, 0, 0)),
                  pl.BlockSpec((1, POOL_WIDTH), lambda b: (0, 0))],
        out_specs=(seq, pl.BlockSpec(w_pool.shape, lambda b: (0, 0, 0)),
                   pl.BlockSpec((1, POOL_WIDTH), lambda b: (0, 0))),
        scratch_shapes=[pltpu.VMEM((S + POOL_HALO, POOL_WIDTH), F32),
                        pltpu.VMEM((S + POOL_HALO, POOL_WIDTH), F32),
                        pltpu.VMEM((S, POOL_WIDTH), F32)],
        sem=("arbitrary",), comm=comm)


def _ret_tables(S):
    half = RET_QK_DIM // 2
    inv = ROPE_BASE ** (-jnp.arange(half, dtype=F32) / half)
    ang = jnp.arange(S, dtype=F32)[:, None] * inv[None, :]
    cos, sin = jnp.cos(ang), jnp.sin(ang)
    cos_full = jnp.concatenate([cos, cos], axis=-1)
    sin_signed = jnp.concatenate([-sin, sin], axis=-1)
    C = RET_CHUNK
    lg = jnp.log1p(-jnp.exp2(-5.0 - jnp.arange(RET_HEADS, dtype=F32)))[:, None, None]
    idx = jnp.arange(C, dtype=F32)
    rel = idx[:, None] - idx[None, :]
    decay = jnp.where(rel >= 0, jnp.exp(jnp.maximum(rel, 0.0) * lg), 0.0)
    q_decay = jnp.broadcast_to(jnp.exp((idx + 1.0)[None, :, None] * lg), (RET_HEADS, C, RET_QK_DIM))
    k_decay = jnp.broadcast_to(jnp.exp((C - 1.0 - idx)[None, :, None] * lg), (RET_HEADS, C, RET_QK_DIM))
    c_decay = jnp.broadcast_to(jnp.exp(C * lg), (RET_HEADS, 1, RET_V_DIM))
    return cos_full, sin_signed, decay, q_decay, k_decay, c_decay


def _rope(x, cos_full, sin_signed):
    return x * cos_full + pltpu.roll(x, RET_QK_DIM // 2, axis=1) * sin_signed


def _rope_t(dy, cos_full, sin_signed):
    return dy * cos_full + pltpu.roll(dy * sin_signed, RET_QK_DIM // 2, axis=1)


RET_COLS = 512


def _ret_specs(N, chunk_of):
    C = RET_CHUNK

    def rows(width, col=0):
        return pl.BlockSpec((C, width), lambda b, i: (b * N + chunk_of(i), col))

    def whole(shape):
        return pl.BlockSpec(shape, lambda b, i: (0,) * len(shape))

    wide = RET_HEADS * RET_V_DIM
    return dict(
        q=rows(RET_COLS, COL_Q // RET_COLS), k=rows(RET_COLS, COL_K // RET_COLS),
        v=[rows(RET_COLS, COL_V // RET_COLS + j) for j in range(2)],
        gr=[rows(RET_COLS, COL_GR // RET_COLS + j) for j in range(2)],
        table=pl.BlockSpec((C, RET_QK_DIM), lambda b, i: (chunk_of(i), 0)),
        decay=whole((RET_HEADS, C, C)), qd=whole((RET_HEADS, C, RET_QK_DIM)), kd=whole((RET_HEADS, C, RET_QK_DIM)),
        cd=whole((RET_HEADS, 1, RET_V_DIM)), vec=whole((1, wide)), qk_rows=rows(RET_COLS), v_rows=rows(wide),
        state=pl.BlockSpec((None, None, RET_HEADS, RET_QK_DIM, RET_V_DIM), lambda b, i: (b, chunk_of(i), 0, 0, 0)))


def _head_cols(h):
    pair = slice((h % 2) * RET_V_DIM, (h % 2 + 1) * RET_V_DIM)
    return slice(h * RET_QK_DIM, (h + 1) * RET_QK_DIM), h // 2, pair, slice(h * RET_V_DIM, (h + 1) * RET_V_DIM)


def _group_norm(o):
    mu = jnp.mean(o, axis=-1, keepdims=True)
    oc = o - mu
    rstd = lax.rsqrt(jnp.mean(oc * oc, axis=-1, keepdims=True) + EPS)
    return oc * rstd, rstd


def _ret_fwd(proj, g_ret, b_ret, tables, B, S, comm=None):
    N = S // RET_CHUNK
    cos_t, sin_t, decay, q_decay, k_decay, c_decay = tables
    sp = _ret_specs(N, lambda i: i)

    def body(q_ref, k_ref, v0_ref, v1_ref, gr0_ref, gr1_ref, cos_ref, sin_ref, dec_ref, qd_ref, kd_ref, cd_ref,
             g_ref, b_ref, y_ref, rs_ref, r_ref):
        @pl.when(pl.program_id(1) == 0)
        def _():
            r_ref[...] = jnp.zeros_like(r_ref)

        cs, sn = cos_ref[...], sin_ref[...]
        heads = range(RET_HEADS)
        cols = [_head_cols(h) for h in heads]
        q = [_rope(q_ref[:, cols[h][0]], cs, sn) for h in heads]
        k = [_rope(k_ref[:, cols[h][0]], cs, sn) * (RET_QK_DIM ** -0.5) for h in heads]
        v = [(v0_ref, v1_ref)[cols[h][1]][:, cols[h][2]] for h in heads]
        R = [r_ref[h] for h in heads]
        s = [_dot(q[h], k[h], "nt") * dec_ref[h] for h in heads]
        o = [_dot(s[h], v[h]) + _dot(q[h] * qd_ref[h], R[h]) for h in heads]
        r_new = [cd_ref[h] * R[h] + _dot(k[h] * kd_ref[h], v[h], "tn") for h in heads]
        for h in heads:
            _, j, pair, wide = cols[h]
            rs_ref[h] = R[h]
            r_ref[h] = r_new[h]
            on, _ = _group_norm(o[h])
            gr = (gr0_ref, gr1_ref)[j][:, pair]
            y_ref[:, wide] = (gr * jax.nn.sigmoid(gr) * (on * g_ref[:, wide] + b_ref[:, wide])).astype(y_ref.dtype)

    state = jax.ShapeDtypeStruct((B, N, RET_HEADS, RET_QK_DIM, RET_V_DIM), F32)
    return _pcall(
        body, (proj,) * 6 + (cos_t, sin_t, decay, q_decay, k_decay, c_decay, g_ret, b_ret),
        name="ret_fwd", out_shape=(jax.ShapeDtypeStruct((B * S, RET_HEADS * RET_V_DIM), BF16), state), grid=(B, N),
        in_specs=[sp["q"], sp["k"], *sp["v"], *sp["gr"], sp["table"], sp["table"], sp["decay"], sp["qd"],
                  sp["kd"], sp["cd"], sp["vec"], sp["vec"]],
        out_specs=(sp["v_rows"], sp["state"]),
        scratch_shapes=[pltpu.VMEM((RET_HEADS, RET_QK_DIM, RET_V_DIM), F32)],
        sem=("parallel", "arbitrary"), comm=comm)


def _ret_bwd(proj, states, d_yr, g_ret, b_ret, tables, B, S, comm=None):
    N = S // RET_CHUNK
    cos_t, sin_t, decay, q_decay, k_decay, c_decay = tables
    sp = _ret_specs(N, lambda i: N - 1 - i)
    qk_scale = RET_QK_DIM ** -0.5

    def body(q_ref, k_ref, v0_ref, v1_ref, gr0_ref, gr1_ref, dy_ref, rs_ref, cos_ref, sin_ref, dec_ref, qd_ref,
             kd_ref, cd_ref, g_ref, b_ref, dq_ref, dk_ref, dv_ref, dgr_ref, dg_ref, db_ref, dr_ref):
        @pl.when((pl.program_id(0) == 0) & (pl.program_id(1) == 0))
        def _():
            dg_ref[...] = jnp.zeros_like(dg_ref)
            db_ref[...] = jnp.zeros_like(db_ref)

        @pl.when(pl.program_id(1) == 0)
        def _():
            dr_ref[...] = jnp.zeros_like(dr_ref)

        cs, sn = cos_ref[...], sin_ref[...]
        heads = range(RET_HEADS)
        cols = [_head_cols(h) for h in heads]
        q = [_rope(q_ref[:, cols[h][0]], cs, sn) for h in heads]
        k = [_rope(k_ref[:, cols[h][0]], cs, sn) * qk_scale for h in heads]
        v = [(v0_ref, v1_ref)[cols[h][1]][:, cols[h][2]] for h in heads]
        s = [_dot(q[h], k[h], "nt") * dec_ref[h] for h in heads]
        o = [_dot(s[h], v[h]) + _dot(q[h] * qd_ref[h], rs_ref[h]) for h in heads]
        do = []
        for h in heads:
            _, j, pair, wide = cols[h]
            on, rstd = _group_norm(o[h])
            g = g_ref[:, wide]
            oaff = on * g + b_ref[:, wide]
            gr = (gr0_ref, gr1_ref)[j][:, pair]
            sg = jax.nn.sigmoid(gr)
            dy = dy_ref[:, wide]
            dgr_ref[:, wide] = (dy * oaff * (sg * (1.0 + gr * (1.0 - sg)))).astype(dgr_ref.dtype)
            doaff = dy * (gr * sg)
            dg_ref[:, wide] += jnp.sum(doaff * on, axis=0, keepdims=True)
            db_ref[:, wide] += jnp.sum(doaff, axis=0, keepdims=True)
            don = doaff * g
            do.append(rstd * (don - jnp.mean(don, axis=-1, keepdims=True)
                              - on * jnp.mean(don * on, axis=-1, keepdims=True)))
        ds = [_dot(do[h], v[h], "nt") * dec_ref[h] for h in heads]
        dq = [_dot(ds[h], k[h]) + qd_ref[h] * _dot(do[h], rs_ref[h], "nt") for h in heads]
        dk = [_dot(ds[h], q[h], "tn") + kd_ref[h] * _dot(v[h], dr_ref[h], "nt") for h in heads]
        dv = [_dot(s[h], do[h], "tn") + _dot(k[h] * kd_ref[h], dr_ref[h]) for h in heads]
        dr = [cd_ref[h] * dr_ref[h] + _dot(q[h] * qd_ref[h], do[h], "tn") for h in heads]
        for h in heads:
            qk, _, _, wide = cols[h]
            dv_ref[:, wide] = dv[h].astype(dv_ref.dtype)
            dr_ref[h] = dr[h]
            dq_ref[:, qk] = _rope_t(dq[h], cs, sn).astype(dq_ref.dtype)
            dk_ref[:, qk] = _rope_t(dk[h] * qk_scale, cs, sn).astype(dk_ref.dtype)

    T = B * S
    qk_shape = jax.ShapeDtypeStruct((T, RET_HEADS * RET_QK_DIM), BF16)
    v_shape = jax.ShapeDtypeStruct((T, RET_HEADS * RET_V_DIM), BF16)
    vec_shape = jax.ShapeDtypeStruct((1, RET_HEADS * RET_V_DIM), F32)
    return _pcall(
        body, (proj,) * 6 + (d_yr, states, cos_t, sin_t, decay, q_decay, k_decay, c_decay, g_ret, b_ret),
        name="ret_bwd", out_shape=(qk_shape, qk_shape, v_shape, v_shape, vec_shape, vec_shape), grid=(B, N),
        in_specs=[sp["q"], sp["k"], *sp["v"], *sp["gr"], sp["v_rows"], sp["state"], sp["table"], sp["table"],
                  sp["decay"], sp["qd"], sp["kd"], sp["cd"], sp["vec"], sp["vec"]],
        out_specs=(sp["qk_rows"], sp["qk_rows"], sp["v_rows"], sp["v_rows"], sp["vec"], sp["vec"]),
        scratch_shapes=[pltpu.VMEM((RET_HEADS, RET_QK_DIM, RET_V_DIM), F32)],
        sem=("arbitrary", "arbitrary"), comm=comm)


def _xa_rows(S):
    return _tile(S, 256)


def _xa_groups(S, rows, size=4):
    chunks = [slice(r, r + rows) for r in range(0, S, rows)]
    return [chunks[g:g + size] for g in range(0, len(chunks), size)]


def _xa_specs(S, M):
    q = pl.BlockSpec((S, XA_HEAD_DIM), lambda b, h: (b, COL_QX // XA_HEAD_DIM + h))
    k = pl.BlockSpec((M, XA_HEAD_DIM), lambda b, h: (b, h))
    v = pl.BlockSpec((M, XA_HEAD_DIM), lambda b, h: (b, XA_HEADS + h))
    o = pl.BlockSpec((S, XA_HEAD_DIM), lambda b, h: (b, h))
    return q, k, v, o


def _softmax_rows(s):
    e = jnp.exp(s - jnp.max(s, axis=-1, keepdims=True))
    return e / jnp.sum(e, axis=-1, keepdims=True)


def _xa_fwd(proj, kv, B, S, M, comm=None):
    CH = _xa_rows(S)
    q_spec, k_spec, v_spec, o_spec = _xa_specs(S, M)

    def body(q_ref, k_ref, v_ref, o_ref):
        for group in _xa_groups(S, CH):
            sc = [_dot(q_ref[rows, :], k_ref[...], "nt") * (XA_HEAD_DIM ** -0.5) for rows in group]
            p = [_softmax_rows(s) for s in sc]
            for rows, pg in zip(group, p):
                o_ref[rows, :] = _dot(pg, v_ref[...]).astype(o_ref.dtype)

    return _pcall(
        body, (proj, kv, kv), name="xattn_fwd", out_shape=jax.ShapeDtypeStruct((B * S, XA_WIDTH), BF16),
        grid=(B, XA_HEADS), in_specs=[q_spec, k_spec, v_spec], out_specs=o_spec,
        sem=("parallel", "parallel"), comm=comm)


def _xa_bwd(proj, kv, d_o, B, S, M, comm=None):
    CH = _xa_rows(S)
    q_spec, k_spec, v_spec, o_spec = _xa_specs(S, M)
    scale = XA_HEAD_DIM ** -0.5

    def body(q_ref, k_ref, v_ref, do_ref, dq_ref, dk_ref, dv_ref):
        dk_ref[...] = jnp.zeros_like(dk_ref)
        dv_ref[...] = jnp.zeros_like(dv_ref)
        for group in _xa_groups(S, CH):
            q = [q_ref[rows, :] for rows in group]
            do = [do_ref[rows, :] for rows in group]
            p = [_softmax_rows(_dot(qg, k_ref[...], "nt") * scale) for qg in q]
            dp = [_dot(dg, v_ref[...], "nt") for dg in do]
            ds = [pg * (dpg - jnp.sum(dpg * pg, axis=-1, keepdims=True)) * scale for pg, dpg in zip(p, dp)]
            for rows, dsg in zip(group, ds):
                dq_ref[rows, :] = _dot(dsg, k_ref[...]).astype(dq_ref.dtype)
            dk_ref[...] += sum(_dot(dsg, qg, "tn") for dsg, qg in zip(ds, q))
            dv_ref[...] += sum(_dot(pg, dg, "tn") for pg, dg in zip(p, do))

    kv_out = pl.BlockSpec((M, XA_HEAD_DIM), lambda b, h: (b, h))
    return _pcall(
        body, (proj, kv, kv, d_o), name="xattn_bwd",
        out_shape=(jax.ShapeDtypeStruct((B * S, XA_WIDTH), BF16), jax.ShapeDtypeStruct((B * M, XA_WIDTH), F32),
                   jax.ShapeDtypeStruct((B * M, XA_WIDTH), F32)),
        grid=(B, XA_HEADS), in_specs=[q_spec, k_spec, v_spec, o_spec], out_specs=(o_spec, kv_out, kv_out),
        sem=("parallel", "parallel"), comm=comm)


def _gate_specs(tm):
    n = COL_GL // D_MODEL
    return [pl.BlockSpec((tm, D_MODEL), lambda i, j=j: (i, n + j)) for j in range(3)]


def _merge_out(proj, ys, w_out, x, g, tm=512, comm=None):
    T = proj.shape[0]
    tm = _tile(T, tm)
    row = pl.BlockSpec((tm, D_MODEL), lambda i: (i, 0))
    vec = pl.BlockSpec((1, D_MODEL), lambda i: (0, 0))

    def body(g0, g1, g2, y0, y1, y2, w_ref, x_ref, gf_ref, m_ref, x1_ref, h_ref):
        acc = jax.nn.sigmoid(g0[...]) * y0[...]
        acc = acc + jax.nn.sigmoid(g1[...]) * y1[...]
        acc = acc + jax.nn.sigmoid(g2[...]) * y2[...]
        merged = acc.astype(m_ref.dtype)
        m_ref[...] = merged
        xv = x_ref[...] + _dot(merged, w_ref[...])
        x1_ref[...] = xv
        r = lax.rsqrt(jnp.mean(xv * xv, axis=-1, keepdims=True) + EPS)
        h_ref[...] = (xv * r * gf_ref[...]).astype(h_ref.dtype)

    bf16_rows = jax.ShapeDtypeStruct((T, D_MODEL), BF16)
    return _pcall(
        body, (proj, proj, proj, *ys, w_out, x, g), name="merge_out",
        out_shape=(bf16_rows, jax.ShapeDtypeStruct((T, D_MODEL), F32), bf16_rows), grid=(T // tm,),
        in_specs=_gate_specs(tm) + [row] * 3 + [pl.BlockSpec(w_out.shape, lambda i: (0, 0)), row, vec],
        out_specs=(row, row, row), sem=("parallel",), comm=comm)


def _merge_bwd(proj, ys, dx1, w_out, tm=512, comm=None):
    T = proj.shape[0]
    tm = _tile(T, tm)
    row = pl.BlockSpec((tm, D_MODEL), lambda i: (i, 0))

    def body(g0, g1, g2, y0, y1, y2, dx_ref, w_ref, dgl_ref, d0, d1, d2):
        dm = _dot(dx_ref[...], w_ref[...], "nt")
        for j, (g_ref, y_ref, d_ref) in enumerate(((g0, y0, d0), (g1, y1, d1), (g2, y2, d2))):
            sg = jax.nn.sigmoid(g_ref[...])
            d_ref[...] = (dm * sg).astype(d_ref.dtype)
            dgl_ref[:, j * D_MODEL:(j + 1) * D_MODEL] = (dm * y_ref[...] * sg * (1.0 - sg)).astype(dgl_ref.dtype)

    dy = jax.ShapeDtypeStruct((T, D_MODEL), BF16)
    return _pcall(
        body, (proj, proj, proj, *ys, dx1, w_out), name="merge_bwd",
        out_shape=(jax.ShapeDtypeStruct((T, 3 * D_MODEL), BF16), dy, dy, dy), grid=(T // tm,),
        in_specs=_gate_specs(tm) + [row] * 4 + [pl.BlockSpec(w_out.shape, lambda i: (0, 0))],
        out_specs=(pl.BlockSpec((tm, 3 * D_MODEL), lambda i: (i, 0)), row, row, row),
        sem=("parallel",), comm=comm)


def _gelu(x):
    return 0.5 * x * (1.0 + jnp.tanh(GELU_C * (x + GELU_A * x * x * x)))


def _gelu_grad(x):
    t = jnp.tanh(GELU_C * (x + GELU_A * x * x * x))
    return 0.5 * (1.0 + t) + 0.5 * x * (1.0 - t * t) * GELU_C * (1.0 + 3.0 * GELU_A * x * x)


GLU_HALO = 16


def _shift_down(x, prev, n):
    last = prev.shape[0]
    r = lax.broadcasted_iota(jnp.int32, (8, 1), 0)
    rolled = pltpu.roll(x, n, axis=0)
    head = rolled[0:8]
    for j in range(n):
        head = jnp.where(r == j, prev[last - n + j:last - n + j + 1, :], head)
    return jnp.concatenate([head, rolled[8:]], axis=0)


def _shift_up(x, nxt, n):
    rows = x.shape[0]
    r = lax.broadcasted_iota(jnp.int32, (8, 1), 0)
    rolled = pltpu.roll(x, rows - n, axis=0)
    tail = rolled[rows - 8:]
    for j in range(n):
        tail = jnp.where(r == 8 - n + j, nxt[j:j + 1, :], tail)
    return jnp.concatenate([rolled[:rows - 8], tail], axis=0)


def _conv(a, prev, cw, cb):
    return _shift_down(a, prev, 2) * cw[0:1, :] + _shift_down(a, prev, 1) * cw[1:2, :] + a * cw[2:3, :] + cb


def _glu_fwd(up, cw, cb, S, tm=1024, comm=None):
    T = up.shape[2]
    tm = _tile(S, tm)
    per_seq = S // tm

    def body(ab_ref, prev_ref, cw_ref, cb_ref, u_ref):
        i = pl.program_id(1)
        prev = jnp.where(i % per_seq == 0, 0.0, prev_ref[...].astype(F32))
        ac = _conv(ab_ref[0].astype(F32), prev, cw_ref[...], cb_ref[...])
        u_ref[...] = (_gelu(ac) * ab_ref[1].astype(F32)).astype(u_ref.dtype)

    before = tm // GLU_HALO
    return _pcall(
        body, (up, up, cw, cb), name="glu_fwd",
        out_shape=jax.ShapeDtypeStruct((FFN_SLABS, T, UP_SHARD), BF16), grid=(FFN_SLABS, T // tm),
        in_specs=[pl.BlockSpec((2, None, tm, UP_SHARD), lambda d, i: (0, d, i, 0)),
                  pl.BlockSpec((None, None, GLU_HALO, UP_SHARD),
                               lambda d, i: (0, d, jnp.maximum(i * before - 1, 0), 0)),
                  pl.BlockSpec((None, 3, UP_SHARD), lambda d, i: (d, 0, 0)),
                  pl.BlockSpec((None, 1, UP_SHARD), lambda d, i: (d, 0, 0))],
        out_specs=pl.BlockSpec((None, tm, UP_SHARD), lambda d, i: (d, i, 0)),
        sem=("parallel", "parallel"), comm=comm)


def _glu_bwd(up, d_u, cw, cb, S, tm=1024, comm=None):
    T = up.shape[2]
    tm = _tile(S, tm)
    per_seq = S // tm
    n_tiles = T // tm
    per_tile = tm // GLU_HALO

    def body(ab_ref, prev_ref, abn_ref, du_ref, dun_ref, cw_ref, cb_ref, dup_ref, dcw_ref, dcb_ref):
        i = pl.program_id(1)

        @pl.when(i == 0)
        def _():
            dcw_ref[...] = jnp.zeros_like(dcw_ref)
            dcb_ref[...] = jnp.zeros_like(dcb_ref)

        cw, cb = cw_ref[...], cb_ref[...]
        a, b = ab_ref[0].astype(F32), ab_ref[1].astype(F32)
        prev = jnp.where(i % per_seq == 0, 0.0, prev_ref[...].astype(F32))
        a2, a1 = _shift_down(a, prev, 2), _shift_down(a, prev, 1)
        ac = a2 * cw[0:1, :] + a1 * cw[1:2, :] + a * cw[2:3, :] + cb
        du = du_ref[...].astype(F32)
        dup_ref[1] = (du * _gelu(ac)).astype(dup_ref.dtype)
        dac = du * b * _gelu_grad(ac)
        dcb_ref[...] += jnp.sum(dac, axis=0, keepdims=True)
        dcw_ref[0:1, :] += jnp.sum(dac * a2, axis=0, keepdims=True)
        dcw_ref[1:2, :] += jnp.sum(dac * a1, axis=0, keepdims=True)
        dcw_ref[2:3, :] += jnp.sum(dac * a, axis=0, keepdims=True)
        acn = _conv(abn_ref[0].astype(F32), a[tm - GLU_HALO:, :], cw, cb)
        dacn = jnp.where(i % per_seq == per_seq - 1, 0.0,
                         dun_ref[...].astype(F32) * abn_ref[1].astype(F32) * _gelu_grad(acn))
        da = dac * cw[2:3, :] + _shift_up(dac, dacn, 1) * cw[1:2, :] + _shift_up(dac, dacn, 2) * cw[0:1, :]
        dup_ref[0] = da.astype(dup_ref.dtype)

    def nxt(i):
        return jnp.minimum((i + 1) * per_tile, T // GLU_HALO - 1)

    return _pcall(
        body, (up, up, up, d_u, d_u, cw, cb), name="glu_bwd",
        out_shape=(jax.ShapeDtypeStruct((2, FFN_SLABS, T, UP_SHARD), BF16),
                   jax.ShapeDtypeStruct((FFN_SLABS, 3, UP_SHARD), F32),
                   jax.ShapeDtypeStruct((FFN_SLABS, 1, UP_SHARD), F32)),
        grid=(FFN_SLABS, n_tiles),
        in_specs=[pl.BlockSpec((2, None, tm, UP_SHARD), lambda d, i: (0, d, i, 0)),
                  pl.BlockSpec((None, None, GLU_HALO, UP_SHARD),
                               lambda d, i: (0, d, jnp.maximum(i * per_tile - 1, 0), 0)),
                  pl.BlockSpec((2, None, GLU_HALO, UP_SHARD), lambda d, i: (0, d, nxt(i), 0)),
                  pl.BlockSpec((None, tm, UP_SHARD), lambda d, i: (d, i, 0)),
                  pl.BlockSpec((None, GLU_HALO, UP_SHARD), lambda d, i: (d, nxt(i), 0)),
                  pl.BlockSpec((None, 3, UP_SHARD), lambda d, i: (d, 0, 0)),
                  pl.BlockSpec((None, 1, UP_SHARD), lambda d, i: (d, 0, 0))],
        out_specs=(pl.BlockSpec((2, None, tm, UP_SHARD), lambda d, i: (0, d, i, 0)),
                   pl.BlockSpec((None, 3, UP_SHARD), lambda d, i: (d, 0, 0)),
                   pl.BlockSpec((None, 1, UP_SHARD), lambda d, i: (d, 0, 0))),
        sem=("parallel", "arbitrary"), comm=comm)


def _mm_up(h2, w_up_t, tm=MM_ROWS, comm=None):
    T, K = h2.shape
    tm = _tile(T, tm)
    return _matmul(
        "mm_up", "nt", h2, w_up_t, jax.ShapeDtypeStruct((N_DEV, T, UP_SHARD), BF16), (N_DEV, T // tm, 1),
        pl.BlockSpec((tm, K), lambda j, i, k: (i, 0)), pl.BlockSpec((None, UP_SHARD, K), lambda j, i, k: (j, 0, 0)),
        pl.BlockSpec((None, tm, UP_SHARD), lambda j, i, k: (j, i, 0)), (tm, UP_SHARD), comm=comm)


def _loss_epilogue(ffn, operands, outputs, first):
    x1_ref, t_ref, g_ref = operands
    dx_ref, dg_ref, loss_ref = outputs

    @pl.when(first)
    def _():
        dg_ref[...] = jnp.zeros_like(dg_ref)
        loss_ref[...] = jnp.zeros_like(loss_ref)

    xv = x1_ref[...] + ffn
    r = lax.rsqrt(jnp.mean(xv * xv, axis=-1, keepdims=True) + EPS)
    xhat = xv * r
    err = xhat * g_ref[...] - t_ref[...]
    loss_ref[...] += (0.5 / D_MODEL) * jnp.sum(err * err)
    dy = err * (1.0 / D_MODEL)
    dg_ref[...] += jnp.sum(dy * xhat, axis=0, keepdims=True)
    dxhat = dy * g_ref[...]
    dx_ref[...] = r * (dxhat - xhat * jnp.mean(dxhat * xhat, axis=-1, keepdims=True))


def _mm_down_loss(u, w_down, x1, target, g_final, tm=MM_ROWS_RES):
    J, T, n = u.shape
    tm = _tile(T, tm)
    row = pl.BlockSpec((tm, D_MODEL), lambda i, d: (i, 0))
    vec = pl.BlockSpec((1, D_MODEL), lambda i, d: (0, 0))
    vec_shape = jax.ShapeDtypeStruct((1, D_MODEL), F32)
    return _matmul(
        "mm_down", "nn", u, w_down, (jax.ShapeDtypeStruct((T, D_MODEL), F32), vec_shape, vec_shape), (T // tm, J),
        pl.BlockSpec((None, tm, n), lambda i, d: (d, i, 0)), pl.BlockSpec((None, n, D_MODEL), lambda i, d: (d, 0, 0)),
        (row, vec, vec), (tm, D_MODEL), [x1, target, g_final], [row, row, vec], epilogue=_loss_epilogue)


def _mm_down_t(dx, w_down, tm=MM_ROWS):
    T = dx.shape[0]
    J, n, _ = w_down.shape
    tm = _tile(T, tm)
    return _matmul(
        "mm_down_t", "nt", dx, w_down, jax.ShapeDtypeStruct((J, T, n), BF16), (J, T // tm, 1),
        pl.BlockSpec((tm, D_MODEL), lambda d, i, k: (i, 0)), pl.BlockSpec((None, n, D_MODEL), lambda d, i, k: (d, 0, 0)),
        pl.BlockSpec((None, tm, n), lambda d, i, k: (d, i, 0)), (tm, n))


def _mm_dw_down(u, dx, tk=MM_TOKENS):
    J, T, n = u.shape
    tk = _tile(T, tk)
    return _matmul(
        "mm_dw_down", "tn", u, dx, jax.ShapeDtypeStruct((J, n, D_MODEL), BF16), (J, T // tk),
        pl.BlockSpec((None, tk, n), lambda d, k: (d, k, 0)), pl.BlockSpec((tk, D_MODEL), lambda d, k: (k, 0)),
        pl.BlockSpec((None, n, D_MODEL), lambda d, k: (d, 0, 0)), (n, D_MODEL))


def _mm_dw_up(h2, d_up, tk=MM_TOKENS):
    T, K = h2.shape
    tk = _tile(T, tk)
    return _matmul(
        "mm_dw_up", "tn", d_up, h2, jax.ShapeDtypeStruct((N_DEV, UP_SHARD, K), BF16), (N_DEV, T // tk),
        pl.BlockSpec((None, tk, UP_SHARD), lambda j, k: (j, k, 0)), pl.BlockSpec((tk, K), lambda j, k: (k, 0)),
        pl.BlockSpec((None, UP_SHARD, K), lambda j, k: (j, 0, 0)), (UP_SHARD, K))


def _mm_up_t(d_up, w_up_t, rms, tm=MM_ROWS_RES, comm=None):
    J, T, n = d_up.shape
    K = w_up_t.shape[2]
    tm = _tile(T, tm)
    fused = _rms_bwd_fused(T, K, tm, rms)
    return _matmul(
        "mm_up_t", "nn", d_up, w_up_t, fused.pop("out_shape"), (T // tm, J),
        pl.BlockSpec((None, tm, n), lambda i, j: (j, i, 0)), pl.BlockSpec((None, n, K), lambda i, j: (j, 0, 0)),
        fused.pop("o_spec"), (tm, K), comm=comm, **fused)


def _cast_shards(shards):
    def body(*refs):
        n = len(refs) // 2
        for src, dst in zip(refs[:n], refs[n:]):
            dst[...] = src[...].astype(dst.dtype)

    return pl.pallas_call(
        body, out_shape=[jax.ShapeDtypeStruct(s.shape, BF16) for s in shards], name="cast_shards",
        compiler_params=pltpu.CompilerParams(vmem_limit_bytes=VMEM_LIMIT),
    )(*shards)


def _adamw(w, g, m, v):
    m = ADAM_B1 * m + (1.0 - ADAM_B1) * g
    v = ADAM_B2 * v + (1.0 - ADAM_B2) * (g * g)
    m_hat = m / (1.0 - ADAM_B1 ** ADAM_STEP)
    v_hat = v / (1.0 - ADAM_B2 ** ADAM_STEP)
    delta = -ADAM_LR * (m_hat / (jnp.sqrt(v_hat) + ADAM_EPS) + ADAM_WD * w)
    return delta, m, v


def _sum_parts(p_ref):
    g = p_ref[0].astype(F32)
    for d in range(1, N_DEV):
        g = g + p_ref[d].astype(F32)
    return g


ADAM_ROWS = 512


def _reduce_adam(name, parts, w, m, v):
    R, Cn = w.shape
    by_rows = sum(p.shape[1] for p in parts) == R and len(parts) > 1
    common = math.gcd(*[p.shape[1] for p in parts])
    tr = max(t for t in range(8, min(common, ADAM_ROWS) + 1, 8) if common % t == 0)
    n_tiles = [p.shape[1] // tr for p in parts]
    first = [sum(n_tiles[:j]) for j in range(len(parts))] if by_rows else [0] * len(parts)

    def body(*refs):
        p_refs = refs[:len(parts)]
        w_ref, m_ref, v_ref, g_out, d_out, m_out, v_out = refs[len(parts):]

        def update(p_ref):
            g = _sum_parts(p_ref)
            delta, m_new, v_new = _adamw(w_ref[...], g, m_ref[...], v_ref[...])
            g_out[...] = g
            d_out[...] = delta
            m_out[...] = m_new
            v_out[...] = v_new

        if len(parts) == 1:
            update(p_refs[0])
        elif by_rows:
            i = pl.program_id(0)
            for p_ref, t0, n in zip(p_refs, first, n_tiles):
                pl.when((i >= t0) & (i < t0 + n))(functools.partial(update, p_ref))
        else:
            c = lax.axis_index("c")
            for side, p_ref in enumerate(p_refs):
                pl.when(c == side)(functools.partial(update, p_ref))

    def part_spec(t0, n):
        return pl.BlockSpec((N_DEV, tr, Cn), lambda i: (0, jnp.clip(i - t0, 0, n - 1), 0))

    row = pl.BlockSpec((tr, Cn), lambda i: (i, 0))
    shape = jax.ShapeDtypeStruct((R, Cn), F32)
    return pl.pallas_call(
        body, out_shape=(shape,) * 4, grid=(R // tr,),
        in_specs=[part_spec(t0, n) for t0, n in zip(first, n_tiles)] + [row, row, row],
        out_specs=(row,) * 4, name=name, compiler_params=_params(("parallel",)),
    )(*parts, w, m, v)


def _small_adam(name, gathered, params):
    n_g, n_p = len(gathered), len(params)

    def body(*refs):
        g_refs = refs[:n_g]
        wmv = refs[n_g:n_g + 3 * n_p]
        sums = refs[n_g + 3 * n_p:2 * n_g + 3 * n_p]
        upd = refs[2 * n_g + 3 * n_p:]
        for j in range(n_g):
            g = _sum_parts(g_refs[j])
            sums[j][...] = g
            if j < n_p:
                w_ref, m_ref, v_ref = wmv[3 * j:3 * j + 3]
                delta, m_new, v_new = _adamw(w_ref[...], g, m_ref[...], v_ref[...])
                upd[3 * j][...] = delta
                upd[3 * j + 1][...] = m_new
                upd[3 * j + 2][...] = v_new

    flat = [a for wmv in params for a in wmv]
    out_shape = [jax.ShapeDtypeStruct(g.shape[1:], F32) for g in gathered]
    out_shape += [jax.ShapeDtypeStruct(a.shape, F32) for a in flat]
    res = pl.pallas_call(body, out_shape=out_shape, name=name)(*gathered, *flat)
    return res[:n_g], [tuple(res[n_g + 3 * j:n_g + 3 * j + 3]) for j in range(n_p)]


def _adam_only(name, g, w, m, v):
    def body(g_ref, w_ref, m_ref, v_ref, d_out, m_out, v_out):
        delta, m_new, v_new = _adamw(w_ref[...], g_ref[...], m_ref[...], v_ref[...])
        d_out[...] = delta
        m_out[...] = m_new
        v_out[...] = v_new

    shape = jax.ShapeDtypeStruct(w.shape, F32)
    return pl.pallas_call(body, out_shape=(shape,) * 3, name=name)(g, w, m, v)


def kernel(x, mem, g_mix, w_in, w_pool, pool_scale, w_a, g_ret, b_ret, w_r, g_mem, w_mem_kv, w_c, w_out, g_ffn, w_up, conv_w, conv_b, w_down, g_final, loss_target, m_g_mix, m_w_in, m_w_pool, m_pool_scale, m_w_a, m_g_ret, m_b_ret, m_w_r, m_g_mem, m_w_mem_kv, m_w_c, m_w_out, m_g_ffn, m_w_up, m_conv_w, m_conv_b, m_w_down, m_g_final, v_g_mix, v_w_in, v_w_pool, v_pool_scale, v_w_a, v_g_ret, v_b_ret, v_w_r, v_g_mem, v_w_mem_kv, v_w_c, v_w_out, v_g_ffn, v_w_up, v_conv_w, v_conv_b, v_w_down, v_g_final):
    B, S, _ = x.shape
    M = mem.shape[1]
    T = B * S
    me = _my_index()
    x2d = x.reshape(T, D_MODEL)
    mem2d = mem.reshape(B * M, D_MODEL)
    tgt2d = loss_target.reshape(T, D_MODEL)
    g_final2 = g_final.reshape(1, D_MODEL)

    big = dict(w_in=w_in[0], w_a=w_a[0], w_r=w_r[0], w_mem_kv=w_mem_kv[0], w_c=w_c[0], w_out=w_out[0],
               w_up=w_up[0].T, w_down=w_down[0])
    names = list(big)
    cast = dict(zip(names, _cast_shards([big[n] for n in names])))
    cb = conv_b[0].reshape(FFN_SLABS, 1, UP_SHARD)
    wp = w_pool[0]
    tables = _ret_tables(S)

    h = _rms_fwd("rms_mix", x2d, g_mix)
    early = ("w_a", "w_r", "w_mem_kv", "w_c", "w_out")
    (proj, Win), landed = _mm_in_gather(h, cast["w_in"], comm=_Gather([cast[n] for n in early] + [conv_w[0]]))
    W = dict(zip(early, landed))
    half = D_MODEL // 2
    (yr, ret_states), (Wup_lo,) = _ret_fwd(proj, g_ret, b_ret, tables, B, S,
                                           comm=_Gather([cast["w_up"][:, :half]]))
    cw_full = landed[-1].transpose(1, 0, 2).reshape(3, FFN_HIDDEN)
    cw = cw_full.reshape(3, FFN_SLABS, UP_SHARD).transpose(1, 0, 2)
    Wa = W["w_a"].transpose(1, 0, 2).reshape(POOL_WIDTH, D_MODEL)
    Wc = W["w_c"].transpose(1, 0, 2).reshape(XA_WIDTH, D_MODEL)
    Wr = W["w_r"].reshape(D_MODEL, D_MODEL)
    Wkv = W["w_mem_kv"].reshape(D_MODEL, D_MODEL)
    Wout = W["w_out"].reshape(D_MODEL, D_MODEL)
    ypre = _pool_fwd(proj, wp, pool_scale, B, S)
    y_pool = _mm_rows("mm_a", ypre, Wa, BF16)
    y_ret = _mm_rows("mm_r", yr, Wr, BF16)
    mem_n = _rms_fwd("rms_mem", mem2d, g_mem)
    kv = _mm_rows("mm_kv", mem_n, Wkv)
    o_mem = _xa_fwd(proj, kv, B, S, M)[0]
    y_mem = _mm_rows("mm_c", o_mem, Wc, BF16)
    ys = (y_pool, y_ret, y_mem)
    (merged, x1, h2), (Wup_hi,) = _merge_out(proj, ys, Wout, x2d, g_ffn, comm=_Gather([cast["w_up"][:, half:]]))
    Wup = jnp.concatenate([Wup_lo, Wup_hi], axis=2)
    up, (Wdown,) = _mm_up(h2, Wup, comm=_Gather([cast["w_down"]]))
    up = up.reshape(2, FFN_SLABS, T, UP_SHARD)
    Wdown = Wdown.reshape(FFN_SLABS, UP_SHARD, D_MODEL)
    u = _glu_fwd(up, cw, cb, S)[0]

    dx2, dg_final, loss_part = _mm_down_loss(u, Wdown, x1, tgt2d, g_final2)
    received = {}
    d_u = _mm_down_t(dx2, Wdown)
    dW_down = _mm_dw_down(u, dx2)
    (d_up, d_cw, d_cb), (received["w_down"],) = _glu_bwd(
        up, d_u, cw, cb, S, comm=_Exchange([dW_down.reshape(N_DEV, -1, D_MODEL)]))
    d_up = d_up.reshape(N_DEV, T, UP_SHARD)
    dW_up = _mm_dw_up(h2, d_up)
    (dx1, dg_ffn), (up_c0,) = _mm_up_t(d_up, Wup, (x1, g_ffn, dx2), comm=_ExchangeTo([dW_up], 0))
    dW_out = _mm_tn("mm_dw_out", merged, dx1, BF16)
    (d_gl, d_y_pool, d_y_ret, d_y_mem), (up_c1,) = _merge_bwd(proj, ys, dx1, Wout, comm=_ExchangeTo([dW_up], 1))
    received["w_up"] = [up_c0, up_c1]
    dW_c = _mm_tn("mm_dw_c", o_mem, d_y_mem, BF16)
    d_o_mem = _mm_rows("mm_c_t", d_y_mem, Wc, kind="nt")
    (d_qx, d_kmem, d_vmem), (received["w_out"],) = _xa_bwd(
        proj, kv, d_o_mem, B, S, M, comm=_Exchange([dW_out.reshape(N_DEV, -1, D_MODEL)]))
    d_kv = jnp.concatenate([d_kmem, d_vmem], axis=1)
    dW_kv = _mm_tn("mm_dw_kv", mem_n, d_kv, BF16)
    d_mem_n = _mm_rows("mm_kv_t", d_kv, Wkv, kind="nt")
    dg_mem = _rms_bwd("rms_mem_bwd", mem2d, g_mem, d_mem_n, None)
    dW_a = _mm_tn("mm_dw_a", ypre, d_y_pool, BF16)
    d_ypre = _mm_rows("mm_a_t", d_y_pool, Wa, kind="nt")
    (d_hp, dw_pool, d_scale), (received["w_a"],) = _pool_bwd(
        proj, d_ypre, wp, pool_scale, B, S,
        comm=_Exchange([dW_a.reshape(POOL_WIDTH, N_DEV, -1).transpose(1, 0, 2)]))
    dW_r = _mm_tn("mm_dw_r", yr, d_y_ret, BF16)
    d_yr = _mm_rows("mm_r_t", d_y_ret, Wr, kind="nt")
    (d_q, d_k, d_v, d_gr, dg_ret, db_ret), landed = _ret_bwd(
        proj, ret_states, d_yr, g_ret, b_ret, tables, B, S,
        comm=_Exchange([dW_r.reshape(N_DEV, -1, D_MODEL), dW_c.reshape(XA_WIDTH, N_DEV, -1).transpose(1, 0, 2),
                        dW_kv.reshape(N_DEV, -1, D_MODEL)]))
    received["w_r"], received["w_c"], received["w_mem_kv"] = landed
    small_names = ["w_pool", "pool_scale", "g_ret", "b_ret", "g_mem", "g_ffn", "conv_b", "g_final"]
    small_grads = [dw_pool, d_scale, dg_ret, db_ret, dg_mem, dg_ffn, d_cb.reshape(1, FFN_HIDDEN), dg_final,
                   d_cw.transpose(1, 0, 2).reshape(3, FFN_HIDDEN), loss_part]
    d_proj = jnp.concatenate([d_hp, d_q, d_k, d_v, d_gr, d_qx, d_gl], axis=1)
    dW_in0, small_all = _mm_tn_slab("mm_dw_in0", h[:, :W_IN_FIRST_ROWS], d_proj, IN_SHARD, BF16,
                                    comm=_Exchange([], whole=small_grads))
    dW_in1, (in0,) = _mm_tn_slab("mm_dw_in1", h[:, W_IN_FIRST_ROWS:], d_proj, IN_SHARD, BF16,
                                 comm=_Exchange([dW_in0]))
    (grad_x, dg_mix), (in1,) = _mm_cols_slab_t("mm_in_t", d_proj, Win, (x2d, g_mix, dx1), comm=_Exchange([dW_in1]))
    received["w_in"] = [in0, in1]
    (g_mix_all,) = _comm_call("gather_g_mix", _Exchange([], whole=[dg_mix]))

    args = dict(g_mix=g_mix, w_in=w_in, w_pool=w_pool, pool_scale=pool_scale, w_a=w_a, g_ret=g_ret, b_ret=b_ret,
                w_r=w_r, g_mem=g_mem, w_mem_kv=w_mem_kv, w_c=w_c, w_out=w_out, g_ffn=g_ffn, w_up=w_up,
                conv_w=conv_w, conv_b=conv_b, w_down=w_down, g_final=g_final)
    m_in = dict(g_mix=m_g_mix, w_in=m_w_in, w_pool=m_w_pool, pool_scale=m_pool_scale, w_a=m_w_a, g_ret=m_g_ret,
                b_ret=m_b_ret, w_r=m_w_r, g_mem=m_g_mem, w_mem_kv=m_w_mem_kv, w_c=m_w_c, w_out=m_w_out,
                g_ffn=m_g_ffn, w_up=m_w_up, conv_w=m_conv_w, conv_b=m_conv_b, w_down=m_w_down, g_final=m_g_final)
    v_in = dict(g_mix=v_g_mix, w_in=v_w_in, w_pool=v_w_pool, pool_scale=v_pool_scale, w_a=v_w_a, g_ret=v_g_ret,
                b_ret=v_b_ret, w_r=v_w_r, g_mem=v_g_mem, w_mem_kv=v_w_mem_kv, w_c=v_w_c, w_out=v_w_out,
                g_ffn=v_g_ffn, w_up=v_w_up, conv_w=v_conv_w, conv_b=v_conv_b, w_down=v_w_down, g_final=v_g_final)

    grads, deltas, new_m, new_v = {}, {}, {}, {}
    for n in names:
        parts = received[n] if isinstance(received[n], list) else [received[n]]
        flip = (lambda a: a.T) if n == "w_up" else (lambda a: a)
        outs = _reduce_adam("adam_" + n, parts, big[n], flip(m_in[n][0]), flip(v_in[n][0]))
        for store, val in zip((grads, deltas, new_m, new_v), outs):
            store[n] = flip(val)[None]

    def as_small(a):
        return a.reshape(a.shape[-3:]) if a.ndim > 2 else a.reshape(1, -1)

    def small_update(call_name, param_names, gathered):
        params = [tuple(as_small(d[n]) for d in (args, m_in, v_in)) for n in param_names]
        sums, updates = _small_adam(call_name, gathered, params)
        for n, g, (d_, m_, v_) in zip(param_names, sums, updates):
            shape = args[n].shape
            grads[n], deltas[n], new_m[n], new_v[n] = (a.reshape(shape) for a in (g, d_, m_, v_))
        return sums[len(param_names):]

    g_cw_full, loss_row = small_update("adam_small", small_names, small_all)
    loss = loss_row[0, 0]
    small_update("adam_g_mix", ["g_mix"], [g_mix_all])

    shard_cols = FFN_HIDDEN // N_DEV
    g_cw = lax.dynamic_slice_in_dim(g_cw_full, me * shard_cols, shard_cols, axis=1)
    d_, m_, v_ = _adam_only("adam_conv_w", g_cw, conv_w[0], m_conv_w[0], v_conv_w[0])
    grads["conv_w"], deltas["conv_w"], new_m["conv_w"], new_v["conv_w"] = g_cw[None], d_[None], m_[None], v_[None]

    order = ["g_mix", "w_in", "w_pool", "pool_scale", "w_a", "g_ret", "b_ret", "w_r", "g_mem", "w_mem_kv", "w_c",
             "w_out", "g_ffn", "w_up", "conv_w", "conv_b", "w_down", "g_final"]
    return (loss, grad_x.reshape(B, S, D_MODEL), *[grads[n] for n in order], *[deltas[n] for n in order],
            *[new_m[n] for n in order], *[new_v[n] for n in order])
```

```python
import functools
import math

import jax
import jax.numpy as jnp
from jax import lax
from jax.experimental import pallas as pl
from jax.experimental.pallas import tpu as pltpu

F32 = jnp.float32
BF16 = jnp.bfloat16

N_DEV = 8
D_MODEL = 1024
POOL_WINDOWS = (2, 4, 8, 16)
POOL_GROUP_DIM = 128
POOL_WIDTH = 512
POOL_HALO = 16
RET_HEADS = 4
RET_QK_DIM = 128
RET_V_DIM = 256
RET_CHUNK = 128
ROPE_BASE = 10000.0
XA_HEADS = 4
XA_HEAD_DIM = 128
XA_WIDTH = 512
IN_WIDTH = 7168
IN_SHARD = IN_WIDTH // N_DEV
FFN_HIDDEN = 2816
UP_SHARD = 2 * FFN_HIDDEN // N_DEV
FFN_SLABS = FFN_HIDDEN // UP_SHARD
EPS = 1e-6
ADAM_LR = 0.001
ADAM_B1 = 0.9
ADAM_B2 = 0.999
ADAM_EPS = 1e-08
ADAM_WD = 0.01
ADAM_STEP = 10
GELU_C = math.sqrt(2.0 / math.pi)
GELU_A = 0.044715
VMEM_LIMIT = 56 * 1024 * 1024
MM_ROWS = 2048
MM_ROWS_RES = 1024
MM_TOKENS = 2048
W_IN_FIRST_ROWS = 384
MESH = pl.DeviceIdType.MESH

COL_Q, COL_K, COL_V, COL_GR, COL_QX, COL_GL = 512, 1024, 1536, 2560, 3584, 4096

_DIMS = {
    "nn": (((1,), (0,)), ((), ())),
    "nt": (((1,), (1,)), ((), ())),
    "tn": (((0,), (0,)), ((), ())),
}


def _dot(a, b, kind="nn"):
    return lax.dot_general(a.astype(BF16), b.astype(BF16), _DIMS[kind], preferred_element_type=F32)


def _params(sem, vmem=VMEM_LIMIT):
    return pltpu.CompilerParams(dimension_semantics=sem, vmem_limit_bytes=vmem)


def _tile(n, pref):
    t = min(n, pref)
    while n % t:
        t //= 2
    return t


def _mesh_pos():
    return lax.axis_index("x"), lax.axis_index("y"), lax.axis_index("c")


def _dev_index(x, y, c):
    return 4 * x + 2 * y + c


def _my_index():
    return _dev_index(*_mesh_pos())


def _remote(src, dst, send_sems, recv_sems, s, to):
    return pltpu.make_async_remote_copy(src_ref=src, dst_ref=dst, send_sem=send_sems.at[s], recv_sem=recv_sems.at[s],
                                        device_id=to, device_id_type=MESH)


class _Gather:
    def __init__(self, shards):
        self.inputs = list(shards)
        self.out_shapes = [jax.ShapeDtypeStruct((N_DEV,) + s.shape, s.dtype) for s in shards]
        n = len(shards)
        self.sem_shapes = [pltpu.SemaphoreType.DMA((7 * n,)), pltpu.SemaphoreType.DMA((7 * n,)),
                           pltpu.SemaphoreType.DMA((n,))]

    def _places(self):
        x, y, c = _mesh_pos()
        return (x, y, c), (x, y, 1 - c), [(1 - x, y), (x, 1 - y), (1 - x, 1 - y)]

    def _local(self, src, dst, sems):
        me = _my_index()
        return [pltpu.make_async_copy(src[w], dst[w].at[me], sems[2].at[w]) for w in range(len(src))]

    def start(self, src, dst, sems):
        me, sib, chips = self._places()
        for cp in self._local(src, dst, sems):
            cp.start()
        for w in range(len(src)):
            land = dst[w].at[_dev_index(*me)]
            _remote(src[w], land, sems[0], sems[1], 7 * w, sib).start()
            for j, chip in enumerate(chips):
                _remote(src[w], land, sems[0], sems[1], 7 * w + 1 + j, (*chip, me[2])).start()

    def middle(self, src, dst, sems):
        me, sib, chips = self._places()
        for j, chip in enumerate(chips):
            for w in range(len(src)):
                block = dst[w].at[_dev_index(*chip, me[2])]
                _remote(src[w], block, sems[0], sems[1], 7 * w + 1 + j, me).wait_recv()
                _remote(block, block, sems[0], sems[1], 7 * w + 4 + j, sib).start()

    def finish(self, src, dst, sems):
        me, sib, chips = self._places()
        n = len(src)
        for w in range(n):
            _remote(src[w], dst[w].at[_dev_index(*sib)], sems[0], sems[1], 7 * w, me).wait_recv()
            for j, chip in enumerate(chips):
                block = dst[w].at[_dev_index(*chip, sib[2])]
                _remote(block, block, sems[0], sems[1], 7 * w + 4 + j, me).wait_recv()
            for k in range(7):
                _remote(src[w], dst[w].at[0], sems[0], sems[1], 7 * w + k, me).wait_send()
        for cp in self._local(src, dst, sems):
            cp.wait()


class _Exchange:
    def __init__(self, partials, whole=()):
        self.n_part = len(partials)
        self.inputs = list(partials) + list(whole)
        self.out_shapes = [jax.ShapeDtypeStruct(p.shape, p.dtype) for p in partials]
        self.out_shapes += [jax.ShapeDtypeStruct((N_DEV,) + a.shape, a.dtype) for a in whole]
        n = len(self.inputs)
        self.sem_shapes = [pltpu.SemaphoreType.DMA((7 * n,)), pltpu.SemaphoreType.DMA((7 * n,)),
                           pltpu.SemaphoreType.DMA((n,))]

    def _peer(self, k):
        x, y, c = _mesh_pos()
        p = (x ^ ((k >> 2) & 1), y ^ ((k >> 1) & 1), c ^ (k & 1))
        return p, _dev_index(*p)

    def _source(self, src, w, slot):
        return src[w].at[slot] if w < self.n_part else src[w]

    def _local(self, src, dst, sems):
        me = _my_index()
        return [pltpu.make_async_copy(self._source(src, w, me), dst[w].at[me], sems[2].at[w])
                for w in range(len(src))]

    def start(self, src, dst, sems):
        me = _my_index()
        for cp in self._local(src, dst, sems):
            cp.start()
        for k in range(1, N_DEV):
            peer, peer_idx = self._peer(k)
            for w in range(len(src)):
                _remote(self._source(src, w, peer_idx), dst[w].at[me], sems[0], sems[1], 7 * w + k - 1, peer).start()

    def finish(self, src, dst, sems):
        for k in range(1, N_DEV):
            peer, peer_idx = self._peer(k)
            for w in range(len(src)):
                cp = _remote(self._source(src, w, peer_idx), dst[w].at[peer_idx], sems[0], sems[1], 7 * w + k - 1, peer)
                cp.wait_send()
                cp.wait_recv()
        for cp in self._local(src, dst, sems):
            cp.wait()


class _ExchangeTo:
    def __init__(self, partials, side):
        self.side = side
        self.inputs = list(partials)
        self.out_shapes = [jax.ShapeDtypeStruct(p.shape, p.dtype) for p in partials]
        n = len(partials)
        self.sem_shapes = [pltpu.SemaphoreType.DMA((7 * n,)), pltpu.SemaphoreType.DMA((7 * n,)),
                           pltpu.SemaphoreType.DMA((n,))]

    def _copies(self, src, dst, sems):
        x, y, c = _mesh_pos()
        me = _dev_index(x, y, c)
        receives = c == self.side
        remote = []
        for k in range(1, N_DEV):
            kx, ky, kc = (k >> 2) & 1, (k >> 1) & 1, k & 1
            peer = (x ^ kx, y ^ ky, c ^ kc)
            peer_idx = _dev_index(*peer)
            sends = c == (self.side ^ kc)
            for w in range(len(src)):
                slab = src[w].at[peer_idx]
                s = 7 * w + k - 1
                remote.append((sends, _remote(slab, dst[w].at[me], sems[0], sems[1], s, peer),
                               _remote(slab, dst[w].at[peer_idx], sems[0], sems[1], s, peer)))
        local = [pltpu.make_async_copy(src[w].at[me], dst[w].at[me], sems[2].at[w]) for w in range(len(src))]
        return receives, remote, local

    def start(self, src, dst, sems):
        receives, remote, local = self._copies(src, dst, sems)

        @pl.when(receives)
        def _():
            for cp in local:
                cp.start()

        for sends, send, _ in remote:
            pl.when(sends)(send.start)

    def finish(self, src, dst, sems):
        receives, remote, local = self._copies(src, dst, sems)
        for sends, send, arrive in remote:
            pl.when(sends)(send.wait_send)
            pl.when(receives)(arrive.wait_recv)

        @pl.when(receives)
        def _():
            for cp in local:
                cp.wait()


def _pcall(body, args, *, name, out_shape, grid, in_specs, out_specs, scratch_shapes=(), sem=None, comm=None):
    single = not isinstance(out_shape, (tuple, list))
    outs = [out_shape] if single else list(out_shape)
    ospecs = [out_specs] if single else list(out_specs)
    n_in, n_out, n_scr = len(args), len(outs), len(scratch_shapes)

    def pick(res):
        return res[0] if single else tuple(res[:n_out])

    if comm is None:
        res = pl.pallas_call(
            body, out_shape=outs, grid=grid, in_specs=list(in_specs), out_specs=ospecs,
            scratch_shapes=list(scratch_shapes), name=name, compiler_params=_params(sem),
        )(*args)
        return pick(res), ()

    nci, nco = len(comm.inputs), len(comm.out_shapes)

    def carrier(*refs):
        at = 0
        parts = []
        for size in (n_in, nci, n_out, nco, n_scr, len(comm.sem_shapes)):
            parts.append(refs[at:at + size])
            at += size
        ins, cins, o, couts, scr, sems = parts
        ids = [pl.program_id(a) for a in range(len(grid))]
        first = functools.reduce(jnp.logical_and, [i == 0 for i in ids])
        last = functools.reduce(jnp.logical_and, [i == g - 1 for i, g in zip(ids, grid)])

        body(*ins, *o, *scr)

        @pl.when(first)
        def _():
            comm.start(cins, couts, sems)

        if hasattr(comm, "middle"):
            steps = math.prod(grid)
            at = functools.reduce(lambda lin, ig: lin * ig[1] + ig[0], zip(ids, grid), 0)

            @pl.when(at == min(steps - 1, (3 * steps) // 4))
            def _():
                comm.middle(cins, couts, sems)

        @pl.when(last)
        def _():
            comm.finish(cins, couts, sems)

    hbm = pl.BlockSpec(memory_space=pltpu.HBM)
    res = pl.pallas_call(
        carrier, out_shape=outs + comm.out_shapes, grid=grid, in_specs=list(in_specs) + [hbm] * nci,
        out_specs=ospecs + [hbm] * nco, scratch_shapes=list(scratch_shapes) + comm.sem_shapes, name=name,
        compiler_params=_params(("arbitrary",) * len(grid)),
    )(*args, *comm.inputs)
    return pick(res), tuple(res[n_out:])


def _comm_call(name, comm):
    def body(*refs):
        nci, nco = len(comm.inputs), len(comm.out_shapes)
        cins, couts, sems = refs[:nci], refs[nci:nci + nco], refs[nci + nco:]
        comm.start(cins, couts, sems)
        if hasattr(comm, "middle"):
            comm.middle(cins, couts, sems)
        comm.finish(cins, couts, sems)

    hbm = pl.BlockSpec(memory_space=pltpu.HBM)
    return pl.pallas_call(
        body, out_shape=comm.out_shapes, in_specs=[hbm] * len(comm.inputs), out_specs=[hbm] * len(comm.out_shapes),
        scratch_shapes=comm.sem_shapes, name=name,
    )(*comm.inputs)


def _matmul(name, kind, a, b, out_shape, grid, a_spec, b_spec, o_spec, acc_shape, res=None, res_spec=None,
            comm=None, epilogue=None):
    nk = grid[-1]
    if epilogue is None:
        extra, extra_specs = ([res], [res_spec]) if res is not None else ([], [])
        n_out = 1
    else:
        extra, extra_specs, n_out = list(res), list(res_spec), len(out_shape)
    n_in = 2 + len(extra)

    def body(*refs):
        a_ref, b_ref = refs[0], refs[1]
        extra_refs, out_refs = refs[2:n_in], refs[n_in:n_in + n_out]

        def prod():
            return _dot(a_ref[...], b_ref[...], kind)

        def finish(acc):
            if epilogue is not None:
                ids = [pl.program_id(ax) for ax in range(len(grid) - 1)]
                first = functools.reduce(jnp.logical_and, [i == 0 for i in ids]) if ids else True
                epilogue(acc, extra_refs, out_refs, first)
                return
            if extra_refs:
                acc = acc + extra_refs[0][...]
            out_refs[0][...] = acc.astype(out_refs[0].dtype)

        if nk == 1:
            finish(prod())
        else:
            acc_ref = refs[n_in + n_out]
            k = pl.program_id(len(grid) - 1)

            @pl.when(k == 0)
            def _():
                acc_ref[...] = prod()

            @pl.when(k > 0)
            def _():
                acc_ref[...] += prod()

            @pl.when(k == nk - 1)
            def _():
                finish(acc_ref[...])

    in_specs = [a_spec, b_spec] + extra_specs
    args = (a, b, *extra)
    scratch = [pltpu.VMEM(acc_shape, F32)] if nk > 1 else []
    sem = ("arbitrary",) * len(grid) if epilogue is not None else ("parallel",) * (len(grid) - 1) + ("arbitrary",)
    out, landed = _pcall(body, args, name=name, out_shape=out_shape, grid=grid, in_specs=in_specs,
                         out_specs=o_spec, scratch_shapes=scratch, sem=sem, comm=comm)
    return out if comm is None else (out, landed)


def _mm_rows(name, a, w, out_dtype=F32, res=None, kind="nn", tm=MM_ROWS, comm=None):
    M, K = a.shape
    N = w.shape[1] if kind == "nn" else w.shape[0]
    tm = _tile(M, tm)
    res_spec = pl.BlockSpec((tm, N), lambda i, k: (i, 0)) if res is not None else None
    return _matmul(
        name, kind, a, w, jax.ShapeDtypeStruct((M, N), out_dtype), (M // tm, 1),
        pl.BlockSpec((tm, K), lambda i, k: (i, 0)), pl.BlockSpec(w.shape, lambda i, k: (0, 0)),
        pl.BlockSpec((tm, N), lambda i, k: (i, 0)), (tm, N), res, res_spec, comm)


def _mm_tn(name, a, b, out_dtype=F32, tk=MM_TOKENS, comm=None):
    T, M = a.shape
    N = b.shape[1]
    tk = _tile(T, tk)
    return _matmul(
        name, "tn", a, b, jax.ShapeDtypeStruct((M, N), out_dtype), (1, T // tk),
        pl.BlockSpec((tk, M), lambda i, k: (k, 0)), pl.BlockSpec((tk, N), lambda i, k: (k, 0)),
        pl.BlockSpec((M, N), lambda i, k: (0, 0)), (M, N), comm=comm)


def _mm_in_gather(h, shard, tm=MM_ROWS, comm=None):
    T, K = h.shape
    n = shard.shape[1]
    tm = _tile(T, tm)
    n_tiles = T // tm
    pair_of_chip_step = {4: 1, 2: 2, 6: 3}

    def slab_of(s):
        x, y, c = _mesh_pos()
        return _dev_index(x ^ ((s >> 2) & 1), y ^ ((s >> 1) & 1), c ^ (s & 1))

    def body(h_ref, shard_ref, proj_ref, win_ref, wbuf, slot_sems, send_sems, recv_sems, local_sem):
        s, i = pl.program_id(0), pl.program_id(1)
        x, y, c = _mesh_pos()
        me, sib = (x, y, c), (x, y, 1 - c)

        def slot_copy(step):
            src = shard_ref if step == 0 else win_ref.at[slab_of(step)]
            return pltpu.make_async_copy(src, wbuf.at[step % 2], slot_sems.at[step % 2])

        def fetch(step):
            if step >= 1:
                block = win_ref.at[slab_of(step)]
                if step == 1:
                    pair = 0
                elif step % 2 == 0:
                    pair = pair_of_chip_step[step]
                else:
                    pair = 3 + pair_of_chip_step[step - 1]
                _remote(block, block, send_sems, recv_sems, pair, me).wait_recv()
                if step % 2 == 0:
                    _remote(block, block, send_sems, recv_sems, 3 + pair, sib).start()
            slot_copy(step).start()

        @pl.when((s == 0) & (i == 0))
        def _():
            land = win_ref.at[_dev_index(*me)]
            pltpu.make_async_copy(shard_ref, land, local_sem).start()
            _remote(shard_ref, land, send_sems, recv_sems, 0, sib).start()
            for step, pair in pair_of_chip_step.items():
                peer = (x ^ ((step >> 2) & 1), y ^ ((step >> 1) & 1), c)
                _remote(shard_ref, land, send_sems, recv_sems, pair, peer).start()
            fetch(0)

        for step in range(N_DEV):
            @pl.when((s == step) & (i == 0))
            def _():
                slot_copy(step).wait()

            if step + 1 < N_DEV:
                @pl.when((s == step) & (i == n_tiles - 1))
                def _():
                    fetch(step + 1)

        proj_ref[...] = _dot(h_ref[...], wbuf[s % 2])

        @pl.when((s == N_DEV - 1) & (i == n_tiles - 1))
        def _():
            for pair in range(7):
                _remote(shard_ref, win_ref.at[0], send_sems, recv_sems, pair, me).wait_send()
            pltpu.make_async_copy(shard_ref, win_ref.at[_dev_index(*me)], local_sem).wait()

    hbm = pl.BlockSpec(memory_space=pltpu.HBM)
    return _pcall(
        body, (h, shard), name="mm_in",
        out_shape=(jax.ShapeDtypeStruct((T, N_DEV * n), F32), jax.ShapeDtypeStruct((N_DEV, K, n), shard.dtype)),
        grid=(N_DEV, n_tiles), in_specs=[pl.BlockSpec((tm, K), lambda s, i: (i, 0)), hbm],
        out_specs=(pl.BlockSpec((tm, n), lambda s, i: (i, slab_of(s))), hbm),
        scratch_shapes=[pltpu.VMEM((2, K, n), shard.dtype), pltpu.SemaphoreType.DMA((2,)),
                        pltpu.SemaphoreType.DMA((7,)), pltpu.SemaphoreType.DMA((7,)), pltpu.SemaphoreType.DMA],
        sem=("arbitrary", "arbitrary"), comm=comm)


def _rms_bwd_epilogue(dh, operands, outputs, first):
    x_ref, g_ref, dres_ref = operands
    dx_ref, dg_ref = outputs
    xv = x_ref[...]
    r = lax.rsqrt(jnp.mean(xv * xv, axis=-1, keepdims=True) + EPS)
    xhat = xv * r

    @pl.when(first)
    def _():
        dg_ref[...] = jnp.zeros_like(dg_ref)

    dg_ref[...] += jnp.sum(dh * xhat, axis=0, keepdims=True)
    dxhat = dh * g_ref[...]
    dx_ref[...] = dres_ref[...] + r * (dxhat - xhat * jnp.mean(dxhat * xhat, axis=-1, keepdims=True))


def _rms_bwd_fused(M, K, tm, rms):
    row = pl.BlockSpec((tm, K), lambda i, j: (i, 0))
    vec = pl.BlockSpec((1, K), lambda i, j: (0, 0))
    x, g, dres = rms
    return dict(res=[x, g, dres], res_spec=[row, vec, row], epilogue=_rms_bwd_epilogue,
                out_shape=(jax.ShapeDtypeStruct((M, K), F32), jax.ShapeDtypeStruct((1, K), F32)), o_spec=(row, vec))


def _mm_cols_slab_t(name, a, w_slabs, rms, tm=MM_ROWS_RES, comm=None):
    M = a.shape[0]
    J, K, n = w_slabs.shape
    tm = _tile(M, tm)
    fused = _rms_bwd_fused(M, K, tm, rms)
    return _matmul(
        name, "nt", a, w_slabs, fused.pop("out_shape"), (M // tm, J),
        pl.BlockSpec((tm, n), lambda i, j: (i, j)), pl.BlockSpec((None, K, n), lambda i, j: (j, 0, 0)),
        fused.pop("o_spec"), (tm, K), comm=comm, **fused)


def _mm_tn_slab(name, a, b, n, out_dtype=F32, tk=MM_TOKENS, comm=None):
    T, M = a.shape
    J = b.shape[1] // n
    tk = _tile(T, tk)
    return _matmul(
        name, "tn", a, b, jax.ShapeDtypeStruct((J, M, n), out_dtype), (J, T // tk),
        pl.BlockSpec((tk, M), lambda j, k: (k, 0)), pl.BlockSpec((tk, n), lambda j, k: (k, j)),
        pl.BlockSpec((None, M, n), lambda j, k: (j, 0, 0)), (M, n), comm=comm)


def _rms_fwd(name, x, g, tm=512, split=None):
    T, Dm = x.shape
    tm = _tile(T, tm)
    widths = [Dm] if split is None else [Dm, split, Dm - split]

    def body(x_ref, g_ref, *h_refs):
        xv = x_ref[...]
        r = lax.rsqrt(jnp.mean(xv * xv, axis=-1, keepdims=True) + EPS)
        h = (xv * r * g_ref[...]).astype(BF16)
        h_refs[0][...] = h
        if split is not None:
            h_refs[1][...] = h[:, :split]
            h_refs[2][...] = h[:, split:]

    out = pl.pallas_call(
        body, out_shape=[jax.ShapeDtypeStruct((T, w), BF16) for w in widths], grid=(T // tm,),
        in_specs=[pl.BlockSpec((tm, Dm), lambda i: (i, 0)), pl.BlockSpec((1, Dm), lambda i: (0, 0))],
        out_specs=[pl.BlockSpec((tm, w), lambda i: (i, 0)) for w in widths], name=name,
        compiler_params=_params(("parallel",)),
    )(x, g)
    return out[0] if split is None else tuple(out)


def _rms_bwd(name, x, g, dh, dres, tm=512):
    T, Dm = x.shape
    tm = _tile(T, tm)
    want_dx = dres is not None

    def body(*refs):
        if want_dx:
            x_ref, g_ref, dh_ref, dres_ref, dx_ref, dg_ref = refs
        else:
            x_ref, g_ref, dh_ref, dg_ref = refs
        xv = x_ref[...]
        r = lax.rsqrt(jnp.mean(xv * xv, axis=-1, keepdims=True) + EPS)
        xhat = xv * r
        dhv = dh_ref[...]

        @pl.when(pl.program_id(0) == 0)
        def _():
            dg_ref[...] = jnp.zeros_like(dg_ref)

        dg_ref[...] += jnp.sum(dhv * xhat, axis=0, keepdims=True)
        if want_dx:
            dxhat = dhv * g_ref[...]
            dx_ref[...] = dres_ref[...] + r * (dxhat - xhat * jnp.mean(dxhat * xhat, axis=-1, keepdims=True))

    row = pl.BlockSpec((tm, Dm), lambda i: (i, 0))
    vec = pl.BlockSpec((1, Dm), lambda i: (0, 0))
    if want_dx:
        return pl.pallas_call(
            body, out_shape=(jax.ShapeDtypeStruct((T, Dm), F32), jax.ShapeDtypeStruct((1, Dm), F32)),
            grid=(T // tm,), in_specs=[row, vec, row, row], out_specs=(row, vec), name=name,
            compiler_params=_params(("arbitrary",)),
        )(x, g, dh, dres)
    return pl.pallas_call(
        body, out_shape=jax.ShapeDtypeStruct((1, Dm), F32), grid=(T // tm,), in_specs=[row, vec, row],
        out_specs=vec, name=name, compiler_params=_params(("arbitrary",)),
    )(x, g, dh)


def _pool_rows(S):
    return _tile(S, 256)


def _pool_count(c0, rows, w):
    t = c0 + lax.broadcasted_iota(jnp.int32, (rows, 1), 0)
    return jnp.minimum(t + 1, w).astype(F32)


def _pool_fwd(proj, w_pool, scale, B, S):
    CH = _pool_rows(S)

    def body(hp_ref, wp_ref, sc_ref, o_ref, pad_ref):
        pad_ref[0:POOL_HALO, :] = jnp.zeros((POOL_HALO, POOL_WIDTH), F32)
        pad_ref[POOL_HALO:, :] = hp_ref[...]
        for gi, w in enumerate(POOL_WINDOWS):
            cols = slice(gi * POOL_GROUP_DIM, (gi + 1) * POOL_GROUP_DIM)
            for c in range(S // CH):
                base = POOL_HALO + c * CH
                acc = pad_ref[base:base + CH, cols]
                tok = acc
                for j in range(1, w):
                    acc = acc + pad_ref[base - j:base - j + CH, cols]
                pooled = acc / _pool_count(c * CH, CH, w) - tok
                z = _dot(pooled, wp_ref[gi])
                o_ref[c * CH:(c + 1) * CH, cols] = (z * sc_ref[:, cols]).astype(o_ref.dtype)

    return pl.pallas_call(
        body, out_shape=jax.ShapeDtypeStruct((B * S, POOL_WIDTH), BF16), grid=(B,),
        in_specs=[pl.BlockSpec((S, POOL_WIDTH), lambda b: (b, 0)),
                  pl.BlockSpec(w_pool.shape, lambda b: (0, 0, 0)),
                  pl.BlockSpec((1, POOL_WIDTH), lambda b: (0, 0))],
        out_specs=pl.BlockSpec((S, POOL_WIDTH), lambda b: (b, 0)),
        scratch_shapes=[pltpu.VMEM((S + POOL_HALO, POOL_WIDTH), F32)],
        name="pool_fwd", compiler_params=_params(("parallel",)),
    )(proj, w_pool, scale)


def _pool_bwd(proj, d_ypre, w_pool, scale, B, S, comm=None):
    CH = _pool_rows(S)

    def body(hp_ref, dy_ref, wp_ref, sc_ref, dhp_ref, dwp_ref, dsc_ref, pad_ref, sc_pad_ref, dp_ref):
        @pl.when(pl.program_id(0) == 0)
        def _():
            dwp_ref[...] = jnp.zeros_like(dwp_ref)
            dsc_ref[...] = jnp.zeros_like(dsc_ref)

        pad_ref[0:POOL_HALO, :] = jnp.zeros((POOL_HALO, POOL_WIDTH), F32)
        pad_ref[POOL_HALO:, :] = hp_ref[...]
        sc_pad_ref[S:, :] = jnp.zeros((POOL_HALO, POOL_WIDTH), F32)
        for gi, w in enumerate(POOL_WINDOWS):
            cols = slice(gi * POOL_GROUP_DIM, (gi + 1) * POOL_GROUP_DIM)
            for c in range(S // CH):
                base = POOL_HALO + c * CH
                rows = slice(c * CH, (c + 1) * CH)
                acc = pad_ref[base:base + CH, cols]
                tok = acc
                for j in range(1, w):
                    acc = acc + pad_ref[base - j:base - j + CH, cols]
                cnt = _pool_count(c * CH, CH, w)
                pooled = acc / cnt - tok
                z = _dot(pooled, wp_ref[gi])
                dy = dy_ref[rows, cols]
                dsc_ref[:, cols] += jnp.sum(dy * z, axis=0, keepdims=True)
                dz = dy * sc_ref[:, cols]
                dwp_ref[gi] += _dot(pooled, dz, "tn")
                dpool = _dot(dz, wp_ref[gi], "nt")
                dp_ref[rows, cols] = dpool
                sc_pad_ref[rows, cols] = dpool / cnt
            for c in range(S // CH):
                rows = slice(c * CH, (c + 1) * CH)
                acc = sc_pad_ref[rows, cols]
                for j in range(1, w):
                    acc = acc + sc_pad_ref[c * CH + j:c * CH + j + CH, cols]
                dhp_ref[rows, cols] = (acc - dp_ref[rows, cols]).astype(dhp_ref.dtype)

    seq = pl.BlockSpec((S, POOL_WIDTH), lambda b: (b, 0))
    return _pcall(
        body, (proj, d_ypre, w_pool, scale), name="pool_bwd",
        out_shape=(jax.ShapeDtypeStruct((B * S, POOL_WIDTH), BF16),
                   jax.ShapeDtypeStruct(w_pool.shape, F32), jax.ShapeDtypeStruct((1, POOL_WIDTH), F32)),
        grid=(B,),
        in_specs=[seq, seq, pl.BlockSpec(w_pool.shape, lambda b: (0, 0, 0)),
                  pl.BlockSpec((1, POOL_WIDTH), lambda b: (0, 0))],
        out_specs=(seq, pl.BlockSpec(w_pool.shape, lambda b: (0, 0, 0)),
                   pl.BlockSpec((1, POOL_WIDTH), lambda b: (0, 0))),
        scratch_shapes=[pltpu.VMEM((S + POOL_HALO, POOL_WIDTH), F32),
                        pltpu.VMEM((S + POOL_HALO, POOL_WIDTH), F32),
                        pltpu.VMEM((S, POOL_WIDTH), F32)],
        sem=("arbitrary",), comm=comm)


def _ret_tables(S):
    half = RET_QK_DIM // 2
    inv = ROPE_BASE ** (-jnp.arange(half, dtype=F32) / half)
    ang = jnp.arange(S, dtype=F32)[:, None] * inv[None, :]
    cos, sin = jnp.cos(ang), jnp.sin(ang)
    cos_full = jnp.concatenate([cos, cos], axis=-1)
    sin_signed = jnp.concatenate([-sin, sin], axis=-1)
    C = RET_CHUNK
    lg = jnp.log1p(-jnp.exp2(-5.0 - jnp.arange(RET_HEADS, dtype=F32)))[:, None, None]
    idx = jnp.arange(C, dtype=F32)
    rel = idx[:, None] - idx[None, :]
    decay = jnp.where(rel >= 0, jnp.exp(jnp.maximum(rel, 0.0) * lg), 0.0)
    q_decay = jnp.broadcast_to(jnp.exp((idx + 1.0)[None, :, None] * lg), (RET_HEADS, C, RET_QK_DIM))
    k_decay = jnp.broadcast_to(jnp.exp((C - 1.0 - idx)[None, :, None] * lg), (RET_HEADS, C, RET_QK_DIM))
    c_decay = jnp.broadcast_to(jnp.exp(C * lg), (RET_HEADS, 1, RET_V_DIM))
    return cos_full, sin_signed, decay, q_decay, k_decay, c_decay


def _rope(x, cos_full, sin_signed):
    return x * cos_full + pltpu.roll(x, RET_QK_DIM // 2, axis=1) * sin_signed


def _rope_t(dy, cos_full, sin_signed):
    return dy * cos_full + pltpu.roll(dy * sin_signed, RET_QK_DIM // 2, axis=1)


RET_COLS = 512


def _ret_specs(N, chunk_of):
    C = RET_CHUNK

    def rows(width, col=0):
        return pl.BlockSpec((C, width), lambda b, i: (b * N + chunk_of(i), col))

    def whole(shape):
        return pl.BlockSpec(shape, lambda b, i: (0,) * len(shape))

    wide = RET_HEADS * RET_V_DIM
    return dict(
        q=rows(RET_COLS, COL_Q // RET_COLS), k=rows(RET_COLS, COL_K // RET_COLS),
        v=[rows(RET_COLS, COL_V // RET_COLS + j) for j in range(2)],
        gr=[rows(RET_COLS, COL_GR // RET_COLS + j) for j in range(2)],
        table=pl.BlockSpec((C, RET_QK_DIM), lambda b, i: (chunk_of(i), 0)),
        decay=whole((RET_HEADS, C, C)), qd=whole((RET_HEADS, C, RET_QK_DIM)), kd=whole((RET_HEADS, C, RET_QK_DIM)),
        cd=whole((RET_HEADS, 1, RET_V_DIM)), vec=whole((1, wide)), qk_rows=rows(RET_COLS), v_rows=rows(wide),
        state=pl.BlockSpec((None, None, RET_HEADS, RET_QK_DIM, RET_V_DIM), lambda b, i: (b, chunk_of(i), 0, 0, 0)))


def _head_cols(h):
    pair = slice((h % 2) * RET_V_DIM, (h % 2 + 1) * RET_V_DIM)
    return slice(h * RET_QK_DIM, (h + 1) * RET_QK_DIM), h // 2, pair, slice(h * RET_V_DIM, (h + 1) * RET_V_DIM)


def _group_norm(o):
    mu = jnp.mean(o, axis=-1, keepdims=True)
    oc = o - mu
    rstd = lax.rsqrt(jnp.mean(oc * oc, axis=-1, keepdims=True) + EPS)
    return oc * rstd, rstd


def _ret_fwd(proj, g_ret, b_ret, tables, B, S, comm=None):
    N = S // RET_CHUNK
    cos_t, sin_t, decay, q_decay, k_decay, c_decay = tables
    sp = _ret_specs(N, lambda i: i)

    def body(q_ref, k_ref, v0_ref, v1_ref, gr0_ref, gr1_ref, cos_ref, sin_ref, dec_ref, qd_ref, kd_ref, cd_ref,
             g_ref, b_ref, y_ref, rs_ref, r_ref):
        @pl.when(pl.program_id(1) == 0)
        def _():
            r_ref[...] = jnp.zeros_like(r_ref)

        cs, sn = cos_ref[...], sin_ref[...]
        heads = range(RET_HEADS)
        cols = [_head_cols(h) for h in heads]
        q = [_rope(q_ref[:, cols[h][0]], cs, sn) for h in heads]
        k = [_rope(k_ref[:, cols[h][0]], cs, sn) * (RET_QK_DIM ** -0.5) for h in heads]
        v = [(v0_ref, v1_ref)[cols[h][1]][:, cols[h][2]] for h in heads]
        R = [r_ref[h] for h in heads]
        s = [_dot(q[h], k[h], "nt") * dec_ref[h] for h in heads]
        o = [_dot(s[h], v[h]) + _dot(q[h] * qd_ref[h], R[h]) for h in heads]
        r_new = [cd_ref[h] * R[h] + _dot(k[h] * kd_ref[h], v[h], "tn") for h in heads]
        for h in heads:
            _, j, pair, wide = cols[h]
            rs_ref[h] = R[h]
            r_ref[h] = r_new[h]
            on, _ = _group_norm(o[h])
            gr = (gr0_ref, gr1_ref)[j][:, pair]
            y_ref[:, wide] = (gr * jax.nn.sigmoid(gr) * (on * g_ref[:, wide] + b_ref[:, wide])).astype(y_ref.dtype)

    state = jax.ShapeDtypeStruct((B, N, RET_HEADS, RET_QK_DIM, RET_V_DIM), F32)
    return _pcall(
        body, (proj,) * 6 + (cos_t, sin_t, decay, q_decay, k_decay, c_decay, g_ret, b_ret),
        name="ret_fwd", out_shape=(jax.ShapeDtypeStruct((B * S, RET_HEADS * RET_V_DIM), BF16), state), grid=(B, N),
        in_specs=[sp["q"], sp["k"], *sp["v"], *sp["gr"], sp["table"], sp["table"], sp["decay"], sp["qd"],
                  sp["kd"], sp["cd"], sp["vec"], sp["vec"]],
        out_specs=(sp["v_rows"], sp["state"]),
        scratch_shapes=[pltpu.VMEM((RET_HEADS, RET_QK_DIM, RET_V_DIM), F32)],
        sem=("parallel", "arbitrary"), comm=comm)


def _ret_bwd(proj, states, d_yr, g_ret, b_ret, tables, B, S, comm=None):
    N = S // RET_CHUNK
    cos_t, sin_t, decay, q_decay, k_decay, c_decay = tables
    sp = _ret_specs(N, lambda i: N - 1 - i)
    qk_scale = RET_QK_DIM ** -0.5

    def body(q_ref, k_ref, v0_ref, v1_ref, gr0_ref, gr1_ref, dy_ref, rs_ref, cos_ref, sin_ref, dec_ref, qd_ref,
             kd_ref, cd_ref, g_ref, b_ref, dq_ref, dk_ref, dv_ref, dgr_ref, dg_ref, db_ref, dr_ref):
        @pl.when((pl.program_id(0) == 0) & (pl.program_id(1) == 0))
        def _():
            dg_ref[...] = jnp.zeros_like(dg_ref)
            db_ref[...] = jnp.zeros_like(db_ref)

        @pl.when(pl.program_id(1) == 0)
        def _():
            dr_ref[...] = jnp.zeros_like(dr_ref)

        cs, sn = cos_ref[...], sin_ref[...]
        heads = range(RET_HEADS)
        cols = [_head_cols(h) for h in heads]
        q = [_rope(q_ref[:, cols[h][0]], cs, sn) for h in heads]
        k = [_rope(k_ref[:, cols[h][0]], cs, sn) * qk_scale for h in heads]
        v = [(v0_ref, v1_ref)[cols[h][1]][:, cols[h][2]] for h in heads]
        s = [_dot(q[h], k[h], "nt") * dec_ref[h] for h in heads]
        o = [_dot(s[h], v[h]) + _dot(q[h] * qd_ref[h], rs_ref[h]) for h in heads]
        do = []
        for h in heads:
            _, j, pair, wide = cols[h]
            on, rstd = _group_norm(o[h])
            g = g_ref[:, wide]
            oaff = on * g + b_ref[:, wide]
            gr = (gr0_ref, gr1_ref)[j][:, pair]
            sg = jax.nn.sigmoid(gr)
            dy = dy_ref[:, wide]
            dgr_ref[:, wide] = (dy * oaff * (sg * (1.0 + gr * (1.0 - sg)))).astype(dgr_ref.dtype)
            doaff = dy * (gr * sg)
            dg_ref[:, wide] += jnp.sum(doaff * on, axis=0, keepdims=True)
            db_ref[:, wide] += jnp.sum(doaff, axis=0, keepdims=True)
            don = doaff * g
            do.append(rstd * (don - jnp.mean(don, axis=-1, keepdims=True)
                              - on * jnp.mean(don * on, axis=-1, keepdims=True)))
        ds = [_dot(do[h], v[h], "nt") * dec_ref[h] for h in heads]
        dq = [_dot(ds[h], k[h]) + qd_ref[h] * _dot(do[h], rs_ref[h], "nt") for h in heads]
        dk = [_dot(ds[h], q[h], "tn") + kd_ref[h] * _dot(v[h], dr_ref[h], "nt") for h in heads]
        dv = [_dot(s[h], do[h], "tn") + _dot(k[h] * kd_ref[h], dr_ref[h]) for h in heads]
        dr = [cd_ref[h] * dr_ref[h] + _dot(q[h] * qd_ref[h], do[h], "tn") for h in heads]
        for h in heads:
            qk, _, _, wide = cols[h]
            dv_ref[:, wide] = dv[h].astype(dv_ref.dtype)
            dr_ref[h] = dr[h]
            dq_ref[:, qk] = _rope_t(dq[h], cs, sn).astype(dq_ref.dtype)
            dk_ref[:, qk] = _rope_t(dk[h] * qk_scale, cs, sn).astype(dk_ref.dtype)

    T = B * S
    qk_shape = jax.ShapeDtypeStruct((T, RET_HEADS * RET_QK_DIM), BF16)
    v_shape = jax.ShapeDtypeStruct((T, RET_HEADS * RET_V_DIM), BF16)
    vec_shape = jax.ShapeDtypeStruct((1, RET_HEADS * RET_V_DIM), F32)
    return _pcall(
        body, (proj,) * 6 + (d_yr, states, cos_t, sin_t, decay, q_decay, k_decay, c_decay, g_ret, b_ret),
        name="ret_bwd", out_shape=(qk_shape, qk_shape, v_shape, v_shape, vec_shape, vec_shape), grid=(B, N),
        in_specs=[sp["q"], sp["k"], *sp["v"], *sp["gr"], sp["v_rows"], sp["state"], sp["table"], sp["table"],
                  sp["decay"], sp["qd"], sp["kd"], sp["cd"], sp["vec"], sp["vec"]],
        out_specs=(sp["qk_rows"], sp["qk_rows"], sp["v_rows"], sp["v_rows"], sp["vec"], sp["vec"]),
        scratch_shapes=[pltpu.VMEM((RET_HEADS, RET_QK_DIM, RET_V_DIM), F32)],
        sem=("arbitrary", "arbitrary"), comm=comm)


def _xa_rows(S):
    return _tile(S, 256)


def _xa_groups(S, rows, size=4):
    chunks = [slice(r, r + rows) for r in range(0, S, rows)]
    return [chunks[g:g + size] for g in range(0, len(chunks), size)]


def _xa_specs(S, M):
    q = pl.BlockSpec((S, XA_HEAD_DIM), lambda b, h: (b, COL_QX // XA_HEAD_DIM + h))
    k = pl.BlockSpec((M, XA_HEAD_DIM), lambda b, h: (b, h))
    v = pl.BlockSpec((M, XA_HEAD_DIM), lambda b, h: (b, XA_HEADS + h))
    o = pl.BlockSpec((S, XA_HEAD_DIM), lambda b, h: (b, h))
    return q, k, v, o


def _softmax_rows(s):
    e = jnp.exp(s - jnp.max(s, axis=-1, keepdims=True))
    return e / jnp.sum(e, axis=-1, keepdims=True)


def _xa_fwd(proj, kv, B, S, M, comm=None):
    CH = _xa_rows(S)
    q_spec, k_spec, v_spec, o_spec = _xa_specs(S, M)

    def body(q_ref, k_ref, v_ref, o_ref):
        for group in _xa_groups(S, CH):
            sc = [_dot(q_ref[rows, :], k_ref[...], "nt") * (XA_HEAD_DIM ** -0.5) for rows in group]
            p = [_softmax_rows(s) for s in sc]
            for rows, pg in zip(group, p):
                o_ref[rows, :] = _dot(pg, v_ref[...]).astype(o_ref.dtype)

    return _pcall(
        body, (proj, kv, kv), name="xattn_fwd", out_shape=jax.ShapeDtypeStruct((B * S, XA_WIDTH), BF16),
        grid=(B, XA_HEADS), in_specs=[q_spec, k_spec, v_spec], out_specs=o_spec,
        sem=("parallel", "parallel"), comm=comm)


def _xa_bwd(proj, kv, d_o, B, S, M, comm=None):
    CH = _xa_rows(S)
    q_spec, k_spec, v_spec, o_spec = _xa_specs(S, M)
    scale = XA_HEAD_DIM ** -0.5

    def body(q_ref, k_ref, v_ref, do_ref, dq_ref, dk_ref, dv_ref):
        dk_ref[...] = jnp.zeros_like(dk_ref)
        dv_ref[...] = jnp.zeros_like(dv_ref)
        for group in _xa_groups(S, CH):
            q = [q_ref[rows, :] for rows in group]
            do = [do_ref[rows, :] for rows in group]
            p = [_softmax_rows(_dot(qg, k_ref[...], "nt") * scale) for qg in q]
            dp = [_dot(dg, v_ref[...], "nt") for dg in do]
            ds = [pg * (dpg - jnp.sum(dpg * pg, axis=-1, keepdims=True)) * scale for pg, dpg in zip(p, dp)]
            for rows, dsg in zip(group, ds):
                dq_ref[rows, :] = _dot(dsg, k_ref[...]).astype(dq_ref.dtype)
            dk_ref[...] += sum(_dot(dsg, qg, "tn") for dsg, qg in zip(ds, q))
            dv_ref[...] += sum(_dot(pg, dg, "tn") for pg, dg in zip(p, do))

    kv_out = pl.BlockSpec((M, XA_HEAD_DIM), lambda b, h: (b, h))
    return _pcall(
        body, (proj, kv, kv, d_o), name="xattn_bwd",
        out_shape=(jax.ShapeDtypeStruct((B * S, XA_WIDTH), BF16), jax.ShapeDtypeStruct((B * M, XA_WIDTH), F32),
                   jax.ShapeDtypeStruct((B * M, XA_WIDTH), F32)),
        grid=(B, XA_HEADS), in_specs=[q_spec, k_spec, v_spec, o_spec], out_specs=(o_spec, kv_out, kv_out),
        sem=("parallel", "parallel"), comm=comm)


def _gate_specs(tm):
    n = COL_GL // D_MODEL
    return [pl.BlockSpec((tm, D_MODEL), lambda i, j=j: (i, n + j)) for j in range(3)]


def _merge_out(proj, ys, w_out, x, g, tm=512, comm=None):
    T = proj.shape[0]
    tm = _tile(T, tm)
    row = pl.BlockSpec((tm, D_MODEL), lambda i: (i, 0))
    vec = pl.BlockSpec((1, D_MODEL), lambda i: (0, 0))

    def body(g0, g1, g2, y0, y1, y2, w_ref, x_ref, gf_ref, m_ref, x1_ref, h_ref):
        acc = jax.nn.sigmoid(g0[...]) * y0[...]
        acc = acc + jax.nn.sigmoid(g1[...]) * y1[...]
        acc = acc + jax.nn.sigmoid(g2[...]) * y2[...]
        merged = acc.astype(m_ref.dtype)
        m_ref[...] = merged
        xv = x_ref[...] + _dot(merged, w_ref[...])
        x1_ref[...] = xv
        r = lax.rsqrt(jnp.mean(xv * xv, axis=-1, keepdims=True) + EPS)
        h_ref[...] = (xv * r * gf_ref[...]).astype(h_ref.dtype)

    bf16_rows = jax.ShapeDtypeStruct((T, D_MODEL), BF16)
    return _pcall(
        body, (proj, proj, proj, *ys, w_out, x, g), name="merge_out",
        out_shape=(bf16_rows, jax.ShapeDtypeStruct((T, D_MODEL), F32), bf16_rows), grid=(T // tm,),
        in_specs=_gate_specs(tm) + [row] * 3 + [pl.BlockSpec(w_out.shape, lambda i: (0, 0)), row, vec],
        out_specs=(row, row, row), sem=("parallel",), comm=comm)


def _merge_bwd(proj, ys, dx1, w_out, tm=512, comm=None):
    T = proj.shape[0]
    tm = _tile(T, tm)
    row = pl.BlockSpec((tm, D_MODEL), lambda i: (i, 0))

    def body(g0, g1, g2, y0, y1, y2, dx_ref, w_ref, dgl_ref, d0, d1, d2):
        dm = _dot(dx_ref[...], w_ref[...], "nt")
        for j, (g_ref, y_ref, d_ref) in enumerate(((g0, y0, d0), (g1, y1, d1), (g2, y2, d2))):
            sg = jax.nn.sigmoid(g_ref[...])
            d_ref[...] = (dm * sg).astype(d_ref.dtype)
            dgl_ref[:, j * D_MODEL:(j + 1) * D_MODEL] = (dm * y_ref[...] * sg * (1.0 - sg)).astype(dgl_ref.dtype)

    dy = jax.ShapeDtypeStruct((T, D_MODEL), BF16)
    return _pcall(
        body, (proj, proj, proj, *ys, dx1, w_out), name="merge_bwd",
        out_shape=(jax.ShapeDtypeStruct((T, 3 * D_MODEL), BF16), dy, dy, dy), grid=(T // tm,),
        in_specs=_gate_specs(tm) + [row] * 4 + [pl.BlockSpec(w_out.shape, lambda i: (0, 0))],
        out_specs=(pl.BlockSpec((tm, 3 * D_MODEL), lambda i: (i, 0)), row, row, row),
        sem=("parallel",), comm=comm)


def _gelu(x):
    return 0.5 * x * (1.0 + jnp.tanh(GELU_C * (x + GELU_A * x * x * x)))


def _gelu_grad(x):
    t = jnp.tanh(GELU_C * (x + GELU_A * x * x * x))
    return 0.5 * (1.0 + t) + 0.5 * x * (1.0 - t * t) * GELU_C * (1.0 + 3.0 * GELU_A * x * x)


GLU_HALO = 16


def _shift_down(x, prev, n):
    last = prev.shape[0]
    r = lax.broadcasted_iota(jnp.int32, (8, 1), 0)
    rolled = pltpu.roll(x, n, axis=0)
    head = rolled[0:8]
    for j in range(n):
        head = jnp.where(r == j, prev[last - n + j:last - n + j + 1, :], head)
    return jnp.concatenate([head, rolled[8:]], axis=0)


def _shift_up(x, nxt, n):
    rows = x.shape[0]
    r = lax.broadcasted_iota(jnp.int32, (8, 1), 0)
    rolled = pltpu.roll(x, rows - n, axis=0)
    tail = rolled[rows - 8:]
    for j in range(n):
        tail = jnp.where(r == 8 - n + j, nxt[j:j + 1, :], tail)
    return jnp.concatenate([rolled[:rows - 8], tail], axis=0)


def _conv(a, prev, cw, cb):
    return _shift_down(a, prev, 2) * cw[0:1, :] + _shift_down(a, prev, 1) * cw[1:2, :] + a * cw[2:3, :] + cb


def _glu_fwd(up, cw, cb, S, tm=1024, comm=None):
    T = up.shape[2]
    tm = _tile(S, tm)
    per_seq = S // tm

    def body(ab_ref, prev_ref, cw_ref, cb_ref, u_ref):
        i = pl.program_id(1)
        prev = jnp.where(i % per_seq == 0, 0.0, prev_ref[...].astype(F32))
        ac = _conv(ab_ref[0].astype(F32), prev, cw_ref[...], cb_ref[...])
        u_ref[...] = (_gelu(ac) * ab_ref[1].astype(F32)).astype(u_ref.dtype)

    before = tm // GLU_HALO
    return _pcall(
        body, (up, up, cw, cb), name="glu_fwd",
        out_shape=jax.ShapeDtypeStruct((FFN_SLABS, T, UP_SHARD), BF16), grid=(FFN_SLABS, T // tm),
        in_specs=[pl.BlockSpec((2, None, tm, UP_SHARD), lambda d, i: (0, d, i, 0)),
                  pl.BlockSpec((None, None, GLU_HALO, UP_SHARD),
                               lambda d, i: (0, d, jnp.maximum(i * before - 1, 0), 0)),
                  pl.BlockSpec((None, 3, UP_SHARD), lambda d, i: (d, 0, 0)),
                  pl.BlockSpec((None, 1, UP_SHARD), lambda d, i: (d, 0, 0))],
        out_specs=pl.BlockSpec((None, tm, UP_SHARD), lambda d, i: (d, i, 0)),
        sem=("parallel", "parallel"), comm=comm)


def _glu_bwd(up, d_u, cw, cb, S, tm=1024, comm=None):
    T = up.shape[2]
    tm = _tile(S, tm)
    per_seq = S // tm
    n_tiles = T // tm
    per_tile = tm // GLU_HALO

    def body(ab_ref, prev_ref, abn_ref, du_ref, dun_ref, cw_ref, cb_ref, dup_ref, dcw_ref, dcb_ref):
        i = pl.program_id(1)

        @pl.when(i == 0)
        def _():
            dcw_ref[...] = jnp.zeros_like(dcw_ref)
            dcb_ref[...] = jnp.zeros_like(dcb_ref)

        cw, cb = cw_ref[...], cb_ref[...]
        a, b = ab_ref[0].astype(F32), ab_ref[1].astype(F32)
        prev = jnp.where(i % per_seq == 0, 0.0, prev_ref[...].astype(F32))
        a2, a1 = _shift_down(a, prev, 2), _shift_down(a, prev, 1)
        ac = a2 * cw[0:1, :] + a1 * cw[1:2, :] + a * cw[2:3, :] + cb
        du = du_ref[...].astype(F32)
        dup_ref[1] = (du * _gelu(ac)).astype(dup_ref.dtype)
        dac = du * b * _gelu_grad(ac)
        dcb_ref[...] += jnp.sum(dac, axis=0, keepdims=True)
        dcw_ref[0:1, :] += jnp.sum(dac * a2, axis=0, keepdims=True)
        dcw_ref[1:2, :] += jnp.sum(dac * a1, axis=0, keepdims=True)
        dcw_ref[2:3, :] += jnp.sum(dac * a, axis=0, keepdims=True)
        acn = _conv(abn_ref[0].astype(F32), a[tm - GLU_HALO:, :], cw, cb)
        dacn = jnp.where(i % per_seq == per_seq - 1, 0.0,
                         dun_ref[...].astype(F32) * abn_ref[1].astype(F32) * _gelu_grad(acn))
        da = dac * cw[2:3, :] + _shift_up(dac, dacn, 1) * cw[1:2, :] + _shift_up(dac, dacn, 2) * cw[0:1, :]
        dup_ref[0] = da.astype(dup_ref.dtype)

    def nxt(i):
        return jnp.minimum((i + 1) * per_tile, T // GLU_HALO - 1)

    return _pcall(
        body, (up, up, up, d_u, d_u, cw, cb), name="glu_bwd",
        out_shape=(jax.ShapeDtypeStruct((2, FFN_SLABS, T, UP_SHARD), BF16),
                   jax.ShapeDtypeStruct((FFN_SLABS, 3, UP_SHARD), F32),
                   jax.ShapeDtypeStruct((FFN_SLABS, 1, UP_SHARD), F32)),
        grid=(FFN_SLABS, n_tiles),
        in_specs=[pl.BlockSpec((2, None, tm, UP_SHARD), lambda d, i: (0, d, i, 0)),
                  pl.BlockSpec((None, None, GLU_HALO, UP_SHARD),
                               lambda d, i: (0, d, jnp.maximum(i * per_tile - 1, 0), 0)),
                  pl.BlockSpec((2, None, GLU_HALO, UP_SHARD), lambda d, i: (0, d, nxt(i), 0)),
                  pl.BlockSpec((None, tm, UP_SHARD), lambda d, i: (d, i, 0)),
                  pl.BlockSpec((None, GLU_HALO, UP_SHARD), lambda d, i: (d, nxt(i), 0)),
                  pl.BlockSpec((None, 3, UP_SHARD), lambda d, i: (d, 0, 0)),
                  pl.BlockSpec((None, 1, UP_SHARD), lambda d, i: (d, 0, 0))],
        out_specs=(pl.BlockSpec((2, None, tm, UP_SHARD), lambda d, i: (0, d, i, 0)),
                   pl.BlockSpec((None, 3, UP_SHARD), lambda d, i: (d, 0, 0)),
                   pl.BlockSpec((None, 1, UP_SHARD), lambda d, i: (d, 0, 0))),
        sem=("parallel", "arbitrary"), comm=comm)


def _mm_up(h2, w_up_t, tm=MM_ROWS, comm=None):
    T, K = h2.shape
    tm = _tile(T, tm)
    return _matmul(
        "mm_up", "nt", h2, w_up_t, jax.ShapeDtypeStruct((N_DEV, T, UP_SHARD), BF16), (N_DEV, T // tm, 1),
        pl.BlockSpec((tm, K), lambda j, i, k: (i, 0)), pl.BlockSpec((None, UP_SHARD, K), lambda j, i, k: (j, 0, 0)),
        pl.BlockSpec((None, tm, UP_SHARD), lambda j, i, k: (j, i, 0)), (tm, UP_SHARD), comm=comm)


def _loss_epilogue(ffn, operands, outputs, first):
    x1_ref, t_ref, g_ref = operands
    dx_ref, dg_ref, loss_ref = outputs

    @pl.when(first)
    def _():
        dg_ref[...] = jnp.zeros_like(dg_ref)
        loss_ref[...] = jnp.zeros_like(loss_ref)

    xv = x1_ref[...] + ffn
    r = lax.rsqrt(jnp.mean(xv * xv, axis=-1, keepdims=True) + EPS)
    xhat = xv * r
    err = xhat * g_ref[...] - t_ref[...]
    loss_ref[...] += (0.5 / D_MODEL) * jnp.sum(err * err)
    dy = err * (1.0 / D_MODEL)
    dg_ref[...] += jnp.sum(dy * xhat, axis=0, keepdims=True)
    dxhat = dy * g_ref[...]
    dx_ref[...] = r * (dxhat - xhat * jnp.mean(dxhat * xhat, axis=-1, keepdims=True))


def _mm_down_loss(u, w_down, x1, target, g_final, tm=MM_ROWS_RES):
    J, T, n = u.shape
    tm = _tile(T, tm)
    row = pl.BlockSpec((tm, D_MODEL), lambda i, d: (i, 0))
    vec = pl.BlockSpec((1, D_MODEL), lambda i, d: (0, 0))
    vec_shape = jax.ShapeDtypeStruct((1, D_MODEL), F32)
    return _matmul(
        "mm_down", "nn", u, w_down, (jax.ShapeDtypeStruct((T, D_MODEL), F32), vec_shape, vec_shape), (T // tm, J),
        pl.BlockSpec((None, tm, n), lambda i, d: (d, i, 0)), pl.BlockSpec((None, n, D_MODEL), lambda i, d: (d, 0, 0)),
        (row, vec, vec), (tm, D_MODEL), [x1, target, g_final], [row, row, vec], epilogue=_loss_epilogue)


def _mm_down_t(dx, w_down, tm=MM_ROWS):
    T = dx.shape[0]
    J, n, _ = w_down.shape
    tm = _tile(T, tm)
    return _matmul(
        "mm_down_t", "nt", dx, w_down, jax.ShapeDtypeStruct((J, T, n), BF16), (J, T // tm, 1),
        pl.BlockSpec((tm, D_MODEL), lambda d, i, k: (i, 0)), pl.BlockSpec((None, n, D_MODEL), lambda d, i, k: (d, 0, 0)),
        pl.BlockSpec((None, tm, n), lambda d, i, k: (d, i, 0)), (tm, n))


def _mm_dw_down(u, dx, tk=MM_TOKENS):
    J, T, n = u.shape
    tk = _tile(T, tk)
    return _matmul(
        "mm_dw_down", "tn", u, dx, jax.ShapeDtypeStruct((J, n, D_MODEL), BF16), (J, T // tk),
        pl.BlockSpec((None, tk, n), lambda d, k: (d, k, 0)), pl.BlockSpec((tk, D_MODEL), lambda d, k: (k, 0)),
        pl.BlockSpec((None, n, D_MODEL), lambda d, k: (d, 0, 0)), (n, D_MODEL))


def _mm_dw_up(h2, d_up, tk=MM_TOKENS):
    T, K = h2.shape
    tk = _tile(T, tk)
    return _matmul(
        "mm_dw_up", "tn", d_up, h2, jax.ShapeDtypeStruct((N_DEV, UP_SHARD, K), BF16), (N_DEV, T // tk),
        pl.BlockSpec((None, tk, UP_SHARD), lambda j, k: (j, k, 0)), pl.BlockSpec((tk, K), lambda j, k: (k, 0)),
        pl.BlockSpec((None, UP_SHARD, K), lambda j, k: (j, 0, 0)), (UP_SHARD, K))


def _mm_up_t(d_up, w_up_t, rms, tm=MM_ROWS_RES, comm=None):
    J, T, n = d_up.shape
    K = w_up_t.shape[2]
    tm = _tile(T, tm)
    fused = _rms_bwd_fused(T, K, tm, rms)
    return _matmul(
        "mm_up_t", "nn", d_up, w_up_t, fused.pop("out_shape"), (T // tm, J),
        pl.BlockSpec((None, tm, n), lambda i, j: (j, i, 0)), pl.BlockSpec((None, n, K), lambda i, j: (j, 0, 0)),
        fused.pop("o_spec"), (tm, K), comm=comm, **fused)


def _cast_shards(shards):
    def body(*refs):
        n = len(refs) // 2
        for src, dst in zip(refs[:n], refs[n:]):
            dst[...] = src[...].astype(dst.dtype)

    return pl.pallas_call(
        body, out_shape=[jax.ShapeDtypeStruct(s.shape, BF16) for s in shards], name="cast_shards",
        compiler_params=pltpu.CompilerParams(vmem_limit_bytes=VMEM_LIMIT),
    )(*shards)


def _adamw(w, g, m, v):
    m = ADAM_B1 * m + (1.0 - ADAM_B1) * g
    v = ADAM_B2 * v + (1.0 - ADAM_B2) * (g * g)
    m_hat = m / (1.0 - ADAM_B1 ** ADAM_STEP)
    v_hat = v / (1.0 - ADAM_B2 ** ADAM_STEP)
    delta = -ADAM_LR * (m_hat / (jnp.sqrt(v_hat) + ADAM_EPS) + ADAM_WD * w)
    return delta, m, v


def _sum_parts(p_ref):
    g = p_ref[0].astype(F32)
    for d in range(1, N_DEV):
        g = g + p_ref[d].astype(F32)
    return g


ADAM_ROWS = 512


def _reduce_adam(name, parts, w, m, v):
    R, Cn = w.shape
    by_rows = sum(p.shape[1] for p in parts) == R and len(parts) > 1
    common = math.gcd(*[p.shape[1] for p in parts])
    tr = max(t for t in range(8, min(common, ADAM_ROWS) + 1, 8) if common % t == 0)
    n_tiles = [p.shape[1] // tr for p in parts]
    first = [sum(n_tiles[:j]) for j in range(len(parts))] if by_rows else [0] * len(parts)

    def body(*refs):
        p_refs = refs[:len(parts)]
        w_ref, m_ref, v_ref, g_out, d_out, m_out, v_out = refs[len(parts):]

        def update(p_ref):
            g = _sum_parts(p_ref)
            delta, m_new, v_new = _adamw(w_ref[...], g, m_ref[...], v_ref[...])
            g_out[...] = g
            d_out[...] = delta
            m_out[...] = m_new
            v_out[...] = v_new

        if len(parts) == 1:
            update(p_refs[0])
        elif by_rows:
            i = pl.program_id(0)
            for p_ref, t0, n in zip(p_refs, first, n_tiles):
                pl.when((i >= t0) & (i < t0 + n))(functools.partial(update, p_ref))
        else:
            c = lax.axis_index("c")
            for side, p_ref in enumerate(p_refs):
                pl.when(c == side)(functools.partial(update, p_ref))

    def part_spec(t0, n):
        return pl.BlockSpec((N_DEV, tr, Cn), lambda i: (0, jnp.clip(i - t0, 0, n - 1), 0))

    row = pl.BlockSpec((tr, Cn), lambda i: (i, 0))
    shape = jax.ShapeDtypeStruct((R, Cn), F32)
    return pl.pallas_call(
        body, out_shape=(shape,) * 4, grid=(R // tr,),
        in_specs=[part_spec(t0, n) for t0, n in zip(first, n_tiles)] + [row, row, row],
        out_specs=(row,) * 4, name=name, compiler_params=_params(("parallel",)),
    )(*parts, w, m, v)


def _small_adam(name, gathered, params):
    n_g, n_p = len(gathered), len(params)

    def body(*refs):
        g_refs = refs[:n_g]
        wmv = refs[n_g:n_g + 3 * n_p]
        sums = refs[n_g + 3 * n_p:2 * n_g + 3 * n_p]
        upd = refs[2 * n_g + 3 * n_p:]
        for j in range(n_g):
            g = _sum_parts(g_refs[j])
            sums[j][...] = g
            if j < n_p:
                w_ref, m_ref, v_ref = wmv[3 * j:3 * j + 3]
                delta, m_new, v_new = _adamw(w_ref[...], g, m_ref[...], v_ref[...])
                upd[3 * j][...] = delta
                upd[3 * j + 1][...] = m_new
                upd[3 * j + 2][...] = v_new

    flat = [a for wmv in params for a in wmv]
    out_shape = [jax.ShapeDtypeStruct(g.shape[1:], F32) for g in gathered]
    out_shape += [jax.ShapeDtypeStruct(a.shape, F32) for a in flat]
    res = pl.pallas_call(body, out_shape=out_shape, name=name)(*gathered, *flat)
    return res[:n_g], [tuple(res[n_g + 3 * j:n_g + 3 * j + 3]) for j in range(n_p)]


def _adam_only(name, g, w, m, v):
    def body(g_ref, w_ref, m_ref, v_ref, d_out, m_out, v_out):
        delta, m_new, v_new = _adamw(w_ref[...], g_ref[...], m_ref[...], v_ref[...])
        d_out[...] = delta
        m_out[...] = m_new
        v_out[...] = v_new

    shape = jax.ShapeDtypeStruct(w.shape, F32)
    return pl.pallas_call(body, out_shape=(shape,) * 3, name=name)(g, w, m, v)


def kernel(x, mem, g_mix, w_in, w_pool, pool_scale, w_a, g_ret, b_ret, w_r, g_mem, w_mem_kv, w_c, w_out, g_ffn, w_up, conv_w, conv_b, w_down, g_final, loss_target, m_g_mix, m_w_in, m_w_pool, m_pool_scale, m_w_a, m_g_ret, m_b_ret, m_w_r, m_g_mem, m_w_mem_kv, m_w_c, m_w_out, m_g_ffn, m_w_up, m_conv_w, m_conv_b, m_w_down, m_g_final, v_g_mix, v_w_in, v_w_pool, v_pool_scale, v_w_a, v_g_ret, v_b_ret, v_w_r, v_g_mem, v_w_mem_kv, v_w_c, v_w_out, v_g_ffn, v_w_up, v_conv_w, v_conv_b, v_w_down, v_g_final):
    B, S, _ = x.shape
    M = mem.shape[1]
    T = B * S
    me = _my_index()
    x2d = x.reshape(T, D_MODEL)
    mem2d = mem.reshape(B * M, D_MODEL)
    tgt2d = loss_target.reshape(T, D_MODEL)
    g_final2 = g_final.reshape(1, D_MODEL)

    big = dict(w_in=w_in[0], w_a=w_a[0], w_r=w_r[0], w_mem_kv=w_mem_kv[0], w_c=w_c[0], w_out=w_out[0],
               w_up=w_up[0].T, w_down=w_down[0])
    names = list(big)
    cast = dict(zip(names, _cast_shards([big[n] for n in names])))
    cb = conv_b[0].reshape(FFN_SLABS, 1, UP_SHARD)
    wp = w_pool[0]
    tables = _ret_tables(S)

    h, h_first, h_rest = _rms_fwd("rms_mix", x2d, g_mix, split=W_IN_FIRST_ROWS)
    early = ("w_a", "w_r", "w_mem_kv", "w_c", "w_out")
    (proj, Win), landed = _mm_in_gather(h, cast["w_in"], comm=_Gather([cast[n] for n in early] + [conv_w[0]]))
    W = dict(zip(early, landed))
    half = D_MODEL // 2
    (yr, ret_states), (Wup_lo,) = _ret_fwd(proj, g_ret, b_ret, tables, B, S,
                                           comm=_Gather([cast["w_up"][:, :half]]))
    cw_full = landed[-1].transpose(1, 0, 2).reshape(3, FFN_HIDDEN)
    cw = cw_full.reshape(3, FFN_SLABS, UP_SHARD).transpose(1, 0, 2)
    Wa = W["w_a"].transpose(1, 0, 2).reshape(POOL_WIDTH, D_MODEL)
    Wc = W["w_c"].transpose(1, 0, 2).reshape(XA_WIDTH, D_MODEL)
    Wr = W["w_r"].reshape(D_MODEL, D_MODEL)
    Wkv = W["w_mem_kv"].reshape(D_MODEL, D_MODEL)
    Wout = W["w_out"].reshape(D_MODEL, D_MODEL)
    ypre = _pool_fwd(proj, wp, pool_scale, B, S)
    y_pool = _mm_rows("mm_a", ypre, Wa, BF16)
    y_ret = _mm_rows("mm_r", yr, Wr, BF16)
    mem_n = _rms_fwd("rms_mem", mem2d, g_mem)
    kv = _mm_rows("mm_kv", mem_n, Wkv)
    o_mem = _xa_fwd(proj, kv, B, S, M)[0]
    y_mem = _mm_rows("mm_c", o_mem, Wc, BF16)
    ys = (y_pool, y_ret, y_mem)
    (merged, x1, h2), (Wup_hi,) = _merge_out(proj, ys, Wout, x2d, g_ffn, comm=_Gather([cast["w_up"][:, half:]]))
    Wup = jnp.concatenate([Wup_lo, Wup_hi], axis=2)
    up, (Wdown,) = _mm_up(h2, Wup, comm=_Gather([cast["w_down"]]))
    up = up.reshape(2, FFN_SLABS, T, UP_SHARD)
    Wdown = Wdown.reshape(FFN_SLABS, UP_SHARD, D_MODEL)
    u = _glu_fwd(up, cw, cb, S)[0]

    dx2, dg_final, loss_part = _mm_down_loss(u, Wdown, x1, tgt2d, g_final2)
    received = {}
    d_u = _mm_down_t(dx2, Wdown)
    dW_down = _mm_dw_down(u, dx2)
    (d_up, d_cw, d_cb), (received["w_down"],) = _glu_bwd(
        up, d_u, cw, cb, S, comm=_Exchange([dW_down.reshape(N_DEV, -1, D_MODEL)]))
    d_up = d_up.reshape(N_DEV, T, UP_SHARD)
    dW_up = _mm_dw_up(h2, d_up)
    (dx1, dg_ffn), (up_c0,) = _mm_up_t(d_up, Wup, (x1, g_ffn, dx2), comm=_ExchangeTo([dW_up], 0))
    dW_out = _mm_tn("mm_dw_out", merged, dx1, BF16)
    (d_gl, d_y_pool, d_y_ret, d_y_mem), (up_c1,) = _merge_bwd(proj, ys, dx1, Wout, comm=_ExchangeTo([dW_up], 1))
    received["w_up"] = [up_c0, up_c1]
    dW_c = _mm_tn("mm_dw_c", o_mem, d_y_mem, BF16)
    d_o_mem = _mm_rows("mm_c_t", d_y_mem, Wc, kind="nt")
    (d_qx, d_kmem, d_vmem), (received["w_out"],) = _xa_bwd(
        proj, kv, d_o_mem, B, S, M, comm=_Exchange([dW_out.reshape(N_DEV, -1, D_MODEL)]))
    d_kv = jnp.concatenate([d_kmem, d_vmem], axis=1)
    dW_kv = _mm_tn("mm_dw_kv", mem_n, d_kv, BF16)
    d_mem_n = _mm_rows("mm_kv_t", d_kv, Wkv, kind="nt")
    dg_mem = _rms_bwd("rms_mem_bwd", mem2d, g_mem, d_mem_n, None)
    dW_a = _mm_tn("mm_dw_a", ypre, d_y_pool, BF16)
    d_ypre = _mm_rows("mm_a_t", d_y_pool, Wa, kind="nt")
    (d_hp, dw_pool, d_scale), (received["w_a"],) = _pool_bwd(
        proj, d_ypre, wp, pool_scale, B, S,
        comm=_Exchange([dW_a.reshape(POOL_WIDTH, N_DEV, -1).transpose(1, 0, 2)]))
    dW_r = _mm_tn("mm_dw_r", yr, d_y_ret, BF16)
    d_yr = _mm_rows("mm_r_t", d_y_ret, Wr, kind="nt")
    (d_q, d_k, d_v, d_gr, dg_ret, db_ret), landed = _ret_bwd(
        proj, ret_states, d_yr, g_ret, b_ret, tables, B, S,
        comm=_Exchange([dW_r.reshape(N_DEV, -1, D_MODEL), dW_c.reshape(XA_WIDTH, N_DEV, -1).transpose(1, 0, 2),
                        dW_kv.reshape(N_DEV, -1, D_MODEL)]))
    received["w_r"], received["w_c"], received["w_mem_kv"] = landed
    small_names = ["w_pool", "pool_scale", "g_ret", "b_ret", "g_mem", "g_ffn", "conv_b", "g_final"]
    small_grads = [dw_pool, d_scale, dg_ret, db_ret, dg_mem, dg_ffn, d_cb.reshape(1, FFN_HIDDEN), dg_final,
                   d_cw.transpose(1, 0, 2).reshape(3, FFN_HIDDEN), loss_part]
    d_proj = jnp.concatenate([d_hp, d_q, d_k, d_v, d_gr, d_qx, d_gl], axis=1)
    dW_in0, small_all = _mm_tn_slab("mm_dw_in0", h_first, d_proj, IN_SHARD, BF16,
                                    comm=_Exchange([], whole=small_grads))
    dW_in1, (in0,) = _mm_tn_slab("mm_dw_in1", h_rest, d_proj, IN_SHARD, BF16,
                                 comm=_Exchange([dW_in0]))
    (grad_x, dg_mix), (in1,) = _mm_cols_slab_t("mm_in_t", d_proj, Win, (x2d, g_mix, dx1), comm=_Exchange([dW_in1]))
    received["w_in"] = [in0, in1]
    (g_mix_all,) = _comm_call("gather_g_mix", _Exchange([], whole=[dg_mix]))

    args = dict(g_mix=g_mix, w_in=w_in, w_pool=w_pool, pool_scale=pool_scale, w_a=w_a, g_ret=g_ret, b_ret=b_ret,
                w_r=w_r, g_mem=g_mem, w_mem_kv=w_mem_kv, w_c=w_c, w_out=w_out, g_ffn=g_ffn, w_up=w_up,
                conv_w=conv_w, conv_b=conv_b, w_down=w_down, g_final=g_final)
    m_in = dict(g_mix=m_g_mix, w_in=m_w_in, w_pool=m_w_pool, pool_scale=m_pool_scale, w_a=m_w_a, g_ret=m_g_ret,
                b_ret=m_b_ret, w_r=m_w_r, g_mem=m_g_mem, w_mem_kv=m_w_mem_kv, w_c=m_w_c, w_out=m_w_out,
                g_ffn=m_g_ffn, w_up=m_w_up, conv_w=m_conv_w, conv_b=m_conv_b, w_down=m_w_down, g_final=m_g_final)
    v_in = dict(g_mix=v_g_mix, w_in=v_w_in, w_pool=v_w_pool, pool_scale=v_pool_scale, w_a=v_w_a, g_ret=v_g_ret,
                b_ret=v_b_ret, w_r=v_w_r, g_mem=v_g_mem, w_mem_kv=v_w_mem_kv, w_c=v_w_c, w_out=v_w_out,
                g_ffn=v_g_ffn, w_up=v_w_up, conv_w=v_conv_w, conv_b=v_conv_b, w_down=v_w_down, g_final=v_g_final)

    grads, deltas, new_m, new_v = {}, {}, {}, {}
    for n in names:
        parts = received[n] if isinstance(received[n], list) else [received[n]]
        flip = (lambda a: a.T) if n == "w_up" else (lambda a: a)
        outs = _reduce_adam("adam_" + n, parts, big[n], flip(m_in[n][0]), flip(v_in[n][0]))
        for store, val in zip((grads, deltas, new_m, new_v), outs):
            store[n] = flip(val)[None]

    def as_small(a):
        return a.reshape(a.shape[-3:]) if a.ndim > 2 else a.reshape(1, -1)

    def small_update(call_name, param_names, gathered):
        params = [tuple(as_small(d[n]) for d in (args, m_in, v_in)) for n in param_names]
        sums, updates = _small_adam(call_name, gathered, params)
        for n, g, (d_, m_, v_) in zip(param_names, sums, updates):
            shape = args[n].shape
            grads[n], deltas[n], new_m[n], new_v[n] = (a.reshape(shape) for a in (g, d_, m_, v_))
        return sums[len(param_names):]

    g_cw_full, loss_row = small_update("adam_small", small_names, small_all)
    loss = loss_row[0, 0]
    small_update("adam_g_mix", ["g_mix"], [g_mix_all])

    shard_cols = FFN_HIDDEN // N_DEV
    g_cw = lax.dynamic_slice_in_dim(g_cw_full, me * shard_cols, shard_cols, axis=1)
    d_, m_, v_ = _adam_only("adam_conv_w", g_cw, conv_w[0], m_conv_w[0], v_conv_w[0])
    grads["conv_w"], deltas["conv_w"], new_m["conv_w"], new_v["conv_w"] = g_cw[None], d_[None], m_[None], v_[None]

    order = ["g_mix", "w_in", "w_pool", "pool_scale", "w_a", "g_ret", "b_ret", "w_r", "g_mem", "w_mem_kv", "w_c",
             "w_out", "g_ffn", "w_up", "conv_w", "conv_b", "w_down", "g_final"]
    return (loss, grad_x.reshape(B, S, D_MODEL), *[grads[n] for n in order], *[deltas[n] for n in order],
            *[new_m[n] for n in order], *[new_v[n] for n in order])
```

```python
import functools
import math

import jax
import jax.numpy as jnp
from jax import lax
from jax.experimental import pallas as pl
from jax.experimental.pallas import tpu as pltpu

F32 = jnp.float32
BF16 = jnp.bfloat16

N_DEV = 8
D_MODEL = 1024
POOL_WINDOWS = (2, 4, 8, 16)
POOL_GROUP_DIM = 128
POOL_WIDTH = 512
POOL_HALO = 16
RET_HEADS = 4
RET_QK_DIM = 128
RET_V_DIM = 256
RET_CHUNK = 128
ROPE_BASE = 10000.0
XA_HEADS = 4
XA_HEAD_DIM = 128
XA_WIDTH = 512
IN_WIDTH = 7168
IN_SHARD = IN_WIDTH // N_DEV
FFN_HIDDEN = 2816
UP_SHARD = 2 * FFN_HIDDEN // N_DEV
FFN_SLABS = FFN_HIDDEN // UP_SHARD
EPS = 1e-6
ADAM_LR = 0.001
ADAM_B1 = 0.9
ADAM_B2 = 0.999
ADAM_EPS = 1e-08
ADAM_WD = 0.01
ADAM_STEP = 10
GELU_C = math.sqrt(2.0 / math.pi)
GELU_A = 0.044715
VMEM_LIMIT = 56 * 1024 * 1024
MM_ROWS = 2048
MM_ROWS_RES = 1024
MM_TOKENS = 2048
W_IN_FIRST_ROWS = 384
MESH = pl.DeviceIdType.MESH

COL_Q, COL_K, COL_V, COL_GR, COL_QX, COL_GL = 512, 1024, 1536, 2560, 3584, 4096

_DIMS = {
    "nn": (((1,), (0,)), ((), ())),
    "nt": (((1,), (1,)), ((), ())),
    "tn": (((0,), (0,)), ((), ())),
}


def _dot(a, b, kind="nn"):
    return lax.dot_general(a.astype(BF16), b.astype(BF16), _DIMS[kind], preferred_element_type=F32)


def _params(sem, vmem=VMEM_LIMIT):
    return pltpu.CompilerParams(dimension_semantics=sem, vmem_limit_bytes=vmem)


def _tile(n, pref):
    t = min(n, pref)
    while n % t:
        t //= 2
    return t


def _mesh_pos():
    return lax.axis_index("x"), lax.axis_index("y"), lax.axis_index("c")


def _dev_index(x, y, c):
    return 4 * x + 2 * y + c


def _my_index():
    return _dev_index(*_mesh_pos())


def _remote(src, dst, send_sems, recv_sems, s, to):
    return pltpu.make_async_remote_copy(src_ref=src, dst_ref=dst, send_sem=send_sems.at[s], recv_sem=recv_sems.at[s],
                                        device_id=to, device_id_type=MESH)


class _Gather:
    def __init__(self, shards):
        self.inputs = list(shards)
        self.out_shapes = [jax.ShapeDtypeStruct((N_DEV,) + s.shape, s.dtype) for s in shards]
        n = len(shards)
        self.sem_shapes = [pltpu.SemaphoreType.DMA((7 * n,)), pltpu.SemaphoreType.DMA((7 * n,)),
                           pltpu.SemaphoreType.DMA((n,))]

    def _places(self):
        x, y, c = _mesh_pos()
        return (x, y, c), (x, y, 1 - c), [(1 - x, y), (x, 1 - y), (1 - x, 1 - y)]

    def _local(self, src, dst, sems):
        me = _my_index()
        return [pltpu.make_async_copy(src[w], dst[w].at[me], sems[2].at[w]) for w in range(len(src))]

    def start(self, src, dst, sems):
        me, sib, chips = self._places()
        for cp in self._local(src, dst, sems):
            cp.start()
        for w in range(len(src)):
            land = dst[w].at[_dev_index(*me)]
            _remote(src[w], land, sems[0], sems[1], 7 * w, sib).start()
            for j, chip in enumerate(chips):
                _remote(src[w], land, sems[0], sems[1], 7 * w + 1 + j, (*chip, me[2])).start()

    def middle(self, src, dst, sems):
        me, sib, chips = self._places()
        for j, chip in enumerate(chips):
            for w in range(len(src)):
                block = dst[w].at[_dev_index(*chip, me[2])]
                _remote(src[w], block, sems[0], sems[1], 7 * w + 1 + j, me).wait_recv()
                _remote(block, block, sems[0], sems[1], 7 * w + 4 + j, sib).start()

    def finish(self, src, dst, sems):
        me, sib, chips = self._places()
        n = len(src)
        for w in range(n):
            _remote(src[w], dst[w].at[_dev_index(*sib)], sems[0], sems[1], 7 * w, me).wait_recv()
            for j, chip in enumerate(chips):
                block = dst[w].at[_dev_index(*chip, sib[2])]
                _remote(block, block, sems[0], sems[1], 7 * w + 4 + j, me).wait_recv()
            for k in range(7):
                _remote(src[w], dst[w].at[0], sems[0], sems[1], 7 * w + k, me).wait_send()
        for cp in self._local(src, dst, sems):
            cp.wait()


class _Exchange:
    def __init__(self, partials, whole=()):
        self.n_part = len(partials)
        self.inputs = list(partials) + list(whole)
        self.out_shapes = [jax.ShapeDtypeStruct(p.shape, p.dtype) for p in partials]
        self.out_shapes += [jax.ShapeDtypeStruct((N_DEV,) + a.shape, a.dtype) for a in whole]
        n = len(self.inputs)
        self.sem_shapes = [pltpu.SemaphoreType.DMA((7 * n,)), pltpu.SemaphoreType.DMA((7 * n,)),
                           pltpu.SemaphoreType.DMA((n,))]

    def _peer(self, k):
        x, y, c = _mesh_pos()
        p = (x ^ ((k >> 2) & 1), y ^ ((k >> 1) & 1), c ^ (k & 1))
        return p, _dev_index(*p)

    def _source(self, src, w, slot):
        return src[w].at[slot] if w < self.n_part else src[w]

    def _local(self, src, dst, sems):
        me = _my_index()
        return [pltpu.make_async_copy(self._source(src, w, me), dst[w].at[me], sems[2].at[w])
                for w in range(len(src))]

    def start(self, src, dst, sems):
        me = _my_index()
        for cp in self._local(src, dst, sems):
            cp.start()
        for k in range(1, N_DEV):
            peer, peer_idx = self._peer(k)
            for w in range(len(src)):
                _remote(self._source(src, w, peer_idx), dst[w].at[me], sems[0], sems[1], 7 * w + k - 1, peer).start()

    def finish(self, src, dst, sems):
        for k in range(1, N_DEV):
            peer, peer_idx = self._peer(k)
            for w in range(len(src)):
                cp = _remote(self._source(src, w, peer_idx), dst[w].at[peer_idx], sems[0], sems[1], 7 * w + k - 1, peer)
                cp.wait_send()
                cp.wait_recv()
        for cp in self._local(src, dst, sems):
            cp.wait()


class _ExchangeTo:
    def __init__(self, partials, side):
        self.side = side
        self.inputs = list(partials)
        self.out_shapes = [jax.ShapeDtypeStruct(p.shape, p.dtype) for p in partials]
        n = len(partials)
        self.sem_shapes = [pltpu.SemaphoreType.DMA((7 * n,)), pltpu.SemaphoreType.DMA((7 * n,)),
                           pltpu.SemaphoreType.DMA((n,))]

    def _copies(self, src, dst, sems):
        x, y, c = _mesh_pos()
        me = _dev_index(x, y, c)
        receives = c == self.side
        remote = []
        for k in range(1, N_DEV):
            kx, ky, kc = (k >> 2) & 1, (k >> 1) & 1, k & 1
            peer = (x ^ kx, y ^ ky, c ^ kc)
            peer_idx = _dev_index(*peer)
            sends = c == (self.side ^ kc)
            for w in range(len(src)):
                slab = src[w].at[peer_idx]
                s = 7 * w + k - 1
                remote.append((sends, _remote(slab, dst[w].at[me], sems[0], sems[1], s, peer),
                               _remote(slab, dst[w].at[peer_idx], sems[0], sems[1], s, peer)))
        local = [pltpu.make_async_copy(src[w].at[me], dst[w].at[me], sems[2].at[w]) for w in range(len(src))]
        return receives, remote, local

    def start(self, src, dst, sems):
        receives, remote, local = self._copies(src, dst, sems)

        @pl.when(receives)
        def _():
            for cp in local:
                cp.start()

        for sends, send, _ in remote:
            pl.when(sends)(send.start)

    def finish(self, src, dst, sems):
        receives, remote, local = self._copies(src, dst, sems)
        for sends, send, arrive in remote:
            pl.when(sends)(send.wait_send)
            pl.when(receives)(arrive.wait_recv)

        @pl.when(receives)
        def _():
            for cp in local:
                cp.wait()


def _pcall(body, args, *, name, out_shape, grid, in_specs, out_specs, scratch_shapes=(), sem=None, comm=None):
    single = not isinstance(out_shape, (tuple, list))
    outs = [out_shape] if single else list(out_shape)
    ospecs = [out_specs] if single else list(out_specs)
    n_in, n_out, n_scr = len(args), len(outs), len(scratch_shapes)

    def pick(res):
        return res[0] if single else tuple(res[:n_out])

    if comm is None:
        res = pl.pallas_call(
            body, out_shape=outs, grid=grid, in_specs=list(in_specs), out_specs=ospecs,
            scratch_shapes=list(scratch_shapes), name=name, compiler_params=_params(sem),
        )(*args)
        return pick(res), ()

    nci, nco = len(comm.inputs), len(comm.out_shapes)

    def carrier(*refs):
        at = 0
        parts = []
        for size in (n_in, nci, n_out, nco, n_scr, len(comm.sem_shapes)):
            parts.append(refs[at:at + size])
            at += size
        ins, cins, o, couts, scr, sems = parts
        ids = [pl.program_id(a) for a in range(len(grid))]
        first = functools.reduce(jnp.logical_and, [i == 0 for i in ids])
        last = functools.reduce(jnp.logical_and, [i == g - 1 for i, g in zip(ids, grid)])

        body(*ins, *o, *scr)

        @pl.when(first)
        def _():
            comm.start(cins, couts, sems)

        if hasattr(comm, "middle"):
            steps = math.prod(grid)
            at = functools.reduce(lambda lin, ig: lin * ig[1] + ig[0], zip(ids, grid), 0)

            @pl.when(at == min(steps - 1, (3 * steps) // 4))
            def _():
                comm.middle(cins, couts, sems)

        @pl.when(last)
        def _():
            comm.finish(cins, couts, sems)

    hbm = pl.BlockSpec(memory_space=pltpu.HBM)
    res = pl.pallas_call(
        carrier, out_shape=outs + comm.out_shapes, grid=grid, in_specs=list(in_specs) + [hbm] * nci,
        out_specs=ospecs + [hbm] * nco, scratch_shapes=list(scratch_shapes) + comm.sem_shapes, name=name,
        compiler_params=_params(("arbitrary",) * len(grid)),
    )(*args, *comm.inputs)
    return pick(res), tuple(res[n_out:])


def _comm_call(name, comm):
    def body(*refs):
        nci, nco = len(comm.inputs), len(comm.out_shapes)
        cins, couts, sems = refs[:nci], refs[nci:nci + nco], refs[nci + nco:]
        comm.start(cins, couts, sems)
        if hasattr(comm, "middle"):
            comm.middle(cins, couts, sems)
        comm.finish(cins, couts, sems)

    hbm = pl.BlockSpec(memory_space=pltpu.HBM)
    return pl.pallas_call(
        body, out_shape=comm.out_shapes, in_specs=[hbm] * len(comm.inputs), out_specs=[hbm] * len(comm.out_shapes),
        scratch_shapes=comm.sem_shapes, name=name,
    )(*comm.inputs)


def _matmul(name, kind, a, b, out_shape, grid, a_spec, b_spec, o_spec, acc_shape, res=None, res_spec=None,
            comm=None, epilogue=None):
    nk = grid[-1]
    if epilogue is None:
        extra, extra_specs = ([res], [res_spec]) if res is not None else ([], [])
        n_out = 1
    else:
        extra, extra_specs, n_out = list(res), list(res_spec), len(out_shape)
    n_in = 2 + len(extra)

    def body(*refs):
        a_ref, b_ref = refs[0], refs[1]
        extra_refs, out_refs = refs[2:n_in], refs[n_in:n_in + n_out]

        def prod():
            return _dot(a_ref[...], b_ref[...], kind)

        def finish(acc):
            if epilogue is not None:
                ids = [pl.program_id(ax) for ax in range(len(grid) - 1)]
                first = functools.reduce(jnp.logical_and, [i == 0 for i in ids]) if ids else True
                epilogue(acc, extra_refs, out_refs, first)
                return
            if extra_refs:
                acc = acc + extra_refs[0][...]
            out_refs[0][...] = acc.astype(out_refs[0].dtype)

        if nk == 1:
            finish(prod())
        else:
            acc_ref = refs[n_in + n_out]
            k = pl.program_id(len(grid) - 1)

            @pl.when(k == 0)
            def _():
                acc_ref[...] = prod()

            @pl.when(k > 0)
            def _():
                acc_ref[...] += prod()

            @pl.when(k == nk - 1)
            def _():
                finish(acc_ref[...])

    in_specs = [a_spec, b_spec] + extra_specs
    args = (a, b, *extra)
    scratch = [pltpu.VMEM(acc_shape, F32)] if nk > 1 else []
    sem = ("arbitrary",) * len(grid) if epilogue is not None else ("parallel",) * (len(grid) - 1) + ("arbitrary",)
    out, landed = _pcall(body, args, name=name, out_shape=out_shape, grid=grid, in_specs=in_specs,
                         out_specs=o_spec, scratch_shapes=scratch, sem=sem, comm=comm)
    return out if comm is None else (out, landed)


def _mm_rows(name, a, w, out_dtype=F32, res=None, kind="nn", tm=MM_ROWS, comm=None):
    M, K = a.shape
    N = w.shape[1] if kind == "nn" else w.shape[0]
    tm = _tile(M, tm)
    res_spec = pl.BlockSpec((tm, N), lambda i, k: (i, 0)) if res is not None else None
    return _matmul(
        name, kind, a, w, jax.ShapeDtypeStruct((M, N), out_dtype), (M // tm, 1),
        pl.BlockSpec((tm, K), lambda i, k: (i, 0)), pl.BlockSpec(w.shape, lambda i, k: (0, 0)),
        pl.BlockSpec((tm, N), lambda i, k: (i, 0)), (tm, N), res, res_spec, comm)


def _mm_tn(name, a, b, out_dtype=F32, tk=MM_TOKENS, comm=None):
    T, M = a.shape
    N = b.shape[1]
    tk = _tile(T, tk)
    return _matmul(
        name, "tn", a, b, jax.ShapeDtypeStruct((M, N), out_dtype), (1, T // tk),
        pl.BlockSpec((tk, M), lambda i, k: (k, 0)), pl.BlockSpec((tk, N), lambda i, k: (k, 0)),
        pl.BlockSpec((M, N), lambda i, k: (0, 0)), (M, N), comm=comm)


def _mm_in_gather(h, shard, tm=MM_ROWS, comm=None):
    T, K = h.shape
    n = shard.shape[1]
    tm = _tile(T, tm)
    n_tiles = T // tm
    pair_of_chip_step = {4: 1, 2: 2, 6: 3}

    def slab_of(s):
        x, y, c = _mesh_pos()
        return _dev_index(x ^ ((s >> 2) & 1), y ^ ((s >> 1) & 1), c ^ (s & 1))

    def body(h_ref, shard_ref, proj_ref, win_ref, wbuf, slot_sems, send_sems, recv_sems, local_sem):
        s, i = pl.program_id(0), pl.program_id(1)
        x, y, c = _mesh_pos()
        me, sib = (x, y, c), (x, y, 1 - c)

        def slot_copy(step):
            src = shard_ref if step == 0 else win_ref.at[slab_of(step)]
            return pltpu.make_async_copy(src, wbuf.at[step % 2], slot_sems.at[step % 2])

        def fetch(step):
            if step >= 1:
                block = win_ref.at[slab_of(step)]
                if step == 1:
                    pair = 0
                elif step % 2 == 0:
                    pair = pair_of_chip_step[step]
                else:
                    pair = 3 + pair_of_chip_step[step - 1]
                _remote(block, block, send_sems, recv_sems, pair, me).wait_recv()
                if step % 2 == 0:
                    _remote(block, block, send_sems, recv_sems, 3 + pair, sib).start()
            slot_copy(step).start()

        @pl.when((s == 0) & (i == 0))
        def _():
            land = win_ref.at[_dev_index(*me)]
            pltpu.make_async_copy(shard_ref, land, local_sem).start()
            _remote(shard_ref, land, send_sems, recv_sems, 0, sib).start()
            for step, pair in pair_of_chip_step.items():
                peer = (x ^ ((step >> 2) & 1), y ^ ((step >> 1) & 1), c)
                _remote(shard_ref, land, send_sems, recv_sems, pair, peer).start()
            fetch(0)

        for step in range(N_DEV):
            @pl.when((s == step) & (i == 0))
            def _():
                slot_copy(step).wait()

            if step + 1 < N_DEV:
                @pl.when((s == step) & (i == n_tiles - 1))
                def _():
                    fetch(step + 1)

        proj_ref[...] = _dot(h_ref[...], wbuf[s % 2])

        @pl.when((s == N_DEV - 1) & (i == n_tiles - 1))
        def _():
            for pair in range(7):
                _remote(shard_ref, win_ref.at[0], send_sems, recv_sems, pair, me).wait_send()
            pltpu.make_async_copy(shard_ref, win_ref.at[_dev_index(*me)], local_sem).wait()

    hbm = pl.BlockSpec(memory_space=pltpu.HBM)
    return _pcall(
        body, (h, shard), name="mm_in",
        out_shape=(jax.ShapeDtypeStruct((T, N_DEV * n), F32), jax.ShapeDtypeStruct((N_DEV, K, n), shard.dtype)),
        grid=(N_DEV, n_tiles), in_specs=[pl.BlockSpec((tm, K), lambda s, i: (i, 0)), hbm],
        out_specs=(pl.BlockSpec((tm, n), lambda s, i: (i, slab_of(s))), hbm),
        scratch_shapes=[pltpu.VMEM((2, K, n), shard.dtype), pltpu.SemaphoreType.DMA((2,)),
                        pltpu.SemaphoreType.DMA((7,)), pltpu.SemaphoreType.DMA((7,)), pltpu.SemaphoreType.DMA],
        sem=("arbitrary", "arbitrary"), comm=comm)


def _rms_bwd_epilogue(dh, operands, outputs, first):
    x_ref, g_ref, dres_ref = operands
    dx_ref, dg_ref = outputs
    xv = x_ref[...]
    r = lax.rsqrt(jnp.mean(xv * xv, axis=-1, keepdims=True) + EPS)
    xhat = xv * r

    @pl.when(first)
    def _():
        dg_ref[...] = jnp.zeros_like(dg_ref)

    dg_ref[...] += jnp.sum(dh * xhat, axis=0, keepdims=True)
    dxhat = dh * g_ref[...]
    dx_ref[...] = dres_ref[...] + r * (dxhat - xhat * jnp.mean(dxhat * xhat, axis=-1, keepdims=True))


def _rms_bwd_fused(M, K, tm, rms):
    row = pl.BlockSpec((tm, K), lambda i, j: (i, 0))
    vec = pl.BlockSpec((1, K), lambda i, j: (0, 0))
    x, g, dres = rms
    return dict(res=[x, g, dres], res_spec=[row, vec, row], epilogue=_rms_bwd_epilogue,
                out_shape=(jax.ShapeDtypeStruct((M, K), F32), jax.ShapeDtypeStruct((1, K), F32)), o_spec=(row, vec))


def _mm_cols_slab_t(name, a, w_slabs, rms, tm=MM_ROWS_RES, comm=None):
    M = a.shape[0]
    J, K, n = w_slabs.shape
    tm = _tile(M, tm)
    fused = _rms_bwd_fused(M, K, tm, rms)
    return _matmul(
        name, "nt", a, w_slabs, fused.pop("out_shape"), (M // tm, J),
        pl.BlockSpec((tm, n), lambda i, j: (i, j)), pl.BlockSpec((None, K, n), lambda i, j: (j, 0, 0)),
        fused.pop("o_spec"), (tm, K), comm=comm, **fused)


def _mm_tn_slab(name, a, b, n, out_dtype=F32, tk=MM_TOKENS, comm=None):
    T, M = a.shape
    J = b.shape[1] // n
    tk = _tile(T, tk)
    return _matmul(
        name, "tn", a, b, jax.ShapeDtypeStruct((J, M, n), out_dtype), (J, T // tk),
        pl.BlockSpec((tk, M), lambda j, k: (k, 0)), pl.BlockSpec((tk, n), lambda j, k: (k, j)),
        pl.BlockSpec((None, M, n), lambda j, k: (j, 0, 0)), (M, n), comm=comm)


def _rms_fwd(name, x, g, tm=512, split=None):
    T, Dm = x.shape
    tm = _tile(T, tm)
    widths = [Dm] if split is None else [Dm, split, Dm - split]

    def body(x_ref, g_ref, *h_refs):
        xv = x_ref[...]
        r = lax.rsqrt(jnp.mean(xv * xv, axis=-1, keepdims=True) + EPS)
        h = (xv * r * g_ref[...]).astype(BF16)
        h_refs[0][...] = h
        if split is not None:
            h_refs[1][...] = h[:, :split]
            h_refs[2][...] = h[:, split:]

    out = pl.pallas_call(
        body, out_shape=[jax.ShapeDtypeStruct((T, w), BF16) for w in widths], grid=(T // tm,),
        in_specs=[pl.BlockSpec((tm, Dm), lambda i: (i, 0)), pl.BlockSpec((1, Dm), lambda i: (0, 0))],
        out_specs=[pl.BlockSpec((tm, w), lambda i: (i, 0)) for w in widths], name=name,
        compiler_params=_params(("parallel",)),
    )(x, g)
    return out[0] if split is None else tuple(out)


def _rms_bwd(name, x, g, dh, dres, tm=512):
    T, Dm = x.shape
    tm = _tile(T, tm)
    want_dx = dres is not None

    def body(*refs):
        if want_dx:
            x_ref, g_ref, dh_ref, dres_ref, dx_ref, dg_ref = refs
        else:
            x_ref, g_ref, dh_ref, dg_ref = refs
        xv = x_ref[...]
        r = lax.rsqrt(jnp.mean(xv * xv, axis=-1, keepdims=True) + EPS)
        xhat = xv * r
        dhv = dh_ref[...]

        @pl.when(pl.program_id(0) == 0)
        def _():
            dg_ref[...] = jnp.zeros_like(dg_ref)

        dg_ref[...] += jnp.sum(dhv * xhat, axis=0, keepdims=True)
        if want_dx:
            dxhat = dhv * g_ref[...]
            dx_ref[...] = dres_ref[...] + r * (dxhat - xhat * jnp.mean(dxhat * xhat, axis=-1, keepdims=True))

    row = pl.BlockSpec((tm, Dm), lambda i: (i, 0))
    vec = pl.BlockSpec((1, Dm), lambda i: (0, 0))
    if want_dx:
        return pl.pallas_call(
            body, out_shape=(jax.ShapeDtypeStruct((T, Dm), F32), jax.ShapeDtypeStruct((1, Dm), F32)),
            grid=(T // tm,), in_specs=[row, vec, row, row], out_specs=(row, vec), name=name,
            compiler_params=_params(("arbitrary",)),
        )(x, g, dh, dres)
    return pl.pallas_call(
        body, out_shape=jax.ShapeDtypeStruct((1, Dm), F32), grid=(T // tm,), in_specs=[row, vec, row],
        out_specs=vec, name=name, compiler_params=_params(("arbitrary",)),
    )(x, g, dh)


def _pool_rows(S):
    return _tile(S, 256)


def _pool_count(c0, rows, w):
    t = c0 + lax.broadcasted_iota(jnp.int32, (rows, 1), 0)
    return jnp.minimum(t + 1, w).astype(F32)


def _pool_fwd(proj, w_pool, scale, B, S):
    CH = _pool_rows(S)

    def body(hp_ref, wp_ref, sc_ref, o_ref, pad_ref):
        pad_ref[0:POOL_HALO, :] = jnp.zeros((POOL_HALO, POOL_WIDTH), F32)
        pad_ref[POOL_HALO:, :] = hp_ref[...]
        for gi, w in enumerate(POOL_WINDOWS):
            cols = slice(gi * POOL_GROUP_DIM, (gi + 1) * POOL_GROUP_DIM)
            for c in range(S // CH):
                base = POOL_HALO + c * CH
                acc = pad_ref[base:base + CH, cols]
                tok = acc
                for j in range(1, w):
                    acc = acc + pad_ref[base - j:base - j + CH, cols]
                pooled = acc / _pool_count(c * CH, CH, w) - tok
                z = _dot(pooled, wp_ref[gi])
                o_ref[c * CH:(c + 1) * CH, cols] = (z * sc_ref[:, cols]).astype(o_ref.dtype)

    return pl.pallas_call(
        body, out_shape=jax.ShapeDtypeStruct((B * S, POOL_WIDTH), BF16), grid=(B,),
        in_specs=[pl.BlockSpec((S, POOL_WIDTH), lambda b: (b, 0)),
                  pl.BlockSpec(w_pool.shape, lambda b: (0, 0, 0)),
                  pl.BlockSpec((1, POOL_WIDTH), lambda b: (0, 0))],
        out_specs=pl.BlockSpec((S, POOL_WIDTH), lambda b: (b, 0)),
        scratch_shapes=[pltpu.VMEM((S + POOL_HALO, POOL_WIDTH), F32)],
        name="pool_fwd", compiler_params=_params(("parallel",)),
    )(proj, w_pool, scale)


def _pool_bwd(proj, d_ypre, w_pool, scale, B, S, comm=None):
    CH = _pool_rows(S)

    def body(hp_ref, dy_ref, wp_ref, sc_ref, dhp_ref, dwp_ref, dsc_ref, pad_ref, sc_pad_ref, dp_ref):
        @pl.when(pl.program_id(0) == 0)
        def _():
            dwp_ref[...] = jnp.zeros_like(dwp_ref)
            dsc_ref[...] = jnp.zeros_like(dsc_ref)

        pad_ref[0:POOL_HALO, :] = jnp.zeros((POOL_HALO, POOL_WIDTH), F32)
        pad_ref[POOL_HALO:, :] = hp_ref[...]
        sc_pad_ref[S:, :] = jnp.zeros((POOL_HALO, POOL_WIDTH), F32)
        for gi, w in enumerate(POOL_WINDOWS):
            cols = slice(gi * POOL_GROUP_DIM, (gi + 1) * POOL_GROUP_DIM)
            for c in range(S // CH):
                base = POOL_HALO + c * CH
                rows = slice(c * CH, (c + 1) * CH)
                acc = pad_ref[base:base + CH, cols]
                tok = acc
                for j in range(1, w):
                    acc = acc + pad_ref[base - j:base - j + CH, cols]
                cnt = _pool_count(c * CH, CH, w)
                pooled = acc / cnt - tok
                z = _dot(pooled, wp_ref[gi])
                dy = dy_ref[rows, cols]
                dsc_ref[:, cols] += jnp.sum(dy * z, axis=0, keepdims=True)
                dz = dy * sc_ref[:, cols]
                dwp_ref[gi] += _dot(pooled, dz, "tn")
                dpool = _dot(dz, wp_ref[gi], "nt")
                dp_ref[rows, cols] = dpool
                sc_pad_ref[rows, cols] = dpool / cnt
            for c in range(S // CH):
                rows = slice(c * CH, (c + 1) * CH)
                acc = sc_pad_ref[rows, cols]
                for j in range(1, w):
                    acc = acc + sc_pad_ref[c * CH + j:c * CH + j + CH, cols]
                dhp_ref[rows, cols] = (acc - dp_ref[rows, cols]).astype(dhp_ref.dtype)

    seq = pl.BlockSpec((S, POOL_WIDTH), lambda b: (b, 0))
    return _pcall(
        body, (proj, d_ypre, w_pool, scale), name="pool_bwd",
        out_shape=(jax.ShapeDtypeStruct((B * S, POOL_WIDTH), BF16),
                   jax.ShapeDtypeStruct(w_pool.shape, F32), jax.ShapeDtypeStruct((1, POOL_WIDTH), F32)),
        grid=(B,),
        in_specs=[seq, seq, pl.BlockSpec(w_pool.shape, lambda b: (0, 0, 0)),
                  pl.BlockSpec((1, POOL_WIDTH), lambda b: (0, 0))],
        out_specs=(seq, pl.BlockSpec(w_pool.shape, lambda b: (0, 0, 0)),
                   pl.BlockSpec((1, POOL_WIDTH), lambda b: (0, 0))),
        scratch_shapes=[pltpu.VMEM((S + POOL_HALO, POOL_WIDTH), F32),
                        pltpu.VMEM((S + POOL_HALO, POOL_WIDTH), F32),
                        pltpu.VMEM((S, POOL_WIDTH), F32)],
        sem=("arbitrary",), comm=comm)


def _ret_tables(S):
    half = RET_QK_DIM // 2
    inv = ROPE_BASE ** (-jnp.arange(half, dtype=F32) / half)
    ang = jnp.arange(S, dtype=F32)[:, None] * inv[None, :]
    cos, sin = jnp.cos(ang), jnp.sin(ang)
    cos_full = jnp.concatenate([cos, cos], axis=-1)
    sin_signed = jnp.concatenate([-sin, sin], axis=-1)
    C = RET_CHUNK
    lg = jnp.log1p(-jnp.exp2(-5.0 - jnp.arange(RET_HEADS, dtype=F32)))[:, None, None]
    idx = jnp.arange(C, dtype=F32)
    rel = idx[:, None] - idx[None, :]
    decay = jnp.where(rel >= 0, jnp.exp(jnp.maximum(rel, 0.0) * lg), 0.0)
    q_decay = jnp.broadcast_to(jnp.exp((idx + 1.0)[None, :, None] * lg), (RET_HEADS, C, RET_QK_DIM))
    k_decay = jnp.broadcast_to(jnp.exp((C - 1.0 - idx)[None, :, None] * lg), (RET_HEADS, C, RET_QK_DIM))
    c_decay = jnp.broadcast_to(jnp.exp(C * lg), (RET_HEADS, 1, RET_V_DIM))
    return cos_full, sin_signed, decay, q_decay, k_decay, c_decay


def _rope(x, cos_full, sin_signed):
    return x * cos_full + pltpu.roll(x, RET_QK_DIM // 2, axis=1) * sin_signed


def _rope_t(dy, cos_full, sin_signed):
    return dy * cos_full + pltpu.roll(dy * sin_signed, RET_QK_DIM // 2, axis=1)


RET_COLS = 512


def _ret_specs(N, chunk_of):
    C = RET_CHUNK

    def rows(width, col=0):
        return pl.BlockSpec((C, width), lambda b, i: (b * N + chunk_of(i), col))

    def whole(shape):
        return pl.BlockSpec(shape, lambda b, i: (0,) * len(shape))

    wide = RET_HEADS * RET_V_DIM
    return dict(
        q=rows(RET_COLS, COL_Q // RET_COLS), k=rows(RET_COLS, COL_K // RET_COLS),
        v=[rows(RET_COLS, COL_V // RET_COLS + j) for j in range(2)],
        gr=[rows(RET_COLS, COL_GR // RET_COLS + j) for j in range(2)],
        table=pl.BlockSpec((C, RET_QK_DIM), lambda b, i: (chunk_of(i), 0)),
        decay=whole((RET_HEADS, C, C)), qd=whole((RET_HEADS, C, RET_QK_DIM)), kd=whole((RET_HEADS, C, RET_QK_DIM)),
        cd=whole((RET_HEADS, 1, RET_V_DIM)), vec=whole((1, wide)), qk_rows=rows(RET_COLS), v_rows=rows(wide),
        state=pl.BlockSpec((None, None, RET_HEADS, RET_QK_DIM, RET_V_DIM), lambda b, i: (b, chunk_of(i), 0, 0, 0)))


def _head_cols(h):
    pair = slice((h % 2) * RET_V_DIM, (h % 2 + 1) * RET_V_DIM)
    return slice(h * RET_QK_DIM, (h + 1) * RET_QK_DIM), h // 2, pair, slice(h * RET_V_DIM, (h + 1) * RET_V_DIM)


def _group_norm(o):
    mu = jnp.mean(o, axis=-1, keepdims=True)
    oc = o - mu
    rstd = lax.rsqrt(jnp.mean(oc * oc, axis=-1, keepdims=True) + EPS)
    return oc * rstd, rstd


def _ret_fwd(proj, g_ret, b_ret, tables, B, S, comm=None):
    N = S // RET_CHUNK
    cos_t, sin_t, decay, q_decay, k_decay, c_decay = tables
    sp = _ret_specs(N, lambda i: i)

    def body(q_ref, k_ref, v0_ref, v1_ref, gr0_ref, gr1_ref, cos_ref, sin_ref, dec_ref, qd_ref, kd_ref, cd_ref,
             g_ref, b_ref, y_ref, rs_ref, r_ref):
        @pl.when(pl.program_id(1) == 0)
        def _():
            r_ref[...] = jnp.zeros_like(r_ref)

        cs, sn = cos_ref[...], sin_ref[...]
        heads = range(RET_HEADS)
        cols = [_head_cols(h) for h in heads]
        q = [_rope(q_ref[:, cols[h][0]], cs, sn) for h in heads]
        k = [_rope(k_ref[:, cols[h][0]], cs, sn) * (RET_QK_DIM ** -0.5) for h in heads]
        v = [(v0_ref, v1_ref)[cols[h][1]][:, cols[h][2]] for h in heads]
        R = [r_ref[h] for h in heads]
        s = [_dot(q[h], k[h], "nt") * dec_ref[h] for h in heads]
        o = [_dot(s[h], v[h]) + _dot(q[h] * qd_ref[h], R[h]) for h in heads]
        r_new = [cd_ref[h] * R[h] + _dot(k[h] * kd_ref[h], v[h], "tn") for h in heads]
        for h in heads:
            _, j, pair, wide = cols[h]
            rs_ref[h] = R[h]
            r_ref[h] = r_new[h]
            on, _ = _group_norm(o[h])
            gr = (gr0_ref, gr1_ref)[j][:, pair]
            y_ref[:, wide] = (gr * jax.nn.sigmoid(gr) * (on * g_ref[:, wide] + b_ref[:, wide])).astype(y_ref.dtype)

    state = jax.ShapeDtypeStruct((B, N, RET_HEADS, RET_QK_DIM, RET_V_DIM), F32)
    return _pcall(
        body, (proj,) * 6 + (cos_t, sin_t, decay, q_decay, k_decay, c_decay, g_ret, b_ret),
        name="ret_fwd", out_shape=(jax.ShapeDtypeStruct((B * S, RET_HEADS * RET_V_DIM), BF16), state), grid=(B, N),
        in_specs=[sp["q"], sp["k"], *sp["v"], *sp["gr"], sp["table"], sp["table"], sp["decay"], sp["qd"],
                  sp["kd"], sp["cd"], sp["vec"], sp["vec"]],
        out_specs=(sp["v_rows"], sp["state"]),
        scratch_shapes=[pltpu.VMEM((RET_HEADS, RET_QK_DIM, RET_V_DIM), F32)],
        sem=("parallel", "arbitrary"), comm=comm)


def _ret_bwd(proj, states, d_yr, g_ret, b_ret, tables, B, S, comm=None):
    N = S // RET_CHUNK
    cos_t, sin_t, decay, q_decay, k_decay, c_decay = tables
    sp = _ret_specs(N, lambda i: N - 1 - i)
    qk_scale = RET_QK_DIM ** -0.5

    def body(q_ref, k_ref, v0_ref, v1_ref, gr0_ref, gr1_ref, dy_ref, rs_ref, cos_ref, sin_ref, dec_ref, qd_ref,
             kd_ref, cd_ref, g_ref, b_ref, dq_ref, dk_ref, dv_ref, dgr_ref, dg_ref, db_ref, dr_ref):
        @pl.when((pl.program_id(0) == 0) & (pl.program_id(1) == 0))
        def _():
            dg_ref[...] = jnp.zeros_like(dg_ref)
            db_ref[...] = jnp.zeros_like(db_ref)

        @pl.when(pl.program_id(1) == 0)
        def _():
            dr_ref[...] = jnp.zeros_like(dr_ref)

        cs, sn = cos_ref[...], sin_ref[...]
        heads = range(RET_HEADS)
        cols = [_head_cols(h) for h in heads]
        q = [_rope(q_ref[:, cols[h][0]], cs, sn) for h in heads]
        k = [_rope(k_ref[:, cols[h][0]], cs, sn) * qk_scale for h in heads]
        v = [(v0_ref, v1_ref)[cols[h][1]][:, cols[h][2]] for h in heads]
        s = [_dot(q[h], k[h], "nt") * dec_ref[h] for h in heads]
        o = [_dot(s[h], v[h]) + _dot(q[h] * qd_ref[h], rs_ref[h]) for h in heads]
        do = []
        for h in heads:
            _, j, pair, wide = cols[h]
            on, rstd = _group_norm(o[h])
            g = g_ref[:, wide]
            oaff = on * g + b_ref[:, wide]
            gr = (gr0_ref, gr1_ref)[j][:, pair]
            sg = jax.nn.sigmoid(gr)
            dy = dy_ref[:, wide]
            dgr_ref[:, wide] = (dy * oaff * (sg * (1.0 + gr * (1.0 - sg)))).astype(dgr_ref.dtype)
            doaff = dy * (gr * sg)
            dg_ref[:, wide] += jnp.sum(doaff * on, axis=0, keepdims=True)
            db_ref[:, wide] += jnp.sum(doaff, axis=0, keepdims=True)
            don = doaff * g
            do.append(rstd * (don - jnp.mean(don, axis=-1, keepdims=True)
                              - on * jnp.mean(don * on, axis=-1, keepdims=True)))
        ds = [_dot(do[h], v[h], "nt") * dec_ref[h] for h in heads]
        dq = [_dot(ds[h], k[h]) + qd_ref[h] * _dot(do[h], rs_ref[h], "nt") for h in heads]
        dk = [_dot(ds[h], q[h], "tn") + kd_ref[h] * _dot(v[h], dr_ref[h], "nt") for h in heads]
        dv = [_dot(s[h], do[h], "tn") + _dot(k[h] * kd_ref[h], dr_ref[h]) for h in heads]
        dr = [cd_ref[h] * dr_ref[h] + _dot(q[h] * qd_ref[h], do[h], "tn") for h in heads]
        for h in heads:
            qk, _, _, wide = cols[h]
            dv_ref[:, wide] = dv[h].astype(dv_ref.dtype)
            dr_ref[h] = dr[h]
            dq_ref[:, qk] = _rope_t(dq[h], cs, sn).astype(dq_ref.dtype)
            dk_ref[:, qk] = _rope_t(dk[h] * qk_scale, cs, sn).astype(dk_ref.dtype)

    T = B * S
    qk_shape = jax.ShapeDtypeStruct((T, RET_HEADS * RET_QK_DIM), BF16)
    v_shape = jax.ShapeDtypeStruct((T, RET_HEADS * RET_V_DIM), BF16)
    vec_shape = jax.ShapeDtypeStruct((1, RET_HEADS * RET_V_DIM), F32)
    return _pcall(
        body, (proj,) * 6 + (d_yr, states, cos_t, sin_t, decay, q_decay, k_decay, c_decay, g_ret, b_ret),
        name="ret_bwd", out_shape=(qk_shape, qk_shape, v_shape, v_shape, vec_shape, vec_shape), grid=(B, N),
        in_specs=[sp["q"], sp["k"], *sp["v"], *sp["gr"], sp["v_rows"], sp["state"], sp["table"], sp["table"],
                  sp["decay"], sp["qd"], sp["kd"], sp["cd"], sp["vec"], sp["vec"]],
        out_specs=(sp["qk_rows"], sp["qk_rows"], sp["v_rows"], sp["v_rows"], sp["vec"], sp["vec"]),
        scratch_shapes=[pltpu.VMEM((RET_HEADS, RET_QK_DIM, RET_V_DIM), F32)],
        sem=("arbitrary", "arbitrary"), comm=comm)


def _xa_rows(S):
    return _tile(S, 256)


def _xa_groups(S, rows, size=4):
    chunks = [slice(r, r + rows) for r in range(0, S, rows)]
    return [chunks[g:g + size] for g in range(0, len(chunks), size)]


def _xa_specs(S, M):
    q = pl.BlockSpec((S, XA_HEAD_DIM), lambda b, h: (b, COL_QX // XA_HEAD_DIM + h))
    k = pl.BlockSpec((M, XA_HEAD_DIM), lambda b, h: (b, h))
    v = pl.BlockSpec((M, XA_HEAD_DIM), lambda b, h: (b, XA_HEADS + h))
    o = pl.BlockSpec((S, XA_HEAD_DIM), lambda b, h: (b, h))
    return q, k, v, o


def _softmax_rows(s):
    e = jnp.exp(s - jnp.max(s, axis=-1, keepdims=True))
    return e / jnp.sum(e, axis=-1, keepdims=True)


def _xa_fwd(proj, kv, B, S, M, comm=None):
    CH = _xa_rows(S)
    q_spec, k_spec, v_spec, o_spec = _xa_specs(S, M)

    def body(q_ref, k_ref, v_ref, o_ref):
        for group in _xa_groups(S, CH):
            sc = [_dot(q_ref[rows, :], k_ref[...], "nt") * (XA_HEAD_DIM ** -0.5) for rows in group]
            p = [_softmax_rows(s) for s in sc]
            for rows, pg in zip(group, p):
                o_ref[rows, :] = _dot(pg, v_ref[...]).astype(o_ref.dtype)

    return _pcall(
        body, (proj, kv, kv), name="xattn_fwd", out_shape=jax.ShapeDtypeStruct((B * S, XA_WIDTH), BF16),
        grid=(B, XA_HEADS), in_specs=[q_spec, k_spec, v_spec], out_specs=o_spec,
        sem=("parallel", "parallel"), comm=comm)


def _xa_bwd(proj, kv, d_o, B, S, M, comm=None):
    CH = _xa_rows(S)
    q_spec, k_spec, v_spec, o_spec = _xa_specs(S, M)
    scale = XA_HEAD_DIM ** -0.5

    def body(q_ref, k_ref, v_ref, do_ref, dq_ref, dk_ref, dv_ref):
        dk_ref[...] = jnp.zeros_like(dk_ref)
        dv_ref[...] = jnp.zeros_like(dv_ref)
        for group in _xa_groups(S, CH):
            q = [q_ref[rows, :] for rows in group]
            do = [do_ref[rows, :] for rows in group]
            p = [_softmax_rows(_dot(qg, k_ref[...], "nt") * scale) for qg in q]
            dp = [_dot(dg, v_ref[...], "nt") for dg in do]
            ds = [pg * (dpg - jnp.sum(dpg * pg, axis=-1, keepdims=True)) * scale for pg, dpg in zip(p, dp)]
            for rows, dsg in zip(group, ds):
                dq_ref[rows, :] = _dot(dsg, k_ref[...]).astype(dq_ref.dtype)
            dk_ref[...] += sum(_dot(dsg, qg, "tn") for dsg, qg in zip(ds, q))
            dv_ref[...] += sum(_dot(pg, dg, "tn") for pg, dg in zip(p, do))

    kv_out = pl.BlockSpec((M, XA_HEAD_DIM), lambda b, h: (b, h))
    return _pcall(
        body, (proj, kv, kv, d_o), name="xattn_bwd",
        out_shape=(jax.ShapeDtypeStruct((B * S, XA_WIDTH), BF16), jax.ShapeDtypeStruct((B * M, XA_WIDTH), F32),
                   jax.ShapeDtypeStruct((B * M, XA_WIDTH), F32)),
        grid=(B, XA_HEADS), in_specs=[q_spec, k_spec, v_spec, o_spec], out_specs=(o_spec, kv_out, kv_out),
        sem=("parallel", "parallel"), comm=comm)


def _gate_specs(tm):
    n = COL_GL // D_MODEL
    return [pl.BlockSpec((tm, D_MODEL), lambda i, j=j: (i, n + j)) for j in range(3)]


def _merge_out(proj, ys, w_out, x, g, tm=512, comm=None):
    T = proj.shape[0]
    tm = _tile(T, tm)
    row = pl.BlockSpec((tm, D_MODEL), lambda i: (i, 0))
    vec = pl.BlockSpec((1, D_MODEL), lambda i: (0, 0))

    def body(g0, g1, g2, y0, y1, y2, w_ref, x_ref, gf_ref, m_ref, x1_ref, h_ref):
        acc = jax.nn.sigmoid(g0[...]) * y0[...]
        acc = acc + jax.nn.sigmoid(g1[...]) * y1[...]
        acc = acc + jax.nn.sigmoid(g2[...]) * y2[...]
        merged = acc.astype(m_ref.dtype)
        m_ref[...] = merged
        xv = x_ref[...] + _dot(merged, w_ref[...])
        x1_ref[...] = xv
        r = lax.rsqrt(jnp.mean(xv * xv, axis=-1, keepdims=True) + EPS)
        h_ref[...] = (xv * r * gf_ref[...]).astype(h_ref.dtype)

    bf16_rows = jax.ShapeDtypeStruct((T, D_MODEL), BF16)
    return _pcall(
        body, (proj, proj, proj, *ys, w_out, x, g), name="merge_out",
        out_shape=(bf16_rows, jax.ShapeDtypeStruct((T, D_MODEL), F32), bf16_rows), grid=(T // tm,),
        in_specs=_gate_specs(tm) + [row] * 3 + [pl.BlockSpec(w_out.shape, lambda i: (0, 0)), row, vec],
        out_specs=(row, row, row), sem=("parallel",), comm=comm)


def _merge_bwd(proj, ys, dx1, w_out, tm=512, comm=None):
    T = proj.shape[0]
    tm = _tile(T, tm)
    row = pl.BlockSpec((tm, D_MODEL), lambda i: (i, 0))

    def body(g0, g1, g2, y0, y1, y2, dx_ref, w_ref, dgl_ref, d0, d1, d2):
        dm = _dot(dx_ref[...], w_ref[...], "nt")
        for j, (g_ref, y_ref, d_ref) in enumerate(((g0, y0, d0), (g1, y1, d1), (g2, y2, d2))):
            sg = jax.nn.sigmoid(g_ref[...])
            d_ref[...] = (dm * sg).astype(d_ref.dtype)
            dgl_ref[:, j * D_MODEL:(j + 1) * D_MODEL] = (dm * y_ref[...] * sg * (1.0 - sg)).astype(dgl_ref.dtype)

    dy = jax.ShapeDtypeStruct((T, D_MODEL), BF16)
    return _pcall(
        body, (proj, proj, proj, *ys, dx1, w_out), name="merge_bwd",
        out_shape=(jax.ShapeDtypeStruct((T, 3 * D_MODEL), BF16), dy, dy, dy), grid=(T // tm,),
        in_specs=_gate_specs(tm) + [row] * 4 + [pl.BlockSpec(w_out.shape, lambda i: (0, 0))],
        out_specs=(pl.BlockSpec((tm, 3 * D_MODEL), lambda i: (i, 0)), row, row, row),
        sem=("parallel",), comm=comm)


def _gelu(x):
    return 0.5 * x * (1.0 + jnp.tanh(GELU_C * (x + GELU_A * x * x * x)))


def _gelu_grad(x):
    t = jnp.tanh(GELU_C * (x + GELU_A * x * x * x))
    return 0.5 * (1.0 + t) + 0.5 * x * (1.0 - t * t) * GELU_C * (1.0 + 3.0 * GELU_A * x * x)


GLU_HALO = 16


def _shift_down(x, prev, n):
    last = prev.shape[0]
    r = lax.broadcasted_iota(jnp.int32, (8, 1), 0)
    rolled = pltpu.roll(x, n, axis=0)
    head = rolled[0:8]
    for j in range(n):
        head = jnp.where(r == j, prev[last - n + j:last - n + j + 1, :], head)
    return jnp.concatenate([head, rolled[8:]], axis=0)


def _shift_up(x, nxt, n):
    rows = x.shape[0]
    r = lax.broadcasted_iota(jnp.int32, (8, 1), 0)
    rolled = pltpu.roll(x, rows - n, axis=0)
    tail = rolled[rows - 8:]
    for j in range(n):
        tail = jnp.where(r == 8 - n + j, nxt[j:j + 1, :], tail)
    return jnp.concatenate([rolled[:rows - 8], tail], axis=0)


def _conv(a, prev, cw, cb):
    return _shift_down(a, prev, 2) * cw[0:1, :] + _shift_down(a, prev, 1) * cw[1:2, :] + a * cw[2:3, :] + cb


def _glu_fwd(up, cw, cb, S, tm=1024, comm=None):
    T = up.shape[2]
    tm = _tile(S, tm)
    per_seq = S // tm

    def body(ab_ref, prev_ref, cw_ref, cb_ref, u_ref):
        i = pl.program_id(1)
        prev = jnp.where(i % per_seq == 0, 0.0, prev_ref[...].astype(F32))
        ac = _conv(ab_ref[0].astype(F32), prev, cw_ref[...], cb_ref[...])
        u_ref[...] = (_gelu(ac) * ab_ref[1].astype(F32)).astype(u_ref.dtype)

    before = tm // GLU_HALO
    return _pcall(
        body, (up, up, cw, cb), name="glu_fwd",
        out_shape=jax.ShapeDtypeStruct((FFN_SLABS, T, UP_SHARD), BF16), grid=(FFN_SLABS, T // tm),
        in_specs=[pl.BlockSpec((2, None, tm, UP_SHARD), lambda d, i: (0, d, i, 0)),
                  pl.BlockSpec((None, None, GLU_HALO, UP_SHARD),
                               lambda d, i: (0, d, jnp.maximum(i * before - 1, 0), 0)),
                  pl.BlockSpec((None, 3, UP_SHARD), lambda d, i: (d, 0, 0)),
                  pl.BlockSpec((None, 1, UP_SHARD), lambda d, i: (d, 0, 0))],
        out_specs=pl.BlockSpec((None, tm, UP_SHARD), lambda d, i: (d, i, 0)),
        sem=("parallel", "parallel"), comm=comm)


def _glu_bwd(up, d_u, cw, cb, S, tm=1024, comm=None):
    T = up.shape[2]
    tm = _tile(S, tm)
    per_seq = S // tm
    n_tiles = T // tm
    per_tile = tm // GLU_HALO

    def body(ab_ref, prev_ref, abn_ref, du_ref, dun_ref, cw_ref, cb_ref, dup_ref, dcw_ref, dcb_ref):
        i = pl.program_id(1)

        @pl.when(i == 0)
        def _():
            dcw_ref[...] = jnp.zeros_like(dcw_ref)
            dcb_ref[...] = jnp.zeros_like(dcb_ref)

        cw, cb = cw_ref[...], cb_ref[...]
        a, b = ab_ref[0].astype(F32), ab_ref[1].astype(F32)
        prev = jnp.where(i % per_seq == 0, 0.0, prev_ref[...].astype(F32))
        a2, a1 = _shift_down(a, prev, 2), _shift_down(a, prev, 1)
        ac = a2 * cw[0:1, :] + a1 * cw[1:2, :] + a * cw[2:3, :] + cb
        du = du_ref[...].astype(F32)
        dup_ref[1] = (du * _gelu(ac)).astype(dup_ref.dtype)
        dac = du * b * _gelu_grad(ac)
        dcb_ref[...] += jnp.sum(dac, axis=0, keepdims=True)
        dcw_ref[0:1, :] += jnp.sum(dac * a2, axis=0, keepdims=True)
        dcw_ref[1:2, :] += jnp.sum(dac * a1, axis=0, keepdims=True)
        dcw_ref[2:3, :] += jnp.sum(dac * a, axis=0, keepdims=True)
        acn = _conv(abn_ref[0].astype(F32), a[tm - GLU_HALO:, :], cw, cb)
        dacn = jnp.where(i % per_seq == per_seq - 1, 0.0,
                         dun_ref[...].astype(F32) * abn_ref[1].astype(F32) * _gelu_grad(acn))
        da = dac * cw[2:3, :] + _shift_up(dac, dacn, 1) * cw[1:2, :] + _shift_up(dac, dacn, 2) * cw[0:1, :]
        dup_ref[0] = da.astype(dup_ref.dtype)

    def nxt(i):
        return jnp.minimum((i + 1) * per_tile, T // GLU_HALO - 1)

    return _pcall(
        body, (up, up, up, d_u, d_u, cw, cb), name="glu_bwd",
        out_shape=(jax.ShapeDtypeStruct((2, FFN_SLABS, T, UP_SHARD), BF16),
                   jax.ShapeDtypeStruct((FFN_SLABS, 3, UP_SHARD), F32),
                   jax.ShapeDtypeStruct((FFN_SLABS, 1, UP_SHARD), F32)),
        grid=(FFN_SLABS, n_tiles),
        in_specs=[pl.BlockSpec((2, None, tm, UP_SHARD), lambda d, i: (0, d, i, 0)),
                  pl.BlockSpec((None, None, GLU_HALO, UP_SHARD),
                               lambda d, i: (0, d, jnp.maximum(i * per_tile - 1, 0), 0)),
                  pl.BlockSpec((2, None, GLU_HALO, UP_SHARD), lambda d, i: (0, d, nxt(i), 0)),
                  pl.BlockSpec((None, tm, UP_SHARD), lambda d, i: (d, i, 0)),
                  pl.BlockSpec((None, GLU_HALO, UP_SHARD), lambda d, i: (d, nxt(i), 0)),
                  pl.BlockSpec((None, 3, UP_SHARD), lambda d, i: (d, 0, 0)),
                  pl.BlockSpec((None, 1, UP_SHARD), lambda d, i: (d, 0, 0))],
        out_specs=(pl.BlockSpec((2, None, tm, UP_SHARD), lambda d, i: (0, d, i, 0)),
                   pl.BlockSpec((None, 3, UP_SHARD), lambda d, i: (d, 0, 0)),
                   pl.BlockSpec((None, 1, UP_SHARD), lambda d, i: (d, 0, 0))),
        sem=("parallel", "arbitrary"), comm=comm)


def _mm_up(h2, w_up_t, tm=MM_ROWS, comm=None):
    T, K = h2.shape
    tm = _tile(T, tm)
    return _matmul(
        "mm_up", "nt", h2, w_up_t, jax.ShapeDtypeStruct((N_DEV, T, UP_SHARD), BF16), (N_DEV, T // tm, 1),
        pl.BlockSpec((tm, K), lambda j, i, k: (i, 0)), pl.BlockSpec((None, UP_SHARD, K), lambda j, i, k: (j, 0, 0)),
        pl.BlockSpec((None, tm, UP_SHARD), lambda j, i, k: (j, i, 0)), (tm, UP_SHARD), comm=comm)


def _loss_epilogue(ffn, operands, outputs, first):
    x1_ref, t_ref, g_ref = operands
    dx_ref, dg_ref, loss_ref = outputs

    @pl.when(first)
    def _():
        dg_ref[...] = jnp.zeros_like(dg_ref)
        loss_ref[...] = jnp.zeros_like(loss_ref)

    xv = x1_ref[...] + ffn
    r = lax.rsqrt(jnp.mean(xv * xv, axis=-1, keepdims=True) + EPS)
    xhat = xv * r
    err = xhat * g_ref[...] - t_ref[...]
    loss_ref[...] += (0.5 / D_MODEL) * jnp.sum(err * err)
    dy = err * (1.0 / D_MODEL)
    dg_ref[...] += jnp.sum(dy * xhat, axis=0, keepdims=True)
    dxhat = dy * g_ref[...]
    dx_ref[...] = r * (dxhat - xhat * jnp.mean(dxhat * xhat, axis=-1, keepdims=True))


def _mm_down_loss(u, w_down, x1, target, g_final, tm=MM_ROWS_RES):
    J, T, n = u.shape
    tm = _tile(T, tm)
    row = pl.BlockSpec((tm, D_MODEL), lambda i, d: (i, 0))
    vec = pl.BlockSpec((1, D_MODEL), lambda i, d: (0, 0))
    vec_shape = jax.ShapeDtypeStruct((1, D_MODEL), F32)
    return _matmul(
        "mm_down", "nn", u, w_down, (jax.ShapeDtypeStruct((T, D_MODEL), F32), vec_shape, vec_shape), (T // tm, J),
        pl.BlockSpec((None, tm, n), lambda i, d: (d, i, 0)), pl.BlockSpec((None, n, D_MODEL), lambda i, d: (d, 0, 0)),
        (row, vec, vec), (tm, D_MODEL), [x1, target, g_final], [row, row, vec], epilogue=_loss_epilogue)


def _mm_down_t(dx, w_down, tm=MM_ROWS):
    T = dx.shape[0]
    J, n, _ = w_down.shape
    tm = _tile(T, tm)
    return _matmul(
        "mm_down_t", "nt", dx, w_down, jax.ShapeDtypeStruct((J, T, n), BF16), (J, T // tm, 1),
        pl.BlockSpec((tm, D_MODEL), lambda d, i, k: (i, 0)), pl.BlockSpec((None, n, D_MODEL), lambda d, i, k: (d, 0, 0)),
        pl.BlockSpec((None, tm, n), lambda d, i, k: (d, i, 0)), (tm, n))


def _mm_dw_down(u, dx, tk=MM_TOKENS):
    J, T, n = u.shape
    tk = _tile(T, tk)
    return _matmul(
        "mm_dw_down", "tn", u, dx, jax.ShapeDtypeStruct((J, n, D_MODEL), BF16), (J, T // tk),
        pl.BlockSpec((None, tk, n), lambda d, k: (d, k, 0)), pl.BlockSpec((tk, D_MODEL), lambda d, k: (k, 0)),
        pl.BlockSpec((None, n, D_MODEL), lambda d, k: (d, 0, 0)), (n, D_MODEL))


def _mm_dw_up(h2, d_up, tk=MM_TOKENS):
    T, K = h2.shape
    tk = _tile(T, tk)
    return _matmul(
        "mm_dw_up", "tn", d_up, h2, jax.ShapeDtypeStruct((N_DEV, UP_SHARD, K), BF16), (N_DEV, T // tk),
        pl.BlockSpec((None, tk, UP_SHARD), lambda j, k: (j, k, 0)), pl.BlockSpec((tk, K), lambda j, k: (k, 0)),
        pl.BlockSpec((None, UP_SHARD, K), lambda j, k: (j, 0, 0)), (UP_SHARD, K))


def _mm_up_t(d_up, w_up_t, rms, tm=MM_ROWS_RES, comm=None):
    J, T, n = d_up.shape
    K = w_up_t.shape[2]
    tm = _tile(T, tm)
    fused = _rms_bwd_fused(T, K, tm, rms)
    return _matmul(
        "mm_up_t", "nn", d_up, w_up_t, fused.pop("out_shape"), (T // tm, J),
        pl.BlockSpec((None, tm, n), lambda i, j: (j, i, 0)), pl.BlockSpec((None, n, K), lambda i, j: (j, 0, 0)),
        fused.pop("o_spec"), (tm, K), comm=comm, **fused)


def _cast_shards(shards):
    def body(*refs):
        n = (len(refs) - 2) // 2
        for src, dst in zip(refs[:n], refs[n:2 * n]):
            dst[...] = src[...].astype(dst.dtype)
        half = refs[n - 1].shape[1] // 2
        refs[2 * n][...] = refs[n - 1][:, :half].astype(BF16)
        refs[2 * n + 1][...] = refs[n - 1][:, half:].astype(BF16)

    last = shards[-1].shape
    halves = [jax.ShapeDtypeStruct((last[0], last[1] // 2), BF16)] * 2
    return pl.pallas_call(
        body, out_shape=[jax.ShapeDtypeStruct(s.shape, BF16) for s in shards] + halves, name="cast_shards",
        compiler_params=pltpu.CompilerParams(vmem_limit_bytes=VMEM_LIMIT),
    )(*shards)


def _adamw(w, g, m, v):
    m = ADAM_B1 * m + (1.0 - ADAM_B1) * g
    v = ADAM_B2 * v + (1.0 - ADAM_B2) * (g * g)
    m_hat = m / (1.0 - ADAM_B1 ** ADAM_STEP)
    v_hat = v / (1.0 - ADAM_B2 ** ADAM_STEP)
    delta = -ADAM_LR * (m_hat / (jnp.sqrt(v_hat) + ADAM_EPS) + ADAM_WD * w)
    return delta, m, v


def _sum_parts(p_ref):
    g = p_ref[0].astype(F32)
    for d in range(1, N_DEV):
        g = g + p_ref[d].astype(F32)
    return g


ADAM_ROWS = 512


def _reduce_adam(name, parts, w, m, v):
    R, Cn = w.shape
    by_rows = sum(p.shape[1] for p in parts) == R and len(parts) > 1
    common = math.gcd(*[p.shape[1] for p in parts])
    tr = max(t for t in range(8, min(common, ADAM_ROWS) + 1, 8) if common % t == 0)
    n_tiles = [p.shape[1] // tr for p in parts]
    first = [sum(n_tiles[:j]) for j in range(len(parts))] if by_rows else [0] * len(parts)

    def body(*refs):
        p_refs = refs[:len(parts)]
        w_ref, m_ref, v_ref, g_out, d_out, m_out, v_out = refs[len(parts):]

        def update(p_ref):
            g = _sum_parts(p_ref)
            delta, m_new, v_new = _adamw(w_ref[...], g, m_ref[...], v_ref[...])
            g_out[...] = g
            d_out[...] = delta
            m_out[...] = m_new
            v_out[...] = v_new

        if len(parts) == 1:
            update(p_refs[0])
        elif by_rows:
            i = pl.program_id(0)
            for p_ref, t0, n in zip(p_refs, first, n_tiles):
                pl.when((i >= t0) & (i < t0 + n))(functools.partial(update, p_ref))
        else:
            c = lax.axis_index("c")
            for side, p_ref in enumerate(p_refs):
                pl.when(c == side)(functools.partial(update, p_ref))

    def part_spec(t0, n):
        return pl.BlockSpec((N_DEV, tr, Cn), lambda i: (0, jnp.clip(i - t0, 0, n - 1), 0))

    row = pl.BlockSpec((tr, Cn), lambda i: (i, 0))
    shape = jax.ShapeDtypeStruct((R, Cn), F32)
    return pl.pallas_call(
        body, out_shape=(shape,) * 4, grid=(R // tr,),
        in_specs=[part_spec(t0, n) for t0, n in zip(first, n_tiles)] + [row, row, row],
        out_specs=(row,) * 4, name=name, compiler_params=_params(("parallel",)),
    )(*parts, w, m, v)


def _small_adam(name, gathered, params):
    n_g, n_p = len(gathered), len(params)

    def body(*refs):
        g_refs = refs[:n_g]
        wmv = refs[n_g:n_g + 3 * n_p]
        sums = refs[n_g + 3 * n_p:2 * n_g + 3 * n_p]
        upd = refs[2 * n_g + 3 * n_p:]
        for j in range(n_g):
            g = _sum_parts(g_refs[j])
            sums[j][...] = g
            if j < n_p:
                w_ref, m_ref, v_ref = wmv[3 * j:3 * j + 3]
                delta, m_new, v_new = _adamw(w_ref[...], g, m_ref[...], v_ref[...])
                upd[3 * j][...] = delta
                upd[3 * j + 1][...] = m_new
                upd[3 * j + 2][...] = v_new

    flat = [a for wmv in params for a in wmv]
    out_shape = [jax.ShapeDtypeStruct(g.shape[1:], F32) for g in gathered]
    out_shape += [jax.ShapeDtypeStruct(a.shape, F32) for a in flat]
    res = pl.pallas_call(body, out_shape=out_shape, name=name)(*gathered, *flat)
    return res[:n_g], [tuple(res[n_g + 3 * j:n_g + 3 * j + 3]) for j in range(n_p)]


def _adam_only(name, g, w, m, v):
    def body(g_ref, w_ref, m_ref, v_ref, d_out, m_out, v_out):
        delta, m_new, v_new = _adamw(w_ref[...], g_ref[...], m_ref[...], v_ref[...])
        d_out[...] = delta
        m_out[...] = m_new
        v_out[...] = v_new

    shape = jax.ShapeDtypeStruct(w.shape, F32)
    return pl.pallas_call(body, out_shape=(shape,) * 3, name=name)(g, w, m, v)


def kernel(x, mem, g_mix, w_in, w_pool, pool_scale, w_a, g_ret, b_ret, w_r, g_mem, w_mem_kv, w_c, w_out, g_ffn, w_up, conv_w, conv_b, w_down, g_final, loss_target, m_g_mix, m_w_in, m_w_pool, m_pool_scale, m_w_a, m_g_ret, m_b_ret, m_w_r, m_g_mem, m_w_mem_kv, m_w_c, m_w_out, m_g_ffn, m_w_up, m_conv_w, m_conv_b, m_w_down, m_g_final, v_g_mix, v_w_in, v_w_pool, v_pool_scale, v_w_a, v_g_ret, v_b_ret, v_w_r, v_g_mem, v_w_mem_kv, v_w_c, v_w_out, v_g_ffn, v_w_up, v_conv_w, v_conv_b, v_w_down, v_g_final):
    B, S, _ = x.shape
    M = mem.shape[1]
    T = B * S
    me = _my_index()
    x2d = x.reshape(T, D_MODEL)
    mem2d = mem.reshape(B * M, D_MODEL)
    tgt2d = loss_target.reshape(T, D_MODEL)
    g_final2 = g_final.reshape(1, D_MODEL)

    big = dict(w_in=w_in[0], w_a=w_a[0], w_r=w_r[0], w_mem_kv=w_mem_kv[0], w_c=w_c[0], w_out=w_out[0],
               w_up=w_up[0].T, w_down=w_down[0])
    names = list(big)
    cast_order = [n for n in names if n != "w_up"] + ["w_up"]
    *cast_out, w_up_lo, w_up_hi = _cast_shards([big[n] for n in cast_order])
    cast = dict(zip(cast_order, cast_out))
    cb = conv_b[0].reshape(FFN_SLABS, 1, UP_SHARD)
    wp = w_pool[0]
    tables = _ret_tables(S)

    h, h_first, h_rest = _rms_fwd("rms_mix", x2d, g_mix, split=W_IN_FIRST_ROWS)
    early = ("w_a", "w_r", "w_mem_kv", "w_c", "w_out")
    (proj, Win), landed = _mm_in_gather(h, cast["w_in"], comm=_Gather([cast[n] for n in early] + [conv_w[0]]))
    W = dict(zip(early, landed))
    half = D_MODEL // 2
    (yr, ret_states), (Wup_lo,) = _ret_fwd(proj, g_ret, b_ret, tables, B, S,
                                           comm=_Gather([w_up_lo]))
    cw_full = landed[-1].transpose(1, 0, 2).reshape(3, FFN_HIDDEN)
    cw = cw_full.reshape(3, FFN_SLABS, UP_SHARD).transpose(1, 0, 2)
    Wa = W["w_a"].transpose(1, 0, 2).reshape(POOL_WIDTH, D_MODEL)
    Wc = W["w_c"].transpose(1, 0, 2).reshape(XA_WIDTH, D_MODEL)
    Wr = W["w_r"].reshape(D_MODEL, D_MODEL)
    Wkv = W["w_mem_kv"].reshape(D_MODEL, D_MODEL)
    Wout = W["w_out"].reshape(D_MODEL, D_MODEL)
    ypre = _pool_fwd(proj, wp, pool_scale, B, S)
    y_pool = _mm_rows("mm_a", ypre, Wa, BF16)
    y_ret = _mm_rows("mm_r", yr, Wr, BF16)
    mem_n = _rms_fwd("rms_mem", mem2d, g_mem)
    kv = _mm_rows("mm_kv", mem_n, Wkv)
    o_mem = _xa_fwd(proj, kv, B, S, M)[0]
    y_mem = _mm_rows("mm_c", o_mem, Wc, BF16)
    ys = (y_pool, y_ret, y_mem)
    (merged, x1, h2), (Wup_hi,) = _merge_out(proj, ys, Wout, x2d, g_ffn, comm=_Gather([w_up_hi]))
    Wup = jnp.concatenate([Wup_lo, Wup_hi], axis=2)
    up, (Wdown,) = _mm_up(h2, Wup, comm=_Gather([cast["w_down"]]))
    up = up.reshape(2, FFN_SLABS, T, UP_SHARD)
    Wdown = Wdown.reshape(FFN_SLABS, UP_SHARD, D_MODEL)
    u = _glu_fwd(up, cw, cb, S)[0]

    dx2, dg_final, loss_part = _mm_down_loss(u, Wdown, x1, tgt2d, g_final2)
    received = {}
    d_u = _mm_down_t(dx2, Wdown)
    dW_down = _mm_dw_down(u, dx2)
    (d_up, d_cw, d_cb), (received["w_down"],) = _glu_bwd(
        up, d_u, cw, cb, S, comm=_Exchange([dW_down.reshape(N_DEV, -1, D_MODEL)]))
    d_up = d_up.reshape(N_DEV, T, UP_SHARD)
    dW_up = _mm_dw_up(h2, d_up)
    (dx1, dg_ffn), (up_c0,) = _mm_up_t(d_up, Wup, (x1, g_ffn, dx2), comm=_ExchangeTo([dW_up], 0))
    dW_out = _mm_tn("mm_dw_out", merged, dx1, BF16)
    (d_gl, d_y_pool, d_y_ret, d_y_mem), (up_c1,) = _merge_bwd(proj, ys, dx1, Wout, comm=_ExchangeTo([dW_up], 1))
    received["w_up"] = [up_c0, up_c1]
    dW_c = _mm_tn("mm_dw_c", o_mem, d_y_mem, BF16)
    d_o_mem = _mm_rows("mm_c_t", d_y_mem, Wc, kind="nt")
    (d_qx, d_kmem, d_vmem), (received["w_out"],) = _xa_bwd(
        proj, kv, d_o_mem, B, S, M, comm=_Exchange([dW_out.reshape(N_DEV, -1, D_MODEL)]))
    d_kv = jnp.concatenate([d_kmem, d_vmem], axis=1)
    dW_kv = _mm_tn("mm_dw_kv", mem_n, d_kv, BF16)
    d_mem_n = _mm_rows("mm_kv_t", d_kv, Wkv, kind="nt")
    dg_mem = _rms_bwd("rms_mem_bwd", mem2d, g_mem, d_mem_n, None)
    dW_a = _mm_tn("mm_dw_a", ypre, d_y_pool, BF16)
    d_ypre = _mm_rows("mm_a_t", d_y_pool, Wa, kind="nt")
    (d_hp, dw_pool, d_scale), (received["w_a"],) = _pool_bwd(
        proj, d_ypre, wp, pool_scale, B, S,
        comm=_Exchange([dW_a.reshape(POOL_WIDTH, N_DEV, -1).transpose(1, 0, 2)]))
    dW_r = _mm_tn("mm_dw_r", yr, d_y_ret, BF16)
    d_yr = _mm_rows("mm_r_t", d_y_ret, Wr, kind="nt")
    (d_q, d_k, d_v, d_gr, dg_ret, db_ret), landed = _ret_bwd(
        proj, ret_states, d_yr, g_ret, b_ret, tables, B, S,
        comm=_Exchange([dW_r.reshape(N_DEV, -1, D_MODEL), dW_c.reshape(XA_WIDTH, N_DEV, -1).transpose(1, 0, 2),
                        dW_kv.reshape(N_DEV, -1, D_MODEL)]))
    received["w_r"], received["w_c"], received["w_mem_kv"] = landed
    small_names = ["w_pool", "pool_scale", "g_ret", "b_ret", "g_mem", "g_ffn", "conv_b", "g_final"]
    small_grads = [dw_pool, d_scale, dg_ret, db_ret, dg_mem, dg_ffn, d_cb.reshape(1, FFN_HIDDEN), dg_final,
                   d_cw.transpose(1, 0, 2).reshape(3, FFN_HIDDEN), loss_part]
    d_proj = jnp.concatenate([d_hp, d_q, d_k, d_v, d_gr, d_qx, d_gl], axis=1)
    dW_in0, small_all = _mm_tn_slab("mm_dw_in0", h_first, d_proj, IN_SHARD, BF16,
                                    comm=_Exchange([], whole=small_grads))
    dW_in1, (in0,) = _mm_tn_slab("mm_dw_in1", h_rest, d_proj, IN_SHARD, BF16,
                                 comm=_Exchange([dW_in0]))
    (grad_x, dg_mix), (in1,) = _mm_cols_slab_t("mm_in_t", d_proj, Win, (x2d, g_mix, dx1), comm=_Exchange([dW_in1]))
    received["w_in"] = [in0, in1]
    (g_mix_all,) = _comm_call("gather_g_mix", _Exchange([], whole=[dg_mix]))

    args = dict(g_mix=g_mix, w_in=w_in, w_pool=w_pool, pool_scale=pool_scale, w_a=w_a, g_ret=g_ret, b_ret=b_ret,
                w_r=w_r, g_mem=g_mem, w_mem_kv=w_mem_kv, w_c=w_c, w_out=w_out, g_ffn=g_ffn, w_up=w_up,
                conv_w=conv_w, conv_b=conv_b, w_down=w_down, g_final=g_final)
    m_in = dict(g_mix=m_g_mix, w_in=m_w_in, w_pool=m_w_pool, pool_scale=m_pool_scale, w_a=m_w_a, g_ret=m_g_ret,
                b_ret=m_b_ret, w_r=m_w_r, g_mem=m_g_mem, w_mem_kv=m_w_mem_kv, w_c=m_w_c, w_out=m_w_out,
                g_ffn=m_g_ffn, w_up=m_w_up, conv_w=m_conv_w, conv_b=m_conv_b, w_down=m_w_down, g_final=m_g_final)
    v_in = dict(g_mix=v_g_mix, w_in=v_w_in, w_pool=v_w_pool, pool_scale=v_pool_scale, w_a=v_w_a, g_ret=v_g_ret,
                b_ret=v_b_ret, w_r=v_w_r, g_mem=v_g_mem, w_mem_kv=v_w_mem_kv, w_c=v_w_c, w_out=v_w_out,
                g_ffn=v_g_ffn, w_up=v_w_up, conv_w=v_conv_w, conv_b=v_conv_b, w_down=v_w_down, g_final=v_g_final)

    grads, deltas, new_m, new_v = {}, {}, {}, {}
    for n in names:
        parts = received[n] if isinstance(received[n], list) else [received[n]]
        flip = (lambda a: a.T) if n == "w_up" else (lambda a: a)
        outs = _reduce_adam("adam_" + n, parts, big[n], flip(m_in[n][0]), flip(v_in[n][0]))
        for store, val in zip((grads, deltas, new_m, new_v), outs):
            store[n] = flip(val)[None]

    def as_small(a):
        return a.reshape(a.shape[-3:]) if a.ndim > 2 else a.reshape(1, -1)

    def small_update(call_name, param_names, gathered):
        params = [tuple(as_small(d[n]) for d in (args, m_in, v_in)) for n in param_names]
        sums, updates = _small_adam(call_name, gathered, params)
        for n, g, (d_, m_, v_) in zip(param_names, sums, updates):
            shape = args[n].shape
            grads[n], deltas[n], new_m[n], new_v[n] = (a.reshape(shape) for a in (g, d_, m_, v_))
        return sums[len(param_names):]

    g_cw_full, loss_row = small_update("adam_small", small_names, small_all)
    loss = loss_row[0, 0]
    small_update("adam_g_mix", ["g_mix"], [g_mix_all])

    shard_cols = FFN_HIDDEN // N_DEV
    g_cw = lax.dynamic_slice_in_dim(g_cw_full, me * shard_cols, shard_cols, axis=1)
    d_, m_, v_ = _adam_only("adam_conv_w", g_cw, conv_w[0], m_conv_w[0], v_conv_w[0])
    grads["conv_w"], deltas["conv_w"], new_m["conv_w"], new_v["conv_w"] = g_cw[None], d_[None], m_[None], v_[None]

    order = ["g_mix", "w_in", "w_pool", "pool_scale", "w_a", "g_ret", "b_ret", "w_r", "g_mem", "w_mem_kv", "w_c",
             "w_out", "g_ffn", "w_up", "conv_w", "conv_b", "w_down", "g_final"]
    return (loss, grad_x.reshape(B, S, D_MODEL), *[grads[n] for n in order], *[deltas[n] for n in order],
            *[new_m[n] for n in order], *[new_v[n] for n in order])
```
